```python
import jax, jax.numpy as jnp
from jax import lax
import numpy as np

D_MODEL = 1024
BATCH = 8
SEQ = 8192
DEPTH = 2

N_META = 16
CHUNK = 64
SUB = 16
RET_HEADS = 4
RET_DK = 128
RET_DV = 128
GLA_HEADS = 4
GLA_DK = 64
GLA_DV = 128
GLA_GATE_RANK = 16
GLA_TAU = 16.0
D_FF = 2816
CONV_W = 3
ROPE_BASE = 10000.0
EPS = 1e-6

RET_QK = RET_HEADS * RET_DK
RET_V = RET_HEADS * RET_DV
GLA_QK = GLA_HEADS * GLA_DK
GLA_V = GLA_HEADS * GLA_DV
D_MIX = RET_V + GLA_V
IN_SPLITS = (RET_QK, RET_QK, RET_V, RET_V, GLA_QK, GLA_QK, GLA_V, GLA_V, GLA_GATE_RANK)
IN_WIDTH = 2 * RET_QK + 2 * RET_V + 2 * GLA_QK + 2 * GLA_V + GLA_GATE_RANK

kernel_name = "hybrid_retention_gla_convffn"


def _rmsnorm(x, w):
    xf = x.astype(jnp.float32)
    y = xf * lax.rsqrt(jnp.mean(xf * xf, axis=-1, keepdims=True) + EPS)
    return (y * w.astype(jnp.float32)).astype(x.dtype)


def _rope(t, pos):
    half = t.shape[-1] // 2
    inv = ROPE_BASE ** (-jnp.arange(half, dtype=jnp.float32) / half)
    ang = pos[:, None] * inv[None, :]
    c = jnp.cos(ang)[None, :, None, :]
    s = jnp.sin(ang)[None, :, None, :]
    t = t.astype(jnp.float32)
    t1, t2 = t[..., :half], t[..., half:]
    return jnp.concatenate([t1 * c - t2 * s, t1 * s + t2 * c], axis=-1)


def _to_chunks(t):
    t = jnp.pad(t.astype(jnp.float32), ((0, 0), (CHUNK - N_META, 0), (0, 0), (0, 0)))
    b, lp, h, d = t.shape
    return t.reshape(b, lp // CHUNK, CHUNK, h, d).transpose(0, 3, 1, 2, 4)


def _from_chunks(o):
    b, h, n, c, d = o.shape
    o = o.transpose(0, 2, 3, 1, 4).reshape(b, n * c, h, d)
    return o[:, CHUNK - N_META:]


def _scan_states(decay, kv):
    def step(state, inp):
        dec_n, kv_n = inp
        return dec_n * state + kv_n, state
    init = jnp.zeros(kv.shape[:2] + kv.shape[3:], kv.dtype)
    _, prev = lax.scan(step, init, (jnp.moveaxis(decay, 2, 0), jnp.moveaxis(kv, 2, 0)))
    return jnp.moveaxis(prev, 0, 2)


def _retention(q, k, v):
    b, h, n, c, _ = q.shape
    log_g = jnp.log(1.0 - 2.0 ** (-5.0 - jnp.arange(h, dtype=jnp.float32)))
    idx = jnp.arange(c, dtype=jnp.float32)
    diff = idx[:, None] - idx[None, :]
    dmat = jnp.where(diff >= 0, jnp.exp(log_g[:, None, None] * jnp.maximum(diff, 0.0)), 0.0)
    k = k * (RET_DK ** -0.5)
    scores = jnp.einsum('bhncd,bhnsd->bhncs', q, k) * dmat[None, :, None]
    o_intra = jnp.einsum('bhncs,bhnsv->bhncv', scores, v)
    zeta = jnp.exp(log_g[:, None] * (c - 1.0 - idx)[None, :])
    kv = jnp.einsum('bhncd,hc,bhncv->bhndv', k, zeta, v)
    chunk_decay = jnp.broadcast_to(jnp.exp(log_g * c)[None, :, None, None, None], (b, h, n, 1, 1))
    prev = _scan_states(chunk_decay, kv)
    xi = jnp.exp(log_g[:, None] * (idx + 1.0)[None, :])
    o_inter = jnp.einsum('bhncd,bhndv->bhncv', q, prev) * xi[None, :, None, :, None]
    return o_intra + o_inter


def _gla(q, k, v, log_a):
    b, h, n, c, dk = q.shape
    dv = v.shape[-1]
    ns = c // SUB
    q = q * (GLA_DK ** -0.5)
    cum = jnp.cumsum(log_a, axis=3)
    last = cum[:, :, :, -1:, :]
    kv = jnp.einsum('bhncd,bhncv->bhndv', k * jnp.exp(last - cum), v)
    prev = _scan_states(jnp.exp(last[:, :, :, 0, :])[..., None], kv)
    o_inter = jnp.einsum('bhncd,bhndv->bhncv', q * jnp.exp(cum), prev)
    qs = q.reshape(b, h, n, ns, SUB, dk)
    ks = k.reshape(b, h, n, ns, SUB, dk)
    vs = v.reshape(b, h, n, ns, SUB, dv)
    cs = cum.reshape(b, h, n, ns, SUB, dk)
    ref = jnp.concatenate([jnp.zeros((b, h, n, 1, dk), cum.dtype), cum[:, :, :, SUB - 1:c - 1:SUB, :]], axis=3)
    q_hat = qs * jnp.exp(cs - ref[:, :, :, :, None, :])
    k_hat = k[:, :, :, None] * jnp.exp(jnp.minimum(ref[:, :, :, :, None, :] - cum[:, :, :, None], 0.0))
    off = jnp.einsum('bhnasd,bhnacd->bhnasc', q_hat, k_hat)
    off_mask = jnp.arange(c)[None, :] < (jnp.arange(ns) * SUB)[:, None]
    off = jnp.where(off_mask[:, None, :], off, 0.0)
    o_off = jnp.einsum('bhnasc,bhncv->bhnasv', off, v)
    causal = jnp.tril(jnp.ones((SUB, SUB), dtype=bool))
    ddiff = cs[..., :, None, :] - cs[..., None, :, :]
    dec = jnp.exp(jnp.where(causal[:, :, None], ddiff, -jnp.inf))
    diag = jnp.einsum('bhnasd,bhnatd,bhnastd->bhnast', qs, ks, dec)
    o_diag = jnp.einsum('bhnast,bhnatv->bhnasv', diag, vs)
    return o_inter + (o_off + o_diag).reshape(b, h, n, c, dv)


def _head_group_norm(o, w):
    mu = jnp.mean(o, axis=-1, keepdims=True)
    var = jnp.mean(jnp.square(o - mu), axis=-1, keepdims=True)
    y = (o - mu) * lax.rsqrt(var + EPS)
    return y.reshape(o.shape[0], o.shape[1], -1) * w.astype(jnp.float32)


def _head_rms_norm(o, w):
    y = o * lax.rsqrt(jnp.mean(o * o, axis=-1, keepdims=True) + EPS)
    return y.reshape(o.shape[0], o.shape[1], -1) * w.astype(jnp.float32)


def _mixer(h, pos, w_in, gla_gate_w2, gla_gate_b, ret_norm_w, gla_norm_w, w_out):
    bsz, length, _ = h.shape
    proj = h @ w_in
    offs = np.cumsum(np.array(IN_SPLITS))[:-1].tolist()
    rq, rk, rv, rg, gq, gk, gv, gr, ga = jnp.split(proj, offs, axis=-1)
    rq = _rope(rq.reshape(bsz, length, RET_HEADS, RET_DK), pos)
    rk = _rope(rk.reshape(bsz, length, RET_HEADS, RET_DK), pos)
    rv = rv.reshape(bsz, length, RET_HEADS, RET_DV)
    o_ret = _from_chunks(_retention(_to_chunks(rq), _to_chunks(rk), _to_chunks(rv)))
    o_ret = _head_group_norm(o_ret, ret_norm_w) * jax.nn.silu(rg.astype(jnp.float32))
    z = (ga @ gla_gate_w2 + gla_gate_b).astype(jnp.float32)
    log_a = (jax.nn.log_sigmoid(z) / GLA_TAU).reshape(bsz, length, GLA_HEADS, GLA_DK)
    gq = gq.reshape(bsz, length, GLA_HEADS, GLA_DK)
    gk = gk.reshape(bsz, length, GLA_HEADS, GLA_DK)
    gv = gv.reshape(bsz, length, GLA_HEADS, GLA_DV)
    o_gla = _from_chunks(_gla(_to_chunks(gq), _to_chunks(gk), _to_chunks(gv), _to_chunks(log_a)))
    o_gla = _head_rms_norm(o_gla, gla_norm_w) * jax.nn.silu(gr.astype(jnp.float32))
    merged = jnp.concatenate([o_ret, o_gla], axis=-1).astype(h.dtype)
    return merged @ w_out


def _conv_ffn(h, ffn_up, ffn_conv_w, ffn_conv_b, ffn_down):
    length = h.shape[1]
    u = h @ ffn_up
    up = jnp.pad(u, ((0, 0), (CONV_W - 1, 0), (0, 0)))
    conv = ffn_conv_b + sum(up[:, i:i + length] * ffn_conv_w[i] for i in range(CONV_W))
    a, g = jnp.split(conv, 2, axis=-1)
    return (jax.nn.gelu(a, approximate=True) * g) @ ffn_down


def _fwd_setup_inputs(seed: int = 0) -> dict:
    key = jax.random.key(seed)
    ks = jax.random.split(key, 18)
    nrm = lambda k, shape, s: jax.random.normal(k, shape, jnp.float32) * s
    gain = lambda k, shape: 1.0 + 0.02 * jax.random.normal(k, shape, jnp.float32)
    return {
        "x": nrm(ks[0], (BATCH, SEQ, D_MODEL), 1.0),
        "meta_tokens": nrm(ks[1], (N_META, D_MODEL), 1.0),
        "pre_mix_norm": gain(ks[2], (DEPTH, D_MODEL)),
        "w_in": nrm(ks[3], (DEPTH, D_MODEL, IN_WIDTH), D_MODEL ** -0.5),
        "gla_gate_w2": nrm(ks[4], (DEPTH, GLA_GATE_RANK, GLA_QK), GLA_GATE_RANK ** -0.5),
        "gla_gate_b": nrm(ks[5], (DEPTH, GLA_QK), 0.1),
        "ret_norm_w": gain(ks[6], (DEPTH, RET_V)),
        "gla_norm_w": gain(ks[7], (DEPTH, GLA_V)),
        "w_out": nrm(ks[8], (DEPTH, D_MIX, D_MODEL), D_MIX ** -0.5),
        "post_mix_norm": gain(ks[9], (DEPTH, D_MODEL)),
        "pre_ffn_norm": gain(ks[10], (DEPTH, D_MODEL)),
        "ffn_up": nrm(ks[11], (DEPTH, D_MODEL, 2 * D_FF), D_MODEL ** -0.5),
        "ffn_conv_w": nrm(ks[12], (DEPTH, CONV_W, 2 * D_FF), CONV_W ** -0.5),
        "ffn_conv_b": nrm(ks[13], (DEPTH, 2 * D_FF), 0.02),
        "ffn_down": nrm(ks[14], (DEPTH, D_FF, D_MODEL), D_FF ** -0.5),
        "post_ffn_norm": gain(ks[15], (DEPTH, D_MODEL)),
    }


def _fwd_reference(x, meta_tokens, pre_mix_norm, w_in, gla_gate_w2, gla_gate_b, ret_norm_w, gla_norm_w,
              w_out, post_mix_norm, pre_ffn_norm, ffn_up, ffn_conv_w, ffn_conv_b, ffn_down, post_ffn_norm):
    bsz = x.shape[0]
    meta = jnp.broadcast_to(meta_tokens.astype(x.dtype)[None], (bsz, N_META, x.shape[-1]))
    h = jnp.concatenate([meta, x], axis=1)
    pos = jnp.arange(h.shape[1], dtype=jnp.float32)
    for i in range(DEPTH):
        m = _mixer(_rmsnorm(h, pre_mix_norm[i]), pos, w_in[i], gla_gate_w2[i], gla_gate_b[i],
                   ret_norm_w[i], gla_norm_w[i], w_out[i])
        h = h + _rmsnorm(m, post_mix_norm[i])
        f = _conv_ffn(_rmsnorm(h, pre_ffn_norm[i]), ffn_up[i], ffn_conv_w[i], ffn_conv_b[i], ffn_down[i])
        h = h + _rmsnorm(f, post_ffn_norm[i])
    return h[:, N_META:]


import jax as _jax
import jax.numpy as _jnp

TWIN_FORMAT = 'train_step'
FWD_PARAMS = ['x', 'meta_tokens', 'pre_mix_norm', 'w_in', 'gla_gate_w2', 'gla_gate_b', 'ret_norm_w', 'gla_norm_w', 'w_out', 'post_mix_norm', 'pre_ffn_norm', 'ffn_up', 'ffn_conv_w', 'ffn_conv_b', 'ffn_down', 'post_ffn_norm']
TWIN_WEIGHTS = ['meta_tokens', 'pre_mix_norm', 'w_in', 'gla_gate_w2', 'gla_gate_b', 'ret_norm_w', 'gla_norm_w', 'w_out', 'post_mix_norm', 'pre_ffn_norm', 'ffn_up', 'ffn_conv_w', 'ffn_conv_b', 'ffn_down', 'post_ffn_norm']
TWIN_DIFF_INPUT = 'x'
TWIN_INPUTS = ['x', 'meta_tokens', 'pre_mix_norm', 'w_in', 'gla_gate_w2', 'gla_gate_b', 'ret_norm_w', 'gla_norm_w', 'w_out', 'post_mix_norm', 'pre_ffn_norm', 'ffn_up', 'ffn_conv_w', 'ffn_conv_b', 'ffn_down', 'post_ffn_norm', 'loss_target', 'm_meta_tokens', 'm_pre_mix_norm', 'm_w_in', 'm_gla_gate_w2', 'm_gla_gate_b', 'm_ret_norm_w', 'm_gla_norm_w', 'm_w_out', 'm_post_mix_norm', 'm_pre_ffn_norm', 'm_ffn_up', 'm_ffn_conv_w', 'm_ffn_conv_b', 'm_ffn_down', 'm_post_ffn_norm', 'v_meta_tokens', 'v_pre_mix_norm', 'v_w_in', 'v_gla_gate_w2', 'v_gla_gate_b', 'v_ret_norm_w', 'v_gla_norm_w', 'v_w_out', 'v_post_mix_norm', 'v_pre_ffn_norm', 'v_ffn_up', 'v_ffn_conv_w', 'v_ffn_conv_b', 'v_ffn_down', 'v_post_ffn_norm']
TWIN_OUTPUTS = ['loss', 'grad_x', 'grad_meta_tokens', 'grad_pre_mix_norm', 'grad_w_in', 'grad_gla_gate_w2', 'grad_gla_gate_b', 'grad_ret_norm_w', 'grad_gla_norm_w', 'grad_w_out', 'grad_post_mix_norm', 'grad_pre_ffn_norm', 'grad_ffn_up', 'grad_ffn_conv_w', 'grad_ffn_conv_b', 'grad_ffn_down', 'grad_post_ffn_norm', 'delta_meta_tokens', 'delta_pre_mix_norm', 'delta_w_in', 'delta_gla_gate_w2', 'delta_gla_gate_b', 'delta_ret_norm_w', 'delta_gla_norm_w', 'delta_w_out', 'delta_post_mix_norm', 'delta_pre_ffn_norm', 'delta_ffn_up', 'delta_ffn_conv_w', 'delta_ffn_conv_b', 'delta_ffn_down', 'delta_post_ffn_norm', 'new_m_meta_tokens', 'new_m_pre_mix_norm', 'new_m_w_in', 'new_m_gla_gate_w2', 'new_m_gla_gate_b', 'new_m_ret_norm_w', 'new_m_gla_norm_w', 'new_m_w_out', 'new_m_post_mix_norm', 'new_m_pre_ffn_norm', 'new_m_ffn_up', 'new_m_ffn_conv_w', 'new_m_ffn_conv_b', 'new_m_ffn_down', 'new_m_post_ffn_norm', 'new_v_meta_tokens', 'new_v_pre_mix_norm', 'new_v_w_in', 'new_v_gla_gate_w2', 'new_v_gla_gate_b', 'new_v_ret_norm_w', 'new_v_gla_norm_w', 'new_v_w_out', 'new_v_post_mix_norm', 'new_v_pre_ffn_norm', 'new_v_ffn_up', 'new_v_ffn_conv_w', 'new_v_ffn_conv_b', 'new_v_ffn_down', 'new_v_post_ffn_norm']
TWIN_LEAF_KINDS = {'loss': 'loss', 'grad_x': 'grad_x', 'grad_meta_tokens': 'grad_w', 'grad_pre_mix_norm': 'grad_w', 'grad_w_in': 'grad_w', 'grad_gla_gate_w2': 'grad_w', 'grad_gla_gate_b': 'grad_w', 'grad_ret_norm_w': 'grad_w', 'grad_gla_norm_w': 'grad_w', 'grad_w_out': 'grad_w', 'grad_post_mix_norm': 'grad_w', 'grad_pre_ffn_norm': 'grad_w', 'grad_ffn_up': 'grad_w', 'grad_ffn_conv_w': 'grad_w', 'grad_ffn_conv_b': 'grad_w', 'grad_ffn_down': 'grad_w', 'grad_post_ffn_norm': 'grad_w', 'delta_meta_tokens': 'delta_w', 'delta_pre_mix_norm': 'delta_w', 'delta_w_in': 'delta_w', 'delta_gla_gate_w2': 'delta_w', 'delta_gla_gate_b': 'delta_w', 'delta_ret_norm_w': 'delta_w', 'delta_gla_norm_w': 'delta_w', 'delta_w_out': 'delta_w', 'delta_post_mix_norm': 'delta_w', 'delta_pre_ffn_norm': 'delta_w', 'delta_ffn_up': 'delta_w', 'delta_ffn_conv_w': 'delta_w', 'delta_ffn_conv_b': 'delta_w', 'delta_ffn_down': 'delta_w', 'delta_post_ffn_norm': 'delta_w', 'new_m_meta_tokens': 'new_m', 'new_m_pre_mix_norm': 'new_m', 'new_m_w_in': 'new_m', 'new_m_gla_gate_w2': 'new_m', 'new_m_gla_gate_b': 'new_m', 'new_m_ret_norm_w': 'new_m', 'new_m_gla_norm_w': 'new_m', 'new_m_w_out': 'new_m', 'new_m_post_mix_norm': 'new_m', 'new_m_pre_ffn_norm': 'new_m', 'new_m_ffn_up': 'new_m', 'new_m_ffn_conv_w': 'new_m', 'new_m_ffn_conv_b': 'new_m', 'new_m_ffn_down': 'new_m', 'new_m_post_ffn_norm': 'new_m', 'new_v_meta_tokens': 'new_v', 'new_v_pre_mix_norm': 'new_v', 'new_v_w_in': 'new_v', 'new_v_gla_gate_w2': 'new_v', 'new_v_gla_gate_b': 'new_v', 'new_v_ret_norm_w': 'new_v', 'new_v_gla_norm_w': 'new_v', 'new_v_w_out': 'new_v', 'new_v_post_mix_norm': 'new_v', 'new_v_pre_ffn_norm': 'new_v', 'new_v_ffn_up': 'new_v', 'new_v_ffn_conv_w': 'new_v', 'new_v_ffn_conv_b': 'new_v', 'new_v_ffn_down': 'new_v', 'new_v_post_ffn_norm': 'new_v'}


def _forward(args):
    return _fwd_reference(*[args[k] for k in FWD_PARAMS])


def _output_shape():
    def fwd():
        inp = _fwd_setup_inputs(0)
        return _fwd_reference(*[inp[k] for k in FWD_PARAMS])
    out = _jax.eval_shape(fwd)
    return out.shape, out.dtype

N_MICROBATCH = 1
ADAM_LR = 0.001
ADAM_B1 = 0.9
ADAM_B2 = 0.999
ADAM_EPS = 1e-08
ADAM_WD = 0.01
ADAM_STEP = 10
PER_EXAMPLE_BATCH_AXIS = {'x': 0, 'loss_target': 0}
SHARED_INPUTS = []
_WEIGHT_DTYPES = {'meta_tokens': _jnp.float32, 'pre_mix_norm': _jnp.float32, 'w_in': _jnp.float32, 'gla_gate_w2': _jnp.float32, 'gla_gate_b': _jnp.float32, 'ret_norm_w': _jnp.float32, 'gla_norm_w': _jnp.float32, 'w_out': _jnp.float32, 'post_mix_norm': _jnp.float32, 'pre_ffn_norm': _jnp.float32, 'ffn_up': _jnp.float32, 'ffn_conv_w': _jnp.float32, 'ffn_conv_b': _jnp.float32, 'ffn_down': _jnp.float32, 'post_ffn_norm': _jnp.float32}
MOMENT_SCALE = {'meta_tokens': 1.902319e-01, 'pre_mix_norm': 2.439576e+00, 'w_in': 1.293016e+00, 'gla_gate_w2': 1.911666e-01, 'gla_gate_b': 8.133843e-01, 'ret_norm_w': 1.348706e+00, 'gla_norm_w': 1.206587e+00, 'w_out': 1.221002e+00, 'post_mix_norm': 6.378869e+01, 'pre_ffn_norm': 1.438800e+00, 'ffn_up': 5.670778e-01, 'ffn_conv_w': 5.948071e-01, 'ffn_conv_b': 7.998069e-01, 'ffn_down': 1.029502e+00, 'post_ffn_norm': 6.387481e+01}


def _to_microbatches(a, axis):
    t = _jnp.moveaxis(a, axis, 0)
    t = t.reshape((N_MICROBATCH, t.shape[0] // N_MICROBATCH) + t.shape[1:])
    return _jnp.moveaxis(t, 1, axis + 1)


def setup_inputs(seed: int = 0) -> dict:
    inp = _fwd_setup_inputs(seed)
    key = _jax.random.fold_in(_jax.random.key(seed), 7919)
    shape, _ = _output_shape()
    out = dict(inp)
    out["loss_target"] = _jax.random.normal(_jax.random.fold_in(key, 0), shape, _jnp.float32)
    for i, name in enumerate(TWIN_WEIGHTS):
        w = inp[name].astype(_jnp.float32)
        if MOMENT_SCALE is None:
            s = _jnp.sqrt(_jnp.mean(_jnp.square(w)) + 1e-30)
        else:
            s = MOMENT_SCALE[name]
        km, kv = _jax.random.split(_jax.random.fold_in(key, i + 1))
        out[name] = w
        out["m_" + name] = s * _jax.random.normal(km, w.shape, _jnp.float32)
        out["v_" + name] = (s * s) * _jax.random.uniform(kv, w.shape, _jnp.float32, 0.5, 1.5)
    if N_MICROBATCH > 1:
        for name, axis in PER_EXAMPLE_BATCH_AXIS.items():
            out[name] = _to_microbatches(out[name], axis)
    return {'x': out['x'], 'meta_tokens': out['meta_tokens'], 'pre_mix_norm': out['pre_mix_norm'], 'w_in': out['w_in'], 'gla_gate_w2': out['gla_gate_w2'], 'gla_gate_b': out['gla_gate_b'], 'ret_norm_w': out['ret_norm_w'], 'gla_norm_w': out['gla_norm_w'], 'w_out': out['w_out'], 'post_mix_norm': out['post_mix_norm'], 'pre_ffn_norm': out['pre_ffn_norm'], 'ffn_up': out['ffn_up'], 'ffn_conv_w': out['ffn_conv_w'], 'ffn_conv_b': out['ffn_conv_b'], 'ffn_down': out['ffn_down'], 'post_ffn_norm': out['post_ffn_norm'], 'loss_target': out['loss_target'], 'm_meta_tokens': out['m_meta_tokens'], 'm_pre_mix_norm': out['m_pre_mix_norm'], 'm_w_in': out['m_w_in'], 'm_gla_gate_w2': out['m_gla_gate_w2'], 'm_gla_gate_b': out['m_gla_gate_b'], 'm_ret_norm_w': out['m_ret_norm_w'], 'm_gla_norm_w': out['m_gla_norm_w'], 'm_w_out': out['m_w_out'], 'm_post_mix_norm': out['m_post_mix_norm'], 'm_pre_ffn_norm': out['m_pre_ffn_norm'], 'm_ffn_up': out['m_ffn_up'], 'm_ffn_conv_w': out['m_ffn_conv_w'], 'm_ffn_conv_b': out['m_ffn_conv_b'], 'm_ffn_down': out['m_ffn_down'], 'm_post_ffn_norm': out['m_post_ffn_norm'], 'v_meta_tokens': out['v_meta_tokens'], 'v_pre_mix_norm': out['v_pre_mix_norm'], 'v_w_in': out['v_w_in'], 'v_gla_gate_w2': out['v_gla_gate_w2'], 'v_gla_gate_b': out['v_gla_gate_b'], 'v_ret_norm_w': out['v_ret_norm_w'], 'v_gla_norm_w': out['v_gla_norm_w'], 'v_w_out': out['v_w_out'], 'v_post_mix_norm': out['v_post_mix_norm'], 'v_pre_ffn_norm': out['v_pre_ffn_norm'], 'v_ffn_up': out['v_ffn_up'], 'v_ffn_conv_w': out['v_ffn_conv_w'], 'v_ffn_conv_b': out['v_ffn_conv_b'], 'v_ffn_down': out['v_ffn_down'], 'v_post_ffn_norm': out['v_post_ffn_norm']}


def _loss(weights, diff, rest, loss_target):
    with _jax.named_scope("forward"):
        args = {**rest, TWIN_DIFF_INPUT: diff, **{k: w.astype(_WEIGHT_DTYPES[k]) for k, w in weights.items()}}
        y = _forward(args)
    with _jax.named_scope("loss_head"):
        err = _jnp.square(y.astype(_jnp.float32) - loss_target)
        return 0.5 * _jnp.sum(_jnp.mean(err, axis=-1)) if err.ndim else 0.5 * err


def _adamw(w, g, m, v):
    m = ADAM_B1 * m + (1.0 - ADAM_B1) * g
    v = ADAM_B2 * v + (1.0 - ADAM_B2) * _jnp.square(g)
    m_hat = m / (1.0 - ADAM_B1 ** ADAM_STEP)
    v_hat = v / (1.0 - ADAM_B2 ** ADAM_STEP)
    delta = -ADAM_LR * (m_hat / (_jnp.sqrt(v_hat) + ADAM_EPS) + ADAM_WD * w)
    return delta, m, v


def reference(x, meta_tokens, pre_mix_norm, w_in, gla_gate_w2, gla_gate_b, ret_norm_w, gla_norm_w, w_out, post_mix_norm, pre_ffn_norm, ffn_up, ffn_conv_w, ffn_conv_b, ffn_down, post_ffn_norm, loss_target, m_meta_tokens, m_pre_mix_norm, m_w_in, m_gla_gate_w2, m_gla_gate_b, m_ret_norm_w, m_gla_norm_w, m_w_out, m_post_mix_norm, m_pre_ffn_norm, m_ffn_up, m_ffn_conv_w, m_ffn_conv_b, m_ffn_down, m_post_ffn_norm, v_meta_tokens, v_pre_mix_norm, v_w_in, v_gla_gate_w2, v_gla_gate_b, v_ret_norm_w, v_gla_norm_w, v_w_out, v_post_mix_norm, v_pre_ffn_norm, v_ffn_up, v_ffn_conv_w, v_ffn_conv_b, v_ffn_down, v_post_ffn_norm):
    given = dict(x=x, meta_tokens=meta_tokens, pre_mix_norm=pre_mix_norm, w_in=w_in, gla_gate_w2=gla_gate_w2, gla_gate_b=gla_gate_b, ret_norm_w=ret_norm_w, gla_norm_w=gla_norm_w, w_out=w_out, post_mix_norm=post_mix_norm, pre_ffn_norm=pre_ffn_norm, ffn_up=ffn_up, ffn_conv_w=ffn_conv_w, ffn_conv_b=ffn_conv_b, ffn_down=ffn_down, post_ffn_norm=post_ffn_norm, loss_target=loss_target, m_meta_tokens=m_meta_tokens, m_pre_mix_norm=m_pre_mix_norm, m_w_in=m_w_in, m_gla_gate_w2=m_gla_gate_w2, m_gla_gate_b=m_gla_gate_b, m_ret_norm_w=m_ret_norm_w, m_gla_norm_w=m_gla_norm_w, m_w_out=m_w_out, m_post_mix_norm=m_post_mix_norm, m_pre_ffn_norm=m_pre_ffn_norm, m_ffn_up=m_ffn_up, m_ffn_conv_w=m_ffn_conv_w, m_ffn_conv_b=m_ffn_conv_b, m_ffn_down=m_ffn_down, m_post_ffn_norm=m_post_ffn_norm, v_meta_tokens=v_meta_tokens, v_pre_mix_norm=v_pre_mix_norm, v_w_in=v_w_in, v_gla_gate_w2=v_gla_gate_w2, v_gla_gate_b=v_gla_gate_b, v_ret_norm_w=v_ret_norm_w, v_gla_norm_w=v_gla_norm_w, v_w_out=v_w_out, v_post_mix_norm=v_post_mix_norm, v_pre_ffn_norm=v_pre_ffn_norm, v_ffn_up=v_ffn_up, v_ffn_conv_w=v_ffn_conv_w, v_ffn_conv_b=v_ffn_conv_b, v_ffn_down=v_ffn_down, v_post_ffn_norm=v_post_ffn_norm)
    weights = {n: given[n] for n in TWIN_WEIGHTS}
    shared = {n: given[n] for n in SHARED_INPUTS}
    per_example = {n: given[n] for n in ['x']}
    grad_fn = _jax.value_and_grad(_loss, argnums=(0, 1))

    def one_microbatch(ex, loss_target):
        ex = dict(ex)
        diff = ex.pop(TWIN_DIFF_INPUT)
        return grad_fn(weights, diff, {**shared, **ex}, loss_target)

    if N_MICROBATCH == 1:
        loss, (grad_w, grad_x) = one_microbatch(per_example, given["loss_target"])
    else:
        def body(carry, xs):
            loss_sum, grad_sum = carry
            l_k, (gw_k, gx_k) = one_microbatch(xs[0], xs[1])
            with _jax.named_scope("update"):
                return (loss_sum + l_k, _jax.tree.map(_jnp.add, grad_sum, gw_k)), gx_k

        init = (_jnp.zeros((), _jnp.float32), _jax.tree.map(_jnp.zeros_like, weights))
        (loss, grad_w), grad_x = _jax.lax.scan(body, init, (per_example, given["loss_target"]))
    with _jax.named_scope("update"):
        delta_w, new_m, new_v = {}, {}, {}
        for n in TWIN_WEIGHTS:
            delta_w[n], new_m[n], new_v[n] = _adamw(weights[n], grad_w[n], given["m_" + n], given["v_" + n])
    return (loss, grad_x, *[grad_w[n] for n in TWIN_WEIGHTS], *[delta_w[n] for n in TWIN_WEIGHTS],
            *[new_m[n] for n in TWIN_WEIGHTS], *[new_v[n] for n in TWIN_WEIGHTS])
```

```python
import math

import jax
import jax.numpy as jnp
from jax import lax
from jax.experimental import pallas as pl
from jax.experimental.pallas import tpu as pltpu

F32 = jnp.float32
BF16 = jnp.bfloat16

D = 1024
SEQ = 8192
DEPTH = 2
N_META = 16
CHUNK = 64
SUB = 16
N_SUB = CHUNK // SUB
PAD_ROWS = CHUNK - N_META
LP = SEQ + CHUNK
N_CHUNKS = LP // CHUNK
RET_HEADS = 4
RET_DK = 128
GLA_HEADS = 4
GLA_DK = 64
GLA_DV = 128
GLA_TAU = 16.0
GATE_RANK = 16
IN_W = 3600
IN_WP = 3840
D_FF = 2816
D_UP = 2 * D_FF
CONV_BLOCK = 256
N_CONV_BLOCKS = D_FF // CONV_BLOCK
ROPE_BASE = 10000.0
EPS = 1e-6
N_DEV = 8
LANES = 1024

O_RQ, O_RK, O_RV, O_RG = 0, 512, 1024, 1536
O_GQ, O_GK, O_GV, O_GR, O_GA = 2048, 2304, 2560, 3072, 3584

ADAM_LR = 0.001
ADAM_B1 = 0.9
ADAM_B2 = 0.999
ADAM_EPS = 1e-08
ADAM_WD = 0.01
ADAM_STEP = 10

VMEM_LIMIT = 56 * 1024 * 1024
MESH_IDS = pl.DeviceIdType.MESH


def _row_tile(rows):
    best = 16
    for t in range(16, min(rows, 688) + 1, 16):
        if rows % t == 0:
            best = t
    return best


TM = _row_tile(LP)


def _cparams(*sem):
    return pltpu.CompilerParams(dimension_semantics=sem, vmem_limit_bytes=VMEM_LIMIT)


def _dot(a, b):
    return jnp.dot(a.astype(BF16), b.astype(BF16), preferred_element_type=F32)


def _dot_nt(a, b):
    return lax.dot_general(a.astype(BF16), b.astype(BF16), (((1,), (1,)), ((), ())), preferred_element_type=F32)


def _dot_tn(a, b):
    return lax.dot_general(a.astype(BF16), b.astype(BF16), (((0,), (0,)), ((), ())), preferred_element_type=F32)


def _split3(x):
    hi = x.astype(BF16)
    r1 = x - hi.astype(F32)
    mid = r1.astype(BF16)
    lo = (r1 - mid.astype(F32)).astype(BF16)
    return hi, mid, lo


def _dot_exact_rhs(t, x):
    hi, mid, lo = _split3(x)
    t = t.astype(BF16)
    return (jnp.dot(t, hi, preferred_element_type=F32) + jnp.dot(t, mid, preferred_element_type=F32)
            + jnp.dot(t, lo, preferred_element_type=F32))


def _dot_tn_exact_lhs(x, ones):
    dims = (((0,), (0,)), ((), ()))
    hi, mid, lo = _split3(x)
    ones = ones.astype(BF16)
    return (lax.dot_general(hi, ones, dims, preferred_element_type=F32)
            + lax.dot_general(mid, ones, dims, preferred_element_type=F32)
            + lax.dot_general(lo, ones, dims, preferred_element_type=F32))


def _sigmoid(x):
    return 1.0 / (1.0 + jnp.exp(-x))


def _matmul(a, b, *, ta=False, tb=False, out_dtype, tm, tn, tk, name):
    m = a.shape[1] if ta else a.shape[0]
    k = a.shape[0] if ta else a.shape[1]
    n = b.shape[0] if tb else b.shape[1]
    assert (b.shape[1] if tb else b.shape[0]) == k
    assert m % tm == 0 and n % tn == 0 and k % tk == 0, (name, m, n, k, tm, tn, tk)
    nk = k // tk
    a_spec = pl.BlockSpec((tk, tm), lambda i, j, kk: (kk, i)) if ta else pl.BlockSpec((tm, tk), lambda i, j, kk: (i, kk))
    b_spec = pl.BlockSpec((tn, tk), lambda i, j, kk: (j, kk)) if tb else pl.BlockSpec((tk, tn), lambda i, j, kk: (kk, j))
    dims = (((0 if ta else 1,), (1 if tb else 0,)), ((), ()))

    def body(a_ref, b_ref, o_ref, acc_ref):
        kk = pl.program_id(2)

        @pl.when(kk == 0)
        def _():
            acc_ref[...] = jnp.zeros_like(acc_ref)

        acc_ref[...] += lax.dot_general(a_ref[...].astype(BF16), b_ref[...].astype(BF16), dims,
                                        preferred_element_type=F32)

        @pl.when(kk == nk - 1)
        def _():
            o_ref[...] = acc_ref[...].astype(out_dtype)

    return pl.pallas_call(
        body, name=name, grid=(m // tm, n // tn, nk),
        in_specs=[a_spec, b_spec],
        out_specs=pl.BlockSpec((tm, tn), lambda i, j, kk: (i, j)),
        out_shape=jax.ShapeDtypeStruct((m, n), out_dtype),
        scratch_shapes=[pltpu.VMEM((tm, tn), F32)],
        compiler_params=_cparams("parallel", "parallel", "arbitrary"),
    )(a, b)


def _rmsnorm_fwd(x, w, name):
    def body(x_ref, w_ref, o_ref):
        xv = x_ref[...]
        r = lax.rsqrt(jnp.mean(xv * xv, axis=-1, keepdims=True) + EPS)
        o_ref[...] = (xv * r * w_ref[...]).astype(BF16)

    return pl.pallas_call(
        body, name=name, grid=(LP // TM,),
        in_specs=[pl.BlockSpec((TM, D), lambda i: (i, 0)), pl.BlockSpec((1, D), lambda i: (0, 0))],
        out_specs=pl.BlockSpec((TM, D), lambda i: (i, 0)),
        out_shape=jax.ShapeDtypeStruct((LP, D), BF16),
        compiler_params=_cparams("parallel"),
    )(x, w)


def _resid_norm(h, m, w, name):
    def body(h_ref, m_ref, w_ref, o_ref):
        mv = m_ref[...]
        r = lax.rsqrt(jnp.mean(mv * mv, axis=-1, keepdims=True) + EPS)
        row = pl.program_id(0) * TM + lax.broadcasted_iota(jnp.int32, (TM, 1), 0)
        o_ref[...] = h_ref[...] + jnp.where(row >= PAD_ROWS, mv * r * w_ref[...], 0.0)

    return pl.pallas_call(
        body, name=name, grid=(LP // TM,),
        in_specs=[pl.BlockSpec((TM, D), lambda i: (i, 0)), pl.BlockSpec((TM, D), lambda i: (i, 0)),
                  pl.BlockSpec((1, D), lambda i: (0, 0))],
        out_specs=pl.BlockSpec((TM, D), lambda i: (i, 0)),
        out_shape=jax.ShapeDtypeStruct((LP, D), F32),
        compiler_params=_cparams("parallel"),
    )(h, m, w)


def _norm_bwd(dy, x, w, resid, out_dtype, name):
    has_resid = resid is not None

    def body(*refs):
        if has_resid:
            dy_ref, x_ref, w_ref, r_ref, dx_ref, dw_ref = refs
        else:
            dy_ref, x_ref, w_ref, dx_ref, dw_ref = refs
        i = pl.program_id(0)

        @pl.when(i == 0)
        def _():
            dw_ref[...] = jnp.zeros_like(dw_ref)

        row = i * TM + lax.broadcasted_iota(jnp.int32, (TM, 1), 0)
        dyv = jnp.where(row >= PAD_ROWS, dy_ref[...], 0.0)
        xv = x_ref[...]
        r = lax.rsqrt(jnp.mean(xv * xv, axis=-1, keepdims=True) + EPS)
        g = dyv * w_ref[...]
        dx = r * g - xv * (r * r * r * jnp.mean(g * xv, axis=-1, keepdims=True))
        if has_resid:
            dx = dx + r_ref[...]
        dx_ref[...] = dx.astype(out_dtype)
        dw_ref[0:1, :] += jnp.sum(dyv * xv * r, axis=0, keepdims=True)

    tile = pl.BlockSpec((TM, D), lambda i: (i, 0))
    in_specs = [tile, tile, pl.BlockSpec((1, D), lambda i: (0, 0))] + ([tile] if has_resid else [])
    args = (dy, x, w) + ((resid,) if has_resid else ())
    return pl.pallas_call(
        body, name=name, grid=(LP // TM,),
        in_specs=in_specs,
        out_specs=[tile, pl.BlockSpec((8, D), lambda i: (0, 0))],
        out_shape=[jax.ShapeDtypeStruct((LP, D), out_dtype), jax.ShapeDtypeStruct((8, D), F32)],
        compiler_params=_cparams("arbitrary"),
    )(*args)


def _loss_head(y, target, name):
    def body(y_ref, t_ref, dy_ref, loss_ref):
        i = pl.program_id(0)

        @pl.when(i == 0)
        def _():
            loss_ref[...] = jnp.zeros_like(loss_ref)

        row = i * TM + lax.broadcasted_iota(jnp.int32, (TM, 1), 0)
        diff = jnp.where(row >= CHUNK, y_ref[...] - t_ref[...], 0.0)
        dy_ref[...] = diff * (1.0 / D)
        loss_ref[...] += (0.5 / D) * jnp.sum(diff * diff)

    tile = pl.BlockSpec((TM, D), lambda i: (i, 0))
    return pl.pallas_call(
        body, name=name, grid=(LP // TM,),
        in_specs=[tile, tile],
        out_specs=[tile, pl.BlockSpec((8, 128), lambda i: (0, 0))],
        out_shape=[jax.ShapeDtypeStruct((LP, D), F32), jax.ShapeDtypeStruct((8, 128), F32)],
        compiler_params=_cparams("arbitrary"),
    )(y, target)


GELU_C = math.sqrt(2.0 / math.pi)
GELU_K = 0.044715


def _shift_down(x, prev8, rows):
    row = lax.broadcasted_iota(jnp.int32, (rows, 1), 0)
    p1 = pltpu.roll(prev8, 1, 0)
    p2 = pltpu.roll(prev8, 2, 0)
    x1 = jnp.where(row == 0, p1[0:1, :], pltpu.roll(x, 1, 0))
    x2 = jnp.where(row == 0, p2[0:1, :], jnp.where(row == 1, p2[1:2, :], pltpu.roll(x, 2, 0)))
    return x1, x2


def _conv_act_fwd(u, cw8, name):
    n_rows = LP // TM
    cb2 = 2 * CONV_BLOCK

    def body(u_ref, cw_ref, act_ref, carry_ref):
        i = pl.program_id(1)

        @pl.when(i == 0)
        def _():
            carry_ref[...] = jnp.zeros_like(carry_ref)

        x = u_ref[...].astype(F32)
        x1, x2 = _shift_down(x, carry_ref[...], TM)
        conv = cw_ref[3:4, :] + x2 * cw_ref[0:1, :] + x1 * cw_ref[1:2, :] + x * cw_ref[2:3, :]
        a = conv[:, :CONV_BLOCK]
        g = conv[:, CONV_BLOCK:]
        t = jnp.tanh(GELU_C * (a + GELU_K * a * a * a))
        act_ref[...] = (0.5 * a * (1.0 + t) * g).astype(BF16)
        carry_ref[...] = x[TM - 8:TM, :]

    return pl.pallas_call(
        body, name=name, grid=(N_CONV_BLOCKS, n_rows),
        in_specs=[pl.BlockSpec((TM, cb2), lambda j, i: (i, j)), pl.BlockSpec((8, cb2), lambda j, i: (0, j))],
        out_specs=pl.BlockSpec((TM, CONV_BLOCK), lambda j, i: (i, j)),
        out_shape=jax.ShapeDtypeStruct((LP, D_FF), BF16),
        scratch_shapes=[pltpu.VMEM((8, cb2), F32)],
        compiler_params=_cparams("arbitrary", "arbitrary"),
    )(u, cw8)


def _conv_act_bwd(dact, u, cw8, name):
    n_rows = LP // TM
    cb2 = 2 * CONV_BLOCK
    halo_per_tile = TM // 16

    def body(dact_ref, u_ref, uh_ref, cw_ref, du_ref, dcw_ref, carry_ref):
        i = pl.program_id(1)
        tile = n_rows - 1 - i

        @pl.when(i == 0)
        def _():
            dcw_ref[...] = jnp.zeros_like(dcw_ref)
            carry_ref[...] = jnp.zeros_like(carry_ref)

        x = u_ref[...].astype(F32)
        prev8 = jnp.where(tile == 0, 0.0, uh_ref[8:16, :].astype(F32))
        x1, x2 = _shift_down(x, prev8, TM)
        w0, w1, w2 = cw_ref[0:1, :], cw_ref[1:2, :], cw_ref[2:3, :]
        conv = cw_ref[3:4, :] + x2 * w0 + x1 * w1 + x * w2
        a = conv[:, :CONV_BLOCK]
        g = conv[:, CONV_BLOCK:]
        t = jnp.tanh(GELU_C * (a + GELU_K * a * a * a))
        gel = 0.5 * a * (1.0 + t)
        dgel = 0.5 * (1.0 + t) + 0.5 * a * (1.0 - t * t) * (GELU_C * (1.0 + 3.0 * GELU_K * a * a))
        dav = dact_ref[...].astype(F32)
        dconv = jnp.concatenate([dav * g * dgel, dav * gel], axis=1)
        dcw_ref[0:1, :] += jnp.sum(dconv * x2, axis=0, keepdims=True)
        dcw_ref[1:2, :] += jnp.sum(dconv * x1, axis=0, keepdims=True)
        dcw_ref[2:3, :] += jnp.sum(dconv * x, axis=0, keepdims=True)
        dcw_ref[3:4, :] += jnp.sum(dconv, axis=0, keepdims=True)
        nxt = carry_ref[...]
        row = lax.broadcasted_iota(jnp.int32, (TM, 1), 0)
        d1 = jnp.where(row == TM - 1, nxt[0:1, :], pltpu.roll(dconv, TM - 1, 0))
        d2 = jnp.where(row == TM - 2, nxt[0:1, :], jnp.where(row == TM - 1, nxt[1:2, :], pltpu.roll(dconv, TM - 2, 0)))
        du_ref[...] = (dconv * w2 + d1 * w1 + d2 * w0).astype(BF16)
        carry_ref[...] = dconv[0:8, :]

    return pl.pallas_call(
        body, name=name, grid=(N_CONV_BLOCKS, n_rows),
        in_specs=[pl.BlockSpec((TM, CONV_BLOCK), lambda j, i: (n_rows - 1 - i, j)),
                  pl.BlockSpec((TM, cb2), lambda j, i: (n_rows - 1 - i, j)),
                  pl.BlockSpec((16, cb2), lambda j, i: (jnp.maximum((n_rows - 1 - i) * halo_per_tile - 1, 0), j)),
                  pl.BlockSpec((8, cb2), lambda j, i: (0, j))],
        out_specs=[pl.BlockSpec((TM, cb2), lambda j, i: (n_rows - 1 - i, j)),
                   pl.BlockSpec((8, cb2), lambda j, i: (0, j))],
        out_shape=[jax.ShapeDtypeStruct((LP, D_UP), BF16), jax.ShapeDtypeStruct((8, D_UP), F32)],
        scratch_shapes=[pltpu.VMEM((8, cb2), F32)],
        compiler_params=_cparams("arbitrary", "arbitrary"),
    )(dact, u, u, cw8)


def _ret_consts(h):
    lg = math.log(1.0 - 2.0 ** (-5.0 - h))
    ri = lax.broadcasted_iota(jnp.int32, (CHUNK, CHUNK), 0)
    ci = lax.broadcasted_iota(jnp.int32, (CHUNK, CHUNK), 1)
    diff = (ri - ci).astype(F32)
    dmat = jnp.where(diff >= 0, jnp.exp(lg * jnp.maximum(diff, 0.0)), 0.0)
    rowf = lax.broadcasted_iota(jnp.int32, (CHUNK, 1), 0).astype(F32)
    zeta = jnp.exp(lg * (CHUNK - 1.0 - rowf))
    xi = jnp.exp(lg * (rowf + 1.0))
    return dmat, zeta, xi, math.exp(lg * CHUNK)


def _rope(t, cosv, sinv):
    return t * cosv + pltpu.roll(t, RET_DK // 2, 1) * sinv


def _unrope(d, cosv, sinv):
    return d * cosv + pltpu.roll(d * sinv, RET_DK // 2, 1)


def _gla_common(p_ref, w2_ref, gb_ref, chunk):
    row = lax.broadcasted_iota(jnp.int32, (CHUNK, 1), 0)
    real = (chunk * CHUNK + row) >= PAD_ROWS
    ga = p_ref[:, O_GA:O_GA + 128]
    z = _dot(ga, w2_ref[...]) + gb_ref[...]
    la = (jnp.minimum(z, 0.0) - jnp.log(1.0 + jnp.exp(-jnp.abs(z)))) * (1.0 / GLA_TAU)
    la = jnp.where(real, la, 0.0)
    ri = lax.broadcasted_iota(jnp.int32, (CHUNK, CHUNK), 0)
    ci = lax.broadcasted_iota(jnp.int32, (CHUNK, CHUNK), 1)
    tril = (ri >= ci).astype(F32)
    cum = _dot_exact_rhs(tril, la)
    last = cum[CHUNK - 1:CHUNK, :]
    qs = p_ref[:, O_GQ:O_GQ + 256] * (GLA_DK ** -0.5)
    k = p_ref[:, O_GK:O_GK + 256]
    ecum = jnp.exp(cum)
    ekl = jnp.exp(last - cum)
    el = jnp.exp(last)
    eq_off, ek_off = [None], [None]
    for a in range(1, N_SUB):
        ref = cum[a * SUB - 1:a * SUB, :]
        eq_off.append(jnp.exp(cum[a * SUB:(a + 1) * SUB, :] - ref))
        ek_off.append(jnp.exp(jnp.minimum(ref - cum, 0.0)))
    lag_w = []
    for r in range(SUB):
        valid = (row % SUB) >= r
        if r == 0:
            lag_w.append(jnp.ones((CHUNK, 256), F32))
        else:
            lag_w.append(jnp.where(valid, jnp.exp(jnp.minimum(cum - pltpu.roll(cum, r, 0), 0.0)), 0.0))
    return dict(real=real, z=z, la=la, cum=cum, last=last, qs=qs, k=k, ecum=ecum, ekl=ekl, el=el,
                eq_off=eq_off, ek_off=ek_off, lag_w=lag_w, ri=ri, ci=ci)


def _gla_scores(c, h):
    sl = slice(GLA_DK * h, GLA_DK * (h + 1))
    qs, k = c["qs"][:, sl], c["k"][:, sl]
    ri, ci = c["ri"], c["ci"]
    p = jnp.zeros((CHUNK, CHUNK), F32)
    for r in range(SUB):
        kr = k if r == 0 else pltpu.roll(k, r, 0)
        pr = jnp.sum(qs * kr * c["lag_w"][r][:, sl], axis=1, keepdims=True)
        p = p + jnp.where(ci == ri - r, pr, 0.0)
    blocks = [jnp.zeros((SUB, CHUNK), F32)]
    for a in range(1, N_SUB):
        qh = qs[a * SUB:(a + 1) * SUB, :] * c["eq_off"][a][:, sl]
        kh = k * c["ek_off"][a][:, sl]
        blocks.append(jnp.where(ci[:SUB, :] < a * SUB, _dot_nt(qh, kh), 0.0))
    return p + jnp.concatenate(blocks, axis=0)


def _mixer_fwd(proj, cos2, sin2, w2p, gb, rnw, gnw, name):
    def body(p_ref, c_ref, s_ref, w2_ref, gb_ref, rnw_ref, gnw_ref,
             ocat_ref, mrg_ref, sr_out, sg_out, sr, sg):
        n = pl.program_id(0)

        @pl.when(n == 0)
        def _():
            sr[...] = jnp.zeros_like(sr)
            sg[...] = jnp.zeros_like(sg)

        sr_out[0] = sr[...]
        sg_out[0] = sg[...]
        cosv, sinv = c_ref[...], s_ref[...]

        for h in range(RET_HEADS):
            dmat, zeta, xi, gc = _ret_consts(h)
            hs = slice(128 * h, 128 * (h + 1))
            q = _rope(p_ref[:, O_RQ + 128 * h:O_RQ + 128 * (h + 1)], cosv, sinv)
            k = _rope(p_ref[:, O_RK + 128 * h:O_RK + 128 * (h + 1)], cosv, sinv) * (RET_DK ** -0.5)
            v = p_ref[:, O_RV + 128 * h:O_RV + 128 * (h + 1)]
            g = p_ref[:, O_RG + 128 * h:O_RG + 128 * (h + 1)]
            s_in = sr[h]
            a = _dot_nt(q, k) * dmat
            o = _dot(a, v) + _dot(q, s_in) * xi
            sr[h] = gc * s_in + _dot_tn(k * zeta, v)
            mu = jnp.mean(o, axis=-1, keepdims=True)
            xc = o - mu
            nrm = xc * lax.rsqrt(jnp.mean(xc * xc, axis=-1, keepdims=True) + EPS)
            ocat_ref[:, hs] = o
            mrg_ref[:, hs] = (nrm * rnw_ref[:, hs] * (g * _sigmoid(g))).astype(BF16)

        c = _gla_common(p_ref, w2_ref, gb_ref, n)
        lastcol = _dot_tn_exact_lhs(c["la"], jnp.ones((CHUNK, GLA_DV), F32))
        qe = c["qs"] * c["ecum"]
        kl = c["k"] * c["ekl"]
        for h in range(GLA_HEADS):
            sl = slice(GLA_DK * h, GLA_DK * (h + 1))
            hs = slice(512 + 128 * h, 512 + 128 * (h + 1))
            v = p_ref[:, O_GV + 128 * h:O_GV + 128 * (h + 1)]
            g = p_ref[:, O_GR + 128 * h:O_GR + 128 * (h + 1)]
            s_in = sg[h]
            o = _dot(_gla_scores(c, h), v) + _dot(qe[:, sl], s_in)
            sg[h] = jnp.exp(lastcol[GLA_DK * h:GLA_DK * (h + 1), :]) * s_in + _dot_tn(kl[:, sl], v)
            nrm = o * lax.rsqrt(jnp.mean(o * o, axis=-1, keepdims=True) + EPS)
            ocat_ref[:, hs] = o
            mrg_ref[:, hs] = (nrm * gnw_ref[:, 128 * h:128 * (h + 1)] * (g * _sigmoid(g))).astype(BF16)

    const = lambda shape: pl.BlockSpec(shape, lambda n: (0,) * len(shape))
    return pl.pallas_call(
        body, name=name, grid=(N_CHUNKS,),
        in_specs=[pl.BlockSpec((CHUNK, IN_WP), lambda n: (n, 0)),
                  pl.BlockSpec((CHUNK, 128), lambda n: (n, 0)), pl.BlockSpec((CHUNK, 128), lambda n: (n, 0)),
                  const((128, 256)), const((1, 256)), const((1, 512)), const((1, 512))],
        out_specs=[pl.BlockSpec((CHUNK, D), lambda n: (n, 0)), pl.BlockSpec((CHUNK, D), lambda n: (n, 0)),
                   pl.BlockSpec((1, RET_HEADS, RET_DK, 128), lambda n: (n, 0, 0, 0)),
                   pl.BlockSpec((1, GLA_HEADS, GLA_DK, GLA_DV), lambda n: (n, 0, 0, 0))],
        out_shape=[jax.ShapeDtypeStruct((LP, D), F32), jax.ShapeDtypeStruct((LP, D), BF16),
                   jax.ShapeDtypeStruct((N_CHUNKS, RET_HEADS, RET_DK, 128), F32),
                   jax.ShapeDtypeStruct((N_CHUNKS, GLA_HEADS, GLA_DK, GLA_DV), F32)],
        scratch_shapes=[pltpu.VMEM((RET_HEADS, RET_DK, 128), F32), pltpu.VMEM((GLA_HEADS, GLA_DK, GLA_DV), F32)],
        compiler_params=_cparams("arbitrary"),
    )(proj, cos2, sin2, w2p, gb, rnw, gnw)


def _mixer_bwd(proj, ocat, dmrg, sr_all, sg_all, cos2, sin2, w2p, gb, rnw, gnw, name):
    last_chunk = N_CHUNKS - 1

    def body(p_ref, ocat_ref, dm_ref, sr_ref, sg_ref, c_ref, s_ref, w2_ref, gb_ref, rnw_ref, gnw_ref,
             dp_ref, dw2_ref, dgb_ref, drn_ref, dgn_ref, dsr, dsg):
        step = pl.program_id(0)
        n = last_chunk - step

        @pl.when(step == 0)
        def _():
            dsr[...] = jnp.zeros_like(dsr)
            dsg[...] = jnp.zeros_like(dsg)
            dw2_ref[...] = jnp.zeros_like(dw2_ref)
            dgb_ref[...] = jnp.zeros_like(dgb_ref)
            drn_ref[...] = jnp.zeros_like(drn_ref)
            dgn_ref[...] = jnp.zeros_like(dgn_ref)

        cosv, sinv = c_ref[...], s_ref[...]
        row = lax.broadcasted_iota(jnp.int32, (CHUNK, 1), 0)
        real = ((n * CHUNK + row) >= PAD_ROWS).astype(F32)

        for h in range(RET_HEADS):
            dmat, zeta, xi, gc = _ret_consts(h)
            hs = slice(128 * h, 128 * (h + 1))
            q = _rope(p_ref[:, O_RQ + 128 * h:O_RQ + 128 * (h + 1)], cosv, sinv)
            k = _rope(p_ref[:, O_RK + 128 * h:O_RK + 128 * (h + 1)], cosv, sinv) * (RET_DK ** -0.5)
            v = p_ref[:, O_RV + 128 * h:O_RV + 128 * (h + 1)]
            g = p_ref[:, O_RG + 128 * h:O_RG + 128 * (h + 1)]
            o = ocat_ref[:, hs]
            dy = dm_ref[:, hs]
            wv = rnw_ref[:, hs]
            mu = jnp.mean(o, axis=-1, keepdims=True)
            xc = o - mu
            rs = lax.rsqrt(jnp.mean(xc * xc, axis=-1, keepdims=True) + EPS)
            nrm = xc * rs
            sgm = _sigmoid(g)
            sil = g * sgm
            drn_ref[0:1, hs] += jnp.sum(dy * nrm * sil, axis=0, keepdims=True)
            dgate = dy * nrm * wv * (sgm * (1.0 + g * (1.0 - sgm)))
            dn = dy * wv * sil
            do = rs * (dn - jnp.mean(dn, axis=-1, keepdims=True) - nrm * jnp.mean(dn * nrm, axis=-1, keepdims=True))
            s_in = sr_ref[0, h]
            ds_out = dsr[h]
            a = _dot_nt(q, k) * dmat
            da = _dot_nt(do, v) * dmat
            dox = do * xi
            dq = _dot(da, k) + _dot_nt(dox, s_in)
            dk = _dot_tn(da, q) + _dot_nt(v, ds_out) * zeta
            dv = _dot_tn(a, do) + _dot(k * zeta, ds_out)
            dsr[h] = gc * ds_out + _dot_tn(q, dox)
            dk = dk * (RET_DK ** -0.5)
            dp_ref[:, O_RQ + 128 * h:O_RQ + 128 * (h + 1)] = (_unrope(dq, cosv, sinv) * real).astype(BF16)
            dp_ref[:, O_RK + 128 * h:O_RK + 128 * (h + 1)] = (_unrope(dk, cosv, sinv) * real).astype(BF16)
            dp_ref[:, O_RV + 128 * h:O_RV + 128 * (h + 1)] = (dv * real).astype(BF16)
            dp_ref[:, O_RG + 128 * h:O_RG + 128 * (h + 1)] = (dgate * real).astype(BF16)

        c = _gla_common(p_ref, w2_ref, gb_ref, n)
        ri, ci = c["ri"], c["ci"]
        causal = ri >= ci
        triu = (ci >= ri).astype(F32)
        qe = c["qs"] * c["ecum"]
        kl = c["k"] * c["ekl"]
        lastcol = _dot_tn_exact_lhs(c["la"], jnp.ones((CHUNK, GLA_DV), F32))
        dla_heads, dq_heads, dk_heads = [], [], []
        for h in range(GLA_HEADS):
            sl = slice(GLA_DK * h, GLA_DK * (h + 1))
            hs = slice(512 + 128 * h, 512 + 128 * (h + 1))
            v = p_ref[:, O_GV + 128 * h:O_GV + 128 * (h + 1)]
            g = p_ref[:, O_GR + 128 * h:O_GR + 128 * (h + 1)]
            o = ocat_ref[:, hs]
            dy = dm_ref[:, hs]
            wv = gnw_ref[:, 128 * h:128 * (h + 1)]
            rs = lax.rsqrt(jnp.mean(o * o, axis=-1, keepdims=True) + EPS)
            nrm = o * rs
            sgm = _sigmoid(g)
            sil = g * sgm
            dgn_ref[0:1, 128 * h:128 * (h + 1)] += jnp.sum(dy * nrm * sil, axis=0, keepdims=True)
            dgate = dy * nrm * wv * (sgm * (1.0 + g * (1.0 - sgm)))
            dn = dy * wv * sil
            do = rs * (dn - nrm * jnp.mean(dn * nrm, axis=-1, keepdims=True))
            qs_h, k_h = c["qs"][:, sl], c["k"][:, sl]
            s_in = sg_ref[0, h]
            ds_out = dsg[h]
            el_col = jnp.exp(lastcol[GLA_DK * h:GLA_DK * (h + 1), :])
            p = _gla_scores(c, h)
            dp = jnp.where(causal, _dot_nt(do, v), 0.0)
            dv = _dot_tn(p, do) + _dot(kl[:, sl], ds_out)
            dqe = _dot_nt(do, s_in)
            dkl = _dot_nt(v, ds_out)
            dsg[h] = _dot_tn(qe[:, sl], do) + el_col * ds_out
            sd = s_in * ds_out
            sd_hi = sd.astype(BF16)
            sd_lo = (sd - sd_hi.astype(F32)).astype(BF16)
            ones8 = jnp.ones((8, GLA_DV), BF16)
            nt = (((1,), (1,)), ((), ()))
            d_el = (lax.dot_general(ones8, sd_hi, nt, preferred_element_type=F32)
                    + lax.dot_general(ones8, sd_lo, nt, preferred_element_type=F32))[0:1, :]
            dqs = dqe * c["ecum"][:, sl]
            dkk = dkl * c["ekl"][:, sl]
            d_last = jnp.sum(dkl * kl[:, sl], axis=0, keepdims=True) + d_el * c["el"][:, sl]
            dq_rows = [jnp.zeros((SUB, GLA_DK), F32)]
            for a in range(1, N_SUB):
                eq, ek = c["eq_off"][a][:, sl], c["ek_off"][a][:, sl]
                qh = qs_h[a * SUB:(a + 1) * SUB, :] * eq
                kh = k_h * ek
                dpa = jnp.where(ci[:SUB, :] < a * SUB, dp[a * SUB:(a + 1) * SUB, :], 0.0)
                dq_rows.append(_dot(dpa, kh) * eq)
                dkk = dkk + _dot_tn(dpa, qh) * ek
            dqs = dqs + jnp.concatenate(dq_rows, axis=0)
            for r in range(SUB):
                w = c["lag_w"][r][:, sl]
                dpr = jnp.sum(jnp.where(ci == ri - r, dp, 0.0), axis=1, keepdims=True)
                kr = k_h if r == 0 else pltpu.roll(k_h, r, 0)
                dqs = dqs + dpr * kr * w
                back = dpr * qs_h * w
                dkk = dkk + (back if r == 0 else pltpu.roll(back, CHUNK - r, 0))
            dcum = qs_h * dqs - k_h * dkk + jnp.where(row == CHUNK - 1, d_last, 0.0)
            dla_heads.append(_dot_exact_rhs(triu, dcum))
            dq_heads.append(dqs * (GLA_DK ** -0.5))
            dk_heads.append(dkk)
            dp_ref[:, O_GV + 128 * h:O_GV + 128 * (h + 1)] = (dv * real).astype(BF16)
            dp_ref[:, O_GR + 128 * h:O_GR + 128 * (h + 1)] = (dgate * real).astype(BF16)

        dla = jnp.concatenate(dla_heads, axis=1)
        dp_ref[:, O_GQ:O_GQ + 256] = (jnp.concatenate(dq_heads, axis=1) * real).astype(BF16)
        dp_ref[:, O_GK:O_GK + 256] = (jnp.concatenate(dk_heads, axis=1) * real).astype(BF16)
        dz = dla * (1.0 / GLA_TAU) * _sigmoid(-c["z"]) * real
        ga = p_ref[:, O_GA:O_GA + 128]
        dp_ref[:, O_GA:O_GA + 128] = _dot_nt(dz, w2_ref[...]).astype(BF16)
        dp_ref[:, O_GA + 128:IN_WP] = jnp.zeros((CHUNK, IN_WP - O_GA - 128), BF16)
        dw2_ref[...] += _dot_tn(ga, dz)
        dgb_ref[0:1, :] += jnp.sum(dz, axis=0, keepdims=True)

    const = lambda shape: pl.BlockSpec(shape, lambda s: (0,) * len(shape))
    rev = lambda s: (last_chunk - s, 0)
    return pl.pallas_call(
        body, name=name, grid=(N_CHUNKS,),
        in_specs=[pl.BlockSpec((CHUNK, IN_WP), rev), pl.BlockSpec((CHUNK, D), rev), pl.BlockSpec((CHUNK, D), rev),
                  pl.BlockSpec((1, RET_HEADS, RET_DK, 128), lambda s: (last_chunk - s, 0, 0, 0)),
                  pl.BlockSpec((1, GLA_HEADS, GLA_DK, GLA_DV), lambda s: (last_chunk - s, 0, 0, 0)),
                  pl.BlockSpec((CHUNK, 128), rev), pl.BlockSpec((CHUNK, 128), rev),
                  const((128, 256)), const((1, 256)), const((1, 512)), const((1, 512))],
        out_specs=[pl.BlockSpec((CHUNK, IN_WP), rev), const((128, 256)), const((8, 256)),
                   const((8, 512)), const((8, 512))],
        out_shape=[jax.ShapeDtypeStruct((LP, IN_WP), BF16), jax.ShapeDtypeStruct((128, 256), F32),
                   jax.ShapeDtypeStruct((8, 256), F32), jax.ShapeDtypeStruct((8, 512), F32),
                   jax.ShapeDtypeStruct((8, 512), F32)],
        scratch_shapes=[pltpu.VMEM((RET_HEADS, RET_DK, 128), F32), pltpu.VMEM((GLA_HEADS, GLA_DK, GLA_DV), F32)],
        compiler_params=_cparams("arbitrary"),
    )(proj, ocat, dmrg, sr_all, sg_all, cos2, sin2, w2p, gb, rnw, gnw)


def _all_gather(x, name):
    def body(x_ref, out_ref, send_sems, recv_sems, local_sem):
        mx, my, mc = lax.axis_index("x"), lax.axis_index("y"), lax.axis_index("c")
        me, sibling = (mx, my, mc), (mx, my, 1 - mc)
        chips = [(1 - mx, my), (mx, 1 - my), (1 - mx, 1 - my)]

        def slot(px, py, pc):
            return out_ref.at[4 * px + 2 * py + pc]

        def copy(k, block, to, src=None):
            return pltpu.make_async_remote_copy(
                src_ref=slot(*block) if src is None else src, dst_ref=slot(*block),
                send_sem=send_sems.at[k], recv_sem=recv_sems.at[k], device_id=to, device_id_type=MESH_IDS)

        mine = pltpu.make_async_copy(x_ref, slot(*me), local_sem)
        mine.start()
        first = [copy(0, me, sibling, src=x_ref)]
        first += [copy(1 + j, me, (*chip, mc), src=x_ref) for j, chip in enumerate(chips)]
        for cp in first:
            cp.start()
        passed = [copy(4 + j, (*chip, mc), sibling) for j, chip in enumerate(chips)]
        for j, chip in enumerate(chips):
            copy(1 + j, (*chip, mc), me).wait_recv()
            passed[j].start()
        copy(0, sibling, me).wait_recv()
        for j, chip in enumerate(chips):
            copy(4 + j, (*chip, 1 - mc), me).wait_recv()
        for cp in first + passed:
            cp.wait_send()
        mine.wait()

    return pl.pallas_call(
        body, name=name,
        in_specs=[pl.BlockSpec(memory_space=pl.ANY)],
        out_specs=pl.BlockSpec(memory_space=pl.ANY),
        out_shape=jax.ShapeDtypeStruct((N_DEV,) + x.shape, x.dtype),
        scratch_shapes=[pltpu.SemaphoreType.DMA((7,)), pltpu.SemaphoreType.DMA((7,)), pltpu.SemaphoreType.DMA],
    )(x)


def _exchange_blocks(g, name):
    def body(g_ref, out_ref, send_sems, recv_sems, local_sem):
        mx, my, mc = lax.axis_index("x"), lax.axis_index("y"), lax.axis_index("c")
        me = 4 * mx + 2 * my + mc
        mine = pltpu.make_async_copy(g_ref.at[me], out_ref.at[me], local_sem)
        mine.start()
        copies = []
        for r in range(1, N_DEV):
            px, py, pc = mx ^ (r >> 2), my ^ ((r >> 1) & 1), mc ^ (r & 1)
            peer = 4 * px + 2 * py + pc
            copies.append(pltpu.make_async_remote_copy(
                src_ref=g_ref.at[peer], dst_ref=out_ref.at[me],
                send_sem=send_sems.at[r - 1], recv_sem=recv_sems.at[r - 1],
                device_id=(px, py, pc), device_id_type=MESH_IDS))
        for cp in copies:
            cp.start()
        for cp in copies:
            cp.wait_recv()
        for cp in copies:
            cp.wait_send()
        mine.wait()

    return pl.pallas_call(
        body, name=name,
        in_specs=[pl.BlockSpec(memory_space=pl.ANY)],
        out_specs=pl.BlockSpec(memory_space=pl.ANY),
        out_shape=jax.ShapeDtypeStruct(g.shape, g.dtype),
        scratch_shapes=[pltpu.SemaphoreType.DMA((7,)), pltpu.SemaphoreType.DMA((7,)), pltpu.SemaphoreType.DMA],
    )(g)


def _adamw(parts, w, m, v, rows_per_step, name):
    rows = w.shape[0]
    assert rows % rows_per_step == 0

    def body(p_ref, w_ref, m_ref, v_ref, g_ref, d_ref, nm_ref, nv_ref):
        g = p_ref[0]
        for j in range(1, N_DEV):
            g = g + p_ref[j]
        m_new = ADAM_B1 * m_ref[...] + (1.0 - ADAM_B1) * g
        v_new = ADAM_B2 * v_ref[...] + (1.0 - ADAM_B2) * (g * g)
        m_hat = m_new / (1.0 - ADAM_B1 ** ADAM_STEP)
        v_hat = v_new / (1.0 - ADAM_B2 ** ADAM_STEP)
        g_ref[...] = g
        d_ref[...] = -ADAM_LR * (m_hat / (jnp.sqrt(v_hat) + ADAM_EPS) + ADAM_WD * w_ref[...])
        nm_ref[...] = m_new
        nv_ref[...] = v_new

    tile = pl.BlockSpec((rows_per_step, LANES), lambda i: (i, 0))
    shape = jax.ShapeDtypeStruct((rows, LANES), F32)
    return pl.pallas_call(
        body, name=name, grid=(rows // rows_per_step,),
        in_specs=[pl.BlockSpec((N_DEV, rows_per_step, LANES), lambda i: (0, i, 0)), tile, tile, tile],
        out_specs=[tile, tile, tile, tile],
        out_shape=[shape, shape, shape, shape],
        compiler_params=_cparams("parallel"),
    )(parts, w, m, v)


BIG = (("w_in", (DEPTH, D, IN_W // N_DEV), 2), ("w_out", (DEPTH, D // N_DEV, D), 1),
       ("ffn_up", (DEPTH, D, D_UP // N_DEV), 2), ("ffn_down", (DEPTH, D_FF // N_DEV, D), 1))
SMALL = (("meta_tokens", (N_META, D // N_DEV), 1), ("gla_gate_w2", (DEPTH, GATE_RANK, 256 // N_DEV), 2),
         ("ffn_conv_w", (DEPTH, 3, D_UP // N_DEV), 2))
SHARDED = BIG + SMALL
REPL = (("pre_mix_norm", (DEPTH, D)), ("gla_gate_b", (DEPTH, 256)), ("ret_norm_w", (DEPTH, 512)),
        ("gla_norm_w", (DEPTH, 512)), ("post_mix_norm", (DEPTH, D)), ("pre_ffn_norm", (DEPTH, D)),
        ("ffn_conv_b", (DEPTH, D_UP)), ("post_ffn_norm", (DEPTH, D)))
WEIGHT_ORDER = ("meta_tokens", "pre_mix_norm", "w_in", "gla_gate_w2", "gla_gate_b", "ret_norm_w", "gla_norm_w",
                "w_out", "post_mix_norm", "pre_ffn_norm", "ffn_up", "ffn_conv_w", "ffn_conv_b", "ffn_down",
                "post_ffn_norm")


def _size(shape):
    return math.prod(shape)


def _round_up(n, mult):
    return -(-n // mult) * mult


SHARD_ROWS_PER_STEP = 128
SHARD_ROWS = _round_up(-(-sum(_size(s) for _, s, _ in SHARDED) // LANES), SHARD_ROWS_PER_STEP)
REPL_ROWS = _round_up(-(-sum(_size(s) for _, s in REPL) // LANES), 8)
BIG_ROWS = sum(_size(s) for _, s, _ in BIG) // LANES
SMALL_ROWS = _round_up(-(-sum(_size(s) for _, s, _ in SMALL) // LANES), 8)


def _pack(arrays, rows, dtype):
    flat = jnp.concatenate([a.reshape(-1).astype(dtype) for a in arrays])
    return jnp.pad(flat, (0, rows * LANES - flat.shape[0])).reshape(rows, LANES)


def _unpack(buf, shapes):
    flat = buf.reshape(-1)
    out, off = [], 0
    for shape in shapes:
        out.append(flat[off:off + _size(shape)].reshape(shape))
        off += _size(shape)
    return out


def _unshard(blocks, axis):
    moved = jnp.moveaxis(blocks, 0, axis)
    shape = list(moved.shape)
    shape[axis:axis + 2] = [shape[axis] * shape[axis + 1]]
    return moved.reshape(shape)


def _to_blocks(full, axis):
    shape = list(full.shape)
    shape[axis:axis + 1] = [N_DEV, shape[axis] // N_DEV]
    return jnp.moveaxis(full.reshape(shape), axis, 0)


def _interleave_cols(w):
    lead = w.shape[:-1]
    return jnp.swapaxes(w.reshape(lead + (2, N_CONV_BLOCKS, CONV_BLOCK)), -3, -2).reshape(lead + (D_UP,))


def _deinterleave_cols(w):
    lead = w.shape[:-1]
    return jnp.swapaxes(w.reshape(lead + (N_CONV_BLOCKS, 2, CONV_BLOCK)), -3, -2).reshape(lead + (D_UP,))


def _rope_tables():
    half = RET_DK // 2
    inv = ROPE_BASE ** (-jnp.arange(half, dtype=F32) / half)
    pos = jnp.arange(LP, dtype=F32) - float(PAD_ROWS)
    ang = pos[:, None] * inv[None, :]
    c, s = jnp.cos(ang), jnp.sin(ang)
    return jnp.concatenate([c, c], axis=1), jnp.concatenate([-s, s], axis=1)


def kernel(x, meta_tokens, pre_mix_norm, w_in, gla_gate_w2, gla_gate_b, ret_norm_w, gla_norm_w, w_out, post_mix_norm, pre_ffn_norm, ffn_up, ffn_conv_w, ffn_conv_b, ffn_down, post_ffn_norm, loss_target, m_meta_tokens, m_pre_mix_norm, m_w_in, m_gla_gate_w2, m_gla_gate_b, m_ret_norm_w, m_gla_norm_w, m_w_out, m_post_mix_norm, m_pre_ffn_norm, m_ffn_up, m_ffn_conv_w, m_ffn_conv_b, m_ffn_down, m_post_ffn_norm, v_meta_tokens, v_pre_mix_norm, v_w_in, v_gla_gate_w2, v_gla_gate_b, v_ret_norm_w, v_gla_norm_w, v_w_out, v_post_mix_norm, v_pre_ffn_norm, v_ffn_up, v_ffn_conv_w, v_ffn_conv_b, v_ffn_down, v_post_ffn_norm):
    weights = dict(meta_tokens=meta_tokens, pre_mix_norm=pre_mix_norm, w_in=w_in, gla_gate_w2=gla_gate_w2,
                   gla_gate_b=gla_gate_b, ret_norm_w=ret_norm_w, gla_norm_w=gla_norm_w, w_out=w_out,
                   post_mix_norm=post_mix_norm, pre_ffn_norm=pre_ffn_norm, ffn_up=ffn_up, ffn_conv_w=ffn_conv_w,
                   ffn_conv_b=ffn_conv_b, ffn_down=ffn_down, post_ffn_norm=post_ffn_norm)
    mom1 = dict(meta_tokens=m_meta_tokens, pre_mix_norm=m_pre_mix_norm, w_in=m_w_in, gla_gate_w2=m_gla_gate_w2,
                gla_gate_b=m_gla_gate_b, ret_norm_w=m_ret_norm_w, gla_norm_w=m_gla_norm_w, w_out=m_w_out,
                post_mix_norm=m_post_mix_norm, pre_ffn_norm=m_pre_ffn_norm, ffn_up=m_ffn_up,
                ffn_conv_w=m_ffn_conv_w, ffn_conv_b=m_ffn_conv_b, ffn_down=m_ffn_down, post_ffn_norm=m_post_ffn_norm)
    mom2 = dict(meta_tokens=v_meta_tokens, pre_mix_norm=v_pre_mix_norm, w_in=v_w_in, gla_gate_w2=v_gla_gate_w2,
                gla_gate_b=v_gla_gate_b, ret_norm_w=v_ret_norm_w, gla_norm_w=v_gla_norm_w, w_out=v_w_out,
                post_mix_norm=v_post_mix_norm, pre_ffn_norm=v_pre_ffn_norm, ffn_up=v_ffn_up,
                ffn_conv_w=v_ffn_conv_w, ffn_conv_b=v_ffn_conv_b, ffn_down=v_ffn_down, post_ffn_norm=v_post_ffn_norm)

    big = _all_gather(_pack([weights[n] for n, _, _ in BIG], BIG_ROWS, BF16), "gather_big_weights")
    small = _all_gather(_pack([weights[n] for n, _, _ in SMALL], SMALL_ROWS, F32), "gather_small_weights")
    big_parts = _unpack_blocks(big, [s for _, s, _ in BIG])
    small_parts = _unpack_blocks(small, [s for _, s, _ in SMALL])
    full = {n: _unshard(p, ax) for (n, _, ax), p in zip(BIG, big_parts)}
    full.update({n: _unshard(p, ax) for (n, _, ax), p in zip(SMALL, small_parts)})
    w_in_p = jnp.pad(full["w_in"], ((0, 0), (0, 0), (0, IN_WP - IN_W)))
    w_up_i = _interleave_cols(full["ffn_up"])
    w2p = jnp.pad(full["gla_gate_w2"], ((0, 0), (0, 128 - GATE_RANK), (0, 0)))
    cw8 = jnp.concatenate([_interleave_cols(full["ffn_conv_w"]), _interleave_cols(ffn_conv_b)[:, None, :],
                           jnp.zeros((DEPTH, 4, D_UP), F32)], axis=1)
    cos2, sin2 = _rope_tables()

    h = jnp.concatenate([jnp.zeros((PAD_ROWS, D), F32), full["meta_tokens"], x[0]], axis=0)
    target = jnp.concatenate([jnp.zeros((CHUNK, D), F32), loss_target[0]], axis=0)
    saved = []
    for l in range(DEPTH):
        a1 = _rmsnorm_fwd(h, pre_mix_norm[l:l + 1], f"pre_mix_norm_{l}")
        proj = _matmul(a1, w_in_p[l], out_dtype=F32, tm=TM, tn=1280, tk=D, name=f"in_proj_{l}")
        ocat, merged, sr_all, sg_all = _mixer_fwd(proj, cos2, sin2, w2p[l], gla_gate_b[l:l + 1],
                                                  ret_norm_w[l:l + 1], gla_norm_w[l:l + 1], f"mixer_fwd_{l}")
        m = _matmul(merged, full["w_out"][l], out_dtype=F32, tm=TM, tn=D, tk=D, name=f"out_proj_{l}")
        h1 = _resid_norm(h, m, post_mix_norm[l:l + 1], f"post_mix_norm_{l}")
        a2 = _rmsnorm_fwd(h1, pre_ffn_norm[l:l + 1], f"pre_ffn_norm_{l}")
        u = _matmul(a2, w_up_i[l], out_dtype=BF16, tm=TM, tn=1408, tk=D, name=f"ffn_up_{l}")
        act = _conv_act_fwd(u, cw8[l], f"ffn_conv_act_{l}")
        f = _matmul(act, full["ffn_down"][l], out_dtype=F32, tm=TM, tn=D, tk=D_FF, name=f"ffn_down_{l}")
        h2 = _resid_norm(h1, f, post_ffn_norm[l:l + 1], f"post_ffn_norm_{l}")
        saved.append(dict(h=h, a1=a1, proj=proj, ocat=ocat, merged=merged, sr=sr_all, sg=sg_all, m=m, h1=h1,
                          a2=a2, u=u, act=act, f=f))
        h = h2

    dh, loss_acc = _loss_head(h, target, "loss_head")
    loss = lax.psum(loss_acc[0, 0], ("x", "y", "c"))

    grads = {n: [None] * DEPTH for n in WEIGHT_ORDER if n != "meta_tokens"}
    for l in reversed(range(DEPTH)):
        s = saved[l]
        df, g_post_ffn = _norm_bwd(dh, s["f"], post_ffn_norm[l:l + 1], None, BF16, f"post_ffn_norm_bwd_{l}")
        dact = _matmul(df, full["ffn_down"][l], tb=True, out_dtype=BF16, tm=TM, tn=D_FF, tk=D, name=f"ffn_down_dx_{l}")
        g_down = _matmul(s["act"], df, ta=True, out_dtype=F32, tm=D_FF // 2, tn=D, tk=TM, name=f"ffn_down_dw_{l}")
        du, dcw = _conv_act_bwd(dact, s["u"], cw8[l], f"ffn_conv_act_bwd_{l}")
        da2 = _matmul(du, w_up_i[l], tb=True, out_dtype=F32, tm=TM, tn=D, tk=1408, name=f"ffn_up_dx_{l}")
        g_up = _matmul(s["a2"], du, ta=True, out_dtype=F32, tm=D, tn=1408, tk=TM, name=f"ffn_up_dw_{l}")
        dh1, g_pre_ffn = _norm_bwd(da2, s["h1"], pre_ffn_norm[l:l + 1], dh, F32, f"pre_ffn_norm_bwd_{l}")
        dm, g_post_mix = _norm_bwd(dh1, s["m"], post_mix_norm[l:l + 1], None, BF16, f"post_mix_norm_bwd_{l}")
        dmerged = _matmul(dm, full["w_out"][l], tb=True, out_dtype=F32, tm=TM, tn=D, tk=D, name=f"out_proj_dx_{l}")
        g_out = _matmul(s["merged"], dm, ta=True, out_dtype=F32, tm=D, tn=D, tk=TM, name=f"out_proj_dw_{l}")
        dproj, g_w2, g_gb, g_rn, g_gn = _mixer_bwd(s["proj"], s["ocat"], dmerged, s["sr"], s["sg"], cos2, sin2,
                                                   w2p[l], gla_gate_b[l:l + 1], ret_norm_w[l:l + 1],
                                                   gla_norm_w[l:l + 1], f"mixer_bwd_{l}")
        da1 = _matmul(dproj, w_in_p[l], tb=True, out_dtype=F32, tm=TM, tn=D, tk=1280, name=f"in_proj_dx_{l}")
        g_in = _matmul(s["a1"], dproj, ta=True, out_dtype=F32, tm=D, tn=1280, tk=TM, name=f"in_proj_dw_{l}")
        dh, g_pre_mix = _norm_bwd(da1, s["h"], pre_mix_norm[l:l + 1], dh1, F32, f"pre_mix_norm_bwd_{l}")
        grads["post_ffn_norm"][l] = g_post_ffn[0]
        grads["ffn_down"][l] = g_down
        grads["ffn_conv_w"][l] = _deinterleave_cols(dcw[0:3])
        grads["ffn_conv_b"][l] = _deinterleave_cols(dcw[3])
        grads["ffn_up"][l] = _deinterleave_cols(g_up)
        grads["pre_ffn_norm"][l] = g_pre_ffn[0]
        grads["post_mix_norm"][l] = g_post_mix[0]
        grads["w_out"][l] = g_out
        grads["gla_gate_w2"][l] = g_w2[:GATE_RANK]
        grads["gla_gate_b"][l] = g_gb[0]
        grads["ret_norm_w"][l] = g_rn[0]
        grads["gla_norm_w"][l] = g_gn[0]
        grads["w_in"][l] = g_in[:, :IN_W]
        grads["pre_mix_norm"][l] = g_pre_mix[0]
    local = {n: jnp.stack(v) for n, v in grads.items()}
    local["meta_tokens"] = dh[PAD_ROWS:CHUNK]
    grad_x = dh[CHUNK:][None]

    blocks = jnp.concatenate([_to_blocks(local[n], ax).reshape(N_DEV, -1) for n, _, ax in SHARDED], axis=1)
    blocks = jnp.pad(blocks, ((0, 0), (0, SHARD_ROWS * LANES - blocks.shape[1]))).reshape(N_DEV, SHARD_ROWS, LANES)
    parts = _exchange_blocks(blocks, "exchange_grad_blocks")
    shard_shapes = [s for _, s, _ in SHARDED]
    packed = [_pack([d[n] for n, _, _ in SHARDED], SHARD_ROWS, F32) for d in (weights, mom1, mom2)]
    results = _adamw(parts, *packed, SHARD_ROWS_PER_STEP, "adamw_sharded")
    out = {kind: dict(zip([n for n, _, _ in SHARDED], _unpack(buf, shard_shapes)))
           for kind, buf in zip(("grad", "delta", "new_m", "new_v"), results)}

    repl_parts = _all_gather(_pack([local[n] for n, _ in REPL], REPL_ROWS, F32), "gather_small_grads")
    packed = [_pack([d[n] for n, _ in REPL], REPL_ROWS, F32) for d in (weights, mom1, mom2)]
    results = _adamw(repl_parts, *packed, REPL_ROWS, "adamw_replicated")
    repl_shapes = [s for _, s in REPL]
    for kind, buf in zip(("grad", "delta", "new_m", "new_v"), results):
        out[kind].update(zip([n for n, _ in REPL], _unpack(buf, repl_shapes)))

    return (loss, grad_x, *[out["grad"][n] for n in WEIGHT_ORDER], *[out["delta"][n] for n in WEIGHT_ORDER],
            *[out["new_m"][n] for n in WEIGHT_ORDER], *[out["new_v"][n] for n in WEIGHT_ORDER])


def _unpack_blocks(gathered, shapes):
    flat = gathered.reshape(N_DEV, -1)
    out, off = [], 0
    for shape in shapes:
        out.append(flat[:, off:off + _size(shape)].reshape((N_DEV,) + shape))
        off += _size(shape)
    return out
```

```python
import math

import jax
import jax.numpy as jnp
from jax import lax
from jax.experimental import pallas as pl
from jax.experimental.pallas import tpu as pltpu

F32 = jnp.float32
BF16 = jnp.bfloat16

D = 1024
SEQ = 8192
DEPTH = 2
N_META = 16
CHUNK = 64
SUB = 16
N_SUB = CHUNK // SUB
PAD_ROWS = CHUNK - N_META
LP = SEQ + CHUNK
N_CHUNKS = LP // CHUNK
RET_HEADS = 4
RET_DK = 128
GLA_HEADS = 4
GLA_DK = 64
GLA_DV = 128
GLA_TAU = 16.0
GATE_RANK = 16
IN_W = 3600
IN_WP = 3840
D_FF = 2816
D_UP = 2 * D_FF
CONV_BLOCK = 256
N_CONV_BLOCKS = D_FF // CONV_BLOCK
ROPE_BASE = 10000.0
EPS = 1e-6
N_DEV = 8
LANES = 1024

O_RQ, O_RK, O_RV, O_RG = 0, 512, 1024, 1536
O_GQ, O_GK, O_GV, O_GR, O_GA = 2048, 2304, 2560, 3072, 3584

ADAM_LR = 0.001
ADAM_B1 = 0.9
ADAM_B2 = 0.999
ADAM_EPS = 1e-08
ADAM_WD = 0.01
ADAM_STEP = 10

VMEM_LIMIT = 56 * 1024 * 1024
MESH_IDS = pl.DeviceIdType.MESH


def _row_tile(rows):
    best = 16
    for t in range(16, min(rows, 688) + 1, 16):
        if rows % t == 0:
            best = t
    return best


TM = _row_tile(LP)


def _cparams(*sem):
    return pltpu.CompilerParams(dimension_semantics=sem, vmem_limit_bytes=VMEM_LIMIT)


def _dot(a, b):
    return jnp.dot(a.astype(BF16), b.astype(BF16), preferred_element_type=F32)


def _dot_nt(a, b):
    return lax.dot_general(a.astype(BF16), b.astype(BF16), (((1,), (1,)), ((), ())), preferred_element_type=F32)


def _dot_tn(a, b):
    return lax.dot_general(a.astype(BF16), b.astype(BF16), (((0,), (0,)), ((), ())), preferred_element_type=F32)


def _split3(x):
    hi = x.astype(BF16)
    r1 = x - hi.astype(F32)
    mid = r1.astype(BF16)
    lo = (r1 - mid.astype(F32)).astype(BF16)
    return hi, mid, lo


def _dot_exact_rhs(t, x):
    hi, mid, lo = _split3(x)
    t = t.astype(BF16)
    return (jnp.dot(t, hi, preferred_element_type=F32) + jnp.dot(t, mid, preferred_element_type=F32)
            + jnp.dot(t, lo, preferred_element_type=F32))


def _dot_tn_exact_lhs(x, ones):
    dims = (((0,), (0,)), ((), ()))
    hi, mid, lo = _split3(x)
    ones = ones.astype(BF16)
    return (lax.dot_general(hi, ones, dims, preferred_element_type=F32)
            + lax.dot_general(mid, ones, dims, preferred_element_type=F32)
            + lax.dot_general(lo, ones, dims, preferred_element_type=F32))


def _sigmoid(x):
    return 1.0 / (1.0 + jnp.exp(-x))


def _matmul(a, b, *, ta=False, tb=False, out_dtype, tm, tn, tk, name):
    m = a.shape[1] if ta else a.shape[0]
    k = a.shape[0] if ta else a.shape[1]
    n = b.shape[0] if tb else b.shape[1]
    assert (b.shape[1] if tb else b.shape[0]) == k
    assert m % tm == 0 and n % tn == 0 and k % tk == 0, (name, m, n, k, tm, tn, tk)
    nk = k // tk
    a_spec = pl.BlockSpec((tk, tm), lambda i, j, kk: (kk, i)) if ta else pl.BlockSpec((tm, tk), lambda i, j, kk: (i, kk))
    b_spec = pl.BlockSpec((tn, tk), lambda i, j, kk: (j, kk)) if tb else pl.BlockSpec((tk, tn), lambda i, j, kk: (kk, j))
    dims = (((0 if ta else 1,), (1 if tb else 0,)), ((), ()))

    def body(a_ref, b_ref, o_ref, acc_ref):
        kk = pl.program_id(2)

        @pl.when(kk == 0)
        def _():
            acc_ref[...] = jnp.zeros_like(acc_ref)

        acc_ref[...] += lax.dot_general(a_ref[...].astype(BF16), b_ref[...].astype(BF16), dims,
                                        preferred_element_type=F32)

        @pl.when(kk == nk - 1)
        def _():
            o_ref[...] = acc_ref[...].astype(out_dtype)

    return pl.pallas_call(
        body, name=name, grid=(m // tm, n // tn, nk),
        in_specs=[a_spec, b_spec],
        out_specs=pl.BlockSpec((tm, tn), lambda i, j, kk: (i, j)),
        out_shape=jax.ShapeDtypeStruct((m, n), out_dtype),
        scratch_shapes=[pltpu.VMEM((tm, tn), F32)],
        compiler_params=_cparams("parallel", "parallel", "arbitrary"),
    )(a, b)


def _rmsnorm_fwd(x, w, name):
    def body(x_ref, w_ref, o_ref):
        xv = x_ref[...]
        r = lax.rsqrt(jnp.mean(xv * xv, axis=-1, keepdims=True) + EPS)
        o_ref[...] = (xv * r * w_ref[...]).astype(BF16)

    return pl.pallas_call(
        body, name=name, grid=(LP // TM,),
        in_specs=[pl.BlockSpec((TM, D), lambda i: (i, 0)), pl.BlockSpec((1, D), lambda i: (0, 0))],
        out_specs=pl.BlockSpec((TM, D), lambda i: (i, 0)),
        out_shape=jax.ShapeDtypeStruct((LP, D), BF16),
        compiler_params=_cparams("parallel"),
    )(x, w)


def _resid_norm(h, m, w, name):
    def body(h_ref, m_ref, w_ref, o_ref):
        mv = m_ref[...]
        r = lax.rsqrt(jnp.mean(mv * mv, axis=-1, keepdims=True) + EPS)
        row = pl.program_id(0) * TM + lax.broadcasted_iota(jnp.int32, (TM, 1), 0)
        o_ref[...] = h_ref[...] + jnp.where(row >= PAD_ROWS, mv * r * w_ref[...], 0.0)

    return pl.pallas_call(
        body, name=name, grid=(LP // TM,),
        in_specs=[pl.BlockSpec((TM, D), lambda i: (i, 0)), pl.BlockSpec((TM, D), lambda i: (i, 0)),
                  pl.BlockSpec((1, D), lambda i: (0, 0))],
        out_specs=pl.BlockSpec((TM, D), lambda i: (i, 0)),
        out_shape=jax.ShapeDtypeStruct((LP, D), F32),
        compiler_params=_cparams("parallel"),
    )(h, m, w)


def _norm_bwd(dy, x, w, resid, out_dtype, name):
    has_resid = resid is not None

    def body(*refs):
        if has_resid:
            dy_ref, x_ref, w_ref, r_ref, dx_ref, dw_ref = refs
        else:
            dy_ref, x_ref, w_ref, dx_ref, dw_ref = refs
        i = pl.program_id(0)

        @pl.when(i == 0)
        def _():
            dw_ref[...] = jnp.zeros_like(dw_ref)

        row = i * TM + lax.broadcasted_iota(jnp.int32, (TM, 1), 0)
        dyv = jnp.where(row >= PAD_ROWS, dy_ref[...], 0.0)
        xv = x_ref[...]
        r = lax.rsqrt(jnp.mean(xv * xv, axis=-1, keepdims=True) + EPS)
        g = dyv * w_ref[...]
        dx = r * g - xv * (r * r * r * jnp.mean(g * xv, axis=-1, keepdims=True))
        if has_resid:
            dx = dx + r_ref[...]
        dx_ref[...] = dx.astype(out_dtype)
        dw_ref[0:1, :] += jnp.sum(dyv * xv * r, axis=0, keepdims=True)

    tile = pl.BlockSpec((TM, D), lambda i: (i, 0))
    in_specs = [tile, tile, pl.BlockSpec((1, D), lambda i: (0, 0))] + ([tile] if has_resid else [])
    args = (dy, x, w) + ((resid,) if has_resid else ())
    return pl.pallas_call(
        body, name=name, grid=(LP // TM,),
        in_specs=in_specs,
        out_specs=[tile, pl.BlockSpec((8, D), lambda i: (0, 0))],
        out_shape=[jax.ShapeDtypeStruct((LP, D), out_dtype), jax.ShapeDtypeStruct((8, D), F32)],
        compiler_params=_cparams("arbitrary"),
    )(*args)


def _loss_head(y, target, name):
    def body(y_ref, t_ref, dy_ref, loss_ref):
        i = pl.program_id(0)

        @pl.when(i == 0)
        def _():
            loss_ref[...] = jnp.zeros_like(loss_ref)

        row = i * TM + lax.broadcasted_iota(jnp.int32, (TM, 1), 0)
        diff = jnp.where(row >= CHUNK, y_ref[...] - t_ref[...], 0.0)
        dy_ref[...] = diff * (1.0 / D)
        loss_ref[...] += (0.5 / D) * jnp.sum(diff * diff)

    tile = pl.BlockSpec((TM, D), lambda i: (i, 0))
    return pl.pallas_call(
        body, name=name, grid=(LP // TM,),
        in_specs=[tile, tile],
        out_specs=[tile, pl.BlockSpec((8, 128), lambda i: (0, 0))],
        out_shape=[jax.ShapeDtypeStruct((LP, D), F32), jax.ShapeDtypeStruct((8, 128), F32)],
        compiler_params=_cparams("arbitrary"),
    )(y, target)


GELU_C = math.sqrt(2.0 / math.pi)
GELU_K = 0.044715


def _shift_down(x, prev8, rows):
    row = lax.broadcasted_iota(jnp.int32, (rows, 1), 0)
    p1 = pltpu.roll(prev8, 1, 0)
    p2 = pltpu.roll(prev8, 2, 0)
    x1 = jnp.where(row == 0, p1[0:1, :], pltpu.roll(x, 1, 0))
    x2 = jnp.where(row == 0, p2[0:1, :], jnp.where(row == 1, p2[1:2, :], pltpu.roll(x, 2, 0)))
    return x1, x2


def _conv_act_fwd(u, cw8, name):
    n_rows = LP // TM
    cb2 = 2 * CONV_BLOCK

    def body(u_ref, cw_ref, act_ref, carry_ref):
        i = pl.program_id(1)

        @pl.when(i == 0)
        def _():
            carry_ref[...] = jnp.zeros_like(carry_ref)

        x = u_ref[...].astype(F32)
        x1, x2 = _shift_down(x, carry_ref[...], TM)
        conv = cw_ref[3:4, :] + x2 * cw_ref[0:1, :] + x1 * cw_ref[1:2, :] + x * cw_ref[2:3, :]
        a = conv[:, :CONV_BLOCK]
        g = conv[:, CONV_BLOCK:]
        t = jnp.tanh(GELU_C * (a + GELU_K * a * a * a))
        act_ref[...] = (0.5 * a * (1.0 + t) * g).astype(BF16)
        carry_ref[...] = x[TM - 8:TM, :]

    return pl.pallas_call(
        body, name=name, grid=(N_CONV_BLOCKS, n_rows),
        in_specs=[pl.BlockSpec((TM, cb2), lambda j, i: (i, j)), pl.BlockSpec((8, cb2), lambda j, i: (0, j))],
        out_specs=pl.BlockSpec((TM, CONV_BLOCK), lambda j, i: (i, j)),
        out_shape=jax.ShapeDtypeStruct((LP, D_FF), BF16),
        scratch_shapes=[pltpu.VMEM((8, cb2), F32)],
        compiler_params=_cparams("arbitrary", "arbitrary"),
    )(u, cw8)


def _conv_act_bwd(dact, u, cw8, name):
    n_rows = LP // TM
    cb2 = 2 * CONV_BLOCK
    halo_per_tile = TM // 16

    def body(dact_ref, u_ref, uh_ref, cw_ref, du_ref, dcw_ref, carry_ref):
        i = pl.program_id(1)
        tile = n_rows - 1 - i

        @pl.when(i == 0)
        def _():
            dcw_ref[...] = jnp.zeros_like(dcw_ref)
            carry_ref[...] = jnp.zeros_like(carry_ref)

        x = u_ref[...].astype(F32)
        prev8 = jnp.where(tile == 0, 0.0, uh_ref[8:16, :].astype(F32))
        x1, x2 = _shift_down(x, prev8, TM)
        w0, w1, w2 = cw_ref[0:1, :], cw_ref[1:2, :], cw_ref[2:3, :]
        conv = cw_ref[3:4, :] + x2 * w0 + x1 * w1 + x * w2
        a = conv[:, :CONV_BLOCK]
        g = conv[:, CONV_BLOCK:]
        t = jnp.tanh(GELU_C * (a + GELU_K * a * a * a))
        gel = 0.5 * a * (1.0 + t)
        dgel = 0.5 * (1.0 + t) + 0.5 * a * (1.0 - t * t) * (GELU_C * (1.0 + 3.0 * GELU_K * a * a))
        dav = dact_ref[...].astype(F32)
        dconv = jnp.concatenate([dav * g * dgel, dav * gel], axis=1)
        dcw_ref[0:1, :] += jnp.sum(dconv * x2, axis=0, keepdims=True)
        dcw_ref[1:2, :] += jnp.sum(dconv * x1, axis=0, keepdims=True)
        dcw_ref[2:3, :] += jnp.sum(dconv * x, axis=0, keepdims=True)
        dcw_ref[3:4, :] += jnp.sum(dconv, axis=0, keepdims=True)
        nxt = carry_ref[...]
        row = lax.broadcasted_iota(jnp.int32, (TM, 1), 0)
        d1 = jnp.where(row == TM - 1, nxt[0:1, :], pltpu.roll(dconv, TM - 1, 0))
        d2 = jnp.where(row == TM - 2, nxt[0:1, :], jnp.where(row == TM - 1, nxt[1:2, :], pltpu.roll(dconv, TM - 2, 0)))
        du_ref[...] = (dconv * w2 + d1 * w1 + d2 * w0).astype(BF16)
        carry_ref[...] = dconv[0:8, :]

    return pl.pallas_call(
        body, name=name, grid=(N_CONV_BLOCKS, n_rows),
        in_specs=[pl.BlockSpec((TM, CONV_BLOCK), lambda j, i: (n_rows - 1 - i, j)),
                  pl.BlockSpec((TM, cb2), lambda j, i: (n_rows - 1 - i, j)),
                  pl.BlockSpec((16, cb2), lambda j, i: (jnp.maximum((n_rows - 1 - i) * halo_per_tile - 1, 0), j)),
                  pl.BlockSpec((8, cb2), lambda j, i: (0, j))],
        out_specs=[pl.BlockSpec((TM, cb2), lambda j, i: (n_rows - 1 - i, j)),
                   pl.BlockSpec((8, cb2), lambda j, i: (0, j))],
        out_shape=[jax.ShapeDtypeStruct((LP, D_UP), BF16), jax.ShapeDtypeStruct((8, D_UP), F32)],
        scratch_shapes=[pltpu.VMEM((8, cb2), F32)],
        compiler_params=_cparams("arbitrary", "arbitrary"),
    )(dact, u, u, cw8)


def _ret_consts(h):
    lg = math.log(1.0 - 2.0 ** (-5.0 - h))
    ri = lax.broadcasted_iota(jnp.int32, (CHUNK, CHUNK), 0)
    ci = lax.broadcasted_iota(jnp.int32, (CHUNK, CHUNK), 1)
    diff = (ri - ci).astype(F32)
    dmat = jnp.where(diff >= 0, jnp.exp(lg * jnp.maximum(diff, 0.0)), 0.0)
    rowf = lax.broadcasted_iota(jnp.int32, (CHUNK, 1), 0).astype(F32)
    zeta = jnp.exp(lg * (CHUNK - 1.0 - rowf))
    xi = jnp.exp(lg * (rowf + 1.0))
    return dmat, zeta, xi, math.exp(lg * CHUNK)


def _rope(t, cosv, sinv):
    return t * cosv + pltpu.roll(t, RET_DK // 2, 1) * sinv


def _unrope(d, cosv, sinv):
    return d * cosv + pltpu.roll(d * sinv, RET_DK // 2, 1)


def _gla_common(p_ref, w2_ref, gb_ref, chunk):
    row = lax.broadcasted_iota(jnp.int32, (CHUNK, 1), 0)
    real = (chunk * CHUNK + row) >= PAD_ROWS
    ga = p_ref[:, O_GA:O_GA + 128]
    z = _dot(ga, w2_ref[...]) + gb_ref[...]
    la = (jnp.minimum(z, 0.0) - jnp.log(1.0 + jnp.exp(-jnp.abs(z)))) * (1.0 / GLA_TAU)
    la = jnp.where(real, la, 0.0)
    ri = lax.broadcasted_iota(jnp.int32, (CHUNK, CHUNK), 0)
    ci = lax.broadcasted_iota(jnp.int32, (CHUNK, CHUNK), 1)
    tril = (ri >= ci).astype(F32)
    cum = _dot_exact_rhs(tril, la)
    last = cum[CHUNK - 1:CHUNK, :]
    qs = p_ref[:, O_GQ:O_GQ + 256] * (GLA_DK ** -0.5)
    k = p_ref[:, O_GK:O_GK + 256]
    ecum = jnp.exp(cum)
    ekl = jnp.exp(last - cum)
    el = jnp.exp(last)
    eq_off, ek_off = [None], [None]
    for a in range(1, N_SUB):
        ref = cum[a * SUB - 1:a * SUB, :]
        eq_off.append(jnp.exp(cum[a * SUB:(a + 1) * SUB, :] - ref))
        ek_off.append(jnp.exp(jnp.minimum(ref - cum, 0.0)))
    lag_w = []
    for r in range(SUB):
        valid = (row % SUB) >= r
        if r == 0:
            lag_w.append(jnp.ones((CHUNK, 256), F32))
        else:
            lag_w.append(jnp.where(valid, jnp.exp(jnp.minimum(cum - pltpu.roll(cum, r, 0), 0.0)), 0.0))
    return dict(real=real, z=z, la=la, cum=cum, last=last, qs=qs, k=k, ecum=ecum, ekl=ekl, el=el,
                eq_off=eq_off, ek_off=ek_off, lag_w=lag_w, ri=ri, ci=ci)


def _gla_scores(c, h):
    sl = slice(GLA_DK * h, GLA_DK * (h + 1))
    qs, k = c["qs"][:, sl], c["k"][:, sl]
    ri, ci = c["ri"], c["ci"]
    p = jnp.zeros((CHUNK, CHUNK), F32)
    for r in range(SUB):
        kr = k if r == 0 else pltpu.roll(k, r, 0)
        pr = jnp.sum(qs * kr * c["lag_w"][r][:, sl], axis=1, keepdims=True)
        p = p + jnp.where(ci == ri - r, pr, 0.0)
    blocks = [jnp.zeros((SUB, CHUNK), F32)]
    for a in range(1, N_SUB):
        qh = qs[a * SUB:(a + 1) * SUB, :] * c["eq_off"][a][:, sl]
        kh = k * c["ek_off"][a][:, sl]
        blocks.append(jnp.where(ci[:SUB, :] < a * SUB, _dot_nt(qh, kh), 0.0))
    return p + jnp.concatenate(blocks, axis=0)


def _mixer_fwd(proj, cos2, sin2, w2p, gb, rnw, gnw, name):
    def body(p_ref, c_ref, s_ref, w2_ref, gb_ref, rnw_ref, gnw_ref,
             ocat_ref, mrg_ref, sr_out, sg_out, sr, sg):
        n = pl.program_id(0)

        @pl.when(n == 0)
        def _():
            sr[...] = jnp.zeros_like(sr)
            sg[...] = jnp.zeros_like(sg)

        sr_out[0] = sr[...]
        sg_out[0] = sg[...]
        cosv, sinv = c_ref[...], s_ref[...]

        for h in range(RET_HEADS):
            dmat, zeta, xi, gc = _ret_consts(h)
            hs = slice(128 * h, 128 * (h + 1))
            q = _rope(p_ref[:, O_RQ + 128 * h:O_RQ + 128 * (h + 1)], cosv, sinv)
            k = _rope(p_ref[:, O_RK + 128 * h:O_RK + 128 * (h + 1)], cosv, sinv) * (RET_DK ** -0.5)
            v = p_ref[:, O_RV + 128 * h:O_RV + 128 * (h + 1)]
            g = p_ref[:, O_RG + 128 * h:O_RG + 128 * (h + 1)]
            s_in = sr[h]
            a = _dot_nt(q, k) * dmat
            o = _dot(a, v) + _dot(q, s_in) * xi
            sr[h] = gc * s_in + _dot_tn(k * zeta, v)
            mu = jnp.mean(o, axis=-1, keepdims=True)
            xc = o - mu
            nrm = xc * lax.rsqrt(jnp.mean(xc * xc, axis=-1, keepdims=True) + EPS)
            ocat_ref[:, hs] = o
            mrg_ref[:, hs] = (nrm * rnw_ref[:, hs] * (g * _sigmoid(g))).astype(BF16)

        c = _gla_common(p_ref, w2_ref, gb_ref, n)
        lastcol = _dot_tn_exact_lhs(c["la"], jnp.ones((CHUNK, GLA_DV), F32))
        qe = c["qs"] * c["ecum"]
        kl = c["k"] * c["ekl"]
        for h in range(GLA_HEADS):
            sl = slice(GLA_DK * h, GLA_DK * (h + 1))
            hs = slice(512 + 128 * h, 512 + 128 * (h + 1))
            v = p_ref[:, O_GV + 128 * h:O_GV + 128 * (h + 1)]
            g = p_ref[:, O_GR + 128 * h:O_GR + 128 * (h + 1)]
            s_in = sg[h]
            o = _dot(_gla_scores(c, h), v) + _dot(qe[:, sl], s_in)
            sg[h] = jnp.exp(lastcol[GLA_DK * h:GLA_DK * (h + 1), :]) * s_in + _dot_tn(kl[:, sl], v)
            nrm = o * lax.rsqrt(jnp.mean(o * o, axis=-1, keepdims=True) + EPS)
            ocat_ref[:, hs] = o
            mrg_ref[:, hs] = (nrm * gnw_ref[:, 128 * h:128 * (h + 1)] * (g * _sigmoid(g))).astype(BF16)

    const = lambda shape: pl.BlockSpec(shape, lambda n: (0,) * len(shape))
    return pl.pallas_call(
        body, name=name, grid=(N_CHUNKS,),
        in_specs=[pl.BlockSpec((CHUNK, IN_WP), lambda n: (n, 0)),
                  pl.BlockSpec((CHUNK, 128), lambda n: (n, 0)), pl.BlockSpec((CHUNK, 128), lambda n: (n, 0)),
                  const((128, 256)), const((1, 256)), const((1, 512)), const((1, 512))],
        out_specs=[pl.BlockSpec((CHUNK, D), lambda n: (n, 0)), pl.BlockSpec((CHUNK, D), lambda n: (n, 0)),
                   pl.BlockSpec((1, RET_HEADS, RET_DK, 128), lambda n: (n, 0, 0, 0)),
                   pl.BlockSpec((1, GLA_HEADS, GLA_DK, GLA_DV), lambda n: (n, 0, 0, 0))],
        out_shape=[jax.ShapeDtypeStruct((LP, D), F32), jax.ShapeDtypeStruct((LP, D), BF16),
                   jax.ShapeDtypeStruct((N_CHUNKS, RET_HEADS, RET_DK, 128), F32),
                   jax.ShapeDtypeStruct((N_CHUNKS, GLA_HEADS, GLA_DK, GLA_DV), F32)],
        scratch_shapes=[pltpu.VMEM((RET_HEADS, RET_DK, 128), F32), pltpu.VMEM((GLA_HEADS, GLA_DK, GLA_DV), F32)],
        compiler_params=_cparams("arbitrary"),
    )(proj, cos2, sin2, w2p, gb, rnw, gnw)


def _mixer_bwd(proj, ocat, dmrg, sr_all, sg_all, cos2, sin2, w2p, gb, rnw, gnw, name):
    last_chunk = N_CHUNKS - 1

    def body(p_ref, ocat_ref, dm_ref, sr_ref, sg_ref, c_ref, s_ref, w2_ref, gb_ref, rnw_ref, gnw_ref,
             dp_ref, dw2_ref, dgb_ref, drn_ref, dgn_ref, dsr, dsg):
        step = pl.program_id(0)
        n = last_chunk - step

        @pl.when(step == 0)
        def _():
            dsr[...] = jnp.zeros_like(dsr)
            dsg[...] = jnp.zeros_like(dsg)
            dw2_ref[...] = jnp.zeros_like(dw2_ref)
            dgb_ref[...] = jnp.zeros_like(dgb_ref)
            drn_ref[...] = jnp.zeros_like(drn_ref)
            dgn_ref[...] = jnp.zeros_like(dgn_ref)

        cosv, sinv = c_ref[...], s_ref[...]
        row = lax.broadcasted_iota(jnp.int32, (CHUNK, 1), 0)
        real = ((n * CHUNK + row) >= PAD_ROWS).astype(F32)

        for h in range(RET_HEADS):
            dmat, zeta, xi, gc = _ret_consts(h)
            hs = slice(128 * h, 128 * (h + 1))
            q = _rope(p_ref[:, O_RQ + 128 * h:O_RQ + 128 * (h + 1)], cosv, sinv)
            k = _rope(p_ref[:, O_RK + 128 * h:O_RK + 128 * (h + 1)], cosv, sinv) * (RET_DK ** -0.5)
            v = p_ref[:, O_RV + 128 * h:O_RV + 128 * (h + 1)]
            g = p_ref[:, O_RG + 128 * h:O_RG + 128 * (h + 1)]
            o = ocat_ref[:, hs]
            dy = dm_ref[:, hs]
            wv = rnw_ref[:, hs]
            mu = jnp.mean(o, axis=-1, keepdims=True)
            xc = o - mu
            rs = lax.rsqrt(jnp.mean(xc * xc, axis=-1, keepdims=True) + EPS)
            nrm = xc * rs
            sgm = _sigmoid(g)
            sil = g * sgm
            drn_ref[0:1, hs] += jnp.sum(dy * nrm * sil, axis=0, keepdims=True)
            dgate = dy * nrm * wv * (sgm * (1.0 + g * (1.0 - sgm)))
            dn = dy * wv * sil
            do = rs * (dn - jnp.mean(dn, axis=-1, keepdims=True) - nrm * jnp.mean(dn * nrm, axis=-1, keepdims=True))
            s_in = sr_ref[0, h]
            ds_out = dsr[h]
            a = _dot_nt(q, k) * dmat
            da = _dot_nt(do, v) * dmat
            dox = do * xi
            dq = _dot(da, k) + _dot_nt(dox, s_in)
            dk = _dot_tn(da, q) + _dot_nt(v, ds_out) * zeta
            dv = _dot_tn(a, do) + _dot(k * zeta, ds_out)
            dsr[h] = gc * ds_out + _dot_tn(q, dox)
            dk = dk * (RET_DK ** -0.5)
            dp_ref[:, O_RQ + 128 * h:O_RQ + 128 * (h + 1)] = (_unrope(dq, cosv, sinv) * real).astype(BF16)
            dp_ref[:, O_RK + 128 * h:O_RK + 128 * (h + 1)] = (_unrope(dk, cosv, sinv) * real).astype(BF16)
            dp_ref[:, O_RV + 128 * h:O_RV + 128 * (h + 1)] = (dv * real).astype(BF16)
            dp_ref[:, O_RG + 128 * h:O_RG + 128 * (h + 1)] = (dgate * real).astype(BF16)

        c = _gla_common(p_ref, w2_ref, gb_ref, n)
        ri, ci = c["ri"], c["ci"]
        causal = ri >= ci
        triu = (ci >= ri).astype(F32)
        qe = c["qs"] * c["ecum"]
        kl = c["k"] * c["ekl"]
        lastcol = _dot_tn_exact_lhs(c["la"], jnp.ones((CHUNK, GLA_DV), F32))
        dla_heads, dq_heads, dk_heads = [], [], []
        for h in range(GLA_HEADS):
            sl = slice(GLA_DK * h, GLA_DK * (h + 1))
            hs = slice(512 + 128 * h, 512 + 128 * (h + 1))
            v = p_ref[:, O_GV + 128 * h:O_GV + 128 * (h + 1)]
            g = p_ref[:, O_GR + 128 * h:O_GR + 128 * (h + 1)]
            o = ocat_ref[:, hs]
            dy = dm_ref[:, hs]
            wv = gnw_ref[:, 128 * h:128 * (h + 1)]
            rs = lax.rsqrt(jnp.mean(o * o, axis=-1, keepdims=True) + EPS)
            nrm = o * rs
            sgm = _sigmoid(g)
            sil = g * sgm
            dgn_ref[0:1, 128 * h:128 * (h + 1)] += jnp.sum(dy * nrm * sil, axis=0, keepdims=True)
            dgate = dy * nrm * wv * (sgm * (1.0 + g * (1.0 - sgm)))
            dn = dy * wv * sil
            do = rs * (dn - nrm * jnp.mean(dn * nrm, axis=-1, keepdims=True))
            qs_h, k_h = c["qs"][:, sl], c["k"][:, sl]
            s_in = sg_ref[0, h]
            ds_out = dsg[h]
            el_col = jnp.exp(lastcol[GLA_DK * h:GLA_DK * (h + 1), :])
            p = _gla_scores(c, h)
            dp = jnp.where(causal, _dot_nt(do, v), 0.0)
            dv = _dot_tn(p, do) + _dot(kl[:, sl], ds_out)
            dqe = _dot_nt(do, s_in)
            dkl = _dot_nt(v, ds_out)
            dsg[h] = _dot_tn(qe[:, sl], do) + el_col * ds_out
            sd = s_in * ds_out
            sd_hi = sd.astype(BF16)
            sd_lo = (sd - sd_hi.astype(F32)).astype(BF16)
            ones8 = jnp.ones((8, GLA_DV), BF16)
            nt = (((1,), (1,)), ((), ()))
            d_el = (lax.dot_general(ones8, sd_hi, nt, preferred_element_type=F32)
                    + lax.dot_general(ones8, sd_lo, nt, preferred_element_type=F32))[0:1, :]
            dqs = dqe * c["ecum"][:, sl]
            dkk = dkl * c["ekl"][:, sl]
            d_last = jnp.sum(dkl * kl[:, sl], axis=0, keepdims=True) + d_el * c["el"][:, sl]
            dq_rows = [jnp.zeros((SUB, GLA_DK), F32)]
            for a in range(1, N_SUB):
                eq, ek = c["eq_off"][a][:, sl], c["ek_off"][a][:, sl]
                qh = qs_h[a * SUB:(a + 1) * SUB, :] * eq
                kh = k_h * ek
                dpa = jnp.where(ci[:SUB, :] < a * SUB, dp[a * SUB:(a + 1) * SUB, :], 0.0)
                dq_rows.append(_dot(dpa, kh) * eq)
                dkk = dkk + _dot_tn(dpa, qh) * ek
            dqs = dqs + jnp.concatenate(dq_rows, axis=0)
            for r in range(SUB):
                w = c["lag_w"][r][:, sl]
                dpr = jnp.sum(jnp.where(ci == ri - r, dp, 0.0), axis=1, keepdims=True)
                kr = k_h if r == 0 else pltpu.roll(k_h, r, 0)
                dqs = dqs + dpr * kr * w
                back = dpr * qs_h * w
                dkk = dkk + (back if r == 0 else pltpu.roll(back, CHUNK - r, 0))
            dcum = qs_h * dqs - k_h * dkk + jnp.where(row == CHUNK - 1, d_last, 0.0)
            dla_heads.append(_dot_exact_rhs(triu, dcum))
            dq_heads.append(dqs * (GLA_DK ** -0.5))
            dk_heads.append(dkk)
            dp_ref[:, O_GV + 128 * h:O_GV + 128 * (h + 1)] = (dv * real).astype(BF16)
            dp_ref[:, O_GR + 128 * h:O_GR + 128 * (h + 1)] = (dgate * real).astype(BF16)

        dla = jnp.concatenate(dla_heads, axis=1)
        dp_ref[:, O_GQ:O_GQ + 256] = (jnp.concatenate(dq_heads, axis=1) * real).astype(BF16)
        dp_ref[:, O_GK:O_GK + 256] = (jnp.concatenate(dk_heads, axis=1) * real).astype(BF16)
        dz = dla * (1.0 / GLA_TAU) * _sigmoid(-c["z"]) * real
        ga = p_ref[:, O_GA:O_GA + 128]
        dp_ref[:, O_GA:O_GA + 128] = _dot_nt(dz, w2_ref[...]).astype(BF16)
        dp_ref[:, O_GA + 128:IN_WP] = jnp.zeros((CHUNK, IN_WP - O_GA - 128), BF16)
        dw2_ref[...] += _dot_tn(ga, dz)
        dgb_ref[0:1, :] += jnp.sum(dz, axis=0, keepdims=True)

    const = lambda shape: pl.BlockSpec(shape, lambda s: (0,) * len(shape))
    rev = lambda s: (last_chunk - s, 0)
    return pl.pallas_call(
        body, name=name, grid=(N_CHUNKS,),
        in_specs=[pl.BlockSpec((CHUNK, IN_WP), rev), pl.BlockSpec((CHUNK, D), rev), pl.BlockSpec((CHUNK, D), rev),
                  pl.BlockSpec((1, RET_HEADS, RET_DK, 128), lambda s: (last_chunk - s, 0, 0, 0)),
                  pl.BlockSpec((1, GLA_HEADS, GLA_DK, GLA_DV), lambda s: (last_chunk - s, 0, 0, 0)),
                  pl.BlockSpec((CHUNK, 128), rev), pl.BlockSpec((CHUNK, 128), rev),
                  const((128, 256)), const((1, 256)), const((1, 512)), const((1, 512))],
        out_specs=[pl.BlockSpec((CHUNK, IN_WP), rev), const((128, 256)), const((8, 256)),
                   const((8, 512)), const((8, 512))],
        out_shape=[jax.ShapeDtypeStruct((LP, IN_WP), BF16), jax.ShapeDtypeStruct((128, 256), F32),
                   jax.ShapeDtypeStruct((8, 256), F32), jax.ShapeDtypeStruct((8, 512), F32),
                   jax.ShapeDtypeStruct((8, 512), F32)],
        scratch_shapes=[pltpu.VMEM((RET_HEADS, RET_DK, 128), F32), pltpu.VMEM((GLA_HEADS, GLA_DK, GLA_DV), F32)],
        compiler_params=_cparams("arbitrary"),
    )(proj, ocat, dmrg, sr_all, sg_all, cos2, sin2, w2p, gb, rnw, gnw)


def _all_gather(xs, name):
    n = len(xs)

    def body(*refs):
        x_refs, out_refs = refs[:n], refs[n:2 * n]
        send_sems, recv_sems, local_sems = refs[2 * n:]
        mx, my, mc = lax.axis_index("x"), lax.axis_index("y"), lax.axis_index("c")
        me, sibling = (mx, my, mc), (mx, my, 1 - mc)
        chips = [(1 - mx, my), (mx, 1 - my), (1 - mx, 1 - my)]

        def slot(a, px, py, pc):
            return out_refs[a].at[4 * px + 2 * py + pc]

        def copy(a, k, block, to, src=None):
            return pltpu.make_async_remote_copy(
                src_ref=slot(a, *block) if src is None else src, dst_ref=slot(a, *block),
                send_sem=send_sems.at[7 * a + k], recv_sem=recv_sems.at[7 * a + k],
                device_id=to, device_id_type=MESH_IDS)

        mine = [pltpu.make_async_copy(x_refs[a], slot(a, *me), local_sems.at[a]) for a in range(n)]
        first, passed = [], []
        for a in range(n):
            mine[a].start()
            first.append(copy(a, 0, me, sibling, src=x_refs[a]))
            first += [copy(a, 1 + j, me, (*chip, mc), src=x_refs[a]) for j, chip in enumerate(chips)]
        for cp in first:
            cp.start()
        for j, chip in enumerate(chips):
            for a in range(n):
                copy(a, 1 + j, (*chip, mc), me).wait_recv()
                passed.append(copy(a, 4 + j, (*chip, mc), sibling))
                passed[-1].start()
        for a in range(n):
            copy(a, 0, sibling, me).wait_recv()
            for j, chip in enumerate(chips):
                copy(a, 4 + j, (*chip, 1 - mc), me).wait_recv()
        for cp in first + passed:
            cp.wait_send()
        for a in range(n):
            mine[a].wait()

    return pl.pallas_call(
        body, name=name,
        in_specs=[pl.BlockSpec(memory_space=pl.ANY)] * n,
        out_specs=[pl.BlockSpec(memory_space=pl.ANY)] * n,
        out_shape=[jax.ShapeDtypeStruct((N_DEV,) + x.shape, x.dtype) for x in xs],
        scratch_shapes=[pltpu.SemaphoreType.DMA((7 * n,)), pltpu.SemaphoreType.DMA((7 * n,)),
                        pltpu.SemaphoreType.DMA((n,))],
    )(*xs)


def _exchange_blocks(gs, name):
    n = len(gs)

    def body(*refs):
        g_refs, out_refs = refs[:n], refs[n:2 * n]
        send_sems, recv_sems, local_sems = refs[2 * n:]
        mx, my, mc = lax.axis_index("x"), lax.axis_index("y"), lax.axis_index("c")
        me = 4 * mx + 2 * my + mc
        mine = [pltpu.make_async_copy(g_refs[a].at[me], out_refs[a].at[me], local_sems.at[a]) for a in range(n)]
        copies = []
        for r in range(1, N_DEV):
            px, py, pc = mx ^ (r >> 2), my ^ ((r >> 1) & 1), mc ^ (r & 1)
            peer = 4 * px + 2 * py + pc
            for a in range(n):
                copies.append(pltpu.make_async_remote_copy(
                    src_ref=g_refs[a].at[peer], dst_ref=out_refs[a].at[me],
                    send_sem=send_sems.at[7 * a + r - 1], recv_sem=recv_sems.at[7 * a + r - 1],
                    device_id=(px, py, pc), device_id_type=MESH_IDS))
        for cp in mine + copies:
            cp.start()
        for cp in copies:
            cp.wait_recv()
        for cp in copies:
            cp.wait_send()
        for cp in mine:
            cp.wait()

    return pl.pallas_call(
        body, name=name,
        in_specs=[pl.BlockSpec(memory_space=pl.ANY)] * n,
        out_specs=[pl.BlockSpec(memory_space=pl.ANY)] * n,
        out_shape=[jax.ShapeDtypeStruct(g.shape, g.dtype) for g in gs],
        scratch_shapes=[pltpu.SemaphoreType.DMA((7 * n,)), pltpu.SemaphoreType.DMA((7 * n,)),
                        pltpu.SemaphoreType.DMA((n,))],
    )(*gs)


IN_SHARD = IN_W // N_DEV
IN_SHARD_P = 512
UP_SHARD = D_UP // N_DEV
UP_SHARD_P = 768
RELAYOUT_ROWS = 256


def _pieces_w_in():
    return [(k, 0, IN_SHARD * k, IN_SHARD) for k in range(N_DEV)]


def _pieces_ffn_up():
    pieces = []
    for k in range(N_DEV):
        n, end = UP_SHARD * k, UP_SHARD * (k + 1)
        while n < end:
            half, r = divmod(n, D_FF)
            blk, off = divmod(r, CONV_BLOCK)
            run = min(CONV_BLOCK - off, end - n)
            pieces.append((k, n - UP_SHARD * k, 2 * CONV_BLOCK * blk + CONV_BLOCK * half + off, run))
            n += run
    return pieces


def _assemble_block(load, spans, dst_block, rows):
    lo = 128 * dst_block
    lane = lax.broadcasted_iota(jnp.int32, (1, 128), 1)
    out = jnp.zeros((rows, 128), F32)
    for key, src_off, dst_off, length in spans:
        a, b = max(lo, dst_off), min(lo + 128, dst_off + length)
        s, s_end = src_off + (a - dst_off), src_off + (b - dst_off)
        d = a
        while s < s_end:
            e = min(s_end, 128 * (s // 128 + 1))
            blk = load(key, s // 128)
            shift = (d - s) % 128
            if shift:
                blk = pltpu.roll(blk, shift, 1)
            out = jnp.where((lane >= d - lo) & (lane < d - lo + (e - s)), blk, out)
            d += e - s
            s = e
    return out


def _shards_to_cols(shards, pieces, width, name):
    _, rows, _ = shards.shape
    tr = RELAYOUT_ROWS

    def body(s_ref, o_ref):
        load = lambda k, b: s_ref[k, :, 128 * b:128 * (b + 1)].astype(F32)
        for db in range(width // 128):
            o_ref[:, 128 * db:128 * (db + 1)] = _assemble_block(load, pieces, db, tr).astype(BF16)

    return pl.pallas_call(
        body, name=name, grid=(rows // tr,),
        in_specs=[pl.BlockSpec((N_DEV, tr, shards.shape[2]), lambda i: (0, i, 0))],
        out_specs=pl.BlockSpec((tr, width), lambda i: (i, 0)),
        out_shape=jax.ShapeDtypeStruct((rows, width), BF16),
        compiler_params=_cparams("parallel"),
    )(shards)


def _cols_to_shards(full, pieces, shard_width, name):
    rows, width = full.shape
    tr = RELAYOUT_ROWS

    def body(f_ref, o_ref):
        load = lambda _, b: f_ref[:, 128 * b:128 * (b + 1)].astype(F32)
        for k in range(N_DEV):
            spans = [(None, dst_off, src_off, length) for dev, src_off, dst_off, length in pieces if dev == k]
            for db in range(shard_width // 128):
                o_ref[k, :, 128 * db:128 * (db + 1)] = _assemble_block(load, spans, db, tr).astype(BF16)

    return pl.pallas_call(
        body, name=name, grid=(rows // tr,),
        in_specs=[pl.BlockSpec((tr, width), lambda i: (i, 0))],
        out_specs=pl.BlockSpec((N_DEV, tr, shard_width), lambda i: (0, i, 0)),
        out_shape=jax.ShapeDtypeStruct((N_DEV, rows, shard_width), BF16),
        compiler_params=_cparams("parallel"),
    )(full)


def _adamw(parts, w, m, v, rows_per_step, name):
    rows, cols = w.shape
    assert rows % rows_per_step == 0 and parts.shape == (N_DEV, rows, cols)

    def body(p_ref, w_ref, m_ref, v_ref, g_ref, d_ref, nm_ref, nv_ref):
        g = p_ref[0].astype(F32)
        for j in range(1, N_DEV):
            g = g + p_ref[j].astype(F32)
        m_new = ADAM_B1 * m_ref[...] + (1.0 - ADAM_B1) * g
        v_new = ADAM_B2 * v_ref[...] + (1.0 - ADAM_B2) * (g * g)
        m_hat = m_new / (1.0 - ADAM_B1 ** ADAM_STEP)
        v_hat = v_new / (1.0 - ADAM_B2 ** ADAM_STEP)
        g_ref[...] = g
        d_ref[...] = -ADAM_LR * (m_hat / (jnp.sqrt(v_hat) + ADAM_EPS) + ADAM_WD * w_ref[...])
        nm_ref[...] = m_new
        nv_ref[...] = v_new

    tile = pl.BlockSpec((rows_per_step, cols), lambda i: (i, 0))
    shape = jax.ShapeDtypeStruct((rows, cols), F32)
    return pl.pallas_call(
        body, name=name, grid=(rows // rows_per_step,),
        in_specs=[pl.BlockSpec((N_DEV, rows_per_step, cols), lambda i: (0, i, 0)), tile, tile, tile],
        out_specs=[tile, tile, tile, tile],
        out_shape=[shape, shape, shape, shape],
        compiler_params=_cparams("parallel"),
    )(parts, w, m, v)


BIG = (("w_in", (DEPTH, D, IN_W // N_DEV), 2), ("w_out", (DEPTH, D // N_DEV, D), 1),
       ("ffn_up", (DEPTH, D, D_UP // N_DEV), 2), ("ffn_down", (DEPTH, D_FF // N_DEV, D), 1))
SMALL = (("meta_tokens", (N_META, D // N_DEV), 1), ("gla_gate_w2", (DEPTH, GATE_RANK, 256 // N_DEV), 2),
         ("ffn_conv_w", (DEPTH, 3, D_UP // N_DEV), 2))
REPL = (("pre_mix_norm", (DEPTH, D)), ("gla_gate_b", (DEPTH, 256)), ("ret_norm_w", (DEPTH, 512)),
        ("gla_norm_w", (DEPTH, 512)), ("post_mix_norm", (DEPTH, D)), ("pre_ffn_norm", (DEPTH, D)),
        ("ffn_conv_b", (DEPTH, D_UP)), ("post_ffn_norm", (DEPTH, D)))
WEIGHT_ORDER = ("meta_tokens", "pre_mix_norm", "w_in", "gla_gate_w2", "gla_gate_b", "ret_norm_w", "gla_norm_w",
                "w_out", "post_mix_norm", "pre_ffn_norm", "ffn_up", "ffn_conv_w", "ffn_conv_b", "ffn_down",
                "post_ffn_norm")


def _size(shape):
    return math.prod(shape)


def _round_up(n, mult):
    return -(-n // mult) * mult


REPL_ROWS = _round_up(-(-sum(_size(s) for _, s in REPL) // LANES), 8)
SMALL_ROWS = _round_up(-(-sum(_size(s) for _, s, _ in SMALL) // LANES), 8)


def _pack(arrays, rows, dtype):
    flat = jnp.concatenate([a.reshape(-1).astype(dtype) for a in arrays])
    return jnp.pad(flat, (0, rows * LANES - flat.shape[0])).reshape(rows, LANES)


def _unpack(buf, shapes):
    flat = buf.reshape(-1)
    out, off = [], 0
    for shape in shapes:
        out.append(flat[off:off + _size(shape)].reshape(shape))
        off += _size(shape)
    return out


def _unshard(blocks, axis):
    moved = jnp.moveaxis(blocks, 0, axis)
    shape = list(moved.shape)
    shape[axis:axis + 2] = [shape[axis] * shape[axis + 1]]
    return moved.reshape(shape)


def _to_blocks(full, axis):
    shape = list(full.shape)
    shape[axis:axis + 1] = [N_DEV, shape[axis] // N_DEV]
    return jnp.moveaxis(full.reshape(shape), axis, 0)


def _interleave_cols(w):
    lead = w.shape[:-1]
    return jnp.swapaxes(w.reshape(lead + (2, N_CONV_BLOCKS, CONV_BLOCK)), -3, -2).reshape(lead + (D_UP,))


def _deinterleave_cols(w):
    lead = w.shape[:-1]
    return jnp.swapaxes(w.reshape(lead + (N_CONV_BLOCKS, 2, CONV_BLOCK)), -3, -2).reshape(lead + (D_UP,))


def _rope_tables():
    half = RET_DK // 2
    inv = ROPE_BASE ** (-jnp.arange(half, dtype=F32) / half)
    pos = jnp.arange(LP, dtype=F32) - float(PAD_ROWS)
    ang = pos[:, None] * inv[None, :]
    c, s = jnp.cos(ang), jnp.sin(ang)
    return jnp.concatenate([c, c], axis=1), jnp.concatenate([-s, s], axis=1)


def kernel(x, meta_tokens, pre_mix_norm, w_in, gla_gate_w2, gla_gate_b, ret_norm_w, gla_norm_w, w_out, post_mix_norm, pre_ffn_norm, ffn_up, ffn_conv_w, ffn_conv_b, ffn_down, post_ffn_norm, loss_target, m_meta_tokens, m_pre_mix_norm, m_w_in, m_gla_gate_w2, m_gla_gate_b, m_ret_norm_w, m_gla_norm_w, m_w_out, m_post_mix_norm, m_pre_ffn_norm, m_ffn_up, m_ffn_conv_w, m_ffn_conv_b, m_ffn_down, m_post_ffn_norm, v_meta_tokens, v_pre_mix_norm, v_w_in, v_gla_gate_w2, v_gla_gate_b, v_ret_norm_w, v_gla_norm_w, v_w_out, v_post_mix_norm, v_pre_ffn_norm, v_ffn_up, v_ffn_conv_w, v_ffn_conv_b, v_ffn_down, v_post_ffn_norm):
    weights = dict(meta_tokens=meta_tokens, pre_mix_norm=pre_mix_norm, w_in=w_in, gla_gate_w2=gla_gate_w2,
                   gla_gate_b=gla_gate_b, ret_norm_w=ret_norm_w, gla_norm_w=gla_norm_w, w_out=w_out,
                   post_mix_norm=post_mix_norm, pre_ffn_norm=pre_ffn_norm, ffn_up=ffn_up, ffn_conv_w=ffn_conv_w,
                   ffn_conv_b=ffn_conv_b, ffn_down=ffn_down, post_ffn_norm=post_ffn_norm)
    mom1 = dict(meta_tokens=m_meta_tokens, pre_mix_norm=m_pre_mix_norm, w_in=m_w_in, gla_gate_w2=m_gla_gate_w2,
                gla_gate_b=m_gla_gate_b, ret_norm_w=m_ret_norm_w, gla_norm_w=m_gla_norm_w, w_out=m_w_out,
                post_mix_norm=m_post_mix_norm, pre_ffn_norm=m_pre_ffn_norm, ffn_up=m_ffn_up,
                ffn_conv_w=m_ffn_conv_w, ffn_conv_b=m_ffn_conv_b, ffn_down=m_ffn_down, post_ffn_norm=m_post_ffn_norm)
    mom2 = dict(meta_tokens=v_meta_tokens, pre_mix_norm=v_pre_mix_norm, w_in=v_w_in, gla_gate_w2=v_gla_gate_w2,
                gla_gate_b=v_gla_gate_b, ret_norm_w=v_ret_norm_w, gla_norm_w=v_gla_norm_w, w_out=v_w_out,
                post_mix_norm=v_post_mix_norm, pre_ffn_norm=v_pre_ffn_norm, ffn_up=v_ffn_up,
                ffn_conv_w=v_ffn_conv_w, ffn_conv_b=v_ffn_conv_b, ffn_down=v_ffn_down, post_ffn_norm=v_post_ffn_norm)

    pad_cols = lambda a, width: jnp.pad(a, ((0, 0), (0, width - a.shape[1])))
    layer_w = []
    for l in range(DEPTH):
        shards = [pad_cols(w_in[l].astype(BF16), IN_SHARD_P), w_out[l].astype(BF16),
                  pad_cols(ffn_up[l].astype(BF16), UP_SHARD_P), ffn_down[l].astype(BF16)]
        in_s, out_s, up_s, down_s = _all_gather(shards, f"gather_weights_{l}")
        layer_w.append(dict(w_in=_shards_to_cols(in_s, _pieces_w_in(), IN_WP, f"w_in_cols_{l}"),
                            w_out=out_s.reshape(D, D),
                            w_up=_shards_to_cols(up_s, _pieces_ffn_up(), D_UP, f"ffn_up_cols_{l}"),
                            w_down=down_s.reshape(D_FF, D)))
    small = _all_gather([_pack([weights[n] for n, _, _ in SMALL], SMALL_ROWS, F32)], "gather_small_weights")[0]
    small_parts = _unpack_blocks(small, [s for _, s, _ in SMALL])
    full = {n: _unshard(p, ax) for (n, _, ax), p in zip(SMALL, small_parts)}
    w2p = jnp.pad(full["gla_gate_w2"], ((0, 0), (0, 128 - GATE_RANK), (0, 0)))
    cw8 = jnp.concatenate([_interleave_cols(full["ffn_conv_w"]), _interleave_cols(ffn_conv_b)[:, None, :],
                           jnp.zeros((DEPTH, 4, D_UP), F32)], axis=1)
    cos2, sin2 = _rope_tables()

    h = jnp.concatenate([jnp.zeros((PAD_ROWS, D), F32), full["meta_tokens"], x[0]], axis=0)
    target = jnp.concatenate([jnp.zeros((CHUNK, D), F32), loss_target[0]], axis=0)
    saved = []
    for l in range(DEPTH):
        lw = layer_w[l]
        a1 = _rmsnorm_fwd(h, pre_mix_norm[l:l + 1], f"pre_mix_norm_{l}")
        proj = _matmul(a1, lw["w_in"], out_dtype=F32, tm=TM, tn=1280, tk=D, name=f"in_proj_{l}")
        ocat, merged, sr_all, sg_all = _mixer_fwd(proj, cos2, sin2, w2p[l], gla_gate_b[l:l + 1],
                                                  ret_norm_w[l:l + 1], gla_norm_w[l:l + 1], f"mixer_fwd_{l}")
        m = _matmul(merged, lw["w_out"], out_dtype=F32, tm=TM, tn=D, tk=D, name=f"out_proj_{l}")
        h1 = _resid_norm(h, m, post_mix_norm[l:l + 1], f"post_mix_norm_{l}")
        a2 = _rmsnorm_fwd(h1, pre_ffn_norm[l:l + 1], f"pre_ffn_norm_{l}")
        u = _matmul(a2, lw["w_up"], out_dtype=BF16, tm=TM, tn=1408, tk=D, name=f"ffn_up_{l}")
        act = _conv_act_fwd(u, cw8[l], f"ffn_conv_act_{l}")
        f = _matmul(act, lw["w_down"], out_dtype=F32, tm=TM, tn=D, tk=D_FF, name=f"ffn_down_{l}")
        h2 = _resid_norm(h1, f, post_ffn_norm[l:l + 1], f"post_ffn_norm_{l}")
        saved.append(dict(h=h, a1=a1, proj=proj, ocat=ocat, merged=merged, sr=sr_all, sg=sg_all, m=m, h1=h1,
                          a2=a2, u=u, act=act, f=f))
        h = h2

    dh, loss_acc = _loss_head(h, target, "loss_head")
    loss = lax.psum(loss_acc[0, 0], ("x", "y", "c"))

    kinds = ("grad", "delta", "new_m", "new_v")
    big_names = [n for n, _, _ in BIG]
    grads = {n: [None] * DEPTH for n in WEIGHT_ORDER if n != "meta_tokens" and n not in big_names}
    big_out = {kind: {n: [None] * DEPTH for n in big_names} for kind in kinds}
    for l in reversed(range(DEPTH)):
        s, lw = saved[l], layer_w[l]
        df, g_post_ffn = _norm_bwd(dh, s["f"], post_ffn_norm[l:l + 1], None, BF16, f"post_ffn_norm_bwd_{l}")
        dact = _matmul(df, lw["w_down"], tb=True, out_dtype=BF16, tm=TM, tn=D_FF, tk=D, name=f"ffn_down_dx_{l}")
        g_down = _matmul(s["act"], df, ta=True, out_dtype=BF16, tm=D_FF // 2, tn=D, tk=TM, name=f"ffn_down_dw_{l}")
        du, dcw = _conv_act_bwd(dact, s["u"], cw8[l], f"ffn_conv_act_bwd_{l}")
        da2 = _matmul(du, lw["w_up"], tb=True, out_dtype=F32, tm=TM, tn=D, tk=1408, name=f"ffn_up_dx_{l}")
        g_up = _matmul(s["a2"], du, ta=True, out_dtype=BF16, tm=D, tn=1408, tk=TM, name=f"ffn_up_dw_{l}")
        dh1, g_pre_ffn = _norm_bwd(da2, s["h1"], pre_ffn_norm[l:l + 1], dh, F32, f"pre_ffn_norm_bwd_{l}")
        dm, g_post_mix = _norm_bwd(dh1, s["m"], post_mix_norm[l:l + 1], None, BF16, f"post_mix_norm_bwd_{l}")
        dmerged = _matmul(dm, lw["w_out"], tb=True, out_dtype=F32, tm=TM, tn=D, tk=D, name=f"out_proj_dx_{l}")
        g_out = _matmul(s["merged"], dm, ta=True, out_dtype=BF16, tm=D, tn=D, tk=TM, name=f"out_proj_dw_{l}")
        dproj, g_w2, g_gb, g_rn, g_gn = _mixer_bwd(s["proj"], s["ocat"], dmerged, s["sr"], s["sg"], cos2, sin2,
                                                   w2p[l], gla_gate_b[l:l + 1], ret_norm_w[l:l + 1],
                                                   gla_norm_w[l:l + 1], f"mixer_bwd_{l}")
        da1 = _matmul(dproj, lw["w_in"], tb=True, out_dtype=F32, tm=TM, tn=D, tk=1280, name=f"in_proj_dx_{l}")
        g_in = _matmul(s["a1"], dproj, ta=True, out_dtype=BF16, tm=D, tn=1280, tk=TM, name=f"in_proj_dw_{l}")
        dh, g_pre_mix = _norm_bwd(da1, s["h"], pre_mix_norm[l:l + 1], dh1, F32, f"pre_mix_norm_bwd_{l}")
        grads["post_ffn_norm"][l] = g_post_ffn[0]
        grads["ffn_conv_w"][l] = _deinterleave_cols(dcw[0:3])
        grads["ffn_conv_b"][l] = _deinterleave_cols(dcw[3])
        grads["pre_ffn_norm"][l] = g_pre_ffn[0]
        grads["post_mix_norm"][l] = g_post_mix[0]
        grads["gla_gate_w2"][l] = g_w2[:GATE_RANK]
        grads["gla_gate_b"][l] = g_gb[0]
        grads["ret_norm_w"][l] = g_rn[0]
        grads["gla_norm_w"][l] = g_gn[0]
        grads["pre_mix_norm"][l] = g_pre_mix[0]

        send = [_cols_to_shards(g_in, _pieces_w_in(), IN_SHARD_P, f"w_in_grad_shards_{l}"),
                g_out.reshape(N_DEV, D // N_DEV, D),
                _cols_to_shards(g_up, _pieces_ffn_up(), UP_SHARD_P, f"ffn_up_grad_shards_{l}"),
                g_down.reshape(N_DEV, D_FF // N_DEV, D)]
        parts = _exchange_blocks(send, f"exchange_grads_{l}")
        widths = (IN_SHARD_P, D, UP_SHARD_P, D)
        steps = (256, D // N_DEV, 256, D_FF // N_DEV // 2)
        for n, part, width, rows_per_step in zip(big_names, parts, widths, steps):
            shard = [pad_cols(d[n][l], width) for d in (weights, mom1, mom2)]
            results = _adamw(part, *shard, rows_per_step, f"adamw_{n}_{l}")
            for kind, r in zip(kinds, results):
                big_out[kind][n][l] = r[:, :weights[n].shape[2]]
    local = {n: jnp.stack(v) for n, v in grads.items()}
    local["meta_tokens"] = dh[PAD_ROWS:CHUNK]
    grad_x = dh[CHUNK:][None]
    out = {kind: {n: jnp.stack(v) for n, v in big_out[kind].items()} for kind in kinds}

    blocks = jnp.concatenate([_to_blocks(local[n], ax).reshape(N_DEV, -1) for n, _, ax in SMALL], axis=1)
    blocks = jnp.pad(blocks, ((0, 0), (0, SMALL_ROWS * LANES - blocks.shape[1]))).reshape(N_DEV, SMALL_ROWS, LANES)
    parts = _exchange_blocks([blocks], "exchange_small_grads")[0]
    shard_shapes = [s for _, s, _ in SMALL]
    packed = [_pack([d[n] for n, _, _ in SMALL], SMALL_ROWS, F32) for d in (weights, mom1, mom2)]
    results = _adamw(parts, *packed, SMALL_ROWS, "adamw_small_sharded")
    for kind, buf in zip(kinds, results):
        out[kind].update(zip([n for n, _, _ in SMALL], _unpack(buf, shard_shapes)))

    repl_parts = _all_gather([_pack([local[n] for n, _ in REPL], REPL_ROWS, F32)], "gather_small_grads")[0]
    packed = [_pack([d[n] for n, _ in REPL], REPL_ROWS, F32) for d in (weights, mom1, mom2)]
    results = _adamw(repl_parts, *packed, REPL_ROWS, "adamw_replicated")
    repl_shapes = [s for _, s in REPL]
    for kind, buf in zip(kinds, results):
        out[kind].update(zip([n for n, _ in REPL], _unpack(buf, repl_shapes)))

    return (loss, grad_x, *[out["grad"][n] for n in WEIGHT_ORDER], *[out["delta"][n] for n in WEIGHT_ORDER],
            *[out["new_m"][n] for n in WEIGHT_ORDER], *[out["new_v"][n] for n in WEIGHT_ORDER])


def _unpack_blocks(gathered, shapes):
    flat = gathered.reshape(N_DEV, -1)
    out, off = [], 0
    for shape in shapes:
        out.append(flat[:, off:off + _size(shape)].reshape((N_DEV,) + shape))
        off += _size(shape)
    return out
```

```python
import math

import jax
import jax.numpy as jnp
from jax import lax
from jax.experimental import pallas as pl
from jax.experimental.pallas import tpu as pltpu

F32 = jnp.float32
BF16 = jnp.bfloat16

D = 1024
SEQ = 8192
DEPTH = 2
N_META = 16
CHUNK = 64
SUB = 16
N_SUB = CHUNK // SUB
PAD_ROWS = CHUNK - N_META
LP = SEQ + CHUNK
N_CHUNKS = LP // CHUNK
RET_HEADS = 4
RET_DK = 128
GLA_HEADS = 4
GLA_DK = 64
GLA_DV = 128
GLA_TAU = 16.0
GATE_RANK = 16
IN_W = 3600
IN_WP = 3840
D_FF = 2816
D_UP = 2 * D_FF
CONV_BLOCK = 256
N_CONV_BLOCKS = D_FF // CONV_BLOCK
ROPE_BASE = 10000.0
EPS = 1e-6
N_DEV = 8
LANES = 1024

O_RQ, O_RK, O_RV, O_RG = 0, 512, 1024, 1536
O_GQ, O_GK, O_GV, O_GR, O_GA = 2048, 2304, 2560, 3072, 3584

ADAM_LR = 0.001
ADAM_B1 = 0.9
ADAM_B2 = 0.999
ADAM_EPS = 1e-08
ADAM_WD = 0.01
ADAM_STEP = 10

VMEM_LIMIT = 56 * 1024 * 1024
MESH_IDS = pl.DeviceIdType.MESH


def _row_tile(rows):
    best = 16
    for t in range(16, min(rows, 688) + 1, 16):
        if rows % t == 0:
            best = t
    return best


TM = _row_tile(LP)


def _cparams(*sem):
    return pltpu.CompilerParams(dimension_semantics=sem, vmem_limit_bytes=VMEM_LIMIT)


def _dot(a, b):
    return jnp.dot(a.astype(BF16), b.astype(BF16), preferred_element_type=F32)


def _dot_nt(a, b):
    return lax.dot_general(a.astype(BF16), b.astype(BF16), (((1,), (1,)), ((), ())), preferred_element_type=F32)


def _dot_tn(a, b):
    return lax.dot_general(a.astype(BF16), b.astype(BF16), (((0,), (0,)), ((), ())), preferred_element_type=F32)


def _split3(x):
    hi = x.astype(BF16)
    r1 = x - hi.astype(F32)
    mid = r1.astype(BF16)
    lo = (r1 - mid.astype(F32)).astype(BF16)
    return hi, mid, lo


def _dot_exact_rhs(t, x):
    hi, mid, lo = _split3(x)
    t = t.astype(BF16)
    return (jnp.dot(t, hi, preferred_element_type=F32) + jnp.dot(t, mid, preferred_element_type=F32)
            + jnp.dot(t, lo, preferred_element_type=F32))


def _dot_tn_exact_lhs(x, ones):
    dims = (((0,), (0,)), ((), ()))
    hi, mid, lo = _split3(x)
    ones = ones.astype(BF16)
    return (lax.dot_general(hi, ones, dims, preferred_element_type=F32)
            + lax.dot_general(mid, ones, dims, preferred_element_type=F32)
            + lax.dot_general(lo, ones, dims, preferred_element_type=F32))


def _sigmoid(x):
    return 1.0 / (1.0 + jnp.exp(-x))


def _matmul(a, b, *, ta=False, tb=False, out_dtype, tm, tn, tk, name):
    m = a.shape[1] if ta else a.shape[0]
    k = a.shape[0] if ta else a.shape[1]
    n = b.shape[0] if tb else b.shape[1]
    assert (b.shape[1] if tb else b.shape[0]) == k
    assert m % tm == 0 and n % tn == 0 and k % tk == 0, (name, m, n, k, tm, tn, tk)
    nk = k // tk
    a_spec = pl.BlockSpec((tk, tm), lambda i, j, kk: (kk, i)) if ta else pl.BlockSpec((tm, tk), lambda i, j, kk: (i, kk))
    b_spec = pl.BlockSpec((tn, tk), lambda i, j, kk: (j, kk)) if tb else pl.BlockSpec((tk, tn), lambda i, j, kk: (kk, j))
    dims = (((0 if ta else 1,), (1 if tb else 0,)), ((), ()))

    def body(a_ref, b_ref, o_ref, acc_ref):
        kk = pl.program_id(2)

        @pl.when(kk == 0)
        def _():
            acc_ref[...] = jnp.zeros_like(acc_ref)

        acc_ref[...] += lax.dot_general(a_ref[...].astype(BF16), b_ref[...].astype(BF16), dims,
                                        preferred_element_type=F32)

        @pl.when(kk == nk - 1)
        def _():
            o_ref[...] = acc_ref[...].astype(out_dtype)

    return pl.pallas_call(
        body, name=name, grid=(m // tm, n // tn, nk),
        in_specs=[a_spec, b_spec],
        out_specs=pl.BlockSpec((tm, tn), lambda i, j, kk: (i, j)),
        out_shape=jax.ShapeDtypeStruct((m, n), out_dtype),
        scratch_shapes=[pltpu.VMEM((tm, tn), F32)],
        compiler_params=_cparams("parallel", "parallel", "arbitrary"),
    )(a, b)


def _rmsnorm_fwd(x, w, name):
    def body(x_ref, w_ref, o_ref):
        xv = x_ref[...]
        r = lax.rsqrt(jnp.mean(xv * xv, axis=-1, keepdims=True) + EPS)
        o_ref[...] = (xv * r * w_ref[...]).astype(BF16)

    return pl.pallas_call(
        body, name=name, grid=(LP // TM,),
        in_specs=[pl.BlockSpec((TM, D), lambda i: (i, 0)), pl.BlockSpec((1, D), lambda i: (0, 0))],
        out_specs=pl.BlockSpec((TM, D), lambda i: (i, 0)),
        out_shape=jax.ShapeDtypeStruct((LP, D), BF16),
        compiler_params=_cparams("parallel"),
    )(x, w)


def _resid_norm(h, m, w, name):
    def body(h_ref, m_ref, w_ref, o_ref):
        mv = m_ref[...]
        r = lax.rsqrt(jnp.mean(mv * mv, axis=-1, keepdims=True) + EPS)
        row = pl.program_id(0) * TM + lax.broadcasted_iota(jnp.int32, (TM, 1), 0)
        o_ref[...] = h_ref[...] + jnp.where(row >= PAD_ROWS, mv * r * w_ref[...], 0.0)

    return pl.pallas_call(
        body, name=name, grid=(LP // TM,),
        in_specs=[pl.BlockSpec((TM, D), lambda i: (i, 0)), pl.BlockSpec((TM, D), lambda i: (i, 0)),
                  pl.BlockSpec((1, D), lambda i: (0, 0))],
        out_specs=pl.BlockSpec((TM, D), lambda i: (i, 0)),
        out_shape=jax.ShapeDtypeStruct((LP, D), F32),
        compiler_params=_cparams("parallel"),
    )(h, m, w)


def _norm_bwd(dy, x, w, resid, out_dtype, name):
    has_resid = resid is not None

    def body(*refs):
        if has_resid:
            dy_ref, x_ref, w_ref, r_ref, dx_ref, dw_ref = refs
        else:
            dy_ref, x_ref, w_ref, dx_ref, dw_ref = refs
        i = pl.program_id(0)

        @pl.when(i == 0)
        def _():
            dw_ref[...] = jnp.zeros_like(dw_ref)

        row = i * TM + lax.broadcasted_iota(jnp.int32, (TM, 1), 0)
        dyv = jnp.where(row >= PAD_ROWS, dy_ref[...], 0.0)
        xv = x_ref[...]
        r = lax.rsqrt(jnp.mean(xv * xv, axis=-1, keepdims=True) + EPS)
        g = dyv * w_ref[...]
        dx = r * g - xv * (r * r * r * jnp.mean(g * xv, axis=-1, keepdims=True))
        if has_resid:
            dx = dx + r_ref[...]
        dx_ref[...] = dx.astype(out_dtype)
        dw_ref[0:1, :] += jnp.sum(dyv * xv * r, axis=0, keepdims=True)

    tile = pl.BlockSpec((TM, D), lambda i: (i, 0))
    in_specs = [tile, tile, pl.BlockSpec((1, D), lambda i: (0, 0))] + ([tile] if has_resid else [])
    args = (dy, x, w) + ((resid,) if has_resid else ())
    return pl.pallas_call(
        body, name=name, grid=(LP // TM,),
        in_specs=in_specs,
        out_specs=[tile, pl.BlockSpec((8, D), lambda i: (0, 0))],
        out_shape=[jax.ShapeDtypeStruct((LP, D), out_dtype), jax.ShapeDtypeStruct((8, D), F32)],
        compiler_params=_cparams("arbitrary"),
    )(*args)


def _loss_head(y, target, name):
    def body(y_ref, t_ref, dy_ref, loss_ref):
        i = pl.program_id(0)

        @pl.when(i == 0)
        def _():
            loss_ref[...] = jnp.zeros_like(loss_ref)

        row = i * TM + lax.broadcasted_iota(jnp.int32, (TM, 1), 0)
        diff = jnp.where(row >= CHUNK, y_ref[...] - t_ref[...], 0.0)
        dy_ref[...] = diff * (1.0 / D)
        loss_ref[...] += (0.5 / D) * jnp.sum(diff * diff)

    tile = pl.BlockSpec((TM, D), lambda i: (i, 0))
    return pl.pallas_call(
        body, name=name, grid=(LP // TM,),
        in_specs=[tile, tile],
        out_specs=[tile, pl.BlockSpec((8, 128), lambda i: (0, 0))],
        out_shape=[jax.ShapeDtypeStruct((LP, D), F32), jax.ShapeDtypeStruct((8, 128), F32)],
        compiler_params=_cparams("arbitrary"),
    )(y, target)


GELU_C = math.sqrt(2.0 / math.pi)
GELU_K = 0.044715


def _shift_down(x, prev8, rows):
    row = lax.broadcasted_iota(jnp.int32, (rows, 1), 0)
    p1 = pltpu.roll(prev8, 1, 0)
    p2 = pltpu.roll(prev8, 2, 0)
    x1 = jnp.where(row == 0, p1[0:1, :], pltpu.roll(x, 1, 0))
    x2 = jnp.where(row == 0, p2[0:1, :], jnp.where(row == 1, p2[1:2, :], pltpu.roll(x, 2, 0)))
    return x1, x2


def _conv_act_fwd(u, cw8, name):
    n_rows = LP // TM
    cb2 = 2 * CONV_BLOCK

    def body(u_ref, cw_ref, act_ref, carry_ref):
        i = pl.program_id(1)

        @pl.when(i == 0)
        def _():
            carry_ref[...] = jnp.zeros_like(carry_ref)

        x = u_ref[...].astype(F32)
        x1, x2 = _shift_down(x, carry_ref[...], TM)
        conv = cw_ref[3:4, :] + x2 * cw_ref[0:1, :] + x1 * cw_ref[1:2, :] + x * cw_ref[2:3, :]
        a = conv[:, :CONV_BLOCK]
        g = conv[:, CONV_BLOCK:]
        t = jnp.tanh(GELU_C * (a + GELU_K * a * a * a))
        act_ref[...] = (0.5 * a * (1.0 + t) * g).astype(BF16)
        carry_ref[...] = x[TM - 8:TM, :]

    return pl.pallas_call(
        body, name=name, grid=(N_CONV_BLOCKS, n_rows),
        in_specs=[pl.BlockSpec((TM, cb2), lambda j, i: (i, j)), pl.BlockSpec((8, cb2), lambda j, i: (0, j))],
        out_specs=pl.BlockSpec((TM, CONV_BLOCK), lambda j, i: (i, j)),
        out_shape=jax.ShapeDtypeStruct((LP, D_FF), BF16),
        scratch_shapes=[pltpu.VMEM((8, cb2), F32)],
        compiler_params=_cparams("arbitrary", "arbitrary"),
    )(u, cw8)


def _conv_act_bwd(dact, u, cw8, name):
    n_rows = LP // TM
    cb2 = 2 * CONV_BLOCK
    halo_per_tile = TM // 16

    def body(dact_ref, u_ref, uh_ref, cw_ref, du_ref, dcw_ref, carry_ref):
        i = pl.program_id(1)
        tile = n_rows - 1 - i

        @pl.when(i == 0)
        def _():
            dcw_ref[...] = jnp.zeros_like(dcw_ref)
            carry_ref[...] = jnp.zeros_like(carry_ref)

        x = u_ref[...].astype(F32)
        prev8 = jnp.where(tile == 0, 0.0, uh_ref[8:16, :].astype(F32))
        x1, x2 = _shift_down(x, prev8, TM)
        w0, w1, w2 = cw_ref[0:1, :], cw_ref[1:2, :], cw_ref[2:3, :]
        conv = cw_ref[3:4, :] + x2 * w0 + x1 * w1 + x * w2
        a = conv[:, :CONV_BLOCK]
        g = conv[:, CONV_BLOCK:]
        t = jnp.tanh(GELU_C * (a + GELU_K * a * a * a))
        gel = 0.5 * a * (1.0 + t)
        dgel = 0.5 * (1.0 + t) + 0.5 * a * (1.0 - t * t) * (GELU_C * (1.0 + 3.0 * GELU_K * a * a))
        dav = dact_ref[...].astype(F32)
        dconv = jnp.concatenate([dav * g * dgel, dav * gel], axis=1)
        dcw_ref[0:1, :] += jnp.sum(dconv * x2, axis=0, keepdims=True)
        dcw_ref[1:2, :] += jnp.sum(dconv * x1, axis=0, keepdims=True)
        dcw_ref[2:3, :] += jnp.sum(dconv * x, axis=0, keepdims=True)
        dcw_ref[3:4, :] += jnp.sum(dconv, axis=0, keepdims=True)
        nxt = carry_ref[...]
        row = lax.broadcasted_iota(jnp.int32, (TM, 1), 0)
        d1 = jnp.where(row == TM - 1, nxt[0:1, :], pltpu.roll(dconv, TM - 1, 0))
        d2 = jnp.where(row == TM - 2, nxt[0:1, :], jnp.where(row == TM - 1, nxt[1:2, :], pltpu.roll(dconv, TM - 2, 0)))
        du_ref[...] = (dconv * w2 + d1 * w1 + d2 * w0).astype(BF16)
        carry_ref[...] = dconv[0:8, :]

    return pl.pallas_call(
        body, name=name, grid=(N_CONV_BLOCKS, n_rows),
        in_specs=[pl.BlockSpec((TM, CONV_BLOCK), lambda j, i: (n_rows - 1 - i, j)),
                  pl.BlockSpec((TM, cb2), lambda j, i: (n_rows - 1 - i, j)),
                  pl.BlockSpec((16, cb2), lambda j, i: (jnp.maximum((n_rows - 1 - i) * halo_per_tile - 1, 0), j)),
                  pl.BlockSpec((8, cb2), lambda j, i: (0, j))],
        out_specs=[pl.BlockSpec((TM, cb2), lambda j, i: (n_rows - 1 - i, j)),
                   pl.BlockSpec((8, cb2), lambda j, i: (0, j))],
        out_shape=[jax.ShapeDtypeStruct((LP, D_UP), BF16), jax.ShapeDtypeStruct((8, D_UP), F32)],
        scratch_shapes=[pltpu.VMEM((8, cb2), F32)],
        compiler_params=_cparams("arbitrary", "arbitrary"),
    )(dact, u, u, cw8)


def _ret_consts(h):
    lg = math.log(1.0 - 2.0 ** (-5.0 - h))
    ri = lax.broadcasted_iota(jnp.int32, (CHUNK, CHUNK), 0)
    ci = lax.broadcasted_iota(jnp.int32, (CHUNK, CHUNK), 1)
    diff = (ri - ci).astype(F32)
    dmat = jnp.where(diff >= 0, jnp.exp(lg * jnp.maximum(diff, 0.0)), 0.0)
    rowf = lax.broadcasted_iota(jnp.int32, (CHUNK, 1), 0).astype(F32)
    zeta = jnp.exp(lg * (CHUNK - 1.0 - rowf))
    xi = jnp.exp(lg * (rowf + 1.0))
    return dmat, zeta, xi, math.exp(lg * CHUNK)


def _rope(t, cosv, sinv):
    return t * cosv + pltpu.roll(t, RET_DK // 2, 1) * sinv


def _unrope(d, cosv, sinv):
    return d * cosv + pltpu.roll(d * sinv, RET_DK // 2, 1)


def _gla_common(p_ref, w2_ref, gb_ref, chunk):
    row = lax.broadcasted_iota(jnp.int32, (CHUNK, 1), 0)
    real = (chunk * CHUNK + row) >= PAD_ROWS
    ga = p_ref[:, O_GA:O_GA + 128]
    z = _dot(ga, w2_ref[...]) + gb_ref[...]
    la = (jnp.minimum(z, 0.0) - jnp.log(1.0 + jnp.exp(-jnp.abs(z)))) * (1.0 / GLA_TAU)
    la = jnp.where(real, la, 0.0)
    ri = lax.broadcasted_iota(jnp.int32, (CHUNK, CHUNK), 0)
    ci = lax.broadcasted_iota(jnp.int32, (CHUNK, CHUNK), 1)
    tril = (ri >= ci).astype(F32)
    cum = _dot_exact_rhs(tril, la)
    last = cum[CHUNK - 1:CHUNK, :]
    qs = p_ref[:, O_GQ:O_GQ + 256] * (GLA_DK ** -0.5)
    k = p_ref[:, O_GK:O_GK + 256]
    ecum = jnp.exp(cum)
    ekl = jnp.exp(last - cum)
    el = jnp.exp(last)
    refs = [jnp.zeros((1, 256), F32)] + [cum[a * SUB - 1:a * SUB, :] for a in range(1, N_SUB)]
    eq = [jnp.exp(cum[a * SUB:(a + 1) * SUB, :] - refs[a]) for a in range(N_SUB)]
    spread = refs[0] - cum[SUB - 1:SUB, :]
    for a in range(1, N_SUB):
        spread = jnp.maximum(spread, refs[a] - cum[(a + 1) * SUB - 1:(a + 1) * SUB, :])
    small = jnp.max(spread) <= GLA_FACTORED_MAX
    return dict(real=real, row=row, z=z, la=la, cum=cum, last=last, qs=qs, k=k, ecum=ecum, ekl=ekl, el=el,
                refs=refs, eq=eq, small=small, ri=ri, ci=ci)


GLA_FACTORED_MAX = 40.0


def _gla_factored_keys(c):
    return [c["k"] * jnp.exp(jnp.minimum(c["refs"][a] - c["cum"], GLA_FACTORED_MAX)) for a in range(N_SUB)]


def _causal_rows(c, a):
    return c["ci"][:SUB, :] <= c["ri"][:SUB, :] + a * SUB


def _gla_scores_factored(c, keys, h):
    sl = slice(GLA_DK * h, GLA_DK * (h + 1))
    blocks = []
    for a in range(N_SUB):
        qh = c["qs"][a * SUB:(a + 1) * SUB, sl] * c["eq"][a][:, sl]
        blocks.append(jnp.where(_causal_rows(c, a), _dot_nt(qh, keys[a][:, sl]), 0.0))
    return jnp.concatenate(blocks, axis=0)


def _gla_lag_weights(c):
    cum, row = c["cum"], c["row"]
    out = [jnp.ones((CHUNK, 256), F32)]
    for r in range(1, SUB):
        out.append(jnp.where((row % SUB) >= r, jnp.exp(jnp.minimum(cum - pltpu.roll(cum, r, 0), 0.0)), 0.0))
    return out


def _gla_pairwise_keys(c):
    return [None] + [c["k"] * jnp.exp(jnp.minimum(c["refs"][a] - c["cum"], 0.0)) for a in range(1, N_SUB)]


def _gla_scores_pairwise(c, lag_w, keys, h):
    sl = slice(GLA_DK * h, GLA_DK * (h + 1))
    qs, k = c["qs"][:, sl], c["k"][:, sl]
    ri, ci = c["ri"], c["ci"]
    p = jnp.zeros((CHUNK, CHUNK), F32)
    for r in range(SUB):
        kr = k if r == 0 else pltpu.roll(k, r, 0)
        pr = jnp.sum(qs * kr * lag_w[r][:, sl], axis=1, keepdims=True)
        p = p + jnp.where(ci == ri - r, pr, 0.0)
    blocks = [jnp.zeros((SUB, CHUNK), F32)]
    for a in range(1, N_SUB):
        qh = qs[a * SUB:(a + 1) * SUB, :] * c["eq"][a][:, sl]
        blocks.append(jnp.where(ci[:SUB, :] < a * SUB, _dot_nt(qh, keys[a][:, sl]), 0.0))
    return p + jnp.concatenate(blocks, axis=0)


def _gla_all_scores(c, p_scr):
    @pl.when(c["small"])
    def _():
        keys = _gla_factored_keys(c)
        for h in range(GLA_HEADS):
            p_scr[h] = _gla_scores_factored(c, keys, h)

    @pl.when(jnp.logical_not(c["small"]))
    def _():
        lag_w, keys = _gla_lag_weights(c), _gla_pairwise_keys(c)
        for h in range(GLA_HEADS):
            p_scr[h] = _gla_scores_pairwise(c, lag_w, keys, h)


def _gla_intra_bwd_factored(c, keys, dp, h):
    sl = slice(GLA_DK * h, GLA_DK * (h + 1))
    dq_rows = []
    dk = jnp.zeros((CHUNK, GLA_DK), F32)
    for a in range(N_SUB):
        eq = c["eq"][a][:, sl]
        qh = c["qs"][a * SUB:(a + 1) * SUB, sl] * eq
        dpa = jnp.where(_causal_rows(c, a), dp[a * SUB:(a + 1) * SUB, :], 0.0)
        dq_rows.append(_dot(dpa, keys[a][:, sl]) * eq)
        ek = jnp.exp(jnp.minimum(c["refs"][a][:, sl] - c["cum"][:, sl], GLA_FACTORED_MAX))
        dk = dk + _dot_tn(dpa, qh) * ek
    return jnp.concatenate(dq_rows, axis=0), dk


def _gla_intra_bwd_pairwise(c, lag_w, keys, dp, h):
    sl = slice(GLA_DK * h, GLA_DK * (h + 1))
    qs_h, k_h = c["qs"][:, sl], c["k"][:, sl]
    ri, ci = c["ri"], c["ci"]
    dq_rows = [jnp.zeros((SUB, GLA_DK), F32)]
    dk = jnp.zeros((CHUNK, GLA_DK), F32)
    for a in range(1, N_SUB):
        eq = c["eq"][a][:, sl]
        qh = qs_h[a * SUB:(a + 1) * SUB, :] * eq
        dpa = jnp.where(ci[:SUB, :] < a * SUB, dp[a * SUB:(a + 1) * SUB, :], 0.0)
        dq_rows.append(_dot(dpa, keys[a][:, sl]) * eq)
        ek = jnp.exp(jnp.minimum(c["refs"][a][:, sl] - c["cum"][:, sl], 0.0))
        dk = dk + _dot_tn(dpa, qh) * ek
    dq = jnp.concatenate(dq_rows, axis=0)
    for r in range(SUB):
        w = lag_w[r][:, sl]
        dpr = jnp.sum(jnp.where(ci == ri - r, dp, 0.0), axis=1, keepdims=True)
        kr = k_h if r == 0 else pltpu.roll(k_h, r, 0)
        dq = dq + dpr * kr * w
        back = dpr * qs_h * w
        dk = dk + (back if r == 0 else pltpu.roll(back, CHUNK - r, 0))
    return dq, dk


def _gla_all_intra_bwd(c, dps, p_scr, dq_scr, dk_scr):
    @pl.when(c["small"])
    def _():
        keys = _gla_factored_keys(c)
        outs = [_gla_intra_bwd_factored(c, keys, dps[h], h) for h in range(GLA_HEADS)]
        for h in range(GLA_HEADS):
            p_scr[h] = _gla_scores_factored(c, keys, h)
        dq_scr[...] = jnp.concatenate([o[0] for o in outs], axis=1)
        dk_scr[...] = jnp.concatenate([o[1] for o in outs], axis=1)

    @pl.when(jnp.logical_not(c["small"]))
    def _():
        lag_w, keys = _gla_lag_weights(c), _gla_pairwise_keys(c)
        outs = [_gla_intra_bwd_pairwise(c, lag_w, keys, dps[h], h) for h in range(GLA_HEADS)]
        for h in range(GLA_HEADS):
            p_scr[h] = _gla_scores_pairwise(c, lag_w, keys, h)
        dq_scr[...] = jnp.concatenate([o[0] for o in outs], axis=1)
        dk_scr[...] = jnp.concatenate([o[1] for o in outs], axis=1)


def _mixer_fwd(proj, cos2, sin2, w2p, gb, rnw, gnw, name):
    def body(p_ref, c_ref, s_ref, w2_ref, gb_ref, rnw_ref, gnw_ref,
             ocat_ref, mrg_ref, sr_out, sg_out, sr, sg, p_scr):
        n = pl.program_id(0)

        @pl.when(n == 0)
        def _():
            sr[...] = jnp.zeros_like(sr)
            sg[...] = jnp.zeros_like(sg)

        sr_out[0] = sr[...]
        sg_out[0] = sg[...]
        cosv, sinv = c_ref[...], s_ref[...]

        for h in range(RET_HEADS):
            dmat, zeta, xi, gc = _ret_consts(h)
            hs = slice(128 * h, 128 * (h + 1))
            q = _rope(p_ref[:, O_RQ + 128 * h:O_RQ + 128 * (h + 1)], cosv, sinv)
            k = _rope(p_ref[:, O_RK + 128 * h:O_RK + 128 * (h + 1)], cosv, sinv) * (RET_DK ** -0.5)
            v = p_ref[:, O_RV + 128 * h:O_RV + 128 * (h + 1)]
            g = p_ref[:, O_RG + 128 * h:O_RG + 128 * (h + 1)]
            s_in = sr[h]
            a = _dot_nt(q, k) * dmat
            o = _dot(a, v) + _dot(q, s_in) * xi
            sr[h] = gc * s_in + _dot_tn(k * zeta, v)
            mu = jnp.mean(o, axis=-1, keepdims=True)
            xc = o - mu
            nrm = xc * lax.rsqrt(jnp.mean(xc * xc, axis=-1, keepdims=True) + EPS)
            ocat_ref[:, hs] = o
            mrg_ref[:, hs] = (nrm * rnw_ref[:, hs] * (g * _sigmoid(g))).astype(BF16)

        c = _gla_common(p_ref, w2_ref, gb_ref, n)
        _gla_all_scores(c, p_scr)
        lastcol = _dot_tn_exact_lhs(c["la"], jnp.ones((CHUNK, GLA_DV), F32))
        qe = c["qs"] * c["ecum"]
        kl = c["k"] * c["ekl"]
        for h in range(GLA_HEADS):
            sl = slice(GLA_DK * h, GLA_DK * (h + 1))
            hs = slice(512 + 128 * h, 512 + 128 * (h + 1))
            v = p_ref[:, O_GV + 128 * h:O_GV + 128 * (h + 1)]
            g = p_ref[:, O_GR + 128 * h:O_GR + 128 * (h + 1)]
            s_in = sg[h]
            o = _dot(p_scr[h], v) + _dot(qe[:, sl], s_in)
            sg[h] = jnp.exp(lastcol[GLA_DK * h:GLA_DK * (h + 1), :]) * s_in + _dot_tn(kl[:, sl], v)
            nrm = o * lax.rsqrt(jnp.mean(o * o, axis=-1, keepdims=True) + EPS)
            ocat_ref[:, hs] = o
            mrg_ref[:, hs] = (nrm * gnw_ref[:, 128 * h:128 * (h + 1)] * (g * _sigmoid(g))).astype(BF16)

    const = lambda shape: pl.BlockSpec(shape, lambda n: (0,) * len(shape))
    return pl.pallas_call(
        body, name=name, grid=(N_CHUNKS,),
        in_specs=[pl.BlockSpec((CHUNK, IN_WP), lambda n: (n, 0)),
                  pl.BlockSpec((CHUNK, 128), lambda n: (n, 0)), pl.BlockSpec((CHUNK, 128), lambda n: (n, 0)),
                  const((128, 256)), const((1, 256)), const((1, 512)), const((1, 512))],
        out_specs=[pl.BlockSpec((CHUNK, D), lambda n: (n, 0)), pl.BlockSpec((CHUNK, D), lambda n: (n, 0)),
                   pl.BlockSpec((1, RET_HEADS, RET_DK, 128), lambda n: (n, 0, 0, 0)),
                   pl.BlockSpec((1, GLA_HEADS, GLA_DK, GLA_DV), lambda n: (n, 0, 0, 0))],
        out_shape=[jax.ShapeDtypeStruct((LP, D), F32), jax.ShapeDtypeStruct((LP, D), BF16),
                   jax.ShapeDtypeStruct((N_CHUNKS, RET_HEADS, RET_DK, 128), F32),
                   jax.ShapeDtypeStruct((N_CHUNKS, GLA_HEADS, GLA_DK, GLA_DV), F32)],
        scratch_shapes=[pltpu.VMEM((RET_HEADS, RET_DK, 128), F32), pltpu.VMEM((GLA_HEADS, GLA_DK, GLA_DV), F32),
                        pltpu.VMEM((GLA_HEADS, CHUNK, CHUNK), F32)],
        compiler_params=_cparams("arbitrary"),
    )(proj, cos2, sin2, w2p, gb, rnw, gnw)


def _mixer_bwd(proj, ocat, dmrg, sr_all, sg_all, cos2, sin2, w2p, gb, rnw, gnw, name):
    last_chunk = N_CHUNKS - 1

    def body(p_ref, ocat_ref, dm_ref, sr_ref, sg_ref, c_ref, s_ref, w2_ref, gb_ref, rnw_ref, gnw_ref,
             dp_ref, dw2_ref, dgb_ref, drn_ref, dgn_ref, dsr, dsg, p_scr, dq_scr, dk_scr):
        step = pl.program_id(0)
        n = last_chunk - step

        @pl.when(step == 0)
        def _():
            dsr[...] = jnp.zeros_like(dsr)
            dsg[...] = jnp.zeros_like(dsg)
            dw2_ref[...] = jnp.zeros_like(dw2_ref)
            dgb_ref[...] = jnp.zeros_like(dgb_ref)
            drn_ref[...] = jnp.zeros_like(drn_ref)
            dgn_ref[...] = jnp.zeros_like(dgn_ref)

        cosv, sinv = c_ref[...], s_ref[...]
        row = lax.broadcasted_iota(jnp.int32, (CHUNK, 1), 0)
        real = ((n * CHUNK + row) >= PAD_ROWS).astype(F32)

        for h in range(RET_HEADS):
            dmat, zeta, xi, gc = _ret_consts(h)
            hs = slice(128 * h, 128 * (h + 1))
            q = _rope(p_ref[:, O_RQ + 128 * h:O_RQ + 128 * (h + 1)], cosv, sinv)
            k = _rope(p_ref[:, O_RK + 128 * h:O_RK + 128 * (h + 1)], cosv, sinv) * (RET_DK ** -0.5)
            v = p_ref[:, O_RV + 128 * h:O_RV + 128 * (h + 1)]
            g = p_ref[:, O_RG + 128 * h:O_RG + 128 * (h + 1)]
            o = ocat_ref[:, hs]
            dy = dm_ref[:, hs]
            wv = rnw_ref[:, hs]
            mu = jnp.mean(o, axis=-1, keepdims=True)
            xc = o - mu
            rs = lax.rsqrt(jnp.mean(xc * xc, axis=-1, keepdims=True) + EPS)
            nrm = xc * rs
            sgm = _sigmoid(g)
            sil = g * sgm
            drn_ref[0:1, hs] += jnp.sum(dy * nrm * sil, axis=0, keepdims=True)
            dgate = dy * nrm * wv * (sgm * (1.0 + g * (1.0 - sgm)))
            dn = dy * wv * sil
            do = rs * (dn - jnp.mean(dn, axis=-1, keepdims=True) - nrm * jnp.mean(dn * nrm, axis=-1, keepdims=True))
            s_in = sr_ref[0, h]
            ds_out = dsr[h]
            a = _dot_nt(q, k) * dmat
            da = _dot_nt(do, v) * dmat
            dox = do * xi
            dq = _dot(da, k) + _dot_nt(dox, s_in)
            dk = _dot_tn(da, q) + _dot_nt(v, ds_out) * zeta
            dv = _dot_tn(a, do) + _dot(k * zeta, ds_out)
            dsr[h] = gc * ds_out + _dot_tn(q, dox)
            dk = dk * (RET_DK ** -0.5)
            dp_ref[:, O_RQ + 128 * h:O_RQ + 128 * (h + 1)] = (_unrope(dq, cosv, sinv) * real).astype(BF16)
            dp_ref[:, O_RK + 128 * h:O_RK + 128 * (h + 1)] = (_unrope(dk, cosv, sinv) * real).astype(BF16)
            dp_ref[:, O_RV + 128 * h:O_RV + 128 * (h + 1)] = (dv * real).astype(BF16)
            dp_ref[:, O_RG + 128 * h:O_RG + 128 * (h + 1)] = (dgate * real).astype(BF16)

        c = _gla_common(p_ref, w2_ref, gb_ref, n)
        ri, ci = c["ri"], c["ci"]
        causal = ri >= ci
        triu = (ci >= ri).astype(F32)
        qe = c["qs"] * c["ecum"]
        kl = c["k"] * c["ekl"]
        lastcol = _dot_tn_exact_lhs(c["la"], jnp.ones((CHUNK, GLA_DV), F32))
        dla_heads, dq_heads, dk_heads = [], [], []
        dos, dps = [], []
        for h in range(GLA_HEADS):
            hs = slice(512 + 128 * h, 512 + 128 * (h + 1))
            v = p_ref[:, O_GV + 128 * h:O_GV + 128 * (h + 1)]
            g = p_ref[:, O_GR + 128 * h:O_GR + 128 * (h + 1)]
            o = ocat_ref[:, hs]
            dy = dm_ref[:, hs]
            wv = gnw_ref[:, 128 * h:128 * (h + 1)]
            rs = lax.rsqrt(jnp.mean(o * o, axis=-1, keepdims=True) + EPS)
            nrm = o * rs
            sgm = _sigmoid(g)
            sil = g * sgm
            dgn_ref[0:1, 128 * h:128 * (h + 1)] += jnp.sum(dy * nrm * sil, axis=0, keepdims=True)
            dgate = dy * nrm * wv * (sgm * (1.0 + g * (1.0 - sgm)))
            dn = dy * wv * sil
            do = rs * (dn - nrm * jnp.mean(dn * nrm, axis=-1, keepdims=True))
            dp_ref[:, O_GR + 128 * h:O_GR + 128 * (h + 1)] = (dgate * real).astype(BF16)
            dos.append(do)
            dps.append(jnp.where(causal, _dot_nt(do, v), 0.0))
        _gla_all_intra_bwd(c, dps, p_scr, dq_scr, dk_scr)
        dq_intra, dk_intra = dq_scr[...], dk_scr[...]
        for h in range(GLA_HEADS):
            sl = slice(GLA_DK * h, GLA_DK * (h + 1))
            v = p_ref[:, O_GV + 128 * h:O_GV + 128 * (h + 1)]
            do = dos[h]
            qs_h, k_h = c["qs"][:, sl], c["k"][:, sl]
            s_in = sg_ref[0, h]
            ds_out = dsg[h]
            el_col = jnp.exp(lastcol[GLA_DK * h:GLA_DK * (h + 1), :])
            dv = _dot_tn(p_scr[h], do) + _dot(kl[:, sl], ds_out)
            dqe = _dot_nt(do, s_in)
            dkl = _dot_nt(v, ds_out)
            dsg[h] = _dot_tn(qe[:, sl], do) + el_col * ds_out
            sd = s_in * ds_out
            sd_hi = sd.astype(BF16)
            sd_lo = (sd - sd_hi.astype(F32)).astype(BF16)
            ones8 = jnp.ones((8, GLA_DV), BF16)
            nt = (((1,), (1,)), ((), ()))
            d_el = (lax.dot_general(ones8, sd_hi, nt, preferred_element_type=F32)
                    + lax.dot_general(ones8, sd_lo, nt, preferred_element_type=F32))[0:1, :]
            dqs = dqe * c["ecum"][:, sl] + dq_intra[:, sl]
            dkk = dkl * c["ekl"][:, sl] + dk_intra[:, sl]
            d_last = jnp.sum(dkl * kl[:, sl], axis=0, keepdims=True) + d_el * c["el"][:, sl]
            dcum = qs_h * dqs - k_h * dkk + jnp.where(row == CHUNK - 1, d_last, 0.0)
            dla_heads.append(_dot_exact_rhs(triu, dcum))
            dq_heads.append(dqs * (GLA_DK ** -0.5))
            dk_heads.append(dkk)
            dp_ref[:, O_GV + 128 * h:O_GV + 128 * (h + 1)] = (dv * real).astype(BF16)

        dla = jnp.concatenate(dla_heads, axis=1)
        dp_ref[:, O_GQ:O_GQ + 256] = (jnp.concatenate(dq_heads, axis=1) * real).astype(BF16)
        dp_ref[:, O_GK:O_GK + 256] = (jnp.concatenate(dk_heads, axis=1) * real).astype(BF16)
        dz = dla * (1.0 / GLA_TAU) * _sigmoid(-c["z"]) * real
        ga = p_ref[:, O_GA:O_GA + 128]
        dp_ref[:, O_GA:O_GA + 128] = _dot_nt(dz, w2_ref[...]).astype(BF16)
        dp_ref[:, O_GA + 128:IN_WP] = jnp.zeros((CHUNK, IN_WP - O_GA - 128), BF16)
        dw2_ref[...] += _dot_tn(ga, dz)
        dgb_ref[0:1, :] += jnp.sum(dz, axis=0, keepdims=True)

    const = lambda shape: pl.BlockSpec(shape, lambda s: (0,) * len(shape))
    rev = lambda s: (last_chunk - s, 0)
    return pl.pallas_call(
        body, name=name, grid=(N_CHUNKS,),
        in_specs=[pl.BlockSpec((CHUNK, IN_WP), rev), pl.BlockSpec((CHUNK, D), rev), pl.BlockSpec((CHUNK, D), rev),
                  pl.BlockSpec((1, RET_HEADS, RET_DK, 128), lambda s: (last_chunk - s, 0, 0, 0)),
                  pl.BlockSpec((1, GLA_HEADS, GLA_DK, GLA_DV), lambda s: (last_chunk - s, 0, 0, 0)),
                  pl.BlockSpec((CHUNK, 128), rev), pl.BlockSpec((CHUNK, 128), rev),
                  const((128, 256)), const((1, 256)), const((1, 512)), const((1, 512))],
        out_specs=[pl.BlockSpec((CHUNK, IN_WP), rev), const((128, 256)), const((8, 256)),
                   const((8, 512)), const((8, 512))],
        out_shape=[jax.ShapeDtypeStruct((LP, IN_WP), BF16), jax.ShapeDtypeStruct((128, 256), F32),
                   jax.ShapeDtypeStruct((8, 256), F32), jax.ShapeDtypeStruct((8, 512), F32),
                   jax.ShapeDtypeStruct((8, 512), F32)],
        scratch_shapes=[pltpu.VMEM((RET_HEADS, RET_DK, 128), F32), pltpu.VMEM((GLA_HEADS, GLA_DK, GLA_DV), F32),
                        pltpu.VMEM((GLA_HEADS, CHUNK, CHUNK), F32), pltpu.VMEM((CHUNK, 256), F32),
                        pltpu.VMEM((CHUNK, 256), F32)],
        compiler_params=_cparams("arbitrary"),
    )(proj, ocat, dmrg, sr_all, sg_all, cos2, sin2, w2p, gb, rnw, gnw)


def _all_gather(xs, name):
    n = len(xs)

    def body(*refs):
        x_refs, out_refs = refs[:n], refs[n:2 * n]
        send_sems, recv_sems, local_sems = refs[2 * n:]
        mx, my, mc = lax.axis_index("x"), lax.axis_index("y"), lax.axis_index("c")
        me, sibling = (mx, my, mc), (mx, my, 1 - mc)
        chips = [(1 - mx, my), (mx, 1 - my), (1 - mx, 1 - my)]

        def slot(a, px, py, pc):
            return out_refs[a].at[4 * px + 2 * py + pc]

        def copy(a, k, block, to, src=None):
            return pltpu.make_async_remote_copy(
                src_ref=slot(a, *block) if src is None else src, dst_ref=slot(a, *block),
                send_sem=send_sems.at[7 * a + k], recv_sem=recv_sems.at[7 * a + k],
                device_id=to, device_id_type=MESH_IDS)

        mine = [pltpu.make_async_copy(x_refs[a], slot(a, *me), local_sems.at[a]) for a in range(n)]
        first, passed = [], []
        for a in range(n):
            mine[a].start()
            first.append(copy(a, 0, me, sibling, src=x_refs[a]))
            first += [copy(a, 1 + j, me, (*chip, mc), src=x_refs[a]) for j, chip in enumerate(chips)]
        for cp in first:
            cp.start()
        for j, chip in enumerate(chips):
            for a in range(n):
                copy(a, 1 + j, (*chip, mc), me).wait_recv()
                passed.append(copy(a, 4 + j, (*chip, mc), sibling))
                passed[-1].start()
        for a in range(n):
            copy(a, 0, sibling, me).wait_recv()
            for j, chip in enumerate(chips):
                copy(a, 4 + j, (*chip, 1 - mc), me).wait_recv()
        for cp in first + passed:
            cp.wait_send()
        for a in range(n):
            mine[a].wait()

    return pl.pallas_call(
        body, name=name,
        in_specs=[pl.BlockSpec(memory_space=pl.ANY)] * n,
        out_specs=[pl.BlockSpec(memory_space=pl.ANY)] * n,
        out_shape=[jax.ShapeDtypeStruct((N_DEV,) + x.shape, x.dtype) for x in xs],
        scratch_shapes=[pltpu.SemaphoreType.DMA((7 * n,)), pltpu.SemaphoreType.DMA((7 * n,)),
                        pltpu.SemaphoreType.DMA((n,))],
    )(*xs)


def _exchange_blocks(gs, name):
    n = len(gs)

    def body(*refs):
        g_refs, out_refs = refs[:n], refs[n:2 * n]
        send_sems, recv_sems, local_sems = refs[2 * n:]
        mx, my, mc = lax.axis_index("x"), lax.axis_index("y"), lax.axis_index("c")
        me = 4 * mx + 2 * my + mc
        mine = [pltpu.make_async_copy(g_refs[a].at[me], out_refs[a].at[me], local_sems.at[a]) for a in range(n)]
        copies = []
        for r in range(1, N_DEV):
            px, py, pc = mx ^ (r >> 2), my ^ ((r >> 1) & 1), mc ^ (r & 1)
            peer = 4 * px + 2 * py + pc
            for a in range(n):
                copies.append(pltpu.make_async_remote_copy(
                    src_ref=g_refs[a].at[peer], dst_ref=out_refs[a].at[me],
                    send_sem=send_sems.at[7 * a + r - 1], recv_sem=recv_sems.at[7 * a + r - 1],
                    device_id=(px, py, pc), device_id_type=MESH_IDS))
        for cp in mine + copies:
            cp.start()
        for cp in copies:
            cp.wait_recv()
        for cp in copies:
            cp.wait_send()
        for cp in mine:
            cp.wait()

    return pl.pallas_call(
        body, name=name,
        in_specs=[pl.BlockSpec(memory_space=pl.ANY)] * n,
        out_specs=[pl.BlockSpec(memory_space=pl.ANY)] * n,
        out_shape=[jax.ShapeDtypeStruct(g.shape, g.dtype) for g in gs],
        scratch_shapes=[pltpu.SemaphoreType.DMA((7 * n,)), pltpu.SemaphoreType.DMA((7 * n,)),
                        pltpu.SemaphoreType.DMA((n,))],
    )(*gs)


IN_SHARD = IN_W // N_DEV
IN_SHARD_P = 512
UP_SHARD = D_UP // N_DEV
UP_SHARD_P = 768
RELAYOUT_ROWS = 256


def _pieces_w_in():
    return [(k, 0, IN_SHARD * k, IN_SHARD) for k in range(N_DEV)]


def _pieces_ffn_up():
    pieces = []
    for k in range(N_DEV):
        n, end = UP_SHARD * k, UP_SHARD * (k + 1)
        while n < end:
            half, r = divmod(n, D_FF)
            blk, off = divmod(r, CONV_BLOCK)
            run = min(CONV_BLOCK - off, end - n)
            pieces.append((k, n - UP_SHARD * k, 2 * CONV_BLOCK * blk + CONV_BLOCK * half + off, run))
            n += run
    return pieces


def _assemble_block(load, spans, dst_block, rows):
    lo = 128 * dst_block
    lane = lax.broadcasted_iota(jnp.int32, (1, 128), 1)
    out = jnp.zeros((rows, 128), F32)
    for key, src_off, dst_off, length in spans:
        a, b = max(lo, dst_off), min(lo + 128, dst_off + length)
        s, s_end = src_off + (a - dst_off), src_off + (b - dst_off)
        d = a
        while s < s_end:
            e = min(s_end, 128 * (s // 128 + 1))
            blk = load(key, s // 128)
            shift = (d - s) % 128
            if shift:
                blk = pltpu.roll(blk, shift, 1)
            out = jnp.where((lane >= d - lo) & (lane < d - lo + (e - s)), blk, out)
            d += e - s
            s = e
    return out


def _shards_to_cols(shards, pieces, width, name):
    _, rows, _ = shards.shape
    tr = RELAYOUT_ROWS

    def body(s_ref, o_ref):
        load = lambda k, b: s_ref[k, :, 128 * b:128 * (b + 1)].astype(F32)
        for db in range(width // 128):
            o_ref[:, 128 * db:128 * (db + 1)] = _assemble_block(load, pieces, db, tr).astype(BF16)

    return pl.pallas_call(
        body, name=name, grid=(rows // tr,),
        in_specs=[pl.BlockSpec((N_DEV, tr, shards.shape[2]), lambda i: (0, i, 0))],
        out_specs=pl.BlockSpec((tr, width), lambda i: (i, 0)),
        out_shape=jax.ShapeDtypeStruct((rows, width), BF16),
        compiler_params=_cparams("parallel"),
    )(shards)


def _cols_to_shards(full, pieces, shard_width, name):
    rows, width = full.shape
    tr = RELAYOUT_ROWS

    def body(f_ref, o_ref):
        load = lambda _, b: f_ref[:, 128 * b:128 * (b + 1)].astype(F32)
        for k in range(N_DEV):
            spans = [(None, dst_off, src_off, length) for dev, src_off, dst_off, length in pieces if dev == k]
            for db in range(shard_width // 128):
                o_ref[k, :, 128 * db:128 * (db + 1)] = _assemble_block(load, spans, db, tr).astype(BF16)

    return pl.pallas_call(
        body, name=name, grid=(rows // tr,),
        in_specs=[pl.BlockSpec((tr, width), lambda i: (i, 0))],
        out_specs=pl.BlockSpec((N_DEV, tr, shard_width), lambda i: (0, i, 0)),
        out_shape=jax.ShapeDtypeStruct((N_DEV, rows, shard_width), BF16),
        compiler_params=_cparams("parallel"),
    )(full)


def _adamw(parts, w, m, v, rows_per_step, name):
    rows, cols = w.shape
    assert rows % rows_per_step == 0 and parts.shape == (N_DEV, rows, cols)

    def body(p_ref, w_ref, m_ref, v_ref, g_ref, d_ref, nm_ref, nv_ref):
        g = p_ref[0].astype(F32)
        for j in range(1, N_DEV):
            g = g + p_ref[j].astype(F32)
        m_new = ADAM_B1 * m_ref[...] + (1.0 - ADAM_B1) * g
        v_new = ADAM_B2 * v_ref[...] + (1.0 - ADAM_B2) * (g * g)
        m_hat = m_new / (1.0 - ADAM_B1 ** ADAM_STEP)
        v_hat = v_new / (1.0 - ADAM_B2 ** ADAM_STEP)
        g_ref[...] = g
        d_ref[...] = -ADAM_LR * (m_hat / (jnp.sqrt(v_hat) + ADAM_EPS) + ADAM_WD * w_ref[...])
        nm_ref[...] = m_new
        nv_ref[...] = v_new

    tile = pl.BlockSpec((rows_per_step, cols), lambda i: (i, 0))
    shape = jax.ShapeDtypeStruct((rows, cols), F32)
    return pl.pallas_call(
        body, name=name, grid=(rows // rows_per_step,),
        in_specs=[pl.BlockSpec((N_DEV, rows_per_step, cols), lambda i: (0, i, 0)), tile, tile, tile],
        out_specs=[tile, tile, tile, tile],
        out_shape=[shape, shape, shape, shape],
        compiler_params=_cparams("parallel"),
    )(parts, w, m, v)


BIG = (("w_in", (DEPTH, D, IN_W // N_DEV), 2), ("w_out", (DEPTH, D // N_DEV, D), 1),
       ("ffn_up", (DEPTH, D, D_UP // N_DEV), 2), ("ffn_down", (DEPTH, D_FF // N_DEV, D), 1))
SMALL = (("meta_tokens", (N_META, D // N_DEV), 1), ("gla_gate_w2", (DEPTH, GATE_RANK, 256 // N_DEV), 2),
         ("ffn_conv_w", (DEPTH, 3, D_UP // N_DEV), 2))
REPL = (("pre_mix_norm", (DEPTH, D)), ("gla_gate_b", (DEPTH, 256)), ("ret_norm_w", (DEPTH, 512)),
        ("gla_norm_w", (DEPTH, 512)), ("post_mix_norm", (DEPTH, D)), ("pre_ffn_norm", (DEPTH, D)),
        ("ffn_conv_b", (DEPTH, D_UP)), ("post_ffn_norm", (DEPTH, D)))
WEIGHT_ORDER = ("meta_tokens", "pre_mix_norm", "w_in", "gla_gate_w2", "gla_gate_b", "ret_norm_w", "gla_norm_w",
                "w_out", "post_mix_norm", "pre_ffn_norm", "ffn_up", "ffn_conv_w", "ffn_conv_b", "ffn_down",
                "post_ffn_norm")


def _size(shape):
    return math.prod(shape)


def _round_up(n, mult):
    return -(-n // mult) * mult


REPL_ROWS = _round_up(-(-sum(_size(s) for _, s in REPL) // LANES), 8)
SMALL_ROWS = _round_up(-(-sum(_size(s) for _, s, _ in SMALL) // LANES), 8)


def _pack(arrays, rows, dtype):
    flat = jnp.concatenate([a.reshape(-1).astype(dtype) for a in arrays])
    return jnp.pad(flat, (0, rows * LANES - flat.shape[0])).reshape(rows, LANES)


def _unpack(buf, shapes):
    flat = buf.reshape(-1)
    out, off = [], 0
    for shape in shapes:
        out.append(flat[off:off + _size(shape)].reshape(shape))
        off += _size(shape)
    return out


def _unshard(blocks, axis):
    moved = jnp.moveaxis(blocks, 0, axis)
    shape = list(moved.shape)
    shape[axis:axis + 2] = [shape[axis] * shape[axis + 1]]
    return moved.reshape(shape)


def _to_blocks(full, axis):
    shape = list(full.shape)
    shape[axis:axis + 1] = [N_DEV, shape[axis] // N_DEV]
    return jnp.moveaxis(full.reshape(shape), axis, 0)


def _interleave_cols(w):
    lead = w.shape[:-1]
    return jnp.swapaxes(w.reshape(lead + (2, N_CONV_BLOCKS, CONV_BLOCK)), -3, -2).reshape(lead + (D_UP,))


def _deinterleave_cols(w):
    lead = w.shape[:-1]
    return jnp.swapaxes(w.reshape(lead + (N_CONV_BLOCKS, 2, CONV_BLOCK)), -3, -2).reshape(lead + (D_UP,))


def _rope_tables():
    half = RET_DK // 2
    inv = ROPE_BASE ** (-jnp.arange(half, dtype=F32) / half)
    pos = jnp.arange(LP, dtype=F32) - float(PAD_ROWS)
    ang = pos[:, None] * inv[None, :]
    c, s = jnp.cos(ang), jnp.sin(ang)
    return jnp.concatenate([c, c], axis=1), jnp.concatenate([-s, s], axis=1)


def kernel(x, meta_tokens, pre_mix_norm, w_in, gla_gate_w2, gla_gate_b, ret_norm_w, gla_norm_w, w_out, post_mix_norm, pre_ffn_norm, ffn_up, ffn_conv_w, ffn_conv_b, ffn_down, post_ffn_norm, loss_target, m_meta_tokens, m_pre_mix_norm, m_w_in, m_gla_gate_w2, m_gla_gate_b, m_ret_norm_w, m_gla_norm_w, m_w_out, m_post_mix_norm, m_pre_ffn_norm, m_ffn_up, m_ffn_conv_w, m_ffn_conv_b, m_ffn_down, m_post_ffn_norm, v_meta_tokens, v_pre_mix_norm, v_w_in, v_gla_gate_w2, v_gla_gate_b, v_ret_norm_w, v_gla_norm_w, v_w_out, v_post_mix_norm, v_pre_ffn_norm, v_ffn_up, v_ffn_conv_w, v_ffn_conv_b, v_ffn_down, v_post_ffn_norm):
    weights = dict(meta_tokens=meta_tokens, pre_mix_norm=pre_mix_norm, w_in=w_in, gla_gate_w2=gla_gate_w2,
                   gla_gate_b=gla_gate_b, ret_norm_w=ret_norm_w, gla_norm_w=gla_norm_w, w_out=w_out,
                   post_mix_norm=post_mix_norm, pre_ffn_norm=pre_ffn_norm, ffn_up=ffn_up, ffn_conv_w=ffn_conv_w,
                   ffn_conv_b=ffn_conv_b, ffn_down=ffn_down, post_ffn_norm=post_ffn_norm)
    mom1 = dict(meta_tokens=m_meta_tokens, pre_mix_norm=m_pre_mix_norm, w_in=m_w_in, gla_gate_w2=m_gla_gate_w2,
                gla_gate_b=m_gla_gate_b, ret_norm_w=m_ret_norm_w, gla_norm_w=m_gla_norm_w, w_out=m_w_out,
                post_mix_norm=m_post_mix_norm, pre_ffn_norm=m_pre_ffn_norm, ffn_up=m_ffn_up,
                ffn_conv_w=m_ffn_conv_w, ffn_conv_b=m_ffn_conv_b, ffn_down=m_ffn_down, post_ffn_norm=m_post_ffn_norm)
    mom2 = dict(meta_tokens=v_meta_tokens, pre_mix_norm=v_pre_mix_norm, w_in=v_w_in, gla_gate_w2=v_gla_gate_w2,
                gla_gate_b=v_gla_gate_b, ret_norm_w=v_ret_norm_w, gla_norm_w=v_gla_norm_w, w_out=v_w_out,
                post_mix_norm=v_post_mix_norm, pre_ffn_norm=v_pre_ffn_norm, ffn_up=v_ffn_up,
                ffn_conv_w=v_ffn_conv_w, ffn_conv_b=v_ffn_conv_b, ffn_down=v_ffn_down, post_ffn_norm=v_post_ffn_norm)

    pad_cols = lambda a, width: jnp.pad(a, ((0, 0), (0, width - a.shape[1])))
    layer_w = []
    for l in range(DEPTH):
        shards = [pad_cols(w_in[l].astype(BF16), IN_SHARD_P), w_out[l].astype(BF16),
                  pad_cols(ffn_up[l].astype(BF16), UP_SHARD_P), ffn_down[l].astype(BF16)]
        in_s, out_s, up_s, down_s = _all_gather(shards, f"gather_weights_{l}")
        layer_w.append(dict(w_in=_shards_to_cols(in_s, _pieces_w_in(), IN_WP, f"w_in_cols_{l}"),
                            w_out=out_s.reshape(D, D),
                            w_up=_shards_to_cols(up_s, _pieces_ffn_up(), D_UP, f"ffn_up_cols_{l}"),
                            w_down=down_s.reshape(D_FF, D)))
    small = _all_gather([_pack([weights[n] for n, _, _ in SMALL], SMALL_ROWS, F32)], "gather_small_weights")[0]
    small_parts = _unpack_blocks(small, [s for _, s, _ in SMALL])
    full = {n: _unshard(p, ax) for (n, _, ax), p in zip(SMALL, small_parts)}
    w2p = jnp.pad(full["gla_gate_w2"], ((0, 0), (0, 128 - GATE_RANK), (0, 0)))
    cw8 = jnp.concatenate([_interleave_cols(full["ffn_conv_w"]), _interleave_cols(ffn_conv_b)[:, None, :],
                           jnp.zeros((DEPTH, 4, D_UP), F32)], axis=1)
    cos2, sin2 = _rope_tables()

    h = jnp.concatenate([jnp.zeros((PAD_ROWS, D), F32), full["meta_tokens"], x[0]], axis=0)
    target = jnp.concatenate([jnp.zeros((CHUNK, D), F32), loss_target[0]], axis=0)
    saved = []
    for l in range(DEPTH):
        lw = layer_w[l]
        a1 = _rmsnorm_fwd(h, pre_mix_norm[l:l + 1], f"pre_mix_norm_{l}")
        proj = _matmul(a1, lw["w_in"], out_dtype=F32, tm=TM, tn=1280, tk=D, name=f"in_proj_{l}")
        ocat, merged, sr_all, sg_all = _mixer_fwd(proj, cos2, sin2, w2p[l], gla_gate_b[l:l + 1],
                                                  ret_norm_w[l:l + 1], gla_norm_w[l:l + 1], f"mixer_fwd_{l}")
        m = _matmul(merged, lw["w_out"], out_dtype=F32, tm=TM, tn=D, tk=D, name=f"out_proj_{l}")
        h1 = _resid_norm(h, m, post_mix_norm[l:l + 1], f"post_mix_norm_{l}")
        a2 = _rmsnorm_fwd(h1, pre_ffn_norm[l:l + 1], f"pre_ffn_norm_{l}")
        u = _matmul(a2, lw["w_up"], out_dtype=BF16, tm=TM, tn=1408, tk=D, name=f"ffn_up_{l}")
        act = _conv_act_fwd(u, cw8[l], f"ffn_conv_act_{l}")
        f = _matmul(act, lw["w_down"], out_dtype=F32, tm=TM, tn=D, tk=D_FF, name=f"ffn_down_{l}")
        h2 = _resid_norm(h1, f, post_ffn_norm[l:l + 1], f"post_ffn_norm_{l}")
        saved.append(dict(h=h, a1=a1, proj=proj, ocat=ocat, merged=merged, sr=sr_all, sg=sg_all, m=m, h1=h1,
                          a2=a2, u=u, act=act, f=f))
        h = h2

    dh, loss_acc = _loss_head(h, target, "loss_head")
    loss = lax.psum(loss_acc[0, 0], ("x", "y", "c"))

    kinds = ("grad", "delta", "new_m", "new_v")
    big_names = [n for n, _, _ in BIG]
    grads = {n: [None] * DEPTH for n in WEIGHT_ORDER if n != "meta_tokens" and n not in big_names}
    big_out = {kind: {n: [None] * DEPTH for n in big_names} for kind in kinds}
    for l in reversed(range(DEPTH)):
        s, lw = saved[l], layer_w[l]
        df, g_post_ffn = _norm_bwd(dh, s["f"], post_ffn_norm[l:l + 1], None, BF16, f"post_ffn_norm_bwd_{l}")
        dact = _matmul(df, lw["w_down"], tb=True, out_dtype=BF16, tm=TM, tn=D_FF, tk=D, name=f"ffn_down_dx_{l}")
        g_down = _matmul(s["act"], df, ta=True, out_dtype=BF16, tm=D_FF // 2, tn=D, tk=TM, name=f"ffn_down_dw_{l}")
        du, dcw = _conv_act_bwd(dact, s["u"], cw8[l], f"ffn_conv_act_bwd_{l}")
        da2 = _matmul(du, lw["w_up"], tb=True, out_dtype=F32, tm=TM, tn=D, tk=1408, name=f"ffn_up_dx_{l}")
        g_up = _matmul(s["a2"], du, ta=True, out_dtype=BF16, tm=D, tn=1408, tk=TM, name=f"ffn_up_dw_{l}")
        dh1, g_pre_ffn = _norm_bwd(da2, s["h1"], pre_ffn_norm[l:l + 1], dh, F32, f"pre_ffn_norm_bwd_{l}")
        dm, g_post_mix = _norm_bwd(dh1, s["m"], post_mix_norm[l:l + 1], None, BF16, f"post_mix_norm_bwd_{l}")
        dmerged = _matmul(dm, lw["w_out"], tb=True, out_dtype=F32, tm=TM, tn=D, tk=D, name=f"out_proj_dx_{l}")
        g_out = _matmul(s["merged"], dm, ta=True, out_dtype=BF16, tm=D, tn=D, tk=TM, name=f"out_proj_dw_{l}")
        dproj, g_w2, g_gb, g_rn, g_gn = _mixer_bwd(s["proj"], s["ocat"], dmerged, s["sr"], s["sg"], cos2, sin2,
                                                   w2p[l], gla_gate_b[l:l + 1], ret_norm_w[l:l + 1],
                                                   gla_norm_w[l:l + 1], f"mixer_bwd_{l}")
        da1 = _matmul(dproj, lw["w_in"], tb=True, out_dtype=F32, tm=TM, tn=D, tk=1280, name=f"in_proj_dx_{l}")
        g_in = _matmul(s["a1"], dproj, ta=True, out_dtype=BF16, tm=D, tn=1280, tk=TM, name=f"in_proj_dw_{l}")
        dh, g_pre_mix = _norm_bwd(da1, s["h"], pre_mix_norm[l:l + 1], dh1, F32, f"pre_mix_norm_bwd_{l}")
        grads["post_ffn_norm"][l] = g_post_ffn[0]
        grads["ffn_conv_w"][l] = _deinterleave_cols(dcw[0:3])
        grads["ffn_conv_b"][l] = _deinterleave_cols(dcw[3])
        grads["pre_ffn_norm"][l] = g_pre_ffn[0]
        grads["post_mix_norm"][l] = g_post_mix[0]
        grads["gla_gate_w2"][l] = g_w2[:GATE_RANK]
        grads["gla_gate_b"][l] = g_gb[0]
        grads["ret_norm_w"][l] = g_rn[0]
        grads["gla_norm_w"][l] = g_gn[0]
        grads["pre_mix_norm"][l] = g_pre_mix[0]

        send = [_cols_to_shards(g_in, _pieces_w_in(), IN_SHARD_P, f"w_in_grad_shards_{l}"),
                g_out.reshape(N_DEV, D // N_DEV, D),
                _cols_to_shards(g_up, _pieces_ffn_up(), UP_SHARD_P, f"ffn_up_grad_shards_{l}"),
                g_down.reshape(N_DEV, D_FF // N_DEV, D)]
        parts = _exchange_blocks(send, f"exchange_grads_{l}")
        widths = (IN_SHARD_P, D, UP_SHARD_P, D)
        steps = (256, D // N_DEV, 256, D_FF // N_DEV // 2)
        for n, part, width, rows_per_step in zip(big_names, parts, widths, steps):
            shard = [pad_cols(d[n][l], width) for d in (weights, mom1, mom2)]
            results = _adamw(part, *shard, rows_per_step, f"adamw_{n}_{l}")
            for kind, r in zip(kinds, results):
                big_out[kind][n][l] = r[:, :weights[n].shape[2]]
    local = {n: jnp.stack(v) for n, v in grads.items()}
    local["meta_tokens"] = dh[PAD_ROWS:CHUNK]
    grad_x = dh[CHUNK:][None]
    out = {kind: {n: jnp.stack(v) for n, v in big_out[kind].items()} for kind in kinds}

    blocks = jnp.concatenate([_to_blocks(local[n], ax).reshape(N_DEV, -1) for n, _, ax in SMALL], axis=1)
    blocks = jnp.pad(blocks, ((0, 0), (0, SMALL_ROWS * LANES - blocks.shape[1]))).reshape(N_DEV, SMALL_ROWS, LANES)
    parts = _exchange_blocks([blocks], "exchange_small_grads")[0]
    shard_shapes = [s for _, s, _ in SMALL]
    packed = [_pack([d[n] for n, _, _ in SMALL], SMALL_ROWS, F32) for d in (weights, mom1, mom2)]
    results = _adamw(parts, *packed, SMALL_ROWS, "adamw_small_sharded")
    for kind, buf in zip(kinds, results):
        out[kind].update(zip([n for n, _, _ in SMALL], _unpack(buf, shard_shapes)))

    repl_parts = _all_gather([_pack([local[n] for n, _ in REPL], REPL_ROWS, F32)], "gather_small_grads")[0]
    packed = [_pack([d[n] for n, _ in REPL], REPL_ROWS, F32) for d in (weights, mom1, mom2)]
    results = _adamw(repl_parts, *packed, REPL_ROWS, "adamw_replicated")
    repl_shapes = [s for _, s in REPL]
    for kind, buf in zip(kinds, results):
        out[kind].update(zip([n for n, _ in REPL], _unpack(buf, repl_shapes)))

    return (loss, grad_x, *[out["grad"][n] for n in WEIGHT_ORDER], *[out["delta"][n] for n in WEIGHT_ORDER],
            *[out["new_m"][n] for n in WEIGHT_ORDER], *[out["new_v"][n] for n in WEIGHT_ORDER])


def _unpack_blocks(gathered, shapes):
    flat = gathered.reshape(N_DEV, -1)
    out, off = [], 0
    for shape in shapes:
        out.append(flat[:, off:off + _size(shape)].reshape((N_DEV,) + shape))
        off += _size(shape)
    return out
```

```python
import math

import jax
import jax.numpy as jnp
from jax import lax
from jax.experimental import pallas as pl
from jax.experimental.pallas import tpu as pltpu

F32 = jnp.float32
BF16 = jnp.bfloat16

D = 1024
SEQ = 8192
DEPTH = 2
N_META = 16
CHUNK = 64
SUB = 16
N_SUB = CHUNK // SUB
PAD_ROWS = CHUNK - N_META
LP = SEQ + CHUNK
N_CHUNKS = LP // CHUNK
RET_HEADS = 4
RET_DK = 128
GLA_HEADS = 4
GLA_DK = 64
GLA_DV = 128
GLA_TAU = 16.0
GATE_RANK = 16
IN_W = 3600
IN_WP = 3840
D_FF = 2816
D_UP = 2 * D_FF
CONV_BLOCK = 256
N_CONV_BLOCKS = D_FF // CONV_BLOCK
ROPE_BASE = 10000.0
EPS = 1e-6
N_DEV = 8
LANES = 1024

O_RQ, O_RK, O_RV, O_RG = 0, 512, 1024, 1536
O_GQ, O_GK, O_GV, O_GR, O_GA = 2048, 2304, 2560, 3072, 3584

ADAM_LR = 0.001
ADAM_B1 = 0.9
ADAM_B2 = 0.999
ADAM_EPS = 1e-08
ADAM_WD = 0.01
ADAM_STEP = 10

VMEM_LIMIT = 56 * 1024 * 1024
MESH_IDS = pl.DeviceIdType.MESH


def _row_tile(rows):
    best = 16
    for t in range(16, min(rows, 688) + 1, 16):
        if rows % t == 0:
            best = t
    return best


TM = _row_tile(LP)


def _cparams(*sem):
    return pltpu.CompilerParams(dimension_semantics=sem, vmem_limit_bytes=VMEM_LIMIT)


def _dot(a, b):
    return jnp.dot(a.astype(BF16), b.astype(BF16), preferred_element_type=F32)


def _dot_nt(a, b):
    return lax.dot_general(a.astype(BF16), b.astype(BF16), (((1,), (1,)), ((), ())), preferred_element_type=F32)


def _dot_tn(a, b):
    return lax.dot_general(a.astype(BF16), b.astype(BF16), (((0,), (0,)), ((), ())), preferred_element_type=F32)


def _split3(x):
    hi = x.astype(BF16)
    r1 = x - hi.astype(F32)
    mid = r1.astype(BF16)
    lo = (r1 - mid.astype(F32)).astype(BF16)
    return hi, mid, lo


def _dot_exact_rhs(t, x):
    hi, mid, lo = _split3(x)
    t = t.astype(BF16)
    return (jnp.dot(t, hi, preferred_element_type=F32) + jnp.dot(t, mid, preferred_element_type=F32)
            + jnp.dot(t, lo, preferred_element_type=F32))


def _dot_tn_exact_lhs(x, ones):
    dims = (((0,), (0,)), ((), ()))
    hi, mid, lo = _split3(x)
    ones = ones.astype(BF16)
    return (lax.dot_general(hi, ones, dims, preferred_element_type=F32)
            + lax.dot_general(mid, ones, dims, preferred_element_type=F32)
            + lax.dot_general(lo, ones, dims, preferred_element_type=F32))


def _sigmoid(x):
    return 1.0 / (1.0 + jnp.exp(-x))


def _matmul(a, b, *, ta=False, tb=False, out_dtype, tm, tn, tk, name):
    m = a.shape[1] if ta else a.shape[0]
    k = a.shape[0] if ta else a.shape[1]
    n = b.shape[0] if tb else b.shape[1]
    assert (b.shape[1] if tb else b.shape[0]) == k
    assert m % tm == 0 and n % tn == 0 and k % tk == 0, (name, m, n, k, tm, tn, tk)
    nk = k // tk
    a_spec = pl.BlockSpec((tk, tm), lambda i, j, kk: (kk, i)) if ta else pl.BlockSpec((tm, tk), lambda i, j, kk: (i, kk))
    b_spec = pl.BlockSpec((tn, tk), lambda i, j, kk: (j, kk)) if tb else pl.BlockSpec((tk, tn), lambda i, j, kk: (kk, j))
    dims = (((0 if ta else 1,), (1 if tb else 0,)), ((), ()))

    def body(a_ref, b_ref, o_ref, acc_ref):
        kk = pl.program_id(2)

        @pl.when(kk == 0)
        def _():
            acc_ref[...] = jnp.zeros_like(acc_ref)

        acc_ref[...] += lax.dot_general(a_ref[...].astype(BF16), b_ref[...].astype(BF16), dims,
                                        preferred_element_type=F32)

        @pl.when(kk == nk - 1)
        def _():
            o_ref[...] = acc_ref[...].astype(out_dtype)

    return pl.pallas_call(
        body, name=name, grid=(m // tm, n // tn, nk),
        in_specs=[a_spec, b_spec],
        out_specs=pl.BlockSpec((tm, tn), lambda i, j, kk: (i, j)),
        out_shape=jax.ShapeDtypeStruct((m, n), out_dtype),
        scratch_shapes=[pltpu.VMEM((tm, tn), F32)],
        compiler_params=_cparams("parallel", "parallel", "arbitrary"),
    )(a, b)


def _rmsnorm_fwd(x, w, name):
    def body(x_ref, w_ref, o_ref):
        xv = x_ref[...]
        r = lax.rsqrt(jnp.mean(xv * xv, axis=-1, keepdims=True) + EPS)
        o_ref[...] = (xv * r * w_ref[...]).astype(BF16)

    return pl.pallas_call(
        body, name=name, grid=(LP // TM,),
        in_specs=[pl.BlockSpec((TM, D), lambda i: (i, 0)), pl.BlockSpec((1, D), lambda i: (0, 0))],
        out_specs=pl.BlockSpec((TM, D), lambda i: (i, 0)),
        out_shape=jax.ShapeDtypeStruct((LP, D), BF16),
        compiler_params=_cparams("parallel"),
    )(x, w)


def _resid_norm(h, m, w, name):
    def body(h_ref, m_ref, w_ref, o_ref):
        mv = m_ref[...]
        r = lax.rsqrt(jnp.mean(mv * mv, axis=-1, keepdims=True) + EPS)
        row = pl.program_id(0) * TM + lax.broadcasted_iota(jnp.int32, (TM, 1), 0)
        o_ref[...] = h_ref[...] + jnp.where(row >= PAD_ROWS, mv * r * w_ref[...], 0.0)

    return pl.pallas_call(
        body, name=name, grid=(LP // TM,),
        in_specs=[pl.BlockSpec((TM, D), lambda i: (i, 0)), pl.BlockSpec((TM, D), lambda i: (i, 0)),
                  pl.BlockSpec((1, D), lambda i: (0, 0))],
        out_specs=pl.BlockSpec((TM, D), lambda i: (i, 0)),
        out_shape=jax.ShapeDtypeStruct((LP, D), F32),
        compiler_params=_cparams("parallel"),
    )(h, m, w)


def _norm_bwd(dy, x, w, resid, out_dtype, name):
    has_resid = resid is not None

    def body(*refs):
        if has_resid:
            dy_ref, x_ref, w_ref, r_ref, dx_ref, dw_ref = refs
        else:
            dy_ref, x_ref, w_ref, dx_ref, dw_ref = refs
        i = pl.program_id(0)

        @pl.when(i == 0)
        def _():
            dw_ref[...] = jnp.zeros_like(dw_ref)

        row = i * TM + lax.broadcasted_iota(jnp.int32, (TM, 1), 0)
        dyv = jnp.where(row >= PAD_ROWS, dy_ref[...], 0.0)
        xv = x_ref[...]
        r = lax.rsqrt(jnp.mean(xv * xv, axis=-1, keepdims=True) + EPS)
        g = dyv * w_ref[...]
        dx = r * g - xv * (r * r * r * jnp.mean(g * xv, axis=-1, keepdims=True))
        if has_resid:
            dx = dx + r_ref[...]
        dx_ref[...] = dx.astype(out_dtype)
        dw_ref[0:1, :] += jnp.sum(dyv * xv * r, axis=0, keepdims=True)

    tile = pl.BlockSpec((TM, D), lambda i: (i, 0))
    in_specs = [tile, tile, pl.BlockSpec((1, D), lambda i: (0, 0))] + ([tile] if has_resid else [])
    args = (dy, x, w) + ((resid,) if has_resid else ())
    return pl.pallas_call(
        body, name=name, grid=(LP // TM,),
        in_specs=in_specs,
        out_specs=[tile, pl.BlockSpec((8, D), lambda i: (0, 0))],
        out_shape=[jax.ShapeDtypeStruct((LP, D), out_dtype), jax.ShapeDtypeStruct((8, D), F32)],
        compiler_params=_cparams("arbitrary"),
    )(*args)


def _loss_head(y, target, name):
    def body(y_ref, t_ref, dy_ref, loss_ref):
        i = pl.program_id(0)

        @pl.when(i == 0)
        def _():
            loss_ref[...] = jnp.zeros_like(loss_ref)

        row = i * TM + lax.broadcasted_iota(jnp.int32, (TM, 1), 0)
        diff = jnp.where(row >= CHUNK, y_ref[...] - t_ref[...], 0.0)
        dy_ref[...] = diff * (1.0 / D)
        loss_ref[...] += (0.5 / D) * jnp.sum(diff * diff)

    tile = pl.BlockSpec((TM, D), lambda i: (i, 0))
    return pl.pallas_call(
        body, name=name, grid=(LP // TM,),
        in_specs=[tile, tile],
        out_specs=[tile, pl.BlockSpec((8, 128), lambda i: (0, 0))],
        out_shape=[jax.ShapeDtypeStruct((LP, D), F32), jax.ShapeDtypeStruct((8, 128), F32)],
        compiler_params=_cparams("arbitrary"),
    )(y, target)


GELU_C = math.sqrt(2.0 / math.pi)
GELU_K = 0.044715


def _shift_down(x, prev8, rows):
    row = lax.broadcasted_iota(jnp.int32, (rows, 1), 0)
    p1 = pltpu.roll(prev8, 1, 0)
    p2 = pltpu.roll(prev8, 2, 0)
    x1 = jnp.where(row == 0, p1[0:1, :], pltpu.roll(x, 1, 0))
    x2 = jnp.where(row == 0, p2[0:1, :], jnp.where(row == 1, p2[1:2, :], pltpu.roll(x, 2, 0)))
    return x1, x2


def _conv_act_fwd(u, cw8, name):
    n_rows = LP // TM
    cb2 = 2 * CONV_BLOCK

    def body(u_ref, cw_ref, act_ref, carry_ref):
        i = pl.program_id(1)

        @pl.when(i == 0)
        def _():
            carry_ref[...] = jnp.zeros_like(carry_ref)

        x = u_ref[...].astype(F32)
        x1, x2 = _shift_down(x, carry_ref[...], TM)
        conv = cw_ref[3:4, :] + x2 * cw_ref[0:1, :] + x1 * cw_ref[1:2, :] + x * cw_ref[2:3, :]
        a = conv[:, :CONV_BLOCK]
        g = conv[:, CONV_BLOCK:]
        t = jnp.tanh(GELU_C * (a + GELU_K * a * a * a))
        act_ref[...] = (0.5 * a * (1.0 + t) * g).astype(BF16)
        carry_ref[...] = x[TM - 8:TM, :]

    return pl.pallas_call(
        body, name=name, grid=(N_CONV_BLOCKS, n_rows),
        in_specs=[pl.BlockSpec((TM, cb2), lambda j, i: (i, j)), pl.BlockSpec((8, cb2), lambda j, i: (0, j))],
        out_specs=pl.BlockSpec((TM, CONV_BLOCK), lambda j, i: (i, j)),
        out_shape=jax.ShapeDtypeStruct((LP, D_FF), BF16),
        scratch_shapes=[pltpu.VMEM((8, cb2), F32)],
        compiler_params=_cparams("arbitrary", "arbitrary"),
    )(u, cw8)


def _conv_act_bwd(dact, u, cw8, name):
    n_rows = LP // TM
    cb2 = 2 * CONV_BLOCK
    halo_per_tile = TM // 16

    def body(dact_ref, u_ref, uh_ref, cw_ref, du_ref, dcw_ref, carry_ref):
        i = pl.program_id(1)
        tile = n_rows - 1 - i

        @pl.when(i == 0)
        def _():
            dcw_ref[...] = jnp.zeros_like(dcw_ref)
            carry_ref[...] = jnp.zeros_like(carry_ref)

        x = u_ref[...].astype(F32)
        prev8 = jnp.where(tile == 0, 0.0, uh_ref[8:16, :].astype(F32))
        x1, x2 = _shift_down(x, prev8, TM)
        w0, w1, w2 = cw_ref[0:1, :], cw_ref[1:2, :], cw_ref[2:3, :]
        conv = cw_ref[3:4, :] + x2 * w0 + x1 * w1 + x * w2
        a = conv[:, :CONV_BLOCK]
        g = conv[:, CONV_BLOCK:]
        t = jnp.tanh(GELU_C * (a + GELU_K * a * a * a))
        gel = 0.5 * a * (1.0 + t)
        dgel = 0.5 * (1.0 + t) + 0.5 * a * (1.0 - t * t) * (GELU_C * (1.0 + 3.0 * GELU_K * a * a))
        dav = dact_ref[...].astype(F32)
        dconv = jnp.concatenate([dav * g * dgel, dav * gel], axis=1)
        dcw_ref[0:1, :] += jnp.sum(dconv * x2, axis=0, keepdims=True)
        dcw_ref[1:2, :] += jnp.sum(dconv * x1, axis=0, keepdims=True)
        dcw_ref[2:3, :] += jnp.sum(dconv * x, axis=0, keepdims=True)
        dcw_ref[3:4, :] += jnp.sum(dconv, axis=0, keepdims=True)
        nxt = carry_ref[...]
        row = lax.broadcasted_iota(jnp.int32, (TM, 1), 0)
        d1 = jnp.where(row == TM - 1, nxt[0:1, :], pltpu.roll(dconv, TM - 1, 0))
        d2 = jnp.where(row == TM - 2, nxt[0:1, :], jnp.where(row == TM - 1, nxt[1:2, :], pltpu.roll(dconv, TM - 2, 0)))
        du_ref[...] = (dconv * w2 + d1 * w1 + d2 * w0).astype(BF16)
        carry_ref[...] = dconv[0:8, :]

    return pl.pallas_call(
        body, name=name, grid=(N_CONV_BLOCKS, n_rows),
        in_specs=[pl.BlockSpec((TM, CONV_BLOCK), lambda j, i: (n_rows - 1 - i, j)),
                  pl.BlockSpec((TM, cb2), lambda j, i: (n_rows - 1 - i, j)),
                  pl.BlockSpec((16, cb2), lambda j, i: (jnp.maximum((n_rows - 1 - i) * halo_per_tile - 1, 0), j)),
                  pl.BlockSpec((8, cb2), lambda j, i: (0, j))],
        out_specs=[pl.BlockSpec((TM, cb2), lambda j, i: (n_rows - 1 - i, j)),
                   pl.BlockSpec((8, cb2), lambda j, i: (0, j))],
        out_shape=[jax.ShapeDtypeStruct((LP, D_UP), BF16), jax.ShapeDtypeStruct((8, D_UP), F32)],
        scratch_shapes=[pltpu.VMEM((8, cb2), F32)],
        compiler_params=_cparams("arbitrary", "arbitrary"),
    )(dact, u, u, cw8)


def _ret_consts(h):
    lg = math.log(1.0 - 2.0 ** (-5.0 - h))
    ri = lax.broadcasted_iota(jnp.int32, (CHUNK, CHUNK), 0)
    ci = lax.broadcasted_iota(jnp.int32, (CHUNK, CHUNK), 1)
    diff = (ri - ci).astype(F32)
    dmat = jnp.where(diff >= 0, jnp.exp(lg * jnp.maximum(diff, 0.0)), 0.0)
    rowf = lax.broadcasted_iota(jnp.int32, (CHUNK, 1), 0).astype(F32)
    zeta = jnp.exp(lg * (CHUNK - 1.0 - rowf))
    xi = jnp.exp(lg * (rowf + 1.0))
    return dmat, zeta, xi, math.exp(lg * CHUNK)


def _rope(t, cosv, sinv):
    return t * cosv + pltpu.roll(t, RET_DK // 2, 1) * sinv


def _unrope(d, cosv, sinv):
    return d * cosv + pltpu.roll(d * sinv, RET_DK // 2, 1)


def _gla_common(p_ref, w2_ref, gb_ref, chunk):
    row = lax.broadcasted_iota(jnp.int32, (CHUNK, 1), 0)
    real = (chunk * CHUNK + row) >= PAD_ROWS
    ga = p_ref[:, O_GA:O_GA + 128]
    z = _dot(ga, w2_ref[...]) + gb_ref[...]
    la = (jnp.minimum(z, 0.0) - jnp.log(1.0 + jnp.exp(-jnp.abs(z)))) * (1.0 / GLA_TAU)
    la = jnp.where(real, la, 0.0)
    ri = lax.broadcasted_iota(jnp.int32, (CHUNK, CHUNK), 0)
    ci = lax.broadcasted_iota(jnp.int32, (CHUNK, CHUNK), 1)
    tril = (ri >= ci).astype(F32)
    cum = _dot_exact_rhs(tril, la)
    last = cum[CHUNK - 1:CHUNK, :]
    qs = p_ref[:, O_GQ:O_GQ + 256] * (GLA_DK ** -0.5)
    k = p_ref[:, O_GK:O_GK + 256]
    ecum = jnp.exp(cum)
    ekl = jnp.exp(last - cum)
    el = jnp.exp(last)
    refs = [jnp.zeros((1, 256), F32)] + [cum[a * SUB - 1:a * SUB, :] for a in range(1, N_SUB)]
    eq = [jnp.exp(cum[a * SUB:(a + 1) * SUB, :] - refs[a]) for a in range(N_SUB)]
    spread = refs[0] - cum[SUB - 1:SUB, :]
    for a in range(1, N_SUB):
        spread = jnp.maximum(spread, refs[a] - cum[(a + 1) * SUB - 1:(a + 1) * SUB, :])
    small = jnp.max(spread) <= GLA_FACTORED_MAX
    return dict(real=real, row=row, z=z, la=la, cum=cum, last=last, qs=qs, k=k, ecum=ecum, ekl=ekl, el=el,
                refs=refs, eq=eq, small=small, ri=ri, ci=ci)


GLA_FACTORED_MAX = 40.0


def _gla_factored_keys(c):
    return [c["k"] * jnp.exp(jnp.minimum(c["refs"][a] - c["cum"], GLA_FACTORED_MAX)) for a in range(N_SUB)]


def _causal_rows(c, a):
    return c["ci"][:SUB, :] <= c["ri"][:SUB, :] + a * SUB


def _gla_scores_factored(c, keys, h):
    sl = slice(GLA_DK * h, GLA_DK * (h + 1))
    blocks = []
    for a in range(N_SUB):
        qh = c["qs"][a * SUB:(a + 1) * SUB, sl] * c["eq"][a][:, sl]
        blocks.append(jnp.where(_causal_rows(c, a), _dot_nt(qh, keys[a][:, sl]), 0.0))
    return jnp.concatenate(blocks, axis=0)


def _gla_lag_weights(c):
    cum, row = c["cum"], c["row"]
    out = [jnp.ones((CHUNK, 256), F32)]
    for r in range(1, SUB):
        out.append(jnp.where((row % SUB) >= r, jnp.exp(jnp.minimum(cum - pltpu.roll(cum, r, 0), 0.0)), 0.0))
    return out


def _gla_pairwise_keys(c):
    return [None] + [c["k"] * jnp.exp(jnp.minimum(c["refs"][a] - c["cum"], 0.0)) for a in range(1, N_SUB)]


def _gla_scores_pairwise(c, lag_w, keys, h):
    sl = slice(GLA_DK * h, GLA_DK * (h + 1))
    qs, k = c["qs"][:, sl], c["k"][:, sl]
    ri, ci = c["ri"], c["ci"]
    p = jnp.zeros((CHUNK, CHUNK), F32)
    for r in range(SUB):
        kr = k if r == 0 else pltpu.roll(k, r, 0)
        pr = jnp.sum(qs * kr * lag_w[r][:, sl], axis=1, keepdims=True)
        p = p + jnp.where(ci == ri - r, pr, 0.0)
    blocks = [jnp.zeros((SUB, CHUNK), F32)]
    for a in range(1, N_SUB):
        qh = qs[a * SUB:(a + 1) * SUB, :] * c["eq"][a][:, sl]
        blocks.append(jnp.where(ci[:SUB, :] < a * SUB, _dot_nt(qh, keys[a][:, sl]), 0.0))
    return p + jnp.concatenate(blocks, axis=0)


def _gla_all_scores(c, p_scr):
    @pl.when(c["small"])
    def _():
        keys = _gla_factored_keys(c)
        for h in range(GLA_HEADS):
            p_scr[h] = _gla_scores_factored(c, keys, h)

    @pl.when(jnp.logical_not(c["small"]))
    def _():
        lag_w, keys = _gla_lag_weights(c), _gla_pairwise_keys(c)
        for h in range(GLA_HEADS):
            p_scr[h] = _gla_scores_pairwise(c, lag_w, keys, h)


def _gla_intra_bwd_factored(c, keys, dp, h):
    sl = slice(GLA_DK * h, GLA_DK * (h + 1))
    dq_rows = []
    dk = jnp.zeros((CHUNK, GLA_DK), F32)
    for a in range(N_SUB):
        eq = c["eq"][a][:, sl]
        qh = c["qs"][a * SUB:(a + 1) * SUB, sl] * eq
        dpa = jnp.where(_causal_rows(c, a), dp[a * SUB:(a + 1) * SUB, :], 0.0)
        dq_rows.append(_dot(dpa, keys[a][:, sl]) * eq)
        ek = jnp.exp(jnp.minimum(c["refs"][a][:, sl] - c["cum"][:, sl], GLA_FACTORED_MAX))
        dk = dk + _dot_tn(dpa, qh) * ek
    return jnp.concatenate(dq_rows, axis=0), dk


def _gla_intra_bwd_pairwise(c, lag_w, keys, dp, h):
    sl = slice(GLA_DK * h, GLA_DK * (h + 1))
    qs_h, k_h = c["qs"][:, sl], c["k"][:, sl]
    ri, ci = c["ri"], c["ci"]
    dq_rows = [jnp.zeros((SUB, GLA_DK), F32)]
    dk = jnp.zeros((CHUNK, GLA_DK), F32)
    for a in range(1, N_SUB):
        eq = c["eq"][a][:, sl]
        qh = qs_h[a * SUB:(a + 1) * SUB, :] * eq
        dpa = jnp.where(ci[:SUB, :] < a * SUB, dp[a * SUB:(a + 1) * SUB, :], 0.0)
        dq_rows.append(_dot(dpa, keys[a][:, sl]) * eq)
        ek = jnp.exp(jnp.minimum(c["refs"][a][:, sl] - c["cum"][:, sl], 0.0))
        dk = dk + _dot_tn(dpa, qh) * ek
    dq = jnp.concatenate(dq_rows, axis=0)
    for r in range(SUB):
        w = lag_w[r][:, sl]
        dpr = jnp.sum(jnp.where(ci == ri - r, dp, 0.0), axis=1, keepdims=True)
        kr = k_h if r == 0 else pltpu.roll(k_h, r, 0)
        dq = dq + dpr * kr * w
        back = dpr * qs_h * w
        dk = dk + (back if r == 0 else pltpu.roll(back, CHUNK - r, 0))
    return dq, dk


def _gla_all_intra_bwd(c, dps, p_scr, dq_scr, dk_scr):
    @pl.when(c["small"])
    def _():
        keys = _gla_factored_keys(c)
        outs = [_gla_intra_bwd_factored(c, keys, dps[h], h) for h in range(GLA_HEADS)]
        for h in range(GLA_HEADS):
            p_scr[h] = _gla_scores_factored(c, keys, h)
        dq_scr[...] = jnp.concatenate([o[0] for o in outs], axis=1)
        dk_scr[...] = jnp.concatenate([o[1] for o in outs], axis=1)

    @pl.when(jnp.logical_not(c["small"]))
    def _():
        lag_w, keys = _gla_lag_weights(c), _gla_pairwise_keys(c)
        outs = [_gla_intra_bwd_pairwise(c, lag_w, keys, dps[h], h) for h in range(GLA_HEADS)]
        for h in range(GLA_HEADS):
            p_scr[h] = _gla_scores_pairwise(c, lag_w, keys, h)
        dq_scr[...] = jnp.concatenate([o[0] for o in outs], axis=1)
        dk_scr[...] = jnp.concatenate([o[1] for o in outs], axis=1)


def _mixer_fwd(proj, cos2, sin2, w2p, gb, rnw, gnw, name, carried=()):
    n_carried = len(carried)

    def body(*refs):
        p_ref, c_ref, s_ref, w2_ref, gb_ref, rnw_ref, gnw_ref = refs[:7]
        x_refs, refs = refs[7:7 + n_carried], refs[7 + n_carried:]
        ocat_ref, mrg_ref, sr_out, sg_out = refs[:4]
        gathered_refs, refs = refs[4:4 + n_carried], refs[4 + n_carried:]
        sr, sg, p_scr = refs[:3]
        n = pl.program_id(0)
        if n_carried:
            start, forward, finish = _gather_phases(x_refs, gathered_refs, *refs[3:])
            pl.when(n == 0)(start)
            pl.when(n == N_CHUNKS // 2)(forward)

        @pl.when(n == 0)
        def _():
            sr[...] = jnp.zeros_like(sr)
            sg[...] = jnp.zeros_like(sg)

        sr_out[0] = sr[...]
        sg_out[0] = sg[...]
        cosv, sinv = c_ref[...], s_ref[...]

        for h in range(RET_HEADS):
            dmat, zeta, xi, gc = _ret_consts(h)
            hs = slice(128 * h, 128 * (h + 1))
            q = _rope(p_ref[:, O_RQ + 128 * h:O_RQ + 128 * (h + 1)], cosv, sinv)
            k = _rope(p_ref[:, O_RK + 128 * h:O_RK + 128 * (h + 1)], cosv, sinv) * (RET_DK ** -0.5)
            v = p_ref[:, O_RV + 128 * h:O_RV + 128 * (h + 1)]
            g = p_ref[:, O_RG + 128 * h:O_RG + 128 * (h + 1)]
            s_in = sr[h]
            a = _dot_nt(q, k) * dmat
            o = _dot(a, v) + _dot(q, s_in) * xi
            sr[h] = gc * s_in + _dot_tn(k * zeta, v)
            mu = jnp.mean(o, axis=-1, keepdims=True)
            xc = o - mu
            nrm = xc * lax.rsqrt(jnp.mean(xc * xc, axis=-1, keepdims=True) + EPS)
            ocat_ref[:, hs] = o
            mrg_ref[:, hs] = (nrm * rnw_ref[:, hs] * (g * _sigmoid(g))).astype(BF16)

        c = _gla_common(p_ref, w2_ref, gb_ref, n)
        _gla_all_scores(c, p_scr)
        lastcol = _dot_tn_exact_lhs(c["la"], jnp.ones((CHUNK, GLA_DV), F32))
        qe = c["qs"] * c["ecum"]
        kl = c["k"] * c["ekl"]
        for h in range(GLA_HEADS):
            sl = slice(GLA_DK * h, GLA_DK * (h + 1))
            hs = slice(512 + 128 * h, 512 + 128 * (h + 1))
            v = p_ref[:, O_GV + 128 * h:O_GV + 128 * (h + 1)]
            g = p_ref[:, O_GR + 128 * h:O_GR + 128 * (h + 1)]
            s_in = sg[h]
            o = _dot(p_scr[h], v) + _dot(qe[:, sl], s_in)
            sg[h] = jnp.exp(lastcol[GLA_DK * h:GLA_DK * (h + 1), :]) * s_in + _dot_tn(kl[:, sl], v)
            nrm = o * lax.rsqrt(jnp.mean(o * o, axis=-1, keepdims=True) + EPS)
            ocat_ref[:, hs] = o
            mrg_ref[:, hs] = (nrm * gnw_ref[:, 128 * h:128 * (h + 1)] * (g * _sigmoid(g))).astype(BF16)

        if n_carried:
            pl.when(n == N_CHUNKS - 1)(finish)

    const = lambda shape: pl.BlockSpec(shape, lambda n: (0,) * len(shape))
    anywhere = [pl.BlockSpec(memory_space=pl.ANY)] * n_carried
    return pl.pallas_call(
        body, name=name, grid=(N_CHUNKS,),
        in_specs=[pl.BlockSpec((CHUNK, IN_WP), lambda n: (n, 0)),
                  pl.BlockSpec((CHUNK, 128), lambda n: (n, 0)), pl.BlockSpec((CHUNK, 128), lambda n: (n, 0)),
                  const((128, 256)), const((1, 256)), const((1, 512)), const((1, 512))] + anywhere,
        out_specs=[pl.BlockSpec((CHUNK, D), lambda n: (n, 0)), pl.BlockSpec((CHUNK, D), lambda n: (n, 0)),
                   pl.BlockSpec((1, RET_HEADS, RET_DK, 128), lambda n: (n, 0, 0, 0)),
                   pl.BlockSpec((1, GLA_HEADS, GLA_DK, GLA_DV), lambda n: (n, 0, 0, 0))] + anywhere,
        out_shape=[jax.ShapeDtypeStruct((LP, D), F32), jax.ShapeDtypeStruct((LP, D), BF16),
                   jax.ShapeDtypeStruct((N_CHUNKS, RET_HEADS, RET_DK, 128), F32),
                   jax.ShapeDtypeStruct((N_CHUNKS, GLA_HEADS, GLA_DK, GLA_DV), F32)] + _gathered_shapes(carried),
        scratch_shapes=[pltpu.VMEM((RET_HEADS, RET_DK, 128), F32), pltpu.VMEM((GLA_HEADS, GLA_DK, GLA_DV), F32),
                        pltpu.VMEM((GLA_HEADS, CHUNK, CHUNK), F32)] + _exchange_sems(n_carried),
        compiler_params=_cparams("arbitrary"),
    )(proj, cos2, sin2, w2p, gb, rnw, gnw, *carried)


def _mixer_bwd(proj, ocat, dmrg, sr_all, sg_all, cos2, sin2, w2p, gb, rnw, gnw, name, carried=()):
    last_chunk = N_CHUNKS - 1
    n_carried = len(carried)

    def body(*refs):
        p_ref, ocat_ref, dm_ref, sr_ref, sg_ref, c_ref, s_ref, w2_ref, gb_ref, rnw_ref, gnw_ref = refs[:11]
        g_refs, refs = refs[11:11 + n_carried], refs[11 + n_carried:]
        dp_ref, dw2_ref, dgb_ref, drn_ref, dgn_ref = refs[:5]
        got_refs, refs = refs[5:5 + n_carried], refs[5 + n_carried:]
        dsr, dsg, p_scr, dq_scr, dk_scr = refs[:5]
        step = pl.program_id(0)
        n = last_chunk - step
        if n_carried:
            start, finish = _exchange_phases(g_refs, got_refs, *refs[5:])
            pl.when(step == 0)(start)

        @pl.when(step == 0)
        def _():
            dsr[...] = jnp.zeros_like(dsr)
            dsg[...] = jnp.zeros_like(dsg)
            dw2_ref[...] = jnp.zeros_like(dw2_ref)
            dgb_ref[...] = jnp.zeros_like(dgb_ref)
            drn_ref[...] = jnp.zeros_like(drn_ref)
            dgn_ref[...] = jnp.zeros_like(dgn_ref)

        cosv, sinv = c_ref[...], s_ref[...]
        row = lax.broadcasted_iota(jnp.int32, (CHUNK, 1), 0)
        real = ((n * CHUNK + row) >= PAD_ROWS).astype(F32)

        for h in range(RET_HEADS):
            dmat, zeta, xi, gc = _ret_consts(h)
            hs = slice(128 * h, 128 * (h + 1))
            q = _rope(p_ref[:, O_RQ + 128 * h:O_RQ + 128 * (h + 1)], cosv, sinv)
            k = _rope(p_ref[:, O_RK + 128 * h:O_RK + 128 * (h + 1)], cosv, sinv) * (RET_DK ** -0.5)
            v = p_ref[:, O_RV + 128 * h:O_RV + 128 * (h + 1)]
            g = p_ref[:, O_RG + 128 * h:O_RG + 128 * (h + 1)]
            o = ocat_ref[:, hs]
            dy = dm_ref[:, hs]
            wv = rnw_ref[:, hs]
            mu = jnp.mean(o, axis=-1, keepdims=True)
            xc = o - mu
            rs = lax.rsqrt(jnp.mean(xc * xc, axis=-1, keepdims=True) + EPS)
            nrm = xc * rs
            sgm = _sigmoid(g)
            sil = g * sgm
            drn_ref[0:1, hs] += jnp.sum(dy * nrm * sil, axis=0, keepdims=True)
            dgate = dy * nrm * wv * (sgm * (1.0 + g * (1.0 - sgm)))
            dn = dy * wv * sil
            do = rs * (dn - jnp.mean(dn, axis=-1, keepdims=True) - nrm * jnp.mean(dn * nrm, axis=-1, keepdims=True))
            s_in = sr_ref[0, h]
            ds_out = dsr[h]
            a = _dot_nt(q, k) * dmat
            da = _dot_nt(do, v) * dmat
            dox = do * xi
            dq = _dot(da, k) + _dot_nt(dox, s_in)
            dk = _dot_tn(da, q) + _dot_nt(v, ds_out) * zeta
            dv = _dot_tn(a, do) + _dot(k * zeta, ds_out)
            dsr[h] = gc * ds_out + _dot_tn(q, dox)
            dk = dk * (RET_DK ** -0.5)
            dp_ref[:, O_RQ + 128 * h:O_RQ + 128 * (h + 1)] = (_unrope(dq, cosv, sinv) * real).astype(BF16)
            dp_ref[:, O_RK + 128 * h:O_RK + 128 * (h + 1)] = (_unrope(dk, cosv, sinv) * real).astype(BF16)
            dp_ref[:, O_RV + 128 * h:O_RV + 128 * (h + 1)] = (dv * real).astype(BF16)
            dp_ref[:, O_RG + 128 * h:O_RG + 128 * (h + 1)] = (dgate * real).astype(BF16)

        c = _gla_common(p_ref, w2_ref, gb_ref, n)
        ri, ci = c["ri"], c["ci"]
        causal = ri >= ci
        triu = (ci >= ri).astype(F32)
        qe = c["qs"] * c["ecum"]
        kl = c["k"] * c["ekl"]
        lastcol = _dot_tn_exact_lhs(c["la"], jnp.ones((CHUNK, GLA_DV), F32))
        dla_heads, dq_heads, dk_heads = [], [], []
        dos, dps = [], []
        for h in range(GLA_HEADS):
            hs = slice(512 + 128 * h, 512 + 128 * (h + 1))
            v = p_ref[:, O_GV + 128 * h:O_GV + 128 * (h + 1)]
            g = p_ref[:, O_GR + 128 * h:O_GR + 128 * (h + 1)]
            o = ocat_ref[:, hs]
            dy = dm_ref[:, hs]
            wv = gnw_ref[:, 128 * h:128 * (h + 1)]
            rs = lax.rsqrt(jnp.mean(o * o, axis=-1, keepdims=True) + EPS)
            nrm = o * rs
            sgm = _sigmoid(g)
            sil = g * sgm
            dgn_ref[0:1, 128 * h:128 * (h + 1)] += jnp.sum(dy * nrm * sil, axis=0, keepdims=True)
            dgate = dy * nrm * wv * (sgm * (1.0 + g * (1.0 - sgm)))
            dn = dy * wv * sil
            do = rs * (dn - nrm * jnp.mean(dn * nrm, axis=-1, keepdims=True))
            dp_ref[:, O_GR + 128 * h:O_GR + 128 * (h + 1)] = (dgate * real).astype(BF16)
            dos.append(do)
            dps.append(jnp.where(causal, _dot_nt(do, v), 0.0))
        _gla_all_intra_bwd(c, dps, p_scr, dq_scr, dk_scr)
        dq_intra, dk_intra = dq_scr[...], dk_scr[...]
        for h in range(GLA_HEADS):
            sl = slice(GLA_DK * h, GLA_DK * (h + 1))
            v = p_ref[:, O_GV + 128 * h:O_GV + 128 * (h + 1)]
            do = dos[h]
            qs_h, k_h = c["qs"][:, sl], c["k"][:, sl]
            s_in = sg_ref[0, h]
            ds_out = dsg[h]
            el_col = jnp.exp(lastcol[GLA_DK * h:GLA_DK * (h + 1), :])
            dv = _dot_tn(p_scr[h], do) + _dot(kl[:, sl], ds_out)
            dqe = _dot_nt(do, s_in)
            dkl = _dot_nt(v, ds_out)
            dsg[h] = _dot_tn(qe[:, sl], do) + el_col * ds_out
            sd = s_in * ds_out
            sd_hi = sd.astype(BF16)
            sd_lo = (sd - sd_hi.astype(F32)).astype(BF16)
            ones8 = jnp.ones((8, GLA_DV), BF16)
            nt = (((1,), (1,)), ((), ()))
            d_el = (lax.dot_general(ones8, sd_hi, nt, preferred_element_type=F32)
                    + lax.dot_general(ones8, sd_lo, nt, preferred_element_type=F32))[0:1, :]
            dqs = dqe * c["ecum"][:, sl] + dq_intra[:, sl]
            dkk = dkl * c["ekl"][:, sl] + dk_intra[:, sl]
            d_last = jnp.sum(dkl * kl[:, sl], axis=0, keepdims=True) + d_el * c["el"][:, sl]
            dcum = qs_h * dqs - k_h * dkk + jnp.where(row == CHUNK - 1, d_last, 0.0)
            dla_heads.append(_dot_exact_rhs(triu, dcum))
            dq_heads.append(dqs * (GLA_DK ** -0.5))
            dk_heads.append(dkk)
            dp_ref[:, O_GV + 128 * h:O_GV + 128 * (h + 1)] = (dv * real).astype(BF16)

        dla = jnp.concatenate(dla_heads, axis=1)
        dp_ref[:, O_GQ:O_GQ + 256] = (jnp.concatenate(dq_heads, axis=1) * real).astype(BF16)
        dp_ref[:, O_GK:O_GK + 256] = (jnp.concatenate(dk_heads, axis=1) * real).astype(BF16)
        dz = dla * (1.0 / GLA_TAU) * _sigmoid(-c["z"]) * real
        ga = p_ref[:, O_GA:O_GA + 128]
        dp_ref[:, O_GA:O_GA + 128] = _dot_nt(dz, w2_ref[...]).astype(BF16)
        dp_ref[:, O_GA + 128:IN_WP] = jnp.zeros((CHUNK, IN_WP - O_GA - 128), BF16)
        dw2_ref[...] += _dot_tn(ga, dz)
        dgb_ref[0:1, :] += jnp.sum(dz, axis=0, keepdims=True)

        if n_carried:
            pl.when(step == last_chunk)(finish)

    const = lambda shape: pl.BlockSpec(shape, lambda s: (0,) * len(shape))
    rev = lambda s: (last_chunk - s, 0)
    anywhere = [pl.BlockSpec(memory_space=pl.ANY)] * n_carried
    return pl.pallas_call(
        body, name=name, grid=(N_CHUNKS,),
        in_specs=[pl.BlockSpec((CHUNK, IN_WP), rev), pl.BlockSpec((CHUNK, D), rev), pl.BlockSpec((CHUNK, D), rev),
                  pl.BlockSpec((1, RET_HEADS, RET_DK, 128), lambda s: (last_chunk - s, 0, 0, 0)),
                  pl.BlockSpec((1, GLA_HEADS, GLA_DK, GLA_DV), lambda s: (last_chunk - s, 0, 0, 0)),
                  pl.BlockSpec((CHUNK, 128), rev), pl.BlockSpec((CHUNK, 128), rev),
                  const((128, 256)), const((1, 256)), const((1, 512)), const((1, 512))] + anywhere,
        out_specs=[pl.BlockSpec((CHUNK, IN_WP), rev), const((128, 256)), const((8, 256)),
                   const((8, 512)), const((8, 512))] + anywhere,
        out_shape=[jax.ShapeDtypeStruct((LP, IN_WP), BF16), jax.ShapeDtypeStruct((128, 256), F32),
                   jax.ShapeDtypeStruct((8, 256), F32), jax.ShapeDtypeStruct((8, 512), F32),
                   jax.ShapeDtypeStruct((8, 512), F32)] + [jax.ShapeDtypeStruct(g.shape, g.dtype) for g in carried],
        scratch_shapes=[pltpu.VMEM((RET_HEADS, RET_DK, 128), F32), pltpu.VMEM((GLA_HEADS, GLA_DK, GLA_DV), F32),
                        pltpu.VMEM((GLA_HEADS, CHUNK, CHUNK), F32), pltpu.VMEM((CHUNK, 256), F32),
                        pltpu.VMEM((CHUNK, 256), F32)] + _exchange_sems(n_carried),
        compiler_params=_cparams("arbitrary"),
    )(proj, ocat, dmrg, sr_all, sg_all, cos2, sin2, w2p, gb, rnw, gnw, *carried)


def _all_gather(xs, name):
    n = len(xs)

    def body(*refs):
        start, forward, finish = _gather_phases(refs[:n], refs[n:2 * n], *refs[2 * n:])
        start()
        forward()
        finish()

    return pl.pallas_call(
        body, name=name,
        in_specs=[pl.BlockSpec(memory_space=pl.ANY)] * n,
        out_specs=[pl.BlockSpec(memory_space=pl.ANY)] * n,
        out_shape=_gathered_shapes(xs),
        scratch_shapes=_exchange_sems(n),
    )(*xs)


def _gathered_shapes(xs):
    return [jax.ShapeDtypeStruct((N_DEV,) + x.shape, x.dtype) for x in xs]


def _exchange_sems(n):
    if n == 0:
        return []
    return [pltpu.SemaphoreType.DMA((7 * n,)), pltpu.SemaphoreType.DMA((7 * n,)), pltpu.SemaphoreType.DMA((n,))]


def _gather_phases(x_refs, out_refs, send_sems, recv_sems, local_sems):
    n = len(x_refs)
    mx, my, mc = lax.axis_index("x"), lax.axis_index("y"), lax.axis_index("c")
    me, sibling = (mx, my, mc), (mx, my, 1 - mc)
    chips = [(1 - mx, my), (mx, 1 - my), (1 - mx, 1 - my)]

    def slot(a, px, py, pc):
        return out_refs[a].at[4 * px + 2 * py + pc]

    def copy(a, k, block, to, src=None):
        return pltpu.make_async_remote_copy(
            src_ref=slot(a, *block) if src is None else src, dst_ref=slot(a, *block),
            send_sem=send_sems.at[7 * a + k], recv_sem=recv_sems.at[7 * a + k],
            device_id=to, device_id_type=MESH_IDS)

    mine = [pltpu.make_async_copy(x_refs[a], slot(a, *me), local_sems.at[a]) for a in range(n)]
    first = []
    for a in range(n):
        first.append(copy(a, 0, me, sibling, src=x_refs[a]))
        first += [copy(a, 1 + j, me, (*chip, mc), src=x_refs[a]) for j, chip in enumerate(chips)]
    passed = [copy(a, 4 + j, (*chip, mc), sibling) for j, chip in enumerate(chips) for a in range(n)]

    def start():
        for cp in mine + first:
            cp.start()

    def forward():
        for j, chip in enumerate(chips):
            for a in range(n):
                copy(a, 1 + j, (*chip, mc), me).wait_recv()
                passed[j * n + a].start()

    def finish():
        for a in range(n):
            copy(a, 0, sibling, me).wait_recv()
            for j, chip in enumerate(chips):
                copy(a, 4 + j, (*chip, 1 - mc), me).wait_recv()
        for cp in first + passed:
            cp.wait_send()
        for cp in mine:
            cp.wait()

    return start, forward, finish


def _exchange_blocks(gs, name):
    n = len(gs)

    def body(*refs):
        start, finish = _exchange_phases(refs[:n], refs[n:2 * n], *refs[2 * n:])
        start()
        finish()

    return pl.pallas_call(
        body, name=name,
        in_specs=[pl.BlockSpec(memory_space=pl.ANY)] * n,
        out_specs=[pl.BlockSpec(memory_space=pl.ANY)] * n,
        out_shape=[jax.ShapeDtypeStruct(g.shape, g.dtype) for g in gs],
        scratch_shapes=_exchange_sems(n),
    )(*gs)


def _exchange_phases(g_refs, out_refs, send_sems, recv_sems, local_sems):
    n = len(g_refs)
    mx, my, mc = lax.axis_index("x"), lax.axis_index("y"), lax.axis_index("c")
    me = 4 * mx + 2 * my + mc
    mine = [pltpu.make_async_copy(g_refs[a].at[me], out_refs[a].at[me], local_sems.at[a]) for a in range(n)]
    copies = []
    for r in range(1, N_DEV):
        px, py, pc = mx ^ (r >> 2), my ^ ((r >> 1) & 1), mc ^ (r & 1)
        peer = 4 * px + 2 * py + pc
        for a in range(n):
            copies.append(pltpu.make_async_remote_copy(
                src_ref=g_refs[a].at[peer], dst_ref=out_refs[a].at[me],
                send_sem=send_sems.at[7 * a + r - 1], recv_sem=recv_sems.at[7 * a + r - 1],
                device_id=(px, py, pc), device_id_type=MESH_IDS))

    def start():
        for cp in mine + copies:
            cp.start()

    def finish():
        for cp in copies:
            cp.wait_recv()
        for cp in copies:
            cp.wait_send()
        for cp in mine:
            cp.wait()

    return start, finish


IN_SHARD = IN_W // N_DEV
IN_SHARD_P = 512
UP_SHARD = D_UP // N_DEV
UP_SHARD_P = 768
RELAYOUT_ROWS = 256


def _pieces_w_in():
    return [(k, 0, IN_SHARD * k, IN_SHARD) for k in range(N_DEV)]


def _pieces_ffn_up():
    pieces = []
    for k in range(N_DEV):
        n, end = UP_SHARD * k, UP_SHARD * (k + 1)
        while n < end:
            half, r = divmod(n, D_FF)
            blk, off = divmod(r, CONV_BLOCK)
            run = min(CONV_BLOCK - off, end - n)
            pieces.append((k, n - UP_SHARD * k, 2 * CONV_BLOCK * blk + CONV_BLOCK * half + off, run))
            n += run
    return pieces


def _assemble_block(load, spans, dst_block, rows):
    lo = 128 * dst_block
    lane = lax.broadcasted_iota(jnp.int32, (1, 128), 1)
    out = jnp.zeros((rows, 128), F32)
    for key, src_off, dst_off, length in spans:
        a, b = max(lo, dst_off), min(lo + 128, dst_off + length)
        s, s_end = src_off + (a - dst_off), src_off + (b - dst_off)
        d = a
        while s < s_end:
            e = min(s_end, 128 * (s // 128 + 1))
            blk = load(key, s // 128)
            shift = (d - s) % 128
            if shift:
                blk = pltpu.roll(blk, shift, 1)
            out = jnp.where((lane >= d - lo) & (lane < d - lo + (e - s)), blk, out)
            d += e - s
            s = e
    return out


def _shards_to_cols(shards, pieces, width, name):
    _, rows, _ = shards.shape
    tr = RELAYOUT_ROWS

    def body(s_ref, o_ref):
        load = lambda k, b: s_ref[k, :, 128 * b:128 * (b + 1)].astype(F32)
        for db in range(width // 128):
            o_ref[:, 128 * db:128 * (db + 1)] = _assemble_block(load, pieces, db, tr).astype(BF16)

    return pl.pallas_call(
        body, name=name, grid=(rows // tr,),
        in_specs=[pl.BlockSpec((N_DEV, tr, shards.shape[2]), lambda i: (0, i, 0))],
        out_specs=pl.BlockSpec((tr, width), lambda i: (i, 0)),
        out_shape=jax.ShapeDtypeStruct((rows, width), BF16),
        compiler_params=_cparams("parallel"),
    )(shards)


def _cols_to_shards(full, pieces, shard_width, name):
    rows, width = full.shape
    tr = RELAYOUT_ROWS

    def body(f_ref, o_ref):
        load = lambda _, b: f_ref[:, 128 * b:128 * (b + 1)].astype(F32)
        for k in range(N_DEV):
            spans = [(None, dst_off, src_off, length) for dev, src_off, dst_off, length in pieces if dev == k]
            for db in range(shard_width // 128):
                o_ref[k, :, 128 * db:128 * (db + 1)] = _assemble_block(load, spans, db, tr).astype(BF16)

    return pl.pallas_call(
        body, name=name, grid=(rows // tr,),
        in_specs=[pl.BlockSpec((tr, width), lambda i: (i, 0))],
        out_specs=pl.BlockSpec((N_DEV, tr, shard_width), lambda i: (0, i, 0)),
        out_shape=jax.ShapeDtypeStruct((N_DEV, rows, shard_width), BF16),
        compiler_params=_cparams("parallel"),
    )(full)


def _adamw(parts, w, m, v, rows_per_step, name):
    rows, cols = w.shape
    assert rows % rows_per_step == 0 and parts.shape == (N_DEV, rows, cols)

    def body(p_ref, w_ref, m_ref, v_ref, g_ref, d_ref, nm_ref, nv_ref):
        g = p_ref[0].astype(F32)
        for j in range(1, N_DEV):
            g = g + p_ref[j].astype(F32)
        m_new = ADAM_B1 * m_ref[...] + (1.0 - ADAM_B1) * g
        v_new = ADAM_B2 * v_ref[...] + (1.0 - ADAM_B2) * (g * g)
        m_hat = m_new / (1.0 - ADAM_B1 ** ADAM_STEP)
        v_hat = v_new / (1.0 - ADAM_B2 ** ADAM_STEP)
        g_ref[...] = g
        d_ref[...] = -ADAM_LR * (m_hat / (jnp.sqrt(v_hat) + ADAM_EPS) + ADAM_WD * w_ref[...])
        nm_ref[...] = m_new
        nv_ref[...] = v_new

    tile = pl.BlockSpec((rows_per_step, cols), lambda i: (i, 0))
    shape = jax.ShapeDtypeStruct((rows, cols), F32)
    return pl.pallas_call(
        body, name=name, grid=(rows // rows_per_step,),
        in_specs=[pl.BlockSpec((N_DEV, rows_per_step, cols), lambda i: (0, i, 0)), tile, tile, tile],
        out_specs=[tile, tile, tile, tile],
        out_shape=[shape, shape, shape, shape],
        compiler_params=_cparams("parallel"),
    )(parts, w, m, v)


BIG = (("w_in", (DEPTH, D, IN_W // N_DEV), 2), ("w_out", (DEPTH, D // N_DEV, D), 1),
       ("ffn_up", (DEPTH, D, D_UP // N_DEV), 2), ("ffn_down", (DEPTH, D_FF // N_DEV, D), 1))
SMALL = (("meta_tokens", (N_META, D // N_DEV), 1), ("gla_gate_w2", (DEPTH, GATE_RANK, 256 // N_DEV), 2),
         ("ffn_conv_w", (DEPTH, 3, D_UP // N_DEV), 2))
REPL = (("pre_mix_norm", (DEPTH, D)), ("gla_gate_b", (DEPTH, 256)), ("ret_norm_w", (DEPTH, 512)),
        ("gla_norm_w", (DEPTH, 512)), ("post_mix_norm", (DEPTH, D)), ("pre_ffn_norm", (DEPTH, D)),
        ("ffn_conv_b", (DEPTH, D_UP)), ("post_ffn_norm", (DEPTH, D)))
WEIGHT_ORDER = ("meta_tokens", "pre_mix_norm", "w_in", "gla_gate_w2", "gla_gate_b", "ret_norm_w", "gla_norm_w",
                "w_out", "post_mix_norm", "pre_ffn_norm", "ffn_up", "ffn_conv_w", "ffn_conv_b", "ffn_down",
                "post_ffn_norm")


def _size(shape):
    return math.prod(shape)


def _round_up(n, mult):
    return -(-n // mult) * mult


REPL_ROWS = _round_up(-(-sum(_size(s) for _, s in REPL) // LANES), 8)
SMALL_ROWS = _round_up(-(-sum(_size(s) for _, s, _ in SMALL) // LANES), 8)


def _pack(arrays, rows, dtype):
    flat = jnp.concatenate([a.reshape(-1).astype(dtype) for a in arrays])
    return jnp.pad(flat, (0, rows * LANES - flat.shape[0])).reshape(rows, LANES)


def _unpack(buf, shapes):
    flat = buf.reshape(-1)
    out, off = [], 0
    for shape in shapes:
        out.append(flat[off:off + _size(shape)].reshape(shape))
        off += _size(shape)
    return out


def _unshard(blocks, axis):
    moved = jnp.moveaxis(blocks, 0, axis)
    shape = list(moved.shape)
    shape[axis:axis + 2] = [shape[axis] * shape[axis + 1]]
    return moved.reshape(shape)


def _to_blocks(full, axis):
    shape = list(full.shape)
    shape[axis:axis + 1] = [N_DEV, shape[axis] // N_DEV]
    return jnp.moveaxis(full.reshape(shape), axis, 0)


def _interleave_cols(w):
    lead = w.shape[:-1]
    return jnp.swapaxes(w.reshape(lead + (2, N_CONV_BLOCKS, CONV_BLOCK)), -3, -2).reshape(lead + (D_UP,))


def _deinterleave_cols(w):
    lead = w.shape[:-1]
    return jnp.swapaxes(w.reshape(lead + (N_CONV_BLOCKS, 2, CONV_BLOCK)), -3, -2).reshape(lead + (D_UP,))


def _rope_tables():
    half = RET_DK // 2
    inv = ROPE_BASE ** (-jnp.arange(half, dtype=F32) / half)
    pos = jnp.arange(LP, dtype=F32) - float(PAD_ROWS)
    ang = pos[:, None] * inv[None, :]
    c, s = jnp.cos(ang), jnp.sin(ang)
    return jnp.concatenate([c, c], axis=1), jnp.concatenate([-s, s], axis=1)


def kernel(x, meta_tokens, pre_mix_norm, w_in, gla_gate_w2, gla_gate_b, ret_norm_w, gla_norm_w, w_out, post_mix_norm, pre_ffn_norm, ffn_up, ffn_conv_w, ffn_conv_b, ffn_down, post_ffn_norm, loss_target, m_meta_tokens, m_pre_mix_norm, m_w_in, m_gla_gate_w2, m_gla_gate_b, m_ret_norm_w, m_gla_norm_w, m_w_out, m_post_mix_norm, m_pre_ffn_norm, m_ffn_up, m_ffn_conv_w, m_ffn_conv_b, m_ffn_down, m_post_ffn_norm, v_meta_tokens, v_pre_mix_norm, v_w_in, v_gla_gate_w2, v_gla_gate_b, v_ret_norm_w, v_gla_norm_w, v_w_out, v_post_mix_norm, v_pre_ffn_norm, v_ffn_up, v_ffn_conv_w, v_ffn_conv_b, v_ffn_down, v_post_ffn_norm):
    weights = dict(meta_tokens=meta_tokens, pre_mix_norm=pre_mix_norm, w_in=w_in, gla_gate_w2=gla_gate_w2,
                   gla_gate_b=gla_gate_b, ret_norm_w=ret_norm_w, gla_norm_w=gla_norm_w, w_out=w_out,
                   post_mix_norm=post_mix_norm, pre_ffn_norm=pre_ffn_norm, ffn_up=ffn_up, ffn_conv_w=ffn_conv_w,
                   ffn_conv_b=ffn_conv_b, ffn_down=ffn_down, post_ffn_norm=post_ffn_norm)
    mom1 = dict(meta_tokens=m_meta_tokens, pre_mix_norm=m_pre_mix_norm, w_in=m_w_in, gla_gate_w2=m_gla_gate_w2,
                gla_gate_b=m_gla_gate_b, ret_norm_w=m_ret_norm_w, gla_norm_w=m_gla_norm_w, w_out=m_w_out,
                post_mix_norm=m_post_mix_norm, pre_ffn_norm=m_pre_ffn_norm, ffn_up=m_ffn_up,
                ffn_conv_w=m_ffn_conv_w, ffn_conv_b=m_ffn_conv_b, ffn_down=m_ffn_down, post_ffn_norm=m_post_ffn_norm)
    mom2 = dict(meta_tokens=v_meta_tokens, pre_mix_norm=v_pre_mix_norm, w_in=v_w_in, gla_gate_w2=v_gla_gate_w2,
                gla_gate_b=v_gla_gate_b, ret_norm_w=v_ret_norm_w, gla_norm_w=v_gla_norm_w, w_out=v_w_out,
                post_mix_norm=v_post_mix_norm, pre_ffn_norm=v_pre_ffn_norm, ffn_up=v_ffn_up,
                ffn_conv_w=v_ffn_conv_w, ffn_conv_b=v_ffn_conv_b, ffn_down=v_ffn_down, post_ffn_norm=v_post_ffn_norm)

    pad_cols = lambda a, width: jnp.pad(a, ((0, 0), (0, width - a.shape[1])))
    big_names = [n for n, _, _ in BIG]
    shard = {}
    for l in range(DEPTH):
        shard[l, "w_in"] = pad_cols(w_in[l].astype(BF16), IN_SHARD_P)
        shard[l, "w_out"] = w_out[l].astype(BF16)
        shard[l, "ffn_up"] = pad_cols(ffn_up[l].astype(BF16), UP_SHARD_P)
        shard[l, "ffn_down"] = ffn_down[l].astype(BF16)
    gathered = {(0, "w_in"): _all_gather([shard[0, "w_in"]], "gather_w_in_0")[0]}
    gather_in_mixer = {0: [(0, n) for n in big_names[1:]] + [(l, n) for l in range(1, DEPTH) for n in big_names]}
    small = _all_gather([_pack([weights[n] for n, _, _ in SMALL], SMALL_ROWS, F32)], "gather_small_weights")[0]
    small_parts = _unpack_blocks(small, [s for _, s, _ in SMALL])
    full = {n: _unshard(p, ax) for (n, _, ax), p in zip(SMALL, small_parts)}
    w2p = jnp.pad(full["gla_gate_w2"], ((0, 0), (0, 128 - GATE_RANK), (0, 0)))
    cw8 = jnp.concatenate([_interleave_cols(full["ffn_conv_w"]), _interleave_cols(ffn_conv_b)[:, None, :],
                           jnp.zeros((DEPTH, 4, D_UP), F32)], axis=1)
    cos2, sin2 = _rope_tables()

    h = jnp.concatenate([jnp.zeros((PAD_ROWS, D), F32), full["meta_tokens"], x[0]], axis=0)
    target = jnp.concatenate([jnp.zeros((CHUNK, D), F32), loss_target[0]], axis=0)
    saved, layer_w = [], []
    for l in range(DEPTH):
        lw = dict(w_in=_shards_to_cols(gathered[l, "w_in"], _pieces_w_in(), IN_WP, f"w_in_cols_{l}"))
        a1 = _rmsnorm_fwd(h, pre_mix_norm[l:l + 1], f"pre_mix_norm_{l}")
        proj = _matmul(a1, lw["w_in"], out_dtype=F32, tm=TM, tn=1280, tk=D, name=f"in_proj_{l}")
        keys = gather_in_mixer.get(l, [])
        ocat, merged, sr_all, sg_all, *got = _mixer_fwd(proj, cos2, sin2, w2p[l], gla_gate_b[l:l + 1],
                                                        ret_norm_w[l:l + 1], gla_norm_w[l:l + 1], f"mixer_fwd_{l}",
                                                        carried=[shard[key] for key in keys])
        gathered.update(zip(keys, got))
        lw["w_out"] = gathered[l, "w_out"].reshape(D, D)
        lw["w_up"] = _shards_to_cols(gathered[l, "ffn_up"], _pieces_ffn_up(), D_UP, f"ffn_up_cols_{l}")
        lw["w_down"] = gathered[l, "ffn_down"].reshape(D_FF, D)
        layer_w.append(lw)
        m = _matmul(merged, lw["w_out"], out_dtype=F32, tm=TM, tn=D, tk=D, name=f"out_proj_{l}")
        h1 = _resid_norm(h, m, post_mix_norm[l:l + 1], f"post_mix_norm_{l}")
        a2 = _rmsnorm_fwd(h1, pre_ffn_norm[l:l + 1], f"pre_ffn_norm_{l}")
        u = _matmul(a2, lw["w_up"], out_dtype=BF16, tm=TM, tn=1408, tk=D, name=f"ffn_up_{l}")
        act = _conv_act_fwd(u, cw8[l], f"ffn_conv_act_{l}")
        f = _matmul(act, lw["w_down"], out_dtype=F32, tm=TM, tn=D, tk=D_FF, name=f"ffn_down_{l}")
        h2 = _resid_norm(h1, f, post_ffn_norm[l:l + 1], f"post_ffn_norm_{l}")
        saved.append(dict(h=h, a1=a1, proj=proj, ocat=ocat, merged=merged, sr=sr_all, sg=sg_all, m=m, h1=h1,
                          a2=a2, u=u, act=act, f=f))
        h = h2

    dh, loss_acc = _loss_head(h, target, "loss_head")
    loss = lax.psum(loss_acc[0, 0], ("x", "y", "c"))

    kinds = ("grad", "delta", "new_m", "new_v")
    grads = {n: [None] * DEPTH for n in WEIGHT_ORDER if n != "meta_tokens" and n not in big_names}
    pending, parts = [], {}
    for l in reversed(range(DEPTH)):
        s, lw = saved[l], layer_w[l]
        df, g_post_ffn = _norm_bwd(dh, s["f"], post_ffn_norm[l:l + 1], None, BF16, f"post_ffn_norm_bwd_{l}")
        dact = _matmul(df, lw["w_down"], tb=True, out_dtype=BF16, tm=TM, tn=D_FF, tk=D, name=f"ffn_down_dx_{l}")
        g_down = _matmul(s["act"], df, ta=True, out_dtype=BF16, tm=D_FF // 2, tn=D, tk=TM, name=f"ffn_down_dw_{l}")
        du, dcw = _conv_act_bwd(dact, s["u"], cw8[l], f"ffn_conv_act_bwd_{l}")
        da2 = _matmul(du, lw["w_up"], tb=True, out_dtype=F32, tm=TM, tn=D, tk=1408, name=f"ffn_up_dx_{l}")
        g_up = _matmul(s["a2"], du, ta=True, out_dtype=BF16, tm=D, tn=1408, tk=TM, name=f"ffn_up_dw_{l}")
        dh1, g_pre_ffn = _norm_bwd(da2, s["h1"], pre_ffn_norm[l:l + 1], dh, F32, f"pre_ffn_norm_bwd_{l}")
        dm, g_post_mix = _norm_bwd(dh1, s["m"], post_mix_norm[l:l + 1], None, BF16, f"post_mix_norm_bwd_{l}")
        dmerged = _matmul(dm, lw["w_out"], tb=True, out_dtype=F32, tm=TM, tn=D, tk=D, name=f"out_proj_dx_{l}")
        g_out = _matmul(s["merged"], dm, ta=True, out_dtype=BF16, tm=D, tn=D, tk=TM, name=f"out_proj_dw_{l}")
        pending += [((l, "ffn_down"), g_down.reshape(N_DEV, D_FF // N_DEV, D)),
                    ((l, "ffn_up"), _cols_to_shards(g_up, _pieces_ffn_up(), UP_SHARD_P, f"ffn_up_grad_shards_{l}")),
                    ((l, "w_out"), g_out.reshape(N_DEV, D // N_DEV, D))]
        dproj, g_w2, g_gb, g_rn, g_gn, *got = _mixer_bwd(s["proj"], s["ocat"], dmerged, s["sr"], s["sg"], cos2, sin2,
                                                         w2p[l], gla_gate_b[l:l + 1], ret_norm_w[l:l + 1],
                                                         gla_norm_w[l:l + 1], f"mixer_bwd_{l}",
                                                         carried=[blocks for _, blocks in pending])
        parts.update(zip([key for key, _ in pending], got))
        da1 = _matmul(dproj, lw["w_in"], tb=True, out_dtype=F32, tm=TM, tn=D, tk=1280, name=f"in_proj_dx_{l}")
        g_in = _matmul(s["a1"], dproj, ta=True, out_dtype=BF16, tm=D, tn=1280, tk=TM, name=f"in_proj_dw_{l}")
        pending = [((l, "w_in"), _cols_to_shards(g_in, _pieces_w_in(), IN_SHARD_P, f"w_in_grad_shards_{l}"))]
        dh, g_pre_mix = _norm_bwd(da1, s["h"], pre_mix_norm[l:l + 1], dh1, F32, f"pre_mix_norm_bwd_{l}")
        grads["post_ffn_norm"][l] = g_post_ffn[0]
        grads["ffn_conv_w"][l] = _deinterleave_cols(dcw[0:3])
        grads["ffn_conv_b"][l] = _deinterleave_cols(dcw[3])
        grads["pre_ffn_norm"][l] = g_pre_ffn[0]
        grads["post_mix_norm"][l] = g_post_mix[0]
        grads["gla_gate_w2"][l] = g_w2[:GATE_RANK]
        grads["gla_gate_b"][l] = g_gb[0]
        grads["ret_norm_w"][l] = g_rn[0]
        grads["gla_norm_w"][l] = g_gn[0]
        grads["pre_mix_norm"][l] = g_pre_mix[0]
    local = {n: jnp.stack(v) for n, v in grads.items()}
    local["meta_tokens"] = dh[PAD_ROWS:CHUNK]
    grad_x = dh[CHUNK:][None]

    blocks = jnp.concatenate([_to_blocks(local[n], ax).reshape(N_DEV, -1) for n, _, ax in SMALL], axis=1)
    blocks = jnp.pad(blocks, ((0, 0), (0, SMALL_ROWS * LANES - blocks.shape[1]))).reshape(N_DEV, SMALL_ROWS, LANES)
    *got, small_grad_parts = _exchange_blocks([b for _, b in pending] + [blocks], "exchange_last_grads")
    parts.update(zip([key for key, _ in pending], got))

    widths = dict(w_in=IN_SHARD_P, w_out=D, ffn_up=UP_SHARD_P, ffn_down=D)
    steps = dict(w_in=256, w_out=D // N_DEV, ffn_up=256, ffn_down=D_FF // N_DEV // 2)
    big_out = {kind: {n: [None] * DEPTH for n in big_names} for kind in kinds}
    for l in range(DEPTH):
        for n in big_names:
            mine = [pad_cols(d[n][l], widths[n]) for d in (weights, mom1, mom2)]
            results = _adamw(parts[l, n], *mine, steps[n], f"adamw_{n}_{l}")
            for kind, r in zip(kinds, results):
                big_out[kind][n][l] = r[:, :weights[n].shape[2]]
    out = {kind: {n: jnp.stack(v) for n, v in big_out[kind].items()} for kind in kinds}
    shard_shapes = [s for _, s, _ in SMALL]
    packed = [_pack([d[n] for n, _, _ in SMALL], SMALL_ROWS, F32) for d in (weights, mom1, mom2)]
    results = _adamw(small_grad_parts, *packed, SMALL_ROWS, "adamw_small_sharded")
    for kind, buf in zip(kinds, results):
        out[kind].update(zip([n for n, _, _ in SMALL], _unpack(buf, shard_shapes)))

    repl_parts = _all_gather([_pack([local[n] for n, _ in REPL], REPL_ROWS, F32)], "gather_small_grads")[0]
    packed = [_pack([d[n] for n, _ in REPL], REPL_ROWS, F32) for d in (weights, mom1, mom2)]
    results = _adamw(repl_parts, *packed, REPL_ROWS, "adamw_replicated")
    repl_shapes = [s for _, s in REPL]
    for kind, buf in zip(kinds, results):
        out[kind].update(zip([n for n, _ in REPL], _unpack(buf, repl_shapes)))

    return (loss, grad_x, *[out["grad"][n] for n in WEIGHT_ORDER], *[out["delta"][n] for n in WEIGHT_ORDER],
            *[out["new_m"][n] for n in WEIGHT_ORDER], *[out["new_v"][n] for n in WEIGHT_ORDER])


def _unpack_blocks(gathered, shapes):
    flat = gathered.reshape(N_DEV, -1)
    out, off = [], 0
    for shape in shapes:
        out.append(flat[:, off:off + _size(shape)].reshape((N_DEV,) + shape))
        off += _size(shape)
    return out
```

```python
import math

import jax
import jax.numpy as jnp
from jax import lax
from jax.experimental import pallas as pl
from jax.experimental.pallas import tpu as pltpu

F32 = jnp.float32
BF16 = jnp.bfloat16

D = 1024
SEQ = 8192
DEPTH = 2
N_META = 16
CHUNK = 64
SUB = 16
N_SUB = CHUNK // SUB
PAD_ROWS = CHUNK - N_META
LP = SEQ + CHUNK
N_CHUNKS = LP // CHUNK
RET_HEADS = 4
RET_DK = 128
GLA_HEADS = 4
GLA_DK = 64
GLA_DV = 128
GLA_TAU = 16.0
GATE_RANK = 16
IN_W = 3600
IN_WP = 3840
D_FF = 2816
D_UP = 2 * D_FF
CONV_BLOCK = 256
N_CONV_BLOCKS = D_FF // CONV_BLOCK
ROPE_BASE = 10000.0
EPS = 1e-6
N_DEV = 8
LANES = 1024

O_RQ, O_RK, O_RV, O_RG = 0, 512, 1024, 1536
O_GQ, O_GK, O_GV, O_GR, O_GA = 2048, 2304, 2560, 3072, 3584

ADAM_LR = 0.001
ADAM_B1 = 0.9
ADAM_B2 = 0.999
ADAM_EPS = 1e-08
ADAM_WD = 0.01
ADAM_STEP = 10

VMEM_LIMIT = 56 * 1024 * 1024
MESH_IDS = pl.DeviceIdType.MESH


def _row_tile(rows, limit):
    best = 16
    for t in range(16, min(rows, limit) + 1, 16):
        if rows % t == 0:
            best = t
    return best


TM = _row_tile(LP, 688)
TK_ROWS = _row_tile(LP, 1376)


def _cparams(*sem):
    return pltpu.CompilerParams(dimension_semantics=sem, vmem_limit_bytes=VMEM_LIMIT)


def _dot(a, b):
    return jnp.dot(a.astype(BF16), b.astype(BF16), preferred_element_type=F32)


def _dot_nt(a, b):
    return lax.dot_general(a.astype(BF16), b.astype(BF16), (((1,), (1,)), ((), ())), preferred_element_type=F32)


def _dot_tn(a, b):
    return lax.dot_general(a.astype(BF16), b.astype(BF16), (((0,), (0,)), ((), ())), preferred_element_type=F32)


def _split3(x):
    hi = x.astype(BF16)
    r1 = x - hi.astype(F32)
    mid = r1.astype(BF16)
    lo = (r1 - mid.astype(F32)).astype(BF16)
    return hi, mid, lo


def _dot_exact_rhs(t, x):
    hi, mid, lo = _split3(x)
    t = t.astype(BF16)
    return (jnp.dot(t, hi, preferred_element_type=F32) + jnp.dot(t, mid, preferred_element_type=F32)
            + jnp.dot(t, lo, preferred_element_type=F32))


def _dot_tn_exact_lhs(x, ones):
    dims = (((0,), (0,)), ((), ()))
    hi, mid, lo = _split3(x)
    ones = ones.astype(BF16)
    return (lax.dot_general(hi, ones, dims, preferred_element_type=F32)
            + lax.dot_general(mid, ones, dims, preferred_element_type=F32)
            + lax.dot_general(lo, ones, dims, preferred_element_type=F32))


def _sigmoid(x):
    return 1.0 / (1.0 + jnp.exp(-x))


def _matmul(a, b, *, ta=False, tb=False, out_dtype, tm, tn, tk, name):
    m = a.shape[1] if ta else a.shape[0]
    k = a.shape[0] if ta else a.shape[1]
    n = b.shape[0] if tb else b.shape[1]
    assert (b.shape[1] if tb else b.shape[0]) == k
    assert m % tm == 0 and n % tn == 0 and k % tk == 0, (name, m, n, k, tm, tn, tk)
    nk = k // tk
    a_spec = pl.BlockSpec((tk, tm), lambda i, j, kk: (kk, i)) if ta else pl.BlockSpec((tm, tk), lambda i, j, kk: (i, kk))
    b_spec = pl.BlockSpec((tn, tk), lambda i, j, kk: (j, kk)) if tb else pl.BlockSpec((tk, tn), lambda i, j, kk: (kk, j))
    dims = (((0 if ta else 1,), (1 if tb else 0,)), ((), ()))

    def body(a_ref, b_ref, o_ref, *acc):
        prod = lax.dot_general(a_ref[...].astype(BF16), b_ref[...].astype(BF16), dims, preferred_element_type=F32)
        if nk == 1:
            o_ref[...] = prod.astype(out_dtype)
            return
        acc_ref, = acc
        kk = pl.program_id(2)

        @pl.when(kk == 0)
        def _():
            acc_ref[...] = prod

        @pl.when(kk > 0)
        def _():
            acc_ref[...] += prod

        @pl.when(kk == nk - 1)
        def _():
            o_ref[...] = acc_ref[...].astype(out_dtype)

    return pl.pallas_call(
        body, name=name, grid=(m // tm, n // tn, nk),
        in_specs=[a_spec, b_spec],
        out_specs=pl.BlockSpec((tm, tn), lambda i, j, kk: (i, j)),
        out_shape=jax.ShapeDtypeStruct((m, n), out_dtype),
        scratch_shapes=[pltpu.VMEM((tm, tn), F32)] if nk > 1 else [],
        compiler_params=_cparams("parallel", "parallel", "arbitrary"),
    )(a, b)


def _rmsnorm_fwd(x, w, name):
    def body(x_ref, w_ref, o_ref):
        xv = x_ref[...]
        r = lax.rsqrt(jnp.mean(xv * xv, axis=-1, keepdims=True) + EPS)
        o_ref[...] = (xv * r * w_ref[...]).astype(BF16)

    return pl.pallas_call(
        body, name=name, grid=(LP // TM,),
        in_specs=[pl.BlockSpec((TM, D), lambda i: (i, 0)), pl.BlockSpec((1, D), lambda i: (0, 0))],
        out_specs=pl.BlockSpec((TM, D), lambda i: (i, 0)),
        out_shape=jax.ShapeDtypeStruct((LP, D), BF16),
        compiler_params=_cparams("parallel"),
    )(x, w)


def _resid_norm(h, m, w, name):
    def body(h_ref, m_ref, w_ref, o_ref):
        mv = m_ref[...]
        r = lax.rsqrt(jnp.mean(mv * mv, axis=-1, keepdims=True) + EPS)
        row = pl.program_id(0) * TM + lax.broadcasted_iota(jnp.int32, (TM, 1), 0)
        o_ref[...] = h_ref[...] + jnp.where(row >= PAD_ROWS, mv * r * w_ref[...], 0.0)

    return pl.pallas_call(
        body, name=name, grid=(LP // TM,),
        in_specs=[pl.BlockSpec((TM, D), lambda i: (i, 0)), pl.BlockSpec((TM, D), lambda i: (i, 0)),
                  pl.BlockSpec((1, D), lambda i: (0, 0))],
        out_specs=pl.BlockSpec((TM, D), lambda i: (i, 0)),
        out_shape=jax.ShapeDtypeStruct((LP, D), F32),
        compiler_params=_cparams("parallel"),
    )(h, m, w)


def _norm_bwd(dy, x, w, resid, out_dtype, name):
    has_resid = resid is not None

    def body(*refs):
        if has_resid:
            dy_ref, x_ref, w_ref, r_ref, dx_ref, dw_ref = refs
        else:
            dy_ref, x_ref, w_ref, dx_ref, dw_ref = refs
        i = pl.program_id(0)

        @pl.when(i == 0)
        def _():
            dw_ref[...] = jnp.zeros_like(dw_ref)

        row = i * TM + lax.broadcasted_iota(jnp.int32, (TM, 1), 0)
        dyv = jnp.where(row >= PAD_ROWS, dy_ref[...], 0.0)
        xv = x_ref[...]
        r = lax.rsqrt(jnp.mean(xv * xv, axis=-1, keepdims=True) + EPS)
        g = dyv * w_ref[...]
        dx = r * g - xv * (r * r * r * jnp.mean(g * xv, axis=-1, keepdims=True))
        if has_resid:
            dx = dx + r_ref[...]
        dx_ref[...] = dx.astype(out_dtype)
        dw_ref[0:1, :] += jnp.sum(dyv * xv * r, axis=0, keepdims=True)

    tile = pl.BlockSpec((TM, D), lambda i: (i, 0))
    in_specs = [tile, tile, pl.BlockSpec((1, D), lambda i: (0, 0))] + ([tile] if has_resid else [])
    args = (dy, x, w) + ((resid,) if has_resid else ())
    return pl.pallas_call(
        body, name=name, grid=(LP // TM,),
        in_specs=in_specs,
        out_specs=[tile, pl.BlockSpec((8, D), lambda i: (0, 0))],
        out_shape=[jax.ShapeDtypeStruct((LP, D), out_dtype), jax.ShapeDtypeStruct((8, D), F32)],
        compiler_params=_cparams("arbitrary"),
    )(*args)


def _loss_head(y, target, name):
    def body(y_ref, t_ref, dy_ref, loss_ref):
        i = pl.program_id(0)

        @pl.when(i == 0)
        def _():
            loss_ref[...] = jnp.zeros_like(loss_ref)

        row = i * TM + lax.broadcasted_iota(jnp.int32, (TM, 1), 0)
        diff = jnp.where(row >= CHUNK, y_ref[...] - t_ref[...], 0.0)
        dy_ref[...] = diff * (1.0 / D)
        loss_ref[...] += (0.5 / D) * jnp.sum(diff * diff)

    tile = pl.BlockSpec((TM, D), lambda i: (i, 0))
    return pl.pallas_call(
        body, name=name, grid=(LP // TM,),
        in_specs=[tile, tile],
        out_specs=[tile, pl.BlockSpec((8, 128), lambda i: (0, 0))],
        out_shape=[jax.ShapeDtypeStruct((LP, D), F32), jax.ShapeDtypeStruct((8, 128), F32)],
        compiler_params=_cparams("arbitrary"),
    )(y, target)


GELU_C = math.sqrt(2.0 / math.pi)
GELU_K = 0.044715
STRIP = 16


def _shift_down(x, prev8, rows):
    row = lax.broadcasted_iota(jnp.int32, (rows, 1), 0)
    p1 = pltpu.roll(prev8, 1, 0)
    p2 = pltpu.roll(prev8, 2, 0)
    x1 = jnp.where(row == 0, p1[0:1, :], pltpu.roll(x, 1, 0))
    x2 = jnp.where(row == 0, p2[0:1, :], jnp.where(row == 1, p2[1:2, :], pltpu.roll(x, 2, 0)))
    return x1, x2


def _conv_act_fwd(u, cw8, name):
    n_rows = LP // TM
    cb2 = 2 * CONV_BLOCK

    def body(u_ref, cw_ref, act_ref, carry_ref):
        i = pl.program_id(1)

        @pl.when(i == 0)
        def _():
            carry_ref[...] = jnp.zeros_like(carry_ref)

        x = u_ref[...].astype(F32)
        x1, x2 = _shift_down(x, carry_ref[...], TM)
        conv = cw_ref[3:4, :] + x2 * cw_ref[0:1, :] + x1 * cw_ref[1:2, :] + x * cw_ref[2:3, :]
        a = conv[:, :CONV_BLOCK]
        g = conv[:, CONV_BLOCK:]
        t = jnp.tanh(GELU_C * (a + GELU_K * a * a * a))
        act_ref[...] = (0.5 * a * (1.0 + t) * g).astype(BF16)
        carry_ref[...] = x[TM - 8:TM, :]

    return pl.pallas_call(
        body, name=name, grid=(N_CONV_BLOCKS, n_rows),
        in_specs=[pl.BlockSpec((TM, cb2), lambda j, i: (i, j)), pl.BlockSpec((8, cb2), lambda j, i: (0, j))],
        out_specs=pl.BlockSpec((TM, CONV_BLOCK), lambda j, i: (i, j)),
        out_shape=jax.ShapeDtypeStruct((LP, D_FF), BF16),
        scratch_shapes=[pltpu.VMEM((8, cb2), F32)],
        compiler_params=_cparams("arbitrary", "arbitrary"),
    )(u, cw8)


def _conv_act_bwd(dact, u, cw8, name):
    n_rows = LP // TM
    cb2 = 2 * CONV_BLOCK
    n_strips = TM // STRIP
    halo_per_tile = TM // STRIP

    def body(dact_ref, u_ref, uh_ref, cw_ref, du_ref, dcw_ref, carry_ref):
        i = pl.program_id(1)
        tile = n_rows - 1 - i

        @pl.when(i == 0)
        def _():
            dcw_ref[...] = jnp.zeros_like(dcw_ref)
            carry_ref[...] = jnp.zeros_like(carry_ref)

        w0, w1, w2, bias = cw_ref[0:1, :], cw_ref[1:2, :], cw_ref[2:3, :], cw_ref[3:4, :]
        row = lax.broadcasted_iota(jnp.int32, (STRIP, 1), 0)
        fold = lambda z: z[:8, :] + z[8:, :]

        def strip(r0, above, carry):
            n1, n2, s0, s1, s2, s3 = carry
            x = u_ref[pl.ds(r0, STRIP), :].astype(F32)
            x1 = jnp.where(row < 1, pltpu.roll(above, 1, 0), pltpu.roll(x, 1, 0))
            x2 = jnp.where(row < 2, pltpu.roll(above, 2, 0), pltpu.roll(x, 2, 0))
            conv = bias + x2 * w0 + x1 * w1 + x * w2
            a = conv[:, :CONV_BLOCK]
            g = conv[:, CONV_BLOCK:]
            t = jnp.tanh(GELU_C * (a + GELU_K * a * a * a))
            gel = 0.5 * a * (1.0 + t)
            dgel = 0.5 * (1.0 + t) + 0.5 * a * (1.0 - t * t) * (GELU_C * (1.0 + 3.0 * GELU_K * a * a))
            dav = dact_ref[pl.ds(r0, STRIP), :].astype(F32)
            dconv = jnp.concatenate([dav * g * dgel, dav * gel], axis=1)
            u1 = pltpu.roll(dconv, STRIP - 1, 0)
            u2 = pltpu.roll(dconv, STRIP - 2, 0)
            d1 = jnp.where(row >= STRIP - 1, n1, u1)
            d2 = jnp.where(row >= STRIP - 2, n2, u2)
            du_ref[pl.ds(r0, STRIP), :] = (dconv * w2 + d1 * w1 + d2 * w0).astype(BF16)
            carry_ref[...] = dconv
            return (u1, u2, s0 + fold(dconv * x2), s1 + fold(dconv * x1), s2 + fold(dconv * x), s3 + fold(dconv))

        below = carry_ref[...]
        zero = jnp.zeros((8, cb2), F32)
        init = (pltpu.roll(below, STRIP - 1, 0), pltpu.roll(below, STRIP - 2, 0), zero, zero, zero, zero)

        def step(k, carry):
            r0 = pl.multiple_of((n_strips - 1 - k) * STRIP, STRIP)
            above = u_ref[pl.ds(pl.multiple_of(r0 - STRIP, STRIP), STRIP), :].astype(F32)
            return strip(r0, above, carry)

        carry = lax.fori_loop(0, n_strips - 1, step, init)
        halo = jnp.where(tile == 0, 0.0, uh_ref[...].astype(F32))
        _, _, s0, s1, s2, s3 = strip(0, halo, carry)
        dcw_ref[0:1, :] += jnp.sum(s0, axis=0, keepdims=True)
        dcw_ref[1:2, :] += jnp.sum(s1, axis=0, keepdims=True)
        dcw_ref[2:3, :] += jnp.sum(s2, axis=0, keepdims=True)
        dcw_ref[3:4, :] += jnp.sum(s3, axis=0, keepdims=True)

    return pl.pallas_call(
        body, name=name, grid=(N_CONV_BLOCKS, n_rows),
        in_specs=[pl.BlockSpec((TM, CONV_BLOCK), lambda j, i: (n_rows - 1 - i, j)),
                  pl.BlockSpec((TM, cb2), lambda j, i: (n_rows - 1 - i, j)),
                  pl.BlockSpec((STRIP, cb2), lambda j, i: (jnp.maximum((n_rows - 1 - i) * halo_per_tile - 1, 0), j)),
                  pl.BlockSpec((8, cb2), lambda j, i: (0, j))],
        out_specs=[pl.BlockSpec((TM, cb2), lambda j, i: (n_rows - 1 - i, j)),
                   pl.BlockSpec((8, cb2), lambda j, i: (0, j))],
        out_shape=[jax.ShapeDtypeStruct((LP, D_UP), BF16), jax.ShapeDtypeStruct((8, D_UP), F32)],
        scratch_shapes=[pltpu.VMEM((STRIP, cb2), F32)],
        compiler_params=_cparams("arbitrary", "arbitrary"),
    )(dact, u, u, cw8)


def _ret_consts(h):
    lg = math.log(1.0 - 2.0 ** (-5.0 - h))
    ri = lax.broadcasted_iota(jnp.int32, (CHUNK, CHUNK), 0)
    ci = lax.broadcasted_iota(jnp.int32, (CHUNK, CHUNK), 1)
    diff = (ri - ci).astype(F32)
    dmat = jnp.where(diff >= 0, jnp.exp(lg * jnp.maximum(diff, 0.0)), 0.0)
    rowf = lax.broadcasted_iota(jnp.int32, (CHUNK, 1), 0).astype(F32)
    zeta = jnp.exp(lg * (CHUNK - 1.0 - rowf))
    xi = jnp.exp(lg * (rowf + 1.0))
    return dmat, zeta, xi, math.exp(lg * CHUNK)


def _rope(t, cosv, sinv):
    return t * cosv + pltpu.roll(t, RET_DK // 2, 1) * sinv


def _unrope(d, cosv, sinv):
    return d * cosv + pltpu.roll(d * sinv, RET_DK // 2, 1)


def _gla_common(p_ref, w2_ref, gb_ref, chunk):
    row = lax.broadcasted_iota(jnp.int32, (CHUNK, 1), 0)
    real = (chunk * CHUNK + row) >= PAD_ROWS
    ga = p_ref[:, O_GA:O_GA + 128]
    z = _dot(ga, w2_ref[...]) + gb_ref[...]
    la = (jnp.minimum(z, 0.0) - jnp.log(1.0 + jnp.exp(-jnp.abs(z)))) * (1.0 / GLA_TAU)
    la = jnp.where(real, la, 0.0)
    ri = lax.broadcasted_iota(jnp.int32, (CHUNK, CHUNK), 0)
    ci = lax.broadcasted_iota(jnp.int32, (CHUNK, CHUNK), 1)
    tril = (ri >= ci).astype(F32)
    cum = _dot_exact_rhs(tril, la)
    last = cum[CHUNK - 1:CHUNK, :]
    qs = p_ref[:, O_GQ:O_GQ + 256] * (GLA_DK ** -0.5)
    k = p_ref[:, O_GK:O_GK + 256]
    ecum = jnp.exp(cum)
    ekl = jnp.exp(last - cum)
    el = jnp.exp(last)
    refs = [jnp.zeros((1, 256), F32)] + [cum[a * SUB - 1:a * SUB, :] for a in range(1, N_SUB)]
    eq = [jnp.exp(cum[a * SUB:(a + 1) * SUB, :] - refs[a]) for a in range(N_SUB)]
    spread = refs[0] - cum[SUB - 1:SUB, :]
    for a in range(1, N_SUB):
        spread = jnp.maximum(spread, refs[a] - cum[(a + 1) * SUB - 1:(a + 1) * SUB, :])
    small = jnp.max(spread) <= GLA_FACTORED_MAX
    return dict(real=real, row=row, z=z, la=la, cum=cum, last=last, qs=qs, k=k, ecum=ecum, ekl=ekl, el=el,
                refs=refs, eq=eq, small=small, ri=ri, ci=ci)


GLA_FACTORED_MAX = 40.0


def _gla_factored_keys(c):
    return [c["k"] * jnp.exp(jnp.minimum(c["refs"][a] - c["cum"], GLA_FACTORED_MAX)) for a in range(N_SUB)]


def _causal_rows(c, a):
    return c["ci"][:SUB, :] <= c["ri"][:SUB, :] + a * SUB


def _gla_scores_factored(c, keys, h):
    sl = slice(GLA_DK * h, GLA_DK * (h + 1))
    blocks = []
    for a in range(N_SUB):
        qh = c["qs"][a * SUB:(a + 1) * SUB, sl] * c["eq"][a][:, sl]
        blocks.append(jnp.where(_causal_rows(c, a), _dot_nt(qh, keys[a][:, sl]), 0.0))
    return jnp.concatenate(blocks, axis=0)


def _gla_lag_weights(c):
    cum, row = c["cum"], c["row"]
    out = [jnp.ones((CHUNK, 256), F32)]
    for r in range(1, SUB):
        out.append(jnp.where((row % SUB) >= r, jnp.exp(jnp.minimum(cum - pltpu.roll(cum, r, 0), 0.0)), 0.0))
    return out


def _gla_pairwise_keys(c):
    return [None] + [c["k"] * jnp.exp(jnp.minimum(c["refs"][a] - c["cum"], 0.0)) for a in range(1, N_SUB)]


def _gla_scores_pairwise(c, lag_w, keys, h):
    sl = slice(GLA_DK * h, GLA_DK * (h + 1))
    qs, k = c["qs"][:, sl], c["k"][:, sl]
    ri, ci = c["ri"], c["ci"]
    p = jnp.zeros((CHUNK, CHUNK), F32)
    for r in range(SUB):
        kr = k if r == 0 else pltpu.roll(k, r, 0)
        pr = jnp.sum(qs * kr * lag_w[r][:, sl], axis=1, keepdims=True)
        p = p + jnp.where(ci == ri - r, pr, 0.0)
    blocks = [jnp.zeros((SUB, CHUNK), F32)]
    for a in range(1, N_SUB):
        qh = qs[a * SUB:(a + 1) * SUB, :] * c["eq"][a][:, sl]
        blocks.append(jnp.where(ci[:SUB, :] < a * SUB, _dot_nt(qh, keys[a][:, sl]), 0.0))
    return p + jnp.concatenate(blocks, axis=0)


def _gla_all_scores(c, p_scr):
    @pl.when(c["small"])
    def _():
        keys = _gla_factored_keys(c)
        for h in range(GLA_HEADS):
            p_scr[h] = _gla_scores_factored(c, keys, h)

    @pl.when(jnp.logical_not(c["small"]))
    def _():
        lag_w, keys = _gla_lag_weights(c), _gla_pairwise_keys(c)
        for h in range(GLA_HEADS):
            p_scr[h] = _gla_scores_pairwise(c, lag_w, keys, h)


def _gla_intra_bwd_factored(c, keys, dp, h):
    sl = slice(GLA_DK * h, GLA_DK * (h + 1))
    dq_rows = []
    dk = jnp.zeros((CHUNK, GLA_DK), F32)
    for a in range(N_SUB):
        eq = c["eq"][a][:, sl]
        qh = c["qs"][a * SUB:(a + 1) * SUB, sl] * eq
        dpa = jnp.where(_causal_rows(c, a), dp[a * SUB:(a + 1) * SUB, :], 0.0)
        dq_rows.append(_dot(dpa, keys[a][:, sl]) * eq)
        ek = jnp.exp(jnp.minimum(c["refs"][a][:, sl] - c["cum"][:, sl], GLA_FACTORED_MAX))
        dk = dk + _dot_tn(dpa, qh) * ek
    return jnp.concatenate(dq_rows, axis=0), dk


def _gla_intra_bwd_pairwise(c, lag_w, keys, dp, h):
    sl = slice(GLA_DK * h, GLA_DK * (h + 1))
    qs_h, k_h = c["qs"][:, sl], c["k"][:, sl]
    ri, ci = c["ri"], c["ci"]
    dq_rows = [jnp.zeros((SUB, GLA_DK), F32)]
    dk = jnp.zeros((CHUNK, GLA_DK), F32)
    for a in range(1, N_SUB):
        eq = c["eq"][a][:, sl]
        qh = qs_h[a * SUB:(a + 1) * SUB, :] * eq
        dpa = jnp.where(ci[:SUB, :] < a * SUB, dp[a * SUB:(a + 1) * SUB, :], 0.0)
        dq_rows.append(_dot(dpa, keys[a][:, sl]) * eq)
        ek = jnp.exp(jnp.minimum(c["refs"][a][:, sl] - c["cum"][:, sl], 0.0))
        dk = dk + _dot_tn(dpa, qh) * ek
    dq = jnp.concatenate(dq_rows, axis=0)
    for r in range(SUB):
        w = lag_w[r][:, sl]
        dpr = jnp.sum(jnp.where(ci == ri - r, dp, 0.0), axis=1, keepdims=True)
        kr = k_h if r == 0 else pltpu.roll(k_h, r, 0)
        dq = dq + dpr * kr * w
        back = dpr * qs_h * w
        dk = dk + (back if r == 0 else pltpu.roll(back, CHUNK - r, 0))
    return dq, dk


def _gla_all_intra_bwd(c, dps, p_scr, dq_scr, dk_scr):
    @pl.when(c["small"])
    def _():
        keys = _gla_factored_keys(c)
        outs = [_gla_intra_bwd_factored(c, keys, dps[h], h) for h in range(GLA_HEADS)]
        for h in range(GLA_HEADS):
            p_scr[h] = _gla_scores_factored(c, keys, h)
        dq_scr[...] = jnp.concatenate([o[0] for o in outs], axis=1)
        dk_scr[...] = jnp.concatenate([o[1] for o in outs], axis=1)

    @pl.when(jnp.logical_not(c["small"]))
    def _():
        lag_w, keys = _gla_lag_weights(c), _gla_pairwise_keys(c)
        outs = [_gla_intra_bwd_pairwise(c, lag_w, keys, dps[h], h) for h in range(GLA_HEADS)]
        for h in range(GLA_HEADS):
            p_scr[h] = _gla_scores_pairwise(c, lag_w, keys, h)
        dq_scr[...] = jnp.concatenate([o[0] for o in outs], axis=1)
        dk_scr[...] = jnp.concatenate([o[1] for o in outs], axis=1)


def _mixer_fwd(proj, cos2, sin2, w2p, gb, rnw, gnw, name, carried=()):
    n_carried = len(carried)

    def body(*refs):
        p_ref, c_ref, s_ref, w2_ref, gb_ref, rnw_ref, gnw_ref = refs[:7]
        x_refs, refs = refs[7:7 + n_carried], refs[7 + n_carried:]
        ocat_ref, mrg_ref, sr_out, sg_out = refs[:4]
        gathered_refs, refs = refs[4:4 + n_carried], refs[4 + n_carried:]
        sr, sg, p_scr = refs[:3]
        n = pl.program_id(0)
        if n_carried:
            start, forward, finish = _gather_phases(x_refs, gathered_refs, *refs[3:])
            pl.when(n == 0)(start)
            pl.when(n == N_CHUNKS // 2)(forward)

        @pl.when(n == 0)
        def _():
            sr[...] = jnp.zeros_like(sr)
            sg[...] = jnp.zeros_like(sg)

        sr_out[0] = sr[...]
        sg_out[0] = sg[...]
        cosv, sinv = c_ref[...], s_ref[...]

        for h in range(RET_HEADS):
            dmat, zeta, xi, gc = _ret_consts(h)
            hs = slice(128 * h, 128 * (h + 1))
            q = _rope(p_ref[:, O_RQ + 128 * h:O_RQ + 128 * (h + 1)], cosv, sinv)
            k = _rope(p_ref[:, O_RK + 128 * h:O_RK + 128 * (h + 1)], cosv, sinv) * (RET_DK ** -0.5)
            v = p_ref[:, O_RV + 128 * h:O_RV + 128 * (h + 1)]
            g = p_ref[:, O_RG + 128 * h:O_RG + 128 * (h + 1)]
            s_in = sr[h]
            a = _dot_nt(q, k) * dmat
            o = _dot(a, v) + _dot(q, s_in) * xi
            sr[h] = gc * s_in + _dot_tn(k * zeta, v)
            mu = jnp.mean(o, axis=-1, keepdims=True)
            xc = o - mu
            nrm = xc * lax.rsqrt(jnp.mean(xc * xc, axis=-1, keepdims=True) + EPS)
            ocat_ref[:, hs] = o
            mrg_ref[:, hs] = (nrm * rnw_ref[:, hs] * (g * _sigmoid(g))).astype(BF16)

        c = _gla_common(p_ref, w2_ref, gb_ref, n)
        _gla_all_scores(c, p_scr)
        lastcol = _dot_tn_exact_lhs(c["la"], jnp.ones((CHUNK, GLA_DV), F32))
        qe = c["qs"] * c["ecum"]
        kl = c["k"] * c["ekl"]
        for h in range(GLA_HEADS):
            sl = slice(GLA_DK * h, GLA_DK * (h + 1))
            hs = slice(512 + 128 * h, 512 + 128 * (h + 1))
            v = p_ref[:, O_GV + 128 * h:O_GV + 128 * (h + 1)]
            g = p_ref[:, O_GR + 128 * h:O_GR + 128 * (h + 1)]
            s_in = sg[h]
            o = _dot(p_scr[h], v) + _dot(qe[:, sl], s_in)
            sg[h] = jnp.exp(lastcol[GLA_DK * h:GLA_DK * (h + 1), :]) * s_in + _dot_tn(kl[:, sl], v)
            nrm = o * lax.rsqrt(jnp.mean(o * o, axis=-1, keepdims=True) + EPS)
            ocat_ref[:, hs] = o
            mrg_ref[:, hs] = (nrm * gnw_ref[:, 128 * h:128 * (h + 1)] * (g * _sigmoid(g))).astype(BF16)

        if n_carried:
            pl.when(n == N_CHUNKS - 1)(finish)

    const = lambda shape: pl.BlockSpec(shape, lambda n: (0,) * len(shape))
    anywhere = [pl.BlockSpec(memory_space=pl.ANY)] * n_carried
    return pl.pallas_call(
        body, name=name, grid=(N_CHUNKS,),
        in_specs=[pl.BlockSpec((CHUNK, IN_WP), lambda n: (n, 0)),
                  pl.BlockSpec((CHUNK, 128), lambda n: (n, 0)), pl.BlockSpec((CHUNK, 128), lambda n: (n, 0)),
                  const((128, 256)), const((1, 256)), const((1, 512)), const((1, 512))] + anywhere,
        out_specs=[pl.BlockSpec((CHUNK, D), lambda n: (n, 0)), pl.BlockSpec((CHUNK, D), lambda n: (n, 0)),
                   pl.BlockSpec((1, RET_HEADS, RET_DK, 128), lambda n: (n, 0, 0, 0)),
                   pl.BlockSpec((1, GLA_HEADS, GLA_DK, GLA_DV), lambda n: (n, 0, 0, 0))] + anywhere,
        out_shape=[jax.ShapeDtypeStruct((LP, D), F32), jax.ShapeDtypeStruct((LP, D), BF16),
                   jax.ShapeDtypeStruct((N_CHUNKS, RET_HEADS, RET_DK, 128), F32),
                   jax.ShapeDtypeStruct((N_CHUNKS, GLA_HEADS, GLA_DK, GLA_DV), F32)] + _gathered_shapes(carried),
        scratch_shapes=[pltpu.VMEM((RET_HEADS, RET_DK, 128), F32), pltpu.VMEM((GLA_HEADS, GLA_DK, GLA_DV), F32),
                        pltpu.VMEM((GLA_HEADS, CHUNK, CHUNK), F32)] + _exchange_sems(n_carried),
        compiler_params=_cparams("arbitrary"),
    )(proj, cos2, sin2, w2p, gb, rnw, gnw, *carried)


def _mixer_bwd(proj, ocat, dmrg, sr_all, sg_all, cos2, sin2, w2p, gb, rnw, gnw, name, carried=()):
    last_chunk = N_CHUNKS - 1
    n_carried = len(carried)

    def body(*refs):
        p_ref, ocat_ref, dm_ref, sr_ref, sg_ref, c_ref, s_ref, w2_ref, gb_ref, rnw_ref, gnw_ref = refs[:11]
        g_refs, refs = refs[11:11 + n_carried], refs[11 + n_carried:]
        dp_ref, dw2_ref, dgb_ref, drn_ref, dgn_ref = refs[:5]
        got_refs, refs = refs[5:5 + n_carried], refs[5 + n_carried:]
        dsr, dsg, p_scr, dq_scr, dk_scr = refs[:5]
        step = pl.program_id(0)
        n = last_chunk - step
        if n_carried:
            start, finish = _exchange_phases(g_refs, got_refs, *refs[5:])
            pl.when(step == 0)(start)

        @pl.when(step == 0)
        def _():
            dsr[...] = jnp.zeros_like(dsr)
            dsg[...] = jnp.zeros_like(dsg)
            dw2_ref[...] = jnp.zeros_like(dw2_ref)
            dgb_ref[...] = jnp.zeros_like(dgb_ref)
            drn_ref[...] = jnp.zeros_like(drn_ref)
            dgn_ref[...] = jnp.zeros_like(dgn_ref)

        cosv, sinv = c_ref[...], s_ref[...]
        row = lax.broadcasted_iota(jnp.int32, (CHUNK, 1), 0)
        real = ((n * CHUNK + row) >= PAD_ROWS).astype(F32)

        for h in range(RET_HEADS):
            dmat, zeta, xi, gc = _ret_consts(h)
            hs = slice(128 * h, 128 * (h + 1))
            q = _rope(p_ref[:, O_RQ + 128 * h:O_RQ + 128 * (h + 1)], cosv, sinv)
            k = _rope(p_ref[:, O_RK + 128 * h:O_RK + 128 * (h + 1)], cosv, sinv) * (RET_DK ** -0.5)
            v = p_ref[:, O_RV + 128 * h:O_RV + 128 * (h + 1)]
            g = p_ref[:, O_RG + 128 * h:O_RG + 128 * (h + 1)]
            o = ocat_ref[:, hs]
            dy = dm_ref[:, hs]
            wv = rnw_ref[:, hs]
            mu = jnp.mean(o, axis=-1, keepdims=True)
            xc = o - mu
            rs = lax.rsqrt(jnp.mean(xc * xc, axis=-1, keepdims=True) + EPS)
            nrm = xc * rs
            sgm = _sigmoid(g)
            sil = g * sgm
            drn_ref[0:1, hs] += jnp.sum(dy * nrm * sil, axis=0, keepdims=True)
            dgate = dy * nrm * wv * (sgm * (1.0 + g * (1.0 - sgm)))
            dn = dy * wv * sil
            do = rs * (dn - jnp.mean(dn, axis=-1, keepdims=True) - nrm * jnp.mean(dn * nrm, axis=-1, keepdims=True))
            s_in = sr_ref[0, h]
            ds_out = dsr[h]
            a = _dot_nt(q, k) * dmat
            da = _dot_nt(do, v) * dmat
            dox = do * xi
            dq = _dot(da, k) + _dot_nt(dox, s_in)
            dk = _dot_tn(da, q) + _dot_nt(v, ds_out) * zeta
            dv = _dot_tn(a, do) + _dot(k * zeta, ds_out)
            dsr[h] = gc * ds_out + _dot_tn(q, dox)
            dk = dk * (RET_DK ** -0.5)
            dp_ref[:, O_RQ + 128 * h:O_RQ + 128 * (h + 1)] = (_unrope(dq, cosv, sinv) * real).astype(BF16)
            dp_ref[:, O_RK + 128 * h:O_RK + 128 * (h + 1)] = (_unrope(dk, cosv, sinv) * real).astype(BF16)
            dp_ref[:, O_RV + 128 * h:O_RV + 128 * (h + 1)] = (dv * real).astype(BF16)
            dp_ref[:, O_RG + 128 * h:O_RG + 128 * (h + 1)] = (dgate * real).astype(BF16)

        c = _gla_common(p_ref, w2_ref, gb_ref, n)
        ri, ci = c["ri"], c["ci"]
        causal = ri >= ci
        triu = (ci >= ri).astype(F32)
        qe = c["qs"] * c["ecum"]
        kl = c["k"] * c["ekl"]
        lastcol = _dot_tn_exact_lhs(c["la"], jnp.ones((CHUNK, GLA_DV), F32))
        dla_heads, dq_heads, dk_heads = [], [], []
        dos, dps = [], []
        for h in range(GLA_HEADS):
            hs = slice(512 + 128 * h, 512 + 128 * (h + 1))
            v = p_ref[:, O_GV + 128 * h:O_GV + 128 * (h + 1)]
            g = p_ref[:, O_GR + 128 * h:O_GR + 128 * (h + 1)]
            o = ocat_ref[:, hs]
            dy = dm_ref[:, hs]
            wv = gnw_ref[:, 128 * h:128 * (h + 1)]
            rs = lax.rsqrt(jnp.mean(o * o, axis=-1, keepdims=True) + EPS)
            nrm = o * rs
            sgm = _sigmoid(g)
            sil = g * sgm
            dgn_ref[0:1, 128 * h:128 * (h + 1)] += jnp.sum(dy * nrm * sil, axis=0, keepdims=True)
            dgate = dy * nrm * wv * (sgm * (1.0 + g * (1.0 - sgm)))
            dn = dy * wv * sil
            do = rs * (dn - nrm * jnp.mean(dn * nrm, axis=-1, keepdims=True))
            dp_ref[:, O_GR + 128 * h:O_GR + 128 * (h + 1)] = (dgate * real).astype(BF16)
            dos.append(do)
            dps.append(jnp.where(causal, _dot_nt(do, v), 0.0))
        _gla_all_intra_bwd(c, dps, p_scr, dq_scr, dk_scr)
        dq_intra, dk_intra = dq_scr[...], dk_scr[...]
        for h in range(GLA_HEADS):
            sl = slice(GLA_DK * h, GLA_DK * (h + 1))
            v = p_ref[:, O_GV + 128 * h:O_GV + 128 * (h + 1)]
            do = dos[h]
            qs_h, k_h = c["qs"][:, sl], c["k"][:, sl]
            s_in = sg_ref[0, h]
            ds_out = dsg[h]
            el_col = jnp.exp(lastcol[GLA_DK * h:GLA_DK * (h + 1), :])
            dv = _dot_tn(p_scr[h], do) + _dot(kl[:, sl], ds_out)
            dqe = _dot_nt(do, s_in)
            dkl = _dot_nt(v, ds_out)
            dsg[h] = _dot_tn(qe[:, sl], do) + el_col * ds_out
            sd = s_in * ds_out
            sd_hi = sd.astype(BF16)
            sd_lo = (sd - sd_hi.astype(F32)).astype(BF16)
            ones8 = jnp.ones((8, GLA_DV), BF16)
            nt = (((1,), (1,)), ((), ()))
            d_el = (lax.dot_general(ones8, sd_hi, nt, preferred_element_type=F32)
                    + lax.dot_general(ones8, sd_lo, nt, preferred_element_type=F32))[0:1, :]
            dqs = dqe * c["ecum"][:, sl] + dq_intra[:, sl]
            dkk = dkl * c["ekl"][:, sl] + dk_intra[:, sl]
            d_last = jnp.sum(dkl * kl[:, sl], axis=0, keepdims=True) + d_el * c["el"][:, sl]
            dcum = qs_h * dqs - k_h * dkk + jnp.where(row == CHUNK - 1, d_last, 0.0)
            dla_heads.append(_dot_exact_rhs(triu, dcum))
            dq_heads.append(dqs * (GLA_DK ** -0.5))
            dk_heads.append(dkk)
            dp_ref[:, O_GV + 128 * h:O_GV + 128 * (h + 1)] = (dv * real).astype(BF16)

        dla = jnp.concatenate(dla_heads, axis=1)
        dp_ref[:, O_GQ:O_GQ + 256] = (jnp.concatenate(dq_heads, axis=1) * real).astype(BF16)
        dp_ref[:, O_GK:O_GK + 256] = (jnp.concatenate(dk_heads, axis=1) * real).astype(BF16)
        dz = dla * (1.0 / GLA_TAU) * _sigmoid(-c["z"]) * real
        ga = p_ref[:, O_GA:O_GA + 128]
        dp_ref[:, O_GA:O_GA + 128] = _dot_nt(dz, w2_ref[...]).astype(BF16)
        dp_ref[:, O_GA + 128:IN_WP] = jnp.zeros((CHUNK, IN_WP - O_GA - 128), BF16)
        dw2_ref[...] += _dot_tn(ga, dz)
        dgb_ref[0:1, :] += jnp.sum(dz, axis=0, keepdims=True)

        if n_carried:
            pl.when(step == last_chunk)(finish)

    const = lambda shape: pl.BlockSpec(shape, lambda s: (0,) * len(shape))
    rev = lambda s: (last_chunk - s, 0)
    anywhere = [pl.BlockSpec(memory_space=pl.ANY)] * n_carried
    return pl.pallas_call(
        body, name=name, grid=(N_CHUNKS,),
        in_specs=[pl.BlockSpec((CHUNK, IN_WP), rev), pl.BlockSpec((CHUNK, D), rev), pl.BlockSpec((CHUNK, D), rev),
                  pl.BlockSpec((1, RET_HEADS, RET_DK, 128), lambda s: (last_chunk - s, 0, 0, 0)),
                  pl.BlockSpec((1, GLA_HEADS, GLA_DK, GLA_DV), lambda s: (last_chunk - s, 0, 0, 0)),
                  pl.BlockSpec((CHUNK, 128), rev), pl.BlockSpec((CHUNK, 128), rev),
                  const((128, 256)), const((1, 256)), const((1, 512)), const((1, 512))] + anywhere,
        out_specs=[pl.BlockSpec((CHUNK, IN_WP), rev), const((128, 256)), const((8, 256)),
                   const((8, 512)), const((8, 512))] + anywhere,
        out_shape=[jax.ShapeDtypeStruct((LP, IN_WP), BF16), jax.ShapeDtypeStruct((128, 256), F32),
                   jax.ShapeDtypeStruct((8, 256), F32), jax.ShapeDtypeStruct((8, 512), F32),
                   jax.ShapeDtypeStruct((8, 512), F32)] + [jax.ShapeDtypeStruct(g.shape, g.dtype) for g in carried],
        scratch_shapes=[pltpu.VMEM((RET_HEADS, RET_DK, 128), F32), pltpu.VMEM((GLA_HEADS, GLA_DK, GLA_DV), F32),
                        pltpu.VMEM((GLA_HEADS, CHUNK, CHUNK), F32), pltpu.VMEM((CHUNK, 256), F32),
                        pltpu.VMEM((CHUNK, 256), F32)] + _exchange_sems(n_carried),
        compiler_params=_cparams("arbitrary"),
    )(proj, ocat, dmrg, sr_all, sg_all, cos2, sin2, w2p, gb, rnw, gnw, *carried)


def _all_gather(xs, name):
    n = len(xs)

    def body(*refs):
        start, forward, finish = _gather_phases(refs[:n], refs[n:2 * n], *refs[2 * n:])
        start()
        forward()
        finish()

    return pl.pallas_call(
        body, name=name,
        in_specs=[pl.BlockSpec(memory_space=pl.ANY)] * n,
        out_specs=[pl.BlockSpec(memory_space=pl.ANY)] * n,
        out_shape=_gathered_shapes(xs),
        scratch_shapes=_exchange_sems(n),
    )(*xs)


def _gathered_shapes(xs):
    return [jax.ShapeDtypeStruct((N_DEV,) + x.shape, x.dtype) for x in xs]


def _exchange_sems(n):
    if n == 0:
        return []
    return [pltpu.SemaphoreType.DMA((7 * n,)), pltpu.SemaphoreType.DMA((7 * n,)), pltpu.SemaphoreType.DMA((n,))]


def _gather_phases(x_refs, out_refs, send_sems, recv_sems, local_sems):
    n = len(x_refs)
    mx, my, mc = lax.axis_index("x"), lax.axis_index("y"), lax.axis_index("c")
    me, sibling = (mx, my, mc), (mx, my, 1 - mc)
    chips = [(1 - mx, my), (mx, 1 - my), (1 - mx, 1 - my)]

    def slot(a, px, py, pc):
        return out_refs[a].at[4 * px + 2 * py + pc]

    def copy(a, k, block, to, src=None):
        return pltpu.make_async_remote_copy(
            src_ref=slot(a, *block) if src is None else src, dst_ref=slot(a, *block),
            send_sem=send_sems.at[7 * a + k], recv_sem=recv_sems.at[7 * a + k],
            device_id=to, device_id_type=MESH_IDS)

    mine = [pltpu.make_async_copy(x_refs[a], slot(a, *me), local_sems.at[a]) for a in range(n)]
    first = []
    for a in range(n):
        first.append(copy(a, 0, me, sibling, src=x_refs[a]))
        first += [copy(a, 1 + j, me, (*chip, mc), src=x_refs[a]) for j, chip in enumerate(chips)]
    passed = [copy(a, 4 + j, (*chip, mc), sibling) for j, chip in enumerate(chips) for a in range(n)]

    def start():
        for cp in mine + first:
            cp.start()

    def forward():
        for j, chip in enumerate(chips):
            for a in range(n):
                copy(a, 1 + j, (*chip, mc), me).wait_recv()
                passed[j * n + a].start()

    def finish():
        for a in range(n):
            copy(a, 0, sibling, me).wait_recv()
            for j, chip in enumerate(chips):
                copy(a, 4 + j, (*chip, 1 - mc), me).wait_recv()
        for cp in first + passed:
            cp.wait_send()
        for cp in mine:
            cp.wait()

    return start, forward, finish


def _exchange_blocks(gs, name):
    n = len(gs)

    def body(*refs):
        start, finish = _exchange_phases(refs[:n], refs[n:2 * n], *refs[2 * n:])
        start()
        finish()

    return pl.pallas_call(
        body, name=name,
        in_specs=[pl.BlockSpec(memory_space=pl.ANY)] * n,
        out_specs=[pl.BlockSpec(memory_space=pl.ANY)] * n,
        out_shape=[jax.ShapeDtypeStruct(g.shape, g.dtype) for g in gs],
        scratch_shapes=_exchange_sems(n),
    )(*gs)


def _exchange_phases(g_refs, out_refs, send_sems, recv_sems, local_sems):
    n = len(g_refs)
    mx, my, mc = lax.axis_index("x"), lax.axis_index("y"), lax.axis_index("c")
    me = 4 * mx + 2 * my + mc
    mine = [pltpu.make_async_copy(g_refs[a].at[me], out_refs[a].at[me], local_sems.at[a]) for a in range(n)]
    copies = []
    for r in range(1, N_DEV):
        px, py, pc = mx ^ (r >> 2), my ^ ((r >> 1) & 1), mc ^ (r & 1)
        peer = 4 * px + 2 * py + pc
        for a in range(n):
            copies.append(pltpu.make_async_remote_copy(
                src_ref=g_refs[a].at[peer], dst_ref=out_refs[a].at[me],
                send_sem=send_sems.at[7 * a + r - 1], recv_sem=recv_sems.at[7 * a + r - 1],
                device_id=(px, py, pc), device_id_type=MESH_IDS))

    def start():
        for cp in mine + copies:
            cp.start()

    def finish():
        for cp in copies:
            cp.wait_recv()
        for cp in copies:
            cp.wait_send()
        for cp in mine:
            cp.wait()

    return start, finish


IN_SHARD = IN_W // N_DEV
IN_SHARD_P = 512
UP_SHARD = D_UP // N_DEV
UP_SHARD_P = 768
RELAYOUT_ROWS = 256


def _pieces_w_in():
    return [(k, 0, IN_SHARD * k, IN_SHARD) for k in range(N_DEV)]


def _pieces_ffn_up():
    pieces = []
    for k in range(N_DEV):
        n, end = UP_SHARD * k, UP_SHARD * (k + 1)
        while n < end:
            half, r = divmod(n, D_FF)
            blk, off = divmod(r, CONV_BLOCK)
            run = min(CONV_BLOCK - off, end - n)
            pieces.append((k, n - UP_SHARD * k, 2 * CONV_BLOCK * blk + CONV_BLOCK * half + off, run))
            n += run
    return pieces


def _assemble_block(load, spans, dst_block, rows):
    lo = 128 * dst_block
    lane = lax.broadcasted_iota(jnp.int32, (1, 128), 1)
    out = jnp.zeros((rows, 128), F32)
    for key, src_off, dst_off, length in spans:
        a, b = max(lo, dst_off), min(lo + 128, dst_off + length)
        s, s_end = src_off + (a - dst_off), src_off + (b - dst_off)
        d = a
        while s < s_end:
            e = min(s_end, 128 * (s // 128 + 1))
            blk = load(key, s // 128)
            shift = (d - s) % 128
            if shift:
                blk = pltpu.roll(blk, shift, 1)
            out = jnp.where((lane >= d - lo) & (lane < d - lo + (e - s)), blk, out)
            d += e - s
            s = e
    return out


def _shards_to_cols(shards, pieces, width, name):
    _, rows, _ = shards.shape
    tr = RELAYOUT_ROWS

    def body(s_ref, o_ref):
        load = lambda k, b: s_ref[k, :, 128 * b:128 * (b + 1)].astype(F32)
        for db in range(width // 128):
            o_ref[:, 128 * db:128 * (db + 1)] = _assemble_block(load, pieces, db, tr).astype(BF16)

    return pl.pallas_call(
        body, name=name, grid=(rows // tr,),
        in_specs=[pl.BlockSpec((N_DEV, tr, shards.shape[2]), lambda i: (0, i, 0))],
        out_specs=pl.BlockSpec((tr, width), lambda i: (i, 0)),
        out_shape=jax.ShapeDtypeStruct((rows, width), BF16),
        compiler_params=_cparams("parallel"),
    )(shards)


def _cols_to_shards(full, pieces, shard_width, name):
    rows, width = full.shape
    tr = RELAYOUT_ROWS

    def body(f_ref, o_ref):
        load = lambda _, b: f_ref[:, 128 * b:128 * (b + 1)].astype(F32)
        for k in range(N_DEV):
            spans = [(None, dst_off, src_off, length) for dev, src_off, dst_off, length in pieces if dev == k]
            for db in range(shard_width // 128):
                o_ref[k, :, 128 * db:128 * (db + 1)] = _assemble_block(load, spans, db, tr).astype(BF16)

    return pl.pallas_call(
        body, name=name, grid=(rows // tr,),
        in_specs=[pl.BlockSpec((tr, width), lambda i: (i, 0))],
        out_specs=pl.BlockSpec((N_DEV, tr, shard_width), lambda i: (0, i, 0)),
        out_shape=jax.ShapeDtypeStruct((N_DEV, rows, shard_width), BF16),
        compiler_params=_cparams("parallel"),
    )(full)


def _adamw(parts, w, m, v, rows_per_step, name):
    rows, cols = w.shape
    assert rows % rows_per_step == 0 and parts.shape == (N_DEV, rows, cols)

    def body(p_ref, w_ref, m_ref, v_ref, g_ref, d_ref, nm_ref, nv_ref):
        g = p_ref[0].astype(F32)
        for j in range(1, N_DEV):
            g = g + p_ref[j].astype(F32)
        m_new = ADAM_B1 * m_ref[...] + (1.0 - ADAM_B1) * g
        v_new = ADAM_B2 * v_ref[...] + (1.0 - ADAM_B2) * (g * g)
        m_hat = m_new / (1.0 - ADAM_B1 ** ADAM_STEP)
        v_hat = v_new / (1.0 - ADAM_B2 ** ADAM_STEP)
        g_ref[...] = g
        d_ref[...] = -ADAM_LR * (m_hat / (jnp.sqrt(v_hat) + ADAM_EPS) + ADAM_WD * w_ref[...])
        nm_ref[...] = m_new
        nv_ref[...] = v_new

    tile = pl.BlockSpec((rows_per_step, cols), lambda i: (i, 0))
    shape = jax.ShapeDtypeStruct((rows, cols), F32)
    return pl.pallas_call(
        body, name=name, grid=(rows // rows_per_step,),
        in_specs=[pl.BlockSpec((N_DEV, rows_per_step, cols), lambda i: (0, i, 0)), tile, tile, tile],
        out_specs=[tile, tile, tile, tile],
        out_shape=[shape, shape, shape, shape],
        compiler_params=_cparams("parallel"),
    )(parts, w, m, v)


BIG = (("w_in", (DEPTH, D, IN_W // N_DEV), 2), ("w_out", (DEPTH, D // N_DEV, D), 1),
       ("ffn_up", (DEPTH, D, D_UP // N_DEV), 2), ("ffn_down", (DEPTH, D_FF // N_DEV, D), 1))
SMALL = (("meta_tokens", (N_META, D // N_DEV), 1), ("gla_gate_w2", (DEPTH, GATE_RANK, 256 // N_DEV), 2),
         ("ffn_conv_w", (DEPTH, 3, D_UP // N_DEV), 2))
REPL = (("pre_mix_norm", (DEPTH, D)), ("gla_gate_b", (DEPTH, 256)), ("ret_norm_w", (DEPTH, 512)),
        ("gla_norm_w", (DEPTH, 512)), ("post_mix_norm", (DEPTH, D)), ("pre_ffn_norm", (DEPTH, D)),
        ("ffn_conv_b", (DEPTH, D_UP)), ("post_ffn_norm", (DEPTH, D)))
WEIGHT_ORDER = ("meta_tokens", "pre_mix_norm", "w_in", "gla_gate_w2", "gla_gate_b", "ret_norm_w", "gla_norm_w",
                "w_out", "post_mix_norm", "pre_ffn_norm", "ffn_up", "ffn_conv_w", "ffn_conv_b", "ffn_down",
                "post_ffn_norm")


def _size(shape):
    return math.prod(shape)


def _round_up(n, mult):
    return -(-n // mult) * mult


REPL_ROWS = _round_up(-(-sum(_size(s) for _, s in REPL) // LANES), 8)
SMALL_ROWS = _round_up(-(-sum(_size(s) for _, s, _ in SMALL) // LANES), 8)


def _pack(arrays, rows, dtype):
    flat = jnp.concatenate([a.reshape(-1).astype(dtype) for a in arrays])
    return jnp.pad(flat, (0, rows * LANES - flat.shape[0])).reshape(rows, LANES)


def _unpack(buf, shapes):
    flat = buf.reshape(-1)
    out, off = [], 0
    for shape in shapes:
        out.append(flat[off:off + _size(shape)].reshape(shape))
        off += _size(shape)
    return out


def _unshard(blocks, axis):
    moved = jnp.moveaxis(blocks, 0, axis)
    shape = list(moved.shape)
    shape[axis:axis + 2] = [shape[axis] * shape[axis + 1]]
    return moved.reshape(shape)


def _to_blocks(full, axis):
    shape = list(full.shape)
    shape[axis:axis + 1] = [N_DEV, shape[axis] // N_DEV]
    return jnp.moveaxis(full.reshape(shape), axis, 0)


def _interleave_cols(w):
    lead = w.shape[:-1]
    return jnp.swapaxes(w.reshape(lead + (2, N_CONV_BLOCKS, CONV_BLOCK)), -3, -2).reshape(lead + (D_UP,))


def _deinterleave_cols(w):
    lead = w.shape[:-1]
    return jnp.swapaxes(w.reshape(lead + (N_CONV_BLOCKS, 2, CONV_BLOCK)), -3, -2).reshape(lead + (D_UP,))


def _rope_tables():
    half = RET_DK // 2
    inv = ROPE_BASE ** (-jnp.arange(half, dtype=F32) / half)
    pos = jnp.arange(LP, dtype=F32) - float(PAD_ROWS)
    ang = pos[:, None] * inv[None, :]
    c, s = jnp.cos(ang), jnp.sin(ang)
    return jnp.concatenate([c, c], axis=1), jnp.concatenate([-s, s], axis=1)


def kernel(x, meta_tokens, pre_mix_norm, w_in, gla_gate_w2, gla_gate_b, ret_norm_w, gla_norm_w, w_out, post_mix_norm, pre_ffn_norm, ffn_up, ffn_conv_w, ffn_conv_b, ffn_down, post_ffn_norm, loss_target, m_meta_tokens, m_pre_mix_norm, m_w_in, m_gla_gate_w2, m_gla_gate_b, m_ret_norm_w, m_gla_norm_w, m_w_out, m_post_mix_norm, m_pre_ffn_norm, m_ffn_up, m_ffn_conv_w, m_ffn_conv_b, m_ffn_down, m_post_ffn_norm, v_meta_tokens, v_pre_mix_norm, v_w_in, v_gla_gate_w2, v_gla_gate_b, v_ret_norm_w, v_gla_norm_w, v_w_out, v_post_mix_norm, v_pre_ffn_norm, v_ffn_up, v_ffn_conv_w, v_ffn_conv_b, v_ffn_down, v_post_ffn_norm):
    weights = dict(meta_tokens=meta_tokens, pre_mix_norm=pre_mix_norm, w_in=w_in, gla_gate_w2=gla_gate_w2,
                   gla_gate_b=gla_gate_b, ret_norm_w=ret_norm_w, gla_norm_w=gla_norm_w, w_out=w_out,
                   post_mix_norm=post_mix_norm, pre_ffn_norm=pre_ffn_norm, ffn_up=ffn_up, ffn_conv_w=ffn_conv_w,
                   ffn_conv_b=ffn_conv_b, ffn_down=ffn_down, post_ffn_norm=post_ffn_norm)
    mom1 = dict(meta_tokens=m_meta_tokens, pre_mix_norm=m_pre_mix_norm, w_in=m_w_in, gla_gate_w2=m_gla_gate_w2,
                gla_gate_b=m_gla_gate_b, ret_norm_w=m_ret_norm_w, gla_norm_w=m_gla_norm_w, w_out=m_w_out,
                post_mix_norm=m_post_mix_norm, pre_ffn_norm=m_pre_ffn_norm, ffn_up=m_ffn_up,
                ffn_conv_w=m_ffn_conv_w, ffn_conv_b=m_ffn_conv_b, ffn_down=m_ffn_down, post_ffn_norm=m_post_ffn_norm)
    mom2 = dict(meta_tokens=v_meta_tokens, pre_mix_norm=v_pre_mix_norm, w_in=v_w_in, gla_gate_w2=v_gla_gate_w2,
                gla_gate_b=v_gla_gate_b, ret_norm_w=v_ret_norm_w, gla_norm_w=v_gla_norm_w, w_out=v_w_out,
                post_mix_norm=v_post_mix_norm, pre_ffn_norm=v_pre_ffn_norm, ffn_up=v_ffn_up,
                ffn_conv_w=v_ffn_conv_w, ffn_conv_b=v_ffn_conv_b, ffn_down=v_ffn_down, post_ffn_norm=v_post_ffn_norm)

    pad_cols = lambda a, width: jnp.pad(a, ((0, 0), (0, width - a.shape[1])))
    big_names = [n for n, _, _ in BIG]
    shard = {}
    for l in range(DEPTH):
        shard[l, "w_in"] = pad_cols(w_in[l].astype(BF16), IN_SHARD_P)
        shard[l, "w_out"] = w_out[l].astype(BF16)
        shard[l, "ffn_up"] = pad_cols(ffn_up[l].astype(BF16), UP_SHARD_P)
        shard[l, "ffn_down"] = ffn_down[l].astype(BF16)
    gathered = {(0, "w_in"): _all_gather([shard[0, "w_in"]], "gather_w_in_0")[0]}
    gather_in_mixer = {l: [(l, n) for n in big_names[1:]] + ([(l + 1, "w_in")] if l + 1 < DEPTH else [])
                       for l in range(DEPTH)}
    small = _all_gather([_pack([weights[n] for n, _, _ in SMALL], SMALL_ROWS, F32)], "gather_small_weights")[0]
    small_parts = _unpack_blocks(small, [s for _, s, _ in SMALL])
    full = {n: _unshard(p, ax) for (n, _, ax), p in zip(SMALL, small_parts)}
    w2p = jnp.pad(full["gla_gate_w2"], ((0, 0), (0, 128 - GATE_RANK), (0, 0)))
    cw8 = jnp.concatenate([_interleave_cols(full["ffn_conv_w"]), _interleave_cols(ffn_conv_b)[:, None, :],
                           jnp.zeros((DEPTH, 4, D_UP), F32)], axis=1)
    cos2, sin2 = _rope_tables()

    h = jnp.concatenate([jnp.zeros((PAD_ROWS, D), F32), full["meta_tokens"], x[0]], axis=0)
    target = jnp.concatenate([jnp.zeros((CHUNK, D), F32), loss_target[0]], axis=0)
    saved, layer_w = [], []
    for l in range(DEPTH):
        lw = dict(w_in=_shards_to_cols(gathered[l, "w_in"], _pieces_w_in(), IN_WP, f"w_in_cols_{l}"))
        a1 = _rmsnorm_fwd(h, pre_mix_norm[l:l + 1], f"pre_mix_norm_{l}")
        proj = _matmul(a1, lw["w_in"], out_dtype=F32, tm=TM, tn=1280, tk=D, name=f"in_proj_{l}")
        keys = gather_in_mixer.get(l, [])
        ocat, merged, sr_all, sg_all, *got = _mixer_fwd(proj, cos2, sin2, w2p[l], gla_gate_b[l:l + 1],
                                                        ret_norm_w[l:l + 1], gla_norm_w[l:l + 1], f"mixer_fwd_{l}",
                                                        carried=[shard[key] for key in keys])
        gathered.update(zip(keys, got))
        lw["w_out"] = gathered[l, "w_out"].reshape(D, D)
        lw["w_up"] = _shards_to_cols(gathered[l, "ffn_up"], _pieces_ffn_up(), D_UP, f"ffn_up_cols_{l}")
        lw["w_down"] = gathered[l, "ffn_down"].reshape(D_FF, D)
        layer_w.append(lw)
        m = _matmul(merged, lw["w_out"], out_dtype=F32, tm=TM, tn=D, tk=D, name=f"out_proj_{l}")
        h1 = _resid_norm(h, m, post_mix_norm[l:l + 1], f"post_mix_norm_{l}")
        a2 = _rmsnorm_fwd(h1, pre_ffn_norm[l:l + 1], f"pre_ffn_norm_{l}")
        u = _matmul(a2, lw["w_up"], out_dtype=BF16, tm=TM, tn=1408, tk=D, name=f"ffn_up_{l}")
        act = _conv_act_fwd(u, cw8[l], f"ffn_conv_act_{l}")
        f = _matmul(act, lw["w_down"], out_dtype=F32, tm=TM, tn=D, tk=D_FF, name=f"ffn_down_{l}")
        h2 = _resid_norm(h1, f, post_ffn_norm[l:l + 1], f"post_ffn_norm_{l}")
        saved.append(dict(h=h, a1=a1, proj=proj, ocat=ocat, merged=merged, sr=sr_all, sg=sg_all, m=m, h1=h1,
                          a2=a2, u=u, act=act, f=f))
        h = h2

    dh, loss_acc = _loss_head(h, target, "loss_head")
    loss = lax.psum(loss_acc[0, 0], ("x", "y", "c"))

    kinds = ("grad", "delta", "new_m", "new_v")
    grads = {n: [None] * DEPTH for n in WEIGHT_ORDER if n != "meta_tokens" and n not in big_names}
    pending, parts = [], {}
    for l in reversed(range(DEPTH)):
        s, lw = saved[l], layer_w[l]
        df, g_post_ffn = _norm_bwd(dh, s["f"], post_ffn_norm[l:l + 1], None, BF16, f"post_ffn_norm_bwd_{l}")
        dact = _matmul(df, lw["w_down"], tb=True, out_dtype=BF16, tm=TM, tn=D_FF, tk=D, name=f"ffn_down_dx_{l}")
        g_down = _matmul(s["act"], df, ta=True, out_dtype=BF16, tm=D_FF // 2, tn=D, tk=TK_ROWS, name=f"ffn_down_dw_{l}")
        du, dcw = _conv_act_bwd(dact, s["u"], cw8[l], f"ffn_conv_act_bwd_{l}")
        da2 = _matmul(du, lw["w_up"], tb=True, out_dtype=F32, tm=TM, tn=D, tk=1408, name=f"ffn_up_dx_{l}")
        g_up = _matmul(s["a2"], du, ta=True, out_dtype=BF16, tm=D, tn=1408, tk=TK_ROWS, name=f"ffn_up_dw_{l}")
        dh1, g_pre_ffn = _norm_bwd(da2, s["h1"], pre_ffn_norm[l:l + 1], dh, F32, f"pre_ffn_norm_bwd_{l}")
        dm, g_post_mix = _norm_bwd(dh1, s["m"], post_mix_norm[l:l + 1], None, BF16, f"post_mix_norm_bwd_{l}")
        dmerged = _matmul(dm, lw["w_out"], tb=True, out_dtype=F32, tm=TM, tn=D, tk=D, name=f"out_proj_dx_{l}")
        g_out = _matmul(s["merged"], dm, ta=True, out_dtype=BF16, tm=D, tn=D, tk=TK_ROWS, name=f"out_proj_dw_{l}")
        pending += [((l, "ffn_down"), g_down.reshape(N_DEV, D_FF // N_DEV, D)),
                    ((l, "ffn_up"), _cols_to_shards(g_up, _pieces_ffn_up(), UP_SHARD_P, f"ffn_up_grad_shards_{l}")),
                    ((l, "w_out"), g_out.reshape(N_DEV, D // N_DEV, D))]
        dproj, g_w2, g_gb, g_rn, g_gn, *got = _mixer_bwd(s["proj"], s["ocat"], dmerged, s["sr"], s["sg"], cos2, sin2,
                                                         w2p[l], gla_gate_b[l:l + 1], ret_norm_w[l:l + 1],
                                                         gla_norm_w[l:l + 1], f"mixer_bwd_{l}",
                                                         carried=[blocks for _, blocks in pending])
        parts.update(zip([key for key, _ in pending], got))
        da1 = _matmul(dproj, lw["w_in"], tb=True, out_dtype=F32, tm=TM, tn=D, tk=1280, name=f"in_proj_dx_{l}")
        g_in = _matmul(s["a1"], dproj, ta=True, out_dtype=BF16, tm=D, tn=1280, tk=TK_ROWS, name=f"in_proj_dw_{l}")
        pending = [((l, "w_in"), _cols_to_shards(g_in, _pieces_w_in(), IN_SHARD_P, f"w_in_grad_shards_{l}"))]
        dh, g_pre_mix = _norm_bwd(da1, s["h"], pre_mix_norm[l:l + 1], dh1, F32, f"pre_mix_norm_bwd_{l}")
        grads["post_ffn_norm"][l] = g_post_ffn[0]
        grads["ffn_conv_w"][l] = _deinterleave_cols(dcw[0:3])
        grads["ffn_conv_b"][l] = _deinterleave_cols(dcw[3])
        grads["pre_ffn_norm"][l] = g_pre_ffn[0]
        grads["post_mix_norm"][l] = g_post_mix[0]
        grads["gla_gate_w2"][l] = g_w2[:GATE_RANK]
        grads["gla_gate_b"][l] = g_gb[0]
        grads["ret_norm_w"][l] = g_rn[0]
        grads["gla_norm_w"][l] = g_gn[0]
        grads["pre_mix_norm"][l] = g_pre_mix[0]
    local = {n: jnp.stack(v) for n, v in grads.items()}
    local["meta_tokens"] = dh[PAD_ROWS:CHUNK]
    grad_x = dh[CHUNK:][None]

    blocks = jnp.concatenate([_to_blocks(local[n], ax).reshape(N_DEV, -1) for n, _, ax in SMALL], axis=1)
    blocks = jnp.pad(blocks, ((0, 0), (0, SMALL_ROWS * LANES - blocks.shape[1]))).reshape(N_DEV, SMALL_ROWS, LANES)
    *got, small_grad_parts = _exchange_blocks([b for _, b in pending] + [blocks], "exchange_last_grads")
    parts.update(zip([key for key, _ in pending], got))

    widths = dict(w_in=IN_SHARD_P, w_out=D, ffn_up=UP_SHARD_P, ffn_down=D)
    steps = dict(w_in=256, w_out=D // N_DEV, ffn_up=256, ffn_down=D_FF // N_DEV // 2)
    big_out = {kind: {n: [None] * DEPTH for n in big_names} for kind in kinds}
    for l in range(DEPTH):
        for n in big_names:
            mine = [pad_cols(d[n][l], widths[n]) for d in (weights, mom1, mom2)]
            results = _adamw(parts[l, n], *mine, steps[n], f"adamw_{n}_{l}")
            for kind, r in zip(kinds, results):
                big_out[kind][n][l] = r[:, :weights[n].shape[2]]
    out = {kind: {n: jnp.stack(v) for n, v in big_out[kind].items()} for kind in kinds}
    shard_shapes = [s for _, s, _ in SMALL]
    packed = [_pack([d[n] for n, _, _ in SMALL], SMALL_ROWS, F32) for d in (weights, mom1, mom2)]
    results = _adamw(small_grad_parts, *packed, SMALL_ROWS, "adamw_small_sharded")
    for kind, buf in zip(kinds, results):
        out[kind].update(zip([n for n, _, _ in SMALL], _unpack(buf, shard_shapes)))

    repl_parts = _all_gather([_pack([local[n] for n, _ in REPL], REPL_ROWS, F32)], "gather_small_grads")[0]
    packed = [_pack([d[n] for n, _ in REPL], REPL_ROWS, F32) for d in (weights, mom1, mom2)]
    results = _adamw(repl_parts, *packed, REPL_ROWS, "adamw_replicated")
    repl_shapes = [s for _, s in REPL]
    for kind, buf in zip(kinds, results):
        out[kind].update(zip([n for n, _ in REPL], _unpack(buf, repl_shapes)))

    return (loss, grad_x, *[out["grad"][n] for n in WEIGHT_ORDER], *[out["delta"][n] for n in WEIGHT_ORDER],
            *[out["new_m"][n] for n in WEIGHT_ORDER], *[out["new_v"][n] for n in WEIGHT_ORDER])


def _unpack_blocks(gathered, shapes):
    flat = gathered.reshape(N_DEV, -1)
    out, off = [], 0
    for shape in shapes:
        out.append(flat[:, off:off + _size(shape)].reshape((N_DEV,) + shape))
        off += _size(shape)
    return out
```

```python
import math

import jax
import jax.numpy as jnp
from jax import lax
from jax.experimental import pallas as pl
from jax.experimental.pallas import tpu as pltpu

F32 = jnp.float32
BF16 = jnp.bfloat16

D = 1024
SEQ = 8192
DEPTH = 2
N_META = 16
CHUNK = 64
SUB = 16
N_SUB = CHUNK // SUB
PAD_ROWS = CHUNK - N_META
LP = SEQ + CHUNK
N_CHUNKS = LP // CHUNK
RET_HEADS = 4
RET_DK = 128
GLA_HEADS = 4
GLA_DK = 64
GLA_DV = 128
GLA_TAU = 16.0
GATE_RANK = 16
IN_W = 3600
IN_WP = 3840
D_FF = 2816
D_UP = 2 * D_FF
CONV_BLOCK = 256
N_CONV_BLOCKS = D_FF // CONV_BLOCK
ROPE_BASE = 10000.0
EPS = 1e-6
N_DEV = 8
LANES = 1024

O_RQ, O_RK, O_RV, O_RG = 0, 512, 1024, 1536
O_GQ, O_GK, O_GV, O_GR, O_GA = 2048, 2304, 2560, 3072, 3584

ADAM_LR = 0.001
ADAM_B1 = 0.9
ADAM_B2 = 0.999
ADAM_EPS = 1e-08
ADAM_WD = 0.01
ADAM_STEP = 10

VMEM_LIMIT = 56 * 1024 * 1024
MESH_IDS = pl.DeviceIdType.MESH


def _row_tile(rows, limit):
    best = 16
    for t in range(16, min(rows, limit) + 1, 16):
        if rows % t == 0:
            best = t
    return best


TM = _row_tile(LP, 688)
TK_ROWS = _row_tile(LP, 1376)


def _cparams(*sem):
    return pltpu.CompilerParams(dimension_semantics=sem, vmem_limit_bytes=VMEM_LIMIT)


def _dot(a, b):
    return jnp.dot(a.astype(BF16), b.astype(BF16), preferred_element_type=F32)


def _dot_nt(a, b):
    return lax.dot_general(a.astype(BF16), b.astype(BF16), (((1,), (1,)), ((), ())), preferred_element_type=F32)


def _dot_tn(a, b):
    return lax.dot_general(a.astype(BF16), b.astype(BF16), (((0,), (0,)), ((), ())), preferred_element_type=F32)


def _split3(x):
    hi = x.astype(BF16)
    r1 = x - hi.astype(F32)
    mid = r1.astype(BF16)
    lo = (r1 - mid.astype(F32)).astype(BF16)
    return hi, mid, lo


def _dot_exact_rhs(t, x):
    hi, mid, lo = _split3(x)
    t = t.astype(BF16)
    return (jnp.dot(t, hi, preferred_element_type=F32) + jnp.dot(t, mid, preferred_element_type=F32)
            + jnp.dot(t, lo, preferred_element_type=F32))


def _dot_tn_exact_lhs(x, ones):
    dims = (((0,), (0,)), ((), ()))
    hi, mid, lo = _split3(x)
    ones = ones.astype(BF16)
    return (lax.dot_general(hi, ones, dims, preferred_element_type=F32)
            + lax.dot_general(mid, ones, dims, preferred_element_type=F32)
            + lax.dot_general(lo, ones, dims, preferred_element_type=F32))


def _sigmoid(x):
    return 1.0 / (1.0 + jnp.exp(-x))


def _matmul(a, b, *, ta=False, tb=False, out_dtype, tm, tn, tk, name):
    m = a.shape[1] if ta else a.shape[0]
    k = a.shape[0] if ta else a.shape[1]
    n = b.shape[0] if tb else b.shape[1]
    assert (b.shape[1] if tb else b.shape[0]) == k
    assert m % tm == 0 and n % tn == 0 and k % tk == 0, (name, m, n, k, tm, tn, tk)
    nk = k // tk
    a_spec = pl.BlockSpec((tk, tm), lambda i, j, kk: (kk, i)) if ta else pl.BlockSpec((tm, tk), lambda i, j, kk: (i, kk))
    b_spec = pl.BlockSpec((tn, tk), lambda i, j, kk: (j, kk)) if tb else pl.BlockSpec((tk, tn), lambda i, j, kk: (kk, j))
    dims = (((0 if ta else 1,), (1 if tb else 0,)), ((), ()))

    def body(a_ref, b_ref, o_ref, *acc):
        prod = lax.dot_general(a_ref[...].astype(BF16), b_ref[...].astype(BF16), dims, preferred_element_type=F32)
        if nk == 1:
            o_ref[...] = prod.astype(out_dtype)
            return
        acc_ref, = acc
        kk = pl.program_id(2)

        @pl.when(kk == 0)
        def _():
            acc_ref[...] = prod

        @pl.when(kk > 0)
        def _():
            acc_ref[...] += prod

        @pl.when(kk == nk - 1)
        def _():
            o_ref[...] = acc_ref[...].astype(out_dtype)

    return pl.pallas_call(
        body, name=name, grid=(m // tm, n // tn, nk),
        in_specs=[a_spec, b_spec],
        out_specs=pl.BlockSpec((tm, tn), lambda i, j, kk: (i, j)),
        out_shape=jax.ShapeDtypeStruct((m, n), out_dtype),
        scratch_shapes=[pltpu.VMEM((tm, tn), F32)] if nk > 1 else [],
        compiler_params=_cparams("parallel", "parallel", "arbitrary"),
    )(a, b)


def _rmsnorm_fwd(x, w, name):
    def body(x_ref, w_ref, o_ref):
        xv = x_ref[...]
        r = lax.rsqrt(jnp.mean(xv * xv, axis=-1, keepdims=True) + EPS)
        o_ref[...] = (xv * r * w_ref[...]).astype(BF16)

    return pl.pallas_call(
        body, name=name, grid=(LP // TM,),
        in_specs=[pl.BlockSpec((TM, D), lambda i: (i, 0)), pl.BlockSpec((1, D), lambda i: (0, 0))],
        out_specs=pl.BlockSpec((TM, D), lambda i: (i, 0)),
        out_shape=jax.ShapeDtypeStruct((LP, D), BF16),
        compiler_params=_cparams("parallel"),
    )(x, w)


def _resid_norm(h, m, w, name):
    def body(h_ref, m_ref, w_ref, o_ref):
        mv = m_ref[...]
        r = lax.rsqrt(jnp.mean(mv * mv, axis=-1, keepdims=True) + EPS)
        row = pl.program_id(0) * TM + lax.broadcasted_iota(jnp.int32, (TM, 1), 0)
        o_ref[...] = h_ref[...] + jnp.where(row >= PAD_ROWS, mv * r * w_ref[...], 0.0)

    return pl.pallas_call(
        body, name=name, grid=(LP // TM,),
        in_specs=[pl.BlockSpec((TM, D), lambda i: (i, 0)), pl.BlockSpec((TM, D), lambda i: (i, 0)),
                  pl.BlockSpec((1, D), lambda i: (0, 0))],
        out_specs=pl.BlockSpec((TM, D), lambda i: (i, 0)),
        out_shape=jax.ShapeDtypeStruct((LP, D), F32),
        compiler_params=_cparams("parallel"),
    )(h, m, w)


def _norm_bwd(dy, x, w, resid, out_dtype, name):
    has_resid = resid is not None

    def body(*refs):
        if has_resid:
            dy_ref, x_ref, w_ref, r_ref, dx_ref, dw_ref = refs
        else:
            dy_ref, x_ref, w_ref, dx_ref, dw_ref = refs
        i = pl.program_id(0)

        @pl.when(i == 0)
        def _():
            dw_ref[...] = jnp.zeros_like(dw_ref)

        row = i * TM + lax.broadcasted_iota(jnp.int32, (TM, 1), 0)
        dyv = jnp.where(row >= PAD_ROWS, dy_ref[...], 0.0)
        xv = x_ref[...]
        r = lax.rsqrt(jnp.mean(xv * xv, axis=-1, keepdims=True) + EPS)
        g = dyv * w_ref[...]
        dx = r * g - xv * (r * r * r * jnp.mean(g * xv, axis=-1, keepdims=True))
        if has_resid:
            dx = dx + r_ref[...]
        dx_ref[...] = dx.astype(out_dtype)
        dw_ref[0:1, :] += jnp.sum(dyv * xv * r, axis=0, keepdims=True)

    tile = pl.BlockSpec((TM, D), lambda i: (i, 0))
    in_specs = [tile, tile, pl.BlockSpec((1, D), lambda i: (0, 0))] + ([tile] if has_resid else [])
    args = (dy, x, w) + ((resid,) if has_resid else ())
    return pl.pallas_call(
        body, name=name, grid=(LP // TM,),
        in_specs=in_specs,
        out_specs=[tile, pl.BlockSpec((8, D), lambda i: (0, 0))],
        out_shape=[jax.ShapeDtypeStruct((LP, D), out_dtype), jax.ShapeDtypeStruct((8, D), F32)],
        compiler_params=_cparams("arbitrary"),
    )(*args)


def _loss_head(y, target, name):
    def body(y_ref, t_ref, dy_ref, loss_ref):
        i = pl.program_id(0)

        @pl.when(i == 0)
        def _():
            loss_ref[...] = jnp.zeros_like(loss_ref)

        row = i * TM + lax.broadcasted_iota(jnp.int32, (TM, 1), 0)
        diff = jnp.where(row >= CHUNK, y_ref[...] - t_ref[...], 0.0)
        dy_ref[...] = diff * (1.0 / D)
        loss_ref[...] += (0.5 / D) * jnp.sum(diff * diff)

    tile = pl.BlockSpec((TM, D), lambda i: (i, 0))
    return pl.pallas_call(
        body, name=name, grid=(LP // TM,),
        in_specs=[tile, tile],
        out_specs=[tile, pl.BlockSpec((8, 128), lambda i: (0, 0))],
        out_shape=[jax.ShapeDtypeStruct((LP, D), F32), jax.ShapeDtypeStruct((8, 128), F32)],
        compiler_params=_cparams("arbitrary"),
    )(y, target)


GELU_C = math.sqrt(2.0 / math.pi)
GELU_K = 0.044715
STRIP = 16


def _shift_down(x, prev8, rows):
    row = lax.broadcasted_iota(jnp.int32, (rows, 1), 0)
    p1 = pltpu.roll(prev8, 1, 0)
    p2 = pltpu.roll(prev8, 2, 0)
    x1 = jnp.where(row == 0, p1[0:1, :], pltpu.roll(x, 1, 0))
    x2 = jnp.where(row == 0, p2[0:1, :], jnp.where(row == 1, p2[1:2, :], pltpu.roll(x, 2, 0)))
    return x1, x2


def _conv_act_fwd(u, cw8, name):
    n_rows = LP // TM
    cb2 = 2 * CONV_BLOCK

    def body(u_ref, cw_ref, conv_ref, act_ref, carry_ref):
        i = pl.program_id(1)

        @pl.when(i == 0)
        def _():
            carry_ref[...] = jnp.zeros_like(carry_ref)

        x = u_ref[...].astype(F32)
        x1, x2 = _shift_down(x, carry_ref[...], TM)
        conv = cw_ref[3:4, :] + x2 * cw_ref[0:1, :] + x1 * cw_ref[1:2, :] + x * cw_ref[2:3, :]
        conv_ref[...] = conv.astype(BF16)
        a = conv[:, :CONV_BLOCK]
        g = conv[:, CONV_BLOCK:]
        t = jnp.tanh(GELU_C * (a + GELU_K * a * a * a))
        act_ref[...] = (0.5 * a * (1.0 + t) * g).astype(BF16)
        carry_ref[...] = x[TM - 8:TM, :]

    return pl.pallas_call(
        body, name=name, grid=(N_CONV_BLOCKS, n_rows),
        in_specs=[pl.BlockSpec((TM, cb2), lambda j, i: (i, j)), pl.BlockSpec((8, cb2), lambda j, i: (0, j))],
        out_specs=[pl.BlockSpec((TM, cb2), lambda j, i: (i, j)), pl.BlockSpec((TM, CONV_BLOCK), lambda j, i: (i, j))],
        out_shape=[jax.ShapeDtypeStruct((LP, D_UP), BF16), jax.ShapeDtypeStruct((LP, D_FF), BF16)],
        scratch_shapes=[pltpu.VMEM((8, cb2), F32)],
        compiler_params=_cparams("arbitrary", "arbitrary"),
    )(u, cw8)


def _conv_act_bwd(dact, conv, u, cw8, name):
    n_rows = LP // TM
    cb2 = 2 * CONV_BLOCK
    n_strips = TM // STRIP

    def body(dact_ref, conv_ref, u_ref, cw_ref, du_ref, dcw_ref, carry_ref):
        i = pl.program_id(1)

        @pl.when(i == 0)
        def _():
            dcw_ref[...] = jnp.zeros_like(dcw_ref)
            carry_ref[...] = jnp.zeros_like(carry_ref)

        w0, w1, w2 = cw_ref[0:1, :], cw_ref[1:2, :], cw_ref[2:3, :]
        row = lax.broadcasted_iota(jnp.int32, (STRIP, 1), 0)
        fold = lambda z: z[:8, :] + z[8:, :]

        def strip(k, carry):
            n1, n2, s0, s1, s2, s3 = carry
            r0 = pl.multiple_of((n_strips - 1 - k) * STRIP, STRIP)
            cv = conv_ref[pl.ds(r0, STRIP), :].astype(F32)
            a = cv[:, :CONV_BLOCK]
            g = cv[:, CONV_BLOCK:]
            t = jnp.tanh(GELU_C * (a + GELU_K * a * a * a))
            gel = 0.5 * a * (1.0 + t)
            dgel = 0.5 * (1.0 + t) + 0.5 * a * (1.0 - t * t) * (GELU_C * (1.0 + 3.0 * GELU_K * a * a))
            dav = dact_ref[pl.ds(r0, STRIP), :].astype(F32)
            dconv = jnp.concatenate([dav * g * dgel, dav * gel], axis=1)
            u1 = pltpu.roll(dconv, STRIP - 1, 0)
            u2 = pltpu.roll(dconv, STRIP - 2, 0)
            d1 = jnp.where(row >= STRIP - 1, n1, u1)
            d2 = jnp.where(row >= STRIP - 2, n2, u2)
            du_ref[pl.ds(r0, STRIP), :] = (dconv * w2 + d1 * w1 + d2 * w0).astype(BF16)
            x = u_ref[pl.ds(r0, STRIP), :].astype(F32)
            return (u1, u2, s0 + fold(d2 * x), s1 + fold(d1 * x), s2 + fold(dconv * x), s3 + fold(dconv))

        below = carry_ref[...]
        zero = jnp.zeros((8, cb2), F32)
        init = (pltpu.roll(below, STRIP - 1, 0), pltpu.roll(below, STRIP - 2, 0), zero, zero, zero, zero)
        u1, _, s0, s1, s2, s3 = lax.fori_loop(0, n_strips, strip, init)
        carry_ref[...] = pltpu.roll(u1, 1, 0)
        dcw_ref[0:1, :] += jnp.sum(s0, axis=0, keepdims=True)
        dcw_ref[1:2, :] += jnp.sum(s1, axis=0, keepdims=True)
        dcw_ref[2:3, :] += jnp.sum(s2, axis=0, keepdims=True)
        dcw_ref[3:4, :] += jnp.sum(s3, axis=0, keepdims=True)

    rev = lambda j, i: (n_rows - 1 - i, j)
    return pl.pallas_call(
        body, name=name, grid=(N_CONV_BLOCKS, n_rows),
        in_specs=[pl.BlockSpec((TM, CONV_BLOCK), rev), pl.BlockSpec((TM, cb2), rev), pl.BlockSpec((TM, cb2), rev),
                  pl.BlockSpec((8, cb2), lambda j, i: (0, j))],
        out_specs=[pl.BlockSpec((TM, cb2), rev), pl.BlockSpec((8, cb2), lambda j, i: (0, j))],
        out_shape=[jax.ShapeDtypeStruct((LP, D_UP), BF16), jax.ShapeDtypeStruct((8, D_UP), F32)],
        scratch_shapes=[pltpu.VMEM((STRIP, cb2), F32)],
        compiler_params=_cparams("arbitrary", "arbitrary"),
    )(dact, conv, u, cw8)


def _ret_consts(h):
    lg = math.log(1.0 - 2.0 ** (-5.0 - h))
    ri = lax.broadcasted_iota(jnp.int32, (CHUNK, CHUNK), 0)
    ci = lax.broadcasted_iota(jnp.int32, (CHUNK, CHUNK), 1)
    diff = (ri - ci).astype(F32)
    dmat = jnp.where(diff >= 0, jnp.exp(lg * jnp.maximum(diff, 0.0)), 0.0)
    rowf = lax.broadcasted_iota(jnp.int32, (CHUNK, 1), 0).astype(F32)
    zeta = jnp.exp(lg * (CHUNK - 1.0 - rowf))
    xi = jnp.exp(lg * (rowf + 1.0))
    return dmat, zeta, xi, math.exp(lg * CHUNK)


def _rope(t, cosv, sinv):
    return t * cosv + pltpu.roll(t, RET_DK // 2, 1) * sinv


def _unrope(d, cosv, sinv):
    return d * cosv + pltpu.roll(d * sinv, RET_DK // 2, 1)


def _gla_common(p_ref, w2_ref, gb_ref, chunk):
    row = lax.broadcasted_iota(jnp.int32, (CHUNK, 1), 0)
    real = (chunk * CHUNK + row) >= PAD_ROWS
    ga = p_ref[:, O_GA:O_GA + 128]
    z = _dot(ga, w2_ref[...]) + gb_ref[...]
    la = (jnp.minimum(z, 0.0) - jnp.log(1.0 + jnp.exp(-jnp.abs(z)))) * (1.0 / GLA_TAU)
    la = jnp.where(real, la, 0.0)
    ri = lax.broadcasted_iota(jnp.int32, (CHUNK, CHUNK), 0)
    ci = lax.broadcasted_iota(jnp.int32, (CHUNK, CHUNK), 1)
    tril = (ri >= ci).astype(F32)
    cum = _dot_exact_rhs(tril, la)
    last = cum[CHUNK - 1:CHUNK, :]
    qs = p_ref[:, O_GQ:O_GQ + 256] * (GLA_DK ** -0.5)
    k = p_ref[:, O_GK:O_GK + 256]
    ecum = jnp.exp(cum)
    ekl = jnp.exp(last - cum)
    el = jnp.exp(last)
    refs = [jnp.zeros((1, 256), F32)] + [cum[a * SUB - 1:a * SUB, :] for a in range(1, N_SUB)]
    eq = [jnp.exp(cum[a * SUB:(a + 1) * SUB, :] - refs[a]) for a in range(N_SUB)]
    spread = refs[0] - cum[SUB - 1:SUB, :]
    for a in range(1, N_SUB):
        spread = jnp.maximum(spread, refs[a] - cum[(a + 1) * SUB - 1:(a + 1) * SUB, :])
    small = jnp.max(spread) <= GLA_FACTORED_MAX
    return dict(real=real, row=row, z=z, la=la, cum=cum, last=last, qs=qs, k=k, ecum=ecum, ekl=ekl, el=el,
                refs=refs, eq=eq, small=small, ri=ri, ci=ci)


GLA_FACTORED_MAX = 40.0


def _head_block_mask():
    r = lax.broadcasted_iota(jnp.int32, (CHUNK, 256), 0)
    col = lax.broadcasted_iota(jnp.int32, (CHUNK, 256), 1)
    return (r // SUB) == (col // GLA_DK)


def _gla_factored(c):
    mask = _head_block_mask()
    eks, keys, queries = [], [], []
    for a in range(N_SUB):
        ek = jnp.exp(jnp.minimum(c["refs"][a] - c["cum"], GLA_FACTORED_MAX))
        qh = c["qs"][a * SUB:(a + 1) * SUB, :] * c["eq"][a]
        eks.append(ek)
        keys.append(c["k"] * ek)
        queries.append(jnp.where(mask, jnp.concatenate([qh] * GLA_HEADS, axis=0), 0.0))
    return eks, keys, queries


def _gla_scores_factored(c, factored, p_scr):
    _, keys, queries = factored
    for a in range(N_SUB):
        out = _dot_nt(queries[a], keys[a])
        out = jnp.where(c["ci"] <= a * SUB + (c["ri"] & (SUB - 1)), out, 0.0)
        for h in range(GLA_HEADS):
            p_scr[h, a * SUB:(a + 1) * SUB, :] = out[h * SUB:(h + 1) * SUB, :]


def _gla_intra_bwd_factored(c, factored, dps, dq_scr, dk_scr):
    eks, keys, queries = factored
    mask = _head_block_mask()
    dk = jnp.zeros((CHUNK, 256), F32)
    for a in range(N_SUB):
        dpa = jnp.concatenate([dps[h][a * SUB:(a + 1) * SUB, :] for h in range(GLA_HEADS)], axis=0)
        dq = jnp.where(mask, _dot(dpa, keys[a]), 0.0)
        dq = dq[0:SUB] + dq[SUB:2 * SUB] + dq[2 * SUB:3 * SUB] + dq[3 * SUB:4 * SUB]
        dq_scr[a * SUB:(a + 1) * SUB, :] = dq * c["eq"][a]
        dk = dk + _dot_tn(dpa, queries[a]) * eks[a]
    dk_scr[...] = dk


def _gla_lag_weights(c):
    cum, row = c["cum"], c["row"]
    out = [jnp.ones((CHUNK, 256), F32)]
    for r in range(1, SUB):
        out.append(jnp.where((row % SUB) >= r, jnp.exp(jnp.minimum(cum - pltpu.roll(cum, r, 0), 0.0)), 0.0))
    return out


def _gla_pairwise_keys(c):
    return [None] + [c["k"] * jnp.exp(jnp.minimum(c["refs"][a] - c["cum"], 0.0)) for a in range(1, N_SUB)]


def _gla_scores_pairwise(c, lag_w, keys, h):
    sl = slice(GLA_DK * h, GLA_DK * (h + 1))
    qs, k = c["qs"][:, sl], c["k"][:, sl]
    ri, ci = c["ri"], c["ci"]
    p = jnp.zeros((CHUNK, CHUNK), F32)
    for r in range(SUB):
        kr = k if r == 0 else pltpu.roll(k, r, 0)
        pr = jnp.sum(qs * kr * lag_w[r][:, sl], axis=1, keepdims=True)
        p = p + jnp.where(ci == ri - r, pr, 0.0)
    blocks = [jnp.zeros((SUB, CHUNK), F32)]
    for a in range(1, N_SUB):
        qh = qs[a * SUB:(a + 1) * SUB, :] * c["eq"][a][:, sl]
        blocks.append(jnp.where(ci[:SUB, :] < a * SUB, _dot_nt(qh, keys[a][:, sl]), 0.0))
    return p + jnp.concatenate(blocks, axis=0)


def _gla_all_scores(c, p_scr):
    @pl.when(c["small"])
    def _():
        _gla_scores_factored(c, _gla_factored(c), p_scr)

    @pl.when(jnp.logical_not(c["small"]))
    def _():
        lag_w, keys = _gla_lag_weights(c), _gla_pairwise_keys(c)
        for h in range(GLA_HEADS):
            p_scr[h] = _gla_scores_pairwise(c, lag_w, keys, h)


def _gla_intra_bwd_pairwise(c, lag_w, keys, dp, h):
    sl = slice(GLA_DK * h, GLA_DK * (h + 1))
    qs_h, k_h = c["qs"][:, sl], c["k"][:, sl]
    ri, ci = c["ri"], c["ci"]
    dq_rows = [jnp.zeros((SUB, GLA_DK), F32)]
    dk = jnp.zeros((CHUNK, GLA_DK), F32)
    for a in range(1, N_SUB):
        eq = c["eq"][a][:, sl]
        qh = qs_h[a * SUB:(a + 1) * SUB, :] * eq
        dpa = jnp.where(ci[:SUB, :] < a * SUB, dp[a * SUB:(a + 1) * SUB, :], 0.0)
        dq_rows.append(_dot(dpa, keys[a][:, sl]) * eq)
        ek = jnp.exp(jnp.minimum(c["refs"][a][:, sl] - c["cum"][:, sl], 0.0))
        dk = dk + _dot_tn(dpa, qh) * ek
    dq = jnp.concatenate(dq_rows, axis=0)
    for r in range(SUB):
        w = lag_w[r][:, sl]
        dpr = jnp.sum(jnp.where(ci == ri - r, dp, 0.0), axis=1, keepdims=True)
        kr = k_h if r == 0 else pltpu.roll(k_h, r, 0)
        dq = dq + dpr * kr * w
        back = dpr * qs_h * w
        dk = dk + (back if r == 0 else pltpu.roll(back, CHUNK - r, 0))
    return dq, dk


def _gla_all_intra_bwd(c, dps, p_scr, dq_scr, dk_scr):
    @pl.when(c["small"])
    def _():
        factored = _gla_factored(c)
        _gla_scores_factored(c, factored, p_scr)
        _gla_intra_bwd_factored(c, factored, dps, dq_scr, dk_scr)

    @pl.when(jnp.logical_not(c["small"]))
    def _():
        lag_w, keys = _gla_lag_weights(c), _gla_pairwise_keys(c)
        outs = [_gla_intra_bwd_pairwise(c, lag_w, keys, dps[h], h) for h in range(GLA_HEADS)]
        for h in range(GLA_HEADS):
            p_scr[h] = _gla_scores_pairwise(c, lag_w, keys, h)
        dq_scr[...] = jnp.concatenate([o[0] for o in outs], axis=1)
        dk_scr[...] = jnp.concatenate([o[1] for o in outs], axis=1)


def _mixer_fwd(proj, cos2, sin2, w2p, gb, rnw, gnw, name, carried=()):
    n_carried = len(carried)

    def body(*refs):
        p_ref, c_ref, s_ref, w2_ref, gb_ref, rnw_ref, gnw_ref = refs[:7]
        x_refs, refs = refs[7:7 + n_carried], refs[7 + n_carried:]
        ocat_ref, mrg_ref, sr_out, sg_out = refs[:4]
        gathered_refs, refs = refs[4:4 + n_carried], refs[4 + n_carried:]
        sr, sg, p_scr = refs[:3]
        n = pl.program_id(0)
        if n_carried:
            start, forward, finish = _gather_phases(x_refs, gathered_refs, *refs[3:])
            pl.when(n == 0)(start)
            pl.when(n == N_CHUNKS // 2)(forward)

        @pl.when(n == 0)
        def _():
            sr[...] = jnp.zeros_like(sr)
            sg[...] = jnp.zeros_like(sg)

        sr_out[0] = sr[...]
        sg_out[0] = sg[...]
        cosv, sinv = c_ref[...], s_ref[...]

        for h in range(RET_HEADS):
            dmat, zeta, xi, gc = _ret_consts(h)
            hs = slice(128 * h, 128 * (h + 1))
            q = _rope(p_ref[:, O_RQ + 128 * h:O_RQ + 128 * (h + 1)], cosv, sinv)
            k = _rope(p_ref[:, O_RK + 128 * h:O_RK + 128 * (h + 1)], cosv, sinv) * (RET_DK ** -0.5)
            v = p_ref[:, O_RV + 128 * h:O_RV + 128 * (h + 1)]
            g = p_ref[:, O_RG + 128 * h:O_RG + 128 * (h + 1)]
            s_in = sr[h]
            a = _dot_nt(q, k) * dmat
            o = _dot(a, v) + _dot(q, s_in) * xi
            sr[h] = gc * s_in + _dot_tn(k * zeta, v)
            mu = jnp.mean(o, axis=-1, keepdims=True)
            xc = o - mu
            nrm = xc * lax.rsqrt(jnp.mean(xc * xc, axis=-1, keepdims=True) + EPS)
            ocat_ref[:, hs] = o
            mrg_ref[:, hs] = (nrm * rnw_ref[:, hs] * (g * _sigmoid(g))).astype(BF16)

        c = _gla_common(p_ref, w2_ref, gb_ref, n)
        _gla_all_scores(c, p_scr)
        lastcol = _dot_tn_exact_lhs(c["la"], jnp.ones((CHUNK, GLA_DV), F32))
        qe = c["qs"] * c["ecum"]
        kl = c["k"] * c["ekl"]
        for h in range(GLA_HEADS):
            sl = slice(GLA_DK * h, GLA_DK * (h + 1))
            hs = slice(512 + 128 * h, 512 + 128 * (h + 1))
            v = p_ref[:, O_GV + 128 * h:O_GV + 128 * (h + 1)]
            g = p_ref[:, O_GR + 128 * h:O_GR + 128 * (h + 1)]
            s_in = sg[h]
            o = _dot(p_scr[h], v) + _dot(qe[:, sl], s_in)
            sg[h] = jnp.exp(lastcol[GLA_DK * h:GLA_DK * (h + 1), :]) * s_in + _dot_tn(kl[:, sl], v)
            nrm = o * lax.rsqrt(jnp.mean(o * o, axis=-1, keepdims=True) + EPS)
            ocat_ref[:, hs] = o
            mrg_ref[:, hs] = (nrm * gnw_ref[:, 128 * h:128 * (h + 1)] * (g * _sigmoid(g))).astype(BF16)

        if n_carried:
            pl.when(n == N_CHUNKS - 1)(finish)

    const = lambda shape: pl.BlockSpec(shape, lambda n: (0,) * len(shape))
    anywhere = [pl.BlockSpec(memory_space=pl.ANY)] * n_carried
    return pl.pallas_call(
        body, name=name, grid=(N_CHUNKS,),
        in_specs=[pl.BlockSpec((CHUNK, IN_WP), lambda n: (n, 0)),
                  pl.BlockSpec((CHUNK, 128), lambda n: (n, 0)), pl.BlockSpec((CHUNK, 128), lambda n: (n, 0)),
                  const((128, 256)), const((1, 256)), const((1, 512)), const((1, 512))] + anywhere,
        out_specs=[pl.BlockSpec((CHUNK, D), lambda n: (n, 0)), pl.BlockSpec((CHUNK, D), lambda n: (n, 0)),
                   pl.BlockSpec((1, RET_HEADS, RET_DK, 128), lambda n: (n, 0, 0, 0)),
                   pl.BlockSpec((1, GLA_HEADS, GLA_DK, GLA_DV), lambda n: (n, 0, 0, 0))] + anywhere,
        out_shape=[jax.ShapeDtypeStruct((LP, D), F32), jax.ShapeDtypeStruct((LP, D), BF16),
                   jax.ShapeDtypeStruct((N_CHUNKS, RET_HEADS, RET_DK, 128), F32),
                   jax.ShapeDtypeStruct((N_CHUNKS, GLA_HEADS, GLA_DK, GLA_DV), F32)] + _gathered_shapes(carried),
        scratch_shapes=[pltpu.VMEM((RET_HEADS, RET_DK, 128), F32), pltpu.VMEM((GLA_HEADS, GLA_DK, GLA_DV), F32),
                        pltpu.VMEM((GLA_HEADS, CHUNK, CHUNK), F32)] + _exchange_sems(n_carried),
        compiler_params=_cparams("arbitrary"),
    )(proj, cos2, sin2, w2p, gb, rnw, gnw, *carried)


def _mixer_bwd(proj, ocat, dmrg, sr_all, sg_all, cos2, sin2, w2p, gb, rnw, gnw, name, carried=()):
    last_chunk = N_CHUNKS - 1
    n_carried = len(carried)

    def body(*refs):
        p_ref, ocat_ref, dm_ref, sr_ref, sg_ref, c_ref, s_ref, w2_ref, gb_ref, rnw_ref, gnw_ref = refs[:11]
        g_refs, refs = refs[11:11 + n_carried], refs[11 + n_carried:]
        dp_ref, dw2_ref, dgb_ref, drn_ref, dgn_ref = refs[:5]
        got_refs, refs = refs[5:5 + n_carried], refs[5 + n_carried:]
        dsr, dsg, p_scr, dq_scr, dk_scr = refs[:5]
        step = pl.program_id(0)
        n = last_chunk - step
        if n_carried:
            start, finish = _exchange_phases(g_refs, got_refs, *refs[5:])
            pl.when(step == 0)(start)

        @pl.when(step == 0)
        def _():
            dsr[...] = jnp.zeros_like(dsr)
            dsg[...] = jnp.zeros_like(dsg)
            dw2_ref[...] = jnp.zeros_like(dw2_ref)
            dgb_ref[...] = jnp.zeros_like(dgb_ref)
            drn_ref[...] = jnp.zeros_like(drn_ref)
            dgn_ref[...] = jnp.zeros_like(dgn_ref)

        cosv, sinv = c_ref[...], s_ref[...]
        row = lax.broadcasted_iota(jnp.int32, (CHUNK, 1), 0)
        real = ((n * CHUNK + row) >= PAD_ROWS).astype(F32)

        for h in range(RET_HEADS):
            dmat, zeta, xi, gc = _ret_consts(h)
            hs = slice(128 * h, 128 * (h + 1))
            q = _rope(p_ref[:, O_RQ + 128 * h:O_RQ + 128 * (h + 1)], cosv, sinv)
            k = _rope(p_ref[:, O_RK + 128 * h:O_RK + 128 * (h + 1)], cosv, sinv) * (RET_DK ** -0.5)
            v = p_ref[:, O_RV + 128 * h:O_RV + 128 * (h + 1)]
            g = p_ref[:, O_RG + 128 * h:O_RG + 128 * (h + 1)]
            o = ocat_ref[:, hs]
            dy = dm_ref[:, hs]
            wv = rnw_ref[:, hs]
            mu = jnp.mean(o, axis=-1, keepdims=True)
            xc = o - mu
            rs = lax.rsqrt(jnp.mean(xc * xc, axis=-1, keepdims=True) + EPS)
            nrm = xc * rs
            sgm = _sigmoid(g)
            sil = g * sgm
            drn_ref[0:1, hs] += jnp.sum(dy * nrm * sil, axis=0, keepdims=True)
            dgate = dy * nrm * wv * (sgm * (1.0 + g * (1.0 - sgm)))
            dn = dy * wv * sil
            do = rs * (dn - jnp.mean(dn, axis=-1, keepdims=True) - nrm * jnp.mean(dn * nrm, axis=-1, keepdims=True))
            s_in = sr_ref[0, h]
            ds_out = dsr[h]
            a = _dot_nt(q, k) * dmat
            da = _dot_nt(do, v) * dmat
            dox = do * xi
            dq = _dot(da, k) + _dot_nt(dox, s_in)
            dk = _dot_tn(da, q) + _dot_nt(v, ds_out) * zeta
            dv = _dot_tn(a, do) + _dot(k * zeta, ds_out)
            dsr[h] = gc * ds_out + _dot_tn(q, dox)
            dk = dk * (RET_DK ** -0.5)
            dp_ref[:, O_RQ + 128 * h:O_RQ + 128 * (h + 1)] = (_unrope(dq, cosv, sinv) * real).astype(BF16)
            dp_ref[:, O_RK + 128 * h:O_RK + 128 * (h + 1)] = (_unrope(dk, cosv, sinv) * real).astype(BF16)
            dp_ref[:, O_RV + 128 * h:O_RV + 128 * (h + 1)] = (dv * real).astype(BF16)
            dp_ref[:, O_RG + 128 * h:O_RG + 128 * (h + 1)] = (dgate * real).astype(BF16)

        c = _gla_common(p_ref, w2_ref, gb_ref, n)
        ri, ci = c["ri"], c["ci"]
        causal = ri >= ci
        triu = (ci >= ri).astype(F32)
        qe = c["qs"] * c["ecum"]
        kl = c["k"] * c["ekl"]
        lastcol = _dot_tn_exact_lhs(c["la"], jnp.ones((CHUNK, GLA_DV), F32))
        dla_heads, dq_heads, dk_heads = [], [], []
        dos, dps = [], []
        for h in range(GLA_HEADS):
            hs = slice(512 + 128 * h, 512 + 128 * (h + 1))
            v = p_ref[:, O_GV + 128 * h:O_GV + 128 * (h + 1)]
            g = p_ref[:, O_GR + 128 * h:O_GR + 128 * (h + 1)]
            o = ocat_ref[:, hs]
            dy = dm_ref[:, hs]
            wv = gnw_ref[:, 128 * h:128 * (h + 1)]
            rs = lax.rsqrt(jnp.mean(o * o, axis=-1, keepdims=True) + EPS)
            nrm = o * rs
            sgm = _sigmoid(g)
            sil = g * sgm
            dgn_ref[0:1, 128 * h:128 * (h + 1)] += jnp.sum(dy * nrm * sil, axis=0, keepdims=True)
            dgate = dy * nrm * wv * (sgm * (1.0 + g * (1.0 - sgm)))
            dn = dy * wv * sil
            do = rs * (dn - nrm * jnp.mean(dn * nrm, axis=-1, keepdims=True))
            dp_ref[:, O_GR + 128 * h:O_GR + 128 * (h + 1)] = (dgate * real).astype(BF16)
            dos.append(do)
            dps.append(jnp.where(causal, _dot_nt(do, v), 0.0))
        _gla_all_intra_bwd(c, dps, p_scr, dq_scr, dk_scr)
        dq_intra, dk_intra = dq_scr[...], dk_scr[...]
        for h in range(GLA_HEADS):
            sl = slice(GLA_DK * h, GLA_DK * (h + 1))
            v = p_ref[:, O_GV + 128 * h:O_GV + 128 * (h + 1)]
            do = dos[h]
            qs_h, k_h = c["qs"][:, sl], c["k"][:, sl]
            s_in = sg_ref[0, h]
            ds_out = dsg[h]
            el_col = jnp.exp(lastcol[GLA_DK * h:GLA_DK * (h + 1), :])
            dv = _dot_tn(p_scr[h], do) + _dot(kl[:, sl], ds_out)
            dqe = _dot_nt(do, s_in)
            dkl = _dot_nt(v, ds_out)
            dsg[h] = _dot_tn(qe[:, sl], do) + el_col * ds_out
            sd = s_in * ds_out
            sd_hi = sd.astype(BF16)
            sd_lo = (sd - sd_hi.astype(F32)).astype(BF16)
            ones8 = jnp.ones((8, GLA_DV), BF16)
            nt = (((1,), (1,)), ((), ()))
            d_el = (lax.dot_general(ones8, sd_hi, nt, preferred_element_type=F32)
                    + lax.dot_general(ones8, sd_lo, nt, preferred_element_type=F32))[0:1, :]
            dqs = dqe * c["ecum"][:, sl] + dq_intra[:, sl]
            dkk = dkl * c["ekl"][:, sl] + dk_intra[:, sl]
            d_last = jnp.sum(dkl * kl[:, sl], axis=0, keepdims=True) + d_el * c["el"][:, sl]
            dcum = qs_h * dqs - k_h * dkk + jnp.where(row == CHUNK - 1, d_last, 0.0)
            dla_heads.append(_dot_exact_rhs(triu, dcum))
            dq_heads.append(dqs * (GLA_DK ** -0.5))
            dk_heads.append(dkk)
            dp_ref[:, O_GV + 128 * h:O_GV + 128 * (h + 1)] = (dv * real).astype(BF16)

        dla = jnp.concatenate(dla_heads, axis=1)
        dp_ref[:, O_GQ:O_GQ + 256] = (jnp.concatenate(dq_heads, axis=1) * real).astype(BF16)
        dp_ref[:, O_GK:O_GK + 256] = (jnp.concatenate(dk_heads, axis=1) * real).astype(BF16)
        dz = dla * (1.0 / GLA_TAU) * _sigmoid(-c["z"]) * real
        ga = p_ref[:, O_GA:O_GA + 128]
        dp_ref[:, O_GA:O_GA + 128] = _dot_nt(dz, w2_ref[...]).astype(BF16)
        dp_ref[:, O_GA + 128:IN_WP] = jnp.zeros((CHUNK, IN_WP - O_GA - 128), BF16)
        dw2_ref[...] += _dot_tn(ga, dz)
        dgb_ref[0:1, :] += jnp.sum(dz, axis=0, keepdims=True)

        if n_carried:
            pl.when(step == last_chunk)(finish)

    const = lambda shape: pl.BlockSpec(shape, lambda s: (0,) * len(shape))
    rev = lambda s: (last_chunk - s, 0)
    anywhere = [pl.BlockSpec(memory_space=pl.ANY)] * n_carried
    return pl.pallas_call(
        body, name=name, grid=(N_CHUNKS,),
        in_specs=[pl.BlockSpec((CHUNK, IN_WP), rev), pl.BlockSpec((CHUNK, D), rev), pl.BlockSpec((CHUNK, D), rev),
                  pl.BlockSpec((1, RET_HEADS, RET_DK, 128), lambda s: (last_chunk - s, 0, 0, 0)),
                  pl.BlockSpec((1, GLA_HEADS, GLA_DK, GLA_DV), lambda s: (last_chunk - s, 0, 0, 0)),
                  pl.BlockSpec((CHUNK, 128), rev), pl.BlockSpec((CHUNK, 128), rev),
                  const((128, 256)), const((1, 256)), const((1, 512)), const((1, 512))] + anywhere,
        out_specs=[pl.BlockSpec((CHUNK, IN_WP), rev), const((128, 256)), const((8, 256)),
                   const((8, 512)), const((8, 512))] + anywhere,
        out_shape=[jax.ShapeDtypeStruct((LP, IN_WP), BF16), jax.ShapeDtypeStruct((128, 256), F32),
                   jax.ShapeDtypeStruct((8, 256), F32), jax.ShapeDtypeStruct((8, 512), F32),
                   jax.ShapeDtypeStruct((8, 512), F32)] + [jax.ShapeDtypeStruct(g.shape, g.dtype) for g in carried],
        scratch_shapes=[pltpu.VMEM((RET_HEADS, RET_DK, 128), F32), pltpu.VMEM((GLA_HEADS, GLA_DK, GLA_DV), F32),
                        pltpu.VMEM((GLA_HEADS, CHUNK, CHUNK), F32), pltpu.VMEM((CHUNK, 256), F32),
                        pltpu.VMEM((CHUNK, 256), F32)] + _exchange_sems(n_carried),
        compiler_params=_cparams("arbitrary"),
    )(proj, ocat, dmrg, sr_all, sg_all, cos2, sin2, w2p, gb, rnw, gnw, *carried)


def _all_gather(xs, name):
    n = len(xs)

    def body(*refs):
        start, forward, finish = _gather_phases(refs[:n], refs[n:2 * n], *refs[2 * n:])
        start()
        forward()
        finish()

    return pl.pallas_call(
        body, name=name,
        in_specs=[pl.BlockSpec(memory_space=pl.ANY)] * n,
        out_specs=[pl.BlockSpec(memory_space=pl.ANY)] * n,
        out_shape=_gathered_shapes(xs),
        scratch_shapes=_exchange_sems(n),
    )(*xs)


def _gathered_shapes(xs):
    return [jax.ShapeDtypeStruct((N_DEV,) + x.shape, x.dtype) for x in xs]


def _exchange_sems(n):
    if n == 0:
        return []
    return [pltpu.SemaphoreType.DMA((7 * n,)), pltpu.SemaphoreType.DMA((7 * n,)), pltpu.SemaphoreType.DMA((n,))]


def _gather_phases(x_refs, out_refs, send_sems, recv_sems, local_sems):
    n = len(x_refs)
    mx, my, mc = lax.axis_index("x"), lax.axis_index("y"), lax.axis_index("c")
    me, sibling = (mx, my, mc), (mx, my, 1 - mc)
    chips = [(1 - mx, my), (mx, 1 - my), (1 - mx, 1 - my)]

    def slot(a, px, py, pc):
        return out_refs[a].at[4 * px + 2 * py + pc]

    def copy(a, k, block, to, src=None):
        return pltpu.make_async_remote_copy(
            src_ref=slot(a, *block) if src is None else src, dst_ref=slot(a, *block),
            send_sem=send_sems.at[7 * a + k], recv_sem=recv_sems.at[7 * a + k],
            device_id=to, device_id_type=MESH_IDS)

    mine = [pltpu.make_async_copy(x_refs[a], slot(a, *me), local_sems.at[a]) for a in range(n)]
    first = []
    for a in range(n):
        first.append(copy(a, 0, me, sibling, src=x_refs[a]))
        first += [copy(a, 1 + j, me, (*chip, mc), src=x_refs[a]) for j, chip in enumerate(chips)]
    passed = [copy(a, 4 + j, (*chip, mc), sibling) for j, chip in enumerate(chips) for a in range(n)]

    def start():
        for cp in mine + first:
            cp.start()

    def forward():
        for j, chip in enumerate(chips):
            for a in range(n):
                copy(a, 1 + j, (*chip, mc), me).wait_recv()
                passed[j * n + a].start()

    def finish():
        for a in range(n):
            copy(a, 0, sibling, me).wait_recv()
            for j, chip in enumerate(chips):
                copy(a, 4 + j, (*chip, 1 - mc), me).wait_recv()
        for cp in first + passed:
            cp.wait_send()
        for cp in mine:
            cp.wait()

    return start, forward, finish


def _exchange_blocks(gs, name):
    n = len(gs)

    def body(*refs):
        start, finish = _exchange_phases(refs[:n], refs[n:2 * n], *refs[2 * n:])
        start()
        finish()

    return pl.pallas_call(
        body, name=name,
        in_specs=[pl.BlockSpec(memory_space=pl.ANY)] * n,
        out_specs=[pl.BlockSpec(memory_space=pl.ANY)] * n,
        out_shape=[jax.ShapeDtypeStruct(g.shape, g.dtype) for g in gs],
        scratch_shapes=_exchange_sems(n),
    )(*gs)


def _exchange_phases(g_refs, out_refs, send_sems, recv_sems, local_sems):
    n = len(g_refs)
    mx, my, mc = lax.axis_index("x"), lax.axis_index("y"), lax.axis_index("c")
    me = 4 * mx + 2 * my + mc
    mine = [pltpu.make_async_copy(g_refs[a].at[me], out_refs[a].at[me], local_sems.at[a]) for a in range(n)]
    copies = []
    for r in range(1, N_DEV):
        px, py, pc = mx ^ (r >> 2), my ^ ((r >> 1) & 1), mc ^ (r & 1)
        peer = 4 * px + 2 * py + pc
        for a in range(n):
            copies.append(pltpu.make_async_remote_copy(
                src_ref=g_refs[a].at[peer], dst_ref=out_refs[a].at[me],
                send_sem=send_sems.at[7 * a + r - 1], recv_sem=recv_sems.at[7 * a + r - 1],
                device_id=(px, py, pc), device_id_type=MESH_IDS))

    def start():
        for cp in mine + copies:
            cp.start()

    def finish():
        for cp in copies:
            cp.wait_recv()
        for cp in copies:
            cp.wait_send()
        for cp in mine:
            cp.wait()

    return start, finish


IN_SHARD = IN_W // N_DEV
IN_SHARD_P = 512
UP_SHARD = D_UP // N_DEV
UP_SHARD_P = 768
RELAYOUT_ROWS = 256


def _pieces_w_in():
    return [(k, 0, IN_SHARD * k, IN_SHARD) for k in range(N_DEV)]


def _pieces_ffn_up():
    pieces = []
    for k in range(N_DEV):
        n, end = UP_SHARD * k, UP_SHARD * (k + 1)
        while n < end:
            half, r = divmod(n, D_FF)
            blk, off = divmod(r, CONV_BLOCK)
            run = min(CONV_BLOCK - off, end - n)
            pieces.append((k, n - UP_SHARD * k, 2 * CONV_BLOCK * blk + CONV_BLOCK * half + off, run))
            n += run
    return pieces


def _assemble_block(load, spans, dst_block, rows):
    lo = 128 * dst_block
    lane = lax.broadcasted_iota(jnp.int32, (1, 128), 1)
    out = jnp.zeros((rows, 128), F32)
    for key, src_off, dst_off, length in spans:
        a, b = max(lo, dst_off), min(lo + 128, dst_off + length)
        s, s_end = src_off + (a - dst_off), src_off + (b - dst_off)
        d = a
        while s < s_end:
            e = min(s_end, 128 * (s // 128 + 1))
            blk = load(key, s // 128)
            shift = (d - s) % 128
            if shift:
                blk = pltpu.roll(blk, shift, 1)
            out = jnp.where((lane >= d - lo) & (lane < d - lo + (e - s)), blk, out)
            d += e - s
            s = e
    return out


def _shards_to_cols(shards, pieces, width, name):
    _, rows, _ = shards.shape
    tr = RELAYOUT_ROWS

    def body(s_ref, o_ref):
        load = lambda k, b: s_ref[k, :, 128 * b:128 * (b + 1)].astype(F32)
        for db in range(width // 128):
            o_ref[:, 128 * db:128 * (db + 1)] = _assemble_block(load, pieces, db, tr).astype(BF16)

    return pl.pallas_call(
        body, name=name, grid=(rows // tr,),
        in_specs=[pl.BlockSpec((N_DEV, tr, shards.shape[2]), lambda i: (0, i, 0))],
        out_specs=pl.BlockSpec((tr, width), lambda i: (i, 0)),
        out_shape=jax.ShapeDtypeStruct((rows, width), BF16),
        compiler_params=_cparams("parallel"),
    )(shards)


def _cols_to_shards(full, pieces, shard_width, name):
    rows, width = full.shape
    tr = RELAYOUT_ROWS

    def body(f_ref, o_ref):
        load = lambda _, b: f_ref[:, 128 * b:128 * (b + 1)].astype(F32)
        for k in range(N_DEV):
            spans = [(None, dst_off, src_off, length) for dev, src_off, dst_off, length in pieces if dev == k]
            for db in range(shard_width // 128):
                o_ref[k, :, 128 * db:128 * (db + 1)] = _assemble_block(load, spans, db, tr).astype(BF16)

    return pl.pallas_call(
        body, name=name, grid=(rows // tr,),
        in_specs=[pl.BlockSpec((tr, width), lambda i: (i, 0))],
        out_specs=pl.BlockSpec((N_DEV, tr, shard_width), lambda i: (0, i, 0)),
        out_shape=jax.ShapeDtypeStruct((N_DEV, rows, shard_width), BF16),
        compiler_params=_cparams("parallel"),
    )(full)


def _adamw(parts, w, m, v, rows_per_step, name):
    rows, cols = w.shape
    assert rows % rows_per_step == 0 and parts.shape == (N_DEV, rows, cols)

    def body(p_ref, w_ref, m_ref, v_ref, g_ref, d_ref, nm_ref, nv_ref):
        g = p_ref[0].astype(F32)
        for j in range(1, N_DEV):
            g = g + p_ref[j].astype(F32)
        m_new = ADAM_B1 * m_ref[...] + (1.0 - ADAM_B1) * g
        v_new = ADAM_B2 * v_ref[...] + (1.0 - ADAM_B2) * (g * g)
        m_hat = m_new / (1.0 - ADAM_B1 ** ADAM_STEP)
        v_hat = v_new / (1.0 - ADAM_B2 ** ADAM_STEP)
        g_ref[...] = g
        d_ref[...] = -ADAM_LR * (m_hat / (jnp.sqrt(v_hat) + ADAM_EPS) + ADAM_WD * w_ref[...])
        nm_ref[...] = m_new
        nv_ref[...] = v_new

    tile = pl.BlockSpec((rows_per_step, cols), lambda i: (i, 0))
    shape = jax.ShapeDtypeStruct((rows, cols), F32)
    return pl.pallas_call(
        body, name=name, grid=(rows // rows_per_step,),
        in_specs=[pl.BlockSpec((N_DEV, rows_per_step, cols), lambda i: (0, i, 0)), tile, tile, tile],
        out_specs=[tile, tile, tile, tile],
        out_shape=[shape, shape, shape, shape],
        compiler_params=_cparams("parallel"),
    )(parts, w, m, v)


BIG = (("w_in", (DEPTH, D, IN_W // N_DEV), 2), ("w_out", (DEPTH, D // N_DEV, D), 1),
       ("ffn_up", (DEPTH, D, D_UP // N_DEV), 2), ("ffn_down", (DEPTH, D_FF // N_DEV, D), 1))
SMALL = (("meta_tokens", (N_META, D // N_DEV), 1), ("gla_gate_w2", (DEPTH, GATE_RANK, 256 // N_DEV), 2),
         ("ffn_conv_w", (DEPTH, 3, D_UP // N_DEV), 2))
REPL = (("pre_mix_norm", (DEPTH, D)), ("gla_gate_b", (DEPTH, 256)), ("ret_norm_w", (DEPTH, 512)),
        ("gla_norm_w", (DEPTH, 512)), ("post_mix_norm", (DEPTH, D)), ("pre_ffn_norm", (DEPTH, D)),
        ("ffn_conv_b", (DEPTH, D_UP)), ("post_ffn_norm", (DEPTH, D)))
WEIGHT_ORDER = ("meta_tokens", "pre_mix_norm", "w_in", "gla_gate_w2", "gla_gate_b", "ret_norm_w", "gla_norm_w",
                "w_out", "post_mix_norm", "pre_ffn_norm", "ffn_up", "ffn_conv_w", "ffn_conv_b", "ffn_down",
                "post_ffn_norm")


def _size(shape):
    return math.prod(shape)


def _round_up(n, mult):
    return -(-n // mult) * mult


REPL_ROWS = _round_up(-(-sum(_size(s) for _, s in REPL) // LANES), 8)
SMALL_ROWS = _round_up(-(-sum(_size(s) for _, s, _ in SMALL) // LANES), 8)


def _pack(arrays, rows, dtype):
    flat = jnp.concatenate([a.reshape(-1).astype(dtype) for a in arrays])
    return jnp.pad(flat, (0, rows * LANES - flat.shape[0])).reshape(rows, LANES)


def _unpack(buf, shapes):
    flat = buf.reshape(-1)
    out, off = [], 0
    for shape in shapes:
        out.append(flat[off:off + _size(shape)].reshape(shape))
        off += _size(shape)
    return out


def _unshard(blocks, axis):
    moved = jnp.moveaxis(blocks, 0, axis)
    shape = list(moved.shape)
    shape[axis:axis + 2] = [shape[axis] * shape[axis + 1]]
    return moved.reshape(shape)


def _to_blocks(full, axis):
    shape = list(full.shape)
    shape[axis:axis + 1] = [N_DEV, shape[axis] // N_DEV]
    return jnp.moveaxis(full.reshape(shape), axis, 0)


def _interleave_cols(w):
    lead = w.shape[:-1]
    return jnp.swapaxes(w.reshape(lead + (2, N_CONV_BLOCKS, CONV_BLOCK)), -3, -2).reshape(lead + (D_UP,))


def _deinterleave_cols(w):
    lead = w.shape[:-1]
    return jnp.swapaxes(w.reshape(lead + (N_CONV_BLOCKS, 2, CONV_BLOCK)), -3, -2).reshape(lead + (D_UP,))


def _rope_tables():
    half = RET_DK // 2
    inv = ROPE_BASE ** (-jnp.arange(half, dtype=F32) / half)
    pos = jnp.arange(LP, dtype=F32) - float(PAD_ROWS)
    ang = pos[:, None] * inv[None, :]
    c, s = jnp.cos(ang), jnp.sin(ang)
    return jnp.concatenate([c, c], axis=1), jnp.concatenate([-s, s], axis=1)


def kernel(x, meta_tokens, pre_mix_norm, w_in, gla_gate_w2, gla_gate_b, ret_norm_w, gla_norm_w, w_out, post_mix_norm, pre_ffn_norm, ffn_up, ffn_conv_w, ffn_conv_b, ffn_down, post_ffn_norm, loss_target, m_meta_tokens, m_pre_mix_norm, m_w_in, m_gla_gate_w2, m_gla_gate_b, m_ret_norm_w, m_gla_norm_w, m_w_out, m_post_mix_norm, m_pre_ffn_norm, m_ffn_up, m_ffn_conv_w, m_ffn_conv_b, m_ffn_down, m_post_ffn_norm, v_meta_tokens, v_pre_mix_norm, v_w_in, v_gla_gate_w2, v_gla_gate_b, v_ret_norm_w, v_gla_norm_w, v_w_out, v_post_mix_norm, v_pre_ffn_norm, v_ffn_up, v_ffn_conv_w, v_ffn_conv_b, v_ffn_down, v_post_ffn_norm):
    weights = dict(meta_tokens=meta_tokens, pre_mix_norm=pre_mix_norm, w_in=w_in, gla_gate_w2=gla_gate_w2,
                   gla_gate_b=gla_gate_b, ret_norm_w=ret_norm_w, gla_norm_w=gla_norm_w, w_out=w_out,
                   post_mix_norm=post_mix_norm, pre_ffn_norm=pre_ffn_norm, ffn_up=ffn_up, ffn_conv_w=ffn_conv_w,
                   ffn_conv_b=ffn_conv_b, ffn_down=ffn_down, post_ffn_norm=post_ffn_norm)
    mom1 = dict(meta_tokens=m_meta_tokens, pre_mix_norm=m_pre_mix_norm, w_in=m_w_in, gla_gate_w2=m_gla_gate_w2,
                gla_gate_b=m_gla_gate_b, ret_norm_w=m_ret_norm_w, gla_norm_w=m_gla_norm_w, w_out=m_w_out,
                post_mix_norm=m_post_mix_norm, pre_ffn_norm=m_pre_ffn_norm, ffn_up=m_ffn_up,
                ffn_conv_w=m_ffn_conv_w, ffn_conv_b=m_ffn_conv_b, ffn_down=m_ffn_down, post_ffn_norm=m_post_ffn_norm)
    mom2 = dict(meta_tokens=v_meta_tokens, pre_mix_norm=v_pre_mix_norm, w_in=v_w_in, gla_gate_w2=v_gla_gate_w2,
                gla_gate_b=v_gla_gate_b, ret_norm_w=v_ret_norm_w, gla_norm_w=v_gla_norm_w, w_out=v_w_out,
                post_mix_norm=v_post_mix_norm, pre_ffn_norm=v_pre_ffn_norm, ffn_up=v_ffn_up,
                ffn_conv_w=v_ffn_conv_w, ffn_conv_b=v_ffn_conv_b, ffn_down=v_ffn_down, post_ffn_norm=v_post_ffn_norm)

    pad_cols = lambda a, width: jnp.pad(a, ((0, 0), (0, width - a.shape[1])))
    big_names = [n for n, _, _ in BIG]
    shard = {}
    for l in range(DEPTH):
        shard[l, "w_in"] = pad_cols(w_in[l].astype(BF16), IN_SHARD_P)
        shard[l, "w_out"] = w_out[l].astype(BF16)
        shard[l, "ffn_up"] = pad_cols(ffn_up[l].astype(BF16), UP_SHARD_P)
        shard[l, "ffn_down"] = ffn_down[l].astype(BF16)
    gathered = {(0, "w_in"): _all_gather([shard[0, "w_in"]], "gather_w_in_0")[0]}
    gather_in_mixer = {l: [(l, n) for n in big_names[1:]] + ([(l + 1, "w_in")] if l + 1 < DEPTH else [])
                       for l in range(DEPTH)}
    small = _all_gather([_pack([weights[n] for n, _, _ in SMALL], SMALL_ROWS, F32)], "gather_small_weights")[0]
    small_parts = _unpack_blocks(small, [s for _, s, _ in SMALL])
    full = {n: _unshard(p, ax) for (n, _, ax), p in zip(SMALL, small_parts)}
    w2p = jnp.pad(full["gla_gate_w2"], ((0, 0), (0, 128 - GATE_RANK), (0, 0)))
    cw8 = jnp.concatenate([_interleave_cols(full["ffn_conv_w"]), _interleave_cols(ffn_conv_b)[:, None, :],
                           jnp.zeros((DEPTH, 4, D_UP), F32)], axis=1)
    cos2, sin2 = _rope_tables()

    h = jnp.concatenate([jnp.zeros((PAD_ROWS, D), F32), full["meta_tokens"], x[0]], axis=0)
    target = jnp.concatenate([jnp.zeros((CHUNK, D), F32), loss_target[0]], axis=0)
    saved, layer_w = [], []
    for l in range(DEPTH):
        lw = dict(w_in=_shards_to_cols(gathered[l, "w_in"], _pieces_w_in(), IN_WP, f"w_in_cols_{l}"))
        a1 = _rmsnorm_fwd(h, pre_mix_norm[l:l + 1], f"pre_mix_norm_{l}")
        proj = _matmul(a1, lw["w_in"], out_dtype=F32, tm=TM, tn=1280, tk=D, name=f"in_proj_{l}")
        keys = gather_in_mixer.get(l, [])
        ocat, merged, sr_all, sg_all, *got = _mixer_fwd(proj, cos2, sin2, w2p[l], gla_gate_b[l:l + 1],
                                                        ret_norm_w[l:l + 1], gla_norm_w[l:l + 1], f"mixer_fwd_{l}",
                                                        carried=[shard[key] for key in keys])
        gathered.update(zip(keys, got))
        lw["w_out"] = gathered[l, "w_out"].reshape(D, D)
        lw["w_up"] = _shards_to_cols(gathered[l, "ffn_up"], _pieces_ffn_up(), D_UP, f"ffn_up_cols_{l}")
        lw["w_down"] = gathered[l, "ffn_down"].reshape(D_FF, D)
        layer_w.append(lw)
        m = _matmul(merged, lw["w_out"], out_dtype=F32, tm=TM, tn=D, tk=D, name=f"out_proj_{l}")
        h1 = _resid_norm(h, m, post_mix_norm[l:l + 1], f"post_mix_norm_{l}")
        a2 = _rmsnorm_fwd(h1, pre_ffn_norm[l:l + 1], f"pre_ffn_norm_{l}")
        u = _matmul(a2, lw["w_up"], out_dtype=BF16, tm=TM, tn=1408, tk=D, name=f"ffn_up_{l}")
        cv, act = _conv_act_fwd(u, cw8[l], f"ffn_conv_act_{l}")
        f = _matmul(act, lw["w_down"], out_dtype=F32, tm=TM, tn=D, tk=D_FF, name=f"ffn_down_{l}")
        h2 = _resid_norm(h1, f, post_ffn_norm[l:l + 1], f"post_ffn_norm_{l}")
        saved.append(dict(h=h, a1=a1, proj=proj, ocat=ocat, merged=merged, sr=sr_all, sg=sg_all, m=m, h1=h1,
                          a2=a2, u=u, cv=cv, act=act, f=f))
        h = h2

    dh, loss_acc = _loss_head(h, target, "loss_head")
    loss = lax.psum(loss_acc[0, 0], ("x", "y", "c"))

    kinds = ("grad", "delta", "new_m", "new_v")
    grads = {n: [None] * DEPTH for n in WEIGHT_ORDER if n != "meta_tokens" and n not in big_names}
    pending, parts = [], {}
    for l in reversed(range(DEPTH)):
        s, lw = saved[l], layer_w[l]
        df, g_post_ffn = _norm_bwd(dh, s["f"], post_ffn_norm[l:l + 1], None, BF16, f"post_ffn_norm_bwd_{l}")
        dact = _matmul(df, lw["w_down"], tb=True, out_dtype=BF16, tm=TM, tn=D_FF, tk=D, name=f"ffn_down_dx_{l}")
        g_down = _matmul(s["act"], df, ta=True, out_dtype=BF16, tm=D_FF // 2, tn=D, tk=TK_ROWS, name=f"ffn_down_dw_{l}")
        du, dcw = _conv_act_bwd(dact, s["cv"], s["u"], cw8[l], f"ffn_conv_act_bwd_{l}")
        da2 = _matmul(du, lw["w_up"], tb=True, out_dtype=F32, tm=TM, tn=D, tk=1408, name=f"ffn_up_dx_{l}")
        g_up = _matmul(s["a2"], du, ta=True, out_dtype=BF16, tm=D, tn=1408, tk=TK_ROWS, name=f"ffn_up_dw_{l}")
        dh1, g_pre_ffn = _norm_bwd(da2, s["h1"], pre_ffn_norm[l:l + 1], dh, F32, f"pre_ffn_norm_bwd_{l}")
        dm, g_post_mix = _norm_bwd(dh1, s["m"], post_mix_norm[l:l + 1], None, BF16, f"post_mix_norm_bwd_{l}")
        dmerged = _matmul(dm, lw["w_out"], tb=True, out_dtype=F32, tm=TM, tn=D, tk=D, name=f"out_proj_dx_{l}")
        g_out = _matmul(s["merged"], dm, ta=True, out_dtype=BF16, tm=D, tn=D, tk=TK_ROWS, name=f"out_proj_dw_{l}")
        pending += [((l, "ffn_down"), g_down.reshape(N_DEV, D_FF // N_DEV, D)),
                    ((l, "ffn_up"), _cols_to_shards(g_up, _pieces_ffn_up(), UP_SHARD_P, f"ffn_up_grad_shards_{l}")),
                    ((l, "w_out"), g_out.reshape(N_DEV, D // N_DEV, D))]
        dproj, g_w2, g_gb, g_rn, g_gn, *got = _mixer_bwd(s["proj"], s["ocat"], dmerged, s["sr"], s["sg"], cos2, sin2,
                                                         w2p[l], gla_gate_b[l:l + 1], ret_norm_w[l:l + 1],
                                                         gla_norm_w[l:l + 1], f"mixer_bwd_{l}",
                                                         carried=[blocks for _, blocks in pending])
        parts.update(zip([key for key, _ in pending], got))
        da1 = _matmul(dproj, lw["w_in"], tb=True, out_dtype=F32, tm=TM, tn=D, tk=1280, name=f"in_proj_dx_{l}")
        g_in = _matmul(s["a1"], dproj, ta=True, out_dtype=BF16, tm=D, tn=1280, tk=TK_ROWS, name=f"in_proj_dw_{l}")
        pending = [((l, "w_in"), _cols_to_shards(g_in, _pieces_w_in(), IN_SHARD_P, f"w_in_grad_shards_{l}"))]
        dh, g_pre_mix = _norm_bwd(da1, s["h"], pre_mix_norm[l:l + 1], dh1, F32, f"pre_mix_norm_bwd_{l}")
        grads["post_ffn_norm"][l] = g_post_ffn[0]
        grads["ffn_conv_w"][l] = _deinterleave_cols(dcw[0:3])
        grads["ffn_conv_b"][l] = _deinterleave_cols(dcw[3])
        grads["pre_ffn_norm"][l] = g_pre_ffn[0]
        grads["post_mix_norm"][l] = g_post_mix[0]
        grads["gla_gate_w2"][l] = g_w2[:GATE_RANK]
        grads["gla_gate_b"][l] = g_gb[0]
        grads["ret_norm_w"][l] = g_rn[0]
        grads["gla_norm_w"][l] = g_gn[0]
        grads["pre_mix_norm"][l] = g_pre_mix[0]
    local = {n: jnp.stack(v) for n, v in grads.items()}
    local["meta_tokens"] = dh[PAD_ROWS:CHUNK]
    grad_x = dh[CHUNK:][None]

    blocks = jnp.concatenate([_to_blocks(local[n], ax).reshape(N_DEV, -1) for n, _, ax in SMALL], axis=1)
    blocks = jnp.pad(blocks, ((0, 0), (0, SMALL_ROWS * LANES - blocks.shape[1]))).reshape(N_DEV, SMALL_ROWS, LANES)
    *got, small_grad_parts = _exchange_blocks([b for _, b in pending] + [blocks], "exchange_last_grads")
    parts.update(zip([key for key, _ in pending], got))

    widths = dict(w_in=IN_SHARD_P, w_out=D, ffn_up=UP_SHARD_P, ffn_down=D)
    steps = dict(w_in=256, w_out=D // N_DEV, ffn_up=256, ffn_down=D_FF // N_DEV // 2)
    big_out = {kind: {n: [None] * DEPTH for n in big_names} for kind in kinds}
    for l in range(DEPTH):
        for n in big_names:
            mine = [pad_cols(d[n][l], widths[n]) for d in (weights, mom1, mom2)]
            results = _adamw(parts[l, n], *mine, steps[n], f"adamw_{n}_{l}")
            for kind, r in zip(kinds, results):
                big_out[kind][n][l] = r[:, :weights[n].shape[2]]
    out = {kind: {n: jnp.stack(v) for n, v in big_out[kind].items()} for kind in kinds}
    shard_shapes = [s for _, s, _ in SMALL]
    packed = [_pack([d[n] for n, _, _ in SMALL], SMALL_ROWS, F32) for d in (weights, mom1, mom2)]
    results = _adamw(small_grad_parts, *packed, SMALL_ROWS, "adamw_small_sharded")
    for kind, buf in zip(kinds, results):
        out[kind].update(zip([n for n, _, _ in SMALL], _unpack(buf, shard_shapes)))

    repl_parts = _all_gather([_pack([local[n] for n, _ in REPL], REPL_ROWS, F32)], "gather_small_grads")[0]
    packed = [_pack([d[n] for n, _ in REPL], REPL_ROWS, F32) for d in (weights, mom1, mom2)]
    results = _adamw(repl_parts, *packed, REPL_ROWS, "adamw_replicated")
    repl_shapes = [s for _, s in REPL]
    for kind, buf in zip(kinds, results):
        out[kind].update(zip([n for n, _ in REPL], _unpack(buf, repl_shapes)))

    return (loss, grad_x, *[out["grad"][n] for n in WEIGHT_ORDER], *[out["delta"][n] for n in WEIGHT_ORDER],
            *[out["new_m"][n] for n in WEIGHT_ORDER], *[out["new_v"][n] for n in WEIGHT_ORDER])


def _unpack_blocks(gathered, shapes):
    flat = gathered.reshape(N_DEV, -1)
    out, off = [], 0
    for shape in shapes:
        out.append(flat[:, off:off + _size(shape)].reshape((N_DEV,) + shape))
        off += _size(shape)
    return out
```

```python
import math

import jax
import jax.numpy as jnp
from jax import lax
from jax.experimental import pallas as pl
from jax.experimental.pallas import tpu as pltpu

F32 = jnp.float32
BF16 = jnp.bfloat16

D = 1024
SEQ = 8192
DEPTH = 2
N_META = 16
CHUNK = 64
SUB = 16
N_SUB = CHUNK // SUB
PAD_ROWS = CHUNK - N_META
LP = SEQ + CHUNK
N_CHUNKS = LP // CHUNK
RET_HEADS = 4
RET_DK = 128
GLA_HEADS = 4
GLA_DK = 64
GLA_DV = 128
GLA_TAU = 16.0
GATE_RANK = 16
IN_W = 3600
IN_WP = 3840
D_FF = 2816
D_UP = 2 * D_FF
CONV_BLOCK = 256
N_CONV_BLOCKS = D_FF // CONV_BLOCK
ROPE_BASE = 10000.0
EPS = 1e-6
N_DEV = 8
LANES = 1024

O_RQ, O_RK, O_RV, O_RG = 0, 512, 1024, 1536
O_GQ, O_GK, O_GV, O_GR, O_GA = 2048, 2304, 2560, 3072, 3584

ADAM_LR = 0.001
ADAM_B1 = 0.9
ADAM_B2 = 0.999
ADAM_EPS = 1e-08
ADAM_WD = 0.01
ADAM_STEP = 10

VMEM_LIMIT = 56 * 1024 * 1024
MESH_IDS = pl.DeviceIdType.MESH


def _row_tile(rows, limit):
    best = 16
    for t in range(16, min(rows, limit) + 1, 16):
        if rows % t == 0:
            best = t
    return best


TM = _row_tile(LP, 688)
TK_ROWS = _row_tile(LP, 1376)


def _cparams(*sem):
    return pltpu.CompilerParams(dimension_semantics=sem, vmem_limit_bytes=VMEM_LIMIT)


def _dot(a, b):
    return jnp.dot(a.astype(BF16), b.astype(BF16), preferred_element_type=F32)


def _dot_nt(a, b):
    return lax.dot_general(a.astype(BF16), b.astype(BF16), (((1,), (1,)), ((), ())), preferred_element_type=F32)


def _dot_tn(a, b):
    return lax.dot_general(a.astype(BF16), b.astype(BF16), (((0,), (0,)), ((), ())), preferred_element_type=F32)


def _split3(x):
    hi = x.astype(BF16)
    r1 = x - hi.astype(F32)
    mid = r1.astype(BF16)
    lo = (r1 - mid.astype(F32)).astype(BF16)
    return hi, mid, lo


def _dot_exact_rhs(t, x):
    hi, mid, lo = _split3(x)
    t = t.astype(BF16)
    return (jnp.dot(t, hi, preferred_element_type=F32) + jnp.dot(t, mid, preferred_element_type=F32)
            + jnp.dot(t, lo, preferred_element_type=F32))


def _dot_tn_exact_lhs(x, ones):
    dims = (((0,), (0,)), ((), ()))
    hi, mid, lo = _split3(x)
    ones = ones.astype(BF16)
    return (lax.dot_general(hi, ones, dims, preferred_element_type=F32)
            + lax.dot_general(mid, ones, dims, preferred_element_type=F32)
            + lax.dot_general(lo, ones, dims, preferred_element_type=F32))


def _sigmoid(x):
    return 1.0 / (1.0 + jnp.exp(-x))


def _matmul(a, b, *, ta=False, tb=False, out_dtype, tm, tn, tk, name):
    m = a.shape[1] if ta else a.shape[0]
    k = a.shape[0] if ta else a.shape[1]
    n = b.shape[0] if tb else b.shape[1]
    assert (b.shape[1] if tb else b.shape[0]) == k
    assert m % tm == 0 and n % tn == 0 and k % tk == 0, (name, m, n, k, tm, tn, tk)
    nk = k // tk
    a_spec = pl.BlockSpec((tk, tm), lambda i, j, kk: (kk, i)) if ta else pl.BlockSpec((tm, tk), lambda i, j, kk: (i, kk))
    b_spec = pl.BlockSpec((tn, tk), lambda i, j, kk: (j, kk)) if tb else pl.BlockSpec((tk, tn), lambda i, j, kk: (kk, j))
    dims = (((0 if ta else 1,), (1 if tb else 0,)), ((), ()))

    def body(a_ref, b_ref, o_ref, *acc):
        prod = lax.dot_general(a_ref[...].astype(BF16), b_ref[...].astype(BF16), dims, preferred_element_type=F32)
        if nk == 1:
            o_ref[...] = prod.astype(out_dtype)
            return
        acc_ref, = acc
        kk = pl.program_id(2)

        @pl.when(kk == 0)
        def _():
            acc_ref[...] = prod

        @pl.when(kk > 0)
        def _():
            acc_ref[...] += prod

        @pl.when(kk == nk - 1)
        def _():
            o_ref[...] = acc_ref[...].astype(out_dtype)

    return pl.pallas_call(
        body, name=name, grid=(m // tm, n // tn, nk),
        in_specs=[a_spec, b_spec],
        out_specs=pl.BlockSpec((tm, tn), lambda i, j, kk: (i, j)),
        out_shape=jax.ShapeDtypeStruct((m, n), out_dtype),
        scratch_shapes=[pltpu.VMEM((tm, tn), F32)] if nk > 1 else [],
        compiler_params=_cparams("parallel", "parallel", "arbitrary"),
    )(a, b)


def _rmsnorm_fwd(x, w, name):
    def body(x_ref, w_ref, o_ref):
        xv = x_ref[...]
        r = lax.rsqrt(jnp.mean(xv * xv, axis=-1, keepdims=True) + EPS)
        o_ref[...] = (xv * r * w_ref[...]).astype(BF16)

    return pl.pallas_call(
        body, name=name, grid=(LP // TM,),
        in_specs=[pl.BlockSpec((TM, D), lambda i: (i, 0)), pl.BlockSpec((1, D), lambda i: (0, 0))],
        out_specs=pl.BlockSpec((TM, D), lambda i: (i, 0)),
        out_shape=jax.ShapeDtypeStruct((LP, D), BF16),
        compiler_params=_cparams("parallel"),
    )(x, w)


def _resid_norm(h, m, w, name):
    def body(h_ref, m_ref, w_ref, o_ref):
        mv = m_ref[...]
        r = lax.rsqrt(jnp.mean(mv * mv, axis=-1, keepdims=True) + EPS)
        row = pl.program_id(0) * TM + lax.broadcasted_iota(jnp.int32, (TM, 1), 0)
        o_ref[...] = h_ref[...] + jnp.where(row >= PAD_ROWS, mv * r * w_ref[...], 0.0)

    return pl.pallas_call(
        body, name=name, grid=(LP // TM,),
        in_specs=[pl.BlockSpec((TM, D), lambda i: (i, 0)), pl.BlockSpec((TM, D), lambda i: (i, 0)),
                  pl.BlockSpec((1, D), lambda i: (0, 0))],
        out_specs=pl.BlockSpec((TM, D), lambda i: (i, 0)),
        out_shape=jax.ShapeDtypeStruct((LP, D), F32),
        compiler_params=_cparams("parallel"),
    )(h, m, w)


def _norm_bwd(dy, x, w, resid, out_dtype, name):
    has_resid = resid is not None

    def body(*refs):
        if has_resid:
            dy_ref, x_ref, w_ref, r_ref, dx_ref, dw_ref = refs
        else:
            dy_ref, x_ref, w_ref, dx_ref, dw_ref = refs
        i = pl.program_id(0)

        @pl.when(i == 0)
        def _():
            dw_ref[...] = jnp.zeros_like(dw_ref)

        row = i * TM + lax.broadcasted_iota(jnp.int32, (TM, 1), 0)
        dyv = jnp.where(row >= PAD_ROWS, dy_ref[...], 0.0)
        xv = x_ref[...]
        r = lax.rsqrt(jnp.mean(xv * xv, axis=-1, keepdims=True) + EPS)
        g = dyv * w_ref[...]
        dx = r * g - xv * (r * r * r * jnp.mean(g * xv, axis=-1, keepdims=True))
        if has_resid:
            dx = dx + r_ref[...]
        dx_ref[...] = dx.astype(out_dtype)
        dw_ref[0:1, :] += jnp.sum(dyv * xv * r, axis=0, keepdims=True)

    tile = pl.BlockSpec((TM, D), lambda i: (i, 0))
    in_specs = [tile, tile, pl.BlockSpec((1, D), lambda i: (0, 0))] + ([tile] if has_resid else [])
    args = (dy, x, w) + ((resid,) if has_resid else ())
    return pl.pallas_call(
        body, name=name, grid=(LP // TM,),
        in_specs=in_specs,
        out_specs=[tile, pl.BlockSpec((8, D), lambda i: (0, 0))],
        out_shape=[jax.ShapeDtypeStruct((LP, D), out_dtype), jax.ShapeDtypeStruct((8, D), F32)],
        compiler_params=_cparams("arbitrary"),
    )(*args)


def _loss_head(y, target, name):
    def body(y_ref, t_ref, dy_ref, loss_ref):
        i = pl.program_id(0)

        @pl.when(i == 0)
        def _():
            loss_ref[...] = jnp.zeros_like(loss_ref)

        row = i * TM + lax.broadcasted_iota(jnp.int32, (TM, 1), 0)
        diff = jnp.where(row >= CHUNK, y_ref[...] - t_ref[...], 0.0)
        dy_ref[...] = diff * (1.0 / D)
        loss_ref[...] += (0.5 / D) * jnp.sum(diff * diff)

    tile = pl.BlockSpec((TM, D), lambda i: (i, 0))
    return pl.pallas_call(
        body, name=name, grid=(LP // TM,),
        in_specs=[tile, tile],
        out_specs=[tile, pl.BlockSpec((8, 128), lambda i: (0, 0))],
        out_shape=[jax.ShapeDtypeStruct((LP, D), F32), jax.ShapeDtypeStruct((8, 128), F32)],
        compiler_params=_cparams("arbitrary"),
    )(y, target)


GELU_C = math.sqrt(2.0 / math.pi)
GELU_K = 0.044715
STRIP = 16


def _shift_down(x, prev8, rows):
    row = lax.broadcasted_iota(jnp.int32, (rows, 1), 0)
    p1 = pltpu.roll(prev8, 1, 0)
    p2 = pltpu.roll(prev8, 2, 0)
    x1 = jnp.where(row == 0, p1[0:1, :], pltpu.roll(x, 1, 0))
    x2 = jnp.where(row == 0, p2[0:1, :], jnp.where(row == 1, p2[1:2, :], pltpu.roll(x, 2, 0)))
    return x1, x2


def _conv_act_fwd(u, cw8, name):
    n_rows = LP // TM
    cb2 = 2 * CONV_BLOCK

    def body(u_ref, cw_ref, conv_ref, act_ref, carry_ref):
        i = pl.program_id(1)

        @pl.when(i == 0)
        def _():
            carry_ref[...] = jnp.zeros_like(carry_ref)

        x = u_ref[...].astype(F32)
        x1, x2 = _shift_down(x, carry_ref[...], TM)
        conv = cw_ref[3:4, :] + x2 * cw_ref[0:1, :] + x1 * cw_ref[1:2, :] + x * cw_ref[2:3, :]
        conv_ref[...] = conv.astype(BF16)
        a = conv[:, :CONV_BLOCK]
        g = conv[:, CONV_BLOCK:]
        t = jnp.tanh(GELU_C * (a + GELU_K * a * a * a))
        act_ref[...] = (0.5 * a * (1.0 + t) * g).astype(BF16)
        carry_ref[...] = x[TM - 8:TM, :]

    return pl.pallas_call(
        body, name=name, grid=(N_CONV_BLOCKS, n_rows),
        in_specs=[pl.BlockSpec((TM, cb2), lambda j, i: (i, j)), pl.BlockSpec((8, cb2), lambda j, i: (0, j))],
        out_specs=[pl.BlockSpec((TM, cb2), lambda j, i: (i, j)), pl.BlockSpec((TM, CONV_BLOCK), lambda j, i: (i, j))],
        out_shape=[jax.ShapeDtypeStruct((LP, D_UP), BF16), jax.ShapeDtypeStruct((LP, D_FF), BF16)],
        scratch_shapes=[pltpu.VMEM((8, cb2), F32)],
        compiler_params=_cparams("arbitrary", "arbitrary"),
    )(u, cw8)


def _conv_act_bwd(dact, conv, u, cw8, name):
    n_rows = LP // TM
    cb2 = 2 * CONV_BLOCK
    n_strips = TM // STRIP

    def body(dact_ref, conv_ref, u_ref, cw_ref, du_ref, dcw_ref, carry_ref):
        i = pl.program_id(1)

        @pl.when(i == 0)
        def _():
            dcw_ref[...] = jnp.zeros_like(dcw_ref)
            carry_ref[...] = jnp.zeros_like(carry_ref)

        w0, w1, w2 = cw_ref[0:1, :], cw_ref[1:2, :], cw_ref[2:3, :]
        row = lax.broadcasted_iota(jnp.int32, (STRIP, 1), 0)
        fold = lambda z: z[:8, :] + z[8:, :]

        def strip(k, carry):
            n1, n2, s0, s1, s2, s3 = carry
            r0 = pl.multiple_of((n_strips - 1 - k) * STRIP, STRIP)
            cv = conv_ref[pl.ds(r0, STRIP), :].astype(F32)
            a = cv[:, :CONV_BLOCK]
            g = cv[:, CONV_BLOCK:]
            t = jnp.tanh(GELU_C * (a + GELU_K * a * a * a))
            gel = 0.5 * a * (1.0 + t)
            dgel = 0.5 * (1.0 + t) + 0.5 * a * (1.0 - t * t) * (GELU_C * (1.0 + 3.0 * GELU_K * a * a))
            dav = dact_ref[pl.ds(r0, STRIP), :].astype(F32)
            dconv = jnp.concatenate([dav * g * dgel, dav * gel], axis=1)
            u1 = pltpu.roll(dconv, STRIP - 1, 0)
            u2 = pltpu.roll(dconv, STRIP - 2, 0)
            d1 = jnp.where(row >= STRIP - 1, n1, u1)
            d2 = jnp.where(row >= STRIP - 2, n2, u2)
            du_ref[pl.ds(r0, STRIP), :] = (dconv * w2 + d1 * w1 + d2 * w0).astype(BF16)
            x = u_ref[pl.ds(r0, STRIP), :].astype(F32)
            return (u1, u2, s0 + fold(d2 * x), s1 + fold(d1 * x), s2 + fold(dconv * x), s3 + fold(dconv))

        below = carry_ref[...]
        zero = jnp.zeros((8, cb2), F32)
        init = (pltpu.roll(below, STRIP - 1, 0), pltpu.roll(below, STRIP - 2, 0), zero, zero, zero, zero)
        u1, _, s0, s1, s2, s3 = lax.fori_loop(0, n_strips, strip, init)
        carry_ref[...] = pltpu.roll(u1, 1, 0)
        dcw_ref[0:1, :] += jnp.sum(s0, axis=0, keepdims=True)
        dcw_ref[1:2, :] += jnp.sum(s1, axis=0, keepdims=True)
        dcw_ref[2:3, :] += jnp.sum(s2, axis=0, keepdims=True)
        dcw_ref[3:4, :] += jnp.sum(s3, axis=0, keepdims=True)

    rev = lambda j, i: (n_rows - 1 - i, j)
    return pl.pallas_call(
        body, name=name, grid=(N_CONV_BLOCKS, n_rows),
        in_specs=[pl.BlockSpec((TM, CONV_BLOCK), rev), pl.BlockSpec((TM, cb2), rev), pl.BlockSpec((TM, cb2), rev),
                  pl.BlockSpec((8, cb2), lambda j, i: (0, j))],
        out_specs=[pl.BlockSpec((TM, cb2), rev), pl.BlockSpec((8, cb2), lambda j, i: (0, j))],
        out_shape=[jax.ShapeDtypeStruct((LP, D_UP), BF16), jax.ShapeDtypeStruct((8, D_UP), F32)],
        scratch_shapes=[pltpu.VMEM((STRIP, cb2), F32)],
        compiler_params=_cparams("arbitrary", "arbitrary"),
    )(dact, conv, u, cw8)


CHUNKS_PER_STEP = 3 if N_CHUNKS % 3 == 0 else 1
STEP_ROWS = CHUNKS_PER_STEP * CHUNK
N_STEPS = N_CHUNKS // CHUNKS_PER_STEP


def _ret_consts(h):
    rows = STEP_ROWS
    lg = math.log(1.0 - 2.0 ** (-5.0 - h))
    ri = lax.broadcasted_iota(jnp.int32, (rows, rows), 0)
    ci = lax.broadcasted_iota(jnp.int32, (rows, rows), 1)
    diff = (ri - ci).astype(F32)
    dmat = jnp.where(diff >= 0, jnp.exp(lg * jnp.maximum(diff, 0.0)), 0.0)
    rowf = lax.broadcasted_iota(jnp.int32, (rows, 1), 0).astype(F32)
    zeta = jnp.exp(lg * (rows - 1.0 - rowf))
    xi = jnp.exp(lg * (rowf + 1.0))
    return dmat, zeta, xi, math.exp(lg * rows)


def _rope(t, cosv, sinv):
    return t * cosv + pltpu.roll(t, RET_DK // 2, 1) * sinv


def _unrope(d, cosv, sinv):
    return d * cosv + pltpu.roll(d * sinv, RET_DK // 2, 1)


def _gla_common(p_ref, w2_ref, gb_ref, chunk, rows):
    row = lax.broadcasted_iota(jnp.int32, (CHUNK, 1), 0)
    real = (chunk * CHUNK + row) >= PAD_ROWS
    ga = p_ref[rows, O_GA:O_GA + 128]
    z = _dot(ga, w2_ref[...]) + gb_ref[...]
    la = (jnp.minimum(z, 0.0) - jnp.log(1.0 + jnp.exp(-jnp.abs(z)))) * (1.0 / GLA_TAU)
    la = jnp.where(real, la, 0.0)
    ri = lax.broadcasted_iota(jnp.int32, (CHUNK, CHUNK), 0)
    ci = lax.broadcasted_iota(jnp.int32, (CHUNK, CHUNK), 1)
    tril = (ri >= ci).astype(F32)
    cum = _dot_exact_rhs(tril, la)
    last = cum[CHUNK - 1:CHUNK, :]
    qs = p_ref[rows, O_GQ:O_GQ + 256] * (GLA_DK ** -0.5)
    k = p_ref[rows, O_GK:O_GK + 256]
    ecum = jnp.exp(cum)
    ekl = jnp.exp(last - cum)
    el = jnp.exp(last)
    refs = [jnp.zeros((1, 256), F32)] + [cum[a * SUB - 1:a * SUB, :] for a in range(1, N_SUB)]
    eq = [jnp.exp(cum[a * SUB:(a + 1) * SUB, :] - refs[a]) for a in range(N_SUB)]
    spread = refs[0] - cum[SUB - 1:SUB, :]
    for a in range(1, N_SUB):
        spread = jnp.maximum(spread, refs[a] - cum[(a + 1) * SUB - 1:(a + 1) * SUB, :])
    small = jnp.max(spread) <= GLA_FACTORED_MAX
    return dict(real=real, row=row, z=z, la=la, cum=cum, last=last, qs=qs, k=k, ecum=ecum, ekl=ekl, el=el,
                refs=refs, eq=eq, small=small, ri=ri, ci=ci)


GLA_FACTORED_MAX = 40.0


def _head_block_mask():
    r = lax.broadcasted_iota(jnp.int32, (CHUNK, 256), 0)
    col = lax.broadcasted_iota(jnp.int32, (CHUNK, 256), 1)
    return (r // SUB) == (col // GLA_DK)


def _gla_factored(c):
    mask = _head_block_mask()
    eks, keys, queries = [], [], []
    for a in range(N_SUB):
        ek = jnp.exp(jnp.minimum(c["refs"][a] - c["cum"], GLA_FACTORED_MAX))
        qh = c["qs"][a * SUB:(a + 1) * SUB, :] * c["eq"][a]
        eks.append(ek)
        keys.append(c["k"] * ek)
        queries.append(jnp.where(mask, jnp.concatenate([qh] * GLA_HEADS, axis=0), 0.0))
    return eks, keys, queries


def _gla_scores_factored(c, factored, p_scr):
    _, keys, queries = factored
    for a in range(N_SUB):
        out = _dot_nt(queries[a], keys[a])
        out = jnp.where(c["ci"] <= a * SUB + (c["ri"] & (SUB - 1)), out, 0.0)
        for h in range(GLA_HEADS):
            p_scr[h, a * SUB:(a + 1) * SUB, :] = out[h * SUB:(h + 1) * SUB, :]


def _gla_intra_bwd_factored(c, factored, dps, dq_scr, dk_scr):
    eks, keys, queries = factored
    mask = _head_block_mask()
    dk = jnp.zeros((CHUNK, 256), F32)
    for a in range(N_SUB):
        dpa = jnp.concatenate([dps[h][a * SUB:(a + 1) * SUB, :] for h in range(GLA_HEADS)], axis=0)
        dq = jnp.where(mask, _dot(dpa, keys[a]), 0.0)
        dq = dq[0:SUB] + dq[SUB:2 * SUB] + dq[2 * SUB:3 * SUB] + dq[3 * SUB:4 * SUB]
        dq_scr[a * SUB:(a + 1) * SUB, :] = dq * c["eq"][a]
        dk = dk + _dot_tn(dpa, queries[a]) * eks[a]
    dk_scr[...] = dk


def _gla_lag_weights(c):
    cum, row = c["cum"], c["row"]
    out = [jnp.ones((CHUNK, 256), F32)]
    for r in range(1, SUB):
        out.append(jnp.where((row % SUB) >= r, jnp.exp(jnp.minimum(cum - pltpu.roll(cum, r, 0), 0.0)), 0.0))
    return out


def _gla_pairwise_keys(c):
    return [None] + [c["k"] * jnp.exp(jnp.minimum(c["refs"][a] - c["cum"], 0.0)) for a in range(1, N_SUB)]


def _gla_scores_pairwise(c, lag_w, keys, h):
    sl = slice(GLA_DK * h, GLA_DK * (h + 1))
    qs, k = c["qs"][:, sl], c["k"][:, sl]
    ri, ci = c["ri"], c["ci"]
    p = jnp.zeros((CHUNK, CHUNK), F32)
    for r in range(SUB):
        kr = k if r == 0 else pltpu.roll(k, r, 0)
        pr = jnp.sum(qs * kr * lag_w[r][:, sl], axis=1, keepdims=True)
        p = p + jnp.where(ci == ri - r, pr, 0.0)
    blocks = [jnp.zeros((SUB, CHUNK), F32)]
    for a in range(1, N_SUB):
        qh = qs[a * SUB:(a + 1) * SUB, :] * c["eq"][a][:, sl]
        blocks.append(jnp.where(ci[:SUB, :] < a * SUB, _dot_nt(qh, keys[a][:, sl]), 0.0))
    return p + jnp.concatenate(blocks, axis=0)


def _gla_all_scores(c, p_scr):
    @pl.when(c["small"])
    def _():
        _gla_scores_factored(c, _gla_factored(c), p_scr)

    @pl.when(jnp.logical_not(c["small"]))
    def _():
        lag_w, keys = _gla_lag_weights(c), _gla_pairwise_keys(c)
        for h in range(GLA_HEADS):
            p_scr[h] = _gla_scores_pairwise(c, lag_w, keys, h)


def _gla_intra_bwd_pairwise(c, lag_w, keys, dp, h):
    sl = slice(GLA_DK * h, GLA_DK * (h + 1))
    qs_h, k_h = c["qs"][:, sl], c["k"][:, sl]
    ri, ci = c["ri"], c["ci"]
    dq_rows = [jnp.zeros((SUB, GLA_DK), F32)]
    dk = jnp.zeros((CHUNK, GLA_DK), F32)
    for a in range(1, N_SUB):
        eq = c["eq"][a][:, sl]
        qh = qs_h[a * SUB:(a + 1) * SUB, :] * eq
        dpa = jnp.where(ci[:SUB, :] < a * SUB, dp[a * SUB:(a + 1) * SUB, :], 0.0)
        dq_rows.append(_dot(dpa, keys[a][:, sl]) * eq)
        ek = jnp.exp(jnp.minimum(c["refs"][a][:, sl] - c["cum"][:, sl], 0.0))
        dk = dk + _dot_tn(dpa, qh) * ek
    dq = jnp.concatenate(dq_rows, axis=0)
    for r in range(SUB):
        w = lag_w[r][:, sl]
        dpr = jnp.sum(jnp.where(ci == ri - r, dp, 0.0), axis=1, keepdims=True)
        kr = k_h if r == 0 else pltpu.roll(k_h, r, 0)
        dq = dq + dpr * kr * w
        back = dpr * qs_h * w
        dk = dk + (back if r == 0 else pltpu.roll(back, CHUNK - r, 0))
    return dq, dk


def _gla_all_intra_bwd(c, dps, p_scr, dq_scr, dk_scr):
    @pl.when(c["small"])
    def _():
        factored = _gla_factored(c)
        _gla_scores_factored(c, factored, p_scr)
        _gla_intra_bwd_factored(c, factored, dps, dq_scr, dk_scr)

    @pl.when(jnp.logical_not(c["small"]))
    def _():
        lag_w, keys = _gla_lag_weights(c), _gla_pairwise_keys(c)
        outs = [_gla_intra_bwd_pairwise(c, lag_w, keys, dps[h], h) for h in range(GLA_HEADS)]
        for h in range(GLA_HEADS):
            p_scr[h] = _gla_scores_pairwise(c, lag_w, keys, h)
        dq_scr[...] = jnp.concatenate([o[0] for o in outs], axis=1)
        dk_scr[...] = jnp.concatenate([o[1] for o in outs], axis=1)


def _mixer_fwd(proj, cos2, sin2, w2p, gb, rnw, gnw, name, carried=()):
    n_carried = len(carried)

    def body(*refs):
        p_ref, c_ref, s_ref, w2_ref, gb_ref, rnw_ref, gnw_ref = refs[:7]
        x_refs, refs = refs[7:7 + n_carried], refs[7 + n_carried:]
        ocat_ref, mrg_ref, sr_out, sg_out = refs[:4]
        gathered_refs, refs = refs[4:4 + n_carried], refs[4 + n_carried:]
        sr, sg, p_scr = refs[:3]
        n = pl.program_id(0)
        if n_carried:
            start, forward, finish = _gather_phases(x_refs, gathered_refs, *refs[3:])
            pl.when(n == 0)(start)
            pl.when(n == N_STEPS // 2)(forward)

        @pl.when(n == 0)
        def _():
            sr[...] = jnp.zeros_like(sr)
            sg[...] = jnp.zeros_like(sg)

        sr_out[0] = sr[...]
        cosv, sinv = c_ref[...], s_ref[...]

        for h in range(RET_HEADS):
            dmat, zeta, xi, gc = _ret_consts(h)
            hs = slice(128 * h, 128 * (h + 1))
            q = _rope(p_ref[:, O_RQ + 128 * h:O_RQ + 128 * (h + 1)], cosv, sinv)
            k = _rope(p_ref[:, O_RK + 128 * h:O_RK + 128 * (h + 1)], cosv, sinv) * (RET_DK ** -0.5)
            v = p_ref[:, O_RV + 128 * h:O_RV + 128 * (h + 1)]
            g = p_ref[:, O_RG + 128 * h:O_RG + 128 * (h + 1)]
            s_in = sr[h]
            a = _dot_nt(q, k) * dmat
            o = _dot(a, v) + _dot(q, s_in) * xi
            sr[h] = gc * s_in + _dot_tn(k * zeta, v)
            mu = jnp.mean(o, axis=-1, keepdims=True)
            xc = o - mu
            nrm = xc * lax.rsqrt(jnp.mean(xc * xc, axis=-1, keepdims=True) + EPS)
            ocat_ref[:, hs] = o
            mrg_ref[:, hs] = (nrm * rnw_ref[:, hs] * (g * _sigmoid(g))).astype(BF16)

        for j in range(CHUNKS_PER_STEP):
            rows = slice(CHUNK * j, CHUNK * (j + 1))
            sg_out[j] = sg[...]
            c = _gla_common(p_ref, w2_ref, gb_ref, n * CHUNKS_PER_STEP + j, rows)
            _gla_all_scores(c, p_scr)
            lastcol = _dot_tn_exact_lhs(c["la"], jnp.ones((CHUNK, GLA_DV), F32))
            qe = c["qs"] * c["ecum"]
            kl = c["k"] * c["ekl"]
            for h in range(GLA_HEADS):
                sl = slice(GLA_DK * h, GLA_DK * (h + 1))
                hs = slice(512 + 128 * h, 512 + 128 * (h + 1))
                v = p_ref[rows, O_GV + 128 * h:O_GV + 128 * (h + 1)]
                g = p_ref[rows, O_GR + 128 * h:O_GR + 128 * (h + 1)]
                s_in = sg[h]
                o = _dot(p_scr[h], v) + _dot(qe[:, sl], s_in)
                sg[h] = jnp.exp(lastcol[GLA_DK * h:GLA_DK * (h + 1), :]) * s_in + _dot_tn(kl[:, sl], v)
                nrm = o * lax.rsqrt(jnp.mean(o * o, axis=-1, keepdims=True) + EPS)
                ocat_ref[rows, hs] = o
                mrg_ref[rows, hs] = (nrm * gnw_ref[:, 128 * h:128 * (h + 1)] * (g * _sigmoid(g))).astype(BF16)

        if n_carried:
            pl.when(n == N_STEPS - 1)(finish)

    const = lambda shape: pl.BlockSpec(shape, lambda n: (0,) * len(shape))
    anywhere = [pl.BlockSpec(memory_space=pl.ANY)] * n_carried
    return pl.pallas_call(
        body, name=name, grid=(N_STEPS,),
        in_specs=[pl.BlockSpec((STEP_ROWS, IN_WP), lambda n: (n, 0)),
                  pl.BlockSpec((STEP_ROWS, 128), lambda n: (n, 0)), pl.BlockSpec((STEP_ROWS, 128), lambda n: (n, 0)),
                  const((128, 256)), const((1, 256)), const((1, 512)), const((1, 512))] + anywhere,
        out_specs=[pl.BlockSpec((STEP_ROWS, D), lambda n: (n, 0)), pl.BlockSpec((STEP_ROWS, D), lambda n: (n, 0)),
                   pl.BlockSpec((1, RET_HEADS, RET_DK, 128), lambda n: (n, 0, 0, 0)),
                   pl.BlockSpec((CHUNKS_PER_STEP, GLA_HEADS, GLA_DK, GLA_DV), lambda n: (n, 0, 0, 0))] + anywhere,
        out_shape=[jax.ShapeDtypeStruct((LP, D), F32), jax.ShapeDtypeStruct((LP, D), BF16),
                   jax.ShapeDtypeStruct((N_STEPS, RET_HEADS, RET_DK, 128), F32),
                   jax.ShapeDtypeStruct((N_CHUNKS, GLA_HEADS, GLA_DK, GLA_DV), F32)] + _gathered_shapes(carried),
        scratch_shapes=[pltpu.VMEM((RET_HEADS, RET_DK, 128), F32), pltpu.VMEM((GLA_HEADS, GLA_DK, GLA_DV), F32),
                        pltpu.VMEM((GLA_HEADS, CHUNK, CHUNK), F32)] + _exchange_sems(n_carried),
        compiler_params=_cparams("arbitrary"),
    )(proj, cos2, sin2, w2p, gb, rnw, gnw, *carried)


def _mixer_bwd(proj, ocat, dmrg, sr_all, sg_all, cos2, sin2, w2p, gb, rnw, gnw, name, carried=()):
    last_step = N_STEPS - 1
    n_carried = len(carried)

    def body(*refs):
        p_ref, ocat_ref, dm_ref, sr_ref, sg_ref, c_ref, s_ref, w2_ref, gb_ref, rnw_ref, gnw_ref = refs[:11]
        g_refs, refs = refs[11:11 + n_carried], refs[11 + n_carried:]
        dp_ref, dw2_ref, dgb_ref, drn_ref, dgn_ref = refs[:5]
        got_refs, refs = refs[5:5 + n_carried], refs[5 + n_carried:]
        dsr, dsg, p_scr, dq_scr, dk_scr = refs[:5]
        step = pl.program_id(0)
        n = last_step - step
        if n_carried:
            start, finish = _exchange_phases(g_refs, got_refs, *refs[5:])
            pl.when(step == 0)(start)

        @pl.when(step == 0)
        def _():
            dsr[...] = jnp.zeros_like(dsr)
            dsg[...] = jnp.zeros_like(dsg)
            dw2_ref[...] = jnp.zeros_like(dw2_ref)
            dgb_ref[...] = jnp.zeros_like(dgb_ref)
            drn_ref[...] = jnp.zeros_like(drn_ref)
            dgn_ref[...] = jnp.zeros_like(dgn_ref)

        cosv, sinv = c_ref[...], s_ref[...]
        step_row = lax.broadcasted_iota(jnp.int32, (STEP_ROWS, 1), 0)
        real = ((n * STEP_ROWS + step_row) >= PAD_ROWS).astype(F32)

        for h in range(RET_HEADS):
            dmat, zeta, xi, gc = _ret_consts(h)
            hs = slice(128 * h, 128 * (h + 1))
            q = _rope(p_ref[:, O_RQ + 128 * h:O_RQ + 128 * (h + 1)], cosv, sinv)
            k = _rope(p_ref[:, O_RK + 128 * h:O_RK + 128 * (h + 1)], cosv, sinv) * (RET_DK ** -0.5)
            v = p_ref[:, O_RV + 128 * h:O_RV + 128 * (h + 1)]
            g = p_ref[:, O_RG + 128 * h:O_RG + 128 * (h + 1)]
            o = ocat_ref[:, hs]
            dy = dm_ref[:, hs]
            wv = rnw_ref[:, hs]
            mu = jnp.mean(o, axis=-1, keepdims=True)
            xc = o - mu
            rs = lax.rsqrt(jnp.mean(xc * xc, axis=-1, keepdims=True) + EPS)
            nrm = xc * rs
            sgm = _sigmoid(g)
            sil = g * sgm
            drn_ref[0:1, hs] += jnp.sum(dy * nrm * sil, axis=0, keepdims=True)
            dgate = dy * nrm * wv * (sgm * (1.0 + g * (1.0 - sgm)))
            dn = dy * wv * sil
            do = rs * (dn - jnp.mean(dn, axis=-1, keepdims=True) - nrm * jnp.mean(dn * nrm, axis=-1, keepdims=True))
            s_in = sr_ref[0, h]
            ds_out = dsr[h]
            a = _dot_nt(q, k) * dmat
            da = _dot_nt(do, v) * dmat
            dox = do * xi
            dq = _dot(da, k) + _dot_nt(dox, s_in)
            dk = _dot_tn(da, q) + _dot_nt(v, ds_out) * zeta
            dv = _dot_tn(a, do) + _dot(k * zeta, ds_out)
            dsr[h] = gc * ds_out + _dot_tn(q, dox)
            dk = dk * (RET_DK ** -0.5)
            dp_ref[:, O_RQ + 128 * h:O_RQ + 128 * (h + 1)] = (_unrope(dq, cosv, sinv) * real).astype(BF16)
            dp_ref[:, O_RK + 128 * h:O_RK + 128 * (h + 1)] = (_unrope(dk, cosv, sinv) * real).astype(BF16)
            dp_ref[:, O_RV + 128 * h:O_RV + 128 * (h + 1)] = (dv * real).astype(BF16)
            dp_ref[:, O_RG + 128 * h:O_RG + 128 * (h + 1)] = (dgate * real).astype(BF16)

        for j in reversed(range(CHUNKS_PER_STEP)):
            gla_chunk_bwd(n * CHUNKS_PER_STEP + j, slice(CHUNK * j, CHUNK * (j + 1)), j, p_ref, ocat_ref, dm_ref,
                          sg_ref, w2_ref, gb_ref, gnw_ref, dp_ref, dw2_ref, dgb_ref, dgn_ref, dsg, p_scr, dq_scr, dk_scr)

        if n_carried:
            pl.when(step == last_step)(finish)

    def gla_chunk_bwd(chunk, rows, j, p_ref, ocat_ref, dm_ref, sg_ref, w2_ref, gb_ref, gnw_ref,
                      dp_ref, dw2_ref, dgb_ref, dgn_ref, dsg, p_scr, dq_scr, dk_scr):
        row = lax.broadcasted_iota(jnp.int32, (CHUNK, 1), 0)
        real = ((chunk * CHUNK + row) >= PAD_ROWS).astype(F32)
        c = _gla_common(p_ref, w2_ref, gb_ref, chunk, rows)
        ri, ci = c["ri"], c["ci"]
        causal = ri >= ci
        triu = (ci >= ri).astype(F32)
        qe = c["qs"] * c["ecum"]
        kl = c["k"] * c["ekl"]
        lastcol = _dot_tn_exact_lhs(c["la"], jnp.ones((CHUNK, GLA_DV), F32))
        dla_heads, dq_heads, dk_heads = [], [], []
        dos, dps = [], []
        for h in range(GLA_HEADS):
            hs = slice(512 + 128 * h, 512 + 128 * (h + 1))
            v = p_ref[rows, O_GV + 128 * h:O_GV + 128 * (h + 1)]
            g = p_ref[rows, O_GR + 128 * h:O_GR + 128 * (h + 1)]
            o = ocat_ref[rows, hs]
            dy = dm_ref[rows, hs]
            wv = gnw_ref[:, 128 * h:128 * (h + 1)]
            rs = lax.rsqrt(jnp.mean(o * o, axis=-1, keepdims=True) + EPS)
            nrm = o * rs
            sgm = _sigmoid(g)
            sil = g * sgm
            dgn_ref[0:1, 128 * h:128 * (h + 1)] += jnp.sum(dy * nrm * sil, axis=0, keepdims=True)
            dgate = dy * nrm * wv * (sgm * (1.0 + g * (1.0 - sgm)))
            dn = dy * wv * sil
            do = rs * (dn - nrm * jnp.mean(dn * nrm, axis=-1, keepdims=True))
            dp_ref[rows, O_GR + 128 * h:O_GR + 128 * (h + 1)] = (dgate * real).astype(BF16)
            dos.append(do)
            dps.append(jnp.where(causal, _dot_nt(do, v), 0.0))
        _gla_all_intra_bwd(c, dps, p_scr, dq_scr, dk_scr)
        dq_intra, dk_intra = dq_scr[...], dk_scr[...]
        for h in range(GLA_HEADS):
            sl = slice(GLA_DK * h, GLA_DK * (h + 1))
            v = p_ref[rows, O_GV + 128 * h:O_GV + 128 * (h + 1)]
            do = dos[h]
            qs_h, k_h = c["qs"][:, sl], c["k"][:, sl]
            s_in = sg_ref[j, h]
            ds_out = dsg[h]
            el_col = jnp.exp(lastcol[GLA_DK * h:GLA_DK * (h + 1), :])
            dv = _dot_tn(p_scr[h], do) + _dot(kl[:, sl], ds_out)
            dqe = _dot_nt(do, s_in)
            dkl = _dot_nt(v, ds_out)
            dsg[h] = _dot_tn(qe[:, sl], do) + el_col * ds_out
            sd = s_in * ds_out
            sd_hi = sd.astype(BF16)
            sd_lo = (sd - sd_hi.astype(F32)).astype(BF16)
            ones8 = jnp.ones((8, GLA_DV), BF16)
            nt = (((1,), (1,)), ((), ()))
            d_el = (lax.dot_general(ones8, sd_hi, nt, preferred_element_type=F32)
                    + lax.dot_general(ones8, sd_lo, nt, preferred_element_type=F32))[0:1, :]
            dqs = dqe * c["ecum"][:, sl] + dq_intra[:, sl]
            dkk = dkl * c["ekl"][:, sl] + dk_intra[:, sl]
            d_last = jnp.sum(dkl * kl[:, sl], axis=0, keepdims=True) + d_el * c["el"][:, sl]
            dcum = qs_h * dqs - k_h * dkk + jnp.where(row == CHUNK - 1, d_last, 0.0)
            dla_heads.append(_dot_exact_rhs(triu, dcum))
            dq_heads.append(dqs * (GLA_DK ** -0.5))
            dk_heads.append(dkk)
            dp_ref[rows, O_GV + 128 * h:O_GV + 128 * (h + 1)] = (dv * real).astype(BF16)

        dla = jnp.concatenate(dla_heads, axis=1)
        dp_ref[rows, O_GQ:O_GQ + 256] = (jnp.concatenate(dq_heads, axis=1) * real).astype(BF16)
        dp_ref[rows, O_GK:O_GK + 256] = (jnp.concatenate(dk_heads, axis=1) * real).astype(BF16)
        dz = dla * (1.0 / GLA_TAU) * _sigmoid(-c["z"]) * real
        ga = p_ref[rows, O_GA:O_GA + 128]
        dp_ref[rows, O_GA:O_GA + 128] = _dot_nt(dz, w2_ref[...]).astype(BF16)
        dp_ref[rows, O_GA + 128:IN_WP] = jnp.zeros((CHUNK, IN_WP - O_GA - 128), BF16)
        dw2_ref[...] += _dot_tn(ga, dz)
        dgb_ref[0:1, :] += jnp.sum(dz, axis=0, keepdims=True)

    const = lambda shape: pl.BlockSpec(shape, lambda s: (0,) * len(shape))
    rev = lambda s: (last_step - s, 0)
    anywhere = [pl.BlockSpec(memory_space=pl.ANY)] * n_carried
    return pl.pallas_call(
        body, name=name, grid=(N_STEPS,),
        in_specs=[pl.BlockSpec((STEP_ROWS, IN_WP), rev), pl.BlockSpec((STEP_ROWS, D), rev),
                  pl.BlockSpec((STEP_ROWS, D), rev),
                  pl.BlockSpec((1, RET_HEADS, RET_DK, 128), lambda s: (last_step - s, 0, 0, 0)),
                  pl.BlockSpec((CHUNKS_PER_STEP, GLA_HEADS, GLA_DK, GLA_DV), lambda s: (last_step - s, 0, 0, 0)),
                  pl.BlockSpec((STEP_ROWS, 128), rev), pl.BlockSpec((STEP_ROWS, 128), rev),
                  const((128, 256)), const((1, 256)), const((1, 512)), const((1, 512))] + anywhere,
        out_specs=[pl.BlockSpec((STEP_ROWS, IN_WP), rev), const((128, 256)), const((8, 256)),
                   const((8, 512)), const((8, 512))] + anywhere,
        out_shape=[jax.ShapeDtypeStruct((LP, IN_WP), BF16), jax.ShapeDtypeStruct((128, 256), F32),
                   jax.ShapeDtypeStruct((8, 256), F32), jax.ShapeDtypeStruct((8, 512), F32),
                   jax.ShapeDtypeStruct((8, 512), F32)] + [jax.ShapeDtypeStruct(g.shape, g.dtype) for g in carried],
        scratch_shapes=[pltpu.VMEM((RET_HEADS, RET_DK, 128), F32), pltpu.VMEM((GLA_HEADS, GLA_DK, GLA_DV), F32),
                        pltpu.VMEM((GLA_HEADS, CHUNK, CHUNK), F32), pltpu.VMEM((CHUNK, 256), F32),
                        pltpu.VMEM((CHUNK, 256), F32)] + _exchange_sems(n_carried),
        compiler_params=_cparams("arbitrary"),
    )(proj, ocat, dmrg, sr_all, sg_all, cos2, sin2, w2p, gb, rnw, gnw, *carried)


def _all_gather(xs, name):
    n = len(xs)

    def body(*refs):
        start, forward, finish = _gather_phases(refs[:n], refs[n:2 * n], *refs[2 * n:])
        start()
        forward()
        finish()

    return pl.pallas_call(
        body, name=name,
        in_specs=[pl.BlockSpec(memory_space=pl.ANY)] * n,
        out_specs=[pl.BlockSpec(memory_space=pl.ANY)] * n,
        out_shape=_gathered_shapes(xs),
        scratch_shapes=_exchange_sems(n),
    )(*xs)


def _gathered_shapes(xs):
    return [jax.ShapeDtypeStruct((N_DEV,) + x.shape, x.dtype) for x in xs]


def _exchange_sems(n):
    if n == 0:
        return []
    return [pltpu.SemaphoreType.DMA((7 * n,)), pltpu.SemaphoreType.DMA((7 * n,)), pltpu.SemaphoreType.DMA((n,))]


def _gather_phases(x_refs, out_refs, send_sems, recv_sems, local_sems):
    n = len(x_refs)
    mx, my, mc = lax.axis_index("x"), lax.axis_index("y"), lax.axis_index("c")
    me, sibling = (mx, my, mc), (mx, my, 1 - mc)
    chips = [(1 - mx, my), (mx, 1 - my), (1 - mx, 1 - my)]

    def slot(a, px, py, pc):
        return out_refs[a].at[4 * px + 2 * py + pc]

    def copy(a, k, block, to, src=None):
        return pltpu.make_async_remote_copy(
            src_ref=slot(a, *block) if src is None else src, dst_ref=slot(a, *block),
            send_sem=send_sems.at[7 * a + k], recv_sem=recv_sems.at[7 * a + k],
            device_id=to, device_id_type=MESH_IDS)

    mine = [pltpu.make_async_copy(x_refs[a], slot(a, *me), local_sems.at[a]) for a in range(n)]
    first = []
    for a in range(n):
        first.append(copy(a, 0, me, sibling, src=x_refs[a]))
        first += [copy(a, 1 + j, me, (*chip, mc), src=x_refs[a]) for j, chip in enumerate(chips)]
    passed = [copy(a, 4 + j, (*chip, mc), sibling) for j, chip in enumerate(chips) for a in range(n)]

    def start():
        for cp in mine + first:
            cp.start()

    def forward():
        for j, chip in enumerate(chips):
            for a in range(n):
                copy(a, 1 + j, (*chip, mc), me).wait_recv()
                passed[j * n + a].start()

    def finish():
        for a in range(n):
            copy(a, 0, sibling, me).wait_recv()
            for j, chip in enumerate(chips):
                copy(a, 4 + j, (*chip, 1 - mc), me).wait_recv()
        for cp in first + passed:
            cp.wait_send()
        for cp in mine:
            cp.wait()

    return start, forward, finish


def _exchange_blocks(gs, name):
    n = len(gs)

    def body(*refs):
        start, finish = _exchange_phases(refs[:n], refs[n:2 * n], *refs[2 * n:])
        start()
        finish()

    return pl.pallas_call(
        body, name=name,
        in_specs=[pl.BlockSpec(memory_space=pl.ANY)] * n,
        out_specs=[pl.BlockSpec(memory_space=pl.ANY)] * n,
        out_shape=[jax.ShapeDtypeStruct(g.shape, g.dtype) for g in gs],
        scratch_shapes=_exchange_sems(n),
    )(*gs)


def _exchange_phases(g_refs, out_refs, send_sems, recv_sems, local_sems):
    n = len(g_refs)
    mx, my, mc = lax.axis_index("x"), lax.axis_index("y"), lax.axis_index("c")
    me = 4 * mx + 2 * my + mc
    mine = [pltpu.make_async_copy(g_refs[a].at[me], out_refs[a].at[me], local_sems.at[a]) for a in range(n)]
    copies = []
    for r in range(1, N_DEV):
        px, py, pc = mx ^ (r >> 2), my ^ ((r >> 1) & 1), mc ^ (r & 1)
        peer = 4 * px + 2 * py + pc
        for a in range(n):
            copies.append(pltpu.make_async_remote_copy(
                src_ref=g_refs[a].at[peer], dst_ref=out_refs[a].at[me],
                send_sem=send_sems.at[7 * a + r - 1], recv_sem=recv_sems.at[7 * a + r - 1],
                device_id=(px, py, pc), device_id_type=MESH_IDS))

    def start():
        for cp in mine + copies:
            cp.start()

    def finish():
        for cp in copies:
            cp.wait_recv()
        for cp in copies:
            cp.wait_send()
        for cp in mine:
            cp.wait()

    return start, finish


IN_SHARD = IN_W // N_DEV
IN_SHARD_P = 512
UP_SHARD = D_UP // N_DEV
UP_SHARD_P = 768
RELAYOUT_ROWS = 256


def _pieces_w_in():
    return [(k, 0, IN_SHARD * k, IN_SHARD) for k in range(N_DEV)]


def _pieces_ffn_up():
    pieces = []
    for k in range(N_DEV):
        n, end = UP_SHARD * k, UP_SHARD * (k + 1)
        while n < end:
            half, r = divmod(n, D_FF)
            blk, off = divmod(r, CONV_BLOCK)
            run = min(CONV_BLOCK - off, end - n)
            pieces.append((k, n - UP_SHARD * k, 2 * CONV_BLOCK * blk + CONV_BLOCK * half + off, run))
            n += run
    return pieces


def _assemble_block(load, spans, dst_block, rows):
    lo = 128 * dst_block
    lane = lax.broadcasted_iota(jnp.int32, (1, 128), 1)
    out = jnp.zeros((rows, 128), F32)
    for key, src_off, dst_off, length in spans:
        a, b = max(lo, dst_off), min(lo + 128, dst_off + length)
        s, s_end = src_off + (a - dst_off), src_off + (b - dst_off)
        d = a
        while s < s_end:
            e = min(s_end, 128 * (s // 128 + 1))
            blk = load(key, s // 128)
            shift = (d - s) % 128
            if shift:
                blk = pltpu.roll(blk, shift, 1)
            out = jnp.where((lane >= d - lo) & (lane < d - lo + (e - s)), blk, out)
            d += e - s
            s = e
    return out


def _shards_to_cols(shards, pieces, width, name):
    _, rows, _ = shards.shape
    tr = RELAYOUT_ROWS

    def body(s_ref, o_ref):
        load = lambda k, b: s_ref[k, :, 128 * b:128 * (b + 1)].astype(F32)
        for db in range(width // 128):
            o_ref[:, 128 * db:128 * (db + 1)] = _assemble_block(load, pieces, db, tr).astype(BF16)

    return pl.pallas_call(
        body, name=name, grid=(rows // tr,),
        in_specs=[pl.BlockSpec((N_DEV, tr, shards.shape[2]), lambda i: (0, i, 0))],
        out_specs=pl.BlockSpec((tr, width), lambda i: (i, 0)),
        out_shape=jax.ShapeDtypeStruct((rows, width), BF16),
        compiler_params=_cparams("parallel"),
    )(shards)


def _cols_to_shards(full, pieces, shard_width, name):
    rows, width = full.shape
    tr = RELAYOUT_ROWS

    def body(f_ref, o_ref):
        load = lambda _, b: f_ref[:, 128 * b:128 * (b + 1)].astype(F32)
        for k in range(N_DEV):
            spans = [(None, dst_off, src_off, length) for dev, src_off, dst_off, length in pieces if dev == k]
            for db in range(shard_width // 128):
                o_ref[k, :, 128 * db:128 * (db + 1)] = _assemble_block(load, spans, db, tr).astype(BF16)

    return pl.pallas_call(
        body, name=name, grid=(rows // tr,),
        in_specs=[pl.BlockSpec((tr, width), lambda i: (i, 0))],
        out_specs=pl.BlockSpec((N_DEV, tr, shard_width), lambda i: (0, i, 0)),
        out_shape=jax.ShapeDtypeStruct((N_DEV, rows, shard_width), BF16),
        compiler_params=_cparams("parallel"),
    )(full)


def _adamw(parts, w, m, v, rows_per_step, name):
    rows, cols = w.shape
    assert rows % rows_per_step == 0 and parts.shape == (N_DEV, rows, cols)

    def body(p_ref, w_ref, m_ref, v_ref, g_ref, d_ref, nm_ref, nv_ref):
        g = p_ref[0].astype(F32)
        for j in range(1, N_DEV):
            g = g + p_ref[j].astype(F32)
        m_new = ADAM_B1 * m_ref[...] + (1.0 - ADAM_B1) * g
        v_new = ADAM_B2 * v_ref[...] + (1.0 - ADAM_B2) * (g * g)
        m_hat = m_new / (1.0 - ADAM_B1 ** ADAM_STEP)
        v_hat = v_new / (1.0 - ADAM_B2 ** ADAM_STEP)
        g_ref[...] = g
        d_ref[...] = -ADAM_LR * (m_hat / (jnp.sqrt(v_hat) + ADAM_EPS) + ADAM_WD * w_ref[...])
        nm_ref[...] = m_new
        nv_ref[...] = v_new

    tile = pl.BlockSpec((rows_per_step, cols), lambda i: (i, 0))
    shape = jax.ShapeDtypeStruct((rows, cols), F32)
    return pl.pallas_call(
        body, name=name, grid=(rows // rows_per_step,),
        in_specs=[pl.BlockSpec((N_DEV, rows_per_step, cols), lambda i: (0, i, 0)), tile, tile, tile],
        out_specs=[tile, tile, tile, tile],
        out_shape=[shape, shape, shape, shape],
        compiler_params=_cparams("parallel"),
    )(parts, w, m, v)


BIG = (("w_in", (DEPTH, D, IN_W // N_DEV), 2), ("w_out", (DEPTH, D // N_DEV, D), 1),
       ("ffn_up", (DEPTH, D, D_UP // N_DEV), 2), ("ffn_down", (DEPTH, D_FF // N_DEV, D), 1))
SMALL = (("meta_tokens", (N_META, D // N_DEV), 1), ("gla_gate_w2", (DEPTH, GATE_RANK, 256 // N_DEV), 2),
         ("ffn_conv_w", (DEPTH, 3, D_UP // N_DEV), 2))
REPL = (("pre_mix_norm", (DEPTH, D)), ("gla_gate_b", (DEPTH, 256)), ("ret_norm_w", (DEPTH, 512)),
        ("gla_norm_w", (DEPTH, 512)), ("post_mix_norm", (DEPTH, D)), ("pre_ffn_norm", (DEPTH, D)),
        ("ffn_conv_b", (DEPTH, D_UP)), ("post_ffn_norm", (DEPTH, D)))
WEIGHT_ORDER = ("meta_tokens", "pre_mix_norm", "w_in", "gla_gate_w2", "gla_gate_b", "ret_norm_w", "gla_norm_w",
                "w_out", "post_mix_norm", "pre_ffn_norm", "ffn_up", "ffn_conv_w", "ffn_conv_b", "ffn_down",
                "post_ffn_norm")


def _size(shape):
    return math.prod(shape)


def _round_up(n, mult):
    return -(-n // mult) * mult


REPL_ROWS = _round_up(-(-sum(_size(s) for _, s in REPL) // LANES), 8)
SMALL_ROWS = _round_up(-(-sum(_size(s) for _, s, _ in SMALL) // LANES), 8)


def _pack(arrays, rows, dtype):
    flat = jnp.concatenate([a.reshape(-1).astype(dtype) for a in arrays])
    return jnp.pad(flat, (0, rows * LANES - flat.shape[0])).reshape(rows, LANES)


def _unpack(buf, shapes):
    flat = buf.reshape(-1)
    out, off = [], 0
    for shape in shapes:
        out.append(flat[off:off + _size(shape)].reshape(shape))
        off += _size(shape)
    return out


def _unshard(blocks, axis):
    moved = jnp.moveaxis(blocks, 0, axis)
    shape = list(moved.shape)
    shape[axis:axis + 2] = [shape[axis] * shape[axis + 1]]
    return moved.reshape(shape)


def _to_blocks(full, axis):
    shape = list(full.shape)
    shape[axis:axis + 1] = [N_DEV, shape[axis] // N_DEV]
    return jnp.moveaxis(full.reshape(shape), axis, 0)


def _interleave_cols(w):
    lead = w.shape[:-1]
    return jnp.swapaxes(w.reshape(lead + (2, N_CONV_BLOCKS, CONV_BLOCK)), -3, -2).reshape(lead + (D_UP,))


def _deinterleave_cols(w):
    lead = w.shape[:-1]
    return jnp.swapaxes(w.reshape(lead + (N_CONV_BLOCKS, 2, CONV_BLOCK)), -3, -2).reshape(lead + (D_UP,))


def _rope_tables():
    half = RET_DK // 2
    inv = ROPE_BASE ** (-jnp.arange(half, dtype=F32) / half)
    pos = jnp.arange(LP, dtype=F32) - float(PAD_ROWS)
    ang = pos[:, None] * inv[None, :]
    c, s = jnp.cos(ang), jnp.sin(ang)
    return jnp.concatenate([c, c], axis=1), jnp.concatenate([-s, s], axis=1)


def kernel(x, meta_tokens, pre_mix_norm, w_in, gla_gate_w2, gla_gate_b, ret_norm_w, gla_norm_w, w_out, post_mix_norm, pre_ffn_norm, ffn_up, ffn_conv_w, ffn_conv_b, ffn_down, post_ffn_norm, loss_target, m_meta_tokens, m_pre_mix_norm, m_w_in, m_gla_gate_w2, m_gla_gate_b, m_ret_norm_w, m_gla_norm_w, m_w_out, m_post_mix_norm, m_pre_ffn_norm, m_ffn_up, m_ffn_conv_w, m_ffn_conv_b, m_ffn_down, m_post_ffn_norm, v_meta_tokens, v_pre_mix_norm, v_w_in, v_gla_gate_w2, v_gla_gate_b, v_ret_norm_w, v_gla_norm_w, v_w_out, v_post_mix_norm, v_pre_ffn_norm, v_ffn_up, v_ffn_conv_w, v_ffn_conv_b, v_ffn_down, v_post_ffn_norm):
    weights = dict(meta_tokens=meta_tokens, pre_mix_norm=pre_mix_norm, w_in=w_in, gla_gate_w2=gla_gate_w2,
                   gla_gate_b=gla_gate_b, ret_norm_w=ret_norm_w, gla_norm_w=gla_norm_w, w_out=w_out,
                   post_mix_norm=post_mix_norm, pre_ffn_norm=pre_ffn_norm, ffn_up=ffn_up, ffn_conv_w=ffn_conv_w,
                   ffn_conv_b=ffn_conv_b, ffn_down=ffn_down, post_ffn_norm=post_ffn_norm)
    mom1 = dict(meta_tokens=m_meta_tokens, pre_mix_norm=m_pre_mix_norm, w_in=m_w_in, gla_gate_w2=m_gla_gate_w2,
                gla_gate_b=m_gla_gate_b, ret_norm_w=m_ret_norm_w, gla_norm_w=m_gla_norm_w, w_out=m_w_out,
                post_mix_norm=m_post_mix_norm, pre_ffn_norm=m_pre_ffn_norm, ffn_up=m_ffn_up,
                ffn_conv_w=m_ffn_conv_w, ffn_conv_b=m_ffn_conv_b, ffn_down=m_ffn_down, post_ffn_norm=m_post_ffn_norm)
    mom2 = dict(meta_tokens=v_meta_tokens, pre_mix_norm=v_pre_mix_norm, w_in=v_w_in, gla_gate_w2=v_gla_gate_w2,
                gla_gate_b=v_gla_gate_b, ret_norm_w=v_ret_norm_w, gla_norm_w=v_gla_norm_w, w_out=v_w_out,
                post_mix_norm=v_post_mix_norm, pre_ffn_norm=v_pre_ffn_norm, ffn_up=v_ffn_up,
                ffn_conv_w=v_ffn_conv_w, ffn_conv_b=v_ffn_conv_b, ffn_down=v_ffn_down, post_ffn_norm=v_post_ffn_norm)

    pad_cols = lambda a, width: jnp.pad(a, ((0, 0), (0, width - a.shape[1])))
    big_names = [n for n, _, _ in BIG]
    shard = {}
    for l in range(DEPTH):
        shard[l, "w_in"] = pad_cols(w_in[l].astype(BF16), IN_SHARD_P)
        shard[l, "w_out"] = w_out[l].astype(BF16)
        shard[l, "ffn_up"] = pad_cols(ffn_up[l].astype(BF16), UP_SHARD_P)
        shard[l, "ffn_down"] = ffn_down[l].astype(BF16)
    gathered = {(0, "w_in"): _all_gather([shard[0, "w_in"]], "gather_w_in_0")[0]}
    gather_in_mixer = {l: [(l, n) for n in big_names[1:]] + ([(l + 1, "w_in")] if l + 1 < DEPTH else [])
                       for l in range(DEPTH)}
    small = _all_gather([_pack([weights[n] for n, _, _ in SMALL], SMALL_ROWS, F32)], "gather_small_weights")[0]
    small_parts = _unpack_blocks(small, [s for _, s, _ in SMALL])
    full = {n: _unshard(p, ax) for (n, _, ax), p in zip(SMALL, small_parts)}
    w2p = jnp.pad(full["gla_gate_w2"], ((0, 0), (0, 128 - GATE_RANK), (0, 0)))
    cw8 = jnp.concatenate([_interleave_cols(full["ffn_conv_w"]), _interleave_cols(ffn_conv_b)[:, None, :],
                           jnp.zeros((DEPTH, 4, D_UP), F32)], axis=1)
    cos2, sin2 = _rope_tables()

    h = jnp.concatenate([jnp.zeros((PAD_ROWS, D), F32), full["meta_tokens"], x[0]], axis=0)
    target = jnp.concatenate([jnp.zeros((CHUNK, D), F32), loss_target[0]], axis=0)
    saved, layer_w = [], []
    for l in range(DEPTH):
        lw = dict(w_in=_shards_to_cols(gathered[l, "w_in"], _pieces_w_in(), IN_WP, f"w_in_cols_{l}"))
        a1 = _rmsnorm_fwd(h, pre_mix_norm[l:l + 1], f"pre_mix_norm_{l}")
        proj = _matmul(a1, lw["w_in"], out_dtype=F32, tm=TM, tn=1280, tk=D, name=f"in_proj_{l}")
        keys = gather_in_mixer.get(l, [])
        ocat, merged, sr_all, sg_all, *got = _mixer_fwd(proj, cos2, sin2, w2p[l], gla_gate_b[l:l + 1],
                                                        ret_norm_w[l:l + 1], gla_norm_w[l:l + 1], f"mixer_fwd_{l}",
                                                        carried=[shard[key] for key in keys])
        gathered.update(zip(keys, got))
        lw["w_out"] = gathered[l, "w_out"].reshape(D, D)
        lw["w_up"] = _shards_to_cols(gathered[l, "ffn_up"], _pieces_ffn_up(), D_UP, f"ffn_up_cols_{l}")
        lw["w_down"] = gathered[l, "ffn_down"].reshape(D_FF, D)
        layer_w.append(lw)
        m = _matmul(merged, lw["w_out"], out_dtype=F32, tm=TM, tn=D, tk=D, name=f"out_proj_{l}")
        h1 = _resid_norm(h, m, post_mix_norm[l:l + 1], f"post_mix_norm_{l}")
        a2 = _rmsnorm_fwd(h1, pre_ffn_norm[l:l + 1], f"pre_ffn_norm_{l}")
        u = _matmul(a2, lw["w_up"], out_dtype=BF16, tm=TM, tn=1408, tk=D, name=f"ffn_up_{l}")
        cv, act = _conv_act_fwd(u, cw8[l], f"ffn_conv_act_{l}")
        f = _matmul(act, lw["w_down"], out_dtype=F32, tm=TM, tn=D, tk=D_FF, name=f"ffn_down_{l}")
        h2 = _resid_norm(h1, f, post_ffn_norm[l:l + 1], f"post_ffn_norm_{l}")
        saved.append(dict(h=h, a1=a1, proj=proj, ocat=ocat, merged=merged, sr=sr_all, sg=sg_all, m=m, h1=h1,
                          a2=a2, u=u, cv=cv, act=act, f=f))
        h = h2

    dh, loss_acc = _loss_head(h, target, "loss_head")
    loss = lax.psum(loss_acc[0, 0], ("x", "y", "c"))

    kinds = ("grad", "delta", "new_m", "new_v")
    grads = {n: [None] * DEPTH for n in WEIGHT_ORDER if n != "meta_tokens" and n not in big_names}
    pending, parts = [], {}
    for l in reversed(range(DEPTH)):
        s, lw = saved[l], layer_w[l]
        df, g_post_ffn = _norm_bwd(dh, s["f"], post_ffn_norm[l:l + 1], None, BF16, f"post_ffn_norm_bwd_{l}")
        dact = _matmul(df, lw["w_down"], tb=True, out_dtype=BF16, tm=TM, tn=D_FF, tk=D, name=f"ffn_down_dx_{l}")
        g_down = _matmul(s["act"], df, ta=True, out_dtype=BF16, tm=D_FF // 2, tn=D, tk=TK_ROWS, name=f"ffn_down_dw_{l}")
        du, dcw = _conv_act_bwd(dact, s["cv"], s["u"], cw8[l], f"ffn_conv_act_bwd_{l}")
        da2 = _matmul(du, lw["w_up"], tb=True, out_dtype=F32, tm=TM, tn=D, tk=1408, name=f"ffn_up_dx_{l}")
        g_up = _matmul(s["a2"], du, ta=True, out_dtype=BF16, tm=D, tn=1408, tk=TK_ROWS, name=f"ffn_up_dw_{l}")
        dh1, g_pre_ffn = _norm_bwd(da2, s["h1"], pre_ffn_norm[l:l + 1], dh, F32, f"pre_ffn_norm_bwd_{l}")
        dm, g_post_mix = _norm_bwd(dh1, s["m"], post_mix_norm[l:l + 1], None, BF16, f"post_mix_norm_bwd_{l}")
        dmerged = _matmul(dm, lw["w_out"], tb=True, out_dtype=F32, tm=TM, tn=D, tk=D, name=f"out_proj_dx_{l}")
        g_out = _matmul(s["merged"], dm, ta=True, out_dtype=BF16, tm=D, tn=D, tk=TK_ROWS, name=f"out_proj_dw_{l}")
        pending += [((l, "ffn_down"), g_down.reshape(N_DEV, D_FF // N_DEV, D)),
                    ((l, "ffn_up"), _cols_to_shards(g_up, _pieces_ffn_up(), UP_SHARD_P, f"ffn_up_grad_shards_{l}")),
                    ((l, "w_out"), g_out.reshape(N_DEV, D // N_DEV, D))]
        dproj, g_w2, g_gb, g_rn, g_gn, *got = _mixer_bwd(s["proj"], s["ocat"], dmerged, s["sr"], s["sg"], cos2, sin2,
                                                         w2p[l], gla_gate_b[l:l + 1], ret_norm_w[l:l + 1],
                                                         gla_norm_w[l:l + 1], f"mixer_bwd_{l}",
                                                         carried=[blocks for _, blocks in pending])
        parts.update(zip([key for key, _ in pending], got))
        da1 = _matmul(dproj, lw["w_in"], tb=True, out_dtype=F32, tm=TM, tn=D, tk=1280, name=f"in_proj_dx_{l}")
        g_in = _matmul(s["a1"], dproj, ta=True, out_dtype=BF16, tm=D, tn=1280, tk=TK_ROWS, name=f"in_proj_dw_{l}")
        pending = [((l, "w_in"), _cols_to_shards(g_in, _pieces_w_in(), IN_SHARD_P, f"w_in_grad_shards_{l}"))]
        dh, g_pre_mix = _norm_bwd(da1, s["h"], pre_mix_norm[l:l + 1], dh1, F32, f"pre_mix_norm_bwd_{l}")
        grads["post_ffn_norm"][l] = g_post_ffn[0]
        grads["ffn_conv_w"][l] = _deinterleave_cols(dcw[0:3])
        grads["ffn_conv_b"][l] = _deinterleave_cols(dcw[3])
        grads["pre_ffn_norm"][l] = g_pre_ffn[0]
        grads["post_mix_norm"][l] = g_post_mix[0]
        grads["gla_gate_w2"][l] = g_w2[:GATE_RANK]
        grads["gla_gate_b"][l] = g_gb[0]
        grads["ret_norm_w"][l] = g_rn[0]
        grads["gla_norm_w"][l] = g_gn[0]
        grads["pre_mix_norm"][l] = g_pre_mix[0]
    local = {n: jnp.stack(v) for n, v in grads.items()}
    local["meta_tokens"] = dh[PAD_ROWS:CHUNK]
    grad_x = dh[CHUNK:][None]

    blocks = jnp.concatenate([_to_blocks(local[n], ax).reshape(N_DEV, -1) for n, _, ax in SMALL], axis=1)
    blocks = jnp.pad(blocks, ((0, 0), (0, SMALL_ROWS * LANES - blocks.shape[1]))).reshape(N_DEV, SMALL_ROWS, LANES)
    *got, small_grad_parts = _exchange_blocks([b for _, b in pending] + [blocks], "exchange_last_grads")
    parts.update(zip([key for key, _ in pending], got))

    widths = dict(w_in=IN_SHARD_P, w_out=D, ffn_up=UP_SHARD_P, ffn_down=D)
    steps = dict(w_in=256, w_out=D // N_DEV, ffn_up=256, ffn_down=D_FF // N_DEV // 2)
    big_out = {kind: {n: [None] * DEPTH for n in big_names} for kind in kinds}
    for l in range(DEPTH):
        for n in big_names:
            mine = [pad_cols(d[n][l], widths[n]) for d in (weights, mom1, mom2)]
            results = _adamw(parts[l, n], *mine, steps[n], f"adamw_{n}_{l}")
            for kind, r in zip(kinds, results):
                big_out[kind][n][l] = r[:, :weights[n].shape[2]]
    out = {kind: {n: jnp.stack(v) for n, v in big_out[kind].items()} for kind in kinds}
    shard_shapes = [s for _, s, _ in SMALL]
    packed = [_pack([d[n] for n, _, _ in SMALL], SMALL_ROWS, F32) for d in (weights, mom1, mom2)]
    results = _adamw(small_grad_parts, *packed, SMALL_ROWS, "adamw_small_sharded")
    for kind, buf in zip(kinds, results):
        out[kind].update(zip([n for n, _, _ in SMALL], _unpack(buf, shard_shapes)))

    repl_parts = _all_gather([_pack([local[n] for n, _ in REPL], REPL_ROWS, F32)], "gather_small_grads")[0]
    packed = [_pack([d[n] for n, _ in REPL], REPL_ROWS, F32) for d in (weights, mom1, mom2)]
    results = _adamw(repl_parts, *packed, REPL_ROWS, "adamw_replicated")
    repl_shapes = [s for _, s in REPL]
    for kind, buf in zip(kinds, results):
        out[kind].update(zip([n for n, _ in REPL], _unpack(buf, repl_shapes)))

    return (loss, grad_x, *[out["grad"][n] for n in WEIGHT_ORDER], *[out["delta"][n] for n in WEIGHT_ORDER],
            *[out["new_m"][n] for n in WEIGHT_ORDER], *[out["new_v"][n] for n in WEIGHT_ORDER])


def _unpack_blocks(gathered, shapes):
    flat = gathered.reshape(N_DEV, -1)
    out, off = [], 0
    for shape in shapes:
        out.append(flat[:, off:off + _size(shape)].reshape((N_DEV,) + shape))
        off += _size(shape)
    return out
```

```python
import math

import jax
import jax.numpy as jnp
from jax import lax
from jax.experimental import pallas as pl
from jax.experimental.pallas import tpu as pltpu

F32 = jnp.float32
BF16 = jnp.bfloat16

D = 1024
SEQ = 8192
DEPTH = 2
N_META = 16
CHUNK = 64
SUB = 16
N_SUB = CHUNK // SUB
PAD_ROWS = CHUNK - N_META
LP = SEQ + CHUNK
N_CHUNKS = LP // CHUNK
RET_HEADS = 4
RET_DK = 128
GLA_HEADS = 4
GLA_DK = 64
GLA_DV = 128
GLA_TAU = 16.0
GATE_RANK = 16
IN_W = 3600
IN_WP = 3840
D_FF = 2816
D_UP = 2 * D_FF
CONV_BLOCK = 256
N_CONV_BLOCKS = D_FF // CONV_BLOCK
ROPE_BASE = 10000.0
EPS = 1e-6
N_DEV = 8
LANES = 1024

O_RQ, O_RK, O_RV, O_RG = 0, 512, 1024, 1536
O_GQ, O_GK, O_GV, O_GR, O_GA = 2048, 2304, 2560, 3072, 3584

ADAM_LR = 0.001
ADAM_B1 = 0.9
ADAM_B2 = 0.999
ADAM_EPS = 1e-08
ADAM_WD = 0.01
ADAM_STEP = 10

VMEM_LIMIT = 56 * 1024 * 1024
MESH_IDS = pl.DeviceIdType.MESH


def _row_tile(rows, limit):
    best = 16
    for t in range(16, min(rows, limit) + 1, 16):
        if rows % t == 0:
            best = t
    return best


TM = _row_tile(LP, 688)
TK_ROWS = _row_tile(LP, 1376)


def _cparams(*sem):
    return pltpu.CompilerParams(dimension_semantics=sem, vmem_limit_bytes=VMEM_LIMIT)


def _dot(a, b):
    return jnp.dot(a.astype(BF16), b.astype(BF16), preferred_element_type=F32)


def _dot_nt(a, b):
    return lax.dot_general(a.astype(BF16), b.astype(BF16), (((1,), (1,)), ((), ())), preferred_element_type=F32)


def _dot_tn(a, b):
    return lax.dot_general(a.astype(BF16), b.astype(BF16), (((0,), (0,)), ((), ())), preferred_element_type=F32)


def _split3(x):
    hi = x.astype(BF16)
    r1 = x - hi.astype(F32)
    mid = r1.astype(BF16)
    lo = (r1 - mid.astype(F32)).astype(BF16)
    return hi, mid, lo


def _dot_exact_rhs(t, x):
    hi, mid, lo = _split3(x)
    t = t.astype(BF16)
    return (jnp.dot(t, hi, preferred_element_type=F32) + jnp.dot(t, mid, preferred_element_type=F32)
            + jnp.dot(t, lo, preferred_element_type=F32))


def _dot_tn_exact_lhs(x, ones):
    dims = (((0,), (0,)), ((), ()))
    hi, mid, lo = _split3(x)
    ones = ones.astype(BF16)
    return (lax.dot_general(hi, ones, dims, preferred_element_type=F32)
            + lax.dot_general(mid, ones, dims, preferred_element_type=F32)
            + lax.dot_general(lo, ones, dims, preferred_element_type=F32))


def _sigmoid(x):
    return 1.0 / (1.0 + jnp.exp(-x))


def _matmul(a, b, *, ta=False, tb=False, out_dtype, tm, tn, tk, name, n_outer=False):
    m = a.shape[1] if ta else a.shape[0]
    k = a.shape[0] if ta else a.shape[1]
    n = b.shape[0] if tb else b.shape[1]
    assert (b.shape[1] if tb else b.shape[0]) == k
    assert m % tm == 0 and n % tn == 0 and k % tk == 0, (name, m, n, k, tm, tn, tk)
    nk = k // tk
    order = (lambda f: (lambda j, i, kk: f(i, j, kk))) if n_outer else (lambda f: f)
    a_spec = (pl.BlockSpec((tk, tm), order(lambda i, j, kk: (kk, i))) if ta
              else pl.BlockSpec((tm, tk), order(lambda i, j, kk: (i, kk))))
    b_spec = (pl.BlockSpec((tn, tk), order(lambda i, j, kk: (j, kk))) if tb
              else pl.BlockSpec((tk, tn), order(lambda i, j, kk: (kk, j))))
    dims = (((0 if ta else 1,), (1 if tb else 0,)), ((), ()))

    def body(a_ref, b_ref, o_ref, *acc):
        prod = lax.dot_general(a_ref[...].astype(BF16), b_ref[...].astype(BF16), dims, preferred_element_type=F32)
        if nk == 1:
            o_ref[...] = prod.astype(out_dtype)
            return
        acc_ref, = acc
        kk = pl.program_id(2)

        @pl.when(kk == 0)
        def _():
            acc_ref[...] = prod

        @pl.when(kk > 0)
        def _():
            acc_ref[...] += prod

        @pl.when(kk == nk - 1)
        def _():
            o_ref[...] = acc_ref[...].astype(out_dtype)

    return pl.pallas_call(
        body, name=name, grid=(n // tn, m // tm, nk) if n_outer else (m // tm, n // tn, nk),
        in_specs=[a_spec, b_spec],
        out_specs=pl.BlockSpec((tm, tn), order(lambda i, j, kk: (i, j))),
        out_shape=jax.ShapeDtypeStruct((m, n), out_dtype),
        scratch_shapes=[pltpu.VMEM((tm, tn), F32)] if nk > 1 else [],
        compiler_params=_cparams("parallel", "parallel", "arbitrary"),
    )(a, b)


def _matmul_resid_norm(a, b, h, w, name):
    k = a.shape[1]

    def body(a_ref, b_ref, h_ref, w_ref, m_ref, o_ref):
        m = jnp.dot(a_ref[...].astype(BF16), b_ref[...].astype(BF16), preferred_element_type=F32)
        m_ref[...] = m
        r = lax.rsqrt(jnp.mean(m * m, axis=-1, keepdims=True) + EPS)
        row = pl.program_id(0) * TM + lax.broadcasted_iota(jnp.int32, (TM, 1), 0)
        o_ref[...] = h_ref[...] + jnp.where(row >= PAD_ROWS, m * r * w_ref[...], 0.0)

    tile = pl.BlockSpec((TM, D), lambda i: (i, 0))
    return pl.pallas_call(
        body, name=name, grid=(LP // TM,),
        in_specs=[pl.BlockSpec((TM, k), lambda i: (i, 0)), pl.BlockSpec((k, D), lambda i: (0, 0)), tile,
                  pl.BlockSpec((1, D), lambda i: (0, 0))],
        out_specs=[tile, tile],
        out_shape=[jax.ShapeDtypeStruct((LP, D), F32), jax.ShapeDtypeStruct((LP, D), F32)],
        compiler_params=_cparams("parallel"),
    )(a, b, h, w)


def _rmsnorm_bwd_rows(dy, x, w):
    r = lax.rsqrt(jnp.mean(x * x, axis=-1, keepdims=True) + EPS)
    g = dy * w
    dx = r * g - x * (r * r * r * jnp.mean(g * x, axis=-1, keepdims=True))
    return dx, jnp.sum(dy * x * r, axis=0, keepdims=True)


def _matmul_norm_bwd(dz, b, x, w, resid, tk, name):
    k = dz.shape[1]
    assert k % tk == 0
    nk = k // tk

    def body(a_ref, b_ref, x_ref, w_ref, r_ref, dx_ref, dw_ref, *acc):
        i, kk = pl.program_id(0), pl.program_id(1)

        @pl.when((i == 0) & (kk == 0))
        def _():
            dw_ref[...] = jnp.zeros_like(dw_ref)

        prod = lax.dot_general(a_ref[...].astype(BF16), b_ref[...].astype(BF16), (((1,), (1,)), ((), ())),
                               preferred_element_type=F32)

        def finish(dy):
            dx, dw = _rmsnorm_bwd_rows(dy, x_ref[...], w_ref[...])
            dx_ref[...] = dx + r_ref[...]
            dw_ref[0:1, :] += dw

        if nk == 1:
            finish(prod)
            return
        acc_ref, = acc

        @pl.when(kk == 0)
        def _():
            acc_ref[...] = prod

        @pl.when((kk > 0) & (kk < nk - 1))
        def _():
            acc_ref[...] += prod

        @pl.when(kk == nk - 1)
        def _():
            finish(acc_ref[...] + prod)

    tile = pl.BlockSpec((TM, D), lambda i, kk: (i, 0))
    return pl.pallas_call(
        body, name=name, grid=(LP // TM, nk),
        in_specs=[pl.BlockSpec((TM, tk), lambda i, kk: (i, kk)), pl.BlockSpec((D, tk), lambda i, kk: (0, kk)), tile,
                  pl.BlockSpec((1, D), lambda i, kk: (0, 0)), tile],
        out_specs=[tile, pl.BlockSpec((8, D), lambda i, kk: (0, 0))],
        out_shape=[jax.ShapeDtypeStruct((LP, D), F32), jax.ShapeDtypeStruct((8, D), F32)],
        scratch_shapes=[pltpu.VMEM((TM, D), F32)] if nk > 1 else [],
        compiler_params=_cparams("arbitrary", "arbitrary"),
    )(dz, b, x, w, resid)


def _norm_bwd_matmul(dh, x, w, b, out_dtype, name):
    n = b.shape[0]

    def body(dh_ref, x_ref, w_ref, b_ref, o_ref, dx_ref, dw_ref):
        i = pl.program_id(0)

        @pl.when(i == 0)
        def _():
            dw_ref[...] = jnp.zeros_like(dw_ref)

        row = i * TM + lax.broadcasted_iota(jnp.int32, (TM, 1), 0)
        dy = jnp.where(row >= PAD_ROWS, dh_ref[...], 0.0)
        dx, dw = _rmsnorm_bwd_rows(dy, x_ref[...], w_ref[...])
        dxb = dx.astype(BF16)
        dx_ref[...] = dxb
        dw_ref[0:1, :] += dw
        o_ref[...] = lax.dot_general(dxb, b_ref[...].astype(BF16), (((1,), (1,)), ((), ())),
                                     preferred_element_type=F32).astype(out_dtype)

    tile = pl.BlockSpec((TM, D), lambda i: (i, 0))
    return pl.pallas_call(
        body, name=name, grid=(LP // TM,),
        in_specs=[tile, tile, pl.BlockSpec((1, D), lambda i: (0, 0)), pl.BlockSpec((n, D), lambda i: (0, 0))],
        out_specs=[pl.BlockSpec((TM, n), lambda i: (i, 0)), tile, pl.BlockSpec((8, D), lambda i: (0, 0))],
        out_shape=[jax.ShapeDtypeStruct((LP, n), out_dtype), jax.ShapeDtypeStruct((LP, D), BF16),
                   jax.ShapeDtypeStruct((8, D), F32)],
        compiler_params=_cparams("arbitrary"),
    )(dh, x, w, b)


def _rmsnorm_fwd(x, w, name):
    def body(x_ref, w_ref, o_ref):
        xv = x_ref[...]
        r = lax.rsqrt(jnp.mean(xv * xv, axis=-1, keepdims=True) + EPS)
        o_ref[...] = (xv * r * w_ref[...]).astype(BF16)

    return pl.pallas_call(
        body, name=name, grid=(LP // TM,),
        in_specs=[pl.BlockSpec((TM, D), lambda i: (i, 0)), pl.BlockSpec((1, D), lambda i: (0, 0))],
        out_specs=pl.BlockSpec((TM, D), lambda i: (i, 0)),
        out_shape=jax.ShapeDtypeStruct((LP, D), BF16),
        compiler_params=_cparams("parallel"),
    )(x, w)


def _loss_head(y, target, name):
    def body(y_ref, t_ref, dy_ref, loss_ref):
        i = pl.program_id(0)

        @pl.when(i == 0)
        def _():
            loss_ref[...] = jnp.zeros_like(loss_ref)

        row = i * TM + lax.broadcasted_iota(jnp.int32, (TM, 1), 0)
        diff = jnp.where(row >= CHUNK, y_ref[...] - t_ref[...], 0.0)
        dy_ref[...] = diff * (1.0 / D)
        loss_ref[...] += (0.5 / D) * jnp.sum(diff * diff)

    tile = pl.BlockSpec((TM, D), lambda i: (i, 0))
    return pl.pallas_call(
        body, name=name, grid=(LP // TM,),
        in_specs=[tile, tile],
        out_specs=[tile, pl.BlockSpec((8, 128), lambda i: (0, 0))],
        out_shape=[jax.ShapeDtypeStruct((LP, D), F32), jax.ShapeDtypeStruct((8, 128), F32)],
        compiler_params=_cparams("arbitrary"),
    )(y, target)


GELU_C = math.sqrt(2.0 / math.pi)
GELU_K = 0.044715
STRIP = 16


def _shift_down(x, prev8, rows):
    row = lax.broadcasted_iota(jnp.int32, (rows, 1), 0)
    p1 = pltpu.roll(prev8, 1, 0)
    p2 = pltpu.roll(prev8, 2, 0)
    x1 = jnp.where(row == 0, p1[0:1, :], pltpu.roll(x, 1, 0))
    x2 = jnp.where(row == 0, p2[0:1, :], jnp.where(row == 1, p2[1:2, :], pltpu.roll(x, 2, 0)))
    return x1, x2


def _conv_act_fwd(u, cw8, name):
    n_rows = LP // TM
    cb2 = 2 * CONV_BLOCK

    def body(u_ref, cw_ref, conv_ref, act_ref, carry_ref):
        i = pl.program_id(1)

        @pl.when(i == 0)
        def _():
            carry_ref[...] = jnp.zeros_like(carry_ref)

        x = u_ref[...].astype(F32)
        x1, x2 = _shift_down(x, carry_ref[...], TM)
        conv = cw_ref[3:4, :] + x2 * cw_ref[0:1, :] + x1 * cw_ref[1:2, :] + x * cw_ref[2:3, :]
        conv_ref[...] = conv.astype(BF16)
        a = conv[:, :CONV_BLOCK]
        g = conv[:, CONV_BLOCK:]
        t = jnp.tanh(GELU_C * (a + GELU_K * a * a * a))
        act_ref[...] = (0.5 * a * (1.0 + t) * g).astype(BF16)
        carry_ref[...] = x[TM - 8:TM, :]

    return pl.pallas_call(
        body, name=name, grid=(N_CONV_BLOCKS, n_rows),
        in_specs=[pl.BlockSpec((TM, cb2), lambda j, i: (i, j)), pl.BlockSpec((8, cb2), lambda j, i: (0, j))],
        out_specs=[pl.BlockSpec((TM, cb2), lambda j, i: (i, j)), pl.BlockSpec((TM, CONV_BLOCK), lambda j, i: (i, j))],
        out_shape=[jax.ShapeDtypeStruct((LP, D_UP), BF16), jax.ShapeDtypeStruct((LP, D_FF), BF16)],
        scratch_shapes=[pltpu.VMEM((8, cb2), F32)],
        compiler_params=_cparams("arbitrary", "arbitrary"),
    )(u, cw8)


def _conv_act_bwd(dact, conv, u, cw8, name):
    n_rows = LP // TM
    cb2 = 2 * CONV_BLOCK
    n_strips = TM // STRIP

    def body(dact_ref, conv_ref, u_ref, cw_ref, du_ref, dcw_ref, carry_ref):
        i = pl.program_id(1)

        @pl.when(i == 0)
        def _():
            dcw_ref[...] = jnp.zeros_like(dcw_ref)
            carry_ref[...] = jnp.zeros_like(carry_ref)

        w0, w1, w2 = cw_ref[0:1, :], cw_ref[1:2, :], cw_ref[2:3, :]
        row = lax.broadcasted_iota(jnp.int32, (STRIP, 1), 0)
        fold = lambda z: z[:8, :] + z[8:, :]

        def strip(k, carry):
            n1, n2, s0, s1, s2, s3 = carry
            r0 = pl.multiple_of((n_strips - 1 - k) * STRIP, STRIP)
            cv = conv_ref[pl.ds(r0, STRIP), :].astype(F32)
            a = cv[:, :CONV_BLOCK]
            g = cv[:, CONV_BLOCK:]
            t = jnp.tanh(GELU_C * (a + GELU_K * a * a * a))
            gel = 0.5 * a * (1.0 + t)
            dgel = 0.5 * (1.0 + t) + 0.5 * a * (1.0 - t * t) * (GELU_C * (1.0 + 3.0 * GELU_K * a * a))
            dav = dact_ref[pl.ds(r0, STRIP), :].astype(F32)
            dconv = jnp.concatenate([dav * g * dgel, dav * gel], axis=1)
            u1 = pltpu.roll(dconv, STRIP - 1, 0)
            u2 = pltpu.roll(dconv, STRIP - 2, 0)
            d1 = jnp.where(row >= STRIP - 1, n1, u1)
            d2 = jnp.where(row >= STRIP - 2, n2, u2)
            du_ref[pl.ds(r0, STRIP), :] = (dconv * w2 + d1 * w1 + d2 * w0).astype(BF16)
            x = u_ref[pl.ds(r0, STRIP), :].astype(F32)
            return (u1, u2, s0 + fold(d2 * x), s1 + fold(d1 * x), s2 + fold(dconv * x), s3 + fold(dconv))

        below = carry_ref[...]
        zero = jnp.zeros((8, cb2), F32)
        init = (pltpu.roll(below, STRIP - 1, 0), pltpu.roll(below, STRIP - 2, 0), zero, zero, zero, zero)
        u1, _, s0, s1, s2, s3 = lax.fori_loop(0, n_strips, strip, init)
        carry_ref[...] = pltpu.roll(u1, 1, 0)
        dcw_ref[0:1, :] += jnp.sum(s0, axis=0, keepdims=True)
        dcw_ref[1:2, :] += jnp.sum(s1, axis=0, keepdims=True)
        dcw_ref[2:3, :] += jnp.sum(s2, axis=0, keepdims=True)
        dcw_ref[3:4, :] += jnp.sum(s3, axis=0, keepdims=True)

    rev = lambda j, i: (n_rows - 1 - i, j)
    return pl.pallas_call(
        body, name=name, grid=(N_CONV_BLOCKS, n_rows),
        in_specs=[pl.BlockSpec((TM, CONV_BLOCK), rev), pl.BlockSpec((TM, cb2), rev), pl.BlockSpec((TM, cb2), rev),
                  pl.BlockSpec((8, cb2), lambda j, i: (0, j))],
        out_specs=[pl.BlockSpec((TM, cb2), rev), pl.BlockSpec((8, cb2), lambda j, i: (0, j))],
        out_shape=[jax.ShapeDtypeStruct((LP, D_UP), BF16), jax.ShapeDtypeStruct((8, D_UP), F32)],
        scratch_shapes=[pltpu.VMEM((STRIP, cb2), F32)],
        compiler_params=_cparams("arbitrary", "arbitrary"),
    )(dact, conv, u, cw8)


CHUNKS_PER_STEP = 3 if N_CHUNKS % 3 == 0 else 1
STEP_ROWS = CHUNKS_PER_STEP * CHUNK
N_STEPS = N_CHUNKS // CHUNKS_PER_STEP


def _ret_consts(h):
    rows = STEP_ROWS
    lg = math.log(1.0 - 2.0 ** (-5.0 - h))
    ri = lax.broadcasted_iota(jnp.int32, (rows, rows), 0)
    ci = lax.broadcasted_iota(jnp.int32, (rows, rows), 1)
    diff = (ri - ci).astype(F32)
    dmat = jnp.where(diff >= 0, jnp.exp(lg * jnp.maximum(diff, 0.0)), 0.0)
    rowf = lax.broadcasted_iota(jnp.int32, (rows, 1), 0).astype(F32)
    zeta = jnp.exp(lg * (rows - 1.0 - rowf))
    xi = jnp.exp(lg * (rowf + 1.0))
    return dmat, zeta, xi, math.exp(lg * rows)


def _rope(t, cosv, sinv):
    return t * cosv + pltpu.roll(t, RET_DK // 2, 1) * sinv


def _unrope(d, cosv, sinv):
    return d * cosv + pltpu.roll(d * sinv, RET_DK // 2, 1)


def _gla_common(p_ref, w2_ref, gb_ref, chunk, rows):
    row = lax.broadcasted_iota(jnp.int32, (CHUNK, 1), 0)
    real = (chunk * CHUNK + row) >= PAD_ROWS
    ga = p_ref[rows, O_GA:O_GA + 128]
    z = _dot(ga, w2_ref[...]) + gb_ref[...]
    la = (jnp.minimum(z, 0.0) - jnp.log(1.0 + jnp.exp(-jnp.abs(z)))) * (1.0 / GLA_TAU)
    la = jnp.where(real, la, 0.0)
    ri = lax.broadcasted_iota(jnp.int32, (CHUNK, CHUNK), 0)
    ci = lax.broadcasted_iota(jnp.int32, (CHUNK, CHUNK), 1)
    tril = (ri >= ci).astype(F32)
    cum = _dot_exact_rhs(tril, la)
    last = cum[CHUNK - 1:CHUNK, :]
    qs = p_ref[rows, O_GQ:O_GQ + 256] * (GLA_DK ** -0.5)
    k = p_ref[rows, O_GK:O_GK + 256]
    ecum = jnp.exp(cum)
    ekl = jnp.exp(last - cum)
    el = jnp.exp(last)
    refs = [jnp.zeros((1, 256), F32)] + [cum[a * SUB - 1:a * SUB, :] for a in range(1, N_SUB)]
    eq = [jnp.exp(cum[a * SUB:(a + 1) * SUB, :] - refs[a]) for a in range(N_SUB)]
    spread = refs[0] - cum[SUB - 1:SUB, :]
    for a in range(1, N_SUB):
        spread = jnp.maximum(spread, refs[a] - cum[(a + 1) * SUB - 1:(a + 1) * SUB, :])
    small = jnp.max(spread) <= GLA_FACTORED_MAX
    return dict(real=real, row=row, z=z, la=la, cum=cum, last=last, qs=qs, k=k, ecum=ecum, ekl=ekl, el=el,
                refs=refs, eq=eq, small=small, ri=ri, ci=ci)


GLA_FACTORED_MAX = 40.0


def _head_block_mask():
    r = lax.broadcasted_iota(jnp.int32, (CHUNK, 256), 0)
    col = lax.broadcasted_iota(jnp.int32, (CHUNK, 256), 1)
    return (r // SUB) == (col // GLA_DK)


def _gla_factored(c):
    mask = _head_block_mask()
    eks, keys, queries = [], [], []
    for a in range(N_SUB):
        ek = jnp.exp(jnp.minimum(c["refs"][a] - c["cum"], GLA_FACTORED_MAX))
        qh = c["qs"][a * SUB:(a + 1) * SUB, :] * c["eq"][a]
        eks.append(ek)
        keys.append(c["k"] * ek)
        queries.append(jnp.where(mask, jnp.concatenate([qh] * GLA_HEADS, axis=0), 0.0))
    return eks, keys, queries


def _gla_scores_factored(c, factored, p_scr):
    _, keys, queries = factored
    for a in range(N_SUB):
        out = _dot_nt(queries[a], keys[a])
        out = jnp.where(c["ci"] <= a * SUB + (c["ri"] & (SUB - 1)), out, 0.0)
        for h in range(GLA_HEADS):
            p_scr[h, a * SUB:(a + 1) * SUB, :] = out[h * SUB:(h + 1) * SUB, :]


def _gla_intra_bwd_factored(c, factored, dps, dq_scr, dk_scr):
    eks, keys, queries = factored
    mask = _head_block_mask()
    dk = jnp.zeros((CHUNK, 256), F32)
    for a in range(N_SUB):
        dpa = jnp.concatenate([dps[h][a * SUB:(a + 1) * SUB, :] for h in range(GLA_HEADS)], axis=0)
        dq = jnp.where(mask, _dot(dpa, keys[a]), 0.0)
        dq = dq[0:SUB] + dq[SUB:2 * SUB] + dq[2 * SUB:3 * SUB] + dq[3 * SUB:4 * SUB]
        dq_scr[a * SUB:(a + 1) * SUB, :] = dq * c["eq"][a]
        dk = dk + _dot_tn(dpa, queries[a]) * eks[a]
    dk_scr[...] = dk


def _gla_lag_weights(c):
    cum, row = c["cum"], c["row"]
    out = [jnp.ones((CHUNK, 256), F32)]
    for r in range(1, SUB):
        out.append(jnp.where((row % SUB) >= r, jnp.exp(jnp.minimum(cum - pltpu.roll(cum, r, 0), 0.0)), 0.0))
    return out


def _gla_pairwise_keys(c):
    return [None] + [c["k"] * jnp.exp(jnp.minimum(c["refs"][a] - c["cum"], 0.0)) for a in range(1, N_SUB)]


def _gla_scores_pairwise(c, lag_w, keys, h):
    sl = slice(GLA_DK * h, GLA_DK * (h + 1))
    qs, k = c["qs"][:, sl], c["k"][:, sl]
    ri, ci = c["ri"], c["ci"]
    p = jnp.zeros((CHUNK, CHUNK), F32)
    for r in range(SUB):
        kr = k if r == 0 else pltpu.roll(k, r, 0)
        pr = jnp.sum(qs * kr * lag_w[r][:, sl], axis=1, keepdims=True)
        p = p + jnp.where(ci == ri - r, pr, 0.0)
    blocks = [jnp.zeros((SUB, CHUNK), F32)]
    for a in range(1, N_SUB):
        qh = qs[a * SUB:(a + 1) * SUB, :] * c["eq"][a][:, sl]
        blocks.append(jnp.where(ci[:SUB, :] < a * SUB, _dot_nt(qh, keys[a][:, sl]), 0.0))
    return p + jnp.concatenate(blocks, axis=0)


def _gla_all_scores(c, p_scr):
    @pl.when(c["small"])
    def _():
        _gla_scores_factored(c, _gla_factored(c), p_scr)

    @pl.when(jnp.logical_not(c["small"]))
    def _():
        lag_w, keys = _gla_lag_weights(c), _gla_pairwise_keys(c)
        for h in range(GLA_HEADS):
            p_scr[h] = _gla_scores_pairwise(c, lag_w, keys, h)


def _gla_intra_bwd_pairwise(c, lag_w, keys, dp, h):
    sl = slice(GLA_DK * h, GLA_DK * (h + 1))
    qs_h, k_h = c["qs"][:, sl], c["k"][:, sl]
    ri, ci = c["ri"], c["ci"]
    dq_rows = [jnp.zeros((SUB, GLA_DK), F32)]
    dk = jnp.zeros((CHUNK, GLA_DK), F32)
    for a in range(1, N_SUB):
        eq = c["eq"][a][:, sl]
        qh = qs_h[a * SUB:(a + 1) * SUB, :] * eq
        dpa = jnp.where(ci[:SUB, :] < a * SUB, dp[a * SUB:(a + 1) * SUB, :], 0.0)
        dq_rows.append(_dot(dpa, keys[a][:, sl]) * eq)
        ek = jnp.exp(jnp.minimum(c["refs"][a][:, sl] - c["cum"][:, sl], 0.0))
        dk = dk + _dot_tn(dpa, qh) * ek
    dq = jnp.concatenate(dq_rows, axis=0)
    for r in range(SUB):
        w = lag_w[r][:, sl]
        dpr = jnp.sum(jnp.where(ci == ri - r, dp, 0.0), axis=1, keepdims=True)
        kr = k_h if r == 0 else pltpu.roll(k_h, r, 0)
        dq = dq + dpr * kr * w
        back = dpr * qs_h * w
        dk = dk + (back if r == 0 else pltpu.roll(back, CHUNK - r, 0))
    return dq, dk


def _gla_all_intra_bwd(c, dps, p_scr, dq_scr, dk_scr):
    @pl.when(c["small"])
    def _():
        factored = _gla_factored(c)
        _gla_scores_factored(c, factored, p_scr)
        _gla_intra_bwd_factored(c, factored, dps, dq_scr, dk_scr)

    @pl.when(jnp.logical_not(c["small"]))
    def _():
        lag_w, keys = _gla_lag_weights(c), _gla_pairwise_keys(c)
        outs = [_gla_intra_bwd_pairwise(c, lag_w, keys, dps[h], h) for h in range(GLA_HEADS)]
        for h in range(GLA_HEADS):
            p_scr[h] = _gla_scores_pairwise(c, lag_w, keys, h)
        dq_scr[...] = jnp.concatenate([o[0] for o in outs], axis=1)
        dk_scr[...] = jnp.concatenate([o[1] for o in outs], axis=1)


def _mixer_fwd(proj, cos2, sin2, w2p, gb, rnw, gnw, name, carried=()):
    n_carried = len(carried)

    def body(*refs):
        p_ref, c_ref, s_ref, w2_ref, gb_ref, rnw_ref, gnw_ref = refs[:7]
        x_refs, refs = refs[7:7 + n_carried], refs[7 + n_carried:]
        ocat_ref, mrg_ref, sr_out, sg_out = refs[:4]
        gathered_refs, refs = refs[4:4 + n_carried], refs[4 + n_carried:]
        sr, sg, p_scr = refs[:3]
        n = pl.program_id(0)
        if n_carried:
            start, forward, finish = _gather_phases(x_refs, gathered_refs, *refs[3:])
            pl.when(n == 0)(start)
            pl.when(n == N_STEPS // 2)(forward)

        @pl.when(n == 0)
        def _():
            sr[...] = jnp.zeros_like(sr)
            sg[...] = jnp.zeros_like(sg)

        sr_out[0] = sr[...]
        cosv, sinv = c_ref[...], s_ref[...]

        for h in range(RET_HEADS):
            dmat, zeta, xi, gc = _ret_consts(h)
            hs = slice(128 * h, 128 * (h + 1))
            q = _rope(p_ref[:, O_RQ + 128 * h:O_RQ + 128 * (h + 1)], cosv, sinv)
            k = _rope(p_ref[:, O_RK + 128 * h:O_RK + 128 * (h + 1)], cosv, sinv) * (RET_DK ** -0.5)
            v = p_ref[:, O_RV + 128 * h:O_RV + 128 * (h + 1)]
            g = p_ref[:, O_RG + 128 * h:O_RG + 128 * (h + 1)]
            s_in = sr[h]
            a = _dot_nt(q, k) * dmat
            o = _dot(a, v) + _dot(q, s_in) * xi
            sr[h] = gc * s_in + _dot_tn(k * zeta, v)
            mu = jnp.mean(o, axis=-1, keepdims=True)
            xc = o - mu
            nrm = xc * lax.rsqrt(jnp.mean(xc * xc, axis=-1, keepdims=True) + EPS)
            ocat_ref[:, hs] = o
            mrg_ref[:, hs] = (nrm * rnw_ref[:, hs] * (g * _sigmoid(g))).astype(BF16)

        for j in range(CHUNKS_PER_STEP):
            rows = slice(CHUNK * j, CHUNK * (j + 1))
            sg_out[j] = sg[...]
            c = _gla_common(p_ref, w2_ref, gb_ref, n * CHUNKS_PER_STEP + j, rows)
            _gla_all_scores(c, p_scr)
            lastcol = _dot_tn_exact_lhs(c["la"], jnp.ones((CHUNK, GLA_DV), F32))
            qe = c["qs"] * c["ecum"]
            kl = c["k"] * c["ekl"]
            for h in range(GLA_HEADS):
                sl = slice(GLA_DK * h, GLA_DK * (h + 1))
                hs = slice(512 + 128 * h, 512 + 128 * (h + 1))
                v = p_ref[rows, O_GV + 128 * h:O_GV + 128 * (h + 1)]
                g = p_ref[rows, O_GR + 128 * h:O_GR + 128 * (h + 1)]
                s_in = sg[h]
                o = _dot(p_scr[h], v) + _dot(qe[:, sl], s_in)
                sg[h] = jnp.exp(lastcol[GLA_DK * h:GLA_DK * (h + 1), :]) * s_in + _dot_tn(kl[:, sl], v)
                nrm = o * lax.rsqrt(jnp.mean(o * o, axis=-1, keepdims=True) + EPS)
                ocat_ref[rows, hs] = o
                mrg_ref[rows, hs] = (nrm * gnw_ref[:, 128 * h:128 * (h + 1)] * (g * _sigmoid(g))).astype(BF16)

        if n_carried:
            pl.when(n == N_STEPS - 1)(finish)

    const = lambda shape: pl.BlockSpec(shape, lambda n: (0,) * len(shape))
    anywhere = [pl.BlockSpec(memory_space=pl.ANY)] * n_carried
    return pl.pallas_call(
        body, name=name, grid=(N_STEPS,),
        in_specs=[pl.BlockSpec((STEP_ROWS, IN_WP), lambda n: (n, 0)),
                  pl.BlockSpec((STEP_ROWS, 128), lambda n: (n, 0)), pl.BlockSpec((STEP_ROWS, 128), lambda n: (n, 0)),
                  const((128, 256)), const((1, 256)), const((1, 512)), const((1, 512))] + anywhere,
        out_specs=[pl.BlockSpec((STEP_ROWS, D), lambda n: (n, 0)), pl.BlockSpec((STEP_ROWS, D), lambda n: (n, 0)),
                   pl.BlockSpec((1, RET_HEADS, RET_DK, 128), lambda n: (n, 0, 0, 0)),
                   pl.BlockSpec((CHUNKS_PER_STEP, GLA_HEADS, GLA_DK, GLA_DV), lambda n: (n, 0, 0, 0))] + anywhere,
        out_shape=[jax.ShapeDtypeStruct((LP, D), F32), jax.ShapeDtypeStruct((LP, D), BF16),
                   jax.ShapeDtypeStruct((N_STEPS, RET_HEADS, RET_DK, 128), F32),
                   jax.ShapeDtypeStruct((N_CHUNKS, GLA_HEADS, GLA_DK, GLA_DV), F32)] + _gathered_shapes(carried),
        scratch_shapes=[pltpu.VMEM((RET_HEADS, RET_DK, 128), F32), pltpu.VMEM((GLA_HEADS, GLA_DK, GLA_DV), F32),
                        pltpu.VMEM((GLA_HEADS, CHUNK, CHUNK), F32)] + _exchange_sems(n_carried),
        compiler_params=_cparams("arbitrary"),
    )(proj, cos2, sin2, w2p, gb, rnw, gnw, *carried)


def _mixer_bwd(proj, ocat, dmrg, sr_all, sg_all, cos2, sin2, w2p, gb, rnw, gnw, name, carried=()):
    last_step = N_STEPS - 1
    n_carried = len(carried)

    def body(*refs):
        p_ref, ocat_ref, dm_ref, sr_ref, sg_ref, c_ref, s_ref, w2_ref, gb_ref, rnw_ref, gnw_ref = refs[:11]
        g_refs, refs = refs[11:11 + n_carried], refs[11 + n_carried:]
        dp_ref, dw2_ref, dgb_ref, drn_ref, dgn_ref = refs[:5]
        got_refs, refs = refs[5:5 + n_carried], refs[5 + n_carried:]
        dsr, dsg, p_scr, dq_scr, dk_scr = refs[:5]
        step = pl.program_id(0)
        n = last_step - step
        if n_carried:
            start, finish = _exchange_phases(g_refs, got_refs, *refs[5:])
            pl.when(step == 0)(start)

        @pl.when(step == 0)
        def _():
            dsr[...] = jnp.zeros_like(dsr)
            dsg[...] = jnp.zeros_like(dsg)
            dw2_ref[...] = jnp.zeros_like(dw2_ref)
            dgb_ref[...] = jnp.zeros_like(dgb_ref)
            drn_ref[...] = jnp.zeros_like(drn_ref)
            dgn_ref[...] = jnp.zeros_like(dgn_ref)

        cosv, sinv = c_ref[...], s_ref[...]
        step_row = lax.broadcasted_iota(jnp.int32, (STEP_ROWS, 1), 0)
        real = ((n * STEP_ROWS + step_row) >= PAD_ROWS).astype(F32)

        for h in range(RET_HEADS):
            dmat, zeta, xi, gc = _ret_consts(h)
            hs = slice(128 * h, 128 * (h + 1))
            q = _rope(p_ref[:, O_RQ + 128 * h:O_RQ + 128 * (h + 1)], cosv, sinv)
            k = _rope(p_ref[:, O_RK + 128 * h:O_RK + 128 * (h + 1)], cosv, sinv) * (RET_DK ** -0.5)
            v = p_ref[:, O_RV + 128 * h:O_RV + 128 * (h + 1)]
            g = p_ref[:, O_RG + 128 * h:O_RG + 128 * (h + 1)]
            o = ocat_ref[:, hs]
            dy = dm_ref[:, hs]
            wv = rnw_ref[:, hs]
            mu = jnp.mean(o, axis=-1, keepdims=True)
            xc = o - mu
            rs = lax.rsqrt(jnp.mean(xc * xc, axis=-1, keepdims=True) + EPS)
            nrm = xc * rs
            sgm = _sigmoid(g)
            sil = g * sgm
            drn_ref[0:1, hs] += jnp.sum(dy * nrm * sil, axis=0, keepdims=True)
            dgate = dy * nrm * wv * (sgm * (1.0 + g * (1.0 - sgm)))
            dn = dy * wv * sil
            do = rs * (dn - jnp.mean(dn, axis=-1, keepdims=True) - nrm * jnp.mean(dn * nrm, axis=-1, keepdims=True))
            s_in = sr_ref[0, h]
            ds_out = dsr[h]
            a = _dot_nt(q, k) * dmat
            da = _dot_nt(do, v) * dmat
            dox = do * xi
            dq = _dot(da, k) + _dot_nt(dox, s_in)
            dk = _dot_tn(da, q) + _dot_nt(v, ds_out) * zeta
            dv = _dot_tn(a, do) + _dot(k * zeta, ds_out)
            dsr[h] = gc * ds_out + _dot_tn(q, dox)
            dk = dk * (RET_DK ** -0.5)
            dp_ref[:, O_RQ + 128 * h:O_RQ + 128 * (h + 1)] = (_unrope(dq, cosv, sinv) * real).astype(BF16)
            dp_ref[:, O_RK + 128 * h:O_RK + 128 * (h + 1)] = (_unrope(dk, cosv, sinv) * real).astype(BF16)
            dp_ref[:, O_RV + 128 * h:O_RV + 128 * (h + 1)] = (dv * real).astype(BF16)
            dp_ref[:, O_RG + 128 * h:O_RG + 128 * (h + 1)] = (dgate * real).astype(BF16)

        for j in reversed(range(CHUNKS_PER_STEP)):
            gla_chunk_bwd(n * CHUNKS_PER_STEP + j, slice(CHUNK * j, CHUNK * (j + 1)), j, p_ref, ocat_ref, dm_ref,
                          sg_ref, w2_ref, gb_ref, gnw_ref, dp_ref, dw2_ref, dgb_ref, dgn_ref, dsg, p_scr, dq_scr, dk_scr)

        if n_carried:
            pl.when(step == last_step)(finish)

    def gla_chunk_bwd(chunk, rows, j, p_ref, ocat_ref, dm_ref, sg_ref, w2_ref, gb_ref, gnw_ref,
                      dp_ref, dw2_ref, dgb_ref, dgn_ref, dsg, p_scr, dq_scr, dk_scr):
        row = lax.broadcasted_iota(jnp.int32, (CHUNK, 1), 0)
        real = ((chunk * CHUNK + row) >= PAD_ROWS).astype(F32)
        c = _gla_common(p_ref, w2_ref, gb_ref, chunk, rows)
        ri, ci = c["ri"], c["ci"]
        causal = ri >= ci
        triu = (ci >= ri).astype(F32)
        qe = c["qs"] * c["ecum"]
        kl = c["k"] * c["ekl"]
        lastcol = _dot_tn_exact_lhs(c["la"], jnp.ones((CHUNK, GLA_DV), F32))
        dla_heads, dq_heads, dk_heads = [], [], []
        dos, dps = [], []
        for h in range(GLA_HEADS):
            hs = slice(512 + 128 * h, 512 + 128 * (h + 1))
            v = p_ref[rows, O_GV + 128 * h:O_GV + 128 * (h + 1)]
            g = p_ref[rows, O_GR + 128 * h:O_GR + 128 * (h + 1)]
            o = ocat_ref[rows, hs]
            dy = dm_ref[rows, hs]
            wv = gnw_ref[:, 128 * h:128 * (h + 1)]
            rs = lax.rsqrt(jnp.mean(o * o, axis=-1, keepdims=True) + EPS)
            nrm = o * rs
            sgm = _sigmoid(g)
            sil = g * sgm
            dgn_ref[0:1, 128 * h:128 * (h + 1)] += jnp.sum(dy * nrm * sil, axis=0, keepdims=True)
            dgate = dy * nrm * wv * (sgm * (1.0 + g * (1.0 - sgm)))
            dn = dy * wv * sil
            do = rs * (dn - nrm * jnp.mean(dn * nrm, axis=-1, keepdims=True))
            dp_ref[rows, O_GR + 128 * h:O_GR + 128 * (h + 1)] = (dgate * real).astype(BF16)
            dos.append(do)
            dps.append(jnp.where(causal, _dot_nt(do, v), 0.0))
        _gla_all_intra_bwd(c, dps, p_scr, dq_scr, dk_scr)
        dq_intra, dk_intra = dq_scr[...], dk_scr[...]
        for h in range(GLA_HEADS):
            sl = slice(GLA_DK * h, GLA_DK * (h + 1))
            v = p_ref[rows, O_GV + 128 * h:O_GV + 128 * (h + 1)]
            do = dos[h]
            qs_h, k_h = c["qs"][:, sl], c["k"][:, sl]
            s_in = sg_ref[j, h]
            ds_out = dsg[h]
            el_col = jnp.exp(lastcol[GLA_DK * h:GLA_DK * (h + 1), :])
            dv = _dot_tn(p_scr[h], do) + _dot(kl[:, sl], ds_out)
            dqe = _dot_nt(do, s_in)
            dkl = _dot_nt(v, ds_out)
            dsg[h] = _dot_tn(qe[:, sl], do) + el_col * ds_out
            sd = s_in * ds_out
            sd_hi = sd.astype(BF16)
            sd_lo = (sd - sd_hi.astype(F32)).astype(BF16)
            ones8 = jnp.ones((8, GLA_DV), BF16)
            nt = (((1,), (1,)), ((), ()))
            d_el = (lax.dot_general(ones8, sd_hi, nt, preferred_element_type=F32)
                    + lax.dot_general(ones8, sd_lo, nt, preferred_element_type=F32))[0:1, :]
            dqs = dqe * c["ecum"][:, sl] + dq_intra[:, sl]
            dkk = dkl * c["ekl"][:, sl] + dk_intra[:, sl]
            d_last = jnp.sum(dkl * kl[:, sl], axis=0, keepdims=True) + d_el * c["el"][:, sl]
            dcum = qs_h * dqs - k_h * dkk + jnp.where(row == CHUNK - 1, d_last, 0.0)
            dla_heads.append(_dot_exact_rhs(triu, dcum))
            dq_heads.append(dqs * (GLA_DK ** -0.5))
            dk_heads.append(dkk)
            dp_ref[rows, O_GV + 128 * h:O_GV + 128 * (h + 1)] = (dv * real).astype(BF16)

        dla = jnp.concatenate(dla_heads, axis=1)
        dp_ref[rows, O_GQ:O_GQ + 256] = (jnp.concatenate(dq_heads, axis=1) * real).astype(BF16)
        dp_ref[rows, O_GK:O_GK + 256] = (jnp.concatenate(dk_heads, axis=1) * real).astype(BF16)
        dz = dla * (1.0 / GLA_TAU) * _sigmoid(-c["z"]) * real
        ga = p_ref[rows, O_GA:O_GA + 128]
        dp_ref[rows, O_GA:O_GA + 128] = _dot_nt(dz, w2_ref[...]).astype(BF16)
        dp_ref[rows, O_GA + 128:IN_WP] = jnp.zeros((CHUNK, IN_WP - O_GA - 128), BF16)
        dw2_ref[...] += _dot_tn(ga, dz)
        dgb_ref[0:1, :] += jnp.sum(dz, axis=0, keepdims=True)

    const = lambda shape: pl.BlockSpec(shape, lambda s: (0,) * len(shape))
    rev = lambda s: (last_step - s, 0)
    anywhere = [pl.BlockSpec(memory_space=pl.ANY)] * n_carried
    return pl.pallas_call(
        body, name=name, grid=(N_STEPS,),
        in_specs=[pl.BlockSpec((STEP_ROWS, IN_WP), rev), pl.BlockSpec((STEP_ROWS, D), rev),
                  pl.BlockSpec((STEP_ROWS, D), rev),
                  pl.BlockSpec((1, RET_HEADS, RET_DK, 128), lambda s: (last_step - s, 0, 0, 0)),
                  pl.BlockSpec((CHUNKS_PER_STEP, GLA_HEADS, GLA_DK, GLA_DV), lambda s: (last_step - s, 0, 0, 0)),
                  pl.BlockSpec((STEP_ROWS, 128), rev), pl.BlockSpec((STEP_ROWS, 128), rev),
                  const((128, 256)), const((1, 256)), const((1, 512)), const((1, 512))] + anywhere,
        out_specs=[pl.BlockSpec((STEP_ROWS, IN_WP), rev), const((128, 256)), const((8, 256)),
                   const((8, 512)), const((8, 512))] + anywhere,
        out_shape=[jax.ShapeDtypeStruct((LP, IN_WP), BF16), jax.ShapeDtypeStruct((128, 256), F32),
                   jax.ShapeDtypeStruct((8, 256), F32), jax.ShapeDtypeStruct((8, 512), F32),
                   jax.ShapeDtypeStruct((8, 512), F32)] + [jax.ShapeDtypeStruct(g.shape, g.dtype) for g in carried],
        scratch_shapes=[pltpu.VMEM((RET_HEADS, RET_DK, 128), F32), pltpu.VMEM((GLA_HEADS, GLA_DK, GLA_DV), F32),
                        pltpu.VMEM((GLA_HEADS, CHUNK, CHUNK), F32), pltpu.VMEM((CHUNK, 256), F32),
                        pltpu.VMEM((CHUNK, 256), F32)] + _exchange_sems(n_carried),
        compiler_params=_cparams("arbitrary"),
    )(proj, ocat, dmrg, sr_all, sg_all, cos2, sin2, w2p, gb, rnw, gnw, *carried)


def _all_gather(xs, name):
    n = len(xs)

    def body(*refs):
        start, forward, finish = _gather_phases(refs[:n], refs[n:2 * n], *refs[2 * n:])
        start()
        forward()
        finish()

    return pl.pallas_call(
        body, name=name,
        in_specs=[pl.BlockSpec(memory_space=pl.ANY)] * n,
        out_specs=[pl.BlockSpec(memory_space=pl.ANY)] * n,
        out_shape=_gathered_shapes(xs),
        scratch_shapes=_exchange_sems(n),
    )(*xs)


def _gathered_shapes(xs):
    return [jax.ShapeDtypeStruct((N_DEV,) + x.shape, x.dtype) for x in xs]


def _exchange_sems(n):
    if n == 0:
        return []
    return [pltpu.SemaphoreType.DMA((7 * n,)), pltpu.SemaphoreType.DMA((7 * n,)), pltpu.SemaphoreType.DMA((n,))]


def _gather_phases(x_refs, out_refs, send_sems, recv_sems, local_sems):
    n = len(x_refs)
    mx, my, mc = lax.axis_index("x"), lax.axis_index("y"), lax.axis_index("c")
    me, sibling = (mx, my, mc), (mx, my, 1 - mc)
    chips = [(1 - mx, my), (mx, 1 - my), (1 - mx, 1 - my)]

    def slot(a, px, py, pc):
        return out_refs[a].at[4 * px + 2 * py + pc]

    def copy(a, k, block, to, src=None):
        return pltpu.make_async_remote_copy(
            src_ref=slot(a, *block) if src is None else src, dst_ref=slot(a, *block),
            send_sem=send_sems.at[7 * a + k], recv_sem=recv_sems.at[7 * a + k],
            device_id=to, device_id_type=MESH_IDS)

    mine = [pltpu.make_async_copy(x_refs[a], slot(a, *me), local_sems.at[a]) for a in range(n)]
    first = []
    for a in range(n):
        first.append(copy(a, 0, me, sibling, src=x_refs[a]))
        first += [copy(a, 1 + j, me, (*chip, mc), src=x_refs[a]) for j, chip in enumerate(chips)]
    passed = [copy(a, 4 + j, (*chip, mc), sibling) for j, chip in enumerate(chips) for a in range(n)]

    def start():
        for cp in mine + first:
            cp.start()

    def forward():
        for j, chip in enumerate(chips):
            for a in range(n):
                copy(a, 1 + j, (*chip, mc), me).wait_recv()
                passed[j * n + a].start()

    def finish():
        for a in range(n):
            copy(a, 0, sibling, me).wait_recv()
            for j, chip in enumerate(chips):
                copy(a, 4 + j, (*chip, 1 - mc), me).wait_recv()
        for cp in first + passed:
            cp.wait_send()
        for cp in mine:
            cp.wait()

    return start, forward, finish


def _exchange_blocks(gs, name):
    n = len(gs)

    def body(*refs):
        start, finish = _exchange_phases(refs[:n], refs[n:2 * n], *refs[2 * n:])
        start()
        finish()

    return pl.pallas_call(
        body, name=name,
        in_specs=[pl.BlockSpec(memory_space=pl.ANY)] * n,
        out_specs=[pl.BlockSpec(memory_space=pl.ANY)] * n,
        out_shape=[jax.ShapeDtypeStruct(g.shape, g.dtype) for g in gs],
        scratch_shapes=_exchange_sems(n),
    )(*gs)


def _exchange_phases(g_refs, out_refs, send_sems, recv_sems, local_sems):
    n = len(g_refs)
    mx, my, mc = lax.axis_index("x"), lax.axis_index("y"), lax.axis_index("c")
    me = 4 * mx + 2 * my + mc
    mine = [pltpu.make_async_copy(g_refs[a].at[me], out_refs[a].at[me], local_sems.at[a]) for a in range(n)]
    copies = []
    for r in range(1, N_DEV):
        px, py, pc = mx ^ (r >> 2), my ^ ((r >> 1) & 1), mc ^ (r & 1)
        peer = 4 * px + 2 * py + pc
        for a in range(n):
            copies.append(pltpu.make_async_remote_copy(
                src_ref=g_refs[a].at[peer], dst_ref=out_refs[a].at[me],
                send_sem=send_sems.at[7 * a + r - 1], recv_sem=recv_sems.at[7 * a + r - 1],
                device_id=(px, py, pc), device_id_type=MESH_IDS))

    def start():
        for cp in mine + copies:
            cp.start()

    def finish():
        for cp in copies:
            cp.wait_recv()
        for cp in copies:
            cp.wait_send()
        for cp in mine:
            cp.wait()

    return start, finish


IN_SHARD = IN_W // N_DEV
IN_SHARD_P = 512
UP_SHARD = D_UP // N_DEV
UP_SHARD_P = 768
RELAYOUT_ROWS = 256


def _pieces_w_in():
    return [(k, 0, IN_SHARD * k, IN_SHARD) for k in range(N_DEV)]


def _pieces_ffn_up():
    pieces = []
    for k in range(N_DEV):
        n, end = UP_SHARD * k, UP_SHARD * (k + 1)
        while n < end:
            half, r = divmod(n, D_FF)
            blk, off = divmod(r, CONV_BLOCK)
            run = min(CONV_BLOCK - off, end - n)
            pieces.append((k, n - UP_SHARD * k, 2 * CONV_BLOCK * blk + CONV_BLOCK * half + off, run))
            n += run
    return pieces


def _assemble_block(load, spans, dst_block, rows):
    lo = 128 * dst_block
    lane = lax.broadcasted_iota(jnp.int32, (1, 128), 1)
    out = jnp.zeros((rows, 128), F32)
    for key, src_off, dst_off, length in spans:
        a, b = max(lo, dst_off), min(lo + 128, dst_off + length)
        s, s_end = src_off + (a - dst_off), src_off + (b - dst_off)
        d = a
        while s < s_end:
            e = min(s_end, 128 * (s // 128 + 1))
            blk = load(key, s // 128)
            shift = (d - s) % 128
            if shift:
                blk = pltpu.roll(blk, shift, 1)
            out = jnp.where((lane >= d - lo) & (lane < d - lo + (e - s)), blk, out)
            d += e - s
            s = e
    return out


def _shards_to_cols(shards, pieces, width, name):
    _, rows, _ = shards.shape
    tr = RELAYOUT_ROWS

    def body(s_ref, o_ref):
        load = lambda k, b: s_ref[k, :, 128 * b:128 * (b + 1)].astype(F32)
        for db in range(width // 128):
            o_ref[:, 128 * db:128 * (db + 1)] = _assemble_block(load, pieces, db, tr).astype(BF16)

    return pl.pallas_call(
        body, name=name, grid=(rows // tr,),
        in_specs=[pl.BlockSpec((N_DEV, tr, shards.shape[2]), lambda i: (0, i, 0))],
        out_specs=pl.BlockSpec((tr, width), lambda i: (i, 0)),
        out_shape=jax.ShapeDtypeStruct((rows, width), BF16),
        compiler_params=_cparams("parallel"),
    )(shards)


def _cols_to_shards(full, pieces, shard_width, name):
    rows, width = full.shape
    tr = RELAYOUT_ROWS

    def body(f_ref, o_ref):
        load = lambda _, b: f_ref[:, 128 * b:128 * (b + 1)].astype(F32)
        for k in range(N_DEV):
            spans = [(None, dst_off, src_off, length) for dev, src_off, dst_off, length in pieces if dev == k]
            for db in range(shard_width // 128):
                o_ref[k, :, 128 * db:128 * (db + 1)] = _assemble_block(load, spans, db, tr).astype(BF16)

    return pl.pallas_call(
        body, name=name, grid=(rows // tr,),
        in_specs=[pl.BlockSpec((tr, width), lambda i: (i, 0))],
        out_specs=pl.BlockSpec((N_DEV, tr, shard_width), lambda i: (0, i, 0)),
        out_shape=jax.ShapeDtypeStruct((N_DEV, rows, shard_width), BF16),
        compiler_params=_cparams("parallel"),
    )(full)


def _adamw(parts, w, m, v, rows_per_step, name):
    rows, cols = w.shape
    assert rows % rows_per_step == 0 and parts.shape == (N_DEV, rows, cols)

    def body(p_ref, w_ref, m_ref, v_ref, g_ref, d_ref, nm_ref, nv_ref):
        g = p_ref[0].astype(F32)
        for j in range(1, N_DEV):
            g = g + p_ref[j].astype(F32)
        m_new = ADAM_B1 * m_ref[...] + (1.0 - ADAM_B1) * g
        v_new = ADAM_B2 * v_ref[...] + (1.0 - ADAM_B2) * (g * g)
        m_hat = m_new / (1.0 - ADAM_B1 ** ADAM_STEP)
        v_hat = v_new / (1.0 - ADAM_B2 ** ADAM_STEP)
        g_ref[...] = g
        d_ref[...] = -ADAM_LR * (m_hat / (jnp.sqrt(v_hat) + ADAM_EPS) + ADAM_WD * w_ref[...])
        nm_ref[...] = m_new
        nv_ref[...] = v_new

    tile = pl.BlockSpec((rows_per_step, cols), lambda i: (i, 0))
    shape = jax.ShapeDtypeStruct((rows, cols), F32)
    return pl.pallas_call(
        body, name=name, grid=(rows // rows_per_step,),
        in_specs=[pl.BlockSpec((N_DEV, rows_per_step, cols), lambda i: (0, i, 0)), tile, tile, tile],
        out_specs=[tile, tile, tile, tile],
        out_shape=[shape, shape, shape, shape],
        compiler_params=_cparams("parallel"),
    )(parts, w, m, v)


BIG = (("w_in", (DEPTH, D, IN_W // N_DEV), 2), ("w_out", (DEPTH, D // N_DEV, D), 1),
       ("ffn_up", (DEPTH, D, D_UP // N_DEV), 2), ("ffn_down", (DEPTH, D_FF // N_DEV, D), 1))
SMALL = (("meta_tokens", (N_META, D // N_DEV), 1), ("gla_gate_w2", (DEPTH, GATE_RANK, 256 // N_DEV), 2),
         ("ffn_conv_w", (DEPTH, 3, D_UP // N_DEV), 2))
REPL = (("pre_mix_norm", (DEPTH, D)), ("gla_gate_b", (DEPTH, 256)), ("ret_norm_w", (DEPTH, 512)),
        ("gla_norm_w", (DEPTH, 512)), ("post_mix_norm", (DEPTH, D)), ("pre_ffn_norm", (DEPTH, D)),
        ("ffn_conv_b", (DEPTH, D_UP)), ("post_ffn_norm", (DEPTH, D)))
WEIGHT_ORDER = ("meta_tokens", "pre_mix_norm", "w_in", "gla_gate_w2", "gla_gate_b", "ret_norm_w", "gla_norm_w",
                "w_out", "post_mix_norm", "pre_ffn_norm", "ffn_up", "ffn_conv_w", "ffn_conv_b", "ffn_down",
                "post_ffn_norm")


def _size(shape):
    return math.prod(shape)


def _round_up(n, mult):
    return -(-n // mult) * mult


REPL_ROWS = _round_up(-(-sum(_size(s) for _, s in REPL) // LANES), 8)
SMALL_ROWS = _round_up(-(-sum(_size(s) for _, s, _ in SMALL) // LANES), 8)


def _pack(arrays, rows, dtype):
    flat = jnp.concatenate([a.reshape(-1).astype(dtype) for a in arrays])
    return jnp.pad(flat, (0, rows * LANES - flat.shape[0])).reshape(rows, LANES)


def _unpack(buf, shapes):
    flat = buf.reshape(-1)
    out, off = [], 0
    for shape in shapes:
        out.append(flat[off:off + _size(shape)].reshape(shape))
        off += _size(shape)
    return out


def _unshard(blocks, axis):
    moved = jnp.moveaxis(blocks, 0, axis)
    shape = list(moved.shape)
    shape[axis:axis + 2] = [shape[axis] * shape[axis + 1]]
    return moved.reshape(shape)


def _to_blocks(full, axis):
    shape = list(full.shape)
    shape[axis:axis + 1] = [N_DEV, shape[axis] // N_DEV]
    return jnp.moveaxis(full.reshape(shape), axis, 0)


def _interleave_cols(w):
    lead = w.shape[:-1]
    return jnp.swapaxes(w.reshape(lead + (2, N_CONV_BLOCKS, CONV_BLOCK)), -3, -2).reshape(lead + (D_UP,))


def _deinterleave_cols(w):
    lead = w.shape[:-1]
    return jnp.swapaxes(w.reshape(lead + (N_CONV_BLOCKS, 2, CONV_BLOCK)), -3, -2).reshape(lead + (D_UP,))


def _rope_tables():
    half = RET_DK // 2
    inv = ROPE_BASE ** (-jnp.arange(half, dtype=F32) / half)
    pos = jnp.arange(LP, dtype=F32) - float(PAD_ROWS)
    ang = pos[:, None] * inv[None, :]
    c, s = jnp.cos(ang), jnp.sin(ang)
    return jnp.concatenate([c, c], axis=1), jnp.concatenate([-s, s], axis=1)


def kernel(x, meta_tokens, pre_mix_norm, w_in, gla_gate_w2, gla_gate_b, ret_norm_w, gla_norm_w, w_out, post_mix_norm, pre_ffn_norm, ffn_up, ffn_conv_w, ffn_conv_b, ffn_down, post_ffn_norm, loss_target, m_meta_tokens, m_pre_mix_norm, m_w_in, m_gla_gate_w2, m_gla_gate_b, m_ret_norm_w, m_gla_norm_w, m_w_out, m_post_mix_norm, m_pre_ffn_norm, m_ffn_up, m_ffn_conv_w, m_ffn_conv_b, m_ffn_down, m_post_ffn_norm, v_meta_tokens, v_pre_mix_norm, v_w_in, v_gla_gate_w2, v_gla_gate_b, v_ret_norm_w, v_gla_norm_w, v_w_out, v_post_mix_norm, v_pre_ffn_norm, v_ffn_up, v_ffn_conv_w, v_ffn_conv_b, v_ffn_down, v_post_ffn_norm):
    weights = dict(meta_tokens=meta_tokens, pre_mix_norm=pre_mix_norm, w_in=w_in, gla_gate_w2=gla_gate_w2,
                   gla_gate_b=gla_gate_b, ret_norm_w=ret_norm_w, gla_norm_w=gla_norm_w, w_out=w_out,
                   post_mix_norm=post_mix_norm, pre_ffn_norm=pre_ffn_norm, ffn_up=ffn_up, ffn_conv_w=ffn_conv_w,
                   ffn_conv_b=ffn_conv_b, ffn_down=ffn_down, post_ffn_norm=post_ffn_norm)
    mom1 = dict(meta_tokens=m_meta_tokens, pre_mix_norm=m_pre_mix_norm, w_in=m_w_in, gla_gate_w2=m_gla_gate_w2,
                gla_gate_b=m_gla_gate_b, ret_norm_w=m_ret_norm_w, gla_norm_w=m_gla_norm_w, w_out=m_w_out,
                post_mix_norm=m_post_mix_norm, pre_ffn_norm=m_pre_ffn_norm, ffn_up=m_ffn_up,
                ffn_conv_w=m_ffn_conv_w, ffn_conv_b=m_ffn_conv_b, ffn_down=m_ffn_down, post_ffn_norm=m_post_ffn_norm)
    mom2 = dict(meta_tokens=v_meta_tokens, pre_mix_norm=v_pre_mix_norm, w_in=v_w_in, gla_gate_w2=v_gla_gate_w2,
                gla_gate_b=v_gla_gate_b, ret_norm_w=v_ret_norm_w, gla_norm_w=v_gla_norm_w, w_out=v_w_out,
                post_mix_norm=v_post_mix_norm, pre_ffn_norm=v_pre_ffn_norm, ffn_up=v_ffn_up,
                ffn_conv_w=v_ffn_conv_w, ffn_conv_b=v_ffn_conv_b, ffn_down=v_ffn_down, post_ffn_norm=v_post_ffn_norm)

    pad_cols = lambda a, width: jnp.pad(a, ((0, 0), (0, width - a.shape[1])))
    big_names = [n for n, _, _ in BIG]
    shard = {}
    for l in range(DEPTH):
        shard[l, "w_in"] = pad_cols(w_in[l].astype(BF16), IN_SHARD_P)
        shard[l, "w_out"] = w_out[l].astype(BF16)
        shard[l, "ffn_up"] = pad_cols(ffn_up[l].astype(BF16), UP_SHARD_P)
        shard[l, "ffn_down"] = ffn_down[l].astype(BF16)
    gathered = {(0, "w_in"): _all_gather([shard[0, "w_in"]], "gather_w_in_0")[0]}
    gather_in_mixer = {l: [(l, n) for n in big_names[1:]] + ([(l + 1, "w_in")] if l + 1 < DEPTH else [])
                       for l in range(DEPTH)}
    small = _all_gather([_pack([weights[n] for n, _, _ in SMALL], SMALL_ROWS, F32)], "gather_small_weights")[0]
    small_parts = _unpack_blocks(small, [s for _, s, _ in SMALL])
    full = {n: _unshard(p, ax) for (n, _, ax), p in zip(SMALL, small_parts)}
    w2p = jnp.pad(full["gla_gate_w2"], ((0, 0), (0, 128 - GATE_RANK), (0, 0)))
    cw8 = jnp.concatenate([_interleave_cols(full["ffn_conv_w"]), _interleave_cols(ffn_conv_b)[:, None, :],
                           jnp.zeros((DEPTH, 4, D_UP), F32)], axis=1)
    cos2, sin2 = _rope_tables()

    h = jnp.concatenate([jnp.zeros((PAD_ROWS, D), F32), full["meta_tokens"], x[0]], axis=0)
    target = jnp.concatenate([jnp.zeros((CHUNK, D), F32), loss_target[0]], axis=0)
    saved, layer_w = [], []
    for l in range(DEPTH):
        lw = dict(w_in=_shards_to_cols(gathered[l, "w_in"], _pieces_w_in(), IN_WP, f"w_in_cols_{l}"))
        a1 = _rmsnorm_fwd(h, pre_mix_norm[l:l + 1], f"pre_mix_norm_{l}")
        proj = _matmul(a1, lw["w_in"], out_dtype=F32, tm=TM, tn=1280, tk=D, name=f"in_proj_{l}", n_outer=True)
        keys = gather_in_mixer.get(l, [])
        ocat, merged, sr_all, sg_all, *got = _mixer_fwd(proj, cos2, sin2, w2p[l], gla_gate_b[l:l + 1],
                                                        ret_norm_w[l:l + 1], gla_norm_w[l:l + 1], f"mixer_fwd_{l}",
                                                        carried=[shard[key] for key in keys])
        gathered.update(zip(keys, got))
        lw["w_out"] = gathered[l, "w_out"].reshape(D, D)
        lw["w_up"] = _shards_to_cols(gathered[l, "ffn_up"], _pieces_ffn_up(), D_UP, f"ffn_up_cols_{l}")
        lw["w_down"] = gathered[l, "ffn_down"].reshape(D_FF, D)
        layer_w.append(lw)
        m, h1 = _matmul_resid_norm(merged, lw["w_out"], h, post_mix_norm[l:l + 1], f"out_proj_{l}")
        a2 = _rmsnorm_fwd(h1, pre_ffn_norm[l:l + 1], f"pre_ffn_norm_{l}")
        u = _matmul(a2, lw["w_up"], out_dtype=BF16, tm=TM, tn=1408, tk=D, name=f"ffn_up_{l}", n_outer=True)
        cv, act = _conv_act_fwd(u, cw8[l], f"ffn_conv_act_{l}")
        f, h2 = _matmul_resid_norm(act, lw["w_down"], h1, post_ffn_norm[l:l + 1], f"ffn_down_{l}")
        saved.append(dict(h=h, a1=a1, proj=proj, ocat=ocat, merged=merged, sr=sr_all, sg=sg_all, m=m, h1=h1,
                          a2=a2, u=u, cv=cv, act=act, f=f))
        h = h2

    dh, loss_acc = _loss_head(h, target, "loss_head")
    loss = lax.psum(loss_acc[0, 0], ("x", "y", "c"))

    kinds = ("grad", "delta", "new_m", "new_v")
    grads = {n: [None] * DEPTH for n in WEIGHT_ORDER if n != "meta_tokens" and n not in big_names}
    pending, parts = [], {}
    for l in reversed(range(DEPTH)):
        s, lw = saved[l], layer_w[l]
        dact, df, g_post_ffn = _norm_bwd_matmul(dh, s["f"], post_ffn_norm[l:l + 1], lw["w_down"], BF16,
                                                f"ffn_down_dx_{l}")
        g_down = _matmul(s["act"], df, ta=True, out_dtype=BF16, tm=D_FF // 2, tn=D, tk=TK_ROWS, name=f"ffn_down_dw_{l}")
        du, dcw = _conv_act_bwd(dact, s["cv"], s["u"], cw8[l], f"ffn_conv_act_bwd_{l}")
        dh1, g_pre_ffn = _matmul_norm_bwd(du, lw["w_up"], s["h1"], pre_ffn_norm[l:l + 1], dh, 1408, f"ffn_up_dx_{l}")
        g_up = _matmul(s["a2"], du, ta=True, out_dtype=BF16, tm=D, tn=1408, tk=TK_ROWS, name=f"ffn_up_dw_{l}")
        dmerged, dm, g_post_mix = _norm_bwd_matmul(dh1, s["m"], post_mix_norm[l:l + 1], lw["w_out"], F32,
                                                   f"out_proj_dx_{l}")
        g_out = _matmul(s["merged"], dm, ta=True, out_dtype=BF16, tm=D, tn=D, tk=TK_ROWS, name=f"out_proj_dw_{l}")
        pending += [((l, "ffn_down"), g_down.reshape(N_DEV, D_FF // N_DEV, D)),
                    ((l, "ffn_up"), _cols_to_shards(g_up, _pieces_ffn_up(), UP_SHARD_P, f"ffn_up_grad_shards_{l}")),
                    ((l, "w_out"), g_out.reshape(N_DEV, D // N_DEV, D))]
        dproj, g_w2, g_gb, g_rn, g_gn, *got = _mixer_bwd(s["proj"], s["ocat"], dmerged, s["sr"], s["sg"], cos2, sin2,
                                                         w2p[l], gla_gate_b[l:l + 1], ret_norm_w[l:l + 1],
                                                         gla_norm_w[l:l + 1], f"mixer_bwd_{l}",
                                                         carried=[blocks for _, blocks in pending])
        parts.update(zip([key for key, _ in pending], got))
        dh, g_pre_mix = _matmul_norm_bwd(dproj, lw["w_in"], s["h"], pre_mix_norm[l:l + 1], dh1, IN_WP,
                                         f"in_proj_dx_{l}")
        g_in = _matmul(s["a1"], dproj, ta=True, out_dtype=BF16, tm=D, tn=1280, tk=TK_ROWS, name=f"in_proj_dw_{l}")
        pending = [((l, "w_in"), _cols_to_shards(g_in, _pieces_w_in(), IN_SHARD_P, f"w_in_grad_shards_{l}"))]
        grads["post_ffn_norm"][l] = g_post_ffn[0]
        grads["ffn_conv_w"][l] = _deinterleave_cols(dcw[0:3])
        grads["ffn_conv_b"][l] = _deinterleave_cols(dcw[3])
        grads["pre_ffn_norm"][l] = g_pre_ffn[0]
        grads["post_mix_norm"][l] = g_post_mix[0]
        grads["gla_gate_w2"][l] = g_w2[:GATE_RANK]
        grads["gla_gate_b"][l] = g_gb[0]
        grads["ret_norm_w"][l] = g_rn[0]
        grads["gla_norm_w"][l] = g_gn[0]
        grads["pre_mix_norm"][l] = g_pre_mix[0]
    local = {n: jnp.stack(v) for n, v in grads.items()}
    local["meta_tokens"] = dh[PAD_ROWS:CHUNK]
    grad_x = dh[CHUNK:][None]

    blocks = jnp.concatenate([_to_blocks(local[n], ax).reshape(N_DEV, -1) for n, _, ax in SMALL], axis=1)
    blocks = jnp.pad(blocks, ((0, 0), (0, SMALL_ROWS * LANES - blocks.shape[1]))).reshape(N_DEV, SMALL_ROWS, LANES)
    *got, small_grad_parts = _exchange_blocks([b for _, b in pending] + [blocks], "exchange_last_grads")
    parts.update(zip([key for key, _ in pending], got))

    widths = dict(w_in=IN_SHARD_P, w_out=D, ffn_up=UP_SHARD_P, ffn_down=D)
    steps = dict(w_in=256, w_out=D // N_DEV, ffn_up=256, ffn_down=D_FF // N_DEV // 2)
    big_out = {kind: {n: [None] * DEPTH for n in big_names} for kind in kinds}
    for l in range(DEPTH):
        for n in big_names:
            mine = [pad_cols(d[n][l], widths[n]) for d in (weights, mom1, mom2)]
            results = _adamw(parts[l, n], *mine, steps[n], f"adamw_{n}_{l}")
            for kind, r in zip(kinds, results):
                big_out[kind][n][l] = r[:, :weights[n].shape[2]]
    out = {kind: {n: jnp.stack(v) for n, v in big_out[kind].items()} for kind in kinds}
    shard_shapes = [s for _, s, _ in SMALL]
    packed = [_pack([d[n] for n, _, _ in SMALL], SMALL_ROWS, F32) for d in (weights, mom1, mom2)]
    results = _adamw(small_grad_parts, *packed, SMALL_ROWS, "adamw_small_sharded")
    for kind, buf in zip(kinds, results):
        out[kind].update(zip([n for n, _, _ in SMALL], _unpack(buf, shard_shapes)))

    repl_parts = _all_gather([_pack([local[n] for n, _ in REPL], REPL_ROWS, F32)], "gather_small_grads")[0]
    packed = [_pack([d[n] for n, _ in REPL], REPL_ROWS, F32) for d in (weights, mom1, mom2)]
    results = _adamw(repl_parts, *packed, REPL_ROWS, "adamw_replicated")
    repl_shapes = [s for _, s in REPL]
    for kind, buf in zip(kinds, results):
        out[kind].update(zip([n for n, _ in REPL], _unpack(buf, repl_shapes)))

    return (loss, grad_x, *[out["grad"][n] for n in WEIGHT_ORDER], *[out["delta"][n] for n in WEIGHT_ORDER],
            *[out["new_m"][n] for n in WEIGHT_ORDER], *[out["new_v"][n] for n in WEIGHT_ORDER])


def _unpack_blocks(gathered, shapes):
    flat = gathered.reshape(N_DEV, -1)
    out, off = [], 0
    for shape in shapes:
        out.append(flat[:, off:off + _size(shape)].reshape((N_DEV,) + shape))
        off += _size(shape)
    return out
```

```python
import math

import jax
import jax.numpy as jnp
from jax import lax
from jax.experimental import pallas as pl
from jax.experimental.pallas import tpu as pltpu

F32 = jnp.float32
BF16 = jnp.bfloat16

D = 1024
SEQ = 8192
DEPTH = 2
N_META = 16
CHUNK = 64
SUB = 16
N_SUB = CHUNK // SUB
PAD_ROWS = CHUNK - N_META
LP = SEQ + CHUNK
N_CHUNKS = LP // CHUNK
RET_HEADS = 4
RET_DK = 128
GLA_HEADS = 4
GLA_DK = 64
GLA_DV = 128
GLA_TAU = 16.0
GATE_RANK = 16
IN_W = 3600
IN_WP = 3840
D_FF = 2816
D_UP = 2 * D_FF
CONV_BLOCK = 256
N_CONV_BLOCKS = D_FF // CONV_BLOCK
ROPE_BASE = 10000.0
EPS = 1e-6
N_DEV = 8
LANES = 1024

O_RQ, O_RK, O_RV, O_RG = 0, 512, 1024, 1536
O_GQ, O_GK, O_GV, O_GR, O_GA = 2048, 2304, 2560, 3072, 3584

ADAM_LR = 0.001
ADAM_B1 = 0.9
ADAM_B2 = 0.999
ADAM_EPS = 1e-08
ADAM_WD = 0.01
ADAM_STEP = 10

VMEM_LIMIT = 56 * 1024 * 1024
MESH_IDS = pl.DeviceIdType.MESH


def _row_tile(rows, limit):
    best = 16
    for t in range(16, min(rows, limit) + 1, 16):
        if rows % t == 0:
            best = t
    return best


TM = _row_tile(LP, 688)
TK_ROWS = _row_tile(LP, 1376)


def _cparams(*sem):
    return pltpu.CompilerParams(dimension_semantics=sem, vmem_limit_bytes=VMEM_LIMIT)


def _dot(a, b):
    return jnp.dot(a.astype(BF16), b.astype(BF16), preferred_element_type=F32)


def _dot_nt(a, b):
    return lax.dot_general(a.astype(BF16), b.astype(BF16), (((1,), (1,)), ((), ())), preferred_element_type=F32)


def _dot_tn(a, b):
    return lax.dot_general(a.astype(BF16), b.astype(BF16), (((0,), (0,)), ((), ())), preferred_element_type=F32)


def _split3(x):
    hi = x.astype(BF16)
    r1 = x - hi.astype(F32)
    mid = r1.astype(BF16)
    lo = (r1 - mid.astype(F32)).astype(BF16)
    return hi, mid, lo


def _dot_exact_rhs(t, x):
    hi, mid, lo = _split3(x)
    t = t.astype(BF16)
    return (jnp.dot(t, hi, preferred_element_type=F32) + jnp.dot(t, mid, preferred_element_type=F32)
            + jnp.dot(t, lo, preferred_element_type=F32))


def _dot_tn_exact_lhs(x, ones):
    dims = (((0,), (0,)), ((), ()))
    hi, mid, lo = _split3(x)
    ones = ones.astype(BF16)
    return (lax.dot_general(hi, ones, dims, preferred_element_type=F32)
            + lax.dot_general(mid, ones, dims, preferred_element_type=F32)
            + lax.dot_general(lo, ones, dims, preferred_element_type=F32))


def _sigmoid(x):
    return 1.0 / (1.0 + jnp.exp(-x))


def _matmul(a, b, *, ta=False, tb=False, out_dtype, tm, tn, tk, name, n_outer=False):
    m = a.shape[1] if ta else a.shape[0]
    k = a.shape[0] if ta else a.shape[1]
    n = b.shape[0] if tb else b.shape[1]
    assert (b.shape[1] if tb else b.shape[0]) == k
    assert m % tm == 0 and n % tn == 0 and k % tk == 0, (name, m, n, k, tm, tn, tk)
    nk = k // tk
    order = (lambda f: (lambda j, i, kk: f(i, j, kk))) if n_outer else (lambda f: f)
    a_spec = (pl.BlockSpec((tk, tm), order(lambda i, j, kk: (kk, i))) if ta
              else pl.BlockSpec((tm, tk), order(lambda i, j, kk: (i, kk))))
    b_spec = (pl.BlockSpec((tn, tk), order(lambda i, j, kk: (j, kk))) if tb
              else pl.BlockSpec((tk, tn), order(lambda i, j, kk: (kk, j))))
    dims = (((0 if ta else 1,), (1 if tb else 0,)), ((), ()))

    def body(a_ref, b_ref, o_ref, *acc):
        prod = lax.dot_general(a_ref[...].astype(BF16), b_ref[...].astype(BF16), dims, preferred_element_type=F32)
        if nk == 1:
            o_ref[...] = prod.astype(out_dtype)
            return
        acc_ref, = acc
        kk = pl.program_id(2)

        @pl.when(kk == 0)
        def _():
            acc_ref[...] = prod

        @pl.when(kk > 0)
        def _():
            acc_ref[...] += prod

        @pl.when(kk == nk - 1)
        def _():
            o_ref[...] = acc_ref[...].astype(out_dtype)

    return pl.pallas_call(
        body, name=name, grid=(n // tn, m // tm, nk) if n_outer else (m // tm, n // tn, nk),
        in_specs=[a_spec, b_spec],
        out_specs=pl.BlockSpec((tm, tn), order(lambda i, j, kk: (i, j))),
        out_shape=jax.ShapeDtypeStruct((m, n), out_dtype),
        scratch_shapes=[pltpu.VMEM((tm, tn), F32)] if nk > 1 else [],
        compiler_params=_cparams("parallel", "parallel", "arbitrary"),
    )(a, b)


def _matmul_resid_norm(a, b, h, w, name):
    k = a.shape[1]

    def body(a_ref, b_ref, h_ref, w_ref, m_ref, o_ref):
        m = jnp.dot(a_ref[...].astype(BF16), b_ref[...].astype(BF16), preferred_element_type=F32)
        m_ref[...] = m
        r = lax.rsqrt(jnp.mean(m * m, axis=-1, keepdims=True) + EPS)
        row = pl.program_id(0) * TM + lax.broadcasted_iota(jnp.int32, (TM, 1), 0)
        o_ref[...] = h_ref[...] + jnp.where(row >= PAD_ROWS, m * r * w_ref[...], 0.0)

    tile = pl.BlockSpec((TM, D), lambda i: (i, 0))
    return pl.pallas_call(
        body, name=name, grid=(LP // TM,),
        in_specs=[pl.BlockSpec((TM, k), lambda i: (i, 0)), pl.BlockSpec((k, D), lambda i: (0, 0)), tile,
                  pl.BlockSpec((1, D), lambda i: (0, 0))],
        out_specs=[tile, tile],
        out_shape=[jax.ShapeDtypeStruct((LP, D), F32), jax.ShapeDtypeStruct((LP, D), F32)],
        compiler_params=_cparams("parallel"),
    )(a, b, h, w)


def _rmsnorm_bwd_rows(dy, x, w):
    r = lax.rsqrt(jnp.mean(x * x, axis=-1, keepdims=True) + EPS)
    g = dy * w
    dx = r * g - x * (r * r * r * jnp.mean(g * x, axis=-1, keepdims=True))
    return dx, jnp.sum(dy * x * r, axis=0, keepdims=True)


def _matmul_norm_bwd(dz, b, x, w, resid, tk, name, carried=()):
    k = dz.shape[1]
    assert k % tk == 0
    nk = k // tk
    n_rows = LP // TM
    n_carried = len(carried)

    def body(*refs):
        a_ref, b_ref, x_ref, w_ref, r_ref = refs[:5]
        g_refs, refs = refs[5:5 + n_carried], refs[5 + n_carried:]
        dx_ref, dw_ref = refs[:2]
        got_refs, refs = refs[2:2 + n_carried], refs[2 + n_carried:]
        acc, sems = (refs[:1], refs[1:]) if nk > 1 else ((), refs)
        i, kk = pl.program_id(0), pl.program_id(1)
        if n_carried:
            exchange_start, exchange_finish = _exchange_phases(g_refs, got_refs, *sems)
            pl.when((i == 0) & (kk == 0))(exchange_start)

        @pl.when((i == 0) & (kk == 0))
        def _():
            dw_ref[...] = jnp.zeros_like(dw_ref)

        prod = lax.dot_general(a_ref[...].astype(BF16), b_ref[...].astype(BF16), (((1,), (1,)), ((), ())),
                               preferred_element_type=F32)

        def finish(dy):
            dx, dw = _rmsnorm_bwd_rows(dy, x_ref[...], w_ref[...])
            dx_ref[...] = dx + r_ref[...]
            dw_ref[0:1, :] += dw

        if nk == 1:
            finish(prod)
        else:
            acc_ref, = acc

            @pl.when(kk == 0)
            def _():
                acc_ref[...] = prod

            @pl.when((kk > 0) & (kk < nk - 1))
            def _():
                acc_ref[...] += prod

            @pl.when(kk == nk - 1)
            def _():
                finish(acc_ref[...] + prod)

        if n_carried:
            pl.when((i == n_rows - 1) & (kk == nk - 1))(exchange_finish)

    tile = pl.BlockSpec((TM, D), lambda i, kk: (i, 0))
    anywhere = [pl.BlockSpec(memory_space=pl.ANY)] * n_carried
    return pl.pallas_call(
        body, name=name, grid=(n_rows, nk),
        in_specs=[pl.BlockSpec((TM, tk), lambda i, kk: (i, kk)), pl.BlockSpec((D, tk), lambda i, kk: (0, kk)), tile,
                  pl.BlockSpec((1, D), lambda i, kk: (0, 0)), tile] + anywhere,
        out_specs=[tile, pl.BlockSpec((8, D), lambda i, kk: (0, 0))] + anywhere,
        out_shape=[jax.ShapeDtypeStruct((LP, D), F32), jax.ShapeDtypeStruct((8, D), F32)]
        + [jax.ShapeDtypeStruct(g.shape, g.dtype) for g in carried],
        scratch_shapes=([pltpu.VMEM((TM, D), F32)] if nk > 1 else []) + _exchange_sems(n_carried),
        compiler_params=_cparams("arbitrary", "arbitrary"),
    )(dz, b, x, w, resid, *carried)


def _norm_bwd_matmul(dh, x, w, b, out_dtype, name):
    n = b.shape[0]

    def body(dh_ref, x_ref, w_ref, b_ref, o_ref, dx_ref, dw_ref):
        i = pl.program_id(0)

        @pl.when(i == 0)
        def _():
            dw_ref[...] = jnp.zeros_like(dw_ref)

        row = i * TM + lax.broadcasted_iota(jnp.int32, (TM, 1), 0)
        dy = jnp.where(row >= PAD_ROWS, dh_ref[...], 0.0)
        dx, dw = _rmsnorm_bwd_rows(dy, x_ref[...], w_ref[...])
        dxb = dx.astype(BF16)
        dx_ref[...] = dxb
        dw_ref[0:1, :] += dw
        o_ref[...] = lax.dot_general(dxb, b_ref[...].astype(BF16), (((1,), (1,)), ((), ())),
                                     preferred_element_type=F32).astype(out_dtype)

    tile = pl.BlockSpec((TM, D), lambda i: (i, 0))
    return pl.pallas_call(
        body, name=name, grid=(LP // TM,),
        in_specs=[tile, tile, pl.BlockSpec((1, D), lambda i: (0, 0)), pl.BlockSpec((n, D), lambda i: (0, 0))],
        out_specs=[pl.BlockSpec((TM, n), lambda i: (i, 0)), tile, pl.BlockSpec((8, D), lambda i: (0, 0))],
        out_shape=[jax.ShapeDtypeStruct((LP, n), out_dtype), jax.ShapeDtypeStruct((LP, D), BF16),
                   jax.ShapeDtypeStruct((8, D), F32)],
        compiler_params=_cparams("arbitrary"),
    )(dh, x, w, b)


def _rmsnorm_fwd(x, w, name):
    def body(x_ref, w_ref, o_ref):
        xv = x_ref[...]
        r = lax.rsqrt(jnp.mean(xv * xv, axis=-1, keepdims=True) + EPS)
        o_ref[...] = (xv * r * w_ref[...]).astype(BF16)

    return pl.pallas_call(
        body, name=name, grid=(LP // TM,),
        in_specs=[pl.BlockSpec((TM, D), lambda i: (i, 0)), pl.BlockSpec((1, D), lambda i: (0, 0))],
        out_specs=pl.BlockSpec((TM, D), lambda i: (i, 0)),
        out_shape=jax.ShapeDtypeStruct((LP, D), BF16),
        compiler_params=_cparams("parallel"),
    )(x, w)


def _loss_head(y, target, name):
    def body(y_ref, t_ref, dy_ref, loss_ref):
        i = pl.program_id(0)

        @pl.when(i == 0)
        def _():
            loss_ref[...] = jnp.zeros_like(loss_ref)

        row = i * TM + lax.broadcasted_iota(jnp.int32, (TM, 1), 0)
        diff = jnp.where(row >= CHUNK, y_ref[...] - t_ref[...], 0.0)
        dy_ref[...] = diff * (1.0 / D)
        loss_ref[...] += (0.5 / D) * jnp.sum(diff * diff)

    tile = pl.BlockSpec((TM, D), lambda i: (i, 0))
    return pl.pallas_call(
        body, name=name, grid=(LP // TM,),
        in_specs=[tile, tile],
        out_specs=[tile, pl.BlockSpec((8, 128), lambda i: (0, 0))],
        out_shape=[jax.ShapeDtypeStruct((LP, D), F32), jax.ShapeDtypeStruct((8, 128), F32)],
        compiler_params=_cparams("arbitrary"),
    )(y, target)


GELU_C = math.sqrt(2.0 / math.pi)
GELU_K = 0.044715
STRIP = 16


def _shift_down(x, prev8, rows):
    row = lax.broadcasted_iota(jnp.int32, (rows, 1), 0)
    p1 = pltpu.roll(prev8, 1, 0)
    p2 = pltpu.roll(prev8, 2, 0)
    x1 = jnp.where(row == 0, p1[0:1, :], pltpu.roll(x, 1, 0))
    x2 = jnp.where(row == 0, p2[0:1, :], jnp.where(row == 1, p2[1:2, :], pltpu.roll(x, 2, 0)))
    return x1, x2


def _conv_act_fwd(u, cw8, name):
    n_rows = LP // TM
    cb2 = 2 * CONV_BLOCK

    def body(u_ref, cw_ref, conv_ref, act_ref, carry_ref):
        i = pl.program_id(1)

        @pl.when(i == 0)
        def _():
            carry_ref[...] = jnp.zeros_like(carry_ref)

        x = u_ref[...].astype(F32)
        x1, x2 = _shift_down(x, carry_ref[...], TM)
        conv = cw_ref[3:4, :] + x2 * cw_ref[0:1, :] + x1 * cw_ref[1:2, :] + x * cw_ref[2:3, :]
        conv_ref[...] = conv.astype(BF16)
        a = conv[:, :CONV_BLOCK]
        g = conv[:, CONV_BLOCK:]
        t = jnp.tanh(GELU_C * (a + GELU_K * a * a * a))
        act_ref[...] = (0.5 * a * (1.0 + t) * g).astype(BF16)
        carry_ref[...] = x[TM - 8:TM, :]

    return pl.pallas_call(
        body, name=name, grid=(N_CONV_BLOCKS, n_rows),
        in_specs=[pl.BlockSpec((TM, cb2), lambda j, i: (i, j)), pl.BlockSpec((8, cb2), lambda j, i: (0, j))],
        out_specs=[pl.BlockSpec((TM, cb2), lambda j, i: (i, j)), pl.BlockSpec((TM, CONV_BLOCK), lambda j, i: (i, j))],
        out_shape=[jax.ShapeDtypeStruct((LP, D_UP), BF16), jax.ShapeDtypeStruct((LP, D_FF), BF16)],
        scratch_shapes=[pltpu.VMEM((8, cb2), F32)],
        compiler_params=_cparams("arbitrary", "arbitrary"),
    )(u, cw8)


def _conv_act_bwd(dact, conv, u, cw8, name):
    n_rows = LP // TM
    cb2 = 2 * CONV_BLOCK
    n_strips = TM // STRIP

    def body(dact_ref, conv_ref, u_ref, cw_ref, du_ref, dcw_ref, carry_ref):
        i = pl.program_id(1)

        @pl.when(i == 0)
        def _():
            dcw_ref[...] = jnp.zeros_like(dcw_ref)
            carry_ref[...] = jnp.zeros_like(carry_ref)

        w0, w1, w2 = cw_ref[0:1, :], cw_ref[1:2, :], cw_ref[2:3, :]
        row = lax.broadcasted_iota(jnp.int32, (STRIP, 1), 0)
        fold = lambda z: z[:8, :] + z[8:, :]

        def strip(k, carry):
            n1, n2, s0, s1, s2, s3 = carry
            r0 = pl.multiple_of((n_strips - 1 - k) * STRIP, STRIP)
            cv = conv_ref[pl.ds(r0, STRIP), :].astype(F32)
            a = cv[:, :CONV_BLOCK]
            g = cv[:, CONV_BLOCK:]
            t = jnp.tanh(GELU_C * (a + GELU_K * a * a * a))
            gel = 0.5 * a * (1.0 + t)
            dgel = 0.5 * (1.0 + t) + 0.5 * a * (1.0 - t * t) * (GELU_C * (1.0 + 3.0 * GELU_K * a * a))
            dav = dact_ref[pl.ds(r0, STRIP), :].astype(F32)
            dconv = jnp.concatenate([dav * g * dgel, dav * gel], axis=1)
            u1 = pltpu.roll(dconv, STRIP - 1, 0)
            u2 = pltpu.roll(dconv, STRIP - 2, 0)
            d1 = jnp.where(row >= STRIP - 1, n1, u1)
            d2 = jnp.where(row >= STRIP - 2, n2, u2)
            du_ref[pl.ds(r0, STRIP), :] = (dconv * w2 + d1 * w1 + d2 * w0).astype(BF16)
            x = u_ref[pl.ds(r0, STRIP), :].astype(F32)
            return (u1, u2, s0 + fold(d2 * x), s1 + fold(d1 * x), s2 + fold(dconv * x), s3 + fold(dconv))

        below = carry_ref[...]
        zero = jnp.zeros((8, cb2), F32)
        init = (pltpu.roll(below, STRIP - 1, 0), pltpu.roll(below, STRIP - 2, 0), zero, zero, zero, zero)
        u1, _, s0, s1, s2, s3 = lax.fori_loop(0, n_strips, strip, init)
        carry_ref[...] = pltpu.roll(u1, 1, 0)
        dcw_ref[0:1, :] += jnp.sum(s0, axis=0, keepdims=True)
        dcw_ref[1:2, :] += jnp.sum(s1, axis=0, keepdims=True)
        dcw_ref[2:3, :] += jnp.sum(s2, axis=0, keepdims=True)
        dcw_ref[3:4, :] += jnp.sum(s3, axis=0, keepdims=True)

    rev = lambda j, i: (n_rows - 1 - i, j)
    return pl.pallas_call(
        body, name=name, grid=(N_CONV_BLOCKS, n_rows),
        in_specs=[pl.BlockSpec((TM, CONV_BLOCK), rev), pl.BlockSpec((TM, cb2), rev), pl.BlockSpec((TM, cb2), rev),
                  pl.BlockSpec((8, cb2), lambda j, i: (0, j))],
        out_specs=[pl.BlockSpec((TM, cb2), rev), pl.BlockSpec((8, cb2), lambda j, i: (0, j))],
        out_shape=[jax.ShapeDtypeStruct((LP, D_UP), BF16), jax.ShapeDtypeStruct((8, D_UP), F32)],
        scratch_shapes=[pltpu.VMEM((STRIP, cb2), F32)],
        compiler_params=_cparams("arbitrary", "arbitrary"),
    )(dact, conv, u, cw8)


CHUNKS_PER_STEP = 3 if N_CHUNKS % 3 == 0 else 1
STEP_ROWS = CHUNKS_PER_STEP * CHUNK
N_STEPS = N_CHUNKS // CHUNKS_PER_STEP


def _ret_consts(h):
    rows = STEP_ROWS
    lg = math.log(1.0 - 2.0 ** (-5.0 - h))
    ri = lax.broadcasted_iota(jnp.int32, (rows, rows), 0)
    ci = lax.broadcasted_iota(jnp.int32, (rows, rows), 1)
    diff = (ri - ci).astype(F32)
    dmat = jnp.where(diff >= 0, jnp.exp(lg * jnp.maximum(diff, 0.0)), 0.0)
    rowf = lax.broadcasted_iota(jnp.int32, (rows, 1), 0).astype(F32)
    zeta = jnp.exp(lg * (rows - 1.0 - rowf))
    xi = jnp.exp(lg * (rowf + 1.0))
    return dmat, zeta, xi, math.exp(lg * rows)


def _rope(t, cosv, sinv):
    return t * cosv + pltpu.roll(t, RET_DK // 2, 1) * sinv


def _unrope(d, cosv, sinv):
    return d * cosv + pltpu.roll(d * sinv, RET_DK // 2, 1)


def _gla_common(p_ref, w2_ref, gb_ref, chunk, rows):
    row = lax.broadcasted_iota(jnp.int32, (CHUNK, 1), 0)
    real = (chunk * CHUNK + row) >= PAD_ROWS
    ga = p_ref[rows, O_GA:O_GA + 128]
    z = _dot(ga, w2_ref[...]) + gb_ref[...]
    la = (jnp.minimum(z, 0.0) - jnp.log(1.0 + jnp.exp(-jnp.abs(z)))) * (1.0 / GLA_TAU)
    la = jnp.where(real, la, 0.0)
    ri = lax.broadcasted_iota(jnp.int32, (CHUNK, CHUNK), 0)
    ci = lax.broadcasted_iota(jnp.int32, (CHUNK, CHUNK), 1)
    tril = (ri >= ci).astype(F32)
    cum = _dot_exact_rhs(tril, la)
    last = cum[CHUNK - 1:CHUNK, :]
    qs = p_ref[rows, O_GQ:O_GQ + 256] * (GLA_DK ** -0.5)
    k = p_ref[rows, O_GK:O_GK + 256]
    ecum = jnp.exp(cum)
    ekl = jnp.exp(last - cum)
    el = jnp.exp(last)
    refs = [jnp.zeros((1, 256), F32)] + [cum[a * SUB - 1:a * SUB, :] for a in range(1, N_SUB)]
    eq = [jnp.exp(cum[a * SUB:(a + 1) * SUB, :] - refs[a]) for a in range(N_SUB)]
    spread = refs[0] - cum[SUB - 1:SUB, :]
    for a in range(1, N_SUB):
        spread = jnp.maximum(spread, refs[a] - cum[(a + 1) * SUB - 1:(a + 1) * SUB, :])
    small = jnp.max(spread) <= GLA_FACTORED_MAX
    return dict(real=real, row=row, z=z, la=la, cum=cum, last=last, qs=qs, k=k, ecum=ecum, ekl=ekl, el=el,
                refs=refs, eq=eq, small=small, ri=ri, ci=ci)


GLA_FACTORED_MAX = 40.0


def _head_block_mask():
    r = lax.broadcasted_iota(jnp.int32, (CHUNK, 256), 0)
    col = lax.broadcasted_iota(jnp.int32, (CHUNK, 256), 1)
    return (r // SUB) == (col // GLA_DK)


def _gla_factored(c):
    mask = _head_block_mask()
    eks, keys, queries = [], [], []
    for a in range(N_SUB):
        ek = jnp.exp(jnp.minimum(c["refs"][a] - c["cum"], GLA_FACTORED_MAX))
        qh = c["qs"][a * SUB:(a + 1) * SUB, :] * c["eq"][a]
        eks.append(ek)
        keys.append(c["k"] * ek)
        queries.append(jnp.where(mask, jnp.concatenate([qh] * GLA_HEADS, axis=0), 0.0))
    return eks, keys, queries


def _gla_scores_factored(c, factored, p_scr):
    _, keys, queries = factored
    for a in range(N_SUB):
        out = _dot_nt(queries[a], keys[a])
        out = jnp.where(c["ci"] <= a * SUB + (c["ri"] & (SUB - 1)), out, 0.0)
        for h in range(GLA_HEADS):
            p_scr[h, a * SUB:(a + 1) * SUB, :] = out[h * SUB:(h + 1) * SUB, :]


def _gla_intra_bwd_factored(c, factored, dps, dq_scr, dk_scr):
    eks, keys, queries = factored
    mask = _head_block_mask()
    dk = jnp.zeros((CHUNK, 256), F32)
    for a in range(N_SUB):
        dpa = jnp.concatenate([dps[h][a * SUB:(a + 1) * SUB, :] for h in range(GLA_HEADS)], axis=0)
        dq = jnp.where(mask, _dot(dpa, keys[a]), 0.0)
        dq = dq[0:SUB] + dq[SUB:2 * SUB] + dq[2 * SUB:3 * SUB] + dq[3 * SUB:4 * SUB]
        dq_scr[a * SUB:(a + 1) * SUB, :] = dq * c["eq"][a]
        dk = dk + _dot_tn(dpa, queries[a]) * eks[a]
    dk_scr[...] = dk


def _gla_lag_weights(c):
    cum, row = c["cum"], c["row"]
    out = [jnp.ones((CHUNK, 256), F32)]
    for r in range(1, SUB):
        out.append(jnp.where((row % SUB) >= r, jnp.exp(jnp.minimum(cum - pltpu.roll(cum, r, 0), 0.0)), 0.0))
    return out


def _gla_pairwise_keys(c):
    return [None] + [c["k"] * jnp.exp(jnp.minimum(c["refs"][a] - c["cum"], 0.0)) for a in range(1, N_SUB)]


def _gla_scores_pairwise(c, lag_w, keys, h):
    sl = slice(GLA_DK * h, GLA_DK * (h + 1))
    qs, k = c["qs"][:, sl], c["k"][:, sl]
    ri, ci = c["ri"], c["ci"]
    p = jnp.zeros((CHUNK, CHUNK), F32)
    for r in range(SUB):
        kr = k if r == 0 else pltpu.roll(k, r, 0)
        pr = jnp.sum(qs * kr * lag_w[r][:, sl], axis=1, keepdims=True)
        p = p + jnp.where(ci == ri - r, pr, 0.0)
    blocks = [jnp.zeros((SUB, CHUNK), F32)]
    for a in range(1, N_SUB):
        qh = qs[a * SUB:(a + 1) * SUB, :] * c["eq"][a][:, sl]
        blocks.append(jnp.where(ci[:SUB, :] < a * SUB, _dot_nt(qh, keys[a][:, sl]), 0.0))
    return p + jnp.concatenate(blocks, axis=0)


def _gla_all_scores(c, p_scr, factored):
    if factored:
        _gla_scores_factored(c, _gla_factored(c), p_scr)
    else:
        lag_w, keys = _gla_lag_weights(c), _gla_pairwise_keys(c)
        for h in range(GLA_HEADS):
            p_scr[h] = _gla_scores_pairwise(c, lag_w, keys, h)


def _either_form(chunks, run):
    small = chunks[0]["small"]
    for c in chunks[1:]:
        small = jnp.logical_and(small, c["small"])
    pl.when(small)(lambda: run(True))
    pl.when(jnp.logical_not(small))(lambda: run(False))


def _gla_intra_bwd_pairwise(c, lag_w, keys, dp, h):
    sl = slice(GLA_DK * h, GLA_DK * (h + 1))
    qs_h, k_h = c["qs"][:, sl], c["k"][:, sl]
    ri, ci = c["ri"], c["ci"]
    dq_rows = [jnp.zeros((SUB, GLA_DK), F32)]
    dk = jnp.zeros((CHUNK, GLA_DK), F32)
    for a in range(1, N_SUB):
        eq = c["eq"][a][:, sl]
        qh = qs_h[a * SUB:(a + 1) * SUB, :] * eq
        dpa = jnp.where(ci[:SUB, :] < a * SUB, dp[a * SUB:(a + 1) * SUB, :], 0.0)
        dq_rows.append(_dot(dpa, keys[a][:, sl]) * eq)
        ek = jnp.exp(jnp.minimum(c["refs"][a][:, sl] - c["cum"][:, sl], 0.0))
        dk = dk + _dot_tn(dpa, qh) * ek
    dq = jnp.concatenate(dq_rows, axis=0)
    for r in range(SUB):
        w = lag_w[r][:, sl]
        dpr = jnp.sum(jnp.where(ci == ri - r, dp, 0.0), axis=1, keepdims=True)
        kr = k_h if r == 0 else pltpu.roll(k_h, r, 0)
        dq = dq + dpr * kr * w
        back = dpr * qs_h * w
        dk = dk + (back if r == 0 else pltpu.roll(back, CHUNK - r, 0))
    return dq, dk


def _gla_all_intra_bwd(c, dps, p_scr, dq_scr, dk_scr, factored):
    if factored:
        terms = _gla_factored(c)
        _gla_scores_factored(c, terms, p_scr)
        _gla_intra_bwd_factored(c, terms, dps, dq_scr, dk_scr)
    else:
        lag_w, keys = _gla_lag_weights(c), _gla_pairwise_keys(c)
        outs = [_gla_intra_bwd_pairwise(c, lag_w, keys, dps[h], h) for h in range(GLA_HEADS)]
        for h in range(GLA_HEADS):
            p_scr[h] = _gla_scores_pairwise(c, lag_w, keys, h)
        dq_scr[...] = jnp.concatenate([o[0] for o in outs], axis=1)
        dk_scr[...] = jnp.concatenate([o[1] for o in outs], axis=1)


def _mixer_fwd(proj, cos2, sin2, w2p, gb, rnw, gnw, name, carried=()):
    n_carried = len(carried)

    def body(*refs):
        p_ref, c_ref, s_ref, w2_ref, gb_ref, rnw_ref, gnw_ref = refs[:7]
        x_refs, refs = refs[7:7 + n_carried], refs[7 + n_carried:]
        ocat_ref, mrg_ref, sr_out, sg_out = refs[:4]
        gathered_refs, refs = refs[4:4 + n_carried], refs[4 + n_carried:]
        sr, sg, p_scr = refs[:3]
        n = pl.program_id(0)
        if n_carried:
            start, forward, finish = _gather_phases(x_refs, gathered_refs, *refs[3:])
            pl.when(n == 0)(start)
            pl.when(n == N_STEPS // 2)(forward)

        @pl.when(n == 0)
        def _():
            sr[...] = jnp.zeros_like(sr)
            sg[...] = jnp.zeros_like(sg)

        sr_out[0] = sr[...]
        cosv, sinv = c_ref[...], s_ref[...]

        for h in range(RET_HEADS):
            dmat, zeta, xi, gc = _ret_consts(h)
            hs = slice(128 * h, 128 * (h + 1))
            q = _rope(p_ref[:, O_RQ + 128 * h:O_RQ + 128 * (h + 1)], cosv, sinv)
            k = _rope(p_ref[:, O_RK + 128 * h:O_RK + 128 * (h + 1)], cosv, sinv) * (RET_DK ** -0.5)
            v = p_ref[:, O_RV + 128 * h:O_RV + 128 * (h + 1)]
            g = p_ref[:, O_RG + 128 * h:O_RG + 128 * (h + 1)]
            s_in = sr[h]
            a = _dot_nt(q, k) * dmat
            o = _dot(a, v) + _dot(q, s_in) * xi
            sr[h] = gc * s_in + _dot_tn(k * zeta, v)
            mu = jnp.mean(o, axis=-1, keepdims=True)
            xc = o - mu
            nrm = xc * lax.rsqrt(jnp.mean(xc * xc, axis=-1, keepdims=True) + EPS)
            ocat_ref[:, hs] = o
            mrg_ref[:, hs] = (nrm * rnw_ref[:, hs] * (g * _sigmoid(g))).astype(BF16)

        row_slices = [slice(CHUNK * j, CHUNK * (j + 1)) for j in range(CHUNKS_PER_STEP)]
        chunks = [_gla_common(p_ref, w2_ref, gb_ref, n * CHUNKS_PER_STEP + j, rows)
                  for j, rows in enumerate(row_slices)]

        def gla_chunks(factored):
            for j, (rows, c) in enumerate(zip(row_slices, chunks)):
                sg_out[j] = sg[...]
                _gla_all_scores(c, p_scr.at[j], factored)
                lastcol = _dot_tn_exact_lhs(c["la"], jnp.ones((CHUNK, GLA_DV), F32))
                qe = c["qs"] * c["ecum"]
                kl = c["k"] * c["ekl"]
                for h in range(GLA_HEADS):
                    sl = slice(GLA_DK * h, GLA_DK * (h + 1))
                    hs = slice(512 + 128 * h, 512 + 128 * (h + 1))
                    v = p_ref[rows, O_GV + 128 * h:O_GV + 128 * (h + 1)]
                    g = p_ref[rows, O_GR + 128 * h:O_GR + 128 * (h + 1)]
                    s_in = sg[h]
                    o = _dot(p_scr[j, h], v) + _dot(qe[:, sl], s_in)
                    sg[h] = jnp.exp(lastcol[GLA_DK * h:GLA_DK * (h + 1), :]) * s_in + _dot_tn(kl[:, sl], v)
                    nrm = o * lax.rsqrt(jnp.mean(o * o, axis=-1, keepdims=True) + EPS)
                    ocat_ref[rows, hs] = o
                    mrg_ref[rows, hs] = (nrm * gnw_ref[:, 128 * h:128 * (h + 1)] * (g * _sigmoid(g))).astype(BF16)

        _either_form(chunks, gla_chunks)

        if n_carried:
            pl.when(n == N_STEPS - 1)(finish)

    const = lambda shape: pl.BlockSpec(shape, lambda n: (0,) * len(shape))
    anywhere = [pl.BlockSpec(memory_space=pl.ANY)] * n_carried
    return pl.pallas_call(
        body, name=name, grid=(N_STEPS,),
        in_specs=[pl.BlockSpec((STEP_ROWS, IN_WP), lambda n: (n, 0)),
                  pl.BlockSpec((STEP_ROWS, 128), lambda n: (n, 0)), pl.BlockSpec((STEP_ROWS, 128), lambda n: (n, 0)),
                  const((128, 256)), const((1, 256)), const((1, 512)), const((1, 512))] + anywhere,
        out_specs=[pl.BlockSpec((STEP_ROWS, D), lambda n: (n, 0)), pl.BlockSpec((STEP_ROWS, D), lambda n: (n, 0)),
                   pl.BlockSpec((1, RET_HEADS, RET_DK, 128), lambda n: (n, 0, 0, 0)),
                   pl.BlockSpec((CHUNKS_PER_STEP, GLA_HEADS, GLA_DK, GLA_DV), lambda n: (n, 0, 0, 0))] + anywhere,
        out_shape=[jax.ShapeDtypeStruct((LP, D), F32), jax.ShapeDtypeStruct((LP, D), BF16),
                   jax.ShapeDtypeStruct((N_STEPS, RET_HEADS, RET_DK, 128), F32),
                   jax.ShapeDtypeStruct((N_CHUNKS, GLA_HEADS, GLA_DK, GLA_DV), F32)] + _gathered_shapes(carried),
        scratch_shapes=[pltpu.VMEM((RET_HEADS, RET_DK, 128), F32), pltpu.VMEM((GLA_HEADS, GLA_DK, GLA_DV), F32),
                        pltpu.VMEM((CHUNKS_PER_STEP, GLA_HEADS, CHUNK, CHUNK), F32)] + _exchange_sems(n_carried),
        compiler_params=_cparams("arbitrary"),
    )(proj, cos2, sin2, w2p, gb, rnw, gnw, *carried)


def _mixer_bwd(proj, ocat, dmrg, sr_all, sg_all, cos2, sin2, w2p, gb, rnw, gnw, name, carried=()):
    last_step = N_STEPS - 1
    n_carried = len(carried)

    def body(*refs):
        p_ref, ocat_ref, dm_ref, sr_ref, sg_ref, c_ref, s_ref, w2_ref, gb_ref, rnw_ref, gnw_ref = refs[:11]
        g_refs, refs = refs[11:11 + n_carried], refs[11 + n_carried:]
        dp_ref, dw2_ref, dgb_ref, drn_ref, dgn_ref = refs[:5]
        got_refs, refs = refs[5:5 + n_carried], refs[5 + n_carried:]
        dsr, dsg, p_scr, dq_scr, dk_scr = refs[:5]
        step = pl.program_id(0)
        n = last_step - step
        if n_carried:
            start, finish = _exchange_phases(g_refs, got_refs, *refs[5:])
            pl.when(step == 0)(start)

        @pl.when(step == 0)
        def _():
            dsr[...] = jnp.zeros_like(dsr)
            dsg[...] = jnp.zeros_like(dsg)
            dw2_ref[...] = jnp.zeros_like(dw2_ref)
            dgb_ref[...] = jnp.zeros_like(dgb_ref)
            drn_ref[...] = jnp.zeros_like(drn_ref)
            dgn_ref[...] = jnp.zeros_like(dgn_ref)

        cosv, sinv = c_ref[...], s_ref[...]
        step_row = lax.broadcasted_iota(jnp.int32, (STEP_ROWS, 1), 0)
        real = ((n * STEP_ROWS + step_row) >= PAD_ROWS).astype(F32)

        for h in range(RET_HEADS):
            dmat, zeta, xi, gc = _ret_consts(h)
            hs = slice(128 * h, 128 * (h + 1))
            q = _rope(p_ref[:, O_RQ + 128 * h:O_RQ + 128 * (h + 1)], cosv, sinv)
            k = _rope(p_ref[:, O_RK + 128 * h:O_RK + 128 * (h + 1)], cosv, sinv) * (RET_DK ** -0.5)
            v = p_ref[:, O_RV + 128 * h:O_RV + 128 * (h + 1)]
            g = p_ref[:, O_RG + 128 * h:O_RG + 128 * (h + 1)]
            o = ocat_ref[:, hs]
            dy = dm_ref[:, hs]
            wv = rnw_ref[:, hs]
            mu = jnp.mean(o, axis=-1, keepdims=True)
            xc = o - mu
            rs = lax.rsqrt(jnp.mean(xc * xc, axis=-1, keepdims=True) + EPS)
            nrm = xc * rs
            sgm = _sigmoid(g)
            sil = g * sgm
            drn_ref[0:1, hs] += jnp.sum(dy * nrm * sil, axis=0, keepdims=True)
            dgate = dy * nrm * wv * (sgm * (1.0 + g * (1.0 - sgm)))
            dn = dy * wv * sil
            do = rs * (dn - jnp.mean(dn, axis=-1, keepdims=True) - nrm * jnp.mean(dn * nrm, axis=-1, keepdims=True))
            s_in = sr_ref[0, h]
            ds_out = dsr[h]
            a = _dot_nt(q, k) * dmat
            da = _dot_nt(do, v) * dmat
            dox = do * xi
            dq = _dot(da, k) + _dot_nt(dox, s_in)
            dk = _dot_tn(da, q) + _dot_nt(v, ds_out) * zeta
            dv = _dot_tn(a, do) + _dot(k * zeta, ds_out)
            dsr[h] = gc * ds_out + _dot_tn(q, dox)
            dk = dk * (RET_DK ** -0.5)
            dp_ref[:, O_RQ + 128 * h:O_RQ + 128 * (h + 1)] = (_unrope(dq, cosv, sinv) * real).astype(BF16)
            dp_ref[:, O_RK + 128 * h:O_RK + 128 * (h + 1)] = (_unrope(dk, cosv, sinv) * real).astype(BF16)
            dp_ref[:, O_RV + 128 * h:O_RV + 128 * (h + 1)] = (dv * real).astype(BF16)
            dp_ref[:, O_RG + 128 * h:O_RG + 128 * (h + 1)] = (dgate * real).astype(BF16)

        row_slices = [slice(CHUNK * j, CHUNK * (j + 1)) for j in range(CHUNKS_PER_STEP)]
        chunks = [_gla_common(p_ref, w2_ref, gb_ref, n * CHUNKS_PER_STEP + j, rows)
                  for j, rows in enumerate(row_slices)]

        def gla_chunks(factored):
            for j in reversed(range(CHUNKS_PER_STEP)):
                gla_chunk_bwd(chunks[j], n * CHUNKS_PER_STEP + j, row_slices[j], j, factored, p_ref, ocat_ref, dm_ref,
                              sg_ref, w2_ref, gnw_ref, dp_ref, dw2_ref, dgb_ref, dgn_ref, dsg, p_scr, dq_scr, dk_scr)

        _either_form(chunks, gla_chunks)
        if n_carried:
            pl.when(step == last_step)(finish)

    def gla_chunk_bwd(c, chunk, rows, j, factored, p_ref, ocat_ref, dm_ref, sg_ref, w2_ref, gnw_ref,
                      dp_ref, dw2_ref, dgb_ref, dgn_ref, dsg, p_scr, dq_scr, dk_scr):
        row = lax.broadcasted_iota(jnp.int32, (CHUNK, 1), 0)
        real = ((chunk * CHUNK + row) >= PAD_ROWS).astype(F32)
        ri, ci = c["ri"], c["ci"]
        causal = ri >= ci
        triu = (ci >= ri).astype(F32)
        qe = c["qs"] * c["ecum"]
        kl = c["k"] * c["ekl"]
        lastcol = _dot_tn_exact_lhs(c["la"], jnp.ones((CHUNK, GLA_DV), F32))
        dla_heads, dq_heads, dk_heads = [], [], []
        dos, dps = [], []
        for h in range(GLA_HEADS):
            hs = slice(512 + 128 * h, 512 + 128 * (h + 1))
            v = p_ref[rows, O_GV + 128 * h:O_GV + 128 * (h + 1)]
            g = p_ref[rows, O_GR + 128 * h:O_GR + 128 * (h + 1)]
            o = ocat_ref[rows, hs]
            dy = dm_ref[rows, hs]
            wv = gnw_ref[:, 128 * h:128 * (h + 1)]
            rs = lax.rsqrt(jnp.mean(o * o, axis=-1, keepdims=True) + EPS)
            nrm = o * rs
            sgm = _sigmoid(g)
            sil = g * sgm
            dgn_ref[0:1, 128 * h:128 * (h + 1)] += jnp.sum(dy * nrm * sil, axis=0, keepdims=True)
            dgate = dy * nrm * wv * (sgm * (1.0 + g * (1.0 - sgm)))
            dn = dy * wv * sil
            do = rs * (dn - nrm * jnp.mean(dn * nrm, axis=-1, keepdims=True))
            dp_ref[rows, O_GR + 128 * h:O_GR + 128 * (h + 1)] = (dgate * real).astype(BF16)
            dos.append(do)
            dps.append(jnp.where(causal, _dot_nt(do, v), 0.0))
        _gla_all_intra_bwd(c, dps, p_scr.at[j], dq_scr.at[j], dk_scr.at[j], factored)
        dq_intra, dk_intra = dq_scr[j], dk_scr[j]
        for h in range(GLA_HEADS):
            sl = slice(GLA_DK * h, GLA_DK * (h + 1))
            v = p_ref[rows, O_GV + 128 * h:O_GV + 128 * (h + 1)]
            do = dos[h]
            qs_h, k_h = c["qs"][:, sl], c["k"][:, sl]
            s_in = sg_ref[j, h]
            ds_out = dsg[h]
            el_col = jnp.exp(lastcol[GLA_DK * h:GLA_DK * (h + 1), :])
            dv = _dot_tn(p_scr[j, h], do) + _dot(kl[:, sl], ds_out)
            dqe = _dot_nt(do, s_in)
            dkl = _dot_nt(v, ds_out)
            dsg[h] = _dot_tn(qe[:, sl], do) + el_col * ds_out
            sd = s_in * ds_out
            sd_hi = sd.astype(BF16)
            sd_lo = (sd - sd_hi.astype(F32)).astype(BF16)
            ones8 = jnp.ones((8, GLA_DV), BF16)
            nt = (((1,), (1,)), ((), ()))
            d_el = (lax.dot_general(ones8, sd_hi, nt, preferred_element_type=F32)
                    + lax.dot_general(ones8, sd_lo, nt, preferred_element_type=F32))[0:1, :]
            dqs = dqe * c["ecum"][:, sl] + dq_intra[:, sl]
            dkk = dkl * c["ekl"][:, sl] + dk_intra[:, sl]
            d_last = jnp.sum(dkl * kl[:, sl], axis=0, keepdims=True) + d_el * c["el"][:, sl]
            dcum = qs_h * dqs - k_h * dkk + jnp.where(row == CHUNK - 1, d_last, 0.0)
            dla_heads.append(_dot_exact_rhs(triu, dcum))
            dq_heads.append(dqs * (GLA_DK ** -0.5))
            dk_heads.append(dkk)
            dp_ref[rows, O_GV + 128 * h:O_GV + 128 * (h + 1)] = (dv * real).astype(BF16)

        dla = jnp.concatenate(dla_heads, axis=1)
        dp_ref[rows, O_GQ:O_GQ + 256] = (jnp.concatenate(dq_heads, axis=1) * real).astype(BF16)
        dp_ref[rows, O_GK:O_GK + 256] = (jnp.concatenate(dk_heads, axis=1) * real).astype(BF16)
        dz = dla * (1.0 / GLA_TAU) * _sigmoid(-c["z"]) * real
        ga = p_ref[rows, O_GA:O_GA + 128]
        dp_ref[rows, O_GA:O_GA + 128] = _dot_nt(dz, w2_ref[...]).astype(BF16)
        dp_ref[rows, O_GA + 128:IN_WP] = jnp.zeros((CHUNK, IN_WP - O_GA - 128), BF16)
        dw2_ref[...] += _dot_tn(ga, dz)
        dgb_ref[0:1, :] += jnp.sum(dz, axis=0, keepdims=True)

    const = lambda shape: pl.BlockSpec(shape, lambda s: (0,) * len(shape))
    rev = lambda s: (last_step - s, 0)
    anywhere = [pl.BlockSpec(memory_space=pl.ANY)] * n_carried
    return pl.pallas_call(
        body, name=name, grid=(N_STEPS,),
        in_specs=[pl.BlockSpec((STEP_ROWS, IN_WP), rev), pl.BlockSpec((STEP_ROWS, D), rev),
                  pl.BlockSpec((STEP_ROWS, D), rev),
                  pl.BlockSpec((1, RET_HEADS, RET_DK, 128), lambda s: (last_step - s, 0, 0, 0)),
                  pl.BlockSpec((CHUNKS_PER_STEP, GLA_HEADS, GLA_DK, GLA_DV), lambda s: (last_step - s, 0, 0, 0)),
                  pl.BlockSpec((STEP_ROWS, 128), rev), pl.BlockSpec((STEP_ROWS, 128), rev),
                  const((128, 256)), const((1, 256)), const((1, 512)), const((1, 512))] + anywhere,
        out_specs=[pl.BlockSpec((STEP_ROWS, IN_WP), rev), const((128, 256)), const((8, 256)),
                   const((8, 512)), const((8, 512))] + anywhere,
        out_shape=[jax.ShapeDtypeStruct((LP, IN_WP), BF16), jax.ShapeDtypeStruct((128, 256), F32),
                   jax.ShapeDtypeStruct((8, 256), F32), jax.ShapeDtypeStruct((8, 512), F32),
                   jax.ShapeDtypeStruct((8, 512), F32)] + [jax.ShapeDtypeStruct(g.shape, g.dtype) for g in carried],
        scratch_shapes=[pltpu.VMEM((RET_HEADS, RET_DK, 128), F32), pltpu.VMEM((GLA_HEADS, GLA_DK, GLA_DV), F32),
                        pltpu.VMEM((CHUNKS_PER_STEP, GLA_HEADS, CHUNK, CHUNK), F32),
                        pltpu.VMEM((CHUNKS_PER_STEP, CHUNK, 256), F32),
                        pltpu.VMEM((CHUNKS_PER_STEP, CHUNK, 256), F32)] + _exchange_sems(n_carried),
        compiler_params=_cparams("arbitrary"),
    )(proj, ocat, dmrg, sr_all, sg_all, cos2, sin2, w2p, gb, rnw, gnw, *carried)


def _all_gather(xs, name):
    n = len(xs)

    def body(*refs):
        start, forward, finish = _gather_phases(refs[:n], refs[n:2 * n], *refs[2 * n:])
        start()
        forward()
        finish()

    return pl.pallas_call(
        body, name=name,
        in_specs=[pl.BlockSpec(memory_space=pl.ANY)] * n,
        out_specs=[pl.BlockSpec(memory_space=pl.ANY)] * n,
        out_shape=_gathered_shapes(xs),
        scratch_shapes=_exchange_sems(n),
    )(*xs)


def _gathered_shapes(xs):
    return [jax.ShapeDtypeStruct((N_DEV,) + x.shape, x.dtype) for x in xs]


def _exchange_sems(n):
    if n == 0:
        return []
    return [pltpu.SemaphoreType.DMA((7 * n,)), pltpu.SemaphoreType.DMA((7 * n,)), pltpu.SemaphoreType.DMA((n,))]


def _gather_phases(x_refs, out_refs, send_sems, recv_sems, local_sems):
    n = len(x_refs)
    mx, my, mc = lax.axis_index("x"), lax.axis_index("y"), lax.axis_index("c")
    me, sibling = (mx, my, mc), (mx, my, 1 - mc)
    chips = [(1 - mx, my), (mx, 1 - my), (1 - mx, 1 - my)]

    def slot(a, px, py, pc):
        return out_refs[a].at[4 * px + 2 * py + pc]

    def copy(a, k, block, to, src=None):
        return pltpu.make_async_remote_copy(
            src_ref=slot(a, *block) if src is None else src, dst_ref=slot(a, *block),
            send_sem=send_sems.at[7 * a + k], recv_sem=recv_sems.at[7 * a + k],
            device_id=to, device_id_type=MESH_IDS)

    mine = [pltpu.make_async_copy(x_refs[a], slot(a, *me), local_sems.at[a]) for a in range(n)]
    first = []
    for a in range(n):
        first.append(copy(a, 0, me, sibling, src=x_refs[a]))
        first += [copy(a, 1 + j, me, (*chip, mc), src=x_refs[a]) for j, chip in enumerate(chips)]
    passed = [copy(a, 4 + j, (*chip, mc), sibling) for j, chip in enumerate(chips) for a in range(n)]

    def start():
        for cp in mine + first:
            cp.start()

    def forward():
        for j, chip in enumerate(chips):
            for a in range(n):
                copy(a, 1 + j, (*chip, mc), me).wait_recv()
                passed[j * n + a].start()

    def finish():
        for a in range(n):
            copy(a, 0, sibling, me).wait_recv()
            for j, chip in enumerate(chips):
                copy(a, 4 + j, (*chip, 1 - mc), me).wait_recv()
        for cp in first + passed:
            cp.wait_send()
        for cp in mine:
            cp.wait()

    return start, forward, finish


def _exchange_blocks(gs, name):
    n = len(gs)

    def body(*refs):
        start, finish = _exchange_phases(refs[:n], refs[n:2 * n], *refs[2 * n:])
        start()
        finish()

    return pl.pallas_call(
        body, name=name,
        in_specs=[pl.BlockSpec(memory_space=pl.ANY)] * n,
        out_specs=[pl.BlockSpec(memory_space=pl.ANY)] * n,
        out_shape=[jax.ShapeDtypeStruct(g.shape, g.dtype) for g in gs],
        scratch_shapes=_exchange_sems(n),
    )(*gs)


def _exchange_phases(g_refs, out_refs, send_sems, recv_sems, local_sems):
    n = len(g_refs)
    mx, my, mc = lax.axis_index("x"), lax.axis_index("y"), lax.axis_index("c")
    me = 4 * mx + 2 * my + mc
    mine = [pltpu.make_async_copy(g_refs[a].at[me], out_refs[a].at[me], local_sems.at[a]) for a in range(n)]
    copies = []
    for r in range(1, N_DEV):
        px, py, pc = mx ^ (r >> 2), my ^ ((r >> 1) & 1), mc ^ (r & 1)
        peer = 4 * px + 2 * py + pc
        for a in range(n):
            copies.append(pltpu.make_async_remote_copy(
                src_ref=g_refs[a].at[peer], dst_ref=out_refs[a].at[me],
                send_sem=send_sems.at[7 * a + r - 1], recv_sem=recv_sems.at[7 * a + r - 1],
                device_id=(px, py, pc), device_id_type=MESH_IDS))

    def start():
        for cp in mine + copies:
            cp.start()

    def finish():
        for cp in copies:
            cp.wait_recv()
        for cp in copies:
            cp.wait_send()
        for cp in mine:
            cp.wait()

    return start, finish


IN_SHARD = IN_W // N_DEV
IN_SHARD_P = 512
UP_SHARD = D_UP // N_DEV
UP_SHARD_P = 768
RELAYOUT_ROWS = 256


def _pieces_w_in():
    return [(k, 0, IN_SHARD * k, IN_SHARD) for k in range(N_DEV)]


def _pieces_ffn_up():
    pieces = []
    for k in range(N_DEV):
        n, end = UP_SHARD * k, UP_SHARD * (k + 1)
        while n < end:
            half, r = divmod(n, D_FF)
            blk, off = divmod(r, CONV_BLOCK)
            run = min(CONV_BLOCK - off, end - n)
            pieces.append((k, n - UP_SHARD * k, 2 * CONV_BLOCK * blk + CONV_BLOCK * half + off, run))
            n += run
    return pieces


def _assemble_block(load, spans, dst_block, rows):
    lo = 128 * dst_block
    lane = lax.broadcasted_iota(jnp.int32, (1, 128), 1)
    out = jnp.zeros((rows, 128), F32)
    for key, src_off, dst_off, length in spans:
        a, b = max(lo, dst_off), min(lo + 128, dst_off + length)
        s, s_end = src_off + (a - dst_off), src_off + (b - dst_off)
        d = a
        while s < s_end:
            e = min(s_end, 128 * (s // 128 + 1))
            blk = load(key, s // 128)
            shift = (d - s) % 128
            if shift:
                blk = pltpu.roll(blk, shift, 1)
            out = jnp.where((lane >= d - lo) & (lane < d - lo + (e - s)), blk, out)
            d += e - s
            s = e
    return out


def _shards_to_cols(shards, pieces, width, name):
    _, rows, _ = shards.shape
    tr = RELAYOUT_ROWS

    def body(s_ref, o_ref):
        load = lambda k, b: s_ref[k, :, 128 * b:128 * (b + 1)].astype(F32)
        for db in range(width // 128):
            o_ref[:, 128 * db:128 * (db + 1)] = _assemble_block(load, pieces, db, tr).astype(BF16)

    return pl.pallas_call(
        body, name=name, grid=(rows // tr,),
        in_specs=[pl.BlockSpec((N_DEV, tr, shards.shape[2]), lambda i: (0, i, 0))],
        out_specs=pl.BlockSpec((tr, width), lambda i: (i, 0)),
        out_shape=jax.ShapeDtypeStruct((rows, width), BF16),
        compiler_params=_cparams("parallel"),
    )(shards)


def _cols_to_shards(full, pieces, shard_width, name):
    rows, width = full.shape
    tr = RELAYOUT_ROWS

    def body(f_ref, o_ref):
        load = lambda _, b: f_ref[:, 128 * b:128 * (b + 1)].astype(F32)
        for k in range(N_DEV):
            spans = [(None, dst_off, src_off, length) for dev, src_off, dst_off, length in pieces if dev == k]
            for db in range(shard_width // 128):
                o_ref[k, :, 128 * db:128 * (db + 1)] = _assemble_block(load, spans, db, tr).astype(BF16)

    return pl.pallas_call(
        body, name=name, grid=(rows // tr,),
        in_specs=[pl.BlockSpec((tr, width), lambda i: (i, 0))],
        out_specs=pl.BlockSpec((N_DEV, tr, shard_width), lambda i: (0, i, 0)),
        out_shape=jax.ShapeDtypeStruct((N_DEV, rows, shard_width), BF16),
        compiler_params=_cparams("parallel"),
    )(full)


def _adamw(parts, w, m, v, rows_per_step, name):
    rows, cols = w.shape
    assert rows % rows_per_step == 0 and parts.shape == (N_DEV, rows, cols)

    def body(p_ref, w_ref, m_ref, v_ref, g_ref, d_ref, nm_ref, nv_ref):
        g = p_ref[0].astype(F32)
        for j in range(1, N_DEV):
            g = g + p_ref[j].astype(F32)
        m_new = ADAM_B1 * m_ref[...] + (1.0 - ADAM_B1) * g
        v_new = ADAM_B2 * v_ref[...] + (1.0 - ADAM_B2) * (g * g)
        m_hat = m_new / (1.0 - ADAM_B1 ** ADAM_STEP)
        v_hat = v_new / (1.0 - ADAM_B2 ** ADAM_STEP)
        g_ref[...] = g
        d_ref[...] = -ADAM_LR * (m_hat / (jnp.sqrt(v_hat) + ADAM_EPS) + ADAM_WD * w_ref[...])
        nm_ref[...] = m_new
        nv_ref[...] = v_new

    tile = pl.BlockSpec((rows_per_step, cols), lambda i: (i, 0))
    shape = jax.ShapeDtypeStruct((rows, cols), F32)
    return pl.pallas_call(
        body, name=name, grid=(rows // rows_per_step,),
        in_specs=[pl.BlockSpec((N_DEV, rows_per_step, cols), lambda i: (0, i, 0)), tile, tile, tile],
        out_specs=[tile, tile, tile, tile],
        out_shape=[shape, shape, shape, shape],
        compiler_params=_cparams("parallel"),
    )(parts, w, m, v)


BIG = (("w_in", (DEPTH, D, IN_W // N_DEV), 2), ("w_out", (DEPTH, D // N_DEV, D), 1),
       ("ffn_up", (DEPTH, D, D_UP // N_DEV), 2), ("ffn_down", (DEPTH, D_FF // N_DEV, D), 1))
SMALL = (("meta_tokens", (N_META, D // N_DEV), 1), ("gla_gate_w2", (DEPTH, GATE_RANK, 256 // N_DEV), 2),
         ("ffn_conv_w", (DEPTH, 3, D_UP // N_DEV), 2))
REPL = (("pre_mix_norm", (DEPTH, D)), ("gla_gate_b", (DEPTH, 256)), ("ret_norm_w", (DEPTH, 512)),
        ("gla_norm_w", (DEPTH, 512)), ("post_mix_norm", (DEPTH, D)), ("pre_ffn_norm", (DEPTH, D)),
        ("ffn_conv_b", (DEPTH, D_UP)), ("post_ffn_norm", (DEPTH, D)))
WEIGHT_ORDER = ("meta_tokens", "pre_mix_norm", "w_in", "gla_gate_w2", "gla_gate_b", "ret_norm_w", "gla_norm_w",
                "w_out", "post_mix_norm", "pre_ffn_norm", "ffn_up", "ffn_conv_w", "ffn_conv_b", "ffn_down",
                "post_ffn_norm")


def _size(shape):
    return math.prod(shape)


def _round_up(n, mult):
    return -(-n // mult) * mult


REPL_ROWS = _round_up(-(-sum(_size(s) for _, s in REPL) // LANES), 8)
SMALL_ROWS = _round_up(-(-sum(_size(s) for _, s, _ in SMALL) // LANES), 8)


def _pack(arrays, rows, dtype):
    flat = jnp.concatenate([a.reshape(-1).astype(dtype) for a in arrays])
    return jnp.pad(flat, (0, rows * LANES - flat.shape[0])).reshape(rows, LANES)


def _unpack(buf, shapes):
    flat = buf.reshape(-1)
    out, off = [], 0
    for shape in shapes:
        out.append(flat[off:off + _size(shape)].reshape(shape))
        off += _size(shape)
    return out


def _unshard(blocks, axis):
    moved = jnp.moveaxis(blocks, 0, axis)
    shape = list(moved.shape)
    shape[axis:axis + 2] = [shape[axis] * shape[axis + 1]]
    return moved.reshape(shape)


def _to_blocks(full, axis):
    shape = list(full.shape)
    shape[axis:axis + 1] = [N_DEV, shape[axis] // N_DEV]
    return jnp.moveaxis(full.reshape(shape), axis, 0)


def _interleave_cols(w):
    lead = w.shape[:-1]
    return jnp.swapaxes(w.reshape(lead + (2, N_CONV_BLOCKS, CONV_BLOCK)), -3, -2).reshape(lead + (D_UP,))


def _deinterleave_cols(w):
    lead = w.shape[:-1]
    return jnp.swapaxes(w.reshape(lead + (N_CONV_BLOCKS, 2, CONV_BLOCK)), -3, -2).reshape(lead + (D_UP,))


def _rope_tables():
    half = RET_DK // 2
    inv = ROPE_BASE ** (-jnp.arange(half, dtype=F32) / half)
    pos = jnp.arange(LP, dtype=F32) - float(PAD_ROWS)
    ang = pos[:, None] * inv[None, :]
    c, s = jnp.cos(ang), jnp.sin(ang)
    return jnp.concatenate([c, c], axis=1), jnp.concatenate([-s, s], axis=1)


def kernel(x, meta_tokens, pre_mix_norm, w_in, gla_gate_w2, gla_gate_b, ret_norm_w, gla_norm_w, w_out, post_mix_norm, pre_ffn_norm, ffn_up, ffn_conv_w, ffn_conv_b, ffn_down, post_ffn_norm, loss_target, m_meta_tokens, m_pre_mix_norm, m_w_in, m_gla_gate_w2, m_gla_gate_b, m_ret_norm_w, m_gla_norm_w, m_w_out, m_post_mix_norm, m_pre_ffn_norm, m_ffn_up, m_ffn_conv_w, m_ffn_conv_b, m_ffn_down, m_post_ffn_norm, v_meta_tokens, v_pre_mix_norm, v_w_in, v_gla_gate_w2, v_gla_gate_b, v_ret_norm_w, v_gla_norm_w, v_w_out, v_post_mix_norm, v_pre_ffn_norm, v_ffn_up, v_ffn_conv_w, v_ffn_conv_b, v_ffn_down, v_post_ffn_norm):
    weights = dict(meta_tokens=meta_tokens, pre_mix_norm=pre_mix_norm, w_in=w_in, gla_gate_w2=gla_gate_w2,
                   gla_gate_b=gla_gate_b, ret_norm_w=ret_norm_w, gla_norm_w=gla_norm_w, w_out=w_out,
                   post_mix_norm=post_mix_norm, pre_ffn_norm=pre_ffn_norm, ffn_up=ffn_up, ffn_conv_w=ffn_conv_w,
                   ffn_conv_b=ffn_conv_b, ffn_down=ffn_down, post_ffn_norm=post_ffn_norm)
    mom1 = dict(meta_tokens=m_meta_tokens, pre_mix_norm=m_pre_mix_norm, w_in=m_w_in, gla_gate_w2=m_gla_gate_w2,
                gla_gate_b=m_gla_gate_b, ret_norm_w=m_ret_norm_w, gla_norm_w=m_gla_norm_w, w_out=m_w_out,
                post_mix_norm=m_post_mix_norm, pre_ffn_norm=m_pre_ffn_norm, ffn_up=m_ffn_up,
                ffn_conv_w=m_ffn_conv_w, ffn_conv_b=m_ffn_conv_b, ffn_down=m_ffn_down, post_ffn_norm=m_post_ffn_norm)
    mom2 = dict(meta_tokens=v_meta_tokens, pre_mix_norm=v_pre_mix_norm, w_in=v_w_in, gla_gate_w2=v_gla_gate_w2,
                gla_gate_b=v_gla_gate_b, ret_norm_w=v_ret_norm_w, gla_norm_w=v_gla_norm_w, w_out=v_w_out,
                post_mix_norm=v_post_mix_norm, pre_ffn_norm=v_pre_ffn_norm, ffn_up=v_ffn_up,
                ffn_conv_w=v_ffn_conv_w, ffn_conv_b=v_ffn_conv_b, ffn_down=v_ffn_down, post_ffn_norm=v_post_ffn_norm)

    pad_cols = lambda a, width: jnp.pad(a, ((0, 0), (0, width - a.shape[1])))
    big_names = [n for n, _, _ in BIG]
    shard = {}
    for l in range(DEPTH):
        shard[l, "w_in"] = pad_cols(w_in[l].astype(BF16), IN_SHARD_P)
        shard[l, "w_out"] = w_out[l].astype(BF16)
        shard[l, "ffn_up"] = pad_cols(ffn_up[l].astype(BF16), UP_SHARD_P)
        shard[l, "ffn_down"] = ffn_down[l].astype(BF16)
    gathered = {(0, "w_in"): _all_gather([shard[0, "w_in"]], "gather_w_in_0")[0]}
    gather_in_mixer = {l: [(l, n) for n in big_names[1:]] + ([(l + 1, "w_in")] if l + 1 < DEPTH else [])
                       for l in range(DEPTH)}
    small = _all_gather([_pack([weights[n] for n, _, _ in SMALL], SMALL_ROWS, F32)], "gather_small_weights")[0]
    small_parts = _unpack_blocks(small, [s for _, s, _ in SMALL])
    full = {n: _unshard(p, ax) for (n, _, ax), p in zip(SMALL, small_parts)}
    w2p = jnp.pad(full["gla_gate_w2"], ((0, 0), (0, 128 - GATE_RANK), (0, 0)))
    cw8 = jnp.concatenate([_interleave_cols(full["ffn_conv_w"]), _interleave_cols(ffn_conv_b)[:, None, :],
                           jnp.zeros((DEPTH, 4, D_UP), F32)], axis=1)
    cos2, sin2 = _rope_tables()

    h = jnp.concatenate([jnp.zeros((PAD_ROWS, D), F32), full["meta_tokens"], x[0]], axis=0)
    target = jnp.concatenate([jnp.zeros((CHUNK, D), F32), loss_target[0]], axis=0)
    saved, layer_w = [], []
    for l in range(DEPTH):
        lw = dict(w_in=_shards_to_cols(gathered[l, "w_in"], _pieces_w_in(), IN_WP, f"w_in_cols_{l}"))
        a1 = _rmsnorm_fwd(h, pre_mix_norm[l:l + 1], f"pre_mix_norm_{l}")
        proj = _matmul(a1, lw["w_in"], out_dtype=F32, tm=TM, tn=1280, tk=D, name=f"in_proj_{l}", n_outer=True)
        keys = gather_in_mixer.get(l, [])
        ocat, merged, sr_all, sg_all, *got = _mixer_fwd(proj, cos2, sin2, w2p[l], gla_gate_b[l:l + 1],
                                                        ret_norm_w[l:l + 1], gla_norm_w[l:l + 1], f"mixer_fwd_{l}",
                                                        carried=[shard[key] for key in keys])
        gathered.update(zip(keys, got))
        lw["w_out"] = gathered[l, "w_out"].reshape(D, D)
        lw["w_up"] = _shards_to_cols(gathered[l, "ffn_up"], _pieces_ffn_up(), D_UP, f"ffn_up_cols_{l}")
        lw["w_down"] = gathered[l, "ffn_down"].reshape(D_FF, D)
        layer_w.append(lw)
        m, h1 = _matmul_resid_norm(merged, lw["w_out"], h, post_mix_norm[l:l + 1], f"out_proj_{l}")
        a2 = _rmsnorm_fwd(h1, pre_ffn_norm[l:l + 1], f"pre_ffn_norm_{l}")
        u = _matmul(a2, lw["w_up"], out_dtype=BF16, tm=TM, tn=1408, tk=D, name=f"ffn_up_{l}", n_outer=True)
        cv, act = _conv_act_fwd(u, cw8[l], f"ffn_conv_act_{l}")
        f, h2 = _matmul_resid_norm(act, lw["w_down"], h1, post_ffn_norm[l:l + 1], f"ffn_down_{l}")
        saved.append(dict(h=h, a1=a1, proj=proj, ocat=ocat, merged=merged, sr=sr_all, sg=sg_all, m=m, h1=h1,
                          a2=a2, u=u, cv=cv, act=act, f=f))
        h = h2

    dh, loss_acc = _loss_head(h, target, "loss_head")
    loss = lax.psum(loss_acc[0, 0], ("x", "y", "c"))

    kinds = ("grad", "delta", "new_m", "new_v")
    grads = {n: [None] * DEPTH for n in WEIGHT_ORDER if n != "meta_tokens" and n not in big_names}
    pending, parts = [], {}
    for l in reversed(range(DEPTH)):
        s, lw = saved[l], layer_w[l]
        dact, df, g_post_ffn = _norm_bwd_matmul(dh, s["f"], post_ffn_norm[l:l + 1], lw["w_down"], BF16,
                                                f"ffn_down_dx_{l}")
        g_down = _matmul(s["act"], df, ta=True, out_dtype=BF16, tm=D_FF // 2, tn=D, tk=TK_ROWS, name=f"ffn_down_dw_{l}")
        du, dcw = _conv_act_bwd(dact, s["cv"], s["u"], cw8[l], f"ffn_conv_act_bwd_{l}")
        dh1, g_pre_ffn = _matmul_norm_bwd(du, lw["w_up"], s["h1"], pre_ffn_norm[l:l + 1], dh, 1408, f"ffn_up_dx_{l}")
        g_up = _matmul(s["a2"], du, ta=True, out_dtype=BF16, tm=D, tn=1408, tk=TK_ROWS, name=f"ffn_up_dw_{l}")
        dmerged, dm, g_post_mix = _norm_bwd_matmul(dh1, s["m"], post_mix_norm[l:l + 1], lw["w_out"], F32,
                                                   f"out_proj_dx_{l}")
        g_out = _matmul(s["merged"], dm, ta=True, out_dtype=BF16, tm=D, tn=D, tk=TK_ROWS, name=f"out_proj_dw_{l}")
        pending += [((l, "ffn_down"), g_down.reshape(N_DEV, D_FF // N_DEV, D)),
                    ((l, "ffn_up"), _cols_to_shards(g_up, _pieces_ffn_up(), UP_SHARD_P, f"ffn_up_grad_shards_{l}")),
                    ((l, "w_out"), g_out.reshape(N_DEV, D // N_DEV, D))]
        dproj, g_w2, g_gb, g_rn, g_gn, *got = _mixer_bwd(s["proj"], s["ocat"], dmerged, s["sr"], s["sg"], cos2, sin2,
                                                         w2p[l], gla_gate_b[l:l + 1], ret_norm_w[l:l + 1],
                                                         gla_norm_w[l:l + 1], f"mixer_bwd_{l}",
                                                         carried=[blocks for _, blocks in pending])
        parts.update(zip([key for key, _ in pending], got))
        g_in = _matmul(s["a1"], dproj, ta=True, out_dtype=BF16, tm=D, tn=1280, tk=TK_ROWS, name=f"in_proj_dw_{l}")
        pending = [((l, "w_in"), _cols_to_shards(g_in, _pieces_w_in(), IN_SHARD_P, f"w_in_grad_shards_{l}"))]
        now = pending if l == 0 else []
        dh, g_pre_mix, *got = _matmul_norm_bwd(dproj, lw["w_in"], s["h"], pre_mix_norm[l:l + 1], dh1, IN_WP,
                                               f"in_proj_dx_{l}", carried=[blocks for _, blocks in now])
        parts.update(zip([key for key, _ in now], got))
        pending = [] if l == 0 else pending
        grads["post_ffn_norm"][l] = g_post_ffn[0]
        grads["ffn_conv_w"][l] = _deinterleave_cols(dcw[0:3])
        grads["ffn_conv_b"][l] = _deinterleave_cols(dcw[3])
        grads["pre_ffn_norm"][l] = g_pre_ffn[0]
        grads["post_mix_norm"][l] = g_post_mix[0]
        grads["gla_gate_w2"][l] = g_w2[:GATE_RANK]
        grads["gla_gate_b"][l] = g_gb[0]
        grads["ret_norm_w"][l] = g_rn[0]
        grads["gla_norm_w"][l] = g_gn[0]
        grads["pre_mix_norm"][l] = g_pre_mix[0]
    local = {n: jnp.stack(v) for n, v in grads.items()}
    local["meta_tokens"] = dh[PAD_ROWS:CHUNK]
    grad_x = dh[CHUNK:][None]

    blocks = jnp.concatenate([_to_blocks(local[n], ax).reshape(N_DEV, -1) for n, _, ax in SMALL], axis=1)
    blocks = jnp.pad(blocks, ((0, 0), (0, SMALL_ROWS * LANES - blocks.shape[1]))).reshape(N_DEV, SMALL_ROWS, LANES)
    *got, small_grad_parts = _exchange_blocks([b for _, b in pending] + [blocks], "exchange_last_grads")
    parts.update(zip([key for key, _ in pending], got))

    widths = dict(w_in=IN_SHARD_P, w_out=D, ffn_up=UP_SHARD_P, ffn_down=D)
    steps = dict(w_in=256, w_out=D // N_DEV, ffn_up=256, ffn_down=D_FF // N_DEV // 2)
    big_out = {kind: {n: [None] * DEPTH for n in big_names} for kind in kinds}
    for l in range(DEPTH):
        for n in big_names:
            mine = [pad_cols(d[n][l], widths[n]) for d in (weights, mom1, mom2)]
            results = _adamw(parts[l, n], *mine, steps[n], f"adamw_{n}_{l}")
            for kind, r in zip(kinds, results):
                big_out[kind][n][l] = r[:, :weights[n].shape[2]]
    out = {kind: {n: jnp.stack(v) for n, v in big_out[kind].items()} for kind in kinds}
    shard_shapes = [s for _, s, _ in SMALL]
    packed = [_pack([d[n] for n, _, _ in SMALL], SMALL_ROWS, F32) for d in (weights, mom1, mom2)]
    results = _adamw(small_grad_parts, *packed, SMALL_ROWS, "adamw_small_sharded")
    for kind, buf in zip(kinds, results):
        out[kind].update(zip([n for n, _, _ in SMALL], _unpack(buf, shard_shapes)))

    repl_parts = _all_gather([_pack([local[n] for n, _ in REPL], REPL_ROWS, F32)], "gather_small_grads")[0]
    packed = [_pack([d[n] for n, _ in REPL], REPL_ROWS, F32) for d in (weights, mom1, mom2)]
    results = _adamw(repl_parts, *packed, REPL_ROWS, "adamw_replicated")
    repl_shapes = [s for _, s in REPL]
    for kind, buf in zip(kinds, results):
        out[kind].update(zip([n for n, _ in REPL], _unpack(buf, repl_shapes)))

    return (loss, grad_x, *[out["grad"][n] for n in WEIGHT_ORDER], *[out["delta"][n] for n in WEIGHT_ORDER],
            *[out["new_m"][n] for n in WEIGHT_ORDER], *[out["new_v"][n] for n in WEIGHT_ORDER])


def _unpack_blocks(gathered, shapes):
    flat = gathered.reshape(N_DEV, -1)
    out, off = [], 0
    for shape in shapes:
        out.append(flat[:, off:off + _size(shape)].reshape((N_DEV,) + shape))
        off += _size(shape)
    return out
```

```python
import math

import jax
import jax.numpy as jnp
from jax import lax
from jax.experimental import pallas as pl
from jax.experimental.pallas import tpu as pltpu

F32 = jnp.float32
BF16 = jnp.bfloat16

D = 1024
SEQ = 8192
DEPTH = 2
N_META = 16
CHUNK = 64
SUB = 16
N_SUB = CHUNK // SUB
PAD_ROWS = CHUNK - N_META
LP = SEQ + CHUNK
N_CHUNKS = LP // CHUNK
RET_HEADS = 4
RET_DK = 128
GLA_HEADS = 4
GLA_DK = 64
GLA_DV = 128
GLA_TAU = 16.0
GATE_RANK = 16
IN_W = 3600
IN_WP = 3840
D_FF = 2816
D_UP = 2 * D_FF
CONV_BLOCK = 256
N_CONV_BLOCKS = D_FF // CONV_BLOCK
ROPE_BASE = 10000.0
EPS = 1e-6
N_DEV = 8
LANES = 1024

O_RQ, O_RK, O_RV, O_RG = 0, 512, 1024, 1536
O_GQ, O_GK, O_GV, O_GR, O_GA = 2048, 2304, 2560, 3072, 3584

ADAM_LR = 0.001
ADAM_B1 = 0.9
ADAM_B2 = 0.999
ADAM_EPS = 1e-08
ADAM_WD = 0.01
ADAM_STEP = 10

VMEM_LIMIT = 56 * 1024 * 1024
MESH_IDS = pl.DeviceIdType.MESH


def _row_tile(rows, limit):
    best = 16
    for t in range(16, min(rows, limit) + 1, 16):
        if rows % t == 0:
            best = t
    return best


TM = _row_tile(LP, 688)
TK_ROWS = _row_tile(LP, 1376)


def _cparams(*sem):
    return pltpu.CompilerParams(dimension_semantics=sem, vmem_limit_bytes=VMEM_LIMIT)


def _dot(a, b):
    return jnp.dot(a.astype(BF16), b.astype(BF16), preferred_element_type=F32)


def _dot_nt(a, b):
    return lax.dot_general(a.astype(BF16), b.astype(BF16), (((1,), (1,)), ((), ())), preferred_element_type=F32)


def _dot_tn(a, b):
    return lax.dot_general(a.astype(BF16), b.astype(BF16), (((0,), (0,)), ((), ())), preferred_element_type=F32)


def _split3(x):
    hi = x.astype(BF16)
    r1 = x - hi.astype(F32)
    mid = r1.astype(BF16)
    lo = (r1 - mid.astype(F32)).astype(BF16)
    return hi, mid, lo


def _dot_exact_rhs(t, x):
    hi, mid, lo = _split3(x)
    t = t.astype(BF16)
    return (jnp.dot(t, hi, preferred_element_type=F32) + jnp.dot(t, mid, preferred_element_type=F32)
            + jnp.dot(t, lo, preferred_element_type=F32))


def _dot_tn_exact_lhs(x, ones):
    dims = (((0,), (0,)), ((), ()))
    hi, mid, lo = _split3(x)
    ones = ones.astype(BF16)
    return (lax.dot_general(hi, ones, dims, preferred_element_type=F32)
            + lax.dot_general(mid, ones, dims, preferred_element_type=F32)
            + lax.dot_general(lo, ones, dims, preferred_element_type=F32))


def _sigmoid(x):
    return 1.0 / (1.0 + jnp.exp(-x))


def _matmul(a, b, *, ta=False, tb=False, out_dtype, tm, tn, tk, name, n_outer=False):
    m = a.shape[1] if ta else a.shape[0]
    k = a.shape[0] if ta else a.shape[1]
    n = b.shape[0] if tb else b.shape[1]
    assert (b.shape[1] if tb else b.shape[0]) == k
    assert m % tm == 0 and n % tn == 0 and k % tk == 0, (name, m, n, k, tm, tn, tk)
    nk = k // tk
    order = (lambda f: (lambda j, i, kk: f(i, j, kk))) if n_outer else (lambda f: f)
    a_spec = (pl.BlockSpec((tk, tm), order(lambda i, j, kk: (kk, i))) if ta
              else pl.BlockSpec((tm, tk), order(lambda i, j, kk: (i, kk))))
    b_spec = (pl.BlockSpec((tn, tk), order(lambda i, j, kk: (j, kk))) if tb
              else pl.BlockSpec((tk, tn), order(lambda i, j, kk: (kk, j))))
    dims = (((0 if ta else 1,), (1 if tb else 0,)), ((), ()))

    def body(a_ref, b_ref, o_ref, *acc):
        prod = lax.dot_general(a_ref[...].astype(BF16), b_ref[...].astype(BF16), dims, preferred_element_type=F32)
        if nk == 1:
            o_ref[...] = prod.astype(out_dtype)
            return
        acc_ref, = acc
        kk = pl.program_id(2)

        @pl.when(kk == 0)
        def _():
            acc_ref[...] = prod

        @pl.when(kk > 0)
        def _():
            acc_ref[...] += prod

        @pl.when(kk == nk - 1)
        def _():
            o_ref[...] = acc_ref[...].astype(out_dtype)

    return pl.pallas_call(
        body, name=name, grid=(n // tn, m // tm, nk) if n_outer else (m // tm, n // tn, nk),
        in_specs=[a_spec, b_spec],
        out_specs=pl.BlockSpec((tm, tn), order(lambda i, j, kk: (i, j))),
        out_shape=jax.ShapeDtypeStruct((m, n), out_dtype),
        scratch_shapes=[pltpu.VMEM((tm, tn), F32)] if nk > 1 else [],
        compiler_params=_cparams("parallel", "parallel", "arbitrary"),
    )(a, b)


def _matmul_resid_norm(a, b, h, w, name):
    k = a.shape[1]

    def body(a_ref, b_ref, h_ref, w_ref, m_ref, o_ref):
        m = jnp.dot(a_ref[...].astype(BF16), b_ref[...].astype(BF16), preferred_element_type=F32)
        m_ref[...] = m
        r = lax.rsqrt(jnp.mean(m * m, axis=-1, keepdims=True) + EPS)
        row = pl.program_id(0) * TM + lax.broadcasted_iota(jnp.int32, (TM, 1), 0)
        o_ref[...] = h_ref[...] + jnp.where(row >= PAD_ROWS, m * r * w_ref[...], 0.0)

    tile = pl.BlockSpec((TM, D), lambda i: (i, 0))
    return pl.pallas_call(
        body, name=name, grid=(LP // TM,),
        in_specs=[pl.BlockSpec((TM, k), lambda i: (i, 0)), pl.BlockSpec((k, D), lambda i: (0, 0)), tile,
                  pl.BlockSpec((1, D), lambda i: (0, 0))],
        out_specs=[tile, tile],
        out_shape=[jax.ShapeDtypeStruct((LP, D), F32), jax.ShapeDtypeStruct((LP, D), F32)],
        compiler_params=_cparams("parallel"),
    )(a, b, h, w)


def _rmsnorm_bwd_rows(dy, x, w):
    r = lax.rsqrt(jnp.mean(x * x, axis=-1, keepdims=True) + EPS)
    g = dy * w
    dx = r * g - x * (r * r * r * jnp.mean(g * x, axis=-1, keepdims=True))
    return dx, jnp.sum(dy * x * r, axis=0, keepdims=True)


def _matmul_norm_bwd(dz, b, x, w, resid, tk, name, carried=()):
    k = dz.shape[1]
    assert k % tk == 0
    nk = k // tk
    n_rows = LP // TM
    n_carried = len(carried)

    def body(*refs):
        a_ref, b_ref, x_ref, w_ref, r_ref = refs[:5]
        g_refs, refs = refs[5:5 + n_carried], refs[5 + n_carried:]
        dx_ref, dw_ref = refs[:2]
        got_refs, refs = refs[2:2 + n_carried], refs[2 + n_carried:]
        acc, sems = (refs[:1], refs[1:]) if nk > 1 else ((), refs)
        i, kk = pl.program_id(0), pl.program_id(1)
        if n_carried:
            exchange_start, exchange_finish = _exchange_phases(g_refs, got_refs, *sems)
            pl.when((i == 0) & (kk == 0))(exchange_start)

        @pl.when((i == 0) & (kk == 0))
        def _():
            dw_ref[...] = jnp.zeros_like(dw_ref)

        prod = lax.dot_general(a_ref[...].astype(BF16), b_ref[...].astype(BF16), (((1,), (1,)), ((), ())),
                               preferred_element_type=F32)

        def finish(dy):
            dx, dw = _rmsnorm_bwd_rows(dy, x_ref[...], w_ref[...])
            dx_ref[...] = dx + r_ref[...]
            dw_ref[0:1, :] += dw

        if nk == 1:
            finish(prod)
        else:
            acc_ref, = acc

            @pl.when(kk == 0)
            def _():
                acc_ref[...] = prod

            @pl.when((kk > 0) & (kk < nk - 1))
            def _():
                acc_ref[...] += prod

            @pl.when(kk == nk - 1)
            def _():
                finish(acc_ref[...] + prod)

        if n_carried:
            pl.when((i == n_rows - 1) & (kk == nk - 1))(exchange_finish)

    tile = pl.BlockSpec((TM, D), lambda i, kk: (i, 0))
    anywhere = [pl.BlockSpec(memory_space=pl.ANY)] * n_carried
    return pl.pallas_call(
        body, name=name, grid=(n_rows, nk),
        in_specs=[pl.BlockSpec((TM, tk), lambda i, kk: (i, kk)), pl.BlockSpec((D, tk), lambda i, kk: (0, kk)), tile,
                  pl.BlockSpec((1, D), lambda i, kk: (0, 0)), tile] + anywhere,
        out_specs=[tile, pl.BlockSpec((8, D), lambda i, kk: (0, 0))] + anywhere,
        out_shape=[jax.ShapeDtypeStruct((LP, D), F32), jax.ShapeDtypeStruct((8, D), F32)]
        + [jax.ShapeDtypeStruct(g.shape, g.dtype) for g in carried],
        scratch_shapes=([pltpu.VMEM((TM, D), F32)] if nk > 1 else []) + _exchange_sems(n_carried),
        compiler_params=_cparams("arbitrary", "arbitrary"),
    )(dz, b, x, w, resid, *carried)


def _norm_bwd_matmul(dh, x, w, b, out_dtype, name):
    n = b.shape[0]

    def body(dh_ref, x_ref, w_ref, b_ref, o_ref, dx_ref, dw_ref):
        i = pl.program_id(0)

        @pl.when(i == 0)
        def _():
            dw_ref[...] = jnp.zeros_like(dw_ref)

        row = i * TM + lax.broadcasted_iota(jnp.int32, (TM, 1), 0)
        dy = jnp.where(row >= PAD_ROWS, dh_ref[...], 0.0)
        dx, dw = _rmsnorm_bwd_rows(dy, x_ref[...], w_ref[...])
        dxb = dx.astype(BF16)
        dx_ref[...] = dxb
        dw_ref[0:1, :] += dw
        o_ref[...] = lax.dot_general(dxb, b_ref[...].astype(BF16), (((1,), (1,)), ((), ())),
                                     preferred_element_type=F32).astype(out_dtype)

    tile = pl.BlockSpec((TM, D), lambda i: (i, 0))
    return pl.pallas_call(
        body, name=name, grid=(LP // TM,),
        in_specs=[tile, tile, pl.BlockSpec((1, D), lambda i: (0, 0)), pl.BlockSpec((n, D), lambda i: (0, 0))],
        out_specs=[pl.BlockSpec((TM, n), lambda i: (i, 0)), tile, pl.BlockSpec((8, D), lambda i: (0, 0))],
        out_shape=[jax.ShapeDtypeStruct((LP, n), out_dtype), jax.ShapeDtypeStruct((LP, D), BF16),
                   jax.ShapeDtypeStruct((8, D), F32)],
        compiler_params=_cparams("arbitrary"),
    )(dh, x, w, b)


def _rmsnorm_fwd(x, w, name):
    def body(x_ref, w_ref, o_ref):
        xv = x_ref[...]
        r = lax.rsqrt(jnp.mean(xv * xv, axis=-1, keepdims=True) + EPS)
        o_ref[...] = (xv * r * w_ref[...]).astype(BF16)

    return pl.pallas_call(
        body, name=name, grid=(LP // TM,),
        in_specs=[pl.BlockSpec((TM, D), lambda i: (i, 0)), pl.BlockSpec((1, D), lambda i: (0, 0))],
        out_specs=pl.BlockSpec((TM, D), lambda i: (i, 0)),
        out_shape=jax.ShapeDtypeStruct((LP, D), BF16),
        compiler_params=_cparams("parallel"),
    )(x, w)


def _loss_head(y, target, name):
    def body(y_ref, t_ref, dy_ref, loss_ref):
        i = pl.program_id(0)

        @pl.when(i == 0)
        def _():
            loss_ref[...] = jnp.zeros_like(loss_ref)

        row = i * TM + lax.broadcasted_iota(jnp.int32, (TM, 1), 0)
        diff = jnp.where(row >= CHUNK, y_ref[...] - t_ref[...], 0.0)
        dy_ref[...] = diff * (1.0 / D)
        loss_ref[...] += (0.5 / D) * jnp.sum(diff * diff)

    tile = pl.BlockSpec((TM, D), lambda i: (i, 0))
    return pl.pallas_call(
        body, name=name, grid=(LP // TM,),
        in_specs=[tile, tile],
        out_specs=[tile, pl.BlockSpec((8, 128), lambda i: (0, 0))],
        out_shape=[jax.ShapeDtypeStruct((LP, D), F32), jax.ShapeDtypeStruct((8, 128), F32)],
        compiler_params=_cparams("arbitrary"),
    )(y, target)


GELU_C = math.sqrt(2.0 / math.pi)
GELU_K = 0.044715
STRIP = 16


def _shift_down(x, prev8, rows):
    row = lax.broadcasted_iota(jnp.int32, (rows, 1), 0)
    p1 = pltpu.roll(prev8, 1, 0)
    p2 = pltpu.roll(prev8, 2, 0)
    x1 = jnp.where(row == 0, p1[0:1, :], pltpu.roll(x, 1, 0))
    x2 = jnp.where(row == 0, p2[0:1, :], jnp.where(row == 1, p2[1:2, :], pltpu.roll(x, 2, 0)))
    return x1, x2


def _conv_act_fwd(u, cw8, name):
    n_rows = LP // TM
    cb2 = 2 * CONV_BLOCK

    def body(u_ref, cw_ref, conv_ref, act_ref, carry_ref):
        i = pl.program_id(1)

        @pl.when(i == 0)
        def _():
            carry_ref[...] = jnp.zeros_like(carry_ref)

        x = u_ref[...].astype(F32)
        x1, x2 = _shift_down(x, carry_ref[...], TM)
        conv = cw_ref[3:4, :] + x2 * cw_ref[0:1, :] + x1 * cw_ref[1:2, :] + x * cw_ref[2:3, :]
        conv_ref[...] = conv.astype(BF16)
        a = conv[:, :CONV_BLOCK]
        g = conv[:, CONV_BLOCK:]
        t = jnp.tanh(GELU_C * (a + GELU_K * a * a * a))
        act_ref[...] = (0.5 * a * (1.0 + t) * g).astype(BF16)
        carry_ref[...] = x[TM - 8:TM, :]

    return pl.pallas_call(
        body, name=name, grid=(N_CONV_BLOCKS, n_rows),
        in_specs=[pl.BlockSpec((TM, cb2), lambda j, i: (i, j)), pl.BlockSpec((8, cb2), lambda j, i: (0, j))],
        out_specs=[pl.BlockSpec((TM, cb2), lambda j, i: (i, j)), pl.BlockSpec((TM, CONV_BLOCK), lambda j, i: (i, j))],
        out_shape=[jax.ShapeDtypeStruct((LP, D_UP), BF16), jax.ShapeDtypeStruct((LP, D_FF), BF16)],
        scratch_shapes=[pltpu.VMEM((8, cb2), F32)],
        compiler_params=_cparams("arbitrary", "arbitrary"),
    )(u, cw8)


def _conv_act_bwd(dact, conv, u, cw8, name):
    n_rows = LP // TM
    cb2 = 2 * CONV_BLOCK
    n_strips = TM // STRIP

    def body(dact_ref, conv_ref, u_ref, cw_ref, du_ref, dcw_ref, carry_ref):
        i = pl.program_id(1)

        @pl.when(i == 0)
        def _():
            dcw_ref[...] = jnp.zeros_like(dcw_ref)
            carry_ref[...] = jnp.zeros_like(carry_ref)

        w0, w1, w2 = cw_ref[0:1, :], cw_ref[1:2, :], cw_ref[2:3, :]
        row = lax.broadcasted_iota(jnp.int32, (STRIP, 1), 0)
        fold = lambda z: z[:8, :] + z[8:, :]

        def strip(k, carry):
            n1, n2, s0, s1, s2, s3 = carry
            r0 = pl.multiple_of((n_strips - 1 - k) * STRIP, STRIP)
            cv = conv_ref[pl.ds(r0, STRIP), :].astype(F32)
            a = cv[:, :CONV_BLOCK]
            g = cv[:, CONV_BLOCK:]
            t = jnp.tanh(GELU_C * (a + GELU_K * a * a * a))
            gel = 0.5 * a * (1.0 + t)
            dgel = 0.5 * (1.0 + t) + 0.5 * a * (1.0 - t * t) * (GELU_C * (1.0 + 3.0 * GELU_K * a * a))
            dav = dact_ref[pl.ds(r0, STRIP), :].astype(F32)
            dconv = jnp.concatenate([dav * g * dgel, dav * gel], axis=1)
            u1 = pltpu.roll(dconv, STRIP - 1, 0)
            u2 = pltpu.roll(dconv, STRIP - 2, 0)
            d1 = jnp.where(row >= STRIP - 1, n1, u1)
            d2 = jnp.where(row >= STRIP - 2, n2, u2)
            du_ref[pl.ds(r0, STRIP), :] = (dconv * w2 + d1 * w1 + d2 * w0).astype(BF16)
            x = u_ref[pl.ds(r0, STRIP), :].astype(F32)
            return (u1, u2, s0 + fold(d2 * x), s1 + fold(d1 * x), s2 + fold(dconv * x), s3 + fold(dconv))

        below = carry_ref[...]
        zero = jnp.zeros((8, cb2), F32)
        init = (pltpu.roll(below, STRIP - 1, 0), pltpu.roll(below, STRIP - 2, 0), zero, zero, zero, zero)
        u1, _, s0, s1, s2, s3 = lax.fori_loop(0, n_strips, strip, init)
        carry_ref[...] = pltpu.roll(u1, 1, 0)
        dcw_ref[0:1, :] += jnp.sum(s0, axis=0, keepdims=True)
        dcw_ref[1:2, :] += jnp.sum(s1, axis=0, keepdims=True)
        dcw_ref[2:3, :] += jnp.sum(s2, axis=0, keepdims=True)
        dcw_ref[3:4, :] += jnp.sum(s3, axis=0, keepdims=True)

    rev = lambda j, i: (n_rows - 1 - i, j)
    return pl.pallas_call(
        body, name=name, grid=(N_CONV_BLOCKS, n_rows),
        in_specs=[pl.BlockSpec((TM, CONV_BLOCK), rev), pl.BlockSpec((TM, cb2), rev), pl.BlockSpec((TM, cb2), rev),
                  pl.BlockSpec((8, cb2), lambda j, i: (0, j))],
        out_specs=[pl.BlockSpec((TM, cb2), rev), pl.BlockSpec((8, cb2), lambda j, i: (0, j))],
        out_shape=[jax.ShapeDtypeStruct((LP, D_UP), BF16), jax.ShapeDtypeStruct((8, D_UP), F32)],
        scratch_shapes=[pltpu.VMEM((STRIP, cb2), F32)],
        compiler_params=_cparams("arbitrary", "arbitrary"),
    )(dact, conv, u, cw8)


CHUNKS_PER_STEP = 3 if N_CHUNKS % 3 == 0 else 1
STEP_ROWS = CHUNKS_PER_STEP * CHUNK
N_STEPS = N_CHUNKS // CHUNKS_PER_STEP


def _ret_consts(h):
    rows = STEP_ROWS
    lg = math.log(1.0 - 2.0 ** (-5.0 - h))
    ri = lax.broadcasted_iota(jnp.int32, (rows, rows), 0)
    ci = lax.broadcasted_iota(jnp.int32, (rows, rows), 1)
    diff = (ri - ci).astype(F32)
    dmat = jnp.where(diff >= 0, jnp.exp(lg * jnp.maximum(diff, 0.0)), 0.0)
    rowf = lax.broadcasted_iota(jnp.int32, (rows, 1), 0).astype(F32)
    zeta = jnp.exp(lg * (rows - 1.0 - rowf))
    xi = jnp.exp(lg * (rowf + 1.0))
    return dmat, zeta, xi, math.exp(lg * rows)


def _rope(t, cosv, sinv):
    return t * cosv + pltpu.roll(t, RET_DK // 2, 1) * sinv


def _unrope(d, cosv, sinv):
    return d * cosv + pltpu.roll(d * sinv, RET_DK // 2, 1)


def _gla_common(p_ref, w2_ref, gb_ref, chunk, rows):
    row = lax.broadcasted_iota(jnp.int32, (CHUNK, 1), 0)
    real = (chunk * CHUNK + row) >= PAD_ROWS
    ga = p_ref[rows, O_GA:O_GA + 128]
    z = _dot(ga, w2_ref[...]) + gb_ref[...]
    la = (jnp.minimum(z, 0.0) - jnp.log(1.0 + jnp.exp(-jnp.abs(z)))) * (1.0 / GLA_TAU)
    la = jnp.where(real, la, 0.0)
    ri = lax.broadcasted_iota(jnp.int32, (CHUNK, CHUNK), 0)
    ci = lax.broadcasted_iota(jnp.int32, (CHUNK, CHUNK), 1)
    tril = (ri >= ci).astype(F32)
    cum = _dot_exact_rhs(tril, la)
    last = cum[CHUNK - 1:CHUNK, :]
    qs = p_ref[rows, O_GQ:O_GQ + 256] * (GLA_DK ** -0.5)
    k = p_ref[rows, O_GK:O_GK + 256]
    ecum = jnp.exp(cum)
    ekl = jnp.exp(last - cum)
    el = jnp.exp(last)
    refs = [jnp.zeros((1, 256), F32)] + [cum[a * SUB - 1:a * SUB, :] for a in range(1, N_SUB)]
    eq = [jnp.exp(cum[a * SUB:(a + 1) * SUB, :] - refs[a]) for a in range(N_SUB)]
    spread = refs[0] - cum[SUB - 1:SUB, :]
    for a in range(1, N_SUB):
        spread = jnp.maximum(spread, refs[a] - cum[(a + 1) * SUB - 1:(a + 1) * SUB, :])
    small = jnp.max(spread) <= GLA_FACTORED_MAX
    return dict(real=real, row=row, z=z, la=la, cum=cum, last=last, qs=qs, k=k, ecum=ecum, ekl=ekl, el=el,
                refs=refs, eq=eq, small=small, ri=ri, ci=ci)


GLA_FACTORED_MAX = 40.0


def _head_block_mask():
    r = lax.broadcasted_iota(jnp.int32, (CHUNK, 256), 0)
    col = lax.broadcasted_iota(jnp.int32, (CHUNK, 256), 1)
    return (r // SUB) == (col // GLA_DK)


def _state_block_mask():
    r = lax.broadcasted_iota(jnp.int32, (GLA_HEADS * GLA_DK, GLA_HEADS * GLA_DV), 0)
    col = lax.broadcasted_iota(jnp.int32, (GLA_HEADS * GLA_DK, GLA_HEADS * GLA_DV), 1)
    return (r // GLA_DK) == (col // GLA_DV)


def _block_diagonal(blocks):
    zero = jnp.zeros((GLA_DK, GLA_DV), F32)
    return jnp.concatenate([jnp.concatenate([blocks[h] if g == h else zero for g in range(GLA_HEADS)], axis=1)
                            for h in range(GLA_HEADS)], axis=0)


def _gla_factored(c):
    mask = _head_block_mask()
    eks, keys, queries = [], [], []
    for a in range(N_SUB):
        ek = jnp.exp(jnp.minimum(c["refs"][a] - c["cum"], GLA_FACTORED_MAX))
        qh = c["qs"][a * SUB:(a + 1) * SUB, :] * c["eq"][a]
        eks.append(ek)
        keys.append(c["k"] * ek)
        queries.append(jnp.where(mask, jnp.concatenate([qh] * GLA_HEADS, axis=0), 0.0))
    return eks, keys, queries


def _gla_scores_factored(c, factored, p_scr):
    _, keys, queries = factored
    for a in range(N_SUB):
        out = _dot_nt(queries[a], keys[a])
        out = jnp.where(c["ci"] <= a * SUB + (c["ri"] & (SUB - 1)), out, 0.0)
        for h in range(GLA_HEADS):
            p_scr[h, a * SUB:(a + 1) * SUB, :] = out[h * SUB:(h + 1) * SUB, :]


def _gla_intra_bwd_factored(c, factored, dps, dq_scr, dk_scr):
    eks, keys, queries = factored
    mask = _head_block_mask()
    dk = jnp.zeros((CHUNK, 256), F32)
    for a in range(N_SUB):
        dpa = jnp.concatenate([dps[h][a * SUB:(a + 1) * SUB, :] for h in range(GLA_HEADS)], axis=0)
        dq = jnp.where(mask, _dot(dpa, keys[a]), 0.0)
        dq = dq[0:SUB] + dq[SUB:2 * SUB] + dq[2 * SUB:3 * SUB] + dq[3 * SUB:4 * SUB]
        dq_scr[a * SUB:(a + 1) * SUB, :] = dq * c["eq"][a]
        dk = dk + _dot_tn(dpa, queries[a]) * eks[a]
    dk_scr[...] = dk


def _gla_lag_weights(c):
    cum, row = c["cum"], c["row"]
    out = [jnp.ones((CHUNK, 256), F32)]
    for r in range(1, SUB):
        out.append(jnp.where((row % SUB) >= r, jnp.exp(jnp.minimum(cum - pltpu.roll(cum, r, 0), 0.0)), 0.0))
    return out


def _gla_pairwise_keys(c):
    return [None] + [c["k"] * jnp.exp(jnp.minimum(c["refs"][a] - c["cum"], 0.0)) for a in range(1, N_SUB)]


def _gla_scores_pairwise(c, lag_w, keys, h):
    sl = slice(GLA_DK * h, GLA_DK * (h + 1))
    qs, k = c["qs"][:, sl], c["k"][:, sl]
    ri, ci = c["ri"], c["ci"]
    p = jnp.zeros((CHUNK, CHUNK), F32)
    for r in range(SUB):
        kr = k if r == 0 else pltpu.roll(k, r, 0)
        pr = jnp.sum(qs * kr * lag_w[r][:, sl], axis=1, keepdims=True)
        p = p + jnp.where(ci == ri - r, pr, 0.0)
    blocks = [jnp.zeros((SUB, CHUNK), F32)]
    for a in range(1, N_SUB):
        qh = qs[a * SUB:(a + 1) * SUB, :] * c["eq"][a][:, sl]
        blocks.append(jnp.where(ci[:SUB, :] < a * SUB, _dot_nt(qh, keys[a][:, sl]), 0.0))
    return p + jnp.concatenate(blocks, axis=0)


def _gla_all_scores(c, p_scr, factored):
    if factored:
        _gla_scores_factored(c, _gla_factored(c), p_scr)
    else:
        lag_w, keys = _gla_lag_weights(c), _gla_pairwise_keys(c)
        for h in range(GLA_HEADS):
            p_scr[h] = _gla_scores_pairwise(c, lag_w, keys, h)


def _either_form(chunks, run):
    small = chunks[0]["small"]
    for c in chunks[1:]:
        small = jnp.logical_and(small, c["small"])
    pl.when(small)(lambda: run(True))
    pl.when(jnp.logical_not(small))(lambda: run(False))


def _gla_intra_bwd_pairwise(c, lag_w, keys, dp, h):
    sl = slice(GLA_DK * h, GLA_DK * (h + 1))
    qs_h, k_h = c["qs"][:, sl], c["k"][:, sl]
    ri, ci = c["ri"], c["ci"]
    dq_rows = [jnp.zeros((SUB, GLA_DK), F32)]
    dk = jnp.zeros((CHUNK, GLA_DK), F32)
    for a in range(1, N_SUB):
        eq = c["eq"][a][:, sl]
        qh = qs_h[a * SUB:(a + 1) * SUB, :] * eq
        dpa = jnp.where(ci[:SUB, :] < a * SUB, dp[a * SUB:(a + 1) * SUB, :], 0.0)
        dq_rows.append(_dot(dpa, keys[a][:, sl]) * eq)
        ek = jnp.exp(jnp.minimum(c["refs"][a][:, sl] - c["cum"][:, sl], 0.0))
        dk = dk + _dot_tn(dpa, qh) * ek
    dq = jnp.concatenate(dq_rows, axis=0)
    for r in range(SUB):
        w = lag_w[r][:, sl]
        dpr = jnp.sum(jnp.where(ci == ri - r, dp, 0.0), axis=1, keepdims=True)
        kr = k_h if r == 0 else pltpu.roll(k_h, r, 0)
        dq = dq + dpr * kr * w
        back = dpr * qs_h * w
        dk = dk + (back if r == 0 else pltpu.roll(back, CHUNK - r, 0))
    return dq, dk


def _gla_all_intra_bwd(c, dps, p_scr, dq_scr, dk_scr, factored):
    if factored:
        terms = _gla_factored(c)
        _gla_scores_factored(c, terms, p_scr)
        _gla_intra_bwd_factored(c, terms, dps, dq_scr, dk_scr)
    else:
        lag_w, keys = _gla_lag_weights(c), _gla_pairwise_keys(c)
        outs = [_gla_intra_bwd_pairwise(c, lag_w, keys, dps[h], h) for h in range(GLA_HEADS)]
        for h in range(GLA_HEADS):
            p_scr[h] = _gla_scores_pairwise(c, lag_w, keys, h)
        dq_scr[...] = jnp.concatenate([o[0] for o in outs], axis=1)
        dk_scr[...] = jnp.concatenate([o[1] for o in outs], axis=1)


def _mixer_fwd(proj, cos2, sin2, w2p, gb, rnw, gnw, name, carried=()):
    n_carried = len(carried)

    def body(*refs):
        p_ref, c_ref, s_ref, w2_ref, gb_ref, rnw_ref, gnw_ref = refs[:7]
        x_refs, refs = refs[7:7 + n_carried], refs[7 + n_carried:]
        ocat_ref, mrg_ref, sr_out, sg_out = refs[:4]
        gathered_refs, refs = refs[4:4 + n_carried], refs[4 + n_carried:]
        sr, sg, p_scr = refs[:3]
        n = pl.program_id(0)
        if n_carried:
            start, forward, finish = _gather_phases(x_refs, gathered_refs, *refs[3:])
            pl.when(n == 0)(start)
            pl.when(n == (3 * N_STEPS) // 4)(forward)

        @pl.when(n == 0)
        def _():
            sr[...] = jnp.zeros_like(sr)
            sg[...] = jnp.zeros_like(sg)

        sr_out[0] = sr[...]
        cosv, sinv = c_ref[...], s_ref[...]

        for h in range(RET_HEADS):
            dmat, zeta, xi, gc = _ret_consts(h)
            hs = slice(128 * h, 128 * (h + 1))
            q = _rope(p_ref[:, O_RQ + 128 * h:O_RQ + 128 * (h + 1)], cosv, sinv)
            k = _rope(p_ref[:, O_RK + 128 * h:O_RK + 128 * (h + 1)], cosv, sinv) * (RET_DK ** -0.5)
            v = p_ref[:, O_RV + 128 * h:O_RV + 128 * (h + 1)]
            g = p_ref[:, O_RG + 128 * h:O_RG + 128 * (h + 1)]
            s_in = sr[h]
            a = _dot_nt(q, k) * dmat
            o = _dot(a, v) + _dot(q, s_in) * xi
            sr[h] = gc * s_in + _dot_tn(k * zeta, v)
            mu = jnp.mean(o, axis=-1, keepdims=True)
            xc = o - mu
            nrm = xc * lax.rsqrt(jnp.mean(xc * xc, axis=-1, keepdims=True) + EPS)
            ocat_ref[:, hs] = o
            mrg_ref[:, hs] = (nrm * rnw_ref[:, hs] * (g * _sigmoid(g))).astype(BF16)

        row_slices = [slice(CHUNK * j, CHUNK * (j + 1)) for j in range(CHUNKS_PER_STEP)]
        chunks = [_gla_common(p_ref, w2_ref, gb_ref, n * CHUNKS_PER_STEP + j, rows)
                  for j, rows in enumerate(row_slices)]

        def gla_chunks(factored):
            own = _state_block_mask()
            for j, (rows, c) in enumerate(zip(row_slices, chunks)):
                s_in = sg[...]
                for h in range(GLA_HEADS):
                    sg_out[j, h] = s_in[GLA_DK * h:GLA_DK * (h + 1), GLA_DV * h:GLA_DV * (h + 1)]
                _gla_all_scores(c, p_scr.at[j], factored)
                v_all = p_ref[rows, O_GV:O_GV + GLA_HEADS * GLA_DV]
                o_inter = _dot(c["qs"] * c["ecum"], s_in)
                decay = jnp.exp(_dot_tn_exact_lhs(c["la"], jnp.ones((CHUNK, GLA_HEADS * GLA_DV), F32)))
                sg[...] = decay * s_in + jnp.where(own, _dot_tn(c["k"] * c["ekl"], v_all), 0.0)
                for h in range(GLA_HEADS):
                    hs = slice(512 + 128 * h, 512 + 128 * (h + 1))
                    v = v_all[:, GLA_DV * h:GLA_DV * (h + 1)]
                    g = p_ref[rows, O_GR + 128 * h:O_GR + 128 * (h + 1)]
                    o = _dot(p_scr[j, h], v) + o_inter[:, GLA_DV * h:GLA_DV * (h + 1)]
                    nrm = o * lax.rsqrt(jnp.mean(o * o, axis=-1, keepdims=True) + EPS)
                    ocat_ref[rows, hs] = o
                    mrg_ref[rows, hs] = (nrm * gnw_ref[:, 128 * h:128 * (h + 1)] * (g * _sigmoid(g))).astype(BF16)

        _either_form(chunks, gla_chunks)

        if n_carried:
            pl.when(n == N_STEPS - 1)(finish)

    const = lambda shape: pl.BlockSpec(shape, lambda n: (0,) * len(shape))
    anywhere = [pl.BlockSpec(memory_space=pl.ANY)] * n_carried
    return pl.pallas_call(
        body, name=name, grid=(N_STEPS,),
        in_specs=[pl.BlockSpec((STEP_ROWS, IN_WP), lambda n: (n, 0)),
                  pl.BlockSpec((STEP_ROWS, 128), lambda n: (n, 0)), pl.BlockSpec((STEP_ROWS, 128), lambda n: (n, 0)),
                  const((128, 256)), const((1, 256)), const((1, 512)), const((1, 512))] + anywhere,
        out_specs=[pl.BlockSpec((STEP_ROWS, D), lambda n: (n, 0)), pl.BlockSpec((STEP_ROWS, D), lambda n: (n, 0)),
                   pl.BlockSpec((1, RET_HEADS, RET_DK, 128), lambda n: (n, 0, 0, 0)),
                   pl.BlockSpec((CHUNKS_PER_STEP, GLA_HEADS, GLA_DK, GLA_DV), lambda n: (n, 0, 0, 0))] + anywhere,
        out_shape=[jax.ShapeDtypeStruct((LP, D), F32), jax.ShapeDtypeStruct((LP, D), BF16),
                   jax.ShapeDtypeStruct((N_STEPS, RET_HEADS, RET_DK, 128), F32),
                   jax.ShapeDtypeStruct((N_CHUNKS, GLA_HEADS, GLA_DK, GLA_DV), F32)] + _gathered_shapes(carried),
        scratch_shapes=[pltpu.VMEM((RET_HEADS, RET_DK, 128), F32),
                        pltpu.VMEM((GLA_HEADS * GLA_DK, GLA_HEADS * GLA_DV), F32),
                        pltpu.VMEM((CHUNKS_PER_STEP, GLA_HEADS, CHUNK, CHUNK), F32)] + _exchange_sems(n_carried),
        compiler_params=_cparams("arbitrary"),
    )(proj, cos2, sin2, w2p, gb, rnw, gnw, *carried)


def _mixer_bwd(proj, ocat, dmrg, sr_all, sg_all, cos2, sin2, w2p, gb, rnw, gnw, name, carried=()):
    last_step = N_STEPS - 1
    n_carried = len(carried)

    def body(*refs):
        p_ref, ocat_ref, dm_ref, sr_ref, sg_ref, c_ref, s_ref, w2_ref, gb_ref, rnw_ref, gnw_ref = refs[:11]
        g_refs, refs = refs[11:11 + n_carried], refs[11 + n_carried:]
        dp_ref, dw2_ref, dgb_ref, drn_ref, dgn_ref = refs[:5]
        got_refs, refs = refs[5:5 + n_carried], refs[5 + n_carried:]
        dsr, dsg, p_scr, dq_scr, dk_scr = refs[:5]
        step = pl.program_id(0)
        n = last_step - step
        if n_carried:
            start, finish = _exchange_phases(g_refs, got_refs, *refs[5:])
            pl.when(step == 0)(start)

        @pl.when(step == 0)
        def _():
            dsr[...] = jnp.zeros_like(dsr)
            dsg[...] = jnp.zeros_like(dsg)
            dw2_ref[...] = jnp.zeros_like(dw2_ref)
            dgb_ref[...] = jnp.zeros_like(dgb_ref)
            drn_ref[...] = jnp.zeros_like(drn_ref)
            dgn_ref[...] = jnp.zeros_like(dgn_ref)

        cosv, sinv = c_ref[...], s_ref[...]
        step_row = lax.broadcasted_iota(jnp.int32, (STEP_ROWS, 1), 0)
        real = ((n * STEP_ROWS + step_row) >= PAD_ROWS).astype(F32)

        for h in range(RET_HEADS):
            dmat, zeta, xi, gc = _ret_consts(h)
            hs = slice(128 * h, 128 * (h + 1))
            q = _rope(p_ref[:, O_RQ + 128 * h:O_RQ + 128 * (h + 1)], cosv, sinv)
            k = _rope(p_ref[:, O_RK + 128 * h:O_RK + 128 * (h + 1)], cosv, sinv) * (RET_DK ** -0.5)
            v = p_ref[:, O_RV + 128 * h:O_RV + 128 * (h + 1)]
            g = p_ref[:, O_RG + 128 * h:O_RG + 128 * (h + 1)]
            o = ocat_ref[:, hs]
            dy = dm_ref[:, hs]
            wv = rnw_ref[:, hs]
            mu = jnp.mean(o, axis=-1, keepdims=True)
            xc = o - mu
            rs = lax.rsqrt(jnp.mean(xc * xc, axis=-1, keepdims=True) + EPS)
            nrm = xc * rs
            sgm = _sigmoid(g)
            sil = g * sgm
            drn_ref[0:1, hs] += jnp.sum(dy * nrm * sil, axis=0, keepdims=True)
            dgate = dy * nrm * wv * (sgm * (1.0 + g * (1.0 - sgm)))
            dn = dy * wv * sil
            do = rs * (dn - jnp.mean(dn, axis=-1, keepdims=True) - nrm * jnp.mean(dn * nrm, axis=-1, keepdims=True))
            s_in = sr_ref[0, h]
            ds_out = dsr[h]
            a = _dot_nt(q, k) * dmat
            da = _dot_nt(do, v) * dmat
            dox = do * xi
            dq = _dot(da, k) + _dot_nt(dox, s_in)
            dk = _dot_tn(da, q) + _dot_nt(v, ds_out) * zeta
            dv = _dot_tn(a, do) + _dot(k * zeta, ds_out)
            dsr[h] = gc * ds_out + _dot_tn(q, dox)
            dk = dk * (RET_DK ** -0.5)
            dp_ref[:, O_RQ + 128 * h:O_RQ + 128 * (h + 1)] = (_unrope(dq, cosv, sinv) * real).astype(BF16)
            dp_ref[:, O_RK + 128 * h:O_RK + 128 * (h + 1)] = (_unrope(dk, cosv, sinv) * real).astype(BF16)
            dp_ref[:, O_RV + 128 * h:O_RV + 128 * (h + 1)] = (dv * real).astype(BF16)
            dp_ref[:, O_RG + 128 * h:O_RG + 128 * (h + 1)] = (dgate * real).astype(BF16)

        row_slices = [slice(CHUNK * j, CHUNK * (j + 1)) for j in range(CHUNKS_PER_STEP)]
        chunks = [_gla_common(p_ref, w2_ref, gb_ref, n * CHUNKS_PER_STEP + j, rows)
                  for j, rows in enumerate(row_slices)]

        def gla_chunks(factored):
            for j in reversed(range(CHUNKS_PER_STEP)):
                gla_chunk_bwd(chunks[j], n * CHUNKS_PER_STEP + j, row_slices[j], j, factored, p_ref, ocat_ref, dm_ref,
                              sg_ref, w2_ref, gnw_ref, dp_ref, dw2_ref, dgb_ref, dgn_ref, dsg, p_scr, dq_scr, dk_scr)

        _either_form(chunks, gla_chunks)
        if n_carried:
            pl.when(step == last_step)(finish)

    def gla_chunk_bwd(c, chunk, rows, j, factored, p_ref, ocat_ref, dm_ref, sg_ref, w2_ref, gnw_ref,
                      dp_ref, dw2_ref, dgb_ref, dgn_ref, dsg, p_scr, dq_scr, dk_scr):
        row = lax.broadcasted_iota(jnp.int32, (CHUNK, 1), 0)
        real = ((chunk * CHUNK + row) >= PAD_ROWS).astype(F32)
        ri, ci = c["ri"], c["ci"]
        causal = ri >= ci
        triu = (ci >= ri).astype(F32)
        qe = c["qs"] * c["ecum"]
        kl = c["k"] * c["ekl"]
        v_all = p_ref[rows, O_GV:O_GV + GLA_HEADS * GLA_DV]
        dos, dps = [], []
        for h in range(GLA_HEADS):
            hs = slice(512 + 128 * h, 512 + 128 * (h + 1))
            v = v_all[:, GLA_DV * h:GLA_DV * (h + 1)]
            g = p_ref[rows, O_GR + 128 * h:O_GR + 128 * (h + 1)]
            o = ocat_ref[rows, hs]
            dy = dm_ref[rows, hs]
            wv = gnw_ref[:, 128 * h:128 * (h + 1)]
            rs = lax.rsqrt(jnp.mean(o * o, axis=-1, keepdims=True) + EPS)
            nrm = o * rs
            sgm = _sigmoid(g)
            sil = g * sgm
            dgn_ref[0:1, 128 * h:128 * (h + 1)] += jnp.sum(dy * nrm * sil, axis=0, keepdims=True)
            dgate = dy * nrm * wv * (sgm * (1.0 + g * (1.0 - sgm)))
            dn = dy * wv * sil
            do = rs * (dn - nrm * jnp.mean(dn * nrm, axis=-1, keepdims=True))
            dp_ref[rows, O_GR + 128 * h:O_GR + 128 * (h + 1)] = (dgate * real).astype(BF16)
            dos.append(do)
            dps.append(jnp.where(causal, _dot_nt(do, v), 0.0))
        _gla_all_intra_bwd(c, dps, p_scr.at[j], dq_scr.at[j], dk_scr.at[j], factored)
        do_all = jnp.concatenate(dos, axis=1)
        s_in = _block_diagonal([sg_ref[j, h] for h in range(GLA_HEADS)])
        ds_out = dsg[...]
        decay = jnp.exp(_dot_tn_exact_lhs(c["la"], jnp.ones((CHUNK, GLA_HEADS * GLA_DV), F32)))
        dv_state = _dot(kl, ds_out)
        dqe = _dot_nt(do_all, s_in)
        dkl = _dot_nt(v_all, ds_out)
        dsg[...] = jnp.where(_state_block_mask(), _dot_tn(qe, do_all), 0.0) + decay * ds_out
        sd = s_in * ds_out
        sd_hi = sd.astype(BF16)
        sd_lo = (sd - sd_hi.astype(F32)).astype(BF16)
        ones8 = jnp.ones((8, GLA_HEADS * GLA_DV), BF16)
        nt = (((1,), (1,)), ((), ()))
        d_el = (lax.dot_general(ones8, sd_hi, nt, preferred_element_type=F32)
                + lax.dot_general(ones8, sd_lo, nt, preferred_element_type=F32))[0:1, :]
        dqs = dqe * c["ecum"] + dq_scr[j]
        dkk = dkl * c["ekl"] + dk_scr[j]
        d_last = jnp.sum(dkl * kl, axis=0, keepdims=True) + d_el * c["el"]
        dcum = c["qs"] * dqs - c["k"] * dkk + jnp.where(row == CHUNK - 1, d_last, 0.0)
        dla = _dot_exact_rhs(triu, dcum)
        for h in range(GLA_HEADS):
            dv = _dot_tn(p_scr[j, h], dos[h]) + dv_state[:, GLA_DV * h:GLA_DV * (h + 1)]
            dp_ref[rows, O_GV + 128 * h:O_GV + 128 * (h + 1)] = (dv * real).astype(BF16)
        dp_ref[rows, O_GQ:O_GQ + 256] = (dqs * (GLA_DK ** -0.5) * real).astype(BF16)
        dp_ref[rows, O_GK:O_GK + 256] = (dkk * real).astype(BF16)
        dz = dla * (1.0 / GLA_TAU) * _sigmoid(-c["z"]) * real
        ga = p_ref[rows, O_GA:O_GA + 128]
        dp_ref[rows, O_GA:O_GA + 128] = _dot_nt(dz, w2_ref[...]).astype(BF16)
        dp_ref[rows, O_GA + 128:IN_WP] = jnp.zeros((CHUNK, IN_WP - O_GA - 128), BF16)
        dw2_ref[...] += _dot_tn(ga, dz)
        dgb_ref[0:1, :] += jnp.sum(dz, axis=0, keepdims=True)

    const = lambda shape: pl.BlockSpec(shape, lambda s: (0,) * len(shape))
    rev = lambda s: (last_step - s, 0)
    anywhere = [pl.BlockSpec(memory_space=pl.ANY)] * n_carried
    return pl.pallas_call(
        body, name=name, grid=(N_STEPS,),
        in_specs=[pl.BlockSpec((STEP_ROWS, IN_WP), rev), pl.BlockSpec((STEP_ROWS, D), rev),
                  pl.BlockSpec((STEP_ROWS, D), rev),
                  pl.BlockSpec((1, RET_HEADS, RET_DK, 128), lambda s: (last_step - s, 0, 0, 0)),
                  pl.BlockSpec((CHUNKS_PER_STEP, GLA_HEADS, GLA_DK, GLA_DV), lambda s: (last_step - s, 0, 0, 0)),
                  pl.BlockSpec((STEP_ROWS, 128), rev), pl.BlockSpec((STEP_ROWS, 128), rev),
                  const((128, 256)), const((1, 256)), const((1, 512)), const((1, 512))] + anywhere,
        out_specs=[pl.BlockSpec((STEP_ROWS, IN_WP), rev), const((128, 256)), const((8, 256)),
                   const((8, 512)), const((8, 512))] + anywhere,
        out_shape=[jax.ShapeDtypeStruct((LP, IN_WP), BF16), jax.ShapeDtypeStruct((128, 256), F32),
                   jax.ShapeDtypeStruct((8, 256), F32), jax.ShapeDtypeStruct((8, 512), F32),
                   jax.ShapeDtypeStruct((8, 512), F32)] + [jax.ShapeDtypeStruct(g.shape, g.dtype) for g in carried],
        scratch_shapes=[pltpu.VMEM((RET_HEADS, RET_DK, 128), F32),
                        pltpu.VMEM((GLA_HEADS * GLA_DK, GLA_HEADS * GLA_DV), F32),
                        pltpu.VMEM((CHUNKS_PER_STEP, GLA_HEADS, CHUNK, CHUNK), F32),
                        pltpu.VMEM((CHUNKS_PER_STEP, CHUNK, 256), F32),
                        pltpu.VMEM((CHUNKS_PER_STEP, CHUNK, 256), F32)] + _exchange_sems(n_carried),
        compiler_params=_cparams("arbitrary"),
    )(proj, ocat, dmrg, sr_all, sg_all, cos2, sin2, w2p, gb, rnw, gnw, *carried)


def _all_gather(xs, name):
    n = len(xs)

    def body(*refs):
        start, forward, finish = _gather_phases(refs[:n], refs[n:2 * n], *refs[2 * n:])
        start()
        forward()
        finish()

    return pl.pallas_call(
        body, name=name,
        in_specs=[pl.BlockSpec(memory_space=pl.ANY)] * n,
        out_specs=[pl.BlockSpec(memory_space=pl.ANY)] * n,
        out_shape=_gathered_shapes(xs),
        scratch_shapes=_exchange_sems(n),
    )(*xs)


def _gathered_shapes(xs):
    return [jax.ShapeDtypeStruct((N_DEV,) + x.shape, x.dtype) for x in xs]


def _exchange_sems(n):
    if n == 0:
        return []
    return [pltpu.SemaphoreType.DMA((7 * n,)), pltpu.SemaphoreType.DMA((7 * n,)), pltpu.SemaphoreType.DMA((n,))]


def _gather_phases(x_refs, out_refs, send_sems, recv_sems, local_sems):
    n = len(x_refs)
    mx, my, mc = lax.axis_index("x"), lax.axis_index("y"), lax.axis_index("c")
    me, sibling = (mx, my, mc), (mx, my, 1 - mc)
    chips = [(1 - mx, my), (mx, 1 - my), (1 - mx, 1 - my)]

    def slot(a, px, py, pc):
        return out_refs[a].at[4 * px + 2 * py + pc]

    def copy(a, k, block, to, src=None):
        return pltpu.make_async_remote_copy(
            src_ref=slot(a, *block) if src is None else src, dst_ref=slot(a, *block),
            send_sem=send_sems.at[7 * a + k], recv_sem=recv_sems.at[7 * a + k],
            device_id=to, device_id_type=MESH_IDS)

    mine = [pltpu.make_async_copy(x_refs[a], slot(a, *me), local_sems.at[a]) for a in range(n)]
    first = []
    for a in range(n):
        first.append(copy(a, 0, me, sibling, src=x_refs[a]))
        first += [copy(a, 1 + j, me, (*chip, mc), src=x_refs[a]) for j, chip in enumerate(chips)]
    passed = [copy(a, 4 + j, (*chip, mc), sibling) for j, chip in enumerate(chips) for a in range(n)]

    def start():
        for cp in mine + first:
            cp.start()

    def forward():
        for j, chip in enumerate(chips):
            for a in range(n):
                copy(a, 1 + j, (*chip, mc), me).wait_recv()
                passed[j * n + a].start()

    def finish():
        for a in range(n):
            copy(a, 0, sibling, me).wait_recv()
            for j, chip in enumerate(chips):
                copy(a, 4 + j, (*chip, 1 - mc), me).wait_recv()
        for cp in first + passed:
            cp.wait_send()
        for cp in mine:
            cp.wait()

    return start, forward, finish


def _exchange_blocks(gs, name):
    n = len(gs)

    def body(*refs):
        start, finish = _exchange_phases(refs[:n], refs[n:2 * n], *refs[2 * n:])
        start()
        finish()

    return pl.pallas_call(
        body, name=name,
        in_specs=[pl.BlockSpec(memory_space=pl.ANY)] * n,
        out_specs=[pl.BlockSpec(memory_space=pl.ANY)] * n,
        out_shape=[jax.ShapeDtypeStruct(g.shape, g.dtype) for g in gs],
        scratch_shapes=_exchange_sems(n),
    )(*gs)


def _exchange_phases(g_refs, out_refs, send_sems, recv_sems, local_sems):
    n = len(g_refs)
    mx, my, mc = lax.axis_index("x"), lax.axis_index("y"), lax.axis_index("c")
    me = 4 * mx + 2 * my + mc
    mine = [pltpu.make_async_copy(g_refs[a].at[me], out_refs[a].at[me], local_sems.at[a]) for a in range(n)]
    copies = []
    for r in range(1, N_DEV):
        px, py, pc = mx ^ (r >> 2), my ^ ((r >> 1) & 1), mc ^ (r & 1)
        peer = 4 * px + 2 * py + pc
        for a in range(n):
            copies.append(pltpu.make_async_remote_copy(
                src_ref=g_refs[a].at[peer], dst_ref=out_refs[a].at[me],
                send_sem=send_sems.at[7 * a + r - 1], recv_sem=recv_sems.at[7 * a + r - 1],
                device_id=(px, py, pc), device_id_type=MESH_IDS))

    def start():
        for cp in mine + copies:
            cp.start()

    def finish():
        for cp in copies:
            cp.wait_recv()
        for cp in copies:
            cp.wait_send()
        for cp in mine:
            cp.wait()

    return start, finish


IN_SHARD = IN_W // N_DEV
IN_SHARD_P = 512
UP_SHARD = D_UP // N_DEV
UP_SHARD_P = 768
RELAYOUT_ROWS = 256


def _pieces_w_in():
    return [(k, 0, IN_SHARD * k, IN_SHARD) for k in range(N_DEV)]


def _pieces_ffn_up():
    pieces = []
    for k in range(N_DEV):
        n, end = UP_SHARD * k, UP_SHARD * (k + 1)
        while n < end:
            half, r = divmod(n, D_FF)
            blk, off = divmod(r, CONV_BLOCK)
            run = min(CONV_BLOCK - off, end - n)
            pieces.append((k, n - UP_SHARD * k, 2 * CONV_BLOCK * blk + CONV_BLOCK * half + off, run))
            n += run
    return pieces


def _assemble_block(load, spans, dst_block, rows):
    lo = 128 * dst_block
    lane = lax.broadcasted_iota(jnp.int32, (1, 128), 1)
    out = jnp.zeros((rows, 128), F32)
    for key, src_off, dst_off, length in spans:
        a, b = max(lo, dst_off), min(lo + 128, dst_off + length)
        s, s_end = src_off + (a - dst_off), src_off + (b - dst_off)
        d = a
        while s < s_end:
            e = min(s_end, 128 * (s // 128 + 1))
            blk = load(key, s // 128)
            shift = (d - s) % 128
            if shift:
                blk = pltpu.roll(blk, shift, 1)
            out = jnp.where((lane >= d - lo) & (lane < d - lo + (e - s)), blk, out)
            d += e - s
            s = e
    return out


def _shards_to_cols(shards, pieces, width, name):
    _, rows, _ = shards.shape
    tr = RELAYOUT_ROWS

    def body(s_ref, o_ref):
        load = lambda k, b: s_ref[k, :, 128 * b:128 * (b + 1)].astype(F32)
        for db in range(width // 128):
            o_ref[:, 128 * db:128 * (db + 1)] = _assemble_block(load, pieces, db, tr).astype(BF16)

    return pl.pallas_call(
        body, name=name, grid=(rows // tr,),
        in_specs=[pl.BlockSpec((N_DEV, tr, shards.shape[2]), lambda i: (0, i, 0))],
        out_specs=pl.BlockSpec((tr, width), lambda i: (i, 0)),
        out_shape=jax.ShapeDtypeStruct((rows, width), BF16),
        compiler_params=_cparams("parallel"),
    )(shards)


def _cols_to_shards(full, pieces, shard_width, name):
    rows, width = full.shape
    tr = RELAYOUT_ROWS

    def body(f_ref, o_ref):
        load = lambda _, b: f_ref[:, 128 * b:128 * (b + 1)].astype(F32)
        for k in range(N_DEV):
            spans = [(None, dst_off, src_off, length) for dev, src_off, dst_off, length in pieces if dev == k]
            for db in range(shard_width // 128):
                o_ref[k, :, 128 * db:128 * (db + 1)] = _assemble_block(load, spans, db, tr).astype(BF16)

    return pl.pallas_call(
        body, name=name, grid=(rows // tr,),
        in_specs=[pl.BlockSpec((tr, width), lambda i: (i, 0))],
        out_specs=pl.BlockSpec((N_DEV, tr, shard_width), lambda i: (0, i, 0)),
        out_shape=jax.ShapeDtypeStruct((N_DEV, rows, shard_width), BF16),
        compiler_params=_cparams("parallel"),
    )(full)


def _adamw(parts, w, m, v, rows_per_step, name):
    rows, cols = w.shape
    assert rows % rows_per_step == 0 and parts.shape == (N_DEV, rows, cols)

    def body(p_ref, w_ref, m_ref, v_ref, g_ref, d_ref, nm_ref, nv_ref):
        g = p_ref[0].astype(F32)
        for j in range(1, N_DEV):
            g = g + p_ref[j].astype(F32)
        m_new = ADAM_B1 * m_ref[...] + (1.0 - ADAM_B1) * g
        v_new = ADAM_B2 * v_ref[...] + (1.0 - ADAM_B2) * (g * g)
        m_hat = m_new / (1.0 - ADAM_B1 ** ADAM_STEP)
        v_hat = v_new / (1.0 - ADAM_B2 ** ADAM_STEP)
        g_ref[...] = g
        d_ref[...] = -ADAM_LR * (m_hat / (jnp.sqrt(v_hat) + ADAM_EPS) + ADAM_WD * w_ref[...])
        nm_ref[...] = m_new
        nv_ref[...] = v_new

    tile = pl.BlockSpec((rows_per_step, cols), lambda i: (i, 0))
    shape = jax.ShapeDtypeStruct((rows, cols), F32)
    return pl.pallas_call(
        body, name=name, grid=(rows // rows_per_step,),
        in_specs=[pl.BlockSpec((N_DEV, rows_per_step, cols), lambda i: (0, i, 0)), tile, tile, tile],
        out_specs=[tile, tile, tile, tile],
        out_shape=[shape, shape, shape, shape],
        compiler_params=_cparams("parallel"),
    )(parts, w, m, v)


BIG = (("w_in", (DEPTH, D, IN_W // N_DEV), 2), ("w_out", (DEPTH, D // N_DEV, D), 1),
       ("ffn_up", (DEPTH, D, D_UP // N_DEV), 2), ("ffn_down", (DEPTH, D_FF // N_DEV, D), 1))
SMALL = (("meta_tokens", (N_META, D // N_DEV), 1), ("gla_gate_w2", (DEPTH, GATE_RANK, 256 // N_DEV), 2),
         ("ffn_conv_w", (DEPTH, 3, D_UP // N_DEV), 2))
REPL = (("pre_mix_norm", (DEPTH, D)), ("gla_gate_b", (DEPTH, 256)), ("ret_norm_w", (DEPTH, 512)),
        ("gla_norm_w", (DEPTH, 512)), ("post_mix_norm", (DEPTH, D)), ("pre_ffn_norm", (DEPTH, D)),
        ("ffn_conv_b", (DEPTH, D_UP)), ("post_ffn_norm", (DEPTH, D)))
WEIGHT_ORDER = ("meta_tokens", "pre_mix_norm", "w_in", "gla_gate_w2", "gla_gate_b", "ret_norm_w", "gla_norm_w",
                "w_out", "post_mix_norm", "pre_ffn_norm", "ffn_up", "ffn_conv_w", "ffn_conv_b", "ffn_down",
                "post_ffn_norm")


def _size(shape):
    return math.prod(shape)


def _round_up(n, mult):
    return -(-n // mult) * mult


REPL_ROWS = _round_up(-(-sum(_size(s) for _, s in REPL) // LANES), 8)
SMALL_ROWS = _round_up(-(-sum(_size(s) for _, s, _ in SMALL) // LANES), 8)


def _pack(arrays, rows, dtype):
    flat = jnp.concatenate([a.reshape(-1).astype(dtype) for a in arrays])
    return jnp.pad(flat, (0, rows * LANES - flat.shape[0])).reshape(rows, LANES)


def _unpack(buf, shapes):
    flat = buf.reshape(-1)
    out, off = [], 0
    for shape in shapes:
        out.append(flat[off:off + _size(shape)].reshape(shape))
        off += _size(shape)
    return out


def _unshard(blocks, axis):
    moved = jnp.moveaxis(blocks, 0, axis)
    shape = list(moved.shape)
    shape[axis:axis + 2] = [shape[axis] * shape[axis + 1]]
    return moved.reshape(shape)


def _to_blocks(full, axis):
    shape = list(full.shape)
    shape[axis:axis + 1] = [N_DEV, shape[axis] // N_DEV]
    return jnp.moveaxis(full.reshape(shape), axis, 0)


def _interleave_cols(w):
    lead = w.shape[:-1]
    return jnp.swapaxes(w.reshape(lead + (2, N_CONV_BLOCKS, CONV_BLOCK)), -3, -2).reshape(lead + (D_UP,))


def _deinterleave_cols(w):
    lead = w.shape[:-1]
    return jnp.swapaxes(w.reshape(lead + (N_CONV_BLOCKS, 2, CONV_BLOCK)), -3, -2).reshape(lead + (D_UP,))


def _rope_tables():
    half = RET_DK // 2
    inv = ROPE_BASE ** (-jnp.arange(half, dtype=F32) / half)
    pos = jnp.arange(LP, dtype=F32) - float(PAD_ROWS)
    ang = pos[:, None] * inv[None, :]
    c, s = jnp.cos(ang), jnp.sin(ang)
    return jnp.concatenate([c, c], axis=1), jnp.concatenate([-s, s], axis=1)


def kernel(x, meta_tokens, pre_mix_norm, w_in, gla_gate_w2, gla_gate_b, ret_norm_w, gla_norm_w, w_out, post_mix_norm, pre_ffn_norm, ffn_up, ffn_conv_w, ffn_conv_b, ffn_down, post_ffn_norm, loss_target, m_meta_tokens, m_pre_mix_norm, m_w_in, m_gla_gate_w2, m_gla_gate_b, m_ret_norm_w, m_gla_norm_w, m_w_out, m_post_mix_norm, m_pre_ffn_norm, m_ffn_up, m_ffn_conv_w, m_ffn_conv_b, m_ffn_down, m_post_ffn_norm, v_meta_tokens, v_pre_mix_norm, v_w_in, v_gla_gate_w2, v_gla_gate_b, v_ret_norm_w, v_gla_norm_w, v_w_out, v_post_mix_norm, v_pre_ffn_norm, v_ffn_up, v_ffn_conv_w, v_ffn_conv_b, v_ffn_down, v_post_ffn_norm):
    weights = dict(meta_tokens=meta_tokens, pre_mix_norm=pre_mix_norm, w_in=w_in, gla_gate_w2=gla_gate_w2,
                   gla_gate_b=gla_gate_b, ret_norm_w=ret_norm_w, gla_norm_w=gla_norm_w, w_out=w_out,
                   post_mix_norm=post_mix_norm, pre_ffn_norm=pre_ffn_norm, ffn_up=ffn_up, ffn_conv_w=ffn_conv_w,
                   ffn_conv_b=ffn_conv_b, ffn_down=ffn_down, post_ffn_norm=post_ffn_norm)
    mom1 = dict(meta_tokens=m_meta_tokens, pre_mix_norm=m_pre_mix_norm, w_in=m_w_in, gla_gate_w2=m_gla_gate_w2,
                gla_gate_b=m_gla_gate_b, ret_norm_w=m_ret_norm_w, gla_norm_w=m_gla_norm_w, w_out=m_w_out,
                post_mix_norm=m_post_mix_norm, pre_ffn_norm=m_pre_ffn_norm, ffn_up=m_ffn_up,
                ffn_conv_w=m_ffn_conv_w, ffn_conv_b=m_ffn_conv_b, ffn_down=m_ffn_down, post_ffn_norm=m_post_ffn_norm)
    mom2 = dict(meta_tokens=v_meta_tokens, pre_mix_norm=v_pre_mix_norm, w_in=v_w_in, gla_gate_w2=v_gla_gate_w2,
                gla_gate_b=v_gla_gate_b, ret_norm_w=v_ret_norm_w, gla_norm_w=v_gla_norm_w, w_out=v_w_out,
                post_mix_norm=v_post_mix_norm, pre_ffn_norm=v_pre_ffn_norm, ffn_up=v_ffn_up,
                ffn_conv_w=v_ffn_conv_w, ffn_conv_b=v_ffn_conv_b, ffn_down=v_ffn_down, post_ffn_norm=v_post_ffn_norm)

    pad_cols = lambda a, width: jnp.pad(a, ((0, 0), (0, width - a.shape[1])))
    big_names = [n for n, _, _ in BIG]
    shard = {}
    for l in range(DEPTH):
        shard[l, "w_in"] = pad_cols(w_in[l].astype(BF16), IN_SHARD_P)
        shard[l, "w_out"] = w_out[l].astype(BF16)
        shard[l, "ffn_up"] = pad_cols(ffn_up[l].astype(BF16), UP_SHARD_P)
        shard[l, "ffn_down"] = ffn_down[l].astype(BF16)
    gathered = {(0, "w_in"): _all_gather([shard[0, "w_in"]], "gather_w_in_0")[0]}
    gather_in_mixer = {l: [(l, n) for n in big_names[1:]] + ([(l + 1, "w_in")] if l + 1 < DEPTH else [])
                       for l in range(DEPTH)}
    small = _all_gather([_pack([weights[n] for n, _, _ in SMALL], SMALL_ROWS, F32)], "gather_small_weights")[0]
    small_parts = _unpack_blocks(small, [s for _, s, _ in SMALL])
    full = {n: _unshard(p, ax) for (n, _, ax), p in zip(SMALL, small_parts)}
    w2p = jnp.pad(full["gla_gate_w2"], ((0, 0), (0, 128 - GATE_RANK), (0, 0)))
    cw8 = jnp.concatenate([_interleave_cols(full["ffn_conv_w"]), _interleave_cols(ffn_conv_b)[:, None, :],
                           jnp.zeros((DEPTH, 4, D_UP), F32)], axis=1)
    cos2, sin2 = _rope_tables()

    h = jnp.concatenate([jnp.zeros((PAD_ROWS, D), F32), full["meta_tokens"], x[0]], axis=0)
    target = jnp.concatenate([jnp.zeros((CHUNK, D), F32), loss_target[0]], axis=0)
    saved, layer_w = [], []
    for l in range(DEPTH):
        lw = dict(w_in=_shards_to_cols(gathered[l, "w_in"], _pieces_w_in(), IN_WP, f"w_in_cols_{l}"))
        a1 = _rmsnorm_fwd(h, pre_mix_norm[l:l + 1], f"pre_mix_norm_{l}")
        proj = _matmul(a1, lw["w_in"], out_dtype=F32, tm=TM, tn=1280, tk=D, name=f"in_proj_{l}", n_outer=True)
        keys = gather_in_mixer.get(l, [])
        ocat, merged, sr_all, sg_all, *got = _mixer_fwd(proj, cos2, sin2, w2p[l], gla_gate_b[l:l + 1],
                                                        ret_norm_w[l:l + 1], gla_norm_w[l:l + 1], f"mixer_fwd_{l}",
                                                        carried=[shard[key] for key in keys])
        gathered.update(zip(keys, got))
        lw["w_out"] = gathered[l, "w_out"].reshape(D, D)
        lw["w_up"] = _shards_to_cols(gathered[l, "ffn_up"], _pieces_ffn_up(), D_UP, f"ffn_up_cols_{l}")
        lw["w_down"] = gathered[l, "ffn_down"].reshape(D_FF, D)
        layer_w.append(lw)
        m, h1 = _matmul_resid_norm(merged, lw["w_out"], h, post_mix_norm[l:l + 1], f"out_proj_{l}")
        a2 = _rmsnorm_fwd(h1, pre_ffn_norm[l:l + 1], f"pre_ffn_norm_{l}")
        u = _matmul(a2, lw["w_up"], out_dtype=BF16, tm=TM, tn=1408, tk=D, name=f"ffn_up_{l}", n_outer=True)
        cv, act = _conv_act_fwd(u, cw8[l], f"ffn_conv_act_{l}")
        f, h2 = _matmul_resid_norm(act, lw["w_down"], h1, post_ffn_norm[l:l + 1], f"ffn_down_{l}")
        saved.append(dict(h=h, a1=a1, proj=proj, ocat=ocat, merged=merged, sr=sr_all, sg=sg_all, m=m, h1=h1,
                          a2=a2, u=u, cv=cv, act=act, f=f))
        h = h2

    dh, loss_acc = _loss_head(h, target, "loss_head")
    loss = lax.psum(loss_acc[0, 0], ("x", "y", "c"))

    kinds = ("grad", "delta", "new_m", "new_v")
    grads = {n: [None] * DEPTH for n in WEIGHT_ORDER if n != "meta_tokens" and n not in big_names}
    pending, parts = [], {}
    for l in reversed(range(DEPTH)):
        s, lw = saved[l], layer_w[l]
        dact, df, g_post_ffn = _norm_bwd_matmul(dh, s["f"], post_ffn_norm[l:l + 1], lw["w_down"], BF16,
                                                f"ffn_down_dx_{l}")
        g_down = _matmul(s["act"], df, ta=True, out_dtype=BF16, tm=D_FF // 2, tn=D, tk=TK_ROWS, name=f"ffn_down_dw_{l}")
        du, dcw = _conv_act_bwd(dact, s["cv"], s["u"], cw8[l], f"ffn_conv_act_bwd_{l}")
        dh1, g_pre_ffn = _matmul_norm_bwd(du, lw["w_up"], s["h1"], pre_ffn_norm[l:l + 1], dh, 1408, f"ffn_up_dx_{l}")
        g_up = _matmul(s["a2"], du, ta=True, out_dtype=BF16, tm=D, tn=1408, tk=TK_ROWS, name=f"ffn_up_dw_{l}")
        dmerged, dm, g_post_mix = _norm_bwd_matmul(dh1, s["m"], post_mix_norm[l:l + 1], lw["w_out"], F32,
                                                   f"out_proj_dx_{l}")
        g_out = _matmul(s["merged"], dm, ta=True, out_dtype=BF16, tm=D, tn=D, tk=TK_ROWS, name=f"out_proj_dw_{l}")
        pending += [((l, "ffn_down"), g_down.reshape(N_DEV, D_FF // N_DEV, D)),
                    ((l, "ffn_up"), _cols_to_shards(g_up, _pieces_ffn_up(), UP_SHARD_P, f"ffn_up_grad_shards_{l}")),
                    ((l, "w_out"), g_out.reshape(N_DEV, D // N_DEV, D))]
        dproj, g_w2, g_gb, g_rn, g_gn, *got = _mixer_bwd(s["proj"], s["ocat"], dmerged, s["sr"], s["sg"], cos2, sin2,
                                                         w2p[l], gla_gate_b[l:l + 1], ret_norm_w[l:l + 1],
                                                         gla_norm_w[l:l + 1], f"mixer_bwd_{l}",
                                                         carried=[blocks for _, blocks in pending])
        parts.update(zip([key for key, _ in pending], got))
        g_in = _matmul(s["a1"], dproj, ta=True, out_dtype=BF16, tm=D, tn=1280, tk=TK_ROWS, name=f"in_proj_dw_{l}")
        pending = [((l, "w_in"), _cols_to_shards(g_in, _pieces_w_in(), IN_SHARD_P, f"w_in_grad_shards_{l}"))]
        now = pending if l == 0 else []
        dh, g_pre_mix, *got = _matmul_norm_bwd(dproj, lw["w_in"], s["h"], pre_mix_norm[l:l + 1], dh1, IN_WP,
                                               f"in_proj_dx_{l}", carried=[blocks for _, blocks in now])
        parts.update(zip([key for key, _ in now], got))
        pending = [] if l == 0 else pending
        grads["post_ffn_norm"][l] = g_post_ffn[0]
        grads["ffn_conv_w"][l] = _deinterleave_cols(dcw[0:3])
        grads["ffn_conv_b"][l] = _deinterleave_cols(dcw[3])
        grads["pre_ffn_norm"][l] = g_pre_ffn[0]
        grads["post_mix_norm"][l] = g_post_mix[0]
        grads["gla_gate_w2"][l] = g_w2[:GATE_RANK]
        grads["gla_gate_b"][l] = g_gb[0]
        grads["ret_norm_w"][l] = g_rn[0]
        grads["gla_norm_w"][l] = g_gn[0]
        grads["pre_mix_norm"][l] = g_pre_mix[0]
    local = {n: jnp.stack(v) for n, v in grads.items()}
    local["meta_tokens"] = dh[PAD_ROWS:CHUNK]
    grad_x = dh[CHUNK:][None]

    blocks = jnp.concatenate([_to_blocks(local[n], ax).reshape(N_DEV, -1) for n, _, ax in SMALL], axis=1)
    blocks = jnp.pad(blocks, ((0, 0), (0, SMALL_ROWS * LANES - blocks.shape[1]))).reshape(N_DEV, SMALL_ROWS, LANES)
    *got, small_grad_parts = _exchange_blocks([b for _, b in pending] + [blocks], "exchange_last_grads")
    parts.update(zip([key for key, _ in pending], got))

    widths = dict(w_in=IN_SHARD_P, w_out=D, ffn_up=UP_SHARD_P, ffn_down=D)
    steps = dict(w_in=256, w_out=D // N_DEV, ffn_up=256, ffn_down=D_FF // N_DEV // 2)
    big_out = {kind: {n: [None] * DEPTH for n in big_names} for kind in kinds}
    for l in range(DEPTH):
        for n in big_names:
            mine = [pad_cols(d[n][l], widths[n]) for d in (weights, mom1, mom2)]
            results = _adamw(parts[l, n], *mine, steps[n], f"adamw_{n}_{l}")
            for kind, r in zip(kinds, results):
                big_out[kind][n][l] = r[:, :weights[n].shape[2]]
    out = {kind: {n: jnp.stack(v) for n, v in big_out[kind].items()} for kind in kinds}
    shard_shapes = [s for _, s, _ in SMALL]
    packed = [_pack([d[n] for n, _, _ in SMALL], SMALL_ROWS, F32) for d in (weights, mom1, mom2)]
    results = _adamw(small_grad_parts, *packed, SMALL_ROWS, "adamw_small_sharded")
    for kind, buf in zip(kinds, results):
        out[kind].update(zip([n for n, _, _ in SMALL], _unpack(buf, shard_shapes)))

    repl_parts = _all_gather([_pack([local[n] for n, _ in REPL], REPL_ROWS, F32)], "gather_small_grads")[0]
    packed = [_pack([d[n] for n, _ in REPL], REPL_ROWS, F32) for d in (weights, mom1, mom2)]
    results = _adamw(repl_parts, *packed, REPL_ROWS, "adamw_replicated")
    repl_shapes = [s for _, s in REPL]
    for kind, buf in zip(kinds, results):
        out[kind].update(zip([n for n, _ in REPL], _unpack(buf, repl_shapes)))

    return (loss, grad_x, *[out["grad"][n] for n in WEIGHT_ORDER], *[out["delta"][n] for n in WEIGHT_ORDER],
            *[out["new_m"][n] for n in WEIGHT_ORDER], *[out["new_v"][n] for n in WEIGHT_ORDER])


def _unpack_blocks(gathered, shapes):
    flat = gathered.reshape(N_DEV, -1)
    out, off = [], 0
    for shape in shapes:
        out.append(flat[:, off:off + _size(shape)].reshape((N_DEV,) + shape))
        off += _size(shape)
    return out
```

```python
import math

import jax
import jax.numpy as jnp
from jax import lax
from jax.experimental import pallas as pl
from jax.experimental.pallas import tpu as pltpu

F32 = jnp.float32
BF16 = jnp.bfloat16

D = 1024
SEQ = 8192
DEPTH = 2
N_META = 16
CHUNK = 64
SUB = 16
N_SUB = CHUNK // SUB
PAD_ROWS = CHUNK - N_META
LP = SEQ + CHUNK
N_CHUNKS = LP // CHUNK
RET_HEADS = 4
RET_DK = 128
GLA_HEADS = 4
GLA_DK = 64
GLA_DV = 128
GLA_TAU = 16.0
GATE_RANK = 16
IN_W = 3600
IN_WP = 3840
D_FF = 2816
D_UP = 2 * D_FF
CONV_BLOCK = 256
N_CONV_BLOCKS = D_FF // CONV_BLOCK
ROPE_BASE = 10000.0
EPS = 1e-6
N_DEV = 8
LANES = 1024

O_RQ, O_RK, O_RV, O_RG = 0, 512, 1024, 1536
O_GQ, O_GK, O_GV, O_GR, O_GA = 2048, 2304, 2560, 3072, 3584

ADAM_LR = 0.001
ADAM_B1 = 0.9
ADAM_B2 = 0.999
ADAM_EPS = 1e-08
ADAM_WD = 0.01
ADAM_STEP = 10

VMEM_LIMIT = 56 * 1024 * 1024
MESH_IDS = pl.DeviceIdType.MESH


def _row_tile(rows, limit):
    best = 16
    for t in range(16, min(rows, limit) + 1, 16):
        if rows % t == 0:
            best = t
    return best


TM = _row_tile(LP, 688)
TK_ROWS = _row_tile(LP, 1376)


def _cparams(*sem):
    return pltpu.CompilerParams(dimension_semantics=sem, vmem_limit_bytes=VMEM_LIMIT)


def _dot(a, b):
    return jnp.dot(a.astype(BF16), b.astype(BF16), preferred_element_type=F32)


def _dot_nt(a, b):
    return lax.dot_general(a.astype(BF16), b.astype(BF16), (((1,), (1,)), ((), ())), preferred_element_type=F32)


def _dot_tn(a, b):
    return lax.dot_general(a.astype(BF16), b.astype(BF16), (((0,), (0,)), ((), ())), preferred_element_type=F32)


def _split3(x):
    hi = x.astype(BF16)
    r1 = x - hi.astype(F32)
    mid = r1.astype(BF16)
    lo = (r1 - mid.astype(F32)).astype(BF16)
    return hi, mid, lo


def _dot_exact_rhs(t, x):
    hi, mid, lo = _split3(x)
    t = t.astype(BF16)
    return (jnp.dot(t, hi, preferred_element_type=F32) + jnp.dot(t, mid, preferred_element_type=F32)
            + jnp.dot(t, lo, preferred_element_type=F32))


def _dot_tn_exact_lhs(x, ones):
    dims = (((0,), (0,)), ((), ()))
    hi, mid, lo = _split3(x)
    ones = ones.astype(BF16)
    return (lax.dot_general(hi, ones, dims, preferred_element_type=F32)
            + lax.dot_general(mid, ones, dims, preferred_element_type=F32)
            + lax.dot_general(lo, ones, dims, preferred_element_type=F32))


def _sigmoid(x):
    return 1.0 / (1.0 + jnp.exp(-x))


def _matmul(a, b, *, ta=False, tb=False, out_dtype, tm, tn, tk, name, n_outer=False):
    m = a.shape[1] if ta else a.shape[0]
    k = a.shape[0] if ta else a.shape[1]
    n = b.shape[0] if tb else b.shape[1]
    assert (b.shape[1] if tb else b.shape[0]) == k
    assert m % tm == 0 and n % tn == 0 and k % tk == 0, (name, m, n, k, tm, tn, tk)
    nk = k // tk
    order = (lambda f: (lambda j, i, kk: f(i, j, kk))) if n_outer else (lambda f: f)
    a_spec = (pl.BlockSpec((tk, tm), order(lambda i, j, kk: (kk, i))) if ta
              else pl.BlockSpec((tm, tk), order(lambda i, j, kk: (i, kk))))
    b_spec = (pl.BlockSpec((tn, tk), order(lambda i, j, kk: (j, kk))) if tb
              else pl.BlockSpec((tk, tn), order(lambda i, j, kk: (kk, j))))
    dims = (((0 if ta else 1,), (1 if tb else 0,)), ((), ()))

    def body(a_ref, b_ref, o_ref, *acc):
        prod = lax.dot_general(a_ref[...].astype(BF16), b_ref[...].astype(BF16), dims, preferred_element_type=F32)
        if nk == 1:
            o_ref[...] = prod.astype(out_dtype)
            return
        acc_ref, = acc
        kk = pl.program_id(2)

        @pl.when(kk == 0)
        def _():
            acc_ref[...] = prod

        @pl.when(kk > 0)
        def _():
            acc_ref[...] += prod

        @pl.when(kk == nk - 1)
        def _():
            o_ref[...] = acc_ref[...].astype(out_dtype)

    return pl.pallas_call(
        body, name=name, grid=(n // tn, m // tm, nk) if n_outer else (m // tm, n // tn, nk),
        in_specs=[a_spec, b_spec],
        out_specs=pl.BlockSpec((tm, tn), order(lambda i, j, kk: (i, j))),
        out_shape=jax.ShapeDtypeStruct((m, n), out_dtype),
        scratch_shapes=[pltpu.VMEM((tm, tn), F32)] if nk > 1 else [],
        compiler_params=_cparams("parallel", "parallel", "arbitrary"),
    )(a, b)


def _matmul_resid_norm(a, b, h, w, name):
    k = a.shape[1]

    def body(a_ref, b_ref, h_ref, w_ref, m_ref, o_ref):
        m = jnp.dot(a_ref[...].astype(BF16), b_ref[...].astype(BF16), preferred_element_type=F32)
        m_ref[...] = m
        r = lax.rsqrt(jnp.mean(m * m, axis=-1, keepdims=True) + EPS)
        row = pl.program_id(0) * TM + lax.broadcasted_iota(jnp.int32, (TM, 1), 0)
        o_ref[...] = h_ref[...] + jnp.where(row >= PAD_ROWS, m * r * w_ref[...], 0.0)

    tile = pl.BlockSpec((TM, D), lambda i: (i, 0))
    return pl.pallas_call(
        body, name=name, grid=(LP // TM,),
        in_specs=[pl.BlockSpec((TM, k), lambda i: (i, 0)), pl.BlockSpec((k, D), lambda i: (0, 0)), tile,
                  pl.BlockSpec((1, D), lambda i: (0, 0))],
        out_specs=[tile, tile],
        out_shape=[jax.ShapeDtypeStruct((LP, D), F32), jax.ShapeDtypeStruct((LP, D), F32)],
        compiler_params=_cparams("parallel"),
    )(a, b, h, w)


def _rmsnorm_bwd_rows(dy, x, w):
    r = lax.rsqrt(jnp.mean(x * x, axis=-1, keepdims=True) + EPS)
    g = dy * w
    dx = r * g - x * (r * r * r * jnp.mean(g * x, axis=-1, keepdims=True))
    return dx, jnp.sum(dy * x * r, axis=0, keepdims=True)


def _matmul_norm_bwd(dz, b, x, w, resid, tk, name, carried=()):
    k = dz.shape[1]
    assert k % tk == 0
    nk = k // tk
    n_rows = LP // TM
    n_carried = len(carried)

    def body(*refs):
        a_ref, b_ref, x_ref, w_ref, r_ref = refs[:5]
        g_refs, refs = refs[5:5 + n_carried], refs[5 + n_carried:]
        dx_ref, dw_ref = refs[:2]
        got_refs, refs = refs[2:2 + n_carried], refs[2 + n_carried:]
        acc, sems = (refs[:1], refs[1:]) if nk > 1 else ((), refs)
        i, kk = pl.program_id(0), pl.program_id(1)
        if n_carried:
            exchange_start, exchange_finish = _exchange_phases(g_refs, got_refs, *sems)
            pl.when((i == 0) & (kk == 0))(exchange_start)

        @pl.when((i == 0) & (kk == 0))
        def _():
            dw_ref[...] = jnp.zeros_like(dw_ref)

        prod = lax.dot_general(a_ref[...].astype(BF16), b_ref[...].astype(BF16), (((1,), (1,)), ((), ())),
                               preferred_element_type=F32)

        def finish(dy):
            dx, dw = _rmsnorm_bwd_rows(dy, x_ref[...], w_ref[...])
            dx_ref[...] = dx + r_ref[...]
            dw_ref[0:1, :] += dw

        if nk == 1:
            finish(prod)
        else:
            acc_ref, = acc

            @pl.when(kk == 0)
            def _():
                acc_ref[...] = prod

            @pl.when((kk > 0) & (kk < nk - 1))
            def _():
                acc_ref[...] += prod

            @pl.when(kk == nk - 1)
            def _():
                finish(acc_ref[...] + prod)

        if n_carried:
            pl.when((i == n_rows - 1) & (kk == nk - 1))(exchange_finish)

    tile = pl.BlockSpec((TM, D), lambda i, kk: (i, 0))
    anywhere = [pl.BlockSpec(memory_space=pl.ANY)] * n_carried
    return pl.pallas_call(
        body, name=name, grid=(n_rows, nk),
        in_specs=[pl.BlockSpec((TM, tk), lambda i, kk: (i, kk)), pl.BlockSpec((D, tk), lambda i, kk: (0, kk)), tile,
                  pl.BlockSpec((1, D), lambda i, kk: (0, 0)), tile] + anywhere,
        out_specs=[tile, pl.BlockSpec((8, D), lambda i, kk: (0, 0))] + anywhere,
        out_shape=[jax.ShapeDtypeStruct((LP, D), F32), jax.ShapeDtypeStruct((8, D), F32)]
        + [jax.ShapeDtypeStruct(g.shape, g.dtype) for g in carried],
        scratch_shapes=([pltpu.VMEM((TM, D), F32)] if nk > 1 else []) + _exchange_sems(n_carried),
        compiler_params=_cparams("arbitrary", "arbitrary"),
    )(dz, b, x, w, resid, *carried)


def _norm_bwd_matmul(dh, x, w, b, out_dtype, name):
    n = b.shape[0]

    def body(dh_ref, x_ref, w_ref, b_ref, o_ref, dx_ref, dw_ref):
        i = pl.program_id(0)

        @pl.when(i == 0)
        def _():
            dw_ref[...] = jnp.zeros_like(dw_ref)

        row = i * TM + lax.broadcasted_iota(jnp.int32, (TM, 1), 0)
        dy = jnp.where(row >= PAD_ROWS, dh_ref[...], 0.0)
        dx, dw = _rmsnorm_bwd_rows(dy, x_ref[...], w_ref[...])
        dxb = dx.astype(BF16)
        dx_ref[...] = dxb
        dw_ref[0:1, :] += dw
        o_ref[...] = lax.dot_general(dxb, b_ref[...].astype(BF16), (((1,), (1,)), ((), ())),
                                     preferred_element_type=F32).astype(out_dtype)

    tile = pl.BlockSpec((TM, D), lambda i: (i, 0))
    return pl.pallas_call(
        body, name=name, grid=(LP // TM,),
        in_specs=[tile, tile, pl.BlockSpec((1, D), lambda i: (0, 0)), pl.BlockSpec((n, D), lambda i: (0, 0))],
        out_specs=[pl.BlockSpec((TM, n), lambda i: (i, 0)), tile, pl.BlockSpec((8, D), lambda i: (0, 0))],
        out_shape=[jax.ShapeDtypeStruct((LP, n), out_dtype), jax.ShapeDtypeStruct((LP, D), BF16),
                   jax.ShapeDtypeStruct((8, D), F32)],
        compiler_params=_cparams("arbitrary"),
    )(dh, x, w, b)


def _rmsnorm_fwd(x, w, name):
    def body(x_ref, w_ref, o_ref):
        xv = x_ref[...]
        r = lax.rsqrt(jnp.mean(xv * xv, axis=-1, keepdims=True) + EPS)
        o_ref[...] = (xv * r * w_ref[...]).astype(BF16)

    return pl.pallas_call(
        body, name=name, grid=(LP // TM,),
        in_specs=[pl.BlockSpec((TM, D), lambda i: (i, 0)), pl.BlockSpec((1, D), lambda i: (0, 0))],
        out_specs=pl.BlockSpec((TM, D), lambda i: (i, 0)),
        out_shape=jax.ShapeDtypeStruct((LP, D), BF16),
        compiler_params=_cparams("parallel"),
    )(x, w)


def _loss_head(y, target, name):
    def body(y_ref, t_ref, dy_ref, loss_ref):
        i = pl.program_id(0)

        @pl.when(i == 0)
        def _():
            loss_ref[...] = jnp.zeros_like(loss_ref)

        row = i * TM + lax.broadcasted_iota(jnp.int32, (TM, 1), 0)
        diff = jnp.where(row >= CHUNK, y_ref[...] - t_ref[...], 0.0)
        dy_ref[...] = diff * (1.0 / D)
        loss_ref[...] += (0.5 / D) * jnp.sum(diff * diff)

    tile = pl.BlockSpec((TM, D), lambda i: (i, 0))
    return pl.pallas_call(
        body, name=name, grid=(LP // TM,),
        in_specs=[tile, tile],
        out_specs=[tile, pl.BlockSpec((8, 128), lambda i: (0, 0))],
        out_shape=[jax.ShapeDtypeStruct((LP, D), F32), jax.ShapeDtypeStruct((8, 128), F32)],
        compiler_params=_cparams("arbitrary"),
    )(y, target)


GELU_C = math.sqrt(2.0 / math.pi)
GELU_K = 0.044715
STRIP = 16


def _shift_down(x, prev8, rows):
    row = lax.broadcasted_iota(jnp.int32, (rows, 1), 0)
    p1 = pltpu.roll(prev8, 1, 0)
    p2 = pltpu.roll(prev8, 2, 0)
    x1 = jnp.where(row == 0, p1[0:1, :], pltpu.roll(x, 1, 0))
    x2 = jnp.where(row == 0, p2[0:1, :], jnp.where(row == 1, p2[1:2, :], pltpu.roll(x, 2, 0)))
    return x1, x2


def _conv_act_fwd(u, cw8, name):
    n_rows = LP // TM
    cb2 = 2 * CONV_BLOCK

    def body(u_ref, cw_ref, conv_ref, act_ref, carry_ref):
        i = pl.program_id(1)

        @pl.when(i == 0)
        def _():
            carry_ref[...] = jnp.zeros_like(carry_ref)

        x = u_ref[...].astype(F32)
        x1, x2 = _shift_down(x, carry_ref[...], TM)
        conv = cw_ref[3:4, :] + x2 * cw_ref[0:1, :] + x1 * cw_ref[1:2, :] + x * cw_ref[2:3, :]
        conv_ref[...] = conv.astype(BF16)
        a = conv[:, :CONV_BLOCK]
        g = conv[:, CONV_BLOCK:]
        t = jnp.tanh(GELU_C * (a + GELU_K * a * a * a))
        act_ref[...] = (0.5 * a * (1.0 + t) * g).astype(BF16)
        carry_ref[...] = x[TM - 8:TM, :]

    return pl.pallas_call(
        body, name=name, grid=(N_CONV_BLOCKS, n_rows),
        in_specs=[pl.BlockSpec((TM, cb2), lambda j, i: (i, j)), pl.BlockSpec((8, cb2), lambda j, i: (0, j))],
        out_specs=[pl.BlockSpec((TM, cb2), lambda j, i: (i, j)), pl.BlockSpec((TM, CONV_BLOCK), lambda j, i: (i, j))],
        out_shape=[jax.ShapeDtypeStruct((LP, D_UP), BF16), jax.ShapeDtypeStruct((LP, D_FF), BF16)],
        scratch_shapes=[pltpu.VMEM((8, cb2), F32)],
        compiler_params=_cparams("arbitrary", "arbitrary"),
    )(u, cw8)


def _conv_act_bwd(dact, conv, u, cw8, name):
    n_rows = LP // TM
    cb2 = 2 * CONV_BLOCK
    n_strips = TM // STRIP

    def body(dact_ref, conv_ref, u_ref, cw_ref, du_ref, dcw_ref, carry_ref):
        i = pl.program_id(1)

        @pl.when(i == 0)
        def _():
            dcw_ref[...] = jnp.zeros_like(dcw_ref)
            carry_ref[...] = jnp.zeros_like(carry_ref)

        w0, w1, w2 = cw_ref[0:1, :], cw_ref[1:2, :], cw_ref[2:3, :]
        row = lax.broadcasted_iota(jnp.int32, (STRIP, 1), 0)
        fold = lambda z: z[:8, :] + z[8:, :]

        def strip(k, carry):
            n1, n2, s0, s1, s2, s3 = carry
            r0 = pl.multiple_of((n_strips - 1 - k) * STRIP, STRIP)
            cv = conv_ref[pl.ds(r0, STRIP), :].astype(F32)
            a = cv[:, :CONV_BLOCK]
            g = cv[:, CONV_BLOCK:]
            t = jnp.tanh(GELU_C * (a + GELU_K * a * a * a))
            gel = 0.5 * a * (1.0 + t)
            dgel = 0.5 * (1.0 + t) + 0.5 * a * (1.0 - t * t) * (GELU_C * (1.0 + 3.0 * GELU_K * a * a))
            dav = dact_ref[pl.ds(r0, STRIP), :].astype(F32)
            dconv = jnp.concatenate([dav * g * dgel, dav * gel], axis=1)
            u1 = pltpu.roll(dconv, STRIP - 1, 0)
            u2 = pltpu.roll(dconv, STRIP - 2, 0)
            d1 = jnp.where(row >= STRIP - 1, n1, u1)
            d2 = jnp.where(row >= STRIP - 2, n2, u2)
            du_ref[pl.ds(r0, STRIP), :] = (dconv * w2 + d1 * w1 + d2 * w0).astype(BF16)
            x = u_ref[pl.ds(r0, STRIP), :].astype(F32)
            return (u1, u2, s0 + fold(d2 * x), s1 + fold(d1 * x), s2 + fold(dconv * x), s3 + fold(dconv))

        below = carry_ref[...]
        zero = jnp.zeros((8, cb2), F32)
        init = (pltpu.roll(below, STRIP - 1, 0), pltpu.roll(below, STRIP - 2, 0), zero, zero, zero, zero)
        u1, _, s0, s1, s2, s3 = lax.fori_loop(0, n_strips, strip, init)
        carry_ref[...] = pltpu.roll(u1, 1, 0)
        dcw_ref[0:1, :] += jnp.sum(s0, axis=0, keepdims=True)
        dcw_ref[1:2, :] += jnp.sum(s1, axis=0, keepdims=True)
        dcw_ref[2:3, :] += jnp.sum(s2, axis=0, keepdims=True)
        dcw_ref[3:4, :] += jnp.sum(s3, axis=0, keepdims=True)

    rev = lambda j, i: (n_rows - 1 - i, j)
    return pl.pallas_call(
        body, name=name, grid=(N_CONV_BLOCKS, n_rows),
        in_specs=[pl.BlockSpec((TM, CONV_BLOCK), rev), pl.BlockSpec((TM, cb2), rev), pl.BlockSpec((TM, cb2), rev),
                  pl.BlockSpec((8, cb2), lambda j, i: (0, j))],
        out_specs=[pl.BlockSpec((TM, cb2), rev), pl.BlockSpec((8, cb2), lambda j, i: (0, j))],
        out_shape=[jax.ShapeDtypeStruct((LP, D_UP), BF16), jax.ShapeDtypeStruct((8, D_UP), F32)],
        scratch_shapes=[pltpu.VMEM((STRIP, cb2), F32)],
        compiler_params=_cparams("arbitrary", "arbitrary"),
    )(dact, conv, u, cw8)


CHUNKS_PER_STEP = 3 if N_CHUNKS % 3 == 0 else 1
STEP_ROWS = CHUNKS_PER_STEP * CHUNK
N_STEPS = N_CHUNKS // CHUNKS_PER_STEP


def _ret_consts(h):
    rows = STEP_ROWS
    lg = math.log(1.0 - 2.0 ** (-5.0 - h))
    ri = lax.broadcasted_iota(jnp.int32, (rows, rows), 0)
    ci = lax.broadcasted_iota(jnp.int32, (rows, rows), 1)
    diff = (ri - ci).astype(F32)
    dmat = jnp.where(diff >= 0, jnp.exp(lg * jnp.maximum(diff, 0.0)), 0.0)
    rowf = lax.broadcasted_iota(jnp.int32, (rows, 1), 0).astype(F32)
    zeta = jnp.exp(lg * (rows - 1.0 - rowf))
    xi = jnp.exp(lg * (rowf + 1.0))
    return dmat, zeta, xi, math.exp(lg * rows)


def _rope(t, cosv, sinv):
    return t * cosv + pltpu.roll(t, RET_DK // 2, 1) * sinv


def _unrope(d, cosv, sinv):
    return d * cosv + pltpu.roll(d * sinv, RET_DK // 2, 1)


def _gla_common(p_ref, w2_ref, gb_ref, chunk, rows):
    row = lax.broadcasted_iota(jnp.int32, (CHUNK, 1), 0)
    real = (chunk * CHUNK + row) >= PAD_ROWS
    ga = p_ref[rows, O_GA:O_GA + 128]
    z = _dot(ga, w2_ref[...]) + gb_ref[...]
    la = (jnp.minimum(z, 0.0) - jnp.log(1.0 + jnp.exp(-jnp.abs(z)))) * (1.0 / GLA_TAU)
    la = jnp.where(real, la, 0.0)
    ri = lax.broadcasted_iota(jnp.int32, (CHUNK, CHUNK), 0)
    ci = lax.broadcasted_iota(jnp.int32, (CHUNK, CHUNK), 1)
    tril = (ri >= ci).astype(F32)
    cum = _dot_exact_rhs(tril, la)
    last = cum[CHUNK - 1:CHUNK, :]
    qs = p_ref[rows, O_GQ:O_GQ + 256] * (GLA_DK ** -0.5)
    k = p_ref[rows, O_GK:O_GK + 256]
    ecum = jnp.exp(cum)
    ekl = jnp.exp(last - cum)
    el = jnp.exp(last)
    refs = [jnp.zeros((1, 256), F32)] + [cum[a * SUB - 1:a * SUB, :] for a in range(1, N_SUB)]
    eq = [jnp.exp(cum[a * SUB:(a + 1) * SUB, :] - refs[a]) for a in range(N_SUB)]
    spread = refs[0] - cum[SUB - 1:SUB, :]
    for a in range(1, N_SUB):
        spread = jnp.maximum(spread, refs[a] - cum[(a + 1) * SUB - 1:(a + 1) * SUB, :])
    small = jnp.max(spread) <= GLA_FACTORED_MAX
    return dict(real=real, row=row, z=z, la=la, cum=cum, last=last, qs=qs, k=k, ecum=ecum, ekl=ekl, el=el,
                refs=refs, eq=eq, small=small, ri=ri, ci=ci)


GLA_FACTORED_MAX = 40.0


def _head_block_mask():
    r = lax.broadcasted_iota(jnp.int32, (CHUNK, 256), 0)
    col = lax.broadcasted_iota(jnp.int32, (CHUNK, 256), 1)
    return (r // SUB) == (col // GLA_DK)


def _state_block_mask():
    r = lax.broadcasted_iota(jnp.int32, (GLA_HEADS * GLA_DK, GLA_HEADS * GLA_DV), 0)
    col = lax.broadcasted_iota(jnp.int32, (GLA_HEADS * GLA_DK, GLA_HEADS * GLA_DV), 1)
    return (r // GLA_DK) == (col // GLA_DV)


def _block_diagonal(blocks):
    zero = jnp.zeros((GLA_DK, GLA_DV), F32)
    return jnp.concatenate([jnp.concatenate([blocks[h] if g == h else zero for g in range(GLA_HEADS)], axis=1)
                            for h in range(GLA_HEADS)], axis=0)


def _gla_factored(c):
    mask = _head_block_mask()
    eks, keys, queries = [], [], []
    for a in range(N_SUB):
        ek = jnp.exp(jnp.minimum(c["refs"][a] - c["cum"], GLA_FACTORED_MAX))
        qh = c["qs"][a * SUB:(a + 1) * SUB, :] * c["eq"][a]
        eks.append(ek)
        keys.append(c["k"] * ek)
        queries.append(jnp.where(mask, jnp.concatenate([qh] * GLA_HEADS, axis=0), 0.0))
    return eks, keys, queries


def _gla_scores_factored(c, factored, p_scr):
    _, keys, queries = factored
    for a in range(N_SUB):
        out = _dot_nt(queries[a], keys[a])
        out = jnp.where(c["ci"] <= a * SUB + (c["ri"] & (SUB - 1)), out, 0.0)
        for h in range(GLA_HEADS):
            p_scr[h, a * SUB:(a + 1) * SUB, :] = out[h * SUB:(h + 1) * SUB, :]


def _gla_intra_bwd_factored(c, factored, dps, dq_scr, dk_scr):
    eks, keys, queries = factored
    mask = _head_block_mask()
    dk = jnp.zeros((CHUNK, 256), F32)
    for a in range(N_SUB):
        dpa = jnp.concatenate([dps[h][a * SUB:(a + 1) * SUB, :] for h in range(GLA_HEADS)], axis=0)
        dq = jnp.where(mask, _dot(dpa, keys[a]), 0.0)
        dq = dq[0:SUB] + dq[SUB:2 * SUB] + dq[2 * SUB:3 * SUB] + dq[3 * SUB:4 * SUB]
        dq_scr[a * SUB:(a + 1) * SUB, :] = dq * c["eq"][a]
        dk = dk + _dot_tn(dpa, queries[a]) * eks[a]
    dk_scr[...] = dk


def _gla_lag_weights(c):
    cum, row = c["cum"], c["row"]
    out = [jnp.ones((CHUNK, 256), F32)]
    for r in range(1, SUB):
        out.append(jnp.where((row % SUB) >= r, jnp.exp(jnp.minimum(cum - pltpu.roll(cum, r, 0), 0.0)), 0.0))
    return out


def _gla_pairwise_keys(c):
    return [None] + [c["k"] * jnp.exp(jnp.minimum(c["refs"][a] - c["cum"], 0.0)) for a in range(1, N_SUB)]


def _gla_scores_pairwise(c, lag_w, keys, h):
    sl = slice(GLA_DK * h, GLA_DK * (h + 1))
    qs, k = c["qs"][:, sl], c["k"][:, sl]
    ri, ci = c["ri"], c["ci"]
    p = jnp.zeros((CHUNK, CHUNK), F32)
    for r in range(SUB):
        kr = k if r == 0 else pltpu.roll(k, r, 0)
        pr = jnp.sum(qs * kr * lag_w[r][:, sl], axis=1, keepdims=True)
        p = p + jnp.where(ci == ri - r, pr, 0.0)
    blocks = [jnp.zeros((SUB, CHUNK), F32)]
    for a in range(1, N_SUB):
        qh = qs[a * SUB:(a + 1) * SUB, :] * c["eq"][a][:, sl]
        blocks.append(jnp.where(ci[:SUB, :] < a * SUB, _dot_nt(qh, keys[a][:, sl]), 0.0))
    return p + jnp.concatenate(blocks, axis=0)


def _gla_all_scores(c, p_scr, factored):
    if factored:
        _gla_scores_factored(c, _gla_factored(c), p_scr)
    else:
        lag_w, keys = _gla_lag_weights(c), _gla_pairwise_keys(c)
        for h in range(GLA_HEADS):
            p_scr[h] = _gla_scores_pairwise(c, lag_w, keys, h)


def _either_form(chunks, run):
    small = chunks[0]["small"]
    for c in chunks[1:]:
        small = jnp.logical_and(small, c["small"])
    pl.when(small)(lambda: run(True))
    pl.when(jnp.logical_not(small))(lambda: run(False))


def _gla_intra_bwd_pairwise(c, lag_w, keys, dp, h):
    sl = slice(GLA_DK * h, GLA_DK * (h + 1))
    qs_h, k_h = c["qs"][:, sl], c["k"][:, sl]
    ri, ci = c["ri"], c["ci"]
    dq_rows = [jnp.zeros((SUB, GLA_DK), F32)]
    dk = jnp.zeros((CHUNK, GLA_DK), F32)
    for a in range(1, N_SUB):
        eq = c["eq"][a][:, sl]
        qh = qs_h[a * SUB:(a + 1) * SUB, :] * eq
        dpa = jnp.where(ci[:SUB, :] < a * SUB, dp[a * SUB:(a + 1) * SUB, :], 0.0)
        dq_rows.append(_dot(dpa, keys[a][:, sl]) * eq)
        ek = jnp.exp(jnp.minimum(c["refs"][a][:, sl] - c["cum"][:, sl], 0.0))
        dk = dk + _dot_tn(dpa, qh) * ek
    dq = jnp.concatenate(dq_rows, axis=0)
    for r in range(SUB):
        w = lag_w[r][:, sl]
        dpr = jnp.sum(jnp.where(ci == ri - r, dp, 0.0), axis=1, keepdims=True)
        kr = k_h if r == 0 else pltpu.roll(k_h, r, 0)
        dq = dq + dpr * kr * w
        back = dpr * qs_h * w
        dk = dk + (back if r == 0 else pltpu.roll(back, CHUNK - r, 0))
    return dq, dk


def _gla_all_intra_bwd(c, dps, p_scr, dq_scr, dk_scr, factored):
    if factored:
        terms = _gla_factored(c)
        _gla_scores_factored(c, terms, p_scr)
        _gla_intra_bwd_factored(c, terms, dps, dq_scr, dk_scr)
    else:
        lag_w, keys = _gla_lag_weights(c), _gla_pairwise_keys(c)
        outs = [_gla_intra_bwd_pairwise(c, lag_w, keys, dps[h], h) for h in range(GLA_HEADS)]
        for h in range(GLA_HEADS):
            p_scr[h] = _gla_scores_pairwise(c, lag_w, keys, h)
        dq_scr[...] = jnp.concatenate([o[0] for o in outs], axis=1)
        dk_scr[...] = jnp.concatenate([o[1] for o in outs], axis=1)


def _mixer_fwd(proj, cos2, sin2, w2p, gb, rnw, gnw, name, carried=()):
    n_carried = len(carried)

    def body(*refs):
        p_ref, c_ref, s_ref, w2_ref, gb_ref, rnw_ref, gnw_ref = refs[:7]
        x_refs, refs = refs[7:7 + n_carried], refs[7 + n_carried:]
        ocat_ref, mrg_ref, sr_out, sg_out = refs[:4]
        gathered_refs, refs = refs[4:4 + n_carried], refs[4 + n_carried:]
        sr, sg, p_scr = refs[:3]
        n = pl.program_id(0)
        if n_carried:
            start, forward, finish = _gather_phases(x_refs, gathered_refs, *refs[3:])
            pl.when(n == 0)(start)
            pl.when(n == (3 * N_STEPS) // 4)(forward)

        @pl.when(n == 0)
        def _():
            sr[...] = jnp.zeros_like(sr)
            sg[...] = jnp.zeros_like(sg)

        sr_out[0] = sr[...]
        cosv, sinv = c_ref[...], s_ref[...]

        for h in range(RET_HEADS):
            dmat, zeta, xi, gc = _ret_consts(h)
            hs = slice(128 * h, 128 * (h + 1))
            q = _rope(p_ref[:, O_RQ + 128 * h:O_RQ + 128 * (h + 1)], cosv, sinv)
            k = _rope(p_ref[:, O_RK + 128 * h:O_RK + 128 * (h + 1)], cosv, sinv) * (RET_DK ** -0.5)
            v = p_ref[:, O_RV + 128 * h:O_RV + 128 * (h + 1)]
            g = p_ref[:, O_RG + 128 * h:O_RG + 128 * (h + 1)]
            s_in = sr[h]
            a = _dot_nt(q, k) * dmat
            o = _dot(a, v) + _dot(q, s_in) * xi
            sr[h] = gc * s_in + _dot_tn(k * zeta, v)
            mu = jnp.mean(o, axis=-1, keepdims=True)
            xc = o - mu
            nrm = xc * lax.rsqrt(jnp.mean(xc * xc, axis=-1, keepdims=True) + EPS)
            ocat_ref[:, hs] = o
            mrg_ref[:, hs] = (nrm * rnw_ref[:, hs] * (g * _sigmoid(g))).astype(BF16)

        row_slices = [slice(CHUNK * j, CHUNK * (j + 1)) for j in range(CHUNKS_PER_STEP)]
        chunks = [_gla_common(p_ref, w2_ref, gb_ref, n * CHUNKS_PER_STEP + j, rows)
                  for j, rows in enumerate(row_slices)]

        def gla_chunks(factored):
            own = _state_block_mask()
            for j, (rows, c) in enumerate(zip(row_slices, chunks)):
                s_in = sg[...]
                for h in range(GLA_HEADS):
                    sg_out[j, h] = s_in[GLA_DK * h:GLA_DK * (h + 1), GLA_DV * h:GLA_DV * (h + 1)]
                _gla_all_scores(c, p_scr.at[j], factored)
                v_all = p_ref[rows, O_GV:O_GV + GLA_HEADS * GLA_DV]
                o_inter = _dot(c["qs"] * c["ecum"], s_in)
                decay = jnp.exp(_dot_tn_exact_lhs(c["la"], jnp.ones((CHUNK, GLA_HEADS * GLA_DV), F32)))
                sg[...] = decay * s_in + jnp.where(own, _dot_tn(c["k"] * c["ekl"], v_all), 0.0)
                o_intra = _dot(p_scr[j].reshape(GLA_HEADS * CHUNK, CHUNK), v_all)
                for h in range(GLA_HEADS):
                    hs = slice(512 + 128 * h, 512 + 128 * (h + 1))
                    g = p_ref[rows, O_GR + 128 * h:O_GR + 128 * (h + 1)]
                    o = (o_intra[CHUNK * h:CHUNK * (h + 1), GLA_DV * h:GLA_DV * (h + 1)]
                         + o_inter[:, GLA_DV * h:GLA_DV * (h + 1)])
                    nrm = o * lax.rsqrt(jnp.mean(o * o, axis=-1, keepdims=True) + EPS)
                    ocat_ref[rows, hs] = o
                    mrg_ref[rows, hs] = (nrm * gnw_ref[:, 128 * h:128 * (h + 1)] * (g * _sigmoid(g))).astype(BF16)

        _either_form(chunks, gla_chunks)

        if n_carried:
            pl.when(n == N_STEPS - 1)(finish)

    const = lambda shape: pl.BlockSpec(shape, lambda n: (0,) * len(shape))
    anywhere = [pl.BlockSpec(memory_space=pl.ANY)] * n_carried
    return pl.pallas_call(
        body, name=name, grid=(N_STEPS,),
        in_specs=[pl.BlockSpec((STEP_ROWS, IN_WP), lambda n: (n, 0)),
                  pl.BlockSpec((STEP_ROWS, 128), lambda n: (n, 0)), pl.BlockSpec((STEP_ROWS, 128), lambda n: (n, 0)),
                  const((128, 256)), const((1, 256)), const((1, 512)), const((1, 512))] + anywhere,
        out_specs=[pl.BlockSpec((STEP_ROWS, D), lambda n: (n, 0)), pl.BlockSpec((STEP_ROWS, D), lambda n: (n, 0)),
                   pl.BlockSpec((1, RET_HEADS, RET_DK, 128), lambda n: (n, 0, 0, 0)),
                   pl.BlockSpec((CHUNKS_PER_STEP, GLA_HEADS, GLA_DK, GLA_DV), lambda n: (n, 0, 0, 0))] + anywhere,
        out_shape=[jax.ShapeDtypeStruct((LP, D), F32), jax.ShapeDtypeStruct((LP, D), BF16),
                   jax.ShapeDtypeStruct((N_STEPS, RET_HEADS, RET_DK, 128), F32),
                   jax.ShapeDtypeStruct((N_CHUNKS, GLA_HEADS, GLA_DK, GLA_DV), F32)] + _gathered_shapes(carried),
        scratch_shapes=[pltpu.VMEM((RET_HEADS, RET_DK, 128), F32),
                        pltpu.VMEM((GLA_HEADS * GLA_DK, GLA_HEADS * GLA_DV), F32),
                        pltpu.VMEM((CHUNKS_PER_STEP, GLA_HEADS, CHUNK, CHUNK), F32)] + _exchange_sems(n_carried),
        compiler_params=_cparams("arbitrary"),
    )(proj, cos2, sin2, w2p, gb, rnw, gnw, *carried)


def _mixer_bwd(proj, ocat, dmrg, sr_all, sg_all, cos2, sin2, w2p, gb, rnw, gnw, name, carried=()):
    last_step = N_STEPS - 1
    n_carried = len(carried)

    def body(*refs):
        p_ref, ocat_ref, dm_ref, sr_ref, sg_ref, c_ref, s_ref, w2_ref, gb_ref, rnw_ref, gnw_ref = refs[:11]
        g_refs, refs = refs[11:11 + n_carried], refs[11 + n_carried:]
        dp_ref, dw2_ref, dgb_ref, drn_ref, dgn_ref = refs[:5]
        got_refs, refs = refs[5:5 + n_carried], refs[5 + n_carried:]
        dsr, dsg, p_scr, dq_scr, dk_scr = refs[:5]
        step = pl.program_id(0)
        n = last_step - step
        if n_carried:
            start, finish = _exchange_phases(g_refs, got_refs, *refs[5:])
            pl.when(step == 0)(start)

        @pl.when(step == 0)
        def _():
            dsr[...] = jnp.zeros_like(dsr)
            dsg[...] = jnp.zeros_like(dsg)
            dw2_ref[...] = jnp.zeros_like(dw2_ref)
            dgb_ref[...] = jnp.zeros_like(dgb_ref)
            drn_ref[...] = jnp.zeros_like(drn_ref)
            dgn_ref[...] = jnp.zeros_like(dgn_ref)

        cosv, sinv = c_ref[...], s_ref[...]
        step_row = lax.broadcasted_iota(jnp.int32, (STEP_ROWS, 1), 0)
        real = ((n * STEP_ROWS + step_row) >= PAD_ROWS).astype(F32)

        for h in range(RET_HEADS):
            dmat, zeta, xi, gc = _ret_consts(h)
            hs = slice(128 * h, 128 * (h + 1))
            q = _rope(p_ref[:, O_RQ + 128 * h:O_RQ + 128 * (h + 1)], cosv, sinv)
            k = _rope(p_ref[:, O_RK + 128 * h:O_RK + 128 * (h + 1)], cosv, sinv) * (RET_DK ** -0.5)
            v = p_ref[:, O_RV + 128 * h:O_RV + 128 * (h + 1)]
            g = p_ref[:, O_RG + 128 * h:O_RG + 128 * (h + 1)]
            o = ocat_ref[:, hs]
            dy = dm_ref[:, hs]
            wv = rnw_ref[:, hs]
            mu = jnp.mean(o, axis=-1, keepdims=True)
            xc = o - mu
            rs = lax.rsqrt(jnp.mean(xc * xc, axis=-1, keepdims=True) + EPS)
            nrm = xc * rs
            sgm = _sigmoid(g)
            sil = g * sgm
            drn_ref[0:1, hs] += jnp.sum(dy * nrm * sil, axis=0, keepdims=True)
            dgate = dy * nrm * wv * (sgm * (1.0 + g * (1.0 - sgm)))
            dn = dy * wv * sil
            do = rs * (dn - jnp.mean(dn, axis=-1, keepdims=True) - nrm * jnp.mean(dn * nrm, axis=-1, keepdims=True))
            s_in = sr_ref[0, h]
            ds_out = dsr[h]
            a = _dot_nt(q, k) * dmat
            da = _dot_nt(do, v) * dmat
            dox = do * xi
            dq = _dot(da, k) + _dot_nt(dox, s_in)
            dk = _dot_tn(da, q) + _dot_nt(v, ds_out) * zeta
            dv = _dot_tn(a, do) + _dot(k * zeta, ds_out)
            dsr[h] = gc * ds_out + _dot_tn(q, dox)
            dk = dk * (RET_DK ** -0.5)
            dp_ref[:, O_RQ + 128 * h:O_RQ + 128 * (h + 1)] = (_unrope(dq, cosv, sinv) * real).astype(BF16)
            dp_ref[:, O_RK + 128 * h:O_RK + 128 * (h + 1)] = (_unrope(dk, cosv, sinv) * real).astype(BF16)
            dp_ref[:, O_RV + 128 * h:O_RV + 128 * (h + 1)] = (dv * real).astype(BF16)
            dp_ref[:, O_RG + 128 * h:O_RG + 128 * (h + 1)] = (dgate * real).astype(BF16)

        row_slices = [slice(CHUNK * j, CHUNK * (j + 1)) for j in range(CHUNKS_PER_STEP)]
        chunks = [_gla_common(p_ref, w2_ref, gb_ref, n * CHUNKS_PER_STEP + j, rows)
                  for j, rows in enumerate(row_slices)]

        def gla_chunks(factored):
            for j in reversed(range(CHUNKS_PER_STEP)):
                gla_chunk_bwd(chunks[j], n * CHUNKS_PER_STEP + j, row_slices[j], j, factored, p_ref, ocat_ref, dm_ref,
                              sg_ref, w2_ref, gnw_ref, dp_ref, dw2_ref, dgb_ref, dgn_ref, dsg, p_scr, dq_scr, dk_scr)

        _either_form(chunks, gla_chunks)
        if n_carried:
            pl.when(step == last_step)(finish)

    def gla_chunk_bwd(c, chunk, rows, j, factored, p_ref, ocat_ref, dm_ref, sg_ref, w2_ref, gnw_ref,
                      dp_ref, dw2_ref, dgb_ref, dgn_ref, dsg, p_scr, dq_scr, dk_scr):
        row = lax.broadcasted_iota(jnp.int32, (CHUNK, 1), 0)
        real = ((chunk * CHUNK + row) >= PAD_ROWS).astype(F32)
        ri, ci = c["ri"], c["ci"]
        causal = ri >= ci
        triu = (ci >= ri).astype(F32)
        qe = c["qs"] * c["ecum"]
        kl = c["k"] * c["ekl"]
        v_all = p_ref[rows, O_GV:O_GV + GLA_HEADS * GLA_DV]
        dos, dps = [], []
        for h in range(GLA_HEADS):
            hs = slice(512 + 128 * h, 512 + 128 * (h + 1))
            v = v_all[:, GLA_DV * h:GLA_DV * (h + 1)]
            g = p_ref[rows, O_GR + 128 * h:O_GR + 128 * (h + 1)]
            o = ocat_ref[rows, hs]
            dy = dm_ref[rows, hs]
            wv = gnw_ref[:, 128 * h:128 * (h + 1)]
            rs = lax.rsqrt(jnp.mean(o * o, axis=-1, keepdims=True) + EPS)
            nrm = o * rs
            sgm = _sigmoid(g)
            sil = g * sgm
            dgn_ref[0:1, 128 * h:128 * (h + 1)] += jnp.sum(dy * nrm * sil, axis=0, keepdims=True)
            dgate = dy * nrm * wv * (sgm * (1.0 + g * (1.0 - sgm)))
            dn = dy * wv * sil
            do = rs * (dn - nrm * jnp.mean(dn * nrm, axis=-1, keepdims=True))
            dp_ref[rows, O_GR + 128 * h:O_GR + 128 * (h + 1)] = (dgate * real).astype(BF16)
            dos.append(do)
        do_all = jnp.concatenate(dos, axis=1)
        do_blocks = jnp.where(_state_block_mask(), jnp.concatenate([do_all] * GLA_HEADS, axis=0), 0.0)
        dp_all = _dot_nt(do_blocks, v_all)
        dps = [jnp.where(causal, dp_all[CHUNK * h:CHUNK * (h + 1), :], 0.0) for h in range(GLA_HEADS)]
        _gla_all_intra_bwd(c, dps, p_scr.at[j], dq_scr.at[j], dk_scr.at[j], factored)
        s_in = _block_diagonal([sg_ref[j, h] for h in range(GLA_HEADS)])
        ds_out = dsg[...]
        decay = jnp.exp(_dot_tn_exact_lhs(c["la"], jnp.ones((CHUNK, GLA_HEADS * GLA_DV), F32)))
        dv_state = _dot(kl, ds_out)
        dqe = _dot_nt(do_all, s_in)
        dkl = _dot_nt(v_all, ds_out)
        dsg[...] = jnp.where(_state_block_mask(), _dot_tn(qe, do_all), 0.0) + decay * ds_out
        sd = s_in * ds_out
        sd_hi = sd.astype(BF16)
        sd_lo = (sd - sd_hi.astype(F32)).astype(BF16)
        ones8 = jnp.ones((8, GLA_HEADS * GLA_DV), BF16)
        nt = (((1,), (1,)), ((), ()))
        d_el = (lax.dot_general(ones8, sd_hi, nt, preferred_element_type=F32)
                + lax.dot_general(ones8, sd_lo, nt, preferred_element_type=F32))[0:1, :]
        dqs = dqe * c["ecum"] + dq_scr[j]
        dkk = dkl * c["ekl"] + dk_scr[j]
        d_last = jnp.sum(dkl * kl, axis=0, keepdims=True) + d_el * c["el"]
        dcum = c["qs"] * dqs - c["k"] * dkk + jnp.where(row == CHUNK - 1, d_last, 0.0)
        dla = _dot_exact_rhs(triu, dcum)
        dv = _dot_tn(p_scr[j].reshape(GLA_HEADS * CHUNK, CHUNK), do_blocks) + dv_state
        dp_ref[rows, O_GV:O_GV + GLA_HEADS * GLA_DV] = (dv * real).astype(BF16)
        dp_ref[rows, O_GQ:O_GQ + 256] = (dqs * (GLA_DK ** -0.5) * real).astype(BF16)
        dp_ref[rows, O_GK:O_GK + 256] = (dkk * real).astype(BF16)
        dz = dla * (1.0 / GLA_TAU) * _sigmoid(-c["z"]) * real
        ga = p_ref[rows, O_GA:O_GA + 128]
        dp_ref[rows, O_GA:O_GA + 128] = _dot_nt(dz, w2_ref[...]).astype(BF16)
        dp_ref[rows, O_GA + 128:IN_WP] = jnp.zeros((CHUNK, IN_WP - O_GA - 128), BF16)
        dw2_ref[...] += _dot_tn(ga, dz)
        dgb_ref[0:1, :] += jnp.sum(dz, axis=0, keepdims=True)

    const = lambda shape: pl.BlockSpec(shape, lambda s: (0,) * len(shape))
    rev = lambda s: (last_step - s, 0)
    anywhere = [pl.BlockSpec(memory_space=pl.ANY)] * n_carried
    return pl.pallas_call(
        body, name=name, grid=(N_STEPS,),
        in_specs=[pl.BlockSpec((STEP_ROWS, IN_WP), rev), pl.BlockSpec((STEP_ROWS, D), rev),
                  pl.BlockSpec((STEP_ROWS, D), rev),
                  pl.BlockSpec((1, RET_HEADS, RET_DK, 128), lambda s: (last_step - s, 0, 0, 0)),
                  pl.BlockSpec((CHUNKS_PER_STEP, GLA_HEADS, GLA_DK, GLA_DV), lambda s: (last_step - s, 0, 0, 0)),
                  pl.BlockSpec((STEP_ROWS, 128), rev), pl.BlockSpec((STEP_ROWS, 128), rev),
                  const((128, 256)), const((1, 256)), const((1, 512)), const((1, 512))] + anywhere,
        out_specs=[pl.BlockSpec((STEP_ROWS, IN_WP), rev), const((128, 256)), const((8, 256)),
                   const((8, 512)), const((8, 512))] + anywhere,
        out_shape=[jax.ShapeDtypeStruct((LP, IN_WP), BF16), jax.ShapeDtypeStruct((128, 256), F32),
                   jax.ShapeDtypeStruct((8, 256), F32), jax.ShapeDtypeStruct((8, 512), F32),
                   jax.ShapeDtypeStruct((8, 512), F32)] + [jax.ShapeDtypeStruct(g.shape, g.dtype) for g in carried],
        scratch_shapes=[pltpu.VMEM((RET_HEADS, RET_DK, 128), F32),
                        pltpu.VMEM((GLA_HEADS * GLA_DK, GLA_HEADS * GLA_DV), F32),
                        pltpu.VMEM((CHUNKS_PER_STEP, GLA_HEADS, CHUNK, CHUNK), F32),
                        pltpu.VMEM((CHUNKS_PER_STEP, CHUNK, 256), F32),
                        pltpu.VMEM((CHUNKS_PER_STEP, CHUNK, 256), F32)] + _exchange_sems(n_carried),
        compiler_params=_cparams("arbitrary"),
    )(proj, ocat, dmrg, sr_all, sg_all, cos2, sin2, w2p, gb, rnw, gnw, *carried)


def _all_gather(xs, name):
    n = len(xs)

    def body(*refs):
        start, forward, finish = _gather_phases(refs[:n], refs[n:2 * n], *refs[2 * n:])
        start()
        forward()
        finish()

    return pl.pallas_call(
        body, name=name,
        in_specs=[pl.BlockSpec(memory_space=pl.ANY)] * n,
        out_specs=[pl.BlockSpec(memory_space=pl.ANY)] * n,
        out_shape=_gathered_shapes(xs),
        scratch_shapes=_exchange_sems(n),
    )(*xs)


def _gathered_shapes(xs):
    return [jax.ShapeDtypeStruct((N_DEV,) + x.shape, x.dtype) for x in xs]


def _exchange_sems(n):
    if n == 0:
        return []
    return [pltpu.SemaphoreType.DMA((7 * n,)), pltpu.SemaphoreType.DMA((7 * n,)), pltpu.SemaphoreType.DMA((n,))]


def _gather_phases(x_refs, out_refs, send_sems, recv_sems, local_sems):
    n = len(x_refs)
    mx, my, mc = lax.axis_index("x"), lax.axis_index("y"), lax.axis_index("c")
    me, sibling = (mx, my, mc), (mx, my, 1 - mc)
    chips = [(1 - mx, my), (mx, 1 - my), (1 - mx, 1 - my)]

    def slot(a, px, py, pc):
        return out_refs[a].at[4 * px + 2 * py + pc]

    def copy(a, k, block, to, src=None):
        return pltpu.make_async_remote_copy(
            src_ref=slot(a, *block) if src is None else src, dst_ref=slot(a, *block),
            send_sem=send_sems.at[7 * a + k], recv_sem=recv_sems.at[7 * a + k],
            device_id=to, device_id_type=MESH_IDS)

    mine = [pltpu.make_async_copy(x_refs[a], slot(a, *me), local_sems.at[a]) for a in range(n)]
    first = []
    for a in range(n):
        first.append(copy(a, 0, me, sibling, src=x_refs[a]))
        first += [copy(a, 1 + j, me, (*chip, mc), src=x_refs[a]) for j, chip in enumerate(chips)]
    passed = [copy(a, 4 + j, (*chip, mc), sibling) for j, chip in enumerate(chips) for a in range(n)]

    def start():
        for cp in mine + first:
            cp.start()

    def forward():
        for j, chip in enumerate(chips):
            for a in range(n):
                copy(a, 1 + j, (*chip, mc), me).wait_recv()
                passed[j * n + a].start()

    def finish():
        for a in range(n):
            copy(a, 0, sibling, me).wait_recv()
            for j, chip in enumerate(chips):
                copy(a, 4 + j, (*chip, 1 - mc), me).wait_recv()
        for cp in first + passed:
            cp.wait_send()
        for cp in mine:
            cp.wait()

    return start, forward, finish


def _exchange_blocks(gs, name):
    n = len(gs)

    def body(*refs):
        start, finish = _exchange_phases(refs[:n], refs[n:2 * n], *refs[2 * n:])
        start()
        finish()

    return pl.pallas_call(
        body, name=name,
        in_specs=[pl.BlockSpec(memory_space=pl.ANY)] * n,
        out_specs=[pl.BlockSpec(memory_space=pl.ANY)] * n,
        out_shape=[jax.ShapeDtypeStruct(g.shape, g.dtype) for g in gs],
        scratch_shapes=_exchange_sems(n),
    )(*gs)


def _exchange_phases(g_refs, out_refs, send_sems, recv_sems, local_sems):
    n = len(g_refs)
    mx, my, mc = lax.axis_index("x"), lax.axis_index("y"), lax.axis_index("c")
    me = 4 * mx + 2 * my + mc
    mine = [pltpu.make_async_copy(g_refs[a].at[me], out_refs[a].at[me], local_sems.at[a]) for a in range(n)]
    copies = []
    for r in range(1, N_DEV):
        px, py, pc = mx ^ (r >> 2), my ^ ((r >> 1) & 1), mc ^ (r & 1)
        peer = 4 * px + 2 * py + pc
        for a in range(n):
            copies.append(pltpu.make_async_remote_copy(
                src_ref=g_refs[a].at[peer], dst_ref=out_refs[a].at[me],
                send_sem=send_sems.at[7 * a + r - 1], recv_sem=recv_sems.at[7 * a + r - 1],
                device_id=(px, py, pc), device_id_type=MESH_IDS))

    def start():
        for cp in mine + copies:
            cp.start()

    def finish():
        for cp in copies:
            cp.wait_recv()
        for cp in copies:
            cp.wait_send()
        for cp in mine:
            cp.wait()

    return start, finish


IN_SHARD = IN_W // N_DEV
IN_SHARD_P = 512
UP_SHARD = D_UP // N_DEV
UP_SHARD_P = 768
RELAYOUT_ROWS = 256


def _pieces_w_in():
    return [(k, 0, IN_SHARD * k, IN_SHARD) for k in range(N_DEV)]


def _pieces_ffn_up():
    pieces = []
    for k in range(N_DEV):
        n, end = UP_SHARD * k, UP_SHARD * (k + 1)
        while n < end:
            half, r = divmod(n, D_FF)
            blk, off = divmod(r, CONV_BLOCK)
            run = min(CONV_BLOCK - off, end - n)
            pieces.append((k, n - UP_SHARD * k, 2 * CONV_BLOCK * blk + CONV_BLOCK * half + off, run))
            n += run
    return pieces


def _assemble_block(load, spans, dst_block, rows):
    lo = 128 * dst_block
    lane = lax.broadcasted_iota(jnp.int32, (1, 128), 1)
    out = jnp.zeros((rows, 128), F32)
    for key, src_off, dst_off, length in spans:
        a, b = max(lo, dst_off), min(lo + 128, dst_off + length)
        s, s_end = src_off + (a - dst_off), src_off + (b - dst_off)
        d = a
        while s < s_end:
            e = min(s_end, 128 * (s // 128 + 1))
            blk = load(key, s // 128)
            shift = (d - s) % 128
            if shift:
                blk = pltpu.roll(blk, shift, 1)
            out = jnp.where((lane >= d - lo) & (lane < d - lo + (e - s)), blk, out)
            d += e - s
            s = e
    return out


def _shards_to_cols(shards, pieces, width, name):
    _, rows, _ = shards.shape
    tr = RELAYOUT_ROWS

    def body(s_ref, o_ref):
        load = lambda k, b: s_ref[k, :, 128 * b:128 * (b + 1)].astype(F32)
        for db in range(width // 128):
            o_ref[:, 128 * db:128 * (db + 1)] = _assemble_block(load, pieces, db, tr).astype(BF16)

    return pl.pallas_call(
        body, name=name, grid=(rows // tr,),
        in_specs=[pl.BlockSpec((N_DEV, tr, shards.shape[2]), lambda i: (0, i, 0))],
        out_specs=pl.BlockSpec((tr, width), lambda i: (i, 0)),
        out_shape=jax.ShapeDtypeStruct((rows, width), BF16),
        compiler_params=_cparams("parallel"),
    )(shards)


def _cols_to_shards(full, pieces, shard_width, name):
    rows, width = full.shape
    tr = RELAYOUT_ROWS

    def body(f_ref, o_ref):
        load = lambda _, b: f_ref[:, 128 * b:128 * (b + 1)].astype(F32)
        for k in range(N_DEV):
            spans = [(None, dst_off, src_off, length) for dev, src_off, dst_off, length in pieces if dev == k]
            for db in range(shard_width // 128):
                o_ref[k, :, 128 * db:128 * (db + 1)] = _assemble_block(load, spans, db, tr).astype(BF16)

    return pl.pallas_call(
        body, name=name, grid=(rows // tr,),
        in_specs=[pl.BlockSpec((tr, width), lambda i: (i, 0))],
        out_specs=pl.BlockSpec((N_DEV, tr, shard_width), lambda i: (0, i, 0)),
        out_shape=jax.ShapeDtypeStruct((N_DEV, rows, shard_width), BF16),
        compiler_params=_cparams("parallel"),
    )(full)


def _adamw(parts, w, m, v, rows_per_step, name):
    rows, cols = w.shape
    assert rows % rows_per_step == 0 and parts.shape == (N_DEV, rows, cols)

    def body(p_ref, w_ref, m_ref, v_ref, g_ref, d_ref, nm_ref, nv_ref):
        g = p_ref[0].astype(F32)
        for j in range(1, N_DEV):
            g = g + p_ref[j].astype(F32)
        m_new = ADAM_B1 * m_ref[...] + (1.0 - ADAM_B1) * g
        v_new = ADAM_B2 * v_ref[...] + (1.0 - ADAM_B2) * (g * g)
        m_hat = m_new / (1.0 - ADAM_B1 ** ADAM_STEP)
        v_hat = v_new / (1.0 - ADAM_B2 ** ADAM_STEP)
        g_ref[...] = g
        d_ref[...] = -ADAM_LR * (m_hat / (jnp.sqrt(v_hat) + ADAM_EPS) + ADAM_WD * w_ref[...])
        nm_ref[...] = m_new
        nv_ref[...] = v_new

    tile = pl.BlockSpec((rows_per_step, cols), lambda i: (i, 0))
    shape = jax.ShapeDtypeStruct((rows, cols), F32)
    return pl.pallas_call(
        body, name=name, grid=(rows // rows_per_step,),
        in_specs=[pl.BlockSpec((N_DEV, rows_per_step, cols), lambda i: (0, i, 0)), tile, tile, tile],
        out_specs=[tile, tile, tile, tile],
        out_shape=[shape, shape, shape, shape],
        compiler_params=_cparams("parallel"),
    )(parts, w, m, v)


BIG = (("w_in", (DEPTH, D, IN_W // N_DEV), 2), ("w_out", (DEPTH, D // N_DEV, D), 1),
       ("ffn_up", (DEPTH, D, D_UP // N_DEV), 2), ("ffn_down", (DEPTH, D_FF // N_DEV, D), 1))
SMALL = (("meta_tokens", (N_META, D // N_DEV), 1), ("gla_gate_w2", (DEPTH, GATE_RANK, 256 // N_DEV), 2),
         ("ffn_conv_w", (DEPTH, 3, D_UP // N_DEV), 2))
REPL = (("pre_mix_norm", (DEPTH, D)), ("gla_gate_b", (DEPTH, 256)), ("ret_norm_w", (DEPTH, 512)),
        ("gla_norm_w", (DEPTH, 512)), ("post_mix_norm", (DEPTH, D)), ("pre_ffn_norm", (DEPTH, D)),
        ("ffn_conv_b", (DEPTH, D_UP)), ("post_ffn_norm", (DEPTH, D)))
WEIGHT_ORDER = ("meta_tokens", "pre_mix_norm", "w_in", "gla_gate_w2", "gla_gate_b", "ret_norm_w", "gla_norm_w",
                "w_out", "post_mix_norm", "pre_ffn_norm", "ffn_up", "ffn_conv_w", "ffn_conv_b", "ffn_down",
                "post_ffn_norm")


def _size(shape):
    return math.prod(shape)


def _round_up(n, mult):
    return -(-n // mult) * mult


REPL_ROWS = _round_up(-(-sum(_size(s) for _, s in REPL) // LANES), 8)
SMALL_ROWS = _round_up(-(-sum(_size(s) for _, s, _ in SMALL) // LANES), 8)


def _pack(arrays, rows, dtype):
    flat = jnp.concatenate([a.reshape(-1).astype(dtype) for a in arrays])
    return jnp.pad(flat, (0, rows * LANES - flat.shape[0])).reshape(rows, LANES)


def _unpack(buf, shapes):
    flat = buf.reshape(-1)
    out, off = [], 0
    for shape in shapes:
        out.append(flat[off:off + _size(shape)].reshape(shape))
        off += _size(shape)
    return out


def _unshard(blocks, axis):
    moved = jnp.moveaxis(blocks, 0, axis)
    shape = list(moved.shape)
    shape[axis:axis + 2] = [shape[axis] * shape[axis + 1]]
    return moved.reshape(shape)


def _to_blocks(full, axis):
    shape = list(full.shape)
    shape[axis:axis + 1] = [N_DEV, shape[axis] // N_DEV]
    return jnp.moveaxis(full.reshape(shape), axis, 0)


def _interleave_cols(w):
    lead = w.shape[:-1]
    return jnp.swapaxes(w.reshape(lead + (2, N_CONV_BLOCKS, CONV_BLOCK)), -3, -2).reshape(lead + (D_UP,))


def _deinterleave_cols(w):
    lead = w.shape[:-1]
    return jnp.swapaxes(w.reshape(lead + (N_CONV_BLOCKS, 2, CONV_BLOCK)), -3, -2).reshape(lead + (D_UP,))


def _rope_tables():
    half = RET_DK // 2
    inv = ROPE_BASE ** (-jnp.arange(half, dtype=F32) / half)
    pos = jnp.arange(LP, dtype=F32) - float(PAD_ROWS)
    ang = pos[:, None] * inv[None, :]
    c, s = jnp.cos(ang), jnp.sin(ang)
    return jnp.concatenate([c, c], axis=1), jnp.concatenate([-s, s], axis=1)


def kernel(x, meta_tokens, pre_mix_norm, w_in, gla_gate_w2, gla_gate_b, ret_norm_w, gla_norm_w, w_out, post_mix_norm, pre_ffn_norm, ffn_up, ffn_conv_w, ffn_conv_b, ffn_down, post_ffn_norm, loss_target, m_meta_tokens, m_pre_mix_norm, m_w_in, m_gla_gate_w2, m_gla_gate_b, m_ret_norm_w, m_gla_norm_w, m_w_out, m_post_mix_norm, m_pre_ffn_norm, m_ffn_up, m_ffn_conv_w, m_ffn_conv_b, m_ffn_down, m_post_ffn_norm, v_meta_tokens, v_pre_mix_norm, v_w_in, v_gla_gate_w2, v_gla_gate_b, v_ret_norm_w, v_gla_norm_w, v_w_out, v_post_mix_norm, v_pre_ffn_norm, v_ffn_up, v_ffn_conv_w, v_ffn_conv_b, v_ffn_down, v_post_ffn_norm):
    weights = dict(meta_tokens=meta_tokens, pre_mix_norm=pre_mix_norm, w_in=w_in, gla_gate_w2=gla_gate_w2,
                   gla_gate_b=gla_gate_b, ret_norm_w=ret_norm_w, gla_norm_w=gla_norm_w, w_out=w_out,
                   post_mix_norm=post_mix_norm, pre_ffn_norm=pre_ffn_norm, ffn_up=ffn_up, ffn_conv_w=ffn_conv_w,
                   ffn_conv_b=ffn_conv_b, ffn_down=ffn_down, post_ffn_norm=post_ffn_norm)
    mom1 = dict(meta_tokens=m_meta_tokens, pre_mix_norm=m_pre_mix_norm, w_in=m_w_in, gla_gate_w2=m_gla_gate_w2,
                gla_gate_b=m_gla_gate_b, ret_norm_w=m_ret_norm_w, gla_norm_w=m_gla_norm_w, w_out=m_w_out,
                post_mix_norm=m_post_mix_norm, pre_ffn_norm=m_pre_ffn_norm, ffn_up=m_ffn_up,
                ffn_conv_w=m_ffn_conv_w, ffn_conv_b=m_ffn_conv_b, ffn_down=m_ffn_down, post_ffn_norm=m_post_ffn_norm)
    mom2 = dict(meta_tokens=v_meta_tokens, pre_mix_norm=v_pre_mix_norm, w_in=v_w_in, gla_gate_w2=v_gla_gate_w2,
                gla_gate_b=v_gla_gate_b, ret_norm_w=v_ret_norm_w, gla_norm_w=v_gla_norm_w, w_out=v_w_out,
                post_mix_norm=v_post_mix_norm, pre_ffn_norm=v_pre_ffn_norm, ffn_up=v_ffn_up,
                ffn_conv_w=v_ffn_conv_w, ffn_conv_b=v_ffn_conv_b, ffn_down=v_ffn_down, post_ffn_norm=v_post_ffn_norm)

    pad_cols = lambda a, width: jnp.pad(a, ((0, 0), (0, width - a.shape[1])))
    big_names = [n for n, _, _ in BIG]
    shard = {}
    for l in range(DEPTH):
        shard[l, "w_in"] = pad_cols(w_in[l].astype(BF16), IN_SHARD_P)
        shard[l, "w_out"] = w_out[l].astype(BF16)
        shard[l, "ffn_up"] = pad_cols(ffn_up[l].astype(BF16), UP_SHARD_P)
        shard[l, "ffn_down"] = ffn_down[l].astype(BF16)
    gathered = {(0, "w_in"): _all_gather([shard[0, "w_in"]], "gather_w_in_0")[0]}
    gather_in_mixer = {l: [(l, n) for n in big_names[1:]] + ([(l + 1, "w_in")] if l + 1 < DEPTH else [])
                       for l in range(DEPTH)}
    small = _all_gather([_pack([weights[n] for n, _, _ in SMALL], SMALL_ROWS, F32)], "gather_small_weights")[0]
    small_parts = _unpack_blocks(small, [s for _, s, _ in SMALL])
    full = {n: _unshard(p, ax) for (n, _, ax), p in zip(SMALL, small_parts)}
    w2p = jnp.pad(full["gla_gate_w2"], ((0, 0), (0, 128 - GATE_RANK), (0, 0)))
    cw8 = jnp.concatenate([_interleave_cols(full["ffn_conv_w"]), _interleave_cols(ffn_conv_b)[:, None, :],
                           jnp.zeros((DEPTH, 4, D_UP), F32)], axis=1)
    cos2, sin2 = _rope_tables()

    h = jnp.concatenate([jnp.zeros((PAD_ROWS, D), F32), full["meta_tokens"], x[0]], axis=0)
    target = jnp.concatenate([jnp.zeros((CHUNK, D), F32), loss_target[0]], axis=0)
    saved, layer_w = [], []
    for l in range(DEPTH):
        lw = dict(w_in=_shards_to_cols(gathered[l, "w_in"], _pieces_w_in(), IN_WP, f"w_in_cols_{l}"))
        a1 = _rmsnorm_fwd(h, pre_mix_norm[l:l + 1], f"pre_mix_norm_{l}")
        proj = _matmul(a1, lw["w_in"], out_dtype=F32, tm=TM, tn=1280, tk=D, name=f"in_proj_{l}", n_outer=True)
        keys = gather_in_mixer.get(l, [])
        ocat, merged, sr_all, sg_all, *got = _mixer_fwd(proj, cos2, sin2, w2p[l], gla_gate_b[l:l + 1],
                                                        ret_norm_w[l:l + 1], gla_norm_w[l:l + 1], f"mixer_fwd_{l}",
                                                        carried=[shard[key] for key in keys])
        gathered.update(zip(keys, got))
        lw["w_out"] = gathered[l, "w_out"].reshape(D, D)
        lw["w_up"] = _shards_to_cols(gathered[l, "ffn_up"], _pieces_ffn_up(), D_UP, f"ffn_up_cols_{l}")
        lw["w_down"] = gathered[l, "ffn_down"].reshape(D_FF, D)
        layer_w.append(lw)
        m, h1 = _matmul_resid_norm(merged, lw["w_out"], h, post_mix_norm[l:l + 1], f"out_proj_{l}")
        a2 = _rmsnorm_fwd(h1, pre_ffn_norm[l:l + 1], f"pre_ffn_norm_{l}")
        u = _matmul(a2, lw["w_up"], out_dtype=BF16, tm=TM, tn=1408, tk=D, name=f"ffn_up_{l}", n_outer=True)
        cv, act = _conv_act_fwd(u, cw8[l], f"ffn_conv_act_{l}")
        f, h2 = _matmul_resid_norm(act, lw["w_down"], h1, post_ffn_norm[l:l + 1], f"ffn_down_{l}")
        saved.append(dict(h=h, a1=a1, proj=proj, ocat=ocat, merged=merged, sr=sr_all, sg=sg_all, m=m, h1=h1,
                          a2=a2, u=u, cv=cv, act=act, f=f))
        h = h2

    dh, loss_acc = _loss_head(h, target, "loss_head")
    loss = lax.psum(loss_acc[0, 0], ("x", "y", "c"))

    kinds = ("grad", "delta", "new_m", "new_v")
    grads = {n: [None] * DEPTH for n in WEIGHT_ORDER if n != "meta_tokens" and n not in big_names}
    pending, parts = [], {}
    for l in reversed(range(DEPTH)):
        s, lw = saved[l], layer_w[l]
        dact, df, g_post_ffn = _norm_bwd_matmul(dh, s["f"], post_ffn_norm[l:l + 1], lw["w_down"], BF16,
                                                f"ffn_down_dx_{l}")
        g_down = _matmul(s["act"], df, ta=True, out_dtype=BF16, tm=D_FF // 2, tn=D, tk=TK_ROWS, name=f"ffn_down_dw_{l}")
        du, dcw = _conv_act_bwd(dact, s["cv"], s["u"], cw8[l], f"ffn_conv_act_bwd_{l}")
        dh1, g_pre_ffn = _matmul_norm_bwd(du, lw["w_up"], s["h1"], pre_ffn_norm[l:l + 1], dh, 1408, f"ffn_up_dx_{l}")
        g_up = _matmul(s["a2"], du, ta=True, out_dtype=BF16, tm=D, tn=1408, tk=TK_ROWS, name=f"ffn_up_dw_{l}")
        dmerged, dm, g_post_mix = _norm_bwd_matmul(dh1, s["m"], post_mix_norm[l:l + 1], lw["w_out"], F32,
                                                   f"out_proj_dx_{l}")
        g_out = _matmul(s["merged"], dm, ta=True, out_dtype=BF16, tm=D, tn=D, tk=TK_ROWS, name=f"out_proj_dw_{l}")
        pending += [((l, "ffn_down"), g_down.reshape(N_DEV, D_FF // N_DEV, D)),
                    ((l, "ffn_up"), _cols_to_shards(g_up, _pieces_ffn_up(), UP_SHARD_P, f"ffn_up_grad_shards_{l}")),
                    ((l, "w_out"), g_out.reshape(N_DEV, D // N_DEV, D))]
        dproj, g_w2, g_gb, g_rn, g_gn, *got = _mixer_bwd(s["proj"], s["ocat"], dmerged, s["sr"], s["sg"], cos2, sin2,
                                                         w2p[l], gla_gate_b[l:l + 1], ret_norm_w[l:l + 1],
                                                         gla_norm_w[l:l + 1], f"mixer_bwd_{l}",
                                                         carried=[blocks for _, blocks in pending])
        parts.update(zip([key for key, _ in pending], got))
        g_in = _matmul(s["a1"], dproj, ta=True, out_dtype=BF16, tm=D, tn=1280, tk=TK_ROWS, name=f"in_proj_dw_{l}")
        pending = [((l, "w_in"), _cols_to_shards(g_in, _pieces_w_in(), IN_SHARD_P, f"w_in_grad_shards_{l}"))]
        now = pending if l == 0 else []
        dh, g_pre_mix, *got = _matmul_norm_bwd(dproj, lw["w_in"], s["h"], pre_mix_norm[l:l + 1], dh1, IN_WP,
                                               f"in_proj_dx_{l}", carried=[blocks for _, blocks in now])
        parts.update(zip([key for key, _ in now], got))
        pending = [] if l == 0 else pending
        grads["post_ffn_norm"][l] = g_post_ffn[0]
        grads["ffn_conv_w"][l] = _deinterleave_cols(dcw[0:3])
        grads["ffn_conv_b"][l] = _deinterleave_cols(dcw[3])
        grads["pre_ffn_norm"][l] = g_pre_ffn[0]
        grads["post_mix_norm"][l] = g_post_mix[0]
        grads["gla_gate_w2"][l] = g_w2[:GATE_RANK]
        grads["gla_gate_b"][l] = g_gb[0]
        grads["ret_norm_w"][l] = g_rn[0]
        grads["gla_norm_w"][l] = g_gn[0]
        grads["pre_mix_norm"][l] = g_pre_mix[0]
    local = {n: jnp.stack(v) for n, v in grads.items()}
    local["meta_tokens"] = dh[PAD_ROWS:CHUNK]
    grad_x = dh[CHUNK:][None]

    blocks = jnp.concatenate([_to_blocks(local[n], ax).reshape(N_DEV, -1) for n, _, ax in SMALL], axis=1)
    blocks = jnp.pad(blocks, ((0, 0), (0, SMALL_ROWS * LANES - blocks.shape[1]))).reshape(N_DEV, SMALL_ROWS, LANES)
    *got, small_grad_parts = _exchange_blocks([b for _, b in pending] + [blocks], "exchange_last_grads")
    parts.update(zip([key for key, _ in pending], got))

    widths = dict(w_in=IN_SHARD_P, w_out=D, ffn_up=UP_SHARD_P, ffn_down=D)
    steps = dict(w_in=256, w_out=D // N_DEV, ffn_up=256, ffn_down=D_FF // N_DEV // 2)
    big_out = {kind: {n: [None] * DEPTH for n in big_names} for kind in kinds}
    for l in range(DEPTH):
        for n in big_names:
            mine = [pad_cols(d[n][l], widths[n]) for d in (weights, mom1, mom2)]
            results = _adamw(parts[l, n], *mine, steps[n], f"adamw_{n}_{l}")
            for kind, r in zip(kinds, results):
                big_out[kind][n][l] = r[:, :weights[n].shape[2]]
    out = {kind: {n: jnp.stack(v) for n, v in big_out[kind].items()} for kind in kinds}
    shard_shapes = [s for _, s, _ in SMALL]
    packed = [_pack([d[n] for n, _, _ in SMALL], SMALL_ROWS, F32) for d in (weights, mom1, mom2)]
    results = _adamw(small_grad_parts, *packed, SMALL_ROWS, "adamw_small_sharded")
    for kind, buf in zip(kinds, results):
        out[kind].update(zip([n for n, _, _ in SMALL], _unpack(buf, shard_shapes)))

    repl_parts = _all_gather([_pack([local[n] for n, _ in REPL], REPL_ROWS, F32)], "gather_small_grads")[0]
    packed = [_pack([d[n] for n, _ in REPL], REPL_ROWS, F32) for d in (weights, mom1, mom2)]
    results = _adamw(repl_parts, *packed, REPL_ROWS, "adamw_replicated")
    repl_shapes = [s for _, s in REPL]
    for kind, buf in zip(kinds, results):
        out[kind].update(zip([n for n, _ in REPL], _unpack(buf, repl_shapes)))

    return (loss, grad_x, *[out["grad"][n] for n in WEIGHT_ORDER], *[out["delta"][n] for n in WEIGHT_ORDER],
            *[out["new_m"][n] for n in WEIGHT_ORDER], *[out["new_v"][n] for n in WEIGHT_ORDER])


def _unpack_blocks(gathered, shapes):
    flat = gathered.reshape(N_DEV, -1)
    out, off = [], 0
    for shape in shapes:
        out.append(flat[:, off:off + _size(shape)].reshape((N_DEV,) + shape))
        off += _size(shape)
    return out
```

```python
import math

import jax
import jax.numpy as jnp
from jax import lax
from jax.experimental import pallas as pl
from jax.experimental.pallas import tpu as pltpu

F32 = jnp.float32
BF16 = jnp.bfloat16

D = 1024
SEQ = 8192
DEPTH = 2
N_META = 16
CHUNK = 64
SUB = 16
N_SUB = CHUNK // SUB
PAD_ROWS = CHUNK - N_META
LP = SEQ + CHUNK
N_CHUNKS = LP // CHUNK
RET_HEADS = 4
RET_DK = 128
GLA_HEADS = 4
GLA_DK = 64
GLA_DV = 128
GLA_TAU = 16.0
GATE_RANK = 16
IN_W = 3600
IN_WP = 3840
D_FF = 2816
D_UP = 2 * D_FF
CONV_BLOCK = 256
N_CONV_BLOCKS = D_FF // CONV_BLOCK
ROPE_BASE = 10000.0
EPS = 1e-6
N_DEV = 8
LANES = 1024

O_RQ, O_RK, O_RV, O_RG = 0, 512, 1024, 1536
O_GQ, O_GK, O_GV, O_GR, O_GA = 2048, 2304, 2560, 3072, 3584

ADAM_LR = 0.001
ADAM_B1 = 0.9
ADAM_B2 = 0.999
ADAM_EPS = 1e-08
ADAM_WD = 0.01
ADAM_STEP = 10

VMEM_LIMIT = 56 * 1024 * 1024
MESH_IDS = pl.DeviceIdType.MESH


def _row_tile(rows, limit):
    best = 16
    for t in range(16, min(rows, limit) + 1, 16):
        if rows % t == 0:
            best = t
    return best


TM = _row_tile(LP, 688)
TK_ROWS = _row_tile(LP, 1376)


def _cparams(*sem):
    return pltpu.CompilerParams(dimension_semantics=sem, vmem_limit_bytes=VMEM_LIMIT)


def _dot(a, b):
    return jnp.dot(a.astype(BF16), b.astype(BF16), preferred_element_type=F32)


def _dot_nt(a, b):
    return lax.dot_general(a.astype(BF16), b.astype(BF16), (((1,), (1,)), ((), ())), preferred_element_type=F32)


def _dot_tn(a, b):
    return lax.dot_general(a.astype(BF16), b.astype(BF16), (((0,), (0,)), ((), ())), preferred_element_type=F32)


def _split3(x):
    hi = x.astype(BF16)
    r1 = x - hi.astype(F32)
    mid = r1.astype(BF16)
    lo = (r1 - mid.astype(F32)).astype(BF16)
    return hi, mid, lo


def _dot_exact_rhs(t, x):
    hi, mid, lo = _split3(x)
    t = t.astype(BF16)
    return (jnp.dot(t, hi, preferred_element_type=F32) + jnp.dot(t, mid, preferred_element_type=F32)
            + jnp.dot(t, lo, preferred_element_type=F32))


def _dot_tn_exact_lhs(x, ones):
    dims = (((0,), (0,)), ((), ()))
    hi, mid, lo = _split3(x)
    ones = ones.astype(BF16)
    return (lax.dot_general(hi, ones, dims, preferred_element_type=F32)
            + lax.dot_general(mid, ones, dims, preferred_element_type=F32)
            + lax.dot_general(lo, ones, dims, preferred_element_type=F32))


def _sigmoid(x):
    return 1.0 / (1.0 + jnp.exp(-x))


def _matmul(a, b, *, ta=False, tb=False, out_dtype, tm, tn, tk, name, n_outer=False):
    m = a.shape[1] if ta else a.shape[0]
    k = a.shape[0] if ta else a.shape[1]
    n = b.shape[0] if tb else b.shape[1]
    assert (b.shape[1] if tb else b.shape[0]) == k
    assert m % tm == 0 and n % tn == 0 and k % tk == 0, (name, m, n, k, tm, tn, tk)
    nk = k // tk
    order = (lambda f: (lambda j, i, kk: f(i, j, kk))) if n_outer else (lambda f: f)
    a_spec = (pl.BlockSpec((tk, tm), order(lambda i, j, kk: (kk, i))) if ta
              else pl.BlockSpec((tm, tk), order(lambda i, j, kk: (i, kk))))
    b_spec = (pl.BlockSpec((tn, tk), order(lambda i, j, kk: (j, kk))) if tb
              else pl.BlockSpec((tk, tn), order(lambda i, j, kk: (kk, j))))
    dims = (((0 if ta else 1,), (1 if tb else 0,)), ((), ()))

    def body(a_ref, b_ref, o_ref, *acc):
        prod = lax.dot_general(a_ref[...].astype(BF16), b_ref[...].astype(BF16), dims, preferred_element_type=F32)
        if nk == 1:
            o_ref[...] = prod.astype(out_dtype)
            return
        acc_ref, = acc
        kk = pl.program_id(2)

        @pl.when(kk == 0)
        def _():
            acc_ref[...] = prod

        @pl.when(kk > 0)
        def _():
            acc_ref[...] += prod

        @pl.when(kk == nk - 1)
        def _():
            o_ref[...] = acc_ref[...].astype(out_dtype)

    return pl.pallas_call(
        body, name=name, grid=(n // tn, m // tm, nk) if n_outer else (m // tm, n // tn, nk),
        in_specs=[a_spec, b_spec],
        out_specs=pl.BlockSpec((tm, tn), order(lambda i, j, kk: (i, j))),
        out_shape=jax.ShapeDtypeStruct((m, n), out_dtype),
        scratch_shapes=[pltpu.VMEM((tm, tn), F32)] if nk > 1 else [],
        compiler_params=_cparams("parallel", "parallel", "arbitrary"),
    )(a, b)


def _matmul_resid_norm(a, b, h, w, name, target=None):
    k = a.shape[1]
    has_loss = target is not None

    def body(a_ref, b_ref, h_ref, w_ref, *refs):
        m = jnp.dot(a_ref[...].astype(BF16), b_ref[...].astype(BF16), preferred_element_type=F32)
        r = lax.rsqrt(jnp.mean(m * m, axis=-1, keepdims=True) + EPS)
        i = pl.program_id(0)
        row = i * TM + lax.broadcasted_iota(jnp.int32, (TM, 1), 0)
        y = h_ref[...] + jnp.where(row >= PAD_ROWS, m * r * w_ref[...], 0.0)
        if not has_loss:
            m_ref, y_ref = refs
            m_ref[...] = m
            y_ref[...] = y
            return
        t_ref, m_ref, dy_ref, loss_ref = refs
        m_ref[...] = m

        @pl.when(i == 0)
        def _():
            loss_ref[...] = jnp.zeros_like(loss_ref)

        diff = jnp.where(row >= CHUNK, y - t_ref[...], 0.0)
        dy_ref[...] = diff * (1.0 / D)
        loss_ref[...] += (0.5 / D) * jnp.sum(diff * diff)

    tile = pl.BlockSpec((TM, D), lambda i: (i, 0))
    shape = jax.ShapeDtypeStruct((LP, D), F32)
    in_specs = [pl.BlockSpec((TM, k), lambda i: (i, 0)), pl.BlockSpec((k, D), lambda i: (0, 0)), tile,
                pl.BlockSpec((1, D), lambda i: (0, 0))]
    if has_loss:
        return pl.pallas_call(
            body, name=name, grid=(LP // TM,),
            in_specs=in_specs + [tile],
            out_specs=[tile, tile, pl.BlockSpec((8, 128), lambda i: (0, 0))],
            out_shape=[shape, shape, jax.ShapeDtypeStruct((8, 128), F32)],
            compiler_params=_cparams("arbitrary"),
        )(a, b, h, w, target)
    return pl.pallas_call(
        body, name=name, grid=(LP // TM,),
        in_specs=in_specs, out_specs=[tile, tile], out_shape=[shape, shape],
        compiler_params=_cparams("parallel"),
    )(a, b, h, w)


def _rmsnorm_bwd_rows(dy, x, w):
    r = lax.rsqrt(jnp.mean(x * x, axis=-1, keepdims=True) + EPS)
    g = dy * w
    dx = r * g - x * (r * r * r * jnp.mean(g * x, axis=-1, keepdims=True))
    return dx, jnp.sum(dy * x * r, axis=0, keepdims=True)


def _matmul_norm_bwd(dz, b, x, w, resid, tk, name, carried=()):
    k = dz.shape[1]
    assert k % tk == 0
    nk = k // tk
    n_rows = LP // TM
    n_carried = len(carried)

    def body(*refs):
        a_ref, b_ref, x_ref, w_ref, r_ref = refs[:5]
        g_refs, refs = refs[5:5 + n_carried], refs[5 + n_carried:]
        dx_ref, dw_ref = refs[:2]
        got_refs, refs = refs[2:2 + n_carried], refs[2 + n_carried:]
        acc, sems = (refs[:1], refs[1:]) if nk > 1 else ((), refs)
        i, kk = pl.program_id(0), pl.program_id(1)
        if n_carried:
            exchange_start, exchange_finish = _exchange_phases(g_refs, got_refs, *sems)
            pl.when((i == 0) & (kk == 0))(exchange_start)

        @pl.when((i == 0) & (kk == 0))
        def _():
            dw_ref[...] = jnp.zeros_like(dw_ref)

        prod = lax.dot_general(a_ref[...].astype(BF16), b_ref[...].astype(BF16), (((1,), (1,)), ((), ())),
                               preferred_element_type=F32)

        def finish(dy):
            dx, dw = _rmsnorm_bwd_rows(dy, x_ref[...], w_ref[...])
            dx_ref[...] = dx + r_ref[...]
            dw_ref[0:1, :] += dw

        if nk == 1:
            finish(prod)
        else:
            acc_ref, = acc

            @pl.when(kk == 0)
            def _():
                acc_ref[...] = prod

            @pl.when((kk > 0) & (kk < nk - 1))
            def _():
                acc_ref[...] += prod

            @pl.when(kk == nk - 1)
            def _():
                finish(acc_ref[...] + prod)

        if n_carried:
            pl.when((i == n_rows - 1) & (kk == nk - 1))(exchange_finish)

    tile = pl.BlockSpec((TM, D), lambda i, kk: (i, 0))
    anywhere = [pl.BlockSpec(memory_space=pl.ANY)] * n_carried
    return pl.pallas_call(
        body, name=name, grid=(n_rows, nk),
        in_specs=[pl.BlockSpec((TM, tk), lambda i, kk: (i, kk)), pl.BlockSpec((D, tk), lambda i, kk: (0, kk)), tile,
                  pl.BlockSpec((1, D), lambda i, kk: (0, 0)), tile] + anywhere,
        out_specs=[tile, pl.BlockSpec((8, D), lambda i, kk: (0, 0))] + anywhere,
        out_shape=[jax.ShapeDtypeStruct((LP, D), F32), jax.ShapeDtypeStruct((8, D), F32)]
        + [jax.ShapeDtypeStruct(g.shape, g.dtype) for g in carried],
        scratch_shapes=([pltpu.VMEM((TM, D), F32)] if nk > 1 else []) + _exchange_sems(n_carried),
        compiler_params=_cparams("arbitrary", "arbitrary"),
    )(dz, b, x, w, resid, *carried)


def _norm_bwd_matmul(dh, x, w, b, out_dtype, name):
    n = b.shape[0]

    def body(dh_ref, x_ref, w_ref, b_ref, o_ref, dx_ref, dw_ref):
        i = pl.program_id(0)

        @pl.when(i == 0)
        def _():
            dw_ref[...] = jnp.zeros_like(dw_ref)

        row = i * TM + lax.broadcasted_iota(jnp.int32, (TM, 1), 0)
        dy = jnp.where(row >= PAD_ROWS, dh_ref[...], 0.0)
        dx, dw = _rmsnorm_bwd_rows(dy, x_ref[...], w_ref[...])
        dxb = dx.astype(BF16)
        dx_ref[...] = dxb
        dw_ref[0:1, :] += dw
        o_ref[...] = lax.dot_general(dxb, b_ref[...].astype(BF16), (((1,), (1,)), ((), ())),
                                     preferred_element_type=F32).astype(out_dtype)

    tile = pl.BlockSpec((TM, D), lambda i: (i, 0))
    return pl.pallas_call(
        body, name=name, grid=(LP // TM,),
        in_specs=[tile, tile, pl.BlockSpec((1, D), lambda i: (0, 0)), pl.BlockSpec((n, D), lambda i: (0, 0))],
        out_specs=[pl.BlockSpec((TM, n), lambda i: (i, 0)), tile, pl.BlockSpec((8, D), lambda i: (0, 0))],
        out_shape=[jax.ShapeDtypeStruct((LP, n), out_dtype), jax.ShapeDtypeStruct((LP, D), BF16),
                   jax.ShapeDtypeStruct((8, D), F32)],
        compiler_params=_cparams("arbitrary"),
    )(dh, x, w, b)


def _rmsnorm_fwd(x, w, name):
    def body(x_ref, w_ref, o_ref):
        xv = x_ref[...]
        r = lax.rsqrt(jnp.mean(xv * xv, axis=-1, keepdims=True) + EPS)
        o_ref[...] = (xv * r * w_ref[...]).astype(BF16)

    return pl.pallas_call(
        body, name=name, grid=(LP // TM,),
        in_specs=[pl.BlockSpec((TM, D), lambda i: (i, 0)), pl.BlockSpec((1, D), lambda i: (0, 0))],
        out_specs=pl.BlockSpec((TM, D), lambda i: (i, 0)),
        out_shape=jax.ShapeDtypeStruct((LP, D), BF16),
        compiler_params=_cparams("parallel"),
    )(x, w)


GELU_C = math.sqrt(2.0 / math.pi)
GELU_K = 0.044715
STRIP = 16


def _gelu_sigmoid(a):
    return pl.reciprocal(1.0 + jnp.exp(a * (a * a * (-2.0 * GELU_C * GELU_K) - 2.0 * GELU_C)), approx=True)


def _gelu_slope(a, s):
    return s + a * (s * (1.0 - s)) * (a * a * (6.0 * GELU_C * GELU_K) + 2.0 * GELU_C)


def _shift_down(x, prev8, rows):
    row = lax.broadcasted_iota(jnp.int32, (rows, 1), 0)
    p1 = pltpu.roll(prev8, 1, 0)
    p2 = pltpu.roll(prev8, 2, 0)
    x1 = jnp.where(row == 0, p1[0:1, :], pltpu.roll(x, 1, 0))
    x2 = jnp.where(row == 0, p2[0:1, :], jnp.where(row == 1, p2[1:2, :], pltpu.roll(x, 2, 0)))
    return x1, x2


def _conv_act_fwd(u, cw8, name, carried=()):
    n_rows = LP // TM
    cb2 = 2 * CONV_BLOCK
    n_carried = len(carried)

    def body(*refs):
        u_ref, cw_ref = refs[:2]
        x_refs, refs = refs[2:2 + n_carried], refs[2 + n_carried:]
        conv_ref, act_ref = refs[:2]
        gathered_refs, refs = refs[2:2 + n_carried], refs[2 + n_carried:]
        carry_ref = refs[0]
        j, i = pl.program_id(0), pl.program_id(1)
        if n_carried:
            start, forward, finish = _gather_phases(x_refs, gathered_refs, *refs[1:])
            pl.when((j == 0) & (i == 0))(start)
            pl.when((j == (3 * N_CONV_BLOCKS) // 4) & (i == 0))(forward)

        @pl.when(i == 0)
        def _():
            carry_ref[...] = jnp.zeros_like(carry_ref)

        x = u_ref[...].astype(F32)
        x1, x2 = _shift_down(x, carry_ref[...], TM)
        conv = cw_ref[3:4, :] + x2 * cw_ref[0:1, :] + x1 * cw_ref[1:2, :] + x * cw_ref[2:3, :]
        conv_ref[...] = conv.astype(BF16)
        a = conv[:, :CONV_BLOCK]
        g = conv[:, CONV_BLOCK:]
        act_ref[...] = (a * _gelu_sigmoid(a) * g).astype(BF16)
        carry_ref[...] = x[TM - 8:TM, :]
        if n_carried:
            pl.when((j == N_CONV_BLOCKS - 1) & (i == n_rows - 1))(finish)

    anywhere = [pl.BlockSpec(memory_space=pl.ANY)] * n_carried
    return pl.pallas_call(
        body, name=name, grid=(N_CONV_BLOCKS, n_rows),
        in_specs=[pl.BlockSpec((TM, cb2), lambda j, i: (i, j)), pl.BlockSpec((8, cb2), lambda j, i: (0, j))] + anywhere,
        out_specs=[pl.BlockSpec((TM, cb2), lambda j, i: (i, j)),
                   pl.BlockSpec((TM, CONV_BLOCK), lambda j, i: (i, j))] + anywhere,
        out_shape=[jax.ShapeDtypeStruct((LP, D_UP), BF16), jax.ShapeDtypeStruct((LP, D_FF), BF16)]
        + _gathered_shapes(carried),
        scratch_shapes=[pltpu.VMEM((8, cb2), F32)] + _exchange_sems(n_carried),
        compiler_params=_cparams("arbitrary", "arbitrary"),
    )(u, cw8, *carried)


def _conv_act_bwd(dact, conv, u, cw8, name):
    n_rows = LP // TM
    cb2 = 2 * CONV_BLOCK
    n_strips = TM // STRIP

    def body(dact_ref, conv_ref, u_ref, cw_ref, du_ref, dcw_ref, carry_ref):
        i = pl.program_id(1)

        @pl.when(i == 0)
        def _():
            dcw_ref[...] = jnp.zeros_like(dcw_ref)
            carry_ref[...] = jnp.zeros_like(carry_ref)

        w0, w1, w2 = cw_ref[0:1, :], cw_ref[1:2, :], cw_ref[2:3, :]
        row = lax.broadcasted_iota(jnp.int32, (STRIP, 1), 0)
        fold = lambda z: z[:8, :] + z[8:, :]

        def strip(k, carry):
            n1, n2, s0, s1, s2, s3 = carry
            r0 = pl.multiple_of((n_strips - 1 - k) * STRIP, STRIP)
            cv = conv_ref[pl.ds(r0, STRIP), :].astype(F32)
            a = cv[:, :CONV_BLOCK]
            g = cv[:, CONV_BLOCK:]
            s = _gelu_sigmoid(a)
            dav = dact_ref[pl.ds(r0, STRIP), :].astype(F32)
            dconv = jnp.concatenate([dav * g * _gelu_slope(a, s), dav * (a * s)], axis=1)
            u1 = pltpu.roll(dconv, STRIP - 1, 0)
            u2 = pltpu.roll(dconv, STRIP - 2, 0)
            d1 = jnp.where(row >= STRIP - 1, n1, u1)
            d2 = jnp.where(row >= STRIP - 2, n2, u2)
            du_ref[pl.ds(r0, STRIP), :] = (dconv * w2 + d1 * w1 + d2 * w0).astype(BF16)
            x = u_ref[pl.ds(r0, STRIP), :].astype(F32)
            return (u1, u2, s0 + fold(d2 * x), s1 + fold(d1 * x), s2 + fold(dconv * x), s3 + fold(dconv))

        below = carry_ref[...]
        zero = jnp.zeros((8, cb2), F32)
        init = (pltpu.roll(below, STRIP - 1, 0), pltpu.roll(below, STRIP - 2, 0), zero, zero, zero, zero)
        u1, _, s0, s1, s2, s3 = lax.fori_loop(0, n_strips, strip, init)
        carry_ref[...] = pltpu.roll(u1, 1, 0)
        dcw_ref[0:1, :] += jnp.sum(s0, axis=0, keepdims=True)
        dcw_ref[1:2, :] += jnp.sum(s1, axis=0, keepdims=True)
        dcw_ref[2:3, :] += jnp.sum(s2, axis=0, keepdims=True)
        dcw_ref[3:4, :] += jnp.sum(s3, axis=0, keepdims=True)

    rev = lambda j, i: (n_rows - 1 - i, j)
    return pl.pallas_call(
        body, name=name, grid=(N_CONV_BLOCKS, n_rows),
        in_specs=[pl.BlockSpec((TM, CONV_BLOCK), rev), pl.BlockSpec((TM, cb2), rev), pl.BlockSpec((TM, cb2), rev),
                  pl.BlockSpec((8, cb2), lambda j, i: (0, j))],
        out_specs=[pl.BlockSpec((TM, cb2), rev), pl.BlockSpec((8, cb2), lambda j, i: (0, j))],
        out_shape=[jax.ShapeDtypeStruct((LP, D_UP), BF16), jax.ShapeDtypeStruct((8, D_UP), F32)],
        scratch_shapes=[pltpu.VMEM((STRIP, cb2), F32)],
        compiler_params=_cparams("arbitrary", "arbitrary"),
    )(dact, conv, u, cw8)


CHUNKS_PER_STEP = 3 if N_CHUNKS % 3 == 0 else 1
STEP_ROWS = CHUNKS_PER_STEP * CHUNK
N_STEPS = N_CHUNKS // CHUNKS_PER_STEP


def _ret_consts(h):
    rows = STEP_ROWS
    lg = math.log(1.0 - 2.0 ** (-5.0 - h))
    ri = lax.broadcasted_iota(jnp.int32, (rows, rows), 0)
    ci = lax.broadcasted_iota(jnp.int32, (rows, rows), 1)
    diff = (ri - ci).astype(F32)
    dmat = jnp.where(diff >= 0, jnp.exp(lg * jnp.maximum(diff, 0.0)), 0.0)
    rowf = lax.broadcasted_iota(jnp.int32, (rows, 1), 0).astype(F32)
    zeta = jnp.exp(lg * (rows - 1.0 - rowf))
    xi = jnp.exp(lg * (rowf + 1.0))
    return dmat, zeta, xi, math.exp(lg * rows)


def _rope(t, cosv, sinv):
    return t * cosv + pltpu.roll(t, RET_DK // 2, 1) * sinv


def _unrope(d, cosv, sinv):
    return d * cosv + pltpu.roll(d * sinv, RET_DK // 2, 1)


def _gla_common(p_ref, w2_ref, gb_ref, chunk, rows):
    row = lax.broadcasted_iota(jnp.int32, (CHUNK, 1), 0)
    real = (chunk * CHUNK + row) >= PAD_ROWS
    ga = p_ref[rows, O_GA:O_GA + 128]
    z = _dot(ga, w2_ref[...]) + gb_ref[...]
    la = (jnp.minimum(z, 0.0) - jnp.log(1.0 + jnp.exp(-jnp.abs(z)))) * (1.0 / GLA_TAU)
    la = jnp.where(real, la, 0.0)
    ri = lax.broadcasted_iota(jnp.int32, (CHUNK, CHUNK), 0)
    ci = lax.broadcasted_iota(jnp.int32, (CHUNK, CHUNK), 1)
    tril = (ri >= ci).astype(F32)
    cum = _dot_exact_rhs(tril, la)
    last = cum[CHUNK - 1:CHUNK, :]
    qs = p_ref[rows, O_GQ:O_GQ + 256] * (GLA_DK ** -0.5)
    k = p_ref[rows, O_GK:O_GK + 256]
    ecum = jnp.exp(cum)
    ekl = jnp.exp(last - cum)
    el = jnp.exp(last)
    refs = [jnp.zeros((1, 256), F32)] + [cum[a * SUB - 1:a * SUB, :] for a in range(1, N_SUB)]
    eq = [jnp.exp(cum[a * SUB:(a + 1) * SUB, :] - refs[a]) for a in range(N_SUB)]
    spread = refs[0] - cum[SUB - 1:SUB, :]
    for a in range(1, N_SUB):
        spread = jnp.maximum(spread, refs[a] - cum[(a + 1) * SUB - 1:(a + 1) * SUB, :])
    small = jnp.max(spread) <= GLA_FACTORED_MAX
    return dict(real=real, row=row, z=z, la=la, cum=cum, last=last, qs=qs, k=k, ecum=ecum, ekl=ekl, el=el,
                refs=refs, eq=eq, small=small, ri=ri, ci=ci)


GLA_FACTORED_MAX = 40.0


def _head_block_mask():
    r = lax.broadcasted_iota(jnp.int32, (CHUNK, 256), 0)
    col = lax.broadcasted_iota(jnp.int32, (CHUNK, 256), 1)
    return (r // SUB) == (col // GLA_DK)


def _state_block_mask():
    r = lax.broadcasted_iota(jnp.int32, (GLA_HEADS * GLA_DK, GLA_HEADS * GLA_DV), 0)
    col = lax.broadcasted_iota(jnp.int32, (GLA_HEADS * GLA_DK, GLA_HEADS * GLA_DV), 1)
    return (r // GLA_DK) == (col // GLA_DV)


def _block_diagonal(blocks):
    zero = jnp.zeros((GLA_DK, GLA_DV), F32)
    return jnp.concatenate([jnp.concatenate([blocks[h] if g == h else zero for g in range(GLA_HEADS)], axis=1)
                            for h in range(GLA_HEADS)], axis=0)


def _gla_factored(c):
    mask = _head_block_mask()
    eks, keys, queries = [], [], []
    for a in range(N_SUB):
        ek = jnp.exp(jnp.minimum(c["refs"][a] - c["cum"], GLA_FACTORED_MAX))
        qh = c["qs"][a * SUB:(a + 1) * SUB, :] * c["eq"][a]
        eks.append(ek)
        keys.append(c["k"] * ek)
        queries.append(jnp.where(mask, jnp.concatenate([qh] * GLA_HEADS, axis=0), 0.0))
    return eks, keys, queries


def _gla_scores_factored(c, factored, p_scr):
    _, keys, queries = factored
    for a in range(N_SUB):
        out = _dot_nt(queries[a], keys[a])
        out = jnp.where(c["ci"] <= a * SUB + (c["ri"] & (SUB - 1)), out, 0.0)
        for h in range(GLA_HEADS):
            p_scr[h, a * SUB:(a + 1) * SUB, :] = out[h * SUB:(h + 1) * SUB, :]


def _gla_intra_bwd_factored(c, factored, dps, dq_scr, dk_scr):
    eks, keys, queries = factored
    mask = _head_block_mask()
    dk = jnp.zeros((CHUNK, 256), F32)
    for a in range(N_SUB):
        dpa = jnp.concatenate([dps[h][a * SUB:(a + 1) * SUB, :] for h in range(GLA_HEADS)], axis=0)
        dq = jnp.where(mask, _dot(dpa, keys[a]), 0.0)
        dq = dq[0:SUB] + dq[SUB:2 * SUB] + dq[2 * SUB:3 * SUB] + dq[3 * SUB:4 * SUB]
        dq_scr[a * SUB:(a + 1) * SUB, :] = dq * c["eq"][a]
        dk = dk + _dot_tn(dpa, queries[a]) * eks[a]
    dk_scr[...] = dk


def _gla_lag_weights(c):
    cum, row = c["cum"], c["row"]
    out = [jnp.ones((CHUNK, 256), F32)]
    for r in range(1, SUB):
        out.append(jnp.where((row % SUB) >= r, jnp.exp(jnp.minimum(cum - pltpu.roll(cum, r, 0), 0.0)), 0.0))
    return out


def _gla_pairwise_keys(c):
    return [None] + [c["k"] * jnp.exp(jnp.minimum(c["refs"][a] - c["cum"], 0.0)) for a in range(1, N_SUB)]


def _gla_scores_pairwise(c, lag_w, keys, h):
    sl = slice(GLA_DK * h, GLA_DK * (h + 1))
    qs, k = c["qs"][:, sl], c["k"][:, sl]
    ri, ci = c["ri"], c["ci"]
    p = jnp.zeros((CHUNK, CHUNK), F32)
    for r in range(SUB):
        kr = k if r == 0 else pltpu.roll(k, r, 0)
        pr = jnp.sum(qs * kr * lag_w[r][:, sl], axis=1, keepdims=True)
        p = p + jnp.where(ci == ri - r, pr, 0.0)
    blocks = [jnp.zeros((SUB, CHUNK), F32)]
    for a in range(1, N_SUB):
        qh = qs[a * SUB:(a + 1) * SUB, :] * c["eq"][a][:, sl]
        blocks.append(jnp.where(ci[:SUB, :] < a * SUB, _dot_nt(qh, keys[a][:, sl]), 0.0))
    return p + jnp.concatenate(blocks, axis=0)


def _gla_all_scores(c, p_scr, factored):
    if factored:
        _gla_scores_factored(c, _gla_factored(c), p_scr)
    else:
        lag_w, keys = _gla_lag_weights(c), _gla_pairwise_keys(c)
        for h in range(GLA_HEADS):
            p_scr[h] = _gla_scores_pairwise(c, lag_w, keys, h)


def _either_form(chunks, run):
    small = chunks[0]["small"]
    for c in chunks[1:]:
        small = jnp.logical_and(small, c["small"])
    pl.when(small)(lambda: run(True))
    pl.when(jnp.logical_not(small))(lambda: run(False))


def _gla_intra_bwd_pairwise(c, lag_w, keys, dp, h):
    sl = slice(GLA_DK * h, GLA_DK * (h + 1))
    qs_h, k_h = c["qs"][:, sl], c["k"][:, sl]
    ri, ci = c["ri"], c["ci"]
    dq_rows = [jnp.zeros((SUB, GLA_DK), F32)]
    dk = jnp.zeros((CHUNK, GLA_DK), F32)
    for a in range(1, N_SUB):
        eq = c["eq"][a][:, sl]
        qh = qs_h[a * SUB:(a + 1) * SUB, :] * eq
        dpa = jnp.where(ci[:SUB, :] < a * SUB, dp[a * SUB:(a + 1) * SUB, :], 0.0)
        dq_rows.append(_dot(dpa, keys[a][:, sl]) * eq)
        ek = jnp.exp(jnp.minimum(c["refs"][a][:, sl] - c["cum"][:, sl], 0.0))
        dk = dk + _dot_tn(dpa, qh) * ek
    dq = jnp.concatenate(dq_rows, axis=0)
    for r in range(SUB):
        w = lag_w[r][:, sl]
        dpr = jnp.sum(jnp.where(ci == ri - r, dp, 0.0), axis=1, keepdims=True)
        kr = k_h if r == 0 else pltpu.roll(k_h, r, 0)
        dq = dq + dpr * kr * w
        back = dpr * qs_h * w
        dk = dk + (back if r == 0 else pltpu.roll(back, CHUNK - r, 0))
    return dq, dk


def _gla_all_intra_bwd(c, dps, p_scr, dq_scr, dk_scr, factored):
    if factored:
        terms = _gla_factored(c)
        _gla_scores_factored(c, terms, p_scr)
        _gla_intra_bwd_factored(c, terms, dps, dq_scr, dk_scr)
    else:
        lag_w, keys = _gla_lag_weights(c), _gla_pairwise_keys(c)
        outs = [_gla_intra_bwd_pairwise(c, lag_w, keys, dps[h], h) for h in range(GLA_HEADS)]
        for h in range(GLA_HEADS):
            p_scr[h] = _gla_scores_pairwise(c, lag_w, keys, h)
        dq_scr[...] = jnp.concatenate([o[0] for o in outs], axis=1)
        dk_scr[...] = jnp.concatenate([o[1] for o in outs], axis=1)


def _mixer_fwd(proj, cos2, sin2, w2p, gb, rnw, gnw, name, carried=()):
    n_carried = len(carried)

    def body(*refs):
        p_ref, c_ref, s_ref, w2_ref, gb_ref, rnw_ref, gnw_ref = refs[:7]
        x_refs, refs = refs[7:7 + n_carried], refs[7 + n_carried:]
        ocat_ref, mrg_ref, sr_out, sg_out = refs[:4]
        gathered_refs, refs = refs[4:4 + n_carried], refs[4 + n_carried:]
        sr, sg, p_scr = refs[:3]
        n = pl.program_id(0)
        if n_carried:
            start, forward, finish = _gather_phases(x_refs, gathered_refs, *refs[3:])
            pl.when(n == 0)(start)
            pl.when(n == (3 * N_STEPS) // 4)(forward)

        @pl.when(n == 0)
        def _():
            sr[...] = jnp.zeros_like(sr)
            sg[...] = jnp.zeros_like(sg)

        sr_out[0] = sr[...]
        cosv, sinv = c_ref[...], s_ref[...]

        for h in range(RET_HEADS):
            dmat, zeta, xi, gc = _ret_consts(h)
            hs = slice(128 * h, 128 * (h + 1))
            q = _rope(p_ref[:, O_RQ + 128 * h:O_RQ + 128 * (h + 1)], cosv, sinv)
            k = _rope(p_ref[:, O_RK + 128 * h:O_RK + 128 * (h + 1)], cosv, sinv) * (RET_DK ** -0.5)
            v = p_ref[:, O_RV + 128 * h:O_RV + 128 * (h + 1)]
            g = p_ref[:, O_RG + 128 * h:O_RG + 128 * (h + 1)]
            s_in = sr[h]
            a = _dot_nt(q, k) * dmat
            o = _dot(a, v) + _dot(q, s_in) * xi
            sr[h] = gc * s_in + _dot_tn(k * zeta, v)
            mu = jnp.mean(o, axis=-1, keepdims=True)
            xc = o - mu
            nrm = xc * lax.rsqrt(jnp.mean(xc * xc, axis=-1, keepdims=True) + EPS)
            ocat_ref[:, hs] = o
            mrg_ref[:, hs] = (nrm * rnw_ref[:, hs] * (g * _sigmoid(g))).astype(BF16)

        row_slices = [slice(CHUNK * j, CHUNK * (j + 1)) for j in range(CHUNKS_PER_STEP)]
        chunks = [_gla_common(p_ref, w2_ref, gb_ref, n * CHUNKS_PER_STEP + j, rows)
                  for j, rows in enumerate(row_slices)]

        def gla_chunks(factored):
            own = _state_block_mask()
            for j, (rows, c) in enumerate(zip(row_slices, chunks)):
                s_in = sg[...]
                for h in range(GLA_HEADS):
                    sg_out[j, h] = s_in[GLA_DK * h:GLA_DK * (h + 1), GLA_DV * h:GLA_DV * (h + 1)]
                _gla_all_scores(c, p_scr.at[j], factored)
                v_all = p_ref[rows, O_GV:O_GV + GLA_HEADS * GLA_DV]
                o_inter = _dot(c["qs"] * c["ecum"], s_in)
                decay = jnp.exp(_dot_tn_exact_lhs(c["la"], jnp.ones((CHUNK, GLA_HEADS * GLA_DV), F32)))
                sg[...] = decay * s_in + jnp.where(own, _dot_tn(c["k"] * c["ekl"], v_all), 0.0)
                o_intra = _dot(p_scr[j].reshape(GLA_HEADS * CHUNK, CHUNK), v_all)
                for h in range(GLA_HEADS):
                    hs = slice(512 + 128 * h, 512 + 128 * (h + 1))
                    g = p_ref[rows, O_GR + 128 * h:O_GR + 128 * (h + 1)]
                    o = (o_intra[CHUNK * h:CHUNK * (h + 1), GLA_DV * h:GLA_DV * (h + 1)]
                         + o_inter[:, GLA_DV * h:GLA_DV * (h + 1)])
                    nrm = o * lax.rsqrt(jnp.mean(o * o, axis=-1, keepdims=True) + EPS)
                    ocat_ref[rows, hs] = o
                    mrg_ref[rows, hs] = (nrm * gnw_ref[:, 128 * h:128 * (h + 1)] * (g * _sigmoid(g))).astype(BF16)

        _either_form(chunks, gla_chunks)

        if n_carried:
            pl.when(n == N_STEPS - 1)(finish)

    const = lambda shape: pl.BlockSpec(shape, lambda n: (0,) * len(shape))
    anywhere = [pl.BlockSpec(memory_space=pl.ANY)] * n_carried
    return pl.pallas_call(
        body, name=name, grid=(N_STEPS,),
        in_specs=[pl.BlockSpec((STEP_ROWS, IN_WP), lambda n: (n, 0)),
                  pl.BlockSpec((STEP_ROWS, 128), lambda n: (n, 0)), pl.BlockSpec((STEP_ROWS, 128), lambda n: (n, 0)),
                  const((128, 256)), const((1, 256)), const((1, 512)), const((1, 512))] + anywhere,
        out_specs=[pl.BlockSpec((STEP_ROWS, D), lambda n: (n, 0)), pl.BlockSpec((STEP_ROWS, D), lambda n: (n, 0)),
                   pl.BlockSpec((1, RET_HEADS, RET_DK, 128), lambda n: (n, 0, 0, 0)),
                   pl.BlockSpec((CHUNKS_PER_STEP, GLA_HEADS, GLA_DK, GLA_DV), lambda n: (n, 0, 0, 0))] + anywhere,
        out_shape=[jax.ShapeDtypeStruct((LP, D), F32), jax.ShapeDtypeStruct((LP, D), BF16),
                   jax.ShapeDtypeStruct((N_STEPS, RET_HEADS, RET_DK, 128), F32),
                   jax.ShapeDtypeStruct((N_CHUNKS, GLA_HEADS, GLA_DK, GLA_DV), F32)] + _gathered_shapes(carried),
        scratch_shapes=[pltpu.VMEM((RET_HEADS, RET_DK, 128), F32),
                        pltpu.VMEM((GLA_HEADS * GLA_DK, GLA_HEADS * GLA_DV), F32),
                        pltpu.VMEM((CHUNKS_PER_STEP, GLA_HEADS, CHUNK, CHUNK), F32)] + _exchange_sems(n_carried),
        compiler_params=_cparams("arbitrary"),
    )(proj, cos2, sin2, w2p, gb, rnw, gnw, *carried)


def _mixer_bwd(proj, ocat, dmrg, sr_all, sg_all, cos2, sin2, w2p, gb, rnw, gnw, name, carried=()):
    last_step = N_STEPS - 1
    n_carried = len(carried)

    def body(*refs):
        p_ref, ocat_ref, dm_ref, sr_ref, sg_ref, c_ref, s_ref, w2_ref, gb_ref, rnw_ref, gnw_ref = refs[:11]
        g_refs, refs = refs[11:11 + n_carried], refs[11 + n_carried:]
        dp_ref, dw2_ref, dgb_ref, drn_ref, dgn_ref = refs[:5]
        got_refs, refs = refs[5:5 + n_carried], refs[5 + n_carried:]
        dsr, dsg, p_scr, dq_scr, dk_scr = refs[:5]
        step = pl.program_id(0)
        n = last_step - step
        if n_carried:
            start, finish = _exchange_phases(g_refs, got_refs, *refs[5:])
            pl.when(step == 0)(start)

        @pl.when(step == 0)
        def _():
            dsr[...] = jnp.zeros_like(dsr)
            dsg[...] = jnp.zeros_like(dsg)
            dw2_ref[...] = jnp.zeros_like(dw2_ref)
            dgb_ref[...] = jnp.zeros_like(dgb_ref)
            drn_ref[...] = jnp.zeros_like(drn_ref)
            dgn_ref[...] = jnp.zeros_like(dgn_ref)

        cosv, sinv = c_ref[...], s_ref[...]
        step_row = lax.broadcasted_iota(jnp.int32, (STEP_ROWS, 1), 0)
        real = ((n * STEP_ROWS + step_row) >= PAD_ROWS).astype(F32)

        for h in range(RET_HEADS):
            dmat, zeta, xi, gc = _ret_consts(h)
            hs = slice(128 * h, 128 * (h + 1))
            q = _rope(p_ref[:, O_RQ + 128 * h:O_RQ + 128 * (h + 1)], cosv, sinv)
            k = _rope(p_ref[:, O_RK + 128 * h:O_RK + 128 * (h + 1)], cosv, sinv) * (RET_DK ** -0.5)
            v = p_ref[:, O_RV + 128 * h:O_RV + 128 * (h + 1)]
            g = p_ref[:, O_RG + 128 * h:O_RG + 128 * (h + 1)]
            o = ocat_ref[:, hs]
            dy = dm_ref[:, hs]
            wv = rnw_ref[:, hs]
            mu = jnp.mean(o, axis=-1, keepdims=True)
            xc = o - mu
            rs = lax.rsqrt(jnp.mean(xc * xc, axis=-1, keepdims=True) + EPS)
            nrm = xc * rs
            sgm = _sigmoid(g)
            sil = g * sgm
            drn_ref[0:1, hs] += jnp.sum(dy * nrm * sil, axis=0, keepdims=True)
            dgate = dy * nrm * wv * (sgm * (1.0 + g * (1.0 - sgm)))
            dn = dy * wv * sil
            do = rs * (dn - jnp.mean(dn, axis=-1, keepdims=True) - nrm * jnp.mean(dn * nrm, axis=-1, keepdims=True))
            s_in = sr_ref[0, h]
            ds_out = dsr[h]
            a = _dot_nt(q, k) * dmat
            da = _dot_nt(do, v) * dmat
            dox = do * xi
            dq = _dot(da, k) + _dot_nt(dox, s_in)
            dk = _dot_tn(da, q) + _dot_nt(v, ds_out) * zeta
            dv = _dot_tn(a, do) + _dot(k * zeta, ds_out)
            dsr[h] = gc * ds_out + _dot_tn(q, dox)
            dk = dk * (RET_DK ** -0.5)
            dp_ref[:, O_RQ + 128 * h:O_RQ + 128 * (h + 1)] = (_unrope(dq, cosv, sinv) * real).astype(BF16)
            dp_ref[:, O_RK + 128 * h:O_RK + 128 * (h + 1)] = (_unrope(dk, cosv, sinv) * real).astype(BF16)
            dp_ref[:, O_RV + 128 * h:O_RV + 128 * (h + 1)] = (dv * real).astype(BF16)
            dp_ref[:, O_RG + 128 * h:O_RG + 128 * (h + 1)] = (dgate * real).astype(BF16)

        row_slices = [slice(CHUNK * j, CHUNK * (j + 1)) for j in range(CHUNKS_PER_STEP)]
        chunks = [_gla_common(p_ref, w2_ref, gb_ref, n * CHUNKS_PER_STEP + j, rows)
                  for j, rows in enumerate(row_slices)]

        def gla_chunks(factored):
            for j in reversed(range(CHUNKS_PER_STEP)):
                gla_chunk_bwd(chunks[j], n * CHUNKS_PER_STEP + j, row_slices[j], j, factored, p_ref, ocat_ref, dm_ref,
                              sg_ref, w2_ref, gnw_ref, dp_ref, dw2_ref, dgb_ref, dgn_ref, dsg, p_scr, dq_scr, dk_scr)

        _either_form(chunks, gla_chunks)
        if n_carried:
            pl.when(step == last_step)(finish)

    def gla_chunk_bwd(c, chunk, rows, j, factored, p_ref, ocat_ref, dm_ref, sg_ref, w2_ref, gnw_ref,
                      dp_ref, dw2_ref, dgb_ref, dgn_ref, dsg, p_scr, dq_scr, dk_scr):
        row = lax.broadcasted_iota(jnp.int32, (CHUNK, 1), 0)
        real = ((chunk * CHUNK + row) >= PAD_ROWS).astype(F32)
        ri, ci = c["ri"], c["ci"]
        causal = ri >= ci
        triu = (ci >= ri).astype(F32)
        qe = c["qs"] * c["ecum"]
        kl = c["k"] * c["ekl"]
        v_all = p_ref[rows, O_GV:O_GV + GLA_HEADS * GLA_DV]
        dos, dps = [], []
        for h in range(GLA_HEADS):
            hs = slice(512 + 128 * h, 512 + 128 * (h + 1))
            v = v_all[:, GLA_DV * h:GLA_DV * (h + 1)]
            g = p_ref[rows, O_GR + 128 * h:O_GR + 128 * (h + 1)]
            o = ocat_ref[rows, hs]
            dy = dm_ref[rows, hs]
            wv = gnw_ref[:, 128 * h:128 * (h + 1)]
            rs = lax.rsqrt(jnp.mean(o * o, axis=-1, keepdims=True) + EPS)
            nrm = o * rs
            sgm = _sigmoid(g)
            sil = g * sgm
            dgn_ref[0:1, 128 * h:128 * (h + 1)] += jnp.sum(dy * nrm * sil, axis=0, keepdims=True)
            dgate = dy * nrm * wv * (sgm * (1.0 + g * (1.0 - sgm)))
            dn = dy * wv * sil
            do = rs * (dn - nrm * jnp.mean(dn * nrm, axis=-1, keepdims=True))
            dp_ref[rows, O_GR + 128 * h:O_GR + 128 * (h + 1)] = (dgate * real).astype(BF16)
            dos.append(do)
        do_all = jnp.concatenate(dos, axis=1)
        do_blocks = jnp.where(_state_block_mask(), jnp.concatenate([do_all] * GLA_HEADS, axis=0), 0.0)
        dp_all = _dot_nt(do_blocks, v_all)
        dps = [jnp.where(causal, dp_all[CHUNK * h:CHUNK * (h + 1), :], 0.0) for h in range(GLA_HEADS)]
        _gla_all_intra_bwd(c, dps, p_scr.at[j], dq_scr.at[j], dk_scr.at[j], factored)
        s_in = _block_diagonal([sg_ref[j, h] for h in range(GLA_HEADS)])
        ds_out = dsg[...]
        decay = jnp.exp(_dot_tn_exact_lhs(c["la"], jnp.ones((CHUNK, GLA_HEADS * GLA_DV), F32)))
        dv_state = _dot(kl, ds_out)
        dqe = _dot_nt(do_all, s_in)
        dkl = _dot_nt(v_all, ds_out)
        dsg[...] = jnp.where(_state_block_mask(), _dot_tn(qe, do_all), 0.0) + decay * ds_out
        sd = s_in * ds_out
        sd_hi = sd.astype(BF16)
        sd_lo = (sd - sd_hi.astype(F32)).astype(BF16)
        ones8 = jnp.ones((8, GLA_HEADS * GLA_DV), BF16)
        nt = (((1,), (1,)), ((), ()))
        d_el = (lax.dot_general(ones8, sd_hi, nt, preferred_element_type=F32)
                + lax.dot_general(ones8, sd_lo, nt, preferred_element_type=F32))[0:1, :]
        dqs = dqe * c["ecum"] + dq_scr[j]
        dkk = dkl * c["ekl"] + dk_scr[j]
        d_last = jnp.sum(dkl * kl, axis=0, keepdims=True) + d_el * c["el"]
        dcum = c["qs"] * dqs - c["k"] * dkk + jnp.where(row == CHUNK - 1, d_last, 0.0)
        dla = _dot_exact_rhs(triu, dcum)
        dv = _dot_tn(p_scr[j].reshape(GLA_HEADS * CHUNK, CHUNK), do_blocks) + dv_state
        dp_ref[rows, O_GV:O_GV + GLA_HEADS * GLA_DV] = (dv * real).astype(BF16)
        dp_ref[rows, O_GQ:O_GQ + 256] = (dqs * (GLA_DK ** -0.5) * real).astype(BF16)
        dp_ref[rows, O_GK:O_GK + 256] = (dkk * real).astype(BF16)
        dz = dla * (1.0 / GLA_TAU) * _sigmoid(-c["z"]) * real
        ga = p_ref[rows, O_GA:O_GA + 128]
        dp_ref[rows, O_GA:O_GA + 128] = _dot_nt(dz, w2_ref[...]).astype(BF16)
        dp_ref[rows, O_GA + 128:IN_WP] = jnp.zeros((CHUNK, IN_WP - O_GA - 128), BF16)
        dw2_ref[...] += _dot_tn(ga, dz)
        dgb_ref[0:1, :] += jnp.sum(dz, axis=0, keepdims=True)

    const = lambda shape: pl.BlockSpec(shape, lambda s: (0,) * len(shape))
    rev = lambda s: (last_step - s, 0)
    anywhere = [pl.BlockSpec(memory_space=pl.ANY)] * n_carried
    return pl.pallas_call(
        body, name=name, grid=(N_STEPS,),
        in_specs=[pl.BlockSpec((STEP_ROWS, IN_WP), rev), pl.BlockSpec((STEP_ROWS, D), rev),
                  pl.BlockSpec((STEP_ROWS, D), rev),
                  pl.BlockSpec((1, RET_HEADS, RET_DK, 128), lambda s: (last_step - s, 0, 0, 0)),
                  pl.BlockSpec((CHUNKS_PER_STEP, GLA_HEADS, GLA_DK, GLA_DV), lambda s: (last_step - s, 0, 0, 0)),
                  pl.BlockSpec((STEP_ROWS, 128), rev), pl.BlockSpec((STEP_ROWS, 128), rev),
                  const((128, 256)), const((1, 256)), const((1, 512)), const((1, 512))] + anywhere,
        out_specs=[pl.BlockSpec((STEP_ROWS, IN_WP), rev), const((128, 256)), const((8, 256)),
                   const((8, 512)), const((8, 512))] + anywhere,
        out_shape=[jax.ShapeDtypeStruct((LP, IN_WP), BF16), jax.ShapeDtypeStruct((128, 256), F32),
                   jax.ShapeDtypeStruct((8, 256), F32), jax.ShapeDtypeStruct((8, 512), F32),
                   jax.ShapeDtypeStruct((8, 512), F32)] + [jax.ShapeDtypeStruct(g.shape, g.dtype) for g in carried],
        scratch_shapes=[pltpu.VMEM((RET_HEADS, RET_DK, 128), F32),
                        pltpu.VMEM((GLA_HEADS * GLA_DK, GLA_HEADS * GLA_DV), F32),
                        pltpu.VMEM((CHUNKS_PER_STEP, GLA_HEADS, CHUNK, CHUNK), F32),
                        pltpu.VMEM((CHUNKS_PER_STEP, CHUNK, 256), F32),
                        pltpu.VMEM((CHUNKS_PER_STEP, CHUNK, 256), F32)] + _exchange_sems(n_carried),
        compiler_params=_cparams("arbitrary"),
    )(proj, ocat, dmrg, sr_all, sg_all, cos2, sin2, w2p, gb, rnw, gnw, *carried)


def _all_gather(xs, name):
    n = len(xs)

    def body(*refs):
        start, forward, finish = _gather_phases(refs[:n], refs[n:2 * n], *refs[2 * n:])
        start()
        forward()
        finish()

    return pl.pallas_call(
        body, name=name,
        in_specs=[pl.BlockSpec(memory_space=pl.ANY)] * n,
        out_specs=[pl.BlockSpec(memory_space=pl.ANY)] * n,
        out_shape=_gathered_shapes(xs),
        scratch_shapes=_exchange_sems(n),
    )(*xs)


def _gathered_shapes(xs):
    return [jax.ShapeDtypeStruct((N_DEV,) + x.shape, x.dtype) for x in xs]


def _exchange_sems(n):
    if n == 0:
        return []
    return [pltpu.SemaphoreType.DMA((7 * n,)), pltpu.SemaphoreType.DMA((7 * n,)), pltpu.SemaphoreType.DMA((n,))]


def _gather_phases(x_refs, out_refs, send_sems, recv_sems, local_sems):
    n = len(x_refs)
    mx, my, mc = lax.axis_index("x"), lax.axis_index("y"), lax.axis_index("c")
    me, sibling = (mx, my, mc), (mx, my, 1 - mc)
    chips = [(1 - mx, my), (mx, 1 - my), (1 - mx, 1 - my)]

    def slot(a, px, py, pc):
        return out_refs[a].at[4 * px + 2 * py + pc]

    def copy(a, k, block, to, src=None):
        return pltpu.make_async_remote_copy(
            src_ref=slot(a, *block) if src is None else src, dst_ref=slot(a, *block),
            send_sem=send_sems.at[7 * a + k], recv_sem=recv_sems.at[7 * a + k],
            device_id=to, device_id_type=MESH_IDS)

    mine = [pltpu.make_async_copy(x_refs[a], slot(a, *me), local_sems.at[a]) for a in range(n)]
    first = []
    for a in range(n):
        first.append(copy(a, 0, me, sibling, src=x_refs[a]))
        first += [copy(a, 1 + j, me, (*chip, mc), src=x_refs[a]) for j, chip in enumerate(chips)]
    passed = [copy(a, 4 + j, (*chip, mc), sibling) for j, chip in enumerate(chips) for a in range(n)]

    def start():
        for cp in mine + first:
            cp.start()

    def forward():
        for j, chip in enumerate(chips):
            for a in range(n):
                copy(a, 1 + j, (*chip, mc), me).wait_recv()
                passed[j * n + a].start()

    def finish():
        for a in range(n):
            copy(a, 0, sibling, me).wait_recv()
            for j, chip in enumerate(chips):
                copy(a, 4 + j, (*chip, 1 - mc), me).wait_recv()
        for cp in first + passed:
            cp.wait_send()
        for cp in mine:
            cp.wait()

    return start, forward, finish


def _exchange_blocks(gs, name):
    n = len(gs)

    def body(*refs):
        start, finish = _exchange_phases(refs[:n], refs[n:2 * n], *refs[2 * n:])
        start()
        finish()

    return pl.pallas_call(
        body, name=name,
        in_specs=[pl.BlockSpec(memory_space=pl.ANY)] * n,
        out_specs=[pl.BlockSpec(memory_space=pl.ANY)] * n,
        out_shape=[jax.ShapeDtypeStruct(g.shape, g.dtype) for g in gs],
        scratch_shapes=_exchange_sems(n),
    )(*gs)


def _exchange_phases(g_refs, out_refs, send_sems, recv_sems, local_sems):
    n = len(g_refs)
    mx, my, mc = lax.axis_index("x"), lax.axis_index("y"), lax.axis_index("c")
    me = 4 * mx + 2 * my + mc
    mine = [pltpu.make_async_copy(g_refs[a].at[me], out_refs[a].at[me], local_sems.at[a]) for a in range(n)]
    copies = []
    for r in range(1, N_DEV):
        px, py, pc = mx ^ (r >> 2), my ^ ((r >> 1) & 1), mc ^ (r & 1)
        peer = 4 * px + 2 * py + pc
        for a in range(n):
            copies.append(pltpu.make_async_remote_copy(
                src_ref=g_refs[a].at[peer], dst_ref=out_refs[a].at[me],
                send_sem=send_sems.at[7 * a + r - 1], recv_sem=recv_sems.at[7 * a + r - 1],
                device_id=(px, py, pc), device_id_type=MESH_IDS))

    def start():
        for cp in mine + copies:
            cp.start()

    def finish():
        for cp in copies:
            cp.wait_recv()
        for cp in copies:
            cp.wait_send()
        for cp in mine:
            cp.wait()

    return start, finish


IN_SHARD = IN_W // N_DEV
IN_SHARD_P = 512
UP_SHARD = D_UP // N_DEV
UP_SHARD_P = 768
RELAYOUT_ROWS = 256


def _pieces_w_in():
    return [(k, 0, IN_SHARD * k, IN_SHARD) for k in range(N_DEV)]


def _pieces_ffn_up():
    pieces = []
    for k in range(N_DEV):
        n, end = UP_SHARD * k, UP_SHARD * (k + 1)
        while n < end:
            half, r = divmod(n, D_FF)
            blk, off = divmod(r, CONV_BLOCK)
            run = min(CONV_BLOCK - off, end - n)
            pieces.append((k, n - UP_SHARD * k, 2 * CONV_BLOCK * blk + CONV_BLOCK * half + off, run))
            n += run
    return pieces


def _assemble_block(load, spans, dst_block, rows):
    lo = 128 * dst_block
    lane = lax.broadcasted_iota(jnp.int32, (1, 128), 1)
    out = jnp.zeros((rows, 128), F32)
    for key, src_off, dst_off, length in spans:
        a, b = max(lo, dst_off), min(lo + 128, dst_off + length)
        s, s_end = src_off + (a - dst_off), src_off + (b - dst_off)
        d = a
        while s < s_end:
            e = min(s_end, 128 * (s // 128 + 1))
            blk = load(key, s // 128)
            shift = (d - s) % 128
            if shift:
                blk = pltpu.roll(blk, shift, 1)
            out = jnp.where((lane >= d - lo) & (lane < d - lo + (e - s)), blk, out)
            d += e - s
            s = e
    return out


def _shards_to_cols(shards, pieces, width, name):
    _, rows, _ = shards.shape
    tr = RELAYOUT_ROWS

    def body(s_ref, o_ref):
        load = lambda k, b: s_ref[k, :, 128 * b:128 * (b + 1)].astype(F32)
        for db in range(width // 128):
            o_ref[:, 128 * db:128 * (db + 1)] = _assemble_block(load, pieces, db, tr).astype(BF16)

    return pl.pallas_call(
        body, name=name, grid=(rows // tr,),
        in_specs=[pl.BlockSpec((N_DEV, tr, shards.shape[2]), lambda i: (0, i, 0))],
        out_specs=pl.BlockSpec((tr, width), lambda i: (i, 0)),
        out_shape=jax.ShapeDtypeStruct((rows, width), BF16),
        compiler_params=_cparams("parallel"),
    )(shards)


def _cols_to_shards(full, pieces, shard_width, name):
    rows, width = full.shape
    tr = RELAYOUT_ROWS

    def body(f_ref, o_ref):
        load = lambda _, b: f_ref[:, 128 * b:128 * (b + 1)].astype(F32)
        for k in range(N_DEV):
            spans = [(None, dst_off, src_off, length) for dev, src_off, dst_off, length in pieces if dev == k]
            for db in range(shard_width // 128):
                o_ref[k, :, 128 * db:128 * (db + 1)] = _assemble_block(load, spans, db, tr).astype(BF16)

    return pl.pallas_call(
        body, name=name, grid=(rows // tr,),
        in_specs=[pl.BlockSpec((tr, width), lambda i: (i, 0))],
        out_specs=pl.BlockSpec((N_DEV, tr, shard_width), lambda i: (0, i, 0)),
        out_shape=jax.ShapeDtypeStruct((N_DEV, rows, shard_width), BF16),
        compiler_params=_cparams("parallel"),
    )(full)


def _adamw(parts, w, m, v, rows_per_step, name):
    rows, cols = w.shape
    assert rows % rows_per_step == 0 and parts.shape == (N_DEV, rows, cols)

    def body(p_ref, w_ref, m_ref, v_ref, g_ref, d_ref, nm_ref, nv_ref):
        g = p_ref[0].astype(F32)
        for j in range(1, N_DEV):
            g = g + p_ref[j].astype(F32)
        m_new = ADAM_B1 * m_ref[...] + (1.0 - ADAM_B1) * g
        v_new = ADAM_B2 * v_ref[...] + (1.0 - ADAM_B2) * (g * g)
        m_hat = m_new / (1.0 - ADAM_B1 ** ADAM_STEP)
        v_hat = v_new / (1.0 - ADAM_B2 ** ADAM_STEP)
        g_ref[...] = g
        d_ref[...] = -ADAM_LR * (m_hat / (jnp.sqrt(v_hat) + ADAM_EPS) + ADAM_WD * w_ref[...])
        nm_ref[...] = m_new
        nv_ref[...] = v_new

    tile = pl.BlockSpec((rows_per_step, cols), lambda i: (i, 0))
    shape = jax.ShapeDtypeStruct((rows, cols), F32)
    return pl.pallas_call(
        body, name=name, grid=(rows // rows_per_step,),
        in_specs=[pl.BlockSpec((N_DEV, rows_per_step, cols), lambda i: (0, i, 0)), tile, tile, tile],
        out_specs=[tile, tile, tile, tile],
        out_shape=[shape, shape, shape, shape],
        compiler_params=_cparams("parallel"),
    )(parts, w, m, v)


BIG = (("w_in", (DEPTH, D, IN_W // N_DEV), 2), ("w_out", (DEPTH, D // N_DEV, D), 1),
       ("ffn_up", (DEPTH, D, D_UP // N_DEV), 2), ("ffn_down", (DEPTH, D_FF // N_DEV, D), 1))
SMALL = (("meta_tokens", (N_META, D // N_DEV), 1), ("gla_gate_w2", (DEPTH, GATE_RANK, 256 // N_DEV), 2),
         ("ffn_conv_w", (DEPTH, 3, D_UP // N_DEV), 2))
REPL = (("pre_mix_norm", (DEPTH, D)), ("gla_gate_b", (DEPTH, 256)), ("ret_norm_w", (DEPTH, 512)),
        ("gla_norm_w", (DEPTH, 512)), ("post_mix_norm", (DEPTH, D)), ("pre_ffn_norm", (DEPTH, D)),
        ("ffn_conv_b", (DEPTH, D_UP)), ("post_ffn_norm", (DEPTH, D)))
WEIGHT_ORDER = ("meta_tokens", "pre_mix_norm", "w_in", "gla_gate_w2", "gla_gate_b", "ret_norm_w", "gla_norm_w",
                "w_out", "post_mix_norm", "pre_ffn_norm", "ffn_up", "ffn_conv_w", "ffn_conv_b", "ffn_down",
                "post_ffn_norm")


def _size(shape):
    return math.prod(shape)


def _round_up(n, mult):
    return -(-n // mult) * mult


REPL_ROWS = _round_up(-(-sum(_size(s) for _, s in REPL) // LANES), 8)
SMALL_ROWS = _round_up(-(-sum(_size(s) for _, s, _ in SMALL) // LANES), 8)


def _pack(arrays, rows, dtype):
    flat = jnp.concatenate([a.reshape(-1).astype(dtype) for a in arrays])
    return jnp.pad(flat, (0, rows * LANES - flat.shape[0])).reshape(rows, LANES)


def _unpack(buf, shapes):
    flat = buf.reshape(-1)
    out, off = [], 0
    for shape in shapes:
        out.append(flat[off:off + _size(shape)].reshape(shape))
        off += _size(shape)
    return out


def _unshard(blocks, axis):
    moved = jnp.moveaxis(blocks, 0, axis)
    shape = list(moved.shape)
    shape[axis:axis + 2] = [shape[axis] * shape[axis + 1]]
    return moved.reshape(shape)


def _to_blocks(full, axis):
    shape = list(full.shape)
    shape[axis:axis + 1] = [N_DEV, shape[axis] // N_DEV]
    return jnp.moveaxis(full.reshape(shape), axis, 0)


def _interleave_cols(w):
    lead = w.shape[:-1]
    return jnp.swapaxes(w.reshape(lead + (2, N_CONV_BLOCKS, CONV_BLOCK)), -3, -2).reshape(lead + (D_UP,))


def _deinterleave_cols(w):
    lead = w.shape[:-1]
    return jnp.swapaxes(w.reshape(lead + (N_CONV_BLOCKS, 2, CONV_BLOCK)), -3, -2).reshape(lead + (D_UP,))


def _rope_tables():
    half = RET_DK // 2
    inv = ROPE_BASE ** (-jnp.arange(half, dtype=F32) / half)
    pos = jnp.arange(LP, dtype=F32) - float(PAD_ROWS)
    ang = pos[:, None] * inv[None, :]
    c, s = jnp.cos(ang), jnp.sin(ang)
    return jnp.concatenate([c, c], axis=1), jnp.concatenate([-s, s], axis=1)


def kernel(x, meta_tokens, pre_mix_norm, w_in, gla_gate_w2, gla_gate_b, ret_norm_w, gla_norm_w, w_out, post_mix_norm, pre_ffn_norm, ffn_up, ffn_conv_w, ffn_conv_b, ffn_down, post_ffn_norm, loss_target, m_meta_tokens, m_pre_mix_norm, m_w_in, m_gla_gate_w2, m_gla_gate_b, m_ret_norm_w, m_gla_norm_w, m_w_out, m_post_mix_norm, m_pre_ffn_norm, m_ffn_up, m_ffn_conv_w, m_ffn_conv_b, m_ffn_down, m_post_ffn_norm, v_meta_tokens, v_pre_mix_norm, v_w_in, v_gla_gate_w2, v_gla_gate_b, v_ret_norm_w, v_gla_norm_w, v_w_out, v_post_mix_norm, v_pre_ffn_norm, v_ffn_up, v_ffn_conv_w, v_ffn_conv_b, v_ffn_down, v_post_ffn_norm):
    weights = dict(meta_tokens=meta_tokens, pre_mix_norm=pre_mix_norm, w_in=w_in, gla_gate_w2=gla_gate_w2,
                   gla_gate_b=gla_gate_b, ret_norm_w=ret_norm_w, gla_norm_w=gla_norm_w, w_out=w_out,
                   post_mix_norm=post_mix_norm, pre_ffn_norm=pre_ffn_norm, ffn_up=ffn_up, ffn_conv_w=ffn_conv_w,
                   ffn_conv_b=ffn_conv_b, ffn_down=ffn_down, post_ffn_norm=post_ffn_norm)
    mom1 = dict(meta_tokens=m_meta_tokens, pre_mix_norm=m_pre_mix_norm, w_in=m_w_in, gla_gate_w2=m_gla_gate_w2,
                gla_gate_b=m_gla_gate_b, ret_norm_w=m_ret_norm_w, gla_norm_w=m_gla_norm_w, w_out=m_w_out,
                post_mix_norm=m_post_mix_norm, pre_ffn_norm=m_pre_ffn_norm, ffn_up=m_ffn_up,
                ffn_conv_w=m_ffn_conv_w, ffn_conv_b=m_ffn_conv_b, ffn_down=m_ffn_down, post_ffn_norm=m_post_ffn_norm)
    mom2 = dict(meta_tokens=v_meta_tokens, pre_mix_norm=v_pre_mix_norm, w_in=v_w_in, gla_gate_w2=v_gla_gate_w2,
                gla_gate_b=v_gla_gate_b, ret_norm_w=v_ret_norm_w, gla_norm_w=v_gla_norm_w, w_out=v_w_out,
                post_mix_norm=v_post_mix_norm, pre_ffn_norm=v_pre_ffn_norm, ffn_up=v_ffn_up,
                ffn_conv_w=v_ffn_conv_w, ffn_conv_b=v_ffn_conv_b, ffn_down=v_ffn_down, post_ffn_norm=v_post_ffn_norm)

    pad_cols = lambda a, width: jnp.pad(a, ((0, 0), (0, width - a.shape[1])))
    big_names = [n for n, _, _ in BIG]
    shard = {}
    for l in range(DEPTH):
        shard[l, "w_in"] = pad_cols(w_in[l].astype(BF16), IN_SHARD_P)
        shard[l, "w_out"] = w_out[l].astype(BF16)
        shard[l, "ffn_up"] = pad_cols(ffn_up[l].astype(BF16), UP_SHARD_P)
        shard[l, "ffn_down"] = ffn_down[l].astype(BF16)
    gathered = {(0, "w_in"): _all_gather([shard[0, "w_in"]], "gather_w_in_0")[0]}
    gather_in_mixer = {l: [(l, n) for n in big_names[1:]] for l in range(DEPTH)}
    gather_in_conv = {l: [(l + 1, "w_in")] for l in range(DEPTH - 1)}
    small = _all_gather([_pack([weights[n] for n, _, _ in SMALL], SMALL_ROWS, F32)], "gather_small_weights")[0]
    small_parts = _unpack_blocks(small, [s for _, s, _ in SMALL])
    full = {n: _unshard(p, ax) for (n, _, ax), p in zip(SMALL, small_parts)}
    w2p = jnp.pad(full["gla_gate_w2"], ((0, 0), (0, 128 - GATE_RANK), (0, 0)))
    cw8 = jnp.concatenate([_interleave_cols(full["ffn_conv_w"]), _interleave_cols(ffn_conv_b)[:, None, :],
                           jnp.zeros((DEPTH, 4, D_UP), F32)], axis=1)
    cos2, sin2 = _rope_tables()

    h = jnp.concatenate([jnp.zeros((PAD_ROWS, D), F32), full["meta_tokens"], x[0]], axis=0)
    target = jnp.concatenate([jnp.zeros((CHUNK, D), F32), loss_target[0]], axis=0)
    saved, layer_w = [], []
    for l in range(DEPTH):
        lw = dict(w_in=_shards_to_cols(gathered[l, "w_in"], _pieces_w_in(), IN_WP, f"w_in_cols_{l}"))
        a1 = _rmsnorm_fwd(h, pre_mix_norm[l:l + 1], f"pre_mix_norm_{l}")
        proj = _matmul(a1, lw["w_in"], out_dtype=F32, tm=TM, tn=1280, tk=D, name=f"in_proj_{l}", n_outer=True)
        keys = gather_in_mixer.get(l, [])
        ocat, merged, sr_all, sg_all, *got = _mixer_fwd(proj, cos2, sin2, w2p[l], gla_gate_b[l:l + 1],
                                                        ret_norm_w[l:l + 1], gla_norm_w[l:l + 1], f"mixer_fwd_{l}",
                                                        carried=[shard[key] for key in keys])
        gathered.update(zip(keys, got))
        lw["w_out"] = gathered[l, "w_out"].reshape(D, D)
        lw["w_up"] = _shards_to_cols(gathered[l, "ffn_up"], _pieces_ffn_up(), D_UP, f"ffn_up_cols_{l}")
        lw["w_down"] = gathered[l, "ffn_down"].reshape(D_FF, D)
        layer_w.append(lw)
        m, h1 = _matmul_resid_norm(merged, lw["w_out"], h, post_mix_norm[l:l + 1], f"out_proj_{l}")
        a2 = _rmsnorm_fwd(h1, pre_ffn_norm[l:l + 1], f"pre_ffn_norm_{l}")
        u = _matmul(a2, lw["w_up"], out_dtype=BF16, tm=TM, tn=1408, tk=D, name=f"ffn_up_{l}", n_outer=True)
        keys = gather_in_conv.get(l, [])
        cv, act, *got = _conv_act_fwd(u, cw8[l], f"ffn_conv_act_{l}", carried=[shard[key] for key in keys])
        gathered.update(zip(keys, got))
        f, h2, *loss_acc = _matmul_resid_norm(act, lw["w_down"], h1, post_ffn_norm[l:l + 1], f"ffn_down_{l}",
                                              target=target if l == DEPTH - 1 else None)
        saved.append(dict(h=h, a1=a1, proj=proj, ocat=ocat, merged=merged, sr=sr_all, sg=sg_all, m=m, h1=h1,
                          a2=a2, u=u, cv=cv, act=act, f=f))
        h = h2

    dh = h
    loss = lax.psum(loss_acc[0][0, 0], ("x", "y", "c"))

    kinds = ("grad", "delta", "new_m", "new_v")
    grads = {n: [None] * DEPTH for n in WEIGHT_ORDER if n != "meta_tokens" and n not in big_names}
    pending, parts = [], {}
    for l in reversed(range(DEPTH)):
        s, lw = saved[l], layer_w[l]
        dact, df, g_post_ffn = _norm_bwd_matmul(dh, s["f"], post_ffn_norm[l:l + 1], lw["w_down"], BF16,
                                                f"ffn_down_dx_{l}")
        g_down = _matmul(s["act"], df, ta=True, out_dtype=BF16, tm=D_FF // 2, tn=D, tk=TK_ROWS, name=f"ffn_down_dw_{l}")
        du, dcw = _conv_act_bwd(dact, s["cv"], s["u"], cw8[l], f"ffn_conv_act_bwd_{l}")
        dh1, g_pre_ffn = _matmul_norm_bwd(du, lw["w_up"], s["h1"], pre_ffn_norm[l:l + 1], dh, 1408, f"ffn_up_dx_{l}")
        g_up = _matmul(s["a2"], du, ta=True, out_dtype=BF16, tm=D, tn=1408, tk=TK_ROWS, name=f"ffn_up_dw_{l}")
        dmerged, dm, g_post_mix = _norm_bwd_matmul(dh1, s["m"], post_mix_norm[l:l + 1], lw["w_out"], F32,
                                                   f"out_proj_dx_{l}")
        g_out = _matmul(s["merged"], dm, ta=True, out_dtype=BF16, tm=D, tn=D, tk=TK_ROWS, name=f"out_proj_dw_{l}")
        pending += [((l, "ffn_down"), g_down.reshape(N_DEV, D_FF // N_DEV, D)),
                    ((l, "ffn_up"), _cols_to_shards(g_up, _pieces_ffn_up(), UP_SHARD_P, f"ffn_up_grad_shards_{l}")),
                    ((l, "w_out"), g_out.reshape(N_DEV, D // N_DEV, D))]
        dproj, g_w2, g_gb, g_rn, g_gn, *got = _mixer_bwd(s["proj"], s["ocat"], dmerged, s["sr"], s["sg"], cos2, sin2,
                                                         w2p[l], gla_gate_b[l:l + 1], ret_norm_w[l:l + 1],
                                                         gla_norm_w[l:l + 1], f"mixer_bwd_{l}",
                                                         carried=[blocks for _, blocks in pending])
        parts.update(zip([key for key, _ in pending], got))
        g_in = _matmul(s["a1"], dproj, ta=True, out_dtype=BF16, tm=D, tn=1280, tk=TK_ROWS, name=f"in_proj_dw_{l}")
        pending = [((l, "w_in"), _cols_to_shards(g_in, _pieces_w_in(), IN_SHARD_P, f"w_in_grad_shards_{l}"))]
        now = pending if l == 0 else []
        dh, g_pre_mix, *got = _matmul_norm_bwd(dproj, lw["w_in"], s["h"], pre_mix_norm[l:l + 1], dh1, IN_WP,
                                               f"in_proj_dx_{l}", carried=[blocks for _, blocks in now])
        parts.update(zip([key for key, _ in now], got))
        pending = [] if l == 0 else pending
        grads["post_ffn_norm"][l] = g_post_ffn[0]
        grads["ffn_conv_w"][l] = _deinterleave_cols(dcw[0:3])
        grads["ffn_conv_b"][l] = _deinterleave_cols(dcw[3])
        grads["pre_ffn_norm"][l] = g_pre_ffn[0]
        grads["post_mix_norm"][l] = g_post_mix[0]
        grads["gla_gate_w2"][l] = g_w2[:GATE_RANK]
        grads["gla_gate_b"][l] = g_gb[0]
        grads["ret_norm_w"][l] = g_rn[0]
        grads["gla_norm_w"][l] = g_gn[0]
        grads["pre_mix_norm"][l] = g_pre_mix[0]
    local = {n: jnp.stack(v) for n, v in grads.items()}
    local["meta_tokens"] = dh[PAD_ROWS:CHUNK]
    grad_x = dh[CHUNK:][None]

    blocks = jnp.concatenate([_to_blocks(local[n], ax).reshape(N_DEV, -1) for n, _, ax in SMALL], axis=1)
    blocks = jnp.pad(blocks, ((0, 0), (0, SMALL_ROWS * LANES - blocks.shape[1]))).reshape(N_DEV, SMALL_ROWS, LANES)
    *got, small_grad_parts = _exchange_blocks([b for _, b in pending] + [blocks], "exchange_last_grads")
    parts.update(zip([key for key, _ in pending], got))

    widths = dict(w_in=IN_SHARD_P, w_out=D, ffn_up=UP_SHARD_P, ffn_down=D)
    steps = dict(w_in=256, w_out=D // N_DEV, ffn_up=256, ffn_down=D_FF // N_DEV // 2)
    big_out = {kind: {n: [None] * DEPTH for n in big_names} for kind in kinds}
    for l in range(DEPTH):
        for n in big_names:
            mine = [pad_cols(d[n][l], widths[n]) for d in (weights, mom1, mom2)]
            results = _adamw(parts[l, n], *mine, steps[n], f"adamw_{n}_{l}")
            for kind, r in zip(kinds, results):
                big_out[kind][n][l] = r[:, :weights[n].shape[2]]
    out = {kind: {n: jnp.stack(v) for n, v in big_out[kind].items()} for kind in kinds}
    shard_shapes = [s for _, s, _ in SMALL]
    packed = [_pack([d[n] for n, _, _ in SMALL], SMALL_ROWS, F32) for d in (weights, mom1, mom2)]
    results = _adamw(small_grad_parts, *packed, SMALL_ROWS, "adamw_small_sharded")
    for kind, buf in zip(kinds, results):
        out[kind].update(zip([n for n, _, _ in SMALL], _unpack(buf, shard_shapes)))

    repl_parts = _all_gather([_pack([local[n] for n, _ in REPL], REPL_ROWS, F32)], "gather_small_grads")[0]
    packed = [_pack([d[n] for n, _ in REPL], REPL_ROWS, F32) for d in (weights, mom1, mom2)]
    results = _adamw(repl_parts, *packed, REPL_ROWS, "adamw_replicated")
    repl_shapes = [s for _, s in REPL]
    for kind, buf in zip(kinds, results):
        out[kind].update(zip([n for n, _ in REPL], _unpack(buf, repl_shapes)))

    return (loss, grad_x, *[out["grad"][n] for n in WEIGHT_ORDER], *[out["delta"][n] for n in WEIGHT_ORDER],
            *[out["new_m"][n] for n in WEIGHT_ORDER], *[out["new_v"][n] for n in WEIGHT_ORDER])


def _unpack_blocks(gathered, shapes):
    flat = gathered.reshape(N_DEV, -1)
    out, off = [], 0
    for shape in shapes:
        out.append(flat[:, off:off + _size(shape)].reshape((N_DEV,) + shape))
        off += _size(shape)
    return out
```

```python
import math

import jax
import jax.numpy as jnp
from jax import lax
from jax.experimental import pallas as pl
from jax.experimental.pallas import tpu as pltpu

F32 = jnp.float32
BF16 = jnp.bfloat16

D = 1024
SEQ = 8192
DEPTH = 2
N_META = 16
CHUNK = 64
SUB = 16
N_SUB = CHUNK // SUB
PAD_ROWS = CHUNK - N_META
LP = SEQ + CHUNK
N_CHUNKS = LP // CHUNK
RET_HEADS = 4
RET_DK = 128
GLA_HEADS = 4
GLA_DK = 64
GLA_DV = 128
GLA_TAU = 16.0
GATE_RANK = 16
IN_W = 3600
IN_WP = 3840
D_FF = 2816
D_UP = 2 * D_FF
CONV_BLOCK = 256
N_CONV_BLOCKS = D_FF // CONV_BLOCK
ROPE_BASE = 10000.0
EPS = 1e-6
N_DEV = 8
LANES = 1024

O_RQ, O_RK, O_RV, O_RG = 0, 512, 1024, 1536
O_GQ, O_GK, O_GV, O_GR, O_GA = 2048, 2304, 2560, 3072, 3584

ADAM_LR = 0.001
ADAM_B1 = 0.9
ADAM_B2 = 0.999
ADAM_EPS = 1e-08
ADAM_WD = 0.01
ADAM_STEP = 10

VMEM_LIMIT = 56 * 1024 * 1024
MESH_IDS = pl.DeviceIdType.MESH


def _row_tile(rows, limit):
    best = 16
    for t in range(16, min(rows, limit) + 1, 16):
        if rows % t == 0:
            best = t
    return best


TM = _row_tile(LP, 688)
TK_ROWS = _row_tile(LP, 1376)


def _cparams(*sem):
    return pltpu.CompilerParams(dimension_semantics=sem, vmem_limit_bytes=VMEM_LIMIT)


def _dot(a, b):
    return jnp.dot(a.astype(BF16), b.astype(BF16), preferred_element_type=F32)


def _dot_nt(a, b):
    return lax.dot_general(a.astype(BF16), b.astype(BF16), (((1,), (1,)), ((), ())), preferred_element_type=F32)


def _dot_tn(a, b):
    return lax.dot_general(a.astype(BF16), b.astype(BF16), (((0,), (0,)), ((), ())), preferred_element_type=F32)


def _split3(x):
    hi = x.astype(BF16)
    r1 = x - hi.astype(F32)
    mid = r1.astype(BF16)
    lo = (r1 - mid.astype(F32)).astype(BF16)
    return hi, mid, lo


def _dot_exact_rhs(t, x):
    hi, mid, lo = _split3(x)
    t = t.astype(BF16)
    return (jnp.dot(t, hi, preferred_element_type=F32) + jnp.dot(t, mid, preferred_element_type=F32)
            + jnp.dot(t, lo, preferred_element_type=F32))


def _dot_tn_exact_lhs(x, ones):
    dims = (((0,), (0,)), ((), ()))
    hi, mid, lo = _split3(x)
    ones = ones.astype(BF16)
    return (lax.dot_general(hi, ones, dims, preferred_element_type=F32)
            + lax.dot_general(mid, ones, dims, preferred_element_type=F32)
            + lax.dot_general(lo, ones, dims, preferred_element_type=F32))


def _sigmoid(x):
    return 1.0 / (1.0 + jnp.exp(-x))


def _matmul(a, b, *, ta=False, tb=False, out_dtype, tm, tn, tk, name, n_outer=False):
    m = a.shape[1] if ta else a.shape[0]
    k = a.shape[0] if ta else a.shape[1]
    n = b.shape[0] if tb else b.shape[1]
    assert (b.shape[1] if tb else b.shape[0]) == k
    assert m % tm == 0 and n % tn == 0 and k % tk == 0, (name, m, n, k, tm, tn, tk)
    nk = k // tk
    order = (lambda f: (lambda j, i, kk: f(i, j, kk))) if n_outer else (lambda f: f)
    a_spec = (pl.BlockSpec((tk, tm), order(lambda i, j, kk: (kk, i))) if ta
              else pl.BlockSpec((tm, tk), order(lambda i, j, kk: (i, kk))))
    b_spec = (pl.BlockSpec((tn, tk), order(lambda i, j, kk: (j, kk))) if tb
              else pl.BlockSpec((tk, tn), order(lambda i, j, kk: (kk, j))))
    dims = (((0 if ta else 1,), (1 if tb else 0,)), ((), ()))

    def body(a_ref, b_ref, o_ref, *acc):
        prod = lax.dot_general(a_ref[...].astype(BF16), b_ref[...].astype(BF16), dims, preferred_element_type=F32)
        if nk == 1:
            o_ref[...] = prod.astype(out_dtype)
            return
        acc_ref, = acc
        kk = pl.program_id(2)

        @pl.when(kk == 0)
        def _():
            acc_ref[...] = prod

        @pl.when(kk > 0)
        def _():
            acc_ref[...] += prod

        @pl.when(kk == nk - 1)
        def _():
            o_ref[...] = acc_ref[...].astype(out_dtype)

    return pl.pallas_call(
        body, name=name, grid=(n // tn, m // tm, nk) if n_outer else (m // tm, n // tn, nk),
        in_specs=[a_spec, b_spec],
        out_specs=pl.BlockSpec((tm, tn), order(lambda i, j, kk: (i, j))),
        out_shape=jax.ShapeDtypeStruct((m, n), out_dtype),
        scratch_shapes=[pltpu.VMEM((tm, tn), F32)] if nk > 1 else [],
        compiler_params=_cparams("parallel", "parallel", "arbitrary"),
    )(a, b)


def _matmul_resid_norm(a, b, h, w, name, target=None):
    k = a.shape[1]
    has_loss = target is not None

    def body(a_ref, b_ref, h_ref, w_ref, *refs):
        m = jnp.dot(a_ref[...].astype(BF16), b_ref[...].astype(BF16), preferred_element_type=F32)
        r = lax.rsqrt(jnp.mean(m * m, axis=-1, keepdims=True) + EPS)
        i = pl.program_id(0)
        row = i * TM + lax.broadcasted_iota(jnp.int32, (TM, 1), 0)
        y = h_ref[...] + jnp.where(row >= PAD_ROWS, m * r * w_ref[...], 0.0)
        if not has_loss:
            m_ref, y_ref = refs
            m_ref[...] = m
            y_ref[...] = y
            return
        t_ref, m_ref, dy_ref, loss_ref = refs
        m_ref[...] = m

        @pl.when(i == 0)
        def _():
            loss_ref[...] = jnp.zeros_like(loss_ref)

        diff = jnp.where(row >= CHUNK, y - t_ref[...], 0.0)
        dy_ref[...] = diff * (1.0 / D)
        loss_ref[...] += (0.5 / D) * jnp.sum(diff * diff)

    tile = pl.BlockSpec((TM, D), lambda i: (i, 0))
    shape = jax.ShapeDtypeStruct((LP, D), F32)
    in_specs = [pl.BlockSpec((TM, k), lambda i: (i, 0)), pl.BlockSpec((k, D), lambda i: (0, 0)), tile,
                pl.BlockSpec((1, D), lambda i: (0, 0))]
    if has_loss:
        return pl.pallas_call(
            body, name=name, grid=(LP // TM,),
            in_specs=in_specs + [tile],
            out_specs=[tile, tile, pl.BlockSpec((8, 128), lambda i: (0, 0))],
            out_shape=[shape, shape, jax.ShapeDtypeStruct((8, 128), F32)],
            compiler_params=_cparams("arbitrary"),
        )(a, b, h, w, target)
    return pl.pallas_call(
        body, name=name, grid=(LP // TM,),
        in_specs=in_specs, out_specs=[tile, tile], out_shape=[shape, shape],
        compiler_params=_cparams("parallel"),
    )(a, b, h, w)


def _rmsnorm_bwd_rows(dy, x, w):
    r = lax.rsqrt(jnp.mean(x * x, axis=-1, keepdims=True) + EPS)
    g = dy * w
    dx = r * g - x * (r * r * r * jnp.mean(g * x, axis=-1, keepdims=True))
    return dx, jnp.sum(dy * x * r, axis=0, keepdims=True)


def _matmul_norm_bwd(dz, b, x, w, resid, tk, name, carried=()):
    k = dz.shape[1]
    assert k % tk == 0
    nk = k // tk
    n_rows = LP // TM
    n_carried = len(carried)

    def body(*refs):
        a_ref, b_ref, x_ref, w_ref, r_ref = refs[:5]
        g_refs, refs = refs[5:5 + n_carried], refs[5 + n_carried:]
        dx_ref, dw_ref = refs[:2]
        got_refs, refs = refs[2:2 + n_carried], refs[2 + n_carried:]
        acc, sems = (refs[:1], refs[1:]) if nk > 1 else ((), refs)
        i, kk = pl.program_id(0), pl.program_id(1)
        if n_carried:
            exchange_start, exchange_finish = _exchange_phases(g_refs, got_refs, *sems)
            pl.when((i == 0) & (kk == 0))(exchange_start)

        @pl.when((i == 0) & (kk == 0))
        def _():
            dw_ref[...] = jnp.zeros_like(dw_ref)

        prod = lax.dot_general(a_ref[...].astype(BF16), b_ref[...].astype(BF16), (((1,), (1,)), ((), ())),
                               preferred_element_type=F32)

        def finish(dy):
            dx, dw = _rmsnorm_bwd_rows(dy, x_ref[...], w_ref[...])
            dx_ref[...] = dx + r_ref[...]
            dw_ref[0:1, :] += dw

        if nk == 1:
            finish(prod)
        else:
            acc_ref, = acc

            @pl.when(kk == 0)
            def _():
                acc_ref[...] = prod

            @pl.when((kk > 0) & (kk < nk - 1))
            def _():
                acc_ref[...] += prod

            @pl.when(kk == nk - 1)
            def _():
                finish(acc_ref[...] + prod)

        if n_carried:
            pl.when((i == n_rows - 1) & (kk == nk - 1))(exchange_finish)

    tile = pl.BlockSpec((TM, D), lambda i, kk: (i, 0))
    anywhere = [pl.BlockSpec(memory_space=pl.ANY)] * n_carried
    return pl.pallas_call(
        body, name=name, grid=(n_rows, nk),
        in_specs=[pl.BlockSpec((TM, tk), lambda i, kk: (i, kk)), pl.BlockSpec((D, tk), lambda i, kk: (0, kk)), tile,
                  pl.BlockSpec((1, D), lambda i, kk: (0, 0)), tile] + anywhere,
        out_specs=[tile, pl.BlockSpec((8, D), lambda i, kk: (0, 0))] + anywhere,
        out_shape=[jax.ShapeDtypeStruct((LP, D), F32), jax.ShapeDtypeStruct((8, D), F32)]
        + [jax.ShapeDtypeStruct(g.shape, g.dtype) for g in carried],
        scratch_shapes=([pltpu.VMEM((TM, D), F32)] if nk > 1 else []) + _exchange_sems(n_carried),
        compiler_params=_cparams("arbitrary", "arbitrary"),
    )(dz, b, x, w, resid, *carried)


def _norm_bwd_matmul(dh, x, w, b, out_dtype, name):
    n = b.shape[0]

    def body(dh_ref, x_ref, w_ref, b_ref, o_ref, dx_ref, dw_ref):
        i = pl.program_id(0)

        @pl.when(i == 0)
        def _():
            dw_ref[...] = jnp.zeros_like(dw_ref)

        row = i * TM + lax.broadcasted_iota(jnp.int32, (TM, 1), 0)
        dy = jnp.where(row >= PAD_ROWS, dh_ref[...], 0.0)
        dx, dw = _rmsnorm_bwd_rows(dy, x_ref[...], w_ref[...])
        dxb = dx.astype(BF16)
        dx_ref[...] = dxb
        dw_ref[0:1, :] += dw
        o_ref[...] = lax.dot_general(dxb, b_ref[...].astype(BF16), (((1,), (1,)), ((), ())),
                                     preferred_element_type=F32).astype(out_dtype)

    tile = pl.BlockSpec((TM, D), lambda i: (i, 0))
    return pl.pallas_call(
        body, name=name, grid=(LP // TM,),
        in_specs=[tile, tile, pl.BlockSpec((1, D), lambda i: (0, 0)), pl.BlockSpec((n, D), lambda i: (0, 0))],
        out_specs=[pl.BlockSpec((TM, n), lambda i: (i, 0)), tile, pl.BlockSpec((8, D), lambda i: (0, 0))],
        out_shape=[jax.ShapeDtypeStruct((LP, n), out_dtype), jax.ShapeDtypeStruct((LP, D), BF16),
                   jax.ShapeDtypeStruct((8, D), F32)],
        compiler_params=_cparams("arbitrary"),
    )(dh, x, w, b)


def _rmsnorm_fwd(x, w, name):
    def body(x_ref, w_ref, o_ref):
        xv = x_ref[...]
        r = lax.rsqrt(jnp.mean(xv * xv, axis=-1, keepdims=True) + EPS)
        o_ref[...] = (xv * r * w_ref[...]).astype(BF16)

    return pl.pallas_call(
        body, name=name, grid=(LP // TM,),
        in_specs=[pl.BlockSpec((TM, D), lambda i: (i, 0)), pl.BlockSpec((1, D), lambda i: (0, 0))],
        out_specs=pl.BlockSpec((TM, D), lambda i: (i, 0)),
        out_shape=jax.ShapeDtypeStruct((LP, D), BF16),
        compiler_params=_cparams("parallel"),
    )(x, w)


GELU_C = math.sqrt(2.0 / math.pi)
GELU_K = 0.044715
STRIP = 16


def _gelu_half(a):
    return 0.5 * jnp.tanh(a * (a * a * (GELU_C * GELU_K) + GELU_C)) + 0.5


def _gelu_slope(a, h):
    return h * (1.0 + (a - a * h) * (a * a * (6.0 * GELU_C * GELU_K) + 2.0 * GELU_C))


def _shift_down(x, prev8, rows):
    row = lax.broadcasted_iota(jnp.int32, (rows, 1), 0)
    p1 = pltpu.roll(prev8, 1, 0)
    p2 = pltpu.roll(prev8, 2, 0)
    x1 = jnp.where(row == 0, p1[0:1, :], pltpu.roll(x, 1, 0))
    x2 = jnp.where(row == 0, p2[0:1, :], jnp.where(row == 1, p2[1:2, :], pltpu.roll(x, 2, 0)))
    return x1, x2


def _conv_act_fwd(u, cw8, name, carried=()):
    n_rows = LP // TM
    cb2 = 2 * CONV_BLOCK
    n_carried = len(carried)

    def body(*refs):
        u_ref, cw_ref = refs[:2]
        x_refs, refs = refs[2:2 + n_carried], refs[2 + n_carried:]
        conv_ref, act_ref = refs[:2]
        gathered_refs, refs = refs[2:2 + n_carried], refs[2 + n_carried:]
        carry_ref = refs[0]
        j, i = pl.program_id(0), pl.program_id(1)
        if n_carried:
            start, forward, finish = _gather_phases(x_refs, gathered_refs, *refs[1:])
            pl.when((j == 0) & (i == 0))(start)
            pl.when((j == (3 * N_CONV_BLOCKS) // 4) & (i == 0))(forward)

        @pl.when(i == 0)
        def _():
            carry_ref[...] = jnp.zeros_like(carry_ref)

        x = u_ref[...].astype(F32)
        x1, x2 = _shift_down(x, carry_ref[...], TM)
        conv = cw_ref[3:4, :] + x2 * cw_ref[0:1, :] + x1 * cw_ref[1:2, :] + x * cw_ref[2:3, :]
        conv_ref[...] = conv.astype(BF16)
        a = conv[:, :CONV_BLOCK]
        g = conv[:, CONV_BLOCK:]
        act_ref[...] = (a * _gelu_half(a) * g).astype(BF16)
        carry_ref[...] = x[TM - 8:TM, :]
        if n_carried:
            pl.when((j == N_CONV_BLOCKS - 1) & (i == n_rows - 1))(finish)

    anywhere = [pl.BlockSpec(memory_space=pl.ANY)] * n_carried
    return pl.pallas_call(
        body, name=name, grid=(N_CONV_BLOCKS, n_rows),
        in_specs=[pl.BlockSpec((TM, cb2), lambda j, i: (i, j)), pl.BlockSpec((8, cb2), lambda j, i: (0, j))] + anywhere,
        out_specs=[pl.BlockSpec((TM, cb2), lambda j, i: (i, j)),
                   pl.BlockSpec((TM, CONV_BLOCK), lambda j, i: (i, j))] + anywhere,
        out_shape=[jax.ShapeDtypeStruct((LP, D_UP), BF16), jax.ShapeDtypeStruct((LP, D_FF), BF16)]
        + _gathered_shapes(carried),
        scratch_shapes=[pltpu.VMEM((8, cb2), F32)] + _exchange_sems(n_carried),
        compiler_params=_cparams("arbitrary", "arbitrary"),
    )(u, cw8, *carried)


def _conv_act_bwd(dact, conv, u, cw8, name):
    n_rows = LP // TM
    cb2 = 2 * CONV_BLOCK
    n_strips = TM // STRIP

    def body(dact_ref, conv_ref, u_ref, cw_ref, du_ref, dcw_ref, carry_ref):
        i = pl.program_id(1)

        @pl.when(i == 0)
        def _():
            dcw_ref[...] = jnp.zeros_like(dcw_ref)
            carry_ref[...] = jnp.zeros_like(carry_ref)

        w0, w1, w2 = cw_ref[0:1, :], cw_ref[1:2, :], cw_ref[2:3, :]
        row = lax.broadcasted_iota(jnp.int32, (STRIP, 1), 0)
        fold = lambda z: z[:8, :] + z[8:, :]

        def strip(k, carry):
            n1, n2, s0, s1, s2, s3 = carry
            r0 = pl.multiple_of((n_strips - 1 - k) * STRIP, STRIP)
            cv = conv_ref[pl.ds(r0, STRIP), :].astype(F32)
            a = cv[:, :CONV_BLOCK]
            g = cv[:, CONV_BLOCK:]
            h = _gelu_half(a)
            dav = dact_ref[pl.ds(r0, STRIP), :].astype(F32)
            dconv = jnp.concatenate([dav * g * _gelu_slope(a, h), dav * (a * h)], axis=1)
            u1 = pltpu.roll(dconv, STRIP - 1, 0)
            u2 = pltpu.roll(dconv, STRIP - 2, 0)
            d1 = jnp.where(row >= STRIP - 1, n1, u1)
            d2 = jnp.where(row >= STRIP - 2, n2, u2)
            du_ref[pl.ds(r0, STRIP), :] = (dconv * w2 + d1 * w1 + d2 * w0).astype(BF16)
            x = u_ref[pl.ds(r0, STRIP), :].astype(F32)
            return (u1, u2, s0 + fold(d2 * x), s1 + fold(d1 * x), s2 + fold(dconv * x), s3 + fold(dconv))

        below = carry_ref[...]
        zero = jnp.zeros((8, cb2), F32)
        init = (pltpu.roll(below, STRIP - 1, 0), pltpu.roll(below, STRIP - 2, 0), zero, zero, zero, zero)
        u1, _, s0, s1, s2, s3 = lax.fori_loop(0, n_strips, strip, init)
        carry_ref[...] = pltpu.roll(u1, 1, 0)
        dcw_ref[0:1, :] += jnp.sum(s0, axis=0, keepdims=True)
        dcw_ref[1:2, :] += jnp.sum(s1, axis=0, keepdims=True)
        dcw_ref[2:3, :] += jnp.sum(s2, axis=0, keepdims=True)
        dcw_ref[3:4, :] += jnp.sum(s3, axis=0, keepdims=True)

    rev = lambda j, i: (n_rows - 1 - i, j)
    return pl.pallas_call(
        body, name=name, grid=(N_CONV_BLOCKS, n_rows),
        in_specs=[pl.BlockSpec((TM, CONV_BLOCK), rev), pl.BlockSpec((TM, cb2), rev), pl.BlockSpec((TM, cb2), rev),
                  pl.BlockSpec((8, cb2), lambda j, i: (0, j))],
        out_specs=[pl.BlockSpec((TM, cb2), rev), pl.BlockSpec((8, cb2), lambda j, i: (0, j))],
        out_shape=[jax.ShapeDtypeStruct((LP, D_UP), BF16), jax.ShapeDtypeStruct((8, D_UP), F32)],
        scratch_shapes=[pltpu.VMEM((STRIP, cb2), F32)],
        compiler_params=_cparams("arbitrary", "arbitrary"),
    )(dact, conv, u, cw8)


CHUNKS_PER_STEP = 3 if N_CHUNKS % 3 == 0 else 1
STEP_ROWS = CHUNKS_PER_STEP * CHUNK
N_STEPS = N_CHUNKS // CHUNKS_PER_STEP


def _ret_consts(h):
    rows = STEP_ROWS
    lg = math.log(1.0 - 2.0 ** (-5.0 - h))
    ri = lax.broadcasted_iota(jnp.int32, (rows, rows), 0)
    ci = lax.broadcasted_iota(jnp.int32, (rows, rows), 1)
    diff = (ri - ci).astype(F32)
    dmat = jnp.where(diff >= 0, jnp.exp(lg * jnp.maximum(diff, 0.0)), 0.0)
    rowf = lax.broadcasted_iota(jnp.int32, (rows, 1), 0).astype(F32)
    zeta = jnp.exp(lg * (rows - 1.0 - rowf))
    xi = jnp.exp(lg * (rowf + 1.0))
    return dmat, zeta, xi, math.exp(lg * rows)


def _rope(t, cosv, sinv):
    return t * cosv + pltpu.roll(t, RET_DK // 2, 1) * sinv


def _unrope(d, cosv, sinv):
    return d * cosv + pltpu.roll(d * sinv, RET_DK // 2, 1)


def _gla_masks():
    ri = lax.broadcasted_iota(jnp.int32, (CHUNK, CHUNK), 0)
    ci = lax.broadcasted_iota(jnp.int32, (CHUNK, CHUNK), 1)
    return dict(ri=ri, ci=ci, tril=(ri >= ci).astype(F32), heads=_head_block_mask(), own=_state_block_mask())


def _gla_common(p_ref, w2_ref, gb_ref, chunk, rows, masks):
    row = lax.broadcasted_iota(jnp.int32, (CHUNK, 1), 0)
    real = (chunk * CHUNK + row) >= PAD_ROWS
    ga = p_ref[rows, O_GA:O_GA + 128]
    z = _dot(ga, w2_ref[...]) + gb_ref[...]
    la = (jnp.minimum(z, 0.0) - jnp.log(1.0 + jnp.exp(-jnp.abs(z)))) * (1.0 / GLA_TAU)
    la = jnp.where(real, la, 0.0)
    ri, ci = masks["ri"], masks["ci"]
    cum = _dot_exact_rhs(masks["tril"], la)
    last = cum[CHUNK - 1:CHUNK, :]
    qs = p_ref[rows, O_GQ:O_GQ + 256] * (GLA_DK ** -0.5)
    k = p_ref[rows, O_GK:O_GK + 256]
    ecum = jnp.exp(cum)
    ekl = jnp.exp(last - cum)
    el = jnp.exp(last)
    refs = [jnp.zeros((1, 256), F32)] + [cum[a * SUB - 1:a * SUB, :] for a in range(1, N_SUB)]
    eq = [jnp.exp(cum[a * SUB:(a + 1) * SUB, :] - refs[a]) for a in range(N_SUB)]
    spread = refs[0] - cum[SUB - 1:SUB, :]
    for a in range(1, N_SUB):
        spread = jnp.maximum(spread, refs[a] - cum[(a + 1) * SUB - 1:(a + 1) * SUB, :])
    small = jnp.max(spread) <= GLA_FACTORED_MAX
    return dict(real=real, row=row, z=z, la=la, cum=cum, last=last, qs=qs, k=k, ecum=ecum, ekl=ekl, el=el,
                refs=refs, eq=eq, small=small, ri=ri, ci=ci, masks=masks)


GLA_FACTORED_MAX = 40.0


def _head_block_mask():
    r = lax.broadcasted_iota(jnp.int32, (CHUNK, 256), 0)
    col = lax.broadcasted_iota(jnp.int32, (CHUNK, 256), 1)
    return (r // SUB) == (col // GLA_DK)


def _state_block_mask():
    r = lax.broadcasted_iota(jnp.int32, (GLA_HEADS * GLA_DK, GLA_HEADS * GLA_DV), 0)
    col = lax.broadcasted_iota(jnp.int32, (GLA_HEADS * GLA_DK, GLA_HEADS * GLA_DV), 1)
    return (r // GLA_DK) == (col // GLA_DV)


def _block_diagonal(blocks):
    zero = jnp.zeros((GLA_DK, GLA_DV), F32)
    return jnp.concatenate([jnp.concatenate([blocks[h] if g == h else zero for g in range(GLA_HEADS)], axis=1)
                            for h in range(GLA_HEADS)], axis=0)


def _gla_factored(c):
    mask = c["masks"]["heads"]
    eks, keys, queries = [], [], []
    for a in range(N_SUB):
        ek = jnp.exp(jnp.minimum(c["refs"][a] - c["cum"], GLA_FACTORED_MAX))
        qh = c["qs"][a * SUB:(a + 1) * SUB, :] * c["eq"][a]
        eks.append(ek)
        keys.append(c["k"] * ek)
        queries.append(jnp.where(mask, jnp.concatenate([qh] * GLA_HEADS, axis=0), 0.0))
    return eks, keys, queries


def _gla_scores_factored(c, factored, p_scr):
    _, keys, queries = factored
    for a in range(N_SUB):
        out = _dot_nt(queries[a], keys[a])
        out = jnp.where(c["ci"] <= a * SUB + (c["ri"] & (SUB - 1)), out, 0.0)
        for h in range(GLA_HEADS):
            p_scr[h, a * SUB:(a + 1) * SUB, :] = out[h * SUB:(h + 1) * SUB, :]


def _gla_intra_bwd_factored(c, factored, dps, dq_scr, dk_scr):
    eks, keys, queries = factored
    mask = c["masks"]["heads"]
    dk = jnp.zeros((CHUNK, 256), F32)
    for a in range(N_SUB):
        dpa = jnp.concatenate([dps[h][a * SUB:(a + 1) * SUB, :] for h in range(GLA_HEADS)], axis=0)
        dq = jnp.where(mask, _dot(dpa, keys[a]), 0.0)
        dq = dq[0:SUB] + dq[SUB:2 * SUB] + dq[2 * SUB:3 * SUB] + dq[3 * SUB:4 * SUB]
        dq_scr[a * SUB:(a + 1) * SUB, :] = dq * c["eq"][a]
        dk = dk + _dot_tn(dpa, queries[a]) * eks[a]
    dk_scr[...] = dk


def _gla_lag_weights(c):
    cum, row = c["cum"], c["row"]
    out = [jnp.ones((CHUNK, 256), F32)]
    for r in range(1, SUB):
        out.append(jnp.where((row % SUB) >= r, jnp.exp(jnp.minimum(cum - pltpu.roll(cum, r, 0), 0.0)), 0.0))
    return out


def _gla_pairwise_keys(c):
    return [None] + [c["k"] * jnp.exp(jnp.minimum(c["refs"][a] - c["cum"], 0.0)) for a in range(1, N_SUB)]


def _gla_scores_pairwise(c, lag_w, keys, h):
    sl = slice(GLA_DK * h, GLA_DK * (h + 1))
    qs, k = c["qs"][:, sl], c["k"][:, sl]
    ri, ci = c["ri"], c["ci"]
    p = jnp.zeros((CHUNK, CHUNK), F32)
    for r in range(SUB):
        kr = k if r == 0 else pltpu.roll(k, r, 0)
        pr = jnp.sum(qs * kr * lag_w[r][:, sl], axis=1, keepdims=True)
        p = p + jnp.where(ci == ri - r, pr, 0.0)
    blocks = [jnp.zeros((SUB, CHUNK), F32)]
    for a in range(1, N_SUB):
        qh = qs[a * SUB:(a + 1) * SUB, :] * c["eq"][a][:, sl]
        blocks.append(jnp.where(ci[:SUB, :] < a * SUB, _dot_nt(qh, keys[a][:, sl]), 0.0))
    return p + jnp.concatenate(blocks, axis=0)


def _gla_all_scores(c, p_scr, factored):
    if factored:
        _gla_scores_factored(c, _gla_factored(c), p_scr)
    else:
        lag_w, keys = _gla_lag_weights(c), _gla_pairwise_keys(c)
        for h in range(GLA_HEADS):
            p_scr[h] = _gla_scores_pairwise(c, lag_w, keys, h)


def _either_form(chunks, run):
    small = chunks[0]["small"]
    for c in chunks[1:]:
        small = jnp.logical_and(small, c["small"])
    pl.when(small)(lambda: run(True))
    pl.when(jnp.logical_not(small))(lambda: run(False))


def _gla_intra_bwd_pairwise(c, lag_w, keys, dp, h):
    sl = slice(GLA_DK * h, GLA_DK * (h + 1))
    qs_h, k_h = c["qs"][:, sl], c["k"][:, sl]
    ri, ci = c["ri"], c["ci"]
    dq_rows = [jnp.zeros((SUB, GLA_DK), F32)]
    dk = jnp.zeros((CHUNK, GLA_DK), F32)
    for a in range(1, N_SUB):
        eq = c["eq"][a][:, sl]
        qh = qs_h[a * SUB:(a + 1) * SUB, :] * eq
        dpa = jnp.where(ci[:SUB, :] < a * SUB, dp[a * SUB:(a + 1) * SUB, :], 0.0)
        dq_rows.append(_dot(dpa, keys[a][:, sl]) * eq)
        ek = jnp.exp(jnp.minimum(c["refs"][a][:, sl] - c["cum"][:, sl], 0.0))
        dk = dk + _dot_tn(dpa, qh) * ek
    dq = jnp.concatenate(dq_rows, axis=0)
    for r in range(SUB):
        w = lag_w[r][:, sl]
        dpr = jnp.sum(jnp.where(ci == ri - r, dp, 0.0), axis=1, keepdims=True)
        kr = k_h if r == 0 else pltpu.roll(k_h, r, 0)
        dq = dq + dpr * kr * w
        back = dpr * qs_h * w
        dk = dk + (back if r == 0 else pltpu.roll(back, CHUNK - r, 0))
    return dq, dk


def _gla_all_intra_bwd(c, dps, p_scr, dq_scr, dk_scr, factored):
    if factored:
        terms = _gla_factored(c)
        _gla_scores_factored(c, terms, p_scr)
        _gla_intra_bwd_factored(c, terms, dps, dq_scr, dk_scr)
    else:
        lag_w, keys = _gla_lag_weights(c), _gla_pairwise_keys(c)
        outs = [_gla_intra_bwd_pairwise(c, lag_w, keys, dps[h], h) for h in range(GLA_HEADS)]
        for h in range(GLA_HEADS):
            p_scr[h] = _gla_scores_pairwise(c, lag_w, keys, h)
        dq_scr[...] = jnp.concatenate([o[0] for o in outs], axis=1)
        dk_scr[...] = jnp.concatenate([o[1] for o in outs], axis=1)


def _mixer_fwd(proj, cos2, sin2, w2p, gb, rnw, gnw, name, carried=()):
    n_carried = len(carried)

    def body(*refs):
        p_ref, c_ref, s_ref, w2_ref, gb_ref, rnw_ref, gnw_ref = refs[:7]
        x_refs, refs = refs[7:7 + n_carried], refs[7 + n_carried:]
        ocat_ref, mrg_ref, sr_out, sg_out = refs[:4]
        gathered_refs, refs = refs[4:4 + n_carried], refs[4 + n_carried:]
        sr, sg, p_scr = refs[:3]
        n = pl.program_id(0)
        if n_carried:
            start, forward, finish = _gather_phases(x_refs, gathered_refs, *refs[3:])
            pl.when(n == 0)(start)
            pl.when(n == (3 * N_STEPS) // 4)(forward)

        @pl.when(n == 0)
        def _():
            sr[...] = jnp.zeros_like(sr)
            sg[...] = jnp.zeros_like(sg)

        sr_out[0] = sr[...]
        cosv, sinv = c_ref[...], s_ref[...]

        for h in range(RET_HEADS):
            dmat, zeta, xi, gc = _ret_consts(h)
            hs = slice(128 * h, 128 * (h + 1))
            q = _rope(p_ref[:, O_RQ + 128 * h:O_RQ + 128 * (h + 1)], cosv, sinv)
            k = _rope(p_ref[:, O_RK + 128 * h:O_RK + 128 * (h + 1)], cosv, sinv) * (RET_DK ** -0.5)
            v = p_ref[:, O_RV + 128 * h:O_RV + 128 * (h + 1)]
            g = p_ref[:, O_RG + 128 * h:O_RG + 128 * (h + 1)]
            s_in = sr[h]
            a = _dot_nt(q, k) * dmat
            o = _dot(a, v) + _dot(q, s_in) * xi
            sr[h] = gc * s_in + _dot_tn(k * zeta, v)
            mu = jnp.mean(o, axis=-1, keepdims=True)
            xc = o - mu
            nrm = xc * lax.rsqrt(jnp.mean(xc * xc, axis=-1, keepdims=True) + EPS)
            ocat_ref[:, hs] = o
            mrg_ref[:, hs] = (nrm * rnw_ref[:, hs] * (g * _sigmoid(g))).astype(BF16)

        row_slices = [slice(CHUNK * j, CHUNK * (j + 1)) for j in range(CHUNKS_PER_STEP)]
        masks = _gla_masks()
        chunks = [_gla_common(p_ref, w2_ref, gb_ref, n * CHUNKS_PER_STEP + j, rows, masks)
                  for j, rows in enumerate(row_slices)]

        def gla_chunks(factored):
            own = masks["own"]
            for j, (rows, c) in enumerate(zip(row_slices, chunks)):
                s_in = sg[...]
                for h in range(GLA_HEADS):
                    sg_out[j, h] = s_in[GLA_DK * h:GLA_DK * (h + 1), GLA_DV * h:GLA_DV * (h + 1)]
                _gla_all_scores(c, p_scr.at[j], factored)
                v_all = p_ref[rows, O_GV:O_GV + GLA_HEADS * GLA_DV]
                o_inter = _dot(c["qs"] * c["ecum"], s_in)
                decay = jnp.exp(_dot_tn_exact_lhs(c["la"], jnp.ones((CHUNK, GLA_HEADS * GLA_DV), F32)))
                sg[...] = decay * s_in + jnp.where(own, _dot_tn(c["k"] * c["ekl"], v_all), 0.0)
                o_intra = _dot(p_scr[j].reshape(GLA_HEADS * CHUNK, CHUNK), v_all)
                for h in range(GLA_HEADS):
                    hs = slice(512 + 128 * h, 512 + 128 * (h + 1))
                    g = p_ref[rows, O_GR + 128 * h:O_GR + 128 * (h + 1)]
                    o = (o_intra[CHUNK * h:CHUNK * (h + 1), GLA_DV * h:GLA_DV * (h + 1)]
                         + o_inter[:, GLA_DV * h:GLA_DV * (h + 1)])
                    nrm = o * lax.rsqrt(jnp.mean(o * o, axis=-1, keepdims=True) + EPS)
                    ocat_ref[rows, hs] = o
                    mrg_ref[rows, hs] = (nrm * gnw_ref[:, 128 * h:128 * (h + 1)] * (g * _sigmoid(g))).astype(BF16)

        _either_form(chunks, gla_chunks)

        if n_carried:
            pl.when(n == N_STEPS - 1)(finish)

    const = lambda shape: pl.BlockSpec(shape, lambda n: (0,) * len(shape))
    anywhere = [pl.BlockSpec(memory_space=pl.ANY)] * n_carried
    return pl.pallas_call(
        body, name=name, grid=(N_STEPS,),
        in_specs=[pl.BlockSpec((STEP_ROWS, IN_WP), lambda n: (n, 0)),
                  pl.BlockSpec((STEP_ROWS, 128), lambda n: (n, 0)), pl.BlockSpec((STEP_ROWS, 128), lambda n: (n, 0)),
                  const((128, 256)), const((1, 256)), const((1, 512)), const((1, 512))] + anywhere,
        out_specs=[pl.BlockSpec((STEP_ROWS, D), lambda n: (n, 0)), pl.BlockSpec((STEP_ROWS, D), lambda n: (n, 0)),
                   pl.BlockSpec((1, RET_HEADS, RET_DK, 128), lambda n: (n, 0, 0, 0)),
                   pl.BlockSpec((CHUNKS_PER_STEP, GLA_HEADS, GLA_DK, GLA_DV), lambda n: (n, 0, 0, 0))] + anywhere,
        out_shape=[jax.ShapeDtypeStruct((LP, D), F32), jax.ShapeDtypeStruct((LP, D), BF16),
                   jax.ShapeDtypeStruct((N_STEPS, RET_HEADS, RET_DK, 128), F32),
                   jax.ShapeDtypeStruct((N_CHUNKS, GLA_HEADS, GLA_DK, GLA_DV), F32)] + _gathered_shapes(carried),
        scratch_shapes=[pltpu.VMEM((RET_HEADS, RET_DK, 128), F32),
                        pltpu.VMEM((GLA_HEADS * GLA_DK, GLA_HEADS * GLA_DV), F32),
                        pltpu.VMEM((CHUNKS_PER_STEP, GLA_HEADS, CHUNK, CHUNK), F32)] + _exchange_sems(n_carried),
        compiler_params=_cparams("arbitrary"),
    )(proj, cos2, sin2, w2p, gb, rnw, gnw, *carried)


def _mixer_bwd(proj, ocat, dmrg, sr_all, sg_all, cos2, sin2, w2p, gb, rnw, gnw, name, carried=()):
    last_step = N_STEPS - 1
    n_carried = len(carried)

    def body(*refs):
        p_ref, ocat_ref, dm_ref, sr_ref, sg_ref, c_ref, s_ref, w2_ref, gb_ref, rnw_ref, gnw_ref = refs[:11]
        g_refs, refs = refs[11:11 + n_carried], refs[11 + n_carried:]
        dp_ref, dw2_ref, dgb_ref, drn_ref, dgn_ref = refs[:5]
        got_refs, refs = refs[5:5 + n_carried], refs[5 + n_carried:]
        dsr, dsg, p_scr, dq_scr, dk_scr = refs[:5]
        step = pl.program_id(0)
        n = last_step - step
        if n_carried:
            start, finish = _exchange_phases(g_refs, got_refs, *refs[5:])
            pl.when(step == 0)(start)

        @pl.when(step == 0)
        def _():
            dsr[...] = jnp.zeros_like(dsr)
            dsg[...] = jnp.zeros_like(dsg)
            dw2_ref[...] = jnp.zeros_like(dw2_ref)
            dgb_ref[...] = jnp.zeros_like(dgb_ref)
            drn_ref[...] = jnp.zeros_like(drn_ref)
            dgn_ref[...] = jnp.zeros_like(dgn_ref)

        cosv, sinv = c_ref[...], s_ref[...]
        step_row = lax.broadcasted_iota(jnp.int32, (STEP_ROWS, 1), 0)
        real = ((n * STEP_ROWS + step_row) >= PAD_ROWS).astype(F32)

        for h in range(RET_HEADS):
            dmat, zeta, xi, gc = _ret_consts(h)
            hs = slice(128 * h, 128 * (h + 1))
            q = _rope(p_ref[:, O_RQ + 128 * h:O_RQ + 128 * (h + 1)], cosv, sinv)
            k = _rope(p_ref[:, O_RK + 128 * h:O_RK + 128 * (h + 1)], cosv, sinv) * (RET_DK ** -0.5)
            v = p_ref[:, O_RV + 128 * h:O_RV + 128 * (h + 1)]
            g = p_ref[:, O_RG + 128 * h:O_RG + 128 * (h + 1)]
            o = ocat_ref[:, hs]
            dy = dm_ref[:, hs]
            wv = rnw_ref[:, hs]
            mu = jnp.mean(o, axis=-1, keepdims=True)
            xc = o - mu
            rs = lax.rsqrt(jnp.mean(xc * xc, axis=-1, keepdims=True) + EPS)
            nrm = xc * rs
            sgm = _sigmoid(g)
            sil = g * sgm
            drn_ref[0:1, hs] += jnp.sum(dy * nrm * sil, axis=0, keepdims=True)
            dgate = dy * nrm * wv * (sgm * (1.0 + g * (1.0 - sgm)))
            dn = dy * wv * sil
            do = rs * (dn - jnp.mean(dn, axis=-1, keepdims=True) - nrm * jnp.mean(dn * nrm, axis=-1, keepdims=True))
            s_in = sr_ref[0, h]
            ds_out = dsr[h]
            a = _dot_nt(q, k) * dmat
            da = _dot_nt(do, v) * dmat
            dox = do * xi
            dq = _dot(da, k) + _dot_nt(dox, s_in)
            dk = _dot_tn(da, q) + _dot_nt(v, ds_out) * zeta
            dv = _dot_tn(a, do) + _dot(k * zeta, ds_out)
            dsr[h] = gc * ds_out + _dot_tn(q, dox)
            dk = dk * (RET_DK ** -0.5)
            dp_ref[:, O_RQ + 128 * h:O_RQ + 128 * (h + 1)] = (_unrope(dq, cosv, sinv) * real).astype(BF16)
            dp_ref[:, O_RK + 128 * h:O_RK + 128 * (h + 1)] = (_unrope(dk, cosv, sinv) * real).astype(BF16)
            dp_ref[:, O_RV + 128 * h:O_RV + 128 * (h + 1)] = (dv * real).astype(BF16)
            dp_ref[:, O_RG + 128 * h:O_RG + 128 * (h + 1)] = (dgate * real).astype(BF16)

        row_slices = [slice(CHUNK * j, CHUNK * (j + 1)) for j in range(CHUNKS_PER_STEP)]
        masks = _gla_masks()
        chunks = [_gla_common(p_ref, w2_ref, gb_ref, n * CHUNKS_PER_STEP + j, rows, masks)
                  for j, rows in enumerate(row_slices)]

        def gla_chunks(factored):
            for j in reversed(range(CHUNKS_PER_STEP)):
                gla_chunk_bwd(chunks[j], n * CHUNKS_PER_STEP + j, row_slices[j], j, factored, p_ref, ocat_ref, dm_ref,
                              sg_ref, w2_ref, gnw_ref, dp_ref, dw2_ref, dgb_ref, dgn_ref, dsg, p_scr, dq_scr, dk_scr)

        _either_form(chunks, gla_chunks)
        if n_carried:
            pl.when(step == last_step)(finish)

    def gla_chunk_bwd(c, chunk, rows, j, factored, p_ref, ocat_ref, dm_ref, sg_ref, w2_ref, gnw_ref,
                      dp_ref, dw2_ref, dgb_ref, dgn_ref, dsg, p_scr, dq_scr, dk_scr):
        row = lax.broadcasted_iota(jnp.int32, (CHUNK, 1), 0)
        real = ((chunk * CHUNK + row) >= PAD_ROWS).astype(F32)
        ri, ci = c["ri"], c["ci"]
        causal = ri >= ci
        triu = (ci >= ri).astype(F32)
        qe = c["qs"] * c["ecum"]
        kl = c["k"] * c["ekl"]
        v_all = p_ref[rows, O_GV:O_GV + GLA_HEADS * GLA_DV]
        dos, dps = [], []
        for h in range(GLA_HEADS):
            hs = slice(512 + 128 * h, 512 + 128 * (h + 1))
            v = v_all[:, GLA_DV * h:GLA_DV * (h + 1)]
            g = p_ref[rows, O_GR + 128 * h:O_GR + 128 * (h + 1)]
            o = ocat_ref[rows, hs]
            dy = dm_ref[rows, hs]
            wv = gnw_ref[:, 128 * h:128 * (h + 1)]
            rs = lax.rsqrt(jnp.mean(o * o, axis=-1, keepdims=True) + EPS)
            nrm = o * rs
            sgm = _sigmoid(g)
            sil = g * sgm
            dgn_ref[0:1, 128 * h:128 * (h + 1)] += jnp.sum(dy * nrm * sil, axis=0, keepdims=True)
            dgate = dy * nrm * wv * (sgm * (1.0 + g * (1.0 - sgm)))
            dn = dy * wv * sil
            do = rs * (dn - nrm * jnp.mean(dn * nrm, axis=-1, keepdims=True))
            dp_ref[rows, O_GR + 128 * h:O_GR + 128 * (h + 1)] = (dgate * real).astype(BF16)
            dos.append(do)
        do_all = jnp.concatenate(dos, axis=1)
        do_blocks = jnp.where(c["masks"]["own"], jnp.concatenate([do_all] * GLA_HEADS, axis=0), 0.0)
        dp_all = _dot_nt(do_blocks, v_all)
        dps = [jnp.where(causal, dp_all[CHUNK * h:CHUNK * (h + 1), :], 0.0) for h in range(GLA_HEADS)]
        _gla_all_intra_bwd(c, dps, p_scr.at[j], dq_scr.at[j], dk_scr.at[j], factored)
        s_in = _block_diagonal([sg_ref[j, h] for h in range(GLA_HEADS)])
        ds_out = dsg[...]
        decay = jnp.exp(_dot_tn_exact_lhs(c["la"], jnp.ones((CHUNK, GLA_HEADS * GLA_DV), F32)))
        dv_state = _dot(kl, ds_out)
        dqe = _dot_nt(do_all, s_in)
        dkl = _dot_nt(v_all, ds_out)
        dsg[...] = jnp.where(c["masks"]["own"], _dot_tn(qe, do_all), 0.0) + decay * ds_out
        sd = s_in * ds_out
        sd_hi = sd.astype(BF16)
        sd_lo = (sd - sd_hi.astype(F32)).astype(BF16)
        ones8 = jnp.ones((8, GLA_HEADS * GLA_DV), BF16)
        nt = (((1,), (1,)), ((), ()))
        d_el = (lax.dot_general(ones8, sd_hi, nt, preferred_element_type=F32)
                + lax.dot_general(ones8, sd_lo, nt, preferred_element_type=F32))[0:1, :]
        dqs = dqe * c["ecum"] + dq_scr[j]
        dkk = dkl * c["ekl"] + dk_scr[j]
        d_last = jnp.sum(dkl * kl, axis=0, keepdims=True) + d_el * c["el"]
        dcum = c["qs"] * dqs - c["k"] * dkk + jnp.where(row == CHUNK - 1, d_last, 0.0)
        dla = _dot_exact_rhs(triu, dcum)
        dv = _dot_tn(p_scr[j].reshape(GLA_HEADS * CHUNK, CHUNK), do_blocks) + dv_state
        dp_ref[rows, O_GV:O_GV + GLA_HEADS * GLA_DV] = (dv * real).astype(BF16)
        dp_ref[rows, O_GQ:O_GQ + 256] = (dqs * (GLA_DK ** -0.5) * real).astype(BF16)
        dp_ref[rows, O_GK:O_GK + 256] = (dkk * real).astype(BF16)
        dz = dla * (1.0 / GLA_TAU) * _sigmoid(-c["z"]) * real
        ga = p_ref[rows, O_GA:O_GA + 128]
        dp_ref[rows, O_GA:O_GA + 128] = _dot_nt(dz, w2_ref[...]).astype(BF16)
        dp_ref[rows, O_GA + 128:IN_WP] = jnp.zeros((CHUNK, IN_WP - O_GA - 128), BF16)
        dw2_ref[...] += _dot_tn(ga, dz)
        dgb_ref[0:1, :] += jnp.sum(dz, axis=0, keepdims=True)

    const = lambda shape: pl.BlockSpec(shape, lambda s: (0,) * len(shape))
    rev = lambda s: (last_step - s, 0)
    anywhere = [pl.BlockSpec(memory_space=pl.ANY)] * n_carried
    return pl.pallas_call(
        body, name=name, grid=(N_STEPS,),
        in_specs=[pl.BlockSpec((STEP_ROWS, IN_WP), rev), pl.BlockSpec((STEP_ROWS, D), rev),
                  pl.BlockSpec((STEP_ROWS, D), rev),
                  pl.BlockSpec((1, RET_HEADS, RET_DK, 128), lambda s: (last_step - s, 0, 0, 0)),
                  pl.BlockSpec((CHUNKS_PER_STEP, GLA_HEADS, GLA_DK, GLA_DV), lambda s: (last_step - s, 0, 0, 0)),
                  pl.BlockSpec((STEP_ROWS, 128), rev), pl.BlockSpec((STEP_ROWS, 128), rev),
                  const((128, 256)), const((1, 256)), const((1, 512)), const((1, 512))] + anywhere,
        out_specs=[pl.BlockSpec((STEP_ROWS, IN_WP), rev), const((128, 256)), const((8, 256)),
                   const((8, 512)), const((8, 512))] + anywhere,
        out_shape=[jax.ShapeDtypeStruct((LP, IN_WP), BF16), jax.ShapeDtypeStruct((128, 256), F32),
                   jax.ShapeDtypeStruct((8, 256), F32), jax.ShapeDtypeStruct((8, 512), F32),
                   jax.ShapeDtypeStruct((8, 512), F32)] + [jax.ShapeDtypeStruct(g.shape, g.dtype) for g in carried],
        scratch_shapes=[pltpu.VMEM((RET_HEADS, RET_DK, 128), F32),
                        pltpu.VMEM((GLA_HEADS * GLA_DK, GLA_HEADS * GLA_DV), F32),
                        pltpu.VMEM((CHUNKS_PER_STEP, GLA_HEADS, CHUNK, CHUNK), F32),
                        pltpu.VMEM((CHUNKS_PER_STEP, CHUNK, 256), F32),
                        pltpu.VMEM((CHUNKS_PER_STEP, CHUNK, 256), F32)] + _exchange_sems(n_carried),
        compiler_params=_cparams("arbitrary"),
    )(proj, ocat, dmrg, sr_all, sg_all, cos2, sin2, w2p, gb, rnw, gnw, *carried)


def _all_gather(xs, name):
    n = len(xs)

    def body(*refs):
        start, forward, finish = _gather_phases(refs[:n], refs[n:2 * n], *refs[2 * n:])
        start()
        forward()
        finish()

    return pl.pallas_call(
        body, name=name,
        in_specs=[pl.BlockSpec(memory_space=pl.ANY)] * n,
        out_specs=[pl.BlockSpec(memory_space=pl.ANY)] * n,
        out_shape=_gathered_shapes(xs),
        scratch_shapes=_exchange_sems(n),
    )(*xs)


def _gathered_shapes(xs):
    return [jax.ShapeDtypeStruct((N_DEV,) + x.shape, x.dtype) for x in xs]


def _exchange_sems(n):
    if n == 0:
        return []
    return [pltpu.SemaphoreType.DMA((7 * n,)), pltpu.SemaphoreType.DMA((7 * n,)), pltpu.SemaphoreType.DMA((n,))]


def _gather_phases(x_refs, out_refs, send_sems, recv_sems, local_sems):
    n = len(x_refs)
    mx, my, mc = lax.axis_index("x"), lax.axis_index("y"), lax.axis_index("c")
    me, sibling = (mx, my, mc), (mx, my, 1 - mc)
    chips = [(1 - mx, my), (mx, 1 - my), (1 - mx, 1 - my)]

    def slot(a, px, py, pc):
        return out_refs[a].at[4 * px + 2 * py + pc]

    def copy(a, k, block, to, src=None):
        return pltpu.make_async_remote_copy(
            src_ref=slot(a, *block) if src is None else src, dst_ref=slot(a, *block),
            send_sem=send_sems.at[7 * a + k], recv_sem=recv_sems.at[7 * a + k],
            device_id=to, device_id_type=MESH_IDS)

    mine = [pltpu.make_async_copy(x_refs[a], slot(a, *me), local_sems.at[a]) for a in range(n)]
    first = []
    for a in range(n):
        first.append(copy(a, 0, me, sibling, src=x_refs[a]))
        first += [copy(a, 1 + j, me, (*chip, mc), src=x_refs[a]) for j, chip in enumerate(chips)]
    passed = [copy(a, 4 + j, (*chip, mc), sibling) for j, chip in enumerate(chips) for a in range(n)]

    def start():
        for cp in mine + first:
            cp.start()

    def forward():
        for j, chip in enumerate(chips):
            for a in range(n):
                copy(a, 1 + j, (*chip, mc), me).wait_recv()
                passed[j * n + a].start()

    def finish():
        for a in range(n):
            copy(a, 0, sibling, me).wait_recv()
            for j, chip in enumerate(chips):
                copy(a, 4 + j, (*chip, 1 - mc), me).wait_recv()
        for cp in first + passed:
            cp.wait_send()
        for cp in mine:
            cp.wait()

    return start, forward, finish


def _exchange_blocks(gs, name):
    n = len(gs)

    def body(*refs):
        start, finish = _exchange_phases(refs[:n], refs[n:2 * n], *refs[2 * n:])
        start()
        finish()

    return pl.pallas_call(
        body, name=name,
        in_specs=[pl.BlockSpec(memory_space=pl.ANY)] * n,
        out_specs=[pl.BlockSpec(memory_space=pl.ANY)] * n,
        out_shape=[jax.ShapeDtypeStruct(g.shape, g.dtype) for g in gs],
        scratch_shapes=_exchange_sems(n),
    )(*gs)


def _exchange_phases(g_refs, out_refs, send_sems, recv_sems, local_sems):
    n = len(g_refs)
    mx, my, mc = lax.axis_index("x"), lax.axis_index("y"), lax.axis_index("c")
    me = 4 * mx + 2 * my + mc
    mine = [pltpu.make_async_copy(g_refs[a].at[me], out_refs[a].at[me], local_sems.at[a]) for a in range(n)]
    copies = []
    for r in range(1, N_DEV):
        px, py, pc = mx ^ (r >> 2), my ^ ((r >> 1) & 1), mc ^ (r & 1)
        peer = 4 * px + 2 * py + pc
        for a in range(n):
            copies.append(pltpu.make_async_remote_copy(
                src_ref=g_refs[a].at[peer], dst_ref=out_refs[a].at[me],
                send_sem=send_sems.at[7 * a + r - 1], recv_sem=recv_sems.at[7 * a + r - 1],
                device_id=(px, py, pc), device_id_type=MESH_IDS))

    def start():
        for cp in mine + copies:
            cp.start()

    def finish():
        for cp in copies:
            cp.wait_recv()
        for cp in copies:
            cp.wait_send()
        for cp in mine:
            cp.wait()

    return start, finish


IN_SHARD = IN_W // N_DEV
IN_SHARD_P = 512
UP_SHARD = D_UP // N_DEV
UP_SHARD_P = 768
RELAYOUT_ROWS = 256


def _pieces_w_in():
    return [(k, 0, IN_SHARD * k, IN_SHARD) for k in range(N_DEV)]


def _pieces_ffn_up():
    pieces = []
    for k in range(N_DEV):
        n, end = UP_SHARD * k, UP_SHARD * (k + 1)
        while n < end:
            half, r = divmod(n, D_FF)
            blk, off = divmod(r, CONV_BLOCK)
            run = min(CONV_BLOCK - off, end - n)
            pieces.append((k, n - UP_SHARD * k, 2 * CONV_BLOCK * blk + CONV_BLOCK * half + off, run))
            n += run
    return pieces


def _assemble_block(load, spans, dst_block, rows):
    lo = 128 * dst_block
    lane = lax.broadcasted_iota(jnp.int32, (1, 128), 1)
    out = jnp.zeros((rows, 128), F32)
    for key, src_off, dst_off, length in spans:
        a, b = max(lo, dst_off), min(lo + 128, dst_off + length)
        s, s_end = src_off + (a - dst_off), src_off + (b - dst_off)
        d = a
        while s < s_end:
            e = min(s_end, 128 * (s // 128 + 1))
            blk = load(key, s // 128)
            shift = (d - s) % 128
            if shift:
                blk = pltpu.roll(blk, shift, 1)
            out = jnp.where((lane >= d - lo) & (lane < d - lo + (e - s)), blk, out)
            d += e - s
            s = e
    return out


def _shards_to_cols(shards, pieces, width, name):
    _, rows, _ = shards.shape
    tr = RELAYOUT_ROWS

    def body(s_ref, o_ref):
        load = lambda k, b: s_ref[k, :, 128 * b:128 * (b + 1)].astype(F32)
        for db in range(width // 128):
            o_ref[:, 128 * db:128 * (db + 1)] = _assemble_block(load, pieces, db, tr).astype(BF16)

    return pl.pallas_call(
        body, name=name, grid=(rows // tr,),
        in_specs=[pl.BlockSpec((N_DEV, tr, shards.shape[2]), lambda i: (0, i, 0))],
        out_specs=pl.BlockSpec((tr, width), lambda i: (i, 0)),
        out_shape=jax.ShapeDtypeStruct((rows, width), BF16),
        compiler_params=_cparams("parallel"),
    )(shards)


def _cols_to_shards(full, pieces, shard_width, name):
    rows, width = full.shape
    tr = RELAYOUT_ROWS

    def body(f_ref, o_ref):
        load = lambda _, b: f_ref[:, 128 * b:128 * (b + 1)].astype(F32)
        for k in range(N_DEV):
            spans = [(None, dst_off, src_off, length) for dev, src_off, dst_off, length in pieces if dev == k]
            for db in range(shard_width // 128):
                o_ref[k, :, 128 * db:128 * (db + 1)] = _assemble_block(load, spans, db, tr).astype(BF16)

    return pl.pallas_call(
        body, name=name, grid=(rows // tr,),
        in_specs=[pl.BlockSpec((tr, width), lambda i: (i, 0))],
        out_specs=pl.BlockSpec((N_DEV, tr, shard_width), lambda i: (0, i, 0)),
        out_shape=jax.ShapeDtypeStruct((N_DEV, rows, shard_width), BF16),
        compiler_params=_cparams("parallel"),
    )(full)


def _adamw(parts, w, m, v, rows_per_step, name):
    rows, cols = w.shape
    assert rows % rows_per_step == 0 and parts.shape == (N_DEV, rows, cols)

    def body(p_ref, w_ref, m_ref, v_ref, g_ref, d_ref, nm_ref, nv_ref):
        g = p_ref[0].astype(F32)
        for j in range(1, N_DEV):
            g = g + p_ref[j].astype(F32)
        m_new = ADAM_B1 * m_ref[...] + (1.0 - ADAM_B1) * g
        v_new = ADAM_B2 * v_ref[...] + (1.0 - ADAM_B2) * (g * g)
        m_hat = m_new / (1.0 - ADAM_B1 ** ADAM_STEP)
        v_hat = v_new / (1.0 - ADAM_B2 ** ADAM_STEP)
        g_ref[...] = g
        d_ref[...] = -ADAM_LR * (m_hat / (jnp.sqrt(v_hat) + ADAM_EPS) + ADAM_WD * w_ref[...])
        nm_ref[...] = m_new
        nv_ref[...] = v_new

    tile = pl.BlockSpec((rows_per_step, cols), lambda i: (i, 0))
    shape = jax.ShapeDtypeStruct((rows, cols), F32)
    return pl.pallas_call(
        body, name=name, grid=(rows // rows_per_step,),
        in_specs=[pl.BlockSpec((N_DEV, rows_per_step, cols), lambda i: (0, i, 0)), tile, tile, tile],
        out_specs=[tile, tile, tile, tile],
        out_shape=[shape, shape, shape, shape],
        compiler_params=_cparams("parallel"),
    )(parts, w, m, v)


BIG = (("w_in", (DEPTH, D, IN_W // N_DEV), 2), ("w_out", (DEPTH, D // N_DEV, D), 1),
       ("ffn_up", (DEPTH, D, D_UP // N_DEV), 2), ("ffn_down", (DEPTH, D_FF // N_DEV, D), 1))
SMALL = (("meta_tokens", (N_META, D // N_DEV), 1), ("gla_gate_w2", (DEPTH, GATE_RANK, 256 // N_DEV), 2),
         ("ffn_conv_w", (DEPTH, 3, D_UP // N_DEV), 2))
REPL = (("pre_mix_norm", (DEPTH, D)), ("gla_gate_b", (DEPTH, 256)), ("ret_norm_w", (DEPTH, 512)),
        ("gla_norm_w", (DEPTH, 512)), ("post_mix_norm", (DEPTH, D)), ("pre_ffn_norm", (DEPTH, D)),
        ("ffn_conv_b", (DEPTH, D_UP)), ("post_ffn_norm", (DEPTH, D)))
WEIGHT_ORDER = ("meta_tokens", "pre_mix_norm", "w_in", "gla_gate_w2", "gla_gate_b", "ret_norm_w", "gla_norm_w",
                "w_out", "post_mix_norm", "pre_ffn_norm", "ffn_up", "ffn_conv_w", "ffn_conv_b", "ffn_down",
                "post_ffn_norm")


def _size(shape):
    return math.prod(shape)


def _round_up(n, mult):
    return -(-n // mult) * mult


REPL_ROWS = _round_up(-(-sum(_size(s) for _, s in REPL) // LANES), 8)
SMALL_ROWS = _round_up(-(-sum(_size(s) for _, s, _ in SMALL) // LANES), 8)


def _pack(arrays, rows, dtype):
    flat = jnp.concatenate([a.reshape(-1).astype(dtype) for a in arrays])
    return jnp.pad(flat, (0, rows * LANES - flat.shape[0])).reshape(rows, LANES)


def _unpack(buf, shapes):
    flat = buf.reshape(-1)
    out, off = [], 0
    for shape in shapes:
        out.append(flat[off:off + _size(shape)].reshape(shape))
        off += _size(shape)
    return out


def _unshard(blocks, axis):
    moved = jnp.moveaxis(blocks, 0, axis)
    shape = list(moved.shape)
    shape[axis:axis + 2] = [shape[axis] * shape[axis + 1]]
    return moved.reshape(shape)


def _to_blocks(full, axis):
    shape = list(full.shape)
    shape[axis:axis + 1] = [N_DEV, shape[axis] // N_DEV]
    return jnp.moveaxis(full.reshape(shape), axis, 0)


def _interleave_cols(w):
    lead = w.shape[:-1]
    return jnp.swapaxes(w.reshape(lead + (2, N_CONV_BLOCKS, CONV_BLOCK)), -3, -2).reshape(lead + (D_UP,))


def _deinterleave_cols(w):
    lead = w.shape[:-1]
    return jnp.swapaxes(w.reshape(lead + (N_CONV_BLOCKS, 2, CONV_BLOCK)), -3, -2).reshape(lead + (D_UP,))


def _rope_tables():
    half = RET_DK // 2
    inv = ROPE_BASE ** (-jnp.arange(half, dtype=F32) / half)
    pos = jnp.arange(LP, dtype=F32) - float(PAD_ROWS)
    ang = pos[:, None] * inv[None, :]
    c, s = jnp.cos(ang), jnp.sin(ang)
    return jnp.concatenate([c, c], axis=1), jnp.concatenate([-s, s], axis=1)


def kernel(x, meta_tokens, pre_mix_norm, w_in, gla_gate_w2, gla_gate_b, ret_norm_w, gla_norm_w, w_out, post_mix_norm, pre_ffn_norm, ffn_up, ffn_conv_w, ffn_conv_b, ffn_down, post_ffn_norm, loss_target, m_meta_tokens, m_pre_mix_norm, m_w_in, m_gla_gate_w2, m_gla_gate_b, m_ret_norm_w, m_gla_norm_w, m_w_out, m_post_mix_norm, m_pre_ffn_norm, m_ffn_up, m_ffn_conv_w, m_ffn_conv_b, m_ffn_down, m_post_ffn_norm, v_meta_tokens, v_pre_mix_norm, v_w_in, v_gla_gate_w2, v_gla_gate_b, v_ret_norm_w, v_gla_norm_w, v_w_out, v_post_mix_norm, v_pre_ffn_norm, v_ffn_up, v_ffn_conv_w, v_ffn_conv_b, v_ffn_down, v_post_ffn_norm):
    weights = dict(meta_tokens=meta_tokens, pre_mix_norm=pre_mix_norm, w_in=w_in, gla_gate_w2=gla_gate_w2,
                   gla_gate_b=gla_gate_b, ret_norm_w=ret_norm_w, gla_norm_w=gla_norm_w, w_out=w_out,
                   post_mix_norm=post_mix_norm, pre_ffn_norm=pre_ffn_norm, ffn_up=ffn_up, ffn_conv_w=ffn_conv_w,
                   ffn_conv_b=ffn_conv_b, ffn_down=ffn_down, post_ffn_norm=post_ffn_norm)
    mom1 = dict(meta_tokens=m_meta_tokens, pre_mix_norm=m_pre_mix_norm, w_in=m_w_in, gla_gate_w2=m_gla_gate_w2,
                gla_gate_b=m_gla_gate_b, ret_norm_w=m_ret_norm_w, gla_norm_w=m_gla_norm_w, w_out=m_w_out,
                post_mix_norm=m_post_mix_norm, pre_ffn_norm=m_pre_ffn_norm, ffn_up=m_ffn_up,
                ffn_conv_w=m_ffn_conv_w, ffn_conv_b=m_ffn_conv_b, ffn_down=m_ffn_down, post_ffn_norm=m_post_ffn_norm)
    mom2 = dict(meta_tokens=v_meta_tokens, pre_mix_norm=v_pre_mix_norm, w_in=v_w_in, gla_gate_w2=v_gla_gate_w2,
                gla_gate_b=v_gla_gate_b, ret_norm_w=v_ret_norm_w, gla_norm_w=v_gla_norm_w, w_out=v_w_out,
                post_mix_norm=v_post_mix_norm, pre_ffn_norm=v_pre_ffn_norm, ffn_up=v_ffn_up,
                ffn_conv_w=v_ffn_conv_w, ffn_conv_b=v_ffn_conv_b, ffn_down=v_ffn_down, post_ffn_norm=v_post_ffn_norm)

    pad_cols = lambda a, width: jnp.pad(a, ((0, 0), (0, width - a.shape[1])))
    big_names = [n for n, _, _ in BIG]
    shard = {}
    for l in range(DEPTH):
        shard[l, "w_in"] = pad_cols(w_in[l].astype(BF16), IN_SHARD_P)
        shard[l, "w_out"] = w_out[l].astype(BF16)
        shard[l, "ffn_up"] = pad_cols(ffn_up[l].astype(BF16), UP_SHARD_P)
        shard[l, "ffn_down"] = ffn_down[l].astype(BF16)
    gathered = {(0, "w_in"): _all_gather([shard[0, "w_in"]], "gather_w_in_0")[0]}
    gather_in_mixer = {l: [(l, n) for n in big_names[1:]] for l in range(DEPTH)}
    gather_in_conv = {l: [(l + 1, "w_in")] for l in range(DEPTH - 1)}
    small = _all_gather([_pack([weights[n] for n, _, _ in SMALL], SMALL_ROWS, F32)], "gather_small_weights")[0]
    small_parts = _unpack_blocks(small, [s for _, s, _ in SMALL])
    full = {n: _unshard(p, ax) for (n, _, ax), p in zip(SMALL, small_parts)}
    w2p = jnp.pad(full["gla_gate_w2"], ((0, 0), (0, 128 - GATE_RANK), (0, 0)))
    cw8 = jnp.concatenate([_interleave_cols(full["ffn_conv_w"]), _interleave_cols(ffn_conv_b)[:, None, :],
                           jnp.zeros((DEPTH, 4, D_UP), F32)], axis=1)
    cos2, sin2 = _rope_tables()

    h = jnp.concatenate([jnp.zeros((PAD_ROWS, D), F32), full["meta_tokens"], x[0]], axis=0)
    target = jnp.concatenate([jnp.zeros((CHUNK, D), F32), loss_target[0]], axis=0)
    saved, layer_w = [], []
    for l in range(DEPTH):
        lw = dict(w_in=_shards_to_cols(gathered[l, "w_in"], _pieces_w_in(), IN_WP, f"w_in_cols_{l}"))
        a1 = _rmsnorm_fwd(h, pre_mix_norm[l:l + 1], f"pre_mix_norm_{l}")
        proj = _matmul(a1, lw["w_in"], out_dtype=F32, tm=TM, tn=1280, tk=D, name=f"in_proj_{l}", n_outer=True)
        keys = gather_in_mixer.get(l, [])
        ocat, merged, sr_all, sg_all, *got = _mixer_fwd(proj, cos2, sin2, w2p[l], gla_gate_b[l:l + 1],
                                                        ret_norm_w[l:l + 1], gla_norm_w[l:l + 1], f"mixer_fwd_{l}",
                                                        carried=[shard[key] for key in keys])
        gathered.update(zip(keys, got))
        lw["w_out"] = gathered[l, "w_out"].reshape(D, D)
        lw["w_up"] = _shards_to_cols(gathered[l, "ffn_up"], _pieces_ffn_up(), D_UP, f"ffn_up_cols_{l}")
        lw["w_down"] = gathered[l, "ffn_down"].reshape(D_FF, D)
        layer_w.append(lw)
        m, h1 = _matmul_resid_norm(merged, lw["w_out"], h, post_mix_norm[l:l + 1], f"out_proj_{l}")
        a2 = _rmsnorm_fwd(h1, pre_ffn_norm[l:l + 1], f"pre_ffn_norm_{l}")
        u = _matmul(a2, lw["w_up"], out_dtype=BF16, tm=TM, tn=1408, tk=D, name=f"ffn_up_{l}", n_outer=True)
        keys = gather_in_conv.get(l, [])
        cv, act, *got = _conv_act_fwd(u, cw8[l], f"ffn_conv_act_{l}", carried=[shard[key] for key in keys])
        gathered.update(zip(keys, got))
        f, h2, *loss_acc = _matmul_resid_norm(act, lw["w_down"], h1, post_ffn_norm[l:l + 1], f"ffn_down_{l}",
                                              target=target if l == DEPTH - 1 else None)
        saved.append(dict(h=h, a1=a1, proj=proj, ocat=ocat, merged=merged, sr=sr_all, sg=sg_all, m=m, h1=h1,
                          a2=a2, u=u, cv=cv, act=act, f=f))
        h = h2

    dh = h
    loss = lax.psum(loss_acc[0][0, 0], ("x", "y", "c"))

    kinds = ("grad", "delta", "new_m", "new_v")
    grads = {n: [None] * DEPTH for n in WEIGHT_ORDER if n != "meta_tokens" and n not in big_names}
    pending, parts = [], {}
    for l in reversed(range(DEPTH)):
        s, lw = saved[l], layer_w[l]
        dact, df, g_post_ffn = _norm_bwd_matmul(dh, s["f"], post_ffn_norm[l:l + 1], lw["w_down"], BF16,
                                                f"ffn_down_dx_{l}")
        g_down = _matmul(s["act"], df, ta=True, out_dtype=BF16, tm=D_FF // 2, tn=D, tk=TK_ROWS, name=f"ffn_down_dw_{l}")
        du, dcw = _conv_act_bwd(dact, s["cv"], s["u"], cw8[l], f"ffn_conv_act_bwd_{l}")
        dh1, g_pre_ffn = _matmul_norm_bwd(du, lw["w_up"], s["h1"], pre_ffn_norm[l:l + 1], dh, 1408, f"ffn_up_dx_{l}")
        g_up = _matmul(s["a2"], du, ta=True, out_dtype=BF16, tm=D, tn=1408, tk=TK_ROWS, name=f"ffn_up_dw_{l}")
        dmerged, dm, g_post_mix = _norm_bwd_matmul(dh1, s["m"], post_mix_norm[l:l + 1], lw["w_out"], F32,
                                                   f"out_proj_dx_{l}")
        g_out = _matmul(s["merged"], dm, ta=True, out_dtype=BF16, tm=D, tn=D, tk=TK_ROWS, name=f"out_proj_dw_{l}")
        pending += [((l, "ffn_down"), g_down.reshape(N_DEV, D_FF // N_DEV, D)),
                    ((l, "ffn_up"), _cols_to_shards(g_up, _pieces_ffn_up(), UP_SHARD_P, f"ffn_up_grad_shards_{l}")),
                    ((l, "w_out"), g_out.reshape(N_DEV, D // N_DEV, D))]
        dproj, g_w2, g_gb, g_rn, g_gn, *got = _mixer_bwd(s["proj"], s["ocat"], dmerged, s["sr"], s["sg"], cos2, sin2,
                                                         w2p[l], gla_gate_b[l:l + 1], ret_norm_w[l:l + 1],
                                                         gla_norm_w[l:l + 1], f"mixer_bwd_{l}",
                                                         carried=[blocks for _, blocks in pending])
        parts.update(zip([key for key, _ in pending], got))
        g_in = _matmul(s["a1"], dproj, ta=True, out_dtype=BF16, tm=D, tn=1280, tk=TK_ROWS, name=f"in_proj_dw_{l}")
        pending = [((l, "w_in"), _cols_to_shards(g_in, _pieces_w_in(), IN_SHARD_P, f"w_in_grad_shards_{l}"))]
        now = pending if l == 0 else []
        dh, g_pre_mix, *got = _matmul_norm_bwd(dproj, lw["w_in"], s["h"], pre_mix_norm[l:l + 1], dh1, IN_WP,
                                               f"in_proj_dx_{l}", carried=[blocks for _, blocks in now])
        parts.update(zip([key for key, _ in now], got))
        pending = [] if l == 0 else pending
        grads["post_ffn_norm"][l] = g_post_ffn[0]
        grads["ffn_conv_w"][l] = _deinterleave_cols(dcw[0:3])
        grads["ffn_conv_b"][l] = _deinterleave_cols(dcw[3])
        grads["pre_ffn_norm"][l] = g_pre_ffn[0]
        grads["post_mix_norm"][l] = g_post_mix[0]
        grads["gla_gate_w2"][l] = g_w2[:GATE_RANK]
        grads["gla_gate_b"][l] = g_gb[0]
        grads["ret_norm_w"][l] = g_rn[0]
        grads["gla_norm_w"][l] = g_gn[0]
        grads["pre_mix_norm"][l] = g_pre_mix[0]
    local = {n: jnp.stack(v) for n, v in grads.items()}
    local["meta_tokens"] = dh[PAD_ROWS:CHUNK]
    grad_x = dh[CHUNK:][None]

    blocks = jnp.concatenate([_to_blocks(local[n], ax).reshape(N_DEV, -1) for n, _, ax in SMALL], axis=1)
    blocks = jnp.pad(blocks, ((0, 0), (0, SMALL_ROWS * LANES - blocks.shape[1]))).reshape(N_DEV, SMALL_ROWS, LANES)
    *got, small_grad_parts = _exchange_blocks([b for _, b in pending] + [blocks], "exchange_last_grads")
    parts.update(zip([key for key, _ in pending], got))

    widths = dict(w_in=IN_SHARD_P, w_out=D, ffn_up=UP_SHARD_P, ffn_down=D)
    steps = dict(w_in=256, w_out=D // N_DEV, ffn_up=256, ffn_down=D_FF // N_DEV // 2)
    big_out = {kind: {n: [None] * DEPTH for n in big_names} for kind in kinds}
    for l in range(DEPTH):
        for n in big_names:
            mine = [pad_cols(d[n][l], widths[n]) for d in (weights, mom1, mom2)]
            results = _adamw(parts[l, n], *mine, steps[n], f"adamw_{n}_{l}")
            for kind, r in zip(kinds, results):
                big_out[kind][n][l] = r[:, :weights[n].shape[2]]
    out = {kind: {n: jnp.stack(v) for n, v in big_out[kind].items()} for kind in kinds}
    shard_shapes = [s for _, s, _ in SMALL]
    packed = [_pack([d[n] for n, _, _ in SMALL], SMALL_ROWS, F32) for d in (weights, mom1, mom2)]
    results = _adamw(small_grad_parts, *packed, SMALL_ROWS, "adamw_small_sharded")
    for kind, buf in zip(kinds, results):
        out[kind].update(zip([n for n, _, _ in SMALL], _unpack(buf, shard_shapes)))

    repl_parts = _all_gather([_pack([local[n] for n, _ in REPL], REPL_ROWS, F32)], "gather_small_grads")[0]
    packed = [_pack([d[n] for n, _ in REPL], REPL_ROWS, F32) for d in (weights, mom1, mom2)]
    results = _adamw(repl_parts, *packed, REPL_ROWS, "adamw_replicated")
    repl_shapes = [s for _, s in REPL]
    for kind, buf in zip(kinds, results):
        out[kind].update(zip([n for n, _ in REPL], _unpack(buf, repl_shapes)))

    return (loss, grad_x, *[out["grad"][n] for n in WEIGHT_ORDER], *[out["delta"][n] for n in WEIGHT_ORDER],
            *[out["new_m"][n] for n in WEIGHT_ORDER], *[out["new_v"][n] for n in WEIGHT_ORDER])


def _unpack_blocks(gathered, shapes):
    flat = gathered.reshape(N_DEV, -1)
    out, off = [], 0
    for shape in shapes:
        out.append(flat[:, off:off + _size(shape)].reshape((N_DEV,) + shape))
        off += _size(shape)
    return out
```

```python
import math

import jax
import jax.numpy as jnp
from jax import lax
from jax.experimental import pallas as pl
from jax.experimental.pallas import tpu as pltpu

F32 = jnp.float32
BF16 = jnp.bfloat16

D = 1024
SEQ = 8192
DEPTH = 2
N_META = 16
CHUNK = 64
SUB = 16
N_SUB = CHUNK // SUB
PAD_ROWS = CHUNK - N_META
LP = SEQ + CHUNK
N_CHUNKS = LP // CHUNK
RET_HEADS = 4
RET_DK = 128
GLA_HEADS = 4
GLA_DK = 64
GLA_DV = 128
GLA_TAU = 16.0
GATE_RANK = 16
IN_W = 3600
IN_WP = 3840
D_FF = 2816
D_UP = 2 * D_FF
CONV_BLOCK = 256
N_CONV_BLOCKS = D_FF // CONV_BLOCK
ROPE_BASE = 10000.0
EPS = 1e-6
N_DEV = 8
LANES = 1024

O_RQ, O_RK, O_RV, O_RG = 0, 512, 1024, 1536
O_GQ, O_GK, O_GV, O_GR, O_GA = 2048, 2304, 2560, 3072, 3584

ADAM_LR = 0.001
ADAM_B1 = 0.9
ADAM_B2 = 0.999
ADAM_EPS = 1e-08
ADAM_WD = 0.01
ADAM_STEP = 10

VMEM_LIMIT = 56 * 1024 * 1024
MESH_IDS = pl.DeviceIdType.MESH


def _row_tile(rows, limit):
    best = 16
    for t in range(16, min(rows, limit) + 1, 16):
        if rows % t == 0:
            best = t
    return best


TM = _row_tile(LP, 688)
TK_ROWS = _row_tile(LP, 1376)


def _cparams(*sem):
    return pltpu.CompilerParams(dimension_semantics=sem, vmem_limit_bytes=VMEM_LIMIT)


def _dot(a, b):
    return jnp.dot(a.astype(BF16), b.astype(BF16), preferred_element_type=F32)


def _dot_nt(a, b):
    return lax.dot_general(a.astype(BF16), b.astype(BF16), (((1,), (1,)), ((), ())), preferred_element_type=F32)


def _dot_tn(a, b):
    return lax.dot_general(a.astype(BF16), b.astype(BF16), (((0,), (0,)), ((), ())), preferred_element_type=F32)


def _split3(x):
    hi = x.astype(BF16)
    r1 = x - hi.astype(F32)
    mid = r1.astype(BF16)
    lo = (r1 - mid.astype(F32)).astype(BF16)
    return hi, mid, lo


def _dot_exact_rhs(t, x):
    n = x.shape[1]
    parts = jnp.dot(t.astype(BF16), jnp.concatenate(_split3(x), axis=1), preferred_element_type=F32)
    return parts[:, :n] + parts[:, n:2 * n] + parts[:, 2 * n:]


def _dot_tn_exact_lhs(x, ones):
    n = x.shape[1]
    parts = lax.dot_general(jnp.concatenate(_split3(x), axis=1), ones.astype(BF16), (((0,), (0,)), ((), ())),
                            preferred_element_type=F32)
    return parts[:n] + parts[n:2 * n] + parts[2 * n:]


def _sigmoid(x):
    return 1.0 / (1.0 + jnp.exp(-x))


def _matmul(a, b, *, ta=False, tb=False, out_dtype, tm, tn, tk, name, n_outer=False):
    m = a.shape[1] if ta else a.shape[0]
    k = a.shape[0] if ta else a.shape[1]
    n = b.shape[0] if tb else b.shape[1]
    assert (b.shape[1] if tb else b.shape[0]) == k
    assert m % tm == 0 and n % tn == 0 and k % tk == 0, (name, m, n, k, tm, tn, tk)
    nk = k // tk
    order = (lambda f: (lambda j, i, kk: f(i, j, kk))) if n_outer else (lambda f: f)
    a_spec = (pl.BlockSpec((tk, tm), order(lambda i, j, kk: (kk, i))) if ta
              else pl.BlockSpec((tm, tk), order(lambda i, j, kk: (i, kk))))
    b_spec = (pl.BlockSpec((tn, tk), order(lambda i, j, kk: (j, kk))) if tb
              else pl.BlockSpec((tk, tn), order(lambda i, j, kk: (kk, j))))
    dims = (((0 if ta else 1,), (1 if tb else 0,)), ((), ()))

    def body(a_ref, b_ref, o_ref, *acc):
        prod = lax.dot_general(a_ref[...].astype(BF16), b_ref[...].astype(BF16), dims, preferred_element_type=F32)
        if nk == 1:
            o_ref[...] = prod.astype(out_dtype)
            return
        acc_ref, = acc
        kk = pl.program_id(2)

        @pl.when(kk == 0)
        def _():
            acc_ref[...] = prod

        @pl.when(kk > 0)
        def _():
            acc_ref[...] += prod

        @pl.when(kk == nk - 1)
        def _():
            o_ref[...] = acc_ref[...].astype(out_dtype)

    return pl.pallas_call(
        body, name=name, grid=(n // tn, m // tm, nk) if n_outer else (m // tm, n // tn, nk),
        in_specs=[a_spec, b_spec],
        out_specs=pl.BlockSpec((tm, tn), order(lambda i, j, kk: (i, j))),
        out_shape=jax.ShapeDtypeStruct((m, n), out_dtype),
        scratch_shapes=[pltpu.VMEM((tm, tn), F32)] if nk > 1 else [],
        compiler_params=_cparams("parallel", "parallel", "arbitrary"),
    )(a, b)


def _matmul_resid_norm(a, b, h, w, name, target=None):
    k = a.shape[1]
    has_loss = target is not None

    def body(a_ref, b_ref, h_ref, w_ref, *refs):
        m = jnp.dot(a_ref[...].astype(BF16), b_ref[...].astype(BF16), preferred_element_type=F32)
        r = lax.rsqrt(jnp.mean(m * m, axis=-1, keepdims=True) + EPS)
        i = pl.program_id(0)
        row = i * TM + lax.broadcasted_iota(jnp.int32, (TM, 1), 0)
        y = h_ref[...] + jnp.where(row >= PAD_ROWS, m * r * w_ref[...], 0.0)
        if not has_loss:
            m_ref, y_ref = refs
            m_ref[...] = m
            y_ref[...] = y
            return
        t_ref, m_ref, dy_ref, loss_ref = refs
        m_ref[...] = m

        @pl.when(i == 0)
        def _():
            loss_ref[...] = jnp.zeros_like(loss_ref)

        diff = jnp.where(row >= CHUNK, y - t_ref[...], 0.0)
        dy_ref[...] = diff * (1.0 / D)
        loss_ref[...] += (0.5 / D) * jnp.sum(diff * diff)

    tile = pl.BlockSpec((TM, D), lambda i: (i, 0))
    shape = jax.ShapeDtypeStruct((LP, D), F32)
    in_specs = [pl.BlockSpec((TM, k), lambda i: (i, 0)), pl.BlockSpec((k, D), lambda i: (0, 0)), tile,
                pl.BlockSpec((1, D), lambda i: (0, 0))]
    if has_loss:
        return pl.pallas_call(
            body, name=name, grid=(LP // TM,),
            in_specs=in_specs + [tile],
            out_specs=[tile, tile, pl.BlockSpec((8, 128), lambda i: (0, 0))],
            out_shape=[shape, shape, jax.ShapeDtypeStruct((8, 128), F32)],
            compiler_params=_cparams("arbitrary"),
        )(a, b, h, w, target)
    return pl.pallas_call(
        body, name=name, grid=(LP // TM,),
        in_specs=in_specs, out_specs=[tile, tile], out_shape=[shape, shape],
        compiler_params=_cparams("parallel"),
    )(a, b, h, w)


def _rmsnorm_bwd_rows(dy, x, w):
    r = lax.rsqrt(jnp.mean(x * x, axis=-1, keepdims=True) + EPS)
    g = dy * w
    dx = r * g - x * (r * r * r * jnp.mean(g * x, axis=-1, keepdims=True))
    return dx, jnp.sum(dy * x * r, axis=0, keepdims=True)


def _matmul_norm_bwd(dz, b, x, w, resid, tk, name, carried=()):
    k = dz.shape[1]
    assert k % tk == 0
    nk = k // tk
    n_rows = LP // TM
    n_carried = len(carried)

    def body(*refs):
        a_ref, b_ref, x_ref, w_ref, r_ref = refs[:5]
        g_refs, refs = refs[5:5 + n_carried], refs[5 + n_carried:]
        dx_ref, dw_ref = refs[:2]
        got_refs, refs = refs[2:2 + n_carried], refs[2 + n_carried:]
        acc, sems = (refs[:1], refs[1:]) if nk > 1 else ((), refs)
        i, kk = pl.program_id(0), pl.program_id(1)
        if n_carried:
            exchange_start, exchange_finish = _exchange_phases(g_refs, got_refs, *sems)
            pl.when((i == 0) & (kk == 0))(exchange_start)

        @pl.when((i == 0) & (kk == 0))
        def _():
            dw_ref[...] = jnp.zeros_like(dw_ref)

        prod = lax.dot_general(a_ref[...].astype(BF16), b_ref[...].astype(BF16), (((1,), (1,)), ((), ())),
                               preferred_element_type=F32)

        def finish(dy):
            dx, dw = _rmsnorm_bwd_rows(dy, x_ref[...], w_ref[...])
            dx_ref[...] = dx + r_ref[...]
            dw_ref[0:1, :] += dw

        if nk == 1:
            finish(prod)
        else:
            acc_ref, = acc

            @pl.when(kk == 0)
            def _():
                acc_ref[...] = prod

            @pl.when((kk > 0) & (kk < nk - 1))
            def _():
                acc_ref[...] += prod

            @pl.when(kk == nk - 1)
            def _():
                finish(acc_ref[...] + prod)

        if n_carried:
            pl.when((i == n_rows - 1) & (kk == nk - 1))(exchange_finish)

    tile = pl.BlockSpec((TM, D), lambda i, kk: (i, 0))
    anywhere = [pl.BlockSpec(memory_space=pl.ANY)] * n_carried
    return pl.pallas_call(
        body, name=name, grid=(n_rows, nk),
        in_specs=[pl.BlockSpec((TM, tk), lambda i, kk: (i, kk)), pl.BlockSpec((D, tk), lambda i, kk: (0, kk)), tile,
                  pl.BlockSpec((1, D), lambda i, kk: (0, 0)), tile] + anywhere,
        out_specs=[tile, pl.BlockSpec((8, D), lambda i, kk: (0, 0))] + anywhere,
        out_shape=[jax.ShapeDtypeStruct((LP, D), F32), jax.ShapeDtypeStruct((8, D), F32)]
        + [jax.ShapeDtypeStruct(g.shape, g.dtype) for g in carried],
        scratch_shapes=([pltpu.VMEM((TM, D), F32)] if nk > 1 else []) + _exchange_sems(n_carried),
        compiler_params=_cparams("arbitrary", "arbitrary"),
    )(dz, b, x, w, resid, *carried)


def _norm_bwd_matmul(dh, x, w, b, out_dtype, name):
    n = b.shape[0]

    def body(dh_ref, x_ref, w_ref, b_ref, o_ref, dx_ref, dw_ref):
        i = pl.program_id(0)

        @pl.when(i == 0)
        def _():
            dw_ref[...] = jnp.zeros_like(dw_ref)

        row = i * TM + lax.broadcasted_iota(jnp.int32, (TM, 1), 0)
        dy = jnp.where(row >= PAD_ROWS, dh_ref[...], 0.0)
        dx, dw = _rmsnorm_bwd_rows(dy, x_ref[...], w_ref[...])
        dxb = dx.astype(BF16)
        dx_ref[...] = dxb
        dw_ref[0:1, :] += dw
        o_ref[...] = lax.dot_general(dxb, b_ref[...].astype(BF16), (((1,), (1,)), ((), ())),
                                     preferred_element_type=F32).astype(out_dtype)

    tile = pl.BlockSpec((TM, D), lambda i: (i, 0))
    return pl.pallas_call(
        body, name=name, grid=(LP // TM,),
        in_specs=[tile, tile, pl.BlockSpec((1, D), lambda i: (0, 0)), pl.BlockSpec((n, D), lambda i: (0, 0))],
        out_specs=[pl.BlockSpec((TM, n), lambda i: (i, 0)), tile, pl.BlockSpec((8, D), lambda i: (0, 0))],
        out_shape=[jax.ShapeDtypeStruct((LP, n), out_dtype), jax.ShapeDtypeStruct((LP, D), BF16),
                   jax.ShapeDtypeStruct((8, D), F32)],
        compiler_params=_cparams("arbitrary"),
    )(dh, x, w, b)


def _rmsnorm_fwd(x, w, name):
    def body(x_ref, w_ref, o_ref):
        xv = x_ref[...]
        r = lax.rsqrt(jnp.mean(xv * xv, axis=-1, keepdims=True) + EPS)
        o_ref[...] = (xv * r * w_ref[...]).astype(BF16)

    return pl.pallas_call(
        body, name=name, grid=(LP // TM,),
        in_specs=[pl.BlockSpec((TM, D), lambda i: (i, 0)), pl.BlockSpec((1, D), lambda i: (0, 0))],
        out_specs=pl.BlockSpec((TM, D), lambda i: (i, 0)),
        out_shape=jax.ShapeDtypeStruct((LP, D), BF16),
        compiler_params=_cparams("parallel"),
    )(x, w)


GELU_C = math.sqrt(2.0 / math.pi)
GELU_K = 0.044715
STRIP = 16


def _gelu_half(a):
    return 0.5 * jnp.tanh(a * (a * a * (GELU_C * GELU_K) + GELU_C)) + 0.5


def _gelu_slope(a, h):
    return h * (1.0 + (a - a * h) * (a * a * (6.0 * GELU_C * GELU_K) + 2.0 * GELU_C))


def _shift_down(x, prev8, rows):
    row = lax.broadcasted_iota(jnp.int32, (rows, 1), 0)
    p1 = pltpu.roll(prev8, 1, 0)
    p2 = pltpu.roll(prev8, 2, 0)
    x1 = jnp.where(row == 0, p1[0:1, :], pltpu.roll(x, 1, 0))
    x2 = jnp.where(row == 0, p2[0:1, :], jnp.where(row == 1, p2[1:2, :], pltpu.roll(x, 2, 0)))
    return x1, x2


def _conv_act_fwd(u, cw8, name, carried=()):
    n_rows = LP // TM
    cb2 = 2 * CONV_BLOCK
    n_carried = len(carried)

    def body(*refs):
        u_ref, cw_ref = refs[:2]
        x_refs, refs = refs[2:2 + n_carried], refs[2 + n_carried:]
        conv_ref, act_ref = refs[:2]
        gathered_refs, refs = refs[2:2 + n_carried], refs[2 + n_carried:]
        carry_ref = refs[0]
        j, i = pl.program_id(0), pl.program_id(1)
        if n_carried:
            start, forward, finish = _gather_phases(x_refs, gathered_refs, *refs[1:])
            pl.when((j == 0) & (i == 0))(start)
            pl.when((j == (3 * N_CONV_BLOCKS) // 4) & (i == 0))(forward)

        @pl.when(i == 0)
        def _():
            carry_ref[...] = jnp.zeros_like(carry_ref)

        x = u_ref[...].astype(F32)
        x1, x2 = _shift_down(x, carry_ref[...], TM)
        conv = cw_ref[3:4, :] + x2 * cw_ref[0:1, :] + x1 * cw_ref[1:2, :] + x * cw_ref[2:3, :]
        conv_ref[...] = conv.astype(BF16)
        a = conv[:, :CONV_BLOCK]
        g = conv[:, CONV_BLOCK:]
        act_ref[...] = (a * _gelu_half(a) * g).astype(BF16)
        carry_ref[...] = x[TM - 8:TM, :]
        if n_carried:
            pl.when((j == N_CONV_BLOCKS - 1) & (i == n_rows - 1))(finish)

    anywhere = [pl.BlockSpec(memory_space=pl.ANY)] * n_carried
    return pl.pallas_call(
        body, name=name, grid=(N_CONV_BLOCKS, n_rows),
        in_specs=[pl.BlockSpec((TM, cb2), lambda j, i: (i, j)), pl.BlockSpec((8, cb2), lambda j, i: (0, j))] + anywhere,
        out_specs=[pl.BlockSpec((TM, cb2), lambda j, i: (i, j)),
                   pl.BlockSpec((TM, CONV_BLOCK), lambda j, i: (i, j))] + anywhere,
        out_shape=[jax.ShapeDtypeStruct((LP, D_UP), BF16), jax.ShapeDtypeStruct((LP, D_FF), BF16)]
        + _gathered_shapes(carried),
        scratch_shapes=[pltpu.VMEM((8, cb2), F32)] + _exchange_sems(n_carried),
        compiler_params=_cparams("arbitrary", "arbitrary"),
    )(u, cw8, *carried)


def _conv_act_bwd(dact, conv, u, cw8, name):
    n_rows = LP // TM
    cb2 = 2 * CONV_BLOCK
    n_strips = TM // STRIP

    def body(dact_ref, conv_ref, u_ref, cw_ref, du_ref, dcw_ref, carry_ref):
        i = pl.program_id(1)

        @pl.when(i == 0)
        def _():
            dcw_ref[...] = jnp.zeros_like(dcw_ref)
            carry_ref[...] = jnp.zeros_like(carry_ref)

        w0, w1, w2 = cw_ref[0:1, :], cw_ref[1:2, :], cw_ref[2:3, :]
        row = lax.broadcasted_iota(jnp.int32, (STRIP, 1), 0)
        fold = lambda z: z[:8, :] + z[8:, :]

        def strip(k, carry):
            n1, n2, s0, s1, s2, s3 = carry
            r0 = pl.multiple_of((n_strips - 1 - k) * STRIP, STRIP)
            cv = conv_ref[pl.ds(r0, STRIP), :].astype(F32)
            a = cv[:, :CONV_BLOCK]
            g = cv[:, CONV_BLOCK:]
            h = _gelu_half(a)
            dav = dact_ref[pl.ds(r0, STRIP), :].astype(F32)
            dconv = jnp.concatenate([dav * g * _gelu_slope(a, h), dav * (a * h)], axis=1)
            u1 = pltpu.roll(dconv, STRIP - 1, 0)
            u2 = pltpu.roll(dconv, STRIP - 2, 0)
            d1 = jnp.where(row >= STRIP - 1, n1, u1)
            d2 = jnp.where(row >= STRIP - 2, n2, u2)
            du_ref[pl.ds(r0, STRIP), :] = (dconv * w2 + d1 * w1 + d2 * w0).astype(BF16)
            x = u_ref[pl.ds(r0, STRIP), :].astype(F32)
            return (u1, u2, s0 + fold(d2 * x), s1 + fold(d1 * x), s2 + fold(dconv * x), s3 + fold(dconv))

        below = carry_ref[...]
        zero = jnp.zeros((8, cb2), F32)
        init = (pltpu.roll(below, STRIP - 1, 0), pltpu.roll(below, STRIP - 2, 0), zero, zero, zero, zero)
        u1, _, s0, s1, s2, s3 = lax.fori_loop(0, n_strips, strip, init)
        carry_ref[...] = pltpu.roll(u1, 1, 0)
        dcw_ref[0:1, :] += jnp.sum(s0, axis=0, keepdims=True)
        dcw_ref[1:2, :] += jnp.sum(s1, axis=0, keepdims=True)
        dcw_ref[2:3, :] += jnp.sum(s2, axis=0, keepdims=True)
        dcw_ref[3:4, :] += jnp.sum(s3, axis=0, keepdims=True)

    rev = lambda j, i: (n_rows - 1 - i, j)
    return pl.pallas_call(
        body, name=name, grid=(N_CONV_BLOCKS, n_rows),
        in_specs=[pl.BlockSpec((TM, CONV_BLOCK), rev), pl.BlockSpec((TM, cb2), rev), pl.BlockSpec((TM, cb2), rev),
                  pl.BlockSpec((8, cb2), lambda j, i: (0, j))],
        out_specs=[pl.BlockSpec((TM, cb2), rev), pl.BlockSpec((8, cb2), lambda j, i: (0, j))],
        out_shape=[jax.ShapeDtypeStruct((LP, D_UP), BF16), jax.ShapeDtypeStruct((8, D_UP), F32)],
        scratch_shapes=[pltpu.VMEM((STRIP, cb2), F32)],
        compiler_params=_cparams("arbitrary", "arbitrary"),
    )(dact, conv, u, cw8)


CHUNKS_PER_STEP = 3 if N_CHUNKS % 3 == 0 else 1
STEP_ROWS = CHUNKS_PER_STEP * CHUNK
N_STEPS = N_CHUNKS // CHUNKS_PER_STEP


def _ret_consts(h):
    rows = STEP_ROWS
    lg = math.log(1.0 - 2.0 ** (-5.0 - h))
    ri = lax.broadcasted_iota(jnp.int32, (rows, rows), 0)
    ci = lax.broadcasted_iota(jnp.int32, (rows, rows), 1)
    diff = (ri - ci).astype(F32)
    dmat = jnp.where(diff >= 0, jnp.exp(lg * jnp.maximum(diff, 0.0)), 0.0)
    rowf = lax.broadcasted_iota(jnp.int32, (rows, 1), 0).astype(F32)
    zeta = jnp.exp(lg * (rows - 1.0 - rowf))
    xi = jnp.exp(lg * (rowf + 1.0))
    return dmat, zeta, xi, math.exp(lg * rows)


def _rope(t, cosv, sinv):
    return t * cosv + pltpu.roll(t, RET_DK // 2, 1) * sinv


def _unrope(d, cosv, sinv):
    return d * cosv + pltpu.roll(d * sinv, RET_DK // 2, 1)


def _gla_masks():
    ri = lax.broadcasted_iota(jnp.int32, (CHUNK, CHUNK), 0)
    ci = lax.broadcasted_iota(jnp.int32, (CHUNK, CHUNK), 1)
    return dict(ri=ri, ci=ci, tril=(ri >= ci).astype(F32), heads=_head_block_mask(), own=_state_block_mask())


def _gla_common(p_ref, w2_ref, gb_ref, chunk, rows, masks):
    row = lax.broadcasted_iota(jnp.int32, (CHUNK, 1), 0)
    real = (chunk * CHUNK + row) >= PAD_ROWS
    ga = p_ref[rows, O_GA:O_GA + 128]
    z = _dot(ga, w2_ref[...]) + gb_ref[...]
    la = (jnp.minimum(z, 0.0) - jnp.log(1.0 + jnp.exp(-jnp.abs(z)))) * (1.0 / GLA_TAU)
    la = jnp.where(real, la, 0.0)
    ri, ci = masks["ri"], masks["ci"]
    cum = _dot_exact_rhs(masks["tril"], la)
    last = cum[CHUNK - 1:CHUNK, :]
    qs = p_ref[rows, O_GQ:O_GQ + 256] * (GLA_DK ** -0.5)
    k = p_ref[rows, O_GK:O_GK + 256]
    ecum = jnp.exp(cum)
    ekl = jnp.exp(last - cum)
    el = jnp.exp(last)
    refs = [jnp.zeros((1, 256), F32)] + [cum[a * SUB - 1:a * SUB, :] for a in range(1, N_SUB)]
    eq = [jnp.exp(cum[a * SUB:(a + 1) * SUB, :] - refs[a]) for a in range(N_SUB)]
    spread = refs[0] - cum[SUB - 1:SUB, :]
    for a in range(1, N_SUB):
        spread = jnp.maximum(spread, refs[a] - cum[(a + 1) * SUB - 1:(a + 1) * SUB, :])
    small = jnp.max(spread) <= GLA_FACTORED_MAX
    return dict(real=real, row=row, z=z, la=la, cum=cum, last=last, qs=qs, k=k, ecum=ecum, ekl=ekl, el=el,
                refs=refs, eq=eq, small=small, ri=ri, ci=ci, masks=masks)


GLA_FACTORED_MAX = 40.0


def _head_block_mask():
    r = lax.broadcasted_iota(jnp.int32, (CHUNK, 256), 0)
    col = lax.broadcasted_iota(jnp.int32, (CHUNK, 256), 1)
    return (r // SUB) == (col // GLA_DK)


def _state_block_mask():
    r = lax.broadcasted_iota(jnp.int32, (GLA_HEADS * GLA_DK, GLA_HEADS * GLA_DV), 0)
    col = lax.broadcasted_iota(jnp.int32, (GLA_HEADS * GLA_DK, GLA_HEADS * GLA_DV), 1)
    return (r // GLA_DK) == (col // GLA_DV)


def _block_diagonal(blocks):
    zero = jnp.zeros((GLA_DK, GLA_DV), F32)
    return jnp.concatenate([jnp.concatenate([blocks[h] if g == h else zero for g in range(GLA_HEADS)], axis=1)
                            for h in range(GLA_HEADS)], axis=0)


def _gla_factored(c):
    mask = c["masks"]["heads"]
    eks, keys, queries = [], [], []
    for a in range(N_SUB):
        ek = jnp.exp(jnp.minimum(c["refs"][a] - c["cum"], GLA_FACTORED_MAX))
        qh = c["qs"][a * SUB:(a + 1) * SUB, :] * c["eq"][a]
        eks.append(ek)
        keys.append(c["k"] * ek)
        queries.append(jnp.where(mask, jnp.concatenate([qh] * GLA_HEADS, axis=0), 0.0))
    return eks, keys, queries


def _gla_scores_factored(c, factored, p_scr):
    _, keys, queries = factored
    for a in range(N_SUB):
        out = _dot_nt(queries[a], keys[a])
        out = jnp.where(c["ci"] <= a * SUB + (c["ri"] & (SUB - 1)), out, 0.0)
        for h in range(GLA_HEADS):
            p_scr[h, a * SUB:(a + 1) * SUB, :] = out[h * SUB:(h + 1) * SUB, :]


def _gla_intra_bwd_factored(c, factored, dps, dq_scr, dk_scr):
    eks, keys, queries = factored
    mask = c["masks"]["heads"]
    dk = jnp.zeros((CHUNK, 256), F32)
    for a in range(N_SUB):
        dpa = jnp.concatenate([dps[h][a * SUB:(a + 1) * SUB, :] for h in range(GLA_HEADS)], axis=0)
        dq = jnp.where(mask, _dot(dpa, keys[a]), 0.0)
        dq = dq[0:SUB] + dq[SUB:2 * SUB] + dq[2 * SUB:3 * SUB] + dq[3 * SUB:4 * SUB]
        dq_scr[a * SUB:(a + 1) * SUB, :] = dq * c["eq"][a]
        dk = dk + _dot_tn(dpa, queries[a]) * eks[a]
    dk_scr[...] = dk


def _gla_lag_weights(c):
    cum, row = c["cum"], c["row"]
    out = [jnp.ones((CHUNK, 256), F32)]
    for r in range(1, SUB):
        out.append(jnp.where((row % SUB) >= r, jnp.exp(jnp.minimum(cum - pltpu.roll(cum, r, 0), 0.0)), 0.0))
    return out


def _gla_pairwise_keys(c):
    return [None] + [c["k"] * jnp.exp(jnp.minimum(c["refs"][a] - c["cum"], 0.0)) for a in range(1, N_SUB)]


def _gla_scores_pairwise(c, lag_w, keys, h):
    sl = slice(GLA_DK * h, GLA_DK * (h + 1))
    qs, k = c["qs"][:, sl], c["k"][:, sl]
    ri, ci = c["ri"], c["ci"]
    p = jnp.zeros((CHUNK, CHUNK), F32)
    for r in range(SUB):
        kr = k if r == 0 else pltpu.roll(k, r, 0)
        pr = jnp.sum(qs * kr * lag_w[r][:, sl], axis=1, keepdims=True)
        p = p + jnp.where(ci == ri - r, pr, 0.0)
    blocks = [jnp.zeros((SUB, CHUNK), F32)]
    for a in range(1, N_SUB):
        qh = qs[a * SUB:(a + 1) * SUB, :] * c["eq"][a][:, sl]
        blocks.append(jnp.where(ci[:SUB, :] < a * SUB, _dot_nt(qh, keys[a][:, sl]), 0.0))
    return p + jnp.concatenate(blocks, axis=0)


def _gla_all_scores(c, p_scr, factored):
    if factored:
        _gla_scores_factored(c, _gla_factored(c), p_scr)
    else:
        lag_w, keys = _gla_lag_weights(c), _gla_pairwise_keys(c)
        for h in range(GLA_HEADS):
            p_scr[h] = _gla_scores_pairwise(c, lag_w, keys, h)


def _either_form(chunks, run):
    small = chunks[0]["small"]
    for c in chunks[1:]:
        small = jnp.logical_and(small, c["small"])
    pl.when(small)(lambda: run(True))
    pl.when(jnp.logical_not(small))(lambda: run(False))


def _gla_intra_bwd_pairwise(c, lag_w, keys, dp, h):
    sl = slice(GLA_DK * h, GLA_DK * (h + 1))
    qs_h, k_h = c["qs"][:, sl], c["k"][:, sl]
    ri, ci = c["ri"], c["ci"]
    dq_rows = [jnp.zeros((SUB, GLA_DK), F32)]
    dk = jnp.zeros((CHUNK, GLA_DK), F32)
    for a in range(1, N_SUB):
        eq = c["eq"][a][:, sl]
        qh = qs_h[a * SUB:(a + 1) * SUB, :] * eq
        dpa = jnp.where(ci[:SUB, :] < a * SUB, dp[a * SUB:(a + 1) * SUB, :], 0.0)
        dq_rows.append(_dot(dpa, keys[a][:, sl]) * eq)
        ek = jnp.exp(jnp.minimum(c["refs"][a][:, sl] - c["cum"][:, sl], 0.0))
        dk = dk + _dot_tn(dpa, qh) * ek
    dq = jnp.concatenate(dq_rows, axis=0)
    for r in range(SUB):
        w = lag_w[r][:, sl]
        dpr = jnp.sum(jnp.where(ci == ri - r, dp, 0.0), axis=1, keepdims=True)
        kr = k_h if r == 0 else pltpu.roll(k_h, r, 0)
        dq = dq + dpr * kr * w
        back = dpr * qs_h * w
        dk = dk + (back if r == 0 else pltpu.roll(back, CHUNK - r, 0))
    return dq, dk


def _gla_all_intra_bwd(c, dps, p_scr, dq_scr, dk_scr, factored):
    if factored:
        terms = _gla_factored(c)
        _gla_scores_factored(c, terms, p_scr)
        _gla_intra_bwd_factored(c, terms, dps, dq_scr, dk_scr)
    else:
        lag_w, keys = _gla_lag_weights(c), _gla_pairwise_keys(c)
        outs = [_gla_intra_bwd_pairwise(c, lag_w, keys, dps[h], h) for h in range(GLA_HEADS)]
        for h in range(GLA_HEADS):
            p_scr[h] = _gla_scores_pairwise(c, lag_w, keys, h)
        dq_scr[...] = jnp.concatenate([o[0] for o in outs], axis=1)
        dk_scr[...] = jnp.concatenate([o[1] for o in outs], axis=1)


def _mixer_fwd(proj, cos2, sin2, w2p, gb, rnw, gnw, name, carried=()):
    n_carried = len(carried)

    def body(*refs):
        p_ref, c_ref, s_ref, w2_ref, gb_ref, rnw_ref, gnw_ref = refs[:7]
        x_refs, refs = refs[7:7 + n_carried], refs[7 + n_carried:]
        ocat_ref, mrg_ref, sr_out, sg_out = refs[:4]
        gathered_refs, refs = refs[4:4 + n_carried], refs[4 + n_carried:]
        sr, sg, p_scr = refs[:3]
        n = pl.program_id(0)
        if n_carried:
            start, forward, finish = _gather_phases(x_refs, gathered_refs, *refs[3:])
            pl.when(n == 0)(start)
            pl.when(n == (3 * N_STEPS) // 4)(forward)

        @pl.when(n == 0)
        def _():
            sr[...] = jnp.zeros_like(sr)
            sg[...] = jnp.zeros_like(sg)

        sr_out[0] = sr[...]
        cosv, sinv = c_ref[...], s_ref[...]

        for h in range(RET_HEADS):
            dmat, zeta, xi, gc = _ret_consts(h)
            hs = slice(128 * h, 128 * (h + 1))
            q = _rope(p_ref[:, O_RQ + 128 * h:O_RQ + 128 * (h + 1)], cosv, sinv)
            k = _rope(p_ref[:, O_RK + 128 * h:O_RK + 128 * (h + 1)], cosv, sinv) * (RET_DK ** -0.5)
            v = p_ref[:, O_RV + 128 * h:O_RV + 128 * (h + 1)]
            g = p_ref[:, O_RG + 128 * h:O_RG + 128 * (h + 1)]
            s_in = sr[h]
            a = _dot_nt(q, k) * dmat
            o = _dot(a, v) + _dot(q, s_in) * xi
            sr[h] = gc * s_in + _dot_tn(k * zeta, v)
            mu = jnp.mean(o, axis=-1, keepdims=True)
            xc = o - mu
            nrm = xc * lax.rsqrt(jnp.mean(xc * xc, axis=-1, keepdims=True) + EPS)
            ocat_ref[:, hs] = o
            mrg_ref[:, hs] = (nrm * rnw_ref[:, hs] * (g * _sigmoid(g))).astype(BF16)

        row_slices = [slice(CHUNK * j, CHUNK * (j + 1)) for j in range(CHUNKS_PER_STEP)]
        masks = _gla_masks()
        chunks = [_gla_common(p_ref, w2_ref, gb_ref, n * CHUNKS_PER_STEP + j, rows, masks)
                  for j, rows in enumerate(row_slices)]

        def gla_chunks(factored):
            own = masks["own"]
            for j, (rows, c) in enumerate(zip(row_slices, chunks)):
                s_in = sg[...]
                for h in range(GLA_HEADS):
                    sg_out[j, h] = s_in[GLA_DK * h:GLA_DK * (h + 1), GLA_DV * h:GLA_DV * (h + 1)]
                _gla_all_scores(c, p_scr.at[j], factored)
                v_all = p_ref[rows, O_GV:O_GV + GLA_HEADS * GLA_DV]
                o_inter = _dot(c["qs"] * c["ecum"], s_in)
                decay = jnp.exp(_dot_tn_exact_lhs(c["la"], jnp.ones((CHUNK, GLA_HEADS * GLA_DV), F32)))
                sg[...] = decay * s_in + jnp.where(own, _dot_tn(c["k"] * c["ekl"], v_all), 0.0)
                o_intra = _dot(p_scr[j].reshape(GLA_HEADS * CHUNK, CHUNK), v_all)
                for h in range(GLA_HEADS):
                    hs = slice(512 + 128 * h, 512 + 128 * (h + 1))
                    g = p_ref[rows, O_GR + 128 * h:O_GR + 128 * (h + 1)]
                    o = (o_intra[CHUNK * h:CHUNK * (h + 1), GLA_DV * h:GLA_DV * (h + 1)]
                         + o_inter[:, GLA_DV * h:GLA_DV * (h + 1)])
                    nrm = o * lax.rsqrt(jnp.mean(o * o, axis=-1, keepdims=True) + EPS)
                    ocat_ref[rows, hs] = o
                    mrg_ref[rows, hs] = (nrm * gnw_ref[:, 128 * h:128 * (h + 1)] * (g * _sigmoid(g))).astype(BF16)

        _either_form(chunks, gla_chunks)

        if n_carried:
            pl.when(n == N_STEPS - 1)(finish)

    const = lambda shape: pl.BlockSpec(shape, lambda n: (0,) * len(shape))
    anywhere = [pl.BlockSpec(memory_space=pl.ANY)] * n_carried
    return pl.pallas_call(
        body, name=name, grid=(N_STEPS,),
        in_specs=[pl.BlockSpec((STEP_ROWS, IN_WP), lambda n: (n, 0)),
                  pl.BlockSpec((STEP_ROWS, 128), lambda n: (n, 0)), pl.BlockSpec((STEP_ROWS, 128), lambda n: (n, 0)),
                  const((128, 256)), const((1, 256)), const((1, 512)), const((1, 512))] + anywhere,
        out_specs=[pl.BlockSpec((STEP_ROWS, D), lambda n: (n, 0)), pl.BlockSpec((STEP_ROWS, D), lambda n: (n, 0)),
                   pl.BlockSpec((1, RET_HEADS, RET_DK, 128), lambda n: (n, 0, 0, 0)),
                   pl.BlockSpec((CHUNKS_PER_STEP, GLA_HEADS, GLA_DK, GLA_DV), lambda n: (n, 0, 0, 0))] + anywhere,
        out_shape=[jax.ShapeDtypeStruct((LP, D), F32), jax.ShapeDtypeStruct((LP, D), BF16),
                   jax.ShapeDtypeStruct((N_STEPS, RET_HEADS, RET_DK, 128), F32),
                   jax.ShapeDtypeStruct((N_CHUNKS, GLA_HEADS, GLA_DK, GLA_DV), F32)] + _gathered_shapes(carried),
        scratch_shapes=[pltpu.VMEM((RET_HEADS, RET_DK, 128), F32),
                        pltpu.VMEM((GLA_HEADS * GLA_DK, GLA_HEADS * GLA_DV), F32),
                        pltpu.VMEM((CHUNKS_PER_STEP, GLA_HEADS, CHUNK, CHUNK), F32)] + _exchange_sems(n_carried),
        compiler_params=_cparams("arbitrary"),
    )(proj, cos2, sin2, w2p, gb, rnw, gnw, *carried)


def _mixer_bwd(proj, ocat, dmrg, sr_all, sg_all, cos2, sin2, w2p, gb, rnw, gnw, name, carried=()):
    last_step = N_STEPS - 1
    n_carried = len(carried)

    def body(*refs):
        p_ref, ocat_ref, dm_ref, sr_ref, sg_ref, c_ref, s_ref, w2_ref, gb_ref, rnw_ref, gnw_ref = refs[:11]
        g_refs, refs = refs[11:11 + n_carried], refs[11 + n_carried:]
        dp_ref, dw2_ref, dgb_ref, drn_ref, dgn_ref = refs[:5]
        got_refs, refs = refs[5:5 + n_carried], refs[5 + n_carried:]
        dsr, dsg, p_scr, dq_scr, dk_scr = refs[:5]
        step = pl.program_id(0)
        n = last_step - step
        if n_carried:
            start, finish = _exchange_phases(g_refs, got_refs, *refs[5:])
            pl.when(step == 0)(start)

        @pl.when(step == 0)
        def _():
            dsr[...] = jnp.zeros_like(dsr)
            dsg[...] = jnp.zeros_like(dsg)
            dw2_ref[...] = jnp.zeros_like(dw2_ref)
            dgb_ref[...] = jnp.zeros_like(dgb_ref)
            drn_ref[...] = jnp.zeros_like(drn_ref)
            dgn_ref[...] = jnp.zeros_like(dgn_ref)

        cosv, sinv = c_ref[...], s_ref[...]
        step_row = lax.broadcasted_iota(jnp.int32, (STEP_ROWS, 1), 0)
        real = ((n * STEP_ROWS + step_row) >= PAD_ROWS).astype(F32)

        for h in range(RET_HEADS):
            dmat, zeta, xi, gc = _ret_consts(h)
            hs = slice(128 * h, 128 * (h + 1))
            q = _rope(p_ref[:, O_RQ + 128 * h:O_RQ + 128 * (h + 1)], cosv, sinv)
            k = _rope(p_ref[:, O_RK + 128 * h:O_RK + 128 * (h + 1)], cosv, sinv) * (RET_DK ** -0.5)
            v = p_ref[:, O_RV + 128 * h:O_RV + 128 * (h + 1)]
            g = p_ref[:, O_RG + 128 * h:O_RG + 128 * (h + 1)]
            o = ocat_ref[:, hs]
            dy = dm_ref[:, hs]
            wv = rnw_ref[:, hs]
            mu = jnp.mean(o, axis=-1, keepdims=True)
            xc = o - mu
            rs = lax.rsqrt(jnp.mean(xc * xc, axis=-1, keepdims=True) + EPS)
            nrm = xc * rs
            sgm = _sigmoid(g)
            sil = g * sgm
            drn_ref[0:1, hs] += jnp.sum(dy * nrm * sil, axis=0, keepdims=True)
            dgate = dy * nrm * wv * (sgm * (1.0 + g * (1.0 - sgm)))
            dn = dy * wv * sil
            do = rs * (dn - jnp.mean(dn, axis=-1, keepdims=True) - nrm * jnp.mean(dn * nrm, axis=-1, keepdims=True))
            s_in = sr_ref[0, h]
            ds_out = dsr[h]
            a = _dot_nt(q, k) * dmat
            da = _dot_nt(do, v) * dmat
            dox = do * xi
            dq = _dot(da, k) + _dot_nt(dox, s_in)
            dk = _dot_tn(da, q) + _dot_nt(v, ds_out) * zeta
            dv = _dot_tn(a, do) + _dot(k * zeta, ds_out)
            dsr[h] = gc * ds_out + _dot_tn(q, dox)
            dk = dk * (RET_DK ** -0.5)
            dp_ref[:, O_RQ + 128 * h:O_RQ + 128 * (h + 1)] = (_unrope(dq, cosv, sinv) * real).astype(BF16)
            dp_ref[:, O_RK + 128 * h:O_RK + 128 * (h + 1)] = (_unrope(dk, cosv, sinv) * real).astype(BF16)
            dp_ref[:, O_RV + 128 * h:O_RV + 128 * (h + 1)] = (dv * real).astype(BF16)
            dp_ref[:, O_RG + 128 * h:O_RG + 128 * (h + 1)] = (dgate * real).astype(BF16)

        row_slices = [slice(CHUNK * j, CHUNK * (j + 1)) for j in range(CHUNKS_PER_STEP)]
        masks = _gla_masks()
        chunks = [_gla_common(p_ref, w2_ref, gb_ref, n * CHUNKS_PER_STEP + j, rows, masks)
                  for j, rows in enumerate(row_slices)]

        def gla_chunks(factored):
            for j in reversed(range(CHUNKS_PER_STEP)):
                gla_chunk_bwd(chunks[j], n * CHUNKS_PER_STEP + j, row_slices[j], j, factored, p_ref, ocat_ref, dm_ref,
                              sg_ref, w2_ref, gnw_ref, dp_ref, dw2_ref, dgb_ref, dgn_ref, dsg, p_scr, dq_scr, dk_scr)

        _either_form(chunks, gla_chunks)
        if n_carried:
            pl.when(step == last_step)(finish)

    def gla_chunk_bwd(c, chunk, rows, j, factored, p_ref, ocat_ref, dm_ref, sg_ref, w2_ref, gnw_ref,
                      dp_ref, dw2_ref, dgb_ref, dgn_ref, dsg, p_scr, dq_scr, dk_scr):
        row = lax.broadcasted_iota(jnp.int32, (CHUNK, 1), 0)
        real = ((chunk * CHUNK + row) >= PAD_ROWS).astype(F32)
        ri, ci = c["ri"], c["ci"]
        causal = ri >= ci
        triu = (ci >= ri).astype(F32)
        qe = c["qs"] * c["ecum"]
        kl = c["k"] * c["ekl"]
        v_all = p_ref[rows, O_GV:O_GV + GLA_HEADS * GLA_DV]
        dos, dps = [], []
        for h in range(GLA_HEADS):
            hs = slice(512 + 128 * h, 512 + 128 * (h + 1))
            v = v_all[:, GLA_DV * h:GLA_DV * (h + 1)]
            g = p_ref[rows, O_GR + 128 * h:O_GR + 128 * (h + 1)]
            o = ocat_ref[rows, hs]
            dy = dm_ref[rows, hs]
            wv = gnw_ref[:, 128 * h:128 * (h + 1)]
            rs = lax.rsqrt(jnp.mean(o * o, axis=-1, keepdims=True) + EPS)
            nrm = o * rs
            sgm = _sigmoid(g)
            sil = g * sgm
            dgn_ref[0:1, 128 * h:128 * (h + 1)] += jnp.sum(dy * nrm * sil, axis=0, keepdims=True)
            dgate = dy * nrm * wv * (sgm * (1.0 + g * (1.0 - sgm)))
            dn = dy * wv * sil
            do = rs * (dn - nrm * jnp.mean(dn * nrm, axis=-1, keepdims=True))
            dp_ref[rows, O_GR + 128 * h:O_GR + 128 * (h + 1)] = (dgate * real).astype(BF16)
            dos.append(do)
        do_all = jnp.concatenate(dos, axis=1)
        do_blocks = jnp.where(c["masks"]["own"], jnp.concatenate([do_all] * GLA_HEADS, axis=0), 0.0)
        dp_all = _dot_nt(do_blocks, v_all)
        dps = [jnp.where(causal, dp_all[CHUNK * h:CHUNK * (h + 1), :], 0.0) for h in range(GLA_HEADS)]
        _gla_all_intra_bwd(c, dps, p_scr.at[j], dq_scr.at[j], dk_scr.at[j], factored)
        s_in = _block_diagonal([sg_ref[j, h] for h in range(GLA_HEADS)])
        ds_out = dsg[...]
        decay = jnp.exp(_dot_tn_exact_lhs(c["la"], jnp.ones((CHUNK, GLA_HEADS * GLA_DV), F32)))
        dv_state = _dot(kl, ds_out)
        dqe = _dot_nt(do_all, s_in)
        dkl = _dot_nt(v_all, ds_out)
        dsg[...] = jnp.where(c["masks"]["own"], _dot_tn(qe, do_all), 0.0) + decay * ds_out
        sd = s_in * ds_out
        sd_hi = sd.astype(BF16)
        sd_lo = (sd - sd_hi.astype(F32)).astype(BF16)
        ones8 = jnp.ones((8, GLA_HEADS * GLA_DV), BF16)
        nt = (((1,), (1,)), ((), ()))
        d_el = (lax.dot_general(ones8, sd_hi, nt, preferred_element_type=F32)
                + lax.dot_general(ones8, sd_lo, nt, preferred_element_type=F32))[0:1, :]
        dqs = dqe * c["ecum"] + dq_scr[j]
        dkk = dkl * c["ekl"] + dk_scr[j]
        d_last = jnp.sum(dkl * kl, axis=0, keepdims=True) + d_el * c["el"]
        dcum = c["qs"] * dqs - c["k"] * dkk + jnp.where(row == CHUNK - 1, d_last, 0.0)
        dla = _dot_exact_rhs(triu, dcum)
        dv = _dot_tn(p_scr[j].reshape(GLA_HEADS * CHUNK, CHUNK), do_blocks) + dv_state
        dp_ref[rows, O_GV:O_GV + GLA_HEADS * GLA_DV] = (dv * real).astype(BF16)
        dp_ref[rows, O_GQ:O_GQ + 256] = (dqs * (GLA_DK ** -0.5) * real).astype(BF16)
        dp_ref[rows, O_GK:O_GK + 256] = (dkk * real).astype(BF16)
        dz = dla * (1.0 / GLA_TAU) * _sigmoid(-c["z"]) * real
        ga = p_ref[rows, O_GA:O_GA + 128]
        dp_ref[rows, O_GA:O_GA + 128] = _dot_nt(dz, w2_ref[...]).astype(BF16)
        dp_ref[rows, O_GA + 128:IN_WP] = jnp.zeros((CHUNK, IN_WP - O_GA - 128), BF16)
        dw2_ref[...] += _dot_tn(ga, dz)
        dgb_ref[0:1, :] += jnp.sum(dz, axis=0, keepdims=True)

    const = lambda shape: pl.BlockSpec(shape, lambda s: (0,) * len(shape))
    rev = lambda s: (last_step - s, 0)
    anywhere = [pl.BlockSpec(memory_space=pl.ANY)] * n_carried
    return pl.pallas_call(
        body, name=name, grid=(N_STEPS,),
        in_specs=[pl.BlockSpec((STEP_ROWS, IN_WP), rev), pl.BlockSpec((STEP_ROWS, D), rev),
                  pl.BlockSpec((STEP_ROWS, D), rev),
                  pl.BlockSpec((1, RET_HEADS, RET_DK, 128), lambda s: (last_step - s, 0, 0, 0)),
                  pl.BlockSpec((CHUNKS_PER_STEP, GLA_HEADS, GLA_DK, GLA_DV), lambda s: (last_step - s, 0, 0, 0)),
                  pl.BlockSpec((STEP_ROWS, 128), rev), pl.BlockSpec((STEP_ROWS, 128), rev),
                  const((128, 256)), const((1, 256)), const((1, 512)), const((1, 512))] + anywhere,
        out_specs=[pl.BlockSpec((STEP_ROWS, IN_WP), rev), const((128, 256)), const((8, 256)),
                   const((8, 512)), const((8, 512))] + anywhere,
        out_shape=[jax.ShapeDtypeStruct((LP, IN_WP), BF16), jax.ShapeDtypeStruct((128, 256), F32),
                   jax.ShapeDtypeStruct((8, 256), F32), jax.ShapeDtypeStruct((8, 512), F32),
                   jax.ShapeDtypeStruct((8, 512), F32)] + [jax.ShapeDtypeStruct(g.shape, g.dtype) for g in carried],
        scratch_shapes=[pltpu.VMEM((RET_HEADS, RET_DK, 128), F32),
                        pltpu.VMEM((GLA_HEADS * GLA_DK, GLA_HEADS * GLA_DV), F32),
                        pltpu.VMEM((CHUNKS_PER_STEP, GLA_HEADS, CHUNK, CHUNK), F32),
                        pltpu.VMEM((CHUNKS_PER_STEP, CHUNK, 256), F32),
                        pltpu.VMEM((CHUNKS_PER_STEP, CHUNK, 256), F32)] + _exchange_sems(n_carried),
        compiler_params=_cparams("arbitrary"),
    )(proj, ocat, dmrg, sr_all, sg_all, cos2, sin2, w2p, gb, rnw, gnw, *carried)


def _all_gather(xs, name):
    n = len(xs)

    def body(*refs):
        start, forward, finish = _gather_phases(refs[:n], refs[n:2 * n], *refs[2 * n:])
        start()
        forward()
        finish()

    return pl.pallas_call(
        body, name=name,
        in_specs=[pl.BlockSpec(memory_space=pl.ANY)] * n,
        out_specs=[pl.BlockSpec(memory_space=pl.ANY)] * n,
        out_shape=_gathered_shapes(xs),
        scratch_shapes=_exchange_sems(n),
    )(*xs)


def _gathered_shapes(xs):
    return [jax.ShapeDtypeStruct((N_DEV,) + x.shape, x.dtype) for x in xs]


def _exchange_sems(n):
    if n == 0:
        return []
    return [pltpu.SemaphoreType.DMA((7 * n,)), pltpu.SemaphoreType.DMA((7 * n,)), pltpu.SemaphoreType.DMA((n,))]


def _gather_phases(x_refs, out_refs, send_sems, recv_sems, local_sems):
    n = len(x_refs)
    mx, my, mc = lax.axis_index("x"), lax.axis_index("y"), lax.axis_index("c")
    me, sibling = (mx, my, mc), (mx, my, 1 - mc)
    chips = [(1 - mx, my), (mx, 1 - my), (1 - mx, 1 - my)]

    def slot(a, px, py, pc):
        return out_refs[a].at[4 * px + 2 * py + pc]

    def copy(a, k, block, to, src=None):
        return pltpu.make_async_remote_copy(
            src_ref=slot(a, *block) if src is None else src, dst_ref=slot(a, *block),
            send_sem=send_sems.at[7 * a + k], recv_sem=recv_sems.at[7 * a + k],
            device_id=to, device_id_type=MESH_IDS)

    mine = [pltpu.make_async_copy(x_refs[a], slot(a, *me), local_sems.at[a]) for a in range(n)]
    first = []
    for a in range(n):
        first.append(copy(a, 0, me, sibling, src=x_refs[a]))
        first += [copy(a, 1 + j, me, (*chip, mc), src=x_refs[a]) for j, chip in enumerate(chips)]
    passed = [copy(a, 4 + j, (*chip, mc), sibling) for j, chip in enumerate(chips) for a in range(n)]

    def start():
        for cp in mine + first:
            cp.start()

    def forward():
        for j, chip in enumerate(chips):
            for a in range(n):
                copy(a, 1 + j, (*chip, mc), me).wait_recv()
                passed[j * n + a].start()

    def finish():
        for a in range(n):
            copy(a, 0, sibling, me).wait_recv()
            for j, chip in enumerate(chips):
                copy(a, 4 + j, (*chip, 1 - mc), me).wait_recv()
        for cp in first + passed:
            cp.wait_send()
        for cp in mine:
            cp.wait()

    return start, forward, finish


def _exchange_blocks(gs, name):
    n = len(gs)

    def body(*refs):
        start, finish = _exchange_phases(refs[:n], refs[n:2 * n], *refs[2 * n:])
        start()
        finish()

    return pl.pallas_call(
        body, name=name,
        in_specs=[pl.BlockSpec(memory_space=pl.ANY)] * n,
        out_specs=[pl.BlockSpec(memory_space=pl.ANY)] * n,
        out_shape=[jax.ShapeDtypeStruct(g.shape, g.dtype) for g in gs],
        scratch_shapes=_exchange_sems(n),
    )(*gs)


def _exchange_phases(g_refs, out_refs, send_sems, recv_sems, local_sems):
    n = len(g_refs)
    mx, my, mc = lax.axis_index("x"), lax.axis_index("y"), lax.axis_index("c")
    me = 4 * mx + 2 * my + mc
    mine = [pltpu.make_async_copy(g_refs[a].at[me], out_refs[a].at[me], local_sems.at[a]) for a in range(n)]
    copies = []
    for r in range(1, N_DEV):
        px, py, pc = mx ^ (r >> 2), my ^ ((r >> 1) & 1), mc ^ (r & 1)
        peer = 4 * px + 2 * py + pc
        for a in range(n):
            copies.append(pltpu.make_async_remote_copy(
                src_ref=g_refs[a].at[peer], dst_ref=out_refs[a].at[me],
                send_sem=send_sems.at[7 * a + r - 1], recv_sem=recv_sems.at[7 * a + r - 1],
                device_id=(px, py, pc), device_id_type=MESH_IDS))

    def start():
        for cp in mine + copies:
            cp.start()

    def finish():
        for cp in copies:
            cp.wait_recv()
        for cp in copies:
            cp.wait_send()
        for cp in mine:
            cp.wait()

    return start, finish


IN_SHARD = IN_W // N_DEV
IN_SHARD_P = 512
UP_SHARD = D_UP // N_DEV
UP_SHARD_P = 768
RELAYOUT_ROWS = 256


def _pieces_w_in():
    return [(k, 0, IN_SHARD * k, IN_SHARD) for k in range(N_DEV)]


def _pieces_ffn_up():
    pieces = []
    for k in range(N_DEV):
        n, end = UP_SHARD * k, UP_SHARD * (k + 1)
        while n < end:
            half, r = divmod(n, D_FF)
            blk, off = divmod(r, CONV_BLOCK)
            run = min(CONV_BLOCK - off, end - n)
            pieces.append((k, n - UP_SHARD * k, 2 * CONV_BLOCK * blk + CONV_BLOCK * half + off, run))
            n += run
    return pieces


def _assemble_block(load, spans, dst_block, rows):
    lo = 128 * dst_block
    lane = lax.broadcasted_iota(jnp.int32, (1, 128), 1)
    out = jnp.zeros((rows, 128), F32)
    for key, src_off, dst_off, length in spans:
        a, b = max(lo, dst_off), min(lo + 128, dst_off + length)
        s, s_end = src_off + (a - dst_off), src_off + (b - dst_off)
        d = a
        while s < s_end:
            e = min(s_end, 128 * (s // 128 + 1))
            blk = load(key, s // 128)
            shift = (d - s) % 128
            if shift:
                blk = pltpu.roll(blk, shift, 1)
            out = jnp.where((lane >= d - lo) & (lane < d - lo + (e - s)), blk, out)
            d += e - s
            s = e
    return out


def _shards_to_cols(shards, pieces, width, name):
    _, rows, _ = shards.shape
    tr = RELAYOUT_ROWS

    def body(s_ref, o_ref):
        load = lambda k, b: s_ref[k, :, 128 * b:128 * (b + 1)].astype(F32)
        for db in range(width // 128):
            o_ref[:, 128 * db:128 * (db + 1)] = _assemble_block(load, pieces, db, tr).astype(BF16)

    return pl.pallas_call(
        body, name=name, grid=(rows // tr,),
        in_specs=[pl.BlockSpec((N_DEV, tr, shards.shape[2]), lambda i: (0, i, 0))],
        out_specs=pl.BlockSpec((tr, width), lambda i: (i, 0)),
        out_shape=jax.ShapeDtypeStruct((rows, width), BF16),
        compiler_params=_cparams("parallel"),
    )(shards)


def _cols_to_shards(full, pieces, shard_width, name):
    rows, width = full.shape
    tr = RELAYOUT_ROWS

    def body(f_ref, o_ref):
        load = lambda _, b: f_ref[:, 128 * b:128 * (b + 1)].astype(F32)
        for k in range(N_DEV):
            spans = [(None, dst_off, src_off, length) for dev, src_off, dst_off, length in pieces if dev == k]
            for db in range(shard_width // 128):
                o_ref[k, :, 128 * db:128 * (db + 1)] = _assemble_block(load, spans, db, tr).astype(BF16)

    return pl.pallas_call(
        body, name=name, grid=(rows // tr,),
        in_specs=[pl.BlockSpec((tr, width), lambda i: (i, 0))],
        out_specs=pl.BlockSpec((N_DEV, tr, shard_width), lambda i: (0, i, 0)),
        out_shape=jax.ShapeDtypeStruct((N_DEV, rows, shard_width), BF16),
        compiler_params=_cparams("parallel"),
    )(full)


def _adamw(parts, w, m, v, rows_per_step, name):
    rows, cols = w.shape
    assert rows % rows_per_step == 0 and parts.shape == (N_DEV, rows, cols)

    def body(p_ref, w_ref, m_ref, v_ref, g_ref, d_ref, nm_ref, nv_ref):
        g = p_ref[0].astype(F32)
        for j in range(1, N_DEV):
            g = g + p_ref[j].astype(F32)
        m_new = ADAM_B1 * m_ref[...] + (1.0 - ADAM_B1) * g
        v_new = ADAM_B2 * v_ref[...] + (1.0 - ADAM_B2) * (g * g)
        m_hat = m_new / (1.0 - ADAM_B1 ** ADAM_STEP)
        v_hat = v_new / (1.0 - ADAM_B2 ** ADAM_STEP)
        g_ref[...] = g
        d_ref[...] = -ADAM_LR * (m_hat / (jnp.sqrt(v_hat) + ADAM_EPS) + ADAM_WD * w_ref[...])
        nm_ref[...] = m_new
        nv_ref[...] = v_new

    tile = pl.BlockSpec((rows_per_step, cols), lambda i: (i, 0))
    shape = jax.ShapeDtypeStruct((rows, cols), F32)
    return pl.pallas_call(
        body, name=name, grid=(rows // rows_per_step,),
        in_specs=[pl.BlockSpec((N_DEV, rows_per_step, cols), lambda i: (0, i, 0)), tile, tile, tile],
        out_specs=[tile, tile, tile, tile],
        out_shape=[shape, shape, shape, shape],
        compiler_params=_cparams("parallel"),
    )(parts, w, m, v)


BIG = (("w_in", (DEPTH, D, IN_W // N_DEV), 2), ("w_out", (DEPTH, D // N_DEV, D), 1),
       ("ffn_up", (DEPTH, D, D_UP // N_DEV), 2), ("ffn_down", (DEPTH, D_FF // N_DEV, D), 1))
SMALL = (("meta_tokens", (N_META, D // N_DEV), 1), ("gla_gate_w2", (DEPTH, GATE_RANK, 256 // N_DEV), 2),
         ("ffn_conv_w", (DEPTH, 3, D_UP // N_DEV), 2))
REPL = (("pre_mix_norm", (DEPTH, D)), ("gla_gate_b", (DEPTH, 256)), ("ret_norm_w", (DEPTH, 512)),
        ("gla_norm_w", (DEPTH, 512)), ("post_mix_norm", (DEPTH, D)), ("pre_ffn_norm", (DEPTH, D)),
        ("ffn_conv_b", (DEPTH, D_UP)), ("post_ffn_norm", (DEPTH, D)))
WEIGHT_ORDER = ("meta_tokens", "pre_mix_norm", "w_in", "gla_gate_w2", "gla_gate_b", "ret_norm_w", "gla_norm_w",
                "w_out", "post_mix_norm", "pre_ffn_norm", "ffn_up", "ffn_conv_w", "ffn_conv_b", "ffn_down",
                "post_ffn_norm")


def _size(shape):
    return math.prod(shape)


def _round_up(n, mult):
    return -(-n // mult) * mult


REPL_ROWS = _round_up(-(-sum(_size(s) for _, s in REPL) // LANES), 8)
SMALL_ROWS = _round_up(-(-sum(_size(s) for _, s, _ in SMALL) // LANES), 8)


def _pack(arrays, rows, dtype):
    flat = jnp.concatenate([a.reshape(-1).astype(dtype) for a in arrays])
    return jnp.pad(flat, (0, rows * LANES - flat.shape[0])).reshape(rows, LANES)


def _unpack(buf, shapes):
    flat = buf.reshape(-1)
    out, off = [], 0
    for shape in shapes:
        out.append(flat[off:off + _size(shape)].reshape(shape))
        off += _size(shape)
    return out


def _unshard(blocks, axis):
    moved = jnp.moveaxis(blocks, 0, axis)
    shape = list(moved.shape)
    shape[axis:axis + 2] = [shape[axis] * shape[axis + 1]]
    return moved.reshape(shape)


def _to_blocks(full, axis):
    shape = list(full.shape)
    shape[axis:axis + 1] = [N_DEV, shape[axis] // N_DEV]
    return jnp.moveaxis(full.reshape(shape), axis, 0)


def _interleave_cols(w):
    lead = w.shape[:-1]
    return jnp.swapaxes(w.reshape(lead + (2, N_CONV_BLOCKS, CONV_BLOCK)), -3, -2).reshape(lead + (D_UP,))


def _deinterleave_cols(w):
    lead = w.shape[:-1]
    return jnp.swapaxes(w.reshape(lead + (N_CONV_BLOCKS, 2, CONV_BLOCK)), -3, -2).reshape(lead + (D_UP,))


def _rope_tables():
    half = RET_DK // 2
    inv = ROPE_BASE ** (-jnp.arange(half, dtype=F32) / half)
    pos = jnp.arange(LP, dtype=F32) - float(PAD_ROWS)
    ang = pos[:, None] * inv[None, :]
    c, s = jnp.cos(ang), jnp.sin(ang)
    return jnp.concatenate([c, c], axis=1), jnp.concatenate([-s, s], axis=1)


def kernel(x, meta_tokens, pre_mix_norm, w_in, gla_gate_w2, gla_gate_b, ret_norm_w, gla_norm_w, w_out, post_mix_norm, pre_ffn_norm, ffn_up, ffn_conv_w, ffn_conv_b, ffn_down, post_ffn_norm, loss_target, m_meta_tokens, m_pre_mix_norm, m_w_in, m_gla_gate_w2, m_gla_gate_b, m_ret_norm_w, m_gla_norm_w, m_w_out, m_post_mix_norm, m_pre_ffn_norm, m_ffn_up, m_ffn_conv_w, m_ffn_conv_b, m_ffn_down, m_post_ffn_norm, v_meta_tokens, v_pre_mix_norm, v_w_in, v_gla_gate_w2, v_gla_gate_b, v_ret_norm_w, v_gla_norm_w, v_w_out, v_post_mix_norm, v_pre_ffn_norm, v_ffn_up, v_ffn_conv_w, v_ffn_conv_b, v_ffn_down, v_post_ffn_norm):
    weights = dict(meta_tokens=meta_tokens, pre_mix_norm=pre_mix_norm, w_in=w_in, gla_gate_w2=gla_gate_w2,
                   gla_gate_b=gla_gate_b, ret_norm_w=ret_norm_w, gla_norm_w=gla_norm_w, w_out=w_out,
                   post_mix_norm=post_mix_norm, pre_ffn_norm=pre_ffn_norm, ffn_up=ffn_up, ffn_conv_w=ffn_conv_w,
                   ffn_conv_b=ffn_conv_b, ffn_down=ffn_down, post_ffn_norm=post_ffn_norm)
    mom1 = dict(meta_tokens=m_meta_tokens, pre_mix_norm=m_pre_mix_norm, w_in=m_w_in, gla_gate_w2=m_gla_gate_w2,
                gla_gate_b=m_gla_gate_b, ret_norm_w=m_ret_norm_w, gla_norm_w=m_gla_norm_w, w_out=m_w_out,
                post_mix_norm=m_post_mix_norm, pre_ffn_norm=m_pre_ffn_norm, ffn_up=m_ffn_up,
                ffn_conv_w=m_ffn_conv_w, ffn_conv_b=m_ffn_conv_b, ffn_down=m_ffn_down, post_ffn_norm=m_post_ffn_norm)
    mom2 = dict(meta_tokens=v_meta_tokens, pre_mix_norm=v_pre_mix_norm, w_in=v_w_in, gla_gate_w2=v_gla_gate_w2,
                gla_gate_b=v_gla_gate_b, ret_norm_w=v_ret_norm_w, gla_norm_w=v_gla_norm_w, w_out=v_w_out,
                post_mix_norm=v_post_mix_norm, pre_ffn_norm=v_pre_ffn_norm, ffn_up=v_ffn_up,
                ffn_conv_w=v_ffn_conv_w, ffn_conv_b=v_ffn_conv_b, ffn_down=v_ffn_down, post_ffn_norm=v_post_ffn_norm)

    pad_cols = lambda a, width: jnp.pad(a, ((0, 0), (0, width - a.shape[1])))
    big_names = [n for n, _, _ in BIG]
    shard = {}
    for l in range(DEPTH):
        shard[l, "w_in"] = pad_cols(w_in[l].astype(BF16), IN_SHARD_P)
        shard[l, "w_out"] = w_out[l].astype(BF16)
        shard[l, "ffn_up"] = pad_cols(ffn_up[l].astype(BF16), UP_SHARD_P)
        shard[l, "ffn_down"] = ffn_down[l].astype(BF16)
    gathered = {(0, "w_in"): _all_gather([shard[0, "w_in"]], "gather_w_in_0")[0]}
    gather_in_mixer = {l: [(l, n) for n in big_names[1:]] for l in range(DEPTH)}
    gather_in_conv = {l: [(l + 1, "w_in")] for l in range(DEPTH - 1)}
    small = _all_gather([_pack([weights[n] for n, _, _ in SMALL], SMALL_ROWS, F32)], "gather_small_weights")[0]
    small_parts = _unpack_blocks(small, [s for _, s, _ in SMALL])
    full = {n: _unshard(p, ax) for (n, _, ax), p in zip(SMALL, small_parts)}
    w2p = jnp.pad(full["gla_gate_w2"], ((0, 0), (0, 128 - GATE_RANK), (0, 0)))
    cw8 = jnp.concatenate([_interleave_cols(full["ffn_conv_w"]), _interleave_cols(ffn_conv_b)[:, None, :],
                           jnp.zeros((DEPTH, 4, D_UP), F32)], axis=1)
    cos2, sin2 = _rope_tables()

    h = jnp.concatenate([jnp.zeros((PAD_ROWS, D), F32), full["meta_tokens"], x[0]], axis=0)
    target = jnp.concatenate([jnp.zeros((CHUNK, D), F32), loss_target[0]], axis=0)
    saved, layer_w = [], []
    for l in range(DEPTH):
        lw = dict(w_in=_shards_to_cols(gathered[l, "w_in"], _pieces_w_in(), IN_WP, f"w_in_cols_{l}"))
        a1 = _rmsnorm_fwd(h, pre_mix_norm[l:l + 1], f"pre_mix_norm_{l}")
        proj = _matmul(a1, lw["w_in"], out_dtype=F32, tm=TK_ROWS, tn=1280, tk=D, name=f"in_proj_{l}", n_outer=True)
        keys = gather_in_mixer.get(l, [])
        ocat, merged, sr_all, sg_all, *got = _mixer_fwd(proj, cos2, sin2, w2p[l], gla_gate_b[l:l + 1],
                                                        ret_norm_w[l:l + 1], gla_norm_w[l:l + 1], f"mixer_fwd_{l}",
                                                        carried=[shard[key] for key in keys])
        gathered.update(zip(keys, got))
        lw["w_out"] = gathered[l, "w_out"].reshape(D, D)
        lw["w_up"] = _shards_to_cols(gathered[l, "ffn_up"], _pieces_ffn_up(), D_UP, f"ffn_up_cols_{l}")
        lw["w_down"] = gathered[l, "ffn_down"].reshape(D_FF, D)
        layer_w.append(lw)
        m, h1 = _matmul_resid_norm(merged, lw["w_out"], h, post_mix_norm[l:l + 1], f"out_proj_{l}")
        a2 = _rmsnorm_fwd(h1, pre_ffn_norm[l:l + 1], f"pre_ffn_norm_{l}")
        u = _matmul(a2, lw["w_up"], out_dtype=BF16, tm=TK_ROWS, tn=1408, tk=D, name=f"ffn_up_{l}", n_outer=True)
        keys = gather_in_conv.get(l, [])
        cv, act, *got = _conv_act_fwd(u, cw8[l], f"ffn_conv_act_{l}", carried=[shard[key] for key in keys])
        gathered.update(zip(keys, got))
        f, h2, *loss_acc = _matmul_resid_norm(act, lw["w_down"], h1, post_ffn_norm[l:l + 1], f"ffn_down_{l}",
                                              target=target if l == DEPTH - 1 else None)
        saved.append(dict(h=h, a1=a1, proj=proj, ocat=ocat, merged=merged, sr=sr_all, sg=sg_all, m=m, h1=h1,
                          a2=a2, u=u, cv=cv, act=act, f=f))
        h = h2

    dh = h
    loss = lax.psum(loss_acc[0][0, 0], ("x", "y", "c"))

    kinds = ("grad", "delta", "new_m", "new_v")
    grads = {n: [None] * DEPTH for n in WEIGHT_ORDER if n != "meta_tokens" and n not in big_names}
    pending, parts = [], {}
    for l in reversed(range(DEPTH)):
        s, lw = saved[l], layer_w[l]
        dact, df, g_post_ffn = _norm_bwd_matmul(dh, s["f"], post_ffn_norm[l:l + 1], lw["w_down"], BF16,
                                                f"ffn_down_dx_{l}")
        g_down = _matmul(s["act"], df, ta=True, out_dtype=BF16, tm=D_FF // 2, tn=D, tk=TK_ROWS, name=f"ffn_down_dw_{l}")
        du, dcw = _conv_act_bwd(dact, s["cv"], s["u"], cw8[l], f"ffn_conv_act_bwd_{l}")
        dh1, g_pre_ffn = _matmul_norm_bwd(du, lw["w_up"], s["h1"], pre_ffn_norm[l:l + 1], dh, 1408, f"ffn_up_dx_{l}")
        g_up = _matmul(s["a2"], du, ta=True, out_dtype=BF16, tm=D, tn=1408, tk=TK_ROWS, name=f"ffn_up_dw_{l}")
        dmerged, dm, g_post_mix = _norm_bwd_matmul(dh1, s["m"], post_mix_norm[l:l + 1], lw["w_out"], F32,
                                                   f"out_proj_dx_{l}")
        g_out = _matmul(s["merged"], dm, ta=True, out_dtype=BF16, tm=D, tn=D, tk=TK_ROWS, name=f"out_proj_dw_{l}")
        pending += [((l, "ffn_down"), g_down.reshape(N_DEV, D_FF // N_DEV, D)),
                    ((l, "ffn_up"), _cols_to_shards(g_up, _pieces_ffn_up(), UP_SHARD_P, f"ffn_up_grad_shards_{l}")),
                    ((l, "w_out"), g_out.reshape(N_DEV, D // N_DEV, D))]
        dproj, g_w2, g_gb, g_rn, g_gn, *got = _mixer_bwd(s["proj"], s["ocat"], dmerged, s["sr"], s["sg"], cos2, sin2,
                                                         w2p[l], gla_gate_b[l:l + 1], ret_norm_w[l:l + 1],
                                                         gla_norm_w[l:l + 1], f"mixer_bwd_{l}",
                                                         carried=[blocks for _, blocks in pending])
        parts.update(zip([key for key, _ in pending], got))
        g_in = _matmul(s["a1"], dproj, ta=True, out_dtype=BF16, tm=D, tn=1280, tk=TK_ROWS, name=f"in_proj_dw_{l}")
        pending = [((l, "w_in"), _cols_to_shards(g_in, _pieces_w_in(), IN_SHARD_P, f"w_in_grad_shards_{l}"))]
        now = pending if l == 0 else []
        dh, g_pre_mix, *got = _matmul_norm_bwd(dproj, lw["w_in"], s["h"], pre_mix_norm[l:l + 1], dh1, IN_WP,
                                               f"in_proj_dx_{l}", carried=[blocks for _, blocks in now])
        parts.update(zip([key for key, _ in now], got))
        pending = [] if l == 0 else pending
        grads["post_ffn_norm"][l] = g_post_ffn[0]
        grads["ffn_conv_w"][l] = _deinterleave_cols(dcw[0:3])
        grads["ffn_conv_b"][l] = _deinterleave_cols(dcw[3])
        grads["pre_ffn_norm"][l] = g_pre_ffn[0]
        grads["post_mix_norm"][l] = g_post_mix[0]
        grads["gla_gate_w2"][l] = g_w2[:GATE_RANK]
        grads["gla_gate_b"][l] = g_gb[0]
        grads["ret_norm_w"][l] = g_rn[0]
        grads["gla_norm_w"][l] = g_gn[0]
        grads["pre_mix_norm"][l] = g_pre_mix[0]
    local = {n: jnp.stack(v) for n, v in grads.items()}
    local["meta_tokens"] = dh[PAD_ROWS:CHUNK]
    grad_x = dh[CHUNK:][None]

    blocks = jnp.concatenate([_to_blocks(local[n], ax).reshape(N_DEV, -1) for n, _, ax in SMALL], axis=1)
    blocks = jnp.pad(blocks, ((0, 0), (0, SMALL_ROWS * LANES - blocks.shape[1]))).reshape(N_DEV, SMALL_ROWS, LANES)
    *got, small_grad_parts = _exchange_blocks([b for _, b in pending] + [blocks], "exchange_last_grads")
    parts.update(zip([key for key, _ in pending], got))

    widths = dict(w_in=IN_SHARD_P, w_out=D, ffn_up=UP_SHARD_P, ffn_down=D)
    steps = dict(w_in=256, w_out=D // N_DEV, ffn_up=256, ffn_down=D_FF // N_DEV // 2)
    big_out = {kind: {n: [None] * DEPTH for n in big_names} for kind in kinds}
    for l in range(DEPTH):
        for n in big_names:
            mine = [pad_cols(d[n][l], widths[n]) for d in (weights, mom1, mom2)]
            results = _adamw(parts[l, n], *mine, steps[n], f"adamw_{n}_{l}")
            for kind, r in zip(kinds, results):
                big_out[kind][n][l] = r[:, :weights[n].shape[2]]
    out = {kind: {n: jnp.stack(v) for n, v in big_out[kind].items()} for kind in kinds}
    shard_shapes = [s for _, s, _ in SMALL]
    packed = [_pack([d[n] for n, _, _ in SMALL], SMALL_ROWS, F32) for d in (weights, mom1, mom2)]
    results = _adamw(small_grad_parts, *packed, SMALL_ROWS, "adamw_small_sharded")
    for kind, buf in zip(kinds, results):
        out[kind].update(zip([n for n, _, _ in SMALL], _unpack(buf, shard_shapes)))

    repl_parts = _all_gather([_pack([local[n] for n, _ in REPL], REPL_ROWS, F32)], "gather_small_grads")[0]
    packed = [_pack([d[n] for n, _ in REPL], REPL_ROWS, F32) for d in (weights, mom1, mom2)]
    results = _adamw(repl_parts, *packed, REPL_ROWS, "adamw_replicated")
    repl_shapes = [s for _, s in REPL]
    for kind, buf in zip(kinds, results):
        out[kind].update(zip([n for n, _ in REPL], _unpack(buf, repl_shapes)))

    return (loss, grad_x, *[out["grad"][n] for n in WEIGHT_ORDER], *[out["delta"][n] for n in WEIGHT_ORDER],
            *[out["new_m"][n] for n in WEIGHT_ORDER], *[out["new_v"][n] for n in WEIGHT_ORDER])


def _unpack_blocks(gathered, shapes):
    flat = gathered.reshape(N_DEV, -1)
    out, off = [], 0
    for shape in shapes:
        out.append(flat[:, off:off + _size(shape)].reshape((N_DEV,) + shape))
        off += _size(shape)
    return out
```

```python
import math

import jax
import jax.numpy as jnp
from jax import lax
from jax.experimental import pallas as pl
from jax.experimental.pallas import tpu as pltpu

F32 = jnp.float32
BF16 = jnp.bfloat16

D = 1024
SEQ = 8192
DEPTH = 2
N_META = 16
CHUNK = 64
SUB = 16
N_SUB = CHUNK // SUB
PAD_ROWS = CHUNK - N_META
LP = SEQ + CHUNK
N_CHUNKS = LP // CHUNK
RET_HEADS = 4
RET_DK = 128
GLA_HEADS = 4
GLA_DK = 64
GLA_DV = 128
GLA_TAU = 16.0
GATE_RANK = 16
IN_W = 3600
IN_WP = 3840
D_FF = 2816
D_UP = 2 * D_FF
CONV_BLOCK = 256
N_CONV_BLOCKS = D_FF // CONV_BLOCK
ROPE_BASE = 10000.0
EPS = 1e-6
N_DEV = 8
LANES = 1024

O_RQ, O_RK, O_RV, O_RG = 0, 512, 1024, 1536
O_GQ, O_GK, O_GV, O_GR, O_GA = 2048, 2304, 2560, 3072, 3584

ADAM_LR = 0.001
ADAM_B1 = 0.9
ADAM_B2 = 0.999
ADAM_EPS = 1e-08
ADAM_WD = 0.01
ADAM_STEP = 10

VMEM_LIMIT = 56 * 1024 * 1024
MESH_IDS = pl.DeviceIdType.MESH


def _row_tile(rows, limit):
    best = 16
    for t in range(16, min(rows, limit) + 1, 16):
        if rows % t == 0:
            best = t
    return best


TM = _row_tile(LP, 688)
TK_ROWS = _row_tile(LP, 1376)


def _cparams(*sem):
    return pltpu.CompilerParams(dimension_semantics=sem, vmem_limit_bytes=VMEM_LIMIT)


def _dot(a, b):
    return jnp.dot(a.astype(BF16), b.astype(BF16), preferred_element_type=F32)


def _dot_nt(a, b):
    return lax.dot_general(a.astype(BF16), b.astype(BF16), (((1,), (1,)), ((), ())), preferred_element_type=F32)


def _dot_tn(a, b):
    return lax.dot_general(a.astype(BF16), b.astype(BF16), (((0,), (0,)), ((), ())), preferred_element_type=F32)


def _split3(x):
    hi = x.astype(BF16)
    r1 = x - hi.astype(F32)
    mid = r1.astype(BF16)
    lo = (r1 - mid.astype(F32)).astype(BF16)
    return hi, mid, lo


def _dot_exact_rhs(t, x):
    n = x.shape[1]
    parts = jnp.dot(t.astype(BF16), jnp.concatenate(_split3(x), axis=1), preferred_element_type=F32)
    return parts[:, :n] + parts[:, n:2 * n] + parts[:, 2 * n:]


def _dot_tn_exact_lhs(x, ones):
    n = x.shape[1]
    parts = lax.dot_general(jnp.concatenate(_split3(x), axis=1), ones.astype(BF16), (((0,), (0,)), ((), ())),
                            preferred_element_type=F32)
    return parts[:n] + parts[n:2 * n] + parts[2 * n:]


def _sigmoid(x):
    return 1.0 / (1.0 + jnp.exp(-x))


def _matmul(a, b, *, ta=False, tb=False, out_dtype, tm, tn, tk, name, n_outer=False):
    m = a.shape[1] if ta else a.shape[0]
    k = a.shape[0] if ta else a.shape[1]
    n = b.shape[0] if tb else b.shape[1]
    assert (b.shape[1] if tb else b.shape[0]) == k
    assert m % tm == 0 and n % tn == 0 and k % tk == 0, (name, m, n, k, tm, tn, tk)
    nk = k // tk
    order = (lambda f: (lambda j, i, kk: f(i, j, kk))) if n_outer else (lambda f: f)
    a_spec = (pl.BlockSpec((tk, tm), order(lambda i, j, kk: (kk, i))) if ta
              else pl.BlockSpec((tm, tk), order(lambda i, j, kk: (i, kk))))
    b_spec = (pl.BlockSpec((tn, tk), order(lambda i, j, kk: (j, kk))) if tb
              else pl.BlockSpec((tk, tn), order(lambda i, j, kk: (kk, j))))
    dims = (((0 if ta else 1,), (1 if tb else 0,)), ((), ()))

    def body(a_ref, b_ref, o_ref, *acc):
        prod = lax.dot_general(a_ref[...].astype(BF16), b_ref[...].astype(BF16), dims, preferred_element_type=F32)
        if nk == 1:
            o_ref[...] = prod.astype(out_dtype)
            return
        acc_ref, = acc
        kk = pl.program_id(2)

        @pl.when(kk == 0)
        def _():
            acc_ref[...] = prod

        @pl.when(kk > 0)
        def _():
            acc_ref[...] += prod

        @pl.when(kk == nk - 1)
        def _():
            o_ref[...] = acc_ref[...].astype(out_dtype)

    return pl.pallas_call(
        body, name=name, grid=(n // tn, m // tm, nk) if n_outer else (m // tm, n // tn, nk),
        in_specs=[a_spec, b_spec],
        out_specs=pl.BlockSpec((tm, tn), order(lambda i, j, kk: (i, j))),
        out_shape=jax.ShapeDtypeStruct((m, n), out_dtype),
        scratch_shapes=[pltpu.VMEM((tm, tn), F32)] if nk > 1 else [],
        compiler_params=_cparams("parallel", "parallel", "arbitrary"),
    )(a, b)


def _matmul_resid_norm(a, b, h, w, name, target=None):
    k = a.shape[1]
    has_loss = target is not None

    def body(a_ref, b_ref, h_ref, w_ref, *refs):
        m = jnp.dot(a_ref[...].astype(BF16), b_ref[...].astype(BF16), preferred_element_type=F32)
        r = lax.rsqrt(jnp.mean(m * m, axis=-1, keepdims=True) + EPS)
        i = pl.program_id(0)
        row = i * TM + lax.broadcasted_iota(jnp.int32, (TM, 1), 0)
        y = h_ref[...] + jnp.where(row >= PAD_ROWS, m * r * w_ref[...], 0.0)
        if not has_loss:
            m_ref, y_ref = refs
            m_ref[...] = m
            y_ref[...] = y
            return
        t_ref, m_ref, dy_ref, loss_ref = refs
        m_ref[...] = m

        @pl.when(i == 0)
        def _():
            loss_ref[...] = jnp.zeros_like(loss_ref)

        diff = jnp.where(row >= CHUNK, y - t_ref[...], 0.0)
        dy_ref[...] = diff * (1.0 / D)
        loss_ref[...] += (0.5 / D) * jnp.sum(diff * diff)

    tile = pl.BlockSpec((TM, D), lambda i: (i, 0))
    shape = jax.ShapeDtypeStruct((LP, D), F32)
    in_specs = [pl.BlockSpec((TM, k), lambda i: (i, 0)), pl.BlockSpec((k, D), lambda i: (0, 0)), tile,
                pl.BlockSpec((1, D), lambda i: (0, 0))]
    if has_loss:
        return pl.pallas_call(
            body, name=name, grid=(LP // TM,),
            in_specs=in_specs + [tile],
            out_specs=[tile, tile, pl.BlockSpec((8, 128), lambda i: (0, 0))],
            out_shape=[shape, shape, jax.ShapeDtypeStruct((8, 128), F32)],
            compiler_params=_cparams("arbitrary"),
        )(a, b, h, w, target)
    return pl.pallas_call(
        body, name=name, grid=(LP // TM,),
        in_specs=in_specs, out_specs=[tile, tile], out_shape=[shape, shape],
        compiler_params=_cparams("parallel"),
    )(a, b, h, w)


def _rmsnorm_bwd_rows(dy, x, w):
    r = lax.rsqrt(jnp.mean(x * x, axis=-1, keepdims=True) + EPS)
    g = dy * w
    dx = r * g - x * (r * r * r * jnp.mean(g * x, axis=-1, keepdims=True))
    return dx, jnp.sum(dy * x * r, axis=0, keepdims=True)


def _matmul_norm_bwd(dz, b, x, w, resid, tk, name, carried=()):
    k = dz.shape[1]
    assert k % tk == 0
    nk = k // tk
    n_rows = LP // TM
    n_carried = len(carried)

    def body(*refs):
        a_ref, b_ref, x_ref, w_ref, r_ref = refs[:5]
        g_refs, refs = refs[5:5 + n_carried], refs[5 + n_carried:]
        dx_ref, dw_ref = refs[:2]
        got_refs, refs = refs[2:2 + n_carried], refs[2 + n_carried:]
        acc, sems = (refs[:1], refs[1:]) if nk > 1 else ((), refs)
        i, kk = pl.program_id(0), pl.program_id(1)
        if n_carried:
            exchange_start, exchange_finish = _exchange_phases(g_refs, got_refs, *sems)
            pl.when((i == 0) & (kk == 0))(exchange_start)

        @pl.when((i == 0) & (kk == 0))
        def _():
            dw_ref[...] = jnp.zeros_like(dw_ref)

        prod = lax.dot_general(a_ref[...].astype(BF16), b_ref[...].astype(BF16), (((1,), (1,)), ((), ())),
                               preferred_element_type=F32)

        def finish(dy):
            dx, dw = _rmsnorm_bwd_rows(dy, x_ref[...], w_ref[...])
            dx_ref[...] = dx + r_ref[...]
            dw_ref[0:1, :] += dw

        if nk == 1:
            finish(prod)
        else:
            acc_ref, = acc

            @pl.when(kk == 0)
            def _():
                acc_ref[...] = prod

            @pl.when((kk > 0) & (kk < nk - 1))
            def _():
                acc_ref[...] += prod

            @pl.when(kk == nk - 1)
            def _():
                finish(acc_ref[...] + prod)

        if n_carried:
            pl.when((i == n_rows - 1) & (kk == nk - 1))(exchange_finish)

    tile = pl.BlockSpec((TM, D), lambda i, kk: (i, 0))
    anywhere = [pl.BlockSpec(memory_space=pl.ANY)] * n_carried
    return pl.pallas_call(
        body, name=name, grid=(n_rows, nk),
        in_specs=[pl.BlockSpec((TM, tk), lambda i, kk: (i, kk)), pl.BlockSpec((D, tk), lambda i, kk: (0, kk)), tile,
                  pl.BlockSpec((1, D), lambda i, kk: (0, 0)), tile] + anywhere,
        out_specs=[tile, pl.BlockSpec((8, D), lambda i, kk: (0, 0))] + anywhere,
        out_shape=[jax.ShapeDtypeStruct((LP, D), F32), jax.ShapeDtypeStruct((8, D), F32)]
        + [jax.ShapeDtypeStruct(g.shape, g.dtype) for g in carried],
        scratch_shapes=([pltpu.VMEM((TM, D), F32)] if nk > 1 else []) + _exchange_sems(n_carried),
        compiler_params=_cparams("arbitrary", "arbitrary"),
    )(dz, b, x, w, resid, *carried)


def _norm_bwd_matmul(dh, x, w, b, out_dtype, name):
    n = b.shape[0]

    def body(dh_ref, x_ref, w_ref, b_ref, o_ref, dx_ref, dw_ref):
        i = pl.program_id(0)

        @pl.when(i == 0)
        def _():
            dw_ref[...] = jnp.zeros_like(dw_ref)

        row = i * TM + lax.broadcasted_iota(jnp.int32, (TM, 1), 0)
        dy = jnp.where(row >= PAD_ROWS, dh_ref[...], 0.0)
        dx, dw = _rmsnorm_bwd_rows(dy, x_ref[...], w_ref[...])
        dxb = dx.astype(BF16)
        dx_ref[...] = dxb
        dw_ref[0:1, :] += dw
        o_ref[...] = lax.dot_general(dxb, b_ref[...].astype(BF16), (((1,), (1,)), ((), ())),
                                     preferred_element_type=F32).astype(out_dtype)

    tile = pl.BlockSpec((TM, D), lambda i: (i, 0))
    return pl.pallas_call(
        body, name=name, grid=(LP // TM,),
        in_specs=[tile, tile, pl.BlockSpec((1, D), lambda i: (0, 0)), pl.BlockSpec((n, D), lambda i: (0, 0))],
        out_specs=[pl.BlockSpec((TM, n), lambda i: (i, 0)), tile, pl.BlockSpec((8, D), lambda i: (0, 0))],
        out_shape=[jax.ShapeDtypeStruct((LP, n), out_dtype), jax.ShapeDtypeStruct((LP, D), BF16),
                   jax.ShapeDtypeStruct((8, D), F32)],
        compiler_params=_cparams("arbitrary"),
    )(dh, x, w, b)


def _rmsnorm_fwd(x, w, name):
    def body(x_ref, w_ref, o_ref):
        xv = x_ref[...]
        r = lax.rsqrt(jnp.mean(xv * xv, axis=-1, keepdims=True) + EPS)
        o_ref[...] = (xv * r * w_ref[...]).astype(BF16)

    return pl.pallas_call(
        body, name=name, grid=(LP // TM,),
        in_specs=[pl.BlockSpec((TM, D), lambda i: (i, 0)), pl.BlockSpec((1, D), lambda i: (0, 0))],
        out_specs=pl.BlockSpec((TM, D), lambda i: (i, 0)),
        out_shape=jax.ShapeDtypeStruct((LP, D), BF16),
        compiler_params=_cparams("parallel"),
    )(x, w)


GELU_C = math.sqrt(2.0 / math.pi)
GELU_K = 0.044715
STRIP = 16
HALF = 8


def _gelu_half(a):
    return 0.5 * jnp.tanh(a * (a * a * (GELU_C * GELU_K) + GELU_C)) + 0.5


def _gelu_slope(a, h):
    return h * (1.0 + (a - a * h) * (a * a * (6.0 * GELU_C * GELU_K) + 2.0 * GELU_C))


def _shift_down(x, prev8, rows):
    row = lax.broadcasted_iota(jnp.int32, (rows, 1), 0)
    p1 = pltpu.roll(prev8, 1, 0)
    p2 = pltpu.roll(prev8, 2, 0)
    x1 = jnp.where(row == 0, p1[0:1, :], pltpu.roll(x, 1, 0))
    x2 = jnp.where(row == 0, p2[0:1, :], jnp.where(row == 1, p2[1:2, :], pltpu.roll(x, 2, 0)))
    return x1, x2


def _conv_act_fwd(u, cw8, name, carried=()):
    n_rows = LP // TM
    cb2 = 2 * CONV_BLOCK
    n_carried = len(carried)

    def body(*refs):
        u_ref, cw_ref = refs[:2]
        x_refs, refs = refs[2:2 + n_carried], refs[2 + n_carried:]
        conv_ref, act_ref = refs[:2]
        gathered_refs, refs = refs[2:2 + n_carried], refs[2 + n_carried:]
        carry_ref = refs[0]
        j, i = pl.program_id(0), pl.program_id(1)
        if n_carried:
            start, forward, finish = _gather_phases(x_refs, gathered_refs, *refs[1:])
            pl.when((j == 0) & (i == 0))(start)
            pl.when((j == (3 * N_CONV_BLOCKS) // 4) & (i == 0))(forward)

        @pl.when(i == 0)
        def _():
            carry_ref[...] = jnp.zeros_like(carry_ref)

        x = u_ref[...].astype(F32)
        x1, x2 = _shift_down(x, carry_ref[...], TM)
        conv = cw_ref[3:4, :] + x2 * cw_ref[0:1, :] + x1 * cw_ref[1:2, :] + x * cw_ref[2:3, :]
        conv_ref[...] = conv.astype(BF16)
        a = conv[:, :CONV_BLOCK]
        g = conv[:, CONV_BLOCK:]
        act_ref[...] = (a * _gelu_half(a) * g).astype(BF16)
        carry_ref[...] = x[TM - 8:TM, :]
        if n_carried:
            pl.when((j == N_CONV_BLOCKS - 1) & (i == n_rows - 1))(finish)

    anywhere = [pl.BlockSpec(memory_space=pl.ANY)] * n_carried
    return pl.pallas_call(
        body, name=name, grid=(N_CONV_BLOCKS, n_rows),
        in_specs=[pl.BlockSpec((TM, cb2), lambda j, i: (i, j)), pl.BlockSpec((8, cb2), lambda j, i: (0, j))] + anywhere,
        out_specs=[pl.BlockSpec((TM, cb2), lambda j, i: (i, j)),
                   pl.BlockSpec((TM, CONV_BLOCK), lambda j, i: (i, j))] + anywhere,
        out_shape=[jax.ShapeDtypeStruct((LP, D_UP), BF16), jax.ShapeDtypeStruct((LP, D_FF), BF16)]
        + _gathered_shapes(carried),
        scratch_shapes=[pltpu.VMEM((8, cb2), F32)] + _exchange_sems(n_carried),
        compiler_params=_cparams("arbitrary", "arbitrary"),
    )(u, cw8, *carried)


def _conv_act_bwd(dact, conv, u, cw8, name):
    n_rows = LP // TM
    cb2 = 2 * CONV_BLOCK
    n_strips = TM // STRIP

    def body(dact_ref, conv_ref, u_ref, cw_ref, du_ref, dcw_ref, carry_ref):
        i = pl.program_id(1)

        @pl.when(i == 0)
        def _():
            dcw_ref[...] = jnp.zeros_like(dcw_ref)
            carry_ref[...] = jnp.zeros_like(carry_ref)

        w0, w1, w2 = cw_ref[0:1, :], cw_ref[1:2, :], cw_ref[2:3, :]
        row = lax.broadcasted_iota(jnp.int32, (HALF, 1), 0)

        def strip(k, carry):
            n1, n2, s0, s1, s2, s3 = carry
            r0 = pl.multiple_of((n_strips - 1 - k) * STRIP, STRIP)
            cv = conv_ref[pl.ds(r0, STRIP), :].astype(F32)
            dav = dact_ref[pl.ds(r0, STRIP), :].astype(F32)
            x = u_ref[pl.ds(r0, STRIP), :].astype(F32)
            du = [None, None]
            for half in (1, 0):
                rows = slice(HALF * half, HALF * (half + 1))
                a, g, dah = cv[rows, :CONV_BLOCK], cv[rows, CONV_BLOCK:], dav[rows]
                h = _gelu_half(a)
                dconv = jnp.concatenate([dah * g * _gelu_slope(a, h), dah * (a * h)], axis=1)
                u1, u2 = pltpu.roll(dconv, HALF - 1, 0), pltpu.roll(dconv, HALF - 2, 0)
                d1 = jnp.where(row >= HALF - 1, n1, u1)
                d2 = jnp.where(row >= HALF - 2, n2, u2)
                du[half] = dconv * w2 + d1 * w1 + d2 * w0
                s0, s1, s2, s3 = s0 + d2 * x[rows], s1 + d1 * x[rows], s2 + dconv * x[rows], s3 + dconv
                n1, n2 = u1, u2
            du_ref[pl.ds(r0, STRIP), :] = jnp.concatenate(du, axis=0).astype(BF16)
            return n1, n2, s0, s1, s2, s3

        below = carry_ref[...]
        zero = jnp.zeros((HALF, cb2), F32)
        init = (pltpu.roll(below, HALF - 1, 0), pltpu.roll(below, HALF - 2, 0), zero, zero, zero, zero)
        u1, _, s0, s1, s2, s3 = lax.fori_loop(0, n_strips, strip, init, unroll=2)
        carry_ref[...] = pltpu.roll(u1, 1, 0)
        dcw_ref[0:1, :] += jnp.sum(s0, axis=0, keepdims=True)
        dcw_ref[1:2, :] += jnp.sum(s1, axis=0, keepdims=True)
        dcw_ref[2:3, :] += jnp.sum(s2, axis=0, keepdims=True)
        dcw_ref[3:4, :] += jnp.sum(s3, axis=0, keepdims=True)

    rev = lambda j, i: (n_rows - 1 - i, j)
    return pl.pallas_call(
        body, name=name, grid=(N_CONV_BLOCKS, n_rows),
        in_specs=[pl.BlockSpec((TM, CONV_BLOCK), rev), pl.BlockSpec((TM, cb2), rev), pl.BlockSpec((TM, cb2), rev),
                  pl.BlockSpec((8, cb2), lambda j, i: (0, j))],
        out_specs=[pl.BlockSpec((TM, cb2), rev), pl.BlockSpec((8, cb2), lambda j, i: (0, j))],
        out_shape=[jax.ShapeDtypeStruct((LP, D_UP), BF16), jax.ShapeDtypeStruct((8, D_UP), F32)],
        scratch_shapes=[pltpu.VMEM((HALF, cb2), F32)],
        compiler_params=_cparams("arbitrary", "arbitrary"),
    )(dact, conv, u, cw8)


CHUNKS_PER_STEP = 3 if N_CHUNKS % 3 == 0 else 1
STEP_ROWS = CHUNKS_PER_STEP * CHUNK
N_STEPS = N_CHUNKS // CHUNKS_PER_STEP


def _ret_consts(h):
    rows = STEP_ROWS
    lg = math.log(1.0 - 2.0 ** (-5.0 - h))
    ri = lax.broadcasted_iota(jnp.int32, (rows, rows), 0)
    ci = lax.broadcasted_iota(jnp.int32, (rows, rows), 1)
    diff = (ri - ci).astype(F32)
    dmat = jnp.where(diff >= 0, jnp.exp(lg * jnp.maximum(diff, 0.0)), 0.0)
    rowf = lax.broadcasted_iota(jnp.int32, (rows, 1), 0).astype(F32)
    zeta = jnp.exp(lg * (rows - 1.0 - rowf))
    xi = jnp.exp(lg * (rowf + 1.0))
    return dmat, zeta, xi, math.exp(lg * rows)


def _rope(t, cosv, sinv):
    return t * cosv + pltpu.roll(t, RET_DK // 2, 1) * sinv


def _unrope(d, cosv, sinv):
    return d * cosv + pltpu.roll(d * sinv, RET_DK // 2, 1)


def _gla_masks():
    ri = lax.broadcasted_iota(jnp.int32, (CHUNK, CHUNK), 0)
    ci = lax.broadcasted_iota(jnp.int32, (CHUNK, CHUNK), 1)
    return dict(ri=ri, ci=ci, tril=(ri >= ci).astype(F32), heads=_head_block_mask(), own=_state_block_mask())


def _gla_common(p_ref, w2_ref, gb_ref, chunk, rows, masks):
    row = lax.broadcasted_iota(jnp.int32, (CHUNK, 1), 0)
    real = (chunk * CHUNK + row) >= PAD_ROWS
    ga = p_ref[rows, O_GA:O_GA + 128]
    z = _dot(ga, w2_ref[...]) + gb_ref[...]
    la = (jnp.minimum(z, 0.0) - jnp.log(1.0 + jnp.exp(-jnp.abs(z)))) * (1.0 / GLA_TAU)
    la = jnp.where(real, la, 0.0)
    ri, ci = masks["ri"], masks["ci"]
    cum = _dot_exact_rhs(masks["tril"], la)
    last = cum[CHUNK - 1:CHUNK, :]
    qs = p_ref[rows, O_GQ:O_GQ + 256] * (GLA_DK ** -0.5)
    k = p_ref[rows, O_GK:O_GK + 256]
    ecum = jnp.exp(cum)
    ekl = jnp.exp(last - cum)
    el = jnp.exp(last)
    refs = [jnp.zeros((1, 256), F32)] + [cum[a * SUB - 1:a * SUB, :] for a in range(1, N_SUB)]
    eq = [jnp.exp(cum[a * SUB:(a + 1) * SUB, :] - refs[a]) for a in range(N_SUB)]
    spread = refs[0] - cum[SUB - 1:SUB, :]
    for a in range(1, N_SUB):
        spread = jnp.maximum(spread, refs[a] - cum[(a + 1) * SUB - 1:(a + 1) * SUB, :])
    small = jnp.max(spread) <= GLA_FACTORED_MAX
    return dict(real=real, row=row, z=z, la=la, cum=cum, last=last, qs=qs, k=k, ecum=ecum, ekl=ekl, el=el,
                refs=refs, eq=eq, small=small, ri=ri, ci=ci, masks=masks)


GLA_FACTORED_MAX = 40.0


def _head_block_mask():
    r = lax.broadcasted_iota(jnp.int32, (CHUNK, 256), 0)
    col = lax.broadcasted_iota(jnp.int32, (CHUNK, 256), 1)
    return (r // SUB) == (col // GLA_DK)


def _state_block_mask():
    r = lax.broadcasted_iota(jnp.int32, (GLA_HEADS * GLA_DK, GLA_HEADS * GLA_DV), 0)
    col = lax.broadcasted_iota(jnp.int32, (GLA_HEADS * GLA_DK, GLA_HEADS * GLA_DV), 1)
    return (r // GLA_DK) == (col // GLA_DV)


def _block_diagonal(blocks):
    zero = jnp.zeros((GLA_DK, GLA_DV), F32)
    return jnp.concatenate([jnp.concatenate([blocks[h] if g == h else zero for g in range(GLA_HEADS)], axis=1)
                            for h in range(GLA_HEADS)], axis=0)


def _gla_factored(c):
    mask = c["masks"]["heads"]
    eks, keys, queries = [], [], []
    for a in range(N_SUB):
        ek = jnp.exp(jnp.minimum(c["refs"][a] - c["cum"], GLA_FACTORED_MAX))
        qh = c["qs"][a * SUB:(a + 1) * SUB, :] * c["eq"][a]
        eks.append(ek)
        keys.append(c["k"] * ek)
        queries.append(jnp.where(mask, jnp.concatenate([qh] * GLA_HEADS, axis=0), 0.0))
    return eks, keys, queries


def _gla_scores_factored(c, factored, p_scr):
    _, keys, queries = factored
    for a in range(N_SUB):
        out = _dot_nt(queries[a], keys[a])
        out = jnp.where(c["ci"] <= a * SUB + (c["ri"] & (SUB - 1)), out, 0.0)
        for h in range(GLA_HEADS):
            p_scr[h, a * SUB:(a + 1) * SUB, :] = out[h * SUB:(h + 1) * SUB, :]


def _gla_intra_bwd_factored(c, factored, dps, dq_scr, dk_scr):
    eks, keys, queries = factored
    mask = c["masks"]["heads"]
    dk = jnp.zeros((CHUNK, 256), F32)
    for a in range(N_SUB):
        dpa = jnp.concatenate([dps[h][a * SUB:(a + 1) * SUB, :] for h in range(GLA_HEADS)], axis=0)
        dq = jnp.where(mask, _dot(dpa, keys[a]), 0.0)
        dq = dq[0:SUB] + dq[SUB:2 * SUB] + dq[2 * SUB:3 * SUB] + dq[3 * SUB:4 * SUB]
        dq_scr[a * SUB:(a + 1) * SUB, :] = dq * c["eq"][a]
        dk = dk + _dot_tn(dpa, queries[a]) * eks[a]
    dk_scr[...] = dk


def _gla_lag_weights(c):
    cum, row = c["cum"], c["row"]
    out = [jnp.ones((CHUNK, 256), F32)]
    for r in range(1, SUB):
        out.append(jnp.where((row % SUB) >= r, jnp.exp(jnp.minimum(cum - pltpu.roll(cum, r, 0), 0.0)), 0.0))
    return out


def _gla_pairwise_keys(c):
    return [None] + [c["k"] * jnp.exp(jnp.minimum(c["refs"][a] - c["cum"], 0.0)) for a in range(1, N_SUB)]


def _gla_scores_pairwise(c, lag_w, keys, h):
    sl = slice(GLA_DK * h, GLA_DK * (h + 1))
    qs, k = c["qs"][:, sl], c["k"][:, sl]
    ri, ci = c["ri"], c["ci"]
    p = jnp.zeros((CHUNK, CHUNK), F32)
    for r in range(SUB):
        kr = k if r == 0 else pltpu.roll(k, r, 0)
        pr = jnp.sum(qs * kr * lag_w[r][:, sl], axis=1, keepdims=True)
        p = p + jnp.where(ci == ri - r, pr, 0.0)
    blocks = [jnp.zeros((SUB, CHUNK), F32)]
    for a in range(1, N_SUB):
        qh = qs[a * SUB:(a + 1) * SUB, :] * c["eq"][a][:, sl]
        blocks.append(jnp.where(ci[:SUB, :] < a * SUB, _dot_nt(qh, keys[a][:, sl]), 0.0))
    return p + jnp.concatenate(blocks, axis=0)


def _gla_all_scores(c, p_scr, factored):
    if factored:
        _gla_scores_factored(c, _gla_factored(c), p_scr)
    else:
        lag_w, keys = _gla_lag_weights(c), _gla_pairwise_keys(c)
        for h in range(GLA_HEADS):
            p_scr[h] = _gla_scores_pairwise(c, lag_w, keys, h)


def _either_form(chunks, run):
    small = chunks[0]["small"]
    for c in chunks[1:]:
        small = jnp.logical_and(small, c["small"])
    pl.when(small)(lambda: run(True))
    pl.when(jnp.logical_not(small))(lambda: run(False))


def _gla_intra_bwd_pairwise(c, lag_w, keys, dp, h):
    sl = slice(GLA_DK * h, GLA_DK * (h + 1))
    qs_h, k_h = c["qs"][:, sl], c["k"][:, sl]
    ri, ci = c["ri"], c["ci"]
    dq_rows = [jnp.zeros((SUB, GLA_DK), F32)]
    dk = jnp.zeros((CHUNK, GLA_DK), F32)
    for a in range(1, N_SUB):
        eq = c["eq"][a][:, sl]
        qh = qs_h[a * SUB:(a + 1) * SUB, :] * eq
        dpa = jnp.where(ci[:SUB, :] < a * SUB, dp[a * SUB:(a + 1) * SUB, :], 0.0)
        dq_rows.append(_dot(dpa, keys[a][:, sl]) * eq)
        ek = jnp.exp(jnp.minimum(c["refs"][a][:, sl] - c["cum"][:, sl], 0.0))
        dk = dk + _dot_tn(dpa, qh) * ek
    dq = jnp.concatenate(dq_rows, axis=0)
    for r in range(SUB):
        w = lag_w[r][:, sl]
        dpr = jnp.sum(jnp.where(ci == ri - r, dp, 0.0), axis=1, keepdims=True)
        kr = k_h if r == 0 else pltpu.roll(k_h, r, 0)
        dq = dq + dpr * kr * w
        back = dpr * qs_h * w
        dk = dk + (back if r == 0 else pltpu.roll(back, CHUNK - r, 0))
    return dq, dk


def _gla_all_intra_bwd(c, dps, p_scr, dq_scr, dk_scr, factored):
    if factored:
        terms = _gla_factored(c)
        _gla_scores_factored(c, terms, p_scr)
        _gla_intra_bwd_factored(c, terms, dps, dq_scr, dk_scr)
    else:
        lag_w, keys = _gla_lag_weights(c), _gla_pairwise_keys(c)
        outs = [_gla_intra_bwd_pairwise(c, lag_w, keys, dps[h], h) for h in range(GLA_HEADS)]
        for h in range(GLA_HEADS):
            p_scr[h] = _gla_scores_pairwise(c, lag_w, keys, h)
        dq_scr[...] = jnp.concatenate([o[0] for o in outs], axis=1)
        dk_scr[...] = jnp.concatenate([o[1] for o in outs], axis=1)


def _mixer_fwd(proj, cos2, sin2, w2p, gb, rnw, gnw, name, carried=()):
    n_carried = len(carried)

    def body(*refs):
        p_ref, c_ref, s_ref, w2_ref, gb_ref, rnw_ref, gnw_ref = refs[:7]
        x_refs, refs = refs[7:7 + n_carried], refs[7 + n_carried:]
        ocat_ref, mrg_ref, sr_out, sg_out = refs[:4]
        gathered_refs, refs = refs[4:4 + n_carried], refs[4 + n_carried:]
        sr, sg, p_scr = refs[:3]
        n = pl.program_id(0)
        if n_carried:
            start, forward, finish = _gather_phases(x_refs, gathered_refs, *refs[3:])
            pl.when(n == 0)(start)
            pl.when(n == (3 * N_STEPS) // 4)(forward)

        @pl.when(n == 0)
        def _():
            sr[...] = jnp.zeros_like(sr)
            sg[...] = jnp.zeros_like(sg)

        sr_out[0] = sr[...]
        cosv, sinv = c_ref[...], s_ref[...]

        for h in range(RET_HEADS):
            dmat, zeta, xi, gc = _ret_consts(h)
            hs = slice(128 * h, 128 * (h + 1))
            q = _rope(p_ref[:, O_RQ + 128 * h:O_RQ + 128 * (h + 1)], cosv, sinv)
            k = _rope(p_ref[:, O_RK + 128 * h:O_RK + 128 * (h + 1)], cosv, sinv) * (RET_DK ** -0.5)
            v = p_ref[:, O_RV + 128 * h:O_RV + 128 * (h + 1)]
            g = p_ref[:, O_RG + 128 * h:O_RG + 128 * (h + 1)]
            s_in = sr[h]
            a = _dot_nt(q, k) * dmat
            o = _dot(a, v) + _dot(q, s_in) * xi
            sr[h] = gc * s_in + _dot_tn(k * zeta, v)
            mu = jnp.mean(o, axis=-1, keepdims=True)
            xc = o - mu
            nrm = xc * lax.rsqrt(jnp.mean(xc * xc, axis=-1, keepdims=True) + EPS)
            ocat_ref[:, hs] = o
            mrg_ref[:, hs] = (nrm * rnw_ref[:, hs] * (g * _sigmoid(g))).astype(BF16)

        row_slices = [slice(CHUNK * j, CHUNK * (j + 1)) for j in range(CHUNKS_PER_STEP)]
        masks = _gla_masks()
        chunks = [_gla_common(p_ref, w2_ref, gb_ref, n * CHUNKS_PER_STEP + j, rows, masks)
                  for j, rows in enumerate(row_slices)]

        def gla_chunks(factored):
            own = masks["own"]
            for j, (rows, c) in enumerate(zip(row_slices, chunks)):
                s_in = sg[...]
                for h in range(GLA_HEADS):
                    sg_out[j, h] = s_in[GLA_DK * h:GLA_DK * (h + 1), GLA_DV * h:GLA_DV * (h + 1)]
                _gla_all_scores(c, p_scr.at[j], factored)
                v_all = p_ref[rows, O_GV:O_GV + GLA_HEADS * GLA_DV]
                o_inter = _dot(c["qs"] * c["ecum"], s_in)
                decay = jnp.exp(_dot_tn_exact_lhs(c["la"], jnp.ones((CHUNK, GLA_HEADS * GLA_DV), F32)))
                sg[...] = decay * s_in + jnp.where(own, _dot_tn(c["k"] * c["ekl"], v_all), 0.0)
                o_intra = _dot(p_scr[j].reshape(GLA_HEADS * CHUNK, CHUNK), v_all)
                for h in range(GLA_HEADS):
                    hs = slice(512 + 128 * h, 512 + 128 * (h + 1))
                    g = p_ref[rows, O_GR + 128 * h:O_GR + 128 * (h + 1)]
                    o = (o_intra[CHUNK * h:CHUNK * (h + 1), GLA_DV * h:GLA_DV * (h + 1)]
                         + o_inter[:, GLA_DV * h:GLA_DV * (h + 1)])
                    nrm = o * lax.rsqrt(jnp.mean(o * o, axis=-1, keepdims=True) + EPS)
                    ocat_ref[rows, hs] = o
                    mrg_ref[rows, hs] = (nrm * gnw_ref[:, 128 * h:128 * (h + 1)] * (g * _sigmoid(g))).astype(BF16)

        _either_form(chunks, gla_chunks)

        if n_carried:
            pl.when(n == N_STEPS - 1)(finish)

    const = lambda shape: pl.BlockSpec(shape, lambda n: (0,) * len(shape))
    anywhere = [pl.BlockSpec(memory_space=pl.ANY)] * n_carried
    return pl.pallas_call(
        body, name=name, grid=(N_STEPS,),
        in_specs=[pl.BlockSpec((STEP_ROWS, IN_WP), lambda n: (n, 0)),
                  pl.BlockSpec((STEP_ROWS, 128), lambda n: (n, 0)), pl.BlockSpec((STEP_ROWS, 128), lambda n: (n, 0)),
                  const((128, 256)), const((1, 256)), const((1, 512)), const((1, 512))] + anywhere,
        out_specs=[pl.BlockSpec((STEP_ROWS, D), lambda n: (n, 0)), pl.BlockSpec((STEP_ROWS, D), lambda n: (n, 0)),
                   pl.BlockSpec((1, RET_HEADS, RET_DK, 128), lambda n: (n, 0, 0, 0)),
                   pl.BlockSpec((CHUNKS_PER_STEP, GLA_HEADS, GLA_DK, GLA_DV), lambda n: (n, 0, 0, 0))] + anywhere,
        out_shape=[jax.ShapeDtypeStruct((LP, D), F32), jax.ShapeDtypeStruct((LP, D), BF16),
                   jax.ShapeDtypeStruct((N_STEPS, RET_HEADS, RET_DK, 128), F32),
                   jax.ShapeDtypeStruct((N_CHUNKS, GLA_HEADS, GLA_DK, GLA_DV), F32)] + _gathered_shapes(carried),
        scratch_shapes=[pltpu.VMEM((RET_HEADS, RET_DK, 128), F32),
                        pltpu.VMEM((GLA_HEADS * GLA_DK, GLA_HEADS * GLA_DV), F32),
                        pltpu.VMEM((CHUNKS_PER_STEP, GLA_HEADS, CHUNK, CHUNK), F32)] + _exchange_sems(n_carried),
        compiler_params=_cparams("arbitrary"),
    )(proj, cos2, sin2, w2p, gb, rnw, gnw, *carried)


def _mixer_bwd(proj, ocat, dmrg, sr_all, sg_all, cos2, sin2, w2p, gb, rnw, gnw, name, carried=()):
    last_step = N_STEPS - 1
    n_carried = len(carried)

    def body(*refs):
        p_ref, ocat_ref, dm_ref, sr_ref, sg_ref, c_ref, s_ref, w2_ref, gb_ref, rnw_ref, gnw_ref = refs[:11]
        g_refs, refs = refs[11:11 + n_carried], refs[11 + n_carried:]
        dp_ref, dw2_ref, dgb_ref, drn_ref, dgn_ref = refs[:5]
        got_refs, refs = refs[5:5 + n_carried], refs[5 + n_carried:]
        dsr, dsg, p_scr, dq_scr, dk_scr = refs[:5]
        step = pl.program_id(0)
        n = last_step - step
        if n_carried:
            start, finish = _exchange_phases(g_refs, got_refs, *refs[5:])
            pl.when(step == 0)(start)

        @pl.when(step == 0)
        def _():
            dsr[...] = jnp.zeros_like(dsr)
            dsg[...] = jnp.zeros_like(dsg)
            dw2_ref[...] = jnp.zeros_like(dw2_ref)
            dgb_ref[...] = jnp.zeros_like(dgb_ref)
            drn_ref[...] = jnp.zeros_like(drn_ref)
            dgn_ref[...] = jnp.zeros_like(dgn_ref)

        cosv, sinv = c_ref[...], s_ref[...]
        step_row = lax.broadcasted_iota(jnp.int32, (STEP_ROWS, 1), 0)
        real = ((n * STEP_ROWS + step_row) >= PAD_ROWS).astype(F32)

        for h in range(RET_HEADS):
            dmat, zeta, xi, gc = _ret_consts(h)
            hs = slice(128 * h, 128 * (h + 1))
            q = _rope(p_ref[:, O_RQ + 128 * h:O_RQ + 128 * (h + 1)], cosv, sinv)
            k = _rope(p_ref[:, O_RK + 128 * h:O_RK + 128 * (h + 1)], cosv, sinv) * (RET_DK ** -0.5)
            v = p_ref[:, O_RV + 128 * h:O_RV + 128 * (h + 1)]
            g = p_ref[:, O_RG + 128 * h:O_RG + 128 * (h + 1)]
            o = ocat_ref[:, hs]
            dy = dm_ref[:, hs]
            wv = rnw_ref[:, hs]
            mu = jnp.mean(o, axis=-1, keepdims=True)
            xc = o - mu
            rs = lax.rsqrt(jnp.mean(xc * xc, axis=-1, keepdims=True) + EPS)
            nrm = xc * rs
            sgm = _sigmoid(g)
            sil = g * sgm
            drn_ref[0:1, hs] += jnp.sum(dy * nrm * sil, axis=0, keepdims=True)
            dgate = dy * nrm * wv * (sgm * (1.0 + g * (1.0 - sgm)))
            dn = dy * wv * sil
            do = rs * (dn - jnp.mean(dn, axis=-1, keepdims=True) - nrm * jnp.mean(dn * nrm, axis=-1, keepdims=True))
            s_in = sr_ref[0, h]
            ds_out = dsr[h]
            a = _dot_nt(q, k) * dmat
            da = _dot_nt(do, v) * dmat
            dox = do * xi
            dq = _dot(da, k) + _dot_nt(dox, s_in)
            dk = _dot_tn(da, q) + _dot_nt(v, ds_out) * zeta
            dv = _dot_tn(a, do) + _dot(k * zeta, ds_out)
            dsr[h] = gc * ds_out + _dot_tn(q, dox)
            dk = dk * (RET_DK ** -0.5)
            dp_ref[:, O_RQ + 128 * h:O_RQ + 128 * (h + 1)] = (_unrope(dq, cosv, sinv) * real).astype(BF16)
            dp_ref[:, O_RK + 128 * h:O_RK + 128 * (h + 1)] = (_unrope(dk, cosv, sinv) * real).astype(BF16)
            dp_ref[:, O_RV + 128 * h:O_RV + 128 * (h + 1)] = (dv * real).astype(BF16)
            dp_ref[:, O_RG + 128 * h:O_RG + 128 * (h + 1)] = (dgate * real).astype(BF16)

        row_slices = [slice(CHUNK * j, CHUNK * (j + 1)) for j in range(CHUNKS_PER_STEP)]
        masks = _gla_masks()
        chunks = [_gla_common(p_ref, w2_ref, gb_ref, n * CHUNKS_PER_STEP + j, rows, masks)
                  for j, rows in enumerate(row_slices)]

        def gla_chunks(factored):
            for j in reversed(range(CHUNKS_PER_STEP)):
                gla_chunk_bwd(chunks[j], n * CHUNKS_PER_STEP + j, row_slices[j], j, factored, p_ref, ocat_ref, dm_ref,
                              sg_ref, w2_ref, gnw_ref, dp_ref, dw2_ref, dgb_ref, dgn_ref, dsg, p_scr, dq_scr, dk_scr)

        _either_form(chunks, gla_chunks)
        if n_carried:
            pl.when(step == last_step)(finish)

    def gla_chunk_bwd(c, chunk, rows, j, factored, p_ref, ocat_ref, dm_ref, sg_ref, w2_ref, gnw_ref,
                      dp_ref, dw2_ref, dgb_ref, dgn_ref, dsg, p_scr, dq_scr, dk_scr):
        row = lax.broadcasted_iota(jnp.int32, (CHUNK, 1), 0)
        real = ((chunk * CHUNK + row) >= PAD_ROWS).astype(F32)
        ri, ci = c["ri"], c["ci"]
        causal = ri >= ci
        triu = (ci >= ri).astype(F32)
        qe = c["qs"] * c["ecum"]
        kl = c["k"] * c["ekl"]
        v_all = p_ref[rows, O_GV:O_GV + GLA_HEADS * GLA_DV]
        dos, dps = [], []
        for h in range(GLA_HEADS):
            hs = slice(512 + 128 * h, 512 + 128 * (h + 1))
            v = v_all[:, GLA_DV * h:GLA_DV * (h + 1)]
            g = p_ref[rows, O_GR + 128 * h:O_GR + 128 * (h + 1)]
            o = ocat_ref[rows, hs]
            dy = dm_ref[rows, hs]
            wv = gnw_ref[:, 128 * h:128 * (h + 1)]
            rs = lax.rsqrt(jnp.mean(o * o, axis=-1, keepdims=True) + EPS)
            nrm = o * rs
            sgm = _sigmoid(g)
            sil = g * sgm
            dgn_ref[0:1, 128 * h:128 * (h + 1)] += jnp.sum(dy * nrm * sil, axis=0, keepdims=True)
            dgate = dy * nrm * wv * (sgm * (1.0 + g * (1.0 - sgm)))
            dn = dy * wv * sil
            do = rs * (dn - nrm * jnp.mean(dn * nrm, axis=-1, keepdims=True))
            dp_ref[rows, O_GR + 128 * h:O_GR + 128 * (h + 1)] = (dgate * real).astype(BF16)
            dos.append(do)
        do_all = jnp.concatenate(dos, axis=1)
        do_blocks = jnp.where(c["masks"]["own"], jnp.concatenate([do_all] * GLA_HEADS, axis=0), 0.0)
        dp_all = _dot_nt(do_blocks, v_all)
        dps = [jnp.where(causal, dp_all[CHUNK * h:CHUNK * (h + 1), :], 0.0) for h in range(GLA_HEADS)]
        _gla_all_intra_bwd(c, dps, p_scr.at[j], dq_scr.at[j], dk_scr.at[j], factored)
        s_in = _block_diagonal([sg_ref[j, h] for h in range(GLA_HEADS)])
        ds_out = dsg[...]
        decay = jnp.exp(_dot_tn_exact_lhs(c["la"], jnp.ones((CHUNK, GLA_HEADS * GLA_DV), F32)))
        dv_state = _dot(kl, ds_out)
        dqe = _dot_nt(do_all, s_in)
        dkl = _dot_nt(v_all, ds_out)
        dsg[...] = jnp.where(c["masks"]["own"], _dot_tn(qe, do_all), 0.0) + decay * ds_out
        sd = s_in * ds_out
        sd_hi = sd.astype(BF16)
        sd_lo = (sd - sd_hi.astype(F32)).astype(BF16)
        ones8 = jnp.ones((8, GLA_HEADS * GLA_DV), BF16)
        nt = (((1,), (1,)), ((), ()))
        d_el = (lax.dot_general(ones8, sd_hi, nt, preferred_element_type=F32)
                + lax.dot_general(ones8, sd_lo, nt, preferred_element_type=F32))[0:1, :]
        dqs = dqe * c["ecum"] + dq_scr[j]
        dkk = dkl * c["ekl"] + dk_scr[j]
        d_last = jnp.sum(dkl * kl, axis=0, keepdims=True) + d_el * c["el"]
        dcum = c["qs"] * dqs - c["k"] * dkk + jnp.where(row == CHUNK - 1, d_last, 0.0)
        dla = _dot_exact_rhs(triu, dcum)
        dv = _dot_tn(p_scr[j].reshape(GLA_HEADS * CHUNK, CHUNK), do_blocks) + dv_state
        dp_ref[rows, O_GV:O_GV + GLA_HEADS * GLA_DV] = (dv * real).astype(BF16)
        dp_ref[rows, O_GQ:O_GQ + 256] = (dqs * (GLA_DK ** -0.5) * real).astype(BF16)
        dp_ref[rows, O_GK:O_GK + 256] = (dkk * real).astype(BF16)
        dz = dla * (1.0 / GLA_TAU) * _sigmoid(-c["z"]) * real
        ga = p_ref[rows, O_GA:O_GA + 128]
        dp_ref[rows, O_GA:O_GA + 128] = _dot_nt(dz, w2_ref[...]).astype(BF16)
        dp_ref[rows, O_GA + 128:IN_WP] = jnp.zeros((CHUNK, IN_WP - O_GA - 128), BF16)
        dw2_ref[...] += _dot_tn(ga, dz)
        dgb_ref[0:1, :] += jnp.sum(dz, axis=0, keepdims=True)

    const = lambda shape: pl.BlockSpec(shape, lambda s: (0,) * len(shape))
    rev = lambda s: (last_step - s, 0)
    anywhere = [pl.BlockSpec(memory_space=pl.ANY)] * n_carried
    return pl.pallas_call(
        body, name=name, grid=(N_STEPS,),
        in_specs=[pl.BlockSpec((STEP_ROWS, IN_WP), rev), pl.BlockSpec((STEP_ROWS, D), rev),
                  pl.BlockSpec((STEP_ROWS, D), rev),
                  pl.BlockSpec((1, RET_HEADS, RET_DK, 128), lambda s: (last_step - s, 0, 0, 0)),
                  pl.BlockSpec((CHUNKS_PER_STEP, GLA_HEADS, GLA_DK, GLA_DV), lambda s: (last_step - s, 0, 0, 0)),
                  pl.BlockSpec((STEP_ROWS, 128), rev), pl.BlockSpec((STEP_ROWS, 128), rev),
                  const((128, 256)), const((1, 256)), const((1, 512)), const((1, 512))] + anywhere,
        out_specs=[pl.BlockSpec((STEP_ROWS, IN_WP), rev), const((128, 256)), const((8, 256)),
                   const((8, 512)), const((8, 512))] + anywhere,
        out_shape=[jax.ShapeDtypeStruct((LP, IN_WP), BF16), jax.ShapeDtypeStruct((128, 256), F32),
                   jax.ShapeDtypeStruct((8, 256), F32), jax.ShapeDtypeStruct((8, 512), F32),
                   jax.ShapeDtypeStruct((8, 512), F32)] + [jax.ShapeDtypeStruct(g.shape, g.dtype) for g in carried],
        scratch_shapes=[pltpu.VMEM((RET_HEADS, RET_DK, 128), F32),
                        pltpu.VMEM((GLA_HEADS * GLA_DK, GLA_HEADS * GLA_DV), F32),
                        pltpu.VMEM((CHUNKS_PER_STEP, GLA_HEADS, CHUNK, CHUNK), F32),
                        pltpu.VMEM((CHUNKS_PER_STEP, CHUNK, 256), F32),
                        pltpu.VMEM((CHUNKS_PER_STEP, CHUNK, 256), F32)] + _exchange_sems(n_carried),
        compiler_params=_cparams("arbitrary"),
    )(proj, ocat, dmrg, sr_all, sg_all, cos2, sin2, w2p, gb, rnw, gnw, *carried)


def _all_gather(xs, name):
    n = len(xs)

    def body(*refs):
        start, forward, finish = _gather_phases(refs[:n], refs[n:2 * n], *refs[2 * n:])
        start()
        forward()
        finish()

    return pl.pallas_call(
        body, name=name,
        in_specs=[pl.BlockSpec(memory_space=pl.ANY)] * n,
        out_specs=[pl.BlockSpec(memory_space=pl.ANY)] * n,
        out_shape=_gathered_shapes(xs),
        scratch_shapes=_exchange_sems(n),
    )(*xs)


def _gathered_shapes(xs):
    return [jax.ShapeDtypeStruct((N_DEV,) + x.shape, x.dtype) for x in xs]


def _exchange_sems(n):
    if n == 0:
        return []
    return [pltpu.SemaphoreType.DMA((7 * n,)), pltpu.SemaphoreType.DMA((7 * n,)), pltpu.SemaphoreType.DMA((n,))]


def _gather_phases(x_refs, out_refs, send_sems, recv_sems, local_sems):
    n = len(x_refs)
    mx, my, mc = lax.axis_index("x"), lax.axis_index("y"), lax.axis_index("c")
    me, sibling = (mx, my, mc), (mx, my, 1 - mc)
    chips = [(1 - mx, my), (mx, 1 - my), (1 - mx, 1 - my)]

    def slot(a, px, py, pc):
        return out_refs[a].at[4 * px + 2 * py + pc]

    def copy(a, k, block, to, src=None):
        return pltpu.make_async_remote_copy(
            src_ref=slot(a, *block) if src is None else src, dst_ref=slot(a, *block),
            send_sem=send_sems.at[7 * a + k], recv_sem=recv_sems.at[7 * a + k],
            device_id=to, device_id_type=MESH_IDS)

    mine = [pltpu.make_async_copy(x_refs[a], slot(a, *me), local_sems.at[a]) for a in range(n)]
    first = []
    for a in range(n):
        first.append(copy(a, 0, me, sibling, src=x_refs[a]))
        first += [copy(a, 1 + j, me, (*chip, mc), src=x_refs[a]) for j, chip in enumerate(chips)]
    passed = [copy(a, 4 + j, (*chip, mc), sibling) for j, chip in enumerate(chips) for a in range(n)]

    def start():
        for cp in mine + first:
            cp.start()

    def forward():
        for j, chip in enumerate(chips):
            for a in range(n):
                copy(a, 1 + j, (*chip, mc), me).wait_recv()
                passed[j * n + a].start()

    def finish():
        for a in range(n):
            copy(a, 0, sibling, me).wait_recv()
            for j, chip in enumerate(chips):
                copy(a, 4 + j, (*chip, 1 - mc), me).wait_recv()
        for cp in first + passed:
            cp.wait_send()
        for cp in mine:
            cp.wait()

    return start, forward, finish


def _exchange_blocks(gs, name):
    n = len(gs)

    def body(*refs):
        start, finish = _exchange_phases(refs[:n], refs[n:2 * n], *refs[2 * n:])
        start()
        finish()

    return pl.pallas_call(
        body, name=name,
        in_specs=[pl.BlockSpec(memory_space=pl.ANY)] * n,
        out_specs=[pl.BlockSpec(memory_space=pl.ANY)] * n,
        out_shape=[jax.ShapeDtypeStruct(g.shape, g.dtype) for g in gs],
        scratch_shapes=_exchange_sems(n),
    )(*gs)


def _exchange_phases(g_refs, out_refs, send_sems, recv_sems, local_sems):
    n = len(g_refs)
    mx, my, mc = lax.axis_index("x"), lax.axis_index("y"), lax.axis_index("c")
    me = 4 * mx + 2 * my + mc
    mine = [pltpu.make_async_copy(g_refs[a].at[me], out_refs[a].at[me], local_sems.at[a]) for a in range(n)]
    copies = []
    for r in range(1, N_DEV):
        px, py, pc = mx ^ (r >> 2), my ^ ((r >> 1) & 1), mc ^ (r & 1)
        peer = 4 * px + 2 * py + pc
        for a in range(n):
            copies.append(pltpu.make_async_remote_copy(
                src_ref=g_refs[a].at[peer], dst_ref=out_refs[a].at[me],
                send_sem=send_sems.at[7 * a + r - 1], recv_sem=recv_sems.at[7 * a + r - 1],
                device_id=(px, py, pc), device_id_type=MESH_IDS))

    def start():
        for cp in mine + copies:
            cp.start()

    def finish():
        for cp in copies:
            cp.wait_recv()
        for cp in copies:
            cp.wait_send()
        for cp in mine:
            cp.wait()

    return start, finish


IN_SHARD = IN_W // N_DEV
IN_SHARD_P = 512
UP_SHARD = D_UP // N_DEV
UP_SHARD_P = 768
RELAYOUT_ROWS = 256


def _pieces_w_in():
    return [(k, 0, IN_SHARD * k, IN_SHARD) for k in range(N_DEV)]


def _pieces_ffn_up():
    pieces = []
    for k in range(N_DEV):
        n, end = UP_SHARD * k, UP_SHARD * (k + 1)
        while n < end:
            half, r = divmod(n, D_FF)
            blk, off = divmod(r, CONV_BLOCK)
            run = min(CONV_BLOCK - off, end - n)
            pieces.append((k, n - UP_SHARD * k, 2 * CONV_BLOCK * blk + CONV_BLOCK * half + off, run))
            n += run
    return pieces


def _assemble_block(load, spans, dst_block, rows):
    lo = 128 * dst_block
    lane = lax.broadcasted_iota(jnp.int32, (1, 128), 1)
    out = jnp.zeros((rows, 128), F32)
    for key, src_off, dst_off, length in spans:
        a, b = max(lo, dst_off), min(lo + 128, dst_off + length)
        s, s_end = src_off + (a - dst_off), src_off + (b - dst_off)
        d = a
        while s < s_end:
            e = min(s_end, 128 * (s // 128 + 1))
            blk = load(key, s // 128)
            shift = (d - s) % 128
            if shift:
                blk = pltpu.roll(blk, shift, 1)
            out = jnp.where((lane >= d - lo) & (lane < d - lo + (e - s)), blk, out)
            d += e - s
            s = e
    return out


def _shards_to_cols(shards, pieces, width, name):
    _, rows, _ = shards.shape
    tr = RELAYOUT_ROWS

    def body(s_ref, o_ref):
        load = lambda k, b: s_ref[k, :, 128 * b:128 * (b + 1)].astype(F32)
        for db in range(width // 128):
            o_ref[:, 128 * db:128 * (db + 1)] = _assemble_block(load, pieces, db, tr).astype(BF16)

    return pl.pallas_call(
        body, name=name, grid=(rows // tr,),
        in_specs=[pl.BlockSpec((N_DEV, tr, shards.shape[2]), lambda i: (0, i, 0))],
        out_specs=pl.BlockSpec((tr, width), lambda i: (i, 0)),
        out_shape=jax.ShapeDtypeStruct((rows, width), BF16),
        compiler_params=_cparams("parallel"),
    )(shards)


def _cols_to_shards(full, pieces, shard_width, name):
    rows, width = full.shape
    tr = RELAYOUT_ROWS

    def body(f_ref, o_ref):
        load = lambda _, b: f_ref[:, 128 * b:128 * (b + 1)].astype(F32)
        for k in range(N_DEV):
            spans = [(None, dst_off, src_off, length) for dev, src_off, dst_off, length in pieces if dev == k]
            for db in range(shard_width // 128):
                o_ref[k, :, 128 * db:128 * (db + 1)] = _assemble_block(load, spans, db, tr).astype(BF16)

    return pl.pallas_call(
        body, name=name, grid=(rows // tr,),
        in_specs=[pl.BlockSpec((tr, width), lambda i: (i, 0))],
        out_specs=pl.BlockSpec((N_DEV, tr, shard_width), lambda i: (0, i, 0)),
        out_shape=jax.ShapeDtypeStruct((N_DEV, rows, shard_width), BF16),
        compiler_params=_cparams("parallel"),
    )(full)


def _adamw(parts, w, m, v, rows_per_step, name):
    rows, cols = w.shape
    assert rows % rows_per_step == 0 and parts.shape == (N_DEV, rows, cols)

    def body(p_ref, w_ref, m_ref, v_ref, g_ref, d_ref, nm_ref, nv_ref):
        g = p_ref[0].astype(F32)
        for j in range(1, N_DEV):
            g = g + p_ref[j].astype(F32)
        m_new = ADAM_B1 * m_ref[...] + (1.0 - ADAM_B1) * g
        v_new = ADAM_B2 * v_ref[...] + (1.0 - ADAM_B2) * (g * g)
        m_hat = m_new / (1.0 - ADAM_B1 ** ADAM_STEP)
        v_hat = v_new / (1.0 - ADAM_B2 ** ADAM_STEP)
        g_ref[...] = g
        d_ref[...] = -ADAM_LR * (m_hat / (jnp.sqrt(v_hat) + ADAM_EPS) + ADAM_WD * w_ref[...])
        nm_ref[...] = m_new
        nv_ref[...] = v_new

    tile = pl.BlockSpec((rows_per_step, cols), lambda i: (i, 0))
    shape = jax.ShapeDtypeStruct((rows, cols), F32)
    return pl.pallas_call(
        body, name=name, grid=(rows // rows_per_step,),
        in_specs=[pl.BlockSpec((N_DEV, rows_per_step, cols), lambda i: (0, i, 0)), tile, tile, tile],
        out_specs=[tile, tile, tile, tile],
        out_shape=[shape, shape, shape, shape],
        compiler_params=_cparams("parallel"),
    )(parts, w, m, v)


BIG = (("w_in", (DEPTH, D, IN_W // N_DEV), 2), ("w_out", (DEPTH, D // N_DEV, D), 1),
       ("ffn_up", (DEPTH, D, D_UP // N_DEV), 2), ("ffn_down", (DEPTH, D_FF // N_DEV, D), 1))
SMALL = (("meta_tokens", (N_META, D // N_DEV), 1), ("gla_gate_w2", (DEPTH, GATE_RANK, 256 // N_DEV), 2),
         ("ffn_conv_w", (DEPTH, 3, D_UP // N_DEV), 2))
REPL = (("pre_mix_norm", (DEPTH, D)), ("gla_gate_b", (DEPTH, 256)), ("ret_norm_w", (DEPTH, 512)),
        ("gla_norm_w", (DEPTH, 512)), ("post_mix_norm", (DEPTH, D)), ("pre_ffn_norm", (DEPTH, D)),
        ("ffn_conv_b", (DEPTH, D_UP)), ("post_ffn_norm", (DEPTH, D)))
WEIGHT_ORDER = ("meta_tokens", "pre_mix_norm", "w_in", "gla_gate_w2", "gla_gate_b", "ret_norm_w", "gla_norm_w",
                "w_out", "post_mix_norm", "pre_ffn_norm", "ffn_up", "ffn_conv_w", "ffn_conv_b", "ffn_down",
                "post_ffn_norm")


def _size(shape):
    return math.prod(shape)


def _round_up(n, mult):
    return -(-n // mult) * mult


REPL_ROWS = _round_up(-(-sum(_size(s) for _, s in REPL) // LANES), 8)
SMALL_ROWS = _round_up(-(-sum(_size(s) for _, s, _ in SMALL) // LANES), 8)


def _pack(arrays, rows, dtype):
    flat = jnp.concatenate([a.reshape(-1).astype(dtype) for a in arrays])
    return jnp.pad(flat, (0, rows * LANES - flat.shape[0])).reshape(rows, LANES)


def _unpack(buf, shapes):
    flat = buf.reshape(-1)
    out, off = [], 0
    for shape in shapes:
        out.append(flat[off:off + _size(shape)].reshape(shape))
        off += _size(shape)
    return out


def _unshard(blocks, axis):
    moved = jnp.moveaxis(blocks, 0, axis)
    shape = list(moved.shape)
    shape[axis:axis + 2] = [shape[axis] * shape[axis + 1]]
    return moved.reshape(shape)


def _to_blocks(full, axis):
    shape = list(full.shape)
    shape[axis:axis + 1] = [N_DEV, shape[axis] // N_DEV]
    return jnp.moveaxis(full.reshape(shape), axis, 0)


def _interleave_cols(w):
    lead = w.shape[:-1]
    return jnp.swapaxes(w.reshape(lead + (2, N_CONV_BLOCKS, CONV_BLOCK)), -3, -2).reshape(lead + (D_UP,))


def _deinterleave_cols(w):
    lead = w.shape[:-1]
    return jnp.swapaxes(w.reshape(lead + (N_CONV_BLOCKS, 2, CONV_BLOCK)), -3, -2).reshape(lead + (D_UP,))


def _rope_tables():
    half = RET_DK // 2
    inv = ROPE_BASE ** (-jnp.arange(half, dtype=F32) / half)
    pos = jnp.arange(LP, dtype=F32) - float(PAD_ROWS)
    ang = pos[:, None] * inv[None, :]
    c, s = jnp.cos(ang), jnp.sin(ang)
    return jnp.concatenate([c, c], axis=1), jnp.concatenate([-s, s], axis=1)


def kernel(x, meta_tokens, pre_mix_norm, w_in, gla_gate_w2, gla_gate_b, ret_norm_w, gla_norm_w, w_out, post_mix_norm, pre_ffn_norm, ffn_up, ffn_conv_w, ffn_conv_b, ffn_down, post_ffn_norm, loss_target, m_meta_tokens, m_pre_mix_norm, m_w_in, m_gla_gate_w2, m_gla_gate_b, m_ret_norm_w, m_gla_norm_w, m_w_out, m_post_mix_norm, m_pre_ffn_norm, m_ffn_up, m_ffn_conv_w, m_ffn_conv_b, m_ffn_down, m_post_ffn_norm, v_meta_tokens, v_pre_mix_norm, v_w_in, v_gla_gate_w2, v_gla_gate_b, v_ret_norm_w, v_gla_norm_w, v_w_out, v_post_mix_norm, v_pre_ffn_norm, v_ffn_up, v_ffn_conv_w, v_ffn_conv_b, v_ffn_down, v_post_ffn_norm):
    weights = dict(meta_tokens=meta_tokens, pre_mix_norm=pre_mix_norm, w_in=w_in, gla_gate_w2=gla_gate_w2,
                   gla_gate_b=gla_gate_b, ret_norm_w=ret_norm_w, gla_norm_w=gla_norm_w, w_out=w_out,
                   post_mix_norm=post_mix_norm, pre_ffn_norm=pre_ffn_norm, ffn_up=ffn_up, ffn_conv_w=ffn_conv_w,
                   ffn_conv_b=ffn_conv_b, ffn_down=ffn_down, post_ffn_norm=post_ffn_norm)
    mom1 = dict(meta_tokens=m_meta_tokens, pre_mix_norm=m_pre_mix_norm, w_in=m_w_in, gla_gate_w2=m_gla_gate_w2,
                gla_gate_b=m_gla_gate_b, ret_norm_w=m_ret_norm_w, gla_norm_w=m_gla_norm_w, w_out=m_w_out,
                post_mix_norm=m_post_mix_norm, pre_ffn_norm=m_pre_ffn_norm, ffn_up=m_ffn_up,
                ffn_conv_w=m_ffn_conv_w, ffn_conv_b=m_ffn_conv_b, ffn_down=m_ffn_down, post_ffn_norm=m_post_ffn_norm)
    mom2 = dict(meta_tokens=v_meta_tokens, pre_mix_norm=v_pre_mix_norm, w_in=v_w_in, gla_gate_w2=v_gla_gate_w2,
                gla_gate_b=v_gla_gate_b, ret_norm_w=v_ret_norm_w, gla_norm_w=v_gla_norm_w, w_out=v_w_out,
                post_mix_norm=v_post_mix_norm, pre_ffn_norm=v_pre_ffn_norm, ffn_up=v_ffn_up,
                ffn_conv_w=v_ffn_conv_w, ffn_conv_b=v_ffn_conv_b, ffn_down=v_ffn_down, post_ffn_norm=v_post_ffn_norm)

    pad_cols = lambda a, width: jnp.pad(a, ((0, 0), (0, width - a.shape[1])))
    big_names = [n for n, _, _ in BIG]
    shard = {}
    for l in range(DEPTH):
        shard[l, "w_in"] = pad_cols(w_in[l].astype(BF16), IN_SHARD_P)
        shard[l, "w_out"] = w_out[l].astype(BF16)
        shard[l, "ffn_up"] = pad_cols(ffn_up[l].astype(BF16), UP_SHARD_P)
        shard[l, "ffn_down"] = ffn_down[l].astype(BF16)
    gathered = {(0, "w_in"): _all_gather([shard[0, "w_in"]], "gather_w_in_0")[0]}
    gather_in_mixer = {l: [(l, n) for n in big_names[1:]] for l in range(DEPTH)}
    gather_in_conv = {l: [(l + 1, "w_in")] for l in range(DEPTH - 1)}
    small = _all_gather([_pack([weights[n] for n, _, _ in SMALL], SMALL_ROWS, F32)], "gather_small_weights")[0]
    small_parts = _unpack_blocks(small, [s for _, s, _ in SMALL])
    full = {n: _unshard(p, ax) for (n, _, ax), p in zip(SMALL, small_parts)}
    w2p = jnp.pad(full["gla_gate_w2"], ((0, 0), (0, 128 - GATE_RANK), (0, 0)))
    cw8 = jnp.concatenate([_interleave_cols(full["ffn_conv_w"]), _interleave_cols(ffn_conv_b)[:, None, :],
                           jnp.zeros((DEPTH, 4, D_UP), F32)], axis=1)
    cos2, sin2 = _rope_tables()

    h = jnp.concatenate([jnp.zeros((PAD_ROWS, D), F32), full["meta_tokens"], x[0]], axis=0)
    target = jnp.concatenate([jnp.zeros((CHUNK, D), F32), loss_target[0]], axis=0)
    saved, layer_w = [], []
    for l in range(DEPTH):
        lw = dict(w_in=_shards_to_cols(gathered[l, "w_in"], _pieces_w_in(), IN_WP, f"w_in_cols_{l}"))
        a1 = _rmsnorm_fwd(h, pre_mix_norm[l:l + 1], f"pre_mix_norm_{l}")
        proj = _matmul(a1, lw["w_in"], out_dtype=F32, tm=TK_ROWS, tn=1280, tk=D, name=f"in_proj_{l}", n_outer=True)
        keys = gather_in_mixer.get(l, [])
        ocat, merged, sr_all, sg_all, *got = _mixer_fwd(proj, cos2, sin2, w2p[l], gla_gate_b[l:l + 1],
                                                        ret_norm_w[l:l + 1], gla_norm_w[l:l + 1], f"mixer_fwd_{l}",
                                                        carried=[shard[key] for key in keys])
        gathered.update(zip(keys, got))
        lw["w_out"] = gathered[l, "w_out"].reshape(D, D)
        lw["w_up"] = _shards_to_cols(gathered[l, "ffn_up"], _pieces_ffn_up(), D_UP, f"ffn_up_cols_{l}")
        lw["w_down"] = gathered[l, "ffn_down"].reshape(D_FF, D)
        layer_w.append(lw)
        m, h1 = _matmul_resid_norm(merged, lw["w_out"], h, post_mix_norm[l:l + 1], f"out_proj_{l}")
        a2 = _rmsnorm_fwd(h1, pre_ffn_norm[l:l + 1], f"pre_ffn_norm_{l}")
        u = _matmul(a2, lw["w_up"], out_dtype=BF16, tm=TK_ROWS, tn=1408, tk=D, name=f"ffn_up_{l}", n_outer=True)
        keys = gather_in_conv.get(l, [])
        cv, act, *got = _conv_act_fwd(u, cw8[l], f"ffn_conv_act_{l}", carried=[shard[key] for key in keys])
        gathered.update(zip(keys, got))
        f, h2, *loss_acc = _matmul_resid_norm(act, lw["w_down"], h1, post_ffn_norm[l:l + 1], f"ffn_down_{l}",
                                              target=target if l == DEPTH - 1 else None)
        saved.append(dict(h=h, a1=a1, proj=proj, ocat=ocat, merged=merged, sr=sr_all, sg=sg_all, m=m, h1=h1,
                          a2=a2, u=u, cv=cv, act=act, f=f))
        h = h2

    dh = h
    loss = lax.psum(loss_acc[0][0, 0], ("x", "y", "c"))

    kinds = ("grad", "delta", "new_m", "new_v")
    grads = {n: [None] * DEPTH for n in WEIGHT_ORDER if n != "meta_tokens" and n not in big_names}
    pending, parts = [], {}
    for l in reversed(range(DEPTH)):
        s, lw = saved[l], layer_w[l]
        dact, df, g_post_ffn = _norm_bwd_matmul(dh, s["f"], post_ffn_norm[l:l + 1], lw["w_down"], BF16,
                                                f"ffn_down_dx_{l}")
        g_down = _matmul(s["act"], df, ta=True, out_dtype=BF16, tm=D_FF // 2, tn=D, tk=TK_ROWS, name=f"ffn_down_dw_{l}")
        du, dcw = _conv_act_bwd(dact, s["cv"], s["u"], cw8[l], f"ffn_conv_act_bwd_{l}")
        dh1, g_pre_ffn = _matmul_norm_bwd(du, lw["w_up"], s["h1"], pre_ffn_norm[l:l + 1], dh, D_FF, f"ffn_up_dx_{l}")
        g_up = _matmul(s["a2"], du, ta=True, out_dtype=BF16, tm=D, tn=1408, tk=TK_ROWS, name=f"ffn_up_dw_{l}")
        dmerged, dm, g_post_mix = _norm_bwd_matmul(dh1, s["m"], post_mix_norm[l:l + 1], lw["w_out"], F32,
                                                   f"out_proj_dx_{l}")
        g_out = _matmul(s["merged"], dm, ta=True, out_dtype=BF16, tm=D, tn=D, tk=TK_ROWS, name=f"out_proj_dw_{l}")
        pending += [((l, "ffn_down"), g_down.reshape(N_DEV, D_FF // N_DEV, D)),
                    ((l, "ffn_up"), _cols_to_shards(g_up, _pieces_ffn_up(), UP_SHARD_P, f"ffn_up_grad_shards_{l}")),
                    ((l, "w_out"), g_out.reshape(N_DEV, D // N_DEV, D))]
        dproj, g_w2, g_gb, g_rn, g_gn, *got = _mixer_bwd(s["proj"], s["ocat"], dmerged, s["sr"], s["sg"], cos2, sin2,
                                                         w2p[l], gla_gate_b[l:l + 1], ret_norm_w[l:l + 1],
                                                         gla_norm_w[l:l + 1], f"mixer_bwd_{l}",
                                                         carried=[blocks for _, blocks in pending])
        parts.update(zip([key for key, _ in pending], got))
        g_in = _matmul(s["a1"], dproj, ta=True, out_dtype=BF16, tm=D, tn=1280, tk=TK_ROWS, name=f"in_proj_dw_{l}")
        pending = [((l, "w_in"), _cols_to_shards(g_in, _pieces_w_in(), IN_SHARD_P, f"w_in_grad_shards_{l}"))]
        now = pending if l == 0 else []
        dh, g_pre_mix, *got = _matmul_norm_bwd(dproj, lw["w_in"], s["h"], pre_mix_norm[l:l + 1], dh1, IN_WP,
                                               f"in_proj_dx_{l}", carried=[blocks for _, blocks in now])
        parts.update(zip([key for key, _ in now], got))
        pending = [] if l == 0 else pending
        grads["post_ffn_norm"][l] = g_post_ffn[0]
        grads["ffn_conv_w"][l] = _deinterleave_cols(dcw[0:3])
        grads["ffn_conv_b"][l] = _deinterleave_cols(dcw[3])
        grads["pre_ffn_norm"][l] = g_pre_ffn[0]
        grads["post_mix_norm"][l] = g_post_mix[0]
        grads["gla_gate_w2"][l] = g_w2[:GATE_RANK]
        grads["gla_gate_b"][l] = g_gb[0]
        grads["ret_norm_w"][l] = g_rn[0]
        grads["gla_norm_w"][l] = g_gn[0]
        grads["pre_mix_norm"][l] = g_pre_mix[0]
    local = {n: jnp.stack(v) for n, v in grads.items()}
    local["meta_tokens"] = dh[PAD_ROWS:CHUNK]
    grad_x = dh[CHUNK:][None]

    blocks = jnp.concatenate([_to_blocks(local[n], ax).reshape(N_DEV, -1) for n, _, ax in SMALL], axis=1)
    blocks = jnp.pad(blocks, ((0, 0), (0, SMALL_ROWS * LANES - blocks.shape[1]))).reshape(N_DEV, SMALL_ROWS, LANES)
    *got, small_grad_parts = _exchange_blocks([b for _, b in pending] + [blocks], "exchange_last_grads")
    parts.update(zip([key for key, _ in pending], got))

    widths = dict(w_in=IN_SHARD_P, w_out=D, ffn_up=UP_SHARD_P, ffn_down=D)
    steps = dict(w_in=256, w_out=D // N_DEV, ffn_up=256, ffn_down=D_FF // N_DEV // 2)
    big_out = {kind: {n: [None] * DEPTH for n in big_names} for kind in kinds}
    for l in range(DEPTH):
        for n in big_names:
            mine = [pad_cols(d[n][l], widths[n]) for d in (weights, mom1, mom2)]
            results = _adamw(parts[l, n], *mine, steps[n], f"adamw_{n}_{l}")
            for kind, r in zip(kinds, results):
                big_out[kind][n][l] = r[:, :weights[n].shape[2]]
    out = {kind: {n: jnp.stack(v) for n, v in big_out[kind].items()} for kind in kinds}
    shard_shapes = [s for _, s, _ in SMALL]
    packed = [_pack([d[n] for n, _, _ in SMALL], SMALL_ROWS, F32) for d in (weights, mom1, mom2)]
    results = _adamw(small_grad_parts, *packed, SMALL_ROWS, "adamw_small_sharded")
    for kind, buf in zip(kinds, results):
        out[kind].update(zip([n for n, _, _ in SMALL], _unpack(buf, shard_shapes)))

    repl_parts = _all_gather([_pack([local[n] for n, _ in REPL], REPL_ROWS, F32)], "gather_small_grads")[0]
    packed = [_pack([d[n] for n, _ in REPL], REPL_ROWS, F32) for d in (weights, mom1, mom2)]
    results = _adamw(repl_parts, *packed, REPL_ROWS, "adamw_replicated")
    repl_shapes = [s for _, s in REPL]
    for kind, buf in zip(kinds, results):
        out[kind].update(zip([n for n, _ in REPL], _unpack(buf, repl_shapes)))

    return (loss, grad_x, *[out["grad"][n] for n in WEIGHT_ORDER], *[out["delta"][n] for n in WEIGHT_ORDER],
            *[out["new_m"][n] for n in WEIGHT_ORDER], *[out["new_v"][n] for n in WEIGHT_ORDER])


def _unpack_blocks(gathered, shapes):
    flat = gathered.reshape(N_DEV, -1)
    out, off = [], 0
    for shape in shapes:
        out.append(flat[:, off:off + _size(shape)].reshape((N_DEV,) + shape))
        off += _size(shape)
    return out
```

```python
import math

import jax
import jax.numpy as jnp
from jax import lax
from jax.experimental import pallas as pl
from jax.experimental.pallas import tpu as pltpu

F32 = jnp.float32
BF16 = jnp.bfloat16

D = 1024
SEQ = 8192
DEPTH = 2
N_META = 16
CHUNK = 64
SUB = 16
N_SUB = CHUNK // SUB
PAD_ROWS = CHUNK - N_META
LP = SEQ + CHUNK
N_CHUNKS = LP // CHUNK
RET_HEADS = 4
RET_DK = 128
GLA_HEADS = 4
GLA_DK = 64
GLA_DV = 128
GLA_TAU = 16.0
GATE_RANK = 16
IN_W = 3600
IN_WP = 3840
D_FF = 2816
D_UP = 2 * D_FF
CONV_BLOCK = 256
N_CONV_BLOCKS = D_FF // CONV_BLOCK
ROPE_BASE = 10000.0
EPS = 1e-6
N_DEV = 8
LANES = 1024

O_RQ, O_RK, O_RV, O_RG = 0, 512, 1024, 1536
O_GQ, O_GK, O_GV, O_GR, O_GA = 2048, 2304, 2560, 3072, 3584

ADAM_LR = 0.001
ADAM_B1 = 0.9
ADAM_B2 = 0.999
ADAM_EPS = 1e-08
ADAM_WD = 0.01
ADAM_STEP = 10

VMEM_LIMIT = 56 * 1024 * 1024
MESH_IDS = pl.DeviceIdType.MESH


def _row_tile(rows, limit):
    best = 16
    for t in range(16, min(rows, limit) + 1, 16):
        if rows % t == 0:
            best = t
    return best


TM = _row_tile(LP, 688)
TK_ROWS = _row_tile(LP, 1376)


def _cparams(*sem):
    return pltpu.CompilerParams(dimension_semantics=sem, vmem_limit_bytes=VMEM_LIMIT)


def _dot(a, b):
    return jnp.dot(a.astype(BF16), b.astype(BF16), preferred_element_type=F32)


def _dot_nt(a, b):
    return lax.dot_general(a.astype(BF16), b.astype(BF16), (((1,), (1,)), ((), ())), preferred_element_type=F32)


def _dot_tn(a, b):
    return lax.dot_general(a.astype(BF16), b.astype(BF16), (((0,), (0,)), ((), ())), preferred_element_type=F32)


def _split3(x):
    hi = x.astype(BF16)
    r1 = x - hi.astype(F32)
    mid = r1.astype(BF16)
    lo = (r1 - mid.astype(F32)).astype(BF16)
    return hi, mid, lo


def _dot_exact_rhs(t, x):
    n = x.shape[1]
    parts = jnp.dot(t.astype(BF16), jnp.concatenate(_split3(x), axis=1), preferred_element_type=F32)
    return parts[:, :n] + parts[:, n:2 * n] + parts[:, 2 * n:]


def _dot_tn_exact_lhs(x, ones):
    n = x.shape[1]
    parts = lax.dot_general(jnp.concatenate(_split3(x), axis=1), ones.astype(BF16), (((0,), (0,)), ((), ())),
                            preferred_element_type=F32)
    return parts[:n] + parts[n:2 * n] + parts[2 * n:]


def _sigmoid(x):
    return 1.0 / (1.0 + jnp.exp(-x))


def _matmul(a, b, *, ta=False, tb=False, out_dtype, tm, tn, tk, name, n_outer=False):
    m = a.shape[1] if ta else a.shape[0]
    k = a.shape[0] if ta else a.shape[1]
    n = b.shape[0] if tb else b.shape[1]
    assert (b.shape[1] if tb else b.shape[0]) == k
    assert m % tm == 0 and n % tn == 0 and k % tk == 0, (name, m, n, k, tm, tn, tk)
    nk = k // tk
    order = (lambda f: (lambda j, i, kk: f(i, j, kk))) if n_outer else (lambda f: f)
    a_spec = (pl.BlockSpec((tk, tm), order(lambda i, j, kk: (kk, i))) if ta
              else pl.BlockSpec((tm, tk), order(lambda i, j, kk: (i, kk))))
    b_spec = (pl.BlockSpec((tn, tk), order(lambda i, j, kk: (j, kk))) if tb
              else pl.BlockSpec((tk, tn), order(lambda i, j, kk: (kk, j))))
    dims = (((0 if ta else 1,), (1 if tb else 0,)), ((), ()))

    def body(a_ref, b_ref, o_ref, *acc):
        prod = lax.dot_general(a_ref[...].astype(BF16), b_ref[...].astype(BF16), dims, preferred_element_type=F32)
        if nk == 1:
            o_ref[...] = prod.astype(out_dtype)
            return
        acc_ref, = acc
        kk = pl.program_id(2)

        @pl.when(kk == 0)
        def _():
            acc_ref[...] = prod

        @pl.when(kk > 0)
        def _():
            acc_ref[...] += prod

        @pl.when(kk == nk - 1)
        def _():
            o_ref[...] = acc_ref[...].astype(out_dtype)

    return pl.pallas_call(
        body, name=name, grid=(n // tn, m // tm, nk) if n_outer else (m // tm, n // tn, nk),
        in_specs=[a_spec, b_spec],
        out_specs=pl.BlockSpec((tm, tn), order(lambda i, j, kk: (i, j))),
        out_shape=jax.ShapeDtypeStruct((m, n), out_dtype),
        scratch_shapes=[pltpu.VMEM((tm, tn), F32)] if nk > 1 else [],
        compiler_params=_cparams("parallel", "parallel", "arbitrary"),
    )(a, b)


def _matmul_resid_norm(a, b, h, w, name, target=None):
    k = a.shape[1]
    has_loss = target is not None

    def body(a_ref, b_ref, h_ref, w_ref, *refs):
        m = jnp.dot(a_ref[...].astype(BF16), b_ref[...].astype(BF16), preferred_element_type=F32)
        r = lax.rsqrt(jnp.mean(m * m, axis=-1, keepdims=True) + EPS)
        i = pl.program_id(0)
        row = i * TM + lax.broadcasted_iota(jnp.int32, (TM, 1), 0)
        y = h_ref[...] + jnp.where(row >= PAD_ROWS, m * r * w_ref[...], 0.0)
        if not has_loss:
            m_ref, y_ref = refs
            m_ref[...] = m
            y_ref[...] = y
            return
        t_ref, m_ref, dy_ref, loss_ref = refs
        m_ref[...] = m

        @pl.when(i == 0)
        def _():
            loss_ref[...] = jnp.zeros_like(loss_ref)

        diff = jnp.where(row >= CHUNK, y - t_ref[...], 0.0)
        dy_ref[...] = diff * (1.0 / D)
        loss_ref[...] += (0.5 / D) * jnp.sum(diff * diff)

    tile = pl.BlockSpec((TM, D), lambda i: (i, 0))
    shape = jax.ShapeDtypeStruct((LP, D), F32)
    in_specs = [pl.BlockSpec((TM, k), lambda i: (i, 0)), pl.BlockSpec((k, D), lambda i: (0, 0)), tile,
                pl.BlockSpec((1, D), lambda i: (0, 0))]
    if has_loss:
        return pl.pallas_call(
            body, name=name, grid=(LP // TM,),
            in_specs=in_specs + [tile],
            out_specs=[tile, tile, pl.BlockSpec((8, 128), lambda i: (0, 0))],
            out_shape=[shape, shape, jax.ShapeDtypeStruct((8, 128), F32)],
            compiler_params=_cparams("arbitrary"),
        )(a, b, h, w, target)
    return pl.pallas_call(
        body, name=name, grid=(LP // TM,),
        in_specs=in_specs, out_specs=[tile, tile], out_shape=[shape, shape],
        compiler_params=_cparams("parallel"),
    )(a, b, h, w)


def _rmsnorm_bwd_rows(dy, x, w):
    r = lax.rsqrt(jnp.mean(x * x, axis=-1, keepdims=True) + EPS)
    g = dy * w
    dx = r * g - x * (r * r * r * jnp.mean(g * x, axis=-1, keepdims=True))
    return dx, jnp.sum(dy * x * r, axis=0, keepdims=True)


def _matmul_norm_bwd(dz, b, x, w, resid, tk, name, carried=()):
    k = dz.shape[1]
    assert k % tk == 0
    nk = k // tk
    n_rows = LP // TM
    n_carried = len(carried)

    def body(*refs):
        a_ref, b_ref, x_ref, w_ref, r_ref = refs[:5]
        g_refs, refs = refs[5:5 + n_carried], refs[5 + n_carried:]
        dx_ref, dw_ref = refs[:2]
        got_refs, refs = refs[2:2 + n_carried], refs[2 + n_carried:]
        acc, sems = (refs[:1], refs[1:]) if nk > 1 else ((), refs)
        i, kk = pl.program_id(0), pl.program_id(1)
        if n_carried:
            exchange_start, exchange_finish = _exchange_phases(g_refs, got_refs, *sems)
            pl.when((i == 0) & (kk == 0))(exchange_start)

        @pl.when((i == 0) & (kk == 0))
        def _():
            dw_ref[...] = jnp.zeros_like(dw_ref)

        prod = lax.dot_general(a_ref[...].astype(BF16), b_ref[...].astype(BF16), (((1,), (1,)), ((), ())),
                               preferred_element_type=F32)

        def finish(dy):
            dx, dw = _rmsnorm_bwd_rows(dy, x_ref[...], w_ref[...])
            dx_ref[...] = dx + r_ref[...]
            dw_ref[0:1, :] += dw

        if nk == 1:
            finish(prod)
        else:
            acc_ref, = acc

            @pl.when(kk == 0)
            def _():
                acc_ref[...] = prod

            @pl.when((kk > 0) & (kk < nk - 1))
            def _():
                acc_ref[...] += prod

            @pl.when(kk == nk - 1)
            def _():
                finish(acc_ref[...] + prod)

        if n_carried:
            pl.when((i == n_rows - 1) & (kk == nk - 1))(exchange_finish)

    tile = pl.BlockSpec((TM, D), lambda i, kk: (i, 0))
    anywhere = [pl.BlockSpec(memory_space=pl.ANY)] * n_carried
    return pl.pallas_call(
        body, name=name, grid=(n_rows, nk),
        in_specs=[pl.BlockSpec((TM, tk), lambda i, kk: (i, kk)), pl.BlockSpec((D, tk), lambda i, kk: (0, kk)), tile,
                  pl.BlockSpec((1, D), lambda i, kk: (0, 0)), tile] + anywhere,
        out_specs=[tile, pl.BlockSpec((8, D), lambda i, kk: (0, 0))] + anywhere,
        out_shape=[jax.ShapeDtypeStruct((LP, D), F32), jax.ShapeDtypeStruct((8, D), F32)]
        + [jax.ShapeDtypeStruct(g.shape, g.dtype) for g in carried],
        scratch_shapes=([pltpu.VMEM((TM, D), F32)] if nk > 1 else []) + _exchange_sems(n_carried),
        compiler_params=_cparams("arbitrary", "arbitrary"),
    )(dz, b, x, w, resid, *carried)


def _norm_bwd_matmul(dh, x, w, b, out_dtype, name):
    n = b.shape[0]

    def body(dh_ref, x_ref, w_ref, b_ref, o_ref, dx_ref, dw_ref):
        i = pl.program_id(0)

        @pl.when(i == 0)
        def _():
            dw_ref[...] = jnp.zeros_like(dw_ref)

        row = i * TM + lax.broadcasted_iota(jnp.int32, (TM, 1), 0)
        dy = jnp.where(row >= PAD_ROWS, dh_ref[...], 0.0)
        dx, dw = _rmsnorm_bwd_rows(dy, x_ref[...], w_ref[...])
        dxb = dx.astype(BF16)
        dx_ref[...] = dxb
        dw_ref[0:1, :] += dw
        o_ref[...] = lax.dot_general(dxb, b_ref[...].astype(BF16), (((1,), (1,)), ((), ())),
                                     preferred_element_type=F32).astype(out_dtype)

    tile = pl.BlockSpec((TM, D), lambda i: (i, 0))
    return pl.pallas_call(
        body, name=name, grid=(LP // TM,),
        in_specs=[tile, tile, pl.BlockSpec((1, D), lambda i: (0, 0)), pl.BlockSpec((n, D), lambda i: (0, 0))],
        out_specs=[pl.BlockSpec((TM, n), lambda i: (i, 0)), tile, pl.BlockSpec((8, D), lambda i: (0, 0))],
        out_shape=[jax.ShapeDtypeStruct((LP, n), out_dtype), jax.ShapeDtypeStruct((LP, D), BF16),
                   jax.ShapeDtypeStruct((8, D), F32)],
        compiler_params=_cparams("arbitrary"),
    )(dh, x, w, b)


def _rmsnorm_fwd(x, w, name):
    def body(x_ref, w_ref, o_ref):
        xv = x_ref[...]
        r = lax.rsqrt(jnp.mean(xv * xv, axis=-1, keepdims=True) + EPS)
        o_ref[...] = (xv * r * w_ref[...]).astype(BF16)

    return pl.pallas_call(
        body, name=name, grid=(LP // TM,),
        in_specs=[pl.BlockSpec((TM, D), lambda i: (i, 0)), pl.BlockSpec((1, D), lambda i: (0, 0))],
        out_specs=pl.BlockSpec((TM, D), lambda i: (i, 0)),
        out_shape=jax.ShapeDtypeStruct((LP, D), BF16),
        compiler_params=_cparams("parallel"),
    )(x, w)


GELU_C = math.sqrt(2.0 / math.pi)
GELU_K = 0.044715
STRIP = 16
HALF = 8


def _gelu_half(a):
    return 0.5 * jnp.tanh(a * (a * a * (GELU_C * GELU_K) + GELU_C)) + 0.5


def _gelu_slope(a, h):
    return h * (1.0 + (a - a * h) * (a * a * (6.0 * GELU_C * GELU_K) + 2.0 * GELU_C))


def _shift_down(x, prev8, rows):
    row = lax.broadcasted_iota(jnp.int32, (rows, 1), 0)
    p1 = pltpu.roll(prev8, 1, 0)
    p2 = pltpu.roll(prev8, 2, 0)
    x1 = jnp.where(row == 0, p1[0:1, :], pltpu.roll(x, 1, 0))
    x2 = jnp.where(row == 0, p2[0:1, :], jnp.where(row == 1, p2[1:2, :], pltpu.roll(x, 2, 0)))
    return x1, x2


def _conv_act_fwd(u, cw8, name, carried=()):
    n_rows = LP // TM
    cb2 = 2 * CONV_BLOCK
    n_carried = len(carried)

    def body(*refs):
        u_ref, cw_ref = refs[:2]
        x_refs, refs = refs[2:2 + n_carried], refs[2 + n_carried:]
        conv_ref, act_ref = refs[:2]
        gathered_refs, refs = refs[2:2 + n_carried], refs[2 + n_carried:]
        carry_ref = refs[0]
        j, i = pl.program_id(0), pl.program_id(1)
        if n_carried:
            start, forward, finish = _gather_phases(x_refs, gathered_refs, *refs[1:])
            pl.when((j == 0) & (i == 0))(start)
            pl.when((j == (3 * N_CONV_BLOCKS) // 4) & (i == 0))(forward)

        @pl.when(i == 0)
        def _():
            carry_ref[...] = jnp.zeros_like(carry_ref)

        x = u_ref[...].astype(F32)
        x1, x2 = _shift_down(x, carry_ref[...], TM)
        conv = cw_ref[3:4, :] + x2 * cw_ref[0:1, :] + x1 * cw_ref[1:2, :] + x * cw_ref[2:3, :]
        conv_ref[...] = conv.astype(BF16)
        a = conv[:, :CONV_BLOCK]
        g = conv[:, CONV_BLOCK:]
        act_ref[...] = (a * _gelu_half(a) * g).astype(BF16)
        carry_ref[...] = x[TM - 8:TM, :]
        if n_carried:
            pl.when((j == N_CONV_BLOCKS - 1) & (i == n_rows - 1))(finish)

    anywhere = [pl.BlockSpec(memory_space=pl.ANY)] * n_carried
    return pl.pallas_call(
        body, name=name, grid=(N_CONV_BLOCKS, n_rows),
        in_specs=[pl.BlockSpec((TM, cb2), lambda j, i: (i, j)), pl.BlockSpec((8, cb2), lambda j, i: (0, j))] + anywhere,
        out_specs=[pl.BlockSpec((TM, cb2), lambda j, i: (i, j)),
                   pl.BlockSpec((TM, CONV_BLOCK), lambda j, i: (i, j))] + anywhere,
        out_shape=[jax.ShapeDtypeStruct((LP, D_UP), BF16), jax.ShapeDtypeStruct((LP, D_FF), BF16)]
        + _gathered_shapes(carried),
        scratch_shapes=[pltpu.VMEM((8, cb2), F32)] + _exchange_sems(n_carried),
        compiler_params=_cparams("arbitrary", "arbitrary"),
    )(u, cw8, *carried)


def _conv_act_bwd(dact, conv, u, cw8, name):
    n_rows = LP // TM
    cb2 = 2 * CONV_BLOCK
    n_strips = TM // STRIP

    def body(dact_ref, conv_ref, u_ref, cw_ref, du_ref, dcw_ref, carry_ref):
        i = pl.program_id(1)

        @pl.when(i == 0)
        def _():
            dcw_ref[...] = jnp.zeros_like(dcw_ref)
            carry_ref[...] = jnp.zeros_like(carry_ref)

        w0, w1, w2 = cw_ref[0:1, :], cw_ref[1:2, :], cw_ref[2:3, :]
        row = lax.broadcasted_iota(jnp.int32, (HALF, 1), 0)

        def strip(k, carry):
            n1, n2, s0, s1, s2, s3 = carry
            r0 = pl.multiple_of((n_strips - 1 - k) * STRIP, STRIP)
            cv = conv_ref[pl.ds(r0, STRIP), :].astype(F32)
            dav = dact_ref[pl.ds(r0, STRIP), :].astype(F32)
            x = u_ref[pl.ds(r0, STRIP), :].astype(F32)
            du = [None, None]
            for half in (1, 0):
                rows = slice(HALF * half, HALF * (half + 1))
                a, g, dah = cv[rows, :CONV_BLOCK], cv[rows, CONV_BLOCK:], dav[rows]
                h = _gelu_half(a)
                dconv = jnp.concatenate([dah * g * _gelu_slope(a, h), dah * (a * h)], axis=1)
                u1, u2 = pltpu.roll(dconv, HALF - 1, 0), pltpu.roll(dconv, HALF - 2, 0)
                d1 = jnp.where(row >= HALF - 1, n1, u1)
                d2 = jnp.where(row >= HALF - 2, n2, u2)
                du[half] = dconv * w2 + d1 * w1 + d2 * w0
                s0, s1, s2, s3 = s0 + d2 * x[rows], s1 + d1 * x[rows], s2 + dconv * x[rows], s3 + dconv
                n1, n2 = u1, u2
            du_ref[pl.ds(r0, STRIP), :] = jnp.concatenate(du, axis=0).astype(BF16)
            return n1, n2, s0, s1, s2, s3

        below = carry_ref[...]
        zero = jnp.zeros((HALF, cb2), F32)
        init = (pltpu.roll(below, HALF - 1, 0), pltpu.roll(below, HALF - 2, 0), zero, zero, zero, zero)
        u1, _, s0, s1, s2, s3 = lax.fori_loop(0, n_strips, strip, init, unroll=2)
        carry_ref[...] = pltpu.roll(u1, 1, 0)
        dcw_ref[0:1, :] += jnp.sum(s0, axis=0, keepdims=True)
        dcw_ref[1:2, :] += jnp.sum(s1, axis=0, keepdims=True)
        dcw_ref[2:3, :] += jnp.sum(s2, axis=0, keepdims=True)
        dcw_ref[3:4, :] += jnp.sum(s3, axis=0, keepdims=True)

    rev = lambda j, i: (n_rows - 1 - i, j)
    return pl.pallas_call(
        body, name=name, grid=(N_CONV_BLOCKS, n_rows),
        in_specs=[pl.BlockSpec((TM, CONV_BLOCK), rev), pl.BlockSpec((TM, cb2), rev), pl.BlockSpec((TM, cb2), rev),
                  pl.BlockSpec((8, cb2), lambda j, i: (0, j))],
        out_specs=[pl.BlockSpec((TM, cb2), rev), pl.BlockSpec((8, cb2), lambda j, i: (0, j))],
        out_shape=[jax.ShapeDtypeStruct((LP, D_UP), BF16), jax.ShapeDtypeStruct((8, D_UP), F32)],
        scratch_shapes=[pltpu.VMEM((HALF, cb2), F32)],
        compiler_params=_cparams("arbitrary", "arbitrary"),
    )(dact, conv, u, cw8)


CHUNKS_PER_STEP = 3 if N_CHUNKS % 3 == 0 else 1
STEP_ROWS = CHUNKS_PER_STEP * CHUNK
N_STEPS = N_CHUNKS // CHUNKS_PER_STEP


def _ret_consts(h):
    rows = STEP_ROWS
    lg = math.log(1.0 - 2.0 ** (-5.0 - h))
    ri = lax.broadcasted_iota(jnp.int32, (rows, rows), 0)
    ci = lax.broadcasted_iota(jnp.int32, (rows, rows), 1)
    diff = (ri - ci).astype(F32)
    dmat = jnp.where(diff >= 0, jnp.exp(lg * jnp.maximum(diff, 0.0)), 0.0)
    rowf = lax.broadcasted_iota(jnp.int32, (rows, 1), 0).astype(F32)
    zeta = jnp.exp(lg * (rows - 1.0 - rowf))
    xi = jnp.exp(lg * (rowf + 1.0))
    return dmat, zeta, xi, math.exp(lg * rows)


def _rope(t, cosv, sinv):
    return t * cosv + pltpu.roll(t, RET_DK // 2, 1) * sinv


def _unrope(d, cosv, sinv):
    return d * cosv + pltpu.roll(d * sinv, RET_DK // 2, 1)


def _gla_masks():
    ri = lax.broadcasted_iota(jnp.int32, (CHUNK, CHUNK), 0)
    ci = lax.broadcasted_iota(jnp.int32, (CHUNK, CHUNK), 1)
    return dict(ri=ri, ci=ci, tril=(ri >= ci).astype(F32), heads=_head_block_mask(), own=_state_block_mask())


def _gla_common(p_ref, w2_ref, gb_ref, chunk, rows, masks):
    row = lax.broadcasted_iota(jnp.int32, (CHUNK, 1), 0)
    real = (chunk * CHUNK + row) >= PAD_ROWS
    ga = p_ref[rows, O_GA:O_GA + 128]
    z = _dot(ga, w2_ref[...]) + gb_ref[...]
    la = (jnp.minimum(z, 0.0) - jnp.log(1.0 + jnp.exp(-jnp.abs(z)))) * (1.0 / GLA_TAU)
    la = jnp.where(real, la, 0.0)
    ri, ci = masks["ri"], masks["ci"]
    cum = _dot_exact_rhs(masks["tril"], la)
    last = cum[CHUNK - 1:CHUNK, :]
    qs = p_ref[rows, O_GQ:O_GQ + 256] * (GLA_DK ** -0.5)
    k = p_ref[rows, O_GK:O_GK + 256]
    ecum = jnp.exp(cum)
    ekl = jnp.exp(last - cum)
    el = jnp.exp(last)
    refs = [jnp.zeros((1, 256), F32)] + [cum[a * SUB - 1:a * SUB, :] for a in range(1, N_SUB)]
    eq = [jnp.exp(cum[a * SUB:(a + 1) * SUB, :] - refs[a]) for a in range(N_SUB)]
    spread = refs[0] - cum[SUB - 1:SUB, :]
    for a in range(1, N_SUB):
        spread = jnp.maximum(spread, refs[a] - cum[(a + 1) * SUB - 1:(a + 1) * SUB, :])
    small = jnp.max(spread) <= GLA_FACTORED_MAX
    return dict(real=real, row=row, z=z, la=la, cum=cum, last=last, qs=qs, k=k, ecum=ecum, ekl=ekl, el=el,
                refs=refs, eq=eq, small=small, ri=ri, ci=ci, masks=masks)


GLA_FACTORED_MAX = 40.0


def _head_block_mask():
    r = lax.broadcasted_iota(jnp.int32, (CHUNK, 256), 0)
    col = lax.broadcasted_iota(jnp.int32, (CHUNK, 256), 1)
    return (r // SUB) == (col // GLA_DK)


def _state_block_mask():
    r = lax.broadcasted_iota(jnp.int32, (GLA_HEADS * GLA_DK, GLA_HEADS * GLA_DV), 0)
    col = lax.broadcasted_iota(jnp.int32, (GLA_HEADS * GLA_DK, GLA_HEADS * GLA_DV), 1)
    return (r // GLA_DK) == (col // GLA_DV)


def _block_diagonal(blocks):
    zero = jnp.zeros((GLA_DK, GLA_DV), F32)
    return jnp.concatenate([jnp.concatenate([blocks[h] if g == h else zero for g in range(GLA_HEADS)], axis=1)
                            for h in range(GLA_HEADS)], axis=0)


def _gla_factored(c):
    mask = c["masks"]["heads"]
    eks, keys, queries = [], [], []
    for a in range(N_SUB):
        ek = jnp.exp(jnp.minimum(c["refs"][a] - c["cum"], GLA_FACTORED_MAX))
        qh = c["qs"][a * SUB:(a + 1) * SUB, :] * c["eq"][a]
        eks.append(ek)
        keys.append(c["k"] * ek)
        queries.append(jnp.where(mask, jnp.concatenate([qh] * GLA_HEADS, axis=0), 0.0))
    return eks, keys, queries


def _gla_scores_factored(c, factored, p_scr):
    _, keys, queries = factored
    for a in range(N_SUB):
        out = _dot_nt(queries[a], keys[a])
        out = jnp.where(c["ci"] <= a * SUB + (c["ri"] & (SUB - 1)), out, 0.0)
        for h in range(GLA_HEADS):
            p_scr[h, a * SUB:(a + 1) * SUB, :] = out[h * SUB:(h + 1) * SUB, :]


def _gla_intra_bwd_factored(c, factored, dps, dq_scr, dk_scr):
    eks, keys, queries = factored
    mask = c["masks"]["heads"]
    dk = jnp.zeros((CHUNK, 256), F32)
    for a in range(N_SUB):
        dpa = jnp.concatenate([dps[h][a * SUB:(a + 1) * SUB, :] for h in range(GLA_HEADS)], axis=0)
        dq = jnp.where(mask, _dot(dpa, keys[a]), 0.0)
        dq = dq[0:SUB] + dq[SUB:2 * SUB] + dq[2 * SUB:3 * SUB] + dq[3 * SUB:4 * SUB]
        dq_scr[a * SUB:(a + 1) * SUB, :] = dq * c["eq"][a]
        dk = dk + _dot_tn(dpa, queries[a]) * eks[a]
    dk_scr[...] = dk


def _gla_lag_weights(c):
    cum, row = c["cum"], c["row"]
    out = [jnp.ones((CHUNK, 256), F32)]
    for r in range(1, SUB):
        out.append(jnp.where((row % SUB) >= r, jnp.exp(jnp.minimum(cum - pltpu.roll(cum, r, 0), 0.0)), 0.0))
    return out


def _gla_pairwise_keys(c):
    return [None] + [c["k"] * jnp.exp(jnp.minimum(c["refs"][a] - c["cum"], 0.0)) for a in range(1, N_SUB)]


def _gla_scores_pairwise(c, lag_w, keys, h):
    sl = slice(GLA_DK * h, GLA_DK * (h + 1))
    qs, k = c["qs"][:, sl], c["k"][:, sl]
    ri, ci = c["ri"], c["ci"]
    p = jnp.zeros((CHUNK, CHUNK), F32)
    for r in range(SUB):
        kr = k if r == 0 else pltpu.roll(k, r, 0)
        pr = jnp.sum(qs * kr * lag_w[r][:, sl], axis=1, keepdims=True)
        p = p + jnp.where(ci == ri - r, pr, 0.0)
    blocks = [jnp.zeros((SUB, CHUNK), F32)]
    for a in range(1, N_SUB):
        qh = qs[a * SUB:(a + 1) * SUB, :] * c["eq"][a][:, sl]
        blocks.append(jnp.where(ci[:SUB, :] < a * SUB, _dot_nt(qh, keys[a][:, sl]), 0.0))
    return p + jnp.concatenate(blocks, axis=0)


def _gla_all_scores(c, p_scr, factored):
    if factored:
        _gla_scores_factored(c, _gla_factored(c), p_scr)
    else:
        lag_w, keys = _gla_lag_weights(c), _gla_pairwise_keys(c)
        for h in range(GLA_HEADS):
            p_scr[h] = _gla_scores_pairwise(c, lag_w, keys, h)


def _either_form(chunks, run):
    small = chunks[0]["small"]
    for c in chunks[1:]:
        small = jnp.logical_and(small, c["small"])
    pl.when(small)(lambda: run(True))
    pl.when(jnp.logical_not(small))(lambda: run(False))


def _gla_intra_bwd_pairwise(c, lag_w, keys, dp, h):
    sl = slice(GLA_DK * h, GLA_DK * (h + 1))
    qs_h, k_h = c["qs"][:, sl], c["k"][:, sl]
    ri, ci = c["ri"], c["ci"]
    dq_rows = [jnp.zeros((SUB, GLA_DK), F32)]
    dk = jnp.zeros((CHUNK, GLA_DK), F32)
    for a in range(1, N_SUB):
        eq = c["eq"][a][:, sl]
        qh = qs_h[a * SUB:(a + 1) * SUB, :] * eq
        dpa = jnp.where(ci[:SUB, :] < a * SUB, dp[a * SUB:(a + 1) * SUB, :], 0.0)
        dq_rows.append(_dot(dpa, keys[a][:, sl]) * eq)
        ek = jnp.exp(jnp.minimum(c["refs"][a][:, sl] - c["cum"][:, sl], 0.0))
        dk = dk + _dot_tn(dpa, qh) * ek
    dq = jnp.concatenate(dq_rows, axis=0)
    for r in range(SUB):
        w = lag_w[r][:, sl]
        dpr = jnp.sum(jnp.where(ci == ri - r, dp, 0.0), axis=1, keepdims=True)
        kr = k_h if r == 0 else pltpu.roll(k_h, r, 0)
        dq = dq + dpr * kr * w
        back = dpr * qs_h * w
        dk = dk + (back if r == 0 else pltpu.roll(back, CHUNK - r, 0))
    return dq, dk


def _gla_all_intra_bwd(c, dps, p_scr, dq_scr, dk_scr, factored):
    if factored:
        terms = _gla_factored(c)
        _gla_scores_factored(c, terms, p_scr)
        _gla_intra_bwd_factored(c, terms, dps, dq_scr, dk_scr)
    else:
        lag_w, keys = _gla_lag_weights(c), _gla_pairwise_keys(c)
        outs = [_gla_intra_bwd_pairwise(c, lag_w, keys, dps[h], h) for h in range(GLA_HEADS)]
        for h in range(GLA_HEADS):
            p_scr[h] = _gla_scores_pairwise(c, lag_w, keys, h)
        dq_scr[...] = jnp.concatenate([o[0] for o in outs], axis=1)
        dk_scr[...] = jnp.concatenate([o[1] for o in outs], axis=1)


def _mixer_fwd(proj, cos2, sin2, w2p, gb, rnw, gnw, name, carried=()):
    n_carried = len(carried)

    def body(*refs):
        p_ref, c_ref, s_ref, w2_ref, gb_ref, rnw_ref, gnw_ref = refs[:7]
        x_refs, refs = refs[7:7 + n_carried], refs[7 + n_carried:]
        ocat_ref, mrg_ref, sr_out, sg_out = refs[:4]
        gathered_refs, refs = refs[4:4 + n_carried], refs[4 + n_carried:]
        sr, sg, p_scr = refs[:3]
        n = pl.program_id(0)
        if n_carried:
            start, forward, finish = _gather_phases(x_refs, gathered_refs, *refs[3:])
            pl.when(n == 0)(start)
            pl.when(n == (3 * N_STEPS) // 4)(forward)

        @pl.when(n == 0)
        def _():
            sr[...] = jnp.zeros_like(sr)
            sg[...] = jnp.zeros_like(sg)

        sr_out[0] = sr[...]
        cosv, sinv = c_ref[...], s_ref[...]

        for h in range(RET_HEADS):
            dmat, zeta, xi, gc = _ret_consts(h)
            hs = slice(128 * h, 128 * (h + 1))
            q = _rope(p_ref[:, O_RQ + 128 * h:O_RQ + 128 * (h + 1)], cosv, sinv)
            k = _rope(p_ref[:, O_RK + 128 * h:O_RK + 128 * (h + 1)], cosv, sinv) * (RET_DK ** -0.5)
            v = p_ref[:, O_RV + 128 * h:O_RV + 128 * (h + 1)]
            g = p_ref[:, O_RG + 128 * h:O_RG + 128 * (h + 1)]
            s_in = sr[h]
            a = _dot_nt(q, k) * dmat
            o = _dot(a, v) + _dot(q, s_in) * xi
            sr[h] = gc * s_in + _dot_tn(k * zeta, v)
            mu = jnp.mean(o, axis=-1, keepdims=True)
            xc = o - mu
            nrm = xc * lax.rsqrt(jnp.mean(xc * xc, axis=-1, keepdims=True) + EPS)
            ocat_ref[:, hs] = o
            mrg_ref[:, hs] = (nrm * rnw_ref[:, hs] * (g * _sigmoid(g))).astype(BF16)

        row_slices = [slice(CHUNK * j, CHUNK * (j + 1)) for j in range(CHUNKS_PER_STEP)]
        masks = _gla_masks()
        chunks = [_gla_common(p_ref, w2_ref, gb_ref, n * CHUNKS_PER_STEP + j, rows, masks)
                  for j, rows in enumerate(row_slices)]

        def gla_chunks(factored):
            own = masks["own"]
            for j, (rows, c) in enumerate(zip(row_slices, chunks)):
                s_in = sg[...]
                for h in range(GLA_HEADS):
                    sg_out[j, h] = s_in[GLA_DK * h:GLA_DK * (h + 1), GLA_DV * h:GLA_DV * (h + 1)]
                _gla_all_scores(c, p_scr.at[j], factored)
                v_all = p_ref[rows, O_GV:O_GV + GLA_HEADS * GLA_DV]
                o_inter = _dot(c["qs"] * c["ecum"], s_in)
                decay = jnp.exp(_dot_tn_exact_lhs(c["la"], jnp.ones((CHUNK, GLA_HEADS * GLA_DV), F32)))
                sg[...] = decay * s_in + jnp.where(own, _dot_tn(c["k"] * c["ekl"], v_all), 0.0)
                o_intra = _dot(p_scr[j].reshape(GLA_HEADS * CHUNK, CHUNK), v_all)
                for h in range(GLA_HEADS):
                    hs = slice(512 + 128 * h, 512 + 128 * (h + 1))
                    g = p_ref[rows, O_GR + 128 * h:O_GR + 128 * (h + 1)]
                    o = (o_intra[CHUNK * h:CHUNK * (h + 1), GLA_DV * h:GLA_DV * (h + 1)]
                         + o_inter[:, GLA_DV * h:GLA_DV * (h + 1)])
                    nrm = o * lax.rsqrt(jnp.mean(o * o, axis=-1, keepdims=True) + EPS)
                    ocat_ref[rows, hs] = o
                    mrg_ref[rows, hs] = (nrm * gnw_ref[:, 128 * h:128 * (h + 1)] * (g * _sigmoid(g))).astype(BF16)

        _either_form(chunks, gla_chunks)

        if n_carried:
            pl.when(n == N_STEPS - 1)(finish)

    const = lambda shape: pl.BlockSpec(shape, lambda n: (0,) * len(shape))
    anywhere = [pl.BlockSpec(memory_space=pl.ANY)] * n_carried
    return pl.pallas_call(
        body, name=name, grid=(N_STEPS,),
        in_specs=[pl.BlockSpec((STEP_ROWS, IN_WP), lambda n: (n, 0)),
                  pl.BlockSpec((STEP_ROWS, 128), lambda n: (n, 0)), pl.BlockSpec((STEP_ROWS, 128), lambda n: (n, 0)),
                  const((128, 256)), const((1, 256)), const((1, 512)), const((1, 512))] + anywhere,
        out_specs=[pl.BlockSpec((STEP_ROWS, D), lambda n: (n, 0)), pl.BlockSpec((STEP_ROWS, D), lambda n: (n, 0)),
                   pl.BlockSpec((1, RET_HEADS, RET_DK, 128), lambda n: (n, 0, 0, 0)),
                   pl.BlockSpec((CHUNKS_PER_STEP, GLA_HEADS, GLA_DK, GLA_DV), lambda n: (n, 0, 0, 0))] + anywhere,
        out_shape=[jax.ShapeDtypeStruct((LP, D), F32), jax.ShapeDtypeStruct((LP, D), BF16),
                   jax.ShapeDtypeStruct((N_STEPS, RET_HEADS, RET_DK, 128), F32),
                   jax.ShapeDtypeStruct((N_CHUNKS, GLA_HEADS, GLA_DK, GLA_DV), F32)] + _gathered_shapes(carried),
        scratch_shapes=[pltpu.VMEM((RET_HEADS, RET_DK, 128), F32),
                        pltpu.VMEM((GLA_HEADS * GLA_DK, GLA_HEADS * GLA_DV), F32),
                        pltpu.VMEM((CHUNKS_PER_STEP, GLA_HEADS, CHUNK, CHUNK), F32)] + _exchange_sems(n_carried),
        compiler_params=_cparams("arbitrary"),
    )(proj, cos2, sin2, w2p, gb, rnw, gnw, *carried)


def _mixer_bwd(proj, ocat, dmrg, sr_all, sg_all, cos2, sin2, w2p, gb, rnw, gnw, name, carried=()):
    last_step = N_STEPS - 1
    n_carried = len(carried)

    def body(*refs):
        p_ref, ocat_ref, dm_ref, sr_ref, sg_ref, c_ref, s_ref, w2_ref, gb_ref, rnw_ref, gnw_ref = refs[:11]
        g_refs, refs = refs[11:11 + n_carried], refs[11 + n_carried:]
        dp_ref, dw2_ref, dgb_ref, drn_ref, dgn_ref = refs[:5]
        got_refs, refs = refs[5:5 + n_carried], refs[5 + n_carried:]
        dsr, dsg, p_scr, dq_scr, dk_scr = refs[:5]
        step = pl.program_id(0)
        n = last_step - step
        if n_carried:
            start, finish = _exchange_phases(g_refs, got_refs, *refs[5:])
            pl.when(step == 0)(start)

        @pl.when(step == 0)
        def _():
            dsr[...] = jnp.zeros_like(dsr)
            dsg[...] = jnp.zeros_like(dsg)
            dw2_ref[...] = jnp.zeros_like(dw2_ref)
            dgb_ref[...] = jnp.zeros_like(dgb_ref)
            drn_ref[...] = jnp.zeros_like(drn_ref)
            dgn_ref[...] = jnp.zeros_like(dgn_ref)

        cosv, sinv = c_ref[...], s_ref[...]
        step_row = lax.broadcasted_iota(jnp.int32, (STEP_ROWS, 1), 0)
        real = ((n * STEP_ROWS + step_row) >= PAD_ROWS).astype(F32)

        for h in range(RET_HEADS):
            dmat, zeta, xi, gc = _ret_consts(h)
            hs = slice(128 * h, 128 * (h + 1))
            q = _rope(p_ref[:, O_RQ + 128 * h:O_RQ + 128 * (h + 1)], cosv, sinv)
            k = _rope(p_ref[:, O_RK + 128 * h:O_RK + 128 * (h + 1)], cosv, sinv) * (RET_DK ** -0.5)
            v = p_ref[:, O_RV + 128 * h:O_RV + 128 * (h + 1)]
            g = p_ref[:, O_RG + 128 * h:O_RG + 128 * (h + 1)]
            o = ocat_ref[:, hs]
            dy = dm_ref[:, hs]
            wv = rnw_ref[:, hs]
            mu = jnp.mean(o, axis=-1, keepdims=True)
            xc = o - mu
            rs = lax.rsqrt(jnp.mean(xc * xc, axis=-1, keepdims=True) + EPS)
            nrm = xc * rs
            sgm = _sigmoid(g)
            sil = g * sgm
            drn_ref[0:1, hs] += jnp.sum(dy * nrm * sil, axis=0, keepdims=True)
            dgate = dy * nrm * wv * (sgm * (1.0 + g * (1.0 - sgm)))
            dn = dy * wv * sil
            do = rs * (dn - jnp.mean(dn, axis=-1, keepdims=True) - nrm * jnp.mean(dn * nrm, axis=-1, keepdims=True))
            s_in = sr_ref[0, h]
            ds_out = dsr[h]
            a = _dot_nt(q, k) * dmat
            da = _dot_nt(do, v) * dmat
            dox = do * xi
            dq = _dot(da, k) + _dot_nt(dox, s_in)
            dk = _dot_tn(da, q) + _dot_nt(v, ds_out) * zeta
            dv = _dot_tn(a, do) + _dot(k * zeta, ds_out)
            dsr[h] = gc * ds_out + _dot_tn(q, dox)
            dk = dk * (RET_DK ** -0.5)
            dp_ref[:, O_RQ + 128 * h:O_RQ + 128 * (h + 1)] = (_unrope(dq, cosv, sinv) * real).astype(BF16)
            dp_ref[:, O_RK + 128 * h:O_RK + 128 * (h + 1)] = (_unrope(dk, cosv, sinv) * real).astype(BF16)
            dp_ref[:, O_RV + 128 * h:O_RV + 128 * (h + 1)] = (dv * real).astype(BF16)
            dp_ref[:, O_RG + 128 * h:O_RG + 128 * (h + 1)] = (dgate * real).astype(BF16)

        row_slices = [slice(CHUNK * j, CHUNK * (j + 1)) for j in range(CHUNKS_PER_STEP)]
        masks = _gla_masks()
        chunks = [_gla_common(p_ref, w2_ref, gb_ref, n * CHUNKS_PER_STEP + j, rows, masks)
                  for j, rows in enumerate(row_slices)]

        def gla_chunks(factored):
            for j in reversed(range(CHUNKS_PER_STEP)):
                gla_chunk_bwd(chunks[j], n * CHUNKS_PER_STEP + j, row_slices[j], j, factored, p_ref, ocat_ref, dm_ref,
                              sg_ref, w2_ref, gnw_ref, dp_ref, dw2_ref, dgb_ref, dgn_ref, dsg, p_scr, dq_scr, dk_scr)

        _either_form(chunks, gla_chunks)
        if n_carried:
            pl.when(step == last_step)(finish)

    def gla_chunk_bwd(c, chunk, rows, j, factored, p_ref, ocat_ref, dm_ref, sg_ref, w2_ref, gnw_ref,
                      dp_ref, dw2_ref, dgb_ref, dgn_ref, dsg, p_scr, dq_scr, dk_scr):
        row = lax.broadcasted_iota(jnp.int32, (CHUNK, 1), 0)
        real = ((chunk * CHUNK + row) >= PAD_ROWS).astype(F32)
        ri, ci = c["ri"], c["ci"]
        causal = ri >= ci
        triu = (ci >= ri).astype(F32)
        qe = c["qs"] * c["ecum"]
        kl = c["k"] * c["ekl"]
        v_all = p_ref[rows, O_GV:O_GV + GLA_HEADS * GLA_DV]
        dos, dps = [], []
        for h in range(GLA_HEADS):
            hs = slice(512 + 128 * h, 512 + 128 * (h + 1))
            v = v_all[:, GLA_DV * h:GLA_DV * (h + 1)]
            g = p_ref[rows, O_GR + 128 * h:O_GR + 128 * (h + 1)]
            o = ocat_ref[rows, hs]
            dy = dm_ref[rows, hs]
            wv = gnw_ref[:, 128 * h:128 * (h + 1)]
            rs = lax.rsqrt(jnp.mean(o * o, axis=-1, keepdims=True) + EPS)
            nrm = o * rs
            sgm = _sigmoid(g)
            sil = g * sgm
            dgn_ref[0:1, 128 * h:128 * (h + 1)] += jnp.sum(dy * nrm * sil, axis=0, keepdims=True)
            dgate = dy * nrm * wv * (sgm * (1.0 + g * (1.0 - sgm)))
            dn = dy * wv * sil
            do = rs * (dn - nrm * jnp.mean(dn * nrm, axis=-1, keepdims=True))
            dp_ref[rows, O_GR + 128 * h:O_GR + 128 * (h + 1)] = (dgate * real).astype(BF16)
            dos.append(do)
        do_all = jnp.concatenate(dos, axis=1)
        do_blocks = jnp.where(c["masks"]["own"], jnp.concatenate([do_all] * GLA_HEADS, axis=0), 0.0)
        dp_all = _dot_nt(do_blocks, v_all)
        dps = [jnp.where(causal, dp_all[CHUNK * h:CHUNK * (h + 1), :], 0.0) for h in range(GLA_HEADS)]
        _gla_all_intra_bwd(c, dps, p_scr.at[j], dq_scr.at[j], dk_scr.at[j], factored)
        s_in = _block_diagonal([sg_ref[j, h] for h in range(GLA_HEADS)])
        ds_out = dsg[...]
        decay = jnp.exp(_dot_tn_exact_lhs(c["la"], jnp.ones((CHUNK, GLA_HEADS * GLA_DV), F32)))
        dv_state = _dot(kl, ds_out)
        dqe = _dot_nt(do_all, s_in)
        dkl = _dot_nt(v_all, ds_out)
        dsg[...] = jnp.where(c["masks"]["own"], _dot_tn(qe, do_all), 0.0) + decay * ds_out
        sd = s_in * ds_out
        sd_hi = sd.astype(BF16)
        sd_lo = (sd - sd_hi.astype(F32)).astype(BF16)
        ones8 = jnp.ones((8, GLA_HEADS * GLA_DV), BF16)
        nt = (((1,), (1,)), ((), ()))
        d_el = (lax.dot_general(ones8, sd_hi, nt, preferred_element_type=F32)
                + lax.dot_general(ones8, sd_lo, nt, preferred_element_type=F32))[0:1, :]
        dqs = dqe * c["ecum"] + dq_scr[j]
        dkk = dkl * c["ekl"] + dk_scr[j]
        d_last = jnp.sum(dkl * kl, axis=0, keepdims=True) + d_el * c["el"]
        dcum = c["qs"] * dqs - c["k"] * dkk + jnp.where(row == CHUNK - 1, d_last, 0.0)
        dla = _dot_exact_rhs(triu, dcum)
        dv = _dot_tn(p_scr[j].reshape(GLA_HEADS * CHUNK, CHUNK), do_blocks) + dv_state
        dp_ref[rows, O_GV:O_GV + GLA_HEADS * GLA_DV] = (dv * real).astype(BF16)
        dp_ref[rows, O_GQ:O_GQ + 256] = (dqs * (GLA_DK ** -0.5) * real).astype(BF16)
        dp_ref[rows, O_GK:O_GK + 256] = (dkk * real).astype(BF16)
        dz = dla * (1.0 / GLA_TAU) * _sigmoid(-c["z"]) * real
        ga = p_ref[rows, O_GA:O_GA + 128]
        dp_ref[rows, O_GA:O_GA + 128] = _dot_nt(dz, w2_ref[...]).astype(BF16)
        dp_ref[rows, O_GA + 128:IN_WP] = jnp.zeros((CHUNK, IN_WP - O_GA - 128), BF16)
        dw2_ref[...] += _dot_tn(ga, dz)
        dgb_ref[0:1, :] += jnp.sum(dz, axis=0, keepdims=True)

    const = lambda shape: pl.BlockSpec(shape, lambda s: (0,) * len(shape))
    rev = lambda s: (last_step - s, 0)
    anywhere = [pl.BlockSpec(memory_space=pl.ANY)] * n_carried
    return pl.pallas_call(
        body, name=name, grid=(N_STEPS,),
        in_specs=[pl.BlockSpec((STEP_ROWS, IN_WP), rev), pl.BlockSpec((STEP_ROWS, D), rev),
                  pl.BlockSpec((STEP_ROWS, D), rev),
                  pl.BlockSpec((1, RET_HEADS, RET_DK, 128), lambda s: (last_step - s, 0, 0, 0)),
                  pl.BlockSpec((CHUNKS_PER_STEP, GLA_HEADS, GLA_DK, GLA_DV), lambda s: (last_step - s, 0, 0, 0)),
                  pl.BlockSpec((STEP_ROWS, 128), rev), pl.BlockSpec((STEP_ROWS, 128), rev),
                  const((128, 256)), const((1, 256)), const((1, 512)), const((1, 512))] + anywhere,
        out_specs=[pl.BlockSpec((STEP_ROWS, IN_WP), rev), const((128, 256)), const((8, 256)),
                   const((8, 512)), const((8, 512))] + anywhere,
        out_shape=[jax.ShapeDtypeStruct((LP, IN_WP), BF16), jax.ShapeDtypeStruct((128, 256), F32),
                   jax.ShapeDtypeStruct((8, 256), F32), jax.ShapeDtypeStruct((8, 512), F32),
                   jax.ShapeDtypeStruct((8, 512), F32)] + [jax.ShapeDtypeStruct(g.shape, g.dtype) for g in carried],
        scratch_shapes=[pltpu.VMEM((RET_HEADS, RET_DK, 128), F32),
                        pltpu.VMEM((GLA_HEADS * GLA_DK, GLA_HEADS * GLA_DV), F32),
                        pltpu.VMEM((CHUNKS_PER_STEP, GLA_HEADS, CHUNK, CHUNK), F32),
                        pltpu.VMEM((CHUNKS_PER_STEP, CHUNK, 256), F32),
                        pltpu.VMEM((CHUNKS_PER_STEP, CHUNK, 256), F32)] + _exchange_sems(n_carried),
        compiler_params=_cparams("arbitrary"),
    )(proj, ocat, dmrg, sr_all, sg_all, cos2, sin2, w2p, gb, rnw, gnw, *carried)


def _all_gather(xs, name):
    n = len(xs)

    def body(*refs):
        start, forward, finish = _gather_phases(refs[:n], refs[n:2 * n], *refs[2 * n:])
        start()
        forward()
        finish()

    return pl.pallas_call(
        body, name=name,
        in_specs=[pl.BlockSpec(memory_space=pl.ANY)] * n,
        out_specs=[pl.BlockSpec(memory_space=pl.ANY)] * n,
        out_shape=_gathered_shapes(xs),
        scratch_shapes=_exchange_sems(n),
    )(*xs)


def _gathered_shapes(xs):
    return [jax.ShapeDtypeStruct((N_DEV,) + x.shape, x.dtype) for x in xs]


def _exchange_sems(n):
    if n == 0:
        return []
    return [pltpu.SemaphoreType.DMA((7 * n,)), pltpu.SemaphoreType.DMA((7 * n,)), pltpu.SemaphoreType.DMA((n,))]


def _gather_phases(x_refs, out_refs, send_sems, recv_sems, local_sems):
    n = len(x_refs)
    mx, my, mc = lax.axis_index("x"), lax.axis_index("y"), lax.axis_index("c")
    me, sibling = (mx, my, mc), (mx, my, 1 - mc)
    chips = [(1 - mx, my), (mx, 1 - my), (1 - mx, 1 - my)]

    def slot(a, px, py, pc):
        return out_refs[a].at[4 * px + 2 * py + pc]

    def copy(a, k, block, to, src=None):
        return pltpu.make_async_remote_copy(
            src_ref=slot(a, *block) if src is None else src, dst_ref=slot(a, *block),
            send_sem=send_sems.at[7 * a + k], recv_sem=recv_sems.at[7 * a + k],
            device_id=to, device_id_type=MESH_IDS)

    mine = [pltpu.make_async_copy(x_refs[a], slot(a, *me), local_sems.at[a]) for a in range(n)]
    first = []
    for a in range(n):
        first.append(copy(a, 0, me, sibling, src=x_refs[a]))
        first += [copy(a, 1 + j, me, (*chip, mc), src=x_refs[a]) for j, chip in enumerate(chips)]
    passed = [copy(a, 4 + j, (*chip, mc), sibling) for j, chip in enumerate(chips) for a in range(n)]

    def start():
        for cp in mine + first:
            cp.start()

    def forward():
        for j, chip in enumerate(chips):
            for a in range(n):
                copy(a, 1 + j, (*chip, mc), me).wait_recv()
                passed[j * n + a].start()

    def finish():
        for a in range(n):
            copy(a, 0, sibling, me).wait_recv()
            for j, chip in enumerate(chips):
                copy(a, 4 + j, (*chip, 1 - mc), me).wait_recv()
        for cp in first + passed:
            cp.wait_send()
        for cp in mine:
            cp.wait()

    return start, forward, finish


def _exchange_blocks(gs, name):
    n = len(gs)

    def body(*refs):
        start, finish = _exchange_phases(refs[:n], refs[n:2 * n], *refs[2 * n:])
        start()
        finish()

    return pl.pallas_call(
        body, name=name,
        in_specs=[pl.BlockSpec(memory_space=pl.ANY)] * n,
        out_specs=[pl.BlockSpec(memory_space=pl.ANY)] * n,
        out_shape=[jax.ShapeDtypeStruct(g.shape, g.dtype) for g in gs],
        scratch_shapes=_exchange_sems(n),
    )(*gs)


def _exchange_phases(g_refs, out_refs, send_sems, recv_sems, local_sems):
    n = len(g_refs)
    mx, my, mc = lax.axis_index("x"), lax.axis_index("y"), lax.axis_index("c")
    me = 4 * mx + 2 * my + mc
    mine = [pltpu.make_async_copy(g_refs[a].at[me], out_refs[a].at[me], local_sems.at[a]) for a in range(n)]
    copies = []
    for r in range(1, N_DEV):
        px, py, pc = mx ^ (r >> 2), my ^ ((r >> 1) & 1), mc ^ (r & 1)
        peer = 4 * px + 2 * py + pc
        for a in range(n):
            copies.append(pltpu.make_async_remote_copy(
                src_ref=g_refs[a].at[peer], dst_ref=out_refs[a].at[me],
                send_sem=send_sems.at[7 * a + r - 1], recv_sem=recv_sems.at[7 * a + r - 1],
                device_id=(px, py, pc), device_id_type=MESH_IDS))

    def start():
        for cp in mine + copies:
            cp.start()

    def finish():
        for cp in copies:
            cp.wait_recv()
        for cp in copies:
            cp.wait_send()
        for cp in mine:
            cp.wait()

    return start, finish


IN_SHARD = IN_W // N_DEV
IN_SHARD_P = 512
UP_SHARD = D_UP // N_DEV
UP_SHARD_P = 768
RELAYOUT_ROWS = 256


def _pieces_w_in():
    return [(k, 0, IN_SHARD * k, IN_SHARD) for k in range(N_DEV)]


def _pieces_ffn_up():
    pieces = []
    for k in range(N_DEV):
        n, end = UP_SHARD * k, UP_SHARD * (k + 1)
        while n < end:
            half, r = divmod(n, D_FF)
            blk, off = divmod(r, CONV_BLOCK)
            run = min(CONV_BLOCK - off, end - n)
            pieces.append((k, n - UP_SHARD * k, 2 * CONV_BLOCK * blk + CONV_BLOCK * half + off, run))
            n += run
    return pieces


def _assemble_block(load, spans, dst_block, rows):
    lo = 128 * dst_block
    lane = lax.broadcasted_iota(jnp.int32, (1, 128), 1)
    out = jnp.zeros((rows, 128), F32)
    for key, src_off, dst_off, length in spans:
        a, b = max(lo, dst_off), min(lo + 128, dst_off + length)
        s, s_end = src_off + (a - dst_off), src_off + (b - dst_off)
        d = a
        while s < s_end:
            e = min(s_end, 128 * (s // 128 + 1))
            blk = load(key, s // 128)
            shift = (d - s) % 128
            if shift:
                blk = pltpu.roll(blk, shift, 1)
            out = jnp.where((lane >= d - lo) & (lane < d - lo + (e - s)), blk, out)
            d += e - s
            s = e
    return out


def _shards_to_cols(shards, pieces, width, name):
    _, rows, _ = shards.shape
    tr = RELAYOUT_ROWS

    def body(s_ref, o_ref):
        load = lambda k, b: s_ref[k, :, 128 * b:128 * (b + 1)].astype(F32)
        for db in range(width // 128):
            o_ref[:, 128 * db:128 * (db + 1)] = _assemble_block(load, pieces, db, tr).astype(BF16)

    return pl.pallas_call(
        body, name=name, grid=(rows // tr,),
        in_specs=[pl.BlockSpec((N_DEV, tr, shards.shape[2]), lambda i: (0, i, 0))],
        out_specs=pl.BlockSpec((tr, width), lambda i: (i, 0)),
        out_shape=jax.ShapeDtypeStruct((rows, width), BF16),
        compiler_params=_cparams("parallel"),
    )(shards)


def _cols_to_shards(full, pieces, shard_width, name):
    rows, width = full.shape
    tr = RELAYOUT_ROWS

    def body(f_ref, o_ref):
        load = lambda _, b: f_ref[:, 128 * b:128 * (b + 1)].astype(F32)
        for k in range(N_DEV):
            spans = [(None, dst_off, src_off, length) for dev, src_off, dst_off, length in pieces if dev == k]
            for db in range(shard_width // 128):
                o_ref[k, :, 128 * db:128 * (db + 1)] = _assemble_block(load, spans, db, tr).astype(BF16)

    return pl.pallas_call(
        body, name=name, grid=(rows // tr,),
        in_specs=[pl.BlockSpec((tr, width), lambda i: (i, 0))],
        out_specs=pl.BlockSpec((N_DEV, tr, shard_width), lambda i: (0, i, 0)),
        out_shape=jax.ShapeDtypeStruct((N_DEV, rows, shard_width), BF16),
        compiler_params=_cparams("parallel"),
    )(full)


def _adamw(parts, w, m, v, rows_per_step, name):
    rows, cols = w.shape
    assert rows % rows_per_step == 0 and parts.shape == (N_DEV, rows, cols)

    def body(p_ref, w_ref, m_ref, v_ref, g_ref, d_ref, nm_ref, nv_ref):
        g = p_ref[0].astype(F32)
        for j in range(1, N_DEV):
            g = g + p_ref[j].astype(F32)
        m_new = ADAM_B1 * m_ref[...] + (1.0 - ADAM_B1) * g
        v_new = ADAM_B2 * v_ref[...] + (1.0 - ADAM_B2) * (g * g)
        m_hat = m_new / (1.0 - ADAM_B1 ** ADAM_STEP)
        v_hat = v_new / (1.0 - ADAM_B2 ** ADAM_STEP)
        g_ref[...] = g
        d_ref[...] = -ADAM_LR * (m_hat / (jnp.sqrt(v_hat) + ADAM_EPS) + ADAM_WD * w_ref[...])
        nm_ref[...] = m_new
        nv_ref[...] = v_new

    tile = pl.BlockSpec((rows_per_step, cols), lambda i: (i, 0))
    shape = jax.ShapeDtypeStruct((rows, cols), F32)
    return pl.pallas_call(
        body, name=name, grid=(rows // rows_per_step,),
        in_specs=[pl.BlockSpec((N_DEV, rows_per_step, cols), lambda i: (0, i, 0)), tile, tile, tile],
        out_specs=[tile, tile, tile, tile],
        out_shape=[shape, shape, shape, shape],
        compiler_params=_cparams("parallel"),
    )(parts, w, m, v)


BIG = (("w_in", (DEPTH, D, IN_W // N_DEV), 2), ("w_out", (DEPTH, D // N_DEV, D), 1),
       ("ffn_up", (DEPTH, D, D_UP // N_DEV), 2), ("ffn_down", (DEPTH, D_FF // N_DEV, D), 1))
SMALL = (("meta_tokens", (N_META, D // N_DEV), 1), ("gla_gate_w2", (DEPTH, GATE_RANK, 256 // N_DEV), 2),
         ("ffn_conv_w", (DEPTH, 3, D_UP // N_DEV), 2))
REPL = (("pre_mix_norm", (DEPTH, D)), ("gla_gate_b", (DEPTH, 256)), ("ret_norm_w", (DEPTH, 512)),
        ("gla_norm_w", (DEPTH, 512)), ("post_mix_norm", (DEPTH, D)), ("pre_ffn_norm", (DEPTH, D)),
        ("ffn_conv_b", (DEPTH, D_UP)), ("post_ffn_norm", (DEPTH, D)))
WEIGHT_ORDER = ("meta_tokens", "pre_mix_norm", "w_in", "gla_gate_w2", "gla_gate_b", "ret_norm_w", "gla_norm_w",
                "w_out", "post_mix_norm", "pre_ffn_norm", "ffn_up", "ffn_conv_w", "ffn_conv_b", "ffn_down",
                "post_ffn_norm")


def _size(shape):
    return math.prod(shape)


def _round_up(n, mult):
    return -(-n // mult) * mult


REPL_ROWS = _round_up(-(-sum(_size(s) for _, s in REPL) // LANES), 8)
SMALL_ROWS = _round_up(-(-sum(_size(s) for _, s, _ in SMALL) // LANES), 8)


def _pack(arrays, rows, dtype):
    flat = jnp.concatenate([a.reshape(-1).astype(dtype) for a in arrays])
    return jnp.pad(flat, (0, rows * LANES - flat.shape[0])).reshape(rows, LANES)


def _unpack(buf, shapes):
    flat = buf.reshape(-1)
    out, off = [], 0
    for shape in shapes:
        out.append(flat[off:off + _size(shape)].reshape(shape))
        off += _size(shape)
    return out


def _unshard(blocks, axis):
    moved = jnp.moveaxis(blocks, 0, axis)
    shape = list(moved.shape)
    shape[axis:axis + 2] = [shape[axis] * shape[axis + 1]]
    return moved.reshape(shape)


def _to_blocks(full, axis):
    shape = list(full.shape)
    shape[axis:axis + 1] = [N_DEV, shape[axis] // N_DEV]
    return jnp.moveaxis(full.reshape(shape), axis, 0)


def _interleave_cols(w):
    lead = w.shape[:-1]
    return jnp.swapaxes(w.reshape(lead + (2, N_CONV_BLOCKS, CONV_BLOCK)), -3, -2).reshape(lead + (D_UP,))


def _deinterleave_cols(w):
    lead = w.shape[:-1]
    return jnp.swapaxes(w.reshape(lead + (N_CONV_BLOCKS, 2, CONV_BLOCK)), -3, -2).reshape(lead + (D_UP,))


def _rope_tables():
    half = RET_DK // 2
    inv = ROPE_BASE ** (-jnp.arange(half, dtype=F32) / half)
    pos = jnp.arange(LP, dtype=F32) - float(PAD_ROWS)
    ang = pos[:, None] * inv[None, :]
    c, s = jnp.cos(ang), jnp.sin(ang)
    return jnp.concatenate([c, c], axis=1), jnp.concatenate([-s, s], axis=1)


def kernel(x, meta_tokens, pre_mix_norm, w_in, gla_gate_w2, gla_gate_b, ret_norm_w, gla_norm_w, w_out, post_mix_norm, pre_ffn_norm, ffn_up, ffn_conv_w, ffn_conv_b, ffn_down, post_ffn_norm, loss_target, m_meta_tokens, m_pre_mix_norm, m_w_in, m_gla_gate_w2, m_gla_gate_b, m_ret_norm_w, m_gla_norm_w, m_w_out, m_post_mix_norm, m_pre_ffn_norm, m_ffn_up, m_ffn_conv_w, m_ffn_conv_b, m_ffn_down, m_post_ffn_norm, v_meta_tokens, v_pre_mix_norm, v_w_in, v_gla_gate_w2, v_gla_gate_b, v_ret_norm_w, v_gla_norm_w, v_w_out, v_post_mix_norm, v_pre_ffn_norm, v_ffn_up, v_ffn_conv_w, v_ffn_conv_b, v_ffn_down, v_post_ffn_norm):
    weights = dict(meta_tokens=meta_tokens, pre_mix_norm=pre_mix_norm, w_in=w_in, gla_gate_w2=gla_gate_w2,
                   gla_gate_b=gla_gate_b, ret_norm_w=ret_norm_w, gla_norm_w=gla_norm_w, w_out=w_out,
                   post_mix_norm=post_mix_norm, pre_ffn_norm=pre_ffn_norm, ffn_up=ffn_up, ffn_conv_w=ffn_conv_w,
                   ffn_conv_b=ffn_conv_b, ffn_down=ffn_down, post_ffn_norm=post_ffn_norm)
    mom1 = dict(meta_tokens=m_meta_tokens, pre_mix_norm=m_pre_mix_norm, w_in=m_w_in, gla_gate_w2=m_gla_gate_w2,
                gla_gate_b=m_gla_gate_b, ret_norm_w=m_ret_norm_w, gla_norm_w=m_gla_norm_w, w_out=m_w_out,
                post_mix_norm=m_post_mix_norm, pre_ffn_norm=m_pre_ffn_norm, ffn_up=m_ffn_up,
                ffn_conv_w=m_ffn_conv_w, ffn_conv_b=m_ffn_conv_b, ffn_down=m_ffn_down, post_ffn_norm=m_post_ffn_norm)
    mom2 = dict(meta_tokens=v_meta_tokens, pre_mix_norm=v_pre_mix_norm, w_in=v_w_in, gla_gate_w2=v_gla_gate_w2,
                gla_gate_b=v_gla_gate_b, ret_norm_w=v_ret_norm_w, gla_norm_w=v_gla_norm_w, w_out=v_w_out,
                post_mix_norm=v_post_mix_norm, pre_ffn_norm=v_pre_ffn_norm, ffn_up=v_ffn_up,
                ffn_conv_w=v_ffn_conv_w, ffn_conv_b=v_ffn_conv_b, ffn_down=v_ffn_down, post_ffn_norm=v_post_ffn_norm)

    pad_cols = lambda a, width: jnp.pad(a, ((0, 0), (0, width - a.shape[1])))
    big_names = [n for n, _, _ in BIG]
    shard = {}
    for l in range(DEPTH):
        shard[l, "w_in"] = pad_cols(w_in[l].astype(BF16), IN_SHARD_P)
        shard[l, "w_out"] = w_out[l].astype(BF16)
        shard[l, "ffn_up"] = pad_cols(ffn_up[l].astype(BF16), UP_SHARD_P)
        shard[l, "ffn_down"] = ffn_down[l].astype(BF16)
    gathered = {(0, "w_in"): _all_gather([shard[0, "w_in"]], "gather_w_in_0")[0]}
    gather_in_mixer = {l: [(l, n) for n in big_names[1:]] for l in range(DEPTH)}
    gather_in_conv = {l: [(l + 1, "w_in")] for l in range(DEPTH - 1)}
    small = _all_gather([_pack([weights[n] for n, _, _ in SMALL], SMALL_ROWS, F32)], "gather_small_weights")[0]
    small_parts = _unpack_blocks(small, [s for _, s, _ in SMALL])
    full = {n: _unshard(p, ax) for (n, _, ax), p in zip(SMALL, small_parts)}
    w2p = jnp.pad(full["gla_gate_w2"], ((0, 0), (0, 128 - GATE_RANK), (0, 0)))
    cw8 = jnp.concatenate([_interleave_cols(full["ffn_conv_w"]), _interleave_cols(ffn_conv_b)[:, None, :],
                           jnp.zeros((DEPTH, 4, D_UP), F32)], axis=1)
    cos2, sin2 = _rope_tables()

    h = jnp.concatenate([jnp.zeros((PAD_ROWS, D), F32), full["meta_tokens"], x[0]], axis=0)
    target = jnp.concatenate([jnp.zeros((CHUNK, D), F32), loss_target[0]], axis=0)
    saved, layer_w = [], []
    for l in range(DEPTH):
        lw = dict(w_in=_shards_to_cols(gathered[l, "w_in"], _pieces_w_in(), IN_WP, f"w_in_cols_{l}"))
        a1 = _rmsnorm_fwd(h, pre_mix_norm[l:l + 1], f"pre_mix_norm_{l}")
        proj = _matmul(a1, lw["w_in"], out_dtype=F32, tm=TK_ROWS, tn=1280, tk=D, name=f"in_proj_{l}", n_outer=True)
        keys = gather_in_mixer.get(l, [])
        ocat, merged, sr_all, sg_all, *got = _mixer_fwd(proj, cos2, sin2, w2p[l], gla_gate_b[l:l + 1],
                                                        ret_norm_w[l:l + 1], gla_norm_w[l:l + 1], f"mixer_fwd_{l}",
                                                        carried=[shard[key] for key in keys])
        gathered.update(zip(keys, got))
        lw["w_out"] = gathered[l, "w_out"].reshape(D, D)
        lw["w_up"] = _shards_to_cols(gathered[l, "ffn_up"], _pieces_ffn_up(), D_UP, f"ffn_up_cols_{l}")
        lw["w_down"] = gathered[l, "ffn_down"].reshape(D_FF, D)
        layer_w.append(lw)
        m, h1 = _matmul_resid_norm(merged, lw["w_out"], h, post_mix_norm[l:l + 1], f"out_proj_{l}")
        a2 = _rmsnorm_fwd(h1, pre_ffn_norm[l:l + 1], f"pre_ffn_norm_{l}")
        u = _matmul(a2, lw["w_up"], out_dtype=BF16, tm=TK_ROWS, tn=1408, tk=D, name=f"ffn_up_{l}", n_outer=True)
        keys = gather_in_conv.get(l, [])
        cv, act, *got = _conv_act_fwd(u, cw8[l], f"ffn_conv_act_{l}", carried=[shard[key] for key in keys])
        gathered.update(zip(keys, got))
        f, h2, *loss_acc = _matmul_resid_norm(act, lw["w_down"], h1, post_ffn_norm[l:l + 1], f"ffn_down_{l}",
                                              target=target if l == DEPTH - 1 else None)
        saved.append(dict(h=h, a1=a1, proj=proj, ocat=ocat, merged=merged, sr=sr_all, sg=sg_all, m=m, h1=h1,
                          a2=a2, u=u, cv=cv, act=act, f=f))
        h = h2

    dh = h
    loss = lax.psum(loss_acc[0][0, 0], ("x", "y", "c"))

    kinds = ("grad", "delta", "new_m", "new_v")
    grads = {n: [None] * DEPTH for n in WEIGHT_ORDER if n != "meta_tokens" and n not in big_names}
    pending, parts = [], {}
    for l in reversed(range(DEPTH)):
        s, lw = saved[l], layer_w[l]
        dact, df, g_post_ffn = _norm_bwd_matmul(dh, s["f"], post_ffn_norm[l:l + 1], lw["w_down"], BF16,
                                                f"ffn_down_dx_{l}")
        g_down = _matmul(s["act"], df, ta=True, out_dtype=BF16, tm=D_FF // 2, tn=D, tk=TK_ROWS, name=f"ffn_down_dw_{l}")
        du, dcw = _conv_act_bwd(dact, s["cv"], s["u"], cw8[l], f"ffn_conv_act_bwd_{l}")
        dh1, g_pre_ffn = _matmul_norm_bwd(du, lw["w_up"], s["h1"], pre_ffn_norm[l:l + 1], dh, D_FF, f"ffn_up_dx_{l}")
        g_up = _matmul(s["a2"], du, ta=True, out_dtype=BF16, tm=D, tn=D_FF, tk=TK_ROWS, name=f"ffn_up_dw_{l}")
        dmerged, dm, g_post_mix = _norm_bwd_matmul(dh1, s["m"], post_mix_norm[l:l + 1], lw["w_out"], F32,
                                                   f"out_proj_dx_{l}")
        g_out = _matmul(s["merged"], dm, ta=True, out_dtype=BF16, tm=D, tn=D, tk=TK_ROWS, name=f"out_proj_dw_{l}")
        pending += [((l, "ffn_down"), g_down.reshape(N_DEV, D_FF // N_DEV, D)),
                    ((l, "ffn_up"), _cols_to_shards(g_up, _pieces_ffn_up(), UP_SHARD_P, f"ffn_up_grad_shards_{l}")),
                    ((l, "w_out"), g_out.reshape(N_DEV, D // N_DEV, D))]
        dproj, g_w2, g_gb, g_rn, g_gn, *got = _mixer_bwd(s["proj"], s["ocat"], dmerged, s["sr"], s["sg"], cos2, sin2,
                                                         w2p[l], gla_gate_b[l:l + 1], ret_norm_w[l:l + 1],
                                                         gla_norm_w[l:l + 1], f"mixer_bwd_{l}",
                                                         carried=[blocks for _, blocks in pending])
        parts.update(zip([key for key, _ in pending], got))
        g_in = _matmul(s["a1"], dproj, ta=True, out_dtype=BF16, tm=D, tn=IN_WP // 2, tk=TK_ROWS, name=f"in_proj_dw_{l}")
        pending = [((l, "w_in"), _cols_to_shards(g_in, _pieces_w_in(), IN_SHARD_P, f"w_in_grad_shards_{l}"))]
        now = pending if l == 0 else []
        dh, g_pre_mix, *got = _matmul_norm_bwd(dproj, lw["w_in"], s["h"], pre_mix_norm[l:l + 1], dh1, IN_WP,
                                               f"in_proj_dx_{l}", carried=[blocks for _, blocks in now])
        parts.update(zip([key for key, _ in now], got))
        pending = [] if l == 0 else pending
        grads["post_ffn_norm"][l] = g_post_ffn[0]
        grads["ffn_conv_w"][l] = _deinterleave_cols(dcw[0:3])
        grads["ffn_conv_b"][l] = _deinterleave_cols(dcw[3])
        grads["pre_ffn_norm"][l] = g_pre_ffn[0]
        grads["post_mix_norm"][l] = g_post_mix[0]
        grads["gla_gate_w2"][l] = g_w2[:GATE_RANK]
        grads["gla_gate_b"][l] = g_gb[0]
        grads["ret_norm_w"][l] = g_rn[0]
        grads["gla_norm_w"][l] = g_gn[0]
        grads["pre_mix_norm"][l] = g_pre_mix[0]
    local = {n: jnp.stack(v) for n, v in grads.items()}
    local["meta_tokens"] = dh[PAD_ROWS:CHUNK]
    grad_x = dh[CHUNK:][None]

    blocks = jnp.concatenate([_to_blocks(local[n], ax).reshape(N_DEV, -1) for n, _, ax in SMALL], axis=1)
    blocks = jnp.pad(blocks, ((0, 0), (0, SMALL_ROWS * LANES - blocks.shape[1]))).reshape(N_DEV, SMALL_ROWS, LANES)
    *got, small_grad_parts = _exchange_blocks([b for _, b in pending] + [blocks], "exchange_last_grads")
    parts.update(zip([key for key, _ in pending], got))

    widths = dict(w_in=IN_SHARD_P, w_out=D, ffn_up=UP_SHARD_P, ffn_down=D)
    steps = dict(w_in=256, w_out=D // N_DEV, ffn_up=256, ffn_down=D_FF // N_DEV // 2)
    big_out = {kind: {n: [None] * DEPTH for n in big_names} for kind in kinds}
    for l in range(DEPTH):
        for n in big_names:
            mine = [pad_cols(d[n][l], widths[n]) for d in (weights, mom1, mom2)]
            results = _adamw(parts[l, n], *mine, steps[n], f"adamw_{n}_{l}")
            for kind, r in zip(kinds, results):
                big_out[kind][n][l] = r[:, :weights[n].shape[2]]
    out = {kind: {n: jnp.stack(v) for n, v in big_out[kind].items()} for kind in kinds}
    shard_shapes = [s for _, s, _ in SMALL]
    packed = [_pack([d[n] for n, _, _ in SMALL], SMALL_ROWS, F32) for d in (weights, mom1, mom2)]
    results = _adamw(small_grad_parts, *packed, SMALL_ROWS, "adamw_small_sharded")
    for kind, buf in zip(kinds, results):
        out[kind].update(zip([n for n, _, _ in SMALL], _unpack(buf, shard_shapes)))

    repl_parts = _all_gather([_pack([local[n] for n, _ in REPL], REPL_ROWS, F32)], "gather_small_grads")[0]
    packed = [_pack([d[n] for n, _ in REPL], REPL_ROWS, F32) for d in (weights, mom1, mom2)]
    results = _adamw(repl_parts, *packed, REPL_ROWS, "adamw_replicated")
    repl_shapes = [s for _, s in REPL]
    for kind, buf in zip(kinds, results):
        out[kind].update(zip([n for n, _ in REPL], _unpack(buf, repl_shapes)))

    return (loss, grad_x, *[out["grad"][n] for n in WEIGHT_ORDER], *[out["delta"][n] for n in WEIGHT_ORDER],
            *[out["new_m"][n] for n in WEIGHT_ORDER], *[out["new_v"][n] for n in WEIGHT_ORDER])


def _unpack_blocks(gathered, shapes):
    flat = gathered.reshape(N_DEV, -1)
    out, off = [], 0
    for shape in shapes:
        out.append(flat[:, off:off + _size(shape)].reshape((N_DEV,) + shape))
        off += _size(shape)
    return out
```

```python
import math

import jax
import jax.numpy as jnp
from jax import lax
from jax.experimental import pallas as pl
from jax.experimental.pallas import tpu as pltpu

F32 = jnp.float32
BF16 = jnp.bfloat16

D = 1024
SEQ = 8192
DEPTH = 2
N_META = 16
CHUNK = 64
SUB = 16
N_SUB = CHUNK // SUB
PAD_ROWS = CHUNK - N_META
LP = SEQ + CHUNK
N_CHUNKS = LP // CHUNK
RET_HEADS = 4
RET_DK = 128
GLA_HEADS = 4
GLA_DK = 64
GLA_DV = 128
GLA_TAU = 16.0
GATE_RANK = 16
IN_W = 3600
IN_WP = 3840
D_FF = 2816
D_UP = 2 * D_FF
CONV_BLOCK = 256
N_CONV_BLOCKS = D_FF // CONV_BLOCK
ROPE_BASE = 10000.0
EPS = 1e-6
N_DEV = 8
LANES = 1024

O_RQ, O_RK, O_RV, O_RG = 0, 512, 1024, 1536
O_GQ, O_GK, O_GV, O_GR, O_GA = 2048, 2304, 2560, 3072, 3584

ADAM_LR = 0.001
ADAM_B1 = 0.9
ADAM_B2 = 0.999
ADAM_EPS = 1e-08
ADAM_WD = 0.01
ADAM_STEP = 10

VMEM_LIMIT = 56 * 1024 * 1024
MESH_IDS = pl.DeviceIdType.MESH


def _row_tile(rows, limit):
    best = 16
    for t in range(16, min(rows, limit) + 1, 16):
        if rows % t == 0:
            best = t
    return best


TM = _row_tile(LP, 688)
TM_BIG = _row_tile(LP, 1376)


def _cparams(*sem):
    return pltpu.CompilerParams(dimension_semantics=sem, vmem_limit_bytes=VMEM_LIMIT)


def _dot(a, b):
    return jnp.dot(a.astype(BF16), b.astype(BF16), preferred_element_type=F32)


def _dot_nt(a, b):
    return lax.dot_general(a.astype(BF16), b.astype(BF16), (((1,), (1,)), ((), ())), preferred_element_type=F32)


def _dot_tn(a, b):
    return lax.dot_general(a.astype(BF16), b.astype(BF16), (((0,), (0,)), ((), ())), preferred_element_type=F32)


def _split3(x):
    hi = x.astype(BF16)
    r1 = x - hi.astype(F32)
    mid = r1.astype(BF16)
    lo = (r1 - mid.astype(F32)).astype(BF16)
    return hi, mid, lo


def _dot_exact_rhs(t, x):
    n = x.shape[1]
    parts = jnp.dot(t.astype(BF16), jnp.concatenate(_split3(x), axis=1), preferred_element_type=F32)
    return parts[:, :n] + parts[:, n:2 * n] + parts[:, 2 * n:]


def _dot_tn_exact_lhs(x, ones):
    n = x.shape[1]
    parts = lax.dot_general(jnp.concatenate(_split3(x), axis=1), ones.astype(BF16), (((0,), (0,)), ((), ())),
                            preferred_element_type=F32)
    return parts[:n] + parts[n:2 * n] + parts[2 * n:]


def _sigmoid(x):
    return 1.0 / (1.0 + jnp.exp(-x))


def _matmul(a, b, *, ta=False, tb=False, out_dtype, tm, tn, tk, name):
    m = a.shape[1] if ta else a.shape[0]
    k = a.shape[0] if ta else a.shape[1]
    n = b.shape[0] if tb else b.shape[1]
    assert (b.shape[1] if tb else b.shape[0]) == k
    assert m % tm == 0 and n % tn == 0 and k % tk == 0, (name, m, n, k, tm, tn, tk)
    nk = k // tk
    a_spec = pl.BlockSpec((tk, tm), lambda i, j, kk: (kk, i)) if ta else pl.BlockSpec((tm, tk), lambda i, j, kk: (i, kk))
    b_spec = pl.BlockSpec((tn, tk), lambda i, j, kk: (j, kk)) if tb else pl.BlockSpec((tk, tn), lambda i, j, kk: (kk, j))
    dims = (((0 if ta else 1,), (1 if tb else 0,)), ((), ()))

    def body(a_ref, b_ref, o_ref, *acc):
        prod = lax.dot_general(a_ref[...].astype(BF16), b_ref[...].astype(BF16), dims, preferred_element_type=F32)
        if nk == 1:
            o_ref[...] = prod.astype(out_dtype)
            return
        acc_ref, = acc
        kk = pl.program_id(2)

        @pl.when(kk == 0)
        def _():
            acc_ref[...] = prod

        @pl.when(kk > 0)
        def _():
            acc_ref[...] += prod

        @pl.when(kk == nk - 1)
        def _():
            o_ref[...] = acc_ref[...].astype(out_dtype)

    return pl.pallas_call(
        body, name=name, grid=(m // tm, n // tn, nk),
        in_specs=[a_spec, b_spec],
        out_specs=pl.BlockSpec((tm, tn), lambda i, j, kk: (i, j)),
        out_shape=jax.ShapeDtypeStruct((m, n), out_dtype),
        scratch_shapes=[pltpu.VMEM((tm, tn), F32)] if nk > 1 else [],
        compiler_params=_cparams("parallel", "parallel", "arbitrary"),
    )(a, b)


def _norm_matmul(x, w, b, *, out_dtype, tm, tn, name):
    n = b.shape[1]
    assert LP % tm == 0 and n % tn == 0

    def body(x_ref, w_ref, b_ref, a_ref, o_ref, a_scr):
        @pl.when(pl.program_id(1) == 0)
        def _():
            xv = x_ref[...]
            r = lax.rsqrt(jnp.mean(xv * xv, axis=-1, keepdims=True) + EPS)
            a = (xv * r * w_ref[...]).astype(BF16)
            a_scr[...] = a
            a_ref[...] = a

        o_ref[...] = jnp.dot(a_scr[...], b_ref[...], preferred_element_type=F32).astype(out_dtype)

    return pl.pallas_call(
        body, name=name, grid=(LP // tm, n // tn),
        in_specs=[pl.BlockSpec((tm, D), lambda i, j: (i, 0)), pl.BlockSpec((1, D), lambda i, j: (0, 0)),
                  pl.BlockSpec((D, tn), lambda i, j: (0, j))],
        out_specs=[pl.BlockSpec((tm, D), lambda i, j: (i, 0)), pl.BlockSpec((tm, tn), lambda i, j: (i, j))],
        out_shape=[jax.ShapeDtypeStruct((LP, D), BF16), jax.ShapeDtypeStruct((LP, n), out_dtype)],
        scratch_shapes=[pltpu.VMEM((tm, D), BF16)],
        compiler_params=_cparams("arbitrary", "arbitrary"),
    )(x, w, b)


def _matmul_resid_norm(a, b, h, w, name, target=None):
    k = a.shape[1]
    has_loss = target is not None

    def body(a_ref, b_ref, h_ref, w_ref, *refs):
        m = jnp.dot(a_ref[...].astype(BF16), b_ref[...].astype(BF16), preferred_element_type=F32)
        r = lax.rsqrt(jnp.mean(m * m, axis=-1, keepdims=True) + EPS)
        i = pl.program_id(0)
        row = i * TM + lax.broadcasted_iota(jnp.int32, (TM, 1), 0)
        y = h_ref[...] + jnp.where(row >= PAD_ROWS, m * r * w_ref[...], 0.0)
        if not has_loss:
            m_ref, y_ref = refs
            m_ref[...] = m
            y_ref[...] = y
            return
        t_ref, m_ref, dy_ref, loss_ref = refs
        m_ref[...] = m

        @pl.when(i == 0)
        def _():
            loss_ref[...] = jnp.zeros_like(loss_ref)

        diff = jnp.where(row >= CHUNK, y - t_ref[...], 0.0)
        dy_ref[...] = diff * (1.0 / D)
        loss_ref[...] += (0.5 / D) * jnp.sum(diff * diff)

    tile = pl.BlockSpec((TM, D), lambda i: (i, 0))
    shape = jax.ShapeDtypeStruct((LP, D), F32)
    in_specs = [pl.BlockSpec((TM, k), lambda i: (i, 0)), pl.BlockSpec((k, D), lambda i: (0, 0)), tile,
                pl.BlockSpec((1, D), lambda i: (0, 0))]
    if has_loss:
        return pl.pallas_call(
            body, name=name, grid=(LP // TM,),
            in_specs=in_specs + [tile],
            out_specs=[tile, tile, pl.BlockSpec((8, 128), lambda i: (0, 0))],
            out_shape=[shape, shape, jax.ShapeDtypeStruct((8, 128), F32)],
            compiler_params=_cparams("arbitrary"),
        )(a, b, h, w, target)
    return pl.pallas_call(
        body, name=name, grid=(LP // TM,),
        in_specs=in_specs, out_specs=[tile, tile], out_shape=[shape, shape],
        compiler_params=_cparams("parallel"),
    )(a, b, h, w)


def _rmsnorm_bwd_rows(dy, x, w):
    r = lax.rsqrt(jnp.mean(x * x, axis=-1, keepdims=True) + EPS)
    g = dy * w
    dx = r * g - x * (r * r * r * jnp.mean(g * x, axis=-1, keepdims=True))
    return dx, jnp.sum(dy * x * r, axis=0, keepdims=True)


def _matmul_norm_bwd(dz, b, x, w, resid, tk, name, carried=()):
    k = dz.shape[1]
    assert k % tk == 0
    nk = k // tk
    n_rows = LP // TM
    n_carried = len(carried)

    def body(*refs):
        a_ref, b_ref, x_ref, w_ref, r_ref = refs[:5]
        g_refs, refs = refs[5:5 + n_carried], refs[5 + n_carried:]
        dx_ref, dw_ref = refs[:2]
        got_refs, refs = refs[2:2 + n_carried], refs[2 + n_carried:]
        acc, sems = (refs[:1], refs[1:]) if nk > 1 else ((), refs)
        i, kk = pl.program_id(0), pl.program_id(1)
        if n_carried:
            exchange_start, exchange_finish = _exchange_phases(g_refs, got_refs, *sems)
            pl.when((i == 0) & (kk == 0))(exchange_start)

        @pl.when((i == 0) & (kk == 0))
        def _():
            dw_ref[...] = jnp.zeros_like(dw_ref)

        prod = lax.dot_general(a_ref[...].astype(BF16), b_ref[...].astype(BF16), (((1,), (1,)), ((), ())),
                               preferred_element_type=F32)

        def finish(dy):
            dx, dw = _rmsnorm_bwd_rows(dy, x_ref[...], w_ref[...])
            dx_ref[...] = dx + r_ref[...]
            dw_ref[0:1, :] += dw

        if nk == 1:
            finish(prod)
        else:
            acc_ref, = acc

            @pl.when(kk == 0)
            def _():
                acc_ref[...] = prod

            @pl.when((kk > 0) & (kk < nk - 1))
            def _():
                acc_ref[...] += prod

            @pl.when(kk == nk - 1)
            def _():
                finish(acc_ref[...] + prod)

        if n_carried:
            pl.when((i == n_rows - 1) & (kk == nk - 1))(exchange_finish)

    tile = pl.BlockSpec((TM, D), lambda i, kk: (i, 0))
    anywhere = [pl.BlockSpec(memory_space=pl.ANY)] * n_carried
    return pl.pallas_call(
        body, name=name, grid=(n_rows, nk),
        in_specs=[pl.BlockSpec((TM, tk), lambda i, kk: (i, kk)), pl.BlockSpec((D, tk), lambda i, kk: (0, kk)), tile,
                  pl.BlockSpec((1, D), lambda i, kk: (0, 0)), tile] + anywhere,
        out_specs=[tile, pl.BlockSpec((8, D), lambda i, kk: (0, 0))] + anywhere,
        out_shape=[jax.ShapeDtypeStruct((LP, D), F32), jax.ShapeDtypeStruct((8, D), F32)]
        + [jax.ShapeDtypeStruct(g.shape, g.dtype) for g in carried],
        scratch_shapes=([pltpu.VMEM((TM, D), F32)] if nk > 1 else []) + _exchange_sems(n_carried),
        compiler_params=_cparams("arbitrary", "arbitrary"),
    )(dz, b, x, w, resid, *carried)


def _norm_bwd_matmul(dh, x, w, b, out_dtype, name):
    n = b.shape[0]

    def body(dh_ref, x_ref, w_ref, b_ref, o_ref, dx_ref, dw_ref):
        i = pl.program_id(0)

        @pl.when(i == 0)
        def _():
            dw_ref[...] = jnp.zeros_like(dw_ref)

        row = i * TM + lax.broadcasted_iota(jnp.int32, (TM, 1), 0)
        dy = jnp.where(row >= PAD_ROWS, dh_ref[...], 0.0)
        dx, dw = _rmsnorm_bwd_rows(dy, x_ref[...], w_ref[...])
        dxb = dx.astype(BF16)
        dx_ref[...] = dxb
        dw_ref[0:1, :] += dw
        o_ref[...] = lax.dot_general(dxb, b_ref[...].astype(BF16), (((1,), (1,)), ((), ())),
                                     preferred_element_type=F32).astype(out_dtype)

    tile = pl.BlockSpec((TM, D), lambda i: (i, 0))
    return pl.pallas_call(
        body, name=name, grid=(LP // TM,),
        in_specs=[tile, tile, pl.BlockSpec((1, D), lambda i: (0, 0)), pl.BlockSpec((n, D), lambda i: (0, 0))],
        out_specs=[pl.BlockSpec((TM, n), lambda i: (i, 0)), tile, pl.BlockSpec((8, D), lambda i: (0, 0))],
        out_shape=[jax.ShapeDtypeStruct((LP, n), out_dtype), jax.ShapeDtypeStruct((LP, D), BF16),
                   jax.ShapeDtypeStruct((8, D), F32)],
        compiler_params=_cparams("arbitrary"),
    )(dh, x, w, b)


GELU_C = math.sqrt(2.0 / math.pi)
GELU_K = 0.044715
STRIP = 16
HALF = 8


def _gelu_half(a):
    return 0.5 * jnp.tanh(a * (a * a * (GELU_C * GELU_K) + GELU_C)) + 0.5


def _gelu_slope(a, h):
    return h * (1.0 + (a - a * h) * (a * a * (6.0 * GELU_C * GELU_K) + 2.0 * GELU_C))


def _shift_down(x, prev8, rows):
    row = lax.broadcasted_iota(jnp.int32, (rows, 1), 0)
    p1 = pltpu.roll(prev8, 1, 0)
    p2 = pltpu.roll(prev8, 2, 0)
    x1 = jnp.where(row == 0, p1[0:1, :], pltpu.roll(x, 1, 0))
    x2 = jnp.where(row == 0, p2[0:1, :], jnp.where(row == 1, p2[1:2, :], pltpu.roll(x, 2, 0)))
    return x1, x2


def _conv_act_fwd(u, cw8, name, carried=()):
    n_rows = LP // TM
    cb2 = 2 * CONV_BLOCK
    n_carried = len(carried)

    def body(*refs):
        u_ref, cw_ref = refs[:2]
        x_refs, refs = refs[2:2 + n_carried], refs[2 + n_carried:]
        conv_ref, act_ref = refs[:2]
        gathered_refs, refs = refs[2:2 + n_carried], refs[2 + n_carried:]
        carry_ref = refs[0]
        j, i = pl.program_id(0), pl.program_id(1)
        if n_carried:
            start, forward, finish = _gather_phases(x_refs, gathered_refs, *refs[1:])
            pl.when((j == 0) & (i == 0))(start)
            pl.when((j == (3 * N_CONV_BLOCKS) // 4) & (i == 0))(forward)

        @pl.when(i == 0)
        def _():
            carry_ref[...] = jnp.zeros_like(carry_ref)

        x = u_ref[...].astype(F32)
        x1, x2 = _shift_down(x, carry_ref[...], TM)
        conv = cw_ref[3:4, :] + x2 * cw_ref[0:1, :] + x1 * cw_ref[1:2, :] + x * cw_ref[2:3, :]
        conv_ref[...] = conv.astype(BF16)
        a = conv[:, :CONV_BLOCK]
        g = conv[:, CONV_BLOCK:]
        act_ref[...] = (a * _gelu_half(a) * g).astype(BF16)
        carry_ref[...] = x[TM - 8:TM, :]
        if n_carried:
            pl.when((j == N_CONV_BLOCKS - 1) & (i == n_rows - 1))(finish)

    anywhere = [pl.BlockSpec(memory_space=pl.ANY)] * n_carried
    return pl.pallas_call(
        body, name=name, grid=(N_CONV_BLOCKS, n_rows),
        in_specs=[pl.BlockSpec((TM, cb2), lambda j, i: (i, j)), pl.BlockSpec((8, cb2), lambda j, i: (0, j))] + anywhere,
        out_specs=[pl.BlockSpec((TM, cb2), lambda j, i: (i, j)),
                   pl.BlockSpec((TM, CONV_BLOCK), lambda j, i: (i, j))] + anywhere,
        out_shape=[jax.ShapeDtypeStruct((LP, D_UP), BF16), jax.ShapeDtypeStruct((LP, D_FF), BF16)]
        + _gathered_shapes(carried),
        scratch_shapes=[pltpu.VMEM((8, cb2), F32)] + _exchange_sems(n_carried),
        compiler_params=_cparams("arbitrary", "arbitrary"),
    )(u, cw8, *carried)


def _conv_act_bwd(dact, conv, u, cw8, name):
    n_rows = LP // TM
    cb2 = 2 * CONV_BLOCK
    n_strips = TM // STRIP

    def body(dact_ref, conv_ref, u_ref, cw_ref, du_ref, dcw_ref, carry_ref):
        i = pl.program_id(1)

        @pl.when(i == 0)
        def _():
            dcw_ref[...] = jnp.zeros_like(dcw_ref)
            carry_ref[...] = jnp.zeros_like(carry_ref)

        w0, w1, w2 = cw_ref[0:1, :], cw_ref[1:2, :], cw_ref[2:3, :]
        row = lax.broadcasted_iota(jnp.int32, (HALF, 1), 0)

        def strip(k, carry):
            n1, n2, s0, s1, s2, s3 = carry
            r0 = pl.multiple_of((n_strips - 1 - k) * STRIP, STRIP)
            cv = conv_ref[pl.ds(r0, STRIP), :].astype(F32)
            dav = dact_ref[pl.ds(r0, STRIP), :].astype(F32)
            x = u_ref[pl.ds(r0, STRIP), :].astype(F32)
            du = [None, None]
            for half in (1, 0):
                rows = slice(HALF * half, HALF * (half + 1))
                a, g, dah = cv[rows, :CONV_BLOCK], cv[rows, CONV_BLOCK:], dav[rows]
                h = _gelu_half(a)
                dconv = jnp.concatenate([dah * g * _gelu_slope(a, h), dah * (a * h)], axis=1)
                u1, u2 = pltpu.roll(dconv, HALF - 1, 0), pltpu.roll(dconv, HALF - 2, 0)
                d1 = jnp.where(row >= HALF - 1, n1, u1)
                d2 = jnp.where(row >= HALF - 2, n2, u2)
                du[half] = dconv * w2 + d1 * w1 + d2 * w0
                s0, s1, s2, s3 = s0 + d2 * x[rows], s1 + d1 * x[rows], s2 + dconv * x[rows], s3 + dconv
                n1, n2 = u1, u2
            du_ref[pl.ds(r0, STRIP), :] = jnp.concatenate(du, axis=0).astype(BF16)
            return n1, n2, s0, s1, s2, s3

        below = carry_ref[...]
        zero = jnp.zeros((HALF, cb2), F32)
        init = (pltpu.roll(below, HALF - 1, 0), pltpu.roll(below, HALF - 2, 0), zero, zero, zero, zero)
        u1, _, s0, s1, s2, s3 = lax.fori_loop(0, n_strips, strip, init, unroll=2)
        carry_ref[...] = pltpu.roll(u1, 1, 0)
        dcw_ref[0:1, :] += jnp.sum(s0, axis=0, keepdims=True)
        dcw_ref[1:2, :] += jnp.sum(s1, axis=0, keepdims=True)
        dcw_ref[2:3, :] += jnp.sum(s2, axis=0, keepdims=True)
        dcw_ref[3:4, :] += jnp.sum(s3, axis=0, keepdims=True)

    rev = lambda j, i: (n_rows - 1 - i, j)
    return pl.pallas_call(
        body, name=name, grid=(N_CONV_BLOCKS, n_rows),
        in_specs=[pl.BlockSpec((TM, CONV_BLOCK), rev), pl.BlockSpec((TM, cb2), rev), pl.BlockSpec((TM, cb2), rev),
                  pl.BlockSpec((8, cb2), lambda j, i: (0, j))],
        out_specs=[pl.BlockSpec((TM, cb2), rev), pl.BlockSpec((8, cb2), lambda j, i: (0, j))],
        out_shape=[jax.ShapeDtypeStruct((LP, D_UP), BF16), jax.ShapeDtypeStruct((8, D_UP), F32)],
        scratch_shapes=[pltpu.VMEM((HALF, cb2), F32)],
        compiler_params=_cparams("arbitrary", "arbitrary"),
    )(dact, conv, u, cw8)


CHUNKS_PER_STEP = 3 if N_CHUNKS % 3 == 0 else 1
STEP_ROWS = CHUNKS_PER_STEP * CHUNK
N_STEPS = N_CHUNKS // CHUNKS_PER_STEP


def _ret_consts(h):
    rows = STEP_ROWS
    lg = math.log(1.0 - 2.0 ** (-5.0 - h))
    ri = lax.broadcasted_iota(jnp.int32, (rows, rows), 0)
    ci = lax.broadcasted_iota(jnp.int32, (rows, rows), 1)
    diff = (ri - ci).astype(F32)
    dmat = jnp.where(diff >= 0, jnp.exp(lg * jnp.maximum(diff, 0.0)), 0.0)
    rowf = lax.broadcasted_iota(jnp.int32, (rows, 1), 0).astype(F32)
    zeta = jnp.exp(lg * (rows - 1.0 - rowf))
    xi = jnp.exp(lg * (rowf + 1.0))
    return dmat, zeta, xi, math.exp(lg * rows)


def _rope(t, cosv, sinv):
    return t * cosv + pltpu.roll(t, RET_DK // 2, 1) * sinv


def _unrope(d, cosv, sinv):
    return d * cosv + pltpu.roll(d * sinv, RET_DK // 2, 1)


def _gla_masks():
    ri = lax.broadcasted_iota(jnp.int32, (CHUNK, CHUNK), 0)
    ci = lax.broadcasted_iota(jnp.int32, (CHUNK, CHUNK), 1)
    return dict(ri=ri, ci=ci, tril=(ri >= ci).astype(F32), heads=_head_block_mask(), own=_state_block_mask())


def _gla_common(p_ref, w2_ref, gb_ref, chunk, rows, masks):
    row = lax.broadcasted_iota(jnp.int32, (CHUNK, 1), 0)
    real = (chunk * CHUNK + row) >= PAD_ROWS
    ga = p_ref[rows, O_GA:O_GA + 128]
    z = _dot(ga, w2_ref[...]) + gb_ref[...]
    la = (jnp.minimum(z, 0.0) - jnp.log(1.0 + jnp.exp(-jnp.abs(z)))) * (1.0 / GLA_TAU)
    la = jnp.where(real, la, 0.0)
    ri, ci = masks["ri"], masks["ci"]
    cum = _dot_exact_rhs(masks["tril"], la)
    last = cum[CHUNK - 1:CHUNK, :]
    qs = p_ref[rows, O_GQ:O_GQ + 256] * (GLA_DK ** -0.5)
    k = p_ref[rows, O_GK:O_GK + 256]
    ecum = jnp.exp(cum)
    ekl = jnp.exp(last - cum)
    el = jnp.exp(last)
    refs = [jnp.zeros((1, 256), F32)] + [cum[a * SUB - 1:a * SUB, :] for a in range(1, N_SUB)]
    eq = [jnp.exp(cum[a * SUB:(a + 1) * SUB, :] - refs[a]) for a in range(N_SUB)]
    spread = refs[0] - cum[SUB - 1:SUB, :]
    for a in range(1, N_SUB):
        spread = jnp.maximum(spread, refs[a] - cum[(a + 1) * SUB - 1:(a + 1) * SUB, :])
    small = jnp.max(spread) <= GLA_FACTORED_MAX
    return dict(real=real, row=row, z=z, la=la, cum=cum, last=last, qs=qs, k=k, ecum=ecum, ekl=ekl, el=el,
                refs=refs, eq=eq, small=small, ri=ri, ci=ci, masks=masks)


GLA_FACTORED_MAX = 40.0


def _head_block_mask():
    r = lax.broadcasted_iota(jnp.int32, (CHUNK, 256), 0)
    col = lax.broadcasted_iota(jnp.int32, (CHUNK, 256), 1)
    return (r // SUB) == (col // GLA_DK)


def _state_block_mask():
    r = lax.broadcasted_iota(jnp.int32, (GLA_HEADS * GLA_DK, GLA_HEADS * GLA_DV), 0)
    col = lax.broadcasted_iota(jnp.int32, (GLA_HEADS * GLA_DK, GLA_HEADS * GLA_DV), 1)
    return (r // GLA_DK) == (col // GLA_DV)


def _block_diagonal(blocks):
    zero = jnp.zeros((GLA_DK, GLA_DV), F32)
    return jnp.concatenate([jnp.concatenate([blocks[h] if g == h else zero for g in range(GLA_HEADS)], axis=1)
                            for h in range(GLA_HEADS)], axis=0)


def _gla_factored(c):
    mask = c["masks"]["heads"]
    eks, keys, queries = [], [], []
    for a in range(N_SUB):
        ek = jnp.exp(jnp.minimum(c["refs"][a] - c["cum"], GLA_FACTORED_MAX))
        qh = c["qs"][a * SUB:(a + 1) * SUB, :] * c["eq"][a]
        eks.append(ek)
        keys.append(c["k"] * ek)
        queries.append(jnp.where(mask, jnp.concatenate([qh] * GLA_HEADS, axis=0), 0.0))
    return eks, keys, queries


def _gla_scores_factored(c, factored, p_scr):
    _, keys, queries = factored
    for a in range(N_SUB):
        out = _dot_nt(queries[a], keys[a])
        out = jnp.where(c["ci"] <= a * SUB + (c["ri"] & (SUB - 1)), out, 0.0)
        for h in range(GLA_HEADS):
            p_scr[h, a * SUB:(a + 1) * SUB, :] = out[h * SUB:(h + 1) * SUB, :]


def _gla_intra_bwd_factored(c, factored, dps, dq_scr, dk_scr):
    eks, keys, queries = factored
    mask = c["masks"]["heads"]
    dk = jnp.zeros((CHUNK, 256), F32)
    for a in range(N_SUB):
        dpa = jnp.concatenate([dps[h][a * SUB:(a + 1) * SUB, :] for h in range(GLA_HEADS)], axis=0)
        dq = jnp.where(mask, _dot(dpa, keys[a]), 0.0)
        dq = dq[0:SUB] + dq[SUB:2 * SUB] + dq[2 * SUB:3 * SUB] + dq[3 * SUB:4 * SUB]
        dq_scr[a * SUB:(a + 1) * SUB, :] = dq * c["eq"][a]
        dk = dk + _dot_tn(dpa, queries[a]) * eks[a]
    dk_scr[...] = dk


def _gla_lag_weights(c):
    cum, row = c["cum"], c["row"]
    out = [jnp.ones((CHUNK, 256), F32)]
    for r in range(1, SUB):
        out.append(jnp.where((row % SUB) >= r, jnp.exp(jnp.minimum(cum - pltpu.roll(cum, r, 0), 0.0)), 0.0))
    return out


def _gla_pairwise_keys(c):
    return [None] + [c["k"] * jnp.exp(jnp.minimum(c["refs"][a] - c["cum"], 0.0)) for a in range(1, N_SUB)]


def _gla_scores_pairwise(c, lag_w, keys, h):
    sl = slice(GLA_DK * h, GLA_DK * (h + 1))
    qs, k = c["qs"][:, sl], c["k"][:, sl]
    ri, ci = c["ri"], c["ci"]
    p = jnp.zeros((CHUNK, CHUNK), F32)
    for r in range(SUB):
        kr = k if r == 0 else pltpu.roll(k, r, 0)
        pr = jnp.sum(qs * kr * lag_w[r][:, sl], axis=1, keepdims=True)
        p = p + jnp.where(ci == ri - r, pr, 0.0)
    blocks = [jnp.zeros((SUB, CHUNK), F32)]
    for a in range(1, N_SUB):
        qh = qs[a * SUB:(a + 1) * SUB, :] * c["eq"][a][:, sl]
        blocks.append(jnp.where(ci[:SUB, :] < a * SUB, _dot_nt(qh, keys[a][:, sl]), 0.0))
    return p + jnp.concatenate(blocks, axis=0)


def _gla_all_scores(c, p_scr, factored):
    if factored:
        _gla_scores_factored(c, _gla_factored(c), p_scr)
    else:
        lag_w, keys = _gla_lag_weights(c), _gla_pairwise_keys(c)
        for h in range(GLA_HEADS):
            p_scr[h] = _gla_scores_pairwise(c, lag_w, keys, h)


def _either_form(chunks, run):
    small = chunks[0]["small"]
    for c in chunks[1:]:
        small = jnp.logical_and(small, c["small"])
    pl.when(small)(lambda: run(True))
    pl.when(jnp.logical_not(small))(lambda: run(False))


def _gla_intra_bwd_pairwise(c, lag_w, keys, dp, h):
    sl = slice(GLA_DK * h, GLA_DK * (h + 1))
    qs_h, k_h = c["qs"][:, sl], c["k"][:, sl]
    ri, ci = c["ri"], c["ci"]
    dq_rows = [jnp.zeros((SUB, GLA_DK), F32)]
    dk = jnp.zeros((CHUNK, GLA_DK), F32)
    for a in range(1, N_SUB):
        eq = c["eq"][a][:, sl]
        qh = qs_h[a * SUB:(a + 1) * SUB, :] * eq
        dpa = jnp.where(ci[:SUB, :] < a * SUB, dp[a * SUB:(a + 1) * SUB, :], 0.0)
        dq_rows.append(_dot(dpa, keys[a][:, sl]) * eq)
        ek = jnp.exp(jnp.minimum(c["refs"][a][:, sl] - c["cum"][:, sl], 0.0))
        dk = dk + _dot_tn(dpa, qh) * ek
    dq = jnp.concatenate(dq_rows, axis=0)
    for r in range(SUB):
        w = lag_w[r][:, sl]
        dpr = jnp.sum(jnp.where(ci == ri - r, dp, 0.0), axis=1, keepdims=True)
        kr = k_h if r == 0 else pltpu.roll(k_h, r, 0)
        dq = dq + dpr * kr * w
        back = dpr * qs_h * w
        dk = dk + (back if r == 0 else pltpu.roll(back, CHUNK - r, 0))
    return dq, dk


def _gla_all_intra_bwd(c, dps, p_scr, dq_scr, dk_scr, factored):
    if factored:
        terms = _gla_factored(c)
        _gla_scores_factored(c, terms, p_scr)
        _gla_intra_bwd_factored(c, terms, dps, dq_scr, dk_scr)
    else:
        lag_w, keys = _gla_lag_weights(c), _gla_pairwise_keys(c)
        outs = [_gla_intra_bwd_pairwise(c, lag_w, keys, dps[h], h) for h in range(GLA_HEADS)]
        for h in range(GLA_HEADS):
            p_scr[h] = _gla_scores_pairwise(c, lag_w, keys, h)
        dq_scr[...] = jnp.concatenate([o[0] for o in outs], axis=1)
        dk_scr[...] = jnp.concatenate([o[1] for o in outs], axis=1)


def _mixer_fwd(proj, cos2, sin2, w2p, gb, rnw, gnw, name, carried=()):
    n_carried = len(carried)

    def body(*refs):
        p_ref, c_ref, s_ref, w2_ref, gb_ref, rnw_ref, gnw_ref = refs[:7]
        x_refs, refs = refs[7:7 + n_carried], refs[7 + n_carried:]
        ocat_ref, mrg_ref, sr_out, sg_out = refs[:4]
        gathered_refs, refs = refs[4:4 + n_carried], refs[4 + n_carried:]
        sr, sg, p_scr = refs[:3]
        n = pl.program_id(0)
        if n_carried:
            start, forward, finish = _gather_phases(x_refs, gathered_refs, *refs[3:])
            pl.when(n == 0)(start)
            pl.when(n == (3 * N_STEPS) // 4)(forward)

        @pl.when(n == 0)
        def _():
            sr[...] = jnp.zeros_like(sr)
            sg[...] = jnp.zeros_like(sg)

        sr_out[0] = sr[...]
        cosv, sinv = c_ref[...], s_ref[...]

        for h in range(RET_HEADS):
            dmat, zeta, xi, gc = _ret_consts(h)
            hs = slice(128 * h, 128 * (h + 1))
            q = _rope(p_ref[:, O_RQ + 128 * h:O_RQ + 128 * (h + 1)], cosv, sinv)
            k = _rope(p_ref[:, O_RK + 128 * h:O_RK + 128 * (h + 1)], cosv, sinv) * (RET_DK ** -0.5)
            v = p_ref[:, O_RV + 128 * h:O_RV + 128 * (h + 1)]
            g = p_ref[:, O_RG + 128 * h:O_RG + 128 * (h + 1)]
            s_in = sr[h]
            a = _dot_nt(q, k) * dmat
            o = _dot(a, v) + _dot(q, s_in) * xi
            sr[h] = gc * s_in + _dot_tn(k * zeta, v)
            mu = jnp.mean(o, axis=-1, keepdims=True)
            xc = o - mu
            nrm = xc * lax.rsqrt(jnp.mean(xc * xc, axis=-1, keepdims=True) + EPS)
            ocat_ref[:, hs] = o
            mrg_ref[:, hs] = (nrm * rnw_ref[:, hs] * (g * _sigmoid(g))).astype(BF16)

        row_slices = [slice(CHUNK * j, CHUNK * (j + 1)) for j in range(CHUNKS_PER_STEP)]
        masks = _gla_masks()
        chunks = [_gla_common(p_ref, w2_ref, gb_ref, n * CHUNKS_PER_STEP + j, rows, masks)
                  for j, rows in enumerate(row_slices)]

        def gla_chunks(factored):
            own = masks["own"]
            for j, (rows, c) in enumerate(zip(row_slices, chunks)):
                s_in = sg[...]
                for h in range(GLA_HEADS):
                    sg_out[j, h] = s_in[GLA_DK * h:GLA_DK * (h + 1), GLA_DV * h:GLA_DV * (h + 1)]
                _gla_all_scores(c, p_scr.at[j], factored)
                v_all = p_ref[rows, O_GV:O_GV + GLA_HEADS * GLA_DV]
                o_inter = _dot(c["qs"] * c["ecum"], s_in)
                decay = jnp.exp(_dot_tn_exact_lhs(c["la"], jnp.ones((CHUNK, GLA_HEADS * GLA_DV), F32)))
                sg[...] = decay * s_in + jnp.where(own, _dot_tn(c["k"] * c["ekl"], v_all), 0.0)
                o_intra = _dot(p_scr[j].reshape(GLA_HEADS * CHUNK, CHUNK), v_all)
                for h in range(GLA_HEADS):
                    hs = slice(512 + 128 * h, 512 + 128 * (h + 1))
                    g = p_ref[rows, O_GR + 128 * h:O_GR + 128 * (h + 1)]
                    o = (o_intra[CHUNK * h:CHUNK * (h + 1), GLA_DV * h:GLA_DV * (h + 1)]
                         + o_inter[:, GLA_DV * h:GLA_DV * (h + 1)])
                    nrm = o * lax.rsqrt(jnp.mean(o * o, axis=-1, keepdims=True) + EPS)
                    ocat_ref[rows, hs] = o
                    mrg_ref[rows, hs] = (nrm * gnw_ref[:, 128 * h:128 * (h + 1)] * (g * _sigmoid(g))).astype(BF16)

        _either_form(chunks, gla_chunks)

        if n_carried:
            pl.when(n == N_STEPS - 1)(finish)

    const = lambda shape: pl.BlockSpec(shape, lambda n: (0,) * len(shape))
    anywhere = [pl.BlockSpec(memory_space=pl.ANY)] * n_carried
    return pl.pallas_call(
        body, name=name, grid=(N_STEPS,),
        in_specs=[pl.BlockSpec((STEP_ROWS, IN_WP), lambda n: (n, 0)),
                  pl.BlockSpec((STEP_ROWS, 128), lambda n: (n, 0)), pl.BlockSpec((STEP_ROWS, 128), lambda n: (n, 0)),
                  const((128, 256)), const((1, 256)), const((1, 512)), const((1, 512))] + anywhere,
        out_specs=[pl.BlockSpec((STEP_ROWS, D), lambda n: (n, 0)), pl.BlockSpec((STEP_ROWS, D), lambda n: (n, 0)),
                   pl.BlockSpec((1, RET_HEADS, RET_DK, 128), lambda n: (n, 0, 0, 0)),
                   pl.BlockSpec((CHUNKS_PER_STEP, GLA_HEADS, GLA_DK, GLA_DV), lambda n: (n, 0, 0, 0))] + anywhere,
        out_shape=[jax.ShapeDtypeStruct((LP, D), F32), jax.ShapeDtypeStruct((LP, D), BF16),
                   jax.ShapeDtypeStruct((N_STEPS, RET_HEADS, RET_DK, 128), F32),
                   jax.ShapeDtypeStruct((N_CHUNKS, GLA_HEADS, GLA_DK, GLA_DV), F32)] + _gathered_shapes(carried),
        scratch_shapes=[pltpu.VMEM((RET_HEADS, RET_DK, 128), F32),
                        pltpu.VMEM((GLA_HEADS * GLA_DK, GLA_HEADS * GLA_DV), F32),
                        pltpu.VMEM((CHUNKS_PER_STEP, GLA_HEADS, CHUNK, CHUNK), F32)] + _exchange_sems(n_carried),
        compiler_params=_cparams("arbitrary"),
    )(proj, cos2, sin2, w2p, gb, rnw, gnw, *carried)


def _mixer_bwd(proj, ocat, dmrg, sr_all, sg_all, cos2, sin2, w2p, gb, rnw, gnw, name, carried=()):
    last_step = N_STEPS - 1
    n_carried = len(carried)

    def body(*refs):
        p_ref, ocat_ref, dm_ref, sr_ref, sg_ref, c_ref, s_ref, w2_ref, gb_ref, rnw_ref, gnw_ref = refs[:11]
        g_refs, refs = refs[11:11 + n_carried], refs[11 + n_carried:]
        dp_ref, dw2_ref, dgb_ref, drn_ref, dgn_ref = refs[:5]
        got_refs, refs = refs[5:5 + n_carried], refs[5 + n_carried:]
        dsr, dsg, p_scr, dq_scr, dk_scr = refs[:5]
        step = pl.program_id(0)
        n = last_step - step
        if n_carried:
            start, finish = _exchange_phases(g_refs, got_refs, *refs[5:])
            pl.when(step == 0)(start)

        @pl.when(step == 0)
        def _():
            dsr[...] = jnp.zeros_like(dsr)
            dsg[...] = jnp.zeros_like(dsg)
            dw2_ref[...] = jnp.zeros_like(dw2_ref)
            dgb_ref[...] = jnp.zeros_like(dgb_ref)
            drn_ref[...] = jnp.zeros_like(drn_ref)
            dgn_ref[...] = jnp.zeros_like(dgn_ref)

        cosv, sinv = c_ref[...], s_ref[...]
        step_row = lax.broadcasted_iota(jnp.int32, (STEP_ROWS, 1), 0)
        real = ((n * STEP_ROWS + step_row) >= PAD_ROWS).astype(F32)

        for h in range(RET_HEADS):
            dmat, zeta, xi, gc = _ret_consts(h)
            hs = slice(128 * h, 128 * (h + 1))
            q = _rope(p_ref[:, O_RQ + 128 * h:O_RQ + 128 * (h + 1)], cosv, sinv)
            k = _rope(p_ref[:, O_RK + 128 * h:O_RK + 128 * (h + 1)], cosv, sinv) * (RET_DK ** -0.5)
            v = p_ref[:, O_RV + 128 * h:O_RV + 128 * (h + 1)]
            g = p_ref[:, O_RG + 128 * h:O_RG + 128 * (h + 1)]
            o = ocat_ref[:, hs]
            dy = dm_ref[:, hs]
            wv = rnw_ref[:, hs]
            mu = jnp.mean(o, axis=-1, keepdims=True)
            xc = o - mu
            rs = lax.rsqrt(jnp.mean(xc * xc, axis=-1, keepdims=True) + EPS)
            nrm = xc * rs
            sgm = _sigmoid(g)
            sil = g * sgm
            drn_ref[0:1, hs] += jnp.sum(dy * nrm * sil, axis=0, keepdims=True)
            dgate = dy * nrm * wv * (sgm * (1.0 + g * (1.0 - sgm)))
            dn = dy * wv * sil
            do = rs * (dn - jnp.mean(dn, axis=-1, keepdims=True) - nrm * jnp.mean(dn * nrm, axis=-1, keepdims=True))
            s_in = sr_ref[0, h]
            ds_out = dsr[h]
            a = _dot_nt(q, k) * dmat
            da = _dot_nt(do, v) * dmat
            dox = do * xi
            dq = _dot(da, k) + _dot_nt(dox, s_in)
            dk = _dot_tn(da, q) + _dot_nt(v, ds_out) * zeta
            dv = _dot_tn(a, do) + _dot(k * zeta, ds_out)
            dsr[h] = gc * ds_out + _dot_tn(q, dox)
            dk = dk * (RET_DK ** -0.5)
            dp_ref[:, O_RQ + 128 * h:O_RQ + 128 * (h + 1)] = (_unrope(dq, cosv, sinv) * real).astype(BF16)
            dp_ref[:, O_RK + 128 * h:O_RK + 128 * (h + 1)] = (_unrope(dk, cosv, sinv) * real).astype(BF16)
            dp_ref[:, O_RV + 128 * h:O_RV + 128 * (h + 1)] = (dv * real).astype(BF16)
            dp_ref[:, O_RG + 128 * h:O_RG + 128 * (h + 1)] = (dgate * real).astype(BF16)

        row_slices = [slice(CHUNK * j, CHUNK * (j + 1)) for j in range(CHUNKS_PER_STEP)]
        masks = _gla_masks()
        chunks = [_gla_common(p_ref, w2_ref, gb_ref, n * CHUNKS_PER_STEP + j, rows, masks)
                  for j, rows in enumerate(row_slices)]

        def gla_chunks(factored):
            for j in reversed(range(CHUNKS_PER_STEP)):
                gla_chunk_bwd(chunks[j], n * CHUNKS_PER_STEP + j, row_slices[j], j, factored, p_ref, ocat_ref, dm_ref,
                              sg_ref, w2_ref, gnw_ref, dp_ref, dw2_ref, dgb_ref, dgn_ref, dsg, p_scr, dq_scr, dk_scr)

        _either_form(chunks, gla_chunks)
        if n_carried:
            pl.when(step == last_step)(finish)

    def gla_chunk_bwd(c, chunk, rows, j, factored, p_ref, ocat_ref, dm_ref, sg_ref, w2_ref, gnw_ref,
                      dp_ref, dw2_ref, dgb_ref, dgn_ref, dsg, p_scr, dq_scr, dk_scr):
        row = lax.broadcasted_iota(jnp.int32, (CHUNK, 1), 0)
        real = ((chunk * CHUNK + row) >= PAD_ROWS).astype(F32)
        ri, ci = c["ri"], c["ci"]
        causal = ri >= ci
        triu = (ci >= ri).astype(F32)
        qe = c["qs"] * c["ecum"]
        kl = c["k"] * c["ekl"]
        v_all = p_ref[rows, O_GV:O_GV + GLA_HEADS * GLA_DV]
        dos, dps = [], []
        for h in range(GLA_HEADS):
            hs = slice(512 + 128 * h, 512 + 128 * (h + 1))
            g = p_ref[rows, O_GR + 128 * h:O_GR + 128 * (h + 1)]
            o = ocat_ref[rows, hs]
            dy = dm_ref[rows, hs]
            wv = gnw_ref[:, 128 * h:128 * (h + 1)]
            rs = lax.rsqrt(jnp.mean(o * o, axis=-1, keepdims=True) + EPS)
            nrm = o * rs
            sgm = _sigmoid(g)
            sil = g * sgm
            dgn_ref[0:1, 128 * h:128 * (h + 1)] += jnp.sum(dy * nrm * sil, axis=0, keepdims=True)
            dgate = dy * nrm * wv * (sgm * (1.0 + g * (1.0 - sgm)))
            dn = dy * wv * sil
            do = rs * (dn - nrm * jnp.mean(dn * nrm, axis=-1, keepdims=True))
            dp_ref[rows, O_GR + 128 * h:O_GR + 128 * (h + 1)] = (dgate * real).astype(BF16)
            dos.append(do)
        do_all = jnp.concatenate(dos, axis=1)
        do_blocks = jnp.where(c["masks"]["own"], jnp.concatenate([do_all] * GLA_HEADS, axis=0), 0.0)
        dp_all = _dot_nt(do_blocks, v_all)
        dps = [jnp.where(causal, dp_all[CHUNK * h:CHUNK * (h + 1), :], 0.0) for h in range(GLA_HEADS)]
        _gla_all_intra_bwd(c, dps, p_scr.at[j], dq_scr.at[j], dk_scr.at[j], factored)
        s_in = _block_diagonal([sg_ref[j, h] for h in range(GLA_HEADS)])
        ds_out = dsg[...]
        decay = jnp.exp(_dot_tn_exact_lhs(c["la"], jnp.ones((CHUNK, GLA_HEADS * GLA_DV), F32)))
        dv_state = _dot(kl, ds_out)
        dqe = _dot_nt(do_all, s_in)
        dkl = _dot_nt(v_all, ds_out)
        dsg[...] = jnp.where(c["masks"]["own"], _dot_tn(qe, do_all), 0.0) + decay * ds_out
        sd = s_in * ds_out
        sd_hi = sd.astype(BF16)
        sd_lo = (sd - sd_hi.astype(F32)).astype(BF16)
        ones8 = jnp.ones((8, GLA_HEADS * GLA_DV), BF16)
        nt = (((1,), (1,)), ((), ()))
        d_el = (lax.dot_general(ones8, sd_hi, nt, preferred_element_type=F32)
                + lax.dot_general(ones8, sd_lo, nt, preferred_element_type=F32))[0:1, :]
        dqs = dqe * c["ecum"] + dq_scr[j]
        dkk = dkl * c["ekl"] + dk_scr[j]
        d_last = jnp.sum(dkl * kl, axis=0, keepdims=True) + d_el * c["el"]
        dcum = c["qs"] * dqs - c["k"] * dkk + jnp.where(row == CHUNK - 1, d_last, 0.0)
        dla = _dot_exact_rhs(triu, dcum)
        dv = _dot_tn(p_scr[j].reshape(GLA_HEADS * CHUNK, CHUNK), do_blocks) + dv_state
        dp_ref[rows, O_GV:O_GV + GLA_HEADS * GLA_DV] = (dv * real).astype(BF16)
        dp_ref[rows, O_GQ:O_GQ + 256] = (dqs * (GLA_DK ** -0.5) * real).astype(BF16)
        dp_ref[rows, O_GK:O_GK + 256] = (dkk * real).astype(BF16)
        dz = dla * (1.0 / GLA_TAU) * _sigmoid(-c["z"]) * real
        ga = p_ref[rows, O_GA:O_GA + 128]
        dp_ref[rows, O_GA:O_GA + 128] = _dot_nt(dz, w2_ref[...]).astype(BF16)
        dp_ref[rows, O_GA + 128:IN_WP] = jnp.zeros((CHUNK, IN_WP - O_GA - 128), BF16)
        dw2_ref[...] += _dot_tn(ga, dz)
        dgb_ref[0:1, :] += jnp.sum(dz, axis=0, keepdims=True)

    const = lambda shape: pl.BlockSpec(shape, lambda s: (0,) * len(shape))
    rev = lambda s: (last_step - s, 0)
    anywhere = [pl.BlockSpec(memory_space=pl.ANY)] * n_carried
    return pl.pallas_call(
        body, name=name, grid=(N_STEPS,),
        in_specs=[pl.BlockSpec((STEP_ROWS, IN_WP), rev), pl.BlockSpec((STEP_ROWS, D), rev),
                  pl.BlockSpec((STEP_ROWS, D), rev),
                  pl.BlockSpec((1, RET_HEADS, RET_DK, 128), lambda s: (last_step - s, 0, 0, 0)),
                  pl.BlockSpec((CHUNKS_PER_STEP, GLA_HEADS, GLA_DK, GLA_DV), lambda s: (last_step - s, 0, 0, 0)),
                  pl.BlockSpec((STEP_ROWS, 128), rev), pl.BlockSpec((STEP_ROWS, 128), rev),
                  const((128, 256)), const((1, 256)), const((1, 512)), const((1, 512))] + anywhere,
        out_specs=[pl.BlockSpec((STEP_ROWS, IN_WP), rev), const((128, 256)), const((8, 256)),
                   const((8, 512)), const((8, 512))] + anywhere,
        out_shape=[jax.ShapeDtypeStruct((LP, IN_WP), BF16), jax.ShapeDtypeStruct((128, 256), F32),
                   jax.ShapeDtypeStruct((8, 256), F32), jax.ShapeDtypeStruct((8, 512), F32),
                   jax.ShapeDtypeStruct((8, 512), F32)] + [jax.ShapeDtypeStruct(g.shape, g.dtype) for g in carried],
        scratch_shapes=[pltpu.VMEM((RET_HEADS, RET_DK, 128), F32),
                        pltpu.VMEM((GLA_HEADS * GLA_DK, GLA_HEADS * GLA_DV), F32),
                        pltpu.VMEM((CHUNKS_PER_STEP, GLA_HEADS, CHUNK, CHUNK), F32),
                        pltpu.VMEM((CHUNKS_PER_STEP, CHUNK, 256), F32),
                        pltpu.VMEM((CHUNKS_PER_STEP, CHUNK, 256), F32)] + _exchange_sems(n_carried),
        compiler_params=_cparams("arbitrary"),
    )(proj, ocat, dmrg, sr_all, sg_all, cos2, sin2, w2p, gb, rnw, gnw, *carried)


def _all_gather(xs, name):
    n = len(xs)

    def body(*refs):
        start, forward, finish = _gather_phases(refs[:n], refs[n:2 * n], *refs[2 * n:])
        start()
        forward()
        finish()

    return pl.pallas_call(
        body, name=name,
        in_specs=[pl.BlockSpec(memory_space=pl.ANY)] * n,
        out_specs=[pl.BlockSpec(memory_space=pl.ANY)] * n,
        out_shape=_gathered_shapes(xs),
        scratch_shapes=_exchange_sems(n),
    )(*xs)


def _gathered_shapes(xs):
    return [jax.ShapeDtypeStruct((N_DEV,) + x.shape, x.dtype) for x in xs]


def _exchange_sems(n):
    if n == 0:
        return []
    return [pltpu.SemaphoreType.DMA((7 * n,)), pltpu.SemaphoreType.DMA((7 * n,)), pltpu.SemaphoreType.DMA((n,))]


def _gather_phases(x_refs, out_refs, send_sems, recv_sems, local_sems):
    n = len(x_refs)
    mx, my, mc = lax.axis_index("x"), lax.axis_index("y"), lax.axis_index("c")
    me, sibling = (mx, my, mc), (mx, my, 1 - mc)
    chips = [(1 - mx, my), (mx, 1 - my), (1 - mx, 1 - my)]

    def slot(a, px, py, pc):
        return out_refs[a].at[4 * px + 2 * py + pc]

    def copy(a, k, block, to, src=None):
        return pltpu.make_async_remote_copy(
            src_ref=slot(a, *block) if src is None else src, dst_ref=slot(a, *block),
            send_sem=send_sems.at[7 * a + k], recv_sem=recv_sems.at[7 * a + k],
            device_id=to, device_id_type=MESH_IDS)

    mine = [pltpu.make_async_copy(x_refs[a], slot(a, *me), local_sems.at[a]) for a in range(n)]
    first = []
    for a in range(n):
        first.append(copy(a, 0, me, sibling, src=x_refs[a]))
        first += [copy(a, 1 + j, me, (*chip, mc), src=x_refs[a]) for j, chip in enumerate(chips)]
    passed = [copy(a, 4 + j, (*chip, mc), sibling) for j, chip in enumerate(chips) for a in range(n)]

    def start():
        for cp in mine + first:
            cp.start()

    def forward():
        for j, chip in enumerate(chips):
            for a in range(n):
                copy(a, 1 + j, (*chip, mc), me).wait_recv()
                passed[j * n + a].start()

    def finish():
        for a in range(n):
            copy(a, 0, sibling, me).wait_recv()
            for j, chip in enumerate(chips):
                copy(a, 4 + j, (*chip, 1 - mc), me).wait_recv()
        for cp in first + passed:
            cp.wait_send()
        for cp in mine:
            cp.wait()

    return start, forward, finish


def _exchange_blocks(gs, name):
    n = len(gs)

    def body(*refs):
        start, finish = _exchange_phases(refs[:n], refs[n:2 * n], *refs[2 * n:])
        start()
        finish()

    return pl.pallas_call(
        body, name=name,
        in_specs=[pl.BlockSpec(memory_space=pl.ANY)] * n,
        out_specs=[pl.BlockSpec(memory_space=pl.ANY)] * n,
        out_shape=[jax.ShapeDtypeStruct(g.shape, g.dtype) for g in gs],
        scratch_shapes=_exchange_sems(n),
    )(*gs)


def _exchange_phases(g_refs, out_refs, send_sems, recv_sems, local_sems):
    n = len(g_refs)
    mx, my, mc = lax.axis_index("x"), lax.axis_index("y"), lax.axis_index("c")
    me = 4 * mx + 2 * my + mc
    mine = [pltpu.make_async_copy(g_refs[a].at[me], out_refs[a].at[me], local_sems.at[a]) for a in range(n)]
    copies = []
    for r in range(1, N_DEV):
        px, py, pc = mx ^ (r >> 2), my ^ ((r >> 1) & 1), mc ^ (r & 1)
        peer = 4 * px + 2 * py + pc
        for a in range(n):
            copies.append(pltpu.make_async_remote_copy(
                src_ref=g_refs[a].at[peer], dst_ref=out_refs[a].at[me],
                send_sem=send_sems.at[7 * a + r - 1], recv_sem=recv_sems.at[7 * a + r - 1],
                device_id=(px, py, pc), device_id_type=MESH_IDS))

    def start():
        for cp in mine + copies:
            cp.start()

    def finish():
        for cp in copies:
            cp.wait_recv()
        for cp in copies:
            cp.wait_send()
        for cp in mine:
            cp.wait()

    return start, finish


IN_SHARD = IN_W // N_DEV
IN_SHARD_P = 512
UP_SHARD = D_UP // N_DEV
UP_SHARD_P = 768
RELAYOUT_ROWS = 256


def _pieces_w_in():
    return [(k, 0, IN_SHARD * k, IN_SHARD) for k in range(N_DEV)]


def _pieces_ffn_up():
    pieces = []
    for k in range(N_DEV):
        n, end = UP_SHARD * k, UP_SHARD * (k + 1)
        while n < end:
            half, r = divmod(n, D_FF)
            blk, off = divmod(r, CONV_BLOCK)
            run = min(CONV_BLOCK - off, end - n)
            pieces.append((k, n - UP_SHARD * k, 2 * CONV_BLOCK * blk + CONV_BLOCK * half + off, run))
            n += run
    return pieces


def _assemble_block(load, spans, dst_block, rows):
    lo = 128 * dst_block
    lane = lax.broadcasted_iota(jnp.int32, (1, 128), 1)
    out = jnp.zeros((rows, 128), F32)
    for key, src_off, dst_off, length in spans:
        a, b = max(lo, dst_off), min(lo + 128, dst_off + length)
        s, s_end = src_off + (a - dst_off), src_off + (b - dst_off)
        d = a
        while s < s_end:
            e = min(s_end, 128 * (s // 128 + 1))
            blk = load(key, s // 128)
            shift = (d - s) % 128
            if shift:
                blk = pltpu.roll(blk, shift, 1)
            out = jnp.where((lane >= d - lo) & (lane < d - lo + (e - s)), blk, out)
            d += e - s
            s = e
    return out


def _shards_to_cols(shards, pieces, width, name):
    _, rows, _ = shards.shape
    tr = RELAYOUT_ROWS

    def body(s_ref, o_ref):
        load = lambda k, b: s_ref[k, :, 128 * b:128 * (b + 1)].astype(F32)
        for db in range(width // 128):
            o_ref[:, 128 * db:128 * (db + 1)] = _assemble_block(load, pieces, db, tr).astype(BF16)

    return pl.pallas_call(
        body, name=name, grid=(rows // tr,),
        in_specs=[pl.BlockSpec((N_DEV, tr, shards.shape[2]), lambda i: (0, i, 0))],
        out_specs=pl.BlockSpec((tr, width), lambda i: (i, 0)),
        out_shape=jax.ShapeDtypeStruct((rows, width), BF16),
        compiler_params=_cparams("parallel"),
    )(shards)


def _cols_to_shards(full, pieces, shard_width, name):
    rows, width = full.shape
    tr = RELAYOUT_ROWS

    def body(f_ref, o_ref):
        load = lambda _, b: f_ref[:, 128 * b:128 * (b + 1)].astype(F32)
        for k in range(N_DEV):
            spans = [(None, dst_off, src_off, length) for dev, src_off, dst_off, length in pieces if dev == k]
            for db in range(shard_width // 128):
                o_ref[k, :, 128 * db:128 * (db + 1)] = _assemble_block(load, spans, db, tr).astype(BF16)

    return pl.pallas_call(
        body, name=name, grid=(rows // tr,),
        in_specs=[pl.BlockSpec((tr, width), lambda i: (i, 0))],
        out_specs=pl.BlockSpec((N_DEV, tr, shard_width), lambda i: (0, i, 0)),
        out_shape=jax.ShapeDtypeStruct((N_DEV, rows, shard_width), BF16),
        compiler_params=_cparams("parallel"),
    )(full)


def _adamw(parts, w, m, v, rows_per_step, name):
    rows, cols = w.shape
    assert rows % rows_per_step == 0 and parts.shape == (N_DEV, rows, cols)

    def body(p_ref, w_ref, m_ref, v_ref, g_ref, d_ref, nm_ref, nv_ref):
        g = p_ref[0].astype(F32)
        for j in range(1, N_DEV):
            g = g + p_ref[j].astype(F32)
        m_new = ADAM_B1 * m_ref[...] + (1.0 - ADAM_B1) * g
        v_new = ADAM_B2 * v_ref[...] + (1.0 - ADAM_B2) * (g * g)
        m_hat = m_new / (1.0 - ADAM_B1 ** ADAM_STEP)
        v_hat = v_new / (1.0 - ADAM_B2 ** ADAM_STEP)
        g_ref[...] = g
        d_ref[...] = -ADAM_LR * (m_hat / (jnp.sqrt(v_hat) + ADAM_EPS) + ADAM_WD * w_ref[...])
        nm_ref[...] = m_new
        nv_ref[...] = v_new

    tile = pl.BlockSpec((rows_per_step, cols), lambda i: (i, 0))
    shape = jax.ShapeDtypeStruct((rows, cols), F32)
    return pl.pallas_call(
        body, name=name, grid=(rows // rows_per_step,),
        in_specs=[pl.BlockSpec((N_DEV, rows_per_step, cols), lambda i: (0, i, 0)), tile, tile, tile],
        out_specs=[tile, tile, tile, tile],
        out_shape=[shape, shape, shape, shape],
        compiler_params=_cparams("parallel"),
    )(parts, w, m, v)


BIG = (("w_in", (DEPTH, D, IN_W // N_DEV), 2), ("w_out", (DEPTH, D // N_DEV, D), 1),
       ("ffn_up", (DEPTH, D, D_UP // N_DEV), 2), ("ffn_down", (DEPTH, D_FF // N_DEV, D), 1))
SMALL = (("meta_tokens", (N_META, D // N_DEV), 1), ("gla_gate_w2", (DEPTH, GATE_RANK, 256 // N_DEV), 2),
         ("ffn_conv_w", (DEPTH, 3, D_UP // N_DEV), 2))
REPL = (("pre_mix_norm", (DEPTH, D)), ("gla_gate_b", (DEPTH, 256)), ("ret_norm_w", (DEPTH, 512)),
        ("gla_norm_w", (DEPTH, 512)), ("post_mix_norm", (DEPTH, D)), ("pre_ffn_norm", (DEPTH, D)),
        ("ffn_conv_b", (DEPTH, D_UP)), ("post_ffn_norm", (DEPTH, D)))
WEIGHT_ORDER = ("meta_tokens", "pre_mix_norm", "w_in", "gla_gate_w2", "gla_gate_b", "ret_norm_w", "gla_norm_w",
                "w_out", "post_mix_norm", "pre_ffn_norm", "ffn_up", "ffn_conv_w", "ffn_conv_b", "ffn_down",
                "post_ffn_norm")


def _size(shape):
    return math.prod(shape)


def _round_up(n, mult):
    return -(-n // mult) * mult


REPL_ROWS = _round_up(-(-sum(_size(s) for _, s in REPL) // LANES), 8)
SMALL_ROWS = _round_up(-(-sum(_size(s) for _, s, _ in SMALL) // LANES), 8)


def _pack(arrays, rows, dtype):
    flat = jnp.concatenate([a.reshape(-1).astype(dtype) for a in arrays])
    return jnp.pad(flat, (0, rows * LANES - flat.shape[0])).reshape(rows, LANES)


def _unpack(buf, shapes):
    flat = buf.reshape(-1)
    out, off = [], 0
    for shape in shapes:
        out.append(flat[off:off + _size(shape)].reshape(shape))
        off += _size(shape)
    return out


def _unshard(blocks, axis):
    moved = jnp.moveaxis(blocks, 0, axis)
    shape = list(moved.shape)
    shape[axis:axis + 2] = [shape[axis] * shape[axis + 1]]
    return moved.reshape(shape)


def _to_blocks(full, axis):
    shape = list(full.shape)
    shape[axis:axis + 1] = [N_DEV, shape[axis] // N_DEV]
    return jnp.moveaxis(full.reshape(shape), axis, 0)


def _interleave_cols(w):
    lead = w.shape[:-1]
    return jnp.swapaxes(w.reshape(lead + (2, N_CONV_BLOCKS, CONV_BLOCK)), -3, -2).reshape(lead + (D_UP,))


def _deinterleave_cols(w):
    lead = w.shape[:-1]
    return jnp.swapaxes(w.reshape(lead + (N_CONV_BLOCKS, 2, CONV_BLOCK)), -3, -2).reshape(lead + (D_UP,))


def _rope_tables():
    half = RET_DK // 2
    inv = ROPE_BASE ** (-jnp.arange(half, dtype=F32) / half)
    pos = jnp.arange(LP, dtype=F32) - float(PAD_ROWS)
    ang = pos[:, None] * inv[None, :]
    c, s = jnp.cos(ang), jnp.sin(ang)
    return jnp.concatenate([c, c], axis=1), jnp.concatenate([-s, s], axis=1)


def kernel(x, meta_tokens, pre_mix_norm, w_in, gla_gate_w2, gla_gate_b, ret_norm_w, gla_norm_w, w_out, post_mix_norm, pre_ffn_norm, ffn_up, ffn_conv_w, ffn_conv_b, ffn_down, post_ffn_norm, loss_target, m_meta_tokens, m_pre_mix_norm, m_w_in, m_gla_gate_w2, m_gla_gate_b, m_ret_norm_w, m_gla_norm_w, m_w_out, m_post_mix_norm, m_pre_ffn_norm, m_ffn_up, m_ffn_conv_w, m_ffn_conv_b, m_ffn_down, m_post_ffn_norm, v_meta_tokens, v_pre_mix_norm, v_w_in, v_gla_gate_w2, v_gla_gate_b, v_ret_norm_w, v_gla_norm_w, v_w_out, v_post_mix_norm, v_pre_ffn_norm, v_ffn_up, v_ffn_conv_w, v_ffn_conv_b, v_ffn_down, v_post_ffn_norm):
    weights = dict(meta_tokens=meta_tokens, pre_mix_norm=pre_mix_norm, w_in=w_in, gla_gate_w2=gla_gate_w2,
                   gla_gate_b=gla_gate_b, ret_norm_w=ret_norm_w, gla_norm_w=gla_norm_w, w_out=w_out,
                   post_mix_norm=post_mix_norm, pre_ffn_norm=pre_ffn_norm, ffn_up=ffn_up, ffn_conv_w=ffn_conv_w,
                   ffn_conv_b=ffn_conv_b, ffn_down=ffn_down, post_ffn_norm=post_ffn_norm)
    mom1 = dict(meta_tokens=m_meta_tokens, pre_mix_norm=m_pre_mix_norm, w_in=m_w_in, gla_gate_w2=m_gla_gate_w2,
                gla_gate_b=m_gla_gate_b, ret_norm_w=m_ret_norm_w, gla_norm_w=m_gla_norm_w, w_out=m_w_out,
                post_mix_norm=m_post_mix_norm, pre_ffn_norm=m_pre_ffn_norm, ffn_up=m_ffn_up,
                ffn_conv_w=m_ffn_conv_w, ffn_conv_b=m_ffn_conv_b, ffn_down=m_ffn_down, post_ffn_norm=m_post_ffn_norm)
    mom2 = dict(meta_tokens=v_meta_tokens, pre_mix_norm=v_pre_mix_norm, w_in=v_w_in, gla_gate_w2=v_gla_gate_w2,
                gla_gate_b=v_gla_gate_b, ret_norm_w=v_ret_norm_w, gla_norm_w=v_gla_norm_w, w_out=v_w_out,
                post_mix_norm=v_post_mix_norm, pre_ffn_norm=v_pre_ffn_norm, ffn_up=v_ffn_up,
                ffn_conv_w=v_ffn_conv_w, ffn_conv_b=v_ffn_conv_b, ffn_down=v_ffn_down, post_ffn_norm=v_post_ffn_norm)

    pad_cols = lambda a, width: jnp.pad(a, ((0, 0), (0, width - a.shape[1])))
    big_names = [n for n, _, _ in BIG]
    shard = {}
    for l in range(DEPTH):
        shard[l, "w_in"] = pad_cols(w_in[l].astype(BF16), IN_SHARD_P)
        shard[l, "w_out"] = w_out[l].astype(BF16)
        shard[l, "ffn_up"] = pad_cols(ffn_up[l].astype(BF16), UP_SHARD_P)
        shard[l, "ffn_down"] = ffn_down[l].astype(BF16)
    gathered = {(0, "w_in"): _all_gather([shard[0, "w_in"]], "gather_w_in_0")[0]}
    gather_in_mixer = {l: [(l, n) for n in big_names[1:]] for l in range(DEPTH)}
    gather_in_conv = {l: [(l + 1, "w_in")] for l in range(DEPTH - 1)}
    small = _all_gather([_pack([weights[n] for n, _, _ in SMALL], SMALL_ROWS, F32)], "gather_small_weights")[0]
    small_parts = _unpack_blocks(small, [s for _, s, _ in SMALL])
    full = {n: _unshard(p, ax) for (n, _, ax), p in zip(SMALL, small_parts)}
    w2p = jnp.pad(full["gla_gate_w2"], ((0, 0), (0, 128 - GATE_RANK), (0, 0)))
    cw8 = jnp.concatenate([_interleave_cols(full["ffn_conv_w"]), _interleave_cols(ffn_conv_b)[:, None, :],
                           jnp.zeros((DEPTH, 4, D_UP), F32)], axis=1)
    cos2, sin2 = _rope_tables()

    h = jnp.concatenate([jnp.zeros((PAD_ROWS, D), F32), full["meta_tokens"], x[0]], axis=0)
    target = jnp.concatenate([jnp.zeros((CHUNK, D), F32), loss_target[0]], axis=0)
    saved, layer_w = [], []
    for l in range(DEPTH):
        lw = dict(w_in=_shards_to_cols(gathered[l, "w_in"], _pieces_w_in(), IN_WP, f"w_in_cols_{l}"))
        a1, proj = _norm_matmul(h, pre_mix_norm[l:l + 1], lw["w_in"], out_dtype=F32, tm=TM_BIG, tn=IN_WP // 3,
                                name=f"in_proj_{l}")
        keys = gather_in_mixer.get(l, [])
        ocat, merged, sr_all, sg_all, *got = _mixer_fwd(proj, cos2, sin2, w2p[l], gla_gate_b[l:l + 1],
                                                        ret_norm_w[l:l + 1], gla_norm_w[l:l + 1], f"mixer_fwd_{l}",
                                                        carried=[shard[key] for key in keys])
        gathered.update(zip(keys, got))
        lw["w_out"] = gathered[l, "w_out"].reshape(D, D)
        lw["w_up"] = _shards_to_cols(gathered[l, "ffn_up"], _pieces_ffn_up(), D_UP, f"ffn_up_cols_{l}")
        lw["w_down"] = gathered[l, "ffn_down"].reshape(D_FF, D)
        layer_w.append(lw)
        m, h1 = _matmul_resid_norm(merged, lw["w_out"], h, post_mix_norm[l:l + 1], f"out_proj_{l}")
        a2, u = _norm_matmul(h1, pre_ffn_norm[l:l + 1], lw["w_up"], out_dtype=BF16, tm=TM_BIG, tn=D_UP // 4,
                             name=f"ffn_up_{l}")
        keys = gather_in_conv.get(l, [])
        cv, act, *got = _conv_act_fwd(u, cw8[l], f"ffn_conv_act_{l}", carried=[shard[key] for key in keys])
        gathered.update(zip(keys, got))
        f, h2, *loss_acc = _matmul_resid_norm(act, lw["w_down"], h1, post_ffn_norm[l:l + 1], f"ffn_down_{l}",
                                              target=target if l == DEPTH - 1 else None)
        saved.append(dict(h=h, a1=a1, proj=proj, ocat=ocat, merged=merged, sr=sr_all, sg=sg_all, m=m, h1=h1,
                          a2=a2, u=u, cv=cv, act=act, f=f))
        h = h2

    dh = h
    loss = lax.psum(loss_acc[0][0, 0], ("x", "y", "c"))

    kinds = ("grad", "delta", "new_m", "new_v")
    grads = {n: [None] * DEPTH for n in WEIGHT_ORDER if n != "meta_tokens" and n not in big_names}
    pending, parts = [], {}
    for l in reversed(range(DEPTH)):
        s, lw = saved[l], layer_w[l]
        dact, df, g_post_ffn = _norm_bwd_matmul(dh, s["f"], post_ffn_norm[l:l + 1], lw["w_down"], BF16,
                                                f"ffn_down_dx_{l}")
        g_down = _matmul(s["act"], df, ta=True, out_dtype=BF16, tm=D_FF // 2, tn=D, tk=TM_BIG, name=f"ffn_down_dw_{l}")
        du, dcw = _conv_act_bwd(dact, s["cv"], s["u"], cw8[l], f"ffn_conv_act_bwd_{l}")
        dh1, g_pre_ffn = _matmul_norm_bwd(du, lw["w_up"], s["h1"], pre_ffn_norm[l:l + 1], dh, D_FF, f"ffn_up_dx_{l}")
        g_up = _matmul(s["a2"], du, ta=True, out_dtype=BF16, tm=D, tn=D_FF, tk=TM_BIG, name=f"ffn_up_dw_{l}")
        dmerged, dm, g_post_mix = _norm_bwd_matmul(dh1, s["m"], post_mix_norm[l:l + 1], lw["w_out"], F32,
                                                   f"out_proj_dx_{l}")
        g_out = _matmul(s["merged"], dm, ta=True, out_dtype=BF16, tm=D, tn=D, tk=TM_BIG, name=f"out_proj_dw_{l}")
        pending += [((l, "ffn_down"), g_down.reshape(N_DEV, D_FF // N_DEV, D)),
                    ((l, "ffn_up"), _cols_to_shards(g_up, _pieces_ffn_up(), UP_SHARD_P, f"ffn_up_grad_shards_{l}")),
                    ((l, "w_out"), g_out.reshape(N_DEV, D // N_DEV, D))]
        dproj, g_w2, g_gb, g_rn, g_gn, *got = _mixer_bwd(s["proj"], s["ocat"], dmerged, s["sr"], s["sg"], cos2, sin2,
                                                         w2p[l], gla_gate_b[l:l + 1], ret_norm_w[l:l + 1],
                                                         gla_norm_w[l:l + 1], f"mixer_bwd_{l}",
                                                         carried=[blocks for _, blocks in pending])
        parts.update(zip([key for key, _ in pending], got))
        g_in = _matmul(s["a1"], dproj, ta=True, out_dtype=BF16, tm=D, tn=IN_WP // 2, tk=TM_BIG, name=f"in_proj_dw_{l}")
        pending = [((l, "w_in"), _cols_to_shards(g_in, _pieces_w_in(), IN_SHARD_P, f"w_in_grad_shards_{l}"))]
        now = pending if l == 0 else []
        dh, g_pre_mix, *got = _matmul_norm_bwd(dproj, lw["w_in"], s["h"], pre_mix_norm[l:l + 1], dh1, IN_WP,
                                               f"in_proj_dx_{l}", carried=[blocks for _, blocks in now])
        parts.update(zip([key for key, _ in now], got))
        pending = [] if l == 0 else pending
        grads["post_ffn_norm"][l] = g_post_ffn[0]
        grads["ffn_conv_w"][l] = _deinterleave_cols(dcw[0:3])
        grads["ffn_conv_b"][l] = _deinterleave_cols(dcw[3])
        grads["pre_ffn_norm"][l] = g_pre_ffn[0]
        grads["post_mix_norm"][l] = g_post_mix[0]
        grads["gla_gate_w2"][l] = g_w2[:GATE_RANK]
        grads["gla_gate_b"][l] = g_gb[0]
        grads["ret_norm_w"][l] = g_rn[0]
        grads["gla_norm_w"][l] = g_gn[0]
        grads["pre_mix_norm"][l] = g_pre_mix[0]
    local = {n: jnp.stack(v) for n, v in grads.items()}
    local["meta_tokens"] = dh[PAD_ROWS:CHUNK]
    grad_x = dh[CHUNK:][None]

    blocks = jnp.concatenate([_to_blocks(local[n], ax).reshape(N_DEV, -1) for n, _, ax in SMALL], axis=1)
    blocks = jnp.pad(blocks, ((0, 0), (0, SMALL_ROWS * LANES - blocks.shape[1]))).reshape(N_DEV, SMALL_ROWS, LANES)
    *got, small_grad_parts = _exchange_blocks([b for _, b in pending] + [blocks], "exchange_last_grads")
    parts.update(zip([key for key, _ in pending], got))

    widths = dict(w_in=IN_SHARD_P, w_out=D, ffn_up=UP_SHARD_P, ffn_down=D)
    steps = dict(w_in=256, w_out=D // N_DEV, ffn_up=256, ffn_down=D_FF // N_DEV // 2)
    big_out = {kind: {n: [None] * DEPTH for n in big_names} for kind in kinds}
    for l in range(DEPTH):
        for n in big_names:
            mine = [pad_cols(d[n][l], widths[n]) for d in (weights, mom1, mom2)]
            results = _adamw(parts[l, n], *mine, steps[n], f"adamw_{n}_{l}")
            for kind, r in zip(kinds, results):
                big_out[kind][n][l] = r[:, :weights[n].shape[2]]
    out = {kind: {n: jnp.stack(v) for n, v in big_out[kind].items()} for kind in kinds}
    shard_shapes = [s for _, s, _ in SMALL]
    packed = [_pack([d[n] for n, _, _ in SMALL], SMALL_ROWS, F32) for d in (weights, mom1, mom2)]
    results = _adamw(small_grad_parts, *packed, SMALL_ROWS, "adamw_small_sharded")
    for kind, buf in zip(kinds, results):
        out[kind].update(zip([n for n, _, _ in SMALL], _unpack(buf, shard_shapes)))

    repl_parts = _all_gather([_pack([local[n] for n, _ in REPL], REPL_ROWS, F32)], "gather_small_grads")[0]
    packed = [_pack([d[n] for n, _ in REPL], REPL_ROWS, F32) for d in (weights, mom1, mom2)]
    results = _adamw(repl_parts, *packed, REPL_ROWS, "adamw_replicated")
    repl_shapes = [s for _, s in REPL]
    for kind, buf in zip(kinds, results):
        out[kind].update(zip([n for n, _ in REPL], _unpack(buf, repl_shapes)))

    return (loss, grad_x, *[out["grad"][n] for n in WEIGHT_ORDER], *[out["delta"][n] for n in WEIGHT_ORDER],
            *[out["new_m"][n] for n in WEIGHT_ORDER], *[out["new_v"][n] for n in WEIGHT_ORDER])


def _unpack_blocks(gathered, shapes):
    flat = gathered.reshape(N_DEV, -1)
    out, off = [], 0
    for shape in shapes:
        out.append(flat[:, off:off + _size(shape)].reshape((N_DEV,) + shape))
        off += _size(shape)
    return out
```

```python
import math

import jax
import jax.numpy as jnp
from jax import lax
from jax.experimental import pallas as pl
from jax.experimental.pallas import tpu as pltpu

F32 = jnp.float32
BF16 = jnp.bfloat16

D = 1024
SEQ = 8192
DEPTH = 2
N_META = 16
CHUNK = 64
SUB = 16
N_SUB = CHUNK // SUB
PAD_ROWS = CHUNK - N_META
LP = SEQ + CHUNK
N_CHUNKS = LP // CHUNK
RET_HEADS = 4
RET_DK = 128
GLA_HEADS = 4
GLA_DK = 64
GLA_DV = 128
GLA_TAU = 16.0
GATE_RANK = 16
IN_W = 3600
IN_WP = 3840
D_FF = 2816
D_UP = 2 * D_FF
CONV_BLOCK = 256
N_CONV_BLOCKS = D_FF // CONV_BLOCK
ROPE_BASE = 10000.0
EPS = 1e-6
N_DEV = 8
LANES = 1024

O_RQ, O_RK, O_RV, O_RG = 0, 512, 1024, 1536
O_GQ, O_GK, O_GV, O_GR, O_GA = 2048, 2304, 2560, 3072, 3584

ADAM_LR = 0.001
ADAM_B1 = 0.9
ADAM_B2 = 0.999
ADAM_EPS = 1e-08
ADAM_WD = 0.01
ADAM_STEP = 10

VMEM_LIMIT = 56 * 1024 * 1024
MESH_IDS = pl.DeviceIdType.MESH


def _row_tile(rows, limit):
    best = 16
    for t in range(16, min(rows, limit) + 1, 16):
        if rows % t == 0:
            best = t
    return best


TM = _row_tile(LP, 688)
TM_BIG = _row_tile(LP, 1376)


def _cparams(*sem):
    return pltpu.CompilerParams(dimension_semantics=sem, vmem_limit_bytes=VMEM_LIMIT)


def _dot(a, b):
    return jnp.dot(a.astype(BF16), b.astype(BF16), preferred_element_type=F32)


def _dot_nt(a, b):
    return lax.dot_general(a.astype(BF16), b.astype(BF16), (((1,), (1,)), ((), ())), preferred_element_type=F32)


def _dot_tn(a, b):
    return lax.dot_general(a.astype(BF16), b.astype(BF16), (((0,), (0,)), ((), ())), preferred_element_type=F32)


def _split3(x):
    hi = x.astype(BF16)
    r1 = x - hi.astype(F32)
    mid = r1.astype(BF16)
    lo = (r1 - mid.astype(F32)).astype(BF16)
    return hi, mid, lo


def _dot_exact_rhs(t, x):
    n = x.shape[1]
    parts = jnp.dot(t.astype(BF16), jnp.concatenate(_split3(x), axis=1), preferred_element_type=F32)
    return parts[:, :n] + parts[:, n:2 * n] + parts[:, 2 * n:]


def _dot_tn_exact_lhs(x, ones):
    n = x.shape[1]
    parts = lax.dot_general(jnp.concatenate(_split3(x), axis=1), ones.astype(BF16), (((0,), (0,)), ((), ())),
                            preferred_element_type=F32)
    return parts[:n] + parts[n:2 * n] + parts[2 * n:]


def _sigmoid(x):
    return 1.0 / (1.0 + jnp.exp(-x))


def _matmul(a, b, *, ta=False, tb=False, out_dtype, tm, tn, tk, name):
    m = a.shape[1] if ta else a.shape[0]
    k = a.shape[0] if ta else a.shape[1]
    n = b.shape[0] if tb else b.shape[1]
    assert (b.shape[1] if tb else b.shape[0]) == k
    assert m % tm == 0 and n % tn == 0 and k % tk == 0, (name, m, n, k, tm, tn, tk)
    nk = k // tk
    a_spec = pl.BlockSpec((tk, tm), lambda i, j, kk: (kk, i)) if ta else pl.BlockSpec((tm, tk), lambda i, j, kk: (i, kk))
    b_spec = pl.BlockSpec((tn, tk), lambda i, j, kk: (j, kk)) if tb else pl.BlockSpec((tk, tn), lambda i, j, kk: (kk, j))
    dims = (((0 if ta else 1,), (1 if tb else 0,)), ((), ()))

    def body(a_ref, b_ref, o_ref, *acc):
        prod = lax.dot_general(a_ref[...].astype(BF16), b_ref[...].astype(BF16), dims, preferred_element_type=F32)
        if nk == 1:
            o_ref[...] = prod.astype(out_dtype)
            return
        acc_ref, = acc
        kk = pl.program_id(2)

        @pl.when(kk == 0)
        def _():
            acc_ref[...] = prod

        @pl.when(kk > 0)
        def _():
            acc_ref[...] += prod

        @pl.when(kk == nk - 1)
        def _():
            o_ref[...] = acc_ref[...].astype(out_dtype)

    return pl.pallas_call(
        body, name=name, grid=(m // tm, n // tn, nk),
        in_specs=[a_spec, b_spec],
        out_specs=pl.BlockSpec((tm, tn), lambda i, j, kk: (i, j)),
        out_shape=jax.ShapeDtypeStruct((m, n), out_dtype),
        scratch_shapes=[pltpu.VMEM((tm, tn), F32)] if nk > 1 else [],
        compiler_params=_cparams("parallel", "parallel", "arbitrary"),
    )(a, b)


def _norm_matmul(x, w, b, *, out_dtype, tm, tn, name):
    n = b.shape[1]
    assert LP % tm == 0 and n % tn == 0

    def body(x_ref, w_ref, b_ref, a_ref, o_ref, a_scr):
        @pl.when(pl.program_id(1) == 0)
        def _():
            xv = x_ref[...]
            r = lax.rsqrt(jnp.mean(xv * xv, axis=-1, keepdims=True) + EPS)
            a = (xv * r * w_ref[...]).astype(BF16)
            a_scr[...] = a
            a_ref[...] = a

        o_ref[...] = jnp.dot(a_scr[...], b_ref[...], preferred_element_type=F32).astype(out_dtype)

    return pl.pallas_call(
        body, name=name, grid=(LP // tm, n // tn),
        in_specs=[pl.BlockSpec((tm, D), lambda i, j: (i, 0)), pl.BlockSpec((1, D), lambda i, j: (0, 0)),
                  pl.BlockSpec((D, tn), lambda i, j: (0, j))],
        out_specs=[pl.BlockSpec((tm, D), lambda i, j: (i, 0)), pl.BlockSpec((tm, tn), lambda i, j: (i, j))],
        out_shape=[jax.ShapeDtypeStruct((LP, D), BF16), jax.ShapeDtypeStruct((LP, n), out_dtype)],
        scratch_shapes=[pltpu.VMEM((tm, D), BF16)],
        compiler_params=_cparams("arbitrary", "arbitrary"),
    )(x, w, b)


def _matmul_resid_norm(a, b, h, w, name, target=None):
    k = a.shape[1]
    has_loss = target is not None

    def body(a_ref, b_ref, h_ref, w_ref, *refs):
        m = jnp.dot(a_ref[...].astype(BF16), b_ref[...].astype(BF16), preferred_element_type=F32)
        r = lax.rsqrt(jnp.mean(m * m, axis=-1, keepdims=True) + EPS)
        i = pl.program_id(0)
        row = i * TM + lax.broadcasted_iota(jnp.int32, (TM, 1), 0)
        y = h_ref[...] + jnp.where(row >= PAD_ROWS, m * r * w_ref[...], 0.0)
        if not has_loss:
            m_ref, y_ref = refs
            m_ref[...] = m
            y_ref[...] = y
            return
        t_ref, m_ref, dy_ref, loss_ref = refs
        m_ref[...] = m

        @pl.when(i == 0)
        def _():
            loss_ref[...] = jnp.zeros_like(loss_ref)

        diff = jnp.where(row >= CHUNK, y - t_ref[...], 0.0)
        dy_ref[...] = diff * (1.0 / D)
        loss_ref[...] += (0.5 / D) * jnp.sum(diff * diff)

    tile = pl.BlockSpec((TM, D), lambda i: (i, 0))
    shape = jax.ShapeDtypeStruct((LP, D), F32)
    in_specs = [pl.BlockSpec((TM, k), lambda i: (i, 0)), pl.BlockSpec((k, D), lambda i: (0, 0)), tile,
                pl.BlockSpec((1, D), lambda i: (0, 0))]
    if has_loss:
        return pl.pallas_call(
            body, name=name, grid=(LP // TM,),
            in_specs=in_specs + [tile],
            out_specs=[tile, tile, pl.BlockSpec((8, 128), lambda i: (0, 0))],
            out_shape=[shape, shape, jax.ShapeDtypeStruct((8, 128), F32)],
            compiler_params=_cparams("arbitrary"),
        )(a, b, h, w, target)
    return pl.pallas_call(
        body, name=name, grid=(LP // TM,),
        in_specs=in_specs, out_specs=[tile, tile], out_shape=[shape, shape],
        compiler_params=_cparams("parallel"),
    )(a, b, h, w)


def _rmsnorm_bwd_rows(dy, x, w):
    r = lax.rsqrt(jnp.mean(x * x, axis=-1, keepdims=True) + EPS)
    g = dy * w
    dx = r * g - x * (r * r * r * jnp.mean(g * x, axis=-1, keepdims=True))
    return dx, jnp.sum(dy * x * r, axis=0, keepdims=True)


def _matmul_norm_bwd(dz, b, x, w, resid, tk, name, carried=()):
    k = dz.shape[1]
    assert k % tk == 0
    nk = k // tk
    n_rows = LP // TM
    n_carried = len(carried)

    def body(*refs):
        a_ref, b_ref, x_ref, w_ref, r_ref = refs[:5]
        g_refs, refs = refs[5:5 + n_carried], refs[5 + n_carried:]
        dx_ref, dw_ref = refs[:2]
        got_refs, refs = refs[2:2 + n_carried], refs[2 + n_carried:]
        acc, sems = (refs[:1], refs[1:]) if nk > 1 else ((), refs)
        i, kk = pl.program_id(0), pl.program_id(1)
        if n_carried:
            exchange_start, exchange_finish = _exchange_phases(g_refs, got_refs, *sems)
            pl.when((i == 0) & (kk == 0))(exchange_start)

        @pl.when((i == 0) & (kk == 0))
        def _():
            dw_ref[...] = jnp.zeros_like(dw_ref)

        prod = lax.dot_general(a_ref[...].astype(BF16), b_ref[...].astype(BF16), (((1,), (1,)), ((), ())),
                               preferred_element_type=F32)

        def finish(dy):
            dx, dw = _rmsnorm_bwd_rows(dy, x_ref[...], w_ref[...])
            dx_ref[...] = dx + r_ref[...]
            dw_ref[0:1, :] += dw

        if nk == 1:
            finish(prod)
        else:
            acc_ref, = acc

            @pl.when(kk == 0)
            def _():
                acc_ref[...] = prod

            @pl.when((kk > 0) & (kk < nk - 1))
            def _():
                acc_ref[...] += prod

            @pl.when(kk == nk - 1)
            def _():
                finish(acc_ref[...] + prod)

        if n_carried:
            pl.when((i == n_rows - 1) & (kk == nk - 1))(exchange_finish)

    tile = pl.BlockSpec((TM, D), lambda i, kk: (i, 0))
    anywhere = [pl.BlockSpec(memory_space=pl.ANY)] * n_carried
    return pl.pallas_call(
        body, name=name, grid=(n_rows, nk),
        in_specs=[pl.BlockSpec((TM, tk), lambda i, kk: (i, kk)), pl.BlockSpec((D, tk), lambda i, kk: (0, kk)), tile,
                  pl.BlockSpec((1, D), lambda i, kk: (0, 0)), tile] + anywhere,
        out_specs=[tile, pl.BlockSpec((8, D), lambda i, kk: (0, 0))] + anywhere,
        out_shape=[jax.ShapeDtypeStruct((LP, D), F32), jax.ShapeDtypeStruct((8, D), F32)]
        + [jax.ShapeDtypeStruct(g.shape, g.dtype) for g in carried],
        scratch_shapes=([pltpu.VMEM((TM, D), F32)] if nk > 1 else []) + _exchange_sems(n_carried),
        compiler_params=_cparams("arbitrary", "arbitrary"),
    )(dz, b, x, w, resid, *carried)


def _norm_bwd_matmul(dh, x, w, b, out_dtype, name):
    n = b.shape[0]

    def body(dh_ref, x_ref, w_ref, b_ref, o_ref, dx_ref, dw_ref):
        i = pl.program_id(0)

        @pl.when(i == 0)
        def _():
            dw_ref[...] = jnp.zeros_like(dw_ref)

        row = i * TM + lax.broadcasted_iota(jnp.int32, (TM, 1), 0)
        dy = jnp.where(row >= PAD_ROWS, dh_ref[...], 0.0)
        dx, dw = _rmsnorm_bwd_rows(dy, x_ref[...], w_ref[...])
        dxb = dx.astype(BF16)
        dx_ref[...] = dxb
        dw_ref[0:1, :] += dw
        o_ref[...] = lax.dot_general(dxb, b_ref[...].astype(BF16), (((1,), (1,)), ((), ())),
                                     preferred_element_type=F32).astype(out_dtype)

    tile = pl.BlockSpec((TM, D), lambda i: (i, 0))
    return pl.pallas_call(
        body, name=name, grid=(LP // TM,),
        in_specs=[tile, tile, pl.BlockSpec((1, D), lambda i: (0, 0)), pl.BlockSpec((n, D), lambda i: (0, 0))],
        out_specs=[pl.BlockSpec((TM, n), lambda i: (i, 0)), tile, pl.BlockSpec((8, D), lambda i: (0, 0))],
        out_shape=[jax.ShapeDtypeStruct((LP, n), out_dtype), jax.ShapeDtypeStruct((LP, D), BF16),
                   jax.ShapeDtypeStruct((8, D), F32)],
        compiler_params=_cparams("arbitrary"),
    )(dh, x, w, b)


GELU_C = math.sqrt(2.0 / math.pi)
GELU_K = 0.044715
STRIP = 16
HALF = 8


def _gelu_half(a):
    return 0.5 * jnp.tanh(a * (a * a * (GELU_C * GELU_K) + GELU_C)) + 0.5


def _gelu_slope(a, h):
    return h * (1.0 + (a - a * h) * (a * a * (6.0 * GELU_C * GELU_K) + 2.0 * GELU_C))


def _shift_down(x, prev8, rows):
    row = lax.broadcasted_iota(jnp.int32, (rows, 1), 0)
    p1 = pltpu.roll(prev8, 1, 0)
    p2 = pltpu.roll(prev8, 2, 0)
    x1 = jnp.where(row == 0, p1[0:1, :], pltpu.roll(x, 1, 0))
    x2 = jnp.where(row == 0, p2[0:1, :], jnp.where(row == 1, p2[1:2, :], pltpu.roll(x, 2, 0)))
    return x1, x2


def _conv_act_fwd(u, cw8, name, carried=()):
    n_rows = LP // TM
    cb2 = 2 * CONV_BLOCK
    n_carried = len(carried)

    def body(*refs):
        u_ref, cw_ref = refs[:2]
        x_refs, refs = refs[2:2 + n_carried], refs[2 + n_carried:]
        conv_ref, act_ref = refs[:2]
        gathered_refs, refs = refs[2:2 + n_carried], refs[2 + n_carried:]
        carry_ref = refs[0]
        j, i = pl.program_id(0), pl.program_id(1)
        if n_carried:
            start, forward, finish = _gather_phases(x_refs, gathered_refs, *refs[1:])
            pl.when((j == 0) & (i == 0))(start)
            pl.when((j == (3 * N_CONV_BLOCKS) // 4) & (i == 0))(forward)

        @pl.when(i == 0)
        def _():
            carry_ref[...] = jnp.zeros_like(carry_ref)

        x = u_ref[...].astype(F32)
        x1, x2 = _shift_down(x, carry_ref[...], TM)
        conv = cw_ref[3:4, :] + x2 * cw_ref[0:1, :] + x1 * cw_ref[1:2, :] + x * cw_ref[2:3, :]
        conv_ref[...] = conv.astype(BF16)
        a = conv[:, :CONV_BLOCK]
        g = conv[:, CONV_BLOCK:]
        act_ref[...] = (a * _gelu_half(a) * g).astype(BF16)
        carry_ref[...] = x[TM - 8:TM, :]
        if n_carried:
            pl.when((j == N_CONV_BLOCKS - 1) & (i == n_rows - 1))(finish)

    anywhere = [pl.BlockSpec(memory_space=pl.ANY)] * n_carried
    return pl.pallas_call(
        body, name=name, grid=(N_CONV_BLOCKS, n_rows),
        in_specs=[pl.BlockSpec((TM, cb2), lambda j, i: (i, j)), pl.BlockSpec((8, cb2), lambda j, i: (0, j))] + anywhere,
        out_specs=[pl.BlockSpec((TM, cb2), lambda j, i: (i, j)),
                   pl.BlockSpec((TM, CONV_BLOCK), lambda j, i: (i, j))] + anywhere,
        out_shape=[jax.ShapeDtypeStruct((LP, D_UP), BF16), jax.ShapeDtypeStruct((LP, D_FF), BF16)]
        + _gathered_shapes(carried),
        scratch_shapes=[pltpu.VMEM((8, cb2), F32)] + _exchange_sems(n_carried),
        compiler_params=_cparams("arbitrary", "arbitrary"),
    )(u, cw8, *carried)


def _conv_act_bwd(dact, conv, u, cw8, name):
    n_rows = LP // TM
    cb2 = 2 * CONV_BLOCK
    n_strips = TM // STRIP

    def body(dact_ref, conv_ref, u_ref, cw_ref, du_ref, dcw_ref, carry_ref):
        i = pl.program_id(1)

        @pl.when(i == 0)
        def _():
            dcw_ref[...] = jnp.zeros_like(dcw_ref)
            carry_ref[...] = jnp.zeros_like(carry_ref)

        w0, w1, w2 = cw_ref[0:1, :], cw_ref[1:2, :], cw_ref[2:3, :]
        row = lax.broadcasted_iota(jnp.int32, (HALF, 1), 0)

        def strip(k, carry):
            n1, n2, s0, s1, s2, s3 = carry
            r0 = pl.multiple_of((n_strips - 1 - k) * STRIP, STRIP)
            cv = conv_ref[pl.ds(r0, STRIP), :].astype(F32)
            dav = dact_ref[pl.ds(r0, STRIP), :].astype(F32)
            x = u_ref[pl.ds(r0, STRIP), :].astype(F32)
            du = [None, None]
            for half in (1, 0):
                rows = slice(HALF * half, HALF * (half + 1))
                a, g, dah = cv[rows, :CONV_BLOCK], cv[rows, CONV_BLOCK:], dav[rows]
                h = _gelu_half(a)
                dconv = jnp.concatenate([dah * g * _gelu_slope(a, h), dah * (a * h)], axis=1)
                u1, u2 = pltpu.roll(dconv, HALF - 1, 0), pltpu.roll(dconv, HALF - 2, 0)
                d1 = jnp.where(row >= HALF - 1, n1, u1)
                d2 = jnp.where(row >= HALF - 2, n2, u2)
                du[half] = dconv * w2 + d1 * w1 + d2 * w0
                s0, s1, s2, s3 = s0 + d2 * x[rows], s1 + d1 * x[rows], s2 + dconv * x[rows], s3 + dconv
                n1, n2 = u1, u2
            du_ref[pl.ds(r0, STRIP), :] = jnp.concatenate(du, axis=0).astype(BF16)
            return n1, n2, s0, s1, s2, s3

        below = carry_ref[...]
        zero = jnp.zeros((HALF, cb2), F32)
        init = (pltpu.roll(below, HALF - 1, 0), pltpu.roll(below, HALF - 2, 0), zero, zero, zero, zero)
        u1, _, s0, s1, s2, s3 = lax.fori_loop(0, n_strips, strip, init, unroll=2)
        carry_ref[...] = pltpu.roll(u1, 1, 0)
        dcw_ref[0:1, :] += jnp.sum(s0, axis=0, keepdims=True)
        dcw_ref[1:2, :] += jnp.sum(s1, axis=0, keepdims=True)
        dcw_ref[2:3, :] += jnp.sum(s2, axis=0, keepdims=True)
        dcw_ref[3:4, :] += jnp.sum(s3, axis=0, keepdims=True)

    rev = lambda j, i: (n_rows - 1 - i, j)
    return pl.pallas_call(
        body, name=name, grid=(N_CONV_BLOCKS, n_rows),
        in_specs=[pl.BlockSpec((TM, CONV_BLOCK), rev), pl.BlockSpec((TM, cb2), rev), pl.BlockSpec((TM, cb2), rev),
                  pl.BlockSpec((8, cb2), lambda j, i: (0, j))],
        out_specs=[pl.BlockSpec((TM, cb2), rev), pl.BlockSpec((8, cb2), lambda j, i: (0, j))],
        out_shape=[jax.ShapeDtypeStruct((LP, D_UP), BF16), jax.ShapeDtypeStruct((8, D_UP), F32)],
        scratch_shapes=[pltpu.VMEM((HALF, cb2), F32)],
        compiler_params=_cparams("arbitrary", "arbitrary"),
    )(dact, conv, u, cw8)


CHUNKS_PER_STEP = 3 if N_CHUNKS % 3 == 0 else 1
STEP_ROWS = CHUNKS_PER_STEP * CHUNK
N_STEPS = N_CHUNKS // CHUNKS_PER_STEP


def _ret_consts(h):
    rows = STEP_ROWS
    lg = math.log(1.0 - 2.0 ** (-5.0 - h))
    ri = lax.broadcasted_iota(jnp.int32, (rows, rows), 0)
    ci = lax.broadcasted_iota(jnp.int32, (rows, rows), 1)
    diff = (ri - ci).astype(F32)
    dmat = jnp.where(diff >= 0, jnp.exp(lg * jnp.maximum(diff, 0.0)), 0.0)
    rowf = lax.broadcasted_iota(jnp.int32, (rows, 1), 0).astype(F32)
    zeta = jnp.exp(lg * (rows - 1.0 - rowf))
    xi = jnp.exp(lg * (rowf + 1.0))
    return dmat, zeta, xi, math.exp(lg * rows)


def _ret_decay_transposed(h):
    lg = math.log(1.0 - 2.0 ** (-5.0 - h))
    ri = lax.broadcasted_iota(jnp.int32, (STEP_ROWS, STEP_ROWS), 0)
    ci = lax.broadcasted_iota(jnp.int32, (STEP_ROWS, STEP_ROWS), 1)
    diff = (ci - ri).astype(F32)
    return jnp.where(diff >= 0, jnp.exp(lg * jnp.maximum(diff, 0.0)), 0.0)


def _rope(t, cosv, sinv):
    return t * cosv + pltpu.roll(t, RET_DK // 2, 1) * sinv


def _unrope(d, cosv, sinv):
    return d * cosv + pltpu.roll(d * sinv, RET_DK // 2, 1)


def _gla_masks():
    ri = lax.broadcasted_iota(jnp.int32, (CHUNK, CHUNK), 0)
    ci = lax.broadcasted_iota(jnp.int32, (CHUNK, CHUNK), 1)
    return dict(ri=ri, ci=ci, tril=(ri >= ci).astype(F32), heads=_head_block_mask(), own=_state_block_mask())


def _gla_common(p_ref, w2_ref, gb_ref, chunk, rows, masks):
    row = lax.broadcasted_iota(jnp.int32, (CHUNK, 1), 0)
    real = (chunk * CHUNK + row) >= PAD_ROWS
    ga = p_ref[rows, O_GA:O_GA + 128]
    z = _dot(ga, w2_ref[...]) + gb_ref[...]
    la = (jnp.minimum(z, 0.0) - jnp.log(1.0 + jnp.exp(-jnp.abs(z)))) * (1.0 / GLA_TAU)
    la = jnp.where(real, la, 0.0)
    ri, ci = masks["ri"], masks["ci"]
    cum = _dot_exact_rhs(masks["tril"], la)
    last = cum[CHUNK - 1:CHUNK, :]
    qs = p_ref[rows, O_GQ:O_GQ + 256] * (GLA_DK ** -0.5)
    k = p_ref[rows, O_GK:O_GK + 256]
    ecum = jnp.exp(cum)
    ekl = jnp.exp(last - cum)
    el = jnp.exp(last)
    refs = [jnp.zeros((1, 256), F32)] + [cum[a * SUB - 1:a * SUB, :] for a in range(1, N_SUB)]
    eq = [jnp.exp(cum[a * SUB:(a + 1) * SUB, :] - refs[a]) for a in range(N_SUB)]
    spread = refs[0] - cum[SUB - 1:SUB, :]
    for a in range(1, N_SUB):
        spread = jnp.maximum(spread, refs[a] - cum[(a + 1) * SUB - 1:(a + 1) * SUB, :])
    small = jnp.max(spread) <= GLA_FACTORED_MAX
    return dict(real=real, row=row, z=z, la=la, cum=cum, last=last, qs=qs, k=k, ecum=ecum, ekl=ekl, el=el,
                refs=refs, eq=eq, small=small, ri=ri, ci=ci, masks=masks)


GLA_FACTORED_MAX = 40.0


def _head_block_mask():
    r = lax.broadcasted_iota(jnp.int32, (CHUNK, 256), 0)
    col = lax.broadcasted_iota(jnp.int32, (CHUNK, 256), 1)
    return (r // SUB) == (col // GLA_DK)


def _state_block_mask():
    r = lax.broadcasted_iota(jnp.int32, (GLA_HEADS * GLA_DK, GLA_HEADS * GLA_DV), 0)
    col = lax.broadcasted_iota(jnp.int32, (GLA_HEADS * GLA_DK, GLA_HEADS * GLA_DV), 1)
    return (r // GLA_DK) == (col // GLA_DV)


def _block_diagonal(blocks):
    zero = jnp.zeros((GLA_DK, GLA_DV), F32)
    return jnp.concatenate([jnp.concatenate([blocks[h] if g == h else zero for g in range(GLA_HEADS)], axis=1)
                            for h in range(GLA_HEADS)], axis=0)


def _gla_factored(c):
    mask = c["masks"]["heads"]
    eks, keys, queries = [], [], []
    for a in range(N_SUB):
        ek = jnp.exp(jnp.minimum(c["refs"][a] - c["cum"], GLA_FACTORED_MAX))
        qh = c["qs"][a * SUB:(a + 1) * SUB, :] * c["eq"][a]
        eks.append(ek)
        keys.append(c["k"] * ek)
        queries.append(jnp.where(mask, jnp.concatenate([qh] * GLA_HEADS, axis=0), 0.0))
    return eks, keys, queries


def _gla_scores_factored(c, factored, p_scr):
    _, keys, queries = factored
    for a in range(N_SUB):
        out = _dot_nt(queries[a], keys[a])
        out = jnp.where(c["ci"] <= a * SUB + (c["ri"] & (SUB - 1)), out, 0.0)
        for h in range(GLA_HEADS):
            p_scr[h, a * SUB:(a + 1) * SUB, :] = out[h * SUB:(h + 1) * SUB, :]


def _gla_intra_bwd_factored(c, factored, dps, dq_scr, dk_scr):
    eks, keys, queries = factored
    mask = c["masks"]["heads"]
    dk = jnp.zeros((CHUNK, 256), F32)
    for a in range(N_SUB):
        dpa = jnp.concatenate([dps[h][a * SUB:(a + 1) * SUB, :] for h in range(GLA_HEADS)], axis=0)
        dq = jnp.where(mask, _dot(dpa, keys[a]), 0.0)
        dq = dq[0:SUB] + dq[SUB:2 * SUB] + dq[2 * SUB:3 * SUB] + dq[3 * SUB:4 * SUB]
        dq_scr[a * SUB:(a + 1) * SUB, :] = dq * c["eq"][a]
        dk = dk + _dot_tn(dpa, queries[a]) * eks[a]
    dk_scr[...] = dk


def _gla_lag_weights(c):
    cum, row = c["cum"], c["row"]
    out = [jnp.ones((CHUNK, 256), F32)]
    for r in range(1, SUB):
        out.append(jnp.where((row % SUB) >= r, jnp.exp(jnp.minimum(cum - pltpu.roll(cum, r, 0), 0.0)), 0.0))
    return out


def _gla_pairwise_keys(c):
    return [None] + [c["k"] * jnp.exp(jnp.minimum(c["refs"][a] - c["cum"], 0.0)) for a in range(1, N_SUB)]


def _gla_scores_pairwise(c, lag_w, keys, h):
    sl = slice(GLA_DK * h, GLA_DK * (h + 1))
    qs, k = c["qs"][:, sl], c["k"][:, sl]
    ri, ci = c["ri"], c["ci"]
    p = jnp.zeros((CHUNK, CHUNK), F32)
    for r in range(SUB):
        kr = k if r == 0 else pltpu.roll(k, r, 0)
        pr = jnp.sum(qs * kr * lag_w[r][:, sl], axis=1, keepdims=True)
        p = p + jnp.where(ci == ri - r, pr, 0.0)
    blocks = [jnp.zeros((SUB, CHUNK), F32)]
    for a in range(1, N_SUB):
        qh = qs[a * SUB:(a + 1) * SUB, :] * c["eq"][a][:, sl]
        blocks.append(jnp.where(ci[:SUB, :] < a * SUB, _dot_nt(qh, keys[a][:, sl]), 0.0))
    return p + jnp.concatenate(blocks, axis=0)


def _gla_all_scores(c, p_scr, factored):
    if factored:
        _gla_scores_factored(c, _gla_factored(c), p_scr)
    else:
        lag_w, keys = _gla_lag_weights(c), _gla_pairwise_keys(c)
        for h in range(GLA_HEADS):
            p_scr[h] = _gla_scores_pairwise(c, lag_w, keys, h)


def _either_form(chunks, run):
    small = chunks[0]["small"]
    for c in chunks[1:]:
        small = jnp.logical_and(small, c["small"])
    pl.when(small)(lambda: run(True))
    pl.when(jnp.logical_not(small))(lambda: run(False))


def _gla_intra_bwd_pairwise(c, lag_w, keys, dp, h):
    sl = slice(GLA_DK * h, GLA_DK * (h + 1))
    qs_h, k_h = c["qs"][:, sl], c["k"][:, sl]
    ri, ci = c["ri"], c["ci"]
    dq_rows = [jnp.zeros((SUB, GLA_DK), F32)]
    dk = jnp.zeros((CHUNK, GLA_DK), F32)
    for a in range(1, N_SUB):
        eq = c["eq"][a][:, sl]
        qh = qs_h[a * SUB:(a + 1) * SUB, :] * eq
        dpa = jnp.where(ci[:SUB, :] < a * SUB, dp[a * SUB:(a + 1) * SUB, :], 0.0)
        dq_rows.append(_dot(dpa, keys[a][:, sl]) * eq)
        ek = jnp.exp(jnp.minimum(c["refs"][a][:, sl] - c["cum"][:, sl], 0.0))
        dk = dk + _dot_tn(dpa, qh) * ek
    dq = jnp.concatenate(dq_rows, axis=0)
    for r in range(SUB):
        w = lag_w[r][:, sl]
        dpr = jnp.sum(jnp.where(ci == ri - r, dp, 0.0), axis=1, keepdims=True)
        kr = k_h if r == 0 else pltpu.roll(k_h, r, 0)
        dq = dq + dpr * kr * w
        back = dpr * qs_h * w
        dk = dk + (back if r == 0 else pltpu.roll(back, CHUNK - r, 0))
    return dq, dk


def _gla_all_intra_bwd(c, dps, p_scr, dq_scr, dk_scr, factored):
    if factored:
        terms = _gla_factored(c)
        _gla_scores_factored(c, terms, p_scr)
        _gla_intra_bwd_factored(c, terms, dps, dq_scr, dk_scr)
    else:
        lag_w, keys = _gla_lag_weights(c), _gla_pairwise_keys(c)
        outs = [_gla_intra_bwd_pairwise(c, lag_w, keys, dps[h], h) for h in range(GLA_HEADS)]
        for h in range(GLA_HEADS):
            p_scr[h] = _gla_scores_pairwise(c, lag_w, keys, h)
        dq_scr[...] = jnp.concatenate([o[0] for o in outs], axis=1)
        dk_scr[...] = jnp.concatenate([o[1] for o in outs], axis=1)


def _mixer_fwd(proj, cos2, sin2, w2p, gb, rnw, gnw, name, carried=()):
    n_carried = len(carried)

    def body(*refs):
        p_ref, c_ref, s_ref, w2_ref, gb_ref, rnw_ref, gnw_ref = refs[:7]
        x_refs, refs = refs[7:7 + n_carried], refs[7 + n_carried:]
        ocat_ref, mrg_ref, sr_out, sg_out = refs[:4]
        gathered_refs, refs = refs[4:4 + n_carried], refs[4 + n_carried:]
        sr, sg, p_scr = refs[:3]
        n = pl.program_id(0)
        if n_carried:
            start, forward, finish = _gather_phases(x_refs, gathered_refs, *refs[3:])
            pl.when(n == 0)(start)
            pl.when(n == (3 * N_STEPS) // 4)(forward)

        @pl.when(n == 0)
        def _():
            sr[...] = jnp.zeros_like(sr)
            sg[...] = jnp.zeros_like(sg)

        sr_out[0] = sr[...]
        cosv, sinv = c_ref[...], s_ref[...]

        for h in range(RET_HEADS):
            dmat, zeta, xi, gc = _ret_consts(h)
            hs = slice(128 * h, 128 * (h + 1))
            q = _rope(p_ref[:, O_RQ + 128 * h:O_RQ + 128 * (h + 1)], cosv, sinv)
            k = _rope(p_ref[:, O_RK + 128 * h:O_RK + 128 * (h + 1)], cosv, sinv) * (RET_DK ** -0.5)
            v = p_ref[:, O_RV + 128 * h:O_RV + 128 * (h + 1)]
            g = p_ref[:, O_RG + 128 * h:O_RG + 128 * (h + 1)]
            s_in = sr[h]
            a = _dot_nt(q, k) * dmat
            o = _dot(a, v) + _dot(q, s_in) * xi
            sr[h] = gc * s_in + _dot_tn(k * zeta, v)
            mu = jnp.mean(o, axis=-1, keepdims=True)
            xc = o - mu
            nrm = xc * lax.rsqrt(jnp.mean(xc * xc, axis=-1, keepdims=True) + EPS)
            ocat_ref[:, hs] = o
            mrg_ref[:, hs] = (nrm * rnw_ref[:, hs] * (g * _sigmoid(g))).astype(BF16)

        row_slices = [slice(CHUNK * j, CHUNK * (j + 1)) for j in range(CHUNKS_PER_STEP)]
        masks = _gla_masks()
        chunks = [_gla_common(p_ref, w2_ref, gb_ref, n * CHUNKS_PER_STEP + j, rows, masks)
                  for j, rows in enumerate(row_slices)]

        def gla_chunks(factored):
            own = masks["own"]
            for j, (rows, c) in enumerate(zip(row_slices, chunks)):
                s_in = sg[...]
                for h in range(GLA_HEADS):
                    sg_out[j, h] = s_in[GLA_DK * h:GLA_DK * (h + 1), GLA_DV * h:GLA_DV * (h + 1)]
                _gla_all_scores(c, p_scr.at[j], factored)
                v_all = p_ref[rows, O_GV:O_GV + GLA_HEADS * GLA_DV]
                o_inter = _dot(c["qs"] * c["ecum"], s_in)
                decay = jnp.exp(_dot_tn_exact_lhs(c["la"], jnp.ones((CHUNK, GLA_HEADS * GLA_DV), F32)))
                sg[...] = decay * s_in + jnp.where(own, _dot_tn(c["k"] * c["ekl"], v_all), 0.0)
                o_intra = _dot(p_scr[j].reshape(GLA_HEADS * CHUNK, CHUNK), v_all)
                for h in range(GLA_HEADS):
                    hs = slice(512 + 128 * h, 512 + 128 * (h + 1))
                    g = p_ref[rows, O_GR + 128 * h:O_GR + 128 * (h + 1)]
                    o = (o_intra[CHUNK * h:CHUNK * (h + 1), GLA_DV * h:GLA_DV * (h + 1)]
                         + o_inter[:, GLA_DV * h:GLA_DV * (h + 1)])
                    nrm = o * lax.rsqrt(jnp.mean(o * o, axis=-1, keepdims=True) + EPS)
                    ocat_ref[rows, hs] = o
                    mrg_ref[rows, hs] = (nrm * gnw_ref[:, 128 * h:128 * (h + 1)] * (g * _sigmoid(g))).astype(BF16)

        _either_form(chunks, gla_chunks)

        if n_carried:
            pl.when(n == N_STEPS - 1)(finish)

    const = lambda shape: pl.BlockSpec(shape, lambda n: (0,) * len(shape))
    anywhere = [pl.BlockSpec(memory_space=pl.ANY)] * n_carried
    return pl.pallas_call(
        body, name=name, grid=(N_STEPS,),
        in_specs=[pl.BlockSpec((STEP_ROWS, IN_WP), lambda n: (n, 0)),
                  pl.BlockSpec((STEP_ROWS, 128), lambda n: (n, 0)), pl.BlockSpec((STEP_ROWS, 128), lambda n: (n, 0)),
                  const((128, 256)), const((1, 256)), const((1, 512)), const((1, 512))] + anywhere,
        out_specs=[pl.BlockSpec((STEP_ROWS, D), lambda n: (n, 0)), pl.BlockSpec((STEP_ROWS, D), lambda n: (n, 0)),
                   pl.BlockSpec((1, RET_HEADS, RET_DK, 128), lambda n: (n, 0, 0, 0)),
                   pl.BlockSpec((CHUNKS_PER_STEP, GLA_HEADS, GLA_DK, GLA_DV), lambda n: (n, 0, 0, 0))] + anywhere,
        out_shape=[jax.ShapeDtypeStruct((LP, D), F32), jax.ShapeDtypeStruct((LP, D), BF16),
                   jax.ShapeDtypeStruct((N_STEPS, RET_HEADS, RET_DK, 128), F32),
                   jax.ShapeDtypeStruct((N_CHUNKS, GLA_HEADS, GLA_DK, GLA_DV), F32)] + _gathered_shapes(carried),
        scratch_shapes=[pltpu.VMEM((RET_HEADS, RET_DK, 128), F32),
                        pltpu.VMEM((GLA_HEADS * GLA_DK, GLA_HEADS * GLA_DV), F32),
                        pltpu.VMEM((CHUNKS_PER_STEP, GLA_HEADS, CHUNK, CHUNK), F32)] + _exchange_sems(n_carried),
        compiler_params=_cparams("arbitrary"),
    )(proj, cos2, sin2, w2p, gb, rnw, gnw, *carried)


def _mixer_bwd(proj, ocat, dmrg, sr_all, sg_all, cos2, sin2, w2p, gb, rnw, gnw, name, carried=()):
    last_step = N_STEPS - 1
    n_carried = len(carried)

    def body(*refs):
        p_ref, ocat_ref, dm_ref, sr_ref, sg_ref, c_ref, s_ref, w2_ref, gb_ref, rnw_ref, gnw_ref = refs[:11]
        g_refs, refs = refs[11:11 + n_carried], refs[11 + n_carried:]
        dp_ref, dw2_ref, dgb_ref, drn_ref, dgn_ref = refs[:5]
        got_refs, refs = refs[5:5 + n_carried], refs[5 + n_carried:]
        dsr, dsg, p_scr, dq_scr, dk_scr = refs[:5]
        step = pl.program_id(0)
        n = last_step - step
        if n_carried:
            start, finish = _exchange_phases(g_refs, got_refs, *refs[5:])
            pl.when(step == 0)(start)

        @pl.when(step == 0)
        def _():
            dsr[...] = jnp.zeros_like(dsr)
            dsg[...] = jnp.zeros_like(dsg)
            dw2_ref[...] = jnp.zeros_like(dw2_ref)
            dgb_ref[...] = jnp.zeros_like(dgb_ref)
            drn_ref[...] = jnp.zeros_like(drn_ref)
            dgn_ref[...] = jnp.zeros_like(dgn_ref)

        cosv, sinv = c_ref[...], s_ref[...]
        step_row = lax.broadcasted_iota(jnp.int32, (STEP_ROWS, 1), 0)
        real = ((n * STEP_ROWS + step_row) >= PAD_ROWS).astype(F32)

        for h in range(RET_HEADS):
            dmat, zeta, xi, gc = _ret_consts(h)
            hs = slice(128 * h, 128 * (h + 1))
            q = _rope(p_ref[:, O_RQ + 128 * h:O_RQ + 128 * (h + 1)], cosv, sinv)
            k = _rope(p_ref[:, O_RK + 128 * h:O_RK + 128 * (h + 1)], cosv, sinv) * (RET_DK ** -0.5)
            v = p_ref[:, O_RV + 128 * h:O_RV + 128 * (h + 1)]
            g = p_ref[:, O_RG + 128 * h:O_RG + 128 * (h + 1)]
            o = ocat_ref[:, hs]
            dy = dm_ref[:, hs]
            wv = rnw_ref[:, hs]
            mu = jnp.mean(o, axis=-1, keepdims=True)
            xc = o - mu
            rs = lax.rsqrt(jnp.mean(xc * xc, axis=-1, keepdims=True) + EPS)
            nrm = xc * rs
            sgm = _sigmoid(g)
            sil = g * sgm
            drn_ref[0:1, hs] += jnp.sum(dy * nrm * sil, axis=0, keepdims=True)
            dgate = dy * nrm * wv * (sgm * (1.0 + g * (1.0 - sgm)))
            dn = dy * wv * sil
            do = rs * (dn - jnp.mean(dn, axis=-1, keepdims=True) - nrm * jnp.mean(dn * nrm, axis=-1, keepdims=True))
            s_in = sr_ref[0, h]
            ds_out = dsr[h]
            dmat_t = _ret_decay_transposed(h)
            a_t = _dot_nt(k, q) * dmat_t
            da = _dot_nt(do, v) * dmat
            da_t = _dot_nt(v, do) * dmat_t
            dox = do * xi
            dq = _dot(da, k) + _dot_nt(dox, s_in)
            dk = _dot(da_t, q) + _dot_nt(v, ds_out) * zeta
            dv = _dot(a_t, do) + _dot(k * zeta, ds_out)
            dsr[h] = gc * ds_out + _dot_tn(q, dox)
            dk = dk * (RET_DK ** -0.5)
            dp_ref[:, O_RQ + 128 * h:O_RQ + 128 * (h + 1)] = (_unrope(dq, cosv, sinv) * real).astype(BF16)
            dp_ref[:, O_RK + 128 * h:O_RK + 128 * (h + 1)] = (_unrope(dk, cosv, sinv) * real).astype(BF16)
            dp_ref[:, O_RV + 128 * h:O_RV + 128 * (h + 1)] = (dv * real).astype(BF16)
            dp_ref[:, O_RG + 128 * h:O_RG + 128 * (h + 1)] = (dgate * real).astype(BF16)

        row_slices = [slice(CHUNK * j, CHUNK * (j + 1)) for j in range(CHUNKS_PER_STEP)]
        masks = _gla_masks()
        chunks = [_gla_common(p_ref, w2_ref, gb_ref, n * CHUNKS_PER_STEP + j, rows, masks)
                  for j, rows in enumerate(row_slices)]

        def gla_chunks(factored):
            for j in reversed(range(CHUNKS_PER_STEP)):
                gla_chunk_bwd(chunks[j], n * CHUNKS_PER_STEP + j, row_slices[j], j, factored, p_ref, ocat_ref, dm_ref,
                              sg_ref, w2_ref, gnw_ref, dp_ref, dw2_ref, dgb_ref, dgn_ref, dsg, p_scr, dq_scr, dk_scr)

        _either_form(chunks, gla_chunks)
        if n_carried:
            pl.when(step == last_step)(finish)

    def gla_chunk_bwd(c, chunk, rows, j, factored, p_ref, ocat_ref, dm_ref, sg_ref, w2_ref, gnw_ref,
                      dp_ref, dw2_ref, dgb_ref, dgn_ref, dsg, p_scr, dq_scr, dk_scr):
        row = lax.broadcasted_iota(jnp.int32, (CHUNK, 1), 0)
        real = ((chunk * CHUNK + row) >= PAD_ROWS).astype(F32)
        ri, ci = c["ri"], c["ci"]
        causal = ri >= ci
        triu = (ci >= ri).astype(F32)
        qe = c["qs"] * c["ecum"]
        kl = c["k"] * c["ekl"]
        v_all = p_ref[rows, O_GV:O_GV + GLA_HEADS * GLA_DV]
        dos, dps = [], []
        for h in range(GLA_HEADS):
            hs = slice(512 + 128 * h, 512 + 128 * (h + 1))
            g = p_ref[rows, O_GR + 128 * h:O_GR + 128 * (h + 1)]
            o = ocat_ref[rows, hs]
            dy = dm_ref[rows, hs]
            wv = gnw_ref[:, 128 * h:128 * (h + 1)]
            rs = lax.rsqrt(jnp.mean(o * o, axis=-1, keepdims=True) + EPS)
            nrm = o * rs
            sgm = _sigmoid(g)
            sil = g * sgm
            dgn_ref[0:1, 128 * h:128 * (h + 1)] += jnp.sum(dy * nrm * sil, axis=0, keepdims=True)
            dgate = dy * nrm * wv * (sgm * (1.0 + g * (1.0 - sgm)))
            dn = dy * wv * sil
            do = rs * (dn - nrm * jnp.mean(dn * nrm, axis=-1, keepdims=True))
            dp_ref[rows, O_GR + 128 * h:O_GR + 128 * (h + 1)] = (dgate * real).astype(BF16)
            dos.append(do)
        do_all = jnp.concatenate(dos, axis=1)
        do_blocks = jnp.where(c["masks"]["own"], jnp.concatenate([do_all] * GLA_HEADS, axis=0), 0.0)
        dp_all = _dot_nt(do_blocks, v_all)
        dps = [jnp.where(causal, dp_all[CHUNK * h:CHUNK * (h + 1), :], 0.0) for h in range(GLA_HEADS)]
        _gla_all_intra_bwd(c, dps, p_scr.at[j], dq_scr.at[j], dk_scr.at[j], factored)
        s_in = _block_diagonal([sg_ref[j, h] for h in range(GLA_HEADS)])
        ds_out = dsg[...]
        decay = jnp.exp(_dot_tn_exact_lhs(c["la"], jnp.ones((CHUNK, GLA_HEADS * GLA_DV), F32)))
        dv_state = _dot(kl, ds_out)
        dqe = _dot_nt(do_all, s_in)
        dkl = _dot_nt(v_all, ds_out)
        dsg[...] = jnp.where(c["masks"]["own"], _dot_tn(qe, do_all), 0.0) + decay * ds_out
        sd = s_in * ds_out
        sd_hi = sd.astype(BF16)
        sd_lo = (sd - sd_hi.astype(F32)).astype(BF16)
        ones8 = jnp.ones((8, GLA_HEADS * GLA_DV), BF16)
        nt = (((1,), (1,)), ((), ()))
        d_el = (lax.dot_general(ones8, sd_hi, nt, preferred_element_type=F32)
                + lax.dot_general(ones8, sd_lo, nt, preferred_element_type=F32))[0:1, :]
        dqs = dqe * c["ecum"] + dq_scr[j]
        dkk = dkl * c["ekl"] + dk_scr[j]
        d_last = jnp.sum(dkl * kl, axis=0, keepdims=True) + d_el * c["el"]
        dcum = c["qs"] * dqs - c["k"] * dkk + jnp.where(row == CHUNK - 1, d_last, 0.0)
        dla = _dot_exact_rhs(triu, dcum)
        dv = _dot_tn(p_scr[j].reshape(GLA_HEADS * CHUNK, CHUNK), do_blocks) + dv_state
        dp_ref[rows, O_GV:O_GV + GLA_HEADS * GLA_DV] = (dv * real).astype(BF16)
        dp_ref[rows, O_GQ:O_GQ + 256] = (dqs * (GLA_DK ** -0.5) * real).astype(BF16)
        dp_ref[rows, O_GK:O_GK + 256] = (dkk * real).astype(BF16)
        dz = dla * (1.0 / GLA_TAU) * _sigmoid(-c["z"]) * real
        ga = p_ref[rows, O_GA:O_GA + 128]
        dp_ref[rows, O_GA:O_GA + 128] = _dot_nt(dz, w2_ref[...]).astype(BF16)
        dp_ref[rows, O_GA + 128:IN_WP] = jnp.zeros((CHUNK, IN_WP - O_GA - 128), BF16)
        dw2_ref[...] += _dot_tn(ga, dz)
        dgb_ref[0:1, :] += jnp.sum(dz, axis=0, keepdims=True)

    const = lambda shape: pl.BlockSpec(shape, lambda s: (0,) * len(shape))
    rev = lambda s: (last_step - s, 0)
    anywhere = [pl.BlockSpec(memory_space=pl.ANY)] * n_carried
    return pl.pallas_call(
        body, name=name, grid=(N_STEPS,),
        in_specs=[pl.BlockSpec((STEP_ROWS, IN_WP), rev), pl.BlockSpec((STEP_ROWS, D), rev),
                  pl.BlockSpec((STEP_ROWS, D), rev),
                  pl.BlockSpec((1, RET_HEADS, RET_DK, 128), lambda s: (last_step - s, 0, 0, 0)),
                  pl.BlockSpec((CHUNKS_PER_STEP, GLA_HEADS, GLA_DK, GLA_DV), lambda s: (last_step - s, 0, 0, 0)),
                  pl.BlockSpec((STEP_ROWS, 128), rev), pl.BlockSpec((STEP_ROWS, 128), rev),
                  const((128, 256)), const((1, 256)), const((1, 512)), const((1, 512))] + anywhere,
        out_specs=[pl.BlockSpec((STEP_ROWS, IN_WP), rev), const((128, 256)), const((8, 256)),
                   const((8, 512)), const((8, 512))] + anywhere,
        out_shape=[jax.ShapeDtypeStruct((LP, IN_WP), BF16), jax.ShapeDtypeStruct((128, 256), F32),
                   jax.ShapeDtypeStruct((8, 256), F32), jax.ShapeDtypeStruct((8, 512), F32),
                   jax.ShapeDtypeStruct((8, 512), F32)] + [jax.ShapeDtypeStruct(g.shape, g.dtype) for g in carried],
        scratch_shapes=[pltpu.VMEM((RET_HEADS, RET_DK, 128), F32),
                        pltpu.VMEM((GLA_HEADS * GLA_DK, GLA_HEADS * GLA_DV), F32),
                        pltpu.VMEM((CHUNKS_PER_STEP, GLA_HEADS, CHUNK, CHUNK), F32),
                        pltpu.VMEM((CHUNKS_PER_STEP, CHUNK, 256), F32),
                        pltpu.VMEM((CHUNKS_PER_STEP, CHUNK, 256), F32)] + _exchange_sems(n_carried),
        compiler_params=_cparams("arbitrary"),
    )(proj, ocat, dmrg, sr_all, sg_all, cos2, sin2, w2p, gb, rnw, gnw, *carried)


def _all_gather(xs, name):
    n = len(xs)

    def body(*refs):
        start, forward, finish = _gather_phases(refs[:n], refs[n:2 * n], *refs[2 * n:])
        start()
        forward()
        finish()

    return pl.pallas_call(
        body, name=name,
        in_specs=[pl.BlockSpec(memory_space=pl.ANY)] * n,
        out_specs=[pl.BlockSpec(memory_space=pl.ANY)] * n,
        out_shape=_gathered_shapes(xs),
        scratch_shapes=_exchange_sems(n),
    )(*xs)


def _gathered_shapes(xs):
    return [jax.ShapeDtypeStruct((N_DEV,) + x.shape, x.dtype) for x in xs]


def _exchange_sems(n):
    if n == 0:
        return []
    return [pltpu.SemaphoreType.DMA((7 * n,)), pltpu.SemaphoreType.DMA((7 * n,)), pltpu.SemaphoreType.DMA((n,))]


def _gather_phases(x_refs, out_refs, send_sems, recv_sems, local_sems):
    n = len(x_refs)
    mx, my, mc = lax.axis_index("x"), lax.axis_index("y"), lax.axis_index("c")
    me, sibling = (mx, my, mc), (mx, my, 1 - mc)
    chips = [(1 - mx, my), (mx, 1 - my), (1 - mx, 1 - my)]

    def slot(a, px, py, pc):
        return out_refs[a].at[4 * px + 2 * py + pc]

    def copy(a, k, block, to, src=None):
        return pltpu.make_async_remote_copy(
            src_ref=slot(a, *block) if src is None else src, dst_ref=slot(a, *block),
            send_sem=send_sems.at[7 * a + k], recv_sem=recv_sems.at[7 * a + k],
            device_id=to, device_id_type=MESH_IDS)

    mine = [pltpu.make_async_copy(x_refs[a], slot(a, *me), local_sems.at[a]) for a in range(n)]
    first = []
    for a in range(n):
        first.append(copy(a, 0, me, sibling, src=x_refs[a]))
        first += [copy(a, 1 + j, me, (*chip, mc), src=x_refs[a]) for j, chip in enumerate(chips)]
    passed = [copy(a, 4 + j, (*chip, mc), sibling) for j, chip in enumerate(chips) for a in range(n)]

    def start():
        for cp in mine + first:
            cp.start()

    def forward():
        for j, chip in enumerate(chips):
            for a in range(n):
                copy(a, 1 + j, (*chip, mc), me).wait_recv()
                passed[j * n + a].start()

    def finish():
        for a in range(n):
            copy(a, 0, sibling, me).wait_recv()
            for j, chip in enumerate(chips):
                copy(a, 4 + j, (*chip, 1 - mc), me).wait_recv()
        for cp in first + passed:
            cp.wait_send()
        for cp in mine:
            cp.wait()

    return start, forward, finish


def _exchange_blocks(gs, name):
    n = len(gs)

    def body(*refs):
        start, finish = _exchange_phases(refs[:n], refs[n:2 * n], *refs[2 * n:])
        start()
        finish()

    return pl.pallas_call(
        body, name=name,
        in_specs=[pl.BlockSpec(memory_space=pl.ANY)] * n,
        out_specs=[pl.BlockSpec(memory_space=pl.ANY)] * n,
        out_shape=[jax.ShapeDtypeStruct(g.shape, g.dtype) for g in gs],
        scratch_shapes=_exchange_sems(n),
    )(*gs)


def _exchange_phases(g_refs, out_refs, send_sems, recv_sems, local_sems):
    n = len(g_refs)
    mx, my, mc = lax.axis_index("x"), lax.axis_index("y"), lax.axis_index("c")
    me = 4 * mx + 2 * my + mc
    mine = [pltpu.make_async_copy(g_refs[a].at[me], out_refs[a].at[me], local_sems.at[a]) for a in range(n)]
    copies = []
    for r in range(1, N_DEV):
        px, py, pc = mx ^ (r >> 2), my ^ ((r >> 1) & 1), mc ^ (r & 1)
        peer = 4 * px + 2 * py + pc
        for a in range(n):
            copies.append(pltpu.make_async_remote_copy(
                src_ref=g_refs[a].at[peer], dst_ref=out_refs[a].at[me],
                send_sem=send_sems.at[7 * a + r - 1], recv_sem=recv_sems.at[7 * a + r - 1],
                device_id=(px, py, pc), device_id_type=MESH_IDS))

    def start():
        for cp in mine + copies:
            cp.start()

    def finish():
        for cp in copies:
            cp.wait_recv()
        for cp in copies:
            cp.wait_send()
        for cp in mine:
            cp.wait()

    return start, finish


IN_SHARD = IN_W // N_DEV
IN_SHARD_P = 512
UP_SHARD = D_UP // N_DEV
UP_SHARD_P = 768
RELAYOUT_ROWS = 256


def _pieces_w_in():
    return [(k, 0, IN_SHARD * k, IN_SHARD) for k in range(N_DEV)]


def _pieces_ffn_up():
    pieces = []
    for k in range(N_DEV):
        n, end = UP_SHARD * k, UP_SHARD * (k + 1)
        while n < end:
            half, r = divmod(n, D_FF)
            blk, off = divmod(r, CONV_BLOCK)
            run = min(CONV_BLOCK - off, end - n)
            pieces.append((k, n - UP_SHARD * k, 2 * CONV_BLOCK * blk + CONV_BLOCK * half + off, run))
            n += run
    return pieces


def _assemble_block(load, spans, dst_block, rows):
    lo = 128 * dst_block
    lane = lax.broadcasted_iota(jnp.int32, (1, 128), 1)
    out = jnp.zeros((rows, 128), F32)
    for key, src_off, dst_off, length in spans:
        a, b = max(lo, dst_off), min(lo + 128, dst_off + length)
        s, s_end = src_off + (a - dst_off), src_off + (b - dst_off)
        d = a
        while s < s_end:
            e = min(s_end, 128 * (s // 128 + 1))
            blk = load(key, s // 128)
            shift = (d - s) % 128
            if shift:
                blk = pltpu.roll(blk, shift, 1)
            out = jnp.where((lane >= d - lo) & (lane < d - lo + (e - s)), blk, out)
            d += e - s
            s = e
    return out


def _shards_to_cols(shards, pieces, width, name):
    _, rows, _ = shards.shape
    tr = RELAYOUT_ROWS

    def body(s_ref, o_ref):
        load = lambda k, b: s_ref[k, :, 128 * b:128 * (b + 1)].astype(F32)
        for db in range(width // 128):
            o_ref[:, 128 * db:128 * (db + 1)] = _assemble_block(load, pieces, db, tr).astype(BF16)

    return pl.pallas_call(
        body, name=name, grid=(rows // tr,),
        in_specs=[pl.BlockSpec((N_DEV, tr, shards.shape[2]), lambda i: (0, i, 0))],
        out_specs=pl.BlockSpec((tr, width), lambda i: (i, 0)),
        out_shape=jax.ShapeDtypeStruct((rows, width), BF16),
        compiler_params=_cparams("parallel"),
    )(shards)


def _cols_to_shards(full, pieces, shard_width, name):
    rows, width = full.shape
    tr = RELAYOUT_ROWS

    def body(f_ref, o_ref):
        load = lambda _, b: f_ref[:, 128 * b:128 * (b + 1)].astype(F32)
        for k in range(N_DEV):
            spans = [(None, dst_off, src_off, length) for dev, src_off, dst_off, length in pieces if dev == k]
            for db in range(shard_width // 128):
                o_ref[k, :, 128 * db:128 * (db + 1)] = _assemble_block(load, spans, db, tr).astype(BF16)

    return pl.pallas_call(
        body, name=name, grid=(rows // tr,),
        in_specs=[pl.BlockSpec((tr, width), lambda i: (i, 0))],
        out_specs=pl.BlockSpec((N_DEV, tr, shard_width), lambda i: (0, i, 0)),
        out_shape=jax.ShapeDtypeStruct((N_DEV, rows, shard_width), BF16),
        compiler_params=_cparams("parallel"),
    )(full)


def _adamw(parts, w, m, v, rows_per_step, name):
    rows, cols = w.shape
    assert rows % rows_per_step == 0 and parts.shape == (N_DEV, rows, cols)

    def body(p_ref, w_ref, m_ref, v_ref, g_ref, d_ref, nm_ref, nv_ref):
        g = p_ref[0].astype(F32)
        for j in range(1, N_DEV):
            g = g + p_ref[j].astype(F32)
        m_new = ADAM_B1 * m_ref[...] + (1.0 - ADAM_B1) * g
        v_new = ADAM_B2 * v_ref[...] + (1.0 - ADAM_B2) * (g * g)
        m_hat = m_new / (1.0 - ADAM_B1 ** ADAM_STEP)
        v_hat = v_new / (1.0 - ADAM_B2 ** ADAM_STEP)
        g_ref[...] = g
        d_ref[...] = -ADAM_LR * (m_hat / (jnp.sqrt(v_hat) + ADAM_EPS) + ADAM_WD * w_ref[...])
        nm_ref[...] = m_new
        nv_ref[...] = v_new

    tile = pl.BlockSpec((rows_per_step, cols), lambda i: (i, 0))
    shape = jax.ShapeDtypeStruct((rows, cols), F32)
    return pl.pallas_call(
        body, name=name, grid=(rows // rows_per_step,),
        in_specs=[pl.BlockSpec((N_DEV, rows_per_step, cols), lambda i: (0, i, 0)), tile, tile, tile],
        out_specs=[tile, tile, tile, tile],
        out_shape=[shape, shape, shape, shape],
        compiler_params=_cparams("parallel"),
    )(parts, w, m, v)


BIG = (("w_in", (DEPTH, D, IN_W // N_DEV), 2), ("w_out", (DEPTH, D // N_DEV, D), 1),
       ("ffn_up", (DEPTH, D, D_UP // N_DEV), 2), ("ffn_down", (DEPTH, D_FF // N_DEV, D), 1))
SMALL = (("meta_tokens", (N_META, D // N_DEV), 1), ("gla_gate_w2", (DEPTH, GATE_RANK, 256 // N_DEV), 2),
         ("ffn_conv_w", (DEPTH, 3, D_UP // N_DEV), 2))
REPL = (("pre_mix_norm", (DEPTH, D)), ("gla_gate_b", (DEPTH, 256)), ("ret_norm_w", (DEPTH, 512)),
        ("gla_norm_w", (DEPTH, 512)), ("post_mix_norm", (DEPTH, D)), ("pre_ffn_norm", (DEPTH, D)),
        ("ffn_conv_b", (DEPTH, D_UP)), ("post_ffn_norm", (DEPTH, D)))
WEIGHT_ORDER = ("meta_tokens", "pre_mix_norm", "w_in", "gla_gate_w2", "gla_gate_b", "ret_norm_w", "gla_norm_w",
                "w_out", "post_mix_norm", "pre_ffn_norm", "ffn_up", "ffn_conv_w", "ffn_conv_b", "ffn_down",
                "post_ffn_norm")


def _size(shape):
    return math.prod(shape)


def _round_up(n, mult):
    return -(-n // mult) * mult


REPL_ROWS = _round_up(-(-sum(_size(s) for _, s in REPL) // LANES), 8)
SMALL_ROWS = _round_up(-(-sum(_size(s) for _, s, _ in SMALL) // LANES), 8)


def _pack(arrays, rows, dtype):
    flat = jnp.concatenate([a.reshape(-1).astype(dtype) for a in arrays])
    return jnp.pad(flat, (0, rows * LANES - flat.shape[0])).reshape(rows, LANES)


def _unpack(buf, shapes):
    flat = buf.reshape(-1)
    out, off = [], 0
    for shape in shapes:
        out.append(flat[off:off + _size(shape)].reshape(shape))
        off += _size(shape)
    return out


def _unshard(blocks, axis):
    moved = jnp.moveaxis(blocks, 0, axis)
    shape = list(moved.shape)
    shape[axis:axis + 2] = [shape[axis] * shape[axis + 1]]
    return moved.reshape(shape)


def _to_blocks(full, axis):
    shape = list(full.shape)
    shape[axis:axis + 1] = [N_DEV, shape[axis] // N_DEV]
    return jnp.moveaxis(full.reshape(shape), axis, 0)


def _interleave_cols(w):
    lead = w.shape[:-1]
    return jnp.swapaxes(w.reshape(lead + (2, N_CONV_BLOCKS, CONV_BLOCK)), -3, -2).reshape(lead + (D_UP,))


def _deinterleave_cols(w):
    lead = w.shape[:-1]
    return jnp.swapaxes(w.reshape(lead + (N_CONV_BLOCKS, 2, CONV_BLOCK)), -3, -2).reshape(lead + (D_UP,))


def _rope_tables():
    half = RET_DK // 2
    inv = ROPE_BASE ** (-jnp.arange(half, dtype=F32) / half)
    pos = jnp.arange(LP, dtype=F32) - float(PAD_ROWS)
    ang = pos[:, None] * inv[None, :]
    c, s = jnp.cos(ang), jnp.sin(ang)
    return jnp.concatenate([c, c], axis=1), jnp.concatenate([-s, s], axis=1)


def kernel(x, meta_tokens, pre_mix_norm, w_in, gla_gate_w2, gla_gate_b, ret_norm_w, gla_norm_w, w_out, post_mix_norm, pre_ffn_norm, ffn_up, ffn_conv_w, ffn_conv_b, ffn_down, post_ffn_norm, loss_target, m_meta_tokens, m_pre_mix_norm, m_w_in, m_gla_gate_w2, m_gla_gate_b, m_ret_norm_w, m_gla_norm_w, m_w_out, m_post_mix_norm, m_pre_ffn_norm, m_ffn_up, m_ffn_conv_w, m_ffn_conv_b, m_ffn_down, m_post_ffn_norm, v_meta_tokens, v_pre_mix_norm, v_w_in, v_gla_gate_w2, v_gla_gate_b, v_ret_norm_w, v_gla_norm_w, v_w_out, v_post_mix_norm, v_pre_ffn_norm, v_ffn_up, v_ffn_conv_w, v_ffn_conv_b, v_ffn_down, v_post_ffn_norm):
    weights = dict(meta_tokens=meta_tokens, pre_mix_norm=pre_mix_norm, w_in=w_in, gla_gate_w2=gla_gate_w2,
                   gla_gate_b=gla_gate_b, ret_norm_w=ret_norm_w, gla_norm_w=gla_norm_w, w_out=w_out,
                   post_mix_norm=post_mix_norm, pre_ffn_norm=pre_ffn_norm, ffn_up=ffn_up, ffn_conv_w=ffn_conv_w,
                   ffn_conv_b=ffn_conv_b, ffn_down=ffn_down, post_ffn_norm=post_ffn_norm)
    mom1 = dict(meta_tokens=m_meta_tokens, pre_mix_norm=m_pre_mix_norm, w_in=m_w_in, gla_gate_w2=m_gla_gate_w2,
                gla_gate_b=m_gla_gate_b, ret_norm_w=m_ret_norm_w, gla_norm_w=m_gla_norm_w, w_out=m_w_out,
                post_mix_norm=m_post_mix_norm, pre_ffn_norm=m_pre_ffn_norm, ffn_up=m_ffn_up,
                ffn_conv_w=m_ffn_conv_w, ffn_conv_b=m_ffn_conv_b, ffn_down=m_ffn_down, post_ffn_norm=m_post_ffn_norm)
    mom2 = dict(meta_tokens=v_meta_tokens, pre_mix_norm=v_pre_mix_norm, w_in=v_w_in, gla_gate_w2=v_gla_gate_w2,
                gla_gate_b=v_gla_gate_b, ret_norm_w=v_ret_norm_w, gla_norm_w=v_gla_norm_w, w_out=v_w_out,
                post_mix_norm=v_post_mix_norm, pre_ffn_norm=v_pre_ffn_norm, ffn_up=v_ffn_up,
                ffn_conv_w=v_ffn_conv_w, ffn_conv_b=v_ffn_conv_b, ffn_down=v_ffn_down, post_ffn_norm=v_post_ffn_norm)

    pad_cols = lambda a, width: jnp.pad(a, ((0, 0), (0, width - a.shape[1])))
    big_names = [n for n, _, _ in BIG]
    shard = {}
    for l in range(DEPTH):
        shard[l, "w_in"] = pad_cols(w_in[l].astype(BF16), IN_SHARD_P)
        shard[l, "w_out"] = w_out[l].astype(BF16)
        shard[l, "ffn_up"] = pad_cols(ffn_up[l].astype(BF16), UP_SHARD_P)
        shard[l, "ffn_down"] = ffn_down[l].astype(BF16)
    gathered = {(0, "w_in"): _all_gather([shard[0, "w_in"]], "gather_w_in_0")[0]}
    gather_in_mixer = {l: [(l, n) for n in big_names[1:]] for l in range(DEPTH)}
    gather_in_conv = {l: [(l + 1, "w_in")] for l in range(DEPTH - 1)}
    small = _all_gather([_pack([weights[n] for n, _, _ in SMALL], SMALL_ROWS, F32)], "gather_small_weights")[0]
    small_parts = _unpack_blocks(small, [s for _, s, _ in SMALL])
    full = {n: _unshard(p, ax) for (n, _, ax), p in zip(SMALL, small_parts)}
    w2p = jnp.pad(full["gla_gate_w2"], ((0, 0), (0, 128 - GATE_RANK), (0, 0)))
    cw8 = jnp.concatenate([_interleave_cols(full["ffn_conv_w"]), _interleave_cols(ffn_conv_b)[:, None, :],
                           jnp.zeros((DEPTH, 4, D_UP), F32)], axis=1)
    cos2, sin2 = _rope_tables()

    h = jnp.concatenate([jnp.zeros((PAD_ROWS, D), F32), full["meta_tokens"], x[0]], axis=0)
    target = jnp.concatenate([jnp.zeros((CHUNK, D), F32), loss_target[0]], axis=0)
    saved, layer_w = [], []
    for l in range(DEPTH):
        lw = dict(w_in=_shards_to_cols(gathered[l, "w_in"], _pieces_w_in(), IN_WP, f"w_in_cols_{l}"))
        a1, proj = _norm_matmul(h, pre_mix_norm[l:l + 1], lw["w_in"], out_dtype=F32, tm=TM_BIG, tn=IN_WP // 3,
                                name=f"in_proj_{l}")
        keys = gather_in_mixer.get(l, [])
        ocat, merged, sr_all, sg_all, *got = _mixer_fwd(proj, cos2, sin2, w2p[l], gla_gate_b[l:l + 1],
                                                        ret_norm_w[l:l + 1], gla_norm_w[l:l + 1], f"mixer_fwd_{l}",
                                                        carried=[shard[key] for key in keys])
        gathered.update(zip(keys, got))
        lw["w_out"] = gathered[l, "w_out"].reshape(D, D)
        lw["w_up"] = _shards_to_cols(gathered[l, "ffn_up"], _pieces_ffn_up(), D_UP, f"ffn_up_cols_{l}")
        lw["w_down"] = gathered[l, "ffn_down"].reshape(D_FF, D)
        layer_w.append(lw)
        m, h1 = _matmul_resid_norm(merged, lw["w_out"], h, post_mix_norm[l:l + 1], f"out_proj_{l}")
        a2, u = _norm_matmul(h1, pre_ffn_norm[l:l + 1], lw["w_up"], out_dtype=BF16, tm=TM_BIG, tn=D_UP // 4,
                             name=f"ffn_up_{l}")
        keys = gather_in_conv.get(l, [])
        cv, act, *got = _conv_act_fwd(u, cw8[l], f"ffn_conv_act_{l}", carried=[shard[key] for key in keys])
        gathered.update(zip(keys, got))
        f, h2, *loss_acc = _matmul_resid_norm(act, lw["w_down"], h1, post_ffn_norm[l:l + 1], f"ffn_down_{l}",
                                              target=target if l == DEPTH - 1 else None)
        saved.append(dict(h=h, a1=a1, proj=proj, ocat=ocat, merged=merged, sr=sr_all, sg=sg_all, m=m, h1=h1,
                          a2=a2, u=u, cv=cv, act=act, f=f))
        h = h2

    dh = h
    loss = lax.psum(loss_acc[0][0, 0], ("x", "y", "c"))

    kinds = ("grad", "delta", "new_m", "new_v")
    grads = {n: [None] * DEPTH for n in WEIGHT_ORDER if n != "meta_tokens" and n not in big_names}
    pending, parts = [], {}
    for l in reversed(range(DEPTH)):
        s, lw = saved[l], layer_w[l]
        dact, df, g_post_ffn = _norm_bwd_matmul(dh, s["f"], post_ffn_norm[l:l + 1], lw["w_down"], BF16,
                                                f"ffn_down_dx_{l}")
        g_down = _matmul(s["act"], df, ta=True, out_dtype=BF16, tm=D_FF // 2, tn=D, tk=TM_BIG, name=f"ffn_down_dw_{l}")
        du, dcw = _conv_act_bwd(dact, s["cv"], s["u"], cw8[l], f"ffn_conv_act_bwd_{l}")
        dh1, g_pre_ffn = _matmul_norm_bwd(du, lw["w_up"], s["h1"], pre_ffn_norm[l:l + 1], dh, D_FF, f"ffn_up_dx_{l}")
        g_up = _matmul(s["a2"], du, ta=True, out_dtype=BF16, tm=D, tn=D_FF, tk=TM_BIG, name=f"ffn_up_dw_{l}")
        dmerged, dm, g_post_mix = _norm_bwd_matmul(dh1, s["m"], post_mix_norm[l:l + 1], lw["w_out"], F32,
                                                   f"out_proj_dx_{l}")
        g_out = _matmul(s["merged"], dm, ta=True, out_dtype=BF16, tm=D, tn=D, tk=TM_BIG, name=f"out_proj_dw_{l}")
        pending += [((l, "ffn_down"), g_down.reshape(N_DEV, D_FF // N_DEV, D)),
                    ((l, "ffn_up"), _cols_to_shards(g_up, _pieces_ffn_up(), UP_SHARD_P, f"ffn_up_grad_shards_{l}")),
                    ((l, "w_out"), g_out.reshape(N_DEV, D // N_DEV, D))]
        dproj, g_w2, g_gb, g_rn, g_gn, *got = _mixer_bwd(s["proj"], s["ocat"], dmerged, s["sr"], s["sg"], cos2, sin2,
                                                         w2p[l], gla_gate_b[l:l + 1], ret_norm_w[l:l + 1],
                                                         gla_norm_w[l:l + 1], f"mixer_bwd_{l}",
                                                         carried=[blocks for _, blocks in pending])
        parts.update(zip([key for key, _ in pending], got))
        g_in = _matmul(s["a1"], dproj, ta=True, out_dtype=BF16, tm=D, tn=IN_WP // 2, tk=TM_BIG, name=f"in_proj_dw_{l}")
        pending = [((l, "w_in"), _cols_to_shards(g_in, _pieces_w_in(), IN_SHARD_P, f"w_in_grad_shards_{l}"))]
        now = pending if l == 0 else []
        dh, g_pre_mix, *got = _matmul_norm_bwd(dproj, lw["w_in"], s["h"], pre_mix_norm[l:l + 1], dh1, IN_WP,
                                               f"in_proj_dx_{l}", carried=[blocks for _, blocks in now])
        parts.update(zip([key for key, _ in now], got))
        pending = [] if l == 0 else pending
        grads["post_ffn_norm"][l] = g_post_ffn[0]
        grads["ffn_conv_w"][l] = _deinterleave_cols(dcw[0:3])
        grads["ffn_conv_b"][l] = _deinterleave_cols(dcw[3])
        grads["pre_ffn_norm"][l] = g_pre_ffn[0]
        grads["post_mix_norm"][l] = g_post_mix[0]
        grads["gla_gate_w2"][l] = g_w2[:GATE_RANK]
        grads["gla_gate_b"][l] = g_gb[0]
        grads["ret_norm_w"][l] = g_rn[0]
        grads["gla_norm_w"][l] = g_gn[0]
        grads["pre_mix_norm"][l] = g_pre_mix[0]
    local = {n: jnp.stack(v) for n, v in grads.items()}
    local["meta_tokens"] = dh[PAD_ROWS:CHUNK]
    grad_x = dh[CHUNK:][None]

    blocks = jnp.concatenate([_to_blocks(local[n], ax).reshape(N_DEV, -1) for n, _, ax in SMALL], axis=1)
    blocks = jnp.pad(blocks, ((0, 0), (0, SMALL_ROWS * LANES - blocks.shape[1]))).reshape(N_DEV, SMALL_ROWS, LANES)
    *got, small_grad_parts = _exchange_blocks([b for _, b in pending] + [blocks], "exchange_last_grads")
    parts.update(zip([key for key, _ in pending], got))

    widths = dict(w_in=IN_SHARD_P, w_out=D, ffn_up=UP_SHARD_P, ffn_down=D)
    steps = dict(w_in=256, w_out=D // N_DEV, ffn_up=256, ffn_down=D_FF // N_DEV // 2)
    big_out = {kind: {n: [None] * DEPTH for n in big_names} for kind in kinds}
    for l in range(DEPTH):
        for n in big_names:
            mine = [pad_cols(d[n][l], widths[n]) for d in (weights, mom1, mom2)]
            results = _adamw(parts[l, n], *mine, steps[n], f"adamw_{n}_{l}")
            for kind, r in zip(kinds, results):
                big_out[kind][n][l] = r[:, :weights[n].shape[2]]
    out = {kind: {n: jnp.stack(v) for n, v in big_out[kind].items()} for kind in kinds}
    shard_shapes = [s for _, s, _ in SMALL]
    packed = [_pack([d[n] for n, _, _ in SMALL], SMALL_ROWS, F32) for d in (weights, mom1, mom2)]
    results = _adamw(small_grad_parts, *packed, SMALL_ROWS, "adamw_small_sharded")
    for kind, buf in zip(kinds, results):
        out[kind].update(zip([n for n, _, _ in SMALL], _unpack(buf, shard_shapes)))

    repl_parts = _all_gather([_pack([local[n] for n, _ in REPL], REPL_ROWS, F32)], "gather_small_grads")[0]
    packed = [_pack([d[n] for n, _ in REPL], REPL_ROWS, F32) for d in (weights, mom1, mom2)]
    results = _adamw(repl_parts, *packed, REPL_ROWS, "adamw_replicated")
    repl_shapes = [s for _, s in REPL]
    for kind, buf in zip(kinds, results):
        out[kind].update(zip([n for n, _ in REPL], _unpack(buf, repl_shapes)))

    return (loss, grad_x, *[out["grad"][n] for n in WEIGHT_ORDER], *[out["delta"][n] for n in WEIGHT_ORDER],
            *[out["new_m"][n] for n in WEIGHT_ORDER], *[out["new_v"][n] for n in WEIGHT_ORDER])


def _unpack_blocks(gathered, shapes):
    flat = gathered.reshape(N_DEV, -1)
    out, off = [], 0
    for shape in shapes:
        out.append(flat[:, off:off + _size(shape)].reshape((N_DEV,) + shape))
        off += _size(shape)
    return out
```

```python
import math

import jax
import jax.numpy as jnp
from jax import lax
from jax.experimental import pallas as pl
from jax.experimental.pallas import tpu as pltpu

F32 = jnp.float32
BF16 = jnp.bfloat16

D = 1024
SEQ = 8192
DEPTH = 2
N_META = 16
CHUNK = 64
SUB = 16
N_SUB = CHUNK // SUB
PAD_ROWS = CHUNK - N_META
LP = SEQ + CHUNK
N_CHUNKS = LP // CHUNK
RET_HEADS = 4
RET_DK = 128
GLA_HEADS = 4
GLA_DK = 64
GLA_DV = 128
GLA_TAU = 16.0
GATE_RANK = 16
IN_W = 3600
IN_WP = 3840
D_FF = 2816
D_UP = 2 * D_FF
CONV_BLOCK = 256
N_CONV_BLOCKS = D_FF // CONV_BLOCK
ROPE_BASE = 10000.0
EPS = 1e-6
N_DEV = 8
LANES = 1024

O_RQ, O_RK, O_RV, O_RG = 0, 512, 1024, 1536
O_GQ, O_GK, O_GV, O_GR, O_GA = 2048, 2304, 2560, 3072, 3584

ADAM_LR = 0.001
ADAM_B1 = 0.9
ADAM_B2 = 0.999
ADAM_EPS = 1e-08
ADAM_WD = 0.01
ADAM_STEP = 10

VMEM_LIMIT = 56 * 1024 * 1024
MESH_IDS = pl.DeviceIdType.MESH


def _row_tile(rows, limit):
    best = 16
    for t in range(16, min(rows, limit) + 1, 16):
        if rows % t == 0:
            best = t
    return best


TM = _row_tile(LP, 688)
TM_BIG = _row_tile(LP, 1376)


def _cparams(*sem):
    return pltpu.CompilerParams(dimension_semantics=sem, vmem_limit_bytes=VMEM_LIMIT)


def _dot(a, b):
    return jnp.dot(a.astype(BF16), b.astype(BF16), preferred_element_type=F32)


def _dot_nt(a, b):
    return lax.dot_general(a.astype(BF16), b.astype(BF16), (((1,), (1,)), ((), ())), preferred_element_type=F32)


def _dot_tn(a, b):
    return lax.dot_general(a.astype(BF16), b.astype(BF16), (((0,), (0,)), ((), ())), preferred_element_type=F32)


def _split3(x):
    hi = x.astype(BF16)
    r1 = x - hi.astype(F32)
    mid = r1.astype(BF16)
    lo = (r1 - mid.astype(F32)).astype(BF16)
    return hi, mid, lo


def _dot_exact_rhs(t, x):
    n = x.shape[1]
    parts = jnp.dot(t.astype(BF16), jnp.concatenate(_split3(x), axis=1), preferred_element_type=F32)
    return parts[:, :n] + parts[:, n:2 * n] + parts[:, 2 * n:]


def _dot_tn_exact_lhs(x, ones):
    n = x.shape[1]
    parts = lax.dot_general(jnp.concatenate(_split3(x), axis=1), ones.astype(BF16), (((0,), (0,)), ((), ())),
                            preferred_element_type=F32)
    return parts[:n] + parts[n:2 * n] + parts[2 * n:]


def _sigmoid(x):
    return 1.0 / (1.0 + jnp.exp(-x))


def _matmul(a, b, *, ta=False, tb=False, out_dtype, tm, tn, tk, name):
    m = a.shape[1] if ta else a.shape[0]
    k = a.shape[0] if ta else a.shape[1]
    n = b.shape[0] if tb else b.shape[1]
    assert (b.shape[1] if tb else b.shape[0]) == k
    assert m % tm == 0 and n % tn == 0 and k % tk == 0, (name, m, n, k, tm, tn, tk)
    nk = k // tk
    a_spec = pl.BlockSpec((tk, tm), lambda i, j, kk: (kk, i)) if ta else pl.BlockSpec((tm, tk), lambda i, j, kk: (i, kk))
    b_spec = pl.BlockSpec((tn, tk), lambda i, j, kk: (j, kk)) if tb else pl.BlockSpec((tk, tn), lambda i, j, kk: (kk, j))
    dims = (((0 if ta else 1,), (1 if tb else 0,)), ((), ()))

    def body(a_ref, b_ref, o_ref, *acc):
        prod = lax.dot_general(a_ref[...].astype(BF16), b_ref[...].astype(BF16), dims, preferred_element_type=F32)
        if nk == 1:
            o_ref[...] = prod.astype(out_dtype)
            return
        acc_ref, = acc
        kk = pl.program_id(2)

        @pl.when(kk == 0)
        def _():
            acc_ref[...] = prod

        @pl.when(kk > 0)
        def _():
            acc_ref[...] += prod

        @pl.when(kk == nk - 1)
        def _():
            o_ref[...] = acc_ref[...].astype(out_dtype)

    return pl.pallas_call(
        body, name=name, grid=(m // tm, n // tn, nk),
        in_specs=[a_spec, b_spec],
        out_specs=pl.BlockSpec((tm, tn), lambda i, j, kk: (i, j)),
        out_shape=jax.ShapeDtypeStruct((m, n), out_dtype),
        scratch_shapes=[pltpu.VMEM((tm, tn), F32)] if nk > 1 else [],
        compiler_params=_cparams("parallel", "parallel", "arbitrary"),
    )(a, b)


def _norm_matmul(x, w, b, *, out_dtype, tm, tn, name):
    n = b.shape[1]
    assert LP % tm == 0 and n % tn == 0

    def body(x_ref, w_ref, b_ref, a_ref, o_ref, a_scr):
        @pl.when(pl.program_id(1) == 0)
        def _():
            xv = x_ref[...]
            r = lax.rsqrt(jnp.mean(xv * xv, axis=-1, keepdims=True) + EPS)
            a = (xv * r * w_ref[...]).astype(BF16)
            a_scr[...] = a
            a_ref[...] = a

        o_ref[...] = jnp.dot(a_scr[...], b_ref[...], preferred_element_type=F32).astype(out_dtype)

    return pl.pallas_call(
        body, name=name, grid=(LP // tm, n // tn),
        in_specs=[pl.BlockSpec((tm, D), lambda i, j: (i, 0)), pl.BlockSpec((1, D), lambda i, j: (0, 0)),
                  pl.BlockSpec((D, tn), lambda i, j: (0, j))],
        out_specs=[pl.BlockSpec((tm, D), lambda i, j: (i, 0)), pl.BlockSpec((tm, tn), lambda i, j: (i, j))],
        out_shape=[jax.ShapeDtypeStruct((LP, D), BF16), jax.ShapeDtypeStruct((LP, n), out_dtype)],
        scratch_shapes=[pltpu.VMEM((tm, D), BF16)],
        compiler_params=_cparams("arbitrary", "arbitrary"),
    )(x, w, b)


def _matmul_resid_norm(a, b, h, w, name, target=None):
    k = a.shape[1]
    has_loss = target is not None

    def body(a_ref, b_ref, h_ref, w_ref, *refs):
        m = jnp.dot(a_ref[...].astype(BF16), b_ref[...].astype(BF16), preferred_element_type=F32)
        r = lax.rsqrt(jnp.mean(m * m, axis=-1, keepdims=True) + EPS)
        i = pl.program_id(0)
        row = i * TM + lax.broadcasted_iota(jnp.int32, (TM, 1), 0)
        y = h_ref[...] + jnp.where(row >= PAD_ROWS, m * r * w_ref[...], 0.0)
        if not has_loss:
            m_ref, y_ref = refs
            m_ref[...] = m
            y_ref[...] = y
            return
        t_ref, m_ref, dy_ref, loss_ref = refs
        m_ref[...] = m

        @pl.when(i == 0)
        def _():
            loss_ref[...] = jnp.zeros_like(loss_ref)

        diff = jnp.where(row >= CHUNK, y - t_ref[...], 0.0)
        dy_ref[...] = diff * (1.0 / D)
        loss_ref[...] += (0.5 / D) * jnp.sum(diff * diff)

    tile = pl.BlockSpec((TM, D), lambda i: (i, 0))
    shape = jax.ShapeDtypeStruct((LP, D), F32)
    in_specs = [pl.BlockSpec((TM, k), lambda i: (i, 0)), pl.BlockSpec((k, D), lambda i: (0, 0)), tile,
                pl.BlockSpec((1, D), lambda i: (0, 0))]
    if has_loss:
        return pl.pallas_call(
            body, name=name, grid=(LP // TM,),
            in_specs=in_specs + [tile],
            out_specs=[tile, tile, pl.BlockSpec((8, 128), lambda i: (0, 0))],
            out_shape=[shape, shape, jax.ShapeDtypeStruct((8, 128), F32)],
            compiler_params=_cparams("arbitrary"),
        )(a, b, h, w, target)
    return pl.pallas_call(
        body, name=name, grid=(LP // TM,),
        in_specs=in_specs, out_specs=[tile, tile], out_shape=[shape, shape],
        compiler_params=_cparams("parallel"),
    )(a, b, h, w)


def _rmsnorm_bwd_rows(dy, x, w):
    r = lax.rsqrt(jnp.mean(x * x, axis=-1, keepdims=True) + EPS)
    g = dy * w
    dx = r * g - x * (r * r * r * jnp.mean(g * x, axis=-1, keepdims=True))
    return dx, jnp.sum(dy * x * r, axis=0, keepdims=True)


def _matmul_norm_bwd(dz, b, x, w, resid, tk, name, carried=()):
    k = dz.shape[1]
    assert k % tk == 0
    nk = k // tk
    n_rows = LP // TM
    n_carried = len(carried)

    def body(*refs):
        a_ref, b_ref, x_ref, w_ref, r_ref = refs[:5]
        g_refs, refs = refs[5:5 + n_carried], refs[5 + n_carried:]
        dx_ref, dw_ref = refs[:2]
        got_refs, refs = refs[2:2 + n_carried], refs[2 + n_carried:]
        acc, sems = (refs[:1], refs[1:]) if nk > 1 else ((), refs)
        i, kk = pl.program_id(0), pl.program_id(1)
        if n_carried:
            exchange_start, exchange_finish = _exchange_phases(g_refs, got_refs, *sems)
            pl.when((i == 0) & (kk == 0))(exchange_start)

        @pl.when((i == 0) & (kk == 0))
        def _():
            dw_ref[...] = jnp.zeros_like(dw_ref)

        prod = lax.dot_general(a_ref[...].astype(BF16), b_ref[...].astype(BF16), (((1,), (1,)), ((), ())),
                               preferred_element_type=F32)

        def finish(dy):
            dx, dw = _rmsnorm_bwd_rows(dy, x_ref[...], w_ref[...])
            dx_ref[...] = dx + r_ref[...]
            dw_ref[0:1, :] += dw

        if nk == 1:
            finish(prod)
        else:
            acc_ref, = acc

            @pl.when(kk == 0)
            def _():
                acc_ref[...] = prod

            @pl.when((kk > 0) & (kk < nk - 1))
            def _():
                acc_ref[...] += prod

            @pl.when(kk == nk - 1)
            def _():
                finish(acc_ref[...] + prod)

        if n_carried:
            pl.when((i == n_rows - 1) & (kk == nk - 1))(exchange_finish)

    tile = pl.BlockSpec((TM, D), lambda i, kk: (i, 0))
    anywhere = [pl.BlockSpec(memory_space=pl.ANY)] * n_carried
    return pl.pallas_call(
        body, name=name, grid=(n_rows, nk),
        in_specs=[pl.BlockSpec((TM, tk), lambda i, kk: (i, kk)), pl.BlockSpec((D, tk), lambda i, kk: (0, kk)), tile,
                  pl.BlockSpec((1, D), lambda i, kk: (0, 0)), tile] + anywhere,
        out_specs=[tile, pl.BlockSpec((8, D), lambda i, kk: (0, 0))] + anywhere,
        out_shape=[jax.ShapeDtypeStruct((LP, D), F32), jax.ShapeDtypeStruct((8, D), F32)]
        + [jax.ShapeDtypeStruct(g.shape, g.dtype) for g in carried],
        scratch_shapes=([pltpu.VMEM((TM, D), F32)] if nk > 1 else []) + _exchange_sems(n_carried),
        compiler_params=_cparams("arbitrary", "arbitrary"),
    )(dz, b, x, w, resid, *carried)


def _norm_bwd_matmul(dh, x, w, b, out_dtype, name):
    n = b.shape[0]

    def body(dh_ref, x_ref, w_ref, b_ref, o_ref, dx_ref, dw_ref):
        i = pl.program_id(0)

        @pl.when(i == 0)
        def _():
            dw_ref[...] = jnp.zeros_like(dw_ref)

        row = i * TM + lax.broadcasted_iota(jnp.int32, (TM, 1), 0)
        dy = jnp.where(row >= PAD_ROWS, dh_ref[...], 0.0)
        dx, dw = _rmsnorm_bwd_rows(dy, x_ref[...], w_ref[...])
        dxb = dx.astype(BF16)
        dx_ref[...] = dxb
        dw_ref[0:1, :] += dw
        o_ref[...] = lax.dot_general(dxb, b_ref[...].astype(BF16), (((1,), (1,)), ((), ())),
                                     preferred_element_type=F32).astype(out_dtype)

    tile = pl.BlockSpec((TM, D), lambda i: (i, 0))
    return pl.pallas_call(
        body, name=name, grid=(LP // TM,),
        in_specs=[tile, tile, pl.BlockSpec((1, D), lambda i: (0, 0)), pl.BlockSpec((n, D), lambda i: (0, 0))],
        out_specs=[pl.BlockSpec((TM, n), lambda i: (i, 0)), tile, pl.BlockSpec((8, D), lambda i: (0, 0))],
        out_shape=[jax.ShapeDtypeStruct((LP, n), out_dtype), jax.ShapeDtypeStruct((LP, D), BF16),
                   jax.ShapeDtypeStruct((8, D), F32)],
        compiler_params=_cparams("arbitrary"),
    )(dh, x, w, b)


GELU_C = math.sqrt(2.0 / math.pi)
GELU_K = 0.044715
STRIP = 16
HALF = 8


def _gelu_half(a):
    return 0.5 * jnp.tanh(a * (a * a * (GELU_C * GELU_K) + GELU_C)) + 0.5


def _gelu_slope(a, h):
    return h * (1.0 + (a - a * h) * (a * a * (6.0 * GELU_C * GELU_K) + 2.0 * GELU_C))


def _shift_down(x, prev8, rows):
    row = lax.broadcasted_iota(jnp.int32, (rows, 1), 0)
    p1 = pltpu.roll(prev8, 1, 0)
    p2 = pltpu.roll(prev8, 2, 0)
    x1 = jnp.where(row == 0, p1[0:1, :], pltpu.roll(x, 1, 0))
    x2 = jnp.where(row == 0, p2[0:1, :], jnp.where(row == 1, p2[1:2, :], pltpu.roll(x, 2, 0)))
    return x1, x2


def _conv_act_fwd(u, cw8, name, carried=()):
    n_rows = LP // TM
    cb2 = 2 * CONV_BLOCK
    n_carried = len(carried)

    def body(*refs):
        u_ref, cw_ref = refs[:2]
        x_refs, refs = refs[2:2 + n_carried], refs[2 + n_carried:]
        conv_ref, act_ref = refs[:2]
        gathered_refs, refs = refs[2:2 + n_carried], refs[2 + n_carried:]
        carry_ref = refs[0]
        j, i = pl.program_id(0), pl.program_id(1)
        if n_carried:
            start, forward, finish = _gather_phases(x_refs, gathered_refs, *refs[1:])
            pl.when((j == 0) & (i == 0))(start)
            pl.when((j == (3 * N_CONV_BLOCKS) // 4) & (i == 0))(forward)

        @pl.when(i == 0)
        def _():
            carry_ref[...] = jnp.zeros_like(carry_ref)

        x = u_ref[...].astype(F32)
        x1, x2 = _shift_down(x, carry_ref[...], TM)
        conv = cw_ref[3:4, :] + x2 * cw_ref[0:1, :] + x1 * cw_ref[1:2, :] + x * cw_ref[2:3, :]
        conv_ref[...] = conv.astype(BF16)
        a = conv[:, :CONV_BLOCK]
        g = conv[:, CONV_BLOCK:]
        act_ref[...] = (a * _gelu_half(a) * g).astype(BF16)
        carry_ref[...] = x[TM - 8:TM, :]
        if n_carried:
            pl.when((j == N_CONV_BLOCKS - 1) & (i == n_rows - 1))(finish)

    anywhere = [pl.BlockSpec(memory_space=pl.ANY)] * n_carried
    return pl.pallas_call(
        body, name=name, grid=(N_CONV_BLOCKS, n_rows),
        in_specs=[pl.BlockSpec((TM, cb2), lambda j, i: (i, j)), pl.BlockSpec((8, cb2), lambda j, i: (0, j))] + anywhere,
        out_specs=[pl.BlockSpec((TM, cb2), lambda j, i: (i, j)),
                   pl.BlockSpec((TM, CONV_BLOCK), lambda j, i: (i, j))] + anywhere,
        out_shape=[jax.ShapeDtypeStruct((LP, D_UP), BF16), jax.ShapeDtypeStruct((LP, D_FF), BF16)]
        + _gathered_shapes(carried),
        scratch_shapes=[pltpu.VMEM((8, cb2), F32)] + _exchange_sems(n_carried),
        compiler_params=_cparams("arbitrary", "arbitrary"),
    )(u, cw8, *carried)


def _conv_act_bwd(dact, conv, u, cw8, name):
    n_rows = LP // TM
    cb2 = 2 * CONV_BLOCK
    n_strips = TM // STRIP

    def body(dact_ref, conv_ref, u_ref, cw_ref, du_ref, dcw_ref, carry_ref):
        i = pl.program_id(1)

        @pl.when(i == 0)
        def _():
            dcw_ref[...] = jnp.zeros_like(dcw_ref)
            carry_ref[...] = jnp.zeros_like(carry_ref)

        w0, w1, w2 = cw_ref[0:1, :], cw_ref[1:2, :], cw_ref[2:3, :]
        row = lax.broadcasted_iota(jnp.int32, (HALF, 1), 0)

        def strip(k, carry):
            n1, n2, s0, s1, s2, s3 = carry
            r0 = pl.multiple_of((n_strips - 1 - k) * STRIP, STRIP)
            cv = conv_ref[pl.ds(r0, STRIP), :].astype(F32)
            dav = dact_ref[pl.ds(r0, STRIP), :].astype(F32)
            x = u_ref[pl.ds(r0, STRIP), :].astype(F32)
            du = [None, None]
            for half in (1, 0):
                rows = slice(HALF * half, HALF * (half + 1))
                a, g, dah = cv[rows, :CONV_BLOCK], cv[rows, CONV_BLOCK:], dav[rows]
                h = _gelu_half(a)
                dconv = jnp.concatenate([dah * g * _gelu_slope(a, h), dah * (a * h)], axis=1)
                u1, u2 = pltpu.roll(dconv, HALF - 1, 0), pltpu.roll(dconv, HALF - 2, 0)
                d1 = jnp.where(row >= HALF - 1, n1, u1)
                d2 = jnp.where(row >= HALF - 2, n2, u2)
                du[half] = dconv * w2 + d1 * w1 + d2 * w0
                s0, s1, s2, s3 = s0 + d2 * x[rows], s1 + d1 * x[rows], s2 + dconv * x[rows], s3 + dconv
                n1, n2 = u1, u2
            du_ref[pl.ds(r0, STRIP), :] = jnp.concatenate(du, axis=0).astype(BF16)
            return n1, n2, s0, s1, s2, s3

        below = carry_ref[...]
        zero = jnp.zeros((HALF, cb2), F32)
        init = (pltpu.roll(below, HALF - 1, 0), pltpu.roll(below, HALF - 2, 0), zero, zero, zero, zero)
        u1, _, s0, s1, s2, s3 = lax.fori_loop(0, n_strips, strip, init, unroll=2)
        carry_ref[...] = pltpu.roll(u1, 1, 0)
        dcw_ref[0:1, :] += jnp.sum(s0, axis=0, keepdims=True)
        dcw_ref[1:2, :] += jnp.sum(s1, axis=0, keepdims=True)
        dcw_ref[2:3, :] += jnp.sum(s2, axis=0, keepdims=True)
        dcw_ref[3:4, :] += jnp.sum(s3, axis=0, keepdims=True)

    rev = lambda j, i: (n_rows - 1 - i, j)
    return pl.pallas_call(
        body, name=name, grid=(N_CONV_BLOCKS, n_rows),
        in_specs=[pl.BlockSpec((TM, CONV_BLOCK), rev), pl.BlockSpec((TM, cb2), rev), pl.BlockSpec((TM, cb2), rev),
                  pl.BlockSpec((8, cb2), lambda j, i: (0, j))],
        out_specs=[pl.BlockSpec((TM, cb2), rev), pl.BlockSpec((8, cb2), lambda j, i: (0, j))],
        out_shape=[jax.ShapeDtypeStruct((LP, D_UP), BF16), jax.ShapeDtypeStruct((8, D_UP), F32)],
        scratch_shapes=[pltpu.VMEM((HALF, cb2), F32)],
        compiler_params=_cparams("arbitrary", "arbitrary"),
    )(dact, conv, u, cw8)


CHUNKS_PER_STEP = 3 if N_CHUNKS % 3 == 0 else 1
STEP_ROWS = CHUNKS_PER_STEP * CHUNK
N_STEPS = N_CHUNKS // CHUNKS_PER_STEP


def _ret_consts(h):
    rows = STEP_ROWS
    lg = math.log(1.0 - 2.0 ** (-5.0 - h))
    ri = lax.broadcasted_iota(jnp.int32, (rows, rows), 0)
    ci = lax.broadcasted_iota(jnp.int32, (rows, rows), 1)
    diff = (ri - ci).astype(F32)
    dmat = jnp.where(diff >= 0, jnp.exp(lg * jnp.maximum(diff, 0.0)), 0.0)
    rowf = lax.broadcasted_iota(jnp.int32, (rows, 1), 0).astype(F32)
    zeta = jnp.exp(lg * (rows - 1.0 - rowf))
    xi = jnp.exp(lg * (rowf + 1.0))
    return dmat, zeta, xi, math.exp(lg * rows)


def _rope(t, cosv, sinv):
    return t * cosv + pltpu.roll(t, RET_DK // 2, 1) * sinv


def _unrope(d, cosv, sinv):
    return d * cosv + pltpu.roll(d * sinv, RET_DK // 2, 1)


def _gla_masks():
    ri = lax.broadcasted_iota(jnp.int32, (CHUNK, CHUNK), 0)
    ci = lax.broadcasted_iota(jnp.int32, (CHUNK, CHUNK), 1)
    return dict(ri=ri, ci=ci, tril=(ri >= ci).astype(F32), heads=_head_block_mask(), own=_state_block_mask())


def _gla_common(p_ref, w2_ref, gb_ref, chunk, rows, masks):
    row = lax.broadcasted_iota(jnp.int32, (CHUNK, 1), 0)
    real = (chunk * CHUNK + row) >= PAD_ROWS
    ga = p_ref[rows, O_GA:O_GA + 128]
    z = _dot(ga, w2_ref[...]) + gb_ref[...]
    la = (jnp.minimum(z, 0.0) - jnp.log(1.0 + jnp.exp(-jnp.abs(z)))) * (1.0 / GLA_TAU)
    la = jnp.where(real, la, 0.0)
    ri, ci = masks["ri"], masks["ci"]
    cum = _dot_exact_rhs(masks["tril"], la)
    last = cum[CHUNK - 1:CHUNK, :]
    qs = p_ref[rows, O_GQ:O_GQ + 256] * (GLA_DK ** -0.5)
    k = p_ref[rows, O_GK:O_GK + 256]
    ecum = jnp.exp(cum)
    ekl = jnp.exp(last - cum)
    el = jnp.exp(last)
    refs = [jnp.zeros((1, 256), F32)] + [cum[a * SUB - 1:a * SUB, :] for a in range(1, N_SUB)]
    eq = [jnp.exp(cum[a * SUB:(a + 1) * SUB, :] - refs[a]) for a in range(N_SUB)]
    spread = refs[0] - cum[SUB - 1:SUB, :]
    for a in range(1, N_SUB):
        spread = jnp.maximum(spread, refs[a] - cum[(a + 1) * SUB - 1:(a + 1) * SUB, :])
    small = jnp.max(spread) <= GLA_FACTORED_MAX
    return dict(real=real, row=row, z=z, la=la, cum=cum, last=last, qs=qs, k=k, ecum=ecum, ekl=ekl, el=el,
                refs=refs, eq=eq, small=small, ri=ri, ci=ci, masks=masks)


GLA_FACTORED_MAX = 40.0


def _head_block_mask():
    r = lax.broadcasted_iota(jnp.int32, (CHUNK, 256), 0)
    col = lax.broadcasted_iota(jnp.int32, (CHUNK, 256), 1)
    return (r // SUB) == (col // GLA_DK)


def _state_block_mask():
    r = lax.broadcasted_iota(jnp.int32, (GLA_HEADS * GLA_DK, GLA_HEADS * GLA_DV), 0)
    col = lax.broadcasted_iota(jnp.int32, (GLA_HEADS * GLA_DK, GLA_HEADS * GLA_DV), 1)
    return (r // GLA_DK) == (col // GLA_DV)


def _block_diagonal(blocks):
    zero = jnp.zeros((GLA_DK, GLA_DV), F32)
    return jnp.concatenate([jnp.concatenate([blocks[h] if g == h else zero for g in range(GLA_HEADS)], axis=1)
                            for h in range(GLA_HEADS)], axis=0)


def _gla_factored(c):
    mask = c["masks"]["heads"]
    eks, keys, queries = [], [], []
    for a in range(N_SUB):
        ek = jnp.exp(jnp.minimum(c["refs"][a] - c["cum"], GLA_FACTORED_MAX))
        qh = c["qs"][a * SUB:(a + 1) * SUB, :] * c["eq"][a]
        eks.append(ek)
        keys.append(c["k"] * ek)
        queries.append(jnp.where(mask, jnp.concatenate([qh] * GLA_HEADS, axis=0), 0.0))
    return eks, keys, queries


def _gla_scores_factored(c, factored, p_scr):
    _, keys, queries = factored
    for a in range(N_SUB):
        out = _dot_nt(queries[a], keys[a])
        out = jnp.where(c["ci"] <= a * SUB + (c["ri"] & (SUB - 1)), out, 0.0)
        for h in range(GLA_HEADS):
            p_scr[h, a * SUB:(a + 1) * SUB, :] = out[h * SUB:(h + 1) * SUB, :]


def _gla_intra_bwd_factored(c, factored, dps, dq_scr, dk_scr):
    eks, keys, queries = factored
    mask = c["masks"]["heads"]
    dk = jnp.zeros((CHUNK, 256), F32)
    for a in range(N_SUB):
        dpa = jnp.concatenate([dps[h][a * SUB:(a + 1) * SUB, :] for h in range(GLA_HEADS)], axis=0)
        dq = jnp.where(mask, _dot(dpa, keys[a]), 0.0)
        dq = dq[0:SUB] + dq[SUB:2 * SUB] + dq[2 * SUB:3 * SUB] + dq[3 * SUB:4 * SUB]
        dq_scr[a * SUB:(a + 1) * SUB, :] = dq * c["eq"][a]
        dk = dk + _dot_tn(dpa, queries[a]) * eks[a]
    dk_scr[...] = dk


def _gla_lag_weights(c):
    cum, row = c["cum"], c["row"]
    out = [jnp.ones((CHUNK, 256), F32)]
    for r in range(1, SUB):
        out.append(jnp.where((row % SUB) >= r, jnp.exp(jnp.minimum(cum - pltpu.roll(cum, r, 0), 0.0)), 0.0))
    return out


def _gla_pairwise_keys(c):
    return [None] + [c["k"] * jnp.exp(jnp.minimum(c["refs"][a] - c["cum"], 0.0)) for a in range(1, N_SUB)]


def _gla_scores_pairwise(c, lag_w, keys, h):
    sl = slice(GLA_DK * h, GLA_DK * (h + 1))
    qs, k = c["qs"][:, sl], c["k"][:, sl]
    ri, ci = c["ri"], c["ci"]
    p = jnp.zeros((CHUNK, CHUNK), F32)
    for r in range(SUB):
        kr = k if r == 0 else pltpu.roll(k, r, 0)
        pr = jnp.sum(qs * kr * lag_w[r][:, sl], axis=1, keepdims=True)
        p = p + jnp.where(ci == ri - r, pr, 0.0)
    blocks = [jnp.zeros((SUB, CHUNK), F32)]
    for a in range(1, N_SUB):
        qh = qs[a * SUB:(a + 1) * SUB, :] * c["eq"][a][:, sl]
        blocks.append(jnp.where(ci[:SUB, :] < a * SUB, _dot_nt(qh, keys[a][:, sl]), 0.0))
    return p + jnp.concatenate(blocks, axis=0)


def _gla_all_scores(c, p_scr, factored):
    if factored:
        _gla_scores_factored(c, _gla_factored(c), p_scr)
    else:
        lag_w, keys = _gla_lag_weights(c), _gla_pairwise_keys(c)
        for h in range(GLA_HEADS):
            p_scr[h] = _gla_scores_pairwise(c, lag_w, keys, h)


def _either_form(chunks, run):
    small = chunks[0]["small"]
    for c in chunks[1:]:
        small = jnp.logical_and(small, c["small"])
    pl.when(small)(lambda: run(True))
    pl.when(jnp.logical_not(small))(lambda: run(False))


def _gla_intra_bwd_pairwise(c, lag_w, keys, dp, h):
    sl = slice(GLA_DK * h, GLA_DK * (h + 1))
    qs_h, k_h = c["qs"][:, sl], c["k"][:, sl]
    ri, ci = c["ri"], c["ci"]
    dq_rows = [jnp.zeros((SUB, GLA_DK), F32)]
    dk = jnp.zeros((CHUNK, GLA_DK), F32)
    for a in range(1, N_SUB):
        eq = c["eq"][a][:, sl]
        qh = qs_h[a * SUB:(a + 1) * SUB, :] * eq
        dpa = jnp.where(ci[:SUB, :] < a * SUB, dp[a * SUB:(a + 1) * SUB, :], 0.0)
        dq_rows.append(_dot(dpa, keys[a][:, sl]) * eq)
        ek = jnp.exp(jnp.minimum(c["refs"][a][:, sl] - c["cum"][:, sl], 0.0))
        dk = dk + _dot_tn(dpa, qh) * ek
    dq = jnp.concatenate(dq_rows, axis=0)
    for r in range(SUB):
        w = lag_w[r][:, sl]
        dpr = jnp.sum(jnp.where(ci == ri - r, dp, 0.0), axis=1, keepdims=True)
        kr = k_h if r == 0 else pltpu.roll(k_h, r, 0)
        dq = dq + dpr * kr * w
        back = dpr * qs_h * w
        dk = dk + (back if r == 0 else pltpu.roll(back, CHUNK - r, 0))
    return dq, dk


def _gla_all_intra_bwd(c, dps, p_scr, dq_scr, dk_scr, factored):
    if factored:
        terms = _gla_factored(c)
        _gla_scores_factored(c, terms, p_scr)
        _gla_intra_bwd_factored(c, terms, dps, dq_scr, dk_scr)
    else:
        lag_w, keys = _gla_lag_weights(c), _gla_pairwise_keys(c)
        outs = [_gla_intra_bwd_pairwise(c, lag_w, keys, dps[h], h) for h in range(GLA_HEADS)]
        for h in range(GLA_HEADS):
            p_scr[h] = _gla_scores_pairwise(c, lag_w, keys, h)
        dq_scr[...] = jnp.concatenate([o[0] for o in outs], axis=1)
        dk_scr[...] = jnp.concatenate([o[1] for o in outs], axis=1)


def _mixer_fwd(proj, cos2, sin2, w2p, gb, rnw, gnw, name, carried=()):
    n_carried = len(carried)

    def body(*refs):
        p_ref, c_ref, s_ref, w2_ref, gb_ref, rnw_ref, gnw_ref = refs[:7]
        x_refs, refs = refs[7:7 + n_carried], refs[7 + n_carried:]
        ocat_ref, mrg_ref, sr_out, sg_out = refs[:4]
        gathered_refs, refs = refs[4:4 + n_carried], refs[4 + n_carried:]
        sr, sg, p_scr = refs[:3]
        n = pl.program_id(0)
        if n_carried:
            start, forward, finish = _gather_phases(x_refs, gathered_refs, *refs[3:])
            pl.when(n == 0)(start)
            pl.when(n == (3 * N_STEPS) // 4)(forward)

        @pl.when(n == 0)
        def _():
            sr[...] = jnp.zeros_like(sr)
            sg[...] = jnp.zeros_like(sg)

        sr_out[0] = sr[...]
        cosv, sinv = c_ref[...], s_ref[...]

        for h in range(RET_HEADS):
            dmat, zeta, xi, gc = _ret_consts(h)
            hs = slice(128 * h, 128 * (h + 1))
            q = _rope(p_ref[:, O_RQ + 128 * h:O_RQ + 128 * (h + 1)], cosv, sinv)
            k = _rope(p_ref[:, O_RK + 128 * h:O_RK + 128 * (h + 1)], cosv, sinv) * (RET_DK ** -0.5)
            v = p_ref[:, O_RV + 128 * h:O_RV + 128 * (h + 1)]
            g = p_ref[:, O_RG + 128 * h:O_RG + 128 * (h + 1)]
            s_in = sr[h]
            a = _dot_nt(q, k) * dmat
            o = _dot(a, v) + _dot(q, s_in) * xi
            sr[h] = gc * s_in + _dot_tn(k * zeta, v)
            mu = jnp.mean(o, axis=-1, keepdims=True)
            xc = o - mu
            nrm = xc * lax.rsqrt(jnp.mean(xc * xc, axis=-1, keepdims=True) + EPS)
            ocat_ref[:, hs] = o
            mrg_ref[:, hs] = (nrm * rnw_ref[:, hs] * (g * _sigmoid(g))).astype(BF16)

        row_slices = [slice(CHUNK * j, CHUNK * (j + 1)) for j in range(CHUNKS_PER_STEP)]
        masks = _gla_masks()
        chunks = [_gla_common(p_ref, w2_ref, gb_ref, n * CHUNKS_PER_STEP + j, rows, masks)
                  for j, rows in enumerate(row_slices)]

        def gla_chunks(factored):
            own = masks["own"]
            for j, (rows, c) in enumerate(zip(row_slices, chunks)):
                s_in = sg[...]
                for h in range(GLA_HEADS):
                    sg_out[j, h] = s_in[GLA_DK * h:GLA_DK * (h + 1), GLA_DV * h:GLA_DV * (h + 1)]
                _gla_all_scores(c, p_scr.at[j], factored)
                v_all = p_ref[rows, O_GV:O_GV + GLA_HEADS * GLA_DV]
                o_inter = _dot(c["qs"] * c["ecum"], s_in)
                decay = jnp.exp(_dot_tn_exact_lhs(c["la"], jnp.ones((CHUNK, GLA_HEADS * GLA_DV), F32)))
                sg[...] = decay * s_in + jnp.where(own, _dot_tn(c["k"] * c["ekl"], v_all), 0.0)
                o_intra = _dot(p_scr[j].reshape(GLA_HEADS * CHUNK, CHUNK), v_all)
                for h in range(GLA_HEADS):
                    hs = slice(512 + 128 * h, 512 + 128 * (h + 1))
                    g = p_ref[rows, O_GR + 128 * h:O_GR + 128 * (h + 1)]
                    o = (o_intra[CHUNK * h:CHUNK * (h + 1), GLA_DV * h:GLA_DV * (h + 1)]
                         + o_inter[:, GLA_DV * h:GLA_DV * (h + 1)])
                    nrm = o * lax.rsqrt(jnp.mean(o * o, axis=-1, keepdims=True) + EPS)
                    ocat_ref[rows, hs] = o
                    mrg_ref[rows, hs] = (nrm * gnw_ref[:, 128 * h:128 * (h + 1)] * (g * _sigmoid(g))).astype(BF16)

        _either_form(chunks, gla_chunks)

        if n_carried:
            pl.when(n == N_STEPS - 1)(finish)

    const = lambda shape: pl.BlockSpec(shape, lambda n: (0,) * len(shape))
    anywhere = [pl.BlockSpec(memory_space=pl.ANY)] * n_carried
    return pl.pallas_call(
        body, name=name, grid=(N_STEPS,),
        in_specs=[pl.BlockSpec((STEP_ROWS, IN_WP), lambda n: (n, 0)),
                  pl.BlockSpec((STEP_ROWS, 128), lambda n: (n, 0)), pl.BlockSpec((STEP_ROWS, 128), lambda n: (n, 0)),
                  const((128, 256)), const((1, 256)), const((1, 512)), const((1, 512))] + anywhere,
        out_specs=[pl.BlockSpec((STEP_ROWS, D), lambda n: (n, 0)), pl.BlockSpec((STEP_ROWS, D), lambda n: (n, 0)),
                   pl.BlockSpec((1, RET_HEADS, RET_DK, 128), lambda n: (n, 0, 0, 0)),
                   pl.BlockSpec((CHUNKS_PER_STEP, GLA_HEADS, GLA_DK, GLA_DV), lambda n: (n, 0, 0, 0))] + anywhere,
        out_shape=[jax.ShapeDtypeStruct((LP, D), F32), jax.ShapeDtypeStruct((LP, D), BF16),
                   jax.ShapeDtypeStruct((N_STEPS, RET_HEADS, RET_DK, 128), F32),
                   jax.ShapeDtypeStruct((N_CHUNKS, GLA_HEADS, GLA_DK, GLA_DV), F32)] + _gathered_shapes(carried),
        scratch_shapes=[pltpu.VMEM((RET_HEADS, RET_DK, 128), F32),
                        pltpu.VMEM((GLA_HEADS * GLA_DK, GLA_HEADS * GLA_DV), F32),
                        pltpu.VMEM((CHUNKS_PER_STEP, GLA_HEADS, CHUNK, CHUNK), F32)] + _exchange_sems(n_carried),
        compiler_params=_cparams("arbitrary"),
    )(proj, cos2, sin2, w2p, gb, rnw, gnw, *carried)


def _mixer_bwd(proj, ocat, dmrg, sr_all, sg_all, cos2, sin2, w2p, gb, rnw, gnw, name, carried=()):
    last_step = N_STEPS - 1
    n_carried = len(carried)

    def body(*refs):
        p_ref, ocat_ref, dm_ref, sr_ref, sg_ref, c_ref, s_ref, w2_ref, gb_ref, rnw_ref, gnw_ref = refs[:11]
        g_refs, refs = refs[11:11 + n_carried], refs[11 + n_carried:]
        dp_ref, dw2_ref, dgb_ref, drn_ref, dgn_ref = refs[:5]
        got_refs, refs = refs[5:5 + n_carried], refs[5 + n_carried:]
        dsr, dsg, p_scr, dq_scr, dk_scr = refs[:5]
        step = pl.program_id(0)
        n = last_step - step
        if n_carried:
            start, finish = _exchange_phases(g_refs, got_refs, *refs[5:])
            pl.when(step == 0)(start)

        @pl.when(step == 0)
        def _():
            dsr[...] = jnp.zeros_like(dsr)
            dsg[...] = jnp.zeros_like(dsg)
            dw2_ref[...] = jnp.zeros_like(dw2_ref)
            dgb_ref[...] = jnp.zeros_like(dgb_ref)
            drn_ref[...] = jnp.zeros_like(drn_ref)
            dgn_ref[...] = jnp.zeros_like(dgn_ref)

        cosv, sinv = c_ref[...], s_ref[...]
        step_row = lax.broadcasted_iota(jnp.int32, (STEP_ROWS, 1), 0)
        real = ((n * STEP_ROWS + step_row) >= PAD_ROWS).astype(F32)

        for h in range(RET_HEADS):
            dmat, zeta, xi, gc = _ret_consts(h)
            hs = slice(128 * h, 128 * (h + 1))
            q = _rope(p_ref[:, O_RQ + 128 * h:O_RQ + 128 * (h + 1)], cosv, sinv)
            k = _rope(p_ref[:, O_RK + 128 * h:O_RK + 128 * (h + 1)], cosv, sinv) * (RET_DK ** -0.5)
            v = p_ref[:, O_RV + 128 * h:O_RV + 128 * (h + 1)]
            g = p_ref[:, O_RG + 128 * h:O_RG + 128 * (h + 1)]
            o = ocat_ref[:, hs]
            dy = dm_ref[:, hs]
            wv = rnw_ref[:, hs]
            mu = jnp.mean(o, axis=-1, keepdims=True)
            xc = o - mu
            rs = lax.rsqrt(jnp.mean(xc * xc, axis=-1, keepdims=True) + EPS)
            nrm = xc * rs
            sgm = _sigmoid(g)
            sil = g * sgm
            drn_ref[0:1, hs] += jnp.sum(dy * nrm * sil, axis=0, keepdims=True)
            dgate = dy * nrm * wv * (sgm * (1.0 + g * (1.0 - sgm)))
            dn = dy * wv * sil
            do = rs * (dn - jnp.mean(dn, axis=-1, keepdims=True) - nrm * jnp.mean(dn * nrm, axis=-1, keepdims=True))
            s_in = sr_ref[0, h]
            ds_out = dsr[h]
            a = _dot_nt(q, k) * dmat
            da = _dot_nt(do, v) * dmat
            dox = do * xi
            dq = _dot(da, k) + _dot_nt(dox, s_in)
            dk = _dot_tn(da, q) + _dot_nt(v, ds_out) * zeta
            dv = _dot_tn(a, do) + _dot(k * zeta, ds_out)
            dsr[h] = gc * ds_out + _dot_tn(q, dox)
            dk = dk * (RET_DK ** -0.5)
            dp_ref[:, O_RQ + 128 * h:O_RQ + 128 * (h + 1)] = (_unrope(dq, cosv, sinv) * real).astype(BF16)
            dp_ref[:, O_RK + 128 * h:O_RK + 128 * (h + 1)] = (_unrope(dk, cosv, sinv) * real).astype(BF16)
            dp_ref[:, O_RV + 128 * h:O_RV + 128 * (h + 1)] = (dv * real).astype(BF16)
            dp_ref[:, O_RG + 128 * h:O_RG + 128 * (h + 1)] = (dgate * real).astype(BF16)

        row_slices = [slice(CHUNK * j, CHUNK * (j + 1)) for j in range(CHUNKS_PER_STEP)]
        masks = _gla_masks()
        chunks = [_gla_common(p_ref, w2_ref, gb_ref, n * CHUNKS_PER_STEP + j, rows, masks)
                  for j, rows in enumerate(row_slices)]

        def gla_chunks(factored):
            for j in reversed(range(CHUNKS_PER_STEP)):
                gla_chunk_bwd(chunks[j], n * CHUNKS_PER_STEP + j, row_slices[j], j, factored, p_ref, ocat_ref, dm_ref,
                              sg_ref, w2_ref, gnw_ref, dp_ref, dw2_ref, dgb_ref, dgn_ref, dsg, p_scr, dq_scr, dk_scr)

        _either_form(chunks, gla_chunks)
        if n_carried:
            pl.when(step == last_step)(finish)

    def gla_chunk_bwd(c, chunk, rows, j, factored, p_ref, ocat_ref, dm_ref, sg_ref, w2_ref, gnw_ref,
                      dp_ref, dw2_ref, dgb_ref, dgn_ref, dsg, p_scr, dq_scr, dk_scr):
        row = lax.broadcasted_iota(jnp.int32, (CHUNK, 1), 0)
        real = ((chunk * CHUNK + row) >= PAD_ROWS).astype(F32)
        ri, ci = c["ri"], c["ci"]
        causal = ri >= ci
        triu = (ci >= ri).astype(F32)
        qe = c["qs"] * c["ecum"]
        kl = c["k"] * c["ekl"]
        v_all = p_ref[rows, O_GV:O_GV + GLA_HEADS * GLA_DV]
        dos, dps = [], []
        for h in range(GLA_HEADS):
            hs = slice(512 + 128 * h, 512 + 128 * (h + 1))
            g = p_ref[rows, O_GR + 128 * h:O_GR + 128 * (h + 1)]
            o = ocat_ref[rows, hs]
            dy = dm_ref[rows, hs]
            wv = gnw_ref[:, 128 * h:128 * (h + 1)]
            rs = lax.rsqrt(jnp.mean(o * o, axis=-1, keepdims=True) + EPS)
            nrm = o * rs
            sgm = _sigmoid(g)
            sil = g * sgm
            dgn_ref[0:1, 128 * h:128 * (h + 1)] += jnp.sum(dy * nrm * sil, axis=0, keepdims=True)
            dgate = dy * nrm * wv * (sgm * (1.0 + g * (1.0 - sgm)))
            dn = dy * wv * sil
            do = rs * (dn - nrm * jnp.mean(dn * nrm, axis=-1, keepdims=True))
            dp_ref[rows, O_GR + 128 * h:O_GR + 128 * (h + 1)] = (dgate * real).astype(BF16)
            dos.append(do)
        do_all = jnp.concatenate(dos, axis=1)
        do_blocks = jnp.where(c["masks"]["own"], jnp.concatenate([do_all] * GLA_HEADS, axis=0), 0.0)
        dp_all = _dot_nt(do_blocks, v_all)
        dps = [jnp.where(causal, dp_all[CHUNK * h:CHUNK * (h + 1), :], 0.0) for h in range(GLA_HEADS)]
        _gla_all_intra_bwd(c, dps, p_scr.at[j], dq_scr.at[j], dk_scr.at[j], factored)
        s_in = _block_diagonal([sg_ref[j, h] for h in range(GLA_HEADS)])
        ds_out = dsg[...]
        decay = jnp.exp(_dot_tn_exact_lhs(c["la"], jnp.ones((CHUNK, GLA_HEADS * GLA_DV), F32)))
        dv_state = _dot(kl, ds_out)
        dqe = _dot_nt(do_all, s_in)
        dkl = _dot_nt(v_all, ds_out)
        dsg[...] = jnp.where(c["masks"]["own"], _dot_tn(qe, do_all), 0.0) + decay * ds_out
        sd = s_in * ds_out
        sd_hi = sd.astype(BF16)
        sd_lo = (sd - sd_hi.astype(F32)).astype(BF16)
        ones8 = jnp.ones((8, GLA_HEADS * GLA_DV), BF16)
        nt = (((1,), (1,)), ((), ()))
        d_el = (lax.dot_general(ones8, sd_hi, nt, preferred_element_type=F32)
                + lax.dot_general(ones8, sd_lo, nt, preferred_element_type=F32))[0:1, :]
        dqs = dqe * c["ecum"] + dq_scr[j]
        dkk = dkl * c["ekl"] + dk_scr[j]
        d_last = jnp.sum(dkl * kl, axis=0, keepdims=True) + d_el * c["el"]
        dcum = c["qs"] * dqs - c["k"] * dkk + jnp.where(row == CHUNK - 1, d_last, 0.0)
        dla = _dot_exact_rhs(triu, dcum)
        dv = _dot_tn(p_scr[j].reshape(GLA_HEADS * CHUNK, CHUNK), do_blocks) + dv_state
        dp_ref[rows, O_GV:O_GV + GLA_HEADS * GLA_DV] = (dv * real).astype(BF16)
        dp_ref[rows, O_GQ:O_GQ + 256] = (dqs * (GLA_DK ** -0.5) * real).astype(BF16)
        dp_ref[rows, O_GK:O_GK + 256] = (dkk * real).astype(BF16)
        dz = dla * (1.0 / GLA_TAU) * _sigmoid(-c["z"]) * real
        ga = p_ref[rows, O_GA:O_GA + 128]
        dp_ref[rows, O_GA:O_GA + 128] = _dot_nt(dz, w2_ref[...]).astype(BF16)
        dp_ref[rows, O_GA + 128:IN_WP] = jnp.zeros((CHUNK, IN_WP - O_GA - 128), BF16)
        dw2_ref[...] += _dot_tn(ga, dz)
        dgb_ref[0:1, :] += jnp.sum(dz, axis=0, keepdims=True)

    const = lambda shape: pl.BlockSpec(shape, lambda s: (0,) * len(shape))
    rev = lambda s: (last_step - s, 0)
    anywhere = [pl.BlockSpec(memory_space=pl.ANY)] * n_carried
    return pl.pallas_call(
        body, name=name, grid=(N_STEPS,),
        in_specs=[pl.BlockSpec((STEP_ROWS, IN_WP), rev), pl.BlockSpec((STEP_ROWS, D), rev),
                  pl.BlockSpec((STEP_ROWS, D), rev),
                  pl.BlockSpec((1, RET_HEADS, RET_DK, 128), lambda s: (last_step - s, 0, 0, 0)),
                  pl.BlockSpec((CHUNKS_PER_STEP, GLA_HEADS, GLA_DK, GLA_DV), lambda s: (last_step - s, 0, 0, 0)),
                  pl.BlockSpec((STEP_ROWS, 128), rev), pl.BlockSpec((STEP_ROWS, 128), rev),
                  const((128, 256)), const((1, 256)), const((1, 512)), const((1, 512))] + anywhere,
        out_specs=[pl.BlockSpec((STEP_ROWS, IN_WP), rev), const((128, 256)), const((8, 256)),
                   const((8, 512)), const((8, 512))] + anywhere,
        out_shape=[jax.ShapeDtypeStruct((LP, IN_WP), BF16), jax.ShapeDtypeStruct((128, 256), F32),
                   jax.ShapeDtypeStruct((8, 256), F32), jax.ShapeDtypeStruct((8, 512), F32),
                   jax.ShapeDtypeStruct((8, 512), F32)] + [jax.ShapeDtypeStruct(g.shape, g.dtype) for g in carried],
        scratch_shapes=[pltpu.VMEM((RET_HEADS, RET_DK, 128), F32),
                        pltpu.VMEM((GLA_HEADS * GLA_DK, GLA_HEADS * GLA_DV), F32),
                        pltpu.VMEM((CHUNKS_PER_STEP, GLA_HEADS, CHUNK, CHUNK), F32),
                        pltpu.VMEM((CHUNKS_PER_STEP, CHUNK, 256), F32),
                        pltpu.VMEM((CHUNKS_PER_STEP, CHUNK, 256), F32)] + _exchange_sems(n_carried),
        compiler_params=_cparams("arbitrary"),
    )(proj, ocat, dmrg, sr_all, sg_all, cos2, sin2, w2p, gb, rnw, gnw, *carried)


def _all_gather(xs, name):
    n = len(xs)

    def body(*refs):
        start, forward, finish = _gather_phases(refs[:n], refs[n:2 * n], *refs[2 * n:])
        start()
        forward()
        finish()

    return pl.pallas_call(
        body, name=name,
        in_specs=[pl.BlockSpec(memory_space=pl.ANY)] * n,
        out_specs=[pl.BlockSpec(memory_space=pl.ANY)] * n,
        out_shape=_gathered_shapes(xs),
        scratch_shapes=_exchange_sems(n),
    )(*xs)


def _gather_and_pad(shard, meta, x, target, name):
    def body(s_ref, meta_ref, x_ref, t_ref, g_ref, h_ref, tp_ref, zeros, send_sems, recv_sems, local_sems, copy_sems):
        start, forward, finish = _gather_phases([s_ref], [g_ref], send_sems, recv_sems, local_sems)
        start()
        zeros[...] = jnp.zeros_like(zeros)
        copies = [pltpu.make_async_copy(x_ref.at[0], h_ref.at[pl.ds(CHUNK, SEQ)], copy_sems.at[0]),
                  pltpu.make_async_copy(meta_ref, h_ref.at[pl.ds(PAD_ROWS, N_META)], copy_sems.at[1]),
                  pltpu.make_async_copy(zeros.at[pl.ds(0, PAD_ROWS)], h_ref.at[pl.ds(0, PAD_ROWS)], copy_sems.at[2]),
                  pltpu.make_async_copy(t_ref.at[0], tp_ref.at[pl.ds(CHUNK, SEQ)], copy_sems.at[3]),
                  pltpu.make_async_copy(zeros, tp_ref.at[pl.ds(0, CHUNK)], copy_sems.at[4])]
        for cp in copies:
            cp.start()
        forward()
        for cp in copies:
            cp.wait()
        finish()

    anywhere = pl.BlockSpec(memory_space=pl.ANY)
    padded = jax.ShapeDtypeStruct((LP, D), F32)
    return pl.pallas_call(
        body, name=name,
        in_specs=[anywhere] * 4, out_specs=[anywhere] * 3,
        out_shape=_gathered_shapes([shard]) + [padded, padded],
        scratch_shapes=[pltpu.VMEM((CHUNK, D), F32)] + _exchange_sems(1) + [pltpu.SemaphoreType.DMA((5,))],
    )(shard, meta, x, target)


def _gathered_shapes(xs):
    return [jax.ShapeDtypeStruct((N_DEV,) + x.shape, x.dtype) for x in xs]


def _exchange_sems(n):
    if n == 0:
        return []
    return [pltpu.SemaphoreType.DMA((7 * n,)), pltpu.SemaphoreType.DMA((7 * n,)), pltpu.SemaphoreType.DMA((n,))]


def _gather_phases(x_refs, out_refs, send_sems, recv_sems, local_sems):
    n = len(x_refs)
    mx, my, mc = lax.axis_index("x"), lax.axis_index("y"), lax.axis_index("c")
    me, sibling = (mx, my, mc), (mx, my, 1 - mc)
    chips = [(1 - mx, my), (mx, 1 - my), (1 - mx, 1 - my)]

    def slot(a, px, py, pc):
        return out_refs[a].at[4 * px + 2 * py + pc]

    def copy(a, k, block, to, src=None):
        return pltpu.make_async_remote_copy(
            src_ref=slot(a, *block) if src is None else src, dst_ref=slot(a, *block),
            send_sem=send_sems.at[7 * a + k], recv_sem=recv_sems.at[7 * a + k],
            device_id=to, device_id_type=MESH_IDS)

    mine = [pltpu.make_async_copy(x_refs[a], slot(a, *me), local_sems.at[a]) for a in range(n)]
    first = []
    for a in range(n):
        first.append(copy(a, 0, me, sibling, src=x_refs[a]))
        first += [copy(a, 1 + j, me, (*chip, mc), src=x_refs[a]) for j, chip in enumerate(chips)]
    passed = [copy(a, 4 + j, (*chip, mc), sibling) for j, chip in enumerate(chips) for a in range(n)]

    def start():
        for cp in mine + first:
            cp.start()

    def forward():
        for j, chip in enumerate(chips):
            for a in range(n):
                copy(a, 1 + j, (*chip, mc), me).wait_recv()
                passed[j * n + a].start()

    def finish():
        for a in range(n):
            copy(a, 0, sibling, me).wait_recv()
            for j, chip in enumerate(chips):
                copy(a, 4 + j, (*chip, 1 - mc), me).wait_recv()
        for cp in first + passed:
            cp.wait_send()
        for cp in mine:
            cp.wait()

    return start, forward, finish


def _exchange_blocks(gs, name):
    n = len(gs)

    def body(*refs):
        start, finish = _exchange_phases(refs[:n], refs[n:2 * n], *refs[2 * n:])
        start()
        finish()

    return pl.pallas_call(
        body, name=name,
        in_specs=[pl.BlockSpec(memory_space=pl.ANY)] * n,
        out_specs=[pl.BlockSpec(memory_space=pl.ANY)] * n,
        out_shape=[jax.ShapeDtypeStruct(g.shape, g.dtype) for g in gs],
        scratch_shapes=_exchange_sems(n),
    )(*gs)


def _exchange_phases(g_refs, out_refs, send_sems, recv_sems, local_sems):
    n = len(g_refs)
    mx, my, mc = lax.axis_index("x"), lax.axis_index("y"), lax.axis_index("c")
    me = 4 * mx + 2 * my + mc
    mine = [pltpu.make_async_copy(g_refs[a].at[me], out_refs[a].at[me], local_sems.at[a]) for a in range(n)]
    copies = []
    for r in range(1, N_DEV):
        px, py, pc = mx ^ (r >> 2), my ^ ((r >> 1) & 1), mc ^ (r & 1)
        peer = 4 * px + 2 * py + pc
        for a in range(n):
            copies.append(pltpu.make_async_remote_copy(
                src_ref=g_refs[a].at[peer], dst_ref=out_refs[a].at[me],
                send_sem=send_sems.at[7 * a + r - 1], recv_sem=recv_sems.at[7 * a + r - 1],
                device_id=(px, py, pc), device_id_type=MESH_IDS))

    def start():
        for cp in mine + copies:
            cp.start()

    def finish():
        for cp in copies:
            cp.wait_recv()
        for cp in copies:
            cp.wait_send()
        for cp in mine:
            cp.wait()

    return start, finish


IN_SHARD = IN_W // N_DEV
IN_SHARD_P = 512
UP_SHARD = D_UP // N_DEV
UP_SHARD_P = 768
RELAYOUT_ROWS = 256


def _pieces_w_in():
    return [(k, 0, IN_SHARD * k, IN_SHARD) for k in range(N_DEV)]


def _pieces_ffn_up():
    pieces = []
    for k in range(N_DEV):
        n, end = UP_SHARD * k, UP_SHARD * (k + 1)
        while n < end:
            half, r = divmod(n, D_FF)
            blk, off = divmod(r, CONV_BLOCK)
            run = min(CONV_BLOCK - off, end - n)
            pieces.append((k, n - UP_SHARD * k, 2 * CONV_BLOCK * blk + CONV_BLOCK * half + off, run))
            n += run
    return pieces


def _assemble_block(load, spans, dst_block, rows):
    lo = 128 * dst_block
    lane = lax.broadcasted_iota(jnp.int32, (1, 128), 1)
    out = jnp.zeros((rows, 128), F32)
    for key, src_off, dst_off, length in spans:
        a, b = max(lo, dst_off), min(lo + 128, dst_off + length)
        s, s_end = src_off + (a - dst_off), src_off + (b - dst_off)
        d = a
        while s < s_end:
            e = min(s_end, 128 * (s // 128 + 1))
            blk = load(key, s // 128)
            shift = (d - s) % 128
            if shift:
                blk = pltpu.roll(blk, shift, 1)
            out = jnp.where((lane >= d - lo) & (lane < d - lo + (e - s)), blk, out)
            d += e - s
            s = e
    return out


def _shards_to_cols(shards, pieces, width, name):
    _, rows, _ = shards.shape
    tr = RELAYOUT_ROWS

    def body(s_ref, o_ref):
        load = lambda k, b: s_ref[k, :, 128 * b:128 * (b + 1)].astype(F32)
        for db in range(width // 128):
            o_ref[:, 128 * db:128 * (db + 1)] = _assemble_block(load, pieces, db, tr).astype(BF16)

    return pl.pallas_call(
        body, name=name, grid=(rows // tr,),
        in_specs=[pl.BlockSpec((N_DEV, tr, shards.shape[2]), lambda i: (0, i, 0))],
        out_specs=pl.BlockSpec((tr, width), lambda i: (i, 0)),
        out_shape=jax.ShapeDtypeStruct((rows, width), BF16),
        compiler_params=_cparams("parallel"),
    )(shards)


def _cols_to_shards(full, pieces, shard_width, name):
    rows, width = full.shape
    tr = RELAYOUT_ROWS

    def body(f_ref, o_ref):
        load = lambda _, b: f_ref[:, 128 * b:128 * (b + 1)].astype(F32)
        for k in range(N_DEV):
            spans = [(None, dst_off, src_off, length) for dev, src_off, dst_off, length in pieces if dev == k]
            for db in range(shard_width // 128):
                o_ref[k, :, 128 * db:128 * (db + 1)] = _assemble_block(load, spans, db, tr).astype(BF16)

    return pl.pallas_call(
        body, name=name, grid=(rows // tr,),
        in_specs=[pl.BlockSpec((tr, width), lambda i: (i, 0))],
        out_specs=pl.BlockSpec((N_DEV, tr, shard_width), lambda i: (0, i, 0)),
        out_shape=jax.ShapeDtypeStruct((N_DEV, rows, shard_width), BF16),
        compiler_params=_cparams("parallel"),
    )(full)


def _adamw(parts, w, m, v, rows_per_step, name):
    rows, cols = w.shape
    assert rows % rows_per_step == 0 and parts.shape == (N_DEV, rows, cols)

    def body(p_ref, w_ref, m_ref, v_ref, g_ref, d_ref, nm_ref, nv_ref):
        g = p_ref[0].astype(F32)
        for j in range(1, N_DEV):
            g = g + p_ref[j].astype(F32)
        m_new = ADAM_B1 * m_ref[...] + (1.0 - ADAM_B1) * g
        v_new = ADAM_B2 * v_ref[...] + (1.0 - ADAM_B2) * (g * g)
        m_hat = m_new / (1.0 - ADAM_B1 ** ADAM_STEP)
        v_hat = v_new / (1.0 - ADAM_B2 ** ADAM_STEP)
        g_ref[...] = g
        d_ref[...] = -ADAM_LR * (m_hat / (jnp.sqrt(v_hat) + ADAM_EPS) + ADAM_WD * w_ref[...])
        nm_ref[...] = m_new
        nv_ref[...] = v_new

    tile = pl.BlockSpec((rows_per_step, cols), lambda i: (i, 0))
    shape = jax.ShapeDtypeStruct((rows, cols), F32)
    return pl.pallas_call(
        body, name=name, grid=(rows // rows_per_step,),
        in_specs=[pl.BlockSpec((N_DEV, rows_per_step, cols), lambda i: (0, i, 0)), tile, tile, tile],
        out_specs=[tile, tile, tile, tile],
        out_shape=[shape, shape, shape, shape],
        compiler_params=_cparams("parallel"),
    )(parts, w, m, v)


BIG = (("w_in", (DEPTH, D, IN_W // N_DEV), 2), ("w_out", (DEPTH, D // N_DEV, D), 1),
       ("ffn_up", (DEPTH, D, D_UP // N_DEV), 2), ("ffn_down", (DEPTH, D_FF // N_DEV, D), 1))
SMALL = (("meta_tokens", (N_META, D // N_DEV), 1), ("gla_gate_w2", (DEPTH, GATE_RANK, 256 // N_DEV), 2),
         ("ffn_conv_w", (DEPTH, 3, D_UP // N_DEV), 2))
REPL = (("pre_mix_norm", (DEPTH, D)), ("gla_gate_b", (DEPTH, 256)), ("ret_norm_w", (DEPTH, 512)),
        ("gla_norm_w", (DEPTH, 512)), ("post_mix_norm", (DEPTH, D)), ("pre_ffn_norm", (DEPTH, D)),
        ("ffn_conv_b", (DEPTH, D_UP)), ("post_ffn_norm", (DEPTH, D)))
WEIGHT_ORDER = ("meta_tokens", "pre_mix_norm", "w_in", "gla_gate_w2", "gla_gate_b", "ret_norm_w", "gla_norm_w",
                "w_out", "post_mix_norm", "pre_ffn_norm", "ffn_up", "ffn_conv_w", "ffn_conv_b", "ffn_down",
                "post_ffn_norm")


def _size(shape):
    return math.prod(shape)


def _round_up(n, mult):
    return -(-n // mult) * mult


REPL_ROWS = _round_up(-(-sum(_size(s) for _, s in REPL) // LANES), 8)
SMALL_ROWS = _round_up(-(-sum(_size(s) for _, s, _ in SMALL) // LANES), 8)


def _pack(arrays, rows, dtype):
    flat = jnp.concatenate([a.reshape(-1).astype(dtype) for a in arrays])
    return jnp.pad(flat, (0, rows * LANES - flat.shape[0])).reshape(rows, LANES)


def _unpack(buf, shapes):
    flat = buf.reshape(-1)
    out, off = [], 0
    for shape in shapes:
        out.append(flat[off:off + _size(shape)].reshape(shape))
        off += _size(shape)
    return out


def _unshard(blocks, axis):
    moved = jnp.moveaxis(blocks, 0, axis)
    shape = list(moved.shape)
    shape[axis:axis + 2] = [shape[axis] * shape[axis + 1]]
    return moved.reshape(shape)


def _to_blocks(full, axis):
    shape = list(full.shape)
    shape[axis:axis + 1] = [N_DEV, shape[axis] // N_DEV]
    return jnp.moveaxis(full.reshape(shape), axis, 0)


def _interleave_cols(w):
    lead = w.shape[:-1]
    return jnp.swapaxes(w.reshape(lead + (2, N_CONV_BLOCKS, CONV_BLOCK)), -3, -2).reshape(lead + (D_UP,))


def _deinterleave_cols(w):
    lead = w.shape[:-1]
    return jnp.swapaxes(w.reshape(lead + (N_CONV_BLOCKS, 2, CONV_BLOCK)), -3, -2).reshape(lead + (D_UP,))


def _rope_tables():
    half = RET_DK // 2
    inv = ROPE_BASE ** (-jnp.arange(half, dtype=F32) / half)
    pos = jnp.arange(LP, dtype=F32) - float(PAD_ROWS)
    ang = pos[:, None] * inv[None, :]
    c, s = jnp.cos(ang), jnp.sin(ang)
    return jnp.concatenate([c, c], axis=1), jnp.concatenate([-s, s], axis=1)


def kernel(x, meta_tokens, pre_mix_norm, w_in, gla_gate_w2, gla_gate_b, ret_norm_w, gla_norm_w, w_out, post_mix_norm, pre_ffn_norm, ffn_up, ffn_conv_w, ffn_conv_b, ffn_down, post_ffn_norm, loss_target, m_meta_tokens, m_pre_mix_norm, m_w_in, m_gla_gate_w2, m_gla_gate_b, m_ret_norm_w, m_gla_norm_w, m_w_out, m_post_mix_norm, m_pre_ffn_norm, m_ffn_up, m_ffn_conv_w, m_ffn_conv_b, m_ffn_down, m_post_ffn_norm, v_meta_tokens, v_pre_mix_norm, v_w_in, v_gla_gate_w2, v_gla_gate_b, v_ret_norm_w, v_gla_norm_w, v_w_out, v_post_mix_norm, v_pre_ffn_norm, v_ffn_up, v_ffn_conv_w, v_ffn_conv_b, v_ffn_down, v_post_ffn_norm):
    weights = dict(meta_tokens=meta_tokens, pre_mix_norm=pre_mix_norm, w_in=w_in, gla_gate_w2=gla_gate_w2,
                   gla_gate_b=gla_gate_b, ret_norm_w=ret_norm_w, gla_norm_w=gla_norm_w, w_out=w_out,
                   post_mix_norm=post_mix_norm, pre_ffn_norm=pre_ffn_norm, ffn_up=ffn_up, ffn_conv_w=ffn_conv_w,
                   ffn_conv_b=ffn_conv_b, ffn_down=ffn_down, post_ffn_norm=post_ffn_norm)
    mom1 = dict(meta_tokens=m_meta_tokens, pre_mix_norm=m_pre_mix_norm, w_in=m_w_in, gla_gate_w2=m_gla_gate_w2,
                gla_gate_b=m_gla_gate_b, ret_norm_w=m_ret_norm_w, gla_norm_w=m_gla_norm_w, w_out=m_w_out,
                post_mix_norm=m_post_mix_norm, pre_ffn_norm=m_pre_ffn_norm, ffn_up=m_ffn_up,
                ffn_conv_w=m_ffn_conv_w, ffn_conv_b=m_ffn_conv_b, ffn_down=m_ffn_down, post_ffn_norm=m_post_ffn_norm)
    mom2 = dict(meta_tokens=v_meta_tokens, pre_mix_norm=v_pre_mix_norm, w_in=v_w_in, gla_gate_w2=v_gla_gate_w2,
                gla_gate_b=v_gla_gate_b, ret_norm_w=v_ret_norm_w, gla_norm_w=v_gla_norm_w, w_out=v_w_out,
                post_mix_norm=v_post_mix_norm, pre_ffn_norm=v_pre_ffn_norm, ffn_up=v_ffn_up,
                ffn_conv_w=v_ffn_conv_w, ffn_conv_b=v_ffn_conv_b, ffn_down=v_ffn_down, post_ffn_norm=v_post_ffn_norm)

    pad_cols = lambda a, width: jnp.pad(a, ((0, 0), (0, width - a.shape[1])))
    big_names = [n for n, _, _ in BIG]
    shard = {}
    for l in range(DEPTH):
        shard[l, "w_in"] = pad_cols(w_in[l].astype(BF16), IN_SHARD_P)
        shard[l, "w_out"] = w_out[l].astype(BF16)
        shard[l, "ffn_up"] = pad_cols(ffn_up[l].astype(BF16), UP_SHARD_P)
        shard[l, "ffn_down"] = ffn_down[l].astype(BF16)
    gather_in_mixer = {l: [(l, n) for n in big_names[1:]] for l in range(DEPTH)}
    gather_in_conv = {l: [(l + 1, "w_in")] for l in range(DEPTH - 1)}
    small = _all_gather([_pack([weights[n] for n, _, _ in SMALL], SMALL_ROWS, F32)], "gather_small_weights")[0]
    small_parts = _unpack_blocks(small, [s for _, s, _ in SMALL])
    full = {n: _unshard(p, ax) for (n, _, ax), p in zip(SMALL, small_parts)}
    w_in_0, h, target = _gather_and_pad(shard[0, "w_in"], full["meta_tokens"], x, loss_target, "gather_w_in_0")
    gathered = {(0, "w_in"): w_in_0}
    w2p = jnp.pad(full["gla_gate_w2"], ((0, 0), (0, 128 - GATE_RANK), (0, 0)))
    cw8 = jnp.concatenate([_interleave_cols(full["ffn_conv_w"]), _interleave_cols(ffn_conv_b)[:, None, :],
                           jnp.zeros((DEPTH, 4, D_UP), F32)], axis=1)
    cos2, sin2 = _rope_tables()

    saved, layer_w = [], []
    for l in range(DEPTH):
        lw = dict(w_in=_shards_to_cols(gathered[l, "w_in"], _pieces_w_in(), IN_WP, f"w_in_cols_{l}"))
        a1, proj = _norm_matmul(h, pre_mix_norm[l:l + 1], lw["w_in"], out_dtype=F32, tm=TM_BIG, tn=IN_WP // 3,
                                name=f"in_proj_{l}")
        keys = gather_in_mixer.get(l, [])
        ocat, merged, sr_all, sg_all, *got = _mixer_fwd(proj, cos2, sin2, w2p[l], gla_gate_b[l:l + 1],
                                                        ret_norm_w[l:l + 1], gla_norm_w[l:l + 1], f"mixer_fwd_{l}",
                                                        carried=[shard[key] for key in keys])
        gathered.update(zip(keys, got))
        lw["w_out"] = gathered[l, "w_out"].reshape(D, D)
        lw["w_up"] = _shards_to_cols(gathered[l, "ffn_up"], _pieces_ffn_up(), D_UP, f"ffn_up_cols_{l}")
        lw["w_down"] = gathered[l, "ffn_down"].reshape(D_FF, D)
        layer_w.append(lw)
        m, h1 = _matmul_resid_norm(merged, lw["w_out"], h, post_mix_norm[l:l + 1], f"out_proj_{l}")
        a2, u = _norm_matmul(h1, pre_ffn_norm[l:l + 1], lw["w_up"], out_dtype=BF16, tm=TM_BIG, tn=D_UP // 4,
                             name=f"ffn_up_{l}")
        keys = gather_in_conv.get(l, [])
        cv, act, *got = _conv_act_fwd(u, cw8[l], f"ffn_conv_act_{l}", carried=[shard[key] for key in keys])
        gathered.update(zip(keys, got))
        f, h2, *loss_acc = _matmul_resid_norm(act, lw["w_down"], h1, post_ffn_norm[l:l + 1], f"ffn_down_{l}",
                                              target=target if l == DEPTH - 1 else None)
        saved.append(dict(h=h, a1=a1, proj=proj, ocat=ocat, merged=merged, sr=sr_all, sg=sg_all, m=m, h1=h1,
                          a2=a2, u=u, cv=cv, act=act, f=f))
        h = h2

    dh = h
    loss = lax.psum(loss_acc[0][0, 0], ("x", "y", "c"))

    kinds = ("grad", "delta", "new_m", "new_v")
    grads = {n: [None] * DEPTH for n in WEIGHT_ORDER if n != "meta_tokens" and n not in big_names}
    pending, parts = [], {}
    for l in reversed(range(DEPTH)):
        s, lw = saved[l], layer_w[l]
        dact, df, g_post_ffn = _norm_bwd_matmul(dh, s["f"], post_ffn_norm[l:l + 1], lw["w_down"], BF16,
                                                f"ffn_down_dx_{l}")
        g_down = _matmul(s["act"], df, ta=True, out_dtype=BF16, tm=D_FF // 2, tn=D, tk=TM_BIG, name=f"ffn_down_dw_{l}")
        du, dcw = _conv_act_bwd(dact, s["cv"], s["u"], cw8[l], f"ffn_conv_act_bwd_{l}")
        dh1, g_pre_ffn = _matmul_norm_bwd(du, lw["w_up"], s["h1"], pre_ffn_norm[l:l + 1], dh, D_FF, f"ffn_up_dx_{l}")
        g_up = _matmul(s["a2"], du, ta=True, out_dtype=BF16, tm=D, tn=D_FF, tk=TM_BIG, name=f"ffn_up_dw_{l}")
        dmerged, dm, g_post_mix = _norm_bwd_matmul(dh1, s["m"], post_mix_norm[l:l + 1], lw["w_out"], F32,
                                                   f"out_proj_dx_{l}")
        g_out = _matmul(s["merged"], dm, ta=True, out_dtype=BF16, tm=D, tn=D, tk=TM_BIG, name=f"out_proj_dw_{l}")
        pending += [((l, "ffn_down"), g_down.reshape(N_DEV, D_FF // N_DEV, D)),
                    ((l, "ffn_up"), _cols_to_shards(g_up, _pieces_ffn_up(), UP_SHARD_P, f"ffn_up_grad_shards_{l}")),
                    ((l, "w_out"), g_out.reshape(N_DEV, D // N_DEV, D))]
        dproj, g_w2, g_gb, g_rn, g_gn, *got = _mixer_bwd(s["proj"], s["ocat"], dmerged, s["sr"], s["sg"], cos2, sin2,
                                                         w2p[l], gla_gate_b[l:l + 1], ret_norm_w[l:l + 1],
                                                         gla_norm_w[l:l + 1], f"mixer_bwd_{l}",
                                                         carried=[blocks for _, blocks in pending])
        parts.update(zip([key for key, _ in pending], got))
        g_in = _matmul(s["a1"], dproj, ta=True, out_dtype=BF16, tm=D, tn=IN_WP // 2, tk=TM_BIG, name=f"in_proj_dw_{l}")
        pending = [((l, "w_in"), _cols_to_shards(g_in, _pieces_w_in(), IN_SHARD_P, f"w_in_grad_shards_{l}"))]
        now = pending if l == 0 else []
        dh, g_pre_mix, *got = _matmul_norm_bwd(dproj, lw["w_in"], s["h"], pre_mix_norm[l:l + 1], dh1, IN_WP,
                                               f"in_proj_dx_{l}", carried=[blocks for _, blocks in now])
        parts.update(zip([key for key, _ in now], got))
        pending = [] if l == 0 else pending
        grads["post_ffn_norm"][l] = g_post_ffn[0]
        grads["ffn_conv_w"][l] = _deinterleave_cols(dcw[0:3])
        grads["ffn_conv_b"][l] = _deinterleave_cols(dcw[3])
        grads["pre_ffn_norm"][l] = g_pre_ffn[0]
        grads["post_mix_norm"][l] = g_post_mix[0]
        grads["gla_gate_w2"][l] = g_w2[:GATE_RANK]
        grads["gla_gate_b"][l] = g_gb[0]
        grads["ret_norm_w"][l] = g_rn[0]
        grads["gla_norm_w"][l] = g_gn[0]
        grads["pre_mix_norm"][l] = g_pre_mix[0]
    local = {n: jnp.stack(v) for n, v in grads.items()}
    local["meta_tokens"] = dh[PAD_ROWS:CHUNK]
    grad_x = dh[CHUNK:][None]

    blocks = jnp.concatenate([_to_blocks(local[n], ax).reshape(N_DEV, -1) for n, _, ax in SMALL], axis=1)
    blocks = jnp.pad(blocks, ((0, 0), (0, SMALL_ROWS * LANES - blocks.shape[1]))).reshape(N_DEV, SMALL_ROWS, LANES)
    *got, small_grad_parts = _exchange_blocks([b for _, b in pending] + [blocks], "exchange_last_grads")
    parts.update(zip([key for key, _ in pending], got))

    widths = dict(w_in=IN_SHARD_P, w_out=D, ffn_up=UP_SHARD_P, ffn_down=D)
    steps = dict(w_in=256, w_out=D // N_DEV, ffn_up=256, ffn_down=D_FF // N_DEV // 2)
    big_out = {kind: {n: [None] * DEPTH for n in big_names} for kind in kinds}
    for l in range(DEPTH):
        for n in big_names:
            mine = [pad_cols(d[n][l], widths[n]) for d in (weights, mom1, mom2)]
            results = _adamw(parts[l, n], *mine, steps[n], f"adamw_{n}_{l}")
            for kind, r in zip(kinds, results):
                big_out[kind][n][l] = r[:, :weights[n].shape[2]]
    out = {kind: {n: jnp.stack(v) for n, v in big_out[kind].items()} for kind in kinds}
    shard_shapes = [s for _, s, _ in SMALL]
    packed = [_pack([d[n] for n, _, _ in SMALL], SMALL_ROWS, F32) for d in (weights, mom1, mom2)]
    results = _adamw(small_grad_parts, *packed, SMALL_ROWS, "adamw_small_sharded")
    for kind, buf in zip(kinds, results):
        out[kind].update(zip([n for n, _, _ in SMALL], _unpack(buf, shard_shapes)))

    repl_parts = _all_gather([_pack([local[n] for n, _ in REPL], REPL_ROWS, F32)], "gather_small_grads")[0]
    packed = [_pack([d[n] for n, _ in REPL], REPL_ROWS, F32) for d in (weights, mom1, mom2)]
    results = _adamw(repl_parts, *packed, REPL_ROWS, "adamw_replicated")
    repl_shapes = [s for _, s in REPL]
    for kind, buf in zip(kinds, results):
        out[kind].update(zip([n for n, _ in REPL], _unpack(buf, repl_shapes)))

    return (loss, grad_x, *[out["grad"][n] for n in WEIGHT_ORDER], *[out["delta"][n] for n in WEIGHT_ORDER],
            *[out["new_m"][n] for n in WEIGHT_ORDER], *[out["new_v"][n] for n in WEIGHT_ORDER])


def _unpack_blocks(gathered, shapes):
    flat = gathered.reshape(N_DEV, -1)
    out, off = [], 0
    for shape in shapes:
        out.append(flat[:, off:off + _size(shape)].reshape((N_DEV,) + shape))
        off += _size(shape)
    return out
```

```python
import math

import jax
import jax.numpy as jnp
from jax import lax
from jax.experimental import pallas as pl
from jax.experimental.pallas import tpu as pltpu

F32 = jnp.float32
BF16 = jnp.bfloat16

D = 1024
SEQ = 8192
DEPTH = 2
N_META = 16
CHUNK = 64
SUB = 16
N_SUB = CHUNK // SUB
PAD_ROWS = CHUNK - N_META
LP = SEQ + CHUNK
N_CHUNKS = LP // CHUNK
RET_HEADS = 4
RET_DK = 128
GLA_HEADS = 4
GLA_DK = 64
GLA_DV = 128
GLA_TAU = 16.0
GATE_RANK = 16
IN_W = 3600
IN_WP = 3840
D_FF = 2816
D_UP = 2 * D_FF
CONV_BLOCK = 256
N_CONV_BLOCKS = D_FF // CONV_BLOCK
ROPE_BASE = 10000.0
EPS = 1e-6
N_DEV = 8
LANES = 1024

O_RQ, O_RK, O_RV, O_RG = 0, 512, 1024, 1536
O_GQ, O_GK, O_GV, O_GR, O_GA = 2048, 2304, 2560, 3072, 3584

ADAM_LR = 0.001
ADAM_B1 = 0.9
ADAM_B2 = 0.999
ADAM_EPS = 1e-08
ADAM_WD = 0.01
ADAM_STEP = 10

VMEM_LIMIT = 56 * 1024 * 1024
MESH_IDS = pl.DeviceIdType.MESH


def _row_tile(rows, limit):
    best = 16
    for t in range(16, min(rows, limit) + 1, 16):
        if rows % t == 0:
            best = t
    return best


TM = _row_tile(LP, 688)
TM_BIG = _row_tile(LP, 1376)


def _cparams(*sem):
    return pltpu.CompilerParams(dimension_semantics=sem, vmem_limit_bytes=VMEM_LIMIT)


def _dot(a, b):
    return jnp.dot(a.astype(BF16), b.astype(BF16), preferred_element_type=F32)


def _dot_nt(a, b):
    return lax.dot_general(a.astype(BF16), b.astype(BF16), (((1,), (1,)), ((), ())), preferred_element_type=F32)


def _dot_tn(a, b):
    return lax.dot_general(a.astype(BF16), b.astype(BF16), (((0,), (0,)), ((), ())), preferred_element_type=F32)


def _split3(x):
    hi = x.astype(BF16)
    r1 = x - hi.astype(F32)
    mid = r1.astype(BF16)
    lo = (r1 - mid.astype(F32)).astype(BF16)
    return hi, mid, lo


def _dot_exact_rhs(t, x):
    n = x.shape[1]
    parts = jnp.dot(t.astype(BF16), jnp.concatenate(_split3(x), axis=1), preferred_element_type=F32)
    return parts[:, :n] + parts[:, n:2 * n] + parts[:, 2 * n:]


def _dot_tn_exact_lhs(x, ones):
    n = x.shape[1]
    parts = lax.dot_general(jnp.concatenate(_split3(x), axis=1), ones.astype(BF16), (((0,), (0,)), ((), ())),
                            preferred_element_type=F32)
    return parts[:n] + parts[n:2 * n] + parts[2 * n:]


def _sigmoid(x):
    return 1.0 / (1.0 + jnp.exp(-x))


def _matmul(a, b, *, ta=False, tb=False, out_dtype, tm, tn, tk, name):
    m = a.shape[1] if ta else a.shape[0]
    k = a.shape[0] if ta else a.shape[1]
    n = b.shape[0] if tb else b.shape[1]
    assert (b.shape[1] if tb else b.shape[0]) == k
    assert m % tm == 0 and n % tn == 0 and k % tk == 0, (name, m, n, k, tm, tn, tk)
    nk = k // tk
    a_spec = pl.BlockSpec((tk, tm), lambda i, j, kk: (kk, i)) if ta else pl.BlockSpec((tm, tk), lambda i, j, kk: (i, kk))
    b_spec = pl.BlockSpec((tn, tk), lambda i, j, kk: (j, kk)) if tb else pl.BlockSpec((tk, tn), lambda i, j, kk: (kk, j))
    dims = (((0 if ta else 1,), (1 if tb else 0,)), ((), ()))

    def body(a_ref, b_ref, o_ref, *acc):
        prod = lax.dot_general(a_ref[...].astype(BF16), b_ref[...].astype(BF16), dims, preferred_element_type=F32)
        if nk == 1:
            o_ref[...] = prod.astype(out_dtype)
            return
        acc_ref, = acc
        kk = pl.program_id(2)

        @pl.when(kk == 0)
        def _():
            acc_ref[...] = prod

        @pl.when(kk > 0)
        def _():
            acc_ref[...] += prod

        @pl.when(kk == nk - 1)
        def _():
            o_ref[...] = acc_ref[...].astype(out_dtype)

    return pl.pallas_call(
        body, name=name, grid=(m // tm, n // tn, nk),
        in_specs=[a_spec, b_spec],
        out_specs=pl.BlockSpec((tm, tn), lambda i, j, kk: (i, j)),
        out_shape=jax.ShapeDtypeStruct((m, n), out_dtype),
        scratch_shapes=[pltpu.VMEM((tm, tn), F32)] if nk > 1 else [],
        compiler_params=_cparams("parallel", "parallel", "arbitrary"),
    )(a, b)


def _norm_matmul(x, w, b, *, out_dtype, tm, tn, name):
    n = b.shape[1]
    assert LP % tm == 0 and n % tn == 0

    def body(x_ref, w_ref, b_ref, a_ref, o_ref, a_scr):
        @pl.when(pl.program_id(1) == 0)
        def _():
            xv = x_ref[...]
            r = lax.rsqrt(jnp.mean(xv * xv, axis=-1, keepdims=True) + EPS)
            a = (xv * r * w_ref[...]).astype(BF16)
            a_scr[...] = a
            a_ref[...] = a

        o_ref[...] = jnp.dot(a_scr[...], b_ref[...], preferred_element_type=F32).astype(out_dtype)

    return pl.pallas_call(
        body, name=name, grid=(LP // tm, n // tn),
        in_specs=[pl.BlockSpec((tm, D), lambda i, j: (i, 0)), pl.BlockSpec((1, D), lambda i, j: (0, 0)),
                  pl.BlockSpec((D, tn), lambda i, j: (0, j))],
        out_specs=[pl.BlockSpec((tm, D), lambda i, j: (i, 0)), pl.BlockSpec((tm, tn), lambda i, j: (i, j))],
        out_shape=[jax.ShapeDtypeStruct((LP, D), BF16), jax.ShapeDtypeStruct((LP, n), out_dtype)],
        scratch_shapes=[pltpu.VMEM((tm, D), BF16)],
        compiler_params=_cparams("arbitrary", "arbitrary"),
    )(x, w, b)


def _matmul_resid_norm(a, b, h, w, name, target=None):
    k = a.shape[1]
    has_loss = target is not None

    def body(a_ref, b_ref, h_ref, w_ref, *refs):
        m = jnp.dot(a_ref[...].astype(BF16), b_ref[...].astype(BF16), preferred_element_type=F32)
        r = lax.rsqrt(jnp.mean(m * m, axis=-1, keepdims=True) + EPS)
        i = pl.program_id(0)
        row = i * TM + lax.broadcasted_iota(jnp.int32, (TM, 1), 0)
        y = h_ref[...] + jnp.where(row >= PAD_ROWS, m * r * w_ref[...], 0.0)
        if not has_loss:
            m_ref, y_ref = refs
            m_ref[...] = m
            y_ref[...] = y
            return
        t_ref, m_ref, dy_ref, loss_ref = refs
        m_ref[...] = m

        @pl.when(i == 0)
        def _():
            loss_ref[...] = jnp.zeros_like(loss_ref)

        diff = jnp.where(row >= CHUNK, y - t_ref[...], 0.0)
        dy_ref[...] = diff * (1.0 / D)
        loss_ref[...] += (0.5 / D) * jnp.sum(diff * diff)

    tile = pl.BlockSpec((TM, D), lambda i: (i, 0))
    shape = jax.ShapeDtypeStruct((LP, D), F32)
    in_specs = [pl.BlockSpec((TM, k), lambda i: (i, 0)), pl.BlockSpec((k, D), lambda i: (0, 0)), tile,
                pl.BlockSpec((1, D), lambda i: (0, 0))]
    if has_loss:
        return pl.pallas_call(
            body, name=name, grid=(LP // TM,),
            in_specs=in_specs + [tile],
            out_specs=[tile, tile, pl.BlockSpec((8, 128), lambda i: (0, 0))],
            out_shape=[shape, shape, jax.ShapeDtypeStruct((8, 128), F32)],
            compiler_params=_cparams("arbitrary"),
        )(a, b, h, w, target)
    return pl.pallas_call(
        body, name=name, grid=(LP // TM,),
        in_specs=in_specs, out_specs=[tile, tile], out_shape=[shape, shape],
        compiler_params=_cparams("parallel"),
    )(a, b, h, w)


def _rmsnorm_bwd_rows(dy, x, w):
    r = lax.rsqrt(jnp.mean(x * x, axis=-1, keepdims=True) + EPS)
    g = dy * w
    dx = r * g - x * (r * r * r * jnp.mean(g * x, axis=-1, keepdims=True))
    return dx, jnp.sum(dy * x * r, axis=0, keepdims=True)


def _matmul_norm_bwd(dz, b, x, w, resid, tk, name, carried=()):
    k = dz.shape[1]
    assert k % tk == 0
    nk = k // tk
    n_rows = LP // TM
    n_carried = len(carried)

    def body(*refs):
        a_ref, b_ref, x_ref, w_ref, r_ref = refs[:5]
        g_refs, refs = refs[5:5 + n_carried], refs[5 + n_carried:]
        dx_ref, dw_ref = refs[:2]
        got_refs, refs = refs[2:2 + n_carried], refs[2 + n_carried:]
        acc, sems = (refs[:1], refs[1:]) if nk > 1 else ((), refs)
        i, kk = pl.program_id(0), pl.program_id(1)
        if n_carried:
            exchange_start, exchange_finish = _exchange_phases(g_refs, got_refs, *sems)
            pl.when((i == 0) & (kk == 0))(exchange_start)

        @pl.when((i == 0) & (kk == 0))
        def _():
            dw_ref[...] = jnp.zeros_like(dw_ref)

        prod = lax.dot_general(a_ref[...].astype(BF16), b_ref[...].astype(BF16), (((1,), (1,)), ((), ())),
                               preferred_element_type=F32)

        def finish(dy):
            dx, dw = _rmsnorm_bwd_rows(dy, x_ref[...], w_ref[...])
            dx_ref[...] = dx + r_ref[...]
            dw_ref[0:1, :] += dw

        if nk == 1:
            finish(prod)
        else:
            acc_ref, = acc

            @pl.when(kk == 0)
            def _():
                acc_ref[...] = prod

            @pl.when((kk > 0) & (kk < nk - 1))
            def _():
                acc_ref[...] += prod

            @pl.when(kk == nk - 1)
            def _():
                finish(acc_ref[...] + prod)

        if n_carried:
            pl.when((i == n_rows - 1) & (kk == nk - 1))(exchange_finish)

    tile = pl.BlockSpec((TM, D), lambda i, kk: (i, 0))
    anywhere = [pl.BlockSpec(memory_space=pl.ANY)] * n_carried
    return pl.pallas_call(
        body, name=name, grid=(n_rows, nk),
        in_specs=[pl.BlockSpec((TM, tk), lambda i, kk: (i, kk)), pl.BlockSpec((D, tk), lambda i, kk: (0, kk)), tile,
                  pl.BlockSpec((1, D), lambda i, kk: (0, 0)), tile] + anywhere,
        out_specs=[tile, pl.BlockSpec((8, D), lambda i, kk: (0, 0))] + anywhere,
        out_shape=[jax.ShapeDtypeStruct((LP, D), F32), jax.ShapeDtypeStruct((8, D), F32)]
        + [jax.ShapeDtypeStruct(g.shape, g.dtype) for g in carried],
        scratch_shapes=([pltpu.VMEM((TM, D), F32)] if nk > 1 else []) + _exchange_sems(n_carried),
        compiler_params=_cparams("arbitrary", "arbitrary"),
    )(dz, b, x, w, resid, *carried)


def _norm_bwd_matmul(dh, x, w, b, out_dtype, name):
    n = b.shape[0]

    def body(dh_ref, x_ref, w_ref, b_ref, o_ref, dx_ref, dw_ref):
        i = pl.program_id(0)

        @pl.when(i == 0)
        def _():
            dw_ref[...] = jnp.zeros_like(dw_ref)

        row = i * TM + lax.broadcasted_iota(jnp.int32, (TM, 1), 0)
        dy = jnp.where(row >= PAD_ROWS, dh_ref[...], 0.0)
        dx, dw = _rmsnorm_bwd_rows(dy, x_ref[...], w_ref[...])
        dxb = dx.astype(BF16)
        dx_ref[...] = dxb
        dw_ref[0:1, :] += dw
        o_ref[...] = lax.dot_general(dxb, b_ref[...].astype(BF16), (((1,), (1,)), ((), ())),
                                     preferred_element_type=F32).astype(out_dtype)

    tile = pl.BlockSpec((TM, D), lambda i: (i, 0))
    return pl.pallas_call(
        body, name=name, grid=(LP // TM,),
        in_specs=[tile, tile, pl.BlockSpec((1, D), lambda i: (0, 0)), pl.BlockSpec((n, D), lambda i: (0, 0))],
        out_specs=[pl.BlockSpec((TM, n), lambda i: (i, 0)), tile, pl.BlockSpec((8, D), lambda i: (0, 0))],
        out_shape=[jax.ShapeDtypeStruct((LP, n), out_dtype), jax.ShapeDtypeStruct((LP, D), BF16),
                   jax.ShapeDtypeStruct((8, D), F32)],
        compiler_params=_cparams("arbitrary"),
    )(dh, x, w, b)


GELU_C = math.sqrt(2.0 / math.pi)
GELU_K = 0.044715
STRIP = 16
HALF = 8


def _gelu_half(a):
    return 0.5 * jnp.tanh(a * (a * a * (GELU_C * GELU_K) + GELU_C)) + 0.5


def _gelu_slope(a, h):
    return h * (1.0 + (a - a * h) * (a * a * (6.0 * GELU_C * GELU_K) + 2.0 * GELU_C))


def _shift_down(x, prev8):
    row = lax.broadcasted_iota(jnp.int32, (8, 1), 0)
    r1, r2 = pltpu.roll(x, 1, 0), pltpu.roll(x, 2, 0)
    top1 = jnp.where(row < 1, pltpu.roll(prev8, 1, 0), r1[0:8, :])
    top2 = jnp.where(row < 2, pltpu.roll(prev8, 2, 0), r2[0:8, :])
    return jnp.concatenate([top1, r1[8:, :]], axis=0), jnp.concatenate([top2, r2[8:, :]], axis=0)


def _conv_act_fwd(u, cw8, name, carried=()):
    n_rows = LP // TM
    cb2 = 2 * CONV_BLOCK
    n_carried = len(carried)

    def body(*refs):
        u_ref, cw_ref = refs[:2]
        x_refs, refs = refs[2:2 + n_carried], refs[2 + n_carried:]
        conv_ref, act_ref = refs[:2]
        gathered_refs, refs = refs[2:2 + n_carried], refs[2 + n_carried:]
        carry_ref = refs[0]
        j, i = pl.program_id(0), pl.program_id(1)
        if n_carried:
            start, forward, finish = _gather_phases(x_refs, gathered_refs, *refs[1:])
            pl.when((j == 0) & (i == 0))(start)
            pl.when((j == (3 * N_CONV_BLOCKS) // 4) & (i == 0))(forward)

        @pl.when(i == 0)
        def _():
            carry_ref[...] = jnp.zeros_like(carry_ref)

        x = u_ref[...].astype(F32)
        x1, x2 = _shift_down(x, carry_ref[...])
        conv = cw_ref[3:4, :] + x2 * cw_ref[0:1, :] + x1 * cw_ref[1:2, :] + x * cw_ref[2:3, :]
        conv_ref[...] = conv.astype(BF16)
        a = conv[:, :CONV_BLOCK]
        g = conv[:, CONV_BLOCK:]
        act_ref[...] = (a * _gelu_half(a) * g).astype(BF16)
        carry_ref[...] = x[TM - 8:TM, :]
        if n_carried:
            pl.when((j == N_CONV_BLOCKS - 1) & (i == n_rows - 1))(finish)

    anywhere = [pl.BlockSpec(memory_space=pl.ANY)] * n_carried
    return pl.pallas_call(
        body, name=name, grid=(N_CONV_BLOCKS, n_rows),
        in_specs=[pl.BlockSpec((TM, cb2), lambda j, i: (i, j)), pl.BlockSpec((8, cb2), lambda j, i: (0, j))] + anywhere,
        out_specs=[pl.BlockSpec((TM, cb2), lambda j, i: (i, j)),
                   pl.BlockSpec((TM, CONV_BLOCK), lambda j, i: (i, j))] + anywhere,
        out_shape=[jax.ShapeDtypeStruct((LP, D_UP), BF16), jax.ShapeDtypeStruct((LP, D_FF), BF16)]
        + _gathered_shapes(carried),
        scratch_shapes=[pltpu.VMEM((8, cb2), F32)] + _exchange_sems(n_carried),
        compiler_params=_cparams("arbitrary", "arbitrary"),
    )(u, cw8, *carried)


def _conv_act_bwd(dact, conv, u, cw8, name):
    n_rows = LP // TM
    cb2 = 2 * CONV_BLOCK
    n_strips = TM // STRIP

    def body(dact_ref, conv_ref, u_ref, cw_ref, du_ref, dcw_ref, carry_ref):
        i = pl.program_id(1)

        @pl.when(i == 0)
        def _():
            dcw_ref[...] = jnp.zeros_like(dcw_ref)
            carry_ref[...] = jnp.zeros_like(carry_ref)

        w0, w1, w2 = cw_ref[0:1, :], cw_ref[1:2, :], cw_ref[2:3, :]
        row = lax.broadcasted_iota(jnp.int32, (HALF, 1), 0)

        def strip(k, carry):
            n1, n2, s0, s1, s2, s3 = carry
            r0 = pl.multiple_of((n_strips - 1 - k) * STRIP, STRIP)
            cv = conv_ref[pl.ds(r0, STRIP), :].astype(F32)
            dav = dact_ref[pl.ds(r0, STRIP), :].astype(F32)
            x = u_ref[pl.ds(r0, STRIP), :].astype(F32)
            du = [None, None]
            for half in (1, 0):
                rows = slice(HALF * half, HALF * (half + 1))
                a, g, dah = cv[rows, :CONV_BLOCK], cv[rows, CONV_BLOCK:], dav[rows]
                h = _gelu_half(a)
                dconv = jnp.concatenate([dah * g * _gelu_slope(a, h), dah * (a * h)], axis=1)
                u1, u2 = pltpu.roll(dconv, HALF - 1, 0), pltpu.roll(dconv, HALF - 2, 0)
                d1 = jnp.where(row >= HALF - 1, n1, u1)
                d2 = jnp.where(row >= HALF - 2, n2, u2)
                du[half] = dconv * w2 + d1 * w1 + d2 * w0
                s0, s1, s2, s3 = s0 + d2 * x[rows], s1 + d1 * x[rows], s2 + dconv * x[rows], s3 + dconv
                n1, n2 = u1, u2
            du_ref[pl.ds(r0, STRIP), :] = jnp.concatenate(du, axis=0).astype(BF16)
            return n1, n2, s0, s1, s2, s3

        below = carry_ref[...]
        zero = jnp.zeros((HALF, cb2), F32)
        init = (pltpu.roll(below, HALF - 1, 0), pltpu.roll(below, HALF - 2, 0), zero, zero, zero, zero)
        u1, _, s0, s1, s2, s3 = lax.fori_loop(0, n_strips, strip, init, unroll=2)
        carry_ref[...] = pltpu.roll(u1, 1, 0)
        dcw_ref[0:1, :] += jnp.sum(s0, axis=0, keepdims=True)
        dcw_ref[1:2, :] += jnp.sum(s1, axis=0, keepdims=True)
        dcw_ref[2:3, :] += jnp.sum(s2, axis=0, keepdims=True)
        dcw_ref[3:4, :] += jnp.sum(s3, axis=0, keepdims=True)

    rev = lambda j, i: (n_rows - 1 - i, j)
    return pl.pallas_call(
        body, name=name, grid=(N_CONV_BLOCKS, n_rows),
        in_specs=[pl.BlockSpec((TM, CONV_BLOCK), rev), pl.BlockSpec((TM, cb2), rev), pl.BlockSpec((TM, cb2), rev),
                  pl.BlockSpec((8, cb2), lambda j, i: (0, j))],
        out_specs=[pl.BlockSpec((TM, cb2), rev), pl.BlockSpec((8, cb2), lambda j, i: (0, j))],
        out_shape=[jax.ShapeDtypeStruct((LP, D_UP), BF16), jax.ShapeDtypeStruct((8, D_UP), F32)],
        scratch_shapes=[pltpu.VMEM((HALF, cb2), F32)],
        compiler_params=_cparams("arbitrary", "arbitrary"),
    )(dact, conv, u, cw8)


CHUNKS_PER_STEP = 3 if N_CHUNKS % 3 == 0 else 1
STEP_ROWS = CHUNKS_PER_STEP * CHUNK
N_STEPS = N_CHUNKS // CHUNKS_PER_STEP


def _ret_consts(h):
    rows = STEP_ROWS
    lg = math.log(1.0 - 2.0 ** (-5.0 - h))
    ri = lax.broadcasted_iota(jnp.int32, (rows, rows), 0)
    ci = lax.broadcasted_iota(jnp.int32, (rows, rows), 1)
    diff = (ri - ci).astype(F32)
    dmat = jnp.where(diff >= 0, jnp.exp(lg * jnp.maximum(diff, 0.0)), 0.0)
    rowf = lax.broadcasted_iota(jnp.int32, (rows, 1), 0).astype(F32)
    zeta = jnp.exp(lg * (rows - 1.0 - rowf))
    xi = jnp.exp(lg * (rowf + 1.0))
    return dmat, zeta, xi, math.exp(lg * rows)


def _rope(t, cosv, sinv):
    return t * cosv + pltpu.roll(t, RET_DK // 2, 1) * sinv


def _unrope(d, cosv, sinv):
    return d * cosv + pltpu.roll(d * sinv, RET_DK // 2, 1)


def _gla_masks():
    ri = lax.broadcasted_iota(jnp.int32, (CHUNK, CHUNK), 0)
    ci = lax.broadcasted_iota(jnp.int32, (CHUNK, CHUNK), 1)
    return dict(ri=ri, ci=ci, tril=(ri >= ci).astype(F32), heads=_head_block_mask(), own=_state_block_mask())


def _gla_common(p_ref, w2_ref, gb_ref, chunk, rows, masks):
    row = lax.broadcasted_iota(jnp.int32, (CHUNK, 1), 0)
    real = (chunk * CHUNK + row) >= PAD_ROWS
    ga = p_ref[rows, O_GA:O_GA + 128]
    z = _dot(ga, w2_ref[...]) + gb_ref[...]
    la = (jnp.minimum(z, 0.0) - jnp.log(1.0 + jnp.exp(-jnp.abs(z)))) * (1.0 / GLA_TAU)
    la = jnp.where(real, la, 0.0)
    ri, ci = masks["ri"], masks["ci"]
    cum = _dot_exact_rhs(masks["tril"], la)
    last = cum[CHUNK - 1:CHUNK, :]
    qs = p_ref[rows, O_GQ:O_GQ + 256] * (GLA_DK ** -0.5)
    k = p_ref[rows, O_GK:O_GK + 256]
    ecum = jnp.exp(cum)
    ekl = jnp.exp(last - cum)
    el = jnp.exp(last)
    refs = [jnp.zeros((1, 256), F32)] + [cum[a * SUB - 1:a * SUB, :] for a in range(1, N_SUB)]
    eq = [jnp.exp(cum[a * SUB:(a + 1) * SUB, :] - refs[a]) for a in range(N_SUB)]
    spread = refs[0] - cum[SUB - 1:SUB, :]
    for a in range(1, N_SUB):
        spread = jnp.maximum(spread, refs[a] - cum[(a + 1) * SUB - 1:(a + 1) * SUB, :])
    small = jnp.max(spread) <= GLA_FACTORED_MAX
    return dict(real=real, row=row, z=z, la=la, cum=cum, last=last, qs=qs, k=k, ecum=ecum, ekl=ekl, el=el,
                refs=refs, eq=eq, small=small, ri=ri, ci=ci, masks=masks)


GLA_FACTORED_MAX = 40.0


def _head_block_mask():
    r = lax.broadcasted_iota(jnp.int32, (CHUNK, 256), 0)
    col = lax.broadcasted_iota(jnp.int32, (CHUNK, 256), 1)
    return (r // SUB) == (col // GLA_DK)


def _state_block_mask():
    r = lax.broadcasted_iota(jnp.int32, (GLA_HEADS * GLA_DK, GLA_HEADS * GLA_DV), 0)
    col = lax.broadcasted_iota(jnp.int32, (GLA_HEADS * GLA_DK, GLA_HEADS * GLA_DV), 1)
    return (r // GLA_DK) == (col // GLA_DV)


def _block_diagonal(blocks):
    zero = jnp.zeros((GLA_DK, GLA_DV), F32)
    return jnp.concatenate([jnp.concatenate([blocks[h] if g == h else zero for g in range(GLA_HEADS)], axis=1)
                            for h in range(GLA_HEADS)], axis=0)


def _gla_factored(c):
    mask = c["masks"]["heads"]
    eks, keys, queries = [], [], []
    for a in range(N_SUB):
        ek = jnp.exp(jnp.minimum(c["refs"][a] - c["cum"], GLA_FACTORED_MAX))
        qh = c["qs"][a * SUB:(a + 1) * SUB, :] * c["eq"][a]
        eks.append(ek)
        keys.append(c["k"] * ek)
        queries.append(jnp.where(mask, jnp.concatenate([qh] * GLA_HEADS, axis=0), 0.0))
    return eks, keys, queries


def _gla_scores_factored(c, factored, p_scr):
    _, keys, queries = factored
    for a in range(N_SUB):
        out = _dot_nt(queries[a], keys[a])
        out = jnp.where(c["ci"] <= a * SUB + (c["ri"] & (SUB - 1)), out, 0.0)
        for h in range(GLA_HEADS):
            p_scr[h, a * SUB:(a + 1) * SUB, :] = out[h * SUB:(h + 1) * SUB, :]


def _gla_intra_bwd_factored(c, factored, dps, dq_scr, dk_scr):
    eks, keys, queries = factored
    mask = c["masks"]["heads"]
    dk = jnp.zeros((CHUNK, 256), F32)
    for a in range(N_SUB):
        dpa = jnp.concatenate([dps[h][a * SUB:(a + 1) * SUB, :] for h in range(GLA_HEADS)], axis=0)
        dq = jnp.where(mask, _dot(dpa, keys[a]), 0.0)
        dq = dq[0:SUB] + dq[SUB:2 * SUB] + dq[2 * SUB:3 * SUB] + dq[3 * SUB:4 * SUB]
        dq_scr[a * SUB:(a + 1) * SUB, :] = dq * c["eq"][a]
        dk = dk + _dot_tn(dpa, queries[a]) * eks[a]
    dk_scr[...] = dk


def _gla_lag_weights(c):
    cum, row = c["cum"], c["row"]
    out = [jnp.ones((CHUNK, 256), F32)]
    for r in range(1, SUB):
        out.append(jnp.where((row % SUB) >= r, jnp.exp(jnp.minimum(cum - pltpu.roll(cum, r, 0), 0.0)), 0.0))
    return out


def _gla_pairwise_keys(c):
    return [None] + [c["k"] * jnp.exp(jnp.minimum(c["refs"][a] - c["cum"], 0.0)) for a in range(1, N_SUB)]


def _gla_scores_pairwise(c, lag_w, keys, h):
    sl = slice(GLA_DK * h, GLA_DK * (h + 1))
    qs, k = c["qs"][:, sl], c["k"][:, sl]
    ri, ci = c["ri"], c["ci"]
    p = jnp.zeros((CHUNK, CHUNK), F32)
    for r in range(SUB):
        kr = k if r == 0 else pltpu.roll(k, r, 0)
        pr = jnp.sum(qs * kr * lag_w[r][:, sl], axis=1, keepdims=True)
        p = p + jnp.where(ci == ri - r, pr, 0.0)
    blocks = [jnp.zeros((SUB, CHUNK), F32)]
    for a in range(1, N_SUB):
        qh = qs[a * SUB:(a + 1) * SUB, :] * c["eq"][a][:, sl]
        blocks.append(jnp.where(ci[:SUB, :] < a * SUB, _dot_nt(qh, keys[a][:, sl]), 0.0))
    return p + jnp.concatenate(blocks, axis=0)


def _gla_all_scores(c, p_scr, factored):
    if factored:
        _gla_scores_factored(c, _gla_factored(c), p_scr)
    else:
        lag_w, keys = _gla_lag_weights(c), _gla_pairwise_keys(c)
        for h in range(GLA_HEADS):
            p_scr[h] = _gla_scores_pairwise(c, lag_w, keys, h)


def _either_form(chunks, run):
    small = chunks[0]["small"]
    for c in chunks[1:]:
        small = jnp.logical_and(small, c["small"])
    pl.when(small)(lambda: run(True))
    pl.when(jnp.logical_not(small))(lambda: run(False))


def _gla_intra_bwd_pairwise(c, lag_w, keys, dp, h):
    sl = slice(GLA_DK * h, GLA_DK * (h + 1))
    qs_h, k_h = c["qs"][:, sl], c["k"][:, sl]
    ri, ci = c["ri"], c["ci"]
    dq_rows = [jnp.zeros((SUB, GLA_DK), F32)]
    dk = jnp.zeros((CHUNK, GLA_DK), F32)
    for a in range(1, N_SUB):
        eq = c["eq"][a][:, sl]
        qh = qs_h[a * SUB:(a + 1) * SUB, :] * eq
        dpa = jnp.where(ci[:SUB, :] < a * SUB, dp[a * SUB:(a + 1) * SUB, :], 0.0)
        dq_rows.append(_dot(dpa, keys[a][:, sl]) * eq)
        ek = jnp.exp(jnp.minimum(c["refs"][a][:, sl] - c["cum"][:, sl], 0.0))
        dk = dk + _dot_tn(dpa, qh) * ek
    dq = jnp.concatenate(dq_rows, axis=0)
    for r in range(SUB):
        w = lag_w[r][:, sl]
        dpr = jnp.sum(jnp.where(ci == ri - r, dp, 0.0), axis=1, keepdims=True)
        kr = k_h if r == 0 else pltpu.roll(k_h, r, 0)
        dq = dq + dpr * kr * w
        back = dpr * qs_h * w
        dk = dk + (back if r == 0 else pltpu.roll(back, CHUNK - r, 0))
    return dq, dk


def _gla_all_intra_bwd(c, dps, p_scr, dq_scr, dk_scr, factored):
    if factored:
        terms = _gla_factored(c)
        _gla_scores_factored(c, terms, p_scr)
        _gla_intra_bwd_factored(c, terms, dps, dq_scr, dk_scr)
    else:
        lag_w, keys = _gla_lag_weights(c), _gla_pairwise_keys(c)
        outs = [_gla_intra_bwd_pairwise(c, lag_w, keys, dps[h], h) for h in range(GLA_HEADS)]
        for h in range(GLA_HEADS):
            p_scr[h] = _gla_scores_pairwise(c, lag_w, keys, h)
        dq_scr[...] = jnp.concatenate([o[0] for o in outs], axis=1)
        dk_scr[...] = jnp.concatenate([o[1] for o in outs], axis=1)


def _mixer_fwd(proj, cos2, sin2, w2p, gb, rnw, gnw, name, carried=()):
    n_carried = len(carried)

    def body(*refs):
        p_ref, c_ref, s_ref, w2_ref, gb_ref, rnw_ref, gnw_ref = refs[:7]
        x_refs, refs = refs[7:7 + n_carried], refs[7 + n_carried:]
        ocat_ref, mrg_ref, sr_out, sg_out = refs[:4]
        gathered_refs, refs = refs[4:4 + n_carried], refs[4 + n_carried:]
        sr, sg, p_scr = refs[:3]
        n = pl.program_id(0)
        if n_carried:
            start, forward, finish = _gather_phases(x_refs, gathered_refs, *refs[3:])
            pl.when(n == 0)(start)
            pl.when(n == (3 * N_STEPS) // 4)(forward)

        @pl.when(n == 0)
        def _():
            sr[...] = jnp.zeros_like(sr)
            sg[...] = jnp.zeros_like(sg)

        sr_out[0] = sr[...]
        cosv, sinv = c_ref[...], s_ref[...]

        for h in range(RET_HEADS):
            dmat, zeta, xi, gc = _ret_consts(h)
            hs = slice(128 * h, 128 * (h + 1))
            q = _rope(p_ref[:, O_RQ + 128 * h:O_RQ + 128 * (h + 1)], cosv, sinv)
            k = _rope(p_ref[:, O_RK + 128 * h:O_RK + 128 * (h + 1)], cosv, sinv) * (RET_DK ** -0.5)
            v = p_ref[:, O_RV + 128 * h:O_RV + 128 * (h + 1)]
            g = p_ref[:, O_RG + 128 * h:O_RG + 128 * (h + 1)]
            s_in = sr[h]
            a = _dot_nt(q, k) * dmat
            o = _dot(a, v) + _dot(q, s_in) * xi
            sr[h] = gc * s_in + _dot_tn(k * zeta, v)
            mu = jnp.mean(o, axis=-1, keepdims=True)
            xc = o - mu
            nrm = xc * lax.rsqrt(jnp.mean(xc * xc, axis=-1, keepdims=True) + EPS)
            ocat_ref[:, hs] = o
            mrg_ref[:, hs] = (nrm * rnw_ref[:, hs] * (g * _sigmoid(g))).astype(BF16)

        row_slices = [slice(CHUNK * j, CHUNK * (j + 1)) for j in range(CHUNKS_PER_STEP)]
        masks = _gla_masks()
        chunks = [_gla_common(p_ref, w2_ref, gb_ref, n * CHUNKS_PER_STEP + j, rows, masks)
                  for j, rows in enumerate(row_slices)]

        def gla_chunks(factored):
            own = masks["own"]
            for j, (rows, c) in enumerate(zip(row_slices, chunks)):
                s_in = sg[...]
                for h in range(GLA_HEADS):
                    sg_out[j, h] = s_in[GLA_DK * h:GLA_DK * (h + 1), GLA_DV * h:GLA_DV * (h + 1)]
                _gla_all_scores(c, p_scr.at[j], factored)
                v_all = p_ref[rows, O_GV:O_GV + GLA_HEADS * GLA_DV]
                o_inter = _dot(c["qs"] * c["ecum"], s_in)
                decay = jnp.exp(_dot_tn_exact_lhs(c["la"], jnp.ones((CHUNK, GLA_HEADS * GLA_DV), F32)))
                sg[...] = decay * s_in + jnp.where(own, _dot_tn(c["k"] * c["ekl"], v_all), 0.0)
                o_intra = _dot(p_scr[j].reshape(GLA_HEADS * CHUNK, CHUNK), v_all)
                for h in range(GLA_HEADS):
                    hs = slice(512 + 128 * h, 512 + 128 * (h + 1))
                    g = p_ref[rows, O_GR + 128 * h:O_GR + 128 * (h + 1)]
                    o = (o_intra[CHUNK * h:CHUNK * (h + 1), GLA_DV * h:GLA_DV * (h + 1)]
                         + o_inter[:, GLA_DV * h:GLA_DV * (h + 1)])
                    nrm = o * lax.rsqrt(jnp.mean(o * o, axis=-1, keepdims=True) + EPS)
                    ocat_ref[rows, hs] = o
                    mrg_ref[rows, hs] = (nrm * gnw_ref[:, 128 * h:128 * (h + 1)] * (g * _sigmoid(g))).astype(BF16)

        _either_form(chunks, gla_chunks)

        if n_carried:
            pl.when(n == N_STEPS - 1)(finish)

    const = lambda shape: pl.BlockSpec(shape, lambda n: (0,) * len(shape))
    anywhere = [pl.BlockSpec(memory_space=pl.ANY)] * n_carried
    return pl.pallas_call(
        body, name=name, grid=(N_STEPS,),
        in_specs=[pl.BlockSpec((STEP_ROWS, IN_WP), lambda n: (n, 0)),
                  pl.BlockSpec((STEP_ROWS, 128), lambda n: (n, 0)), pl.BlockSpec((STEP_ROWS, 128), lambda n: (n, 0)),
                  const((128, 256)), const((1, 256)), const((1, 512)), const((1, 512))] + anywhere,
        out_specs=[pl.BlockSpec((STEP_ROWS, D), lambda n: (n, 0)), pl.BlockSpec((STEP_ROWS, D), lambda n: (n, 0)),
                   pl.BlockSpec((1, RET_HEADS, RET_DK, 128), lambda n: (n, 0, 0, 0)),
                   pl.BlockSpec((CHUNKS_PER_STEP, GLA_HEADS, GLA_DK, GLA_DV), lambda n: (n, 0, 0, 0))] + anywhere,
        out_shape=[jax.ShapeDtypeStruct((LP, D), F32), jax.ShapeDtypeStruct((LP, D), BF16),
                   jax.ShapeDtypeStruct((N_STEPS, RET_HEADS, RET_DK, 128), F32),
                   jax.ShapeDtypeStruct((N_CHUNKS, GLA_HEADS, GLA_DK, GLA_DV), F32)] + _gathered_shapes(carried),
        scratch_shapes=[pltpu.VMEM((RET_HEADS, RET_DK, 128), F32),
                        pltpu.VMEM((GLA_HEADS * GLA_DK, GLA_HEADS * GLA_DV), F32),
                        pltpu.VMEM((CHUNKS_PER_STEP, GLA_HEADS, CHUNK, CHUNK), F32)] + _exchange_sems(n_carried),
        compiler_params=_cparams("arbitrary"),
    )(proj, cos2, sin2, w2p, gb, rnw, gnw, *carried)


def _mixer_bwd(proj, ocat, dmrg, sr_all, sg_all, cos2, sin2, w2p, gb, rnw, gnw, name, carried=()):
    last_step = N_STEPS - 1
    n_carried = len(carried)

    def body(*refs):
        p_ref, ocat_ref, dm_ref, sr_ref, sg_ref, c_ref, s_ref, w2_ref, gb_ref, rnw_ref, gnw_ref = refs[:11]
        g_refs, refs = refs[11:11 + n_carried], refs[11 + n_carried:]
        dp_ref, dw2_ref, dgb_ref, drn_ref, dgn_ref = refs[:5]
        got_refs, refs = refs[5:5 + n_carried], refs[5 + n_carried:]
        dsr, dsg, p_scr, dq_scr, dk_scr = refs[:5]
        step = pl.program_id(0)
        n = last_step - step
        if n_carried:
            start, finish = _exchange_phases(g_refs, got_refs, *refs[5:])
            pl.when(step == 0)(start)

        @pl.when(step == 0)
        def _():
            dsr[...] = jnp.zeros_like(dsr)
            dsg[...] = jnp.zeros_like(dsg)
            dw2_ref[...] = jnp.zeros_like(dw2_ref)
            dgb_ref[...] = jnp.zeros_like(dgb_ref)
            drn_ref[...] = jnp.zeros_like(drn_ref)
            dgn_ref[...] = jnp.zeros_like(dgn_ref)

        cosv, sinv = c_ref[...], s_ref[...]
        step_row = lax.broadcasted_iota(jnp.int32, (STEP_ROWS, 1), 0)
        real = ((n * STEP_ROWS + step_row) >= PAD_ROWS).astype(F32)

        for h in range(RET_HEADS):
            dmat, zeta, xi, gc = _ret_consts(h)
            hs = slice(128 * h, 128 * (h + 1))
            q = _rope(p_ref[:, O_RQ + 128 * h:O_RQ + 128 * (h + 1)], cosv, sinv)
            k = _rope(p_ref[:, O_RK + 128 * h:O_RK + 128 * (h + 1)], cosv, sinv) * (RET_DK ** -0.5)
            v = p_ref[:, O_RV + 128 * h:O_RV + 128 * (h + 1)]
            g = p_ref[:, O_RG + 128 * h:O_RG + 128 * (h + 1)]
            o = ocat_ref[:, hs]
            dy = dm_ref[:, hs]
            wv = rnw_ref[:, hs]
            mu = jnp.mean(o, axis=-1, keepdims=True)
            xc = o - mu
            rs = lax.rsqrt(jnp.mean(xc * xc, axis=-1, keepdims=True) + EPS)
            nrm = xc * rs
            sgm = _sigmoid(g)
            sil = g * sgm
            drn_ref[0:1, hs] += jnp.sum(dy * nrm * sil, axis=0, keepdims=True)
            dgate = dy * nrm * wv * (sgm * (1.0 + g * (1.0 - sgm)))
            dn = dy * wv * sil
            do = rs * (dn - jnp.mean(dn, axis=-1, keepdims=True) - nrm * jnp.mean(dn * nrm, axis=-1, keepdims=True))
            s_in = sr_ref[0, h]
            ds_out = dsr[h]
            a = _dot_nt(q, k) * dmat
            da = _dot_nt(do, v) * dmat
            dox = do * xi
            dq = _dot(da, k) + _dot_nt(dox, s_in)
            dk = _dot_tn(da, q) + _dot_nt(v, ds_out) * zeta
            dv = _dot_tn(a, do) + _dot(k * zeta, ds_out)
            dsr[h] = gc * ds_out + _dot_tn(q, dox)
            dk = dk * (RET_DK ** -0.5)
            dp_ref[:, O_RQ + 128 * h:O_RQ + 128 * (h + 1)] = (_unrope(dq, cosv, sinv) * real).astype(BF16)
            dp_ref[:, O_RK + 128 * h:O_RK + 128 * (h + 1)] = (_unrope(dk, cosv, sinv) * real).astype(BF16)
            dp_ref[:, O_RV + 128 * h:O_RV + 128 * (h + 1)] = (dv * real).astype(BF16)
            dp_ref[:, O_RG + 128 * h:O_RG + 128 * (h + 1)] = (dgate * real).astype(BF16)

        row_slices = [slice(CHUNK * j, CHUNK * (j + 1)) for j in range(CHUNKS_PER_STEP)]
        masks = _gla_masks()
        chunks = [_gla_common(p_ref, w2_ref, gb_ref, n * CHUNKS_PER_STEP + j, rows, masks)
                  for j, rows in enumerate(row_slices)]

        def gla_chunks(factored):
            for j in reversed(range(CHUNKS_PER_STEP)):
                gla_chunk_bwd(chunks[j], n * CHUNKS_PER_STEP + j, row_slices[j], j, factored, p_ref, ocat_ref, dm_ref,
                              sg_ref, w2_ref, gnw_ref, dp_ref, dw2_ref, dgb_ref, dgn_ref, dsg, p_scr, dq_scr, dk_scr)

        _either_form(chunks, gla_chunks)
        if n_carried:
            pl.when(step == last_step)(finish)

    def gla_chunk_bwd(c, chunk, rows, j, factored, p_ref, ocat_ref, dm_ref, sg_ref, w2_ref, gnw_ref,
                      dp_ref, dw2_ref, dgb_ref, dgn_ref, dsg, p_scr, dq_scr, dk_scr):
        row = lax.broadcasted_iota(jnp.int32, (CHUNK, 1), 0)
        real = ((chunk * CHUNK + row) >= PAD_ROWS).astype(F32)
        ri, ci = c["ri"], c["ci"]
        causal = ri >= ci
        triu = (ci >= ri).astype(F32)
        qe = c["qs"] * c["ecum"]
        kl = c["k"] * c["ekl"]
        v_all = p_ref[rows, O_GV:O_GV + GLA_HEADS * GLA_DV]
        dos, dps = [], []
        for h in range(GLA_HEADS):
            hs = slice(512 + 128 * h, 512 + 128 * (h + 1))
            g = p_ref[rows, O_GR + 128 * h:O_GR + 128 * (h + 1)]
            o = ocat_ref[rows, hs]
            dy = dm_ref[rows, hs]
            wv = gnw_ref[:, 128 * h:128 * (h + 1)]
            rs = lax.rsqrt(jnp.mean(o * o, axis=-1, keepdims=True) + EPS)
            nrm = o * rs
            sgm = _sigmoid(g)
            sil = g * sgm
            dgn_ref[0:1, 128 * h:128 * (h + 1)] += jnp.sum(dy * nrm * sil, axis=0, keepdims=True)
            dgate = dy * nrm * wv * (sgm * (1.0 + g * (1.0 - sgm)))
            dn = dy * wv * sil
            do = rs * (dn - nrm * jnp.mean(dn * nrm, axis=-1, keepdims=True))
            dp_ref[rows, O_GR + 128 * h:O_GR + 128 * (h + 1)] = (dgate * real).astype(BF16)
            dos.append(do)
        do_all = jnp.concatenate(dos, axis=1)
        do_blocks = jnp.where(c["masks"]["own"], jnp.concatenate([do_all] * GLA_HEADS, axis=0), 0.0)
        dp_all = _dot_nt(do_blocks, v_all)
        dps = [jnp.where(causal, dp_all[CHUNK * h:CHUNK * (h + 1), :], 0.0) for h in range(GLA_HEADS)]
        _gla_all_intra_bwd(c, dps, p_scr.at[j], dq_scr.at[j], dk_scr.at[j], factored)
        s_in = _block_diagonal([sg_ref[j, h] for h in range(GLA_HEADS)])
        ds_out = dsg[...]
        decay = jnp.exp(_dot_tn_exact_lhs(c["la"], jnp.ones((CHUNK, GLA_HEADS * GLA_DV), F32)))
        dv_state = _dot(kl, ds_out)
        dqe = _dot_nt(do_all, s_in)
        dkl = _dot_nt(v_all, ds_out)
        dsg[...] = jnp.where(c["masks"]["own"], _dot_tn(qe, do_all), 0.0) + decay * ds_out
        sd = s_in * ds_out
        sd_hi = sd.astype(BF16)
        sd_lo = (sd - sd_hi.astype(F32)).astype(BF16)
        ones8 = jnp.ones((8, GLA_HEADS * GLA_DV), BF16)
        nt = (((1,), (1,)), ((), ()))
        d_el = (lax.dot_general(ones8, sd_hi, nt, preferred_element_type=F32)
                + lax.dot_general(ones8, sd_lo, nt, preferred_element_type=F32))[0:1, :]
        dqs = dqe * c["ecum"] + dq_scr[j]
        dkk = dkl * c["ekl"] + dk_scr[j]
        d_last = jnp.sum(dkl * kl, axis=0, keepdims=True) + d_el * c["el"]
        dcum = c["qs"] * dqs - c["k"] * dkk + jnp.where(row == CHUNK - 1, d_last, 0.0)
        dla = _dot_exact_rhs(triu, dcum)
        dv = _dot_tn(p_scr[j].reshape(GLA_HEADS * CHUNK, CHUNK), do_blocks) + dv_state
        dp_ref[rows, O_GV:O_GV + GLA_HEADS * GLA_DV] = (dv * real).astype(BF16)
        dp_ref[rows, O_GQ:O_GQ + 256] = (dqs * (GLA_DK ** -0.5) * real).astype(BF16)
        dp_ref[rows, O_GK:O_GK + 256] = (dkk * real).astype(BF16)
        dz = dla * (1.0 / GLA_TAU) * _sigmoid(-c["z"]) * real
        ga = p_ref[rows, O_GA:O_GA + 128]
        dp_ref[rows, O_GA:O_GA + 128] = _dot_nt(dz, w2_ref[...]).astype(BF16)
        dp_ref[rows, O_GA + 128:IN_WP] = jnp.zeros((CHUNK, IN_WP - O_GA - 128), BF16)
        dw2_ref[...] += _dot_tn(ga, dz)
        dgb_ref[0:1, :] += jnp.sum(dz, axis=0, keepdims=True)

    const = lambda shape: pl.BlockSpec(shape, lambda s: (0,) * len(shape))
    rev = lambda s: (last_step - s, 0)
    anywhere = [pl.BlockSpec(memory_space=pl.ANY)] * n_carried
    return pl.pallas_call(
        body, name=name, grid=(N_STEPS,),
        in_specs=[pl.BlockSpec((STEP_ROWS, IN_WP), rev), pl.BlockSpec((STEP_ROWS, D), rev),
                  pl.BlockSpec((STEP_ROWS, D), rev),
                  pl.BlockSpec((1, RET_HEADS, RET_DK, 128), lambda s: (last_step - s, 0, 0, 0)),
                  pl.BlockSpec((CHUNKS_PER_STEP, GLA_HEADS, GLA_DK, GLA_DV), lambda s: (last_step - s, 0, 0, 0)),
                  pl.BlockSpec((STEP_ROWS, 128), rev), pl.BlockSpec((STEP_ROWS, 128), rev),
                  const((128, 256)), const((1, 256)), const((1, 512)), const((1, 512))] + anywhere,
        out_specs=[pl.BlockSpec((STEP_ROWS, IN_WP), rev), const((128, 256)), const((8, 256)),
                   const((8, 512)), const((8, 512))] + anywhere,
        out_shape=[jax.ShapeDtypeStruct((LP, IN_WP), BF16), jax.ShapeDtypeStruct((128, 256), F32),
                   jax.ShapeDtypeStruct((8, 256), F32), jax.ShapeDtypeStruct((8, 512), F32),
                   jax.ShapeDtypeStruct((8, 512), F32)] + [jax.ShapeDtypeStruct(g.shape, g.dtype) for g in carried],
        scratch_shapes=[pltpu.VMEM((RET_HEADS, RET_DK, 128), F32),
                        pltpu.VMEM((GLA_HEADS * GLA_DK, GLA_HEADS * GLA_DV), F32),
                        pltpu.VMEM((CHUNKS_PER_STEP, GLA_HEADS, CHUNK, CHUNK), F32),
                        pltpu.VMEM((CHUNKS_PER_STEP, CHUNK, 256), F32),
                        pltpu.VMEM((CHUNKS_PER_STEP, CHUNK, 256), F32)] + _exchange_sems(n_carried),
        compiler_params=_cparams("arbitrary"),
    )(proj, ocat, dmrg, sr_all, sg_all, cos2, sin2, w2p, gb, rnw, gnw, *carried)


def _all_gather(xs, name):
    n = len(xs)

    def body(*refs):
        start, forward, finish = _gather_phases(refs[:n], refs[n:2 * n], *refs[2 * n:])
        start()
        forward()
        finish()

    return pl.pallas_call(
        body, name=name,
        in_specs=[pl.BlockSpec(memory_space=pl.ANY)] * n,
        out_specs=[pl.BlockSpec(memory_space=pl.ANY)] * n,
        out_shape=_gathered_shapes(xs),
        scratch_shapes=_exchange_sems(n),
    )(*xs)


def _gathered_shapes(xs):
    return [jax.ShapeDtypeStruct((N_DEV,) + x.shape, x.dtype) for x in xs]


def _exchange_sems(n):
    if n == 0:
        return []
    return [pltpu.SemaphoreType.DMA((7 * n,)), pltpu.SemaphoreType.DMA((7 * n,)), pltpu.SemaphoreType.DMA((n,))]


def _gather_phases(x_refs, out_refs, send_sems, recv_sems, local_sems):
    n = len(x_refs)
    mx, my, mc = lax.axis_index("x"), lax.axis_index("y"), lax.axis_index("c")
    me, sibling = (mx, my, mc), (mx, my, 1 - mc)
    chips = [(1 - mx, my), (mx, 1 - my), (1 - mx, 1 - my)]

    def slot(a, px, py, pc):
        return out_refs[a].at[4 * px + 2 * py + pc]

    def copy(a, k, block, to, src=None):
        return pltpu.make_async_remote_copy(
            src_ref=slot(a, *block) if src is None else src, dst_ref=slot(a, *block),
            send_sem=send_sems.at[7 * a + k], recv_sem=recv_sems.at[7 * a + k],
            device_id=to, device_id_type=MESH_IDS)

    mine = [pltpu.make_async_copy(x_refs[a], slot(a, *me), local_sems.at[a]) for a in range(n)]
    first = []
    for a in range(n):
        first.append(copy(a, 0, me, sibling, src=x_refs[a]))
        first += [copy(a, 1 + j, me, (*chip, mc), src=x_refs[a]) for j, chip in enumerate(chips)]
    passed = [copy(a, 4 + j, (*chip, mc), sibling) for j, chip in enumerate(chips) for a in range(n)]

    def start():
        for cp in mine + first:
            cp.start()

    def forward():
        for j, chip in enumerate(chips):
            for a in range(n):
                copy(a, 1 + j, (*chip, mc), me).wait_recv()
                passed[j * n + a].start()

    def finish():
        for a in range(n):
            copy(a, 0, sibling, me).wait_recv()
            for j, chip in enumerate(chips):
                copy(a, 4 + j, (*chip, 1 - mc), me).wait_recv()
        for cp in first + passed:
            cp.wait_send()
        for cp in mine:
            cp.wait()

    return start, forward, finish


def _exchange_blocks(gs, name):
    n = len(gs)

    def body(*refs):
        start, finish = _exchange_phases(refs[:n], refs[n:2 * n], *refs[2 * n:])
        start()
        finish()

    return pl.pallas_call(
        body, name=name,
        in_specs=[pl.BlockSpec(memory_space=pl.ANY)] * n,
        out_specs=[pl.BlockSpec(memory_space=pl.ANY)] * n,
        out_shape=[jax.ShapeDtypeStruct(g.shape, g.dtype) for g in gs],
        scratch_shapes=_exchange_sems(n),
    )(*gs)


def _exchange_phases(g_refs, out_refs, send_sems, recv_sems, local_sems):
    n = len(g_refs)
    mx, my, mc = lax.axis_index("x"), lax.axis_index("y"), lax.axis_index("c")
    me = 4 * mx + 2 * my + mc
    mine = [pltpu.make_async_copy(g_refs[a].at[me], out_refs[a].at[me], local_sems.at[a]) for a in range(n)]
    copies = []
    for r in range(1, N_DEV):
        px, py, pc = mx ^ (r >> 2), my ^ ((r >> 1) & 1), mc ^ (r & 1)
        peer = 4 * px + 2 * py + pc
        for a in range(n):
            copies.append(pltpu.make_async_remote_copy(
                src_ref=g_refs[a].at[peer], dst_ref=out_refs[a].at[me],
                send_sem=send_sems.at[7 * a + r - 1], recv_sem=recv_sems.at[7 * a + r - 1],
                device_id=(px, py, pc), device_id_type=MESH_IDS))

    def start():
        for cp in mine + copies:
            cp.start()

    def finish():
        for cp in copies:
            cp.wait_recv()
        for cp in copies:
            cp.wait_send()
        for cp in mine:
            cp.wait()

    return start, finish


IN_SHARD = IN_W // N_DEV
IN_SHARD_P = 512
UP_SHARD = D_UP // N_DEV
UP_SHARD_P = 768
RELAYOUT_ROWS = 256


def _pieces_w_in():
    return [(k, 0, IN_SHARD * k, IN_SHARD) for k in range(N_DEV)]


def _pieces_ffn_up():
    pieces = []
    for k in range(N_DEV):
        n, end = UP_SHARD * k, UP_SHARD * (k + 1)
        while n < end:
            half, r = divmod(n, D_FF)
            blk, off = divmod(r, CONV_BLOCK)
            run = min(CONV_BLOCK - off, end - n)
            pieces.append((k, n - UP_SHARD * k, 2 * CONV_BLOCK * blk + CONV_BLOCK * half + off, run))
            n += run
    return pieces


def _assemble_block(load, spans, dst_block, rows):
    lo = 128 * dst_block
    lane = lax.broadcasted_iota(jnp.int32, (1, 128), 1)
    out = jnp.zeros((rows, 128), F32)
    for key, src_off, dst_off, length in spans:
        a, b = max(lo, dst_off), min(lo + 128, dst_off + length)
        s, s_end = src_off + (a - dst_off), src_off + (b - dst_off)
        d = a
        while s < s_end:
            e = min(s_end, 128 * (s // 128 + 1))
            blk = load(key, s // 128)
            shift = (d - s) % 128
            if shift:
                blk = pltpu.roll(blk, shift, 1)
            out = jnp.where((lane >= d - lo) & (lane < d - lo + (e - s)), blk, out)
            d += e - s
            s = e
    return out


def _shards_to_cols(shards, pieces, width, name):
    _, rows, _ = shards.shape
    tr = RELAYOUT_ROWS

    def body(s_ref, o_ref):
        load = lambda k, b: s_ref[k, :, 128 * b:128 * (b + 1)].astype(F32)
        for db in range(width // 128):
            o_ref[:, 128 * db:128 * (db + 1)] = _assemble_block(load, pieces, db, tr).astype(BF16)

    return pl.pallas_call(
        body, name=name, grid=(rows // tr,),
        in_specs=[pl.BlockSpec((N_DEV, tr, shards.shape[2]), lambda i: (0, i, 0))],
        out_specs=pl.BlockSpec((tr, width), lambda i: (i, 0)),
        out_shape=jax.ShapeDtypeStruct((rows, width), BF16),
        compiler_params=_cparams("parallel"),
    )(shards)


def _cols_to_shards(full, pieces, shard_width, name):
    rows, width = full.shape
    tr = RELAYOUT_ROWS

    def body(f_ref, o_ref):
        load = lambda _, b: f_ref[:, 128 * b:128 * (b + 1)].astype(F32)
        for k in range(N_DEV):
            spans = [(None, dst_off, src_off, length) for dev, src_off, dst_off, length in pieces if dev == k]
            for db in range(shard_width // 128):
                o_ref[k, :, 128 * db:128 * (db + 1)] = _assemble_block(load, spans, db, tr).astype(BF16)

    return pl.pallas_call(
        body, name=name, grid=(rows // tr,),
        in_specs=[pl.BlockSpec((tr, width), lambda i: (i, 0))],
        out_specs=pl.BlockSpec((N_DEV, tr, shard_width), lambda i: (0, i, 0)),
        out_shape=jax.ShapeDtypeStruct((N_DEV, rows, shard_width), BF16),
        compiler_params=_cparams("parallel"),
    )(full)


def _adamw(parts, w, m, v, rows_per_step, name):
    rows, cols = w.shape
    assert rows % rows_per_step == 0 and parts.shape == (N_DEV, rows, cols)

    def body(p_ref, w_ref, m_ref, v_ref, g_ref, d_ref, nm_ref, nv_ref):
        g = p_ref[0].astype(F32)
        for j in range(1, N_DEV):
            g = g + p_ref[j].astype(F32)
        m_new = ADAM_B1 * m_ref[...] + (1.0 - ADAM_B1) * g
        v_new = ADAM_B2 * v_ref[...] + (1.0 - ADAM_B2) * (g * g)
        m_hat = m_new / (1.0 - ADAM_B1 ** ADAM_STEP)
        v_hat = v_new / (1.0 - ADAM_B2 ** ADAM_STEP)
        g_ref[...] = g
        d_ref[...] = -ADAM_LR * (m_hat / (jnp.sqrt(v_hat) + ADAM_EPS) + ADAM_WD * w_ref[...])
        nm_ref[...] = m_new
        nv_ref[...] = v_new

    tile = pl.BlockSpec((rows_per_step, cols), lambda i: (i, 0))
    shape = jax.ShapeDtypeStruct((rows, cols), F32)
    return pl.pallas_call(
        body, name=name, grid=(rows // rows_per_step,),
        in_specs=[pl.BlockSpec((N_DEV, rows_per_step, cols), lambda i: (0, i, 0)), tile, tile, tile],
        out_specs=[tile, tile, tile, tile],
        out_shape=[shape, shape, shape, shape],
        compiler_params=_cparams("parallel"),
    )(parts, w, m, v)


BIG = (("w_in", (DEPTH, D, IN_W // N_DEV), 2), ("w_out", (DEPTH, D // N_DEV, D), 1),
       ("ffn_up", (DEPTH, D, D_UP // N_DEV), 2), ("ffn_down", (DEPTH, D_FF // N_DEV, D), 1))
SMALL = (("meta_tokens", (N_META, D // N_DEV), 1), ("gla_gate_w2", (DEPTH, GATE_RANK, 256 // N_DEV), 2),
         ("ffn_conv_w", (DEPTH, 3, D_UP // N_DEV), 2))
REPL = (("pre_mix_norm", (DEPTH, D)), ("gla_gate_b", (DEPTH, 256)), ("ret_norm_w", (DEPTH, 512)),
        ("gla_norm_w", (DEPTH, 512)), ("post_mix_norm", (DEPTH, D)), ("pre_ffn_norm", (DEPTH, D)),
        ("ffn_conv_b", (DEPTH, D_UP)), ("post_ffn_norm", (DEPTH, D)))
WEIGHT_ORDER = ("meta_tokens", "pre_mix_norm", "w_in", "gla_gate_w2", "gla_gate_b", "ret_norm_w", "gla_norm_w",
                "w_out", "post_mix_norm", "pre_ffn_norm", "ffn_up", "ffn_conv_w", "ffn_conv_b", "ffn_down",
                "post_ffn_norm")


def _size(shape):
    return math.prod(shape)


def _round_up(n, mult):
    return -(-n // mult) * mult


REPL_ROWS = _round_up(-(-sum(_size(s) for _, s in REPL) // LANES), 8)
SMALL_ROWS = _round_up(-(-sum(_size(s) for _, s, _ in SMALL) // LANES), 8)


def _pack(arrays, rows, dtype):
    flat = jnp.concatenate([a.reshape(-1).astype(dtype) for a in arrays])
    return jnp.pad(flat, (0, rows * LANES - flat.shape[0])).reshape(rows, LANES)


def _unpack(buf, shapes):
    flat = buf.reshape(-1)
    out, off = [], 0
    for shape in shapes:
        out.append(flat[off:off + _size(shape)].reshape(shape))
        off += _size(shape)
    return out


def _unshard(blocks, axis):
    moved = jnp.moveaxis(blocks, 0, axis)
    shape = list(moved.shape)
    shape[axis:axis + 2] = [shape[axis] * shape[axis + 1]]
    return moved.reshape(shape)


def _to_blocks(full, axis):
    shape = list(full.shape)
    shape[axis:axis + 1] = [N_DEV, shape[axis] // N_DEV]
    return jnp.moveaxis(full.reshape(shape), axis, 0)


def _interleave_cols(w):
    lead = w.shape[:-1]
    return jnp.swapaxes(w.reshape(lead + (2, N_CONV_BLOCKS, CONV_BLOCK)), -3, -2).reshape(lead + (D_UP,))


def _deinterleave_cols(w):
    lead = w.shape[:-1]
    return jnp.swapaxes(w.reshape(lead + (N_CONV_BLOCKS, 2, CONV_BLOCK)), -3, -2).reshape(lead + (D_UP,))


def _rope_tables():
    half = RET_DK // 2
    inv = ROPE_BASE ** (-jnp.arange(half, dtype=F32) / half)
    pos = jnp.arange(LP, dtype=F32) - float(PAD_ROWS)
    ang = pos[:, None] * inv[None, :]
    c, s = jnp.cos(ang), jnp.sin(ang)
    return jnp.concatenate([c, c], axis=1), jnp.concatenate([-s, s], axis=1)


def kernel(x, meta_tokens, pre_mix_norm, w_in, gla_gate_w2, gla_gate_b, ret_norm_w, gla_norm_w, w_out, post_mix_norm, pre_ffn_norm, ffn_up, ffn_conv_w, ffn_conv_b, ffn_down, post_ffn_norm, loss_target, m_meta_tokens, m_pre_mix_norm, m_w_in, m_gla_gate_w2, m_gla_gate_b, m_ret_norm_w, m_gla_norm_w, m_w_out, m_post_mix_norm, m_pre_ffn_norm, m_ffn_up, m_ffn_conv_w, m_ffn_conv_b, m_ffn_down, m_post_ffn_norm, v_meta_tokens, v_pre_mix_norm, v_w_in, v_gla_gate_w2, v_gla_gate_b, v_ret_norm_w, v_gla_norm_w, v_w_out, v_post_mix_norm, v_pre_ffn_norm, v_ffn_up, v_ffn_conv_w, v_ffn_conv_b, v_ffn_down, v_post_ffn_norm):
    weights = dict(meta_tokens=meta_tokens, pre_mix_norm=pre_mix_norm, w_in=w_in, gla_gate_w2=gla_gate_w2,
                   gla_gate_b=gla_gate_b, ret_norm_w=ret_norm_w, gla_norm_w=gla_norm_w, w_out=w_out,
                   post_mix_norm=post_mix_norm, pre_ffn_norm=pre_ffn_norm, ffn_up=ffn_up, ffn_conv_w=ffn_conv_w,
                   ffn_conv_b=ffn_conv_b, ffn_down=ffn_down, post_ffn_norm=post_ffn_norm)
    mom1 = dict(meta_tokens=m_meta_tokens, pre_mix_norm=m_pre_mix_norm, w_in=m_w_in, gla_gate_w2=m_gla_gate_w2,
                gla_gate_b=m_gla_gate_b, ret_norm_w=m_ret_norm_w, gla_norm_w=m_gla_norm_w, w_out=m_w_out,
                post_mix_norm=m_post_mix_norm, pre_ffn_norm=m_pre_ffn_norm, ffn_up=m_ffn_up,
                ffn_conv_w=m_ffn_conv_w, ffn_conv_b=m_ffn_conv_b, ffn_down=m_ffn_down, post_ffn_norm=m_post_ffn_norm)
    mom2 = dict(meta_tokens=v_meta_tokens, pre_mix_norm=v_pre_mix_norm, w_in=v_w_in, gla_gate_w2=v_gla_gate_w2,
                gla_gate_b=v_gla_gate_b, ret_norm_w=v_ret_norm_w, gla_norm_w=v_gla_norm_w, w_out=v_w_out,
                post_mix_norm=v_post_mix_norm, pre_ffn_norm=v_pre_ffn_norm, ffn_up=v_ffn_up,
                ffn_conv_w=v_ffn_conv_w, ffn_conv_b=v_ffn_conv_b, ffn_down=v_ffn_down, post_ffn_norm=v_post_ffn_norm)

    pad_cols = lambda a, width: jnp.pad(a, ((0, 0), (0, width - a.shape[1])))
    big_names = [n for n, _, _ in BIG]
    shard = {}
    for l in range(DEPTH):
        shard[l, "w_in"] = pad_cols(w_in[l].astype(BF16), IN_SHARD_P)
        shard[l, "w_out"] = w_out[l].astype(BF16)
        shard[l, "ffn_up"] = pad_cols(ffn_up[l].astype(BF16), UP_SHARD_P)
        shard[l, "ffn_down"] = ffn_down[l].astype(BF16)
    gathered = {(0, "w_in"): _all_gather([shard[0, "w_in"]], "gather_w_in_0")[0]}
    gather_in_mixer = {l: [(l, n) for n in big_names[1:]] for l in range(DEPTH)}
    gather_in_conv = {l: [(l + 1, "w_in")] for l in range(DEPTH - 1)}
    small = _all_gather([_pack([weights[n] for n, _, _ in SMALL], SMALL_ROWS, F32)], "gather_small_weights")[0]
    small_parts = _unpack_blocks(small, [s for _, s, _ in SMALL])
    full = {n: _unshard(p, ax) for (n, _, ax), p in zip(SMALL, small_parts)}
    w2p = jnp.pad(full["gla_gate_w2"], ((0, 0), (0, 128 - GATE_RANK), (0, 0)))
    cw8 = jnp.concatenate([_interleave_cols(full["ffn_conv_w"]), _interleave_cols(ffn_conv_b)[:, None, :],
                           jnp.zeros((DEPTH, 4, D_UP), F32)], axis=1)
    cos2, sin2 = _rope_tables()

    h = jnp.concatenate([jnp.zeros((PAD_ROWS, D), F32), full["meta_tokens"], x[0]], axis=0)
    target = jnp.concatenate([jnp.zeros((CHUNK, D), F32), loss_target[0]], axis=0)
    saved, layer_w = [], []
    for l in range(DEPTH):
        lw = dict(w_in=_shards_to_cols(gathered[l, "w_in"], _pieces_w_in(), IN_WP, f"w_in_cols_{l}"))
        a1, proj = _norm_matmul(h, pre_mix_norm[l:l + 1], lw["w_in"], out_dtype=F32, tm=TM_BIG, tn=IN_WP // 3,
                                name=f"in_proj_{l}")
        keys = gather_in_mixer.get(l, [])
        ocat, merged, sr_all, sg_all, *got = _mixer_fwd(proj, cos2, sin2, w2p[l], gla_gate_b[l:l + 1],
                                                        ret_norm_w[l:l + 1], gla_norm_w[l:l + 1], f"mixer_fwd_{l}",
                                                        carried=[shard[key] for key in keys])
        gathered.update(zip(keys, got))
        lw["w_out"] = gathered[l, "w_out"].reshape(D, D)
        lw["w_up"] = _shards_to_cols(gathered[l, "ffn_up"], _pieces_ffn_up(), D_UP, f"ffn_up_cols_{l}")
        lw["w_down"] = gathered[l, "ffn_down"].reshape(D_FF, D)
        layer_w.append(lw)
        m, h1 = _matmul_resid_norm(merged, lw["w_out"], h, post_mix_norm[l:l + 1], f"out_proj_{l}")
        a2, u = _norm_matmul(h1, pre_ffn_norm[l:l + 1], lw["w_up"], out_dtype=BF16, tm=TM_BIG, tn=D_UP // 4,
                             name=f"ffn_up_{l}")
        keys = gather_in_conv.get(l, [])
        cv, act, *got = _conv_act_fwd(u, cw8[l], f"ffn_conv_act_{l}", carried=[shard[key] for key in keys])
        gathered.update(zip(keys, got))
        f, h2, *loss_acc = _matmul_resid_norm(act, lw["w_down"], h1, post_ffn_norm[l:l + 1], f"ffn_down_{l}",
                                              target=target if l == DEPTH - 1 else None)
        saved.append(dict(h=h, a1=a1, proj=proj, ocat=ocat, merged=merged, sr=sr_all, sg=sg_all, m=m, h1=h1,
                          a2=a2, u=u, cv=cv, act=act, f=f))
        h = h2

    dh = h
    loss = lax.psum(loss_acc[0][0, 0], ("x", "y", "c"))

    kinds = ("grad", "delta", "new_m", "new_v")
    grads = {n: [None] * DEPTH for n in WEIGHT_ORDER if n != "meta_tokens" and n not in big_names}
    pending, parts = [], {}
    for l in reversed(range(DEPTH)):
        s, lw = saved[l], layer_w[l]
        dact, df, g_post_ffn = _norm_bwd_matmul(dh, s["f"], post_ffn_norm[l:l + 1], lw["w_down"], BF16,
                                                f"ffn_down_dx_{l}")
        g_down = _matmul(s["act"], df, ta=True, out_dtype=BF16, tm=D_FF // 2, tn=D, tk=TM_BIG, name=f"ffn_down_dw_{l}")
        du, dcw = _conv_act_bwd(dact, s["cv"], s["u"], cw8[l], f"ffn_conv_act_bwd_{l}")
        dh1, g_pre_ffn = _matmul_norm_bwd(du, lw["w_up"], s["h1"], pre_ffn_norm[l:l + 1], dh, D_FF, f"ffn_up_dx_{l}")
        g_up = _matmul(s["a2"], du, ta=True, out_dtype=BF16, tm=D, tn=D_FF, tk=TM_BIG, name=f"ffn_up_dw_{l}")
        dmerged, dm, g_post_mix = _norm_bwd_matmul(dh1, s["m"], post_mix_norm[l:l + 1], lw["w_out"], F32,
                                                   f"out_proj_dx_{l}")
        g_out = _matmul(s["merged"], dm, ta=True, out_dtype=BF16, tm=D, tn=D, tk=TM_BIG, name=f"out_proj_dw_{l}")
        pending += [((l, "ffn_down"), g_down.reshape(N_DEV, D_FF // N_DEV, D)),
                    ((l, "ffn_up"), _cols_to_shards(g_up, _pieces_ffn_up(), UP_SHARD_P, f"ffn_up_grad_shards_{l}")),
                    ((l, "w_out"), g_out.reshape(N_DEV, D // N_DEV, D))]
        dproj, g_w2, g_gb, g_rn, g_gn, *got = _mixer_bwd(s["proj"], s["ocat"], dmerged, s["sr"], s["sg"], cos2, sin2,
                                                         w2p[l], gla_gate_b[l:l + 1], ret_norm_w[l:l + 1],
                                                         gla_norm_w[l:l + 1], f"mixer_bwd_{l}",
                                                         carried=[blocks for _, blocks in pending])
        parts.update(zip([key for key, _ in pending], got))
        g_in = _matmul(s["a1"], dproj, ta=True, out_dtype=BF16, tm=D, tn=IN_WP // 2, tk=TM_BIG, name=f"in_proj_dw_{l}")
        pending = [((l, "w_in"), _cols_to_shards(g_in, _pieces_w_in(), IN_SHARD_P, f"w_in_grad_shards_{l}"))]
        now = pending if l == 0 else []
        dh, g_pre_mix, *got = _matmul_norm_bwd(dproj, lw["w_in"], s["h"], pre_mix_norm[l:l + 1], dh1, IN_WP,
                                               f"in_proj_dx_{l}", carried=[blocks for _, blocks in now])
        parts.update(zip([key for key, _ in now], got))
        pending = [] if l == 0 else pending
        grads["post_ffn_norm"][l] = g_post_ffn[0]
        grads["ffn_conv_w"][l] = _deinterleave_cols(dcw[0:3])
        grads["ffn_conv_b"][l] = _deinterleave_cols(dcw[3])
        grads["pre_ffn_norm"][l] = g_pre_ffn[0]
        grads["post_mix_norm"][l] = g_post_mix[0]
        grads["gla_gate_w2"][l] = g_w2[:GATE_RANK]
        grads["gla_gate_b"][l] = g_gb[0]
        grads["ret_norm_w"][l] = g_rn[0]
        grads["gla_norm_w"][l] = g_gn[0]
        grads["pre_mix_norm"][l] = g_pre_mix[0]
    local = {n: jnp.stack(v) for n, v in grads.items()}
    local["meta_tokens"] = dh[PAD_ROWS:CHUNK]
    grad_x = dh[CHUNK:][None]

    blocks = jnp.concatenate([_to_blocks(local[n], ax).reshape(N_DEV, -1) for n, _, ax in SMALL], axis=1)
    blocks = jnp.pad(blocks, ((0, 0), (0, SMALL_ROWS * LANES - blocks.shape[1]))).reshape(N_DEV, SMALL_ROWS, LANES)
    *got, small_grad_parts = _exchange_blocks([b for _, b in pending] + [blocks], "exchange_last_grads")
    parts.update(zip([key for key, _ in pending], got))

    widths = dict(w_in=IN_SHARD_P, w_out=D, ffn_up=UP_SHARD_P, ffn_down=D)
    steps = dict(w_in=256, w_out=D // N_DEV, ffn_up=256, ffn_down=D_FF // N_DEV // 2)
    big_out = {kind: {n: [None] * DEPTH for n in big_names} for kind in kinds}
    for l in range(DEPTH):
        for n in big_names:
            mine = [pad_cols(d[n][l], widths[n]) for d in (weights, mom1, mom2)]
            results = _adamw(parts[l, n], *mine, steps[n], f"adamw_{n}_{l}")
            for kind, r in zip(kinds, results):
                big_out[kind][n][l] = r[:, :weights[n].shape[2]]
    out = {kind: {n: jnp.stack(v) for n, v in big_out[kind].items()} for kind in kinds}
    shard_shapes = [s for _, s, _ in SMALL]
    packed = [_pack([d[n] for n, _, _ in SMALL], SMALL_ROWS, F32) for d in (weights, mom1, mom2)]
    results = _adamw(small_grad_parts, *packed, SMALL_ROWS, "adamw_small_sharded")
    for kind, buf in zip(kinds, results):
        out[kind].update(zip([n for n, _, _ in SMALL], _unpack(buf, shard_shapes)))

    repl_parts = _all_gather([_pack([local[n] for n, _ in REPL], REPL_ROWS, F32)], "gather_small_grads")[0]
    packed = [_pack([d[n] for n, _ in REPL], REPL_ROWS, F32) for d in (weights, mom1, mom2)]
    results = _adamw(repl_parts, *packed, REPL_ROWS, "adamw_replicated")
    repl_shapes = [s for _, s in REPL]
    for kind, buf in zip(kinds, results):
        out[kind].update(zip([n for n, _ in REPL], _unpack(buf, repl_shapes)))

    return (loss, grad_x, *[out["grad"][n] for n in WEIGHT_ORDER], *[out["delta"][n] for n in WEIGHT_ORDER],
            *[out["new_m"][n] for n in WEIGHT_ORDER], *[out["new_v"][n] for n in WEIGHT_ORDER])


def _unpack_blocks(gathered, shapes):
    flat = gathered.reshape(N_DEV, -1)
    out, off = [], 0
    for shape in shapes:
        out.append(flat[:, off:off + _size(shape)].reshape((N_DEV,) + shape))
        off += _size(shape)
    return out
```

```python
import math

import jax
import jax.numpy as jnp
from jax import lax
from jax.experimental import pallas as pl
from jax.experimental.pallas import tpu as pltpu

F32 = jnp.float32
BF16 = jnp.bfloat16

D = 1024
SEQ = 8192
DEPTH = 2
N_META = 16
CHUNK = 64
SUB = 16
N_SUB = CHUNK // SUB
PAD_ROWS = CHUNK - N_META
LP = SEQ + CHUNK
N_CHUNKS = LP // CHUNK
RET_HEADS = 4
RET_DK = 128
GLA_HEADS = 4
GLA_DK = 64
GLA_DV = 128
GLA_TAU = 16.0
GATE_RANK = 16
IN_W = 3600
IN_WP = 3840
D_FF = 2816
D_UP = 2 * D_FF
CONV_BLOCK = 256
N_CONV_BLOCKS = D_FF // CONV_BLOCK
ROPE_BASE = 10000.0
EPS = 1e-6
N_DEV = 8
LANES = 1024

O_RQ, O_RK, O_RV, O_RG = 0, 512, 1024, 1536
O_GQ, O_GK, O_GV, O_GR, O_GA = 2048, 2304, 2560, 3072, 3584

ADAM_LR = 0.001
ADAM_B1 = 0.9
ADAM_B2 = 0.999
ADAM_EPS = 1e-08
ADAM_WD = 0.01
ADAM_STEP = 10

VMEM_LIMIT = 56 * 1024 * 1024
MESH_IDS = pl.DeviceIdType.MESH


def _row_tile(rows, limit):
    best = 16
    for t in range(16, min(rows, limit) + 1, 16):
        if rows % t == 0:
            best = t
    return best


TM = _row_tile(LP, 688)
TM_BIG = _row_tile(LP, 1376)
TK_LONG = _row_tile(LP, 2752)


def _cparams(*sem):
    return pltpu.CompilerParams(dimension_semantics=sem, vmem_limit_bytes=VMEM_LIMIT)


def _dot(a, b):
    return jnp.dot(a.astype(BF16), b.astype(BF16), preferred_element_type=F32)


def _dot_nt(a, b):
    return lax.dot_general(a.astype(BF16), b.astype(BF16), (((1,), (1,)), ((), ())), preferred_element_type=F32)


def _dot_tn(a, b):
    return lax.dot_general(a.astype(BF16), b.astype(BF16), (((0,), (0,)), ((), ())), preferred_element_type=F32)


def _split3(x):
    hi = x.astype(BF16)
    r1 = x - hi.astype(F32)
    mid = r1.astype(BF16)
    lo = (r1 - mid.astype(F32)).astype(BF16)
    return hi, mid, lo


def _dot_exact_rhs(t, x):
    n = x.shape[1]
    parts = jnp.dot(t.astype(BF16), jnp.concatenate(_split3(x), axis=1), preferred_element_type=F32)
    return parts[:, :n] + parts[:, n:2 * n] + parts[:, 2 * n:]


def _dot_tn_exact_lhs(x, ones):
    n = x.shape[1]
    parts = lax.dot_general(jnp.concatenate(_split3(x), axis=1), ones.astype(BF16), (((0,), (0,)), ((), ())),
                            preferred_element_type=F32)
    return parts[:n] + parts[n:2 * n] + parts[2 * n:]


def _sigmoid(x):
    return 1.0 / (1.0 + jnp.exp(-x))


def _matmul(a, b, *, ta=False, tb=False, out_dtype, tm, tn, tk, name):
    m = a.shape[1] if ta else a.shape[0]
    k = a.shape[0] if ta else a.shape[1]
    n = b.shape[0] if tb else b.shape[1]
    assert (b.shape[1] if tb else b.shape[0]) == k
    assert m % tm == 0 and n % tn == 0 and k % tk == 0, (name, m, n, k, tm, tn, tk)
    nk = k // tk
    a_spec = pl.BlockSpec((tk, tm), lambda i, j, kk: (kk, i)) if ta else pl.BlockSpec((tm, tk), lambda i, j, kk: (i, kk))
    b_spec = pl.BlockSpec((tn, tk), lambda i, j, kk: (j, kk)) if tb else pl.BlockSpec((tk, tn), lambda i, j, kk: (kk, j))
    dims = (((0 if ta else 1,), (1 if tb else 0,)), ((), ()))

    def body(a_ref, b_ref, o_ref, *acc):
        prod = lax.dot_general(a_ref[...].astype(BF16), b_ref[...].astype(BF16), dims, preferred_element_type=F32)
        if nk == 1:
            o_ref[...] = prod.astype(out_dtype)
            return
        acc_ref, = acc
        kk = pl.program_id(2)

        @pl.when(kk == 0)
        def _():
            acc_ref[...] = prod

        @pl.when(kk > 0)
        def _():
            acc_ref[...] += prod

        @pl.when(kk == nk - 1)
        def _():
            o_ref[...] = acc_ref[...].astype(out_dtype)

    return pl.pallas_call(
        body, name=name, grid=(m // tm, n // tn, nk),
        in_specs=[a_spec, b_spec],
        out_specs=pl.BlockSpec((tm, tn), lambda i, j, kk: (i, j)),
        out_shape=jax.ShapeDtypeStruct((m, n), out_dtype),
        scratch_shapes=[pltpu.VMEM((tm, tn), F32)] if nk > 1 else [],
        compiler_params=_cparams("parallel", "parallel", "arbitrary"),
    )(a, b)


def _norm_matmul(x, w, b, *, out_dtype, tm, tn, name):
    n = b.shape[1]
    assert LP % tm == 0 and n % tn == 0

    def body(x_ref, w_ref, b_ref, a_ref, o_ref, a_scr):
        @pl.when(pl.program_id(1) == 0)
        def _():
            xv = x_ref[...]
            r = lax.rsqrt(jnp.mean(xv * xv, axis=-1, keepdims=True) + EPS)
            a = (xv * r * w_ref[...]).astype(BF16)
            a_scr[...] = a
            a_ref[...] = a

        o_ref[...] = jnp.dot(a_scr[...], b_ref[...], preferred_element_type=F32).astype(out_dtype)

    return pl.pallas_call(
        body, name=name, grid=(LP // tm, n // tn),
        in_specs=[pl.BlockSpec((tm, D), lambda i, j: (i, 0)), pl.BlockSpec((1, D), lambda i, j: (0, 0)),
                  pl.BlockSpec((D, tn), lambda i, j: (0, j))],
        out_specs=[pl.BlockSpec((tm, D), lambda i, j: (i, 0)), pl.BlockSpec((tm, tn), lambda i, j: (i, j))],
        out_shape=[jax.ShapeDtypeStruct((LP, D), BF16), jax.ShapeDtypeStruct((LP, n), out_dtype)],
        scratch_shapes=[pltpu.VMEM((tm, D), BF16)],
        compiler_params=_cparams("arbitrary", "arbitrary"),
    )(x, w, b)


def _matmul_resid_norm(a, b, h, w, name, target=None):
    k = a.shape[1]
    has_loss = target is not None

    def body(a_ref, b_ref, h_ref, w_ref, *refs):
        m = jnp.dot(a_ref[...].astype(BF16), b_ref[...].astype(BF16), preferred_element_type=F32)
        r = lax.rsqrt(jnp.mean(m * m, axis=-1, keepdims=True) + EPS)
        i = pl.program_id(0)
        row = i * TM + lax.broadcasted_iota(jnp.int32, (TM, 1), 0)
        y = h_ref[...] + jnp.where(row >= PAD_ROWS, m * r * w_ref[...], 0.0)
        if not has_loss:
            m_ref, y_ref = refs
            m_ref[...] = m
            y_ref[...] = y
            return
        t_ref, m_ref, dy_ref, loss_ref = refs
        m_ref[...] = m

        @pl.when(i == 0)
        def _():
            loss_ref[...] = jnp.zeros_like(loss_ref)

        diff = jnp.where(row >= CHUNK, y - t_ref[...], 0.0)
        dy_ref[...] = diff * (1.0 / D)
        loss_ref[...] += (0.5 / D) * jnp.sum(diff * diff)

    tile = pl.BlockSpec((TM, D), lambda i: (i, 0))
    shape = jax.ShapeDtypeStruct((LP, D), F32)
    in_specs = [pl.BlockSpec((TM, k), lambda i: (i, 0)), pl.BlockSpec((k, D), lambda i: (0, 0)), tile,
                pl.BlockSpec((1, D), lambda i: (0, 0))]
    if has_loss:
        return pl.pallas_call(
            body, name=name, grid=(LP // TM,),
            in_specs=in_specs + [tile],
            out_specs=[tile, tile, pl.BlockSpec((8, 128), lambda i: (0, 0))],
            out_shape=[shape, shape, jax.ShapeDtypeStruct((8, 128), F32)],
            compiler_params=_cparams("arbitrary"),
        )(a, b, h, w, target)
    return pl.pallas_call(
        body, name=name, grid=(LP // TM,),
        in_specs=in_specs, out_specs=[tile, tile], out_shape=[shape, shape],
        compiler_params=_cparams("parallel"),
    )(a, b, h, w)


def _rmsnorm_bwd_rows(dy, x, w):
    r = lax.rsqrt(jnp.mean(x * x, axis=-1, keepdims=True) + EPS)
    g = dy * w
    dx = r * g - x * (r * r * r * jnp.mean(g * x, axis=-1, keepdims=True))
    return dx, jnp.sum(dy * x * r, axis=0, keepdims=True)


def _matmul_norm_bwd(dz, b, x, w, resid, tk, name, carried=()):
    k = dz.shape[1]
    assert k % tk == 0
    nk = k // tk
    n_rows = LP // TM
    n_carried = len(carried)

    def body(*refs):
        a_ref, b_ref, x_ref, w_ref, r_ref = refs[:5]
        g_refs, refs = refs[5:5 + n_carried], refs[5 + n_carried:]
        dx_ref, dw_ref = refs[:2]
        got_refs, refs = refs[2:2 + n_carried], refs[2 + n_carried:]
        acc, sems = (refs[:1], refs[1:]) if nk > 1 else ((), refs)
        i, kk = pl.program_id(0), pl.program_id(1)
        if n_carried:
            exchange_start, exchange_finish = _exchange_phases(g_refs, got_refs, *sems)
            pl.when((i == 0) & (kk == 0))(exchange_start)

        @pl.when((i == 0) & (kk == 0))
        def _():
            dw_ref[...] = jnp.zeros_like(dw_ref)

        prod = lax.dot_general(a_ref[...].astype(BF16), b_ref[...].astype(BF16), (((1,), (1,)), ((), ())),
                               preferred_element_type=F32)

        def finish(dy):
            dx, dw = _rmsnorm_bwd_rows(dy, x_ref[...], w_ref[...])
            dx_ref[...] = dx + r_ref[...]
            dw_ref[0:1, :] += dw

        if nk == 1:
            finish(prod)
        else:
            acc_ref, = acc

            @pl.when(kk == 0)
            def _():
                acc_ref[...] = prod

            @pl.when((kk > 0) & (kk < nk - 1))
            def _():
                acc_ref[...] += prod

            @pl.when(kk == nk - 1)
            def _():
                finish(acc_ref[...] + prod)

        if n_carried:
            pl.when((i == n_rows - 1) & (kk == nk - 1))(exchange_finish)

    tile = pl.BlockSpec((TM, D), lambda i, kk: (i, 0))
    anywhere = [pl.BlockSpec(memory_space=pl.ANY)] * n_carried
    return pl.pallas_call(
        body, name=name, grid=(n_rows, nk),
        in_specs=[pl.BlockSpec((TM, tk), lambda i, kk: (i, kk)), pl.BlockSpec((D, tk), lambda i, kk: (0, kk)), tile,
                  pl.BlockSpec((1, D), lambda i, kk: (0, 0)), tile] + anywhere,
        out_specs=[tile, pl.BlockSpec((8, D), lambda i, kk: (0, 0))] + anywhere,
        out_shape=[jax.ShapeDtypeStruct((LP, D), F32), jax.ShapeDtypeStruct((8, D), F32)]
        + [jax.ShapeDtypeStruct(g.shape, g.dtype) for g in carried],
        scratch_shapes=([pltpu.VMEM((TM, D), F32)] if nk > 1 else []) + _exchange_sems(n_carried),
        compiler_params=_cparams("arbitrary", "arbitrary"),
    )(dz, b, x, w, resid, *carried)


def _norm_bwd_matmul(dh, x, w, b, out_dtype, name):
    n = b.shape[0]

    def body(dh_ref, x_ref, w_ref, b_ref, o_ref, dx_ref, dw_ref):
        i = pl.program_id(0)

        @pl.when(i == 0)
        def _():
            dw_ref[...] = jnp.zeros_like(dw_ref)

        row = i * TM + lax.broadcasted_iota(jnp.int32, (TM, 1), 0)
        dy = jnp.where(row >= PAD_ROWS, dh_ref[...], 0.0)
        dx, dw = _rmsnorm_bwd_rows(dy, x_ref[...], w_ref[...])
        dxb = dx.astype(BF16)
        dx_ref[...] = dxb
        dw_ref[0:1, :] += dw
        o_ref[...] = lax.dot_general(dxb, b_ref[...].astype(BF16), (((1,), (1,)), ((), ())),
                                     preferred_element_type=F32).astype(out_dtype)

    tile = pl.BlockSpec((TM, D), lambda i: (i, 0))
    return pl.pallas_call(
        body, name=name, grid=(LP // TM,),
        in_specs=[tile, tile, pl.BlockSpec((1, D), lambda i: (0, 0)), pl.BlockSpec((n, D), lambda i: (0, 0))],
        out_specs=[pl.BlockSpec((TM, n), lambda i: (i, 0)), tile, pl.BlockSpec((8, D), lambda i: (0, 0))],
        out_shape=[jax.ShapeDtypeStruct((LP, n), out_dtype), jax.ShapeDtypeStruct((LP, D), BF16),
                   jax.ShapeDtypeStruct((8, D), F32)],
        compiler_params=_cparams("arbitrary"),
    )(dh, x, w, b)


GELU_C = math.sqrt(2.0 / math.pi)
GELU_K = 0.044715
STRIP = 16
HALF = 8


def _gelu_half(a):
    return 0.5 * jnp.tanh(a * (a * a * (GELU_C * GELU_K) + GELU_C)) + 0.5


def _gelu_slope(a, h):
    return h * (1.0 + (a - a * h) * (a * a * (6.0 * GELU_C * GELU_K) + 2.0 * GELU_C))


def _shift_down(x, prev8):
    row = lax.broadcasted_iota(jnp.int32, (8, 1), 0)
    r1, r2 = pltpu.roll(x, 1, 0), pltpu.roll(x, 2, 0)
    top1 = jnp.where(row < 1, pltpu.roll(prev8, 1, 0), r1[0:8, :])
    top2 = jnp.where(row < 2, pltpu.roll(prev8, 2, 0), r2[0:8, :])
    return jnp.concatenate([top1, r1[8:, :]], axis=0), jnp.concatenate([top2, r2[8:, :]], axis=0)


def _conv_act_fwd(u, cw8, name, carried=()):
    n_rows = LP // TM
    cb2 = 2 * CONV_BLOCK
    n_carried = len(carried)

    def body(*refs):
        u_ref, cw_ref = refs[:2]
        x_refs, refs = refs[2:2 + n_carried], refs[2 + n_carried:]
        conv_ref, act_ref = refs[:2]
        gathered_refs, refs = refs[2:2 + n_carried], refs[2 + n_carried:]
        carry_ref = refs[0]
        j, i = pl.program_id(0), pl.program_id(1)
        if n_carried:
            start, forward, finish = _gather_phases(x_refs, gathered_refs, *refs[1:])
            pl.when((j == 0) & (i == 0))(start)
            pl.when((j == (3 * N_CONV_BLOCKS) // 4) & (i == 0))(forward)

        @pl.when(i == 0)
        def _():
            carry_ref[...] = jnp.zeros_like(carry_ref)

        x = u_ref[...].astype(F32)
        x1, x2 = _shift_down(x, carry_ref[...])
        conv = cw_ref[3:4, :] + x2 * cw_ref[0:1, :] + x1 * cw_ref[1:2, :] + x * cw_ref[2:3, :]
        conv_ref[...] = conv.astype(BF16)
        a = conv[:, :CONV_BLOCK]
        g = conv[:, CONV_BLOCK:]
        act_ref[...] = (a * _gelu_half(a) * g).astype(BF16)
        carry_ref[...] = x[TM - 8:TM, :]
        if n_carried:
            pl.when((j == N_CONV_BLOCKS - 1) & (i == n_rows - 1))(finish)

    anywhere = [pl.BlockSpec(memory_space=pl.ANY)] * n_carried
    return pl.pallas_call(
        body, name=name, grid=(N_CONV_BLOCKS, n_rows),
        in_specs=[pl.BlockSpec((TM, cb2), lambda j, i: (i, j)), pl.BlockSpec((8, cb2), lambda j, i: (0, j))] + anywhere,
        out_specs=[pl.BlockSpec((TM, cb2), lambda j, i: (i, j)),
                   pl.BlockSpec((TM, CONV_BLOCK), lambda j, i: (i, j))] + anywhere,
        out_shape=[jax.ShapeDtypeStruct((LP, D_UP), BF16), jax.ShapeDtypeStruct((LP, D_FF), BF16)]
        + _gathered_shapes(carried),
        scratch_shapes=[pltpu.VMEM((8, cb2), F32)] + _exchange_sems(n_carried),
        compiler_params=_cparams("arbitrary", "arbitrary"),
    )(u, cw8, *carried)


def _conv_act_bwd(dact, conv, u, cw8, name):
    n_rows = LP // TM
    cb2 = 2 * CONV_BLOCK
    n_strips = TM // STRIP

    def body(dact_ref, conv_ref, u_ref, cw_ref, du_ref, dcw_ref, carry_ref):
        i = pl.program_id(1)

        @pl.when(i == 0)
        def _():
            dcw_ref[...] = jnp.zeros_like(dcw_ref)
            carry_ref[...] = jnp.zeros_like(carry_ref)

        w0, w1, w2 = cw_ref[0:1, :], cw_ref[1:2, :], cw_ref[2:3, :]
        row = lax.broadcasted_iota(jnp.int32, (HALF, 1), 0)

        def strip(k, carry):
            n1, n2, s0, s1, s2, s3 = carry
            r0 = pl.multiple_of((n_strips - 1 - k) * STRIP, STRIP)
            cv = conv_ref[pl.ds(r0, STRIP), :].astype(F32)
            dav = dact_ref[pl.ds(r0, STRIP), :].astype(F32)
            x = u_ref[pl.ds(r0, STRIP), :].astype(F32)
            du = [None, None]
            for half in (1, 0):
                rows = slice(HALF * half, HALF * (half + 1))
                a, g, dah = cv[rows, :CONV_BLOCK], cv[rows, CONV_BLOCK:], dav[rows]
                h = _gelu_half(a)
                dconv = jnp.concatenate([dah * g * _gelu_slope(a, h), dah * (a * h)], axis=1)
                u1, u2 = pltpu.roll(dconv, HALF - 1, 0), pltpu.roll(dconv, HALF - 2, 0)
                d1 = jnp.where(row >= HALF - 1, n1, u1)
                d2 = jnp.where(row >= HALF - 2, n2, u2)
                du[half] = dconv * w2 + d1 * w1 + d2 * w0
                s0, s1, s2, s3 = s0 + d2 * x[rows], s1 + d1 * x[rows], s2 + dconv * x[rows], s3 + dconv
                n1, n2 = u1, u2
            du_ref[pl.ds(r0, STRIP), :] = jnp.concatenate(du, axis=0).astype(BF16)
            return n1, n2, s0, s1, s2, s3

        below = carry_ref[...]
        zero = jnp.zeros((HALF, cb2), F32)
        init = (pltpu.roll(below, HALF - 1, 0), pltpu.roll(below, HALF - 2, 0), zero, zero, zero, zero)
        u1, _, s0, s1, s2, s3 = lax.fori_loop(0, n_strips, strip, init, unroll=2)
        carry_ref[...] = pltpu.roll(u1, 1, 0)
        dcw_ref[0:1, :] += jnp.sum(s0, axis=0, keepdims=True)
        dcw_ref[1:2, :] += jnp.sum(s1, axis=0, keepdims=True)
        dcw_ref[2:3, :] += jnp.sum(s2, axis=0, keepdims=True)
        dcw_ref[3:4, :] += jnp.sum(s3, axis=0, keepdims=True)

    rev = lambda j, i: (n_rows - 1 - i, j)
    return pl.pallas_call(
        body, name=name, grid=(N_CONV_BLOCKS, n_rows),
        in_specs=[pl.BlockSpec((TM, CONV_BLOCK), rev), pl.BlockSpec((TM, cb2), rev), pl.BlockSpec((TM, cb2), rev),
                  pl.BlockSpec((8, cb2), lambda j, i: (0, j))],
        out_specs=[pl.BlockSpec((TM, cb2), rev), pl.BlockSpec((8, cb2), lambda j, i: (0, j))],
        out_shape=[jax.ShapeDtypeStruct((LP, D_UP), BF16), jax.ShapeDtypeStruct((8, D_UP), F32)],
        scratch_shapes=[pltpu.VMEM((HALF, cb2), F32)],
        compiler_params=_cparams("arbitrary", "arbitrary"),
    )(dact, conv, u, cw8)


CHUNKS_PER_STEP = 3 if N_CHUNKS % 3 == 0 else 1
STEP_ROWS = CHUNKS_PER_STEP * CHUNK
N_STEPS = N_CHUNKS // CHUNKS_PER_STEP


def _ret_consts(h):
    rows = STEP_ROWS
    lg = math.log(1.0 - 2.0 ** (-5.0 - h))
    ri = lax.broadcasted_iota(jnp.int32, (rows, rows), 0)
    ci = lax.broadcasted_iota(jnp.int32, (rows, rows), 1)
    diff = (ri - ci).astype(F32)
    dmat = jnp.where(diff >= 0, jnp.exp(lg * jnp.maximum(diff, 0.0)), 0.0)
    rowf = lax.broadcasted_iota(jnp.int32, (rows, 1), 0).astype(F32)
    zeta = jnp.exp(lg * (rows - 1.0 - rowf))
    xi = jnp.exp(lg * (rowf + 1.0))
    return dmat, zeta, xi, math.exp(lg * rows)


def _rope(t, cosv, sinv):
    return t * cosv + pltpu.roll(t, RET_DK // 2, 1) * sinv


def _unrope(d, cosv, sinv):
    return d * cosv + pltpu.roll(d * sinv, RET_DK // 2, 1)


def _gla_masks():
    ri = lax.broadcasted_iota(jnp.int32, (CHUNK, CHUNK), 0)
    ci = lax.broadcasted_iota(jnp.int32, (CHUNK, CHUNK), 1)
    return dict(ri=ri, ci=ci, tril=(ri >= ci).astype(F32), heads=_head_block_mask(), own=_state_block_mask())


def _gla_common(p_ref, w2_ref, gb_ref, chunk, rows, masks):
    row = lax.broadcasted_iota(jnp.int32, (CHUNK, 1), 0)
    real = (chunk * CHUNK + row) >= PAD_ROWS
    ga = p_ref[rows, O_GA:O_GA + 128]
    z = _dot(ga, w2_ref[...]) + gb_ref[...]
    la = (jnp.minimum(z, 0.0) - jnp.log(1.0 + jnp.exp(-jnp.abs(z)))) * (1.0 / GLA_TAU)
    la = jnp.where(real, la, 0.0)
    ri, ci = masks["ri"], masks["ci"]
    cum = _dot_exact_rhs(masks["tril"], la)
    last = cum[CHUNK - 1:CHUNK, :]
    qs = p_ref[rows, O_GQ:O_GQ + 256] * (GLA_DK ** -0.5)
    k = p_ref[rows, O_GK:O_GK + 256]
    ecum = jnp.exp(cum)
    ekl = jnp.exp(last - cum)
    el = jnp.exp(last)
    refs = [jnp.zeros((1, 256), F32)] + [cum[a * SUB - 1:a * SUB, :] for a in range(1, N_SUB)]
    eq = [jnp.exp(cum[a * SUB:(a + 1) * SUB, :] - refs[a]) for a in range(N_SUB)]
    spread = refs[0] - cum[SUB - 1:SUB, :]
    for a in range(1, N_SUB):
        spread = jnp.maximum(spread, refs[a] - cum[(a + 1) * SUB - 1:(a + 1) * SUB, :])
    small = jnp.max(spread) <= GLA_FACTORED_MAX
    return dict(real=real, row=row, z=z, la=la, cum=cum, last=last, qs=qs, k=k, ecum=ecum, ekl=ekl, el=el,
                refs=refs, eq=eq, small=small, ri=ri, ci=ci, masks=masks)


GLA_FACTORED_MAX = 40.0


def _head_block_mask():
    r = lax.broadcasted_iota(jnp.int32, (CHUNK, 256), 0)
    col = lax.broadcasted_iota(jnp.int32, (CHUNK, 256), 1)
    return (r // SUB) == (col // GLA_DK)


def _state_block_mask():
    r = lax.broadcasted_iota(jnp.int32, (GLA_HEADS * GLA_DK, GLA_HEADS * GLA_DV), 0)
    col = lax.broadcasted_iota(jnp.int32, (GLA_HEADS * GLA_DK, GLA_HEADS * GLA_DV), 1)
    return (r // GLA_DK) == (col // GLA_DV)


def _block_diagonal(blocks):
    zero = jnp.zeros((GLA_DK, GLA_DV), F32)
    return jnp.concatenate([jnp.concatenate([blocks[h] if g == h else zero for g in range(GLA_HEADS)], axis=1)
                            for h in range(GLA_HEADS)], axis=0)


def _gla_factored(c):
    mask = c["masks"]["heads"]
    eks, keys, queries = [], [], []
    for a in range(N_SUB):
        ek = jnp.exp(jnp.minimum(c["refs"][a] - c["cum"], GLA_FACTORED_MAX))
        qh = c["qs"][a * SUB:(a + 1) * SUB, :] * c["eq"][a]
        eks.append(ek)
        keys.append(c["k"] * ek)
        queries.append(jnp.where(mask, jnp.concatenate([qh] * GLA_HEADS, axis=0), 0.0))
    return eks, keys, queries


def _gla_scores_factored(c, factored, p_scr):
    _, keys, queries = factored
    for a in range(N_SUB):
        out = _dot_nt(queries[a], keys[a])
        out = jnp.where(c["ci"] <= a * SUB + (c["ri"] & (SUB - 1)), out, 0.0)
        for h in range(GLA_HEADS):
            p_scr[h, a * SUB:(a + 1) * SUB, :] = out[h * SUB:(h + 1) * SUB, :]


def _gla_intra_bwd_factored(c, factored, dps, dq_scr, dk_scr):
    eks, keys, queries = factored
    mask = c["masks"]["heads"]
    dk = jnp.zeros((CHUNK, 256), F32)
    for a in range(N_SUB):
        dpa = jnp.concatenate([dps[h][a * SUB:(a + 1) * SUB, :] for h in range(GLA_HEADS)], axis=0)
        dq = jnp.where(mask, _dot(dpa, keys[a]), 0.0)
        dq = dq[0:SUB] + dq[SUB:2 * SUB] + dq[2 * SUB:3 * SUB] + dq[3 * SUB:4 * SUB]
        dq_scr[a * SUB:(a + 1) * SUB, :] = dq * c["eq"][a]
        dk = dk + _dot_tn(dpa, queries[a]) * eks[a]
    dk_scr[...] = dk


def _gla_lag_weights(c):
    cum, row = c["cum"], c["row"]
    out = [jnp.ones((CHUNK, 256), F32)]
    for r in range(1, SUB):
        out.append(jnp.where((row % SUB) >= r, jnp.exp(jnp.minimum(cum - pltpu.roll(cum, r, 0), 0.0)), 0.0))
    return out


def _gla_pairwise_keys(c):
    return [None] + [c["k"] * jnp.exp(jnp.minimum(c["refs"][a] - c["cum"], 0.0)) for a in range(1, N_SUB)]


def _gla_scores_pairwise(c, lag_w, keys, h):
    sl = slice(GLA_DK * h, GLA_DK * (h + 1))
    qs, k = c["qs"][:, sl], c["k"][:, sl]
    ri, ci = c["ri"], c["ci"]
    p = jnp.zeros((CHUNK, CHUNK), F32)
    for r in range(SUB):
        kr = k if r == 0 else pltpu.roll(k, r, 0)
        pr = jnp.sum(qs * kr * lag_w[r][:, sl], axis=1, keepdims=True)
        p = p + jnp.where(ci == ri - r, pr, 0.0)
    blocks = [jnp.zeros((SUB, CHUNK), F32)]
    for a in range(1, N_SUB):
        qh = qs[a * SUB:(a + 1) * SUB, :] * c["eq"][a][:, sl]
        blocks.append(jnp.where(ci[:SUB, :] < a * SUB, _dot_nt(qh, keys[a][:, sl]), 0.0))
    return p + jnp.concatenate(blocks, axis=0)


def _gla_all_scores(c, p_scr, factored):
    if factored:
        _gla_scores_factored(c, _gla_factored(c), p_scr)
    else:
        lag_w, keys = _gla_lag_weights(c), _gla_pairwise_keys(c)
        for h in range(GLA_HEADS):
            p_scr[h] = _gla_scores_pairwise(c, lag_w, keys, h)


def _either_form(chunks, run):
    small = chunks[0]["small"]
    for c in chunks[1:]:
        small = jnp.logical_and(small, c["small"])
    pl.when(small)(lambda: run(True))
    pl.when(jnp.logical_not(small))(lambda: run(False))


def _gla_intra_bwd_pairwise(c, lag_w, keys, dp, h):
    sl = slice(GLA_DK * h, GLA_DK * (h + 1))
    qs_h, k_h = c["qs"][:, sl], c["k"][:, sl]
    ri, ci = c["ri"], c["ci"]
    dq_rows = [jnp.zeros((SUB, GLA_DK), F32)]
    dk = jnp.zeros((CHUNK, GLA_DK), F32)
    for a in range(1, N_SUB):
        eq = c["eq"][a][:, sl]
        qh = qs_h[a * SUB:(a + 1) * SUB, :] * eq
        dpa = jnp.where(ci[:SUB, :] < a * SUB, dp[a * SUB:(a + 1) * SUB, :], 0.0)
        dq_rows.append(_dot(dpa, keys[a][:, sl]) * eq)
        ek = jnp.exp(jnp.minimum(c["refs"][a][:, sl] - c["cum"][:, sl], 0.0))
        dk = dk + _dot_tn(dpa, qh) * ek
    dq = jnp.concatenate(dq_rows, axis=0)
    for r in range(SUB):
        w = lag_w[r][:, sl]
        dpr = jnp.sum(jnp.where(ci == ri - r, dp, 0.0), axis=1, keepdims=True)
        kr = k_h if r == 0 else pltpu.roll(k_h, r, 0)
        dq = dq + dpr * kr * w
        back = dpr * qs_h * w
        dk = dk + (back if r == 0 else pltpu.roll(back, CHUNK - r, 0))
    return dq, dk


def _gla_all_intra_bwd(c, dps, p_scr, dq_scr, dk_scr, factored):
    if factored:
        terms = _gla_factored(c)
        _gla_scores_factored(c, terms, p_scr)
        _gla_intra_bwd_factored(c, terms, dps, dq_scr, dk_scr)
    else:
        lag_w, keys = _gla_lag_weights(c), _gla_pairwise_keys(c)
        outs = [_gla_intra_bwd_pairwise(c, lag_w, keys, dps[h], h) for h in range(GLA_HEADS)]
        for h in range(GLA_HEADS):
            p_scr[h] = _gla_scores_pairwise(c, lag_w, keys, h)
        dq_scr[...] = jnp.concatenate([o[0] for o in outs], axis=1)
        dk_scr[...] = jnp.concatenate([o[1] for o in outs], axis=1)


def _mixer_fwd(proj, cos2, sin2, w2p, gb, rnw, gnw, name, carried=()):
    n_carried = len(carried)

    def body(*refs):
        p_ref, c_ref, s_ref, w2_ref, gb_ref, rnw_ref, gnw_ref = refs[:7]
        x_refs, refs = refs[7:7 + n_carried], refs[7 + n_carried:]
        ocat_ref, mrg_ref, sr_out, sg_out = refs[:4]
        gathered_refs, refs = refs[4:4 + n_carried], refs[4 + n_carried:]
        sr, sg, p_scr = refs[:3]
        n = pl.program_id(0)
        if n_carried:
            start, forward, finish = _gather_phases(x_refs, gathered_refs, *refs[3:])
            pl.when(n == 0)(start)
            pl.when(n == (3 * N_STEPS) // 4)(forward)

        @pl.when(n == 0)
        def _():
            sr[...] = jnp.zeros_like(sr)
            sg[...] = jnp.zeros_like(sg)

        sr_out[0] = sr[...]
        cosv, sinv = c_ref[...], s_ref[...]

        for h in range(RET_HEADS):
            dmat, zeta, xi, gc = _ret_consts(h)
            hs = slice(128 * h, 128 * (h + 1))
            q = _rope(p_ref[:, O_RQ + 128 * h:O_RQ + 128 * (h + 1)], cosv, sinv)
            k = _rope(p_ref[:, O_RK + 128 * h:O_RK + 128 * (h + 1)], cosv, sinv) * (RET_DK ** -0.5)
            v = p_ref[:, O_RV + 128 * h:O_RV + 128 * (h + 1)]
            g = p_ref[:, O_RG + 128 * h:O_RG + 128 * (h + 1)]
            s_in = sr[h]
            a = _dot_nt(q, k) * dmat
            o = _dot(a, v) + _dot(q, s_in) * xi
            sr[h] = gc * s_in + _dot_tn(k * zeta, v)
            mu = jnp.mean(o, axis=-1, keepdims=True)
            xc = o - mu
            nrm = xc * lax.rsqrt(jnp.mean(xc * xc, axis=-1, keepdims=True) + EPS)
            ocat_ref[:, hs] = o
            mrg_ref[:, hs] = (nrm * rnw_ref[:, hs] * (g * _sigmoid(g))).astype(BF16)

        row_slices = [slice(CHUNK * j, CHUNK * (j + 1)) for j in range(CHUNKS_PER_STEP)]
        masks = _gla_masks()
        chunks = [_gla_common(p_ref, w2_ref, gb_ref, n * CHUNKS_PER_STEP + j, rows, masks)
                  for j, rows in enumerate(row_slices)]

        def gla_chunks(factored):
            own = masks["own"]
            for j, (rows, c) in enumerate(zip(row_slices, chunks)):
                s_in = sg[...]
                for h in range(GLA_HEADS):
                    sg_out[j, h] = s_in[GLA_DK * h:GLA_DK * (h + 1), GLA_DV * h:GLA_DV * (h + 1)]
                _gla_all_scores(c, p_scr.at[j], factored)
                v_all = p_ref[rows, O_GV:O_GV + GLA_HEADS * GLA_DV]
                o_inter = _dot(c["qs"] * c["ecum"], s_in)
                decay = jnp.exp(_dot_tn_exact_lhs(c["la"], jnp.ones((CHUNK, GLA_HEADS * GLA_DV), F32)))
                sg[...] = decay * s_in + jnp.where(own, _dot_tn(c["k"] * c["ekl"], v_all), 0.0)
                o_intra = _dot(p_scr[j].reshape(GLA_HEADS * CHUNK, CHUNK), v_all)
                for h in range(GLA_HEADS):
                    hs = slice(512 + 128 * h, 512 + 128 * (h + 1))
                    g = p_ref[rows, O_GR + 128 * h:O_GR + 128 * (h + 1)]
                    o = (o_intra[CHUNK * h:CHUNK * (h + 1), GLA_DV * h:GLA_DV * (h + 1)]
                         + o_inter[:, GLA_DV * h:GLA_DV * (h + 1)])
                    nrm = o * lax.rsqrt(jnp.mean(o * o, axis=-1, keepdims=True) + EPS)
                    ocat_ref[rows, hs] = o
                    mrg_ref[rows, hs] = (nrm * gnw_ref[:, 128 * h:128 * (h + 1)] * (g * _sigmoid(g))).astype(BF16)

        _either_form(chunks, gla_chunks)

        if n_carried:
            pl.when(n == N_STEPS - 1)(finish)

    const = lambda shape: pl.BlockSpec(shape, lambda n: (0,) * len(shape))
    anywhere = [pl.BlockSpec(memory_space=pl.ANY)] * n_carried
    return pl.pallas_call(
        body, name=name, grid=(N_STEPS,),
        in_specs=[pl.BlockSpec((STEP_ROWS, IN_WP), lambda n: (n, 0)),
                  pl.BlockSpec((STEP_ROWS, 128), lambda n: (n, 0)), pl.BlockSpec((STEP_ROWS, 128), lambda n: (n, 0)),
                  const((128, 256)), const((1, 256)), const((1, 512)), const((1, 512))] + anywhere,
        out_specs=[pl.BlockSpec((STEP_ROWS, D), lambda n: (n, 0)), pl.BlockSpec((STEP_ROWS, D), lambda n: (n, 0)),
                   pl.BlockSpec((1, RET_HEADS, RET_DK, 128), lambda n: (n, 0, 0, 0)),
                   pl.BlockSpec((CHUNKS_PER_STEP, GLA_HEADS, GLA_DK, GLA_DV), lambda n: (n, 0, 0, 0))] + anywhere,
        out_shape=[jax.ShapeDtypeStruct((LP, D), F32), jax.ShapeDtypeStruct((LP, D), BF16),
                   jax.ShapeDtypeStruct((N_STEPS, RET_HEADS, RET_DK, 128), F32),
                   jax.ShapeDtypeStruct((N_CHUNKS, GLA_HEADS, GLA_DK, GLA_DV), F32)] + _gathered_shapes(carried),
        scratch_shapes=[pltpu.VMEM((RET_HEADS, RET_DK, 128), F32),
                        pltpu.VMEM((GLA_HEADS * GLA_DK, GLA_HEADS * GLA_DV), F32),
                        pltpu.VMEM((CHUNKS_PER_STEP, GLA_HEADS, CHUNK, CHUNK), F32)] + _exchange_sems(n_carried),
        compiler_params=_cparams("arbitrary"),
    )(proj, cos2, sin2, w2p, gb, rnw, gnw, *carried)


def _mixer_bwd(proj, ocat, dmrg, sr_all, sg_all, cos2, sin2, w2p, gb, rnw, gnw, name, carried=()):
    last_step = N_STEPS - 1
    n_carried = len(carried)

    def body(*refs):
        p_ref, ocat_ref, dm_ref, sr_ref, sg_ref, c_ref, s_ref, w2_ref, gb_ref, rnw_ref, gnw_ref = refs[:11]
        g_refs, refs = refs[11:11 + n_carried], refs[11 + n_carried:]
        dp_ref, dw2_ref, dgb_ref, drn_ref, dgn_ref = refs[:5]
        got_refs, refs = refs[5:5 + n_carried], refs[5 + n_carried:]
        dsr, dsg, p_scr, dq_scr, dk_scr = refs[:5]
        step = pl.program_id(0)
        n = last_step - step
        if n_carried:
            start, finish = _exchange_phases(g_refs, got_refs, *refs[5:])
            pl.when(step == 0)(start)

        @pl.when(step == 0)
        def _():
            dsr[...] = jnp.zeros_like(dsr)
            dsg[...] = jnp.zeros_like(dsg)
            dw2_ref[...] = jnp.zeros_like(dw2_ref)
            dgb_ref[...] = jnp.zeros_like(dgb_ref)
            drn_ref[...] = jnp.zeros_like(drn_ref)
            dgn_ref[...] = jnp.zeros_like(dgn_ref)

        cosv, sinv = c_ref[...], s_ref[...]
        step_row = lax.broadcasted_iota(jnp.int32, (STEP_ROWS, 1), 0)
        real = ((n * STEP_ROWS + step_row) >= PAD_ROWS).astype(F32)

        for h in range(RET_HEADS):
            dmat, zeta, xi, gc = _ret_consts(h)
            hs = slice(128 * h, 128 * (h + 1))
            q = _rope(p_ref[:, O_RQ + 128 * h:O_RQ + 128 * (h + 1)], cosv, sinv)
            k = _rope(p_ref[:, O_RK + 128 * h:O_RK + 128 * (h + 1)], cosv, sinv) * (RET_DK ** -0.5)
            v = p_ref[:, O_RV + 128 * h:O_RV + 128 * (h + 1)]
            g = p_ref[:, O_RG + 128 * h:O_RG + 128 * (h + 1)]
            o = ocat_ref[:, hs]
            dy = dm_ref[:, hs]
            wv = rnw_ref[:, hs]
            mu = jnp.mean(o, axis=-1, keepdims=True)
            xc = o - mu
            rs = lax.rsqrt(jnp.mean(xc * xc, axis=-1, keepdims=True) + EPS)
            nrm = xc * rs
            sgm = _sigmoid(g)
            sil = g * sgm
            drn_ref[0:1, hs] += jnp.sum(dy * nrm * sil, axis=0, keepdims=True)
            dgate = dy * nrm * wv * (sgm * (1.0 + g * (1.0 - sgm)))
            dn = dy * wv * sil
            do = rs * (dn - jnp.mean(dn, axis=-1, keepdims=True) - nrm * jnp.mean(dn * nrm, axis=-1, keepdims=True))
            s_in = sr_ref[0, h]
            ds_out = dsr[h]
            a = _dot_nt(q, k) * dmat
            da = _dot_nt(do, v) * dmat
            dox = do * xi
            dq = _dot(da, k) + _dot_nt(dox, s_in)
            dk = _dot_tn(da, q) + _dot_nt(v, ds_out) * zeta
            dv = _dot_tn(a, do) + _dot(k * zeta, ds_out)
            dsr[h] = gc * ds_out + _dot_tn(q, dox)
            dk = dk * (RET_DK ** -0.5)
            dp_ref[:, O_RQ + 128 * h:O_RQ + 128 * (h + 1)] = (_unrope(dq, cosv, sinv) * real).astype(BF16)
            dp_ref[:, O_RK + 128 * h:O_RK + 128 * (h + 1)] = (_unrope(dk, cosv, sinv) * real).astype(BF16)
            dp_ref[:, O_RV + 128 * h:O_RV + 128 * (h + 1)] = (dv * real).astype(BF16)
            dp_ref[:, O_RG + 128 * h:O_RG + 128 * (h + 1)] = (dgate * real).astype(BF16)

        row_slices = [slice(CHUNK * j, CHUNK * (j + 1)) for j in range(CHUNKS_PER_STEP)]
        masks = _gla_masks()
        chunks = [_gla_common(p_ref, w2_ref, gb_ref, n * CHUNKS_PER_STEP + j, rows, masks)
                  for j, rows in enumerate(row_slices)]

        def gla_chunks(factored):
            for j in reversed(range(CHUNKS_PER_STEP)):
                gla_chunk_bwd(chunks[j], n * CHUNKS_PER_STEP + j, row_slices[j], j, factored, p_ref, ocat_ref, dm_ref,
                              sg_ref, w2_ref, gnw_ref, dp_ref, dw2_ref, dgb_ref, dgn_ref, dsg, p_scr, dq_scr, dk_scr)

        _either_form(chunks, gla_chunks)
        if n_carried:
            pl.when(step == last_step)(finish)

    def gla_chunk_bwd(c, chunk, rows, j, factored, p_ref, ocat_ref, dm_ref, sg_ref, w2_ref, gnw_ref,
                      dp_ref, dw2_ref, dgb_ref, dgn_ref, dsg, p_scr, dq_scr, dk_scr):
        row = lax.broadcasted_iota(jnp.int32, (CHUNK, 1), 0)
        real = ((chunk * CHUNK + row) >= PAD_ROWS).astype(F32)
        ri, ci = c["ri"], c["ci"]
        causal = ri >= ci
        triu = (ci >= ri).astype(F32)
        qe = c["qs"] * c["ecum"]
        kl = c["k"] * c["ekl"]
        v_all = p_ref[rows, O_GV:O_GV + GLA_HEADS * GLA_DV]
        dos, dps = [], []
        for h in range(GLA_HEADS):
            hs = slice(512 + 128 * h, 512 + 128 * (h + 1))
            g = p_ref[rows, O_GR + 128 * h:O_GR + 128 * (h + 1)]
            o = ocat_ref[rows, hs]
            dy = dm_ref[rows, hs]
            wv = gnw_ref[:, 128 * h:128 * (h + 1)]
            rs = lax.rsqrt(jnp.mean(o * o, axis=-1, keepdims=True) + EPS)
            nrm = o * rs
            sgm = _sigmoid(g)
            sil = g * sgm
            dgn_ref[0:1, 128 * h:128 * (h + 1)] += jnp.sum(dy * nrm * sil, axis=0, keepdims=True)
            dgate = dy * nrm * wv * (sgm * (1.0 + g * (1.0 - sgm)))
            dn = dy * wv * sil
            do = rs * (dn - nrm * jnp.mean(dn * nrm, axis=-1, keepdims=True))
            dp_ref[rows, O_GR + 128 * h:O_GR + 128 * (h + 1)] = (dgate * real).astype(BF16)
            dos.append(do)
        do_all = jnp.concatenate(dos, axis=1)
        do_blocks = jnp.where(c["masks"]["own"], jnp.concatenate([do_all] * GLA_HEADS, axis=0), 0.0)
        dp_all = _dot_nt(do_blocks, v_all)
        dps = [jnp.where(causal, dp_all[CHUNK * h:CHUNK * (h + 1), :], 0.0) for h in range(GLA_HEADS)]
        _gla_all_intra_bwd(c, dps, p_scr.at[j], dq_scr.at[j], dk_scr.at[j], factored)
        s_in = _block_diagonal([sg_ref[j, h] for h in range(GLA_HEADS)])
        ds_out = dsg[...]
        decay = jnp.exp(_dot_tn_exact_lhs(c["la"], jnp.ones((CHUNK, GLA_HEADS * GLA_DV), F32)))
        dv_state = _dot(kl, ds_out)
        dqe = _dot_nt(do_all, s_in)
        dkl = _dot_nt(v_all, ds_out)
        dsg[...] = jnp.where(c["masks"]["own"], _dot_tn(qe, do_all), 0.0) + decay * ds_out
        sd = s_in * ds_out
        sd_hi = sd.astype(BF16)
        sd_lo = (sd - sd_hi.astype(F32)).astype(BF16)
        ones8 = jnp.ones((8, GLA_HEADS * GLA_DV), BF16)
        nt = (((1,), (1,)), ((), ()))
        d_el = (lax.dot_general(ones8, sd_hi, nt, preferred_element_type=F32)
                + lax.dot_general(ones8, sd_lo, nt, preferred_element_type=F32))[0:1, :]
        dqs = dqe * c["ecum"] + dq_scr[j]
        dkk = dkl * c["ekl"] + dk_scr[j]
        d_last = jnp.sum(dkl * kl, axis=0, keepdims=True) + d_el * c["el"]
        dcum = c["qs"] * dqs - c["k"] * dkk + jnp.where(row == CHUNK - 1, d_last, 0.0)
        dla = _dot_exact_rhs(triu, dcum)
        dv = _dot_tn(p_scr[j].reshape(GLA_HEADS * CHUNK, CHUNK), do_blocks) + dv_state
        dp_ref[rows, O_GV:O_GV + GLA_HEADS * GLA_DV] = (dv * real).astype(BF16)
        dp_ref[rows, O_GQ:O_GQ + 256] = (dqs * (GLA_DK ** -0.5) * real).astype(BF16)
        dp_ref[rows, O_GK:O_GK + 256] = (dkk * real).astype(BF16)
        dz = dla * (1.0 / GLA_TAU) * _sigmoid(-c["z"]) * real
        ga = p_ref[rows, O_GA:O_GA + 128]
        dp_ref[rows, O_GA:O_GA + 128] = _dot_nt(dz, w2_ref[...]).astype(BF16)
        dp_ref[rows, O_GA + 128:IN_WP] = jnp.zeros((CHUNK, IN_WP - O_GA - 128), BF16)
        dw2_ref[...] += _dot_tn(ga, dz)
        dgb_ref[0:1, :] += jnp.sum(dz, axis=0, keepdims=True)

    const = lambda shape: pl.BlockSpec(shape, lambda s: (0,) * len(shape))
    rev = lambda s: (last_step - s, 0)
    anywhere = [pl.BlockSpec(memory_space=pl.ANY)] * n_carried
    return pl.pallas_call(
        body, name=name, grid=(N_STEPS,),
        in_specs=[pl.BlockSpec((STEP_ROWS, IN_WP), rev), pl.BlockSpec((STEP_ROWS, D), rev),
                  pl.BlockSpec((STEP_ROWS, D), rev),
                  pl.BlockSpec((1, RET_HEADS, RET_DK, 128), lambda s: (last_step - s, 0, 0, 0)),
                  pl.BlockSpec((CHUNKS_PER_STEP, GLA_HEADS, GLA_DK, GLA_DV), lambda s: (last_step - s, 0, 0, 0)),
                  pl.BlockSpec((STEP_ROWS, 128), rev), pl.BlockSpec((STEP_ROWS, 128), rev),
                  const((128, 256)), const((1, 256)), const((1, 512)), const((1, 512))] + anywhere,
        out_specs=[pl.BlockSpec((STEP_ROWS, IN_WP), rev), const((128, 256)), const((8, 256)),
                   const((8, 512)), const((8, 512))] + anywhere,
        out_shape=[jax.ShapeDtypeStruct((LP, IN_WP), BF16), jax.ShapeDtypeStruct((128, 256), F32),
                   jax.ShapeDtypeStruct((8, 256), F32), jax.ShapeDtypeStruct((8, 512), F32),
                   jax.ShapeDtypeStruct((8, 512), F32)] + [jax.ShapeDtypeStruct(g.shape, g.dtype) for g in carried],
        scratch_shapes=[pltpu.VMEM((RET_HEADS, RET_DK, 128), F32),
                        pltpu.VMEM((GLA_HEADS * GLA_DK, GLA_HEADS * GLA_DV), F32),
                        pltpu.VMEM((CHUNKS_PER_STEP, GLA_HEADS, CHUNK, CHUNK), F32),
                        pltpu.VMEM((CHUNKS_PER_STEP, CHUNK, 256), F32),
                        pltpu.VMEM((CHUNKS_PER_STEP, CHUNK, 256), F32)] + _exchange_sems(n_carried),
        compiler_params=_cparams("arbitrary"),
    )(proj, ocat, dmrg, sr_all, sg_all, cos2, sin2, w2p, gb, rnw, gnw, *carried)


def _all_gather(xs, name):
    n = len(xs)

    def body(*refs):
        start, forward, finish = _gather_phases(refs[:n], refs[n:2 * n], *refs[2 * n:])
        start()
        forward()
        finish()

    return pl.pallas_call(
        body, name=name,
        in_specs=[pl.BlockSpec(memory_space=pl.ANY)] * n,
        out_specs=[pl.BlockSpec(memory_space=pl.ANY)] * n,
        out_shape=_gathered_shapes(xs),
        scratch_shapes=_exchange_sems(n),
    )(*xs)


def _gathered_shapes(xs):
    return [jax.ShapeDtypeStruct((N_DEV,) + x.shape, x.dtype) for x in xs]


def _exchange_sems(n):
    if n == 0:
        return []
    return [pltpu.SemaphoreType.DMA((7 * n,)), pltpu.SemaphoreType.DMA((7 * n,)), pltpu.SemaphoreType.DMA((n,))]


def _gather_phases(x_refs, out_refs, send_sems, recv_sems, local_sems):
    n = len(x_refs)
    mx, my, mc = lax.axis_index("x"), lax.axis_index("y"), lax.axis_index("c")
    me, sibling = (mx, my, mc), (mx, my, 1 - mc)
    chips = [(1 - mx, my), (mx, 1 - my), (1 - mx, 1 - my)]

    def slot(a, px, py, pc):
        return out_refs[a].at[4 * px + 2 * py + pc]

    def copy(a, k, block, to, src=None):
        return pltpu.make_async_remote_copy(
            src_ref=slot(a, *block) if src is None else src, dst_ref=slot(a, *block),
            send_sem=send_sems.at[7 * a + k], recv_sem=recv_sems.at[7 * a + k],
            device_id=to, device_id_type=MESH_IDS)

    mine = [pltpu.make_async_copy(x_refs[a], slot(a, *me), local_sems.at[a]) for a in range(n)]
    first = []
    for a in range(n):
        first.append(copy(a, 0, me, sibling, src=x_refs[a]))
        first += [copy(a, 1 + j, me, (*chip, mc), src=x_refs[a]) for j, chip in enumerate(chips)]
    passed = [copy(a, 4 + j, (*chip, mc), sibling) for j, chip in enumerate(chips) for a in range(n)]

    def start():
        for cp in mine + first:
            cp.start()

    def forward():
        for j, chip in enumerate(chips):
            for a in range(n):
                copy(a, 1 + j, (*chip, mc), me).wait_recv()
                passed[j * n + a].start()

    def finish():
        for a in range(n):
            copy(a, 0, sibling, me).wait_recv()
            for j, chip in enumerate(chips):
                copy(a, 4 + j, (*chip, 1 - mc), me).wait_recv()
        for cp in first + passed:
            cp.wait_send()
        for cp in mine:
            cp.wait()

    return start, forward, finish


def _exchange_blocks(gs, name):
    n = len(gs)

    def body(*refs):
        start, finish = _exchange_phases(refs[:n], refs[n:2 * n], *refs[2 * n:])
        start()
        finish()

    return pl.pallas_call(
        body, name=name,
        in_specs=[pl.BlockSpec(memory_space=pl.ANY)] * n,
        out_specs=[pl.BlockSpec(memory_space=pl.ANY)] * n,
        out_shape=[jax.ShapeDtypeStruct(g.shape, g.dtype) for g in gs],
        scratch_shapes=_exchange_sems(n),
    )(*gs)


def _exchange_phases(g_refs, out_refs, send_sems, recv_sems, local_sems):
    n = len(g_refs)
    mx, my, mc = lax.axis_index("x"), lax.axis_index("y"), lax.axis_index("c")
    me = 4 * mx + 2 * my + mc
    mine = [pltpu.make_async_copy(g_refs[a].at[me], out_refs[a].at[me], local_sems.at[a]) for a in range(n)]
    copies = []
    for r in range(1, N_DEV):
        px, py, pc = mx ^ (r >> 2), my ^ ((r >> 1) & 1), mc ^ (r & 1)
        peer = 4 * px + 2 * py + pc
        for a in range(n):
            copies.append(pltpu.make_async_remote_copy(
                src_ref=g_refs[a].at[peer], dst_ref=out_refs[a].at[me],
                send_sem=send_sems.at[7 * a + r - 1], recv_sem=recv_sems.at[7 * a + r - 1],
                device_id=(px, py, pc), device_id_type=MESH_IDS))

    def start():
        for cp in mine + copies:
            cp.start()

    def finish():
        for cp in copies:
            cp.wait_recv()
        for cp in copies:
            cp.wait_send()
        for cp in mine:
            cp.wait()

    return start, finish


IN_SHARD = IN_W // N_DEV
IN_SHARD_P = 512
UP_SHARD = D_UP // N_DEV
UP_SHARD_P = 768
RELAYOUT_ROWS = 256


def _pieces_w_in():
    return [(k, 0, IN_SHARD * k, IN_SHARD) for k in range(N_DEV)]


def _pieces_ffn_up():
    pieces = []
    for k in range(N_DEV):
        n, end = UP_SHARD * k, UP_SHARD * (k + 1)
        while n < end:
            half, r = divmod(n, D_FF)
            blk, off = divmod(r, CONV_BLOCK)
            run = min(CONV_BLOCK - off, end - n)
            pieces.append((k, n - UP_SHARD * k, 2 * CONV_BLOCK * blk + CONV_BLOCK * half + off, run))
            n += run
    return pieces


def _assemble_block(load, spans, dst_block, rows):
    lo = 128 * dst_block
    lane = lax.broadcasted_iota(jnp.int32, (1, 128), 1)
    out = jnp.zeros((rows, 128), F32)
    for key, src_off, dst_off, length in spans:
        a, b = max(lo, dst_off), min(lo + 128, dst_off + length)
        s, s_end = src_off + (a - dst_off), src_off + (b - dst_off)
        d = a
        while s < s_end:
            e = min(s_end, 128 * (s // 128 + 1))
            blk = load(key, s // 128)
            shift = (d - s) % 128
            if shift:
                blk = pltpu.roll(blk, shift, 1)
            out = jnp.where((lane >= d - lo) & (lane < d - lo + (e - s)), blk, out)
            d += e - s
            s = e
    return out


def _shards_to_cols(shards, pieces, width, name):
    _, rows, _ = shards.shape
    tr = RELAYOUT_ROWS

    def body(s_ref, o_ref):
        load = lambda k, b: s_ref[k, :, 128 * b:128 * (b + 1)].astype(F32)
        for db in range(width // 128):
            o_ref[:, 128 * db:128 * (db + 1)] = _assemble_block(load, pieces, db, tr).astype(BF16)

    return pl.pallas_call(
        body, name=name, grid=(rows // tr,),
        in_specs=[pl.BlockSpec((N_DEV, tr, shards.shape[2]), lambda i: (0, i, 0))],
        out_specs=pl.BlockSpec((tr, width), lambda i: (i, 0)),
        out_shape=jax.ShapeDtypeStruct((rows, width), BF16),
        compiler_params=_cparams("parallel"),
    )(shards)


def _cols_to_shards(full, pieces, shard_width, name):
    rows, width = full.shape
    tr = RELAYOUT_ROWS

    def body(f_ref, o_ref):
        load = lambda _, b: f_ref[:, 128 * b:128 * (b + 1)].astype(F32)
        for k in range(N_DEV):
            spans = [(None, dst_off, src_off, length) for dev, src_off, dst_off, length in pieces if dev == k]
            for db in range(shard_width // 128):
                o_ref[k, :, 128 * db:128 * (db + 1)] = _assemble_block(load, spans, db, tr).astype(BF16)

    return pl.pallas_call(
        body, name=name, grid=(rows // tr,),
        in_specs=[pl.BlockSpec((tr, width), lambda i: (i, 0))],
        out_specs=pl.BlockSpec((N_DEV, tr, shard_width), lambda i: (0, i, 0)),
        out_shape=jax.ShapeDtypeStruct((N_DEV, rows, shard_width), BF16),
        compiler_params=_cparams("parallel"),
    )(full)


def _adamw(parts, w, m, v, rows_per_step, name):
    rows, cols = w.shape
    assert rows % rows_per_step == 0 and parts.shape == (N_DEV, rows, cols)

    def body(p_ref, w_ref, m_ref, v_ref, g_ref, d_ref, nm_ref, nv_ref):
        g = p_ref[0].astype(F32)
        for j in range(1, N_DEV):
            g = g + p_ref[j].astype(F32)
        m_new = ADAM_B1 * m_ref[...] + (1.0 - ADAM_B1) * g
        v_new = ADAM_B2 * v_ref[...] + (1.0 - ADAM_B2) * (g * g)
        m_hat = m_new / (1.0 - ADAM_B1 ** ADAM_STEP)
        v_hat = v_new / (1.0 - ADAM_B2 ** ADAM_STEP)
        g_ref[...] = g
        d_ref[...] = -ADAM_LR * (m_hat / (jnp.sqrt(v_hat) + ADAM_EPS) + ADAM_WD * w_ref[...])
        nm_ref[...] = m_new
        nv_ref[...] = v_new

    tile = pl.BlockSpec((rows_per_step, cols), lambda i: (i, 0))
    shape = jax.ShapeDtypeStruct((rows, cols), F32)
    return pl.pallas_call(
        body, name=name, grid=(rows // rows_per_step,),
        in_specs=[pl.BlockSpec((N_DEV, rows_per_step, cols), lambda i: (0, i, 0)), tile, tile, tile],
        out_specs=[tile, tile, tile, tile],
        out_shape=[shape, shape, shape, shape],
        compiler_params=_cparams("parallel"),
    )(parts, w, m, v)


BIG = (("w_in", (DEPTH, D, IN_W // N_DEV), 2), ("w_out", (DEPTH, D // N_DEV, D), 1),
       ("ffn_up", (DEPTH, D, D_UP // N_DEV), 2), ("ffn_down", (DEPTH, D_FF // N_DEV, D), 1))
SMALL = (("meta_tokens", (N_META, D // N_DEV), 1), ("gla_gate_w2", (DEPTH, GATE_RANK, 256 // N_DEV), 2),
         ("ffn_conv_w", (DEPTH, 3, D_UP // N_DEV), 2))
REPL = (("pre_mix_norm", (DEPTH, D)), ("gla_gate_b", (DEPTH, 256)), ("ret_norm_w", (DEPTH, 512)),
        ("gla_norm_w", (DEPTH, 512)), ("post_mix_norm", (DEPTH, D)), ("pre_ffn_norm", (DEPTH, D)),
        ("ffn_conv_b", (DEPTH, D_UP)), ("post_ffn_norm", (DEPTH, D)))
WEIGHT_ORDER = ("meta_tokens", "pre_mix_norm", "w_in", "gla_gate_w2", "gla_gate_b", "ret_norm_w", "gla_norm_w",
                "w_out", "post_mix_norm", "pre_ffn_norm", "ffn_up", "ffn_conv_w", "ffn_conv_b", "ffn_down",
                "post_ffn_norm")


def _size(shape):
    return math.prod(shape)


def _round_up(n, mult):
    return -(-n // mult) * mult


REPL_ROWS = _round_up(-(-sum(_size(s) for _, s in REPL) // LANES), 8)
SMALL_ROWS = _round_up(-(-sum(_size(s) for _, s, _ in SMALL) // LANES), 8)


def _pack(arrays, rows, dtype):
    flat = jnp.concatenate([a.reshape(-1).astype(dtype) for a in arrays])
    return jnp.pad(flat, (0, rows * LANES - flat.shape[0])).reshape(rows, LANES)


def _unpack(buf, shapes):
    flat = buf.reshape(-1)
    out, off = [], 0
    for shape in shapes:
        out.append(flat[off:off + _size(shape)].reshape(shape))
        off += _size(shape)
    return out


def _unshard(blocks, axis):
    moved = jnp.moveaxis(blocks, 0, axis)
    shape = list(moved.shape)
    shape[axis:axis + 2] = [shape[axis] * shape[axis + 1]]
    return moved.reshape(shape)


def _to_blocks(full, axis):
    shape = list(full.shape)
    shape[axis:axis + 1] = [N_DEV, shape[axis] // N_DEV]
    return jnp.moveaxis(full.reshape(shape), axis, 0)


def _interleave_cols(w):
    lead = w.shape[:-1]
    return jnp.swapaxes(w.reshape(lead + (2, N_CONV_BLOCKS, CONV_BLOCK)), -3, -2).reshape(lead + (D_UP,))


def _deinterleave_cols(w):
    lead = w.shape[:-1]
    return jnp.swapaxes(w.reshape(lead + (N_CONV_BLOCKS, 2, CONV_BLOCK)), -3, -2).reshape(lead + (D_UP,))


def _rope_tables():
    half = RET_DK // 2
    inv = ROPE_BASE ** (-jnp.arange(half, dtype=F32) / half)
    pos = jnp.arange(LP, dtype=F32) - float(PAD_ROWS)
    ang = pos[:, None] * inv[None, :]
    c, s = jnp.cos(ang), jnp.sin(ang)
    return jnp.concatenate([c, c], axis=1), jnp.concatenate([-s, s], axis=1)


def kernel(x, meta_tokens, pre_mix_norm, w_in, gla_gate_w2, gla_gate_b, ret_norm_w, gla_norm_w, w_out, post_mix_norm, pre_ffn_norm, ffn_up, ffn_conv_w, ffn_conv_b, ffn_down, post_ffn_norm, loss_target, m_meta_tokens, m_pre_mix_norm, m_w_in, m_gla_gate_w2, m_gla_gate_b, m_ret_norm_w, m_gla_norm_w, m_w_out, m_post_mix_norm, m_pre_ffn_norm, m_ffn_up, m_ffn_conv_w, m_ffn_conv_b, m_ffn_down, m_post_ffn_norm, v_meta_tokens, v_pre_mix_norm, v_w_in, v_gla_gate_w2, v_gla_gate_b, v_ret_norm_w, v_gla_norm_w, v_w_out, v_post_mix_norm, v_pre_ffn_norm, v_ffn_up, v_ffn_conv_w, v_ffn_conv_b, v_ffn_down, v_post_ffn_norm):
    weights = dict(meta_tokens=meta_tokens, pre_mix_norm=pre_mix_norm, w_in=w_in, gla_gate_w2=gla_gate_w2,
                   gla_gate_b=gla_gate_b, ret_norm_w=ret_norm_w, gla_norm_w=gla_norm_w, w_out=w_out,
                   post_mix_norm=post_mix_norm, pre_ffn_norm=pre_ffn_norm, ffn_up=ffn_up, ffn_conv_w=ffn_conv_w,
                   ffn_conv_b=ffn_conv_b, ffn_down=ffn_down, post_ffn_norm=post_ffn_norm)
    mom1 = dict(meta_tokens=m_meta_tokens, pre_mix_norm=m_pre_mix_norm, w_in=m_w_in, gla_gate_w2=m_gla_gate_w2,
                gla_gate_b=m_gla_gate_b, ret_norm_w=m_ret_norm_w, gla_norm_w=m_gla_norm_w, w_out=m_w_out,
                post_mix_norm=m_post_mix_norm, pre_ffn_norm=m_pre_ffn_norm, ffn_up=m_ffn_up,
                ffn_conv_w=m_ffn_conv_w, ffn_conv_b=m_ffn_conv_b, ffn_down=m_ffn_down, post_ffn_norm=m_post_ffn_norm)
    mom2 = dict(meta_tokens=v_meta_tokens, pre_mix_norm=v_pre_mix_norm, w_in=v_w_in, gla_gate_w2=v_gla_gate_w2,
                gla_gate_b=v_gla_gate_b, ret_norm_w=v_ret_norm_w, gla_norm_w=v_gla_norm_w, w_out=v_w_out,
                post_mix_norm=v_post_mix_norm, pre_ffn_norm=v_pre_ffn_norm, ffn_up=v_ffn_up,
                ffn_conv_w=v_ffn_conv_w, ffn_conv_b=v_ffn_conv_b, ffn_down=v_ffn_down, post_ffn_norm=v_post_ffn_norm)

    pad_cols = lambda a, width: jnp.pad(a, ((0, 0), (0, width - a.shape[1])))
    big_names = [n for n, _, _ in BIG]
    shard = {}
    for l in range(DEPTH):
        shard[l, "w_in"] = pad_cols(w_in[l].astype(BF16), IN_SHARD_P)
        shard[l, "w_out"] = w_out[l].astype(BF16)
        shard[l, "ffn_up"] = pad_cols(ffn_up[l].astype(BF16), UP_SHARD_P)
        shard[l, "ffn_down"] = ffn_down[l].astype(BF16)
    w_in_0, small = _all_gather([shard[0, "w_in"], _pack([weights[n] for n, _, _ in SMALL], SMALL_ROWS, F32)],
                                "gather_first_weights")
    gathered = {(0, "w_in"): w_in_0}
    gather_in_mixer = {l: [(l, n) for n in big_names[1:]] for l in range(DEPTH)}
    gather_in_conv = {l: [(l + 1, "w_in")] for l in range(DEPTH - 1)}
    small_parts = _unpack_blocks(small, [s for _, s, _ in SMALL])
    full = {n: _unshard(p, ax) for (n, _, ax), p in zip(SMALL, small_parts)}
    w2p = jnp.pad(full["gla_gate_w2"], ((0, 0), (0, 128 - GATE_RANK), (0, 0)))
    cw8 = jnp.concatenate([_interleave_cols(full["ffn_conv_w"]), _interleave_cols(ffn_conv_b)[:, None, :],
                           jnp.zeros((DEPTH, 4, D_UP), F32)], axis=1)
    cos2, sin2 = _rope_tables()

    h = jnp.concatenate([jnp.zeros((PAD_ROWS, D), F32), full["meta_tokens"], x[0]], axis=0)
    target = jnp.concatenate([jnp.zeros((CHUNK, D), F32), loss_target[0]], axis=0)
    saved, layer_w = [], []
    for l in range(DEPTH):
        lw = dict(w_in=_shards_to_cols(gathered[l, "w_in"], _pieces_w_in(), IN_WP, f"w_in_cols_{l}"))
        a1, proj = _norm_matmul(h, pre_mix_norm[l:l + 1], lw["w_in"], out_dtype=F32, tm=TM_BIG, tn=IN_WP // 3,
                                name=f"in_proj_{l}")
        keys = gather_in_mixer.get(l, [])
        ocat, merged, sr_all, sg_all, *got = _mixer_fwd(proj, cos2, sin2, w2p[l], gla_gate_b[l:l + 1],
                                                        ret_norm_w[l:l + 1], gla_norm_w[l:l + 1], f"mixer_fwd_{l}",
                                                        carried=[shard[key] for key in keys])
        gathered.update(zip(keys, got))
        lw["w_out"] = gathered[l, "w_out"].reshape(D, D)
        lw["w_up"] = _shards_to_cols(gathered[l, "ffn_up"], _pieces_ffn_up(), D_UP, f"ffn_up_cols_{l}")
        lw["w_down"] = gathered[l, "ffn_down"].reshape(D_FF, D)
        layer_w.append(lw)
        m, h1 = _matmul_resid_norm(merged, lw["w_out"], h, post_mix_norm[l:l + 1], f"out_proj_{l}")
        a2, u = _norm_matmul(h1, pre_ffn_norm[l:l + 1], lw["w_up"], out_dtype=BF16, tm=TM_BIG, tn=D_UP // 4,
                             name=f"ffn_up_{l}")
        keys = gather_in_conv.get(l, [])
        cv, act, *got = _conv_act_fwd(u, cw8[l], f"ffn_conv_act_{l}", carried=[shard[key] for key in keys])
        gathered.update(zip(keys, got))
        f, h2, *loss_acc = _matmul_resid_norm(act, lw["w_down"], h1, post_ffn_norm[l:l + 1], f"ffn_down_{l}",
                                              target=target if l == DEPTH - 1 else None)
        saved.append(dict(h=h, a1=a1, proj=proj, ocat=ocat, merged=merged, sr=sr_all, sg=sg_all, m=m, h1=h1,
                          a2=a2, u=u, cv=cv, act=act, f=f))
        h = h2

    dh = h
    loss = lax.psum(loss_acc[0][0, 0], ("x", "y", "c"))

    kinds = ("grad", "delta", "new_m", "new_v")
    grads = {n: [None] * DEPTH for n in WEIGHT_ORDER if n != "meta_tokens" and n not in big_names}
    pending, parts = [], {}
    for l in reversed(range(DEPTH)):
        s, lw = saved[l], layer_w[l]
        dact, df, g_post_ffn = _norm_bwd_matmul(dh, s["f"], post_ffn_norm[l:l + 1], lw["w_down"], BF16,
                                                f"ffn_down_dx_{l}")
        g_down = _matmul(s["act"], df, ta=True, out_dtype=BF16, tm=D_FF // 2, tn=D, tk=TK_LONG, name=f"ffn_down_dw_{l}")
        du, dcw = _conv_act_bwd(dact, s["cv"], s["u"], cw8[l], f"ffn_conv_act_bwd_{l}")
        dh1, g_pre_ffn = _matmul_norm_bwd(du, lw["w_up"], s["h1"], pre_ffn_norm[l:l + 1], dh, D_FF, f"ffn_up_dx_{l}")
        g_up = _matmul(s["a2"], du, ta=True, out_dtype=BF16, tm=D, tn=D_FF, tk=TM_BIG, name=f"ffn_up_dw_{l}")
        dmerged, dm, g_post_mix = _norm_bwd_matmul(dh1, s["m"], post_mix_norm[l:l + 1], lw["w_out"], F32,
                                                   f"out_proj_dx_{l}")
        g_out = _matmul(s["merged"], dm, ta=True, out_dtype=BF16, tm=D, tn=D, tk=TK_LONG, name=f"out_proj_dw_{l}")
        pending += [((l, "ffn_down"), g_down.reshape(N_DEV, D_FF // N_DEV, D)),
                    ((l, "ffn_up"), _cols_to_shards(g_up, _pieces_ffn_up(), UP_SHARD_P, f"ffn_up_grad_shards_{l}")),
                    ((l, "w_out"), g_out.reshape(N_DEV, D // N_DEV, D))]
        dproj, g_w2, g_gb, g_rn, g_gn, *got = _mixer_bwd(s["proj"], s["ocat"], dmerged, s["sr"], s["sg"], cos2, sin2,
                                                         w2p[l], gla_gate_b[l:l + 1], ret_norm_w[l:l + 1],
                                                         gla_norm_w[l:l + 1], f"mixer_bwd_{l}",
                                                         carried=[blocks for _, blocks in pending])
        parts.update(zip([key for key, _ in pending], got))
        g_in = _matmul(s["a1"], dproj, ta=True, out_dtype=BF16, tm=D, tn=IN_WP // 2, tk=TM_BIG, name=f"in_proj_dw_{l}")
        pending = [((l, "w_in"), _cols_to_shards(g_in, _pieces_w_in(), IN_SHARD_P, f"w_in_grad_shards_{l}"))]
        now = pending if l == 0 else []
        dh, g_pre_mix, *got = _matmul_norm_bwd(dproj, lw["w_in"], s["h"], pre_mix_norm[l:l + 1], dh1, IN_WP,
                                               f"in_proj_dx_{l}", carried=[blocks for _, blocks in now])
        parts.update(zip([key for key, _ in now], got))
        pending = [] if l == 0 else pending
        grads["post_ffn_norm"][l] = g_post_ffn[0]
        grads["ffn_conv_w"][l] = _deinterleave_cols(dcw[0:3])
        grads["ffn_conv_b"][l] = _deinterleave_cols(dcw[3])
        grads["pre_ffn_norm"][l] = g_pre_ffn[0]
        grads["post_mix_norm"][l] = g_post_mix[0]
        grads["gla_gate_w2"][l] = g_w2[:GATE_RANK]
        grads["gla_gate_b"][l] = g_gb[0]
        grads["ret_norm_w"][l] = g_rn[0]
        grads["gla_norm_w"][l] = g_gn[0]
        grads["pre_mix_norm"][l] = g_pre_mix[0]
    local = {n: jnp.stack(v) for n, v in grads.items()}
    local["meta_tokens"] = dh[PAD_ROWS:CHUNK]
    grad_x = dh[CHUNK:][None]

    blocks = jnp.concatenate([_to_blocks(local[n], ax).reshape(N_DEV, -1) for n, _, ax in SMALL], axis=1)
    blocks = jnp.pad(blocks, ((0, 0), (0, SMALL_ROWS * LANES - blocks.shape[1]))).reshape(N_DEV, SMALL_ROWS, LANES)
    *got, small_grad_parts = _exchange_blocks([b for _, b in pending] + [blocks], "exchange_last_grads")
    parts.update(zip([key for key, _ in pending], got))

    widths = dict(w_in=IN_SHARD_P, w_out=D, ffn_up=UP_SHARD_P, ffn_down=D)
    steps = dict(w_in=256, w_out=D // N_DEV, ffn_up=256, ffn_down=D_FF // N_DEV // 2)
    big_out = {kind: {n: [None] * DEPTH for n in big_names} for kind in kinds}
    for l in range(DEPTH):
        for n in big_names:
            mine = [pad_cols(d[n][l], widths[n]) for d in (weights, mom1, mom2)]
            results = _adamw(parts[l, n], *mine, steps[n], f"adamw_{n}_{l}")
            for kind, r in zip(kinds, results):
                big_out[kind][n][l] = r[:, :weights[n].shape[2]]
    out = {kind: {n: jnp.stack(v) for n, v in big_out[kind].items()} for kind in kinds}
    shard_shapes = [s for _, s, _ in SMALL]
    packed = [_pack([d[n] for n, _, _ in SMALL], SMALL_ROWS, F32) for d in (weights, mom1, mom2)]
    results = _adamw(small_grad_parts, *packed, SMALL_ROWS, "adamw_small_sharded")
    for kind, buf in zip(kinds, results):
        out[kind].update(zip([n for n, _, _ in SMALL], _unpack(buf, shard_shapes)))

    repl_parts = _all_gather([_pack([local[n] for n, _ in REPL], REPL_ROWS, F32)], "gather_small_grads")[0]
    packed = [_pack([d[n] for n, _ in REPL], REPL_ROWS, F32) for d in (weights, mom1, mom2)]
    results = _adamw(repl_parts, *packed, REPL_ROWS, "adamw_replicated")
    repl_shapes = [s for _, s in REPL]
    for kind, buf in zip(kinds, results):
        out[kind].update(zip([n for n, _ in REPL], _unpack(buf, repl_shapes)))

    return (loss, grad_x, *[out["grad"][n] for n in WEIGHT_ORDER], *[out["delta"][n] for n in WEIGHT_ORDER],
            *[out["new_m"][n] for n in WEIGHT_ORDER], *[out["new_v"][n] for n in WEIGHT_ORDER])


def _unpack_blocks(gathered, shapes):
    flat = gathered.reshape(N_DEV, -1)
    out, off = [], 0
    for shape in shapes:
        out.append(flat[:, off:off + _size(shape)].reshape((N_DEV,) + shape))
        off += _size(shape)
    return out
```

```python
import math

import jax
import jax.numpy as jnp
from jax import lax
from jax.experimental import pallas as pl
from jax.experimental.pallas import tpu as pltpu

F32 = jnp.float32
BF16 = jnp.bfloat16

D = 1024
SEQ = 8192
DEPTH = 2
N_META = 16
CHUNK = 64
SUB = 16
N_SUB = CHUNK // SUB
PAD_ROWS = CHUNK - N_META
LP = SEQ + CHUNK
N_CHUNKS = LP // CHUNK
RET_HEADS = 4
RET_DK = 128
GLA_HEADS = 4
GLA_DK = 64
GLA_DV = 128
GLA_TAU = 16.0
GATE_RANK = 16
IN_W = 3600
IN_WP = 3840
D_FF = 2816
D_UP = 2 * D_FF
CONV_BLOCK = 256
N_CONV_BLOCKS = D_FF // CONV_BLOCK
ROPE_BASE = 10000.0
EPS = 1e-6
N_DEV = 8
LANES = 1024

O_RQ, O_RK, O_RV, O_RG = 0, 512, 1024, 1536
O_GQ, O_GK, O_GV, O_GR, O_GA = 2048, 2304, 2560, 3072, 3584

ADAM_LR = 0.001
ADAM_B1 = 0.9
ADAM_B2 = 0.999
ADAM_EPS = 1e-08
ADAM_WD = 0.01
ADAM_STEP = 10

VMEM_LIMIT = 56 * 1024 * 1024
MESH_IDS = pl.DeviceIdType.MESH


def _row_tile(rows, limit):
    best = 16
    for t in range(16, min(rows, limit) + 1, 16):
        if rows % t == 0:
            best = t
    return best


TM = _row_tile(LP, 688)
TM_BIG = _row_tile(LP, 1376)
TK_LONG = _row_tile(LP, 2752)


def _cparams(*sem):
    return pltpu.CompilerParams(dimension_semantics=sem, vmem_limit_bytes=VMEM_LIMIT)


def _dot(a, b):
    return jnp.dot(a.astype(BF16), b.astype(BF16), preferred_element_type=F32)


def _dot_nt(a, b):
    return lax.dot_general(a.astype(BF16), b.astype(BF16), (((1,), (1,)), ((), ())), preferred_element_type=F32)


def _dot_tn(a, b):
    return lax.dot_general(a.astype(BF16), b.astype(BF16), (((0,), (0,)), ((), ())), preferred_element_type=F32)


def _split3(x):
    hi = x.astype(BF16)
    r1 = x - hi.astype(F32)
    mid = r1.astype(BF16)
    lo = (r1 - mid.astype(F32)).astype(BF16)
    return hi, mid, lo


def _dot_exact_rhs(t, x):
    n = x.shape[1]
    parts = jnp.dot(t.astype(BF16), jnp.concatenate(_split3(x), axis=1), preferred_element_type=F32)
    return parts[:, :n] + parts[:, n:2 * n] + parts[:, 2 * n:]


def _dot_tn_exact_lhs(x, ones):
    n = x.shape[1]
    parts = lax.dot_general(jnp.concatenate(_split3(x), axis=1), ones.astype(BF16), (((0,), (0,)), ((), ())),
                            preferred_element_type=F32)
    return parts[:n] + parts[n:2 * n] + parts[2 * n:]


def _sigmoid(x):
    return 1.0 / (1.0 + jnp.exp(-x))


def _matmul(a, b, *, ta=False, tb=False, out_dtype, tm, tn, tk, name):
    m = a.shape[1] if ta else a.shape[0]
    k = a.shape[0] if ta else a.shape[1]
    n = b.shape[0] if tb else b.shape[1]
    assert (b.shape[1] if tb else b.shape[0]) == k
    assert m % tm == 0 and n % tn == 0 and k % tk == 0, (name, m, n, k, tm, tn, tk)
    nk = k // tk
    a_spec = pl.BlockSpec((tk, tm), lambda i, j, kk: (kk, i)) if ta else pl.BlockSpec((tm, tk), lambda i, j, kk: (i, kk))
    b_spec = pl.BlockSpec((tn, tk), lambda i, j, kk: (j, kk)) if tb else pl.BlockSpec((tk, tn), lambda i, j, kk: (kk, j))
    dims = (((0 if ta else 1,), (1 if tb else 0,)), ((), ()))

    def body(a_ref, b_ref, o_ref, *acc):
        prod = lax.dot_general(a_ref[...].astype(BF16), b_ref[...].astype(BF16), dims, preferred_element_type=F32)
        if nk == 1:
            o_ref[...] = prod.astype(out_dtype)
            return
        acc_ref, = acc
        kk = pl.program_id(2)

        @pl.when(kk == 0)
        def _():
            acc_ref[...] = prod

        @pl.when(kk > 0)
        def _():
            acc_ref[...] += prod

        @pl.when(kk == nk - 1)
        def _():
            o_ref[...] = acc_ref[...].astype(out_dtype)

    return pl.pallas_call(
        body, name=name, grid=(m // tm, n // tn, nk),
        in_specs=[a_spec, b_spec],
        out_specs=pl.BlockSpec((tm, tn), lambda i, j, kk: (i, j)),
        out_shape=jax.ShapeDtypeStruct((m, n), out_dtype),
        scratch_shapes=[pltpu.VMEM((tm, tn), F32)] if nk > 1 else [],
        compiler_params=_cparams("parallel", "parallel", "arbitrary"),
    )(a, b)


def _norm_matmul(x, w, b, *, out_dtype, tm, tn, name):
    n = b.shape[1]
    assert LP % tm == 0 and n % tn == 0

    def body(x_ref, w_ref, b_ref, a_ref, o_ref, a_scr):
        @pl.when(pl.program_id(1) == 0)
        def _():
            xv = x_ref[...]
            r = lax.rsqrt(jnp.mean(xv * xv, axis=-1, keepdims=True) + EPS)
            a = (xv * r * w_ref[...]).astype(BF16)
            a_scr[...] = a
            a_ref[...] = a

        o_ref[...] = jnp.dot(a_scr[...], b_ref[...], preferred_element_type=F32).astype(out_dtype)

    return pl.pallas_call(
        body, name=name, grid=(LP // tm, n // tn),
        in_specs=[pl.BlockSpec((tm, D), lambda i, j: (i, 0)), pl.BlockSpec((1, D), lambda i, j: (0, 0)),
                  pl.BlockSpec((D, tn), lambda i, j: (0, j))],
        out_specs=[pl.BlockSpec((tm, D), lambda i, j: (i, 0)), pl.BlockSpec((tm, tn), lambda i, j: (i, j))],
        out_shape=[jax.ShapeDtypeStruct((LP, D), BF16), jax.ShapeDtypeStruct((LP, n), out_dtype)],
        scratch_shapes=[pltpu.VMEM((tm, D), BF16)],
        compiler_params=_cparams("arbitrary", "arbitrary"),
    )(x, w, b)


def _matmul_resid_norm(a, b, h, w, name, target=None):
    k = a.shape[1]
    has_loss = target is not None

    def body(a_ref, b_ref, h_ref, w_ref, *refs):
        m = jnp.dot(a_ref[...].astype(BF16), b_ref[...].astype(BF16), preferred_element_type=F32)
        r = lax.rsqrt(jnp.mean(m * m, axis=-1, keepdims=True) + EPS)
        i = pl.program_id(0)
        row = i * TM + lax.broadcasted_iota(jnp.int32, (TM, 1), 0)
        y = h_ref[...] + jnp.where(row >= PAD_ROWS, m * r * w_ref[...], 0.0)
        if not has_loss:
            m_ref, y_ref = refs
            m_ref[...] = m
            y_ref[...] = y
            return
        t_ref, m_ref, dy_ref, loss_ref = refs
        m_ref[...] = m

        @pl.when(i == 0)
        def _():
            loss_ref[...] = jnp.zeros_like(loss_ref)

        diff = jnp.where(row >= CHUNK, y - t_ref[...], 0.0)
        dy_ref[...] = diff * (1.0 / D)
        loss_ref[...] += (0.5 / D) * jnp.sum(diff * diff)

    tile = pl.BlockSpec((TM, D), lambda i: (i, 0))
    shape = jax.ShapeDtypeStruct((LP, D), F32)
    in_specs = [pl.BlockSpec((TM, k), lambda i: (i, 0)), pl.BlockSpec((k, D), lambda i: (0, 0)), tile,
                pl.BlockSpec((1, D), lambda i: (0, 0))]
    if has_loss:
        return pl.pallas_call(
            body, name=name, grid=(LP // TM,),
            in_specs=in_specs + [tile],
            out_specs=[tile, tile, pl.BlockSpec((8, 128), lambda i: (0, 0))],
            out_shape=[shape, shape, jax.ShapeDtypeStruct((8, 128), F32)],
            compiler_params=_cparams("arbitrary"),
        )(a, b, h, w, target)
    return pl.pallas_call(
        body, name=name, grid=(LP // TM,),
        in_specs=in_specs, out_specs=[tile, tile], out_shape=[shape, shape],
        compiler_params=_cparams("parallel"),
    )(a, b, h, w)


def _rmsnorm_bwd_rows(dy, x, w):
    r = lax.rsqrt(jnp.mean(x * x, axis=-1, keepdims=True) + EPS)
    g = dy * w
    dx = r * g - x * (r * r * r * jnp.mean(g * x, axis=-1, keepdims=True))
    return dx, jnp.sum(dy * x * r, axis=0, keepdims=True)


def _matmul_norm_bwd(dz, b, x, w, resid, tk, name, carried=()):
    k = dz.shape[1]
    assert k % tk == 0
    nk = k // tk
    n_rows = LP // TM
    n_carried = len(carried)

    def body(*refs):
        a_ref, b_ref, x_ref, w_ref, r_ref = refs[:5]
        g_refs, refs = refs[5:5 + n_carried], refs[5 + n_carried:]
        dx_ref, dw_ref = refs[:2]
        got_refs, refs = refs[2:2 + n_carried], refs[2 + n_carried:]
        acc, sems = (refs[:1], refs[1:]) if nk > 1 else ((), refs)
        i, kk = pl.program_id(0), pl.program_id(1)
        if n_carried:
            exchange_start, exchange_finish = _exchange_phases(g_refs, got_refs, *sems)
            pl.when((i == 0) & (kk == 0))(exchange_start)

        @pl.when((i == 0) & (kk == 0))
        def _():
            dw_ref[...] = jnp.zeros_like(dw_ref)

        prod = lax.dot_general(a_ref[...].astype(BF16), b_ref[...].astype(BF16), (((1,), (1,)), ((), ())),
                               preferred_element_type=F32)

        def finish(dy):
            dx, dw = _rmsnorm_bwd_rows(dy, x_ref[...], w_ref[...])
            dx_ref[...] = dx + r_ref[...]
            dw_ref[0:1, :] += dw

        if nk == 1:
            finish(prod)
        else:
            acc_ref, = acc

            @pl.when(kk == 0)
            def _():
                acc_ref[...] = prod

            @pl.when((kk > 0) & (kk < nk - 1))
            def _():
                acc_ref[...] += prod

            @pl.when(kk == nk - 1)
            def _():
                finish(acc_ref[...] + prod)

        if n_carried:
            pl.when((i == n_rows - 1) & (kk == nk - 1))(exchange_finish)

    tile = pl.BlockSpec((TM, D), lambda i, kk: (i, 0))
    anywhere = [pl.BlockSpec(memory_space=pl.ANY)] * n_carried
    return pl.pallas_call(
        body, name=name, grid=(n_rows, nk),
        in_specs=[pl.BlockSpec((TM, tk), lambda i, kk: (i, kk)), pl.BlockSpec((D, tk), lambda i, kk: (0, kk)), tile,
                  pl.BlockSpec((1, D), lambda i, kk: (0, 0)), tile] + anywhere,
        out_specs=[tile, pl.BlockSpec((8, D), lambda i, kk: (0, 0))] + anywhere,
        out_shape=[jax.ShapeDtypeStruct((LP, D), F32), jax.ShapeDtypeStruct((8, D), F32)]
        + [jax.ShapeDtypeStruct(g.shape, g.dtype) for g in carried],
        scratch_shapes=([pltpu.VMEM((TM, D), F32)] if nk > 1 else []) + _exchange_sems(n_carried),
        compiler_params=_cparams("arbitrary", "arbitrary"),
    )(dz, b, x, w, resid, *carried)


def _norm_bwd_matmul(dh, x, w, b, out_dtype, name):
    n = b.shape[0]

    def body(dh_ref, x_ref, w_ref, b_ref, o_ref, dx_ref, dw_ref):
        i = pl.program_id(0)

        @pl.when(i == 0)
        def _():
            dw_ref[...] = jnp.zeros_like(dw_ref)

        row = i * TM + lax.broadcasted_iota(jnp.int32, (TM, 1), 0)
        dy = jnp.where(row >= PAD_ROWS, dh_ref[...], 0.0)
        dx, dw = _rmsnorm_bwd_rows(dy, x_ref[...], w_ref[...])
        dxb = dx.astype(BF16)
        dx_ref[...] = dxb
        dw_ref[0:1, :] += dw
        o_ref[...] = lax.dot_general(dxb, b_ref[...].astype(BF16), (((1,), (1,)), ((), ())),
                                     preferred_element_type=F32).astype(out_dtype)

    tile = pl.BlockSpec((TM, D), lambda i: (i, 0))
    return pl.pallas_call(
        body, name=name, grid=(LP // TM,),
        in_specs=[tile, tile, pl.BlockSpec((1, D), lambda i: (0, 0)), pl.BlockSpec((n, D), lambda i: (0, 0))],
        out_specs=[pl.BlockSpec((TM, n), lambda i: (i, 0)), tile, pl.BlockSpec((8, D), lambda i: (0, 0))],
        out_shape=[jax.ShapeDtypeStruct((LP, n), out_dtype), jax.ShapeDtypeStruct((LP, D), BF16),
                   jax.ShapeDtypeStruct((8, D), F32)],
        compiler_params=_cparams("arbitrary"),
    )(dh, x, w, b)


GELU_C = math.sqrt(2.0 / math.pi)
GELU_K = 0.044715
STRIP = 16
HALF = 8


def _gelu_half(a):
    return 0.5 * jnp.tanh(a * (a * a * (GELU_C * GELU_K) + GELU_C)) + 0.5


def _gelu_slope(a, h):
    return h * (1.0 + (a - a * h) * (a * a * (6.0 * GELU_C * GELU_K) + 2.0 * GELU_C))


def _shift_down(x, prev8):
    row = lax.broadcasted_iota(jnp.int32, (8, 1), 0)
    r1, r2 = pltpu.roll(x, 1, 0), pltpu.roll(x, 2, 0)
    top1 = jnp.where(row < 1, pltpu.roll(prev8, 1, 0), r1[0:8, :])
    top2 = jnp.where(row < 2, pltpu.roll(prev8, 2, 0), r2[0:8, :])
    return jnp.concatenate([top1, r1[8:, :]], axis=0), jnp.concatenate([top2, r2[8:, :]], axis=0)


def _conv_act_fwd(u, cw8, name, carried=()):
    n_rows = LP // TM
    cb2 = 2 * CONV_BLOCK
    n_carried = len(carried)

    def body(*refs):
        u_ref, cw_ref = refs[:2]
        x_refs, refs = refs[2:2 + n_carried], refs[2 + n_carried:]
        conv_ref, act_ref = refs[:2]
        gathered_refs, refs = refs[2:2 + n_carried], refs[2 + n_carried:]
        carry_ref = refs[0]
        j, i = pl.program_id(0), pl.program_id(1)
        if n_carried:
            start, forward, finish = _gather_phases(x_refs, gathered_refs, *refs[1:])
            pl.when((j == 0) & (i == 0))(start)
            pl.when((j == (3 * N_CONV_BLOCKS) // 4) & (i == 0))(forward)

        @pl.when(i == 0)
        def _():
            carry_ref[...] = jnp.zeros_like(carry_ref)

        x = u_ref[...].astype(F32)
        x1, x2 = _shift_down(x, carry_ref[...])
        conv = cw_ref[3:4, :] + x2 * cw_ref[0:1, :] + x1 * cw_ref[1:2, :] + x * cw_ref[2:3, :]
        conv_ref[...] = conv.astype(BF16)
        a = conv[:, :CONV_BLOCK]
        g = conv[:, CONV_BLOCK:]
        act_ref[...] = (a * _gelu_half(a) * g).astype(BF16)
        carry_ref[...] = x[TM - 8:TM, :]
        if n_carried:
            pl.when((j == N_CONV_BLOCKS - 1) & (i == n_rows - 1))(finish)

    anywhere = [pl.BlockSpec(memory_space=pl.ANY)] * n_carried
    return pl.pallas_call(
        body, name=name, grid=(N_CONV_BLOCKS, n_rows),
        in_specs=[pl.BlockSpec((TM, cb2), lambda j, i: (i, j)), pl.BlockSpec((8, cb2), lambda j, i: (0, j))] + anywhere,
        out_specs=[pl.BlockSpec((TM, cb2), lambda j, i: (i, j)),
                   pl.BlockSpec((TM, CONV_BLOCK), lambda j, i: (i, j))] + anywhere,
        out_shape=[jax.ShapeDtypeStruct((LP, D_UP), BF16), jax.ShapeDtypeStruct((LP, D_FF), BF16)]
        + _gathered_shapes(carried),
        scratch_shapes=[pltpu.VMEM((8, cb2), F32)] + _exchange_sems(n_carried),
        compiler_params=_cparams("arbitrary", "arbitrary"),
    )(u, cw8, *carried)


def _conv_act_bwd(dact, conv, u, cw8, name):
    n_rows = LP // TM
    cb2 = 2 * CONV_BLOCK
    n_strips = TM // STRIP

    def body(dact_ref, conv_ref, u_ref, cw_ref, du_ref, dcw_ref, carry_ref):
        i = pl.program_id(1)

        @pl.when(i == 0)
        def _():
            dcw_ref[...] = jnp.zeros_like(dcw_ref)
            carry_ref[...] = jnp.zeros_like(carry_ref)

        w0, w1, w2 = cw_ref[0:1, :], cw_ref[1:2, :], cw_ref[2:3, :]
        row = lax.broadcasted_iota(jnp.int32, (HALF, 1), 0)

        def strip(k, carry):
            n1, n2, s0, s1, s2, s3 = carry
            r0 = pl.multiple_of((n_strips - 1 - k) * STRIP, STRIP)
            cv = conv_ref[pl.ds(r0, STRIP), :].astype(F32)
            dav = dact_ref[pl.ds(r0, STRIP), :].astype(F32)
            x = u_ref[pl.ds(r0, STRIP), :].astype(F32)
            du = [None, None]
            for half in (1, 0):
                rows = slice(HALF * half, HALF * (half + 1))
                a, g, dah = cv[rows, :CONV_BLOCK], cv[rows, CONV_BLOCK:], dav[rows]
                h = _gelu_half(a)
                dconv = jnp.concatenate([dah * g * _gelu_slope(a, h), dah * (a * h)], axis=1)
                u1, u2 = pltpu.roll(dconv, HALF - 1, 0), pltpu.roll(dconv, HALF - 2, 0)
                d1 = jnp.where(row >= HALF - 1, n1, u1)
                d2 = jnp.where(row >= HALF - 2, n2, u2)
                du[half] = dconv * w2 + d1 * w1 + d2 * w0
                s0, s1, s2, s3 = s0 + d2 * x[rows], s1 + d1 * x[rows], s2 + dconv * x[rows], s3 + dconv
                n1, n2 = u1, u2
            du_ref[pl.ds(r0, STRIP), :] = jnp.concatenate(du, axis=0).astype(BF16)
            return n1, n2, s0, s1, s2, s3

        below = carry_ref[...]
        zero = jnp.zeros((HALF, cb2), F32)
        init = (pltpu.roll(below, HALF - 1, 0), pltpu.roll(below, HALF - 2, 0), zero, zero, zero, zero)
        u1, _, s0, s1, s2, s3 = lax.fori_loop(0, n_strips, strip, init, unroll=2)
        carry_ref[...] = pltpu.roll(u1, 1, 0)
        dcw_ref[0:1, :] += jnp.sum(s0, axis=0, keepdims=True)
        dcw_ref[1:2, :] += jnp.sum(s1, axis=0, keepdims=True)
        dcw_ref[2:3, :] += jnp.sum(s2, axis=0, keepdims=True)
        dcw_ref[3:4, :] += jnp.sum(s3, axis=0, keepdims=True)

    rev = lambda j, i: (n_rows - 1 - i, j)
    return pl.pallas_call(
        body, name=name, grid=(N_CONV_BLOCKS, n_rows),
        in_specs=[pl.BlockSpec((TM, CONV_BLOCK), rev), pl.BlockSpec((TM, cb2), rev), pl.BlockSpec((TM, cb2), rev),
                  pl.BlockSpec((8, cb2), lambda j, i: (0, j))],
        out_specs=[pl.BlockSpec((TM, cb2), rev), pl.BlockSpec((8, cb2), lambda j, i: (0, j))],
        out_shape=[jax.ShapeDtypeStruct((LP, D_UP), BF16), jax.ShapeDtypeStruct((8, D_UP), F32)],
        scratch_shapes=[pltpu.VMEM((HALF, cb2), F32)],
        compiler_params=_cparams("arbitrary", "arbitrary"),
    )(dact, conv, u, cw8)


CHUNKS_PER_STEP = 3 if N_CHUNKS % 3 == 0 else 1
STEP_ROWS = CHUNKS_PER_STEP * CHUNK
N_STEPS = N_CHUNKS // CHUNKS_PER_STEP


def _ret_consts(h):
    rows = STEP_ROWS
    lg = math.log(1.0 - 2.0 ** (-5.0 - h))
    ri = lax.broadcasted_iota(jnp.int32, (rows, rows), 0)
    ci = lax.broadcasted_iota(jnp.int32, (rows, rows), 1)
    diff = (ri - ci).astype(F32)
    dmat = jnp.where(diff >= 0, jnp.exp(lg * jnp.maximum(diff, 0.0)), 0.0)
    rowf = lax.broadcasted_iota(jnp.int32, (rows, 1), 0).astype(F32)
    zeta = jnp.exp(lg * (rows - 1.0 - rowf))
    xi = jnp.exp(lg * (rowf + 1.0))
    return dmat, zeta, xi, math.exp(lg * rows)


def _rope(t, cosv, sinv):
    return t * cosv + pltpu.roll(t, RET_DK // 2, 1) * sinv


def _unrope(d, cosv, sinv):
    return d * cosv + pltpu.roll(d * sinv, RET_DK // 2, 1)


def _gla_masks():
    ri = lax.broadcasted_iota(jnp.int32, (CHUNK, CHUNK), 0)
    ci = lax.broadcasted_iota(jnp.int32, (CHUNK, CHUNK), 1)
    return dict(ri=ri, ci=ci, tril=(ri >= ci).astype(F32), heads=_head_block_mask(), own=_state_block_mask())


def _gla_common(p_ref, w2_ref, gb_ref, chunk, rows, masks):
    row = lax.broadcasted_iota(jnp.int32, (CHUNK, 1), 0)
    real = (chunk * CHUNK + row) >= PAD_ROWS
    ga = p_ref[rows, O_GA:O_GA + 128]
    z = _dot(ga, w2_ref[...]) + gb_ref[...]
    la = (jnp.minimum(z, 0.0) - jnp.log(1.0 + jnp.exp(-jnp.abs(z)))) * (1.0 / GLA_TAU)
    la = jnp.where(real, la, 0.0)
    ri, ci = masks["ri"], masks["ci"]
    cum = _dot_exact_rhs(masks["tril"], la)
    last = cum[CHUNK - 1:CHUNK, :]
    qs = p_ref[rows, O_GQ:O_GQ + 256] * (GLA_DK ** -0.5)
    k = p_ref[rows, O_GK:O_GK + 256]
    ecum = jnp.exp(cum)
    ekl = jnp.exp(last - cum)
    el = jnp.exp(last)
    refs = [jnp.zeros((1, 256), F32)] + [cum[a * SUB - 1:a * SUB, :] for a in range(1, N_SUB)]
    eq = [jnp.exp(cum[a * SUB:(a + 1) * SUB, :] - refs[a]) for a in range(N_SUB)]
    spread = refs[0] - cum[SUB - 1:SUB, :]
    for a in range(1, N_SUB):
        spread = jnp.maximum(spread, refs[a] - cum[(a + 1) * SUB - 1:(a + 1) * SUB, :])
    small = jnp.max(spread) <= GLA_FACTORED_MAX
    return dict(real=real, row=row, z=z, la=la, cum=cum, last=last, qs=qs, k=k, ecum=ecum, ekl=ekl, el=el,
                refs=refs, eq=eq, small=small, ri=ri, ci=ci, masks=masks)


GLA_FACTORED_MAX = 40.0


def _head_block_mask():
    r = lax.broadcasted_iota(jnp.int32, (CHUNK, 256), 0)
    col = lax.broadcasted_iota(jnp.int32, (CHUNK, 256), 1)
    return (r // SUB) == (col // GLA_DK)


def _state_block_mask():
    r = lax.broadcasted_iota(jnp.int32, (GLA_HEADS * GLA_DK, GLA_HEADS * GLA_DV), 0)
    col = lax.broadcasted_iota(jnp.int32, (GLA_HEADS * GLA_DK, GLA_HEADS * GLA_DV), 1)
    return (r // GLA_DK) == (col // GLA_DV)


def _block_diagonal(blocks):
    zero = jnp.zeros((GLA_DK, GLA_DV), F32)
    return jnp.concatenate([jnp.concatenate([blocks[h] if g == h else zero for g in range(GLA_HEADS)], axis=1)
                            for h in range(GLA_HEADS)], axis=0)


def _gla_factored(c):
    mask = c["masks"]["heads"]
    eks, keys, queries = [], [], []
    for a in range(N_SUB):
        ek = jnp.exp(jnp.minimum(c["refs"][a] - c["cum"], GLA_FACTORED_MAX))
        qh = c["qs"][a * SUB:(a + 1) * SUB, :] * c["eq"][a]
        eks.append(ek)
        keys.append(c["k"] * ek)
        queries.append(jnp.where(mask, jnp.concatenate([qh] * GLA_HEADS, axis=0), 0.0))
    return eks, keys, queries


def _gla_scores_factored(c, factored, p_scr):
    _, keys, queries = factored
    for a in range(N_SUB):
        out = _dot_nt(queries[a], keys[a])
        out = jnp.where(c["ci"] <= a * SUB + (c["ri"] & (SUB - 1)), out, 0.0)
        for h in range(GLA_HEADS):
            p_scr[h, a * SUB:(a + 1) * SUB, :] = out[h * SUB:(h + 1) * SUB, :]


def _gla_intra_bwd_factored(c, factored, dps, dq_scr, dk_scr):
    eks, keys, queries = factored
    mask = c["masks"]["heads"]
    dk = jnp.zeros((CHUNK, 256), F32)
    for a in range(N_SUB):
        dpa = jnp.concatenate([dps[h][a * SUB:(a + 1) * SUB, :] for h in range(GLA_HEADS)], axis=0)
        dq = jnp.where(mask, _dot(dpa, keys[a]), 0.0)
        dq = dq[0:SUB] + dq[SUB:2 * SUB] + dq[2 * SUB:3 * SUB] + dq[3 * SUB:4 * SUB]
        dq_scr[a * SUB:(a + 1) * SUB, :] = dq * c["eq"][a]
        dk = dk + _dot_tn(dpa, queries[a]) * eks[a]
    dk_scr[...] = dk


def _gla_lag_weights(c):
    cum, row = c["cum"], c["row"]
    out = [jnp.ones((CHUNK, 256), F32)]
    for r in range(1, SUB):
        out.append(jnp.where((row % SUB) >= r, jnp.exp(jnp.minimum(cum - pltpu.roll(cum, r, 0), 0.0)), 0.0))
    return out


def _gla_pairwise_keys(c):
    return [None] + [c["k"] * jnp.exp(jnp.minimum(c["refs"][a] - c["cum"], 0.0)) for a in range(1, N_SUB)]


def _gla_scores_pairwise(c, lag_w, keys, h):
    sl = slice(GLA_DK * h, GLA_DK * (h + 1))
    qs, k = c["qs"][:, sl], c["k"][:, sl]
    ri, ci = c["ri"], c["ci"]
    p = jnp.zeros((CHUNK, CHUNK), F32)
    for r in range(SUB):
        kr = k if r == 0 else pltpu.roll(k, r, 0)
        pr = jnp.sum(qs * kr * lag_w[r][:, sl], axis=1, keepdims=True)
        p = p + jnp.where(ci == ri - r, pr, 0.0)
    blocks = [jnp.zeros((SUB, CHUNK), F32)]
    for a in range(1, N_SUB):
        qh = qs[a * SUB:(a + 1) * SUB, :] * c["eq"][a][:, sl]
        blocks.append(jnp.where(ci[:SUB, :] < a * SUB, _dot_nt(qh, keys[a][:, sl]), 0.0))
    return p + jnp.concatenate(blocks, axis=0)


def _gla_all_scores(c, p_scr, factored):
    if factored:
        _gla_scores_factored(c, _gla_factored(c), p_scr)
    else:
        lag_w, keys = _gla_lag_weights(c), _gla_pairwise_keys(c)
        for h in range(GLA_HEADS):
            p_scr[h] = _gla_scores_pairwise(c, lag_w, keys, h)


def _either_form(chunks, run):
    small = chunks[0]["small"]
    for c in chunks[1:]:
        small = jnp.logical_and(small, c["small"])
    pl.when(small)(lambda: run(True))
    pl.when(jnp.logical_not(small))(lambda: run(False))


def _gla_intra_bwd_pairwise(c, lag_w, keys, dp, h):
    sl = slice(GLA_DK * h, GLA_DK * (h + 1))
    qs_h, k_h = c["qs"][:, sl], c["k"][:, sl]
    ri, ci = c["ri"], c["ci"]
    dq_rows = [jnp.zeros((SUB, GLA_DK), F32)]
    dk = jnp.zeros((CHUNK, GLA_DK), F32)
    for a in range(1, N_SUB):
        eq = c["eq"][a][:, sl]
        qh = qs_h[a * SUB:(a + 1) * SUB, :] * eq
        dpa = jnp.where(ci[:SUB, :] < a * SUB, dp[a * SUB:(a + 1) * SUB, :], 0.0)
        dq_rows.append(_dot(dpa, keys[a][:, sl]) * eq)
        ek = jnp.exp(jnp.minimum(c["refs"][a][:, sl] - c["cum"][:, sl], 0.0))
        dk = dk + _dot_tn(dpa, qh) * ek
    dq = jnp.concatenate(dq_rows, axis=0)
    for r in range(SUB):
        w = lag_w[r][:, sl]
        dpr = jnp.sum(jnp.where(ci == ri - r, dp, 0.0), axis=1, keepdims=True)
        kr = k_h if r == 0 else pltpu.roll(k_h, r, 0)
        dq = dq + dpr * kr * w
        back = dpr * qs_h * w
        dk = dk + (back if r == 0 else pltpu.roll(back, CHUNK - r, 0))
    return dq, dk


def _gla_all_intra_bwd(c, dps, p_scr, dq_scr, dk_scr, factored):
    if factored:
        terms = _gla_factored(c)
        _gla_scores_factored(c, terms, p_scr)
        _gla_intra_bwd_factored(c, terms, dps, dq_scr, dk_scr)
    else:
        lag_w, keys = _gla_lag_weights(c), _gla_pairwise_keys(c)
        outs = [_gla_intra_bwd_pairwise(c, lag_w, keys, dps[h], h) for h in range(GLA_HEADS)]
        for h in range(GLA_HEADS):
            p_scr[h] = _gla_scores_pairwise(c, lag_w, keys, h)
        dq_scr[...] = jnp.concatenate([o[0] for o in outs], axis=1)
        dk_scr[...] = jnp.concatenate([o[1] for o in outs], axis=1)


def _mixer_fwd(proj, cos2, sin2, w2p, gb, rnw, gnw, name, carried=()):
    n_carried = len(carried)

    def body(*refs):
        p_ref, c_ref, s_ref, w2_ref, gb_ref, rnw_ref, gnw_ref = refs[:7]
        x_refs, refs = refs[7:7 + n_carried], refs[7 + n_carried:]
        ocat_ref, mrg_ref, sr_out, sg_out = refs[:4]
        gathered_refs, refs = refs[4:4 + n_carried], refs[4 + n_carried:]
        sr, sg, p_scr = refs[:3]
        n = pl.program_id(0)
        if n_carried:
            start, forward, finish = _gather_phases(x_refs, gathered_refs, *refs[3:])
            pl.when(n == 0)(start)
            pl.when(n == (3 * N_STEPS) // 4)(forward)

        @pl.when(n == 0)
        def _():
            sr[...] = jnp.zeros_like(sr)
            sg[...] = jnp.zeros_like(sg)

        sr_out[0] = sr[...]
        cosv, sinv = c_ref[...], s_ref[...]

        for h in range(RET_HEADS):
            dmat, zeta, xi, gc = _ret_consts(h)
            hs = slice(128 * h, 128 * (h + 1))
            q = _rope(p_ref[:, O_RQ + 128 * h:O_RQ + 128 * (h + 1)], cosv, sinv)
            k = _rope(p_ref[:, O_RK + 128 * h:O_RK + 128 * (h + 1)], cosv, sinv) * (RET_DK ** -0.5)
            v = p_ref[:, O_RV + 128 * h:O_RV + 128 * (h + 1)]
            g = p_ref[:, O_RG + 128 * h:O_RG + 128 * (h + 1)]
            s_in = sr[h]
            a = _dot_nt(q, k) * dmat
            o = _dot(a, v) + _dot(q, s_in) * xi
            sr[h] = gc * s_in + _dot_tn(k * zeta, v)
            mu = jnp.mean(o, axis=-1, keepdims=True)
            xc = o - mu
            nrm = xc * lax.rsqrt(jnp.mean(xc * xc, axis=-1, keepdims=True) + EPS)
            ocat_ref[:, hs] = o
            mrg_ref[:, hs] = (nrm * rnw_ref[:, hs] * (g * _sigmoid(g))).astype(BF16)

        row_slices = [slice(CHUNK * j, CHUNK * (j + 1)) for j in range(CHUNKS_PER_STEP)]
        masks = _gla_masks()
        chunks = [_gla_common(p_ref, w2_ref, gb_ref, n * CHUNKS_PER_STEP + j, rows, masks)
                  for j, rows in enumerate(row_slices)]

        def gla_chunks(factored):
            own = masks["own"]
            for j, (rows, c) in enumerate(zip(row_slices, chunks)):
                s_in = sg[...]
                for h in range(GLA_HEADS):
                    sg_out[j, h] = s_in[GLA_DK * h:GLA_DK * (h + 1), GLA_DV * h:GLA_DV * (h + 1)]
                _gla_all_scores(c, p_scr.at[j], factored)
                v_all = p_ref[rows, O_GV:O_GV + GLA_HEADS * GLA_DV]
                o_inter = _dot(c["qs"] * c["ecum"], s_in)
                decay = jnp.exp(_dot_tn_exact_lhs(c["la"], jnp.ones((CHUNK, GLA_HEADS * GLA_DV), F32)))
                sg[...] = decay * s_in + jnp.where(own, _dot_tn(c["k"] * c["ekl"], v_all), 0.0)
                o_intra = _dot(p_scr[j].reshape(GLA_HEADS * CHUNK, CHUNK), v_all)
                for h in range(GLA_HEADS):
                    hs = slice(512 + 128 * h, 512 + 128 * (h + 1))
                    g = p_ref[rows, O_GR + 128 * h:O_GR + 128 * (h + 1)]
                    o = (o_intra[CHUNK * h:CHUNK * (h + 1), GLA_DV * h:GLA_DV * (h + 1)]
                         + o_inter[:, GLA_DV * h:GLA_DV * (h + 1)])
                    nrm = o * lax.rsqrt(jnp.mean(o * o, axis=-1, keepdims=True) + EPS)
                    ocat_ref[rows, hs] = o
                    mrg_ref[rows, hs] = (nrm * gnw_ref[:, 128 * h:128 * (h + 1)] * (g * _sigmoid(g))).astype(BF16)

        _either_form(chunks, gla_chunks)

        if n_carried:
            pl.when(n == N_STEPS - 1)(finish)

    const = lambda shape: pl.BlockSpec(shape, lambda n: (0,) * len(shape))
    anywhere = [pl.BlockSpec(memory_space=pl.ANY)] * n_carried
    return pl.pallas_call(
        body, name=name, grid=(N_STEPS,),
        in_specs=[pl.BlockSpec((STEP_ROWS, IN_WP), lambda n: (n, 0)),
                  pl.BlockSpec((STEP_ROWS, 128), lambda n: (n, 0)), pl.BlockSpec((STEP_ROWS, 128), lambda n: (n, 0)),
                  const((128, 256)), const((1, 256)), const((1, 512)), const((1, 512))] + anywhere,
        out_specs=[pl.BlockSpec((STEP_ROWS, D), lambda n: (n, 0)), pl.BlockSpec((STEP_ROWS, D), lambda n: (n, 0)),
                   pl.BlockSpec((1, RET_HEADS, RET_DK, 128), lambda n: (n, 0, 0, 0)),
                   pl.BlockSpec((CHUNKS_PER_STEP, GLA_HEADS, GLA_DK, GLA_DV), lambda n: (n, 0, 0, 0))] + anywhere,
        out_shape=[jax.ShapeDtypeStruct((LP, D), F32), jax.ShapeDtypeStruct((LP, D), BF16),
                   jax.ShapeDtypeStruct((N_STEPS, RET_HEADS, RET_DK, 128), F32),
                   jax.ShapeDtypeStruct((N_CHUNKS, GLA_HEADS, GLA_DK, GLA_DV), F32)] + _gathered_shapes(carried),
        scratch_shapes=[pltpu.VMEM((RET_HEADS, RET_DK, 128), F32),
                        pltpu.VMEM((GLA_HEADS * GLA_DK, GLA_HEADS * GLA_DV), F32),
                        pltpu.VMEM((CHUNKS_PER_STEP, GLA_HEADS, CHUNK, CHUNK), F32)] + _exchange_sems(n_carried),
        compiler_params=_cparams("arbitrary"),
    )(proj, cos2, sin2, w2p, gb, rnw, gnw, *carried)


def _mixer_bwd(proj, ocat, dmrg, sr_all, sg_all, cos2, sin2, w2p, gb, rnw, gnw, name, carried=()):
    last_step = N_STEPS - 1
    n_carried = len(carried)

    def body(*refs):
        p_ref, ocat_ref, dm_ref, sr_ref, sg_ref, c_ref, s_ref, w2_ref, gb_ref, rnw_ref, gnw_ref = refs[:11]
        g_refs, refs = refs[11:11 + n_carried], refs[11 + n_carried:]
        dp_ref, dw2_ref, dgb_ref, drn_ref, dgn_ref = refs[:5]
        got_refs, refs = refs[5:5 + n_carried], refs[5 + n_carried:]
        dsr, dsg, p_scr, dq_scr, dk_scr = refs[:5]
        step = pl.program_id(0)
        n = last_step - step
        if n_carried:
            start, finish = _exchange_phases(g_refs, got_refs, *refs[5:])
            pl.when(step == 0)(start)

        @pl.when(step == 0)
        def _():
            dsr[...] = jnp.zeros_like(dsr)
            dsg[...] = jnp.zeros_like(dsg)
            dw2_ref[...] = jnp.zeros_like(dw2_ref)
            dgb_ref[...] = jnp.zeros_like(dgb_ref)
            drn_ref[...] = jnp.zeros_like(drn_ref)
            dgn_ref[...] = jnp.zeros_like(dgn_ref)

        cosv, sinv = c_ref[...], s_ref[...]
        step_row = lax.broadcasted_iota(jnp.int32, (STEP_ROWS, 1), 0)
        real = ((n * STEP_ROWS + step_row) >= PAD_ROWS).astype(F32)

        for h in range(RET_HEADS):
            dmat, zeta, xi, gc = _ret_consts(h)
            hs = slice(128 * h, 128 * (h + 1))
            q = _rope(p_ref[:, O_RQ + 128 * h:O_RQ + 128 * (h + 1)], cosv, sinv)
            k = _rope(p_ref[:, O_RK + 128 * h:O_RK + 128 * (h + 1)], cosv, sinv) * (RET_DK ** -0.5)
            v = p_ref[:, O_RV + 128 * h:O_RV + 128 * (h + 1)]
            g = p_ref[:, O_RG + 128 * h:O_RG + 128 * (h + 1)]
            o = ocat_ref[:, hs]
            dy = dm_ref[:, hs]
            wv = rnw_ref[:, hs]
            mu = jnp.mean(o, axis=-1, keepdims=True)
            xc = o - mu
            rs = lax.rsqrt(jnp.mean(xc * xc, axis=-1, keepdims=True) + EPS)
            nrm = xc * rs
            sgm = _sigmoid(g)
            sil = g * sgm
            drn_ref[0:1, hs] += jnp.sum(dy * nrm * sil, axis=0, keepdims=True)
            dgate = dy * nrm * wv * (sgm * (1.0 + g * (1.0 - sgm)))
            dn = dy * wv * sil
            do = rs * (dn - jnp.mean(dn, axis=-1, keepdims=True) - nrm * jnp.mean(dn * nrm, axis=-1, keepdims=True))
            s_in = sr_ref[0, h]
            ds_out = dsr[h]
            a = _dot_nt(q, k) * dmat
            da = _dot_nt(do, v) * dmat
            dox = do * xi
            dq = _dot(da, k) + _dot_nt(dox, s_in)
            dk = _dot_tn(da, q) + _dot_nt(v, ds_out) * zeta
            dv = _dot_tn(a, do) + _dot(k * zeta, ds_out)
            dsr[h] = gc * ds_out + _dot_tn(q, dox)
            dk = dk * (RET_DK ** -0.5)
            dp_ref[:, O_RQ + 128 * h:O_RQ + 128 * (h + 1)] = (_unrope(dq, cosv, sinv) * real).astype(BF16)
            dp_ref[:, O_RK + 128 * h:O_RK + 128 * (h + 1)] = (_unrope(dk, cosv, sinv) * real).astype(BF16)
            dp_ref[:, O_RV + 128 * h:O_RV + 128 * (h + 1)] = (dv * real).astype(BF16)
            dp_ref[:, O_RG + 128 * h:O_RG + 128 * (h + 1)] = (dgate * real).astype(BF16)

        row_slices = [slice(CHUNK * j, CHUNK * (j + 1)) for j in range(CHUNKS_PER_STEP)]
        masks = _gla_masks()
        chunks = [_gla_common(p_ref, w2_ref, gb_ref, n * CHUNKS_PER_STEP + j, rows, masks)
                  for j, rows in enumerate(row_slices)]

        def gla_chunks(factored):
            for j in reversed(range(CHUNKS_PER_STEP)):
                gla_chunk_bwd(chunks[j], n * CHUNKS_PER_STEP + j, row_slices[j], j, factored, p_ref, ocat_ref, dm_ref,
                              sg_ref, w2_ref, gnw_ref, dp_ref, dw2_ref, dgb_ref, dgn_ref, dsg, p_scr, dq_scr, dk_scr)

        _either_form(chunks, gla_chunks)
        if n_carried:
            pl.when(step == last_step)(finish)

    def gla_chunk_bwd(c, chunk, rows, j, factored, p_ref, ocat_ref, dm_ref, sg_ref, w2_ref, gnw_ref,
                      dp_ref, dw2_ref, dgb_ref, dgn_ref, dsg, p_scr, dq_scr, dk_scr):
        row = lax.broadcasted_iota(jnp.int32, (CHUNK, 1), 0)
        real = ((chunk * CHUNK + row) >= PAD_ROWS).astype(F32)
        ri, ci = c["ri"], c["ci"]
        causal = ri >= ci
        triu = (ci >= ri).astype(F32)
        qe = c["qs"] * c["ecum"]
        kl = c["k"] * c["ekl"]
        v_all = p_ref[rows, O_GV:O_GV + GLA_HEADS * GLA_DV]
        dos, dps = [], []
        for h in range(GLA_HEADS):
            hs = slice(512 + 128 * h, 512 + 128 * (h + 1))
            g = p_ref[rows, O_GR + 128 * h:O_GR + 128 * (h + 1)]
            o = ocat_ref[rows, hs]
            dy = dm_ref[rows, hs]
            wv = gnw_ref[:, 128 * h:128 * (h + 1)]
            rs = lax.rsqrt(jnp.mean(o * o, axis=-1, keepdims=True) + EPS)
            nrm = o * rs
            sgm = _sigmoid(g)
            sil = g * sgm
            dgn_ref[0:1, 128 * h:128 * (h + 1)] += jnp.sum(dy * nrm * sil, axis=0, keepdims=True)
            dgate = dy * nrm * wv * (sgm * (1.0 + g * (1.0 - sgm)))
            dn = dy * wv * sil
            do = rs * (dn - nrm * jnp.mean(dn * nrm, axis=-1, keepdims=True))
            dp_ref[rows, O_GR + 128 * h:O_GR + 128 * (h + 1)] = (dgate * real).astype(BF16)
            dos.append(do)
        do_all = jnp.concatenate(dos, axis=1)
        do_blocks = jnp.where(c["masks"]["own"], jnp.concatenate([do_all] * GLA_HEADS, axis=0), 0.0)
        dp_all = _dot_nt(do_blocks, v_all)
        dps = [jnp.where(causal, dp_all[CHUNK * h:CHUNK * (h + 1), :], 0.0) for h in range(GLA_HEADS)]
        _gla_all_intra_bwd(c, dps, p_scr.at[j], dq_scr.at[j], dk_scr.at[j], factored)
        s_in = _block_diagonal([sg_ref[j, h] for h in range(GLA_HEADS)])
        ds_out = dsg[...]
        decay = jnp.exp(_dot_tn_exact_lhs(c["la"], jnp.ones((CHUNK, GLA_HEADS * GLA_DV), F32)))
        dv_state = _dot(kl, ds_out)
        dqe = _dot_nt(do_all, s_in)
        dkl = _dot_nt(v_all, ds_out)
        dsg[...] = jnp.where(c["masks"]["own"], _dot_tn(qe, do_all), 0.0) + decay * ds_out
        sd = s_in * ds_out
        sd_hi = sd.astype(BF16)
        sd_lo = (sd - sd_hi.astype(F32)).astype(BF16)
        ones8 = jnp.ones((8, GLA_HEADS * GLA_DV), BF16)
        nt = (((1,), (1,)), ((), ()))
        d_el = (lax.dot_general(ones8, sd_hi, nt, preferred_element_type=F32)
                + lax.dot_general(ones8, sd_lo, nt, preferred_element_type=F32))[0:1, :]
        dqs = dqe * c["ecum"] + dq_scr[j]
        dkk = dkl * c["ekl"] + dk_scr[j]
        d_last = jnp.sum(dkl * kl, axis=0, keepdims=True) + d_el * c["el"]
        dcum = c["qs"] * dqs - c["k"] * dkk + jnp.where(row == CHUNK - 1, d_last, 0.0)
        dla = _dot_exact_rhs(triu, dcum)
        dv = _dot_tn(p_scr[j].reshape(GLA_HEADS * CHUNK, CHUNK), do_blocks) + dv_state
        dp_ref[rows, O_GV:O_GV + GLA_HEADS * GLA_DV] = (dv * real).astype(BF16)
        dp_ref[rows, O_GQ:O_GQ + 256] = (dqs * (GLA_DK ** -0.5) * real).astype(BF16)
        dp_ref[rows, O_GK:O_GK + 256] = (dkk * real).astype(BF16)
        dz = dla * (1.0 / GLA_TAU) * _sigmoid(-c["z"]) * real
        ga = p_ref[rows, O_GA:O_GA + 128]
        dp_ref[rows, O_GA:O_GA + 128] = _dot_nt(dz, w2_ref[...]).astype(BF16)
        dp_ref[rows, O_GA + 128:IN_WP] = jnp.zeros((CHUNK, IN_WP - O_GA - 128), BF16)
        dw2_ref[...] += _dot_tn(ga, dz)
        dgb_ref[0:1, :] += jnp.sum(dz, axis=0, keepdims=True)

    const = lambda shape: pl.BlockSpec(shape, lambda s: (0,) * len(shape))
    rev = lambda s: (last_step - s, 0)
    anywhere = [pl.BlockSpec(memory_space=pl.ANY)] * n_carried
    return pl.pallas_call(
        body, name=name, grid=(N_STEPS,),
        in_specs=[pl.BlockSpec((STEP_ROWS, IN_WP), rev), pl.BlockSpec((STEP_ROWS, D), rev),
                  pl.BlockSpec((STEP_ROWS, D), rev),
                  pl.BlockSpec((1, RET_HEADS, RET_DK, 128), lambda s: (last_step - s, 0, 0, 0)),
                  pl.BlockSpec((CHUNKS_PER_STEP, GLA_HEADS, GLA_DK, GLA_DV), lambda s: (last_step - s, 0, 0, 0)),
                  pl.BlockSpec((STEP_ROWS, 128), rev), pl.BlockSpec((STEP_ROWS, 128), rev),
                  const((128, 256)), const((1, 256)), const((1, 512)), const((1, 512))] + anywhere,
        out_specs=[pl.BlockSpec((STEP_ROWS, IN_WP), rev), const((128, 256)), const((8, 256)),
                   const((8, 512)), const((8, 512))] + anywhere,
        out_shape=[jax.ShapeDtypeStruct((LP, IN_WP), BF16), jax.ShapeDtypeStruct((128, 256), F32),
                   jax.ShapeDtypeStruct((8, 256), F32), jax.ShapeDtypeStruct((8, 512), F32),
                   jax.ShapeDtypeStruct((8, 512), F32)] + [jax.ShapeDtypeStruct(g.shape, g.dtype) for g in carried],
        scratch_shapes=[pltpu.VMEM((RET_HEADS, RET_DK, 128), F32),
                        pltpu.VMEM((GLA_HEADS * GLA_DK, GLA_HEADS * GLA_DV), F32),
                        pltpu.VMEM((CHUNKS_PER_STEP, GLA_HEADS, CHUNK, CHUNK), F32),
                        pltpu.VMEM((CHUNKS_PER_STEP, CHUNK, 256), F32),
                        pltpu.VMEM((CHUNKS_PER_STEP, CHUNK, 256), F32)] + _exchange_sems(n_carried),
        compiler_params=_cparams("arbitrary"),
    )(proj, ocat, dmrg, sr_all, sg_all, cos2, sin2, w2p, gb, rnw, gnw, *carried)


def _all_gather(xs, name):
    n = len(xs)

    def body(*refs):
        start, forward, finish = _gather_phases(refs[:n], refs[n:2 * n], *refs[2 * n:])
        start()
        forward()
        finish()

    return pl.pallas_call(
        body, name=name,
        in_specs=[pl.BlockSpec(memory_space=pl.ANY)] * n,
        out_specs=[pl.BlockSpec(memory_space=pl.ANY)] * n,
        out_shape=_gathered_shapes(xs),
        scratch_shapes=_exchange_sems(n),
    )(*xs)


def _gathered_shapes(xs):
    return [jax.ShapeDtypeStruct((N_DEV,) + x.shape, x.dtype) for x in xs]


def _exchange_sems(n):
    if n == 0:
        return []
    return [pltpu.SemaphoreType.DMA((7 * n,)), pltpu.SemaphoreType.DMA((7 * n,)), pltpu.SemaphoreType.DMA((n,))]


def _gather_phases(x_refs, out_refs, send_sems, recv_sems, local_sems):
    n = len(x_refs)
    mx, my, mc = lax.axis_index("x"), lax.axis_index("y"), lax.axis_index("c")
    me, sibling = (mx, my, mc), (mx, my, 1 - mc)
    chips = [(1 - mx, my), (mx, 1 - my), (1 - mx, 1 - my)]

    def slot(a, px, py, pc):
        return out_refs[a].at[4 * px + 2 * py + pc]

    def copy(a, k, block, to, src=None):
        return pltpu.make_async_remote_copy(
            src_ref=slot(a, *block) if src is None else src, dst_ref=slot(a, *block),
            send_sem=send_sems.at[7 * a + k], recv_sem=recv_sems.at[7 * a + k],
            device_id=to, device_id_type=MESH_IDS)

    mine = [pltpu.make_async_copy(x_refs[a], slot(a, *me), local_sems.at[a]) for a in range(n)]
    first = []
    for a in range(n):
        first.append(copy(a, 0, me, sibling, src=x_refs[a]))
        first += [copy(a, 1 + j, me, (*chip, mc), src=x_refs[a]) for j, chip in enumerate(chips)]
    passed = [copy(a, 4 + j, (*chip, mc), sibling) for j, chip in enumerate(chips) for a in range(n)]

    def start():
        for cp in mine + first:
            cp.start()

    def forward():
        for j, chip in enumerate(chips):
            for a in range(n):
                copy(a, 1 + j, (*chip, mc), me).wait_recv()
                passed[j * n + a].start()

    def finish():
        for a in range(n):
            copy(a, 0, sibling, me).wait_recv()
            for j, chip in enumerate(chips):
                copy(a, 4 + j, (*chip, 1 - mc), me).wait_recv()
        for cp in first + passed:
            cp.wait_send()
        for cp in mine:
            cp.wait()

    return start, forward, finish


def _exchange_blocks(gs, name):
    n = len(gs)

    def body(*refs):
        start, finish = _exchange_phases(refs[:n], refs[n:2 * n], *refs[2 * n:])
        start()
        finish()

    return pl.pallas_call(
        body, name=name,
        in_specs=[pl.BlockSpec(memory_space=pl.ANY)] * n,
        out_specs=[pl.BlockSpec(memory_space=pl.ANY)] * n,
        out_shape=[jax.ShapeDtypeStruct(g.shape, g.dtype) for g in gs],
        scratch_shapes=_exchange_sems(n),
    )(*gs)


def _exchange_phases(g_refs, out_refs, send_sems, recv_sems, local_sems):
    n = len(g_refs)
    mx, my, mc = lax.axis_index("x"), lax.axis_index("y"), lax.axis_index("c")
    me = 4 * mx + 2 * my + mc
    mine = [pltpu.make_async_copy(g_refs[a].at[me], out_refs[a].at[me], local_sems.at[a]) for a in range(n)]
    copies = []
    for r in range(1, N_DEV):
        px, py, pc = mx ^ (r >> 2), my ^ ((r >> 1) & 1), mc ^ (r & 1)
        peer = 4 * px + 2 * py + pc
        for a in range(n):
            copies.append(pltpu.make_async_remote_copy(
                src_ref=g_refs[a].at[peer], dst_ref=out_refs[a].at[me],
                send_sem=send_sems.at[7 * a + r - 1], recv_sem=recv_sems.at[7 * a + r - 1],
                device_id=(px, py, pc), device_id_type=MESH_IDS))

    def start():
        for cp in mine + copies:
            cp.start()

    def finish():
        for cp in copies:
            cp.wait_recv()
        for cp in copies:
            cp.wait_send()
        for cp in mine:
            cp.wait()

    return start, finish


IN_SHARD = IN_W // N_DEV
IN_SHARD_P = 512
UP_SHARD = D_UP // N_DEV
UP_SHARD_P = 768
RELAYOUT_ROWS = 256


def _pieces_w_in():
    return [(k, 0, IN_SHARD * k, IN_SHARD) for k in range(N_DEV)]


def _pieces_ffn_up():
    pieces = []
    for k in range(N_DEV):
        n, end = UP_SHARD * k, UP_SHARD * (k + 1)
        while n < end:
            half, r = divmod(n, D_FF)
            blk, off = divmod(r, CONV_BLOCK)
            run = min(CONV_BLOCK - off, end - n)
            pieces.append((k, n - UP_SHARD * k, 2 * CONV_BLOCK * blk + CONV_BLOCK * half + off, run))
            n += run
    return pieces


def _assemble_block(load, spans, dst_block, rows):
    lo = 128 * dst_block
    lane = lax.broadcasted_iota(jnp.int32, (1, 128), 1)
    out = jnp.zeros((rows, 128), F32)
    for key, src_off, dst_off, length in spans:
        a, b = max(lo, dst_off), min(lo + 128, dst_off + length)
        s, s_end = src_off + (a - dst_off), src_off + (b - dst_off)
        d = a
        while s < s_end:
            e = min(s_end, 128 * (s // 128 + 1))
            blk = load(key, s // 128)
            shift = (d - s) % 128
            if shift:
                blk = pltpu.roll(blk, shift, 1)
            out = jnp.where((lane >= d - lo) & (lane < d - lo + (e - s)), blk, out)
            d += e - s
            s = e
    return out


def _shards_to_cols(shards, pieces, width, name):
    _, rows, _ = shards.shape
    tr = RELAYOUT_ROWS

    def body(s_ref, o_ref):
        load = lambda k, b: s_ref[k, :, 128 * b:128 * (b + 1)].astype(F32)
        for db in range(width // 128):
            o_ref[:, 128 * db:128 * (db + 1)] = _assemble_block(load, pieces, db, tr).astype(BF16)

    return pl.pallas_call(
        body, name=name, grid=(rows // tr,),
        in_specs=[pl.BlockSpec((N_DEV, tr, shards.shape[2]), lambda i: (0, i, 0))],
        out_specs=pl.BlockSpec((tr, width), lambda i: (i, 0)),
        out_shape=jax.ShapeDtypeStruct((rows, width), BF16),
        compiler_params=_cparams("parallel"),
    )(shards)


def _cols_to_shards(full, pieces, shard_width, name):
    rows, width = full.shape
    tr = RELAYOUT_ROWS

    def body(f_ref, o_ref):
        load = lambda _, b: f_ref[:, 128 * b:128 * (b + 1)].astype(F32)
        for k in range(N_DEV):
            spans = [(None, dst_off, src_off, length) for dev, src_off, dst_off, length in pieces if dev == k]
            for db in range(shard_width // 128):
                o_ref[k, :, 128 * db:128 * (db + 1)] = _assemble_block(load, spans, db, tr).astype(BF16)

    return pl.pallas_call(
        body, name=name, grid=(rows // tr,),
        in_specs=[pl.BlockSpec((tr, width), lambda i: (i, 0))],
        out_specs=pl.BlockSpec((N_DEV, tr, shard_width), lambda i: (0, i, 0)),
        out_shape=jax.ShapeDtypeStruct((N_DEV, rows, shard_width), BF16),
        compiler_params=_cparams("parallel"),
    )(full)


def _adamw(parts, w, m, v, rows_per_step, name):
    rows, cols = w.shape
    assert rows % rows_per_step == 0 and parts.shape == (N_DEV, rows, cols)

    def body(p_ref, w_ref, m_ref, v_ref, g_ref, d_ref, nm_ref, nv_ref):
        g = p_ref[0].astype(F32)
        for j in range(1, N_DEV):
            g = g + p_ref[j].astype(F32)
        m_new = ADAM_B1 * m_ref[...] + (1.0 - ADAM_B1) * g
        v_new = ADAM_B2 * v_ref[...] + (1.0 - ADAM_B2) * (g * g)
        m_hat = m_new / (1.0 - ADAM_B1 ** ADAM_STEP)
        v_hat = v_new / (1.0 - ADAM_B2 ** ADAM_STEP)
        g_ref[...] = g
        d_ref[...] = -ADAM_LR * (m_hat / (jnp.sqrt(v_hat) + ADAM_EPS) + ADAM_WD * w_ref[...])
        nm_ref[...] = m_new
        nv_ref[...] = v_new

    tile = pl.BlockSpec((rows_per_step, cols), lambda i: (i, 0))
    shape = jax.ShapeDtypeStruct((rows, cols), F32)
    return pl.pallas_call(
        body, name=name, grid=(rows // rows_per_step,),
        in_specs=[pl.BlockSpec((N_DEV, rows_per_step, cols), lambda i: (0, i, 0)), tile, tile, tile],
        out_specs=[tile, tile, tile, tile],
        out_shape=[shape, shape, shape, shape],
        compiler_params=_cparams("parallel"),
    )(parts, w, m, v)


BIG = (("w_in", (DEPTH, D, IN_W // N_DEV), 2), ("w_out", (DEPTH, D // N_DEV, D), 1),
       ("ffn_up", (DEPTH, D, D_UP // N_DEV), 2), ("ffn_down", (DEPTH, D_FF // N_DEV, D), 1))
SMALL = (("meta_tokens", (N_META, D // N_DEV), 1), ("gla_gate_w2", (DEPTH, GATE_RANK, 256 // N_DEV), 2),
         ("ffn_conv_w", (DEPTH, 3, D_UP // N_DEV), 2))
REPL = (("pre_mix_norm", (DEPTH, D)), ("gla_gate_b", (DEPTH, 256)), ("ret_norm_w", (DEPTH, 512)),
        ("gla_norm_w", (DEPTH, 512)), ("post_mix_norm", (DEPTH, D)), ("pre_ffn_norm", (DEPTH, D)),
        ("ffn_conv_b", (DEPTH, D_UP)), ("post_ffn_norm", (DEPTH, D)))
WEIGHT_ORDER = ("meta_tokens", "pre_mix_norm", "w_in", "gla_gate_w2", "gla_gate_b", "ret_norm_w", "gla_norm_w",
                "w_out", "post_mix_norm", "pre_ffn_norm", "ffn_up", "ffn_conv_w", "ffn_conv_b", "ffn_down",
                "post_ffn_norm")


def _size(shape):
    return math.prod(shape)


def _round_up(n, mult):
    return -(-n // mult) * mult


REPL_ROWS = _round_up(-(-sum(_size(s) for _, s in REPL) // LANES), 8)
SMALL_ROWS = _round_up(-(-sum(_size(s) for _, s, _ in SMALL) // LANES), 8)


def _pack(arrays, rows, dtype):
    flat = jnp.concatenate([a.reshape(-1).astype(dtype) for a in arrays])
    return jnp.pad(flat, (0, rows * LANES - flat.shape[0])).reshape(rows, LANES)


def _unpack(buf, shapes):
    flat = buf.reshape(-1)
    out, off = [], 0
    for shape in shapes:
        out.append(flat[off:off + _size(shape)].reshape(shape))
        off += _size(shape)
    return out


def _unshard(blocks, axis):
    moved = jnp.moveaxis(blocks, 0, axis)
    shape = list(moved.shape)
    shape[axis:axis + 2] = [shape[axis] * shape[axis + 1]]
    return moved.reshape(shape)


def _to_blocks(full, axis):
    shape = list(full.shape)
    shape[axis:axis + 1] = [N_DEV, shape[axis] // N_DEV]
    return jnp.moveaxis(full.reshape(shape), axis, 0)


def _interleave_cols(w):
    lead = w.shape[:-1]
    return jnp.swapaxes(w.reshape(lead + (2, N_CONV_BLOCKS, CONV_BLOCK)), -3, -2).reshape(lead + (D_UP,))


def _deinterleave_cols(w):
    lead = w.shape[:-1]
    return jnp.swapaxes(w.reshape(lead + (N_CONV_BLOCKS, 2, CONV_BLOCK)), -3, -2).reshape(lead + (D_UP,))


def _rope_tables():
    half = RET_DK // 2
    inv = ROPE_BASE ** (-jnp.arange(half, dtype=F32) / half)
    pos = jnp.arange(LP, dtype=F32) - float(PAD_ROWS)
    ang = pos[:, None] * inv[None, :]
    c, s = jnp.cos(ang), jnp.sin(ang)
    return jnp.concatenate([c, c], axis=1), jnp.concatenate([-s, s], axis=1)


def kernel(x, meta_tokens, pre_mix_norm, w_in, gla_gate_w2, gla_gate_b, ret_norm_w, gla_norm_w, w_out, post_mix_norm, pre_ffn_norm, ffn_up, ffn_conv_w, ffn_conv_b, ffn_down, post_ffn_norm, loss_target, m_meta_tokens, m_pre_mix_norm, m_w_in, m_gla_gate_w2, m_gla_gate_b, m_ret_norm_w, m_gla_norm_w, m_w_out, m_post_mix_norm, m_pre_ffn_norm, m_ffn_up, m_ffn_conv_w, m_ffn_conv_b, m_ffn_down, m_post_ffn_norm, v_meta_tokens, v_pre_mix_norm, v_w_in, v_gla_gate_w2, v_gla_gate_b, v_ret_norm_w, v_gla_norm_w, v_w_out, v_post_mix_norm, v_pre_ffn_norm, v_ffn_up, v_ffn_conv_w, v_ffn_conv_b, v_ffn_down, v_post_ffn_norm):
    weights = dict(meta_tokens=meta_tokens, pre_mix_norm=pre_mix_norm, w_in=w_in, gla_gate_w2=gla_gate_w2,
                   gla_gate_b=gla_gate_b, ret_norm_w=ret_norm_w, gla_norm_w=gla_norm_w, w_out=w_out,
                   post_mix_norm=post_mix_norm, pre_ffn_norm=pre_ffn_norm, ffn_up=ffn_up, ffn_conv_w=ffn_conv_w,
                   ffn_conv_b=ffn_conv_b, ffn_down=ffn_down, post_ffn_norm=post_ffn_norm)
    mom1 = dict(meta_tokens=m_meta_tokens, pre_mix_norm=m_pre_mix_norm, w_in=m_w_in, gla_gate_w2=m_gla_gate_w2,
                gla_gate_b=m_gla_gate_b, ret_norm_w=m_ret_norm_w, gla_norm_w=m_gla_norm_w, w_out=m_w_out,
                post_mix_norm=m_post_mix_norm, pre_ffn_norm=m_pre_ffn_norm, ffn_up=m_ffn_up,
                ffn_conv_w=m_ffn_conv_w, ffn_conv_b=m_ffn_conv_b, ffn_down=m_ffn_down, post_ffn_norm=m_post_ffn_norm)
    mom2 = dict(meta_tokens=v_meta_tokens, pre_mix_norm=v_pre_mix_norm, w_in=v_w_in, gla_gate_w2=v_gla_gate_w2,
                gla_gate_b=v_gla_gate_b, ret_norm_w=v_ret_norm_w, gla_norm_w=v_gla_norm_w, w_out=v_w_out,
                post_mix_norm=v_post_mix_norm, pre_ffn_norm=v_pre_ffn_norm, ffn_up=v_ffn_up,
                ffn_conv_w=v_ffn_conv_w, ffn_conv_b=v_ffn_conv_b, ffn_down=v_ffn_down, post_ffn_norm=v_post_ffn_norm)

    pad_cols = lambda a, width: jnp.pad(a, ((0, 0), (0, width - a.shape[1])))
    big_names = [n for n, _, _ in BIG]
    shard = {}
    for l in range(DEPTH):
        shard[l, "w_in"] = pad_cols(w_in[l].astype(BF16), IN_SHARD_P)
        shard[l, "w_out"] = w_out[l].astype(BF16)
        shard[l, "ffn_up"] = pad_cols(ffn_up[l].astype(BF16), UP_SHARD_P)
        shard[l, "ffn_down"] = ffn_down[l].astype(BF16)
    w_in_0, small = _all_gather([shard[0, "w_in"], _pack([weights[n] for n, _, _ in SMALL], SMALL_ROWS, F32)],
                                "gather_first_weights")
    gathered = {(0, "w_in"): w_in_0}
    gather_in_mixer = {l: [(l, n) for n in big_names[1:]] for l in range(DEPTH)}
    gather_in_conv = {l: [(l + 1, "w_in")] for l in range(DEPTH - 1)}
    small_parts = _unpack_blocks(small, [s for _, s, _ in SMALL])
    full = {n: _unshard(p, ax) for (n, _, ax), p in zip(SMALL, small_parts)}
    w2p = jnp.pad(full["gla_gate_w2"], ((0, 0), (0, 128 - GATE_RANK), (0, 0)))
    cw8 = jnp.concatenate([_interleave_cols(full["ffn_conv_w"]), _interleave_cols(ffn_conv_b)[:, None, :],
                           jnp.zeros((DEPTH, 4, D_UP), F32)], axis=1)
    cos2, sin2 = _rope_tables()

    h = jnp.concatenate([jnp.zeros((PAD_ROWS, D), F32), full["meta_tokens"], x[0]], axis=0)
    target = jnp.concatenate([jnp.zeros((CHUNK, D), F32), loss_target[0]], axis=0)
    saved, layer_w = [], []
    for l in range(DEPTH):
        lw = dict(w_in=_shards_to_cols(gathered[l, "w_in"], _pieces_w_in(), IN_WP, f"w_in_cols_{l}"))
        a1, proj = _norm_matmul(h, pre_mix_norm[l:l + 1], lw["w_in"], out_dtype=F32, tm=TM_BIG, tn=IN_WP // 3,
                                name=f"in_proj_{l}")
        keys = gather_in_mixer.get(l, [])
        ocat, merged, sr_all, sg_all, *got = _mixer_fwd(proj, cos2, sin2, w2p[l], gla_gate_b[l:l + 1],
                                                        ret_norm_w[l:l + 1], gla_norm_w[l:l + 1], f"mixer_fwd_{l}",
                                                        carried=[shard[key] for key in keys])
        gathered.update(zip(keys, got))
        lw["w_out"] = gathered[l, "w_out"].reshape(D, D)
        lw["w_up"] = _shards_to_cols(gathered[l, "ffn_up"], _pieces_ffn_up(), D_UP, f"ffn_up_cols_{l}")
        lw["w_down"] = gathered[l, "ffn_down"].reshape(D_FF, D)
        layer_w.append(lw)
        m, h1 = _matmul_resid_norm(merged, lw["w_out"], h, post_mix_norm[l:l + 1], f"out_proj_{l}")
        a2, u = _norm_matmul(h1, pre_ffn_norm[l:l + 1], lw["w_up"], out_dtype=BF16, tm=TM_BIG, tn=D_UP // 4,
                             name=f"ffn_up_{l}")
        keys = gather_in_conv.get(l, [])
        cv, act, *got = _conv_act_fwd(u, cw8[l], f"ffn_conv_act_{l}", carried=[shard[key] for key in keys])
        gathered.update(zip(keys, got))
        f, h2, *loss_acc = _matmul_resid_norm(act, lw["w_down"], h1, post_ffn_norm[l:l + 1], f"ffn_down_{l}",
                                              target=target if l == DEPTH - 1 else None)
        saved.append(dict(h=h, a1=a1, proj=proj, ocat=ocat, merged=merged, sr=sr_all, sg=sg_all, m=m, h1=h1,
                          a2=a2, u=u, cv=cv, act=act, f=f))
        h = h2

    dh = h
    loss = lax.psum(loss_acc[0][0, 0], ("x", "y", "c"))

    kinds = ("grad", "delta", "new_m", "new_v")
    grads = {n: [None] * DEPTH for n in WEIGHT_ORDER if n != "meta_tokens" and n not in big_names}
    pending, parts = [], {}
    for l in reversed(range(DEPTH)):
        s, lw = saved[l], layer_w[l]
        dact, df, g_post_ffn = _norm_bwd_matmul(dh, s["f"], post_ffn_norm[l:l + 1], lw["w_down"], BF16,
                                                f"ffn_down_dx_{l}")
        g_down = _matmul(s["act"], df, ta=True, out_dtype=BF16, tm=D_FF // 2, tn=D, tk=TK_LONG, name=f"ffn_down_dw_{l}")
        du, dcw = _conv_act_bwd(dact, s["cv"], s["u"], cw8[l], f"ffn_conv_act_bwd_{l}")
        dh1, g_pre_ffn = _matmul_norm_bwd(du, lw["w_up"], s["h1"], pre_ffn_norm[l:l + 1], dh, D_FF, f"ffn_up_dx_{l}")
        g_up = _matmul(s["a2"], du, ta=True, out_dtype=BF16, tm=D, tn=D_UP // 4, tk=TK_LONG, name=f"ffn_up_dw_{l}")
        dmerged, dm, g_post_mix = _norm_bwd_matmul(dh1, s["m"], post_mix_norm[l:l + 1], lw["w_out"], F32,
                                                   f"out_proj_dx_{l}")
        g_out = _matmul(s["merged"], dm, ta=True, out_dtype=BF16, tm=D, tn=D, tk=TK_LONG, name=f"out_proj_dw_{l}")
        pending += [((l, "ffn_down"), g_down.reshape(N_DEV, D_FF // N_DEV, D)),
                    ((l, "ffn_up"), _cols_to_shards(g_up, _pieces_ffn_up(), UP_SHARD_P, f"ffn_up_grad_shards_{l}")),
                    ((l, "w_out"), g_out.reshape(N_DEV, D // N_DEV, D))]
        dproj, g_w2, g_gb, g_rn, g_gn, *got = _mixer_bwd(s["proj"], s["ocat"], dmerged, s["sr"], s["sg"], cos2, sin2,
                                                         w2p[l], gla_gate_b[l:l + 1], ret_norm_w[l:l + 1],
                                                         gla_norm_w[l:l + 1], f"mixer_bwd_{l}",
                                                         carried=[blocks for _, blocks in pending])
        parts.update(zip([key for key, _ in pending], got))
        g_in = _matmul(s["a1"], dproj, ta=True, out_dtype=BF16, tm=D, tn=IN_WP // 3, tk=TK_LONG, name=f"in_proj_dw_{l}")
        pending = [((l, "w_in"), _cols_to_shards(g_in, _pieces_w_in(), IN_SHARD_P, f"w_in_grad_shards_{l}"))]
        now = pending if l == 0 else []
        dh, g_pre_mix, *got = _matmul_norm_bwd(dproj, lw["w_in"], s["h"], pre_mix_norm[l:l + 1], dh1, IN_WP,
                                               f"in_proj_dx_{l}", carried=[blocks for _, blocks in now])
        parts.update(zip([key for key, _ in now], got))
        pending = [] if l == 0 else pending
        grads["post_ffn_norm"][l] = g_post_ffn[0]
        grads["ffn_conv_w"][l] = _deinterleave_cols(dcw[0:3])
        grads["ffn_conv_b"][l] = _deinterleave_cols(dcw[3])
        grads["pre_ffn_norm"][l] = g_pre_ffn[0]
        grads["post_mix_norm"][l] = g_post_mix[0]
        grads["gla_gate_w2"][l] = g_w2[:GATE_RANK]
        grads["gla_gate_b"][l] = g_gb[0]
        grads["ret_norm_w"][l] = g_rn[0]
        grads["gla_norm_w"][l] = g_gn[0]
        grads["pre_mix_norm"][l] = g_pre_mix[0]
    local = {n: jnp.stack(v) for n, v in grads.items()}
    local["meta_tokens"] = dh[PAD_ROWS:CHUNK]
    grad_x = dh[CHUNK:][None]

    blocks = jnp.concatenate([_to_blocks(local[n], ax).reshape(N_DEV, -1) for n, _, ax in SMALL], axis=1)
    blocks = jnp.pad(blocks, ((0, 0), (0, SMALL_ROWS * LANES - blocks.shape[1]))).reshape(N_DEV, SMALL_ROWS, LANES)
    *got, small_grad_parts = _exchange_blocks([b for _, b in pending] + [blocks], "exchange_last_grads")
    parts.update(zip([key for key, _ in pending], got))

    widths = dict(w_in=IN_SHARD_P, w_out=D, ffn_up=UP_SHARD_P, ffn_down=D)
    steps = dict(w_in=256, w_out=D // N_DEV, ffn_up=256, ffn_down=D_FF // N_DEV // 2)
    big_out = {kind: {n: [None] * DEPTH for n in big_names} for kind in kinds}
    for l in range(DEPTH):
        for n in big_names:
            mine = [pad_cols(d[n][l], widths[n]) for d in (weights, mom1, mom2)]
            results = _adamw(parts[l, n], *mine, steps[n], f"adamw_{n}_{l}")
            for kind, r in zip(kinds, results):
                big_out[kind][n][l] = r[:, :weights[n].shape[2]]
    out = {kind: {n: jnp.stack(v) for n, v in big_out[kind].items()} for kind in kinds}
    shard_shapes = [s for _, s, _ in SMALL]
    packed = [_pack([d[n] for n, _, _ in SMALL], SMALL_ROWS, F32) for d in (weights, mom1, mom2)]
    results = _adamw(small_grad_parts, *packed, SMALL_ROWS, "adamw_small_sharded")
    for kind, buf in zip(kinds, results):
        out[kind].update(zip([n for n, _, _ in SMALL], _unpack(buf, shard_shapes)))

    repl_parts = _all_gather([_pack([local[n] for n, _ in REPL], REPL_ROWS, F32)], "gather_small_grads")[0]
    packed = [_pack([d[n] for n, _ in REPL], REPL_ROWS, F32) for d in (weights, mom1, mom2)]
    results = _adamw(repl_parts, *packed, REPL_ROWS, "adamw_replicated")
    repl_shapes = [s for _, s in REPL]
    for kind, buf in zip(kinds, results):
        out[kind].update(zip([n for n, _ in REPL], _unpack(buf, repl_shapes)))

    return (loss, grad_x, *[out["grad"][n] for n in WEIGHT_ORDER], *[out["delta"][n] for n in WEIGHT_ORDER],
            *[out["new_m"][n] for n in WEIGHT_ORDER], *[out["new_v"][n] for n in WEIGHT_ORDER])


def _unpack_blocks(gathered, shapes):
    flat = gathered.reshape(N_DEV, -1)
    out, off = [], 0
    for shape in shapes:
        out.append(flat[:, off:off + _size(shape)].reshape((N_DEV,) + shape))
        off += _size(shape)
    return out
```

```python
import math

import jax
import jax.numpy as jnp
from jax import lax
from jax.experimental import pallas as pl
from jax.experimental.pallas import tpu as pltpu

F32 = jnp.float32
BF16 = jnp.bfloat16

D = 1024
SEQ = 8192
DEPTH = 2
N_META = 16
CHUNK = 64
SUB = 16
N_SUB = CHUNK // SUB
PAD_ROWS = CHUNK - N_META
LP = SEQ + CHUNK
N_CHUNKS = LP // CHUNK
RET_HEADS = 4
RET_DK = 128
GLA_HEADS = 4
GLA_DK = 64
GLA_DV = 128
GLA_TAU = 16.0
GATE_RANK = 16
IN_W = 3600
IN_WP = 3840
D_FF = 2816
D_UP = 2 * D_FF
CONV_BLOCK = 256
N_CONV_BLOCKS = D_FF // CONV_BLOCK
ROPE_BASE = 10000.0
EPS = 1e-6
N_DEV = 8
LANES = 1024

O_RQ, O_RK, O_RV, O_RG = 0, 512, 1024, 1536
O_GQ, O_GK, O_GV, O_GR, O_GA = 2048, 2304, 2560, 3072, 3584

ADAM_LR = 0.001
ADAM_B1 = 0.9
ADAM_B2 = 0.999
ADAM_EPS = 1e-08
ADAM_WD = 0.01
ADAM_STEP = 10

VMEM_LIMIT = 56 * 1024 * 1024
MESH_IDS = pl.DeviceIdType.MESH


def _row_tile(rows, limit):
    best = 16
    for t in range(16, min(rows, limit) + 1, 16):
        if rows % t == 0:
            best = t
    return best


TM = _row_tile(LP, 688)
TM_BIG = _row_tile(LP, 1376)
TK_LONG = _row_tile(LP, 2752)


def _cparams(*sem):
    return pltpu.CompilerParams(dimension_semantics=sem, vmem_limit_bytes=VMEM_LIMIT)


def _dot(a, b):
    return jnp.dot(a.astype(BF16), b.astype(BF16), preferred_element_type=F32)


def _dot_nt(a, b):
    return lax.dot_general(a.astype(BF16), b.astype(BF16), (((1,), (1,)), ((), ())), preferred_element_type=F32)


def _dot_tn(a, b):
    return lax.dot_general(a.astype(BF16), b.astype(BF16), (((0,), (0,)), ((), ())), preferred_element_type=F32)


def _split3(x):
    hi = x.astype(BF16)
    r1 = x - hi.astype(F32)
    mid = r1.astype(BF16)
    lo = (r1 - mid.astype(F32)).astype(BF16)
    return hi, mid, lo


def _dot_exact_rhs(t, x):
    n = x.shape[1]
    parts = jnp.dot(t.astype(BF16), jnp.concatenate(_split3(x), axis=1), preferred_element_type=F32)
    return parts[:, :n] + parts[:, n:2 * n] + parts[:, 2 * n:]


def _dot_tn_exact_lhs(x, ones):
    n = x.shape[1]
    parts = lax.dot_general(jnp.concatenate(_split3(x), axis=1), ones.astype(BF16), (((0,), (0,)), ((), ())),
                            preferred_element_type=F32)
    return parts[:n] + parts[n:2 * n] + parts[2 * n:]


def _sigmoid(x):
    return 1.0 / (1.0 + jnp.exp(-x))


def _matmul(a, b, *, ta=False, tb=False, out_dtype, tm, tn, tk, name):
    m = a.shape[1] if ta else a.shape[0]
    k = a.shape[0] if ta else a.shape[1]
    n = b.shape[0] if tb else b.shape[1]
    assert (b.shape[1] if tb else b.shape[0]) == k
    assert m % tm == 0 and n % tn == 0 and k % tk == 0, (name, m, n, k, tm, tn, tk)
    nk = k // tk
    a_spec = pl.BlockSpec((tk, tm), lambda i, j, kk: (kk, i)) if ta else pl.BlockSpec((tm, tk), lambda i, j, kk: (i, kk))
    b_spec = pl.BlockSpec((tn, tk), lambda i, j, kk: (j, kk)) if tb else pl.BlockSpec((tk, tn), lambda i, j, kk: (kk, j))
    dims = (((0 if ta else 1,), (1 if tb else 0,)), ((), ()))

    def body(a_ref, b_ref, o_ref, *acc):
        prod = lax.dot_general(a_ref[...].astype(BF16), b_ref[...].astype(BF16), dims, preferred_element_type=F32)
        if nk == 1:
            o_ref[...] = prod.astype(out_dtype)
            return
        acc_ref, = acc
        kk = pl.program_id(2)

        @pl.when(kk == 0)
        def _():
            acc_ref[...] = prod

        @pl.when(kk > 0)
        def _():
            acc_ref[...] += prod

        @pl.when(kk == nk - 1)
        def _():
            o_ref[...] = acc_ref[...].astype(out_dtype)

    return pl.pallas_call(
        body, name=name, grid=(m // tm, n // tn, nk),
        in_specs=[a_spec, b_spec],
        out_specs=pl.BlockSpec((tm, tn), lambda i, j, kk: (i, j)),
        out_shape=jax.ShapeDtypeStruct((m, n), out_dtype),
        scratch_shapes=[pltpu.VMEM((tm, tn), F32)] if nk > 1 else [],
        compiler_params=_cparams("parallel", "parallel", "arbitrary"),
    )(a, b)


def _norm_matmul(x, w, b, *, out_dtype, tm, tn, name):
    n = b.shape[1]
    assert LP % tm == 0 and n % tn == 0

    def body(x_ref, w_ref, b_ref, a_ref, o_ref, a_scr):
        @pl.when(pl.program_id(1) == 0)
        def _():
            xv = x_ref[...]
            r = lax.rsqrt(jnp.mean(xv * xv, axis=-1, keepdims=True) + EPS)
            a = (xv * r * w_ref[...]).astype(BF16)
            a_scr[...] = a
            a_ref[...] = a

        o_ref[...] = jnp.dot(a_scr[...], b_ref[...], preferred_element_type=F32).astype(out_dtype)

    return pl.pallas_call(
        body, name=name, grid=(LP // tm, n // tn),
        in_specs=[pl.BlockSpec((tm, D), lambda i, j: (i, 0)), pl.BlockSpec((1, D), lambda i, j: (0, 0)),
                  pl.BlockSpec((D, tn), lambda i, j: (0, j))],
        out_specs=[pl.BlockSpec((tm, D), lambda i, j: (i, 0)), pl.BlockSpec((tm, tn), lambda i, j: (i, j))],
        out_shape=[jax.ShapeDtypeStruct((LP, D), BF16), jax.ShapeDtypeStruct((LP, n), out_dtype)],
        scratch_shapes=[pltpu.VMEM((tm, D), BF16)],
        compiler_params=_cparams("arbitrary", "arbitrary"),
    )(x, w, b)


def _matmul_resid_norm(a, b, h, w, name, target=None):
    k = a.shape[1]
    has_loss = target is not None

    def body(a_ref, b_ref, h_ref, w_ref, *refs):
        m = jnp.dot(a_ref[...].astype(BF16), b_ref[...].astype(BF16), preferred_element_type=F32)
        r = lax.rsqrt(jnp.mean(m * m, axis=-1, keepdims=True) + EPS)
        i = pl.program_id(0)
        row = i * TM + lax.broadcasted_iota(jnp.int32, (TM, 1), 0)
        y = h_ref[...] + jnp.where(row >= PAD_ROWS, m * r * w_ref[...], 0.0)
        if not has_loss:
            m_ref, y_ref = refs
            m_ref[...] = m
            y_ref[...] = y
            return
        t_ref, m_ref, dy_ref, loss_ref = refs
        m_ref[...] = m

        @pl.when(i == 0)
        def _():
            loss_ref[...] = jnp.zeros_like(loss_ref)

        diff = jnp.where(row >= CHUNK, y - t_ref[...], 0.0)
        dy_ref[...] = diff * (1.0 / D)
        loss_ref[...] += (0.5 / D) * jnp.sum(diff * diff)

    tile = pl.BlockSpec((TM, D), lambda i: (i, 0))
    shape = jax.ShapeDtypeStruct((LP, D), F32)
    in_specs = [pl.BlockSpec((TM, k), lambda i: (i, 0)), pl.BlockSpec((k, D), lambda i: (0, 0)), tile,
                pl.BlockSpec((1, D), lambda i: (0, 0))]
    if has_loss:
        return pl.pallas_call(
            body, name=name, grid=(LP // TM,),
            in_specs=in_specs + [tile],
            out_specs=[tile, tile, pl.BlockSpec((8, 128), lambda i: (0, 0))],
            out_shape=[shape, shape, jax.ShapeDtypeStruct((8, 128), F32)],
            compiler_params=_cparams("arbitrary"),
        )(a, b, h, w, target)
    return pl.pallas_call(
        body, name=name, grid=(LP // TM,),
        in_specs=in_specs, out_specs=[tile, tile], out_shape=[shape, shape],
        compiler_params=_cparams("parallel"),
    )(a, b, h, w)


def _rmsnorm_bwd_rows(dy, x, w):
    r = lax.rsqrt(jnp.mean(x * x, axis=-1, keepdims=True) + EPS)
    g = dy * w
    dx = r * g - x * (r * r * r * jnp.mean(g * x, axis=-1, keepdims=True))
    return dx, jnp.sum(dy * x * r, axis=0, keepdims=True)


def _matmul_norm_bwd(dz, b, x, w, resid, tk, name, carried=()):
    k = dz.shape[1]
    assert k % tk == 0
    nk = k // tk
    n_rows = LP // TM
    n_carried = len(carried)

    def body(*refs):
        a_ref, b_ref, x_ref, w_ref, r_ref = refs[:5]
        g_refs, refs = refs[5:5 + n_carried], refs[5 + n_carried:]
        dx_ref, dw_ref = refs[:2]
        got_refs, refs = refs[2:2 + n_carried], refs[2 + n_carried:]
        acc, sems = (refs[:1], refs[1:]) if nk > 1 else ((), refs)
        i, kk = pl.program_id(0), pl.program_id(1)
        if n_carried:
            exchange_start, exchange_finish = _exchange_phases(g_refs, got_refs, *sems)
            pl.when((i == 0) & (kk == 0))(exchange_start)

        @pl.when((i == 0) & (kk == 0))
        def _():
            dw_ref[...] = jnp.zeros_like(dw_ref)

        prod = lax.dot_general(a_ref[...].astype(BF16), b_ref[...].astype(BF16), (((1,), (1,)), ((), ())),
                               preferred_element_type=F32)

        def finish(dy):
            dx, dw = _rmsnorm_bwd_rows(dy, x_ref[...], w_ref[...])
            dx_ref[...] = dx + r_ref[...]
            dw_ref[0:1, :] += dw

        if nk == 1:
            finish(prod)
        else:
            acc_ref, = acc

            @pl.when(kk == 0)
            def _():
                acc_ref[...] = prod

            @pl.when((kk > 0) & (kk < nk - 1))
            def _():
                acc_ref[...] += prod

            @pl.when(kk == nk - 1)
            def _():
                finish(acc_ref[...] + prod)

        if n_carried:
            pl.when((i == n_rows - 1) & (kk == nk - 1))(exchange_finish)

    tile = pl.BlockSpec((TM, D), lambda i, kk: (i, 0))
    anywhere = [pl.BlockSpec(memory_space=pl.ANY)] * n_carried
    return pl.pallas_call(
        body, name=name, grid=(n_rows, nk),
        in_specs=[pl.BlockSpec((TM, tk), lambda i, kk: (i, kk)), pl.BlockSpec((D, tk), lambda i, kk: (0, kk)), tile,
                  pl.BlockSpec((1, D), lambda i, kk: (0, 0)), tile] + anywhere,
        out_specs=[tile, pl.BlockSpec((8, D), lambda i, kk: (0, 0))] + anywhere,
        out_shape=[jax.ShapeDtypeStruct((LP, D), F32), jax.ShapeDtypeStruct((8, D), F32)]
        + [jax.ShapeDtypeStruct(g.shape, g.dtype) for g in carried],
        scratch_shapes=([pltpu.VMEM((TM, D), F32)] if nk > 1 else []) + _exchange_sems(n_carried),
        compiler_params=_cparams("arbitrary", "arbitrary"),
    )(dz, b, x, w, resid, *carried)


def _norm_bwd_matmul(dh, x, w, b, out_dtype, name):
    n = b.shape[0]

    def body(dh_ref, x_ref, w_ref, b_ref, o_ref, dx_ref, dw_ref):
        i = pl.program_id(0)

        @pl.when(i == 0)
        def _():
            dw_ref[...] = jnp.zeros_like(dw_ref)

        row = i * TM + lax.broadcasted_iota(jnp.int32, (TM, 1), 0)
        dy = jnp.where(row >= PAD_ROWS, dh_ref[...], 0.0)
        dx, dw = _rmsnorm_bwd_rows(dy, x_ref[...], w_ref[...])
        dxb = dx.astype(BF16)
        dx_ref[...] = dxb
        dw_ref[0:1, :] += dw
        o_ref[...] = lax.dot_general(dxb, b_ref[...].astype(BF16), (((1,), (1,)), ((), ())),
                                     preferred_element_type=F32).astype(out_dtype)

    tile = pl.BlockSpec((TM, D), lambda i: (i, 0))
    return pl.pallas_call(
        body, name=name, grid=(LP // TM,),
        in_specs=[tile, tile, pl.BlockSpec((1, D), lambda i: (0, 0)), pl.BlockSpec((n, D), lambda i: (0, 0))],
        out_specs=[pl.BlockSpec((TM, n), lambda i: (i, 0)), tile, pl.BlockSpec((8, D), lambda i: (0, 0))],
        out_shape=[jax.ShapeDtypeStruct((LP, n), out_dtype), jax.ShapeDtypeStruct((LP, D), BF16),
                   jax.ShapeDtypeStruct((8, D), F32)],
        compiler_params=_cparams("arbitrary"),
    )(dh, x, w, b)


GELU_C = math.sqrt(2.0 / math.pi)
GELU_K = 0.044715
STRIP = 16
HALF = 8


def _gelu_half(a):
    return 0.5 * jnp.tanh(a * (a * a * (GELU_C * GELU_K) + GELU_C)) + 0.5


def _gelu_slope(a, h):
    return h * (1.0 + (a - a * h) * (a * a * (6.0 * GELU_C * GELU_K) + 2.0 * GELU_C))


def _shift_down(x, prev8):
    row = lax.broadcasted_iota(jnp.int32, (8, 1), 0)
    r1, r2 = pltpu.roll(x, 1, 0), pltpu.roll(x, 2, 0)
    top1 = jnp.where(row < 1, pltpu.roll(prev8, 1, 0), r1[0:8, :])
    top2 = jnp.where(row < 2, pltpu.roll(prev8, 2, 0), r2[0:8, :])
    return jnp.concatenate([top1, r1[8:, :]], axis=0), jnp.concatenate([top2, r2[8:, :]], axis=0)


def _conv_act_fwd(u, cw8, name, carried=()):
    n_rows = LP // TM
    cb2 = 2 * CONV_BLOCK
    n_carried = len(carried)

    def body(*refs):
        u_ref, cw_ref = refs[:2]
        x_refs, refs = refs[2:2 + n_carried], refs[2 + n_carried:]
        conv_ref, act_ref = refs[:2]
        gathered_refs, refs = refs[2:2 + n_carried], refs[2 + n_carried:]
        carry_ref = refs[0]
        j, i = pl.program_id(0), pl.program_id(1)
        if n_carried:
            start, forward, finish = _gather_phases(x_refs, gathered_refs, *refs[1:])
            pl.when((j == 0) & (i == 0))(start)
            pl.when((j == (3 * N_CONV_BLOCKS) // 4) & (i == 0))(forward)

        @pl.when(i == 0)
        def _():
            carry_ref[...] = jnp.zeros_like(carry_ref)

        x = u_ref[...].astype(F32)
        x1, x2 = _shift_down(x, carry_ref[...])
        conv = cw_ref[3:4, :] + x2 * cw_ref[0:1, :] + x1 * cw_ref[1:2, :] + x * cw_ref[2:3, :]
        conv_ref[...] = conv.astype(BF16)
        a = conv[:, :CONV_BLOCK]
        g = conv[:, CONV_BLOCK:]
        act_ref[...] = (a * _gelu_half(a) * g).astype(BF16)
        carry_ref[...] = x[TM - 8:TM, :]
        if n_carried:
            pl.when((j == N_CONV_BLOCKS - 1) & (i == n_rows - 1))(finish)

    anywhere = [pl.BlockSpec(memory_space=pl.ANY)] * n_carried
    return pl.pallas_call(
        body, name=name, grid=(N_CONV_BLOCKS, n_rows),
        in_specs=[pl.BlockSpec((TM, cb2), lambda j, i: (i, j)), pl.BlockSpec((8, cb2), lambda j, i: (0, j))] + anywhere,
        out_specs=[pl.BlockSpec((TM, cb2), lambda j, i: (i, j)),
                   pl.BlockSpec((TM, CONV_BLOCK), lambda j, i: (i, j))] + anywhere,
        out_shape=[jax.ShapeDtypeStruct((LP, D_UP), BF16), jax.ShapeDtypeStruct((LP, D_FF), BF16)]
        + _gathered_shapes(carried),
        scratch_shapes=[pltpu.VMEM((8, cb2), F32)] + _exchange_sems(n_carried),
        compiler_params=_cparams("arbitrary", "arbitrary"),
    )(u, cw8, *carried)


def _conv_act_bwd(dact, conv, u, cw8, name):
    n_rows = LP // TM
    cb2 = 2 * CONV_BLOCK
    n_strips = TM // STRIP

    def body(dact_ref, conv_ref, u_ref, cw_ref, du_ref, dcw_ref, carry_ref):
        i = pl.program_id(1)

        @pl.when(i == 0)
        def _():
            dcw_ref[...] = jnp.zeros_like(dcw_ref)
            carry_ref[...] = jnp.zeros_like(carry_ref)

        w0, w1, w2 = cw_ref[0:1, :], cw_ref[1:2, :], cw_ref[2:3, :]
        row = lax.broadcasted_iota(jnp.int32, (HALF, 1), 0)

        def strip(k, carry):
            n1, n2, s0, s1, s2, s3 = carry
            r0 = pl.multiple_of((n_strips - 1 - k) * STRIP, STRIP)
            cv = conv_ref[pl.ds(r0, STRIP), :].astype(F32)
            dav = dact_ref[pl.ds(r0, STRIP), :].astype(F32)
            x = u_ref[pl.ds(r0, STRIP), :].astype(F32)
            du = [None, None]
            for half in (1, 0):
                rows = slice(HALF * half, HALF * (half + 1))
                a, g, dah = cv[rows, :CONV_BLOCK], cv[rows, CONV_BLOCK:], dav[rows]
                h = _gelu_half(a)
                dconv = jnp.concatenate([dah * g * _gelu_slope(a, h), dah * (a * h)], axis=1)
                u1, u2 = pltpu.roll(dconv, HALF - 1, 0), pltpu.roll(dconv, HALF - 2, 0)
                d1 = jnp.where(row >= HALF - 1, n1, u1)
                d2 = jnp.where(row >= HALF - 2, n2, u2)
                du[half] = dconv * w2 + d1 * w1 + d2 * w0
                s0, s1, s2, s3 = s0 + d2 * x[rows], s1 + d1 * x[rows], s2 + dconv * x[rows], s3 + dconv
                n1, n2 = u1, u2
            du_ref[pl.ds(r0, STRIP), :] = jnp.concatenate(du, axis=0).astype(BF16)
            return n1, n2, s0, s1, s2, s3

        below = carry_ref[...]
        zero = jnp.zeros((HALF, cb2), F32)
        init = (pltpu.roll(below, HALF - 1, 0), pltpu.roll(below, HALF - 2, 0), zero, zero, zero, zero)
        u1, _, s0, s1, s2, s3 = lax.fori_loop(0, n_strips, strip, init, unroll=2)
        carry_ref[...] = pltpu.roll(u1, 1, 0)
        dcw_ref[0:1, :] += jnp.sum(s0, axis=0, keepdims=True)
        dcw_ref[1:2, :] += jnp.sum(s1, axis=0, keepdims=True)
        dcw_ref[2:3, :] += jnp.sum(s2, axis=0, keepdims=True)
        dcw_ref[3:4, :] += jnp.sum(s3, axis=0, keepdims=True)

    rev = lambda j, i: (n_rows - 1 - i, j)
    return pl.pallas_call(
        body, name=name, grid=(N_CONV_BLOCKS, n_rows),
        in_specs=[pl.BlockSpec((TM, CONV_BLOCK), rev), pl.BlockSpec((TM, cb2), rev), pl.BlockSpec((TM, cb2), rev),
                  pl.BlockSpec((8, cb2), lambda j, i: (0, j))],
        out_specs=[pl.BlockSpec((TM, cb2), rev), pl.BlockSpec((8, cb2), lambda j, i: (0, j))],
        out_shape=[jax.ShapeDtypeStruct((LP, D_UP), BF16), jax.ShapeDtypeStruct((8, D_UP), F32)],
        scratch_shapes=[pltpu.VMEM((HALF, cb2), F32)],
        compiler_params=_cparams("arbitrary", "arbitrary"),
    )(dact, conv, u, cw8)


CHUNKS_PER_STEP = 3 if N_CHUNKS % 3 == 0 else 1
STEP_ROWS = CHUNKS_PER_STEP * CHUNK
N_STEPS = N_CHUNKS // CHUNKS_PER_STEP


def _ret_consts(h):
    rows = STEP_ROWS
    lg = math.log(1.0 - 2.0 ** (-5.0 - h))
    ri = lax.broadcasted_iota(jnp.int32, (rows, rows), 0)
    ci = lax.broadcasted_iota(jnp.int32, (rows, rows), 1)
    diff = (ri - ci).astype(F32)
    dmat = jnp.where(diff >= 0, jnp.exp(lg * jnp.maximum(diff, 0.0)), 0.0)
    rowf = lax.broadcasted_iota(jnp.int32, (rows, 1), 0).astype(F32)
    zeta = jnp.exp(lg * (rows - 1.0 - rowf))
    xi = jnp.exp(lg * (rowf + 1.0))
    return dmat, zeta, xi, math.exp(lg * rows)


def _rope(t, cosv, sinv):
    return t * cosv + pltpu.roll(t, RET_DK // 2, 1) * sinv


def _unrope(d, cosv, sinv):
    return d * cosv + pltpu.roll(d * sinv, RET_DK // 2, 1)


def _gla_masks():
    ri = lax.broadcasted_iota(jnp.int32, (CHUNK, CHUNK), 0)
    ci = lax.broadcasted_iota(jnp.int32, (CHUNK, CHUNK), 1)
    return dict(ri=ri, ci=ci, tril=(ri >= ci).astype(F32), heads=_head_block_mask(), own=_state_block_mask())


def _gla_common(p_ref, w2_ref, gb_ref, chunk, rows, masks):
    row = lax.broadcasted_iota(jnp.int32, (CHUNK, 1), 0)
    real = (chunk * CHUNK + row) >= PAD_ROWS
    ga = p_ref[rows, O_GA:O_GA + 128]
    z = _dot(ga, w2_ref[...]) + gb_ref[...]
    la = (jnp.minimum(z, 0.0) - jnp.log(1.0 + jnp.exp(-jnp.abs(z)))) * (1.0 / GLA_TAU)
    la = jnp.where(real, la, 0.0)
    ri, ci = masks["ri"], masks["ci"]
    cum = _dot_exact_rhs(masks["tril"], la)
    last = cum[CHUNK - 1:CHUNK, :]
    qs = p_ref[rows, O_GQ:O_GQ + 256] * (GLA_DK ** -0.5)
    k = p_ref[rows, O_GK:O_GK + 256]
    ecum = jnp.exp(cum)
    ekl = jnp.exp(last - cum)
    el = jnp.exp(last)
    refs = [jnp.zeros((1, 256), F32)] + [cum[a * SUB - 1:a * SUB, :] for a in range(1, N_SUB)]
    eq = [jnp.exp(cum[a * SUB:(a + 1) * SUB, :] - refs[a]) for a in range(N_SUB)]
    spread = refs[0] - cum[SUB - 1:SUB, :]
    for a in range(1, N_SUB):
        spread = jnp.maximum(spread, refs[a] - cum[(a + 1) * SUB - 1:(a + 1) * SUB, :])
    small = jnp.max(spread) <= GLA_FACTORED_MAX
    return dict(real=real, row=row, z=z, la=la, cum=cum, last=last, qs=qs, k=k, ecum=ecum, ekl=ekl, el=el,
                refs=refs, eq=eq, small=small, ri=ri, ci=ci, masks=masks)


GLA_FACTORED_MAX = 40.0


def _head_block_mask():
    r = lax.broadcasted_iota(jnp.int32, (CHUNK, 256), 0)
    col = lax.broadcasted_iota(jnp.int32, (CHUNK, 256), 1)
    return (r // SUB) == (col // GLA_DK)


def _state_block_mask():
    r = lax.broadcasted_iota(jnp.int32, (GLA_HEADS * GLA_DK, GLA_HEADS * GLA_DV), 0)
    col = lax.broadcasted_iota(jnp.int32, (GLA_HEADS * GLA_DK, GLA_HEADS * GLA_DV), 1)
    return (r // GLA_DK) == (col // GLA_DV)


def _block_diagonal(blocks):
    zero = jnp.zeros((GLA_DK, GLA_DV), F32)
    return jnp.concatenate([jnp.concatenate([blocks[h] if g == h else zero for g in range(GLA_HEADS)], axis=1)
                            for h in range(GLA_HEADS)], axis=0)


def _gla_factored(c):
    mask = c["masks"]["heads"]
    eks, keys, queries = [], [], []
    for a in range(N_SUB):
        ek = jnp.exp(jnp.minimum(c["refs"][a] - c["cum"], GLA_FACTORED_MAX))
        qh = c["qs"][a * SUB:(a + 1) * SUB, :] * c["eq"][a]
        eks.append(ek)
        keys.append(c["k"] * ek)
        queries.append(jnp.where(mask, jnp.concatenate([qh] * GLA_HEADS, axis=0), 0.0))
    return eks, keys, queries


def _gla_scores_factored(c, factored, p_scr):
    _, keys, queries = factored
    for a in range(N_SUB):
        out = _dot_nt(queries[a], keys[a])
        out = jnp.where(c["ci"] <= a * SUB + (c["ri"] & (SUB - 1)), out, 0.0)
        for h in range(GLA_HEADS):
            p_scr[h, a * SUB:(a + 1) * SUB, :] = out[h * SUB:(h + 1) * SUB, :]


def _gla_intra_bwd_factored(c, factored, dps, dq_scr, dk_scr):
    eks, keys, queries = factored
    mask = c["masks"]["heads"]
    dk = jnp.zeros((CHUNK, 256), F32)
    for a in range(N_SUB):
        dpa = jnp.concatenate([dps[h][a * SUB:(a + 1) * SUB, :] for h in range(GLA_HEADS)], axis=0)
        dq = jnp.where(mask, _dot(dpa, keys[a]), 0.0)
        dq = dq[0:SUB] + dq[SUB:2 * SUB] + dq[2 * SUB:3 * SUB] + dq[3 * SUB:4 * SUB]
        dq_scr[a * SUB:(a + 1) * SUB, :] = dq * c["eq"][a]
        dk = dk + _dot_tn(dpa, queries[a]) * eks[a]
    dk_scr[...] = dk


def _gla_lag_weights(c):
    cum, row = c["cum"], c["row"]
    out = [jnp.ones((CHUNK, 256), F32)]
    for r in range(1, SUB):
        out.append(jnp.where((row % SUB) >= r, jnp.exp(jnp.minimum(cum - pltpu.roll(cum, r, 0), 0.0)), 0.0))
    return out


def _gla_pairwise_keys(c):
    return [None] + [c["k"] * jnp.exp(jnp.minimum(c["refs"][a] - c["cum"], 0.0)) for a in range(1, N_SUB)]


def _gla_scores_pairwise(c, lag_w, keys, h):
    sl = slice(GLA_DK * h, GLA_DK * (h + 1))
    qs, k = c["qs"][:, sl], c["k"][:, sl]
    ri, ci = c["ri"], c["ci"]
    p = jnp.zeros((CHUNK, CHUNK), F32)
    for r in range(SUB):
        kr = k if r == 0 else pltpu.roll(k, r, 0)
        pr = jnp.sum(qs * kr * lag_w[r][:, sl], axis=1, keepdims=True)
        p = p + jnp.where(ci == ri - r, pr, 0.0)
    blocks = [jnp.zeros((SUB, CHUNK), F32)]
    for a in range(1, N_SUB):
        qh = qs[a * SUB:(a + 1) * SUB, :] * c["eq"][a][:, sl]
        blocks.append(jnp.where(ci[:SUB, :] < a * SUB, _dot_nt(qh, keys[a][:, sl]), 0.0))
    return p + jnp.concatenate(blocks, axis=0)


def _gla_all_scores(c, p_scr, factored):
    if factored:
        _gla_scores_factored(c, _gla_factored(c), p_scr)
    else:
        lag_w, keys = _gla_lag_weights(c), _gla_pairwise_keys(c)
        for h in range(GLA_HEADS):
            p_scr[h] = _gla_scores_pairwise(c, lag_w, keys, h)


def _either_form(chunks, run):
    small = chunks[0]["small"]
    for c in chunks[1:]:
        small = jnp.logical_and(small, c["small"])
    pl.when(small)(lambda: run(True))
    pl.when(jnp.logical_not(small))(lambda: run(False))


def _gla_intra_bwd_pairwise(c, lag_w, keys, dp, h):
    sl = slice(GLA_DK * h, GLA_DK * (h + 1))
    qs_h, k_h = c["qs"][:, sl], c["k"][:, sl]
    ri, ci = c["ri"], c["ci"]
    dq_rows = [jnp.zeros((SUB, GLA_DK), F32)]
    dk = jnp.zeros((CHUNK, GLA_DK), F32)
    for a in range(1, N_SUB):
        eq = c["eq"][a][:, sl]
        qh = qs_h[a * SUB:(a + 1) * SUB, :] * eq
        dpa = jnp.where(ci[:SUB, :] < a * SUB, dp[a * SUB:(a + 1) * SUB, :], 0.0)
        dq_rows.append(_dot(dpa, keys[a][:, sl]) * eq)
        ek = jnp.exp(jnp.minimum(c["refs"][a][:, sl] - c["cum"][:, sl], 0.0))
        dk = dk + _dot_tn(dpa, qh) * ek
    dq = jnp.concatenate(dq_rows, axis=0)
    for r in range(SUB):
        w = lag_w[r][:, sl]
        dpr = jnp.sum(jnp.where(ci == ri - r, dp, 0.0), axis=1, keepdims=True)
        kr = k_h if r == 0 else pltpu.roll(k_h, r, 0)
        dq = dq + dpr * kr * w
        back = dpr * qs_h * w
        dk = dk + (back if r == 0 else pltpu.roll(back, CHUNK - r, 0))
    return dq, dk


def _gla_all_intra_bwd(c, dps, p_scr, dq_scr, dk_scr, factored):
    if factored:
        terms = _gla_factored(c)
        _gla_scores_factored(c, terms, p_scr)
        _gla_intra_bwd_factored(c, terms, dps, dq_scr, dk_scr)
    else:
        lag_w, keys = _gla_lag_weights(c), _gla_pairwise_keys(c)
        outs = [_gla_intra_bwd_pairwise(c, lag_w, keys, dps[h], h) for h in range(GLA_HEADS)]
        for h in range(GLA_HEADS):
            p_scr[h] = _gla_scores_pairwise(c, lag_w, keys, h)
        dq_scr[...] = jnp.concatenate([o[0] for o in outs], axis=1)
        dk_scr[...] = jnp.concatenate([o[1] for o in outs], axis=1)


def _mixer_fwd(proj, cos2, sin2, w2p, gb, rnw, gnw, name, carried=()):
    n_carried = len(carried)

    def body(*refs):
        p_ref, c_ref, s_ref, w2_ref, gb_ref, rnw_ref, gnw_ref = refs[:7]
        x_refs, refs = refs[7:7 + n_carried], refs[7 + n_carried:]
        ocat_ref, mrg_ref, sr_out, sg_out = refs[:4]
        gathered_refs, refs = refs[4:4 + n_carried], refs[4 + n_carried:]
        sr, sg, p_scr = refs[:3]
        n = pl.program_id(0)
        if n_carried:
            start, forward, finish = _gather_phases(x_refs, gathered_refs, *refs[3:])
            pl.when(n == 0)(start)
            pl.when(n == (3 * N_STEPS) // 4)(forward)

        @pl.when(n == 0)
        def _():
            sr[...] = jnp.zeros_like(sr)
            sg[...] = jnp.zeros_like(sg)

        sr_out[0] = sr[...]
        cosv, sinv = c_ref[...], s_ref[...]

        for h in range(RET_HEADS):
            dmat, zeta, xi, gc = _ret_consts(h)
            hs = slice(128 * h, 128 * (h + 1))
            q = _rope(p_ref[:, O_RQ + 128 * h:O_RQ + 128 * (h + 1)], cosv, sinv)
            k = _rope(p_ref[:, O_RK + 128 * h:O_RK + 128 * (h + 1)], cosv, sinv) * (RET_DK ** -0.5)
            v = p_ref[:, O_RV + 128 * h:O_RV + 128 * (h + 1)]
            g = p_ref[:, O_RG + 128 * h:O_RG + 128 * (h + 1)]
            s_in = sr[h]
            a = _dot_nt(q, k) * dmat
            o = _dot(a, v) + _dot(q, s_in) * xi
            sr[h] = gc * s_in + _dot_tn(k * zeta, v)
            mu = jnp.mean(o, axis=-1, keepdims=True)
            xc = o - mu
            nrm = xc * lax.rsqrt(jnp.mean(xc * xc, axis=-1, keepdims=True) + EPS)
            ocat_ref[:, hs] = o
            mrg_ref[:, hs] = (nrm * rnw_ref[:, hs] * (g * _sigmoid(g))).astype(BF16)

        row_slices = [slice(CHUNK * j, CHUNK * (j + 1)) for j in range(CHUNKS_PER_STEP)]
        masks = _gla_masks()
        chunks = [_gla_common(p_ref, w2_ref, gb_ref, n * CHUNKS_PER_STEP + j, rows, masks)
                  for j, rows in enumerate(row_slices)]

        def gla_chunks(factored):
            own = masks["own"]
            for j, (rows, c) in enumerate(zip(row_slices, chunks)):
                s_in = sg[...]
                for h in range(GLA_HEADS):
                    sg_out[j, h] = s_in[GLA_DK * h:GLA_DK * (h + 1), GLA_DV * h:GLA_DV * (h + 1)]
                _gla_all_scores(c, p_scr.at[j], factored)
                v_all = p_ref[rows, O_GV:O_GV + GLA_HEADS * GLA_DV]
                o_inter = _dot(c["qs"] * c["ecum"], s_in)
                decay = jnp.exp(_dot_tn_exact_lhs(c["la"], jnp.ones((CHUNK, GLA_HEADS * GLA_DV), F32)))
                sg[...] = decay * s_in + jnp.where(own, _dot_tn(c["k"] * c["ekl"], v_all), 0.0)
                o_intra = _dot(p_scr[j].reshape(GLA_HEADS * CHUNK, CHUNK), v_all)
                for h in range(GLA_HEADS):
                    hs = slice(512 + 128 * h, 512 + 128 * (h + 1))
                    g = p_ref[rows, O_GR + 128 * h:O_GR + 128 * (h + 1)]
                    o = (o_intra[CHUNK * h:CHUNK * (h + 1), GLA_DV * h:GLA_DV * (h + 1)]
                         + o_inter[:, GLA_DV * h:GLA_DV * (h + 1)])
                    nrm = o * lax.rsqrt(jnp.mean(o * o, axis=-1, keepdims=True) + EPS)
                    ocat_ref[rows, hs] = o
                    mrg_ref[rows, hs] = (nrm * gnw_ref[:, 128 * h:128 * (h + 1)] * (g * _sigmoid(g))).astype(BF16)

        _either_form(chunks, gla_chunks)

        if n_carried:
            pl.when(n == N_STEPS - 1)(finish)

    const = lambda shape: pl.BlockSpec(shape, lambda n: (0,) * len(shape))
    anywhere = [pl.BlockSpec(memory_space=pl.ANY)] * n_carried
    return pl.pallas_call(
        body, name=name, grid=(N_STEPS,),
        in_specs=[pl.BlockSpec((STEP_ROWS, IN_WP), lambda n: (n, 0)),
                  pl.BlockSpec((STEP_ROWS, 128), lambda n: (n, 0)), pl.BlockSpec((STEP_ROWS, 128), lambda n: (n, 0)),
                  const((128, 256)), const((1, 256)), const((1, 512)), const((1, 512))] + anywhere,
        out_specs=[pl.BlockSpec((STEP_ROWS, D), lambda n: (n, 0)), pl.BlockSpec((STEP_ROWS, D), lambda n: (n, 0)),
                   pl.BlockSpec((1, RET_HEADS, RET_DK, 128), lambda n: (n, 0, 0, 0)),
                   pl.BlockSpec((CHUNKS_PER_STEP, GLA_HEADS, GLA_DK, GLA_DV), lambda n: (n, 0, 0, 0))] + anywhere,
        out_shape=[jax.ShapeDtypeStruct((LP, D), F32), jax.ShapeDtypeStruct((LP, D), BF16),
                   jax.ShapeDtypeStruct((N_STEPS, RET_HEADS, RET_DK, 128), F32),
                   jax.ShapeDtypeStruct((N_CHUNKS, GLA_HEADS, GLA_DK, GLA_DV), F32)] + _gathered_shapes(carried),
        scratch_shapes=[pltpu.VMEM((RET_HEADS, RET_DK, 128), F32),
                        pltpu.VMEM((GLA_HEADS * GLA_DK, GLA_HEADS * GLA_DV), F32),
                        pltpu.VMEM((CHUNKS_PER_STEP, GLA_HEADS, CHUNK, CHUNK), F32)] + _exchange_sems(n_carried),
        compiler_params=_cparams("arbitrary"),
    )(proj, cos2, sin2, w2p, gb, rnw, gnw, *carried)


def _mixer_bwd(proj, ocat, dmrg, sr_all, sg_all, cos2, sin2, w2p, gb, rnw, gnw, name, carried=()):
    last_step = N_STEPS - 1
    n_carried = len(carried)

    def body(*refs):
        p_ref, ocat_ref, dm_ref, sr_ref, sg_ref, c_ref, s_ref, w2_ref, gb_ref, rnw_ref, gnw_ref = refs[:11]
        g_refs, refs = refs[11:11 + n_carried], refs[11 + n_carried:]
        dp_ref, dw2_ref, dgb_ref, drn_ref, dgn_ref = refs[:5]
        got_refs, refs = refs[5:5 + n_carried], refs[5 + n_carried:]
        dsr, dsg, p_scr, dq_scr, dk_scr = refs[:5]
        step = pl.program_id(0)
        n = last_step - step
        if n_carried:
            start, finish = _exchange_phases(g_refs, got_refs, *refs[5:])
            pl.when(step == 0)(start)

        @pl.when(step == 0)
        def _():
            dsr[...] = jnp.zeros_like(dsr)
            dsg[...] = jnp.zeros_like(dsg)
            dw2_ref[...] = jnp.zeros_like(dw2_ref)
            dgb_ref[...] = jnp.zeros_like(dgb_ref)
            drn_ref[...] = jnp.zeros_like(drn_ref)
            dgn_ref[...] = jnp.zeros_like(dgn_ref)

        cosv, sinv = c_ref[...], s_ref[...]
        step_row = lax.broadcasted_iota(jnp.int32, (STEP_ROWS, 1), 0)
        real = ((n * STEP_ROWS + step_row) >= PAD_ROWS).astype(F32)

        for h in range(RET_HEADS):
            dmat, zeta, xi, gc = _ret_consts(h)
            hs = slice(128 * h, 128 * (h + 1))
            q = _rope(p_ref[:, O_RQ + 128 * h:O_RQ + 128 * (h + 1)], cosv, sinv)
            k = _rope(p_ref[:, O_RK + 128 * h:O_RK + 128 * (h + 1)], cosv, sinv) * (RET_DK ** -0.5)
            v = p_ref[:, O_RV + 128 * h:O_RV + 128 * (h + 1)]
            g = p_ref[:, O_RG + 128 * h:O_RG + 128 * (h + 1)]
            o = ocat_ref[:, hs]
            dy = dm_ref[:, hs]
            wv = rnw_ref[:, hs]
            mu = jnp.mean(o, axis=-1, keepdims=True)
            xc = o - mu
            rs = lax.rsqrt(jnp.mean(xc * xc, axis=-1, keepdims=True) + EPS)
            nrm = xc * rs
            sgm = _sigmoid(g)
            sil = g * sgm
            drn_ref[0:1, hs] += jnp.sum(dy * nrm * sil, axis=0, keepdims=True)
            dgate = dy * nrm * wv * (sgm * (1.0 + g * (1.0 - sgm)))
            dn = dy * wv * sil
            do = rs * (dn - jnp.mean(dn, axis=-1, keepdims=True) - nrm * jnp.mean(dn * nrm, axis=-1, keepdims=True))
            s_in = sr_ref[0, h]
            ds_out = dsr[h]
            a = _dot_nt(q, k) * dmat
            da = _dot_nt(do, v) * dmat
            dox = do * xi
            dq = _dot(da, k) + _dot_nt(dox, s_in)
            dk = _dot_tn(da, q) + _dot_nt(v, ds_out) * zeta
            dv = _dot_tn(a, do) + _dot(k * zeta, ds_out)
            dsr[h] = gc * ds_out + _dot_tn(q, dox)
            dk = dk * (RET_DK ** -0.5)
            dp_ref[:, O_RQ + 128 * h:O_RQ + 128 * (h + 1)] = (_unrope(dq, cosv, sinv) * real).astype(BF16)
            dp_ref[:, O_RK + 128 * h:O_RK + 128 * (h + 1)] = (_unrope(dk, cosv, sinv) * real).astype(BF16)
            dp_ref[:, O_RV + 128 * h:O_RV + 128 * (h + 1)] = (dv * real).astype(BF16)
            dp_ref[:, O_RG + 128 * h:O_RG + 128 * (h + 1)] = (dgate * real).astype(BF16)

        row_slices = [slice(CHUNK * j, CHUNK * (j + 1)) for j in range(CHUNKS_PER_STEP)]
        masks = _gla_masks()
        chunks = [_gla_common(p_ref, w2_ref, gb_ref, n * CHUNKS_PER_STEP + j, rows, masks)
                  for j, rows in enumerate(row_slices)]

        def gla_chunks(factored):
            for j in reversed(range(CHUNKS_PER_STEP)):
                gla_chunk_bwd(chunks[j], n * CHUNKS_PER_STEP + j, row_slices[j], j, factored, p_ref, ocat_ref, dm_ref,
                              sg_ref, w2_ref, gnw_ref, dp_ref, dw2_ref, dgb_ref, dgn_ref, dsg, p_scr, dq_scr, dk_scr)

        _either_form(chunks, gla_chunks)
        if n_carried:
            pl.when(step == last_step)(finish)

    def gla_chunk_bwd(c, chunk, rows, j, factored, p_ref, ocat_ref, dm_ref, sg_ref, w2_ref, gnw_ref,
                      dp_ref, dw2_ref, dgb_ref, dgn_ref, dsg, p_scr, dq_scr, dk_scr):
        row = lax.broadcasted_iota(jnp.int32, (CHUNK, 1), 0)
        real = ((chunk * CHUNK + row) >= PAD_ROWS).astype(F32)
        ri, ci = c["ri"], c["ci"]
        causal = ri >= ci
        triu = (ci >= ri).astype(F32)
        qe = c["qs"] * c["ecum"]
        kl = c["k"] * c["ekl"]
        v_all = p_ref[rows, O_GV:O_GV + GLA_HEADS * GLA_DV]
        dos, dps = [], []
        for h in range(GLA_HEADS):
            hs = slice(512 + 128 * h, 512 + 128 * (h + 1))
            g = p_ref[rows, O_GR + 128 * h:O_GR + 128 * (h + 1)]
            o = ocat_ref[rows, hs]
            dy = dm_ref[rows, hs]
            wv = gnw_ref[:, 128 * h:128 * (h + 1)]
            rs = lax.rsqrt(jnp.mean(o * o, axis=-1, keepdims=True) + EPS)
            nrm = o * rs
            sgm = _sigmoid(g)
            sil = g * sgm
            dgn_ref[0:1, 128 * h:128 * (h + 1)] += jnp.sum(dy * nrm * sil, axis=0, keepdims=True)
            dgate = dy * nrm * wv * (sgm * (1.0 + g * (1.0 - sgm)))
            dn = dy * wv * sil
            do = rs * (dn - nrm * jnp.mean(dn * nrm, axis=-1, keepdims=True))
            dp_ref[rows, O_GR + 128 * h:O_GR + 128 * (h + 1)] = (dgate * real).astype(BF16)
            dos.append(do)
        do_all = jnp.concatenate(dos, axis=1)
        do_blocks = jnp.where(c["masks"]["own"], jnp.concatenate([do_all] * GLA_HEADS, axis=0), 0.0)
        dp_all = _dot_nt(do_blocks, v_all)
        dps = [jnp.where(causal, dp_all[CHUNK * h:CHUNK * (h + 1), :], 0.0) for h in range(GLA_HEADS)]
        _gla_all_intra_bwd(c, dps, p_scr.at[j], dq_scr.at[j], dk_scr.at[j], factored)
        s_in = _block_diagonal([sg_ref[j, h] for h in range(GLA_HEADS)])
        ds_out = dsg[...]
        decay = jnp.exp(_dot_tn_exact_lhs(c["la"], jnp.ones((CHUNK, GLA_HEADS * GLA_DV), F32)))
        dv_state = _dot(kl, ds_out)
        dqe = _dot_nt(do_all, s_in)
        dkl = _dot_nt(v_all, ds_out)
        dsg[...] = jnp.where(c["masks"]["own"], _dot_tn(qe, do_all), 0.0) + decay * ds_out
        sd = s_in * ds_out
        sd_hi = sd.astype(BF16)
        sd_lo = (sd - sd_hi.astype(F32)).astype(BF16)
        ones8 = jnp.ones((8, GLA_HEADS * GLA_DV), BF16)
        nt = (((1,), (1,)), ((), ()))
        d_el = (lax.dot_general(ones8, sd_hi, nt, preferred_element_type=F32)
                + lax.dot_general(ones8, sd_lo, nt, preferred_element_type=F32))[0:1, :]
        dqs = dqe * c["ecum"] + dq_scr[j]
        dkk = dkl * c["ekl"] + dk_scr[j]
        d_last = jnp.sum(dkl * kl, axis=0, keepdims=True) + d_el * c["el"]
        dcum = c["qs"] * dqs - c["k"] * dkk + jnp.where(row == CHUNK - 1, d_last, 0.0)
        dla = _dot_exact_rhs(triu, dcum)
        dv = _dot_tn(p_scr[j].reshape(GLA_HEADS * CHUNK, CHUNK), do_blocks) + dv_state
        dp_ref[rows, O_GV:O_GV + GLA_HEADS * GLA_DV] = (dv * real).astype(BF16)
        dp_ref[rows, O_GQ:O_GQ + 256] = (dqs * (GLA_DK ** -0.5) * real).astype(BF16)
        dp_ref[rows, O_GK:O_GK + 256] = (dkk * real).astype(BF16)
        dz = dla * (1.0 / GLA_TAU) * _sigmoid(-c["z"]) * real
        ga = p_ref[rows, O_GA:O_GA + 128]
        dp_ref[rows, O_GA:O_GA + 128] = _dot_nt(dz, w2_ref[...]).astype(BF16)
        dp_ref[rows, O_GA + 128:IN_WP] = jnp.zeros((CHUNK, IN_WP - O_GA - 128), BF16)
        dw2_ref[...] += _dot_tn(ga, dz)
        dgb_ref[0:1, :] += jnp.sum(dz, axis=0, keepdims=True)

    const = lambda shape: pl.BlockSpec(shape, lambda s: (0,) * len(shape))
    rev = lambda s: (last_step - s, 0)
    anywhere = [pl.BlockSpec(memory_space=pl.ANY)] * n_carried
    return pl.pallas_call(
        body, name=name, grid=(N_STEPS,),
        in_specs=[pl.BlockSpec((STEP_ROWS, IN_WP), rev), pl.BlockSpec((STEP_ROWS, D), rev),
                  pl.BlockSpec((STEP_ROWS, D), rev),
                  pl.BlockSpec((1, RET_HEADS, RET_DK, 128), lambda s: (last_step - s, 0, 0, 0)),
                  pl.BlockSpec((CHUNKS_PER_STEP, GLA_HEADS, GLA_DK, GLA_DV), lambda s: (last_step - s, 0, 0, 0)),
                  pl.BlockSpec((STEP_ROWS, 128), rev), pl.BlockSpec((STEP_ROWS, 128), rev),
                  const((128, 256)), const((1, 256)), const((1, 512)), const((1, 512))] + anywhere,
        out_specs=[pl.BlockSpec((STEP_ROWS, IN_WP), rev), const((128, 256)), const((8, 256)),
                   const((8, 512)), const((8, 512))] + anywhere,
        out_shape=[jax.ShapeDtypeStruct((LP, IN_WP), BF16), jax.ShapeDtypeStruct((128, 256), F32),
                   jax.ShapeDtypeStruct((8, 256), F32), jax.ShapeDtypeStruct((8, 512), F32),
                   jax.ShapeDtypeStruct((8, 512), F32)] + [jax.ShapeDtypeStruct(g.shape, g.dtype) for g in carried],
        scratch_shapes=[pltpu.VMEM((RET_HEADS, RET_DK, 128), F32),
                        pltpu.VMEM((GLA_HEADS * GLA_DK, GLA_HEADS * GLA_DV), F32),
                        pltpu.VMEM((CHUNKS_PER_STEP, GLA_HEADS, CHUNK, CHUNK), F32),
                        pltpu.VMEM((CHUNKS_PER_STEP, CHUNK, 256), F32),
                        pltpu.VMEM((CHUNKS_PER_STEP, CHUNK, 256), F32)] + _exchange_sems(n_carried),
        compiler_params=_cparams("arbitrary"),
    )(proj, ocat, dmrg, sr_all, sg_all, cos2, sin2, w2p, gb, rnw, gnw, *carried)


def _all_gather(xs, name):
    n = len(xs)

    def body(*refs):
        start, forward, finish = _gather_phases(refs[:n], refs[n:2 * n], *refs[2 * n:])
        start()
        forward()
        finish()

    return pl.pallas_call(
        body, name=name,
        in_specs=[pl.BlockSpec(memory_space=pl.ANY)] * n,
        out_specs=[pl.BlockSpec(memory_space=pl.ANY)] * n,
        out_shape=_gathered_shapes(xs),
        scratch_shapes=_exchange_sems(n),
    )(*xs)


def _gathered_shapes(xs):
    return [jax.ShapeDtypeStruct((N_DEV,) + x.shape, x.dtype) for x in xs]


def _exchange_sems(n):
    if n == 0:
        return []
    return [pltpu.SemaphoreType.DMA((7 * n,)), pltpu.SemaphoreType.DMA((7 * n,)), pltpu.SemaphoreType.DMA((n,))]


def _gather_phases(x_refs, out_refs, send_sems, recv_sems, local_sems):
    n = len(x_refs)
    mx, my, mc = lax.axis_index("x"), lax.axis_index("y"), lax.axis_index("c")
    me, sibling = (mx, my, mc), (mx, my, 1 - mc)
    chips = [(1 - mx, my), (mx, 1 - my), (1 - mx, 1 - my)]

    def slot(a, px, py, pc):
        return out_refs[a].at[4 * px + 2 * py + pc]

    def copy(a, k, block, to, src=None):
        return pltpu.make_async_remote_copy(
            src_ref=slot(a, *block) if src is None else src, dst_ref=slot(a, *block),
            send_sem=send_sems.at[7 * a + k], recv_sem=recv_sems.at[7 * a + k],
            device_id=to, device_id_type=MESH_IDS)

    mine = [pltpu.make_async_copy(x_refs[a], slot(a, *me), local_sems.at[a]) for a in range(n)]
    first = []
    for a in range(n):
        first.append(copy(a, 0, me, sibling, src=x_refs[a]))
        first += [copy(a, 1 + j, me, (*chip, mc), src=x_refs[a]) for j, chip in enumerate(chips)]
    passed = [copy(a, 4 + j, (*chip, mc), sibling) for j, chip in enumerate(chips) for a in range(n)]

    def start():
        for cp in mine + first:
            cp.start()

    def forward():
        for j, chip in enumerate(chips):
            for a in range(n):
                copy(a, 1 + j, (*chip, mc), me).wait_recv()
                passed[j * n + a].start()

    def finish():
        for a in range(n):
            copy(a, 0, sibling, me).wait_recv()
            for j, chip in enumerate(chips):
                copy(a, 4 + j, (*chip, 1 - mc), me).wait_recv()
        for cp in first + passed:
            cp.wait_send()
        for cp in mine:
            cp.wait()

    return start, forward, finish


def _exchange_blocks(gs, name):
    n = len(gs)

    def body(*refs):
        start, finish = _exchange_phases(refs[:n], refs[n:2 * n], *refs[2 * n:])
        start()
        finish()

    return pl.pallas_call(
        body, name=name,
        in_specs=[pl.BlockSpec(memory_space=pl.ANY)] * n,
        out_specs=[pl.BlockSpec(memory_space=pl.ANY)] * n,
        out_shape=[jax.ShapeDtypeStruct(g.shape, g.dtype) for g in gs],
        scratch_shapes=_exchange_sems(n),
    )(*gs)


def _exchange_phases(g_refs, out_refs, send_sems, recv_sems, local_sems):
    n = len(g_refs)
    mx, my, mc = lax.axis_index("x"), lax.axis_index("y"), lax.axis_index("c")
    me = 4 * mx + 2 * my + mc
    mine = [pltpu.make_async_copy(g_refs[a].at[me], out_refs[a].at[me], local_sems.at[a]) for a in range(n)]
    copies = []
    for r in range(1, N_DEV):
        px, py, pc = mx ^ (r >> 2), my ^ ((r >> 1) & 1), mc ^ (r & 1)
        peer = 4 * px + 2 * py + pc
        for a in range(n):
            copies.append(pltpu.make_async_remote_copy(
                src_ref=g_refs[a].at[peer], dst_ref=out_refs[a].at[me],
                send_sem=send_sems.at[7 * a + r - 1], recv_sem=recv_sems.at[7 * a + r - 1],
                device_id=(px, py, pc), device_id_type=MESH_IDS))

    def start():
        for cp in mine + copies:
            cp.start()

    def finish():
        for cp in copies:
            cp.wait_recv()
        for cp in copies:
            cp.wait_send()
        for cp in mine:
            cp.wait()

    return start, finish


IN_SHARD = IN_W // N_DEV
IN_SHARD_P = 512
UP_SHARD = D_UP // N_DEV
UP_SHARD_P = 768
RELAYOUT_ROWS = 256


def _pieces_w_in():
    return [(k, 0, IN_SHARD * k, IN_SHARD) for k in range(N_DEV)]


def _pieces_ffn_up():
    pieces = []
    for k in range(N_DEV):
        n, end = UP_SHARD * k, UP_SHARD * (k + 1)
        while n < end:
            half, r = divmod(n, D_FF)
            blk, off = divmod(r, CONV_BLOCK)
            run = min(CONV_BLOCK - off, end - n)
            pieces.append((k, n - UP_SHARD * k, 2 * CONV_BLOCK * blk + CONV_BLOCK * half + off, run))
            n += run
    return pieces


def _assemble_block(load, spans, dst_block, rows):
    lo = 128 * dst_block
    lane = lax.broadcasted_iota(jnp.int32, (1, 128), 1)
    out = jnp.zeros((rows, 128), F32)
    for key, src_off, dst_off, length in spans:
        a, b = max(lo, dst_off), min(lo + 128, dst_off + length)
        s, s_end = src_off + (a - dst_off), src_off + (b - dst_off)
        d = a
        while s < s_end:
            e = min(s_end, 128 * (s // 128 + 1))
            blk = load(key, s // 128)
            shift = (d - s) % 128
            if shift:
                blk = pltpu.roll(blk, shift, 1)
            out = jnp.where((lane >= d - lo) & (lane < d - lo + (e - s)), blk, out)
            d += e - s
            s = e
    return out


def _shards_to_cols(shards, pieces, width, name):
    _, rows, _ = shards.shape
    tr = RELAYOUT_ROWS

    def body(s_ref, o_ref):
        load = lambda k, b: s_ref[k, :, 128 * b:128 * (b + 1)].astype(F32)
        for db in range(width // 128):
            o_ref[:, 128 * db:128 * (db + 1)] = _assemble_block(load, pieces, db, tr).astype(BF16)

    return pl.pallas_call(
        body, name=name, grid=(rows // tr,),
        in_specs=[pl.BlockSpec((N_DEV, tr, shards.shape[2]), lambda i: (0, i, 0))],
        out_specs=pl.BlockSpec((tr, width), lambda i: (i, 0)),
        out_shape=jax.ShapeDtypeStruct((rows, width), BF16),
        compiler_params=_cparams("parallel"),
    )(shards)


def _cols_to_shards(full, pieces, shard_width, name):
    rows, width = full.shape
    tr = RELAYOUT_ROWS

    def body(f_ref, o_ref):
        load = lambda _, b: f_ref[:, 128 * b:128 * (b + 1)].astype(F32)
        for k in range(N_DEV):
            spans = [(None, dst_off, src_off, length) for dev, src_off, dst_off, length in pieces if dev == k]
            for db in range(shard_width // 128):
                o_ref[k, :, 128 * db:128 * (db + 1)] = _assemble_block(load, spans, db, tr).astype(BF16)

    return pl.pallas_call(
        body, name=name, grid=(rows // tr,),
        in_specs=[pl.BlockSpec((tr, width), lambda i: (i, 0))],
        out_specs=pl.BlockSpec((N_DEV, tr, shard_width), lambda i: (0, i, 0)),
        out_shape=jax.ShapeDtypeStruct((N_DEV, rows, shard_width), BF16),
        compiler_params=_cparams("parallel"),
    )(full)


def _adamw(parts, w, m, v, rows_per_step, name):
    rows, cols = w.shape
    assert rows % rows_per_step == 0 and parts.shape == (N_DEV, rows, cols)

    def body(p_ref, w_ref, m_ref, v_ref, g_ref, d_ref, nm_ref, nv_ref):
        g = p_ref[0].astype(F32)
        for j in range(1, N_DEV):
            g = g + p_ref[j].astype(F32)
        m_new = ADAM_B1 * m_ref[...] + (1.0 - ADAM_B1) * g
        v_new = ADAM_B2 * v_ref[...] + (1.0 - ADAM_B2) * (g * g)
        m_hat = m_new / (1.0 - ADAM_B1 ** ADAM_STEP)
        v_hat = v_new / (1.0 - ADAM_B2 ** ADAM_STEP)
        g_ref[...] = g
        d_ref[...] = -ADAM_LR * (m_hat / (jnp.sqrt(v_hat) + ADAM_EPS) + ADAM_WD * w_ref[...])
        nm_ref[...] = m_new
        nv_ref[...] = v_new

    tile = pl.BlockSpec((rows_per_step, cols), lambda i: (i, 0))
    shape = jax.ShapeDtypeStruct((rows, cols), F32)
    return pl.pallas_call(
        body, name=name, grid=(rows // rows_per_step,),
        in_specs=[pl.BlockSpec((N_DEV, rows_per_step, cols), lambda i: (0, i, 0)), tile, tile, tile],
        out_specs=[tile, tile, tile, tile],
        out_shape=[shape, shape, shape, shape],
        compiler_params=_cparams("parallel"),
    )(parts, w, m, v)


BIG = (("w_in", (DEPTH, D, IN_W // N_DEV), 2), ("w_out", (DEPTH, D // N_DEV, D), 1),
       ("ffn_up", (DEPTH, D, D_UP // N_DEV), 2), ("ffn_down", (DEPTH, D_FF // N_DEV, D), 1))
SMALL = (("meta_tokens", (N_META, D // N_DEV), 1), ("gla_gate_w2", (DEPTH, GATE_RANK, 256 // N_DEV), 2),
         ("ffn_conv_w", (DEPTH, 3, D_UP // N_DEV), 2))
REPL = (("pre_mix_norm", (DEPTH, D)), ("gla_gate_b", (DEPTH, 256)), ("ret_norm_w", (DEPTH, 512)),
        ("gla_norm_w", (DEPTH, 512)), ("post_mix_norm", (DEPTH, D)), ("pre_ffn_norm", (DEPTH, D)),
        ("ffn_conv_b", (DEPTH, D_UP)), ("post_ffn_norm", (DEPTH, D)))
WEIGHT_ORDER = ("meta_tokens", "pre_mix_norm", "w_in", "gla_gate_w2", "gla_gate_b", "ret_norm_w", "gla_norm_w",
                "w_out", "post_mix_norm", "pre_ffn_norm", "ffn_up", "ffn_conv_w", "ffn_conv_b", "ffn_down",
                "post_ffn_norm")


def _size(shape):
    return math.prod(shape)


def _round_up(n, mult):
    return -(-n // mult) * mult


REPL_ROWS = _round_up(-(-sum(_size(s) for _, s in REPL) // LANES), 8)
SMALL_ROWS = _round_up(-(-sum(_size(s) for _, s, _ in SMALL) // LANES), 8)


def _pack(arrays, rows, dtype):
    flat = jnp.concatenate([a.reshape(-1).astype(dtype) for a in arrays])
    return jnp.pad(flat, (0, rows * LANES - flat.shape[0])).reshape(rows, LANES)


def _unpack(buf, shapes):
    flat = buf.reshape(-1)
    out, off = [], 0
    for shape in shapes:
        out.append(flat[off:off + _size(shape)].reshape(shape))
        off += _size(shape)
    return out


def _unshard(blocks, axis):
    moved = jnp.moveaxis(blocks, 0, axis)
    shape = list(moved.shape)
    shape[axis:axis + 2] = [shape[axis] * shape[axis + 1]]
    return moved.reshape(shape)


def _to_blocks(full, axis):
    shape = list(full.shape)
    shape[axis:axis + 1] = [N_DEV, shape[axis] // N_DEV]
    return jnp.moveaxis(full.reshape(shape), axis, 0)


def _interleave_cols(w):
    lead = w.shape[:-1]
    return jnp.swapaxes(w.reshape(lead + (2, N_CONV_BLOCKS, CONV_BLOCK)), -3, -2).reshape(lead + (D_UP,))


def _deinterleave_cols(w):
    lead = w.shape[:-1]
    return jnp.swapaxes(w.reshape(lead + (N_CONV_BLOCKS, 2, CONV_BLOCK)), -3, -2).reshape(lead + (D_UP,))


def _rope_tables():
    half = RET_DK // 2
    inv = ROPE_BASE ** (-jnp.arange(half, dtype=F32) / half)
    pos = jnp.arange(LP, dtype=F32) - float(PAD_ROWS)
    ang = pos[:, None] * inv[None, :]
    c, s = jnp.cos(ang), jnp.sin(ang)
    return jnp.concatenate([c, c], axis=1), jnp.concatenate([-s, s], axis=1)


def kernel(x, meta_tokens, pre_mix_norm, w_in, gla_gate_w2, gla_gate_b, ret_norm_w, gla_norm_w, w_out, post_mix_norm, pre_ffn_norm, ffn_up, ffn_conv_w, ffn_conv_b, ffn_down, post_ffn_norm, loss_target, m_meta_tokens, m_pre_mix_norm, m_w_in, m_gla_gate_w2, m_gla_gate_b, m_ret_norm_w, m_gla_norm_w, m_w_out, m_post_mix_norm, m_pre_ffn_norm, m_ffn_up, m_ffn_conv_w, m_ffn_conv_b, m_ffn_down, m_post_ffn_norm, v_meta_tokens, v_pre_mix_norm, v_w_in, v_gla_gate_w2, v_gla_gate_b, v_ret_norm_w, v_gla_norm_w, v_w_out, v_post_mix_norm, v_pre_ffn_norm, v_ffn_up, v_ffn_conv_w, v_ffn_conv_b, v_ffn_down, v_post_ffn_norm):
    weights = dict(meta_tokens=meta_tokens, pre_mix_norm=pre_mix_norm, w_in=w_in, gla_gate_w2=gla_gate_w2,
                   gla_gate_b=gla_gate_b, ret_norm_w=ret_norm_w, gla_norm_w=gla_norm_w, w_out=w_out,
                   post_mix_norm=post_mix_norm, pre_ffn_norm=pre_ffn_norm, ffn_up=ffn_up, ffn_conv_w=ffn_conv_w,
                   ffn_conv_b=ffn_conv_b, ffn_down=ffn_down, post_ffn_norm=post_ffn_norm)
    mom1 = dict(meta_tokens=m_meta_tokens, pre_mix_norm=m_pre_mix_norm, w_in=m_w_in, gla_gate_w2=m_gla_gate_w2,
                gla_gate_b=m_gla_gate_b, ret_norm_w=m_ret_norm_w, gla_norm_w=m_gla_norm_w, w_out=m_w_out,
                post_mix_norm=m_post_mix_norm, pre_ffn_norm=m_pre_ffn_norm, ffn_up=m_ffn_up,
                ffn_conv_w=m_ffn_conv_w, ffn_conv_b=m_ffn_conv_b, ffn_down=m_ffn_down, post_ffn_norm=m_post_ffn_norm)
    mom2 = dict(meta_tokens=v_meta_tokens, pre_mix_norm=v_pre_mix_norm, w_in=v_w_in, gla_gate_w2=v_gla_gate_w2,
                gla_gate_b=v_gla_gate_b, ret_norm_w=v_ret_norm_w, gla_norm_w=v_gla_norm_w, w_out=v_w_out,
                post_mix_norm=v_post_mix_norm, pre_ffn_norm=v_pre_ffn_norm, ffn_up=v_ffn_up,
                ffn_conv_w=v_ffn_conv_w, ffn_conv_b=v_ffn_conv_b, ffn_down=v_ffn_down, post_ffn_norm=v_post_ffn_norm)

    pad_cols = lambda a, width: jnp.pad(a, ((0, 0), (0, width - a.shape[1])))
    big_names = [n for n, _, _ in BIG]
    shard = {}
    for l in range(DEPTH):
        shard[l, "w_in"] = pad_cols(w_in[l].astype(BF16), IN_SHARD_P)
        shard[l, "w_out"] = w_out[l].astype(BF16)
        shard[l, "ffn_up"] = pad_cols(ffn_up[l].astype(BF16), UP_SHARD_P)
        shard[l, "ffn_down"] = ffn_down[l].astype(BF16)
    w_in_0, small = _all_gather([shard[0, "w_in"], _pack([weights[n] for n, _, _ in SMALL], SMALL_ROWS, F32)],
                                "gather_first_weights")
    gathered = {(0, "w_in"): w_in_0}
    gather_in_mixer = {l: [(l, n) for n in big_names[1:]] for l in range(DEPTH)}
    gather_in_conv = {l: [(l + 1, "w_in")] for l in range(DEPTH - 1)}
    small_parts = _unpack_blocks(small, [s for _, s, _ in SMALL])
    full = {n: _unshard(p, ax) for (n, _, ax), p in zip(SMALL, small_parts)}
    w2p = jnp.pad(full["gla_gate_w2"], ((0, 0), (0, 128 - GATE_RANK), (0, 0)))
    cw8 = jnp.concatenate([_interleave_cols(full["ffn_conv_w"]), _interleave_cols(ffn_conv_b)[:, None, :],
                           jnp.zeros((DEPTH, 4, D_UP), F32)], axis=1)
    cos2, sin2 = _rope_tables()

    h = jnp.concatenate([jnp.zeros((PAD_ROWS, D), F32), full["meta_tokens"], x[0]], axis=0)
    target = jnp.concatenate([jnp.zeros((CHUNK, D), F32), loss_target[0]], axis=0)
    saved, layer_w = [], []
    for l in range(DEPTH):
        lw = dict(w_in=_shards_to_cols(gathered[l, "w_in"], _pieces_w_in(), IN_WP, f"w_in_cols_{l}"))
        a1, proj = _norm_matmul(h, pre_mix_norm[l:l + 1], lw["w_in"], out_dtype=F32, tm=TM_BIG, tn=IN_WP // 3,
                                name=f"in_proj_{l}")
        keys = gather_in_mixer.get(l, [])
        ocat, merged, sr_all, sg_all, *got = _mixer_fwd(proj, cos2, sin2, w2p[l], gla_gate_b[l:l + 1],
                                                        ret_norm_w[l:l + 1], gla_norm_w[l:l + 1], f"mixer_fwd_{l}",
                                                        carried=[shard[key] for key in keys])
        gathered.update(zip(keys, got))
        lw["w_out"] = gathered[l, "w_out"].reshape(D, D)
        lw["w_up"] = _shards_to_cols(gathered[l, "ffn_up"], _pieces_ffn_up(), D_UP, f"ffn_up_cols_{l}")
        lw["w_down"] = gathered[l, "ffn_down"].reshape(D_FF, D)
        layer_w.append(lw)
        m, h1 = _matmul_resid_norm(merged, lw["w_out"], h, post_mix_norm[l:l + 1], f"out_proj_{l}")
        a2, u = _norm_matmul(h1, pre_ffn_norm[l:l + 1], lw["w_up"], out_dtype=BF16, tm=TM_BIG, tn=D_UP // 2,
                             name=f"ffn_up_{l}")
        keys = gather_in_conv.get(l, [])
        cv, act, *got = _conv_act_fwd(u, cw8[l], f"ffn_conv_act_{l}", carried=[shard[key] for key in keys])
        gathered.update(zip(keys, got))
        f, h2, *loss_acc = _matmul_resid_norm(act, lw["w_down"], h1, post_ffn_norm[l:l + 1], f"ffn_down_{l}",
                                              target=target if l == DEPTH - 1 else None)
        saved.append(dict(h=h, a1=a1, proj=proj, ocat=ocat, merged=merged, sr=sr_all, sg=sg_all, m=m, h1=h1,
                          a2=a2, u=u, cv=cv, act=act, f=f))
        h = h2

    dh = h
    loss = lax.psum(loss_acc[0][0, 0], ("x", "y", "c"))

    kinds = ("grad", "delta", "new_m", "new_v")
    grads = {n: [None] * DEPTH for n in WEIGHT_ORDER if n != "meta_tokens" and n not in big_names}
    pending, parts = [], {}
    for l in reversed(range(DEPTH)):
        s, lw = saved[l], layer_w[l]
        dact, df, g_post_ffn = _norm_bwd_matmul(dh, s["f"], post_ffn_norm[l:l + 1], lw["w_down"], BF16,
                                                f"ffn_down_dx_{l}")
        g_down = _matmul(s["act"], df, ta=True, out_dtype=BF16, tm=D_FF // 2, tn=D, tk=TK_LONG, name=f"ffn_down_dw_{l}")
        du, dcw = _conv_act_bwd(dact, s["cv"], s["u"], cw8[l], f"ffn_conv_act_bwd_{l}")
        dh1, g_pre_ffn = _matmul_norm_bwd(du, lw["w_up"], s["h1"], pre_ffn_norm[l:l + 1], dh, D_FF, f"ffn_up_dx_{l}")
        g_up = _matmul(s["a2"], du, ta=True, out_dtype=BF16, tm=D, tn=D_UP // 4, tk=TK_LONG, name=f"ffn_up_dw_{l}")
        dmerged, dm, g_post_mix = _norm_bwd_matmul(dh1, s["m"], post_mix_norm[l:l + 1], lw["w_out"], F32,
                                                   f"out_proj_dx_{l}")
        g_out = _matmul(s["merged"], dm, ta=True, out_dtype=BF16, tm=D, tn=D, tk=TK_LONG, name=f"out_proj_dw_{l}")
        pending += [((l, "ffn_down"), g_down.reshape(N_DEV, D_FF // N_DEV, D)),
                    ((l, "ffn_up"), _cols_to_shards(g_up, _pieces_ffn_up(), UP_SHARD_P, f"ffn_up_grad_shards_{l}")),
                    ((l, "w_out"), g_out.reshape(N_DEV, D // N_DEV, D))]
        dproj, g_w2, g_gb, g_rn, g_gn, *got = _mixer_bwd(s["proj"], s["ocat"], dmerged, s["sr"], s["sg"], cos2, sin2,
                                                         w2p[l], gla_gate_b[l:l + 1], ret_norm_w[l:l + 1],
                                                         gla_norm_w[l:l + 1], f"mixer_bwd_{l}",
                                                         carried=[blocks for _, blocks in pending])
        parts.update(zip([key for key, _ in pending], got))
        g_in = _matmul(s["a1"], dproj, ta=True, out_dtype=BF16, tm=D, tn=IN_WP // 3, tk=TK_LONG, name=f"in_proj_dw_{l}")
        pending = [((l, "w_in"), _cols_to_shards(g_in, _pieces_w_in(), IN_SHARD_P, f"w_in_grad_shards_{l}"))]
        now = pending if l == 0 else []
        dh, g_pre_mix, *got = _matmul_norm_bwd(dproj, lw["w_in"], s["h"], pre_mix_norm[l:l + 1], dh1, IN_WP,
                                               f"in_proj_dx_{l}", carried=[blocks for _, blocks in now])
        parts.update(zip([key for key, _ in now], got))
        pending = [] if l == 0 else pending
        grads["post_ffn_norm"][l] = g_post_ffn[0]
        grads["ffn_conv_w"][l] = _deinterleave_cols(dcw[0:3])
        grads["ffn_conv_b"][l] = _deinterleave_cols(dcw[3])
        grads["pre_ffn_norm"][l] = g_pre_ffn[0]
        grads["post_mix_norm"][l] = g_post_mix[0]
        grads["gla_gate_w2"][l] = g_w2[:GATE_RANK]
        grads["gla_gate_b"][l] = g_gb[0]
        grads["ret_norm_w"][l] = g_rn[0]
        grads["gla_norm_w"][l] = g_gn[0]
        grads["pre_mix_norm"][l] = g_pre_mix[0]
    local = {n: jnp.stack(v) for n, v in grads.items()}
    local["meta_tokens"] = dh[PAD_ROWS:CHUNK]
    grad_x = dh[CHUNK:][None]

    blocks = jnp.concatenate([_to_blocks(local[n], ax).reshape(N_DEV, -1) for n, _, ax in SMALL], axis=1)
    blocks = jnp.pad(blocks, ((0, 0), (0, SMALL_ROWS * LANES - blocks.shape[1]))).reshape(N_DEV, SMALL_ROWS, LANES)
    *got, small_grad_parts = _exchange_blocks([b for _, b in pending] + [blocks], "exchange_last_grads")
    parts.update(zip([key for key, _ in pending], got))

    widths = dict(w_in=IN_SHARD_P, w_out=D, ffn_up=UP_SHARD_P, ffn_down=D)
    steps = dict(w_in=256, w_out=D // N_DEV, ffn_up=256, ffn_down=D_FF // N_DEV // 2)
    big_out = {kind: {n: [None] * DEPTH for n in big_names} for kind in kinds}
    for l in range(DEPTH):
        for n in big_names:
            mine = [pad_cols(d[n][l], widths[n]) for d in (weights, mom1, mom2)]
            results = _adamw(parts[l, n], *mine, steps[n], f"adamw_{n}_{l}")
            for kind, r in zip(kinds, results):
                big_out[kind][n][l] = r[:, :weights[n].shape[2]]
    out = {kind: {n: jnp.stack(v) for n, v in big_out[kind].items()} for kind in kinds}
    shard_shapes = [s for _, s, _ in SMALL]
    packed = [_pack([d[n] for n, _, _ in SMALL], SMALL_ROWS, F32) for d in (weights, mom1, mom2)]
    results = _adamw(small_grad_parts, *packed, SMALL_ROWS, "adamw_small_sharded")
    for kind, buf in zip(kinds, results):
        out[kind].update(zip([n for n, _, _ in SMALL], _unpack(buf, shard_shapes)))

    repl_parts = _all_gather([_pack([local[n] for n, _ in REPL], REPL_ROWS, F32)], "gather_small_grads")[0]
    packed = [_pack([d[n] for n, _ in REPL], REPL_ROWS, F32) for d in (weights, mom1, mom2)]
    results = _adamw(repl_parts, *packed, REPL_ROWS, "adamw_replicated")
    repl_shapes = [s for _, s in REPL]
    for kind, buf in zip(kinds, results):
        out[kind].update(zip([n for n, _ in REPL], _unpack(buf, repl_shapes)))

    return (loss, grad_x, *[out["grad"][n] for n in WEIGHT_ORDER], *[out["delta"][n] for n in WEIGHT_ORDER],
            *[out["new_m"][n] for n in WEIGHT_ORDER], *[out["new_v"][n] for n in WEIGHT_ORDER])


def _unpack_blocks(gathered, shapes):
    flat = gathered.reshape(N_DEV, -1)
    out, off = [], 0
    for shape in shapes:
        out.append(flat[:, off:off + _size(shape)].reshape((N_DEV,) + shape))
        off += _size(shape)
    return out
```

```python
import math

import jax
import jax.numpy as jnp
from jax import lax
from jax.experimental import pallas as pl
from jax.experimental.pallas import tpu as pltpu

F32 = jnp.float32
BF16 = jnp.bfloat16

D = 1024
SEQ = 8192
DEPTH = 2
N_META = 16
CHUNK = 64
SUB = 16
N_SUB = CHUNK // SUB
PAD_ROWS = CHUNK - N_META
LP = SEQ + CHUNK
N_CHUNKS = LP // CHUNK
RET_HEADS = 4
RET_DK = 128
GLA_HEADS = 4
GLA_DK = 64
GLA_DV = 128
GLA_TAU = 16.0
GATE_RANK = 16
IN_W = 3600
IN_WP = 3840
D_FF = 2816
D_UP = 2 * D_FF
CONV_BLOCK = 256
N_CONV_BLOCKS = D_FF // CONV_BLOCK
ROPE_BASE = 10000.0
EPS = 1e-6
N_DEV = 8
LANES = 1024

O_RQ, O_RK, O_RV, O_RG = 0, 512, 1024, 1536
O_GQ, O_GK, O_GV, O_GR, O_GA = 2048, 2304, 2560, 3072, 3584

ADAM_LR = 0.001
ADAM_B1 = 0.9
ADAM_B2 = 0.999
ADAM_EPS = 1e-08
ADAM_WD = 0.01
ADAM_STEP = 10

VMEM_LIMIT = 56 * 1024 * 1024
MESH_IDS = pl.DeviceIdType.MESH


def _row_tile(rows, limit):
    best = 16
    for t in range(16, min(rows, limit) + 1, 16):
        if rows % t == 0:
            best = t
    return best


TM = _row_tile(LP, 688)
TM_BIG = _row_tile(LP, 1376)
TK_LONG = _row_tile(LP, 2752)


def _cparams(*sem):
    return pltpu.CompilerParams(dimension_semantics=sem, vmem_limit_bytes=VMEM_LIMIT)


def _dot(a, b):
    return jnp.dot(a.astype(BF16), b.astype(BF16), preferred_element_type=F32)


def _dot_nt(a, b):
    return lax.dot_general(a.astype(BF16), b.astype(BF16), (((1,), (1,)), ((), ())), preferred_element_type=F32)


def _dot_tn(a, b):
    return lax.dot_general(a.astype(BF16), b.astype(BF16), (((0,), (0,)), ((), ())), preferred_element_type=F32)


def _split3(x):
    hi = x.astype(BF16)
    r1 = x - hi.astype(F32)
    mid = r1.astype(BF16)
    lo = (r1 - mid.astype(F32)).astype(BF16)
    return hi, mid, lo


def _dot_exact_rhs(t, x):
    n = x.shape[1]
    parts = jnp.dot(t.astype(BF16), jnp.concatenate(_split3(x), axis=1), preferred_element_type=F32)
    return parts[:, :n] + parts[:, n:2 * n] + parts[:, 2 * n:]


def _dot_tn_exact_lhs(x, ones):
    n = x.shape[1]
    parts = lax.dot_general(jnp.concatenate(_split3(x), axis=1), ones.astype(BF16), (((0,), (0,)), ((), ())),
                            preferred_element_type=F32)
    return parts[:n] + parts[n:2 * n] + parts[2 * n:]


def _sigmoid(x):
    return 1.0 / (1.0 + jnp.exp(-x))


def _matmul(a, b, *, ta=False, tb=False, out_dtype, tm, tn, tk, name):
    m = a.shape[1] if ta else a.shape[0]
    k = a.shape[0] if ta else a.shape[1]
    n = b.shape[0] if tb else b.shape[1]
    assert (b.shape[1] if tb else b.shape[0]) == k
    assert m % tm == 0 and n % tn == 0 and k % tk == 0, (name, m, n, k, tm, tn, tk)
    nk = k // tk
    a_spec = pl.BlockSpec((tk, tm), lambda i, j, kk: (kk, i)) if ta else pl.BlockSpec((tm, tk), lambda i, j, kk: (i, kk))
    b_spec = pl.BlockSpec((tn, tk), lambda i, j, kk: (j, kk)) if tb else pl.BlockSpec((tk, tn), lambda i, j, kk: (kk, j))
    dims = (((0 if ta else 1,), (1 if tb else 0,)), ((), ()))

    def body(a_ref, b_ref, o_ref, *acc):
        prod = lax.dot_general(a_ref[...].astype(BF16), b_ref[...].astype(BF16), dims, preferred_element_type=F32)
        if nk == 1:
            o_ref[...] = prod.astype(out_dtype)
            return
        acc_ref, = acc
        kk = pl.program_id(2)

        @pl.when(kk == 0)
        def _():
            acc_ref[...] = prod

        @pl.when(kk > 0)
        def _():
            acc_ref[...] += prod

        @pl.when(kk == nk - 1)
        def _():
            o_ref[...] = acc_ref[...].astype(out_dtype)

    return pl.pallas_call(
        body, name=name, grid=(m // tm, n // tn, nk),
        in_specs=[a_spec, b_spec],
        out_specs=pl.BlockSpec((tm, tn), lambda i, j, kk: (i, j)),
        out_shape=jax.ShapeDtypeStruct((m, n), out_dtype),
        scratch_shapes=[pltpu.VMEM((tm, tn), F32)] if nk > 1 else [],
        compiler_params=_cparams("parallel", "parallel", "arbitrary"),
    )(a, b)


def _norm_matmul(x, w, b, *, out_dtype, tm, tn, name):
    n = b.shape[1]
    assert LP % tm == 0 and n % tn == 0

    def body(x_ref, w_ref, b_ref, a_ref, o_ref, a_scr):
        @pl.when(pl.program_id(1) == 0)
        def _():
            xv = x_ref[...]
            r = lax.rsqrt(jnp.mean(xv * xv, axis=-1, keepdims=True) + EPS)
            a = (xv * r * w_ref[...]).astype(BF16)
            a_scr[...] = a
            a_ref[...] = a

        o_ref[...] = jnp.dot(a_scr[...], b_ref[...], preferred_element_type=F32).astype(out_dtype)

    return pl.pallas_call(
        body, name=name, grid=(LP // tm, n // tn),
        in_specs=[pl.BlockSpec((tm, D), lambda i, j: (i, 0)), pl.BlockSpec((1, D), lambda i, j: (0, 0)),
                  pl.BlockSpec((D, tn), lambda i, j: (0, j))],
        out_specs=[pl.BlockSpec((tm, D), lambda i, j: (i, 0)), pl.BlockSpec((tm, tn), lambda i, j: (i, j))],
        out_shape=[jax.ShapeDtypeStruct((LP, D), BF16), jax.ShapeDtypeStruct((LP, n), out_dtype)],
        scratch_shapes=[pltpu.VMEM((tm, D), BF16)],
        compiler_params=_cparams("arbitrary", "arbitrary"),
    )(x, w, b)


def _matmul_resid_norm(a, b, h, w, name, target=None):
    k = a.shape[1]
    has_loss = target is not None

    def body(a_ref, b_ref, h_ref, w_ref, *refs):
        m = jnp.dot(a_ref[...].astype(BF16), b_ref[...].astype(BF16), preferred_element_type=F32)
        r = lax.rsqrt(jnp.mean(m * m, axis=-1, keepdims=True) + EPS)
        i = pl.program_id(0)
        row = i * TM + lax.broadcasted_iota(jnp.int32, (TM, 1), 0)
        y = h_ref[...] + jnp.where(row >= PAD_ROWS, m * r * w_ref[...], 0.0)
        if not has_loss:
            m_ref, y_ref = refs
            m_ref[...] = m
            y_ref[...] = y
            return
        t_ref, m_ref, dy_ref, loss_ref = refs
        m_ref[...] = m

        @pl.when(i == 0)
        def _():
            loss_ref[...] = jnp.zeros_like(loss_ref)

        diff = jnp.where(row >= CHUNK, y - t_ref[...], 0.0)
        dy_ref[...] = diff * (1.0 / D)
        loss_ref[...] += (0.5 / D) * jnp.sum(diff * diff)

    tile = pl.BlockSpec((TM, D), lambda i: (i, 0))
    shape = jax.ShapeDtypeStruct((LP, D), F32)
    in_specs = [pl.BlockSpec((TM, k), lambda i: (i, 0)), pl.BlockSpec((k, D), lambda i: (0, 0)), tile,
                pl.BlockSpec((1, D), lambda i: (0, 0))]
    if has_loss:
        return pl.pallas_call(
            body, name=name, grid=(LP // TM,),
            in_specs=in_specs + [tile],
            out_specs=[tile, tile, pl.BlockSpec((8, 128), lambda i: (0, 0))],
            out_shape=[shape, shape, jax.ShapeDtypeStruct((8, 128), F32)],
            compiler_params=_cparams("arbitrary"),
        )(a, b, h, w, target)
    return pl.pallas_call(
        body, name=name, grid=(LP // TM,),
        in_specs=in_specs, out_specs=[tile, tile], out_shape=[shape, shape],
        compiler_params=_cparams("parallel"),
    )(a, b, h, w)


def _rmsnorm_bwd_rows(dy, x, w):
    r = lax.rsqrt(jnp.mean(x * x, axis=-1, keepdims=True) + EPS)
    g = dy * w
    dx = r * g - x * (r * r * r * jnp.mean(g * x, axis=-1, keepdims=True))
    return dx, jnp.sum(dy * x * r, axis=0, keepdims=True)


def _matmul_norm_bwd(dz, b, x, w, resid, tk, name, carried=()):
    k = dz.shape[1]
    assert k % tk == 0
    nk = k // tk
    n_rows = LP // TM
    n_carried = len(carried)

    def body(*refs):
        a_ref, b_ref, x_ref, w_ref, r_ref = refs[:5]
        g_refs, refs = refs[5:5 + n_carried], refs[5 + n_carried:]
        dx_ref, dw_ref = refs[:2]
        got_refs, refs = refs[2:2 + n_carried], refs[2 + n_carried:]
        acc, sems = (refs[:1], refs[1:]) if nk > 1 else ((), refs)
        i, kk = pl.program_id(0), pl.program_id(1)
        if n_carried:
            exchange_start, exchange_finish = _exchange_phases(g_refs, got_refs, *sems)
            pl.when((i == 0) & (kk == 0))(exchange_start)

        @pl.when((i == 0) & (kk == 0))
        def _():
            dw_ref[...] = jnp.zeros_like(dw_ref)

        prod = lax.dot_general(a_ref[...].astype(BF16), b_ref[...].astype(BF16), (((1,), (1,)), ((), ())),
                               preferred_element_type=F32)

        def finish(dy):
            dx, dw = _rmsnorm_bwd_rows(dy, x_ref[...], w_ref[...])
            dx_ref[...] = dx + r_ref[...]
            dw_ref[0:1, :] += dw

        if nk == 1:
            finish(prod)
        else:
            acc_ref, = acc

            @pl.when(kk == 0)
            def _():
                acc_ref[...] = prod

            @pl.when((kk > 0) & (kk < nk - 1))
            def _():
                acc_ref[...] += prod

            @pl.when(kk == nk - 1)
            def _():
                finish(acc_ref[...] + prod)

        if n_carried:
            pl.when((i == n_rows - 1) & (kk == nk - 1))(exchange_finish)

    tile = pl.BlockSpec((TM, D), lambda i, kk: (i, 0))
    anywhere = [pl.BlockSpec(memory_space=pl.ANY)] * n_carried
    return pl.pallas_call(
        body, name=name, grid=(n_rows, nk),
        in_specs=[pl.BlockSpec((TM, tk), lambda i, kk: (i, kk)), pl.BlockSpec((D, tk), lambda i, kk: (0, kk)), tile,
                  pl.BlockSpec((1, D), lambda i, kk: (0, 0)), tile] + anywhere,
        out_specs=[tile, pl.BlockSpec((8, D), lambda i, kk: (0, 0))] + anywhere,
        out_shape=[jax.ShapeDtypeStruct((LP, D), F32), jax.ShapeDtypeStruct((8, D), F32)]
        + [jax.ShapeDtypeStruct(g.shape, g.dtype) for g in carried],
        scratch_shapes=([pltpu.VMEM((TM, D), F32)] if nk > 1 else []) + _exchange_sems(n_carried),
        compiler_params=_cparams("arbitrary", "arbitrary"),
    )(dz, b, x, w, resid, *carried)


def _norm_bwd_matmul(dh, x, w, b, out_dtype, name):
    n = b.shape[0]

    def body(dh_ref, x_ref, w_ref, b_ref, o_ref, dx_ref, dw_ref):
        i = pl.program_id(0)

        @pl.when(i == 0)
        def _():
            dw_ref[...] = jnp.zeros_like(dw_ref)

        row = i * TM + lax.broadcasted_iota(jnp.int32, (TM, 1), 0)
        dy = jnp.where(row >= PAD_ROWS, dh_ref[...], 0.0)
        dx, dw = _rmsnorm_bwd_rows(dy, x_ref[...], w_ref[...])
        dxb = dx.astype(BF16)
        dx_ref[...] = dxb
        dw_ref[0:1, :] += dw
        o_ref[...] = lax.dot_general(dxb, b_ref[...].astype(BF16), (((1,), (1,)), ((), ())),
                                     preferred_element_type=F32).astype(out_dtype)

    tile = pl.BlockSpec((TM, D), lambda i: (i, 0))
    return pl.pallas_call(
        body, name=name, grid=(LP // TM,),
        in_specs=[tile, tile, pl.BlockSpec((1, D), lambda i: (0, 0)), pl.BlockSpec((n, D), lambda i: (0, 0))],
        out_specs=[pl.BlockSpec((TM, n), lambda i: (i, 0)), tile, pl.BlockSpec((8, D), lambda i: (0, 0))],
        out_shape=[jax.ShapeDtypeStruct((LP, n), out_dtype), jax.ShapeDtypeStruct((LP, D), BF16),
                   jax.ShapeDtypeStruct((8, D), F32)],
        compiler_params=_cparams("arbitrary"),
    )(dh, x, w, b)


GELU_C = math.sqrt(2.0 / math.pi)
GELU_K = 0.044715
STRIP = 16
HALF = 8


def _gelu_half(a):
    return 0.5 * jnp.tanh(a * (a * a * (GELU_C * GELU_K) + GELU_C)) + 0.5


def _gelu_slope(a, h):
    return h * (1.0 + (a - a * h) * (a * a * (6.0 * GELU_C * GELU_K) + 2.0 * GELU_C))


def _shift_down(x, prev8):
    row = lax.broadcasted_iota(jnp.int32, (8, 1), 0)
    r1, r2 = pltpu.roll(x, 1, 0), pltpu.roll(x, 2, 0)
    top1 = jnp.where(row < 1, pltpu.roll(prev8, 1, 0), r1[0:8, :])
    top2 = jnp.where(row < 2, pltpu.roll(prev8, 2, 0), r2[0:8, :])
    return jnp.concatenate([top1, r1[8:, :]], axis=0), jnp.concatenate([top2, r2[8:, :]], axis=0)


def _conv_act_fwd(u, cw8, name, carried=()):
    n_rows = LP // TM
    cb2 = 2 * CONV_BLOCK
    n_carried = len(carried)

    def body(*refs):
        u_ref, cw_ref = refs[:2]
        x_refs, refs = refs[2:2 + n_carried], refs[2 + n_carried:]
        conv_ref, act_ref = refs[:2]
        gathered_refs, refs = refs[2:2 + n_carried], refs[2 + n_carried:]
        carry_ref = refs[0]
        j, i = pl.program_id(0), pl.program_id(1)
        if n_carried:
            start, forward, finish = _gather_phases(x_refs, gathered_refs, *refs[1:])
            pl.when((j == 0) & (i == 0))(start)
            pl.when((j == (3 * N_CONV_BLOCKS) // 4) & (i == 0))(forward)

        @pl.when(i == 0)
        def _():
            carry_ref[...] = jnp.zeros_like(carry_ref)

        x = u_ref[...].astype(F32)
        x1, x2 = _shift_down(x, carry_ref[...])
        conv = cw_ref[3:4, :] + x2 * cw_ref[0:1, :] + x1 * cw_ref[1:2, :] + x * cw_ref[2:3, :]
        conv_ref[...] = conv.astype(BF16)
        a = conv[:, :CONV_BLOCK]
        g = conv[:, CONV_BLOCK:]
        act_ref[...] = (a * _gelu_half(a) * g).astype(BF16)
        carry_ref[...] = x[TM - 8:TM, :]
        if n_carried:
            pl.when((j == N_CONV_BLOCKS - 1) & (i == n_rows - 1))(finish)

    anywhere = [pl.BlockSpec(memory_space=pl.ANY)] * n_carried
    return pl.pallas_call(
        body, name=name, grid=(N_CONV_BLOCKS, n_rows),
        in_specs=[pl.BlockSpec((TM, cb2), lambda j, i: (i, j)), pl.BlockSpec((8, cb2), lambda j, i: (0, j))] + anywhere,
        out_specs=[pl.BlockSpec((TM, cb2), lambda j, i: (i, j)),
                   pl.BlockSpec((TM, CONV_BLOCK), lambda j, i: (i, j))] + anywhere,
        out_shape=[jax.ShapeDtypeStruct((LP, D_UP), BF16), jax.ShapeDtypeStruct((LP, D_FF), BF16)]
        + _gathered_shapes(carried),
        scratch_shapes=[pltpu.VMEM((8, cb2), F32)] + _exchange_sems(n_carried),
        compiler_params=_cparams("arbitrary", "arbitrary"),
    )(u, cw8, *carried)


def _conv_act_bwd(dact, conv, u, cw8, name):
    n_rows = LP // TM
    cb2 = 2 * CONV_BLOCK
    n_strips = TM // STRIP

    def body(dact_ref, conv_ref, u_ref, cw_ref, du_ref, dcw_ref, carry_ref):
        i = pl.program_id(1)

        @pl.when(i == 0)
        def _():
            dcw_ref[...] = jnp.zeros_like(dcw_ref)
            carry_ref[...] = jnp.zeros_like(carry_ref)

        w0, w1, w2 = cw_ref[0:1, :], cw_ref[1:2, :], cw_ref[2:3, :]
        row = lax.broadcasted_iota(jnp.int32, (HALF, 1), 0)

        def strip(k, carry):
            n1, n2, s0, s1, s2, s3 = carry
            r0 = pl.multiple_of((n_strips - 1 - k) * STRIP, STRIP)
            cv = conv_ref[pl.ds(r0, STRIP), :].astype(F32)
            dav = dact_ref[pl.ds(r0, STRIP), :].astype(F32)
            x = u_ref[pl.ds(r0, STRIP), :].astype(F32)
            du = [None, None]
            for half in (1, 0):
                rows = slice(HALF * half, HALF * (half + 1))
                a, g, dah = cv[rows, :CONV_BLOCK], cv[rows, CONV_BLOCK:], dav[rows]
                h = _gelu_half(a)
                dconv = jnp.concatenate([dah * g * _gelu_slope(a, h), dah * (a * h)], axis=1)
                u1, u2 = pltpu.roll(dconv, HALF - 1, 0), pltpu.roll(dconv, HALF - 2, 0)
                d1 = jnp.where(row >= HALF - 1, n1, u1)
                d2 = jnp.where(row >= HALF - 2, n2, u2)
                du[half] = dconv * w2 + d1 * w1 + d2 * w0
                s0, s1, s2, s3 = s0 + d2 * x[rows], s1 + d1 * x[rows], s2 + dconv * x[rows], s3 + dconv
                n1, n2 = u1, u2
            du_ref[pl.ds(r0, STRIP), :] = jnp.concatenate(du, axis=0).astype(BF16)
            return n1, n2, s0, s1, s2, s3

        below = carry_ref[...]
        zero = jnp.zeros((HALF, cb2), F32)
        init = (pltpu.roll(below, HALF - 1, 0), pltpu.roll(below, HALF - 2, 0), zero, zero, zero, zero)
        u1, _, s0, s1, s2, s3 = lax.fori_loop(0, n_strips, strip, init, unroll=2)
        carry_ref[...] = pltpu.roll(u1, 1, 0)
        dcw_ref[0:1, :] += jnp.sum(s0, axis=0, keepdims=True)
        dcw_ref[1:2, :] += jnp.sum(s1, axis=0, keepdims=True)
        dcw_ref[2:3, :] += jnp.sum(s2, axis=0, keepdims=True)
        dcw_ref[3:4, :] += jnp.sum(s3, axis=0, keepdims=True)

    rev = lambda j, i: (n_rows - 1 - i, j)
    return pl.pallas_call(
        body, name=name, grid=(N_CONV_BLOCKS, n_rows),
        in_specs=[pl.BlockSpec((TM, CONV_BLOCK), rev), pl.BlockSpec((TM, cb2), rev), pl.BlockSpec((TM, cb2), rev),
                  pl.BlockSpec((8, cb2), lambda j, i: (0, j))],
        out_specs=[pl.BlockSpec((TM, cb2), rev), pl.BlockSpec((8, cb2), lambda j, i: (0, j))],
        out_shape=[jax.ShapeDtypeStruct((LP, D_UP), BF16), jax.ShapeDtypeStruct((8, D_UP), F32)],
        scratch_shapes=[pltpu.VMEM((HALF, cb2), F32)],
        compiler_params=_cparams("arbitrary", "arbitrary"),
    )(dact, conv, u, cw8)


CHUNKS_PER_STEP = 3 if N_CHUNKS % 3 == 0 else 1
STEP_ROWS = CHUNKS_PER_STEP * CHUNK
N_STEPS = N_CHUNKS // CHUNKS_PER_STEP


def _ret_consts(h):
    rows = STEP_ROWS
    lg = math.log(1.0 - 2.0 ** (-5.0 - h))
    ri = lax.broadcasted_iota(jnp.int32, (rows, rows), 0)
    ci = lax.broadcasted_iota(jnp.int32, (rows, rows), 1)
    diff = (ri - ci).astype(F32)
    dmat = jnp.where(diff >= 0, jnp.exp(lg * jnp.maximum(diff, 0.0)), 0.0)
    rowf = lax.broadcasted_iota(jnp.int32, (rows, 1), 0).astype(F32)
    zeta = jnp.exp(lg * (rows - 1.0 - rowf))
    xi = jnp.exp(lg * (rowf + 1.0))
    return dmat, zeta, xi, math.exp(lg * rows)


def _rope(t, cosv, sinv):
    return t * cosv + pltpu.roll(t, RET_DK // 2, 1) * sinv


def _unrope(d, cosv, sinv):
    return d * cosv + pltpu.roll(d * sinv, RET_DK // 2, 1)


def _gla_masks():
    ri = lax.broadcasted_iota(jnp.int32, (CHUNK, CHUNK), 0)
    ci = lax.broadcasted_iota(jnp.int32, (CHUNK, CHUNK), 1)
    return dict(ri=ri, ci=ci, tril=(ri >= ci).astype(F32), heads=_head_block_mask(), own=_state_block_mask())


def _gla_common(p_ref, w2_ref, gb_ref, chunk, rows, masks):
    row = lax.broadcasted_iota(jnp.int32, (CHUNK, 1), 0)
    real = (chunk * CHUNK + row) >= PAD_ROWS
    ga = p_ref[rows, O_GA:O_GA + 128]
    z = _dot(ga, w2_ref[...]) + gb_ref[...]
    la = (jnp.minimum(z, 0.0) - jnp.log(1.0 + jnp.exp(-jnp.abs(z)))) * (1.0 / GLA_TAU)
    la = jnp.where(real, la, 0.0)
    ri, ci = masks["ri"], masks["ci"]
    cum = _dot_exact_rhs(masks["tril"], la)
    last = cum[CHUNK - 1:CHUNK, :]
    qs = p_ref[rows, O_GQ:O_GQ + 256] * (GLA_DK ** -0.5)
    k = p_ref[rows, O_GK:O_GK + 256]
    ecum = jnp.exp(cum)
    ekl = jnp.exp(last - cum)
    el = jnp.exp(last)
    refs = [jnp.zeros((1, 256), F32)] + [cum[a * SUB - 1:a * SUB, :] for a in range(1, N_SUB)]
    eq = [jnp.exp(cum[a * SUB:(a + 1) * SUB, :] - refs[a]) for a in range(N_SUB)]
    spread = refs[0] - cum[SUB - 1:SUB, :]
    for a in range(1, N_SUB):
        spread = jnp.maximum(spread, refs[a] - cum[(a + 1) * SUB - 1:(a + 1) * SUB, :])
    small = jnp.max(spread) <= GLA_FACTORED_MAX
    return dict(real=real, row=row, z=z, la=la, cum=cum, last=last, qs=qs, k=k, ecum=ecum, ekl=ekl, el=el,
                refs=refs, eq=eq, small=small, ri=ri, ci=ci, masks=masks)


GLA_FACTORED_MAX = 40.0


def _head_block_mask():
    r = lax.broadcasted_iota(jnp.int32, (CHUNK, 256), 0)
    col = lax.broadcasted_iota(jnp.int32, (CHUNK, 256), 1)
    return (r // SUB) == (col // GLA_DK)


def _state_block_mask():
    r = lax.broadcasted_iota(jnp.int32, (GLA_HEADS * GLA_DK, GLA_HEADS * GLA_DV), 0)
    col = lax.broadcasted_iota(jnp.int32, (GLA_HEADS * GLA_DK, GLA_HEADS * GLA_DV), 1)
    return (r // GLA_DK) == (col // GLA_DV)


def _block_diagonal(blocks):
    zero = jnp.zeros((GLA_DK, GLA_DV), F32)
    return jnp.concatenate([jnp.concatenate([blocks[h] if g == h else zero for g in range(GLA_HEADS)], axis=1)
                            for h in range(GLA_HEADS)], axis=0)


def _gla_factored(c):
    mask = c["masks"]["heads"]
    eks, keys, queries = [], [], []
    for a in range(N_SUB):
        ek = jnp.exp(jnp.minimum(c["refs"][a] - c["cum"], GLA_FACTORED_MAX))
        qh = c["qs"][a * SUB:(a + 1) * SUB, :] * c["eq"][a]
        eks.append(ek)
        keys.append(c["k"] * ek)
        queries.append(jnp.where(mask, jnp.concatenate([qh] * GLA_HEADS, axis=0), 0.0))
    return eks, keys, queries


def _gla_scores_factored(c, factored, p_scr):
    _, keys, queries = factored
    for a in range(N_SUB):
        out = _dot_nt(queries[a], keys[a])
        out = jnp.where(c["ci"] <= a * SUB + (c["ri"] & (SUB - 1)), out, 0.0)
        for h in range(GLA_HEADS):
            p_scr[h, a * SUB:(a + 1) * SUB, :] = out[h * SUB:(h + 1) * SUB, :]


def _gla_intra_bwd_factored(c, factored, dps, dq_scr, dk_scr):
    eks, keys, queries = factored
    mask = c["masks"]["heads"]
    dk = jnp.zeros((CHUNK, 256), F32)
    for a in range(N_SUB):
        dpa = jnp.concatenate([dps[h][a * SUB:(a + 1) * SUB, :] for h in range(GLA_HEADS)], axis=0)
        dq = jnp.where(mask, _dot(dpa, keys[a]), 0.0)
        dq = dq[0:SUB] + dq[SUB:2 * SUB] + dq[2 * SUB:3 * SUB] + dq[3 * SUB:4 * SUB]
        dq_scr[a * SUB:(a + 1) * SUB, :] = dq * c["eq"][a]
        dk = dk + _dot_tn(dpa, queries[a]) * eks[a]
    dk_scr[...] = dk


def _gla_lag_weights(c):
    cum, row = c["cum"], c["row"]
    out = [jnp.ones((CHUNK, 256), F32)]
    for r in range(1, SUB):
        out.append(jnp.where((row % SUB) >= r, jnp.exp(jnp.minimum(cum - pltpu.roll(cum, r, 0), 0.0)), 0.0))
    return out


def _gla_pairwise_keys(c):
    return [None] + [c["k"] * jnp.exp(jnp.minimum(c["refs"][a] - c["cum"], 0.0)) for a in range(1, N_SUB)]


def _gla_scores_pairwise(c, lag_w, keys, h):
    sl = slice(GLA_DK * h, GLA_DK * (h + 1))
    qs, k = c["qs"][:, sl], c["k"][:, sl]
    ri, ci = c["ri"], c["ci"]
    p = jnp.zeros((CHUNK, CHUNK), F32)
    for r in range(SUB):
        kr = k if r == 0 else pltpu.roll(k, r, 0)
        pr = jnp.sum(qs * kr * lag_w[r][:, sl], axis=1, keepdims=True)
        p = p + jnp.where(ci == ri - r, pr, 0.0)
    blocks = [jnp.zeros((SUB, CHUNK), F32)]
    for a in range(1, N_SUB):
        qh = qs[a * SUB:(a + 1) * SUB, :] * c["eq"][a][:, sl]
        blocks.append(jnp.where(ci[:SUB, :] < a * SUB, _dot_nt(qh, keys[a][:, sl]), 0.0))
    return p + jnp.concatenate(blocks, axis=0)


def _gla_all_scores(c, p_scr, factored):
    if factored:
        _gla_scores_factored(c, _gla_factored(c), p_scr)
    else:
        lag_w, keys = _gla_lag_weights(c), _gla_pairwise_keys(c)
        for h in range(GLA_HEADS):
            p_scr[h] = _gla_scores_pairwise(c, lag_w, keys, h)


def _either_form(chunks, run):
    small = chunks[0]["small"]
    for c in chunks[1:]:
        small = jnp.logical_and(small, c["small"])
    pl.when(small)(lambda: run(True))
    pl.when(jnp.logical_not(small))(lambda: run(False))


def _gla_intra_bwd_pairwise(c, lag_w, keys, dp, h):
    sl = slice(GLA_DK * h, GLA_DK * (h + 1))
    qs_h, k_h = c["qs"][:, sl], c["k"][:, sl]
    ri, ci = c["ri"], c["ci"]
    dq_rows = [jnp.zeros((SUB, GLA_DK), F32)]
    dk = jnp.zeros((CHUNK, GLA_DK), F32)
    for a in range(1, N_SUB):
        eq = c["eq"][a][:, sl]
        qh = qs_h[a * SUB:(a + 1) * SUB, :] * eq
        dpa = jnp.where(ci[:SUB, :] < a * SUB, dp[a * SUB:(a + 1) * SUB, :], 0.0)
        dq_rows.append(_dot(dpa, keys[a][:, sl]) * eq)
        ek = jnp.exp(jnp.minimum(c["refs"][a][:, sl] - c["cum"][:, sl], 0.0))
        dk = dk + _dot_tn(dpa, qh) * ek
    dq = jnp.concatenate(dq_rows, axis=0)
    for r in range(SUB):
        w = lag_w[r][:, sl]
        dpr = jnp.sum(jnp.where(ci == ri - r, dp, 0.0), axis=1, keepdims=True)
        kr = k_h if r == 0 else pltpu.roll(k_h, r, 0)
        dq = dq + dpr * kr * w
        back = dpr * qs_h * w
        dk = dk + (back if r == 0 else pltpu.roll(back, CHUNK - r, 0))
    return dq, dk


def _gla_all_intra_bwd(c, dps, p_scr, dq_scr, dk_scr, factored):
    if factored:
        terms = _gla_factored(c)
        _gla_scores_factored(c, terms, p_scr)
        _gla_intra_bwd_factored(c, terms, dps, dq_scr, dk_scr)
    else:
        lag_w, keys = _gla_lag_weights(c), _gla_pairwise_keys(c)
        outs = [_gla_intra_bwd_pairwise(c, lag_w, keys, dps[h], h) for h in range(GLA_HEADS)]
        for h in range(GLA_HEADS):
            p_scr[h] = _gla_scores_pairwise(c, lag_w, keys, h)
        dq_scr[...] = jnp.concatenate([o[0] for o in outs], axis=1)
        dk_scr[...] = jnp.concatenate([o[1] for o in outs], axis=1)


def _mixer_fwd(proj, cos2, sin2, w2p, gb, rnw, gnw, name, carried=()):
    n_carried = len(carried)

    def body(*refs):
        p_ref, c_ref, s_ref, w2_ref, gb_ref, rnw_ref, gnw_ref = refs[:7]
        x_refs, refs = refs[7:7 + n_carried], refs[7 + n_carried:]
        ocat_ref, mrg_ref, sr_out, sg_out = refs[:4]
        gathered_refs, refs = refs[4:4 + n_carried], refs[4 + n_carried:]
        sr, sg, p_scr = refs[:3]
        n = pl.program_id(0)
        if n_carried:
            start, forward, finish = _gather_phases(x_refs, gathered_refs, *refs[3:])
            pl.when(n == 0)(start)
            pl.when(n == (3 * N_STEPS) // 4)(forward)

        @pl.when(n == 0)
        def _():
            sr[...] = jnp.zeros_like(sr)
            sg[...] = jnp.zeros_like(sg)

        sr_out[0] = sr[...]
        cosv, sinv = c_ref[...], s_ref[...]

        for h in range(RET_HEADS):
            dmat, zeta, xi, gc = _ret_consts(h)
            hs = slice(128 * h, 128 * (h + 1))
            q = _rope(p_ref[:, O_RQ + 128 * h:O_RQ + 128 * (h + 1)], cosv, sinv)
            k = _rope(p_ref[:, O_RK + 128 * h:O_RK + 128 * (h + 1)], cosv, sinv) * (RET_DK ** -0.5)
            v = p_ref[:, O_RV + 128 * h:O_RV + 128 * (h + 1)]
            g = p_ref[:, O_RG + 128 * h:O_RG + 128 * (h + 1)]
            s_in = sr[h]
            a = _dot_nt(q, k) * dmat
            o = _dot(a, v) + _dot(q, s_in) * xi
            sr[h] = gc * s_in + _dot_tn(k * zeta, v)
            mu = jnp.mean(o, axis=-1, keepdims=True)
            xc = o - mu
            nrm = xc * lax.rsqrt(jnp.mean(xc * xc, axis=-1, keepdims=True) + EPS)
            ocat_ref[:, hs] = o
            mrg_ref[:, hs] = (nrm * rnw_ref[:, hs] * (g * _sigmoid(g))).astype(BF16)

        row_slices = [slice(CHUNK * j, CHUNK * (j + 1)) for j in range(CHUNKS_PER_STEP)]
        masks = _gla_masks()
        chunks = [_gla_common(p_ref, w2_ref, gb_ref, n * CHUNKS_PER_STEP + j, rows, masks)
                  for j, rows in enumerate(row_slices)]

        def gla_chunks(factored):
            own = masks["own"]
            for j, (rows, c) in enumerate(zip(row_slices, chunks)):
                s_in = sg[...]
                for h in range(GLA_HEADS):
                    sg_out[j, h] = s_in[GLA_DK * h:GLA_DK * (h + 1), GLA_DV * h:GLA_DV * (h + 1)]
                _gla_all_scores(c, p_scr.at[j], factored)
                v_all = p_ref[rows, O_GV:O_GV + GLA_HEADS * GLA_DV]
                o_inter = _dot(c["qs"] * c["ecum"], s_in)
                decay = jnp.exp(_dot_tn_exact_lhs(c["la"], jnp.ones((CHUNK, GLA_HEADS * GLA_DV), F32)))
                sg[...] = decay * s_in + jnp.where(own, _dot_tn(c["k"] * c["ekl"], v_all), 0.0)
                o_intra = _dot(p_scr[j].reshape(GLA_HEADS * CHUNK, CHUNK), v_all)
                for h in range(GLA_HEADS):
                    hs = slice(512 + 128 * h, 512 + 128 * (h + 1))
                    g = p_ref[rows, O_GR + 128 * h:O_GR + 128 * (h + 1)]
                    o = (o_intra[CHUNK * h:CHUNK * (h + 1), GLA_DV * h:GLA_DV * (h + 1)]
                         + o_inter[:, GLA_DV * h:GLA_DV * (h + 1)])
                    nrm = o * lax.rsqrt(jnp.mean(o * o, axis=-1, keepdims=True) + EPS)
                    ocat_ref[rows, hs] = o
                    mrg_ref[rows, hs] = (nrm * gnw_ref[:, 128 * h:128 * (h + 1)] * (g * _sigmoid(g))).astype(BF16)

        _either_form(chunks, gla_chunks)

        if n_carried:
            pl.when(n == N_STEPS - 1)(finish)

    const = lambda shape: pl.BlockSpec(shape, lambda n: (0,) * len(shape))
    anywhere = [pl.BlockSpec(memory_space=pl.ANY)] * n_carried
    return pl.pallas_call(
        body, name=name, grid=(N_STEPS,),
        in_specs=[pl.BlockSpec((STEP_ROWS, IN_WP), lambda n: (n, 0)),
                  pl.BlockSpec((STEP_ROWS, 128), lambda n: (n, 0)), pl.BlockSpec((STEP_ROWS, 128), lambda n: (n, 0)),
                  const((128, 256)), const((1, 256)), const((1, 512)), const((1, 512))] + anywhere,
        out_specs=[pl.BlockSpec((STEP_ROWS, D), lambda n: (n, 0)), pl.BlockSpec((STEP_ROWS, D), lambda n: (n, 0)),
                   pl.BlockSpec((1, RET_HEADS, RET_DK, 128), lambda n: (n, 0, 0, 0)),
                   pl.BlockSpec((CHUNKS_PER_STEP, GLA_HEADS, GLA_DK, GLA_DV), lambda n: (n, 0, 0, 0))] + anywhere,
        out_shape=[jax.ShapeDtypeStruct((LP, D), F32), jax.ShapeDtypeStruct((LP, D), BF16),
                   jax.ShapeDtypeStruct((N_STEPS, RET_HEADS, RET_DK, 128), F32),
                   jax.ShapeDtypeStruct((N_CHUNKS, GLA_HEADS, GLA_DK, GLA_DV), F32)] + _gathered_shapes(carried),
        scratch_shapes=[pltpu.VMEM((RET_HEADS, RET_DK, 128), F32),
                        pltpu.VMEM((GLA_HEADS * GLA_DK, GLA_HEADS * GLA_DV), F32),
                        pltpu.VMEM((CHUNKS_PER_STEP, GLA_HEADS, CHUNK, CHUNK), F32)] + _exchange_sems(n_carried),
        compiler_params=_cparams("arbitrary"),
    )(proj, cos2, sin2, w2p, gb, rnw, gnw, *carried)


def _mixer_bwd(proj, ocat, dmrg, sr_all, sg_all, cos2, sin2, w2p, gb, rnw, gnw, name, carried=()):
    last_step = N_STEPS - 1
    n_carried = len(carried)

    def body(*refs):
        p_ref, ocat_ref, dm_ref, sr_ref, sg_ref, c_ref, s_ref, w2_ref, gb_ref, rnw_ref, gnw_ref = refs[:11]
        g_refs, refs = refs[11:11 + n_carried], refs[11 + n_carried:]
        dp_ref, dw2_ref, dgb_ref, drn_ref, dgn_ref = refs[:5]
        got_refs, refs = refs[5:5 + n_carried], refs[5 + n_carried:]
        dsr, dsg, p_scr, dq_scr, dk_scr = refs[:5]
        step = pl.program_id(0)
        n = last_step - step
        if n_carried:
            start, finish = _exchange_phases(g_refs, got_refs, *refs[5:])
            pl.when(step == 0)(start)

        @pl.when(step == 0)
        def _():
            dsr[...] = jnp.zeros_like(dsr)
            dsg[...] = jnp.zeros_like(dsg)
            dw2_ref[...] = jnp.zeros_like(dw2_ref)
            dgb_ref[...] = jnp.zeros_like(dgb_ref)
            drn_ref[...] = jnp.zeros_like(drn_ref)
            dgn_ref[...] = jnp.zeros_like(dgn_ref)

        cosv, sinv = c_ref[...], s_ref[...]
        step_row = lax.broadcasted_iota(jnp.int32, (STEP_ROWS, 1), 0)
        real = ((n * STEP_ROWS + step_row) >= PAD_ROWS).astype(F32)

        for h in range(RET_HEADS):
            dmat, zeta, xi, gc = _ret_consts(h)
            hs = slice(128 * h, 128 * (h + 1))
            q = _rope(p_ref[:, O_RQ + 128 * h:O_RQ + 128 * (h + 1)], cosv, sinv)
            k = _rope(p_ref[:, O_RK + 128 * h:O_RK + 128 * (h + 1)], cosv, sinv) * (RET_DK ** -0.5)
            v = p_ref[:, O_RV + 128 * h:O_RV + 128 * (h + 1)]
            g = p_ref[:, O_RG + 128 * h:O_RG + 128 * (h + 1)]
            o = ocat_ref[:, hs]
            dy = dm_ref[:, hs]
            wv = rnw_ref[:, hs]
            mu = jnp.mean(o, axis=-1, keepdims=True)
            xc = o - mu
            rs = lax.rsqrt(jnp.mean(xc * xc, axis=-1, keepdims=True) + EPS)
            nrm = xc * rs
            sgm = _sigmoid(g)
            sil = g * sgm
            drn_ref[0:1, hs] += jnp.sum(dy * nrm * sil, axis=0, keepdims=True)
            dgate = dy * nrm * wv * (sgm * (1.0 + g * (1.0 - sgm)))
            dn = dy * wv * sil
            do = rs * (dn - jnp.mean(dn, axis=-1, keepdims=True) - nrm * jnp.mean(dn * nrm, axis=-1, keepdims=True))
            s_in = sr_ref[0, h]
            ds_out = dsr[h]
            a = _dot_nt(q, k) * dmat
            da = _dot_nt(do, v) * dmat
            dox = do * xi
            dq = _dot(da, k) + _dot_nt(dox, s_in)
            dk = _dot_tn(da, q) + _dot_nt(v, ds_out) * zeta
            dv = _dot_tn(a, do) + _dot(k * zeta, ds_out)
            dsr[h] = gc * ds_out + _dot_tn(q, dox)
            dk = dk * (RET_DK ** -0.5)
            dp_ref[:, O_RQ + 128 * h:O_RQ + 128 * (h + 1)] = (_unrope(dq, cosv, sinv) * real).astype(BF16)
            dp_ref[:, O_RK + 128 * h:O_RK + 128 * (h + 1)] = (_unrope(dk, cosv, sinv) * real).astype(BF16)
            dp_ref[:, O_RV + 128 * h:O_RV + 128 * (h + 1)] = (dv * real).astype(BF16)
            dp_ref[:, O_RG + 128 * h:O_RG + 128 * (h + 1)] = (dgate * real).astype(BF16)

        row_slices = [slice(CHUNK * j, CHUNK * (j + 1)) for j in range(CHUNKS_PER_STEP)]
        masks = _gla_masks()
        chunks = [_gla_common(p_ref, w2_ref, gb_ref, n * CHUNKS_PER_STEP + j, rows, masks)
                  for j, rows in enumerate(row_slices)]

        def gla_chunks(factored):
            for j in reversed(range(CHUNKS_PER_STEP)):
                gla_chunk_bwd(chunks[j], n * CHUNKS_PER_STEP + j, row_slices[j], j, factored, p_ref, ocat_ref, dm_ref,
                              sg_ref, w2_ref, gnw_ref, dp_ref, dw2_ref, dgb_ref, dgn_ref, dsg, p_scr, dq_scr, dk_scr)

        _either_form(chunks, gla_chunks)
        if n_carried:
            pl.when(step == last_step)(finish)

    def gla_chunk_bwd(c, chunk, rows, j, factored, p_ref, ocat_ref, dm_ref, sg_ref, w2_ref, gnw_ref,
                      dp_ref, dw2_ref, dgb_ref, dgn_ref, dsg, p_scr, dq_scr, dk_scr):
        row = lax.broadcasted_iota(jnp.int32, (CHUNK, 1), 0)
        real = ((chunk * CHUNK + row) >= PAD_ROWS).astype(F32)
        ri, ci = c["ri"], c["ci"]
        causal = ri >= ci
        triu = (ci >= ri).astype(F32)
        qe = c["qs"] * c["ecum"]
        kl = c["k"] * c["ekl"]
        v_all = p_ref[rows, O_GV:O_GV + GLA_HEADS * GLA_DV]
        dos, dps = [], []
        for h in range(GLA_HEADS):
            hs = slice(512 + 128 * h, 512 + 128 * (h + 1))
            g = p_ref[rows, O_GR + 128 * h:O_GR + 128 * (h + 1)]
            o = ocat_ref[rows, hs]
            dy = dm_ref[rows, hs]
            wv = gnw_ref[:, 128 * h:128 * (h + 1)]
            rs = lax.rsqrt(jnp.mean(o * o, axis=-1, keepdims=True) + EPS)
            nrm = o * rs
            sgm = _sigmoid(g)
            sil = g * sgm
            dgn_ref[0:1, 128 * h:128 * (h + 1)] += jnp.sum(dy * nrm * sil, axis=0, keepdims=True)
            dgate = dy * nrm * wv * (sgm * (1.0 + g * (1.0 - sgm)))
            dn = dy * wv * sil
            do = rs * (dn - nrm * jnp.mean(dn * nrm, axis=-1, keepdims=True))
            dp_ref[rows, O_GR + 128 * h:O_GR + 128 * (h + 1)] = (dgate * real).astype(BF16)
            dos.append(do)
        do_all = jnp.concatenate(dos, axis=1)
        do_blocks = jnp.where(c["masks"]["own"], jnp.concatenate([do_all] * GLA_HEADS, axis=0), 0.0)
        dp_all = _dot_nt(do_blocks, v_all)
        dps = [jnp.where(causal, dp_all[CHUNK * h:CHUNK * (h + 1), :], 0.0) for h in range(GLA_HEADS)]
        _gla_all_intra_bwd(c, dps, p_scr.at[j], dq_scr.at[j], dk_scr.at[j], factored)
        s_in = _block_diagonal([sg_ref[j, h] for h in range(GLA_HEADS)])
        ds_out = dsg[...]
        decay = jnp.exp(_dot_tn_exact_lhs(c["la"], jnp.ones((CHUNK, GLA_HEADS * GLA_DV), F32)))
        dv_state = _dot(kl, ds_out)
        dqe = _dot_nt(do_all, s_in)
        dkl = _dot_nt(v_all, ds_out)
        dsg[...] = jnp.where(c["masks"]["own"], _dot_tn(qe, do_all), 0.0) + decay * ds_out
        sd = s_in * ds_out
        sd_hi = sd.astype(BF16)
        sd_lo = (sd - sd_hi.astype(F32)).astype(BF16)
        ones8 = jnp.ones((8, GLA_HEADS * GLA_DV), BF16)
        nt = (((1,), (1,)), ((), ()))
        d_el = (lax.dot_general(ones8, sd_hi, nt, preferred_element_type=F32)
                + lax.dot_general(ones8, sd_lo, nt, preferred_element_type=F32))[0:1, :]
        dqs = dqe * c["ecum"] + dq_scr[j]
        dkk = dkl * c["ekl"] + dk_scr[j]
        d_last = jnp.sum(dkl * kl, axis=0, keepdims=True) + d_el * c["el"]
        dcum = c["qs"] * dqs - c["k"] * dkk + jnp.where(row == CHUNK - 1, d_last, 0.0)
        dla = _dot_exact_rhs(triu, dcum)
        dv = _dot_tn(p_scr[j].reshape(GLA_HEADS * CHUNK, CHUNK), do_blocks) + dv_state
        dp_ref[rows, O_GV:O_GV + GLA_HEADS * GLA_DV] = (dv * real).astype(BF16)
        dp_ref[rows, O_GQ:O_GQ + 256] = (dqs * (GLA_DK ** -0.5) * real).astype(BF16)
        dp_ref[rows, O_GK:O_GK + 256] = (dkk * real).astype(BF16)
        dz = dla * (1.0 / GLA_TAU) * _sigmoid(-c["z"]) * real
        ga = p_ref[rows, O_GA:O_GA + 128]
        dp_ref[rows, O_GA:O_GA + 128] = _dot_nt(dz, w2_ref[...]).astype(BF16)
        dp_ref[rows, O_GA + 128:IN_WP] = jnp.zeros((CHUNK, IN_WP - O_GA - 128), BF16)
        dw2_ref[...] += _dot_tn(ga, dz)
        dgb_ref[0:1, :] += jnp.sum(dz, axis=0, keepdims=True)

    const = lambda shape: pl.BlockSpec(shape, lambda s: (0,) * len(shape))
    rev = lambda s: (last_step - s, 0)
    anywhere = [pl.BlockSpec(memory_space=pl.ANY)] * n_carried
    return pl.pallas_call(
        body, name=name, grid=(N_STEPS,),
        in_specs=[pl.BlockSpec((STEP_ROWS, IN_WP), rev), pl.BlockSpec((STEP_ROWS, D), rev),
                  pl.BlockSpec((STEP_ROWS, D), rev),
                  pl.BlockSpec((1, RET_HEADS, RET_DK, 128), lambda s: (last_step - s, 0, 0, 0)),
                  pl.BlockSpec((CHUNKS_PER_STEP, GLA_HEADS, GLA_DK, GLA_DV), lambda s: (last_step - s, 0, 0, 0)),
                  pl.BlockSpec((STEP_ROWS, 128), rev), pl.BlockSpec((STEP_ROWS, 128), rev),
                  const((128, 256)), const((1, 256)), const((1, 512)), const((1, 512))] + anywhere,
        out_specs=[pl.BlockSpec((STEP_ROWS, IN_WP), rev), const((128, 256)), const((8, 256)),
                   const((8, 512)), const((8, 512))] + anywhere,
        out_shape=[jax.ShapeDtypeStruct((LP, IN_WP), BF16), jax.ShapeDtypeStruct((128, 256), F32),
                   jax.ShapeDtypeStruct((8, 256), F32), jax.ShapeDtypeStruct((8, 512), F32),
                   jax.ShapeDtypeStruct((8, 512), F32)] + [jax.ShapeDtypeStruct(g.shape, g.dtype) for g in carried],
        scratch_shapes=[pltpu.VMEM((RET_HEADS, RET_DK, 128), F32),
                        pltpu.VMEM((GLA_HEADS * GLA_DK, GLA_HEADS * GLA_DV), F32),
                        pltpu.VMEM((CHUNKS_PER_STEP, GLA_HEADS, CHUNK, CHUNK), F32),
                        pltpu.VMEM((CHUNKS_PER_STEP, CHUNK, 256), F32),
                        pltpu.VMEM((CHUNKS_PER_STEP, CHUNK, 256), F32)] + _exchange_sems(n_carried),
        compiler_params=_cparams("arbitrary"),
    )(proj, ocat, dmrg, sr_all, sg_all, cos2, sin2, w2p, gb, rnw, gnw, *carried)


def _all_gather(xs, name):
    n = len(xs)

    def body(*refs):
        start, forward, finish = _gather_phases(refs[:n], refs[n:2 * n], *refs[2 * n:])
        start()
        forward()
        finish()

    return pl.pallas_call(
        body, name=name,
        in_specs=[pl.BlockSpec(memory_space=pl.ANY)] * n,
        out_specs=[pl.BlockSpec(memory_space=pl.ANY)] * n,
        out_shape=_gathered_shapes(xs),
        scratch_shapes=_exchange_sems(n),
    )(*xs)


def _gathered_shapes(xs):
    return [jax.ShapeDtypeStruct((N_DEV,) + x.shape, x.dtype) for x in xs]


def _exchange_sems(n):
    if n == 0:
        return []
    return [pltpu.SemaphoreType.DMA((7 * n,)), pltpu.SemaphoreType.DMA((7 * n,)), pltpu.SemaphoreType.DMA((n,))]


def _gather_phases(x_refs, out_refs, send_sems, recv_sems, local_sems):
    n = len(x_refs)
    mx, my, mc = lax.axis_index("x"), lax.axis_index("y"), lax.axis_index("c")
    me, sibling = (mx, my, mc), (mx, my, 1 - mc)
    chips = [(1 - mx, my), (mx, 1 - my), (1 - mx, 1 - my)]

    def slot(a, px, py, pc):
        return out_refs[a].at[4 * px + 2 * py + pc]

    def copy(a, k, block, to, src=None):
        return pltpu.make_async_remote_copy(
            src_ref=slot(a, *block) if src is None else src, dst_ref=slot(a, *block),
            send_sem=send_sems.at[7 * a + k], recv_sem=recv_sems.at[7 * a + k],
            device_id=to, device_id_type=MESH_IDS)

    mine = [pltpu.make_async_copy(x_refs[a], slot(a, *me), local_sems.at[a]) for a in range(n)]
    first = []
    for a in range(n):
        first.append(copy(a, 0, me, sibling, src=x_refs[a]))
        first += [copy(a, 1 + j, me, (*chip, mc), src=x_refs[a]) for j, chip in enumerate(chips)]
    passed = [copy(a, 4 + j, (*chip, mc), sibling) for j, chip in enumerate(chips) for a in range(n)]

    def start():
        for cp in mine + first:
            cp.start()

    def forward():
        for j, chip in enumerate(chips):
            for a in range(n):
                copy(a, 1 + j, (*chip, mc), me).wait_recv()
                passed[j * n + a].start()

    def finish():
        for a in range(n):
            copy(a, 0, sibling, me).wait_recv()
            for j, chip in enumerate(chips):
                copy(a, 4 + j, (*chip, 1 - mc), me).wait_recv()
        for cp in first + passed:
            cp.wait_send()
        for cp in mine:
            cp.wait()

    return start, forward, finish


def _exchange_blocks(gs, name):
    n = len(gs)

    def body(*refs):
        start, finish = _exchange_phases(refs[:n], refs[n:2 * n], *refs[2 * n:])
        start()
        finish()

    return pl.pallas_call(
        body, name=name,
        in_specs=[pl.BlockSpec(memory_space=pl.ANY)] * n,
        out_specs=[pl.BlockSpec(memory_space=pl.ANY)] * n,
        out_shape=[jax.ShapeDtypeStruct(g.shape, g.dtype) for g in gs],
        scratch_shapes=_exchange_sems(n),
    )(*gs)


def _exchange_phases(g_refs, out_refs, send_sems, recv_sems, local_sems):
    n = len(g_refs)
    mx, my, mc = lax.axis_index("x"), lax.axis_index("y"), lax.axis_index("c")
    me = 4 * mx + 2 * my + mc
    mine = [pltpu.make_async_copy(g_refs[a].at[me], out_refs[a].at[me], local_sems.at[a]) for a in range(n)]
    copies = []
    for r in range(1, N_DEV):
        px, py, pc = mx ^ (r >> 2), my ^ ((r >> 1) & 1), mc ^ (r & 1)
        peer = 4 * px + 2 * py + pc
        for a in range(n):
            copies.append(pltpu.make_async_remote_copy(
                src_ref=g_refs[a].at[peer], dst_ref=out_refs[a].at[me],
                send_sem=send_sems.at[7 * a + r - 1], recv_sem=recv_sems.at[7 * a + r - 1],
                device_id=(px, py, pc), device_id_type=MESH_IDS))

    def start():
        for cp in mine + copies:
            cp.start()

    def finish():
        for cp in copies:
            cp.wait_recv()
        for cp in copies:
            cp.wait_send()
        for cp in mine:
            cp.wait()

    return start, finish


IN_SHARD = IN_W // N_DEV
IN_SHARD_P = 512
UP_SHARD = D_UP // N_DEV
UP_SHARD_P = 768
RELAYOUT_ROWS = 256


def _pieces_w_in():
    return [(k, 0, IN_SHARD * k, IN_SHARD) for k in range(N_DEV)]


def _pieces_ffn_up():
    pieces = []
    for k in range(N_DEV):
        n, end = UP_SHARD * k, UP_SHARD * (k + 1)
        while n < end:
            half, r = divmod(n, D_FF)
            blk, off = divmod(r, CONV_BLOCK)
            run = min(CONV_BLOCK - off, end - n)
            pieces.append((k, n - UP_SHARD * k, 2 * CONV_BLOCK * blk + CONV_BLOCK * half + off, run))
            n += run
    return pieces


def _assemble_block(load, spans, dst_block, rows):
    lo = 128 * dst_block
    lane = lax.broadcasted_iota(jnp.int32, (1, 128), 1)
    out = jnp.zeros((rows, 128), F32)
    for key, src_off, dst_off, length in spans:
        a, b = max(lo, dst_off), min(lo + 128, dst_off + length)
        s, s_end = src_off + (a - dst_off), src_off + (b - dst_off)
        d = a
        while s < s_end:
            e = min(s_end, 128 * (s // 128 + 1))
            blk = load(key, s // 128)
            shift = (d - s) % 128
            if shift:
                blk = pltpu.roll(blk, shift, 1)
            out = jnp.where((lane >= d - lo) & (lane < d - lo + (e - s)), blk, out)
            d += e - s
            s = e
    return out


def _shards_to_cols(shards, pieces, width, name):
    _, rows, _ = shards.shape
    tr = RELAYOUT_ROWS

    def body(s_ref, o_ref):
        load = lambda k, b: s_ref[k, :, 128 * b:128 * (b + 1)].astype(F32)
        for db in range(width // 128):
            o_ref[:, 128 * db:128 * (db + 1)] = _assemble_block(load, pieces, db, tr).astype(BF16)

    return pl.pallas_call(
        body, name=name, grid=(rows // tr,),
        in_specs=[pl.BlockSpec((N_DEV, tr, shards.shape[2]), lambda i: (0, i, 0))],
        out_specs=pl.BlockSpec((tr, width), lambda i: (i, 0)),
        out_shape=jax.ShapeDtypeStruct((rows, width), BF16),
        compiler_params=_cparams("parallel"),
    )(shards)


def _cols_to_shards(full, pieces, shard_width, name):
    rows, width = full.shape
    tr = RELAYOUT_ROWS

    def body(f_ref, o_ref):
        load = lambda _, b: f_ref[:, 128 * b:128 * (b + 1)].astype(F32)
        for k in range(N_DEV):
            spans = [(None, dst_off, src_off, length) for dev, src_off, dst_off, length in pieces if dev == k]
            for db in range(shard_width // 128):
                o_ref[k, :, 128 * db:128 * (db + 1)] = _assemble_block(load, spans, db, tr).astype(BF16)

    return pl.pallas_call(
        body, name=name, grid=(rows // tr,),
        in_specs=[pl.BlockSpec((tr, width), lambda i: (i, 0))],
        out_specs=pl.BlockSpec((N_DEV, tr, shard_width), lambda i: (0, i, 0)),
        out_shape=jax.ShapeDtypeStruct((N_DEV, rows, shard_width), BF16),
        compiler_params=_cparams("parallel"),
    )(full)


def _adamw(parts, w, m, v, rows_per_step, name):
    rows, cols = w.shape
    assert rows % rows_per_step == 0 and parts.shape == (N_DEV, rows, cols)

    def body(p_ref, w_ref, m_ref, v_ref, g_ref, d_ref, nm_ref, nv_ref):
        g = p_ref[0].astype(F32)
        for j in range(1, N_DEV):
            g = g + p_ref[j].astype(F32)
        m_new = ADAM_B1 * m_ref[...] + (1.0 - ADAM_B1) * g
        v_new = ADAM_B2 * v_ref[...] + (1.0 - ADAM_B2) * (g * g)
        m_hat = m_new / (1.0 - ADAM_B1 ** ADAM_STEP)
        v_hat = v_new / (1.0 - ADAM_B2 ** ADAM_STEP)
        g_ref[...] = g
        d_ref[...] = -ADAM_LR * (m_hat / (jnp.sqrt(v_hat) + ADAM_EPS) + ADAM_WD * w_ref[...])
        nm_ref[...] = m_new
        nv_ref[...] = v_new

    tile = pl.BlockSpec((rows_per_step, cols), lambda i: (i, 0))
    shape = jax.ShapeDtypeStruct((rows, cols), F32)
    return pl.pallas_call(
        body, name=name, grid=(rows // rows_per_step,),
        in_specs=[pl.BlockSpec((N_DEV, rows_per_step, cols), lambda i: (0, i, 0)), tile, tile, tile],
        out_specs=[tile, tile, tile, tile],
        out_shape=[shape, shape, shape, shape],
        compiler_params=_cparams("parallel"),
    )(parts, w, m, v)


BIG = (("w_in", (DEPTH, D, IN_W // N_DEV), 2), ("w_out", (DEPTH, D // N_DEV, D), 1),
       ("ffn_up", (DEPTH, D, D_UP // N_DEV), 2), ("ffn_down", (DEPTH, D_FF // N_DEV, D), 1))
SMALL = (("meta_tokens", (N_META, D // N_DEV), 1), ("gla_gate_w2", (DEPTH, GATE_RANK, 256 // N_DEV), 2),
         ("ffn_conv_w", (DEPTH, 3, D_UP // N_DEV), 2))
REPL = (("pre_mix_norm", (DEPTH, D)), ("gla_gate_b", (DEPTH, 256)), ("ret_norm_w", (DEPTH, 512)),
        ("gla_norm_w", (DEPTH, 512)), ("post_mix_norm", (DEPTH, D)), ("pre_ffn_norm", (DEPTH, D)),
        ("ffn_conv_b", (DEPTH, D_UP)), ("post_ffn_norm", (DEPTH, D)))
WEIGHT_ORDER = ("meta_tokens", "pre_mix_norm", "w_in", "gla_gate_w2", "gla_gate_b", "ret_norm_w", "gla_norm_w",
                "w_out", "post_mix_norm", "pre_ffn_norm", "ffn_up", "ffn_conv_w", "ffn_conv_b", "ffn_down",
                "post_ffn_norm")


def _size(shape):
    return math.prod(shape)


def _round_up(n, mult):
    return -(-n // mult) * mult


REPL_ROWS = _round_up(-(-sum(_size(s) for _, s in REPL) // LANES), 8)
SMALL_ROWS = _round_up(-(-sum(_size(s) for _, s, _ in SMALL) // LANES), 8)


def _pack(arrays, rows, dtype):
    flat = jnp.concatenate([a.reshape(-1).astype(dtype) for a in arrays])
    return jnp.pad(flat, (0, rows * LANES - flat.shape[0])).reshape(rows, LANES)


def _unpack(buf, shapes):
    flat = buf.reshape(-1)
    out, off = [], 0
    for shape in shapes:
        out.append(flat[off:off + _size(shape)].reshape(shape))
        off += _size(shape)
    return out


def _unshard(blocks, axis):
    moved = jnp.moveaxis(blocks, 0, axis)
    shape = list(moved.shape)
    shape[axis:axis + 2] = [shape[axis] * shape[axis + 1]]
    return moved.reshape(shape)


def _to_blocks(full, axis):
    shape = list(full.shape)
    shape[axis:axis + 1] = [N_DEV, shape[axis] // N_DEV]
    return jnp.moveaxis(full.reshape(shape), axis, 0)


def _interleave_cols(w):
    lead = w.shape[:-1]
    return jnp.swapaxes(w.reshape(lead + (2, N_CONV_BLOCKS, CONV_BLOCK)), -3, -2).reshape(lead + (D_UP,))


def _deinterleave_cols(w):
    lead = w.shape[:-1]
    return jnp.swapaxes(w.reshape(lead + (N_CONV_BLOCKS, 2, CONV_BLOCK)), -3, -2).reshape(lead + (D_UP,))


def _rope_tables():
    half = RET_DK // 2
    inv = ROPE_BASE ** (-jnp.arange(half, dtype=F32) / half)
    pos = jnp.arange(LP, dtype=F32) - float(PAD_ROWS)
    ang = pos[:, None] * inv[None, :]
    c, s = jnp.cos(ang), jnp.sin(ang)
    return jnp.concatenate([c, c], axis=1), jnp.concatenate([-s, s], axis=1)


def kernel(x, meta_tokens, pre_mix_norm, w_in, gla_gate_w2, gla_gate_b, ret_norm_w, gla_norm_w, w_out, post_mix_norm, pre_ffn_norm, ffn_up, ffn_conv_w, ffn_conv_b, ffn_down, post_ffn_norm, loss_target, m_meta_tokens, m_pre_mix_norm, m_w_in, m_gla_gate_w2, m_gla_gate_b, m_ret_norm_w, m_gla_norm_w, m_w_out, m_post_mix_norm, m_pre_ffn_norm, m_ffn_up, m_ffn_conv_w, m_ffn_conv_b, m_ffn_down, m_post_ffn_norm, v_meta_tokens, v_pre_mix_norm, v_w_in, v_gla_gate_w2, v_gla_gate_b, v_ret_norm_w, v_gla_norm_w, v_w_out, v_post_mix_norm, v_pre_ffn_norm, v_ffn_up, v_ffn_conv_w, v_ffn_conv_b, v_ffn_down, v_post_ffn_norm):
    weights = dict(meta_tokens=meta_tokens, pre_mix_norm=pre_mix_norm, w_in=w_in, gla_gate_w2=gla_gate_w2,
                   gla_gate_b=gla_gate_b, ret_norm_w=ret_norm_w, gla_norm_w=gla_norm_w, w_out=w_out,
                   post_mix_norm=post_mix_norm, pre_ffn_norm=pre_ffn_norm, ffn_up=ffn_up, ffn_conv_w=ffn_conv_w,
                   ffn_conv_b=ffn_conv_b, ffn_down=ffn_down, post_ffn_norm=post_ffn_norm)
    mom1 = dict(meta_tokens=m_meta_tokens, pre_mix_norm=m_pre_mix_norm, w_in=m_w_in, gla_gate_w2=m_gla_gate_w2,
                gla_gate_b=m_gla_gate_b, ret_norm_w=m_ret_norm_w, gla_norm_w=m_gla_norm_w, w_out=m_w_out,
                post_mix_norm=m_post_mix_norm, pre_ffn_norm=m_pre_ffn_norm, ffn_up=m_ffn_up,
                ffn_conv_w=m_ffn_conv_w, ffn_conv_b=m_ffn_conv_b, ffn_down=m_ffn_down, post_ffn_norm=m_post_ffn_norm)
    mom2 = dict(meta_tokens=v_meta_tokens, pre_mix_norm=v_pre_mix_norm, w_in=v_w_in, gla_gate_w2=v_gla_gate_w2,
                gla_gate_b=v_gla_gate_b, ret_norm_w=v_ret_norm_w, gla_norm_w=v_gla_norm_w, w_out=v_w_out,
                post_mix_norm=v_post_mix_norm, pre_ffn_norm=v_pre_ffn_norm, ffn_up=v_ffn_up,
                ffn_conv_w=v_ffn_conv_w, ffn_conv_b=v_ffn_conv_b, ffn_down=v_ffn_down, post_ffn_norm=v_post_ffn_norm)

    pad_cols = lambda a, width: jnp.pad(a, ((0, 0), (0, width - a.shape[1])))
    big_names = [n for n, _, _ in BIG]
    shard = {}
    for l in range(DEPTH):
        shard[l, "w_in"] = pad_cols(w_in[l].astype(BF16), IN_SHARD_P)
        shard[l, "w_out"] = w_out[l].astype(BF16)
        shard[l, "ffn_up"] = pad_cols(ffn_up[l].astype(BF16), UP_SHARD_P)
        shard[l, "ffn_down"] = ffn_down[l].astype(BF16)
    w_in_0, small = _all_gather([shard[0, "w_in"], _pack([weights[n] for n, _, _ in SMALL], SMALL_ROWS, F32)],
                                "gather_first_weights")
    gathered = {(0, "w_in"): w_in_0}
    gather_in_mixer = {l: [(l, n) for n in big_names[1:]] for l in range(DEPTH)}
    gather_in_conv = {l: [(l + 1, "w_in")] for l in range(DEPTH - 1)}
    small_parts = _unpack_blocks(small, [s for _, s, _ in SMALL])
    full = {n: _unshard(p, ax) for (n, _, ax), p in zip(SMALL, small_parts)}
    w2p = jnp.pad(full["gla_gate_w2"], ((0, 0), (0, 128 - GATE_RANK), (0, 0)))
    cw8 = jnp.concatenate([_interleave_cols(full["ffn_conv_w"]), _interleave_cols(ffn_conv_b)[:, None, :],
                           jnp.zeros((DEPTH, 4, D_UP), F32)], axis=1)
    cos2, sin2 = _rope_tables()

    h = jnp.concatenate([jnp.zeros((PAD_ROWS, D), F32), full["meta_tokens"], x[0]], axis=0)
    target = jnp.concatenate([jnp.zeros((CHUNK, D), F32), loss_target[0]], axis=0)
    saved, layer_w = [], []
    for l in range(DEPTH):
        lw = dict(w_in=_shards_to_cols(gathered[l, "w_in"], _pieces_w_in(), IN_WP, f"w_in_cols_{l}"))
        a1, proj = _norm_matmul(h, pre_mix_norm[l:l + 1], lw["w_in"], out_dtype=F32, tm=TM_BIG, tn=IN_WP // 2,
                                name=f"in_proj_{l}")
        keys = gather_in_mixer.get(l, [])
        ocat, merged, sr_all, sg_all, *got = _mixer_fwd(proj, cos2, sin2, w2p[l], gla_gate_b[l:l + 1],
                                                        ret_norm_w[l:l + 1], gla_norm_w[l:l + 1], f"mixer_fwd_{l}",
                                                        carried=[shard[key] for key in keys])
        gathered.update(zip(keys, got))
        lw["w_out"] = gathered[l, "w_out"].reshape(D, D)
        lw["w_up"] = _shards_to_cols(gathered[l, "ffn_up"], _pieces_ffn_up(), D_UP, f"ffn_up_cols_{l}")
        lw["w_down"] = gathered[l, "ffn_down"].reshape(D_FF, D)
        layer_w.append(lw)
        m, h1 = _matmul_resid_norm(merged, lw["w_out"], h, post_mix_norm[l:l + 1], f"out_proj_{l}")
        a2, u = _norm_matmul(h1, pre_ffn_norm[l:l + 1], lw["w_up"], out_dtype=BF16, tm=TM_BIG, tn=D_UP // 2,
                             name=f"ffn_up_{l}")
        keys = gather_in_conv.get(l, [])
        cv, act, *got = _conv_act_fwd(u, cw8[l], f"ffn_conv_act_{l}", carried=[shard[key] for key in keys])
        gathered.update(zip(keys, got))
        f, h2, *loss_acc = _matmul_resid_norm(act, lw["w_down"], h1, post_ffn_norm[l:l + 1], f"ffn_down_{l}",
                                              target=target if l == DEPTH - 1 else None)
        saved.append(dict(h=h, a1=a1, proj=proj, ocat=ocat, merged=merged, sr=sr_all, sg=sg_all, m=m, h1=h1,
                          a2=a2, u=u, cv=cv, act=act, f=f))
        h = h2

    dh = h
    loss = lax.psum(loss_acc[0][0, 0], ("x", "y", "c"))

    kinds = ("grad", "delta", "new_m", "new_v")
    grads = {n: [None] * DEPTH for n in WEIGHT_ORDER if n != "meta_tokens" and n not in big_names}
    pending, parts = [], {}
    for l in reversed(range(DEPTH)):
        s, lw = saved[l], layer_w[l]
        dact, df, g_post_ffn = _norm_bwd_matmul(dh, s["f"], post_ffn_norm[l:l + 1], lw["w_down"], BF16,
                                                f"ffn_down_dx_{l}")
        g_down = _matmul(s["act"], df, ta=True, out_dtype=BF16, tm=D_FF // 2, tn=D, tk=TK_LONG, name=f"ffn_down_dw_{l}")
        du, dcw = _conv_act_bwd(dact, s["cv"], s["u"], cw8[l], f"ffn_conv_act_bwd_{l}")
        dh1, g_pre_ffn = _matmul_norm_bwd(du, lw["w_up"], s["h1"], pre_ffn_norm[l:l + 1], dh, D_FF, f"ffn_up_dx_{l}")
        g_up = _matmul(s["a2"], du, ta=True, out_dtype=BF16, tm=D, tn=D_UP // 4, tk=TK_LONG, name=f"ffn_up_dw_{l}")
        dmerged, dm, g_post_mix = _norm_bwd_matmul(dh1, s["m"], post_mix_norm[l:l + 1], lw["w_out"], F32,
                                                   f"out_proj_dx_{l}")
        g_out = _matmul(s["merged"], dm, ta=True, out_dtype=BF16, tm=D, tn=D, tk=TK_LONG, name=f"out_proj_dw_{l}")
        pending += [((l, "ffn_down"), g_down.reshape(N_DEV, D_FF // N_DEV, D)),
                    ((l, "ffn_up"), _cols_to_shards(g_up, _pieces_ffn_up(), UP_SHARD_P, f"ffn_up_grad_shards_{l}")),
                    ((l, "w_out"), g_out.reshape(N_DEV, D // N_DEV, D))]
        dproj, g_w2, g_gb, g_rn, g_gn, *got = _mixer_bwd(s["proj"], s["ocat"], dmerged, s["sr"], s["sg"], cos2, sin2,
                                                         w2p[l], gla_gate_b[l:l + 1], ret_norm_w[l:l + 1],
                                                         gla_norm_w[l:l + 1], f"mixer_bwd_{l}",
                                                         carried=[blocks for _, blocks in pending])
        parts.update(zip([key for key, _ in pending], got))
        g_in = _matmul(s["a1"], dproj, ta=True, out_dtype=BF16, tm=D, tn=IN_WP // 3, tk=TK_LONG, name=f"in_proj_dw_{l}")
        pending = [((l, "w_in"), _cols_to_shards(g_in, _pieces_w_in(), IN_SHARD_P, f"w_in_grad_shards_{l}"))]
        now = pending if l == 0 else []
        dh, g_pre_mix, *got = _matmul_norm_bwd(dproj, lw["w_in"], s["h"], pre_mix_norm[l:l + 1], dh1, IN_WP,
                                               f"in_proj_dx_{l}", carried=[blocks for _, blocks in now])
        parts.update(zip([key for key, _ in now], got))
        pending = [] if l == 0 else pending
        grads["post_ffn_norm"][l] = g_post_ffn[0]
        grads["ffn_conv_w"][l] = _deinterleave_cols(dcw[0:3])
        grads["ffn_conv_b"][l] = _deinterleave_cols(dcw[3])
        grads["pre_ffn_norm"][l] = g_pre_ffn[0]
        grads["post_mix_norm"][l] = g_post_mix[0]
        grads["gla_gate_w2"][l] = g_w2[:GATE_RANK]
        grads["gla_gate_b"][l] = g_gb[0]
        grads["ret_norm_w"][l] = g_rn[0]
        grads["gla_norm_w"][l] = g_gn[0]
        grads["pre_mix_norm"][l] = g_pre_mix[0]
    local = {n: jnp.stack(v) for n, v in grads.items()}
    local["meta_tokens"] = dh[PAD_ROWS:CHUNK]
    grad_x = dh[CHUNK:][None]

    blocks = jnp.concatenate([_to_blocks(local[n], ax).reshape(N_DEV, -1) for n, _, ax in SMALL], axis=1)
    blocks = jnp.pad(blocks, ((0, 0), (0, SMALL_ROWS * LANES - blocks.shape[1]))).reshape(N_DEV, SMALL_ROWS, LANES)
    *got, small_grad_parts = _exchange_blocks([b for _, b in pending] + [blocks], "exchange_last_grads")
    parts.update(zip([key for key, _ in pending], got))

    widths = dict(w_in=IN_SHARD_P, w_out=D, ffn_up=UP_SHARD_P, ffn_down=D)
    steps = dict(w_in=256, w_out=D // N_DEV, ffn_up=256, ffn_down=D_FF // N_DEV // 2)
    big_out = {kind: {n: [None] * DEPTH for n in big_names} for kind in kinds}
    for l in range(DEPTH):
        for n in big_names:
            mine = [pad_cols(d[n][l], widths[n]) for d in (weights, mom1, mom2)]
            results = _adamw(parts[l, n], *mine, steps[n], f"adamw_{n}_{l}")
            for kind, r in zip(kinds, results):
                big_out[kind][n][l] = r[:, :weights[n].shape[2]]
    out = {kind: {n: jnp.stack(v) for n, v in big_out[kind].items()} for kind in kinds}
    shard_shapes = [s for _, s, _ in SMALL]
    packed = [_pack([d[n] for n, _, _ in SMALL], SMALL_ROWS, F32) for d in (weights, mom1, mom2)]
    results = _adamw(small_grad_parts, *packed, SMALL_ROWS, "adamw_small_sharded")
    for kind, buf in zip(kinds, results):
        out[kind].update(zip([n for n, _, _ in SMALL], _unpack(buf, shard_shapes)))

    repl_parts = _all_gather([_pack([local[n] for n, _ in REPL], REPL_ROWS, F32)], "gather_small_grads")[0]
    packed = [_pack([d[n] for n, _ in REPL], REPL_ROWS, F32) for d in (weights, mom1, mom2)]
    results = _adamw(repl_parts, *packed, REPL_ROWS, "adamw_replicated")
    repl_shapes = [s for _, s in REPL]
    for kind, buf in zip(kinds, results):
        out[kind].update(zip([n for n, _ in REPL], _unpack(buf, repl_shapes)))

    return (loss, grad_x, *[out["grad"][n] for n in WEIGHT_ORDER], *[out["delta"][n] for n in WEIGHT_ORDER],
            *[out["new_m"][n] for n in WEIGHT_ORDER], *[out["new_v"][n] for n in WEIGHT_ORDER])


def _unpack_blocks(gathered, shapes):
    flat = gathered.reshape(N_DEV, -1)
    out, off = [], 0
    for shape in shapes:
        out.append(flat[:, off:off + _size(shape)].reshape((N_DEV,) + shape))
        off += _size(shape)
    return out
```

```python
import math

import jax
import jax.numpy as jnp
from jax import lax
from jax.experimental import pallas as pl
from jax.experimental.pallas import tpu as pltpu

F32 = jnp.float32
BF16 = jnp.bfloat16

D = 1024
SEQ = 8192
DEPTH = 2
N_META = 16
CHUNK = 64
SUB = 16
N_SUB = CHUNK // SUB
PAD_ROWS = CHUNK - N_META
LP = SEQ + CHUNK
N_CHUNKS = LP // CHUNK
RET_HEADS = 4
RET_DK = 128
GLA_HEADS = 4
GLA_DK = 64
GLA_DV = 128
GLA_TAU = 16.0
GATE_RANK = 16
IN_W = 3600
IN_WP = 3840
D_FF = 2816
D_UP = 2 * D_FF
CONV_BLOCK = 256
N_CONV_BLOCKS = D_FF // CONV_BLOCK
ROPE_BASE = 10000.0
EPS = 1e-6
N_DEV = 8
LANES = 1024

O_RQ, O_RK, O_RV, O_RG = 0, 512, 1024, 1536
O_GQ, O_GK, O_GV, O_GR, O_GA = 2048, 2304, 2560, 3072, 3584

ADAM_LR = 0.001
ADAM_B1 = 0.9
ADAM_B2 = 0.999
ADAM_EPS = 1e-08
ADAM_WD = 0.01
ADAM_STEP = 10

VMEM_LIMIT = 56 * 1024 * 1024
MESH_IDS = pl.DeviceIdType.MESH


def _row_tile(rows, limit):
    best = 16
    for t in range(16, min(rows, limit) + 1, 16):
        if rows % t == 0:
            best = t
    return best


TM = _row_tile(LP, 688)
TM_BIG = _row_tile(LP, 1376)
TK_LONG = _row_tile(LP, 2752)


def _cparams(*sem):
    return pltpu.CompilerParams(dimension_semantics=sem, vmem_limit_bytes=VMEM_LIMIT)


def _dot(a, b):
    return jnp.dot(a.astype(BF16), b.astype(BF16), preferred_element_type=F32)


def _dot_nt(a, b):
    return lax.dot_general(a.astype(BF16), b.astype(BF16), (((1,), (1,)), ((), ())), preferred_element_type=F32)


def _dot_tn(a, b):
    return lax.dot_general(a.astype(BF16), b.astype(BF16), (((0,), (0,)), ((), ())), preferred_element_type=F32)


def _split3(x):
    hi = x.astype(BF16)
    r1 = x - hi.astype(F32)
    mid = r1.astype(BF16)
    lo = (r1 - mid.astype(F32)).astype(BF16)
    return hi, mid, lo


def _dot_exact_rhs(t, x):
    n = x.shape[1]
    parts = jnp.dot(t.astype(BF16), jnp.concatenate(_split3(x), axis=1), preferred_element_type=F32)
    return parts[:, :n] + parts[:, n:2 * n] + parts[:, 2 * n:]


def _dot_tn_exact_lhs(x, ones):
    n = x.shape[1]
    parts = lax.dot_general(jnp.concatenate(_split3(x), axis=1), ones.astype(BF16), (((0,), (0,)), ((), ())),
                            preferred_element_type=F32)
    return parts[:n] + parts[n:2 * n] + parts[2 * n:]


def _sigmoid(x):
    return 1.0 / (1.0 + jnp.exp(-x))


def _matmul(a, b, *, ta=False, tb=False, out_dtype, tm, tn, tk, name):
    m = a.shape[1] if ta else a.shape[0]
    k = a.shape[0] if ta else a.shape[1]
    n = b.shape[0] if tb else b.shape[1]
    assert (b.shape[1] if tb else b.shape[0]) == k
    assert m % tm == 0 and n % tn == 0 and k % tk == 0, (name, m, n, k, tm, tn, tk)
    nk = k // tk
    a_spec = pl.BlockSpec((tk, tm), lambda i, j, kk: (kk, i)) if ta else pl.BlockSpec((tm, tk), lambda i, j, kk: (i, kk))
    b_spec = pl.BlockSpec((tn, tk), lambda i, j, kk: (j, kk)) if tb else pl.BlockSpec((tk, tn), lambda i, j, kk: (kk, j))
    dims = (((0 if ta else 1,), (1 if tb else 0,)), ((), ()))

    def body(a_ref, b_ref, o_ref, *acc):
        prod = lax.dot_general(a_ref[...].astype(BF16), b_ref[...].astype(BF16), dims, preferred_element_type=F32)
        if nk == 1:
            o_ref[...] = prod.astype(out_dtype)
            return
        acc_ref, = acc
        kk = pl.program_id(2)

        @pl.when(kk == 0)
        def _():
            acc_ref[...] = prod

        @pl.when(kk > 0)
        def _():
            acc_ref[...] += prod

        @pl.when(kk == nk - 1)
        def _():
            o_ref[...] = acc_ref[...].astype(out_dtype)

    return pl.pallas_call(
        body, name=name, grid=(m // tm, n // tn, nk),
        in_specs=[a_spec, b_spec],
        out_specs=pl.BlockSpec((tm, tn), lambda i, j, kk: (i, j)),
        out_shape=jax.ShapeDtypeStruct((m, n), out_dtype),
        scratch_shapes=[pltpu.VMEM((tm, tn), F32)] if nk > 1 else [],
        compiler_params=_cparams("parallel", "parallel", "arbitrary"),
    )(a, b)


def _norm_matmul(x, w, b, *, out_dtype, tm, tn, name):
    n = b.shape[1]
    assert LP % tm == 0 and n % tn == 0

    def body(x_ref, w_ref, b_ref, a_ref, o_ref, a_scr):
        @pl.when(pl.program_id(1) == 0)
        def _():
            xv = x_ref[...]
            r = lax.rsqrt(jnp.mean(xv * xv, axis=-1, keepdims=True) + EPS)
            a = (xv * r * w_ref[...]).astype(BF16)
            a_scr[...] = a
            a_ref[...] = a

        o_ref[...] = jnp.dot(a_scr[...], b_ref[...], preferred_element_type=F32).astype(out_dtype)

    return pl.pallas_call(
        body, name=name, grid=(LP // tm, n // tn),
        in_specs=[pl.BlockSpec((tm, D), lambda i, j: (i, 0)), pl.BlockSpec((1, D), lambda i, j: (0, 0)),
                  pl.BlockSpec((D, tn), lambda i, j: (0, j))],
        out_specs=[pl.BlockSpec((tm, D), lambda i, j: (i, 0)), pl.BlockSpec((tm, tn), lambda i, j: (i, j))],
        out_shape=[jax.ShapeDtypeStruct((LP, D), BF16), jax.ShapeDtypeStruct((LP, n), out_dtype)],
        scratch_shapes=[pltpu.VMEM((tm, D), BF16)],
        compiler_params=_cparams("arbitrary", "arbitrary"),
    )(x, w, b)


def _matmul_resid_norm(a, b, h, w, name, target=None):
    k = a.shape[1]
    has_loss = target is not None

    def body(a_ref, b_ref, h_ref, w_ref, *refs):
        m = jnp.dot(a_ref[...].astype(BF16), b_ref[...].astype(BF16), preferred_element_type=F32)
        r = lax.rsqrt(jnp.mean(m * m, axis=-1, keepdims=True) + EPS)
        i = pl.program_id(0)
        row = i * TM + lax.broadcasted_iota(jnp.int32, (TM, 1), 0)
        y = h_ref[...] + jnp.where(row >= PAD_ROWS, m * r * w_ref[...], 0.0)
        if not has_loss:
            m_ref, y_ref = refs
            m_ref[...] = m
            y_ref[...] = y
            return
        t_ref, m_ref, dy_ref, loss_ref = refs
        m_ref[...] = m

        @pl.when(i == 0)
        def _():
            loss_ref[...] = jnp.zeros_like(loss_ref)

        diff = jnp.where(row >= CHUNK, y - t_ref[...], 0.0)
        dy_ref[...] = diff * (1.0 / D)
        loss_ref[...] += (0.5 / D) * jnp.sum(diff * diff)

    tile = pl.BlockSpec((TM, D), lambda i: (i, 0))
    shape = jax.ShapeDtypeStruct((LP, D), F32)
    in_specs = [pl.BlockSpec((TM, k), lambda i: (i, 0)), pl.BlockSpec((k, D), lambda i: (0, 0)), tile,
                pl.BlockSpec((1, D), lambda i: (0, 0))]
    if has_loss:
        return pl.pallas_call(
            body, name=name, grid=(LP // TM,),
            in_specs=in_specs + [tile],
            out_specs=[tile, tile, pl.BlockSpec((8, 128), lambda i: (0, 0))],
            out_shape=[shape, shape, jax.ShapeDtypeStruct((8, 128), F32)],
            compiler_params=_cparams("arbitrary"),
        )(a, b, h, w, target)
    return pl.pallas_call(
        body, name=name, grid=(LP // TM,),
        in_specs=in_specs, out_specs=[tile, tile], out_shape=[shape, shape],
        compiler_params=_cparams("parallel"),
    )(a, b, h, w)


def _rmsnorm_bwd_rows(dy, x, w):
    r = lax.rsqrt(jnp.mean(x * x, axis=-1, keepdims=True) + EPS)
    g = dy * w
    dx = r * g - x * (r * r * r * jnp.mean(g * x, axis=-1, keepdims=True))
    return dx, jnp.sum(dy * x * r, axis=0, keepdims=True)


def _matmul_norm_bwd(dz, b, x, w, resid, tk, name, carried=()):
    k = dz.shape[1]
    assert k % tk == 0
    nk = k // tk
    n_rows = LP // TM
    n_carried = len(carried)

    def body(*refs):
        a_ref, b_ref, x_ref, w_ref, r_ref = refs[:5]
        g_refs, refs = refs[5:5 + n_carried], refs[5 + n_carried:]
        dx_ref, dw_ref = refs[:2]
        got_refs, refs = refs[2:2 + n_carried], refs[2 + n_carried:]
        acc, sems = (refs[:1], refs[1:]) if nk > 1 else ((), refs)
        i, kk = pl.program_id(0), pl.program_id(1)
        if n_carried:
            exchange_start, exchange_finish = _exchange_phases(g_refs, got_refs, *sems)
            pl.when((i == 0) & (kk == 0))(exchange_start)

        @pl.when((i == 0) & (kk == 0))
        def _():
            dw_ref[...] = jnp.zeros_like(dw_ref)

        prod = lax.dot_general(a_ref[...].astype(BF16), b_ref[...].astype(BF16), (((1,), (1,)), ((), ())),
                               preferred_element_type=F32)

        def finish(dy):
            dx, dw = _rmsnorm_bwd_rows(dy, x_ref[...], w_ref[...])
            dx_ref[...] = dx + r_ref[...]
            dw_ref[0:1, :] += dw

        if nk == 1:
            finish(prod)
        else:
            acc_ref, = acc

            @pl.when(kk == 0)
            def _():
                acc_ref[...] = prod

            @pl.when((kk > 0) & (kk < nk - 1))
            def _():
                acc_ref[...] += prod

            @pl.when(kk == nk - 1)
            def _():
                finish(acc_ref[...] + prod)

        if n_carried:
            pl.when((i == n_rows - 1) & (kk == nk - 1))(exchange_finish)

    tile = pl.BlockSpec((TM, D), lambda i, kk: (i, 0))
    anywhere = [pl.BlockSpec(memory_space=pl.ANY)] * n_carried
    return pl.pallas_call(
        body, name=name, grid=(n_rows, nk),
        in_specs=[pl.BlockSpec((TM, tk), lambda i, kk: (i, kk)), pl.BlockSpec((D, tk), lambda i, kk: (0, kk)), tile,
                  pl.BlockSpec((1, D), lambda i, kk: (0, 0)), tile] + anywhere,
        out_specs=[tile, pl.BlockSpec((8, D), lambda i, kk: (0, 0))] + anywhere,
        out_shape=[jax.ShapeDtypeStruct((LP, D), F32), jax.ShapeDtypeStruct((8, D), F32)]
        + [jax.ShapeDtypeStruct(g.shape, g.dtype) for g in carried],
        scratch_shapes=([pltpu.VMEM((TM, D), F32)] if nk > 1 else []) + _exchange_sems(n_carried),
        compiler_params=_cparams("arbitrary", "arbitrary"),
    )(dz, b, x, w, resid, *carried)


def _norm_bwd_matmul(dh, x, w, b, out_dtype, name):
    n = b.shape[0]

    def body(dh_ref, x_ref, w_ref, b_ref, o_ref, dx_ref, dw_ref):
        i = pl.program_id(0)

        @pl.when(i == 0)
        def _():
            dw_ref[...] = jnp.zeros_like(dw_ref)

        row = i * TM + lax.broadcasted_iota(jnp.int32, (TM, 1), 0)
        dy = jnp.where(row >= PAD_ROWS, dh_ref[...], 0.0)
        dx, dw = _rmsnorm_bwd_rows(dy, x_ref[...], w_ref[...])
        dxb = dx.astype(BF16)
        dx_ref[...] = dxb
        dw_ref[0:1, :] += dw
        o_ref[...] = lax.dot_general(dxb, b_ref[...].astype(BF16), (((1,), (1,)), ((), ())),
                                     preferred_element_type=F32).astype(out_dtype)

    tile = pl.BlockSpec((TM, D), lambda i: (i, 0))
    return pl.pallas_call(
        body, name=name, grid=(LP // TM,),
        in_specs=[tile, tile, pl.BlockSpec((1, D), lambda i: (0, 0)), pl.BlockSpec((n, D), lambda i: (0, 0))],
        out_specs=[pl.BlockSpec((TM, n), lambda i: (i, 0)), tile, pl.BlockSpec((8, D), lambda i: (0, 0))],
        out_shape=[jax.ShapeDtypeStruct((LP, n), out_dtype), jax.ShapeDtypeStruct((LP, D), BF16),
                   jax.ShapeDtypeStruct((8, D), F32)],
        compiler_params=_cparams("arbitrary"),
    )(dh, x, w, b)


GELU_C = math.sqrt(2.0 / math.pi)
GELU_K = 0.044715
STRIP = 16
HALF = 8


def _gelu_half(a):
    return 0.5 * jnp.tanh(a * (a * a * (GELU_C * GELU_K) + GELU_C)) + 0.5


def _gelu_slope(a, h):
    return h * (1.0 + (a - a * h) * (a * a * (6.0 * GELU_C * GELU_K) + 2.0 * GELU_C))


def _shift_down(x, prev8):
    row = lax.broadcasted_iota(jnp.int32, (8, 1), 0)
    r1, r2 = pltpu.roll(x, 1, 0), pltpu.roll(x, 2, 0)
    top1 = jnp.where(row < 1, pltpu.roll(prev8, 1, 0), r1[0:8, :])
    top2 = jnp.where(row < 2, pltpu.roll(prev8, 2, 0), r2[0:8, :])
    return jnp.concatenate([top1, r1[8:, :]], axis=0), jnp.concatenate([top2, r2[8:, :]], axis=0)


def _conv_act_fwd(u, cw8, name, carried=()):
    n_rows = LP // TM
    cb2 = 2 * CONV_BLOCK
    n_carried = len(carried)

    def body(*refs):
        u_ref, cw_ref = refs[:2]
        x_refs, refs = refs[2:2 + n_carried], refs[2 + n_carried:]
        conv_ref, act_ref = refs[:2]
        gathered_refs, refs = refs[2:2 + n_carried], refs[2 + n_carried:]
        carry_ref = refs[0]
        j, i = pl.program_id(0), pl.program_id(1)
        if n_carried:
            start, forward, finish = _gather_phases(x_refs, gathered_refs, *refs[1:])
            pl.when((j == 0) & (i == 0))(start)
            pl.when((j == (3 * N_CONV_BLOCKS) // 4) & (i == 0))(forward)

        @pl.when(i == 0)
        def _():
            carry_ref[...] = jnp.zeros_like(carry_ref)

        x = u_ref[...].astype(F32)
        x1, x2 = _shift_down(x, carry_ref[...])
        conv = cw_ref[3:4, :] + x2 * cw_ref[0:1, :] + x1 * cw_ref[1:2, :] + x * cw_ref[2:3, :]
        conv_ref[...] = conv.astype(BF16)
        a = conv[:, :CONV_BLOCK]
        g = conv[:, CONV_BLOCK:]
        act_ref[...] = (a * _gelu_half(a) * g).astype(BF16)
        carry_ref[...] = x[TM - 8:TM, :]
        if n_carried:
            pl.when((j == N_CONV_BLOCKS - 1) & (i == n_rows - 1))(finish)

    anywhere = [pl.BlockSpec(memory_space=pl.ANY)] * n_carried
    return pl.pallas_call(
        body, name=name, grid=(N_CONV_BLOCKS, n_rows),
        in_specs=[pl.BlockSpec((TM, cb2), lambda j, i: (i, j)), pl.BlockSpec((8, cb2), lambda j, i: (0, j))] + anywhere,
        out_specs=[pl.BlockSpec((TM, cb2), lambda j, i: (i, j)),
                   pl.BlockSpec((TM, CONV_BLOCK), lambda j, i: (i, j))] + anywhere,
        out_shape=[jax.ShapeDtypeStruct((LP, D_UP), BF16), jax.ShapeDtypeStruct((LP, D_FF), BF16)]
        + _gathered_shapes(carried),
        scratch_shapes=[pltpu.VMEM((8, cb2), F32)] + _exchange_sems(n_carried),
        compiler_params=_cparams("arbitrary", "arbitrary"),
    )(u, cw8, *carried)


def _conv_act_bwd(dact, conv, u, cw8, name):
    n_rows = LP // TM
    cb2 = 2 * CONV_BLOCK
    n_strips = TM // STRIP

    def body(dact_ref, conv_ref, u_ref, cw_ref, du_ref, dcw_ref, carry_ref):
        i = pl.program_id(1)

        @pl.when(i == 0)
        def _():
            dcw_ref[...] = jnp.zeros_like(dcw_ref)
            carry_ref[...] = jnp.zeros_like(carry_ref)

        w0, w1, w2 = cw_ref[0:1, :], cw_ref[1:2, :], cw_ref[2:3, :]
        row = lax.broadcasted_iota(jnp.int32, (HALF, 1), 0)

        def strip(k, carry):
            n1, n2, s0, s1, s2, s3 = carry
            r0 = pl.multiple_of((n_strips - 1 - k) * STRIP, STRIP)
            cv = conv_ref[pl.ds(r0, STRIP), :].astype(F32)
            dav = dact_ref[pl.ds(r0, STRIP), :].astype(F32)
            x = u_ref[pl.ds(r0, STRIP), :].astype(F32)
            du = [None, None]
            for half in (1, 0):
                rows = slice(HALF * half, HALF * (half + 1))
                a, g, dah = cv[rows, :CONV_BLOCK], cv[rows, CONV_BLOCK:], dav[rows]
                h = _gelu_half(a)
                dconv = jnp.concatenate([dah * g * _gelu_slope(a, h), dah * (a * h)], axis=1)
                u1, u2 = pltpu.roll(dconv, HALF - 1, 0), pltpu.roll(dconv, HALF - 2, 0)
                d1 = jnp.where(row >= HALF - 1, n1, u1)
                d2 = jnp.where(row >= HALF - 2, n2, u2)
                du[half] = dconv * w2 + d1 * w1 + d2 * w0
                s0, s1, s2, s3 = s0 + d2 * x[rows], s1 + d1 * x[rows], s2 + dconv * x[rows], s3 + dconv
                n1, n2 = u1, u2
            du_ref[pl.ds(r0, STRIP), :] = jnp.concatenate(du, axis=0).astype(BF16)
            return n1, n2, s0, s1, s2, s3

        below = carry_ref[...]
        zero = jnp.zeros((HALF, cb2), F32)
        init = (pltpu.roll(below, HALF - 1, 0), pltpu.roll(below, HALF - 2, 0), zero, zero, zero, zero)
        u1, _, s0, s1, s2, s3 = lax.fori_loop(0, n_strips, strip, init, unroll=2)
        carry_ref[...] = pltpu.roll(u1, 1, 0)
        dcw_ref[0:1, :] += jnp.sum(s0, axis=0, keepdims=True)
        dcw_ref[1:2, :] += jnp.sum(s1, axis=0, keepdims=True)
        dcw_ref[2:3, :] += jnp.sum(s2, axis=0, keepdims=True)
        dcw_ref[3:4, :] += jnp.sum(s3, axis=0, keepdims=True)

    rev = lambda j, i: (n_rows - 1 - i, j)
    return pl.pallas_call(
        body, name=name, grid=(N_CONV_BLOCKS, n_rows),
        in_specs=[pl.BlockSpec((TM, CONV_BLOCK), rev), pl.BlockSpec((TM, cb2), rev), pl.BlockSpec((TM, cb2), rev),
                  pl.BlockSpec((8, cb2), lambda j, i: (0, j))],
        out_specs=[pl.BlockSpec((TM, cb2), rev), pl.BlockSpec((8, cb2), lambda j, i: (0, j))],
        out_shape=[jax.ShapeDtypeStruct((LP, D_UP), BF16), jax.ShapeDtypeStruct((8, D_UP), F32)],
        scratch_shapes=[pltpu.VMEM((HALF, cb2), F32)],
        compiler_params=_cparams("arbitrary", "arbitrary"),
    )(dact, conv, u, cw8)


CHUNKS_PER_STEP = 3 if N_CHUNKS % 3 == 0 else 1
STEP_ROWS = CHUNKS_PER_STEP * CHUNK
N_STEPS = N_CHUNKS // CHUNKS_PER_STEP


class _ReadAsF32:
    def __init__(self, ref):
        self.ref = ref

    def __getitem__(self, idx):
        return self.ref[idx].astype(F32)


def _ret_consts(h):
    rows = STEP_ROWS
    lg = math.log(1.0 - 2.0 ** (-5.0 - h))
    ri = lax.broadcasted_iota(jnp.int32, (rows, rows), 0)
    ci = lax.broadcasted_iota(jnp.int32, (rows, rows), 1)
    diff = (ri - ci).astype(F32)
    dmat = jnp.where(diff >= 0, jnp.exp(lg * jnp.maximum(diff, 0.0)), 0.0)
    rowf = lax.broadcasted_iota(jnp.int32, (rows, 1), 0).astype(F32)
    zeta = jnp.exp(lg * (rows - 1.0 - rowf))
    xi = jnp.exp(lg * (rowf + 1.0))
    return dmat, zeta, xi, math.exp(lg * rows)


def _rope(t, cosv, sinv):
    return t * cosv + pltpu.roll(t, RET_DK // 2, 1) * sinv


def _unrope(d, cosv, sinv):
    return d * cosv + pltpu.roll(d * sinv, RET_DK // 2, 1)


def _gla_masks():
    ri = lax.broadcasted_iota(jnp.int32, (CHUNK, CHUNK), 0)
    ci = lax.broadcasted_iota(jnp.int32, (CHUNK, CHUNK), 1)
    return dict(ri=ri, ci=ci, tril=(ri >= ci).astype(F32), heads=_head_block_mask(), own=_state_block_mask())


def _gla_common(p_ref, w2_ref, gb_ref, chunk, rows, masks):
    row = lax.broadcasted_iota(jnp.int32, (CHUNK, 1), 0)
    real = (chunk * CHUNK + row) >= PAD_ROWS
    ga = p_ref[rows, O_GA:O_GA + 128]
    z = _dot(ga, w2_ref[...]) + gb_ref[...]
    la = (jnp.minimum(z, 0.0) - jnp.log(1.0 + jnp.exp(-jnp.abs(z)))) * (1.0 / GLA_TAU)
    la = jnp.where(real, la, 0.0)
    ri, ci = masks["ri"], masks["ci"]
    cum = _dot_exact_rhs(masks["tril"], la)
    last = cum[CHUNK - 1:CHUNK, :]
    qs = p_ref[rows, O_GQ:O_GQ + 256] * (GLA_DK ** -0.5)
    k = p_ref[rows, O_GK:O_GK + 256]
    ecum = jnp.exp(cum)
    ekl = jnp.exp(last - cum)
    el = jnp.exp(last)
    refs = [jnp.zeros((1, 256), F32)] + [cum[a * SUB - 1:a * SUB, :] for a in range(1, N_SUB)]
    eq = [jnp.exp(cum[a * SUB:(a + 1) * SUB, :] - refs[a]) for a in range(N_SUB)]
    spread = refs[0] - cum[SUB - 1:SUB, :]
    for a in range(1, N_SUB):
        spread = jnp.maximum(spread, refs[a] - cum[(a + 1) * SUB - 1:(a + 1) * SUB, :])
    small = jnp.max(spread) <= GLA_FACTORED_MAX
    return dict(real=real, row=row, z=z, la=la, cum=cum, last=last, qs=qs, k=k, ecum=ecum, ekl=ekl, el=el,
                refs=refs, eq=eq, small=small, ri=ri, ci=ci, masks=masks)


GLA_FACTORED_MAX = 40.0


def _head_block_mask():
    r = lax.broadcasted_iota(jnp.int32, (CHUNK, 256), 0)
    col = lax.broadcasted_iota(jnp.int32, (CHUNK, 256), 1)
    return (r // SUB) == (col // GLA_DK)


def _state_block_mask():
    r = lax.broadcasted_iota(jnp.int32, (GLA_HEADS * GLA_DK, GLA_HEADS * GLA_DV), 0)
    col = lax.broadcasted_iota(jnp.int32, (GLA_HEADS * GLA_DK, GLA_HEADS * GLA_DV), 1)
    return (r // GLA_DK) == (col // GLA_DV)


def _block_diagonal(blocks):
    zero = jnp.zeros((GLA_DK, GLA_DV), F32)
    return jnp.concatenate([jnp.concatenate([blocks[h] if g == h else zero for g in range(GLA_HEADS)], axis=1)
                            for h in range(GLA_HEADS)], axis=0)


def _gla_factored(c):
    mask = c["masks"]["heads"]
    eks, keys, queries = [], [], []
    for a in range(N_SUB):
        ek = jnp.exp(jnp.minimum(c["refs"][a] - c["cum"], GLA_FACTORED_MAX))
        qh = c["qs"][a * SUB:(a + 1) * SUB, :] * c["eq"][a]
        eks.append(ek)
        keys.append(c["k"] * ek)
        queries.append(jnp.where(mask, jnp.concatenate([qh] * GLA_HEADS, axis=0), 0.0))
    return eks, keys, queries


def _gla_scores_factored(c, factored, p_scr):
    _, keys, queries = factored
    for a in range(N_SUB):
        out = _dot_nt(queries[a], keys[a])
        out = jnp.where(c["ci"] <= a * SUB + (c["ri"] & (SUB - 1)), out, 0.0)
        for h in range(GLA_HEADS):
            p_scr[h, a * SUB:(a + 1) * SUB, :] = out[h * SUB:(h + 1) * SUB, :]


def _gla_intra_bwd_factored(c, factored, dps, dq_scr, dk_scr):
    eks, keys, queries = factored
    mask = c["masks"]["heads"]
    dk = jnp.zeros((CHUNK, 256), F32)
    for a in range(N_SUB):
        dpa = jnp.concatenate([dps[h][a * SUB:(a + 1) * SUB, :] for h in range(GLA_HEADS)], axis=0)
        dq = jnp.where(mask, _dot(dpa, keys[a]), 0.0)
        dq = dq[0:SUB] + dq[SUB:2 * SUB] + dq[2 * SUB:3 * SUB] + dq[3 * SUB:4 * SUB]
        dq_scr[a * SUB:(a + 1) * SUB, :] = dq * c["eq"][a]
        dk = dk + _dot_tn(dpa, queries[a]) * eks[a]
    dk_scr[...] = dk


def _gla_lag_weights(c):
    cum, row = c["cum"], c["row"]
    out = [jnp.ones((CHUNK, 256), F32)]
    for r in range(1, SUB):
        out.append(jnp.where((row % SUB) >= r, jnp.exp(jnp.minimum(cum - pltpu.roll(cum, r, 0), 0.0)), 0.0))
    return out


def _gla_pairwise_keys(c):
    return [None] + [c["k"] * jnp.exp(jnp.minimum(c["refs"][a] - c["cum"], 0.0)) for a in range(1, N_SUB)]


def _gla_scores_pairwise(c, lag_w, keys, h):
    sl = slice(GLA_DK * h, GLA_DK * (h + 1))
    qs, k = c["qs"][:, sl], c["k"][:, sl]
    ri, ci = c["ri"], c["ci"]
    p = jnp.zeros((CHUNK, CHUNK), F32)
    for r in range(SUB):
        kr = k if r == 0 else pltpu.roll(k, r, 0)
        pr = jnp.sum(qs * kr * lag_w[r][:, sl], axis=1, keepdims=True)
        p = p + jnp.where(ci == ri - r, pr, 0.0)
    blocks = [jnp.zeros((SUB, CHUNK), F32)]
    for a in range(1, N_SUB):
        qh = qs[a * SUB:(a + 1) * SUB, :] * c["eq"][a][:, sl]
        blocks.append(jnp.where(ci[:SUB, :] < a * SUB, _dot_nt(qh, keys[a][:, sl]), 0.0))
    return p + jnp.concatenate(blocks, axis=0)


def _gla_all_scores(c, p_scr, factored):
    if factored:
        _gla_scores_factored(c, _gla_factored(c), p_scr)
    else:
        lag_w, keys = _gla_lag_weights(c), _gla_pairwise_keys(c)
        for h in range(GLA_HEADS):
            p_scr[h] = _gla_scores_pairwise(c, lag_w, keys, h)


def _either_form(chunks, run):
    small = chunks[0]["small"]
    for c in chunks[1:]:
        small = jnp.logical_and(small, c["small"])
    pl.when(small)(lambda: run(True))
    pl.when(jnp.logical_not(small))(lambda: run(False))


def _gla_intra_bwd_pairwise(c, lag_w, keys, dp, h):
    sl = slice(GLA_DK * h, GLA_DK * (h + 1))
    qs_h, k_h = c["qs"][:, sl], c["k"][:, sl]
    ri, ci = c["ri"], c["ci"]
    dq_rows = [jnp.zeros((SUB, GLA_DK), F32)]
    dk = jnp.zeros((CHUNK, GLA_DK), F32)
    for a in range(1, N_SUB):
        eq = c["eq"][a][:, sl]
        qh = qs_h[a * SUB:(a + 1) * SUB, :] * eq
        dpa = jnp.where(ci[:SUB, :] < a * SUB, dp[a * SUB:(a + 1) * SUB, :], 0.0)
        dq_rows.append(_dot(dpa, keys[a][:, sl]) * eq)
        ek = jnp.exp(jnp.minimum(c["refs"][a][:, sl] - c["cum"][:, sl], 0.0))
        dk = dk + _dot_tn(dpa, qh) * ek
    dq = jnp.concatenate(dq_rows, axis=0)
    for r in range(SUB):
        w = lag_w[r][:, sl]
        dpr = jnp.sum(jnp.where(ci == ri - r, dp, 0.0), axis=1, keepdims=True)
        kr = k_h if r == 0 else pltpu.roll(k_h, r, 0)
        dq = dq + dpr * kr * w
        back = dpr * qs_h * w
        dk = dk + (back if r == 0 else pltpu.roll(back, CHUNK - r, 0))
    return dq, dk


def _gla_all_intra_bwd(c, dps, p_scr, dq_scr, dk_scr, factored):
    if factored:
        terms = _gla_factored(c)
        _gla_scores_factored(c, terms, p_scr)
        _gla_intra_bwd_factored(c, terms, dps, dq_scr, dk_scr)
    else:
        lag_w, keys = _gla_lag_weights(c), _gla_pairwise_keys(c)
        outs = [_gla_intra_bwd_pairwise(c, lag_w, keys, dps[h], h) for h in range(GLA_HEADS)]
        for h in range(GLA_HEADS):
            p_scr[h] = _gla_scores_pairwise(c, lag_w, keys, h)
        dq_scr[...] = jnp.concatenate([o[0] for o in outs], axis=1)
        dk_scr[...] = jnp.concatenate([o[1] for o in outs], axis=1)


def _mixer_fwd(proj, cos2, sin2, w2p, gb, rnw, gnw, name, carried=()):
    n_carried = len(carried)

    def body(*refs):
        p_ref, c_ref, s_ref, w2_ref, gb_ref, rnw_ref, gnw_ref = refs[:7]
        p_ref = _ReadAsF32(p_ref)
        x_refs, refs = refs[7:7 + n_carried], refs[7 + n_carried:]
        ocat_ref, mrg_ref, sr_out, sg_out = refs[:4]
        gathered_refs, refs = refs[4:4 + n_carried], refs[4 + n_carried:]
        sr, sg, p_scr = refs[:3]
        n = pl.program_id(0)
        if n_carried:
            start, forward, finish = _gather_phases(x_refs, gathered_refs, *refs[3:])
            pl.when(n == 0)(start)
            pl.when(n == (3 * N_STEPS) // 4)(forward)

        @pl.when(n == 0)
        def _():
            sr[...] = jnp.zeros_like(sr)
            sg[...] = jnp.zeros_like(sg)

        sr_out[0] = sr[...]
        cosv, sinv = c_ref[...], s_ref[...]

        for h in range(RET_HEADS):
            dmat, zeta, xi, gc = _ret_consts(h)
            hs = slice(128 * h, 128 * (h + 1))
            q = _rope(p_ref[:, O_RQ + 128 * h:O_RQ + 128 * (h + 1)], cosv, sinv)
            k = _rope(p_ref[:, O_RK + 128 * h:O_RK + 128 * (h + 1)], cosv, sinv) * (RET_DK ** -0.5)
            v = p_ref[:, O_RV + 128 * h:O_RV + 128 * (h + 1)]
            g = p_ref[:, O_RG + 128 * h:O_RG + 128 * (h + 1)]
            s_in = sr[h]
            a = _dot_nt(q, k) * dmat
            o = _dot(a, v) + _dot(q, s_in) * xi
            sr[h] = gc * s_in + _dot_tn(k * zeta, v)
            mu = jnp.mean(o, axis=-1, keepdims=True)
            xc = o - mu
            nrm = xc * lax.rsqrt(jnp.mean(xc * xc, axis=-1, keepdims=True) + EPS)
            ocat_ref[:, hs] = o
            mrg_ref[:, hs] = (nrm * rnw_ref[:, hs] * (g * _sigmoid(g))).astype(BF16)

        row_slices = [slice(CHUNK * j, CHUNK * (j + 1)) for j in range(CHUNKS_PER_STEP)]
        masks = _gla_masks()
        chunks = [_gla_common(p_ref, w2_ref, gb_ref, n * CHUNKS_PER_STEP + j, rows, masks)
                  for j, rows in enumerate(row_slices)]

        def gla_chunks(factored):
            own = masks["own"]
            for j, (rows, c) in enumerate(zip(row_slices, chunks)):
                s_in = sg[...]
                for h in range(GLA_HEADS):
                    sg_out[j, h] = s_in[GLA_DK * h:GLA_DK * (h + 1), GLA_DV * h:GLA_DV * (h + 1)]
                _gla_all_scores(c, p_scr.at[j], factored)
                v_all = p_ref[rows, O_GV:O_GV + GLA_HEADS * GLA_DV]
                o_inter = _dot(c["qs"] * c["ecum"], s_in)
                decay = jnp.exp(_dot_tn_exact_lhs(c["la"], jnp.ones((CHUNK, GLA_HEADS * GLA_DV), F32)))
                sg[...] = decay * s_in + jnp.where(own, _dot_tn(c["k"] * c["ekl"], v_all), 0.0)
                o_intra = _dot(p_scr[j].reshape(GLA_HEADS * CHUNK, CHUNK), v_all)
                for h in range(GLA_HEADS):
                    hs = slice(512 + 128 * h, 512 + 128 * (h + 1))
                    g = p_ref[rows, O_GR + 128 * h:O_GR + 128 * (h + 1)]
                    o = (o_intra[CHUNK * h:CHUNK * (h + 1), GLA_DV * h:GLA_DV * (h + 1)]
                         + o_inter[:, GLA_DV * h:GLA_DV * (h + 1)])
                    nrm = o * lax.rsqrt(jnp.mean(o * o, axis=-1, keepdims=True) + EPS)
                    ocat_ref[rows, hs] = o
                    mrg_ref[rows, hs] = (nrm * gnw_ref[:, 128 * h:128 * (h + 1)] * (g * _sigmoid(g))).astype(BF16)

        _either_form(chunks, gla_chunks)

        if n_carried:
            pl.when(n == N_STEPS - 1)(finish)

    const = lambda shape: pl.BlockSpec(shape, lambda n: (0,) * len(shape))
    anywhere = [pl.BlockSpec(memory_space=pl.ANY)] * n_carried
    return pl.pallas_call(
        body, name=name, grid=(N_STEPS,),
        in_specs=[pl.BlockSpec((STEP_ROWS, IN_WP), lambda n: (n, 0)),
                  pl.BlockSpec((STEP_ROWS, 128), lambda n: (n, 0)), pl.BlockSpec((STEP_ROWS, 128), lambda n: (n, 0)),
                  const((128, 256)), const((1, 256)), const((1, 512)), const((1, 512))] + anywhere,
        out_specs=[pl.BlockSpec((STEP_ROWS, D), lambda n: (n, 0)), pl.BlockSpec((STEP_ROWS, D), lambda n: (n, 0)),
                   pl.BlockSpec((1, RET_HEADS, RET_DK, 128), lambda n: (n, 0, 0, 0)),
                   pl.BlockSpec((CHUNKS_PER_STEP, GLA_HEADS, GLA_DK, GLA_DV), lambda n: (n, 0, 0, 0))] + anywhere,
        out_shape=[jax.ShapeDtypeStruct((LP, D), F32), jax.ShapeDtypeStruct((LP, D), BF16),
                   jax.ShapeDtypeStruct((N_STEPS, RET_HEADS, RET_DK, 128), F32),
                   jax.ShapeDtypeStruct((N_CHUNKS, GLA_HEADS, GLA_DK, GLA_DV), F32)] + _gathered_shapes(carried),
        scratch_shapes=[pltpu.VMEM((RET_HEADS, RET_DK, 128), F32),
                        pltpu.VMEM((GLA_HEADS * GLA_DK, GLA_HEADS * GLA_DV), F32),
                        pltpu.VMEM((CHUNKS_PER_STEP, GLA_HEADS, CHUNK, CHUNK), F32)] + _exchange_sems(n_carried),
        compiler_params=_cparams("arbitrary"),
    )(proj, cos2, sin2, w2p, gb, rnw, gnw, *carried)


def _mixer_bwd(proj, ocat, dmrg, sr_all, sg_all, cos2, sin2, w2p, gb, rnw, gnw, name, carried=()):
    last_step = N_STEPS - 1
    n_carried = len(carried)

    def body(*refs):
        p_ref, ocat_ref, dm_ref, sr_ref, sg_ref, c_ref, s_ref, w2_ref, gb_ref, rnw_ref, gnw_ref = refs[:11]
        p_ref = _ReadAsF32(p_ref)
        g_refs, refs = refs[11:11 + n_carried], refs[11 + n_carried:]
        dp_ref, dw2_ref, dgb_ref, drn_ref, dgn_ref = refs[:5]
        got_refs, refs = refs[5:5 + n_carried], refs[5 + n_carried:]
        dsr, dsg, p_scr, dq_scr, dk_scr = refs[:5]
        step = pl.program_id(0)
        n = last_step - step
        if n_carried:
            start, finish = _exchange_phases(g_refs, got_refs, *refs[5:])
            pl.when(step == 0)(start)

        @pl.when(step == 0)
        def _():
            dsr[...] = jnp.zeros_like(dsr)
            dsg[...] = jnp.zeros_like(dsg)
            dw2_ref[...] = jnp.zeros_like(dw2_ref)
            dgb_ref[...] = jnp.zeros_like(dgb_ref)
            drn_ref[...] = jnp.zeros_like(drn_ref)
            dgn_ref[...] = jnp.zeros_like(dgn_ref)

        cosv, sinv = c_ref[...], s_ref[...]
        step_row = lax.broadcasted_iota(jnp.int32, (STEP_ROWS, 1), 0)
        real = ((n * STEP_ROWS + step_row) >= PAD_ROWS).astype(F32)

        for h in range(RET_HEADS):
            dmat, zeta, xi, gc = _ret_consts(h)
            hs = slice(128 * h, 128 * (h + 1))
            q = _rope(p_ref[:, O_RQ + 128 * h:O_RQ + 128 * (h + 1)], cosv, sinv)
            k = _rope(p_ref[:, O_RK + 128 * h:O_RK + 128 * (h + 1)], cosv, sinv) * (RET_DK ** -0.5)
            v = p_ref[:, O_RV + 128 * h:O_RV + 128 * (h + 1)]
            g = p_ref[:, O_RG + 128 * h:O_RG + 128 * (h + 1)]
            o = ocat_ref[:, hs]
            dy = dm_ref[:, hs]
            wv = rnw_ref[:, hs]
            mu = jnp.mean(o, axis=-1, keepdims=True)
            xc = o - mu
            rs = lax.rsqrt(jnp.mean(xc * xc, axis=-1, keepdims=True) + EPS)
            nrm = xc * rs
            sgm = _sigmoid(g)
            sil = g * sgm
            drn_ref[0:1, hs] += jnp.sum(dy * nrm * sil, axis=0, keepdims=True)
            dgate = dy * nrm * wv * (sgm * (1.0 + g * (1.0 - sgm)))
            dn = dy * wv * sil
            do = rs * (dn - jnp.mean(dn, axis=-1, keepdims=True) - nrm * jnp.mean(dn * nrm, axis=-1, keepdims=True))
            s_in = sr_ref[0, h]
            ds_out = dsr[h]
            a = _dot_nt(q, k) * dmat
            da = _dot_nt(do, v) * dmat
            dox = do * xi
            dq = _dot(da, k) + _dot_nt(dox, s_in)
            dk = _dot_tn(da, q) + _dot_nt(v, ds_out) * zeta
            dv = _dot_tn(a, do) + _dot(k * zeta, ds_out)
            dsr[h] = gc * ds_out + _dot_tn(q, dox)
            dk = dk * (RET_DK ** -0.5)
            dp_ref[:, O_RQ + 128 * h:O_RQ + 128 * (h + 1)] = (_unrope(dq, cosv, sinv) * real).astype(BF16)
            dp_ref[:, O_RK + 128 * h:O_RK + 128 * (h + 1)] = (_unrope(dk, cosv, sinv) * real).astype(BF16)
            dp_ref[:, O_RV + 128 * h:O_RV + 128 * (h + 1)] = (dv * real).astype(BF16)
            dp_ref[:, O_RG + 128 * h:O_RG + 128 * (h + 1)] = (dgate * real).astype(BF16)

        row_slices = [slice(CHUNK * j, CHUNK * (j + 1)) for j in range(CHUNKS_PER_STEP)]
        masks = _gla_masks()
        chunks = [_gla_common(p_ref, w2_ref, gb_ref, n * CHUNKS_PER_STEP + j, rows, masks)
                  for j, rows in enumerate(row_slices)]

        def gla_chunks(factored):
            for j in reversed(range(CHUNKS_PER_STEP)):
                gla_chunk_bwd(chunks[j], n * CHUNKS_PER_STEP + j, row_slices[j], j, factored, p_ref, ocat_ref, dm_ref,
                              sg_ref, w2_ref, gnw_ref, dp_ref, dw2_ref, dgb_ref, dgn_ref, dsg, p_scr, dq_scr, dk_scr)

        _either_form(chunks, gla_chunks)
        if n_carried:
            pl.when(step == last_step)(finish)

    def gla_chunk_bwd(c, chunk, rows, j, factored, p_ref, ocat_ref, dm_ref, sg_ref, w2_ref, gnw_ref,
                      dp_ref, dw2_ref, dgb_ref, dgn_ref, dsg, p_scr, dq_scr, dk_scr):
        row = lax.broadcasted_iota(jnp.int32, (CHUNK, 1), 0)
        real = ((chunk * CHUNK + row) >= PAD_ROWS).astype(F32)
        ri, ci = c["ri"], c["ci"]
        causal = ri >= ci
        triu = (ci >= ri).astype(F32)
        qe = c["qs"] * c["ecum"]
        kl = c["k"] * c["ekl"]
        v_all = p_ref[rows, O_GV:O_GV + GLA_HEADS * GLA_DV]
        dos, dps = [], []
        for h in range(GLA_HEADS):
            hs = slice(512 + 128 * h, 512 + 128 * (h + 1))
            g = p_ref[rows, O_GR + 128 * h:O_GR + 128 * (h + 1)]
            o = ocat_ref[rows, hs]
            dy = dm_ref[rows, hs]
            wv = gnw_ref[:, 128 * h:128 * (h + 1)]
            rs = lax.rsqrt(jnp.mean(o * o, axis=-1, keepdims=True) + EPS)
            nrm = o * rs
            sgm = _sigmoid(g)
            sil = g * sgm
            dgn_ref[0:1, 128 * h:128 * (h + 1)] += jnp.sum(dy * nrm * sil, axis=0, keepdims=True)
            dgate = dy * nrm * wv * (sgm * (1.0 + g * (1.0 - sgm)))
            dn = dy * wv * sil
            do = rs * (dn - nrm * jnp.mean(dn * nrm, axis=-1, keepdims=True))
            dp_ref[rows, O_GR + 128 * h:O_GR + 128 * (h + 1)] = (dgate * real).astype(BF16)
            dos.append(do)
        do_all = jnp.concatenate(dos, axis=1)
        do_blocks = jnp.where(c["masks"]["own"], jnp.concatenate([do_all] * GLA_HEADS, axis=0), 0.0)
        dp_all = _dot_nt(do_blocks, v_all)
        dps = [jnp.where(causal, dp_all[CHUNK * h:CHUNK * (h + 1), :], 0.0) for h in range(GLA_HEADS)]
        _gla_all_intra_bwd(c, dps, p_scr.at[j], dq_scr.at[j], dk_scr.at[j], factored)
        s_in = _block_diagonal([sg_ref[j, h] for h in range(GLA_HEADS)])
        ds_out = dsg[...]
        decay = jnp.exp(_dot_tn_exact_lhs(c["la"], jnp.ones((CHUNK, GLA_HEADS * GLA_DV), F32)))
        dv_state = _dot(kl, ds_out)
        dqe = _dot_nt(do_all, s_in)
        dkl = _dot_nt(v_all, ds_out)
        dsg[...] = jnp.where(c["masks"]["own"], _dot_tn(qe, do_all), 0.0) + decay * ds_out
        sd = s_in * ds_out
        sd_hi = sd.astype(BF16)
        sd_lo = (sd - sd_hi.astype(F32)).astype(BF16)
        ones8 = jnp.ones((8, GLA_HEADS * GLA_DV), BF16)
        nt = (((1,), (1,)), ((), ()))
        d_el = (lax.dot_general(ones8, sd_hi, nt, preferred_element_type=F32)
                + lax.dot_general(ones8, sd_lo, nt, preferred_element_type=F32))[0:1, :]
        dqs = dqe * c["ecum"] + dq_scr[j]
        dkk = dkl * c["ekl"] + dk_scr[j]
        d_last = jnp.sum(dkl * kl, axis=0, keepdims=True) + d_el * c["el"]
        dcum = c["qs"] * dqs - c["k"] * dkk + jnp.where(row == CHUNK - 1, d_last, 0.0)
        dla = _dot_exact_rhs(triu, dcum)
        dv = _dot_tn(p_scr[j].reshape(GLA_HEADS * CHUNK, CHUNK), do_blocks) + dv_state
        dp_ref[rows, O_GV:O_GV + GLA_HEADS * GLA_DV] = (dv * real).astype(BF16)
        dp_ref[rows, O_GQ:O_GQ + 256] = (dqs * (GLA_DK ** -0.5) * real).astype(BF16)
        dp_ref[rows, O_GK:O_GK + 256] = (dkk * real).astype(BF16)
        dz = dla * (1.0 / GLA_TAU) * _sigmoid(-c["z"]) * real
        ga = p_ref[rows, O_GA:O_GA + 128]
        dp_ref[rows, O_GA:O_GA + 128] = _dot_nt(dz, w2_ref[...]).astype(BF16)
        dp_ref[rows, O_GA + 128:IN_WP] = jnp.zeros((CHUNK, IN_WP - O_GA - 128), BF16)
        dw2_ref[...] += _dot_tn(ga, dz)
        dgb_ref[0:1, :] += jnp.sum(dz, axis=0, keepdims=True)

    const = lambda shape: pl.BlockSpec(shape, lambda s: (0,) * len(shape))
    rev = lambda s: (last_step - s, 0)
    anywhere = [pl.BlockSpec(memory_space=pl.ANY)] * n_carried
    return pl.pallas_call(
        body, name=name, grid=(N_STEPS,),
        in_specs=[pl.BlockSpec((STEP_ROWS, IN_WP), rev), pl.BlockSpec((STEP_ROWS, D), rev),
                  pl.BlockSpec((STEP_ROWS, D), rev),
                  pl.BlockSpec((1, RET_HEADS, RET_DK, 128), lambda s: (last_step - s, 0, 0, 0)),
                  pl.BlockSpec((CHUNKS_PER_STEP, GLA_HEADS, GLA_DK, GLA_DV), lambda s: (last_step - s, 0, 0, 0)),
                  pl.BlockSpec((STEP_ROWS, 128), rev), pl.BlockSpec((STEP_ROWS, 128), rev),
                  const((128, 256)), const((1, 256)), const((1, 512)), const((1, 512))] + anywhere,
        out_specs=[pl.BlockSpec((STEP_ROWS, IN_WP), rev), const((128, 256)), const((8, 256)),
                   const((8, 512)), const((8, 512))] + anywhere,
        out_shape=[jax.ShapeDtypeStruct((LP, IN_WP), BF16), jax.ShapeDtypeStruct((128, 256), F32),
                   jax.ShapeDtypeStruct((8, 256), F32), jax.ShapeDtypeStruct((8, 512), F32),
                   jax.ShapeDtypeStruct((8, 512), F32)] + [jax.ShapeDtypeStruct(g.shape, g.dtype) for g in carried],
        scratch_shapes=[pltpu.VMEM((RET_HEADS, RET_DK, 128), F32),
                        pltpu.VMEM((GLA_HEADS * GLA_DK, GLA_HEADS * GLA_DV), F32),
                        pltpu.VMEM((CHUNKS_PER_STEP, GLA_HEADS, CHUNK, CHUNK), F32),
                        pltpu.VMEM((CHUNKS_PER_STEP, CHUNK, 256), F32),
                        pltpu.VMEM((CHUNKS_PER_STEP, CHUNK, 256), F32)] + _exchange_sems(n_carried),
        compiler_params=_cparams("arbitrary"),
    )(proj, ocat, dmrg, sr_all, sg_all, cos2, sin2, w2p, gb, rnw, gnw, *carried)


def _all_gather(xs, name):
    n = len(xs)

    def body(*refs):
        start, forward, finish = _gather_phases(refs[:n], refs[n:2 * n], *refs[2 * n:])
        start()
        forward()
        finish()

    return pl.pallas_call(
        body, name=name,
        in_specs=[pl.BlockSpec(memory_space=pl.ANY)] * n,
        out_specs=[pl.BlockSpec(memory_space=pl.ANY)] * n,
        out_shape=_gathered_shapes(xs),
        scratch_shapes=_exchange_sems(n),
    )(*xs)


def _gathered_shapes(xs):
    return [jax.ShapeDtypeStruct((N_DEV,) + x.shape, x.dtype) for x in xs]


def _exchange_sems(n):
    if n == 0:
        return []
    return [pltpu.SemaphoreType.DMA((7 * n,)), pltpu.SemaphoreType.DMA((7 * n,)), pltpu.SemaphoreType.DMA((n,))]


def _gather_phases(x_refs, out_refs, send_sems, recv_sems, local_sems):
    n = len(x_refs)
    mx, my, mc = lax.axis_index("x"), lax.axis_index("y"), lax.axis_index("c")
    me, sibling = (mx, my, mc), (mx, my, 1 - mc)
    chips = [(1 - mx, my), (mx, 1 - my), (1 - mx, 1 - my)]

    def slot(a, px, py, pc):
        return out_refs[a].at[4 * px + 2 * py + pc]

    def copy(a, k, block, to, src=None):
        return pltpu.make_async_remote_copy(
            src_ref=slot(a, *block) if src is None else src, dst_ref=slot(a, *block),
            send_sem=send_sems.at[7 * a + k], recv_sem=recv_sems.at[7 * a + k],
            device_id=to, device_id_type=MESH_IDS)

    mine = [pltpu.make_async_copy(x_refs[a], slot(a, *me), local_sems.at[a]) for a in range(n)]
    first = []
    for a in range(n):
        first.append(copy(a, 0, me, sibling, src=x_refs[a]))
        first += [copy(a, 1 + j, me, (*chip, mc), src=x_refs[a]) for j, chip in enumerate(chips)]
    passed = [copy(a, 4 + j, (*chip, mc), sibling) for j, chip in enumerate(chips) for a in range(n)]

    def start():
        for cp in mine + first:
            cp.start()

    def forward():
        for j, chip in enumerate(chips):
            for a in range(n):
                copy(a, 1 + j, (*chip, mc), me).wait_recv()
                passed[j * n + a].start()

    def finish():
        for a in range(n):
            copy(a, 0, sibling, me).wait_recv()
            for j, chip in enumerate(chips):
                copy(a, 4 + j, (*chip, 1 - mc), me).wait_recv()
        for cp in first + passed:
            cp.wait_send()
        for cp in mine:
            cp.wait()

    return start, forward, finish


def _exchange_blocks(gs, name):
    n = len(gs)

    def body(*refs):
        start, finish = _exchange_phases(refs[:n], refs[n:2 * n], *refs[2 * n:])
        start()
        finish()

    return pl.pallas_call(
        body, name=name,
        in_specs=[pl.BlockSpec(memory_space=pl.ANY)] * n,
        out_specs=[pl.BlockSpec(memory_space=pl.ANY)] * n,
        out_shape=[jax.ShapeDtypeStruct(g.shape, g.dtype) for g in gs],
        scratch_shapes=_exchange_sems(n),
    )(*gs)


def _exchange_phases(g_refs, out_refs, send_sems, recv_sems, local_sems):
    n = len(g_refs)
    mx, my, mc = lax.axis_index("x"), lax.axis_index("y"), lax.axis_index("c")
    me = 4 * mx + 2 * my + mc
    mine = [pltpu.make_async_copy(g_refs[a].at[me], out_refs[a].at[me], local_sems.at[a]) for a in range(n)]
    copies = []
    for r in range(1, N_DEV):
        px, py, pc = mx ^ (r >> 2), my ^ ((r >> 1) & 1), mc ^ (r & 1)
        peer = 4 * px + 2 * py + pc
        for a in range(n):
            copies.append(pltpu.make_async_remote_copy(
                src_ref=g_refs[a].at[peer], dst_ref=out_refs[a].at[me],
                send_sem=send_sems.at[7 * a + r - 1], recv_sem=recv_sems.at[7 * a + r - 1],
                device_id=(px, py, pc), device_id_type=MESH_IDS))

    def start():
        for cp in mine + copies:
            cp.start()

    def finish():
        for cp in copies:
            cp.wait_recv()
        for cp in copies:
            cp.wait_send()
        for cp in mine:
            cp.wait()

    return start, finish


IN_SHARD = IN_W // N_DEV
IN_SHARD_P = 512
UP_SHARD = D_UP // N_DEV
UP_SHARD_P = 768
RELAYOUT_ROWS = 256


def _pieces_w_in():
    return [(k, 0, IN_SHARD * k, IN_SHARD) for k in range(N_DEV)]


def _pieces_ffn_up():
    pieces = []
    for k in range(N_DEV):
        n, end = UP_SHARD * k, UP_SHARD * (k + 1)
        while n < end:
            half, r = divmod(n, D_FF)
            blk, off = divmod(r, CONV_BLOCK)
            run = min(CONV_BLOCK - off, end - n)
            pieces.append((k, n - UP_SHARD * k, 2 * CONV_BLOCK * blk + CONV_BLOCK * half + off, run))
            n += run
    return pieces


def _assemble_block(load, spans, dst_block, rows):
    lo = 128 * dst_block
    lane = lax.broadcasted_iota(jnp.int32, (1, 128), 1)
    out = jnp.zeros((rows, 128), F32)
    for key, src_off, dst_off, length in spans:
        a, b = max(lo, dst_off), min(lo + 128, dst_off + length)
        s, s_end = src_off + (a - dst_off), src_off + (b - dst_off)
        d = a
        while s < s_end:
            e = min(s_end, 128 * (s // 128 + 1))
            blk = load(key, s // 128)
            shift = (d - s) % 128
            if shift:
                blk = pltpu.roll(blk, shift, 1)
            out = jnp.where((lane >= d - lo) & (lane < d - lo + (e - s)), blk, out)
            d += e - s
            s = e
    return out


def _shards_to_cols(shards, pieces, width, name):
    _, rows, _ = shards.shape
    tr = RELAYOUT_ROWS

    def body(s_ref, o_ref):
        load = lambda k, b: s_ref[k, :, 128 * b:128 * (b + 1)].astype(F32)
        for db in range(width // 128):
            o_ref[:, 128 * db:128 * (db + 1)] = _assemble_block(load, pieces, db, tr).astype(BF16)

    return pl.pallas_call(
        body, name=name, grid=(rows // tr,),
        in_specs=[pl.BlockSpec((N_DEV, tr, shards.shape[2]), lambda i: (0, i, 0))],
        out_specs=pl.BlockSpec((tr, width), lambda i: (i, 0)),
        out_shape=jax.ShapeDtypeStruct((rows, width), BF16),
        compiler_params=_cparams("parallel"),
    )(shards)


def _cols_to_shards(full, pieces, shard_width, name):
    rows, width = full.shape
    tr = RELAYOUT_ROWS

    def body(f_ref, o_ref):
        load = lambda _, b: f_ref[:, 128 * b:128 * (b + 1)].astype(F32)
        for k in range(N_DEV):
            spans = [(None, dst_off, src_off, length) for dev, src_off, dst_off, length in pieces if dev == k]
            for db in range(shard_width // 128):
                o_ref[k, :, 128 * db:128 * (db + 1)] = _assemble_block(load, spans, db, tr).astype(BF16)

    return pl.pallas_call(
        body, name=name, grid=(rows // tr,),
        in_specs=[pl.BlockSpec((tr, width), lambda i: (i, 0))],
        out_specs=pl.BlockSpec((N_DEV, tr, shard_width), lambda i: (0, i, 0)),
        out_shape=jax.ShapeDtypeStruct((N_DEV, rows, shard_width), BF16),
        compiler_params=_cparams("parallel"),
    )(full)


def _adamw(parts, w, m, v, rows_per_step, name):
    rows, cols = w.shape
    assert rows % rows_per_step == 0 and parts.shape == (N_DEV, rows, cols)

    def body(p_ref, w_ref, m_ref, v_ref, g_ref, d_ref, nm_ref, nv_ref):
        g = p_ref[0].astype(F32)
        for j in range(1, N_DEV):
            g = g + p_ref[j].astype(F32)
        m_new = ADAM_B1 * m_ref[...] + (1.0 - ADAM_B1) * g
        v_new = ADAM_B2 * v_ref[...] + (1.0 - ADAM_B2) * (g * g)
        m_hat = m_new / (1.0 - ADAM_B1 ** ADAM_STEP)
        v_hat = v_new / (1.0 - ADAM_B2 ** ADAM_STEP)
        g_ref[...] = g
        d_ref[...] = -ADAM_LR * (m_hat / (jnp.sqrt(v_hat) + ADAM_EPS) + ADAM_WD * w_ref[...])
        nm_ref[...] = m_new
        nv_ref[...] = v_new

    tile = pl.BlockSpec((rows_per_step, cols), lambda i: (i, 0))
    shape = jax.ShapeDtypeStruct((rows, cols), F32)
    return pl.pallas_call(
        body, name=name, grid=(rows // rows_per_step,),
        in_specs=[pl.BlockSpec((N_DEV, rows_per_step, cols), lambda i: (0, i, 0)), tile, tile, tile],
        out_specs=[tile, tile, tile, tile],
        out_shape=[shape, shape, shape, shape],
        compiler_params=_cparams("parallel"),
    )(parts, w, m, v)


BIG = (("w_in", (DEPTH, D, IN_W // N_DEV), 2), ("w_out", (DEPTH, D // N_DEV, D), 1),
       ("ffn_up", (DEPTH, D, D_UP // N_DEV), 2), ("ffn_down", (DEPTH, D_FF // N_DEV, D), 1))
SMALL = (("meta_tokens", (N_META, D // N_DEV), 1), ("gla_gate_w2", (DEPTH, GATE_RANK, 256 // N_DEV), 2),
         ("ffn_conv_w", (DEPTH, 3, D_UP // N_DEV), 2))
REPL = (("pre_mix_norm", (DEPTH, D)), ("gla_gate_b", (DEPTH, 256)), ("ret_norm_w", (DEPTH, 512)),
        ("gla_norm_w", (DEPTH, 512)), ("post_mix_norm", (DEPTH, D)), ("pre_ffn_norm", (DEPTH, D)),
        ("ffn_conv_b", (DEPTH, D_UP)), ("post_ffn_norm", (DEPTH, D)))
WEIGHT_ORDER = ("meta_tokens", "pre_mix_norm", "w_in", "gla_gate_w2", "gla_gate_b", "ret_norm_w", "gla_norm_w",
                "w_out", "post_mix_norm", "pre_ffn_norm", "ffn_up", "ffn_conv_w", "ffn_conv_b", "ffn_down",
                "post_ffn_norm")


def _size(shape):
    return math.prod(shape)


def _round_up(n, mult):
    return -(-n // mult) * mult


REPL_ROWS = _round_up(-(-sum(_size(s) for _, s in REPL) // LANES), 8)
SMALL_ROWS = _round_up(-(-sum(_size(s) for _, s, _ in SMALL) // LANES), 8)


def _pack(arrays, rows, dtype):
    flat = jnp.concatenate([a.reshape(-1).astype(dtype) for a in arrays])
    return jnp.pad(flat, (0, rows * LANES - flat.shape[0])).reshape(rows, LANES)


def _unpack(buf, shapes):
    flat = buf.reshape(-1)
    out, off = [], 0
    for shape in shapes:
        out.append(flat[off:off + _size(shape)].reshape(shape))
        off += _size(shape)
    return out


def _unshard(blocks, axis):
    moved = jnp.moveaxis(blocks, 0, axis)
    shape = list(moved.shape)
    shape[axis:axis + 2] = [shape[axis] * shape[axis + 1]]
    return moved.reshape(shape)


def _to_blocks(full, axis):
    shape = list(full.shape)
    shape[axis:axis + 1] = [N_DEV, shape[axis] // N_DEV]
    return jnp.moveaxis(full.reshape(shape), axis, 0)


def _interleave_cols(w):
    lead = w.shape[:-1]
    return jnp.swapaxes(w.reshape(lead + (2, N_CONV_BLOCKS, CONV_BLOCK)), -3, -2).reshape(lead + (D_UP,))


def _deinterleave_cols(w):
    lead = w.shape[:-1]
    return jnp.swapaxes(w.reshape(lead + (N_CONV_BLOCKS, 2, CONV_BLOCK)), -3, -2).reshape(lead + (D_UP,))


def _rope_tables():
    half = RET_DK // 2
    inv = ROPE_BASE ** (-jnp.arange(half, dtype=F32) / half)
    pos = jnp.arange(LP, dtype=F32) - float(PAD_ROWS)
    ang = pos[:, None] * inv[None, :]
    c, s = jnp.cos(ang), jnp.sin(ang)
    return jnp.concatenate([c, c], axis=1), jnp.concatenate([-s, s], axis=1)


def kernel(x, meta_tokens, pre_mix_norm, w_in, gla_gate_w2, gla_gate_b, ret_norm_w, gla_norm_w, w_out, post_mix_norm, pre_ffn_norm, ffn_up, ffn_conv_w, ffn_conv_b, ffn_down, post_ffn_norm, loss_target, m_meta_tokens, m_pre_mix_norm, m_w_in, m_gla_gate_w2, m_gla_gate_b, m_ret_norm_w, m_gla_norm_w, m_w_out, m_post_mix_norm, m_pre_ffn_norm, m_ffn_up, m_ffn_conv_w, m_ffn_conv_b, m_ffn_down, m_post_ffn_norm, v_meta_tokens, v_pre_mix_norm, v_w_in, v_gla_gate_w2, v_gla_gate_b, v_ret_norm_w, v_gla_norm_w, v_w_out, v_post_mix_norm, v_pre_ffn_norm, v_ffn_up, v_ffn_conv_w, v_ffn_conv_b, v_ffn_down, v_post_ffn_norm):
    weights = dict(meta_tokens=meta_tokens, pre_mix_norm=pre_mix_norm, w_in=w_in, gla_gate_w2=gla_gate_w2,
                   gla_gate_b=gla_gate_b, ret_norm_w=ret_norm_w, gla_norm_w=gla_norm_w, w_out=w_out,
                   post_mix_norm=post_mix_norm, pre_ffn_norm=pre_ffn_norm, ffn_up=ffn_up, ffn_conv_w=ffn_conv_w,
                   ffn_conv_b=ffn_conv_b, ffn_down=ffn_down, post_ffn_norm=post_ffn_norm)
    mom1 = dict(meta_tokens=m_meta_tokens, pre_mix_norm=m_pre_mix_norm, w_in=m_w_in, gla_gate_w2=m_gla_gate_w2,
                gla_gate_b=m_gla_gate_b, ret_norm_w=m_ret_norm_w, gla_norm_w=m_gla_norm_w, w_out=m_w_out,
                post_mix_norm=m_post_mix_norm, pre_ffn_norm=m_pre_ffn_norm, ffn_up=m_ffn_up,
                ffn_conv_w=m_ffn_conv_w, ffn_conv_b=m_ffn_conv_b, ffn_down=m_ffn_down, post_ffn_norm=m_post_ffn_norm)
    mom2 = dict(meta_tokens=v_meta_tokens, pre_mix_norm=v_pre_mix_norm, w_in=v_w_in, gla_gate_w2=v_gla_gate_w2,
                gla_gate_b=v_gla_gate_b, ret_norm_w=v_ret_norm_w, gla_norm_w=v_gla_norm_w, w_out=v_w_out,
                post_mix_norm=v_post_mix_norm, pre_ffn_norm=v_pre_ffn_norm, ffn_up=v_ffn_up,
                ffn_conv_w=v_ffn_conv_w, ffn_conv_b=v_ffn_conv_b, ffn_down=v_ffn_down, post_ffn_norm=v_post_ffn_norm)

    pad_cols = lambda a, width: jnp.pad(a, ((0, 0), (0, width - a.shape[1])))
    big_names = [n for n, _, _ in BIG]
    shard = {}
    for l in range(DEPTH):
        shard[l, "w_in"] = pad_cols(w_in[l].astype(BF16), IN_SHARD_P)
        shard[l, "w_out"] = w_out[l].astype(BF16)
        shard[l, "ffn_up"] = pad_cols(ffn_up[l].astype(BF16), UP_SHARD_P)
        shard[l, "ffn_down"] = ffn_down[l].astype(BF16)
    w_in_0, small = _all_gather([shard[0, "w_in"], _pack([weights[n] for n, _, _ in SMALL], SMALL_ROWS, F32)],
                                "gather_first_weights")
    gathered = {(0, "w_in"): w_in_0}
    gather_in_mixer = {l: [(l, n) for n in big_names[1:]] for l in range(DEPTH)}
    gather_in_conv = {l: [(l + 1, "w_in")] for l in range(DEPTH - 1)}
    small_parts = _unpack_blocks(small, [s for _, s, _ in SMALL])
    full = {n: _unshard(p, ax) for (n, _, ax), p in zip(SMALL, small_parts)}
    w2p = jnp.pad(full["gla_gate_w2"], ((0, 0), (0, 128 - GATE_RANK), (0, 0)))
    cw8 = jnp.concatenate([_interleave_cols(full["ffn_conv_w"]), _interleave_cols(ffn_conv_b)[:, None, :],
                           jnp.zeros((DEPTH, 4, D_UP), F32)], axis=1)
    cos2, sin2 = _rope_tables()

    h = jnp.concatenate([jnp.zeros((PAD_ROWS, D), F32), full["meta_tokens"], x[0]], axis=0)
    target = jnp.concatenate([jnp.zeros((CHUNK, D), F32), loss_target[0]], axis=0)
    saved, layer_w = [], []
    for l in range(DEPTH):
        lw = dict(w_in=_shards_to_cols(gathered[l, "w_in"], _pieces_w_in(), IN_WP, f"w_in_cols_{l}"))
        a1, proj = _norm_matmul(h, pre_mix_norm[l:l + 1], lw["w_in"], out_dtype=BF16, tm=TM_BIG, tn=IN_WP // 3,
                                name=f"in_proj_{l}")
        keys = gather_in_mixer.get(l, [])
        ocat, merged, sr_all, sg_all, *got = _mixer_fwd(proj, cos2, sin2, w2p[l], gla_gate_b[l:l + 1],
                                                        ret_norm_w[l:l + 1], gla_norm_w[l:l + 1], f"mixer_fwd_{l}",
                                                        carried=[shard[key] for key in keys])
        gathered.update(zip(keys, got))
        lw["w_out"] = gathered[l, "w_out"].reshape(D, D)
        lw["w_up"] = _shards_to_cols(gathered[l, "ffn_up"], _pieces_ffn_up(), D_UP, f"ffn_up_cols_{l}")
        lw["w_down"] = gathered[l, "ffn_down"].reshape(D_FF, D)
        layer_w.append(lw)
        m, h1 = _matmul_resid_norm(merged, lw["w_out"], h, post_mix_norm[l:l + 1], f"out_proj_{l}")
        a2, u = _norm_matmul(h1, pre_ffn_norm[l:l + 1], lw["w_up"], out_dtype=BF16, tm=TM_BIG, tn=D_UP // 2,
                             name=f"ffn_up_{l}")
        keys = gather_in_conv.get(l, [])
        cv, act, *got = _conv_act_fwd(u, cw8[l], f"ffn_conv_act_{l}", carried=[shard[key] for key in keys])
        gathered.update(zip(keys, got))
        f, h2, *loss_acc = _matmul_resid_norm(act, lw["w_down"], h1, post_ffn_norm[l:l + 1], f"ffn_down_{l}",
                                              target=target if l == DEPTH - 1 else None)
        saved.append(dict(h=h, a1=a1, proj=proj, ocat=ocat, merged=merged, sr=sr_all, sg=sg_all, m=m, h1=h1,
                          a2=a2, u=u, cv=cv, act=act, f=f))
        h = h2

    dh = h
    loss = lax.psum(loss_acc[0][0, 0], ("x", "y", "c"))

    kinds = ("grad", "delta", "new_m", "new_v")
    grads = {n: [None] * DEPTH for n in WEIGHT_ORDER if n != "meta_tokens" and n not in big_names}
    pending, parts = [], {}
    for l in reversed(range(DEPTH)):
        s, lw = saved[l], layer_w[l]
        dact, df, g_post_ffn = _norm_bwd_matmul(dh, s["f"], post_ffn_norm[l:l + 1], lw["w_down"], BF16,
                                                f"ffn_down_dx_{l}")
        g_down = _matmul(s["act"], df, ta=True, out_dtype=BF16, tm=D_FF // 2, tn=D, tk=TK_LONG, name=f"ffn_down_dw_{l}")
        du, dcw = _conv_act_bwd(dact, s["cv"], s["u"], cw8[l], f"ffn_conv_act_bwd_{l}")
        dh1, g_pre_ffn = _matmul_norm_bwd(du, lw["w_up"], s["h1"], pre_ffn_norm[l:l + 1], dh, D_FF, f"ffn_up_dx_{l}")
        g_up = _matmul(s["a2"], du, ta=True, out_dtype=BF16, tm=D, tn=D_UP // 4, tk=TK_LONG, name=f"ffn_up_dw_{l}")
        dmerged, dm, g_post_mix = _norm_bwd_matmul(dh1, s["m"], post_mix_norm[l:l + 1], lw["w_out"], F32,
                                                   f"out_proj_dx_{l}")
        g_out = _matmul(s["merged"], dm, ta=True, out_dtype=BF16, tm=D, tn=D, tk=TK_LONG, name=f"out_proj_dw_{l}")
        pending += [((l, "ffn_down"), g_down.reshape(N_DEV, D_FF // N_DEV, D)),
                    ((l, "ffn_up"), _cols_to_shards(g_up, _pieces_ffn_up(), UP_SHARD_P, f"ffn_up_grad_shards_{l}")),
                    ((l, "w_out"), g_out.reshape(N_DEV, D // N_DEV, D))]
        dproj, g_w2, g_gb, g_rn, g_gn, *got = _mixer_bwd(s["proj"], s["ocat"], dmerged, s["sr"], s["sg"], cos2, sin2,
                                                         w2p[l], gla_gate_b[l:l + 1], ret_norm_w[l:l + 1],
                                                         gla_norm_w[l:l + 1], f"mixer_bwd_{l}",
                                                         carried=[blocks for _, blocks in pending])
        parts.update(zip([key for key, _ in pending], got))
        g_in = _matmul(s["a1"], dproj, ta=True, out_dtype=BF16, tm=D, tn=IN_WP // 3, tk=TK_LONG, name=f"in_proj_dw_{l}")
        pending = [((l, "w_in"), _cols_to_shards(g_in, _pieces_w_in(), IN_SHARD_P, f"w_in_grad_shards_{l}"))]
        now = pending if l == 0 else []
        dh, g_pre_mix, *got = _matmul_norm_bwd(dproj, lw["w_in"], s["h"], pre_mix_norm[l:l + 1], dh1, IN_WP,
                                               f"in_proj_dx_{l}", carried=[blocks for _, blocks in now])
        parts.update(zip([key for key, _ in now], got))
        pending = [] if l == 0 else pending
        grads["post_ffn_norm"][l] = g_post_ffn[0]
        grads["ffn_conv_w"][l] = _deinterleave_cols(dcw[0:3])
        grads["ffn_conv_b"][l] = _deinterleave_cols(dcw[3])
        grads["pre_ffn_norm"][l] = g_pre_ffn[0]
        grads["post_mix_norm"][l] = g_post_mix[0]
        grads["gla_gate_w2"][l] = g_w2[:GATE_RANK]
        grads["gla_gate_b"][l] = g_gb[0]
        grads["ret_norm_w"][l] = g_rn[0]
        grads["gla_norm_w"][l] = g_gn[0]
        grads["pre_mix_norm"][l] = g_pre_mix[0]
    local = {n: jnp.stack(v) for n, v in grads.items()}
    local["meta_tokens"] = dh[PAD_ROWS:CHUNK]
    grad_x = dh[CHUNK:][None]

    blocks = jnp.concatenate([_to_blocks(local[n], ax).reshape(N_DEV, -1) for n, _, ax in SMALL], axis=1)
    blocks = jnp.pad(blocks, ((0, 0), (0, SMALL_ROWS * LANES - blocks.shape[1]))).reshape(N_DEV, SMALL_ROWS, LANES)
    *got, small_grad_parts = _exchange_blocks([b for _, b in pending] + [blocks], "exchange_last_grads")
    parts.update(zip([key for key, _ in pending], got))

    widths = dict(w_in=IN_SHARD_P, w_out=D, ffn_up=UP_SHARD_P, ffn_down=D)
    steps = dict(w_in=256, w_out=D // N_DEV, ffn_up=256, ffn_down=D_FF // N_DEV // 2)
    big_out = {kind: {n: [None] * DEPTH for n in big_names} for kind in kinds}
    for l in range(DEPTH):
        for n in big_names:
            mine = [pad_cols(d[n][l], widths[n]) for d in (weights, mom1, mom2)]
            results = _adamw(parts[l, n], *mine, steps[n], f"adamw_{n}_{l}")
            for kind, r in zip(kinds, results):
                big_out[kind][n][l] = r[:, :weights[n].shape[2]]
    out = {kind: {n: jnp.stack(v) for n, v in big_out[kind].items()} for kind in kinds}
    shard_shapes = [s for _, s, _ in SMALL]
    packed = [_pack([d[n] for n, _, _ in SMALL], SMALL_ROWS, F32) for d in (weights, mom1, mom2)]
    results = _adamw(small_grad_parts, *packed, SMALL_ROWS, "adamw_small_sharded")
    for kind, buf in zip(kinds, results):
        out[kind].update(zip([n for n, _, _ in SMALL], _unpack(buf, shard_shapes)))

    repl_parts = _all_gather([_pack([local[n] for n, _ in REPL], REPL_ROWS, F32)], "gather_small_grads")[0]
    packed = [_pack([d[n] for n, _ in REPL], REPL_ROWS, F32) for d in (weights, mom1, mom2)]
    results = _adamw(repl_parts, *packed, REPL_ROWS, "adamw_replicated")
    repl_shapes = [s for _, s in REPL]
    for kind, buf in zip(kinds, results):
        out[kind].update(zip([n for n, _ in REPL], _unpack(buf, repl_shapes)))

    return (loss, grad_x, *[out["grad"][n] for n in WEIGHT_ORDER], *[out["delta"][n] for n in WEIGHT_ORDER],
            *[out["new_m"][n] for n in WEIGHT_ORDER], *[out["new_v"][n] for n in WEIGHT_ORDER])


def _unpack_blocks(gathered, shapes):
    flat = gathered.reshape(N_DEV, -1)
    out, off = [], 0
    for shape in shapes:
        out.append(flat[:, off:off + _size(shape)].reshape((N_DEV,) + shape))
        off += _size(shape)
    return out
```

```python
import math

import jax
import jax.numpy as jnp
from jax import lax
from jax.experimental import pallas as pl
from jax.experimental.pallas import tpu as pltpu

F32 = jnp.float32
BF16 = jnp.bfloat16

D = 1024
SEQ = 8192
DEPTH = 2
N_META = 16
CHUNK = 64
SUB = 16
N_SUB = CHUNK // SUB
PAD_ROWS = CHUNK - N_META
LP = SEQ + CHUNK
N_CHUNKS = LP // CHUNK
RET_HEADS = 4
RET_DK = 128
GLA_HEADS = 4
GLA_DK = 64
GLA_DV = 128
GLA_TAU = 16.0
GATE_RANK = 16
IN_W = 3600
IN_WP = 3840
D_FF = 2816
D_UP = 2 * D_FF
CONV_BLOCK = 256
N_CONV_BLOCKS = D_FF // CONV_BLOCK
ROPE_BASE = 10000.0
EPS = 1e-6
N_DEV = 8
LANES = 1024

O_RQ, O_RK, O_RV, O_RG = 0, 512, 1024, 1536
O_GQ, O_GK, O_GV, O_GR, O_GA = 2048, 2304, 2560, 3072, 3584

ADAM_LR = 0.001
ADAM_B1 = 0.9
ADAM_B2 = 0.999
ADAM_EPS = 1e-08
ADAM_WD = 0.01
ADAM_STEP = 10

VMEM_LIMIT = 56 * 1024 * 1024
MESH_IDS = pl.DeviceIdType.MESH


def _row_tile(rows, limit):
    best = 16
    for t in range(16, min(rows, limit) + 1, 16):
        if rows % t == 0:
            best = t
    return best


TM = _row_tile(LP, 688)
TM_BIG = _row_tile(LP, 1376)
TK_LONG = _row_tile(LP, 2752)


def _cparams(*sem):
    return pltpu.CompilerParams(dimension_semantics=sem, vmem_limit_bytes=VMEM_LIMIT)


def _dot(a, b):
    return jnp.dot(a.astype(BF16), b.astype(BF16), preferred_element_type=F32)


def _dot_nt(a, b):
    return lax.dot_general(a.astype(BF16), b.astype(BF16), (((1,), (1,)), ((), ())), preferred_element_type=F32)


def _dot_tn(a, b):
    return lax.dot_general(a.astype(BF16), b.astype(BF16), (((0,), (0,)), ((), ())), preferred_element_type=F32)


def _split3(x):
    hi = x.astype(BF16)
    r1 = x - hi.astype(F32)
    mid = r1.astype(BF16)
    lo = (r1 - mid.astype(F32)).astype(BF16)
    return hi, mid, lo


def _dot_exact_rhs(t, x):
    n = x.shape[1]
    parts = jnp.dot(t.astype(BF16), jnp.concatenate(_split3(x), axis=1), preferred_element_type=F32)
    return parts[:, :n] + parts[:, n:2 * n] + parts[:, 2 * n:]


def _dot_tn_exact_lhs(x, ones):
    n = x.shape[1]
    parts = lax.dot_general(jnp.concatenate(_split3(x), axis=1), ones.astype(BF16), (((0,), (0,)), ((), ())),
                            preferred_element_type=F32)
    return parts[:n] + parts[n:2 * n] + parts[2 * n:]


def _sigmoid(x):
    return 1.0 / (1.0 + jnp.exp(-x))


def _matmul(a, b, *, ta=False, tb=False, out_dtype, tm, tn, tk, name):
    m = a.shape[1] if ta else a.shape[0]
    k = a.shape[0] if ta else a.shape[1]
    n = b.shape[0] if tb else b.shape[1]
    assert (b.shape[1] if tb else b.shape[0]) == k
    assert m % tm == 0 and n % tn == 0 and k % tk == 0, (name, m, n, k, tm, tn, tk)
    nk = k // tk
    a_spec = pl.BlockSpec((tk, tm), lambda i, j, kk: (kk, i)) if ta else pl.BlockSpec((tm, tk), lambda i, j, kk: (i, kk))
    b_spec = pl.BlockSpec((tn, tk), lambda i, j, kk: (j, kk)) if tb else pl.BlockSpec((tk, tn), lambda i, j, kk: (kk, j))
    dims = (((0 if ta else 1,), (1 if tb else 0,)), ((), ()))

    def body(a_ref, b_ref, o_ref, *acc):
        prod = lax.dot_general(a_ref[...].astype(BF16), b_ref[...].astype(BF16), dims, preferred_element_type=F32)
        if nk == 1:
            o_ref[...] = prod.astype(out_dtype)
            return
        acc_ref, = acc
        kk = pl.program_id(2)

        @pl.when(kk == 0)
        def _():
            acc_ref[...] = prod

        @pl.when(kk > 0)
        def _():
            acc_ref[...] += prod

        @pl.when(kk == nk - 1)
        def _():
            o_ref[...] = acc_ref[...].astype(out_dtype)

    return pl.pallas_call(
        body, name=name, grid=(m // tm, n // tn, nk),
        in_specs=[a_spec, b_spec],
        out_specs=pl.BlockSpec((tm, tn), lambda i, j, kk: (i, j)),
        out_shape=jax.ShapeDtypeStruct((m, n), out_dtype),
        scratch_shapes=[pltpu.VMEM((tm, tn), F32)] if nk > 1 else [],
        compiler_params=_cparams("parallel", "parallel", "arbitrary"),
    )(a, b)


def _norm_matmul(x, w, b, *, out_dtype, tm, tn, name):
    n = b.shape[1]
    assert LP % tm == 0 and n % tn == 0

    def body(x_ref, w_ref, b_ref, a_ref, o_ref, a_scr):
        @pl.when(pl.program_id(1) == 0)
        def _():
            xv = x_ref[...]
            r = lax.rsqrt(jnp.mean(xv * xv, axis=-1, keepdims=True) + EPS)
            a = (xv * r * w_ref[...]).astype(BF16)
            a_scr[...] = a
            a_ref[...] = a

        o_ref[...] = jnp.dot(a_scr[...], b_ref[...], preferred_element_type=F32).astype(out_dtype)

    return pl.pallas_call(
        body, name=name, grid=(LP // tm, n // tn),
        in_specs=[pl.BlockSpec((tm, D), lambda i, j: (i, 0)), pl.BlockSpec((1, D), lambda i, j: (0, 0)),
                  pl.BlockSpec((D, tn), lambda i, j: (0, j))],
        out_specs=[pl.BlockSpec((tm, D), lambda i, j: (i, 0)), pl.BlockSpec((tm, tn), lambda i, j: (i, j))],
        out_shape=[jax.ShapeDtypeStruct((LP, D), BF16), jax.ShapeDtypeStruct((LP, n), out_dtype)],
        scratch_shapes=[pltpu.VMEM((tm, D), BF16)],
        compiler_params=_cparams("arbitrary", "arbitrary"),
    )(x, w, b)


def _matmul_resid_norm(a, b, h, w, name, target=None):
    k = a.shape[1]
    has_loss = target is not None

    def body(a_ref, b_ref, h_ref, w_ref, *refs):
        m = jnp.dot(a_ref[...].astype(BF16), b_ref[...].astype(BF16), preferred_element_type=F32)
        r = lax.rsqrt(jnp.mean(m * m, axis=-1, keepdims=True) + EPS)
        i = pl.program_id(0)
        row = i * TM + lax.broadcasted_iota(jnp.int32, (TM, 1), 0)
        y = h_ref[...] + jnp.where(row >= PAD_ROWS, m * r * w_ref[...], 0.0)
        if not has_loss:
            m_ref, y_ref = refs
            m_ref[...] = m
            y_ref[...] = y
            return
        t_ref, m_ref, dy_ref, loss_ref = refs
        m_ref[...] = m

        @pl.when(i == 0)
        def _():
            loss_ref[...] = jnp.zeros_like(loss_ref)

        diff = jnp.where(row >= CHUNK, y - t_ref[...], 0.0)
        dy_ref[...] = diff * (1.0 / D)
        loss_ref[...] += (0.5 / D) * jnp.sum(diff * diff)

    tile = pl.BlockSpec((TM, D), lambda i: (i, 0))
    shape = jax.ShapeDtypeStruct((LP, D), F32)
    in_specs = [pl.BlockSpec((TM, k), lambda i: (i, 0)), pl.BlockSpec((k, D), lambda i: (0, 0)), tile,
                pl.BlockSpec((1, D), lambda i: (0, 0))]
    if has_loss:
        return pl.pallas_call(
            body, name=name, grid=(LP // TM,),
            in_specs=in_specs + [tile],
            out_specs=[tile, tile, pl.BlockSpec((8, 128), lambda i: (0, 0))],
            out_shape=[shape, shape, jax.ShapeDtypeStruct((8, 128), F32)],
            compiler_params=_cparams("arbitrary"),
        )(a, b, h, w, target)
    return pl.pallas_call(
        body, name=name, grid=(LP // TM,),
        in_specs=in_specs, out_specs=[tile, tile], out_shape=[shape, shape],
        compiler_params=_cparams("parallel"),
    )(a, b, h, w)


def _rmsnorm_bwd_rows(dy, x, w):
    r = lax.rsqrt(jnp.mean(x * x, axis=-1, keepdims=True) + EPS)
    g = dy * w
    dx = r * g - x * (r * r * r * jnp.mean(g * x, axis=-1, keepdims=True))
    return dx, jnp.sum(dy * x * r, axis=0, keepdims=True)


def _matmul_norm_bwd(dz, b, x, w, resid, tk, name, carried=()):
    k = dz.shape[1]
    assert k % tk == 0
    nk = k // tk
    n_rows = LP // TM
    n_carried = len(carried)

    def body(*refs):
        a_ref, b_ref, x_ref, w_ref, r_ref = refs[:5]
        g_refs, refs = refs[5:5 + n_carried], refs[5 + n_carried:]
        dx_ref, dw_ref = refs[:2]
        got_refs, refs = refs[2:2 + n_carried], refs[2 + n_carried:]
        acc, sems = (refs[:1], refs[1:]) if nk > 1 else ((), refs)
        i, kk = pl.program_id(0), pl.program_id(1)
        if n_carried:
            exchange_start, exchange_finish = _exchange_phases(g_refs, got_refs, *sems)
            pl.when((i == 0) & (kk == 0))(exchange_start)

        @pl.when((i == 0) & (kk == 0))
        def _():
            dw_ref[...] = jnp.zeros_like(dw_ref)

        prod = lax.dot_general(a_ref[...].astype(BF16), b_ref[...].astype(BF16), (((1,), (1,)), ((), ())),
                               preferred_element_type=F32)

        def finish(dy):
            dx, dw = _rmsnorm_bwd_rows(dy, x_ref[...], w_ref[...])
            dx_ref[...] = dx + r_ref[...]
            dw_ref[0:1, :] += dw

        if nk == 1:
            finish(prod)
        else:
            acc_ref, = acc

            @pl.when(kk == 0)
            def _():
                acc_ref[...] = prod

            @pl.when((kk > 0) & (kk < nk - 1))
            def _():
                acc_ref[...] += prod

            @pl.when(kk == nk - 1)
            def _():
                finish(acc_ref[...] + prod)

        if n_carried:
            pl.when((i == n_rows - 1) & (kk == nk - 1))(exchange_finish)

    tile = pl.BlockSpec((TM, D), lambda i, kk: (i, 0))
    anywhere = [pl.BlockSpec(memory_space=pl.ANY)] * n_carried
    return pl.pallas_call(
        body, name=name, grid=(n_rows, nk),
        in_specs=[pl.BlockSpec((TM, tk), lambda i, kk: (i, kk)), pl.BlockSpec((D, tk), lambda i, kk: (0, kk)), tile,
                  pl.BlockSpec((1, D), lambda i, kk: (0, 0)), tile] + anywhere,
        out_specs=[tile, pl.BlockSpec((8, D), lambda i, kk: (0, 0))] + anywhere,
        out_shape=[jax.ShapeDtypeStruct((LP, D), F32), jax.ShapeDtypeStruct((8, D), F32)]
        + [jax.ShapeDtypeStruct(g.shape, g.dtype) for g in carried],
        scratch_shapes=([pltpu.VMEM((TM, D), F32)] if nk > 1 else []) + _exchange_sems(n_carried),
        compiler_params=_cparams("arbitrary", "arbitrary"),
    )(dz, b, x, w, resid, *carried)


def _norm_bwd_matmul(dh, x, w, b, out_dtype, name):
    n = b.shape[0]

    def body(dh_ref, x_ref, w_ref, b_ref, o_ref, dx_ref, dw_ref):
        i = pl.program_id(0)

        @pl.when(i == 0)
        def _():
            dw_ref[...] = jnp.zeros_like(dw_ref)

        row = i * TM + lax.broadcasted_iota(jnp.int32, (TM, 1), 0)
        dy = jnp.where(row >= PAD_ROWS, dh_ref[...], 0.0)
        dx, dw = _rmsnorm_bwd_rows(dy, x_ref[...], w_ref[...])
        dxb = dx.astype(BF16)
        dx_ref[...] = dxb
        dw_ref[0:1, :] += dw
        o_ref[...] = lax.dot_general(dxb, b_ref[...].astype(BF16), (((1,), (1,)), ((), ())),
                                     preferred_element_type=F32).astype(out_dtype)

    tile = pl.BlockSpec((TM, D), lambda i: (i, 0))
    return pl.pallas_call(
        body, name=name, grid=(LP // TM,),
        in_specs=[tile, tile, pl.BlockSpec((1, D), lambda i: (0, 0)), pl.BlockSpec((n, D), lambda i: (0, 0))],
        out_specs=[pl.BlockSpec((TM, n), lambda i: (i, 0)), tile, pl.BlockSpec((8, D), lambda i: (0, 0))],
        out_shape=[jax.ShapeDtypeStruct((LP, n), out_dtype), jax.ShapeDtypeStruct((LP, D), BF16),
                   jax.ShapeDtypeStruct((8, D), F32)],
        compiler_params=_cparams("arbitrary"),
    )(dh, x, w, b)


GELU_C = math.sqrt(2.0 / math.pi)
GELU_K = 0.044715
STRIP = 16
HALF = 8


def _gelu_half(a):
    return 0.5 * jnp.tanh(a * (a * a * (GELU_C * GELU_K) + GELU_C)) + 0.5


def _gelu_slope(a, h):
    return h * (1.0 + (a - a * h) * (a * a * (6.0 * GELU_C * GELU_K) + 2.0 * GELU_C))


def _shift_down(x, prev8):
    row = lax.broadcasted_iota(jnp.int32, (8, 1), 0)
    r1, r2 = pltpu.roll(x, 1, 0), pltpu.roll(x, 2, 0)
    top1 = jnp.where(row < 1, pltpu.roll(prev8, 1, 0), r1[0:8, :])
    top2 = jnp.where(row < 2, pltpu.roll(prev8, 2, 0), r2[0:8, :])
    return jnp.concatenate([top1, r1[8:, :]], axis=0), jnp.concatenate([top2, r2[8:, :]], axis=0)


def _conv_act_fwd(u, cw8, name, carried=()):
    n_rows = LP // TM
    cb2 = 2 * CONV_BLOCK
    n_carried = len(carried)

    def body(*refs):
        u_ref, cw_ref = refs[:2]
        x_refs, refs = refs[2:2 + n_carried], refs[2 + n_carried:]
        conv_ref, act_ref = refs[:2]
        gathered_refs, refs = refs[2:2 + n_carried], refs[2 + n_carried:]
        carry_ref = refs[0]
        j, i = pl.program_id(0), pl.program_id(1)
        if n_carried:
            start, forward, finish = _gather_phases(x_refs, gathered_refs, *refs[1:])
            pl.when((j == 0) & (i == 0))(start)
            pl.when((j == (3 * N_CONV_BLOCKS) // 4) & (i == 0))(forward)

        @pl.when(i == 0)
        def _():
            carry_ref[...] = jnp.zeros_like(carry_ref)

        x = u_ref[...].astype(F32)
        x1, x2 = _shift_down(x, carry_ref[...])
        conv = cw_ref[3:4, :] + x2 * cw_ref[0:1, :] + x1 * cw_ref[1:2, :] + x * cw_ref[2:3, :]
        conv_ref[...] = conv.astype(BF16)
        a = conv[:, :CONV_BLOCK]
        g = conv[:, CONV_BLOCK:]
        act_ref[...] = (a * _gelu_half(a) * g).astype(BF16)
        carry_ref[...] = x[TM - 8:TM, :]
        if n_carried:
            pl.when((j == N_CONV_BLOCKS - 1) & (i == n_rows - 1))(finish)

    anywhere = [pl.BlockSpec(memory_space=pl.ANY)] * n_carried
    return pl.pallas_call(
        body, name=name, grid=(N_CONV_BLOCKS, n_rows),
        in_specs=[pl.BlockSpec((TM, cb2), lambda j, i: (i, j)), pl.BlockSpec((8, cb2), lambda j, i: (0, j))] + anywhere,
        out_specs=[pl.BlockSpec((TM, cb2), lambda j, i: (i, j)),
                   pl.BlockSpec((TM, CONV_BLOCK), lambda j, i: (i, j))] + anywhere,
        out_shape=[jax.ShapeDtypeStruct((LP, D_UP), BF16), jax.ShapeDtypeStruct((LP, D_FF), BF16)]
        + _gathered_shapes(carried),
        scratch_shapes=[pltpu.VMEM((8, cb2), F32)] + _exchange_sems(n_carried),
        compiler_params=_cparams("arbitrary", "arbitrary"),
    )(u, cw8, *carried)


def _conv_act_bwd(dact, conv, u, cw8, name):
    n_rows = LP // TM
    cb2 = 2 * CONV_BLOCK
    n_strips = TM // STRIP

    def body(dact_ref, conv_ref, u_ref, cw_ref, du_ref, dcw_ref, carry_ref):
        i = pl.program_id(1)

        @pl.when(i == 0)
        def _():
            dcw_ref[...] = jnp.zeros_like(dcw_ref)
            carry_ref[...] = jnp.zeros_like(carry_ref)

        w0, w1, w2 = cw_ref[0:1, :], cw_ref[1:2, :], cw_ref[2:3, :]
        row = lax.broadcasted_iota(jnp.int32, (HALF, 1), 0)

        def strip(k, carry):
            n1, n2, s0, s1, s2, s3 = carry
            r0 = pl.multiple_of((n_strips - 1 - k) * STRIP, STRIP)
            cv = conv_ref[pl.ds(r0, STRIP), :].astype(F32)
            dav = dact_ref[pl.ds(r0, STRIP), :].astype(F32)
            x = u_ref[pl.ds(r0, STRIP), :].astype(F32)
            du = [None, None]
            for half in (1, 0):
                rows = slice(HALF * half, HALF * (half + 1))
                a, g, dah = cv[rows, :CONV_BLOCK], cv[rows, CONV_BLOCK:], dav[rows]
                h = _gelu_half(a)
                dconv = jnp.concatenate([dah * g * _gelu_slope(a, h), dah * (a * h)], axis=1)
                u1, u2 = pltpu.roll(dconv, HALF - 1, 0), pltpu.roll(dconv, HALF - 2, 0)
                d1 = jnp.where(row >= HALF - 1, n1, u1)
                d2 = jnp.where(row >= HALF - 2, n2, u2)
                du[half] = dconv * w2 + d1 * w1 + d2 * w0
                s0, s1, s2, s3 = s0 + d2 * x[rows], s1 + d1 * x[rows], s2 + dconv * x[rows], s3 + dconv
                n1, n2 = u1, u2
            du_ref[pl.ds(r0, STRIP), :] = jnp.concatenate(du, axis=0).astype(BF16)
            return n1, n2, s0, s1, s2, s3

        below = carry_ref[...]
        zero = jnp.zeros((HALF, cb2), F32)
        init = (pltpu.roll(below, HALF - 1, 0), pltpu.roll(below, HALF - 2, 0), zero, zero, zero, zero)
        u1, _, s0, s1, s2, s3 = lax.fori_loop(0, n_strips, strip, init, unroll=2)
        carry_ref[...] = pltpu.roll(u1, 1, 0)
        dcw_ref[0:1, :] += jnp.sum(s0, axis=0, keepdims=True)
        dcw_ref[1:2, :] += jnp.sum(s1, axis=0, keepdims=True)
        dcw_ref[2:3, :] += jnp.sum(s2, axis=0, keepdims=True)
        dcw_ref[3:4, :] += jnp.sum(s3, axis=0, keepdims=True)

    rev = lambda j, i: (n_rows - 1 - i, j)
    return pl.pallas_call(
        body, name=name, grid=(N_CONV_BLOCKS, n_rows),
        in_specs=[pl.BlockSpec((TM, CONV_BLOCK), rev), pl.BlockSpec((TM, cb2), rev), pl.BlockSpec((TM, cb2), rev),
                  pl.BlockSpec((8, cb2), lambda j, i: (0, j))],
        out_specs=[pl.BlockSpec((TM, cb2), rev), pl.BlockSpec((8, cb2), lambda j, i: (0, j))],
        out_shape=[jax.ShapeDtypeStruct((LP, D_UP), BF16), jax.ShapeDtypeStruct((8, D_UP), F32)],
        scratch_shapes=[pltpu.VMEM((HALF, cb2), F32)],
        compiler_params=_cparams("arbitrary", "arbitrary"),
    )(dact, conv, u, cw8)


CHUNKS_PER_STEP = 3 if N_CHUNKS % 3 == 0 else 1
STEP_ROWS = CHUNKS_PER_STEP * CHUNK
N_STEPS = N_CHUNKS // CHUNKS_PER_STEP


class _ReadAsF32:
    def __init__(self, ref):
        self.ref = ref

    def __getitem__(self, idx):
        return self.ref[idx].astype(F32)


def _ret_consts(h):
    rows = STEP_ROWS
    lg = math.log(1.0 - 2.0 ** (-5.0 - h))
    ri = lax.broadcasted_iota(jnp.int32, (rows, rows), 0)
    ci = lax.broadcasted_iota(jnp.int32, (rows, rows), 1)
    diff = (ri - ci).astype(F32)
    dmat = jnp.where(diff >= 0, jnp.exp(lg * jnp.maximum(diff, 0.0)), 0.0)
    rowf = lax.broadcasted_iota(jnp.int32, (rows, 1), 0).astype(F32)
    zeta = jnp.exp(lg * (rows - 1.0 - rowf))
    xi = jnp.exp(lg * (rowf + 1.0))
    return dmat, zeta, xi, math.exp(lg * rows)


def _rope(t, cosv, sinv):
    return t * cosv + pltpu.roll(t, RET_DK // 2, 1) * sinv


def _unrope(d, cosv, sinv):
    return d * cosv + pltpu.roll(d * sinv, RET_DK // 2, 1)


def _gla_masks():
    ri = lax.broadcasted_iota(jnp.int32, (CHUNK, CHUNK), 0)
    ci = lax.broadcasted_iota(jnp.int32, (CHUNK, CHUNK), 1)
    return dict(ri=ri, ci=ci, tril=(ri >= ci).astype(F32), heads=_head_block_mask(), own=_state_block_mask())


def _gla_common(p_ref, w2_ref, gb_ref, chunk, rows, masks):
    row = lax.broadcasted_iota(jnp.int32, (CHUNK, 1), 0)
    real = (chunk * CHUNK + row) >= PAD_ROWS
    ga = p_ref[rows, O_GA:O_GA + 128]
    z = _dot(ga, w2_ref[...]) + gb_ref[...]
    la = (jnp.minimum(z, 0.0) - jnp.log(1.0 + jnp.exp(-jnp.abs(z)))) * (1.0 / GLA_TAU)
    la = jnp.where(real, la, 0.0)
    ri, ci = masks["ri"], masks["ci"]
    cum = _dot_exact_rhs(masks["tril"], la)
    last = cum[CHUNK - 1:CHUNK, :]
    qs = p_ref[rows, O_GQ:O_GQ + 256] * (GLA_DK ** -0.5)
    k = p_ref[rows, O_GK:O_GK + 256]
    ecum = jnp.exp(cum)
    ekl = jnp.exp(last - cum)
    el = jnp.exp(last)
    refs = [jnp.zeros((1, 256), F32)] + [cum[a * SUB - 1:a * SUB, :] for a in range(1, N_SUB)]
    eq = [jnp.exp(cum[a * SUB:(a + 1) * SUB, :] - refs[a]) for a in range(N_SUB)]
    spread = refs[0] - cum[SUB - 1:SUB, :]
    for a in range(1, N_SUB):
        spread = jnp.maximum(spread, refs[a] - cum[(a + 1) * SUB - 1:(a + 1) * SUB, :])
    small = jnp.max(spread) <= GLA_FACTORED_MAX
    return dict(real=real, row=row, z=z, la=la, cum=cum, last=last, qs=qs, k=k, ecum=ecum, ekl=ekl, el=el,
                refs=refs, eq=eq, small=small, ri=ri, ci=ci, masks=masks)


GLA_FACTORED_MAX = 40.0


def _head_block_mask():
    r = lax.broadcasted_iota(jnp.int32, (CHUNK, 256), 0)
    col = lax.broadcasted_iota(jnp.int32, (CHUNK, 256), 1)
    return (r // SUB) == (col // GLA_DK)


def _state_block_mask():
    r = lax.broadcasted_iota(jnp.int32, (GLA_HEADS * GLA_DK, GLA_HEADS * GLA_DV), 0)
    col = lax.broadcasted_iota(jnp.int32, (GLA_HEADS * GLA_DK, GLA_HEADS * GLA_DV), 1)
    return (r // GLA_DK) == (col // GLA_DV)


def _block_diagonal(blocks):
    zero = jnp.zeros((GLA_DK, GLA_DV), F32)
    return jnp.concatenate([jnp.concatenate([blocks[h] if g == h else zero for g in range(GLA_HEADS)], axis=1)
                            for h in range(GLA_HEADS)], axis=0)


def _gla_factored(c):
    mask = c["masks"]["heads"]
    eks, keys, queries = [], [], []
    for a in range(N_SUB):
        ek = jnp.exp(jnp.minimum(c["refs"][a] - c["cum"], GLA_FACTORED_MAX))
        qh = c["qs"][a * SUB:(a + 1) * SUB, :] * c["eq"][a]
        eks.append(ek)
        keys.append(c["k"] * ek)
        queries.append(jnp.where(mask, jnp.concatenate([qh] * GLA_HEADS, axis=0), 0.0))
    return eks, keys, queries


def _gla_scores_factored(c, factored, p_scr):
    _, keys, queries = factored
    for a in range(N_SUB):
        out = _dot_nt(queries[a], keys[a])
        out = jnp.where(c["ci"] <= a * SUB + (c["ri"] & (SUB - 1)), out, 0.0)
        for h in range(GLA_HEADS):
            p_scr[h, a * SUB:(a + 1) * SUB, :] = out[h * SUB:(h + 1) * SUB, :]


def _gla_intra_bwd_factored(c, factored, dps, dq_scr, dk_scr):
    eks, keys, queries = factored
    mask = c["masks"]["heads"]
    dk = jnp.zeros((CHUNK, 256), F32)
    for a in range(N_SUB):
        dpa = jnp.concatenate([dps[h][a * SUB:(a + 1) * SUB, :] for h in range(GLA_HEADS)], axis=0)
        dq = jnp.where(mask, _dot(dpa, keys[a]), 0.0)
        dq = dq[0:SUB] + dq[SUB:2 * SUB] + dq[2 * SUB:3 * SUB] + dq[3 * SUB:4 * SUB]
        dq_scr[a * SUB:(a + 1) * SUB, :] = dq * c["eq"][a]
        dk = dk + _dot_tn(dpa, queries[a]) * eks[a]
    dk_scr[...] = dk


def _gla_lag_weights(c):
    cum, row = c["cum"], c["row"]
    out = [jnp.ones((CHUNK, 256), F32)]
    for r in range(1, SUB):
        out.append(jnp.where((row % SUB) >= r, jnp.exp(jnp.minimum(cum - pltpu.roll(cum, r, 0), 0.0)), 0.0))
    return out


def _gla_pairwise_keys(c):
    return [None] + [c["k"] * jnp.exp(jnp.minimum(c["refs"][a] - c["cum"], 0.0)) for a in range(1, N_SUB)]


def _gla_scores_pairwise(c, lag_w, keys, h):
    sl = slice(GLA_DK * h, GLA_DK * (h + 1))
    qs, k = c["qs"][:, sl], c["k"][:, sl]
    ri, ci = c["ri"], c["ci"]
    p = jnp.zeros((CHUNK, CHUNK), F32)
    for r in range(SUB):
        kr = k if r == 0 else pltpu.roll(k, r, 0)
        pr = jnp.sum(qs * kr * lag_w[r][:, sl], axis=1, keepdims=True)
        p = p + jnp.where(ci == ri - r, pr, 0.0)
    blocks = [jnp.zeros((SUB, CHUNK), F32)]
    for a in range(1, N_SUB):
        qh = qs[a * SUB:(a + 1) * SUB, :] * c["eq"][a][:, sl]
        blocks.append(jnp.where(ci[:SUB, :] < a * SUB, _dot_nt(qh, keys[a][:, sl]), 0.0))
    return p + jnp.concatenate(blocks, axis=0)


def _gla_all_scores(c, p_scr, factored):
    if factored:
        _gla_scores_factored(c, _gla_factored(c), p_scr)
    else:
        lag_w, keys = _gla_lag_weights(c), _gla_pairwise_keys(c)
        for h in range(GLA_HEADS):
            p_scr[h] = _gla_scores_pairwise(c, lag_w, keys, h)


def _either_form(chunks, run):
    small = chunks[0]["small"]
    for c in chunks[1:]:
        small = jnp.logical_and(small, c["small"])
    pl.when(small)(lambda: run(True))
    pl.when(jnp.logical_not(small))(lambda: run(False))


def _gla_intra_bwd_pairwise(c, lag_w, keys, dp, h):
    sl = slice(GLA_DK * h, GLA_DK * (h + 1))
    qs_h, k_h = c["qs"][:, sl], c["k"][:, sl]
    ri, ci = c["ri"], c["ci"]
    dq_rows = [jnp.zeros((SUB, GLA_DK), F32)]
    dk = jnp.zeros((CHUNK, GLA_DK), F32)
    for a in range(1, N_SUB):
        eq = c["eq"][a][:, sl]
        qh = qs_h[a * SUB:(a + 1) * SUB, :] * eq
        dpa = jnp.where(ci[:SUB, :] < a * SUB, dp[a * SUB:(a + 1) * SUB, :], 0.0)
        dq_rows.append(_dot(dpa, keys[a][:, sl]) * eq)
        ek = jnp.exp(jnp.minimum(c["refs"][a][:, sl] - c["cum"][:, sl], 0.0))
        dk = dk + _dot_tn(dpa, qh) * ek
    dq = jnp.concatenate(dq_rows, axis=0)
    for r in range(SUB):
        w = lag_w[r][:, sl]
        dpr = jnp.sum(jnp.where(ci == ri - r, dp, 0.0), axis=1, keepdims=True)
        kr = k_h if r == 0 else pltpu.roll(k_h, r, 0)
        dq = dq + dpr * kr * w
        back = dpr * qs_h * w
        dk = dk + (back if r == 0 else pltpu.roll(back, CHUNK - r, 0))
    return dq, dk


def _gla_all_intra_bwd(c, dps, p_scr, dq_scr, dk_scr, factored):
    if factored:
        terms = _gla_factored(c)
        _gla_scores_factored(c, terms, p_scr)
        _gla_intra_bwd_factored(c, terms, dps, dq_scr, dk_scr)
    else:
        lag_w, keys = _gla_lag_weights(c), _gla_pairwise_keys(c)
        outs = [_gla_intra_bwd_pairwise(c, lag_w, keys, dps[h], h) for h in range(GLA_HEADS)]
        for h in range(GLA_HEADS):
            p_scr[h] = _gla_scores_pairwise(c, lag_w, keys, h)
        dq_scr[...] = jnp.concatenate([o[0] for o in outs], axis=1)
        dk_scr[...] = jnp.concatenate([o[1] for o in outs], axis=1)


def _mixer_fwd(proj, cos2, sin2, w2p, gb, rnw, gnw, name, carried=()):
    n_carried = len(carried)

    def body(*refs):
        p_ref, c_ref, s_ref, w2_ref, gb_ref, rnw_ref, gnw_ref = refs[:7]
        p_ref = _ReadAsF32(p_ref)
        x_refs, refs = refs[7:7 + n_carried], refs[7 + n_carried:]
        ocat_ref, mrg_ref, sr_out, sg_out = refs[:4]
        gathered_refs, refs = refs[4:4 + n_carried], refs[4 + n_carried:]
        sr, sg, p_scr = refs[:3]
        n = pl.program_id(0)
        if n_carried:
            start, forward, finish = _gather_phases(x_refs, gathered_refs, *refs[3:])
            pl.when(n == 0)(start)
            pl.when(n == (3 * N_STEPS) // 4)(forward)

        @pl.when(n == 0)
        def _():
            sr[...] = jnp.zeros_like(sr)
            sg[...] = jnp.zeros_like(sg)

        sr_out[0] = sr[...]
        cosv, sinv = c_ref[...], s_ref[...]

        for h in range(RET_HEADS):
            dmat, zeta, xi, gc = _ret_consts(h)
            hs = slice(128 * h, 128 * (h + 1))
            q = _rope(p_ref[:, O_RQ + 128 * h:O_RQ + 128 * (h + 1)], cosv, sinv)
            k = _rope(p_ref[:, O_RK + 128 * h:O_RK + 128 * (h + 1)], cosv, sinv) * (RET_DK ** -0.5)
            v = p_ref[:, O_RV + 128 * h:O_RV + 128 * (h + 1)]
            g = p_ref[:, O_RG + 128 * h:O_RG + 128 * (h + 1)]
            s_in = sr[h]
            a = _dot_nt(q, k) * dmat
            o = _dot(a, v) + _dot(q, s_in) * xi
            sr[h] = gc * s_in + _dot_tn(k * zeta, v)
            mu = jnp.mean(o, axis=-1, keepdims=True)
            xc = o - mu
            nrm = xc * lax.rsqrt(jnp.mean(xc * xc, axis=-1, keepdims=True) + EPS)
            ocat_ref[:, hs] = o
            mrg_ref[:, hs] = (nrm * rnw_ref[:, hs] * (g * _sigmoid(g))).astype(BF16)

        row_slices = [slice(CHUNK * j, CHUNK * (j + 1)) for j in range(CHUNKS_PER_STEP)]
        masks = _gla_masks()
        chunks = [_gla_common(p_ref, w2_ref, gb_ref, n * CHUNKS_PER_STEP + j, rows, masks)
                  for j, rows in enumerate(row_slices)]

        def gla_chunks(factored):
            own = masks["own"]
            for j, (rows, c) in enumerate(zip(row_slices, chunks)):
                s_in = sg[...]
                for h in range(GLA_HEADS):
                    sg_out[j, h] = s_in[GLA_DK * h:GLA_DK * (h + 1), GLA_DV * h:GLA_DV * (h + 1)]
                _gla_all_scores(c, p_scr.at[j], factored)
                v_all = p_ref[rows, O_GV:O_GV + GLA_HEADS * GLA_DV]
                o_inter = _dot(c["qs"] * c["ecum"], s_in)
                decay = jnp.exp(_dot_tn_exact_lhs(c["la"], jnp.ones((CHUNK, GLA_HEADS * GLA_DV), F32)))
                sg[...] = decay * s_in + jnp.where(own, _dot_tn(c["k"] * c["ekl"], v_all), 0.0)
                o_intra = _dot(p_scr[j].reshape(GLA_HEADS * CHUNK, CHUNK), v_all)
                for h in range(GLA_HEADS):
                    hs = slice(512 + 128 * h, 512 + 128 * (h + 1))
                    g = p_ref[rows, O_GR + 128 * h:O_GR + 128 * (h + 1)]
                    o = (o_intra[CHUNK * h:CHUNK * (h + 1), GLA_DV * h:GLA_DV * (h + 1)]
                         + o_inter[:, GLA_DV * h:GLA_DV * (h + 1)])
                    nrm = o * lax.rsqrt(jnp.mean(o * o, axis=-1, keepdims=True) + EPS)
                    ocat_ref[rows, hs] = o
                    mrg_ref[rows, hs] = (nrm * gnw_ref[:, 128 * h:128 * (h + 1)] * (g * _sigmoid(g))).astype(BF16)

        _either_form(chunks, gla_chunks)

        if n_carried:
            pl.when(n == N_STEPS - 1)(finish)

    const = lambda shape: pl.BlockSpec(shape, lambda n: (0,) * len(shape))
    anywhere = [pl.BlockSpec(memory_space=pl.ANY)] * n_carried
    return pl.pallas_call(
        body, name=name, grid=(N_STEPS,),
        in_specs=[pl.BlockSpec((STEP_ROWS, IN_WP), lambda n: (n, 0)),
                  pl.BlockSpec((STEP_ROWS, 128), lambda n: (n, 0)), pl.BlockSpec((STEP_ROWS, 128), lambda n: (n, 0)),
                  const((128, 256)), const((1, 256)), const((1, 512)), const((1, 512))] + anywhere,
        out_specs=[pl.BlockSpec((STEP_ROWS, D), lambda n: (n, 0)), pl.BlockSpec((STEP_ROWS, D), lambda n: (n, 0)),
                   pl.BlockSpec((1, RET_HEADS, RET_DK, 128), lambda n: (n, 0, 0, 0)),
                   pl.BlockSpec((CHUNKS_PER_STEP, GLA_HEADS, GLA_DK, GLA_DV), lambda n: (n, 0, 0, 0))] + anywhere,
        out_shape=[jax.ShapeDtypeStruct((LP, D), F32), jax.ShapeDtypeStruct((LP, D), BF16),
                   jax.ShapeDtypeStruct((N_STEPS, RET_HEADS, RET_DK, 128), F32),
                   jax.ShapeDtypeStruct((N_CHUNKS, GLA_HEADS, GLA_DK, GLA_DV), F32)] + _gathered_shapes(carried),
        scratch_shapes=[pltpu.VMEM((RET_HEADS, RET_DK, 128), F32),
                        pltpu.VMEM((GLA_HEADS * GLA_DK, GLA_HEADS * GLA_DV), F32),
                        pltpu.VMEM((CHUNKS_PER_STEP, GLA_HEADS, CHUNK, CHUNK), F32)] + _exchange_sems(n_carried),
        compiler_params=_cparams("arbitrary"),
    )(proj, cos2, sin2, w2p, gb, rnw, gnw, *carried)


def _mixer_bwd(proj, ocat, dmrg, sr_all, sg_all, cos2, sin2, w2p, gb, rnw, gnw, name, carried=()):
    last_step = N_STEPS - 1
    n_carried = len(carried)

    def body(*refs):
        p_ref, ocat_ref, dm_ref, sr_ref, sg_ref, c_ref, s_ref, w2_ref, gb_ref, rnw_ref, gnw_ref = refs[:11]
        p_ref = _ReadAsF32(p_ref)
        g_refs, refs = refs[11:11 + n_carried], refs[11 + n_carried:]
        dp_ref, dw2_ref, dgb_ref, drn_ref, dgn_ref = refs[:5]
        got_refs, refs = refs[5:5 + n_carried], refs[5 + n_carried:]
        dsr, dsg, p_scr, dq_scr, dk_scr = refs[:5]
        step = pl.program_id(0)
        n = last_step - step
        if n_carried:
            start, finish = _exchange_phases(g_refs, got_refs, *refs[5:])
            pl.when(step == 0)(start)

        @pl.when(step == 0)
        def _():
            dsr[...] = jnp.zeros_like(dsr)
            dsg[...] = jnp.zeros_like(dsg)
            dw2_ref[...] = jnp.zeros_like(dw2_ref)
            dgb_ref[...] = jnp.zeros_like(dgb_ref)
            drn_ref[...] = jnp.zeros_like(drn_ref)
            dgn_ref[...] = jnp.zeros_like(dgn_ref)

        cosv, sinv = c_ref[...], s_ref[...]
        step_row = lax.broadcasted_iota(jnp.int32, (STEP_ROWS, 1), 0)
        real = ((n * STEP_ROWS + step_row) >= PAD_ROWS).astype(F32)

        for h in range(RET_HEADS):
            dmat, zeta, xi, gc = _ret_consts(h)
            hs = slice(128 * h, 128 * (h + 1))
            q = _rope(p_ref[:, O_RQ + 128 * h:O_RQ + 128 * (h + 1)], cosv, sinv)
            k = _rope(p_ref[:, O_RK + 128 * h:O_RK + 128 * (h + 1)], cosv, sinv) * (RET_DK ** -0.5)
            v = p_ref[:, O_RV + 128 * h:O_RV + 128 * (h + 1)]
            g = p_ref[:, O_RG + 128 * h:O_RG + 128 * (h + 1)]
            o = ocat_ref[:, hs]
            dy = dm_ref[:, hs]
            wv = rnw_ref[:, hs]
            mu = jnp.mean(o, axis=-1, keepdims=True)
            xc = o - mu
            rs = lax.rsqrt(jnp.mean(xc * xc, axis=-1, keepdims=True) + EPS)
            nrm = xc * rs
            sgm = _sigmoid(g)
            sil = g * sgm
            drn_ref[0:1, hs] += jnp.sum(dy * nrm * sil, axis=0, keepdims=True)
            dgate = dy * nrm * wv * (sgm * (1.0 + g * (1.0 - sgm)))
            dn = dy * wv * sil
            do = rs * (dn - jnp.mean(dn, axis=-1, keepdims=True) - nrm * jnp.mean(dn * nrm, axis=-1, keepdims=True))
            s_in = sr_ref[0, h]
            ds_out = dsr[h]
            a = _dot_nt(q, k) * dmat
            da = _dot_nt(do, v) * dmat
            dox = do * xi
            dq = _dot(da, k) + _dot_nt(dox, s_in)
            dk = _dot_tn(da, q) + _dot_nt(v, ds_out) * zeta
            dv = _dot_tn(a, do) + _dot(k * zeta, ds_out)
            dsr[h] = gc * ds_out + _dot_tn(q, dox)
            dk = dk * (RET_DK ** -0.5)
            dp_ref[:, O_RQ + 128 * h:O_RQ + 128 * (h + 1)] = (_unrope(dq, cosv, sinv) * real).astype(BF16)
            dp_ref[:, O_RK + 128 * h:O_RK + 128 * (h + 1)] = (_unrope(dk, cosv, sinv) * real).astype(BF16)
            dp_ref[:, O_RV + 128 * h:O_RV + 128 * (h + 1)] = (dv * real).astype(BF16)
            dp_ref[:, O_RG + 128 * h:O_RG + 128 * (h + 1)] = (dgate * real).astype(BF16)

        row_slices = [slice(CHUNK * j, CHUNK * (j + 1)) for j in range(CHUNKS_PER_STEP)]
        masks = _gla_masks()
        chunks = [_gla_common(p_ref, w2_ref, gb_ref, n * CHUNKS_PER_STEP + j, rows, masks)
                  for j, rows in enumerate(row_slices)]

        def gla_chunks(factored):
            for j in reversed(range(CHUNKS_PER_STEP)):
                gla_chunk_bwd(chunks[j], n * CHUNKS_PER_STEP + j, row_slices[j], j, factored, p_ref, ocat_ref, dm_ref,
                              sg_ref, w2_ref, gnw_ref, dp_ref, dw2_ref, dgb_ref, dgn_ref, dsg, p_scr, dq_scr, dk_scr)

        _either_form(chunks, gla_chunks)
        if n_carried:
            pl.when(step == last_step)(finish)

    def gla_chunk_bwd(c, chunk, rows, j, factored, p_ref, ocat_ref, dm_ref, sg_ref, w2_ref, gnw_ref,
                      dp_ref, dw2_ref, dgb_ref, dgn_ref, dsg, p_scr, dq_scr, dk_scr):
        row = lax.broadcasted_iota(jnp.int32, (CHUNK, 1), 0)
        real = ((chunk * CHUNK + row) >= PAD_ROWS).astype(F32)
        ri, ci = c["ri"], c["ci"]
        causal = ri >= ci
        triu = (ci >= ri).astype(F32)
        qe = c["qs"] * c["ecum"]
        kl = c["k"] * c["ekl"]
        v_all = p_ref[rows, O_GV:O_GV + GLA_HEADS * GLA_DV]
        dos, dps = [], []
        for h in range(GLA_HEADS):
            hs = slice(512 + 128 * h, 512 + 128 * (h + 1))
            g = p_ref[rows, O_GR + 128 * h:O_GR + 128 * (h + 1)]
            o = ocat_ref[rows, hs]
            dy = dm_ref[rows, hs]
            wv = gnw_ref[:, 128 * h:128 * (h + 1)]
            rs = lax.rsqrt(jnp.mean(o * o, axis=-1, keepdims=True) + EPS)
            nrm = o * rs
            sgm = _sigmoid(g)
            sil = g * sgm
            dgn_ref[0:1, 128 * h:128 * (h + 1)] += jnp.sum(dy * nrm * sil, axis=0, keepdims=True)
            dgate = dy * nrm * wv * (sgm * (1.0 + g * (1.0 - sgm)))
            dn = dy * wv * sil
            do = rs * (dn - nrm * jnp.mean(dn * nrm, axis=-1, keepdims=True))
            dp_ref[rows, O_GR + 128 * h:O_GR + 128 * (h + 1)] = (dgate * real).astype(BF16)
            dos.append(do)
        do_all = jnp.concatenate(dos, axis=1)
        do_blocks = jnp.where(c["masks"]["own"], jnp.concatenate([do_all] * GLA_HEADS, axis=0), 0.0)
        dp_all = _dot_nt(do_blocks, v_all)
        dps = [jnp.where(causal, dp_all[CHUNK * h:CHUNK * (h + 1), :], 0.0) for h in range(GLA_HEADS)]
        _gla_all_intra_bwd(c, dps, p_scr.at[j], dq_scr.at[j], dk_scr.at[j], factored)
        s_in = _block_diagonal([sg_ref[j, h] for h in range(GLA_HEADS)])
        ds_out = dsg[...]
        decay = jnp.exp(_dot_tn_exact_lhs(c["la"], jnp.ones((CHUNK, GLA_HEADS * GLA_DV), F32)))
        dv_state = _dot(kl, ds_out)
        dqe = _dot_nt(do_all, s_in)
        dkl = _dot_nt(v_all, ds_out)
        dsg[...] = jnp.where(c["masks"]["own"], _dot_tn(qe, do_all), 0.0) + decay * ds_out
        sd = s_in * ds_out
        sd_hi = sd.astype(BF16)
        sd_lo = (sd - sd_hi.astype(F32)).astype(BF16)
        ones8 = jnp.ones((8, GLA_HEADS * GLA_DV), BF16)
        nt = (((1,), (1,)), ((), ()))
        d_el = (lax.dot_general(ones8, sd_hi, nt, preferred_element_type=F32)
                + lax.dot_general(ones8, sd_lo, nt, preferred_element_type=F32))[0:1, :]
        dqs = dqe * c["ecum"] + dq_scr[j]
        dkk = dkl * c["ekl"] + dk_scr[j]
        d_last = jnp.sum(dkl * kl, axis=0, keepdims=True) + d_el * c["el"]
        dcum = c["qs"] * dqs - c["k"] * dkk + jnp.where(row == CHUNK - 1, d_last, 0.0)
        dla = _dot_exact_rhs(triu, dcum)
        dv = _dot_tn(p_scr[j].reshape(GLA_HEADS * CHUNK, CHUNK), do_blocks) + dv_state
        dp_ref[rows, O_GV:O_GV + GLA_HEADS * GLA_DV] = (dv * real).astype(BF16)
        dp_ref[rows, O_GQ:O_GQ + 256] = (dqs * (GLA_DK ** -0.5) * real).astype(BF16)
        dp_ref[rows, O_GK:O_GK + 256] = (dkk * real).astype(BF16)
        dz = dla * (1.0 / GLA_TAU) * _sigmoid(-c["z"]) * real
        ga = p_ref[rows, O_GA:O_GA + 128]
        dp_ref[rows, O_GA:O_GA + 128] = _dot_nt(dz, w2_ref[...]).astype(BF16)
        dp_ref[rows, O_GA + 128:IN_WP] = jnp.zeros((CHUNK, IN_WP - O_GA - 128), BF16)
        dw2_ref[...] += _dot_tn(ga, dz)
        dgb_ref[0:1, :] += jnp.sum(dz, axis=0, keepdims=True)

    const = lambda shape: pl.BlockSpec(shape, lambda s: (0,) * len(shape))
    rev = lambda s: (last_step - s, 0)
    anywhere = [pl.BlockSpec(memory_space=pl.ANY)] * n_carried
    return pl.pallas_call(
        body, name=name, grid=(N_STEPS,),
        in_specs=[pl.BlockSpec((STEP_ROWS, IN_WP), rev), pl.BlockSpec((STEP_ROWS, D), rev),
                  pl.BlockSpec((STEP_ROWS, D), rev),
                  pl.BlockSpec((1, RET_HEADS, RET_DK, 128), lambda s: (last_step - s, 0, 0, 0)),
                  pl.BlockSpec((CHUNKS_PER_STEP, GLA_HEADS, GLA_DK, GLA_DV), lambda s: (last_step - s, 0, 0, 0)),
                  pl.BlockSpec((STEP_ROWS, 128), rev), pl.BlockSpec((STEP_ROWS, 128), rev),
                  const((128, 256)), const((1, 256)), const((1, 512)), const((1, 512))] + anywhere,
        out_specs=[pl.BlockSpec((STEP_ROWS, IN_WP), rev), const((128, 256)), const((8, 256)),
                   const((8, 512)), const((8, 512))] + anywhere,
        out_shape=[jax.ShapeDtypeStruct((LP, IN_WP), BF16), jax.ShapeDtypeStruct((128, 256), F32),
                   jax.ShapeDtypeStruct((8, 256), F32), jax.ShapeDtypeStruct((8, 512), F32),
                   jax.ShapeDtypeStruct((8, 512), F32)] + [jax.ShapeDtypeStruct(g.shape, g.dtype) for g in carried],
        scratch_shapes=[pltpu.VMEM((RET_HEADS, RET_DK, 128), F32),
                        pltpu.VMEM((GLA_HEADS * GLA_DK, GLA_HEADS * GLA_DV), F32),
                        pltpu.VMEM((CHUNKS_PER_STEP, GLA_HEADS, CHUNK, CHUNK), F32),
                        pltpu.VMEM((CHUNKS_PER_STEP, CHUNK, 256), F32),
                        pltpu.VMEM((CHUNKS_PER_STEP, CHUNK, 256), F32)] + _exchange_sems(n_carried),
        compiler_params=_cparams("arbitrary"),
    )(proj, ocat, dmrg, sr_all, sg_all, cos2, sin2, w2p, gb, rnw, gnw, *carried)


def _all_gather(xs, name):
    n = len(xs)

    def body(*refs):
        start, forward, finish = _gather_phases(refs[:n], refs[n:2 * n], *refs[2 * n:])
        start()
        forward()
        finish()

    return pl.pallas_call(
        body, name=name,
        in_specs=[pl.BlockSpec(memory_space=pl.ANY)] * n,
        out_specs=[pl.BlockSpec(memory_space=pl.ANY)] * n,
        out_shape=_gathered_shapes(xs),
        scratch_shapes=_exchange_sems(n),
    )(*xs)


def _gathered_shapes(xs):
    return [jax.ShapeDtypeStruct((N_DEV,) + x.shape, x.dtype) for x in xs]


def _exchange_sems(n):
    if n == 0:
        return []
    return [pltpu.SemaphoreType.DMA((7 * n,)), pltpu.SemaphoreType.DMA((7 * n,)), pltpu.SemaphoreType.DMA((n,))]


def _gather_phases(x_refs, out_refs, send_sems, recv_sems, local_sems):
    n = len(x_refs)
    mx, my, mc = lax.axis_index("x"), lax.axis_index("y"), lax.axis_index("c")
    me, sibling = (mx, my, mc), (mx, my, 1 - mc)
    chips = [(1 - mx, my), (mx, 1 - my), (1 - mx, 1 - my)]

    def slot(a, px, py, pc):
        return out_refs[a].at[4 * px + 2 * py + pc]

    def copy(a, k, block, to, src=None):
        return pltpu.make_async_remote_copy(
            src_ref=slot(a, *block) if src is None else src, dst_ref=slot(a, *block),
            send_sem=send_sems.at[7 * a + k], recv_sem=recv_sems.at[7 * a + k],
            device_id=to, device_id_type=MESH_IDS)

    mine = [pltpu.make_async_copy(x_refs[a], slot(a, *me), local_sems.at[a]) for a in range(n)]
    first = []
    for a in range(n):
        first.append(copy(a, 0, me, sibling, src=x_refs[a]))
        first += [copy(a, 1 + j, me, (*chip, mc), src=x_refs[a]) for j, chip in enumerate(chips)]
    passed = [copy(a, 4 + j, (*chip, mc), sibling) for j, chip in enumerate(chips) for a in range(n)]

    def start():
        for cp in mine + first:
            cp.start()

    def forward():
        for j, chip in enumerate(chips):
            for a in range(n):
                copy(a, 1 + j, (*chip, mc), me).wait_recv()
                passed[j * n + a].start()

    def finish():
        for a in range(n):
            copy(a, 0, sibling, me).wait_recv()
            for j, chip in enumerate(chips):
                copy(a, 4 + j, (*chip, 1 - mc), me).wait_recv()
        for cp in first + passed:
            cp.wait_send()
        for cp in mine:
            cp.wait()

    return start, forward, finish


def _exchange_blocks(gs, name):
    n = len(gs)

    def body(*refs):
        start, finish = _exchange_phases(refs[:n], refs[n:2 * n], *refs[2 * n:])
        start()
        finish()

    return pl.pallas_call(
        body, name=name,
        in_specs=[pl.BlockSpec(memory_space=pl.ANY)] * n,
        out_specs=[pl.BlockSpec(memory_space=pl.ANY)] * n,
        out_shape=[jax.ShapeDtypeStruct(g.shape, g.dtype) for g in gs],
        scratch_shapes=_exchange_sems(n),
    )(*gs)


def _exchange_phases(g_refs, out_refs, send_sems, recv_sems, local_sems):
    n = len(g_refs)
    mx, my, mc = lax.axis_index("x"), lax.axis_index("y"), lax.axis_index("c")
    me = 4 * mx + 2 * my + mc
    mine = [pltpu.make_async_copy(g_refs[a].at[me], out_refs[a].at[me], local_sems.at[a]) for a in range(n)]
    copies = []
    for r in range(1, N_DEV):
        px, py, pc = mx ^ (r >> 2), my ^ ((r >> 1) & 1), mc ^ (r & 1)
        peer = 4 * px + 2 * py + pc
        for a in range(n):
            copies.append(pltpu.make_async_remote_copy(
                src_ref=g_refs[a].at[peer], dst_ref=out_refs[a].at[me],
                send_sem=send_sems.at[7 * a + r - 1], recv_sem=recv_sems.at[7 * a + r - 1],
                device_id=(px, py, pc), device_id_type=MESH_IDS))

    def start():
        for cp in mine + copies:
            cp.start()

    def finish():
        for cp in copies:
            cp.wait_recv()
        for cp in copies:
            cp.wait_send()
        for cp in mine:
            cp.wait()

    return start, finish


IN_SHARD = IN_W // N_DEV
IN_SHARD_P = 512
UP_SHARD = D_UP // N_DEV
UP_SHARD_P = 768
RELAYOUT_ROWS = 256


def _pieces_w_in():
    return [(k, 0, IN_SHARD * k, IN_SHARD) for k in range(N_DEV)]


def _pieces_ffn_up():
    pieces = []
    for k in range(N_DEV):
        n, end = UP_SHARD * k, UP_SHARD * (k + 1)
        while n < end:
            half, r = divmod(n, D_FF)
            blk, off = divmod(r, CONV_BLOCK)
            run = min(CONV_BLOCK - off, end - n)
            pieces.append((k, n - UP_SHARD * k, 2 * CONV_BLOCK * blk + CONV_BLOCK * half + off, run))
            n += run
    return pieces


def _assemble_block(load, spans, dst_block, rows):
    lo = 128 * dst_block
    lane = lax.broadcasted_iota(jnp.int32, (1, 128), 1)
    out = jnp.zeros((rows, 128), F32)
    for key, src_off, dst_off, length in spans:
        a, b = max(lo, dst_off), min(lo + 128, dst_off + length)
        s, s_end = src_off + (a - dst_off), src_off + (b - dst_off)
        d = a
        while s < s_end:
            e = min(s_end, 128 * (s // 128 + 1))
            blk = load(key, s // 128)
            shift = (d - s) % 128
            if shift:
                blk = pltpu.roll(blk, shift, 1)
            out = jnp.where((lane >= d - lo) & (lane < d - lo + (e - s)), blk, out)
            d += e - s
            s = e
    return out


def _shards_to_cols(shards, pieces, width, name):
    _, rows, _ = shards.shape
    tr = RELAYOUT_ROWS

    def body(s_ref, o_ref):
        load = lambda k, b: s_ref[k, :, 128 * b:128 * (b + 1)].astype(F32)
        for db in range(width // 128):
            o_ref[:, 128 * db:128 * (db + 1)] = _assemble_block(load, pieces, db, tr).astype(BF16)

    return pl.pallas_call(
        body, name=name, grid=(rows // tr,),
        in_specs=[pl.BlockSpec((N_DEV, tr, shards.shape[2]), lambda i: (0, i, 0))],
        out_specs=pl.BlockSpec((tr, width), lambda i: (i, 0)),
        out_shape=jax.ShapeDtypeStruct((rows, width), BF16),
        compiler_params=_cparams("parallel"),
    )(shards)


def _cols_to_shards(full, pieces, shard_width, name):
    rows, width = full.shape
    tr = RELAYOUT_ROWS

    def body(f_ref, o_ref):
        load = lambda _, b: f_ref[:, 128 * b:128 * (b + 1)].astype(F32)
        for k in range(N_DEV):
            spans = [(None, dst_off, src_off, length) for dev, src_off, dst_off, length in pieces if dev == k]
            for db in range(shard_width // 128):
                o_ref[k, :, 128 * db:128 * (db + 1)] = _assemble_block(load, spans, db, tr).astype(BF16)

    return pl.pallas_call(
        body, name=name, grid=(rows // tr,),
        in_specs=[pl.BlockSpec((tr, width), lambda i: (i, 0))],
        out_specs=pl.BlockSpec((N_DEV, tr, shard_width), lambda i: (0, i, 0)),
        out_shape=jax.ShapeDtypeStruct((N_DEV, rows, shard_width), BF16),
        compiler_params=_cparams("parallel"),
    )(full)


def _adamw(parts, w, m, v, rows_per_step, name):
    rows, cols = w.shape
    assert rows % rows_per_step == 0 and parts.shape == (N_DEV, rows, cols)

    def body(p_ref, w_ref, m_ref, v_ref, g_ref, d_ref, nm_ref, nv_ref):
        g = p_ref[0].astype(F32)
        for j in range(1, N_DEV):
            g = g + p_ref[j].astype(F32)
        m_new = ADAM_B1 * m_ref[...] + (1.0 - ADAM_B1) * g
        v_new = ADAM_B2 * v_ref[...] + (1.0 - ADAM_B2) * (g * g)
        m_hat = m_new / (1.0 - ADAM_B1 ** ADAM_STEP)
        v_hat = v_new / (1.0 - ADAM_B2 ** ADAM_STEP)
        g_ref[...] = g
        d_ref[...] = -ADAM_LR * (m_hat / (jnp.sqrt(v_hat) + ADAM_EPS) + ADAM_WD * w_ref[...])
        nm_ref[...] = m_new
        nv_ref[...] = v_new

    tile = pl.BlockSpec((rows_per_step, cols), lambda i: (i, 0))
    shape = jax.ShapeDtypeStruct((rows, cols), F32)
    return pl.pallas_call(
        body, name=name, grid=(rows // rows_per_step,),
        in_specs=[pl.BlockSpec((N_DEV, rows_per_step, cols), lambda i: (0, i, 0)), tile, tile, tile],
        out_specs=[tile, tile, tile, tile],
        out_shape=[shape, shape, shape, shape],
        compiler_params=_cparams("parallel"),
    )(parts, w, m, v)


BIG = (("w_in", (DEPTH, D, IN_W // N_DEV), 2), ("w_out", (DEPTH, D // N_DEV, D), 1),
       ("ffn_up", (DEPTH, D, D_UP // N_DEV), 2), ("ffn_down", (DEPTH, D_FF // N_DEV, D), 1))
SMALL = (("meta_tokens", (N_META, D // N_DEV), 1), ("gla_gate_w2", (DEPTH, GATE_RANK, 256 // N_DEV), 2),
         ("ffn_conv_w", (DEPTH, 3, D_UP // N_DEV), 2))
REPL = (("pre_mix_norm", (DEPTH, D)), ("gla_gate_b", (DEPTH, 256)), ("ret_norm_w", (DEPTH, 512)),
        ("gla_norm_w", (DEPTH, 512)), ("post_mix_norm", (DEPTH, D)), ("pre_ffn_norm", (DEPTH, D)),
        ("ffn_conv_b", (DEPTH, D_UP)), ("post_ffn_norm", (DEPTH, D)))
WEIGHT_ORDER = ("meta_tokens", "pre_mix_norm", "w_in", "gla_gate_w2", "gla_gate_b", "ret_norm_w", "gla_norm_w",
                "w_out", "post_mix_norm", "pre_ffn_norm", "ffn_up", "ffn_conv_w", "ffn_conv_b", "ffn_down",
                "post_ffn_norm")


def _size(shape):
    return math.prod(shape)


def _round_up(n, mult):
    return -(-n // mult) * mult


REPL_ROWS = _round_up(-(-sum(_size(s) for _, s in REPL) // LANES), 8)
SMALL_ROWS = _round_up(-(-sum(_size(s) for _, s, _ in SMALL) // LANES), 8)


def _pack(arrays, rows, dtype):
    flat = jnp.concatenate([a.reshape(-1).astype(dtype) for a in arrays])
    return jnp.pad(flat, (0, rows * LANES - flat.shape[0])).reshape(rows, LANES)


def _unpack(buf, shapes):
    flat = buf.reshape(-1)
    out, off = [], 0
    for shape in shapes:
        out.append(flat[off:off + _size(shape)].reshape(shape))
        off += _size(shape)
    return out


def _unshard(blocks, axis):
    moved = jnp.moveaxis(blocks, 0, axis)
    shape = list(moved.shape)
    shape[axis:axis + 2] = [shape[axis] * shape[axis + 1]]
    return moved.reshape(shape)


def _to_blocks(full, axis):
    shape = list(full.shape)
    shape[axis:axis + 1] = [N_DEV, shape[axis] // N_DEV]
    return jnp.moveaxis(full.reshape(shape), axis, 0)


def _interleave_cols(w):
    lead = w.shape[:-1]
    return jnp.swapaxes(w.reshape(lead + (2, N_CONV_BLOCKS, CONV_BLOCK)), -3, -2).reshape(lead + (D_UP,))


def _deinterleave_cols(w):
    lead = w.shape[:-1]
    return jnp.swapaxes(w.reshape(lead + (N_CONV_BLOCKS, 2, CONV_BLOCK)), -3, -2).reshape(lead + (D_UP,))


def _rope_tables():
    half = RET_DK // 2
    inv = ROPE_BASE ** (-jnp.arange(half, dtype=F32) / half)
    pos = jnp.arange(LP, dtype=F32) - float(PAD_ROWS)
    ang = pos[:, None] * inv[None, :]
    c, s = jnp.cos(ang), jnp.sin(ang)
    return jnp.concatenate([c, c], axis=1), jnp.concatenate([-s, s], axis=1)


def kernel(x, meta_tokens, pre_mix_norm, w_in, gla_gate_w2, gla_gate_b, ret_norm_w, gla_norm_w, w_out, post_mix_norm, pre_ffn_norm, ffn_up, ffn_conv_w, ffn_conv_b, ffn_down, post_ffn_norm, loss_target, m_meta_tokens, m_pre_mix_norm, m_w_in, m_gla_gate_w2, m_gla_gate_b, m_ret_norm_w, m_gla_norm_w, m_w_out, m_post_mix_norm, m_pre_ffn_norm, m_ffn_up, m_ffn_conv_w, m_ffn_conv_b, m_ffn_down, m_post_ffn_norm, v_meta_tokens, v_pre_mix_norm, v_w_in, v_gla_gate_w2, v_gla_gate_b, v_ret_norm_w, v_gla_norm_w, v_w_out, v_post_mix_norm, v_pre_ffn_norm, v_ffn_up, v_ffn_conv_w, v_ffn_conv_b, v_ffn_down, v_post_ffn_norm):
    weights = dict(meta_tokens=meta_tokens, pre_mix_norm=pre_mix_norm, w_in=w_in, gla_gate_w2=gla_gate_w2,
                   gla_gate_b=gla_gate_b, ret_norm_w=ret_norm_w, gla_norm_w=gla_norm_w, w_out=w_out,
                   post_mix_norm=post_mix_norm, pre_ffn_norm=pre_ffn_norm, ffn_up=ffn_up, ffn_conv_w=ffn_conv_w,
                   ffn_conv_b=ffn_conv_b, ffn_down=ffn_down, post_ffn_norm=post_ffn_norm)
    mom1 = dict(meta_tokens=m_meta_tokens, pre_mix_norm=m_pre_mix_norm, w_in=m_w_in, gla_gate_w2=m_gla_gate_w2,
                gla_gate_b=m_gla_gate_b, ret_norm_w=m_ret_norm_w, gla_norm_w=m_gla_norm_w, w_out=m_w_out,
                post_mix_norm=m_post_mix_norm, pre_ffn_norm=m_pre_ffn_norm, ffn_up=m_ffn_up,
                ffn_conv_w=m_ffn_conv_w, ffn_conv_b=m_ffn_conv_b, ffn_down=m_ffn_down, post_ffn_norm=m_post_ffn_norm)
    mom2 = dict(meta_tokens=v_meta_tokens, pre_mix_norm=v_pre_mix_norm, w_in=v_w_in, gla_gate_w2=v_gla_gate_w2,
                gla_gate_b=v_gla_gate_b, ret_norm_w=v_ret_norm_w, gla_norm_w=v_gla_norm_w, w_out=v_w_out,
                post_mix_norm=v_post_mix_norm, pre_ffn_norm=v_pre_ffn_norm, ffn_up=v_ffn_up,
                ffn_conv_w=v_ffn_conv_w, ffn_conv_b=v_ffn_conv_b, ffn_down=v_ffn_down, post_ffn_norm=v_post_ffn_norm)

    pad_cols = lambda a, width: jnp.pad(a, ((0, 0), (0, width - a.shape[1])))
    big_names = [n for n, _, _ in BIG]
    shard = {}
    for l in range(DEPTH):
        shard[l, "w_in"] = pad_cols(w_in[l].astype(BF16), IN_SHARD_P)
        shard[l, "w_out"] = w_out[l].astype(BF16)
        shard[l, "ffn_up"] = pad_cols(ffn_up[l].astype(BF16), UP_SHARD_P)
        shard[l, "ffn_down"] = ffn_down[l].astype(BF16)
    w_in_0, small = _all_gather([shard[0, "w_in"], _pack([weights[n] for n, _, _ in SMALL], SMALL_ROWS, F32)],
                                "gather_first_weights")
    gathered = {(0, "w_in"): w_in_0}
    gather_in_mixer = {l: [(l, n) for n in big_names[1:]] for l in range(DEPTH)}
    gather_in_conv = {l: [(l + 1, "w_in")] for l in range(DEPTH - 1)}
    small_parts = _unpack_blocks(small, [s for _, s, _ in SMALL])
    full = {n: _unshard(p, ax) for (n, _, ax), p in zip(SMALL, small_parts)}
    w2p = jnp.pad(full["gla_gate_w2"], ((0, 0), (0, 128 - GATE_RANK), (0, 0)))
    cw8 = jnp.concatenate([_interleave_cols(full["ffn_conv_w"]), _interleave_cols(ffn_conv_b)[:, None, :],
                           jnp.zeros((DEPTH, 4, D_UP), F32)], axis=1)
    cos2, sin2 = _rope_tables()

    h = jnp.concatenate([jnp.zeros((PAD_ROWS, D), F32), full["meta_tokens"], x[0]], axis=0)
    target = jnp.concatenate([jnp.zeros((CHUNK, D), F32), loss_target[0]], axis=0)
    saved, layer_w = [], []
    for l in range(DEPTH):
        lw = dict(w_in=_shards_to_cols(gathered[l, "w_in"], _pieces_w_in(), IN_WP, f"w_in_cols_{l}"))
        a1, proj = _norm_matmul(h, pre_mix_norm[l:l + 1], lw["w_in"], out_dtype=BF16, tm=TM_BIG, tn=IN_WP // 2,
                                name=f"in_proj_{l}")
        keys = gather_in_mixer.get(l, [])
        ocat, merged, sr_all, sg_all, *got = _mixer_fwd(proj, cos2, sin2, w2p[l], gla_gate_b[l:l + 1],
                                                        ret_norm_w[l:l + 1], gla_norm_w[l:l + 1], f"mixer_fwd_{l}",
                                                        carried=[shard[key] for key in keys])
        gathered.update(zip(keys, got))
        lw["w_out"] = gathered[l, "w_out"].reshape(D, D)
        lw["w_up"] = _shards_to_cols(gathered[l, "ffn_up"], _pieces_ffn_up(), D_UP, f"ffn_up_cols_{l}")
        lw["w_down"] = gathered[l, "ffn_down"].reshape(D_FF, D)
        layer_w.append(lw)
        m, h1 = _matmul_resid_norm(merged, lw["w_out"], h, post_mix_norm[l:l + 1], f"out_proj_{l}")
        a2, u = _norm_matmul(h1, pre_ffn_norm[l:l + 1], lw["w_up"], out_dtype=BF16, tm=TM_BIG, tn=D_UP // 2,
                             name=f"ffn_up_{l}")
        keys = gather_in_conv.get(l, [])
        cv, act, *got = _conv_act_fwd(u, cw8[l], f"ffn_conv_act_{l}", carried=[shard[key] for key in keys])
        gathered.update(zip(keys, got))
        f, h2, *loss_acc = _matmul_resid_norm(act, lw["w_down"], h1, post_ffn_norm[l:l + 1], f"ffn_down_{l}",
                                              target=target if l == DEPTH - 1 else None)
        saved.append(dict(h=h, a1=a1, proj=proj, ocat=ocat, merged=merged, sr=sr_all, sg=sg_all, m=m, h1=h1,
                          a2=a2, u=u, cv=cv, act=act, f=f))
        h = h2

    dh = h
    loss = lax.psum(loss_acc[0][0, 0], ("x", "y", "c"))

    kinds = ("grad", "delta", "new_m", "new_v")
    grads = {n: [None] * DEPTH for n in WEIGHT_ORDER if n != "meta_tokens" and n not in big_names}
    pending, parts = [], {}
    for l in reversed(range(DEPTH)):
        s, lw = saved[l], layer_w[l]
        dact, df, g_post_ffn = _norm_bwd_matmul(dh, s["f"], post_ffn_norm[l:l + 1], lw["w_down"], BF16,
                                                f"ffn_down_dx_{l}")
        g_down = _matmul(s["act"], df, ta=True, out_dtype=BF16, tm=D_FF // 2, tn=D, tk=TK_LONG, name=f"ffn_down_dw_{l}")
        du, dcw = _conv_act_bwd(dact, s["cv"], s["u"], cw8[l], f"ffn_conv_act_bwd_{l}")
        dh1, g_pre_ffn = _matmul_norm_bwd(du, lw["w_up"], s["h1"], pre_ffn_norm[l:l + 1], dh, D_FF, f"ffn_up_dx_{l}")
        g_up = _matmul(s["a2"], du, ta=True, out_dtype=BF16, tm=D, tn=D_UP // 4, tk=TK_LONG, name=f"ffn_up_dw_{l}")
        dmerged, dm, g_post_mix = _norm_bwd_matmul(dh1, s["m"], post_mix_norm[l:l + 1], lw["w_out"], F32,
                                                   f"out_proj_dx_{l}")
        g_out = _matmul(s["merged"], dm, ta=True, out_dtype=BF16, tm=D, tn=D, tk=TK_LONG, name=f"out_proj_dw_{l}")
        pending += [((l, "ffn_down"), g_down.reshape(N_DEV, D_FF // N_DEV, D)),
                    ((l, "ffn_up"), _cols_to_shards(g_up, _pieces_ffn_up(), UP_SHARD_P, f"ffn_up_grad_shards_{l}")),
                    ((l, "w_out"), g_out.reshape(N_DEV, D // N_DEV, D))]
        dproj, g_w2, g_gb, g_rn, g_gn, *got = _mixer_bwd(s["proj"], s["ocat"], dmerged, s["sr"], s["sg"], cos2, sin2,
                                                         w2p[l], gla_gate_b[l:l + 1], ret_norm_w[l:l + 1],
                                                         gla_norm_w[l:l + 1], f"mixer_bwd_{l}",
                                                         carried=[blocks for _, blocks in pending])
        parts.update(zip([key for key, _ in pending], got))
        g_in = _matmul(s["a1"], dproj, ta=True, out_dtype=BF16, tm=D, tn=IN_WP // 3, tk=TK_LONG, name=f"in_proj_dw_{l}")
        pending = [((l, "w_in"), _cols_to_shards(g_in, _pieces_w_in(), IN_SHARD_P, f"w_in_grad_shards_{l}"))]
        now = pending if l == 0 else []
        dh, g_pre_mix, *got = _matmul_norm_bwd(dproj, lw["w_in"], s["h"], pre_mix_norm[l:l + 1], dh1, IN_WP,
                                               f"in_proj_dx_{l}", carried=[blocks for _, blocks in now])
        parts.update(zip([key for key, _ in now], got))
        pending = [] if l == 0 else pending
        grads["post_ffn_norm"][l] = g_post_ffn[0]
        grads["ffn_conv_w"][l] = _deinterleave_cols(dcw[0:3])
        grads["ffn_conv_b"][l] = _deinterleave_cols(dcw[3])
        grads["pre_ffn_norm"][l] = g_pre_ffn[0]
        grads["post_mix_norm"][l] = g_post_mix[0]
        grads["gla_gate_w2"][l] = g_w2[:GATE_RANK]
        grads["gla_gate_b"][l] = g_gb[0]
        grads["ret_norm_w"][l] = g_rn[0]
        grads["gla_norm_w"][l] = g_gn[0]
        grads["pre_mix_norm"][l] = g_pre_mix[0]
    local = {n: jnp.stack(v) for n, v in grads.items()}
    local["meta_tokens"] = dh[PAD_ROWS:CHUNK]
    grad_x = dh[CHUNK:][None]

    blocks = jnp.concatenate([_to_blocks(local[n], ax).reshape(N_DEV, -1) for n, _, ax in SMALL], axis=1)
    blocks = jnp.pad(blocks, ((0, 0), (0, SMALL_ROWS * LANES - blocks.shape[1]))).reshape(N_DEV, SMALL_ROWS, LANES)
    *got, small_grad_parts = _exchange_blocks([b for _, b in pending] + [blocks], "exchange_last_grads")
    parts.update(zip([key for key, _ in pending], got))

    widths = dict(w_in=IN_SHARD_P, w_out=D, ffn_up=UP_SHARD_P, ffn_down=D)
    steps = dict(w_in=256, w_out=D // N_DEV, ffn_up=256, ffn_down=D_FF // N_DEV // 2)
    big_out = {kind: {n: [None] * DEPTH for n in big_names} for kind in kinds}
    for l in range(DEPTH):
        for n in big_names:
            mine = [pad_cols(d[n][l], widths[n]) for d in (weights, mom1, mom2)]
            results = _adamw(parts[l, n], *mine, steps[n], f"adamw_{n}_{l}")
            for kind, r in zip(kinds, results):
                big_out[kind][n][l] = r[:, :weights[n].shape[2]]
    out = {kind: {n: jnp.stack(v) for n, v in big_out[kind].items()} for kind in kinds}
    shard_shapes = [s for _, s, _ in SMALL]
    packed = [_pack([d[n] for n, _, _ in SMALL], SMALL_ROWS, F32) for d in (weights, mom1, mom2)]
    results = _adamw(small_grad_parts, *packed, SMALL_ROWS, "adamw_small_sharded")
    for kind, buf in zip(kinds, results):
        out[kind].update(zip([n for n, _, _ in SMALL], _unpack(buf, shard_shapes)))

    repl_parts = _all_gather([_pack([local[n] for n, _ in REPL], REPL_ROWS, F32)], "gather_small_grads")[0]
    packed = [_pack([d[n] for n, _ in REPL], REPL_ROWS, F32) for d in (weights, mom1, mom2)]
    results = _adamw(repl_parts, *packed, REPL_ROWS, "adamw_replicated")
    repl_shapes = [s for _, s in REPL]
    for kind, buf in zip(kinds, results):
        out[kind].update(zip([n for n, _ in REPL], _unpack(buf, repl_shapes)))

    return (loss, grad_x, *[out["grad"][n] for n in WEIGHT_ORDER], *[out["delta"][n] for n in WEIGHT_ORDER],
            *[out["new_m"][n] for n in WEIGHT_ORDER], *[out["new_v"][n] for n in WEIGHT_ORDER])


def _unpack_blocks(gathered, shapes):
    flat = gathered.reshape(N_DEV, -1)
    out, off = [], 0
    for shape in shapes:
        out.append(flat[:, off:off + _size(shape)].reshape((N_DEV,) + shape))
        off += _size(shape)
    return out
```
